```python
import math
import jax, jax.numpy as jnp
from jax import lax
import numpy as np

D_MODEL = 1024
BATCH = 8
SEQ = 4096
DEPTH = 1

N_META = 16
HEAD_DIM = 64
SWA_Q_HEADS = D_MODEL // (2 * HEAD_DIM)
SWA_KV_HEADS = max(SWA_Q_HEADS // 4, 1)
SWA_GROUP = SWA_Q_HEADS // SWA_KV_HEADS
FOX_HEADS = D_MODEL // (2 * HEAD_DIM)
SWA_Q_W = SWA_Q_HEADS * HEAD_DIM
SWA_KV_W = SWA_KV_HEADS * HEAD_DIM
FOX_W = FOX_HEADS * HEAD_DIM
D_MIX = SWA_Q_W + FOX_W
OFF_QA = SWA_Q_W
OFF_KA = OFF_QA + SWA_KV_W
OFF_VA = OFF_KA + SWA_KV_W
OFF_QB = OFF_VA + FOX_W
OFF_KB = OFF_QB + FOX_W
OFF_VB = OFF_KB + FOX_W
D_PROJ = OFF_VB + FOX_HEADS
WINDOW = 128
BLOCK = 128
N_BUCKETS = 32
MAX_DISTANCE = 128
D_FF = -(-8 * D_MODEL // (3 * 256)) * 256
EPS = 1e-6
NEG_INF = -1e30

kernel_name = "hymba_swa_sink_fox_t5bias_sandwich"


def rms_norm(x, g):
    xf = x.astype(jnp.float32)
    y = xf * lax.rsqrt(jnp.mean(xf * xf, axis=-1, keepdims=True) + EPS)
    return (y * g.astype(jnp.float32)).astype(x.dtype)


def t5_bucket(dist):
    n = jnp.maximum(dist, 0).astype(jnp.int32)
    max_exact = N_BUCKETS // 2
    nf = jnp.maximum(n, 1).astype(jnp.float32)
    large = max_exact + (jnp.log(nf / max_exact) / math.log(MAX_DISTANCE / max_exact)
                         * (N_BUCKETS - max_exact)).astype(jnp.int32)
    large = jnp.minimum(large, N_BUCKETS - 1)
    return jnp.where(n < max_exact, n, large)


def softmax_with_sink(s, sink):
    sink_r = sink.reshape((1,) + sink.shape + (1,) * (s.ndim - 3))
    col = jnp.broadcast_to(sink_r, s.shape[:-1] + (1,))
    p = jax.nn.softmax(jnp.concatenate([s, col], axis=-1), axis=-1)
    return p[..., :-1]


def swa_sink_attention(q, k, v, sinks, rel_bias):
    B, L = q.shape[0], q.shape[1]
    n_blk = (L - N_META) // BLOCK
    scale = HEAD_DIM ** -0.5
    sink = sinks.astype(jnp.float32).reshape(SWA_KV_HEADS, SWA_GROUP)
    tab = rel_bias.astype(jnp.float32)
    mi = jnp.arange(N_META)
    km, vm = k[:, :N_META], v[:, :N_META]

    qm = q[:, :N_META].reshape(B, N_META, SWA_KV_HEADS, SWA_GROUP, HEAD_DIM)
    d_mm = mi[:, None] - mi[None, :]
    b_mm = tab[t5_bucket(d_mm)].transpose(2, 0, 1).reshape(SWA_KV_HEADS, SWA_GROUP, N_META, N_META)
    s_mm = jnp.einsum('bqhgd,bkhd->bhgqk', qm, km, preferred_element_type=jnp.float32) * scale + b_mm
    s_mm = jnp.where(d_mm >= 0, s_mm, NEG_INF)
    p_mm = softmax_with_sink(s_mm, sink).astype(v.dtype)
    o_meta = jnp.einsum('bhgqk,bkhd->bqhgd', p_mm, vm).reshape(B, N_META, SWA_Q_HEADS, HEAD_DIM)

    qr = q[:, N_META:].reshape(B, n_blk, BLOCK, SWA_KV_HEADS, SWA_GROUP, HEAD_DIM)
    kr = k[:, N_META:].reshape(B, n_blk, BLOCK, SWA_KV_HEADS, HEAD_DIM)
    vr = v[:, N_META:].reshape(B, n_blk, BLOCK, SWA_KV_HEADS, HEAD_DIM)

    def with_prev(t):
        prev = jnp.pad(t, ((0, 0), (1, 0), (0, 0), (0, 0), (0, 0)))[:, :-1]
        return jnp.concatenate([prev, t], axis=2)

    kw, vw = with_prev(kr), with_prev(vr)
    qi = jnp.arange(BLOCK)[:, None]
    ki = jnp.arange(2 * BLOCK)[None, :]
    d_w = qi + BLOCK - ki
    b_w = tab[t5_bucket(d_w)].transpose(2, 0, 1).reshape(SWA_KV_HEADS, SWA_GROUP, 1, BLOCK, 2 * BLOCK)
    blk = jnp.arange(n_blk)[:, None, None]
    valid_w = ((d_w >= 0) & (d_w < WINDOW))[None] & ((blk > 0) | (ki >= BLOCK)[None])
    s_w = jnp.einsum('bnqhgd,bnkhd->bhgnqk', qr, kw, preferred_element_type=jnp.float32) * scale + b_w
    s_w = jnp.where(valid_w, s_w, NEG_INF)
    d_m = N_META + blk * BLOCK + qi[None] - mi[None, None, :]
    b_m = tab[t5_bucket(d_m)].transpose(3, 0, 1, 2).reshape(SWA_KV_HEADS, SWA_GROUP, n_blk, BLOCK, N_META)
    s_m = jnp.einsum('bnqhgd,bmhd->bhgnqm', qr, km, preferred_element_type=jnp.float32) * scale + b_m
    p = softmax_with_sink(jnp.concatenate([s_m, s_w], axis=-1), sink).astype(v.dtype)
    o = (jnp.einsum('bhgnqm,bmhd->bnqhgd', p[..., :N_META], vm)
         + jnp.einsum('bhgnqk,bnkhd->bnqhgd', p[..., N_META:], vw))
    o_real = o.reshape(B, n_blk * BLOCK, SWA_Q_HEADS, HEAD_DIM)
    return jnp.concatenate([o_meta, o_real], axis=1)


def forgetting_attention(q, k, v, f_logit):
    B, L = q.shape[0], q.shape[1]
    n_blk = (L - N_META) // BLOCK
    scale = HEAD_DIM ** -0.5
    cum = jnp.cumsum(jax.nn.log_sigmoid(f_logit.astype(jnp.float32)), axis=1).transpose(0, 2, 1)

    mi = jnp.arange(N_META)
    s = jnp.einsum('bqhd,bkhd->bhqk', q[:, :N_META], k[:, :N_META], preferred_element_type=jnp.float32) * scale
    s = s + cum[:, :, :N_META, None] - cum[:, :, None, :N_META]
    s = jnp.where(mi[:, None] >= mi[None, :], s, NEG_INF)
    p = jax.nn.softmax(s, axis=-1).astype(v.dtype)
    o_meta = jnp.einsum('bhqk,bkhd->bqhd', p, v[:, :N_META])

    qb = q[:, N_META:].reshape(B, n_blk, BLOCK, FOX_HEADS, HEAD_DIM).transpose(1, 0, 2, 3, 4)
    cb = cum[:, :, N_META:].reshape(B, FOX_HEADS, n_blk, BLOCK).transpose(2, 0, 1, 3)
    k_pos = jnp.arange(L)

    def one_block(args):
        q_blk, c_blk, b = args
        q_pos = N_META + b * BLOCK + jnp.arange(BLOCK)
        sb = jnp.einsum('bqhd,bkhd->bhqk', q_blk, k, preferred_element_type=jnp.float32) * scale
        sb = sb + c_blk[..., None] - cum[:, :, None, :]
        sb = jnp.where(k_pos[None, :] <= q_pos[:, None], sb, NEG_INF)
        pb = jax.nn.softmax(sb, axis=-1).astype(v.dtype)
        return jnp.einsum('bhqk,bkhd->bqhd', pb, v)

    o = lax.map(one_block, (qb, cb, jnp.arange(n_blk)))
    o_real = o.transpose(1, 0, 2, 3, 4).reshape(B, n_blk * BLOCK, FOX_HEADS, HEAD_DIM)
    return jnp.concatenate([o_meta, o_real], axis=1)


def _fwd_setup_inputs(seed: int = 0) -> dict:
    key = jax.random.key(seed)
    ks = jax.random.split(key, 14)
    f32 = jnp.float32
    gain = lambda k: 1.0 + 0.05 * jax.random.normal(k, (DEPTH, D_MODEL), f32)
    return {
        "x": jax.random.normal(ks[0], (BATCH, SEQ, D_MODEL), f32),
        "meta_tokens": jax.random.normal(ks[1], (N_META, D_MODEL), f32),
        "rel_bias": 0.5 * jax.random.normal(ks[2], (N_BUCKETS, SWA_Q_HEADS), f32),
        "ln_pre_mix": gain(ks[3]),
        "ln_post_mix": gain(ks[4]),
        "ln_pre_ffn": gain(ks[5]),
        "ln_post_ffn": gain(ks[6]),
        "w_in": jax.random.normal(ks[7], (DEPTH, D_MODEL, D_PROJ), f32) * D_MODEL ** -0.5,
        "b_forget": jax.random.uniform(ks[8], (DEPTH, FOX_HEADS), f32, minval=1.0, maxval=5.0),
        "sinks": 0.5 * jax.random.normal(ks[9], (DEPTH, SWA_Q_HEADS), f32),
        "w_out": jax.random.normal(ks[10], (DEPTH, D_MIX, D_MODEL), f32) * D_MIX ** -0.5,
        "w_gate_up": jax.random.normal(ks[11], (DEPTH, D_MODEL, 2 * D_FF), f32) * D_MODEL ** -0.5,
        "w_down": jax.random.normal(ks[12], (DEPTH, D_FF, D_MODEL), f32) * D_FF ** -0.5,
    }


def _fwd_reference(x, meta_tokens, rel_bias, ln_pre_mix, ln_post_mix, ln_pre_ffn, ln_post_ffn,
              w_in, b_forget, sinks, w_out, w_gate_up, w_down):
    B = x.shape[0]
    meta = jnp.broadcast_to(meta_tokens[None].astype(x.dtype), (B, N_META, D_MODEL))
    h = jnp.concatenate([meta, x], axis=1)
    L = h.shape[1]
    for layer in range(DEPTH):
        hn = rms_norm(h, ln_pre_mix[layer])
        proj = jnp.einsum('bld,dc->blc', hn, w_in[layer])
        q_a = proj[..., :OFF_QA].reshape(B, L, SWA_Q_HEADS, HEAD_DIM)
        k_a = proj[..., OFF_QA:OFF_KA].reshape(B, L, SWA_KV_HEADS, HEAD_DIM)
        v_a = proj[..., OFF_KA:OFF_VA].reshape(B, L, SWA_KV_HEADS, HEAD_DIM)
        q_b = proj[..., OFF_VA:OFF_QB].reshape(B, L, FOX_HEADS, HEAD_DIM)
        k_b = proj[..., OFF_QB:OFF_KB].reshape(B, L, FOX_HEADS, HEAD_DIM)
        v_b = proj[..., OFF_KB:OFF_VB].reshape(B, L, FOX_HEADS, HEAD_DIM)
        f_b = proj[..., OFF_VB:] + b_forget[layer].astype(proj.dtype)
        o_a = swa_sink_attention(q_a, k_a, v_a, sinks[layer], rel_bias)
        o_b = forgetting_attention(q_b, k_b, v_b, f_b)
        mix = jnp.concatenate([o_a.reshape(B, L, SWA_Q_W), o_b.reshape(B, L, FOX_W)], axis=-1)
        h = h + rms_norm(jnp.einsum('blc,cd->bld', mix, w_out[layer]), ln_post_mix[layer])
        hn = rms_norm(h, ln_pre_ffn[layer])
        gu = jnp.einsum('bld,df->blf', hn, w_gate_up[layer])
        ff = jnp.einsum('blf,fd->bld', jax.nn.silu(gu[..., :D_FF]) * gu[..., D_FF:], w_down[layer])
        h = h + rms_norm(ff, ln_post_ffn[layer])
    return h[:, N_META:]


import jax as _jax
import jax.numpy as _jnp

TWIN_FORMAT = 'train_step'
FWD_PARAMS = ['x', 'meta_tokens', 'rel_bias', 'ln_pre_mix', 'ln_post_mix', 'ln_pre_ffn', 'ln_post_ffn', 'w_in', 'b_forget', 'sinks', 'w_out', 'w_gate_up', 'w_down']
TWIN_WEIGHTS = ['meta_tokens', 'rel_bias', 'ln_pre_mix', 'ln_post_mix', 'ln_pre_ffn', 'ln_post_ffn', 'w_in', 'b_forget', 'sinks', 'w_out', 'w_gate_up', 'w_down']
TWIN_DIFF_INPUT = 'x'
TWIN_INPUTS = ['x', 'meta_tokens', 'rel_bias', 'ln_pre_mix', 'ln_post_mix', 'ln_pre_ffn', 'ln_post_ffn', 'w_in', 'b_forget', 'sinks', 'w_out', 'w_gate_up', 'w_down', 'loss_target', 'm_meta_tokens', 'm_rel_bias', 'm_ln_pre_mix', 'm_ln_post_mix', 'm_ln_pre_ffn', 'm_ln_post_ffn', 'm_w_in', 'm_b_forget', 'm_sinks', 'm_w_out', 'm_w_gate_up', 'm_w_down', 'v_meta_tokens', 'v_rel_bias', 'v_ln_pre_mix', 'v_ln_post_mix', 'v_ln_pre_ffn', 'v_ln_post_ffn', 'v_w_in', 'v_b_forget', 'v_sinks', 'v_w_out', 'v_w_gate_up', 'v_w_down']
TWIN_OUTPUTS = ['loss', 'grad_x', 'grad_meta_tokens', 'grad_rel_bias', 'grad_ln_pre_mix', 'grad_ln_post_mix', 'grad_ln_pre_ffn', 'grad_ln_post_ffn', 'grad_w_in', 'grad_b_forget', 'grad_sinks', 'grad_w_out', 'grad_w_gate_up', 'grad_w_down', 'delta_meta_tokens', 'delta_rel_bias', 'delta_ln_pre_mix', 'delta_ln_post_mix', 'delta_ln_pre_ffn', 'delta_ln_post_ffn', 'delta_w_in', 'delta_b_forget', 'delta_sinks', 'delta_w_out', 'delta_w_gate_up', 'delta_w_down', 'new_m_meta_tokens', 'new_m_rel_bias', 'new_m_ln_pre_mix', 'new_m_ln_post_mix', 'new_m_ln_pre_ffn', 'new_m_ln_post_ffn', 'new_m_w_in', 'new_m_b_forget', 'new_m_sinks', 'new_m_w_out', 'new_m_w_gate_up', 'new_m_w_down', 'new_v_meta_tokens', 'new_v_rel_bias', 'new_v_ln_pre_mix', 'new_v_ln_post_mix', 'new_v_ln_pre_ffn', 'new_v_ln_post_ffn', 'new_v_w_in', 'new_v_b_forget', 'new_v_sinks', 'new_v_w_out', 'new_v_w_gate_up', 'new_v_w_down']
TWIN_LEAF_KINDS = {'loss': 'loss', 'grad_x': 'grad_x', 'grad_meta_tokens': 'grad_w', 'grad_rel_bias': 'grad_w', 'grad_ln_pre_mix': 'grad_w', 'grad_ln_post_mix': 'grad_w', 'grad_ln_pre_ffn': 'grad_w', 'grad_ln_post_ffn': 'grad_w', 'grad_w_in': 'grad_w', 'grad_b_forget': 'grad_w', 'grad_sinks': 'grad_w', 'grad_w_out': 'grad_w', 'grad_w_gate_up': 'grad_w', 'grad_w_down': 'grad_w', 'delta_meta_tokens': 'delta_w', 'delta_rel_bias': 'delta_w', 'delta_ln_pre_mix': 'delta_w', 'delta_ln_post_mix': 'delta_w', 'delta_ln_pre_ffn': 'delta_w', 'delta_ln_post_ffn': 'delta_w', 'delta_w_in': 'delta_w', 'delta_b_forget': 'delta_w', 'delta_sinks': 'delta_w', 'delta_w_out': 'delta_w', 'delta_w_gate_up': 'delta_w', 'delta_w_down': 'delta_w', 'new_m_meta_tokens': 'new_m', 'new_m_rel_bias': 'new_m', 'new_m_ln_pre_mix': 'new_m', 'new_m_ln_post_mix': 'new_m', 'new_m_ln_pre_ffn': 'new_m', 'new_m_ln_post_ffn': 'new_m', 'new_m_w_in': 'new_m', 'new_m_b_forget': 'new_m', 'new_m_sinks': 'new_m', 'new_m_w_out': 'new_m', 'new_m_w_gate_up': 'new_m', 'new_m_w_down': 'new_m', 'new_v_meta_tokens': 'new_v', 'new_v_rel_bias': 'new_v', 'new_v_ln_pre_mix': 'new_v', 'new_v_ln_post_mix': 'new_v', 'new_v_ln_pre_ffn': 'new_v', 'new_v_ln_post_ffn': 'new_v', 'new_v_w_in': 'new_v', 'new_v_b_forget': 'new_v', 'new_v_sinks': 'new_v', 'new_v_w_out': 'new_v', 'new_v_w_gate_up': 'new_v', 'new_v_w_down': 'new_v'}


def _forward(args):
    return _fwd_reference(*[args[k] for k in FWD_PARAMS])


def _output_shape():
    def fwd():
        inp = _fwd_setup_inputs(0)
        return _fwd_reference(*[inp[k] for k in FWD_PARAMS])
    out = _jax.eval_shape(fwd)
    return out.shape, out.dtype

N_MICROBATCH = 1
ADAM_LR = 0.001
ADAM_B1 = 0.9
ADAM_B2 = 0.999
ADAM_EPS = 1e-08
ADAM_WD = 0.01
ADAM_STEP = 10
PER_EXAMPLE_BATCH_AXIS = {'x': 0, 'loss_target': 0}
SHARED_INPUTS = []
_WEIGHT_DTYPES = {'meta_tokens': _jnp.float32, 'rel_bias': _jnp.float32, 'ln_pre_mix': _jnp.float32, 'ln_post_mix': _jnp.float32, 'ln_pre_ffn': _jnp.float32, 'ln_post_ffn': _jnp.float32, 'w_in': _jnp.float32, 'b_forget': _jnp.float32, 'sinks': _jnp.float32, 'w_out': _jnp.float32, 'w_gate_up': _jnp.float32, 'w_down': _jnp.float32}
MOMENT_SCALE = {'meta_tokens': 4.125696e-02, 'rel_bias': 3.846075e-01, 'ln_pre_mix': 7.520215e-01, 'ln_post_mix': 3.197812e+01, 'ln_pre_ffn': 6.349894e-01, 'ln_post_ffn': 3.200197e+01, 'w_in': 5.120380e-01, 'b_forget': 5.145181e+00, 'sinks': 3.517688e-02, 'w_out': 6.903657e-01, 'w_gate_up': 2.779514e-01, 'w_down': 5.770451e-01}


def _to_microbatches(a, axis):
    t = _jnp.moveaxis(a, axis, 0)
    t = t.reshape((N_MICROBATCH, t.shape[0] // N_MICROBATCH) + t.shape[1:])
    return _jnp.moveaxis(t, 1, axis + 1)


def setup_inputs(seed: int = 0) -> dict:
    inp = _fwd_setup_inputs(seed)
    key = _jax.random.fold_in(_jax.random.key(seed), 7919)
    shape, _ = _output_shape()
    out = dict(inp)
    out["loss_target"] = _jax.random.normal(_jax.random.fold_in(key, 0), shape, _jnp.float32)
    for i, name in enumerate(TWIN_WEIGHTS):
        w = inp[name].astype(_jnp.float32)
        if MOMENT_SCALE is None:
            s = _jnp.sqrt(_jnp.mean(_jnp.square(w)) + 1e-30)
        else:
            s = MOMENT_SCALE[name]
        km, kv = _jax.random.split(_jax.random.fold_in(key, i + 1))
        out[name] = w
        out["m_" + name] = s * _jax.random.normal(km, w.shape, _jnp.float32)
        out["v_" + name] = (s * s) * _jax.random.uniform(kv, w.shape, _jnp.float32, 0.5, 1.5)
    if N_MICROBATCH > 1:
        for name, axis in PER_EXAMPLE_BATCH_AXIS.items():
            out[name] = _to_microbatches(out[name], axis)
    return {'x': out['x'], 'meta_tokens': out['meta_tokens'], 'rel_bias': out['rel_bias'], 'ln_pre_mix': out['ln_pre_mix'], 'ln_post_mix': out['ln_post_mix'], 'ln_pre_ffn': out['ln_pre_ffn'], 'ln_post_ffn': out['ln_post_ffn'], 'w_in': out['w_in'], 'b_forget': out['b_forget'], 'sinks': out['sinks'], 'w_out': out['w_out'], 'w_gate_up': out['w_gate_up'], 'w_down': out['w_down'], 'loss_target': out['loss_target'], 'm_meta_tokens': out['m_meta_tokens'], 'm_rel_bias': out['m_rel_bias'], 'm_ln_pre_mix': out['m_ln_pre_mix'], 'm_ln_post_mix': out['m_ln_post_mix'], 'm_ln_pre_ffn': out['m_ln_pre_ffn'], 'm_ln_post_ffn': out['m_ln_post_ffn'], 'm_w_in': out['m_w_in'], 'm_b_forget': out['m_b_forget'], 'm_sinks': out['m_sinks'], 'm_w_out': out['m_w_out'], 'm_w_gate_up': out['m_w_gate_up'], 'm_w_down': out['m_w_down'], 'v_meta_tokens': out['v_meta_tokens'], 'v_rel_bias': out['v_rel_bias'], 'v_ln_pre_mix': out['v_ln_pre_mix'], 'v_ln_post_mix': out['v_ln_post_mix'], 'v_ln_pre_ffn': out['v_ln_pre_ffn'], 'v_ln_post_ffn': out['v_ln_post_ffn'], 'v_w_in': out['v_w_in'], 'v_b_forget': out['v_b_forget'], 'v_sinks': out['v_sinks'], 'v_w_out': out['v_w_out'], 'v_w_gate_up': out['v_w_gate_up'], 'v_w_down': out['v_w_down']}


def _loss(weights, diff, rest, loss_target):
    with _jax.named_scope("forward"):
        args = {**rest, TWIN_DIFF_INPUT: diff, **{k: w.astype(_WEIGHT_DTYPES[k]) for k, w in weights.items()}}
        y = _forward(args)
    with _jax.named_scope("loss_head"):
        err = _jnp.square(y.astype(_jnp.float32) - loss_target)
        return 0.5 * _jnp.sum(_jnp.mean(err, axis=-1)) if err.ndim else 0.5 * err


def _adamw(w, g, m, v):
    m = ADAM_B1 * m + (1.0 - ADAM_B1) * g
    v = ADAM_B2 * v + (1.0 - ADAM_B2) * _jnp.square(g)
    m_hat = m / (1.0 - ADAM_B1 ** ADAM_STEP)
    v_hat = v / (1.0 - ADAM_B2 ** ADAM_STEP)
    delta = -ADAM_LR * (m_hat / (_jnp.sqrt(v_hat) + ADAM_EPS) + ADAM_WD * w)
    return delta, m, v


def reference(x, meta_tokens, rel_bias, ln_pre_mix, ln_post_mix, ln_pre_ffn, ln_post_ffn, w_in, b_forget, sinks, w_out, w_gate_up, w_down, loss_target, m_meta_tokens, m_rel_bias, m_ln_pre_mix, m_ln_post_mix, m_ln_pre_ffn, m_ln_post_ffn, m_w_in, m_b_forget, m_sinks, m_w_out, m_w_gate_up, m_w_down, v_meta_tokens, v_rel_bias, v_ln_pre_mix, v_ln_post_mix, v_ln_pre_ffn, v_ln_post_ffn, v_w_in, v_b_forget, v_sinks, v_w_out, v_w_gate_up, v_w_down):
    given = dict(x=x, meta_tokens=meta_tokens, rel_bias=rel_bias, ln_pre_mix=ln_pre_mix, ln_post_mix=ln_post_mix, ln_pre_ffn=ln_pre_ffn, ln_post_ffn=ln_post_ffn, w_in=w_in, b_forget=b_forget, sinks=sinks, w_out=w_out, w_gate_up=w_gate_up, w_down=w_down, loss_target=loss_target, m_meta_tokens=m_meta_tokens, m_rel_bias=m_rel_bias, m_ln_pre_mix=m_ln_pre_mix, m_ln_post_mix=m_ln_post_mix, m_ln_pre_ffn=m_ln_pre_ffn, m_ln_post_ffn=m_ln_post_ffn, m_w_in=m_w_in, m_b_forget=m_b_forget, m_sinks=m_sinks, m_w_out=m_w_out, m_w_gate_up=m_w_gate_up, m_w_down=m_w_down, v_meta_tokens=v_meta_tokens, v_rel_bias=v_rel_bias, v_ln_pre_mix=v_ln_pre_mix, v_ln_post_mix=v_ln_post_mix, v_ln_pre_ffn=v_ln_pre_ffn, v_ln_post_ffn=v_ln_post_ffn, v_w_in=v_w_in, v_b_forget=v_b_forget, v_sinks=v_sinks, v_w_out=v_w_out, v_w_gate_up=v_w_gate_up, v_w_down=v_w_down)
    weights = {n: given[n] for n in TWIN_WEIGHTS}
    shared = {n: given[n] for n in SHARED_INPUTS}
    per_example = {n: given[n] for n in ['x']}
    grad_fn = _jax.value_and_grad(_loss, argnums=(0, 1))

    def one_microbatch(ex, loss_target):
        ex = dict(ex)
        diff = ex.pop(TWIN_DIFF_INPUT)
        return grad_fn(weights, diff, {**shared, **ex}, loss_target)

    if N_MICROBATCH == 1:
        loss, (grad_w, grad_x) = one_microbatch(per_example, given["loss_target"])
    else:
        def body(carry, xs):
            loss_sum, grad_sum = carry
            l_k, (gw_k, gx_k) = one_microbatch(xs[0], xs[1])
            with _jax.named_scope("update"):
                return (loss_sum + l_k, _jax.tree.map(_jnp.add, grad_sum, gw_k)), gx_k

        init = (_jnp.zeros((), _jnp.float32), _jax.tree.map(_jnp.zeros_like, weights))
        (loss, grad_w), grad_x = _jax.lax.scan(body, init, (per_example, given["loss_target"]))
    with _jax.named_scope("update"):
        delta_w, new_m, new_v = {}, {}, {}
        for n in TWIN_WEIGHTS:
            delta_w[n], new_m[n], new_v[n] = _adamw(weights[n], grad_w[n], given["m_" + n], given["v_" + n])
    return (loss, grad_x, *[grad_w[n] for n in TWIN_WEIGHTS], *[delta_w[n] for n in TWIN_WEIGHTS],
            *[new_m[n] for n in TWIN_WEIGHTS], *[new_v[n] for n in TWIN_WEIGHTS])
```

```python
import math

import numpy as np
import jax
import jax.numpy as jnp
from jax import lax
from jax.experimental import pallas as pl
from jax.experimental.pallas import tpu as pltpu

F32 = jnp.float32
BF16 = jnp.bfloat16
HIGHEST = lax.Precision.HIGHEST
MESH = pl.DeviceIdType.MESH

N_DEV = 8
D_MODEL = 1024
N_META = 16
HEAD_DIM = 64
SWA_Q_HEADS = 8
SWA_KV_HEADS = 2
SWA_GROUP = 4
FOX_HEADS = 8
BLOCK = 128
PAD_ROWS = BLOCK - N_META
N_BUCKETS = 32
MAX_DISTANCE = 128
D_FF = 2816
D_QKV = 2304
D_PROJ = D_QKV + FOX_HEADS
D_PROJ_PAD = 2560
EPS = 1e-6
NEG = -1e30
SCALE = HEAD_DIM ** -0.5
ADAM_LR, ADAM_B1, ADAM_B2, ADAM_EPS, ADAM_WD, ADAM_STEP = 0.001, 0.9, 0.999, 1e-08, 0.01, 10
VMEM_LIMIT = 48 * 1024 * 1024
FOX_TILE = 384
PACK_ROWS = 1536
SMALL_ROWS = 8

NT = (((1,), (1,)), ((), ()))
NN = (((1,), (0,)), ((), ()))
TN = (((0,), (0,)), ((), ()))


def _params(sem=None, **kw):
    if sem is not None:
        kw["dimension_semantics"] = sem
    return pltpu.CompilerParams(vmem_limit_bytes=VMEM_LIMIT, **kw)


def _tile(n, target, mult=16):
    best = None
    for t in range(mult, min(n, target) + 1, mult):
        if n % t == 0:
            best = t
    assert best is not None, (n, target)
    return best


def _matmul(a, b, *, nt=False, out_dtype, tm, tn, tk=None, name):
    M, K = a.shape
    N = b.shape[0] if nt else b.shape[1]
    tk = K if tk is None else tk
    assert M % tm == 0 and N % tn == 0 and K % tk == 0, (name, a.shape, b.shape, tm, tn, tk)
    nk = K // tk
    dn = NT if nt else NN

    def body(a_ref, b_ref, o_ref, *scr):
        part = lax.dot_general(a_ref[...], b_ref[...], dn, preferred_element_type=F32)
        if nk == 1:
            o_ref[...] = part.astype(o_ref.dtype)
        else:
            acc = scr[0]
            k = pl.program_id(2)

            @pl.when(k == 0)
            def _():
                acc[...] = part

            @pl.when(k > 0)
            def _():
                acc[...] += part

            @pl.when(k == nk - 1)
            def _():
                o_ref[...] = acc[...].astype(o_ref.dtype)

    if nt:
        b_spec = pl.BlockSpec((tn, tk), lambda i, j, k: (j, k))
    else:
        b_spec = pl.BlockSpec((tk, tn), lambda i, j, k: (k, j))
    return pl.pallas_call(
        body,
        out_shape=jax.ShapeDtypeStruct((M, N), out_dtype),
        grid=(M // tm, N // tn, nk),
        in_specs=[pl.BlockSpec((tm, tk), lambda i, j, k: (i, k)), b_spec],
        out_specs=pl.BlockSpec((tm, tn), lambda i, j, k: (i, j)),
        scratch_shapes=[pltpu.VMEM((tm, tn), F32)] if nk > 1 else [],
        compiler_params=_params(("parallel", "parallel", "arbitrary")),
        name=name,
    )(a, b)


def _rstd(x):
    return lax.rsqrt(jnp.mean(x * x, axis=-1, keepdims=True) + EPS)


def _rms_fwd(x, g, *, name):
    T, D = x.shape
    tm = _tile(T, 512)

    def body(x_ref, g_ref, o_ref):
        x = x_ref[...]
        o_ref[...] = (x * _rstd(x) * g_ref[...]).astype(o_ref.dtype)

    return pl.pallas_call(
        body, out_shape=jax.ShapeDtypeStruct((T, D), BF16), grid=(T // tm,),
        in_specs=[pl.BlockSpec((tm, D), lambda i: (i, 0)), pl.BlockSpec((1, D), lambda i: (0, 0))],
        out_specs=pl.BlockSpec((tm, D), lambda i: (i, 0)),
        compiler_params=_params(("parallel",)), name=name)(x, g)


def _post_res(a, g, h, *, name):
    T, D = a.shape
    tm = _tile(T, 512)

    def body(a_ref, g_ref, h_ref, o_ref):
        a = a_ref[...]
        o_ref[...] = h_ref[...] + a * _rstd(a) * g_ref[...]

    row = pl.BlockSpec((tm, D), lambda i: (i, 0))
    return pl.pallas_call(
        body, out_shape=jax.ShapeDtypeStruct((T, D), F32), grid=(T // tm,),
        in_specs=[row, pl.BlockSpec((1, D), lambda i: (0, 0)), row], out_specs=row,
        compiler_params=_params(("parallel",)), name=name)(a, g, h)


def _loss_head(a, g, h, target, *, name):
    T, D = a.shape
    tm = _tile(T, 512)

    def body(a_ref, g_ref, h_ref, t_ref, dy_ref, loss_ref):
        i = pl.program_id(0)
        a = a_ref[...]
        y = h_ref[...] + a * _rstd(a) * g_ref[...]
        rows = i * tm + lax.broadcasted_iota(jnp.int32, (tm, 1), 0)
        err = jnp.where(rows >= BLOCK, y - t_ref[...], 0.0)
        dy_ref[...] = err / D
        part = jnp.sum(jnp.sum(err * err, axis=1, keepdims=True), axis=0, keepdims=True)

        @pl.when(i == 0)
        def _():
            loss_ref[...] = jnp.zeros_like(loss_ref)

        loss_ref[...] += jnp.broadcast_to(part, loss_ref.shape)

    row = pl.BlockSpec((tm, D), lambda i: (i, 0))
    return pl.pallas_call(
        body, out_shape=(jax.ShapeDtypeStruct((T, D), F32), jax.ShapeDtypeStruct((8, 128), F32)),
        grid=(T // tm,),
        in_specs=[row, pl.BlockSpec((1, D), lambda i: (0, 0)), row, row],
        out_specs=(row, pl.BlockSpec((8, 128), lambda i: (0, 0))),
        compiler_params=_params(("arbitrary",)), name=name)(a, g, h, target)


def _rms_bwd(x, g, dy, res, *, out_dtype, name):
    T, D = x.shape
    tm = _tile(T, 512)
    has_res = res is not None

    def body(*refs):
        if has_res:
            x_ref, g_ref, dy_ref, r_ref, dx_ref, dg_ref = refs
        else:
            x_ref, g_ref, dy_ref, dx_ref, dg_ref = refs
        i = pl.program_id(0)
        x = x_ref[...]
        dy = dy_ref[...].astype(F32)
        r = _rstd(x)
        xh = x * r
        dxh = dy * g_ref[...]
        dx = r * (dxh - xh * jnp.mean(dxh * xh, axis=-1, keepdims=True))
        if has_res:
            dx = dx + r_ref[...]
        dx_ref[...] = dx.astype(dx_ref.dtype)

        @pl.when(i == 0)
        def _():
            dg_ref[...] = jnp.zeros_like(dg_ref)

        dg_ref[...] += jnp.sum(dy * xh, axis=0, keepdims=True)

    row = pl.BlockSpec((tm, D), lambda i: (i, 0))
    vec = pl.BlockSpec((1, D), lambda i: (0, 0))
    ins = [x, g, dy] + ([res] if has_res else [])
    return pl.pallas_call(
        body, out_shape=(jax.ShapeDtypeStruct((T, D), out_dtype), jax.ShapeDtypeStruct((1, D), F32)),
        grid=(T // tm,),
        in_specs=[row, vec, row] + ([row] if has_res else []),
        out_specs=(row, vec),
        compiler_params=_params(("arbitrary",)), name=name)(*ins)


def _swiglu_fwd(gu, *, name):
    T, F2 = gu.shape
    F = F2 // 2
    tm = _tile(T, 384)

    def body(g_ref, u_ref, o_ref):
        g = g_ref[...].astype(F32)
        o_ref[...] = (g / (1.0 + jnp.exp(-g)) * u_ref[...].astype(F32)).astype(o_ref.dtype)

    return pl.pallas_call(
        body, out_shape=jax.ShapeDtypeStruct((T, F), BF16), grid=(T // tm,),
        in_specs=[pl.BlockSpec((tm, F), lambda i: (i, 0)), pl.BlockSpec((tm, F), lambda i: (i, 1))],
        out_specs=pl.BlockSpec((tm, F), lambda i: (i, 0)),
        compiler_params=_params(("parallel",)), name=name)(gu, gu)


def _swiglu_bwd(gu, dact, *, name):
    T, F2 = gu.shape
    F = F2 // 2
    tm = _tile(T, 384)

    def body(g_ref, u_ref, d_ref, o_ref):
        g = g_ref[...].astype(F32)
        u = u_ref[...].astype(F32)
        d = d_ref[...].astype(F32)
        sg = 1.0 / (1.0 + jnp.exp(-g))
        o_ref[:, :F] = (d * u * (sg * (1.0 + g * (1.0 - sg)))).astype(o_ref.dtype)
        o_ref[:, F:] = (d * (g * sg)).astype(o_ref.dtype)

    return pl.pallas_call(
        body, out_shape=jax.ShapeDtypeStruct((T, F2), BF16), grid=(T // tm,),
        in_specs=[pl.BlockSpec((tm, F), lambda i: (i, 0)), pl.BlockSpec((tm, F), lambda i: (i, 1)),
                  pl.BlockSpec((tm, F), lambda i: (i, 0))],
        out_specs=pl.BlockSpec((tm, F2), lambda i: (i, 0)),
        compiler_params=_params(("parallel",)), name=name)(gu, gu, dact)


def _fox_gates_fwd(f_t, b, *, name):
    H, T = f_t.shape
    nb = T // BLOCK

    def body(f_ref, b_ref, cum_ref):
        f = f_ref[...] + b_ref[...]
        ls = jnp.minimum(f, 0.0) - jnp.log(1.0 + jnp.exp(-jnp.abs(f)))
        t = lax.broadcasted_iota(jnp.int32, (H, T), 1)
        ls = jnp.where(t >= PAD_ROWS, ls, 0.0)
        upper = (lax.broadcasted_iota(jnp.int32, (BLOCK, BLOCK), 0)
                 <= lax.broadcasted_iota(jnp.int32, (BLOCK, BLOCK), 1)).astype(F32)
        carry = jnp.zeros((H, 1), F32)
        for blk in range(nb):
            seg = ls[:, blk * BLOCK:(blk + 1) * BLOCK]
            pre = jnp.dot(seg, upper, precision=HIGHEST, preferred_element_type=F32) + carry
            cum_ref[:, blk * BLOCK:(blk + 1) * BLOCK] = pre
            carry = pre[:, BLOCK - 1:BLOCK]

    return pl.pallas_call(
        body, out_shape=jax.ShapeDtypeStruct((H, T), F32),
        in_specs=[pl.BlockSpec(memory_space=pltpu.VMEM), pl.BlockSpec(memory_space=pltpu.VMEM)],
        out_specs=pl.BlockSpec(memory_space=pltpu.VMEM),
        compiler_params=_params(), name=name)(f_t, b)


def _fox_gates_bwd(dcq, dck, f_t, b, *, name):
    H, T = f_t.shape
    nb = T // BLOCK

    def body(dq_ref, d_ref, f_ref, b_ref, df_ref, db_ref):
        lower = (lax.broadcasted_iota(jnp.int32, (BLOCK, BLOCK), 0)
                 >= lax.broadcasted_iota(jnp.int32, (BLOCK, BLOCK), 1)).astype(F32)
        carry = jnp.zeros((H, 1), F32)
        for blk in range(nb - 1, -1, -1):
            seg = dq_ref[:, blk * BLOCK:(blk + 1) * BLOCK] - d_ref[:, blk * BLOCK:(blk + 1) * BLOCK]
            suf = jnp.dot(seg, lower, precision=HIGHEST, preferred_element_type=F32) + carry
            df_ref[:, blk * BLOCK:(blk + 1) * BLOCK] = suf
            carry = suf[:, 0:1]
        f = f_ref[...] + b_ref[...]
        t = lax.broadcasted_iota(jnp.int32, (H, T), 1)
        df = jnp.where(t >= PAD_ROWS, df_ref[...] / (1.0 + jnp.exp(f)), 0.0)
        df_ref[...] = df
        db_ref[...] = jnp.sum(df, axis=1, keepdims=True)

    vm = pl.BlockSpec(memory_space=pltpu.VMEM)
    return pl.pallas_call(
        body, out_shape=(jax.ShapeDtypeStruct((H, T), F32), jax.ShapeDtypeStruct((H, 1), F32)),
        in_specs=[vm, vm, vm, vm], out_specs=(vm, vm),
        compiler_params=_params(), name=name)(dcq, dck, f_t, b)


def _fox_fwd(q, k, v, cq_col, ck_row, *, name):
    H, T, dh = q.shape
    tq = FOX_TILE
    nq = T // tq

    def body(q_ref, k_ref, v_ref, cq_ref, ck_ref, o_ref, lse_ref, m_scr, l_scr, acc_scr):
        i = pl.program_id(1)
        qs = q_ref[...] * SCALE
        cqv = cq_ref[...]
        m_scr[...] = jnp.full(m_scr.shape, NEG, F32)
        l_scr[...] = jnp.zeros(l_scr.shape, F32)
        acc_scr[...] = jnp.zeros(acc_scr.shape, F32)

        def step(kb, diag):
            off = pl.multiple_of(kb * tq, tq)
            kblk = k_ref[pl.ds(off, tq), :]
            vblk = v_ref[pl.ds(off, tq), :]
            s = lax.dot_general(qs, kblk, NT, preferred_element_type=F32)
            s = s + (cqv - ck_ref[kb])
            if diag:
                r = lax.broadcasted_iota(jnp.int32, (tq, tq), 0)
                c = lax.broadcasted_iota(jnp.int32, (tq, tq), 1)
                s = jnp.where(r >= c, s, NEG)
            m_prev = m_scr[...]
            m_new = jnp.maximum(m_prev, jnp.max(s, axis=1, keepdims=True))
            p = jnp.exp(s - m_new)
            alpha = jnp.exp(m_prev - m_new)
            l_scr[...] = alpha * l_scr[...] + jnp.sum(p, axis=1, keepdims=True)
            acc_scr[...] = alpha * acc_scr[...] + jnp.dot(p.astype(BF16), vblk, preferred_element_type=F32)
            m_scr[...] = m_new

        def loop_body(kb, carry):
            step(kb, False)
            return carry

        lax.fori_loop(0, i, loop_body, 0)
        step(i, True)
        l = l_scr[...]
        o_ref[...] = (acc_scr[...] / l).astype(o_ref.dtype)
        lse_ref[...] = m_scr[...] + jnp.log(l)

    blk = pl.BlockSpec((None, tq, dh), lambda h, i: (h, i, 0))
    full = pl.BlockSpec((None, T, dh), lambda h, i: (h, 0, 0))
    col = pl.BlockSpec((None, tq, 1), lambda h, i: (h, i, 0))
    return pl.pallas_call(
        body,
        out_shape=(jax.ShapeDtypeStruct((H, T, dh), BF16), jax.ShapeDtypeStruct((H, T, 1), F32)),
        grid=(H, nq),
        in_specs=[blk, full, full, col, pl.BlockSpec((None, nq, 1, tq), lambda h, i: (h, 0, 0, 0))],
        out_specs=(blk, col),
        scratch_shapes=[pltpu.VMEM((tq, 1), F32), pltpu.VMEM((tq, 1), F32), pltpu.VMEM((tq, dh), F32)],
        compiler_params=_params(("parallel", "arbitrary")), name=name)(q, k, v, cq_col, ck_row)


def _row_dot(o, do, *, name):
    H, T, dh = o.shape
    tq = FOX_TILE
    nq = T // tq

    def body(o_ref, do_ref, d_ref):
        prod = o_ref[...].astype(F32) * do_ref[...].astype(F32)
        ones = jnp.ones((8, dh), F32)
        d = lax.dot_general(ones, prod, NT, precision=HIGHEST, preferred_element_type=F32)
        d_ref[...] = d[0:1, :]

    blk = pl.BlockSpec((None, tq, dh), lambda h, i: (h, i, 0))
    return pl.pallas_call(
        body, out_shape=jax.ShapeDtypeStruct((H, nq, 1, tq), F32), grid=(H, nq),
        in_specs=[blk, blk], out_specs=pl.BlockSpec((None, None, 1, tq), lambda h, i: (h, i, 0, 0)),
        compiler_params=_params(("parallel", "parallel")), name=name)(o, do)


def _fox_bwd(q, k, v, do, lse_row, delta_row, cq_row, ck_col, *, name):
    H, T, dh = q.shape
    tq = FOX_TILE
    nq = T // tq

    def body(q_ref, k_ref, v_ref, do_ref, lse_ref, dl_ref, cq_ref, ck_ref,
             dq_ref, dk_ref, dv_ref, dck_ref, dcq_ref, dk_acc, dv_acc, dck_acc):
        j = pl.program_id(1)

        @pl.when(j == 0)
        def _():
            dq_ref[...] = jnp.zeros(dq_ref.shape, F32)
            dcq_ref[...] = jnp.zeros(dcq_ref.shape, F32)

        ks = k_ref[...] * SCALE
        vb = v_ref[...]
        ckv = ck_ref[...]
        dk_acc[...] = jnp.zeros(dk_acc.shape, F32)
        dv_acc[...] = jnp.zeros(dv_acc.shape, F32)
        dck_acc[...] = jnp.zeros(dck_acc.shape, F32)

        def step(qb, diag):
            off = pl.multiple_of(qb * tq, tq)
            qblk = q_ref[pl.ds(off, tq), :]
            doblk = do_ref[pl.ds(off, tq), :]
            s_t = lax.dot_general(ks, qblk, NT, preferred_element_type=F32)
            p_t = jnp.exp(s_t + ((cq_ref[qb] - lse_ref[qb]) - ckv))
            if diag:
                r = lax.broadcasted_iota(jnp.int32, (tq, tq), 0)
                c = lax.broadcasted_iota(jnp.int32, (tq, tq), 1)
                p_t = jnp.where(c >= r, p_t, 0.0)
            dv_acc[...] += jnp.dot(p_t.astype(BF16), doblk, preferred_element_type=F32)
            dp_t = lax.dot_general(vb, doblk, NT, preferred_element_type=F32)
            ds_t = p_t * (dp_t - dl_ref[qb])
            dck_acc[...] += jnp.sum(ds_t, axis=1, keepdims=True)
            dcq_ref[qb] += jnp.sum(ds_t, axis=0, keepdims=True)
            dsb = ds_t.astype(BF16)
            dk_acc[...] += jnp.dot(dsb, qblk, preferred_element_type=F32)
            dq_ref[pl.ds(off, tq), :] += lax.dot_general(dsb, ks, TN, preferred_element_type=F32)

        step(j, True)

        def loop_body(qb, carry):
            step(qb, False)
            return carry

        lax.fori_loop(j + 1, nq, loop_body, 0)
        dk_ref[...] = (dk_acc[...] * SCALE).astype(dk_ref.dtype)
        dv_ref[...] = dv_acc[...].astype(dv_ref.dtype)
        dck_ref[...] = dck_acc[...]

    blk = pl.BlockSpec((None, tq, dh), lambda h, j: (h, j, 0))
    full = pl.BlockSpec((None, T, dh), lambda h, j: (h, 0, 0))
    rows = pl.BlockSpec((None, nq, 1, tq), lambda h, j: (h, 0, 0, 0))
    col = pl.BlockSpec((None, tq, 1), lambda h, j: (h, j, 0))
    return pl.pallas_call(
        body,
        out_shape=(jax.ShapeDtypeStruct((H, T, dh), F32), jax.ShapeDtypeStruct((H, T, dh), BF16),
                   jax.ShapeDtypeStruct((H, T, dh), BF16), jax.ShapeDtypeStruct((H, T, 1), F32),
                   jax.ShapeDtypeStruct((H, nq, 1, tq), F32)),
        grid=(H, nq),
        in_specs=[full, blk, blk, full, rows, rows, rows, col],
        out_specs=(full, blk, blk, col, rows),
        scratch_shapes=[pltpu.VMEM((tq, dh), F32), pltpu.VMEM((tq, dh), F32), pltpu.VMEM((tq, 1), F32)],
        compiler_params=_params(("parallel", "arbitrary")), name=name,
    )(q, k, v, do, lse_row, delta_row, cq_row, ck_col)


def _t5_bucket_np(d):
    n = np.maximum(d, 0).astype(np.int32)
    max_exact = N_BUCKETS // 2
    nf = np.maximum(n, 1).astype(np.float32)
    large = max_exact + (np.log(nf / max_exact) / math.log(MAX_DISTANCE / max_exact)
                         * (N_BUCKETS - max_exact)).astype(np.int32)
    large = np.minimum(large, N_BUCKETS - 1)
    return np.where(n < max_exact, n, large)


def _bucket_onehots():
    r = np.arange(BLOCK)[:, None]
    c = np.arange(BLOCK)[None, :]
    eye = np.eye(N_BUCKETS, dtype=np.float32)
    cur = eye[_t5_bucket_np(r - c).reshape(-1)]
    prev = eye[_t5_bucket_np(BLOCK + r - c).reshape(-1)]
    return cur, prev


def _swa_probs(qs, kc, kp, km, bc, bp, far, sink, n):
    r = lax.broadcasted_iota(jnp.int32, (BLOCK, BLOCK), 0)
    c = lax.broadcasted_iota(jnp.int32, (BLOCK, BLOCK), 1)
    never = 2 * BLOCK
    s_c = lax.dot_general(qs, kc, NT, preferred_element_type=F32) + bc
    s_p = lax.dot_general(qs, kp, NT, preferred_element_type=F32) + bp
    s_m = lax.dot_general(qs, km, NT, preferred_element_type=F32) + jnp.where(n == 1, bp, far)
    s_c = jnp.where((c <= r) & (c >= jnp.where(n >= 1, 0, PAD_ROWS)), s_c, NEG)
    s_p = jnp.where(c > r + jnp.where(n >= 2, 0, never), s_p, NEG)
    s_m = jnp.where(c >= jnp.where(n >= 1, PAD_ROWS, never), s_m, NEG)
    m = jnp.maximum(jnp.maximum(jnp.max(s_c, axis=1, keepdims=True), jnp.max(s_p, axis=1, keepdims=True)),
                    jnp.maximum(jnp.max(s_m, axis=1, keepdims=True), sink))
    e_c = jnp.exp(s_c - m)
    e_p = jnp.exp(s_p - m)
    e_m = jnp.exp(s_m - m)
    e_s = jnp.exp(sink - m)
    l = (jnp.sum(e_c, axis=1, keepdims=True) + jnp.sum(e_p, axis=1, keepdims=True)
         + jnp.sum(e_m, axis=1, keepdims=True) + e_s)
    return e_c, e_p, e_m, e_s, l


def _swa_specs(T):
    G = SWA_GROUP
    qblk = pl.BlockSpec((G, BLOCK, HEAD_DIM), lambda kv, n: (kv, n, 0))
    cur = pl.BlockSpec((None, BLOCK, HEAD_DIM), lambda kv, n: (kv, n + 1, 0))
    prev = pl.BlockSpec((None, BLOCK, HEAD_DIM), lambda kv, n: (kv, n, 0))
    meta = pl.BlockSpec((None, BLOCK, HEAD_DIM), lambda kv, n: (kv, 1, 0))
    bias = pl.BlockSpec((G, BLOCK, BLOCK), lambda kv, n: (kv, 0, 0))
    smem = pl.BlockSpec(memory_space=pltpu.SMEM)
    return qblk, cur, prev, meta, bias, smem


def _swa_fwd(q, kpad, vpad, bc, bp, far, sinks, *, name):
    Hq, T, dh = q.shape
    nb = T // BLOCK
    G = SWA_GROUP

    def body(q_ref, kc_ref, kp_ref, km_ref, vc_ref, vp_ref, vm_ref, bc_ref, bp_ref, far_ref, sink_ref, o_ref):
        kv = pl.program_id(0)
        n = pl.program_id(1)
        kc, kp, km = kc_ref[...], kp_ref[...], km_ref[...]
        vc, vp, vm = vc_ref[...], vp_ref[...], vm_ref[...]
        for g in range(G):
            h = kv * G + g
            qs = q_ref[g] * SCALE
            e_c, e_p, e_m, _, l = _swa_probs(qs, kc, kp, km, bc_ref[g], bp_ref[g], far_ref[h], sink_ref[h], n)
            o = (jnp.dot(e_c.astype(BF16), vc, preferred_element_type=F32)
                 + jnp.dot(e_p.astype(BF16), vp, preferred_element_type=F32)
                 + jnp.dot(e_m.astype(BF16), vm, preferred_element_type=F32))
            o_ref[g] = (o / l).astype(o_ref.dtype)

    qblk, cur, prev, meta, bias, smem = _swa_specs(T)
    return pl.pallas_call(
        body, out_shape=jax.ShapeDtypeStruct((Hq, T, dh), BF16), grid=(SWA_KV_HEADS, nb),
        in_specs=[qblk, cur, prev, meta, cur, prev, meta, bias, bias, smem, smem],
        out_specs=qblk,
        compiler_params=_params(("parallel", "parallel")), name=name,
    )(q, kpad, kpad, kpad, vpad, vpad, vpad, bc, bp, far, sinks)


def _swa_bwd(q, kpad, vpad, do, bc, bp, far, sinks, *, name):
    Hq, T, dh = q.shape
    nb = T // BLOCK
    G = SWA_GROUP

    def body(q_ref, kc_ref, kp_ref, km_ref, vc_ref, vp_ref, vm_ref, do_ref, bc_ref, bp_ref, far_ref, sink_ref,
             dq_ref, dk_ref, dv_ref, dbc_ref, dbp_ref, dbf_ref, dsk_ref):
        kv = pl.program_id(0)
        n = pl.program_id(1)

        @pl.when(n == 0)
        def _():
            for ref in (dk_ref, dv_ref, dbc_ref, dbp_ref, dbf_ref, dsk_ref):
                ref[...] = jnp.zeros(ref.shape, F32)

        kc, kp, km = kc_ref[...], kp_ref[...], km_ref[...]
        vc, vp, vm = vc_ref[...], vp_ref[...], vm_ref[...]
        dkc = dkp = dkm = dvc = dvp = dvm = jnp.zeros((BLOCK, dh), F32)
        for g in range(G):
            h = kv * G + g
            qs = q_ref[g] * SCALE
            e_c, e_p, e_m, e_s, l = _swa_probs(qs, kc, kp, km, bc_ref[g], bp_ref[g], far_ref[h], sink_ref[h], n)
            inv = 1.0 / l
            p_c, p_p, p_m = e_c * inv, e_p * inv, e_m * inv
            dob = do_ref[g]
            dp_c = lax.dot_general(dob, vc, NT, preferred_element_type=F32)
            dp_p = lax.dot_general(dob, vp, NT, preferred_element_type=F32)
            dp_m = lax.dot_general(dob, vm, NT, preferred_element_type=F32)
            delta = (jnp.sum(p_c * dp_c, axis=1, keepdims=True) + jnp.sum(p_p * dp_p, axis=1, keepdims=True)
                     + jnp.sum(p_m * dp_m, axis=1, keepdims=True))
            ds_c = p_c * (dp_c - delta)
            ds_p = p_p * (dp_p - delta)
            ds_m = p_m * (dp_m - delta)
            dsk_ref[g] += -(e_s * inv) * delta
            dbc_ref[g] += ds_c
            dbp_ref[g] += ds_p + jnp.where(n == 1, ds_m, 0.0)
            dbf_ref[g] += jnp.where(n >= 2, ds_m, 0.0)
            bc16, bp16, bm16 = ds_c.astype(BF16), ds_p.astype(BF16), ds_m.astype(BF16)
            dq = (jnp.dot(bc16, kc, preferred_element_type=F32) + jnp.dot(bp16, kp, preferred_element_type=F32)
                  + jnp.dot(bm16, km, preferred_element_type=F32))
            dq_ref[g] = (dq * SCALE).astype(dq_ref.dtype)
            dkc += lax.dot_general(bc16, qs, TN, preferred_element_type=F32)
            dkp += lax.dot_general(bp16, qs, TN, preferred_element_type=F32)
            dkm += lax.dot_general(bm16, qs, TN, preferred_element_type=F32)
            dvc += lax.dot_general(p_c.astype(BF16), dob, TN, preferred_element_type=F32)
            dvp += lax.dot_general(p_p.astype(BF16), dob, TN, preferred_element_type=F32)
            dvm += lax.dot_general(p_m.astype(BF16), dob, TN, preferred_element_type=F32)
        cur_off = pl.multiple_of((n + 1) * BLOCK, BLOCK)
        prev_off = pl.multiple_of(n * BLOCK, BLOCK)
        dk_ref[pl.ds(cur_off, BLOCK), :] += dkc
        dk_ref[pl.ds(prev_off, BLOCK), :] += dkp
        dk_ref[BLOCK:2 * BLOCK, :] += dkm
        dv_ref[pl.ds(cur_off, BLOCK), :] += dvc
        dv_ref[pl.ds(prev_off, BLOCK), :] += dvp
        dv_ref[BLOCK:2 * BLOCK, :] += dvm

    qblk, cur, prev, meta, bias, smem = _swa_specs(T)
    kvfull = pl.BlockSpec((None, T + BLOCK, dh), lambda kv, n: (kv, 0, 0))
    dsk = pl.BlockSpec((G, BLOCK, 1), lambda kv, n: (kv, 0, 0))
    return pl.pallas_call(
        body,
        out_shape=(jax.ShapeDtypeStruct((Hq, T, dh), BF16),
                   jax.ShapeDtypeStruct((SWA_KV_HEADS, T + BLOCK, dh), F32),
                   jax.ShapeDtypeStruct((SWA_KV_HEADS, T + BLOCK, dh), F32),
                   jax.ShapeDtypeStruct((Hq, BLOCK, BLOCK), F32), jax.ShapeDtypeStruct((Hq, BLOCK, BLOCK), F32),
                   jax.ShapeDtypeStruct((Hq, BLOCK, BLOCK), F32), jax.ShapeDtypeStruct((Hq, BLOCK, 1), F32)),
        grid=(SWA_KV_HEADS, nb),
        in_specs=[qblk, cur, prev, meta, cur, prev, meta, qblk, bias, bias, smem, smem],
        out_specs=(qblk, kvfull, kvfull, bias, bias, bias, dsk),
        compiler_params=_params(("parallel", "arbitrary")), name=name,
    )(q, kpad, kpad, kpad, vpad, vpad, vpad, do, bc, bp, far, sinks)


def _small_grads(dbc, dbp, dbf, dsk, oh_cur, oh_prev, *, name):
    Hq = dbc.shape[0]

    def body(dbc_ref, dbp_ref, dbf_ref, dsk_ref, oc_ref, op_ref, tab_ref, sink_ref):
        tab = (jnp.dot(dbc_ref[...], oc_ref[...], precision=HIGHEST, preferred_element_type=F32)
               + jnp.dot(dbp_ref[...], op_ref[...], precision=HIGHEST, preferred_element_type=F32))
        far = jnp.sum(dbf_ref[...], axis=1, keepdims=True)
        last = lax.broadcasted_iota(jnp.int32, (Hq, N_BUCKETS), 1) == N_BUCKETS - 1
        tab_ref[...] = tab + jnp.where(last, far, 0.0)
        sink_ref[...] = jnp.sum(dsk_ref[...], axis=1, keepdims=True)

    vm = pl.BlockSpec(memory_space=pltpu.VMEM)
    return pl.pallas_call(
        body, out_shape=(jax.ShapeDtypeStruct((Hq, N_BUCKETS), F32), jax.ShapeDtypeStruct((Hq, 1), F32)),
        in_specs=[vm] * 6, out_specs=(vm, vm), compiler_params=_params(), name=name,
    )(dbc.reshape(Hq, -1), dbp.reshape(Hq, -1), dbf.reshape(Hq, -1), dsk.reshape(Hq, -1), oh_cur, oh_prev)


def _coords():
    return lax.axis_index("x"), lax.axis_index("y"), lax.axis_index("c")


def _all_gather(shards, *, name):
    nt = len(shards)

    def body(*refs):
        ins, outs = refs[:nt], refs[nt:2 * nt]
        send_sems, recv_sems, local_sems = refs[2 * nt:]
        x, y, c = _coords()
        me, sibling = (x, y, c), (x, y, 1 - c)
        chips = [(1 - x, y), (x, 1 - y), (1 - x, 1 - y)]

        def slot(t, dev):
            return outs[t].at[4 * dev[0] + 2 * dev[1] + dev[2]]

        def copy(t, k, block, to, src=None):
            dst = slot(t, block)
            return pltpu.make_async_remote_copy(
                src_ref=dst if src is None else src, dst_ref=dst,
                send_sem=send_sems.at[t, k], recv_sem=recv_sems.at[t, k], device_id=to, device_id_type=MESH)

        mine = [pltpu.make_async_copy(ins[t], slot(t, me), local_sems.at[t]) for t in range(nt)]
        for cp in mine:
            cp.start()
        first = []
        for t in range(nt):
            first.append(copy(t, 0, me, sibling, src=ins[t]))
            first += [copy(t, 1 + j, me, (*chip, c), src=ins[t]) for j, chip in enumerate(chips)]
        for cp in first:
            cp.start()
        passed = []
        for j, chip in enumerate(chips):
            for t in range(nt):
                copy(t, 1 + j, (*chip, c), me).wait_recv()
                cp = copy(t, 4 + j, (*chip, c), sibling)
                cp.start()
                passed.append(cp)
        for t in range(nt):
            copy(t, 0, sibling, me).wait_recv()
            for j, chip in enumerate(chips):
                copy(t, 4 + j, (*chip, 1 - c), me).wait_recv()
        for cp in first + passed:
            cp.wait_send()
        for cp in mine:
            cp.wait()

    hbm = pl.BlockSpec(memory_space=pl.ANY)
    return pl.pallas_call(
        body,
        out_shape=tuple(jax.ShapeDtypeStruct((N_DEV,) + s.shape, s.dtype) for s in shards),
        in_specs=[hbm] * nt, out_specs=tuple([hbm] * nt),
        scratch_shapes=[pltpu.SemaphoreType.DMA((nt, 7)), pltpu.SemaphoreType.DMA((nt, 7)),
                        pltpu.SemaphoreType.DMA((nt,))],
        compiler_params=_params(), name=name)(*shards)


def _exchange_cores(g, *, name):
    _, R, W = g.shape

    def body(g_ref, o_ref, send_sems, recv_sems, local_sems):
        x, y, c = _coords()
        local, remote = [], []
        for j in range(4):
            local.append(pltpu.make_async_copy(g_ref.at[2 * j + c], o_ref.at[c, j], local_sems.at[j]))
            remote.append(pltpu.make_async_remote_copy(
                src_ref=g_ref.at[2 * j + (1 - c)], dst_ref=o_ref.at[c, j],
                send_sem=send_sems.at[j], recv_sem=recv_sems.at[j],
                device_id=(x, y, 1 - c), device_id_type=MESH))
        for cp in local + remote:
            cp.start()
        for j in range(4):
            pltpu.make_async_remote_copy(
                src_ref=g_ref.at[2 * j + c], dst_ref=o_ref.at[1 - c, j],
                send_sem=send_sems.at[j], recv_sem=recv_sems.at[j],
                device_id=(x, y, 1 - c), device_id_type=MESH).wait_recv()
        for cp in remote:
            cp.wait_send()
        for cp in local:
            cp.wait()

    hbm = pl.BlockSpec(memory_space=pl.ANY)
    return pl.pallas_call(
        body, out_shape=jax.ShapeDtypeStruct((2, 4, R, W), g.dtype), in_specs=[hbm], out_specs=hbm,
        scratch_shapes=[pltpu.SemaphoreType.DMA((4,)), pltpu.SemaphoreType.DMA((4,)), pltpu.SemaphoreType.DMA((4,))],
        compiler_params=_params(), name=name)(g)


def _exchange_chips(p, *, name):
    _, R, W = p.shape

    def body(p_ref, o_ref, send_sems, recv_sems, local_sem):
        x, y, c = _coords()
        mine = 2 * x + y
        local = pltpu.make_async_copy(p_ref.at[mine], o_ref.at[mine], local_sem)
        local.start()
        peers = [(1 - x, y), (x, 1 - y), (1 - x, 1 - y)]
        sends = []
        for k, (px, py) in enumerate(peers):
            sends.append(pltpu.make_async_remote_copy(
                src_ref=p_ref.at[2 * px + py], dst_ref=o_ref.at[mine],
                send_sem=send_sems.at[k], recv_sem=recv_sems.at[k],
                device_id=(px, py, c), device_id_type=MESH))
        for cp in sends:
            cp.start()
        for k, (px, py) in enumerate(peers):
            pltpu.make_async_remote_copy(
                src_ref=p_ref.at[mine], dst_ref=o_ref.at[2 * px + py],
                send_sem=send_sems.at[k], recv_sem=recv_sems.at[k],
                device_id=(px, py, c), device_id_type=MESH).wait_recv()
        for cp in sends:
            cp.wait_send()
        local.wait()

    hbm = pl.BlockSpec(memory_space=pl.ANY)
    return pl.pallas_call(
        body, out_shape=jax.ShapeDtypeStruct(p.shape, p.dtype), in_specs=[hbm], out_specs=hbm,
        scratch_shapes=[pltpu.SemaphoreType.DMA((3,)), pltpu.SemaphoreType.DMA((3,)), pltpu.SemaphoreType.DMA],
        compiler_params=_params(), name=name)(p)


def _add_pairs(r, *, name):
    _, _, R, W = r.shape
    tr = _tile(R, 512, 8)

    def body(a_ref, b_ref, o_ref):
        o_ref[...] = a_ref[...] + b_ref[...]

    return pl.pallas_call(
        body, out_shape=jax.ShapeDtypeStruct((4, R, W), r.dtype), grid=(4, R // tr),
        in_specs=[pl.BlockSpec((None, None, tr, W), lambda j, i: (0, j, i, 0)),
                  pl.BlockSpec((None, None, tr, W), lambda j, i: (1, j, i, 0))],
        out_specs=pl.BlockSpec((None, tr, W), lambda j, i: (j, i, 0)),
        compiler_params=_params(("parallel", "parallel")), name=name)(r, r)


def _adamw_math(w, g, m, v):
    m = ADAM_B1 * m + (1.0 - ADAM_B1) * g
    v = ADAM_B2 * v + (1.0 - ADAM_B2) * (g * g)
    m_hat = m / (1.0 - ADAM_B1 ** ADAM_STEP)
    v_hat = v / (1.0 - ADAM_B2 ** ADAM_STEP)
    delta = -ADAM_LR * (m_hat / (jnp.sqrt(v_hat) + ADAM_EPS) + ADAM_WD * w)
    return delta, m, v


def _sum_adamw(parts, w, m, v, *, name):
    _, R, W = parts.shape
    tr = _tile(R, 256, 8)

    def body(p0, p1, p2, p3, w_ref, m_ref, v_ref, g_out, d_out, m_out, v_out):
        g = ((p0[...] + p1[...]) + p2[...]) + p3[...]
        delta, m_new, v_new = _adamw_math(w_ref[...], g, m_ref[...], v_ref[...])
        g_out[...] = g
        d_out[...] = delta
        m_out[...] = m_new
        v_out[...] = v_new

    row = pl.BlockSpec((tr, W), lambda i: (i, 0))
    pspecs = [pl.BlockSpec((None, tr, W), (lambda i, s=s: (s, i, 0))) for s in range(4)]
    shp = jax.ShapeDtypeStruct((R, W), F32)
    return pl.pallas_call(
        body, out_shape=(shp, shp, shp, shp), grid=(R // tr,),
        in_specs=pspecs + [row, row, row], out_specs=(row, row, row, row),
        compiler_params=_params(("parallel",)), name=name)(parts, parts, parts, parts, w, m, v)


def _small_allreduce_adamw(s, w, m, v, *, name):
    R, W = s.shape

    def body(s_ref, w_ref, m_ref, v_ref, g_out, d_out, m_out, v_out, gath, send_sems, recv_sems):
        x, y, c = _coords()
        mine = 4 * x + 2 * y + c
        gath[mine] = s_ref[...]
        peers = [((1 - x) if k & 4 else x, (1 - y) if k & 2 else y, (1 - c) if k & 1 else c) for k in range(1, N_DEV)]
        sends = []
        for k in range(1, N_DEV):
            peer = peers[k - 1]
            sends.append(pltpu.make_async_remote_copy(
                src_ref=s_ref, dst_ref=gath.at[mine], send_sem=send_sems.at[k - 1], recv_sem=recv_sems.at[k - 1],
                device_id=peer, device_id_type=MESH))
        for cp in sends:
            cp.start()
        for k in range(1, N_DEV):
            peer = peers[k - 1]
            pltpu.make_async_remote_copy(
                src_ref=s_ref, dst_ref=gath.at[4 * peer[0] + 2 * peer[1] + peer[2]],
                send_sem=send_sems.at[k - 1], recv_sem=recv_sems.at[k - 1],
                device_id=peer, device_id_type=MESH).wait_recv()
        for cp in sends:
            cp.wait_send()
        g = gath[0]
        for d in range(1, N_DEV):
            g = g + gath[d]
        delta, m_new, v_new = _adamw_math(w_ref[...], g, m_ref[...], v_ref[...])
        g_out[...] = g
        d_out[...] = delta
        m_out[...] = m_new
        v_out[...] = v_new

    vm = pl.BlockSpec(memory_space=pltpu.VMEM)
    shp = jax.ShapeDtypeStruct((R, W), F32)
    return pl.pallas_call(
        body, out_shape=(shp, shp, shp, shp), in_specs=[vm] * 4, out_specs=(vm, vm, vm, vm),
        scratch_shapes=[pltpu.VMEM((N_DEV, R, W), F32), pltpu.SemaphoreType.DMA((N_DEV - 1,)),
                        pltpu.SemaphoreType.DMA((N_DEV - 1,))],
        compiler_params=_params(), name=name)(s, w, m, v)


def _pack_local(meta, w_in, w_out, w_gu, w_down):
    parts = [w_in.reshape(-1, D_MODEL), w_out.reshape(-1, D_MODEL), w_gu.reshape(-1, D_MODEL),
             w_down.reshape(-1, D_MODEL), meta.reshape(-1, D_MODEL)]
    rows = sum(p.shape[0] for p in parts)
    return jnp.concatenate(parts + [jnp.zeros((PACK_ROWS - rows, D_MODEL), F32)], axis=0)


def _unpack_local(p, shapes):
    out, r = [], 0
    for shp in shapes:
        n = int(np.prod(shp)) // D_MODEL
        out.append(p[r:r + n].reshape(shp))
        r += n
    return out


def _pack_full_grads(d_meta, d_w_in, d_w_out, d_w_gu, d_w_down):
    def cols(a):
        R, C = a.shape
        return a.reshape(R, N_DEV, C // N_DEV).transpose(1, 0, 2).reshape(N_DEV, -1, D_MODEL)

    parts = [cols(d_w_in), d_w_out.reshape(N_DEV, -1, D_MODEL), cols(d_w_gu),
             d_w_down.reshape(N_DEV, -1, D_MODEL), cols(d_meta)]
    rows = sum(p.shape[1] for p in parts)
    return jnp.concatenate(parts + [jnp.zeros((N_DEV, PACK_ROWS - rows, D_MODEL), F32)], axis=1)


def _pack_small(rel_bias, g1, g2, g3, g4, b_forget, sinks, extra=None):
    misc = jnp.concatenate([rel_bias.reshape(-1), b_forget.reshape(-1), sinks.reshape(-1)])
    misc = jnp.concatenate([misc, jnp.zeros((D_MODEL - misc.shape[0],), F32)])[None]
    last = jnp.zeros((1, D_MODEL), F32) if extra is None else extra
    return jnp.concatenate([g1, g2, g3, g4, misc, last, jnp.zeros((SMALL_ROWS - 6, D_MODEL), F32)], axis=0)


def _unpack_small(p):
    nrb = N_BUCKETS * SWA_Q_HEADS
    misc = p[4]
    return dict(rel_bias=misc[:nrb].reshape(N_BUCKETS, SWA_Q_HEADS), ln_pre_mix=p[0:1], ln_post_mix=p[1:2],
                ln_pre_ffn=p[2:3], ln_post_ffn=p[3:4], b_forget=misc[nrb:nrb + 8].reshape(1, 8),
                sinks=misc[nrb + 8:nrb + 16].reshape(1, 8))


def _heads(a, n):
    return a.reshape(a.shape[0], n, HEAD_DIM).transpose(1, 0, 2)


def _unheads(a):
    return a.transpose(1, 0, 2).reshape(a.shape[1], -1)


def kernel(x, meta_tokens, rel_bias, ln_pre_mix, ln_post_mix, ln_pre_ffn, ln_post_ffn, w_in, b_forget, sinks, w_out, w_gate_up, w_down, loss_target, m_meta_tokens, m_rel_bias, m_ln_pre_mix, m_ln_post_mix, m_ln_pre_ffn, m_ln_post_ffn, m_w_in, m_b_forget, m_sinks, m_w_out, m_w_gate_up, m_w_down, v_meta_tokens, v_rel_bias, v_ln_pre_mix, v_ln_post_mix, v_ln_pre_ffn, v_ln_post_ffn, v_w_in, v_b_forget, v_sinks, v_w_out, v_w_gate_up, v_w_down):
    seq = x.shape[1]
    T = BLOCK + seq
    assert T % FOX_TILE == 0
    nq = T // FOX_TILE
    tm = _tile(T, 1056)
    cin = w_in.shape[2]
    cgu = w_gate_up.shape[2]

    w_in_s = jnp.pad(w_in[0].astype(BF16), ((0, 0), (0, 384 - cin)))
    w_gu_s = jnp.pad(w_gate_up[0].astype(BF16), ((0, 0), (0, 768 - cgu)))
    g_in, g_out, g_gu, g_down, g_meta = _all_gather(
        [w_in_s, w_out[0].astype(BF16), w_gu_s, w_down[0].astype(BF16), meta_tokens], name="ag_weights")
    w_in_full = g_in[:, :, :cin].transpose(1, 0, 2).reshape(D_MODEL, N_DEV * cin)
    w_qkv = w_in_full[:, :D_QKV]
    w_f = jnp.pad(w_in_full[:, D_QKV:], ((0, 0), (0, BLOCK - FOX_HEADS)))
    w_in_cat = jnp.concatenate([w_qkv, w_f, jnp.zeros((D_MODEL, D_PROJ_PAD - D_QKV - BLOCK), BF16)], axis=1)
    w_out_full = g_out.reshape(D_MODEL, D_MODEL)
    w_gu_full = g_gu[:, :, :cgu].transpose(1, 0, 2).reshape(D_MODEL, 2 * D_FF)
    w_down_full = g_down.reshape(D_FF, D_MODEL)
    meta_full = g_meta.transpose(1, 0, 2).reshape(N_META, D_MODEL)

    h0 = jnp.concatenate([jnp.zeros((PAD_ROWS, D_MODEL), F32), meta_full, x[0]], axis=0)
    target = jnp.concatenate([jnp.zeros((BLOCK, D_MODEL), F32), loss_target[0]], axis=0)
    hn1 = _rms_fwd(h0, ln_pre_mix, name="rms_pre_mix")
    proj = _matmul(hn1, w_qkv, out_dtype=BF16, tm=tm, tn=768, name="mm_in_proj")
    proj_f = _matmul(hn1, w_f, out_dtype=F32, tm=tm, tn=BLOCK, name="mm_in_proj_f")

    q_a = _heads(proj[:, 0:512], 8)
    k_a = jnp.pad(_heads(proj[:, 512:640], 2), ((0, 0), (BLOCK, 0), (0, 0)))
    v_a = jnp.pad(_heads(proj[:, 640:768], 2), ((0, 0), (BLOCK, 0), (0, 0)))
    q_b = _heads(proj[:, 768:1280], 8)
    k_b = _heads(proj[:, 1280:1792], 8)
    v_b = _heads(proj[:, 1792:2304], 8)
    f_t = proj_f[:, :FOX_HEADS].T
    bf_col = b_forget.reshape(FOX_HEADS, 1)

    oh_cur, oh_prev = _bucket_onehots()
    bias_c = jnp.einsum("pb,bh->hp", jnp.asarray(oh_cur), rel_bias, precision=HIGHEST).reshape(8, BLOCK, BLOCK)
    bias_p = jnp.einsum("pb,bh->hp", jnp.asarray(oh_prev), rel_bias, precision=HIGHEST).reshape(8, BLOCK, BLOCK)
    far = rel_bias[N_BUCKETS - 1]
    sink_v = sinks[0]
    o_a = _swa_fwd(q_a, k_a, v_a, bias_c, bias_p, far, sink_v, name="swa_fwd")

    cum = _fox_gates_fwd(f_t, bf_col, name="fox_gates_fwd")
    valid = (jnp.arange(T) >= PAD_ROWS)[None, :]
    cq_col = cum[:, :, None]
    cq_row = cum.reshape(FOX_HEADS, nq, 1, FOX_TILE)
    ck = jnp.where(valid, cum, -NEG)
    ck_row = ck.reshape(FOX_HEADS, nq, 1, FOX_TILE)
    ck_col = ck[:, :, None]
    o_b, lse = _fox_fwd(q_b, k_b, v_b, cq_col, ck_row, name="fox_fwd")

    mix = jnp.concatenate([_unheads(o_a), _unheads(o_b)], axis=1)
    a1 = _matmul(mix, w_out_full, out_dtype=F32, tm=tm, tn=512, name="mm_out_proj")
    h1 = _post_res(a1, ln_post_mix, h0, name="post_mix")
    hn2 = _rms_fwd(h1, ln_pre_ffn, name="rms_pre_ffn")
    gu = _matmul(hn2, w_gu_full, out_dtype=BF16, tm=tm, tn=512, name="mm_gate_up")
    act = _swiglu_fwd(gu, name="swiglu_fwd")
    ff = _matmul(act, w_down_full, out_dtype=F32, tm=tm, tn=512, name="mm_down")
    dh2, loss_acc = _loss_head(ff, ln_post_ffn, h1, target, name="loss_head")

    dff, dg_post_ffn = _rms_bwd(ff, ln_post_ffn, dh2, None, out_dtype=BF16, name="rms_bwd_post_ffn")
    dact = _matmul(dff, w_down_full, nt=True, out_dtype=BF16, tm=tm, tn=1408, name="mm_d_act")
    d_w_down = _matmul(act.T, dff, out_dtype=F32, tm=704, tn=512, name="mm_dw_down")
    dgu = _swiglu_bwd(gu, dact, name="swiglu_bwd")
    dhn2 = _matmul(dgu, w_gu_full, nt=True, out_dtype=F32, tm=tm, tn=512, tk=2816, name="mm_d_hn2")
    d_w_gu = _matmul(hn2.T, dgu, out_dtype=F32, tm=512, tn=512, name="mm_dw_gate_up")
    dh1, dg_pre_ffn = _rms_bwd(h1, ln_pre_ffn, dhn2, dh2, out_dtype=F32, name="rms_bwd_pre_ffn")
    da1, dg_post_mix = _rms_bwd(a1, ln_post_mix, dh1, None, out_dtype=BF16, name="rms_bwd_post_mix")
    dmix = _matmul(da1, w_out_full, nt=True, out_dtype=BF16, tm=tm, tn=512, name="mm_d_mix")
    d_w_out = _matmul(mix.T, da1, out_dtype=F32, tm=512, tn=512, name="mm_dw_out")

    do_a = _heads(dmix[:, :512], 8)
    do_b = _heads(dmix[:, 512:], 8)
    dq_a, dk_a, dv_a, dbc, dbp, dbf, dsk = _swa_bwd(q_a, k_a, v_a, do_a, bias_c, bias_p, far, sink_v, name="swa_bwd")
    d_tab, d_sink = _small_grads(dbc, dbp, dbf, dsk, jnp.asarray(oh_cur), jnp.asarray(oh_prev), name="small_grads")

    delta_row = _row_dot(o_b, do_b, name="fox_delta")
    lse_row = lse.reshape(FOX_HEADS, nq, 1, FOX_TILE)
    dq_b, dk_b, dv_b, dck, dcq = _fox_bwd(q_b, k_b, v_b, do_b, lse_row, delta_row, cq_row, ck_col, name="fox_bwd")
    df_t, d_bf = _fox_gates_bwd(dcq.reshape(FOX_HEADS, T), dck.reshape(FOX_HEADS, T), f_t, bf_col, name="fox_gates_bwd")

    dproj = jnp.concatenate([
        _unheads(dq_a), _unheads(dk_a[:, BLOCK:]).astype(BF16), _unheads(dv_a[:, BLOCK:]).astype(BF16),
        _unheads(dq_b).astype(BF16), _unheads(dk_b), _unheads(dv_b),
        df_t.T.astype(BF16), jnp.zeros((T, D_PROJ_PAD - D_PROJ), BF16)], axis=1)
    dhn1 = _matmul(dproj, w_in_cat, nt=True, out_dtype=F32, tm=tm, tn=512, name="mm_d_hn1")
    d_w_in = _matmul(hn1.T, dproj, out_dtype=F32, tm=512, tn=512, name="mm_dw_in")
    dh0, dg_pre_mix = _rms_bwd(h0, ln_pre_mix, dhn1, dh1, out_dtype=F32, name="rms_bwd_pre_mix")
    grad_x = dh0[BLOCK:][None]
    d_meta = dh0[PAD_ROWS:BLOCK]

    packed = _pack_full_grads(d_meta, d_w_in[:, :N_DEV * cin], d_w_out, d_w_gu, d_w_down)
    by_core = _exchange_cores(packed, name="rs_cores")
    chip_sum = _add_pairs(by_core, name="rs_add_cores")
    by_chip = _exchange_chips(chip_sum, name="rs_chips")
    w_p = _pack_local(meta_tokens, w_in[0], w_out[0], w_gate_up[0], w_down[0])
    m_p = _pack_local(m_meta_tokens, m_w_in[0], m_w_out[0], m_w_gate_up[0], m_w_down[0])
    v_p = _pack_local(v_meta_tokens, v_w_in[0], v_w_out[0], v_w_gate_up[0], v_w_down[0])
    big = _sum_adamw(by_chip, w_p, m_p, v_p, name="rs_sum_adamw")
    shapes = [w_in.shape, w_out.shape, w_gate_up.shape, w_down.shape, meta_tokens.shape]
    big = [dict(zip(["w_in", "w_out", "w_gate_up", "w_down", "meta_tokens"], _unpack_local(p, shapes))) for p in big]

    loss_row = jnp.pad(loss_acc[0:1, 0:1] * (0.5 / D_MODEL), ((0, 0), (0, D_MODEL - 1)))
    s_small = _pack_small(d_tab.T, dg_pre_mix, dg_post_mix, dg_pre_ffn, dg_post_ffn, d_bf, d_sink, extra=loss_row)
    w_s = _pack_small(rel_bias, ln_pre_mix, ln_post_mix, ln_pre_ffn, ln_post_ffn, b_forget, sinks)
    m_s = _pack_small(m_rel_bias, m_ln_pre_mix, m_ln_post_mix, m_ln_pre_ffn, m_ln_post_ffn, m_b_forget, m_sinks)
    v_s = _pack_small(v_rel_bias, v_ln_pre_mix, v_ln_post_mix, v_ln_pre_ffn, v_ln_post_ffn, v_b_forget, v_sinks)
    small = _small_allreduce_adamw(s_small, w_s, m_s, v_s, name="small_allreduce_adamw")
    loss = small[0][5, 0]
    small = [_unpack_small(p) for p in small]

    names = ["meta_tokens", "rel_bias", "ln_pre_mix", "ln_post_mix", "ln_pre_ffn", "ln_post_ffn", "w_in",
             "b_forget", "sinks", "w_out", "w_gate_up", "w_down"]
    outs = [loss, grad_x]
    for kind in range(4):
        for nme in names:
            outs.append(big[kind][nme] if nme in big[kind] else small[kind][nme])
    return tuple(outs)
```

```python
import math

import numpy as np
import jax
import jax.numpy as jnp
from jax import lax
from jax.experimental import pallas as pl
from jax.experimental.pallas import tpu as pltpu

F32 = jnp.float32
BF16 = jnp.bfloat16
HIGHEST = lax.Precision.HIGHEST
MESH = pl.DeviceIdType.MESH

N_DEV = 8
D_MODEL = 1024
N_META = 16
HEAD_DIM = 64
SWA_Q_HEADS = 8
SWA_KV_HEADS = 2
SWA_GROUP = 4
FOX_HEADS = 8
BLOCK = 128
PAD_ROWS = BLOCK - N_META
N_BUCKETS = 32
MAX_DISTANCE = 128
D_FF = 2816
D_QKV = 2304
D_PROJ = D_QKV + FOX_HEADS
D_PROJ_PAD = 2560
EPS = 1e-6
NEG = -1e30
SCALE = HEAD_DIM ** -0.5
ADAM_LR, ADAM_B1, ADAM_B2, ADAM_EPS, ADAM_WD, ADAM_STEP = 0.001, 0.9, 0.999, 1e-08, 0.01, 10
VMEM_LIMIT = 48 * 1024 * 1024
FOX_TILE = 384
W_IN_PAD = 384
HID_PAD = 384

NT = (((1,), (1,)), ((), ()))
NN = (((1,), (0,)), ((), ()))
TN = (((0,), (0,)), ((), ()))


def _params(sem=None, **kw):
    if sem is not None:
        kw["dimension_semantics"] = sem
    return pltpu.CompilerParams(vmem_limit_bytes=VMEM_LIMIT, **kw)


def _tile(n, target, mult=16):
    best = None
    for t in range(mult, min(n, target) + 1, mult):
        if n % t == 0:
            best = t
    assert best is not None, (n, target)
    return best


def _matmul(a, b, *, nt=False, b_shards=False, out_shards=False, out_dtype, tm, tn=None, tk=None, name):
    M, K = a.shape
    if b_shards and nt:
        N, tk = b.shape[1], b.shape[2]
    elif b_shards:
        N, tn = b.shape[0] * b.shape[2], b.shape[2]
    else:
        N = b.shape[0] if nt else b.shape[1]
    tk = K if tk is None else tk
    assert M % tm == 0 and N % tn == 0 and K % tk == 0, (name, a.shape, b.shape, tm, tn, tk)
    nk = K // tk
    dn = NT if nt else NN

    def body(a_ref, b_ref, o_ref, *scr):
        part = lax.dot_general(a_ref[...], b_ref[...], dn, preferred_element_type=F32)
        if nk == 1:
            o_ref[...] = part.astype(o_ref.dtype)
        else:
            acc = scr[0]
            k = pl.program_id(2)

            @pl.when(k == 0)
            def _():
                acc[...] = part

            @pl.when(k > 0)
            def _():
                acc[...] += part

            @pl.when(k == nk - 1)
            def _():
                o_ref[...] = acc[...].astype(o_ref.dtype)

    if b_shards and nt:
        b_spec = pl.BlockSpec((None, tn, tk), lambda i, j, k: (k, j, 0))
    elif b_shards:
        b_spec = pl.BlockSpec((None, tk, tn), lambda i, j, k: (j, k, 0))
    elif nt:
        b_spec = pl.BlockSpec((tn, tk), lambda i, j, k: (j, k))
    else:
        b_spec = pl.BlockSpec((tk, tn), lambda i, j, k: (k, j))
    if out_shards:
        out_shape = jax.ShapeDtypeStruct((N // tn, M, tn), out_dtype)
        out_spec = pl.BlockSpec((None, tm, tn), lambda i, j, k: (j, i, 0))
    else:
        out_shape = jax.ShapeDtypeStruct((M, N), out_dtype)
        out_spec = pl.BlockSpec((tm, tn), lambda i, j, k: (i, j))
    return pl.pallas_call(
        body,
        out_shape=out_shape,
        grid=(M // tm, N // tn, nk),
        in_specs=[pl.BlockSpec((tm, tk), lambda i, j, k: (i, k)), b_spec],
        out_specs=out_spec,
        scratch_shapes=[pltpu.VMEM((tm, tn), F32)] if nk > 1 else [],
        compiler_params=_params(("parallel", "parallel", "arbitrary")),
        name=name,
    )(a, b)


def _rstd(x):
    return lax.rsqrt(jnp.mean(x * x, axis=-1, keepdims=True) + EPS)


def _rms_fwd(x, g, *, name):
    T, D = x.shape
    tm = _tile(T, 512)

    def body(x_ref, g_ref, o_ref):
        x = x_ref[...]
        o_ref[...] = (x * _rstd(x) * g_ref[...]).astype(o_ref.dtype)

    return pl.pallas_call(
        body, out_shape=jax.ShapeDtypeStruct((T, D), BF16), grid=(T // tm,),
        in_specs=[pl.BlockSpec((tm, D), lambda i: (i, 0)), pl.BlockSpec((1, D), lambda i: (0, 0))],
        out_specs=pl.BlockSpec((tm, D), lambda i: (i, 0)),
        compiler_params=_params(("parallel",)), name=name)(x, g)


def _post_res(a, g, h, *, name):
    T, D = a.shape
    tm = _tile(T, 512)

    def body(a_ref, g_ref, h_ref, o_ref):
        a = a_ref[...]
        o_ref[...] = h_ref[...] + a * _rstd(a) * g_ref[...]

    row = pl.BlockSpec((tm, D), lambda i: (i, 0))
    return pl.pallas_call(
        body, out_shape=jax.ShapeDtypeStruct((T, D), F32), grid=(T // tm,),
        in_specs=[row, pl.BlockSpec((1, D), lambda i: (0, 0)), row], out_specs=row,
        compiler_params=_params(("parallel",)), name=name)(a, g, h)


def _loss_head(a, g, h, target, *, name):
    T, D = a.shape
    tm = _tile(T, 512)

    def body(a_ref, g_ref, h_ref, t_ref, dy_ref, loss_ref):
        i = pl.program_id(0)
        a = a_ref[...]
        y = h_ref[...] + a * _rstd(a) * g_ref[...]
        rows = i * tm + lax.broadcasted_iota(jnp.int32, (tm, 1), 0)
        err = jnp.where(rows >= BLOCK, y - t_ref[...], 0.0)
        dy_ref[...] = err / D
        part = jnp.sum(jnp.sum(err * err, axis=1, keepdims=True), axis=0, keepdims=True)

        @pl.when(i == 0)
        def _():
            loss_ref[...] = jnp.zeros_like(loss_ref)

        loss_ref[...] += jnp.broadcast_to(part, loss_ref.shape)

    row = pl.BlockSpec((tm, D), lambda i: (i, 0))
    return pl.pallas_call(
        body, out_shape=(jax.ShapeDtypeStruct((T, D), F32), jax.ShapeDtypeStruct((8, 128), F32)),
        grid=(T // tm,),
        in_specs=[row, pl.BlockSpec((1, D), lambda i: (0, 0)), row, row],
        out_specs=(row, pl.BlockSpec((8, 128), lambda i: (0, 0))),
        compiler_params=_params(("arbitrary",)), name=name)(a, g, h, target)


def _rms_bwd(x, g, dy, res, *, out_dtype, name):
    T, D = x.shape
    tm = _tile(T, 512)
    has_res = res is not None

    def body(*refs):
        if has_res:
            x_ref, g_ref, dy_ref, r_ref, dx_ref, dg_ref = refs
        else:
            x_ref, g_ref, dy_ref, dx_ref, dg_ref = refs
        i = pl.program_id(0)
        x = x_ref[...]
        dy = dy_ref[...].astype(F32)
        r = _rstd(x)
        xh = x * r
        dxh = dy * g_ref[...]
        dx = r * (dxh - xh * jnp.mean(dxh * xh, axis=-1, keepdims=True))
        if has_res:
            dx = dx + r_ref[...]
        dx_ref[...] = dx.astype(dx_ref.dtype)

        @pl.when(i == 0)
        def _():
            dg_ref[...] = jnp.zeros_like(dg_ref)

        dg_ref[...] += jnp.sum(dy * xh, axis=0, keepdims=True)

    row = pl.BlockSpec((tm, D), lambda i: (i, 0))
    vec = pl.BlockSpec((1, D), lambda i: (0, 0))
    ins = [x, g, dy] + ([res] if has_res else [])
    return pl.pallas_call(
        body, out_shape=(jax.ShapeDtypeStruct((T, D), out_dtype), jax.ShapeDtypeStruct((1, D), F32)),
        grid=(T // tm,),
        in_specs=[row, vec, row] + ([row] if has_res else []),
        out_specs=(row, vec),
        compiler_params=_params(("arbitrary",)), name=name)(*ins)


def _swiglu_fwd(gu, *, name):
    T, F2 = gu.shape
    F = F2 // 2
    tm = _tile(T, 384)

    def body(g_ref, u_ref, o_ref):
        g = g_ref[...].astype(F32)
        o_ref[...] = (g / (1.0 + jnp.exp(-g)) * u_ref[...].astype(F32)).astype(o_ref.dtype)

    return pl.pallas_call(
        body, out_shape=jax.ShapeDtypeStruct((T, F), BF16), grid=(T // tm,),
        in_specs=[pl.BlockSpec((tm, F), lambda i: (i, 0)), pl.BlockSpec((tm, F), lambda i: (i, 1))],
        out_specs=pl.BlockSpec((tm, F), lambda i: (i, 0)),
        compiler_params=_params(("parallel",)), name=name)(gu, gu)


def _swiglu_bwd(gu, dact, *, name):
    T, F2 = gu.shape
    F = F2 // 2
    tm = _tile(T, 384)

    def body(g_ref, u_ref, d_ref, o_ref):
        g = g_ref[...].astype(F32)
        u = u_ref[...].astype(F32)
        d = d_ref[...].astype(F32)
        sg = 1.0 / (1.0 + jnp.exp(-g))
        o_ref[:, :F] = (d * u * (sg * (1.0 + g * (1.0 - sg)))).astype(o_ref.dtype)
        o_ref[:, F:] = (d * (g * sg)).astype(o_ref.dtype)

    return pl.pallas_call(
        body, out_shape=jax.ShapeDtypeStruct((T, F2), BF16), grid=(T // tm,),
        in_specs=[pl.BlockSpec((tm, F), lambda i: (i, 0)), pl.BlockSpec((tm, F), lambda i: (i, 1)),
                  pl.BlockSpec((tm, F), lambda i: (i, 0))],
        out_specs=pl.BlockSpec((tm, F2), lambda i: (i, 0)),
        compiler_params=_params(("parallel",)), name=name)(gu, gu, dact)


def _fox_gates_fwd(f_t, b, *, name):
    H, T = f_t.shape
    nb = T // BLOCK

    def body(f_ref, b_ref, cum_ref):
        f = f_ref[...] + b_ref[...]
        ls = jnp.minimum(f, 0.0) - jnp.log(1.0 + jnp.exp(-jnp.abs(f)))
        t = lax.broadcasted_iota(jnp.int32, (H, T), 1)
        ls = jnp.where(t >= PAD_ROWS, ls, 0.0)
        upper = (lax.broadcasted_iota(jnp.int32, (BLOCK, BLOCK), 0)
                 <= lax.broadcasted_iota(jnp.int32, (BLOCK, BLOCK), 1)).astype(F32)
        carry = jnp.zeros((H, 1), F32)
        for blk in range(nb):
            seg = ls[:, blk * BLOCK:(blk + 1) * BLOCK]
            pre = jnp.dot(seg, upper, precision=HIGHEST, preferred_element_type=F32) + carry
            cum_ref[:, blk * BLOCK:(blk + 1) * BLOCK] = pre
            carry = pre[:, BLOCK - 1:BLOCK]

    return pl.pallas_call(
        body, out_shape=jax.ShapeDtypeStruct((H, T), F32),
        in_specs=[pl.BlockSpec(memory_space=pltpu.VMEM), pl.BlockSpec(memory_space=pltpu.VMEM)],
        out_specs=pl.BlockSpec(memory_space=pltpu.VMEM),
        compiler_params=_params(), name=name)(f_t, b)


def _fox_gates_bwd(dcq, dck, f_t, b, *, name):
    H, T = f_t.shape
    nb = T // BLOCK

    def body(dq_ref, d_ref, f_ref, b_ref, df_ref, db_ref):
        lower = (lax.broadcasted_iota(jnp.int32, (BLOCK, BLOCK), 0)
                 >= lax.broadcasted_iota(jnp.int32, (BLOCK, BLOCK), 1)).astype(F32)
        carry = jnp.zeros((H, 1), F32)
        for blk in range(nb - 1, -1, -1):
            seg = dq_ref[:, blk * BLOCK:(blk + 1) * BLOCK] - d_ref[:, blk * BLOCK:(blk + 1) * BLOCK]
            suf = jnp.dot(seg, lower, precision=HIGHEST, preferred_element_type=F32) + carry
            df_ref[:, blk * BLOCK:(blk + 1) * BLOCK] = suf
            carry = suf[:, 0:1]
        f = f_ref[...] + b_ref[...]
        t = lax.broadcasted_iota(jnp.int32, (H, T), 1)
        df = jnp.where(t >= PAD_ROWS, df_ref[...] / (1.0 + jnp.exp(f)), 0.0)
        df_ref[...] = df
        db_ref[...] = jnp.sum(df, axis=1, keepdims=True)

    vm = pl.BlockSpec(memory_space=pltpu.VMEM)
    return pl.pallas_call(
        body, out_shape=(jax.ShapeDtypeStruct((H, T), F32), jax.ShapeDtypeStruct((H, 1), F32)),
        in_specs=[vm, vm, vm, vm], out_specs=(vm, vm),
        compiler_params=_params(), name=name)(dcq, dck, f_t, b)


def _fox_fwd(q, k, v, cq_col, ck_row, *, name):
    H, T, dh = q.shape
    tq = FOX_TILE
    nq = T // tq

    def body(q_ref, k_ref, v_ref, cq_ref, ck_ref, o_ref, lse_ref, m_scr, l_scr, acc_scr):
        i = pl.program_id(1)
        qs = q_ref[...] * SCALE
        cqv = cq_ref[...]
        m_scr[...] = jnp.full(m_scr.shape, NEG, F32)
        l_scr[...] = jnp.zeros(l_scr.shape, F32)
        acc_scr[...] = jnp.zeros(acc_scr.shape, F32)

        def step(kb, diag):
            off = pl.multiple_of(kb * tq, tq)
            kblk = k_ref[pl.ds(off, tq), :]
            vblk = v_ref[pl.ds(off, tq), :]
            s = lax.dot_general(qs, kblk, NT, preferred_element_type=F32)
            s = s + (cqv - ck_ref[kb])
            if diag:
                r = lax.broadcasted_iota(jnp.int32, (tq, tq), 0)
                c = lax.broadcasted_iota(jnp.int32, (tq, tq), 1)
                s = jnp.where(r >= c, s, NEG)
            m_prev = m_scr[...]
            m_new = jnp.maximum(m_prev, jnp.max(s, axis=1, keepdims=True))
            p = jnp.exp(s - m_new)
            alpha = jnp.exp(m_prev - m_new)
            l_scr[...] = alpha * l_scr[...] + jnp.sum(p, axis=1, keepdims=True)
            acc_scr[...] = alpha * acc_scr[...] + jnp.dot(p.astype(BF16), vblk, preferred_element_type=F32)
            m_scr[...] = m_new

        def loop_body(kb, carry):
            step(kb, False)
            return carry

        lax.fori_loop(0, i, loop_body, 0)
        step(i, True)
        l = l_scr[...]
        o_ref[...] = (acc_scr[...] / l).astype(o_ref.dtype)
        lse_ref[...] = m_scr[...] + jnp.log(l)

    blk = pl.BlockSpec((None, tq, dh), lambda h, i: (h, i, 0))
    full = pl.BlockSpec((None, T, dh), lambda h, i: (h, 0, 0))
    col = pl.BlockSpec((None, tq, 1), lambda h, i: (h, i, 0))
    return pl.pallas_call(
        body,
        out_shape=(jax.ShapeDtypeStruct((H, T, dh), BF16), jax.ShapeDtypeStruct((H, T, 1), F32)),
        grid=(H, nq),
        in_specs=[blk, full, full, col, pl.BlockSpec((None, nq, 1, tq), lambda h, i: (h, 0, 0, 0))],
        out_specs=(blk, col),
        scratch_shapes=[pltpu.VMEM((tq, 1), F32), pltpu.VMEM((tq, 1), F32), pltpu.VMEM((tq, dh), F32)],
        compiler_params=_params(("parallel", "arbitrary")), name=name)(q, k, v, cq_col, ck_row)


def _row_dot(o, do, *, name):
    H, T, dh = o.shape
    tq = FOX_TILE
    nq = T // tq

    def body(o_ref, do_ref, d_ref):
        prod = o_ref[...].astype(F32) * do_ref[...].astype(F32)
        ones = jnp.ones((8, dh), F32)
        d = lax.dot_general(ones, prod, NT, precision=HIGHEST, preferred_element_type=F32)
        d_ref[...] = d[0:1, :]

    blk = pl.BlockSpec((None, tq, dh), lambda h, i: (h, i, 0))
    return pl.pallas_call(
        body, out_shape=jax.ShapeDtypeStruct((H, nq, 1, tq), F32), grid=(H, nq),
        in_specs=[blk, blk], out_specs=pl.BlockSpec((None, None, 1, tq), lambda h, i: (h, i, 0, 0)),
        compiler_params=_params(("parallel", "parallel")), name=name)(o, do)


def _fox_bwd(q, k, v, do, lse_row, delta_row, cq_row, ck_col, *, name):
    H, T, dh = q.shape
    tq = FOX_TILE
    nq = T // tq

    def body(q_ref, k_ref, v_ref, do_ref, lse_ref, dl_ref, cq_ref, ck_ref,
             dq_ref, dk_ref, dv_ref, dck_ref, dcq_ref, dk_acc, dv_acc, dck_acc):
        j = pl.program_id(1)

        @pl.when(j == 0)
        def _():
            dq_ref[...] = jnp.zeros(dq_ref.shape, F32)
            dcq_ref[...] = jnp.zeros(dcq_ref.shape, F32)

        ks = k_ref[...] * SCALE
        vb = v_ref[...]
        ckv = ck_ref[...]
        dk_acc[...] = jnp.zeros(dk_acc.shape, F32)
        dv_acc[...] = jnp.zeros(dv_acc.shape, F32)
        dck_acc[...] = jnp.zeros(dck_acc.shape, F32)

        def step(qb, diag):
            off = pl.multiple_of(qb * tq, tq)
            qblk = q_ref[pl.ds(off, tq), :]
            doblk = do_ref[pl.ds(off, tq), :]
            s_t = lax.dot_general(ks, qblk, NT, preferred_element_type=F32)
            p_t = jnp.exp(s_t + ((cq_ref[qb] - lse_ref[qb]) - ckv))
            if diag:
                r = lax.broadcasted_iota(jnp.int32, (tq, tq), 0)
                c = lax.broadcasted_iota(jnp.int32, (tq, tq), 1)
                p_t = jnp.where(c >= r, p_t, 0.0)
            dv_acc[...] += jnp.dot(p_t.astype(BF16), doblk, preferred_element_type=F32)
            dp_t = lax.dot_general(vb, doblk, NT, preferred_element_type=F32)
            ds_t = p_t * (dp_t - dl_ref[qb])
            dck_acc[...] += jnp.sum(ds_t, axis=1, keepdims=True)
            dcq_ref[qb] += jnp.sum(ds_t, axis=0, keepdims=True)
            dsb = ds_t.astype(BF16)
            dk_acc[...] += jnp.dot(dsb, qblk, preferred_element_type=F32)
            dq_ref[pl.ds(off, tq), :] += lax.dot_general(dsb, ks, TN, preferred_element_type=F32)

        step(j, True)

        def loop_body(qb, carry):
            step(qb, False)
            return carry

        lax.fori_loop(j + 1, nq, loop_body, 0)
        dk_ref[...] = (dk_acc[...] * SCALE).astype(dk_ref.dtype)
        dv_ref[...] = dv_acc[...].astype(dv_ref.dtype)
        dck_ref[...] = dck_acc[...]

    blk = pl.BlockSpec((None, tq, dh), lambda h, j: (h, j, 0))
    full = pl.BlockSpec((None, T, dh), lambda h, j: (h, 0, 0))
    rows = pl.BlockSpec((None, nq, 1, tq), lambda h, j: (h, 0, 0, 0))
    col = pl.BlockSpec((None, tq, 1), lambda h, j: (h, j, 0))
    return pl.pallas_call(
        body,
        out_shape=(jax.ShapeDtypeStruct((H, T, dh), F32), jax.ShapeDtypeStruct((H, T, dh), BF16),
                   jax.ShapeDtypeStruct((H, T, dh), BF16), jax.ShapeDtypeStruct((H, T, 1), F32),
                   jax.ShapeDtypeStruct((H, nq, 1, tq), F32)),
        grid=(H, nq),
        in_specs=[full, blk, blk, full, rows, rows, rows, col],
        out_specs=(full, blk, blk, col, rows),
        scratch_shapes=[pltpu.VMEM((tq, dh), F32), pltpu.VMEM((tq, dh), F32), pltpu.VMEM((tq, 1), F32)],
        compiler_params=_params(("parallel", "arbitrary")), name=name,
    )(q, k, v, do, lse_row, delta_row, cq_row, ck_col)


def _t5_bucket_np(d):
    n = np.maximum(d, 0).astype(np.int32)
    max_exact = N_BUCKETS // 2
    nf = np.maximum(n, 1).astype(np.float32)
    large = max_exact + (np.log(nf / max_exact) / math.log(MAX_DISTANCE / max_exact)
                         * (N_BUCKETS - max_exact)).astype(np.int32)
    large = np.minimum(large, N_BUCKETS - 1)
    return np.where(n < max_exact, n, large)


def _bucket_onehots():
    r = np.arange(BLOCK)[:, None]
    c = np.arange(BLOCK)[None, :]
    eye = np.eye(N_BUCKETS, dtype=np.float32)
    cur = eye[_t5_bucket_np(r - c).reshape(-1)]
    prev = eye[_t5_bucket_np(BLOCK + r - c).reshape(-1)]
    return cur, prev


def _swa_probs(qs, kc, kp, km, bc, bp, far, sink, n):
    r = lax.broadcasted_iota(jnp.int32, (BLOCK, BLOCK), 0)
    c = lax.broadcasted_iota(jnp.int32, (BLOCK, BLOCK), 1)
    never = 2 * BLOCK
    s_c = lax.dot_general(qs, kc, NT, preferred_element_type=F32) + bc
    s_p = lax.dot_general(qs, kp, NT, preferred_element_type=F32) + bp
    s_m = lax.dot_general(qs, km, NT, preferred_element_type=F32) + jnp.where(n == 1, bp, far)
    s_c = jnp.where((c <= r) & (c >= jnp.where(n >= 1, 0, PAD_ROWS)), s_c, NEG)
    s_p = jnp.where(c > r + jnp.where(n >= 2, 0, never), s_p, NEG)
    s_m = jnp.where(c >= jnp.where(n >= 1, PAD_ROWS, never), s_m, NEG)
    m = jnp.maximum(jnp.maximum(jnp.max(s_c, axis=1, keepdims=True), jnp.max(s_p, axis=1, keepdims=True)),
                    jnp.maximum(jnp.max(s_m, axis=1, keepdims=True), sink))
    e_c = jnp.exp(s_c - m)
    e_p = jnp.exp(s_p - m)
    e_m = jnp.exp(s_m - m)
    e_s = jnp.exp(sink - m)
    l = (jnp.sum(e_c, axis=1, keepdims=True) + jnp.sum(e_p, axis=1, keepdims=True)
         + jnp.sum(e_m, axis=1, keepdims=True) + e_s)
    return e_c, e_p, e_m, e_s, l


def _swa_specs(T):
    G = SWA_GROUP
    qblk = pl.BlockSpec((G, BLOCK, HEAD_DIM), lambda kv, n: (kv, n, 0))
    cur = pl.BlockSpec((None, BLOCK, HEAD_DIM), lambda kv, n: (kv, n + 1, 0))
    prev = pl.BlockSpec((None, BLOCK, HEAD_DIM), lambda kv, n: (kv, n, 0))
    meta = pl.BlockSpec((None, BLOCK, HEAD_DIM), lambda kv, n: (kv, 1, 0))
    bias = pl.BlockSpec((G, BLOCK, BLOCK), lambda kv, n: (kv, 0, 0))
    smem = pl.BlockSpec(memory_space=pltpu.SMEM)
    return qblk, cur, prev, meta, bias, smem


def _swa_fwd(q, kpad, vpad, bc, bp, far, sinks, *, name):
    Hq, T, dh = q.shape
    nb = T // BLOCK
    G = SWA_GROUP

    def body(q_ref, kc_ref, kp_ref, km_ref, vc_ref, vp_ref, vm_ref, bc_ref, bp_ref, far_ref, sink_ref, o_ref):
        kv = pl.program_id(0)
        n = pl.program_id(1)
        kc, kp, km = kc_ref[...], kp_ref[...], km_ref[...]
        vc, vp, vm = vc_ref[...], vp_ref[...], vm_ref[...]
        for g in range(G):
            h = kv * G + g
            qs = q_ref[g] * SCALE
            e_c, e_p, e_m, _, l = _swa_probs(qs, kc, kp, km, bc_ref[g], bp_ref[g], far_ref[h], sink_ref[h], n)
            o = (jnp.dot(e_c.astype(BF16), vc, preferred_element_type=F32)
                 + jnp.dot(e_p.astype(BF16), vp, preferred_element_type=F32)
                 + jnp.dot(e_m.astype(BF16), vm, preferred_element_type=F32))
            o_ref[g] = (o / l).astype(o_ref.dtype)

    qblk, cur, prev, meta, bias, smem = _swa_specs(T)
    return pl.pallas_call(
        body, out_shape=jax.ShapeDtypeStruct((Hq, T, dh), BF16), grid=(SWA_KV_HEADS, nb),
        in_specs=[qblk, cur, prev, meta, cur, prev, meta, bias, bias, smem, smem],
        out_specs=qblk,
        compiler_params=_params(("parallel", "parallel")), name=name,
    )(q, kpad, kpad, kpad, vpad, vpad, vpad, bc, bp, far, sinks)


def _swa_bwd(q, kpad, vpad, do, bc, bp, far, sinks, *, name):
    Hq, T, dh = q.shape
    nb = T // BLOCK
    G = SWA_GROUP

    def body(q_ref, kc_ref, kp_ref, km_ref, vc_ref, vp_ref, vm_ref, do_ref, bc_ref, bp_ref, far_ref, sink_ref,
             dq_ref, dk_ref, dv_ref, dbc_ref, dbp_ref, dbf_ref, dsk_ref):
        kv = pl.program_id(0)
        n = pl.program_id(1)

        @pl.when(n == 0)
        def _():
            for ref in (dk_ref, dv_ref, dbc_ref, dbp_ref, dbf_ref, dsk_ref):
                ref[...] = jnp.zeros(ref.shape, F32)

        kc, kp, km = kc_ref[...], kp_ref[...], km_ref[...]
        vc, vp, vm = vc_ref[...], vp_ref[...], vm_ref[...]
        dkc = dkp = dkm = dvc = dvp = dvm = jnp.zeros((BLOCK, dh), F32)
        for g in range(G):
            h = kv * G + g
            qs = q_ref[g] * SCALE
            e_c, e_p, e_m, e_s, l = _swa_probs(qs, kc, kp, km, bc_ref[g], bp_ref[g], far_ref[h], sink_ref[h], n)
            inv = 1.0 / l
            p_c, p_p, p_m = e_c * inv, e_p * inv, e_m * inv
            dob = do_ref[g]
            dp_c = lax.dot_general(dob, vc, NT, preferred_element_type=F32)
            dp_p = lax.dot_general(dob, vp, NT, preferred_element_type=F32)
            dp_m = lax.dot_general(dob, vm, NT, preferred_element_type=F32)
            delta = (jnp.sum(p_c * dp_c, axis=1, keepdims=True) + jnp.sum(p_p * dp_p, axis=1, keepdims=True)
                     + jnp.sum(p_m * dp_m, axis=1, keepdims=True))
            ds_c = p_c * (dp_c - delta)
            ds_p = p_p * (dp_p - delta)
            ds_m = p_m * (dp_m - delta)
            dsk_ref[g] += -(e_s * inv) * delta
            dbc_ref[g] += ds_c
            dbp_ref[g] += ds_p + jnp.where(n == 1, ds_m, 0.0)
            dbf_ref[g] += jnp.where(n >= 2, ds_m, 0.0)
            bc16, bp16, bm16 = ds_c.astype(BF16), ds_p.astype(BF16), ds_m.astype(BF16)
            dq = (jnp.dot(bc16, kc, preferred_element_type=F32) + jnp.dot(bp16, kp, preferred_element_type=F32)
                  + jnp.dot(bm16, km, preferred_element_type=F32))
            dq_ref[g] = (dq * SCALE).astype(dq_ref.dtype)
            dkc += lax.dot_general(bc16, qs, TN, preferred_element_type=F32)
            dkp += lax.dot_general(bp16, qs, TN, preferred_element_type=F32)
            dkm += lax.dot_general(bm16, qs, TN, preferred_element_type=F32)
            dvc += lax.dot_general(p_c.astype(BF16), dob, TN, preferred_element_type=F32)
            dvp += lax.dot_general(p_p.astype(BF16), dob, TN, preferred_element_type=F32)
            dvm += lax.dot_general(p_m.astype(BF16), dob, TN, preferred_element_type=F32)
        cur_off = pl.multiple_of((n + 1) * BLOCK, BLOCK)
        prev_off = pl.multiple_of(n * BLOCK, BLOCK)
        dk_ref[pl.ds(cur_off, BLOCK), :] += dkc
        dk_ref[pl.ds(prev_off, BLOCK), :] += dkp
        dk_ref[BLOCK:2 * BLOCK, :] += dkm
        dv_ref[pl.ds(cur_off, BLOCK), :] += dvc
        dv_ref[pl.ds(prev_off, BLOCK), :] += dvp
        dv_ref[BLOCK:2 * BLOCK, :] += dvm

    qblk, cur, prev, meta, bias, smem = _swa_specs(T)
    kvfull = pl.BlockSpec((None, T + BLOCK, dh), lambda kv, n: (kv, 0, 0))
    dsk = pl.BlockSpec((G, BLOCK, 1), lambda kv, n: (kv, 0, 0))
    return pl.pallas_call(
        body,
        out_shape=(jax.ShapeDtypeStruct((Hq, T, dh), BF16),
                   jax.ShapeDtypeStruct((SWA_KV_HEADS, T + BLOCK, dh), F32),
                   jax.ShapeDtypeStruct((SWA_KV_HEADS, T + BLOCK, dh), F32),
                   jax.ShapeDtypeStruct((Hq, BLOCK, BLOCK), F32), jax.ShapeDtypeStruct((Hq, BLOCK, BLOCK), F32),
                   jax.ShapeDtypeStruct((Hq, BLOCK, BLOCK), F32), jax.ShapeDtypeStruct((Hq, BLOCK, 1), F32)),
        grid=(SWA_KV_HEADS, nb),
        in_specs=[qblk, cur, prev, meta, cur, prev, meta, qblk, bias, bias, smem, smem],
        out_specs=(qblk, kvfull, kvfull, bias, bias, bias, dsk),
        compiler_params=_params(("parallel", "arbitrary")), name=name,
    )(q, kpad, kpad, kpad, vpad, vpad, vpad, do, bc, bp, far, sinks)


def _small_grads(dbc, dbp, dbf, dsk, oh_cur, oh_prev, *, name):
    Hq = dbc.shape[0]

    def body(dbc_ref, dbp_ref, dbf_ref, dsk_ref, oc_ref, op_ref, tab_ref, sink_ref):
        tab = (jnp.dot(dbc_ref[...], oc_ref[...], precision=HIGHEST, preferred_element_type=F32)
               + jnp.dot(dbp_ref[...], op_ref[...], precision=HIGHEST, preferred_element_type=F32))
        far = jnp.sum(dbf_ref[...], axis=1, keepdims=True)
        last = lax.broadcasted_iota(jnp.int32, (Hq, N_BUCKETS), 1) == N_BUCKETS - 1
        tab_ref[...] = tab + jnp.where(last, far, 0.0)
        sink_ref[...] = jnp.sum(dsk_ref[...], axis=1, keepdims=True)

    vm = pl.BlockSpec(memory_space=pltpu.VMEM)
    return pl.pallas_call(
        body, out_shape=(jax.ShapeDtypeStruct((Hq, N_BUCKETS), F32), jax.ShapeDtypeStruct((Hq, 1), F32)),
        in_specs=[vm] * 6, out_specs=(vm, vm), compiler_params=_params(), name=name,
    )(dbc.reshape(Hq, -1), dbp.reshape(Hq, -1), dbf.reshape(Hq, -1), dsk.reshape(Hq, -1), oh_cur, oh_prev)


def _coords():
    return lax.axis_index("x"), lax.axis_index("y"), lax.axis_index("c")


def _all_gather(shards, *, name):
    nt = len(shards)

    def body(*refs):
        ins, outs = refs[:nt], refs[nt:2 * nt]
        send_sems, recv_sems, local_sems = refs[2 * nt:]
        x, y, c = _coords()
        me, sibling = (x, y, c), (x, y, 1 - c)
        chips = [(1 - x, y), (x, 1 - y), (1 - x, 1 - y)]

        def slot(t, dev):
            return outs[t].at[4 * dev[0] + 2 * dev[1] + dev[2]]

        def copy(t, k, block, to, src=None):
            dst = slot(t, block)
            return pltpu.make_async_remote_copy(
                src_ref=dst if src is None else src, dst_ref=dst,
                send_sem=send_sems.at[t, k], recv_sem=recv_sems.at[t, k], device_id=to, device_id_type=MESH)

        mine = [pltpu.make_async_copy(ins[t], slot(t, me), local_sems.at[t]) for t in range(nt)]
        for cp in mine:
            cp.start()
        first = []
        for t in range(nt):
            first.append(copy(t, 0, me, sibling, src=ins[t]))
            first += [copy(t, 1 + j, me, (*chip, c), src=ins[t]) for j, chip in enumerate(chips)]
        for cp in first:
            cp.start()
        passed = []
        for j, chip in enumerate(chips):
            for t in range(nt):
                copy(t, 1 + j, (*chip, c), me).wait_recv()
                cp = copy(t, 4 + j, (*chip, c), sibling)
                cp.start()
                passed.append(cp)
        for t in range(nt):
            copy(t, 0, sibling, me).wait_recv()
            for j, chip in enumerate(chips):
                copy(t, 4 + j, (*chip, 1 - c), me).wait_recv()
        for cp in first + passed:
            cp.wait_send()
        for cp in mine:
            cp.wait()

    hbm = pl.BlockSpec(memory_space=pl.ANY)
    return pl.pallas_call(
        body,
        out_shape=tuple(jax.ShapeDtypeStruct((N_DEV,) + s.shape, s.dtype) for s in shards),
        in_specs=[hbm] * nt, out_specs=tuple([hbm] * nt),
        scratch_shapes=[pltpu.SemaphoreType.DMA((nt, 7)), pltpu.SemaphoreType.DMA((nt, 7)),
                        pltpu.SemaphoreType.DMA((nt,))],
        compiler_params=_params(), name=name)(*shards)


def _exchange_cores(gs, *, name):
    nt = len(gs)

    def body(*refs):
        g_refs, o_refs = refs[:nt], refs[nt:2 * nt]
        send_sems, recv_sems = refs[2 * nt:]
        x, y, c = _coords()
        sibling = (x, y, 1 - c)

        def copy(t, j, core):
            return pltpu.make_async_remote_copy(
                src_ref=g_refs[t].at[2 * j + core], dst_ref=o_refs[t].at[j],
                send_sem=send_sems.at[t, j], recv_sem=recv_sems.at[t, j], device_id=sibling, device_id_type=MESH)

        sends = [copy(t, j, 1 - c) for t in range(nt) for j in range(4)]
        for cp in sends:
            cp.start()
        for cp in sends:
            cp.wait_recv()
        for cp in sends:
            cp.wait_send()

    hbm = pl.BlockSpec(memory_space=pl.ANY)
    return pl.pallas_call(
        body, out_shape=tuple(jax.ShapeDtypeStruct((4,) + g.shape[1:], g.dtype) for g in gs),
        in_specs=[hbm] * nt, out_specs=tuple([hbm] * nt),
        scratch_shapes=[pltpu.SemaphoreType.DMA((nt, 4)), pltpu.SemaphoreType.DMA((nt, 4))],
        compiler_params=_params(), name=name)(*gs)


def _exchange_chips(ps, *, name):
    nt = len(ps)

    def body(*refs):
        p_refs, o_refs = refs[:nt], refs[nt:2 * nt]
        send_sems, recv_sems = refs[2 * nt:]
        x, y, c = _coords()
        peers = [(1 - x, y), (x, 1 - y), (1 - x, 1 - y)]
        sends = []
        for t in range(nt):
            for k, (px, py) in enumerate(peers):
                sends.append(pltpu.make_async_remote_copy(
                    src_ref=p_refs[t].at[2 * px + py], dst_ref=o_refs[t].at[k],
                    send_sem=send_sems.at[t, k], recv_sem=recv_sems.at[t, k],
                    device_id=(px, py, c), device_id_type=MESH))
        for cp in sends:
            cp.start()
        for cp in sends:
            cp.wait_recv()
        for cp in sends:
            cp.wait_send()

    hbm = pl.BlockSpec(memory_space=pl.ANY)
    return pl.pallas_call(
        body, out_shape=tuple(jax.ShapeDtypeStruct((3,) + p.shape[1:], p.dtype) for p in ps),
        in_specs=[hbm] * nt, out_specs=tuple([hbm] * nt),
        scratch_shapes=[pltpu.SemaphoreType.DMA((nt, 3)), pltpu.SemaphoreType.DMA((nt, 3))],
        compiler_params=_params(), name=name)(*ps)


def _add_cores(g, r, core, *, name):
    _, A, B = g.shape
    ta = _tile(A, 512, 8)

    def body(core_ref, a_ref, b_ref, o_ref):
        o_ref[...] = a_ref[...] + b_ref[...]

    blk = (None, ta, B)
    return pl.pallas_call(
        body, out_shape=jax.ShapeDtypeStruct((4, A, B), g.dtype),
        grid_spec=pltpu.PrefetchScalarGridSpec(
            num_scalar_prefetch=1, grid=(4, A // ta),
            in_specs=[pl.BlockSpec(blk, lambda j, i, core_ref: (2 * j + core_ref[0], i, 0)),
                      pl.BlockSpec(blk, lambda j, i, core_ref: (j, i, 0))],
            out_specs=pl.BlockSpec(blk, lambda j, i, core_ref: (j, i, 0))),
        compiler_params=_params(("parallel", "parallel")), name=name)(core, g, r)


def _adamw_math(w, g, m, v):
    m = ADAM_B1 * m + (1.0 - ADAM_B1) * g
    v = ADAM_B2 * v + (1.0 - ADAM_B2) * (g * g)
    m_hat = m / (1.0 - ADAM_B1 ** ADAM_STEP)
    v_hat = v / (1.0 - ADAM_B2 ** ADAM_STEP)
    delta = -ADAM_LR * (m_hat / (jnp.sqrt(v_hat) + ADAM_EPS) + ADAM_WD * w)
    return delta, m, v


def _sum_adamw(p, r, chip, w, m, v, *, segs, ta, name):
    Aw, Bw = w.shape
    Bg = p.shape[2]
    assert Aw % ta == 0

    def body(chip_ref, p_ref, r0, r1, r2, w_ref, m_ref, v_ref, g_out, d_out, m_out, v_out):
        for gc, wc, n in segs:
            g = ((p_ref[:, gc:gc + n] + r0[:, gc:gc + n]) + r1[:, gc:gc + n]) + r2[:, gc:gc + n]
            delta, m_new, v_new = _adamw_math(w_ref[:, wc:wc + n], g, m_ref[:, wc:wc + n], v_ref[:, wc:wc + n])
            g_out[:, wc:wc + n] = g
            d_out[:, wc:wc + n] = delta
            m_out[:, wc:wc + n] = m_new
            v_out[:, wc:wc + n] = v_new

    gblk = (None, ta, Bg)
    row = pl.BlockSpec((ta, Bw), lambda i, chip_ref: (i, 0))
    rspecs = [pl.BlockSpec(gblk, (lambda i, chip_ref, k=k: (k, i, 0))) for k in range(3)]
    shp = jax.ShapeDtypeStruct((Aw, Bw), F32)
    return pl.pallas_call(
        body, out_shape=(shp, shp, shp, shp),
        grid_spec=pltpu.PrefetchScalarGridSpec(
            num_scalar_prefetch=1, grid=(Aw // ta,),
            in_specs=[pl.BlockSpec(gblk, lambda i, chip_ref: (chip_ref[0], i, 0))] + rspecs + [row, row, row],
            out_specs=(row, row, row, row)),
        compiler_params=_params(("parallel",)), name=name)(chip, p, r, r, r, w, m, v)


def _adamw(w, g, m, v, *, name):
    def body(w_ref, g_ref, m_ref, v_ref, d_out, m_out, v_out):
        delta, m_new, v_new = _adamw_math(w_ref[...], g_ref[...], m_ref[...], v_ref[...])
        d_out[...] = delta
        m_out[...] = m_new
        v_out[...] = v_new

    vm = pl.BlockSpec(memory_space=pltpu.VMEM)
    shp = jax.ShapeDtypeStruct(w.shape, F32)
    return pl.pallas_call(body, out_shape=(shp, shp, shp), in_specs=[vm] * 4, out_specs=(vm, vm, vm),
                          compiler_params=_params(), name=name)(w, g, m, v)


def _small_allreduce_adamw(s, w, m, v, *, name):
    R, W = s.shape

    def body(s_ref, w_ref, m_ref, v_ref, g_out, d_out, m_out, v_out, gath, send_sems, recv_sems):
        x, y, c = _coords()
        mine = 4 * x + 2 * y + c
        gath[mine] = s_ref[...]
        peers = [((1 - x) if k & 4 else x, (1 - y) if k & 2 else y, (1 - c) if k & 1 else c) for k in range(1, N_DEV)]
        sends = []
        for k in range(1, N_DEV):
            peer = peers[k - 1]
            sends.append(pltpu.make_async_remote_copy(
                src_ref=s_ref, dst_ref=gath.at[mine], send_sem=send_sems.at[k - 1], recv_sem=recv_sems.at[k - 1],
                device_id=peer, device_id_type=MESH))
        for cp in sends:
            cp.start()
        for k in range(1, N_DEV):
            peer = peers[k - 1]
            pltpu.make_async_remote_copy(
                src_ref=s_ref, dst_ref=gath.at[4 * peer[0] + 2 * peer[1] + peer[2]],
                send_sem=send_sems.at[k - 1], recv_sem=recv_sems.at[k - 1],
                device_id=peer, device_id_type=MESH).wait_recv()
        for cp in sends:
            cp.wait_send()
        g = gath[0]
        for d in range(1, N_DEV):
            g = g + gath[d]
        delta, m_new, v_new = _adamw_math(w_ref[...], g, m_ref[...], v_ref[...])
        g_out[...] = g
        d_out[...] = delta
        m_out[...] = m_new
        v_out[...] = v_new

    vm = pl.BlockSpec(memory_space=pltpu.VMEM)
    shp = jax.ShapeDtypeStruct((R, W), F32)
    return pl.pallas_call(
        body, out_shape=(shp, shp, shp, shp), in_specs=[vm] * 4, out_specs=(vm, vm, vm, vm),
        scratch_shapes=[pltpu.VMEM((N_DEV, R, W), F32), pltpu.SemaphoreType.DMA((N_DEV - 1,)),
                        pltpu.SemaphoreType.DMA((N_DEV - 1,))],
        compiler_params=_params(), name=name)(s, w, m, v)


def _pack_small(rel_bias, g1, g2, g3, g4, b_forget, sinks, extra=None, meta=None):
    misc = jnp.concatenate([rel_bias.reshape(-1), b_forget.reshape(-1), sinks.reshape(-1)])
    misc = jnp.concatenate([misc, jnp.zeros((D_MODEL - misc.shape[0],), F32)])[None]
    last = jnp.zeros((1, D_MODEL), F32) if extra is None else extra
    meta = jnp.zeros((N_META, D_MODEL), F32) if meta is None else meta
    return jnp.concatenate([g1, g2, g3, g4, misc, last, jnp.zeros((2, D_MODEL), F32), meta], axis=0)


def _unpack_small(p):
    nrb = N_BUCKETS * SWA_Q_HEADS
    misc = p[4]
    return dict(rel_bias=misc[:nrb].reshape(N_BUCKETS, SWA_Q_HEADS), ln_pre_mix=p[0:1], ln_post_mix=p[1:2],
                ln_pre_ffn=p[2:3], ln_post_ffn=p[3:4], b_forget=misc[nrb:nrb + 8].reshape(1, 8),
                sinks=misc[nrb + 8:nrb + 16].reshape(1, 8))


def _heads(a, n):
    return a.reshape(a.shape[0], n, HEAD_DIM).transpose(1, 0, 2)


def _unheads(a):
    return a.transpose(1, 0, 2).reshape(a.shape[1], -1)


def kernel(x, meta_tokens, rel_bias, ln_pre_mix, ln_post_mix, ln_pre_ffn, ln_post_ffn, w_in, b_forget, sinks, w_out, w_gate_up, w_down, loss_target, m_meta_tokens, m_rel_bias, m_ln_pre_mix, m_ln_post_mix, m_ln_pre_ffn, m_ln_post_ffn, m_w_in, m_b_forget, m_sinks, m_w_out, m_w_gate_up, m_w_down, v_meta_tokens, v_rel_bias, v_ln_pre_mix, v_ln_post_mix, v_ln_pre_ffn, v_ln_post_ffn, v_w_in, v_b_forget, v_sinks, v_w_out, v_w_gate_up, v_w_down):
    seq = x.shape[1]
    T = BLOCK + seq
    assert T % FOX_TILE == 0
    nq = T // FOX_TILE
    tm = _tile(T, 1056)
    cin = w_in.shape[2]
    hid = w_down.shape[1]
    assert w_gate_up.shape[2] == 2 * hid and cin <= W_IN_PAD and hid <= HID_PAD

    x_i, y_i, c_i = _coords()
    core = jnp.reshape(c_i, (1,)).astype(jnp.int32)
    chip = jnp.reshape(2 * x_i + y_i, (1,)).astype(jnp.int32)
    w_in_s = jnp.pad(w_in[0].astype(BF16), ((0, 0), (0, W_IN_PAD - cin)))
    w_gu_s = jnp.pad(w_gate_up[0].astype(BF16).reshape(D_MODEL, 2, hid), ((0, 0), (0, 0), (0, HID_PAD - hid)))
    w_gu_s = w_gu_s.reshape(D_MODEL, 2 * HID_PAD)
    w_down_s = jnp.pad(w_down[0].astype(BF16), ((0, HID_PAD - hid), (0, 0)))
    g_in, g_out, g_gu, g_down, g_meta = _all_gather(
        [w_in_s, w_out[0].astype(BF16), w_gu_s, w_down_s, meta_tokens], name="ag_weights")
    w_in_full = g_in[:, :, :cin].transpose(1, 0, 2).reshape(D_MODEL, N_DEV * cin)
    w_qkv = w_in_full[:, :D_QKV]
    w_f = jnp.pad(w_in_full[:, D_QKV:], ((0, 0), (0, BLOCK - FOX_HEADS)))
    w_in_cat = jnp.concatenate([w_qkv, w_f, jnp.zeros((D_MODEL, D_PROJ_PAD - D_QKV - BLOCK), BF16)], axis=1)
    w_out_full = g_out.reshape(D_MODEL, D_MODEL)
    w_down_full = g_down.reshape(N_DEV * HID_PAD, D_MODEL)
    meta_full = g_meta.transpose(1, 0, 2).reshape(N_META, D_MODEL)

    h0 = jnp.concatenate([jnp.zeros((PAD_ROWS, D_MODEL), F32), meta_full, x[0]], axis=0)
    target = jnp.concatenate([jnp.zeros((BLOCK, D_MODEL), F32), loss_target[0]], axis=0)
    hn1 = _rms_fwd(h0, ln_pre_mix, name="rms_pre_mix")
    proj = _matmul(hn1, w_qkv, out_dtype=BF16, tm=tm, tn=768, name="mm_in_proj")
    proj_f = _matmul(hn1, w_f, out_dtype=F32, tm=tm, tn=BLOCK, name="mm_in_proj_f")

    q_a = _heads(proj[:, 0:512], 8)
    k_a = jnp.pad(_heads(proj[:, 512:640], 2), ((0, 0), (BLOCK, 0), (0, 0)))
    v_a = jnp.pad(_heads(proj[:, 640:768], 2), ((0, 0), (BLOCK, 0), (0, 0)))
    q_b = _heads(proj[:, 768:1280], 8)
    k_b = _heads(proj[:, 1280:1792], 8)
    v_b = _heads(proj[:, 1792:2304], 8)
    f_t = proj_f[:, :FOX_HEADS].T
    bf_col = b_forget.reshape(FOX_HEADS, 1)

    oh_cur, oh_prev = _bucket_onehots()
    bias_c = jnp.einsum("pb,bh->hp", jnp.asarray(oh_cur), rel_bias, precision=HIGHEST).reshape(8, BLOCK, BLOCK)
    bias_p = jnp.einsum("pb,bh->hp", jnp.asarray(oh_prev), rel_bias, precision=HIGHEST).reshape(8, BLOCK, BLOCK)
    far = rel_bias[N_BUCKETS - 1]
    sink_v = sinks[0]
    o_a = _swa_fwd(q_a, k_a, v_a, bias_c, bias_p, far, sink_v, name="swa_fwd")

    cum = _fox_gates_fwd(f_t, bf_col, name="fox_gates_fwd")
    valid = (jnp.arange(T) >= PAD_ROWS)[None, :]
    cq_col = cum[:, :, None]
    cq_row = cum.reshape(FOX_HEADS, nq, 1, FOX_TILE)
    ck = jnp.where(valid, cum, -NEG)
    ck_row = ck.reshape(FOX_HEADS, nq, 1, FOX_TILE)
    ck_col = ck[:, :, None]
    o_b, lse = _fox_fwd(q_b, k_b, v_b, cq_col, ck_row, name="fox_fwd")

    mix = jnp.concatenate([_unheads(o_a), _unheads(o_b)], axis=1)
    a1 = _matmul(mix, w_out_full, out_dtype=F32, tm=tm, tn=512, name="mm_out_proj")
    h1 = _post_res(a1, ln_post_mix, h0, name="post_mix")
    hn2 = _rms_fwd(h1, ln_pre_ffn, name="rms_pre_ffn")
    gu = _matmul(hn2, g_gu, b_shards=True, out_dtype=BF16, tm=tm, name="mm_gate_up")
    act = _swiglu_fwd(gu, name="swiglu_fwd")
    ff = _matmul(act, w_down_full, out_dtype=F32, tm=tm, tn=512, name="mm_down")
    dh2, loss_acc = _loss_head(ff, ln_post_ffn, h1, target, name="loss_head")

    dff, dg_post_ffn = _rms_bwd(ff, ln_post_ffn, dh2, None, out_dtype=BF16, name="rms_bwd_post_ffn")
    dact = _matmul(dff, w_down_full, nt=True, out_dtype=BF16, tm=tm, tn=1536, name="mm_d_act")
    d_w_down = _matmul(act.T, dff, out_dtype=F32, tm=768, tn=512, name="mm_dw_down")
    dgu = _swiglu_bwd(gu, dact, name="swiglu_bwd")
    dhn2 = _matmul(dgu, g_gu, nt=True, b_shards=True, out_dtype=F32, tm=tm, tn=512, name="mm_d_hn2")
    d_w_gu = _matmul(hn2.T, dgu, out_shards=True, out_dtype=F32, tm=512, tn=2 * HID_PAD, name="mm_dw_gate_up")
    dh1, dg_pre_ffn = _rms_bwd(h1, ln_pre_ffn, dhn2, dh2, out_dtype=F32, name="rms_bwd_pre_ffn")
    da1, dg_post_mix = _rms_bwd(a1, ln_post_mix, dh1, None, out_dtype=BF16, name="rms_bwd_post_mix")
    dmix = _matmul(da1, w_out_full, nt=True, out_dtype=BF16, tm=tm, tn=512, name="mm_d_mix")
    d_w_out = _matmul(mix.T, da1, out_dtype=F32, tm=512, tn=512, name="mm_dw_out")

    do_a = _heads(dmix[:, :512], 8)
    do_b = _heads(dmix[:, 512:], 8)
    dq_a, dk_a, dv_a, dbc, dbp, dbf, dsk = _swa_bwd(q_a, k_a, v_a, do_a, bias_c, bias_p, far, sink_v, name="swa_bwd")
    d_tab, d_sink = _small_grads(dbc, dbp, dbf, dsk, jnp.asarray(oh_cur), jnp.asarray(oh_prev), name="small_grads")

    delta_row = _row_dot(o_b, do_b, name="fox_delta")
    lse_row = lse.reshape(FOX_HEADS, nq, 1, FOX_TILE)
    dq_b, dk_b, dv_b, dck, dcq = _fox_bwd(q_b, k_b, v_b, do_b, lse_row, delta_row, cq_row, ck_col, name="fox_bwd")
    df_t, d_bf = _fox_gates_bwd(dcq.reshape(FOX_HEADS, T), dck.reshape(FOX_HEADS, T), f_t, bf_col, name="fox_gates_bwd")

    dproj = jnp.concatenate([
        _unheads(dq_a), _unheads(dk_a[:, BLOCK:]).astype(BF16), _unheads(dv_a[:, BLOCK:]).astype(BF16),
        _unheads(dq_b).astype(BF16), _unheads(dk_b), _unheads(dv_b),
        df_t.T.astype(BF16), jnp.zeros((T, D_PROJ_PAD - D_PROJ), BF16)], axis=1)
    dhn1 = _matmul(dproj, w_in_cat, nt=True, out_dtype=F32, tm=tm, tn=512, name="mm_d_hn1")
    dproj_s = jnp.pad(dproj[:, :N_DEV * cin].reshape(T, N_DEV, cin), ((0, 0), (0, 0), (0, W_IN_PAD - cin)))
    d_w_in = _matmul(hn1.T, dproj_s.reshape(T, N_DEV * W_IN_PAD), out_shards=True, out_dtype=F32, tm=512,
                     tn=W_IN_PAD, name="mm_dw_in")
    dh0, dg_pre_mix = _rms_bwd(h0, ln_pre_mix, dhn1, dh1, out_dtype=F32, name="rms_bwd_pre_mix")
    grad_x = dh0[BLOCK:][None]
    d_meta = dh0[PAD_ROWS:BLOCK]

    by_dev = [d_w_in, d_w_out.reshape(N_DEV, -1, D_MODEL), d_w_gu, d_w_down.reshape(N_DEV, HID_PAD, D_MODEL)]
    tags = ["w_in", "w_out", "w_gate_up", "w_down"]
    from_sibling = _exchange_cores(by_dev, name="rs_cores")
    chip_sum = [_add_cores(g, r, core, name="rs_add_" + t) for g, r, t in zip(by_dev, from_sibling, tags)]
    from_chips = _exchange_chips(chip_sum, name="rs_chips")
    shard_w = [(w_in, m_w_in, v_w_in), (w_out, m_w_out, v_w_out), (w_gate_up, m_w_gate_up, v_w_gate_up),
               (w_down, m_w_down, v_w_down)]
    segs = [[(0, 0, cin)], [(0, 0, D_MODEL)], [(0, 0, hid), (HID_PAD, hid, hid)], [(0, 0, D_MODEL)]]
    tas = [256, BLOCK, 256, hid]
    big = [{}, {}, {}, {}]
    for i, t in enumerate(tags):
        w_t, m_t, v_t = shard_w[i]
        res = _sum_adamw(chip_sum[i], from_chips[i], chip, w_t[0], m_t[0], v_t[0], segs=segs[i], ta=tas[i],
                         name="rs_adamw_" + t)
        for kind in range(4):
            big[kind][t] = res[kind][None]

    loss_row = jnp.pad(loss_acc[0:1, 0:1] * (0.5 / D_MODEL), ((0, 0), (0, D_MODEL - 1)))
    s_small = _pack_small(d_tab.T, dg_pre_mix, dg_post_mix, dg_pre_ffn, dg_post_ffn, d_bf, d_sink,
                          extra=loss_row, meta=d_meta)
    w_s = _pack_small(rel_bias, ln_pre_mix, ln_post_mix, ln_pre_ffn, ln_post_ffn, b_forget, sinks)
    m_s = _pack_small(m_rel_bias, m_ln_pre_mix, m_ln_post_mix, m_ln_pre_ffn, m_ln_post_ffn, m_b_forget, m_sinks)
    v_s = _pack_small(v_rel_bias, v_ln_pre_mix, v_ln_post_mix, v_ln_pre_ffn, v_ln_post_ffn, v_b_forget, v_sinks)
    small = _small_allreduce_adamw(s_small, w_s, m_s, v_s, name="small_allreduce_adamw")
    loss = small[0][5, 0]
    mcols = meta_tokens.shape[1]
    g_meta_mine = lax.dynamic_slice(small[0][8:8 + N_META], (0, (4 * x_i + 2 * y_i + c_i) * mcols), (N_META, mcols))
    big[0]["meta_tokens"] = g_meta_mine
    for kind, arr in enumerate(_adamw(meta_tokens, g_meta_mine, m_meta_tokens, v_meta_tokens, name="adamw_meta")):
        big[kind + 1]["meta_tokens"] = arr
    small = [_unpack_small(p) for p in small]

    names = ["meta_tokens", "rel_bias", "ln_pre_mix", "ln_post_mix", "ln_pre_ffn", "ln_post_ffn", "w_in",
             "b_forget", "sinks", "w_out", "w_gate_up", "w_down"]
    outs = [loss, grad_x]
    for kind in range(4):
        for nme in names:
            outs.append(big[kind][nme] if nme in big[kind] else small[kind][nme])
    return tuple(outs)
```

```python
import math

import numpy as np
import jax
import jax.numpy as jnp
from jax import lax
from jax.experimental import pallas as pl
from jax.experimental.pallas import tpu as pltpu

F32 = jnp.float32
BF16 = jnp.bfloat16
HIGHEST = lax.Precision.HIGHEST
MESH = pl.DeviceIdType.MESH

N_DEV = 8
D_MODEL = 1024
N_META = 16
HEAD_DIM = 64
SWA_Q_HEADS = 8
SWA_KV_HEADS = 2
SWA_GROUP = 4
FOX_HEADS = 8
FOX_W = FOX_HEADS * HEAD_DIM
BLOCK = 128
PAD_ROWS = BLOCK - N_META
N_BUCKETS = 32
MAX_DISTANCE = 128
D_FF = 2816
D_QKV = 2304
D_PROJ = D_QKV + FOX_HEADS
D_PROJ_PAD = 2560
EPS = 1e-6
NEG = -1e30
SCALE = HEAD_DIM ** -0.5
ADAM_LR, ADAM_B1, ADAM_B2, ADAM_EPS, ADAM_WD, ADAM_STEP = 0.001, 0.9, 0.999, 1e-08, 0.01, 10
VMEM_LIMIT = 48 * 1024 * 1024
FOX_TILE = 384
W_IN_PAD = 384
HID_PAD = 384

NT = (((1,), (1,)), ((), ()))
NN = (((1,), (0,)), ((), ()))
TN = (((0,), (0,)), ((), ()))


def _params(sem=None, **kw):
    if sem is not None:
        kw["dimension_semantics"] = sem
    return pltpu.CompilerParams(vmem_limit_bytes=VMEM_LIMIT, **kw)


def _tile(n, target, mult=16):
    best = None
    for t in range(mult, min(n, target) + 1, mult):
        if n % t == 0:
            best = t
    assert best is not None, (n, target)
    return best


def _matmul(a, b, *, nt=False, b_shards=False, out_shards=False, out_dtype, tm, tn=None, tk=None, name):
    M, K = a.shape
    if b_shards and nt:
        N, tk = b.shape[1], b.shape[2]
    elif b_shards:
        N, tn = b.shape[0] * b.shape[2], b.shape[2]
    else:
        N = b.shape[0] if nt else b.shape[1]
    tk = K if tk is None else tk
    assert M % tm == 0 and N % tn == 0 and K % tk == 0, (name, a.shape, b.shape, tm, tn, tk)
    nk = K // tk
    dn = NT if nt else NN

    def body(a_ref, b_ref, o_ref, *scr):
        part = lax.dot_general(a_ref[...], b_ref[...], dn, preferred_element_type=F32)
        if nk == 1:
            o_ref[...] = part.astype(o_ref.dtype)
        else:
            acc = scr[0]
            k = pl.program_id(2)

            @pl.when(k == 0)
            def _():
                acc[...] = part

            @pl.when(k > 0)
            def _():
                acc[...] += part

            @pl.when(k == nk - 1)
            def _():
                o_ref[...] = acc[...].astype(o_ref.dtype)

    if b_shards and nt:
        b_spec = pl.BlockSpec((None, tn, tk), lambda i, j, k: (k, j, 0))
    elif b_shards:
        b_spec = pl.BlockSpec((None, tk, tn), lambda i, j, k: (j, k, 0))
    elif nt:
        b_spec = pl.BlockSpec((tn, tk), lambda i, j, k: (j, k))
    else:
        b_spec = pl.BlockSpec((tk, tn), lambda i, j, k: (k, j))
    if out_shards:
        out_shape = jax.ShapeDtypeStruct((N // tn, M, tn), out_dtype)
        out_spec = pl.BlockSpec((None, tm, tn), lambda i, j, k: (j, i, 0))
    else:
        out_shape = jax.ShapeDtypeStruct((M, N), out_dtype)
        out_spec = pl.BlockSpec((tm, tn), lambda i, j, k: (i, j))
    return pl.pallas_call(
        body,
        out_shape=out_shape,
        grid=(M // tm, N // tn, nk),
        in_specs=[pl.BlockSpec((tm, tk), lambda i, j, k: (i, k)), b_spec],
        out_specs=out_spec,
        scratch_shapes=[pltpu.VMEM((tm, tn), F32)] if nk > 1 else [],
        compiler_params=_params(("parallel", "parallel", "arbitrary")),
        name=name,
    )(a, b)


def _rstd(x):
    return lax.rsqrt(jnp.mean(x * x, axis=-1, keepdims=True) + EPS)


def _rms_fwd(x, g, *, name):
    T, D = x.shape
    tm = _tile(T, 512)

    def body(x_ref, g_ref, o_ref):
        x = x_ref[...]
        o_ref[...] = (x * _rstd(x) * g_ref[...]).astype(o_ref.dtype)

    return pl.pallas_call(
        body, out_shape=jax.ShapeDtypeStruct((T, D), BF16), grid=(T // tm,),
        in_specs=[pl.BlockSpec((tm, D), lambda i: (i, 0)), pl.BlockSpec((1, D), lambda i: (0, 0))],
        out_specs=pl.BlockSpec((tm, D), lambda i: (i, 0)),
        compiler_params=_params(("parallel",)), name=name)(x, g)


def _post_res(a, g, h, *, name):
    T, D = a.shape
    tm = _tile(T, 512)

    def body(a_ref, g_ref, h_ref, o_ref):
        a = a_ref[...]
        o_ref[...] = h_ref[...] + a * _rstd(a) * g_ref[...]

    row = pl.BlockSpec((tm, D), lambda i: (i, 0))
    return pl.pallas_call(
        body, out_shape=jax.ShapeDtypeStruct((T, D), F32), grid=(T // tm,),
        in_specs=[row, pl.BlockSpec((1, D), lambda i: (0, 0)), row], out_specs=row,
        compiler_params=_params(("parallel",)), name=name)(a, g, h)


def _loss_head(a, g, h, target, *, name):
    T, D = a.shape
    tm = _tile(T, 512)

    def body(a_ref, g_ref, h_ref, t_ref, dy_ref, loss_ref):
        i = pl.program_id(0)
        a = a_ref[...]
        y = h_ref[...] + a * _rstd(a) * g_ref[...]
        rows = i * tm + lax.broadcasted_iota(jnp.int32, (tm, 1), 0)
        err = jnp.where(rows >= BLOCK, y - t_ref[...], 0.0)
        dy_ref[...] = err / D
        part = jnp.sum(jnp.sum(err * err, axis=1, keepdims=True), axis=0, keepdims=True)

        @pl.when(i == 0)
        def _():
            loss_ref[...] = jnp.zeros_like(loss_ref)

        loss_ref[...] += jnp.broadcast_to(part, loss_ref.shape)

    row = pl.BlockSpec((tm, D), lambda i: (i, 0))
    return pl.pallas_call(
        body, out_shape=(jax.ShapeDtypeStruct((T, D), F32), jax.ShapeDtypeStruct((8, 128), F32)),
        grid=(T // tm,),
        in_specs=[row, pl.BlockSpec((1, D), lambda i: (0, 0)), row, row],
        out_specs=(row, pl.BlockSpec((8, 128), lambda i: (0, 0))),
        compiler_params=_params(("arbitrary",)), name=name)(a, g, h, target)


def _rms_bwd(x, g, dy, res, *, out_dtype, name):
    T, D = x.shape
    tm = _tile(T, 512)
    has_res = res is not None

    def body(*refs):
        if has_res:
            x_ref, g_ref, dy_ref, r_ref, dx_ref, dg_ref = refs
        else:
            x_ref, g_ref, dy_ref, dx_ref, dg_ref = refs
        i = pl.program_id(0)
        x = x_ref[...]
        dy = dy_ref[...].astype(F32)
        r = _rstd(x)
        xh = x * r
        dxh = dy * g_ref[...]
        dx = r * (dxh - xh * jnp.mean(dxh * xh, axis=-1, keepdims=True))
        if has_res:
            dx = dx + r_ref[...]
        dx_ref[...] = dx.astype(dx_ref.dtype)

        @pl.when(i == 0)
        def _():
            dg_ref[...] = jnp.zeros_like(dg_ref)

        dg_ref[...] += jnp.sum(dy * xh, axis=0, keepdims=True)

    row = pl.BlockSpec((tm, D), lambda i: (i, 0))
    vec = pl.BlockSpec((1, D), lambda i: (0, 0))
    ins = [x, g, dy] + ([res] if has_res else [])
    return pl.pallas_call(
        body, out_shape=(jax.ShapeDtypeStruct((T, D), out_dtype), jax.ShapeDtypeStruct((1, D), F32)),
        grid=(T // tm,),
        in_specs=[row, vec, row] + ([row] if has_res else []),
        out_specs=(row, vec),
        compiler_params=_params(("arbitrary",)), name=name)(*ins)


def _swiglu_fwd(gu, *, name):
    T, F2 = gu.shape
    F = F2 // 2
    tm = _tile(T, 384)

    def body(g_ref, u_ref, o_ref):
        g = g_ref[...].astype(F32)
        o_ref[...] = (g / (1.0 + jnp.exp(-g)) * u_ref[...].astype(F32)).astype(o_ref.dtype)

    return pl.pallas_call(
        body, out_shape=jax.ShapeDtypeStruct((T, F), BF16), grid=(T // tm,),
        in_specs=[pl.BlockSpec((tm, F), lambda i: (i, 0)), pl.BlockSpec((tm, F), lambda i: (i, 1))],
        out_specs=pl.BlockSpec((tm, F), lambda i: (i, 0)),
        compiler_params=_params(("parallel",)), name=name)(gu, gu)


def _swiglu_bwd(gu, dact, *, name):
    T, F2 = gu.shape
    F = F2 // 2
    tm = _tile(T, 384)

    def body(g_ref, u_ref, d_ref, o_ref):
        g = g_ref[...].astype(F32)
        u = u_ref[...].astype(F32)
        d = d_ref[...].astype(F32)
        sg = 1.0 / (1.0 + jnp.exp(-g))
        o_ref[:, :F] = (d * u * (sg * (1.0 + g * (1.0 - sg)))).astype(o_ref.dtype)
        o_ref[:, F:] = (d * (g * sg)).astype(o_ref.dtype)

    return pl.pallas_call(
        body, out_shape=jax.ShapeDtypeStruct((T, F2), BF16), grid=(T // tm,),
        in_specs=[pl.BlockSpec((tm, F), lambda i: (i, 0)), pl.BlockSpec((tm, F), lambda i: (i, 1)),
                  pl.BlockSpec((tm, F), lambda i: (i, 0))],
        out_specs=pl.BlockSpec((tm, F2), lambda i: (i, 0)),
        compiler_params=_params(("parallel",)), name=name)(gu, gu, dact)


def _fox_gates_fwd(f_t, b, *, name):
    H, T = f_t.shape
    nb = T // BLOCK

    def body(f_ref, b_ref, cum_ref, col_ref):
        f = f_ref[...] + b_ref[...]
        ls = jnp.minimum(f, 0.0) - jnp.log(1.0 + jnp.exp(-jnp.abs(f)))
        t = lax.broadcasted_iota(jnp.int32, (H, T), 1)
        ls = jnp.where(t >= PAD_ROWS, ls, 0.0)
        upper = (lax.broadcasted_iota(jnp.int32, (BLOCK, BLOCK), 0)
                 <= lax.broadcasted_iota(jnp.int32, (BLOCK, BLOCK), 1)).astype(F32)
        carry = jnp.zeros((H, 1), F32)
        for blk in range(nb):
            seg = ls[:, blk * BLOCK:(blk + 1) * BLOCK]
            pre = jnp.dot(seg, upper, precision=HIGHEST, preferred_element_type=F32) + carry
            cum_ref[:, blk * BLOCK:(blk + 1) * BLOCK] = pre
            col_ref[blk * BLOCK:(blk + 1) * BLOCK, :] = jnp.concatenate(
                [pre, jnp.zeros((BLOCK - H, BLOCK), F32)], axis=0).T
            carry = pre[:, BLOCK - 1:BLOCK]

    vm = pl.BlockSpec(memory_space=pltpu.VMEM)
    return pl.pallas_call(
        body, out_shape=(jax.ShapeDtypeStruct((H, T), F32), jax.ShapeDtypeStruct((T, BLOCK), F32)),
        in_specs=[vm, vm], out_specs=(vm, vm),
        compiler_params=_params(), name=name)(f_t, b)


def _fox_gates_bwd(dcq, dck, f_t, b, *, name):
    H, T = f_t.shape
    nb = T // BLOCK

    def body(dq_ref, d_ref, f_ref, b_ref, df_ref, db_ref):
        lower = (lax.broadcasted_iota(jnp.int32, (BLOCK, BLOCK), 0)
                 >= lax.broadcasted_iota(jnp.int32, (BLOCK, BLOCK), 1)).astype(F32)
        carry = jnp.zeros((H, 1), F32)
        for blk in range(nb - 1, -1, -1):
            seg = dq_ref[:, blk * BLOCK:(blk + 1) * BLOCK] - d_ref[:, blk * BLOCK:(blk + 1) * BLOCK]
            suf = jnp.dot(seg, lower, precision=HIGHEST, preferred_element_type=F32) + carry
            df_ref[:, blk * BLOCK:(blk + 1) * BLOCK] = suf
            carry = suf[:, 0:1]
        f = f_ref[...] + b_ref[...]
        t = lax.broadcasted_iota(jnp.int32, (H, T), 1)
        df = jnp.where(t >= PAD_ROWS, df_ref[...] / (1.0 + jnp.exp(f)), 0.0)
        df_ref[...] = df
        db_ref[...] = jnp.sum(df, axis=1, keepdims=True)

    vm = pl.BlockSpec(memory_space=pltpu.VMEM)
    return pl.pallas_call(
        body, out_shape=(jax.ShapeDtypeStruct((H, T), F32), jax.ShapeDtypeStruct((H, 1), F32)),
        in_specs=[vm, vm, vm, vm], out_specs=(vm, vm),
        compiler_params=_params(), name=name)(dcq, dck, f_t, b)


LANE_KC = HEAD_DIM
LANE_QC = HEAD_DIM + 3
LANE_END = HEAD_DIM + 6


def _split3(c):
    hi = c.astype(BF16).astype(F32)
    r = c - hi
    mid = r.astype(BF16).astype(F32)
    lo = (r - mid).astype(BF16).astype(F32)
    return hi, mid, lo


def _lanes(lane, data, start, terms, rest):
    out = rest
    for i, t in enumerate(terms):
        out = jnp.where(lane == start + i, t, out)
    return jnp.where(lane < HEAD_DIM, data, out)


def _fox_prep(proj, cum_col, *, name):
    T = proj.shape[0]
    tm = FOX_TILE
    nt = T // tm
    H = FOX_HEADS
    lanes = 2 * HEAD_DIM
    qb, kb, vb = 768 // lanes, 1280 // lanes, 1792 // lanes

    def body(q_ref, k_ref, v_ref, c_ref, qa_ref, ka_ref, va_ref):
        p = pl.program_id(0)
        i = pl.program_id(1)
        lane = lax.broadcasted_iota(jnp.int32, (tm, lanes), 1)
        rows = i * tm + lax.broadcasted_iota(jnp.int32, (tm, 1), 0)
        q2 = q_ref[...].astype(F32)
        k2 = k_ref[...].astype(F32)
        v2 = v_ref[...].astype(F32)
        cum = c_ref[...]
        for e in range(2):
            c = jnp.sum(jnp.where(lane == 2 * p + e, cum, 0.0), axis=1, keepdims=True)
            ck = jnp.where(rows >= PAD_ROWS, c, -NEG)
            qe, ke, ve = (q2, k2, v2) if e == 0 else tuple(pltpu.roll(a, HEAD_DIM, 1) for a in (q2, k2, v2))
            one = jnp.where(lane < LANE_END, 1.0, 0.0)
            qa = _lanes(lane, qe * SCALE, LANE_QC, _split3(c), jnp.where(lane < LANE_QC, -1.0, 0.0))
            ka = _lanes(lane, ke, LANE_KC, _split3(ck), one)
            va = jnp.where(lane < HEAD_DIM, ve, jnp.where(lane < LANE_QC, 1.0, 0.0))
            qa_ref[e] = qa.astype(BF16)
            ka_ref[e] = ka.astype(BF16)
            va_ref[e] = va.astype(BF16)

    def col(b):
        return pl.BlockSpec((tm, lanes), lambda p, i, b=b: (i, b + p))

    out = pl.BlockSpec((2, tm, lanes), lambda p, i: (p, i, 0))
    shp = jax.ShapeDtypeStruct((H, T, lanes), BF16)
    return pl.pallas_call(
        body, out_shape=(shp, shp, shp), grid=(H // 2, nt),
        in_specs=[col(qb), col(kb), col(vb), pl.BlockSpec((tm, lanes), lambda p, i: (i, 0))],
        out_specs=(out, out, out),
        compiler_params=_params(("parallel", "parallel")), name=name)(proj, proj, proj, cum_col)


def _fox_fwd(q_aug, k_aug, v_aug, *, name):
    H, T, lanes = q_aug.shape
    tq = FOX_TILE
    nq = T // tq

    def body(q_ref, k_ref, v_ref, o_ref, lse_ref, m_scr, acc_scr):
        i = pl.program_id(1)
        qa = q_ref[...]
        m_scr[...] = jnp.full(m_scr.shape, NEG, F32)
        acc_scr[...] = jnp.zeros(acc_scr.shape, F32)

        def step(kb, diag):
            off = pl.multiple_of(kb * tq, tq)
            s_t = lax.dot_general(k_ref[pl.ds(off, tq), :], qa, NT, preferred_element_type=F32)
            if diag:
                r = lax.broadcasted_iota(jnp.int32, (tq, tq), 0)
                c = lax.broadcasted_iota(jnp.int32, (tq, tq), 1)
                s_t = jnp.where(c >= r, s_t, NEG)
            m_prev = m_scr[...]
            m_new = jnp.maximum(m_prev, jnp.max(s_t, axis=0, keepdims=True))
            p_t = jnp.exp(s_t - m_new).astype(BF16)
            alpha = jnp.exp(m_prev - m_new)
            acc_scr[...] = alpha * acc_scr[...] + lax.dot_general(
                v_ref[pl.ds(off, tq), :], p_t, TN, preferred_element_type=F32)
            m_scr[...] = m_new

        def loop_body(kb, carry):
            step(kb, False)
            return carry

        lax.fori_loop(0, i, loop_body, 0)
        step(i, True)
        acc = acc_scr[...]
        lse_ref[...] = m_scr[...] + jnp.log(acc[HEAD_DIM:HEAD_DIM + 1, :])
        acc_t = acc.T
        o_ref[...] = (acc_t[:, :HEAD_DIM] / acc_t[:, HEAD_DIM:HEAD_DIM + 1]).astype(o_ref.dtype)

    blk = pl.BlockSpec((None, tq, lanes), lambda h, i: (h, i, 0))
    full = pl.BlockSpec((None, T, lanes), lambda h, i: (h, 0, 0))
    return pl.pallas_call(
        body,
        out_shape=(jax.ShapeDtypeStruct((H, T, HEAD_DIM), BF16), jax.ShapeDtypeStruct((H, nq, 1, tq), F32)),
        grid=(H, nq),
        in_specs=[blk, full, full],
        out_specs=(pl.BlockSpec((None, tq, HEAD_DIM), lambda h, i: (h, i, 0)),
                   pl.BlockSpec((None, None, 1, tq), lambda h, i: (h, i, 0, 0))),
        scratch_shapes=[pltpu.VMEM((1, tq), F32), pltpu.VMEM((lanes, tq), F32)],
        compiler_params=_params(("parallel", "arbitrary")), name=name)(q_aug, k_aug, v_aug)


def _fox_prep_bwd(dmix, o, *, name):
    T = dmix.shape[0]
    H = o.shape[0]
    tm = FOX_TILE
    lanes = 2 * HEAD_DIM
    first = 512 // lanes

    def body(d_ref, o_ref, da_ref):
        lane = lax.broadcasted_iota(jnp.int32, (tm, lanes), 1)
        d2 = d_ref[...].astype(F32)
        for e in range(2):
            de = d2 if e == 0 else pltpu.roll(d2, HEAD_DIM, 1)
            d64 = d_ref[:, e * HEAD_DIM:(e + 1) * HEAD_DIM].astype(F32)
            delta = jnp.sum(d64 * o_ref[e].astype(F32), axis=1, keepdims=True)
            da_ref[e] = _lanes(lane, de, LANE_KC, _split3(-delta), jnp.zeros((), F32)).astype(BF16)

    return pl.pallas_call(
        body, out_shape=jax.ShapeDtypeStruct((H, T, lanes), BF16), grid=(H // 2, T // tm),
        in_specs=[pl.BlockSpec((tm, lanes), lambda p, i: (i, first + p)),
                  pl.BlockSpec((2, tm, HEAD_DIM), lambda p, i: (p, i, 0))],
        out_specs=pl.BlockSpec((2, tm, lanes), lambda p, i: (p, i, 0)),
        compiler_params=_params(("parallel", "parallel")), name=name)(dmix, o)


def _fox_bwd(q_aug, k_aug, v_aug, do_aug, lse_row, *, name):
    H, T, lanes = q_aug.shape
    tq = FOX_TILE
    nq = T // tq

    def body(q_ref, k_ref, v_ref, do_ref, lse_ref, dq_ref, dk_ref, dv_ref, dck_ref, dk_acc, dv_acc):
        j = pl.program_id(1)

        @pl.when(j == 0)
        def _():
            dq_ref[...] = jnp.zeros(dq_ref.shape, F32)

        ka = k_ref[...]
        va = v_ref[...]
        dk_acc[...] = jnp.zeros(dk_acc.shape, F32)
        dv_acc[...] = jnp.zeros(dv_acc.shape, F32)

        def step(qb, diag):
            off = pl.multiple_of(qb * tq, tq)
            qa = q_ref[pl.ds(off, tq), :]
            da = do_ref[pl.ds(off, tq), :]
            s_t = lax.dot_general(ka, qa, NT, preferred_element_type=F32)
            p_t = jnp.exp(s_t - lse_ref[qb])
            if diag:
                r = lax.broadcasted_iota(jnp.int32, (tq, tq), 0)
                c = lax.broadcasted_iota(jnp.int32, (tq, tq), 1)
                p_t = jnp.where(c >= r, p_t, 0.0)
            dv_acc[...] += jnp.dot(p_t.astype(BF16), da, preferred_element_type=F32)
            dp_t = lax.dot_general(va, da, NT, preferred_element_type=F32)
            dsb = (p_t * dp_t).astype(BF16)
            dk_acc[...] += jnp.dot(dsb, qa, preferred_element_type=F32)
            dq_ref[qb] += lax.dot_general(ka, dsb, TN, preferred_element_type=F32)

        step(j, True)

        def loop_body(qb, carry):
            step(qb, False)
            return carry

        lax.fori_loop(j + 1, nq, loop_body, 0)
        dk = dk_acc[...]
        dk_ref[...] = dk.astype(dk_ref.dtype)
        dck_ref[...] = -dk[:, LANE_KC:LANE_KC + 1]
        dv_ref[...] = dv_acc[...].astype(dv_ref.dtype)

    blk = pl.BlockSpec((None, tq, lanes), lambda h, j: (h, j, 0))
    full = pl.BlockSpec((None, T, lanes), lambda h, j: (h, 0, 0))
    return pl.pallas_call(
        body,
        out_shape=(jax.ShapeDtypeStruct((H, nq, lanes, tq), F32), jax.ShapeDtypeStruct((H, T, lanes), BF16),
                   jax.ShapeDtypeStruct((H, T, lanes), BF16), jax.ShapeDtypeStruct((H, T, 1), F32)),
        grid=(H, nq),
        in_specs=[full, blk, blk, full, pl.BlockSpec((None, nq, 1, tq), lambda h, j: (h, 0, 0, 0))],
        out_specs=(pl.BlockSpec((None, nq, lanes, tq), lambda h, j: (h, 0, 0, 0)), blk, blk,
                   pl.BlockSpec((None, tq, 1), lambda h, j: (h, j, 0))),
        scratch_shapes=[pltpu.VMEM((tq, lanes), F32), pltpu.VMEM((tq, lanes), F32)],
        compiler_params=_params(("parallel", "arbitrary")), name=name,
    )(q_aug, k_aug, v_aug, do_aug, lse_row)


def _t5_bucket_np(d):
    n = np.maximum(d, 0).astype(np.int32)
    max_exact = N_BUCKETS // 2
    nf = np.maximum(n, 1).astype(np.float32)
    large = max_exact + (np.log(nf / max_exact) / math.log(MAX_DISTANCE / max_exact)
                         * (N_BUCKETS - max_exact)).astype(np.int32)
    large = np.minimum(large, N_BUCKETS - 1)
    return np.where(n < max_exact, n, large)


def _bucket_onehots():
    r = np.arange(BLOCK)[:, None]
    c = np.arange(BLOCK)[None, :]
    eye = np.eye(N_BUCKETS, dtype=np.float32)
    cur = eye[_t5_bucket_np(r - c).reshape(-1)]
    prev = eye[_t5_bucket_np(BLOCK + r - c).reshape(-1)]
    return cur, prev


def _swa_probs(qs, kc, kp, km, bc, bp, far, sink, n):
    r = lax.broadcasted_iota(jnp.int32, (BLOCK, BLOCK), 0)
    c = lax.broadcasted_iota(jnp.int32, (BLOCK, BLOCK), 1)
    never = 2 * BLOCK
    s_c = lax.dot_general(qs, kc, NT, preferred_element_type=F32) + bc
    s_p = lax.dot_general(qs, kp, NT, preferred_element_type=F32) + bp
    s_m = lax.dot_general(qs, km, NT, preferred_element_type=F32) + jnp.where(n == 1, bp, far)
    s_c = jnp.where((c <= r) & (c >= jnp.where(n >= 1, 0, PAD_ROWS)), s_c, NEG)
    s_p = jnp.where(c > r + jnp.where(n >= 2, 0, never), s_p, NEG)
    s_m = jnp.where(c >= jnp.where(n >= 1, PAD_ROWS, never), s_m, NEG)
    m = jnp.maximum(jnp.maximum(jnp.max(s_c, axis=1, keepdims=True), jnp.max(s_p, axis=1, keepdims=True)),
                    jnp.maximum(jnp.max(s_m, axis=1, keepdims=True), sink))
    e_c = jnp.exp(s_c - m)
    e_p = jnp.exp(s_p - m)
    e_m = jnp.exp(s_m - m)
    e_s = jnp.exp(sink - m)
    l = (jnp.sum(e_c, axis=1, keepdims=True) + jnp.sum(e_p, axis=1, keepdims=True)
         + jnp.sum(e_m, axis=1, keepdims=True) + e_s)
    return e_c, e_p, e_m, e_s, l


def _swa_specs(T):
    G = SWA_GROUP
    qblk = pl.BlockSpec((G, BLOCK, HEAD_DIM), lambda kv, n: (kv, n, 0))
    cur = pl.BlockSpec((None, BLOCK, HEAD_DIM), lambda kv, n: (kv, n + 1, 0))
    prev = pl.BlockSpec((None, BLOCK, HEAD_DIM), lambda kv, n: (kv, n, 0))
    meta = pl.BlockSpec((None, BLOCK, HEAD_DIM), lambda kv, n: (kv, 1, 0))
    bias = pl.BlockSpec((G, BLOCK, BLOCK), lambda kv, n: (kv, 0, 0))
    smem = pl.BlockSpec(memory_space=pltpu.SMEM)
    return qblk, cur, prev, meta, bias, smem


def _swa_fwd(q, kpad, vpad, bc, bp, far, sinks, *, name):
    Hq, T, dh = q.shape
    nb = T // BLOCK
    G = SWA_GROUP

    def body(q_ref, kc_ref, kp_ref, km_ref, vc_ref, vp_ref, vm_ref, bc_ref, bp_ref, far_ref, sink_ref, o_ref):
        kv = pl.program_id(0)
        n = pl.program_id(1)
        kc, kp, km = kc_ref[...], kp_ref[...], km_ref[...]
        vc, vp, vm = vc_ref[...], vp_ref[...], vm_ref[...]
        for g in range(G):
            h = kv * G + g
            qs = q_ref[g] * SCALE
            e_c, e_p, e_m, _, l = _swa_probs(qs, kc, kp, km, bc_ref[g], bp_ref[g], far_ref[h], sink_ref[h], n)
            o = (jnp.dot(e_c.astype(BF16), vc, preferred_element_type=F32)
                 + jnp.dot(e_p.astype(BF16), vp, preferred_element_type=F32)
                 + jnp.dot(e_m.astype(BF16), vm, preferred_element_type=F32))
            o_ref[g] = (o / l).astype(o_ref.dtype)

    qblk, cur, prev, meta, bias, smem = _swa_specs(T)
    return pl.pallas_call(
        body, out_shape=jax.ShapeDtypeStruct((Hq, T, dh), BF16), grid=(SWA_KV_HEADS, nb),
        in_specs=[qblk, cur, prev, meta, cur, prev, meta, bias, bias, smem, smem],
        out_specs=qblk,
        compiler_params=_params(("parallel", "parallel")), name=name,
    )(q, kpad, kpad, kpad, vpad, vpad, vpad, bc, bp, far, sinks)


def _swa_bwd(q, kpad, vpad, do, bc, bp, far, sinks, *, name):
    Hq, T, dh = q.shape
    nb = T // BLOCK
    G = SWA_GROUP

    def body(q_ref, kc_ref, kp_ref, km_ref, vc_ref, vp_ref, vm_ref, do_ref, bc_ref, bp_ref, far_ref, sink_ref,
             dq_ref, dk_ref, dv_ref, dbc_ref, dbp_ref, dbf_ref, dsk_ref):
        kv = pl.program_id(0)
        n = pl.program_id(1)

        @pl.when(n == 0)
        def _():
            for ref in (dk_ref, dv_ref, dbc_ref, dbp_ref, dbf_ref, dsk_ref):
                ref[...] = jnp.zeros(ref.shape, F32)

        kc, kp, km = kc_ref[...], kp_ref[...], km_ref[...]
        vc, vp, vm = vc_ref[...], vp_ref[...], vm_ref[...]
        dkc = dkp = dkm = dvc = dvp = dvm = jnp.zeros((BLOCK, dh), F32)
        for g in range(G):
            h = kv * G + g
            qs = q_ref[g] * SCALE
            e_c, e_p, e_m, e_s, l = _swa_probs(qs, kc, kp, km, bc_ref[g], bp_ref[g], far_ref[h], sink_ref[h], n)
            inv = 1.0 / l
            p_c, p_p, p_m = e_c * inv, e_p * inv, e_m * inv
            dob = do_ref[g]
            dp_c = lax.dot_general(dob, vc, NT, preferred_element_type=F32)
            dp_p = lax.dot_general(dob, vp, NT, preferred_element_type=F32)
            dp_m = lax.dot_general(dob, vm, NT, preferred_element_type=F32)
            delta = (jnp.sum(p_c * dp_c, axis=1, keepdims=True) + jnp.sum(p_p * dp_p, axis=1, keepdims=True)
                     + jnp.sum(p_m * dp_m, axis=1, keepdims=True))
            ds_c = p_c * (dp_c - delta)
            ds_p = p_p * (dp_p - delta)
            ds_m = p_m * (dp_m - delta)
            dsk_ref[g] += -(e_s * inv) * delta
            dbc_ref[g] += ds_c
            dbp_ref[g] += ds_p + jnp.where(n == 1, ds_m, 0.0)
            dbf_ref[g] += jnp.where(n >= 2, ds_m, 0.0)
            bc16, bp16, bm16 = ds_c.astype(BF16), ds_p.astype(BF16), ds_m.astype(BF16)
            dq = (jnp.dot(bc16, kc, preferred_element_type=F32) + jnp.dot(bp16, kp, preferred_element_type=F32)
                  + jnp.dot(bm16, km, preferred_element_type=F32))
            dq_ref[g] = (dq * SCALE).astype(dq_ref.dtype)
            dkc += lax.dot_general(bc16, qs, TN, preferred_element_type=F32)
            dkp += lax.dot_general(bp16, qs, TN, preferred_element_type=F32)
            dkm += lax.dot_general(bm16, qs, TN, preferred_element_type=F32)
            dvc += lax.dot_general(p_c.astype(BF16), dob, TN, preferred_element_type=F32)
            dvp += lax.dot_general(p_p.astype(BF16), dob, TN, preferred_element_type=F32)
            dvm += lax.dot_general(p_m.astype(BF16), dob, TN, preferred_element_type=F32)
        cur_off = pl.multiple_of((n + 1) * BLOCK, BLOCK)
        prev_off = pl.multiple_of(n * BLOCK, BLOCK)
        dk_ref[pl.ds(cur_off, BLOCK), :] += dkc
        dk_ref[pl.ds(prev_off, BLOCK), :] += dkp
        dk_ref[BLOCK:2 * BLOCK, :] += dkm
        dv_ref[pl.ds(cur_off, BLOCK), :] += dvc
        dv_ref[pl.ds(prev_off, BLOCK), :] += dvp
        dv_ref[BLOCK:2 * BLOCK, :] += dvm

    qblk, cur, prev, meta, bias, smem = _swa_specs(T)
    kvfull = pl.BlockSpec((None, T + BLOCK, dh), lambda kv, n: (kv, 0, 0))
    dsk = pl.BlockSpec((G, BLOCK, 1), lambda kv, n: (kv, 0, 0))
    return pl.pallas_call(
        body,
        out_shape=(jax.ShapeDtypeStruct((Hq, T, dh), BF16),
                   jax.ShapeDtypeStruct((SWA_KV_HEADS, T + BLOCK, dh), F32),
                   jax.ShapeDtypeStruct((SWA_KV_HEADS, T + BLOCK, dh), F32),
                   jax.ShapeDtypeStruct((Hq, BLOCK, BLOCK), F32), jax.ShapeDtypeStruct((Hq, BLOCK, BLOCK), F32),
                   jax.ShapeDtypeStruct((Hq, BLOCK, BLOCK), F32), jax.ShapeDtypeStruct((Hq, BLOCK, 1), F32)),
        grid=(SWA_KV_HEADS, nb),
        in_specs=[qblk, cur, prev, meta, cur, prev, meta, qblk, bias, bias, smem, smem],
        out_specs=(qblk, kvfull, kvfull, bias, bias, bias, dsk),
        compiler_params=_params(("parallel", "arbitrary")), name=name,
    )(q, kpad, kpad, kpad, vpad, vpad, vpad, do, bc, bp, far, sinks)


def _small_grads(dbc, dbp, dbf, dsk, oh_cur, oh_prev, *, name):
    Hq = dbc.shape[0]

    def body(dbc_ref, dbp_ref, dbf_ref, dsk_ref, oc_ref, op_ref, tab_ref, sink_ref):
        tab = (jnp.dot(dbc_ref[...], oc_ref[...], precision=HIGHEST, preferred_element_type=F32)
               + jnp.dot(dbp_ref[...], op_ref[...], precision=HIGHEST, preferred_element_type=F32))
        far = jnp.sum(dbf_ref[...], axis=1, keepdims=True)
        last = lax.broadcasted_iota(jnp.int32, (Hq, N_BUCKETS), 1) == N_BUCKETS - 1
        tab_ref[...] = tab + jnp.where(last, far, 0.0)
        sink_ref[...] = jnp.sum(dsk_ref[...], axis=1, keepdims=True)

    vm = pl.BlockSpec(memory_space=pltpu.VMEM)
    return pl.pallas_call(
        body, out_shape=(jax.ShapeDtypeStruct((Hq, N_BUCKETS), F32), jax.ShapeDtypeStruct((Hq, 1), F32)),
        in_specs=[vm] * 6, out_specs=(vm, vm), compiler_params=_params(), name=name,
    )(dbc.reshape(Hq, -1), dbp.reshape(Hq, -1), dbf.reshape(Hq, -1), dsk.reshape(Hq, -1), oh_cur, oh_prev)


def _coords():
    return lax.axis_index("x"), lax.axis_index("y"), lax.axis_index("c")


def _all_gather(shards, *, name):
    nt = len(shards)

    def body(*refs):
        ins, outs = refs[:nt], refs[nt:2 * nt]
        send_sems, recv_sems, local_sems = refs[2 * nt:]
        x, y, c = _coords()
        me, sibling = (x, y, c), (x, y, 1 - c)
        chips = [(1 - x, y), (x, 1 - y), (1 - x, 1 - y)]

        def slot(t, dev):
            return outs[t].at[4 * dev[0] + 2 * dev[1] + dev[2]]

        def copy(t, k, block, to, src=None):
            dst = slot(t, block)
            return pltpu.make_async_remote_copy(
                src_ref=dst if src is None else src, dst_ref=dst,
                send_sem=send_sems.at[t, k], recv_sem=recv_sems.at[t, k], device_id=to, device_id_type=MESH)

        mine = [pltpu.make_async_copy(ins[t], slot(t, me), local_sems.at[t]) for t in range(nt)]
        for cp in mine:
            cp.start()
        first = []
        for t in range(nt):
            first.append(copy(t, 0, me, sibling, src=ins[t]))
            first += [copy(t, 1 + j, me, (*chip, c), src=ins[t]) for j, chip in enumerate(chips)]
        for cp in first:
            cp.start()
        passed = []
        for j, chip in enumerate(chips):
            for t in range(nt):
                copy(t, 1 + j, (*chip, c), me).wait_recv()
                cp = copy(t, 4 + j, (*chip, c), sibling)
                cp.start()
                passed.append(cp)
        for t in range(nt):
            copy(t, 0, sibling, me).wait_recv()
            for j, chip in enumerate(chips):
                copy(t, 4 + j, (*chip, 1 - c), me).wait_recv()
        for cp in first + passed:
            cp.wait_send()
        for cp in mine:
            cp.wait()

    hbm = pl.BlockSpec(memory_space=pl.ANY)
    return pl.pallas_call(
        body,
        out_shape=tuple(jax.ShapeDtypeStruct((N_DEV,) + s.shape, s.dtype) for s in shards),
        in_specs=[hbm] * nt, out_specs=tuple([hbm] * nt),
        scratch_shapes=[pltpu.SemaphoreType.DMA((nt, 7)), pltpu.SemaphoreType.DMA((nt, 7)),
                        pltpu.SemaphoreType.DMA((nt,))],
        compiler_params=_params(), name=name)(*shards)


def _exchange_cores(gs, *, name):
    nt = len(gs)

    def body(*refs):
        g_refs, o_refs = refs[:nt], refs[nt:2 * nt]
        send_sems, recv_sems = refs[2 * nt:]
        x, y, c = _coords()
        sibling = (x, y, 1 - c)

        def copy(t, j, core):
            return pltpu.make_async_remote_copy(
                src_ref=g_refs[t].at[2 * j + core], dst_ref=o_refs[t].at[j],
                send_sem=send_sems.at[t, j], recv_sem=recv_sems.at[t, j], device_id=sibling, device_id_type=MESH)

        sends = [copy(t, j, 1 - c) for t in range(nt) for j in range(4)]
        for cp in sends:
            cp.start()
        for cp in sends:
            cp.wait_recv()
        for cp in sends:
            cp.wait_send()

    hbm = pl.BlockSpec(memory_space=pl.ANY)
    return pl.pallas_call(
        body, out_shape=tuple(jax.ShapeDtypeStruct((4,) + g.shape[1:], g.dtype) for g in gs),
        in_specs=[hbm] * nt, out_specs=tuple([hbm] * nt),
        scratch_shapes=[pltpu.SemaphoreType.DMA((nt, 4)), pltpu.SemaphoreType.DMA((nt, 4))],
        compiler_params=_params(), name=name)(*gs)


def _exchange_chips(ps, *, name):
    nt = len(ps)

    def body(*refs):
        p_refs, o_refs = refs[:nt], refs[nt:2 * nt]
        send_sems, recv_sems = refs[2 * nt:]
        x, y, c = _coords()
        peers = [(1 - x, y), (x, 1 - y), (1 - x, 1 - y)]
        sends = []
        for t in range(nt):
            for k, (px, py) in enumerate(peers):
                sends.append(pltpu.make_async_remote_copy(
                    src_ref=p_refs[t].at[2 * px + py], dst_ref=o_refs[t].at[k],
                    send_sem=send_sems.at[t, k], recv_sem=recv_sems.at[t, k],
                    device_id=(px, py, c), device_id_type=MESH))
        for cp in sends:
            cp.start()
        for cp in sends:
            cp.wait_recv()
        for cp in sends:
            cp.wait_send()

    hbm = pl.BlockSpec(memory_space=pl.ANY)
    return pl.pallas_call(
        body, out_shape=tuple(jax.ShapeDtypeStruct((3,) + p.shape[1:], p.dtype) for p in ps),
        in_specs=[hbm] * nt, out_specs=tuple([hbm] * nt),
        scratch_shapes=[pltpu.SemaphoreType.DMA((nt, 3)), pltpu.SemaphoreType.DMA((nt, 3))],
        compiler_params=_params(), name=name)(*ps)


def _add_cores(g, r, core, *, name):
    _, A, B = g.shape
    ta = _tile(A, 512, 8)

    def body(core_ref, a_ref, b_ref, o_ref):
        o_ref[...] = a_ref[...] + b_ref[...]

    blk = (None, ta, B)
    return pl.pallas_call(
        body, out_shape=jax.ShapeDtypeStruct((4, A, B), g.dtype),
        grid_spec=pltpu.PrefetchScalarGridSpec(
            num_scalar_prefetch=1, grid=(4, A // ta),
            in_specs=[pl.BlockSpec(blk, lambda j, i, core_ref: (2 * j + core_ref[0], i, 0)),
                      pl.BlockSpec(blk, lambda j, i, core_ref: (j, i, 0))],
            out_specs=pl.BlockSpec(blk, lambda j, i, core_ref: (j, i, 0))),
        compiler_params=_params(("parallel", "parallel")), name=name)(core, g, r)


def _adamw_math(w, g, m, v):
    m = ADAM_B1 * m + (1.0 - ADAM_B1) * g
    v = ADAM_B2 * v + (1.0 - ADAM_B2) * (g * g)
    m_hat = m / (1.0 - ADAM_B1 ** ADAM_STEP)
    v_hat = v / (1.0 - ADAM_B2 ** ADAM_STEP)
    delta = -ADAM_LR * (m_hat / (jnp.sqrt(v_hat) + ADAM_EPS) + ADAM_WD * w)
    return delta, m, v


def _sum_adamw(p, r, chip, w, m, v, *, segs, ta, name):
    Aw, Bw = w.shape
    Bg = p.shape[2]
    assert Aw % ta == 0

    def body(chip_ref, p_ref, r0, r1, r2, w_ref, m_ref, v_ref, g_out, d_out, m_out, v_out):
        for gc, wc, n in segs:
            g = ((p_ref[:, gc:gc + n] + r0[:, gc:gc + n]) + r1[:, gc:gc + n]) + r2[:, gc:gc + n]
            delta, m_new, v_new = _adamw_math(w_ref[:, wc:wc + n], g, m_ref[:, wc:wc + n], v_ref[:, wc:wc + n])
            g_out[:, wc:wc + n] = g
            d_out[:, wc:wc + n] = delta
            m_out[:, wc:wc + n] = m_new
            v_out[:, wc:wc + n] = v_new

    gblk = (None, ta, Bg)
    row = pl.BlockSpec((ta, Bw), lambda i, chip_ref: (i, 0))
    rspecs = [pl.BlockSpec(gblk, (lambda i, chip_ref, k=k: (k, i, 0))) for k in range(3)]
    shp = jax.ShapeDtypeStruct((Aw, Bw), F32)
    return pl.pallas_call(
        body, out_shape=(shp, shp, shp, shp),
        grid_spec=pltpu.PrefetchScalarGridSpec(
            num_scalar_prefetch=1, grid=(Aw // ta,),
            in_specs=[pl.BlockSpec(gblk, lambda i, chip_ref: (chip_ref[0], i, 0))] + rspecs + [row, row, row],
            out_specs=(row, row, row, row)),
        compiler_params=_params(("parallel",)), name=name)(chip, p, r, r, r, w, m, v)


def _adamw(w, g, m, v, *, name):
    def body(w_ref, g_ref, m_ref, v_ref, d_out, m_out, v_out):
        delta, m_new, v_new = _adamw_math(w_ref[...], g_ref[...], m_ref[...], v_ref[...])
        d_out[...] = delta
        m_out[...] = m_new
        v_out[...] = v_new

    vm = pl.BlockSpec(memory_space=pltpu.VMEM)
    shp = jax.ShapeDtypeStruct(w.shape, F32)
    return pl.pallas_call(body, out_shape=(shp, shp, shp), in_specs=[vm] * 4, out_specs=(vm, vm, vm),
                          compiler_params=_params(), name=name)(w, g, m, v)


def _small_allreduce_adamw(s, w, m, v, *, name):
    R, W = s.shape

    def body(s_ref, w_ref, m_ref, v_ref, g_out, d_out, m_out, v_out, gath, send_sems, recv_sems):
        x, y, c = _coords()
        mine = 4 * x + 2 * y + c
        gath[mine] = s_ref[...]
        peers = [((1 - x) if k & 4 else x, (1 - y) if k & 2 else y, (1 - c) if k & 1 else c) for k in range(1, N_DEV)]
        sends = []
        for k in range(1, N_DEV):
            peer = peers[k - 1]
            sends.append(pltpu.make_async_remote_copy(
                src_ref=s_ref, dst_ref=gath.at[mine], send_sem=send_sems.at[k - 1], recv_sem=recv_sems.at[k - 1],
                device_id=peer, device_id_type=MESH))
        for cp in sends:
            cp.start()
        for k in range(1, N_DEV):
            peer = peers[k - 1]
            pltpu.make_async_remote_copy(
                src_ref=s_ref, dst_ref=gath.at[4 * peer[0] + 2 * peer[1] + peer[2]],
                send_sem=send_sems.at[k - 1], recv_sem=recv_sems.at[k - 1],
                device_id=peer, device_id_type=MESH).wait_recv()
        for cp in sends:
            cp.wait_send()
        g = gath[0]
        for d in range(1, N_DEV):
            g = g + gath[d]
        delta, m_new, v_new = _adamw_math(w_ref[...], g, m_ref[...], v_ref[...])
        g_out[...] = g
        d_out[...] = delta
        m_out[...] = m_new
        v_out[...] = v_new

    vm = pl.BlockSpec(memory_space=pltpu.VMEM)
    shp = jax.ShapeDtypeStruct((R, W), F32)
    return pl.pallas_call(
        body, out_shape=(shp, shp, shp, shp), in_specs=[vm] * 4, out_specs=(vm, vm, vm, vm),
        scratch_shapes=[pltpu.VMEM((N_DEV, R, W), F32), pltpu.SemaphoreType.DMA((N_DEV - 1,)),
                        pltpu.SemaphoreType.DMA((N_DEV - 1,))],
        compiler_params=_params(), name=name)(s, w, m, v)


def _pack_small(rel_bias, g1, g2, g3, g4, b_forget, sinks, extra=None, meta=None):
    misc = jnp.concatenate([rel_bias.reshape(-1), b_forget.reshape(-1), sinks.reshape(-1)])
    misc = jnp.concatenate([misc, jnp.zeros((D_MODEL - misc.shape[0],), F32)])[None]
    last = jnp.zeros((1, D_MODEL), F32) if extra is None else extra
    meta = jnp.zeros((N_META, D_MODEL), F32) if meta is None else meta
    return jnp.concatenate([g1, g2, g3, g4, misc, last, jnp.zeros((2, D_MODEL), F32), meta], axis=0)


def _unpack_small(p):
    nrb = N_BUCKETS * SWA_Q_HEADS
    misc = p[4]
    return dict(rel_bias=misc[:nrb].reshape(N_BUCKETS, SWA_Q_HEADS), ln_pre_mix=p[0:1], ln_post_mix=p[1:2],
                ln_pre_ffn=p[2:3], ln_post_ffn=p[3:4], b_forget=misc[nrb:nrb + 8].reshape(1, 8),
                sinks=misc[nrb + 8:nrb + 16].reshape(1, 8))


def _heads(a, n):
    return a.reshape(a.shape[0], n, HEAD_DIM).transpose(1, 0, 2)


def _unheads(a):
    return a.transpose(1, 0, 2).reshape(a.shape[1], -1)


def kernel(x, meta_tokens, rel_bias, ln_pre_mix, ln_post_mix, ln_pre_ffn, ln_post_ffn, w_in, b_forget, sinks, w_out, w_gate_up, w_down, loss_target, m_meta_tokens, m_rel_bias, m_ln_pre_mix, m_ln_post_mix, m_ln_pre_ffn, m_ln_post_ffn, m_w_in, m_b_forget, m_sinks, m_w_out, m_w_gate_up, m_w_down, v_meta_tokens, v_rel_bias, v_ln_pre_mix, v_ln_post_mix, v_ln_pre_ffn, v_ln_post_ffn, v_w_in, v_b_forget, v_sinks, v_w_out, v_w_gate_up, v_w_down):
    seq = x.shape[1]
    T = BLOCK + seq
    assert T % FOX_TILE == 0
    nq = T // FOX_TILE
    tm = _tile(T, 1056)
    cin = w_in.shape[2]
    hid = w_down.shape[1]
    assert w_gate_up.shape[2] == 2 * hid and cin <= W_IN_PAD and hid <= HID_PAD

    x_i, y_i, c_i = _coords()
    core = jnp.reshape(c_i, (1,)).astype(jnp.int32)
    chip = jnp.reshape(2 * x_i + y_i, (1,)).astype(jnp.int32)
    w_in_s = jnp.pad(w_in[0].astype(BF16), ((0, 0), (0, W_IN_PAD - cin)))
    w_gu_s = jnp.pad(w_gate_up[0].astype(BF16).reshape(D_MODEL, 2, hid), ((0, 0), (0, 0), (0, HID_PAD - hid)))
    w_gu_s = w_gu_s.reshape(D_MODEL, 2 * HID_PAD)
    w_down_s = jnp.pad(w_down[0].astype(BF16), ((0, HID_PAD - hid), (0, 0)))
    g_in, g_out, g_gu, g_down, g_meta = _all_gather(
        [w_in_s, w_out[0].astype(BF16), w_gu_s, w_down_s, meta_tokens], name="ag_weights")
    w_in_full = g_in[:, :, :cin].transpose(1, 0, 2).reshape(D_MODEL, N_DEV * cin)
    w_qkv = w_in_full[:, :D_QKV]
    w_f = jnp.pad(w_in_full[:, D_QKV:], ((0, 0), (0, BLOCK - FOX_HEADS)))
    w_in_cat = jnp.concatenate([w_qkv, w_f, jnp.zeros((D_MODEL, D_PROJ_PAD - D_QKV - BLOCK), BF16)], axis=1)
    w_out_full = g_out.reshape(D_MODEL, D_MODEL)
    w_down_full = g_down.reshape(N_DEV * HID_PAD, D_MODEL)
    meta_full = g_meta.transpose(1, 0, 2).reshape(N_META, D_MODEL)

    h0 = jnp.concatenate([jnp.zeros((PAD_ROWS, D_MODEL), F32), meta_full, x[0]], axis=0)
    target = jnp.concatenate([jnp.zeros((BLOCK, D_MODEL), F32), loss_target[0]], axis=0)
    hn1 = _rms_fwd(h0, ln_pre_mix, name="rms_pre_mix")
    proj = _matmul(hn1, w_qkv, out_dtype=BF16, tm=tm, tn=768, name="mm_in_proj")
    proj_f = _matmul(hn1, w_f, out_dtype=F32, tm=tm, tn=BLOCK, name="mm_in_proj_f")

    q_a = _heads(proj[:, 0:512], 8)
    k_a = jnp.pad(_heads(proj[:, 512:640], 2), ((0, 0), (BLOCK, 0), (0, 0)))
    v_a = jnp.pad(_heads(proj[:, 640:768], 2), ((0, 0), (BLOCK, 0), (0, 0)))
    f_t = proj_f[:, :FOX_HEADS].T
    bf_col = b_forget.reshape(FOX_HEADS, 1)

    oh_cur, oh_prev = _bucket_onehots()
    bias_c = jnp.einsum("pb,bh->hp", jnp.asarray(oh_cur), rel_bias, precision=HIGHEST).reshape(8, BLOCK, BLOCK)
    bias_p = jnp.einsum("pb,bh->hp", jnp.asarray(oh_prev), rel_bias, precision=HIGHEST).reshape(8, BLOCK, BLOCK)
    far = rel_bias[N_BUCKETS - 1]
    sink_v = sinks[0]
    o_a = _swa_fwd(q_a, k_a, v_a, bias_c, bias_p, far, sink_v, name="swa_fwd")

    _, cum_col = _fox_gates_fwd(f_t, bf_col, name="fox_gates_fwd")
    q_b, k_b, v_b = _fox_prep(proj, cum_col, name="fox_prep")
    o_b, lse_row = _fox_fwd(q_b, k_b, v_b, name="fox_fwd")

    mix = jnp.concatenate([_unheads(o_a), _unheads(o_b)], axis=1)
    a1 = _matmul(mix, w_out_full, out_dtype=F32, tm=tm, tn=512, name="mm_out_proj")
    h1 = _post_res(a1, ln_post_mix, h0, name="post_mix")
    hn2 = _rms_fwd(h1, ln_pre_ffn, name="rms_pre_ffn")
    gu = _matmul(hn2, g_gu, b_shards=True, out_dtype=BF16, tm=tm, name="mm_gate_up")
    act = _swiglu_fwd(gu, name="swiglu_fwd")
    ff = _matmul(act, w_down_full, out_dtype=F32, tm=tm, tn=512, name="mm_down")
    dh2, loss_acc = _loss_head(ff, ln_post_ffn, h1, target, name="loss_head")

    dff, dg_post_ffn = _rms_bwd(ff, ln_post_ffn, dh2, None, out_dtype=BF16, name="rms_bwd_post_ffn")
    dact = _matmul(dff, w_down_full, nt=True, out_dtype=BF16, tm=tm, tn=1536, name="mm_d_act")
    d_w_down = _matmul(act.T, dff, out_dtype=F32, tm=768, tn=512, name="mm_dw_down")
    dgu = _swiglu_bwd(gu, dact, name="swiglu_bwd")
    dhn2 = _matmul(dgu, g_gu, nt=True, b_shards=True, out_dtype=F32, tm=tm, tn=512, name="mm_d_hn2")
    d_w_gu = _matmul(hn2.T, dgu, out_shards=True, out_dtype=F32, tm=512, tn=2 * HID_PAD, name="mm_dw_gate_up")
    dh1, dg_pre_ffn = _rms_bwd(h1, ln_pre_ffn, dhn2, dh2, out_dtype=F32, name="rms_bwd_pre_ffn")
    da1, dg_post_mix = _rms_bwd(a1, ln_post_mix, dh1, None, out_dtype=BF16, name="rms_bwd_post_mix")
    dmix = _matmul(da1, w_out_full, nt=True, out_dtype=BF16, tm=tm, tn=512, name="mm_d_mix")
    d_w_out = _matmul(mix.T, da1, out_dtype=F32, tm=512, tn=512, name="mm_dw_out")

    do_a = _heads(dmix[:, :512], 8)
    dq_a, dk_a, dv_a, dbc, dbp, dbf, dsk = _swa_bwd(q_a, k_a, v_a, do_a, bias_c, bias_p, far, sink_v, name="swa_bwd")
    d_tab, d_sink = _small_grads(dbc, dbp, dbf, dsk, jnp.asarray(oh_cur), jnp.asarray(oh_prev), name="small_grads")

    do_b = _fox_prep_bwd(dmix, o_b, name="fox_prep_bwd")
    dq_t, dk_b, dv_b, dck = _fox_bwd(q_b, k_b, v_b, do_b, lse_row, name="fox_bwd")
    dcq = dq_t[:, :, LANE_QC, :].reshape(FOX_HEADS, T)
    df_t, d_bf = _fox_gates_bwd(dcq, dck.reshape(FOX_HEADS, T), f_t, bf_col, name="fox_gates_bwd")
    dq_b = (dq_t[:, :, :HEAD_DIM, :].transpose(1, 3, 0, 2).reshape(T, FOX_W) * SCALE).astype(BF16)
    dk_b = dk_b[:, :, :HEAD_DIM].transpose(1, 0, 2).reshape(T, FOX_W)
    dv_b = dv_b[:, :, :HEAD_DIM].transpose(1, 0, 2).reshape(T, FOX_W)

    dproj = jnp.concatenate([
        _unheads(dq_a), _unheads(dk_a[:, BLOCK:]).astype(BF16), _unheads(dv_a[:, BLOCK:]).astype(BF16),
        dq_b, dk_b, dv_b, df_t.T.astype(BF16), jnp.zeros((T, D_PROJ_PAD - D_PROJ), BF16)], axis=1)
    dhn1 = _matmul(dproj, w_in_cat, nt=True, out_dtype=F32, tm=tm, tn=512, name="mm_d_hn1")
    dproj_s = jnp.pad(dproj[:, :N_DEV * cin].reshape(T, N_DEV, cin), ((0, 0), (0, 0), (0, W_IN_PAD - cin)))
    d_w_in = _matmul(hn1.T, dproj_s.reshape(T, N_DEV * W_IN_PAD), out_shards=True, out_dtype=F32, tm=512,
                     tn=W_IN_PAD, name="mm_dw_in")
    dh0, dg_pre_mix = _rms_bwd(h0, ln_pre_mix, dhn1, dh1, out_dtype=F32, name="rms_bwd_pre_mix")
    grad_x = dh0[BLOCK:][None]
    d_meta = dh0[PAD_ROWS:BLOCK]

    by_dev = [d_w_in, d_w_out.reshape(N_DEV, -1, D_MODEL), d_w_gu, d_w_down.reshape(N_DEV, HID_PAD, D_MODEL)]
    tags = ["w_in", "w_out", "w_gate_up", "w_down"]
    from_sibling = _exchange_cores(by_dev, name="rs_cores")
    chip_sum = [_add_cores(g, r, core, name="rs_add_" + t) for g, r, t in zip(by_dev, from_sibling, tags)]
    from_chips = _exchange_chips(chip_sum, name="rs_chips")
    shard_w = [(w_in, m_w_in, v_w_in), (w_out, m_w_out, v_w_out), (w_gate_up, m_w_gate_up, v_w_gate_up),
               (w_down, m_w_down, v_w_down)]
    segs = [[(0, 0, cin)], [(0, 0, D_MODEL)], [(0, 0, hid), (HID_PAD, hid, hid)], [(0, 0, D_MODEL)]]
    tas = [256, BLOCK, 256, hid]
    big = [{}, {}, {}, {}]
    for i, t in enumerate(tags):
        w_t, m_t, v_t = shard_w[i]
        res = _sum_adamw(chip_sum[i], from_chips[i], chip, w_t[0], m_t[0], v_t[0], segs=segs[i], ta=tas[i],
                         name="rs_adamw_" + t)
        for kind in range(4):
            big[kind][t] = res[kind][None]

    loss_row = jnp.pad(loss_acc[0:1, 0:1] * (0.5 / D_MODEL), ((0, 0), (0, D_MODEL - 1)))
    s_small = _pack_small(d_tab.T, dg_pre_mix, dg_post_mix, dg_pre_ffn, dg_post_ffn, d_bf, d_sink,
                          extra=loss_row, meta=d_meta)
    w_s = _pack_small(rel_bias, ln_pre_mix, ln_post_mix, ln_pre_ffn, ln_post_ffn, b_forget, sinks)
    m_s = _pack_small(m_rel_bias, m_ln_pre_mix, m_ln_post_mix, m_ln_pre_ffn, m_ln_post_ffn, m_b_forget, m_sinks)
    v_s = _pack_small(v_rel_bias, v_ln_pre_mix, v_ln_post_mix, v_ln_pre_ffn, v_ln_post_ffn, v_b_forget, v_sinks)
    small = _small_allreduce_adamw(s_small, w_s, m_s, v_s, name="small_allreduce_adamw")
    loss = small[0][5, 0]
    mcols = meta_tokens.shape[1]
    g_meta_mine = lax.dynamic_slice(small[0][8:8 + N_META], (0, (4 * x_i + 2 * y_i + c_i) * mcols), (N_META, mcols))
    big[0]["meta_tokens"] = g_meta_mine
    for kind, arr in enumerate(_adamw(meta_tokens, g_meta_mine, m_meta_tokens, v_meta_tokens, name="adamw_meta")):
        big[kind + 1]["meta_tokens"] = arr
    small = [_unpack_small(p) for p in small]

    names = ["meta_tokens", "rel_bias", "ln_pre_mix", "ln_post_mix", "ln_pre_ffn", "ln_post_ffn", "w_in",
             "b_forget", "sinks", "w_out", "w_gate_up", "w_down"]
    outs = [loss, grad_x]
    for kind in range(4):
        for nme in names:
            outs.append(big[kind][nme] if nme in big[kind] else small[kind][nme])
    return tuple(outs)
```

```python
import math

import numpy as np
import jax
import jax.numpy as jnp
from jax import lax
from jax.experimental import pallas as pl
from jax.experimental.pallas import tpu as pltpu

F32 = jnp.float32
BF16 = jnp.bfloat16
HIGHEST = lax.Precision.HIGHEST
MESH = pl.DeviceIdType.MESH

N_DEV = 8
D_MODEL = 1024
N_META = 16
HEAD_DIM = 64
SWA_Q_HEADS = 8
SWA_KV_HEADS = 2
SWA_GROUP = 4
FOX_HEADS = 8
FOX_W = FOX_HEADS * HEAD_DIM
BLOCK = 128
PAD_ROWS = BLOCK - N_META
N_BUCKETS = 32
MAX_DISTANCE = 128
D_FF = 2816
D_QKV = 2304
D_PROJ = D_QKV + FOX_HEADS
D_PROJ_PAD = 2560
EPS = 1e-6
NEG = -1e30
SCALE = HEAD_DIM ** -0.5
ADAM_LR, ADAM_B1, ADAM_B2, ADAM_EPS, ADAM_WD, ADAM_STEP = 0.001, 0.9, 0.999, 1e-08, 0.01, 10
VMEM_LIMIT = 48 * 1024 * 1024
FOX_TILE = 384
FOX_GROUP = 4
W_IN_PAD = 384
HID_PAD = 384

NT = (((1,), (1,)), ((), ()))
NN = (((1,), (0,)), ((), ()))
TN = (((0,), (0,)), ((), ()))


def _params(sem=None, **kw):
    if sem is not None:
        kw["dimension_semantics"] = sem
    return pltpu.CompilerParams(vmem_limit_bytes=VMEM_LIMIT, **kw)


def _tile(n, target, mult=16):
    best = None
    for t in range(mult, min(n, target) + 1, mult):
        if n % t == 0:
            best = t
    assert best is not None, (n, target)
    return best


def _matmul(a, b, *, nt=False, b_shards=False, out_shards=False, out_dtype, tm, tn=None, tk=None, name):
    M, K = a.shape
    if b_shards and nt:
        N, tk = b.shape[1], b.shape[2]
    elif b_shards:
        N, tn = b.shape[0] * b.shape[2], b.shape[2]
    else:
        N = b.shape[0] if nt else b.shape[1]
    tk = K if tk is None else tk
    assert M % tm == 0 and N % tn == 0 and K % tk == 0, (name, a.shape, b.shape, tm, tn, tk)
    nk = K // tk
    dn = NT if nt else NN

    def body(a_ref, b_ref, o_ref, *scr):
        part = lax.dot_general(a_ref[...], b_ref[...], dn, preferred_element_type=F32)
        if nk == 1:
            o_ref[...] = part.astype(o_ref.dtype)
        else:
            acc = scr[0]
            k = pl.program_id(2)

            @pl.when(k == 0)
            def _():
                acc[...] = part

            @pl.when(k > 0)
            def _():
                acc[...] += part

            @pl.when(k == nk - 1)
            def _():
                o_ref[...] = acc[...].astype(o_ref.dtype)

    if b_shards and nt:
        b_spec = pl.BlockSpec((None, tn, tk), lambda i, j, k: (k, j, 0))
    elif b_shards:
        b_spec = pl.BlockSpec((None, tk, tn), lambda i, j, k: (j, k, 0))
    elif nt:
        b_spec = pl.BlockSpec((tn, tk), lambda i, j, k: (j, k))
    else:
        b_spec = pl.BlockSpec((tk, tn), lambda i, j, k: (k, j))
    if out_shards:
        out_shape = jax.ShapeDtypeStruct((N // tn, M, tn), out_dtype)
        out_spec = pl.BlockSpec((None, tm, tn), lambda i, j, k: (j, i, 0))
    else:
        out_shape = jax.ShapeDtypeStruct((M, N), out_dtype)
        out_spec = pl.BlockSpec((tm, tn), lambda i, j, k: (i, j))
    return pl.pallas_call(
        body,
        out_shape=out_shape,
        grid=(M // tm, N // tn, nk),
        in_specs=[pl.BlockSpec((tm, tk), lambda i, j, k: (i, k)), b_spec],
        out_specs=out_spec,
        scratch_shapes=[pltpu.VMEM((tm, tn), F32)] if nk > 1 else [],
        compiler_params=_params(("parallel", "parallel", "arbitrary")),
        name=name,
    )(a, b)


def _rstd(x):
    return lax.rsqrt(jnp.mean(x * x, axis=-1, keepdims=True) + EPS)


def _rms_fwd(x, g, *, name):
    T, D = x.shape
    tm = _tile(T, 512)

    def body(x_ref, g_ref, o_ref):
        x = x_ref[...]
        o_ref[...] = (x * _rstd(x) * g_ref[...]).astype(o_ref.dtype)

    return pl.pallas_call(
        body, out_shape=jax.ShapeDtypeStruct((T, D), BF16), grid=(T // tm,),
        in_specs=[pl.BlockSpec((tm, D), lambda i: (i, 0)), pl.BlockSpec((1, D), lambda i: (0, 0))],
        out_specs=pl.BlockSpec((tm, D), lambda i: (i, 0)),
        compiler_params=_params(("parallel",)), name=name)(x, g)


def _post_res(a, g, h, *, name):
    T, D = a.shape
    tm = _tile(T, 512)

    def body(a_ref, g_ref, h_ref, o_ref):
        a = a_ref[...]
        o_ref[...] = h_ref[...] + a * _rstd(a) * g_ref[...]

    row = pl.BlockSpec((tm, D), lambda i: (i, 0))
    return pl.pallas_call(
        body, out_shape=jax.ShapeDtypeStruct((T, D), F32), grid=(T // tm,),
        in_specs=[row, pl.BlockSpec((1, D), lambda i: (0, 0)), row], out_specs=row,
        compiler_params=_params(("parallel",)), name=name)(a, g, h)


def _loss_head(a, g, h, target, *, name):
    T, D = a.shape
    tm = _tile(T, 512)

    def body(a_ref, g_ref, h_ref, t_ref, dy_ref, loss_ref):
        i = pl.program_id(0)
        a = a_ref[...]
        y = h_ref[...] + a * _rstd(a) * g_ref[...]
        rows = i * tm + lax.broadcasted_iota(jnp.int32, (tm, 1), 0)
        err = jnp.where(rows >= BLOCK, y - t_ref[...], 0.0)
        dy_ref[...] = err / D
        part = jnp.sum(jnp.sum(err * err, axis=1, keepdims=True), axis=0, keepdims=True)

        @pl.when(i == 0)
        def _():
            loss_ref[...] = jnp.zeros_like(loss_ref)

        loss_ref[...] += jnp.broadcast_to(part, loss_ref.shape)

    row = pl.BlockSpec((tm, D), lambda i: (i, 0))
    return pl.pallas_call(
        body, out_shape=(jax.ShapeDtypeStruct((T, D), F32), jax.ShapeDtypeStruct((8, 128), F32)),
        grid=(T // tm,),
        in_specs=[row, pl.BlockSpec((1, D), lambda i: (0, 0)), row, row],
        out_specs=(row, pl.BlockSpec((8, 128), lambda i: (0, 0))),
        compiler_params=_params(("arbitrary",)), name=name)(a, g, h, target)


def _rms_bwd(x, g, dy, res, *, out_dtype, name):
    T, D = x.shape
    tm = _tile(T, 512)
    has_res = res is not None

    def body(*refs):
        if has_res:
            x_ref, g_ref, dy_ref, r_ref, dx_ref, dg_ref = refs
        else:
            x_ref, g_ref, dy_ref, dx_ref, dg_ref = refs
        i = pl.program_id(0)
        x = x_ref[...]
        dy = dy_ref[...].astype(F32)
        r = _rstd(x)
        xh = x * r
        dxh = dy * g_ref[...]
        dx = r * (dxh - xh * jnp.mean(dxh * xh, axis=-1, keepdims=True))
        if has_res:
            dx = dx + r_ref[...]
        dx_ref[...] = dx.astype(dx_ref.dtype)

        @pl.when(i == 0)
        def _():
            dg_ref[...] = jnp.zeros_like(dg_ref)

        dg_ref[...] += jnp.sum(dy * xh, axis=0, keepdims=True)

    row = pl.BlockSpec((tm, D), lambda i: (i, 0))
    vec = pl.BlockSpec((1, D), lambda i: (0, 0))
    ins = [x, g, dy] + ([res] if has_res else [])
    return pl.pallas_call(
        body, out_shape=(jax.ShapeDtypeStruct((T, D), out_dtype), jax.ShapeDtypeStruct((1, D), F32)),
        grid=(T // tm,),
        in_specs=[row, vec, row] + ([row] if has_res else []),
        out_specs=(row, vec),
        compiler_params=_params(("arbitrary",)), name=name)(*ins)


def _swiglu_fwd(gu, *, name):
    T, F2 = gu.shape
    F = F2 // 2
    tm = _tile(T, 384)

    def body(g_ref, u_ref, o_ref):
        g = g_ref[...].astype(F32)
        o_ref[...] = (g / (1.0 + jnp.exp(-g)) * u_ref[...].astype(F32)).astype(o_ref.dtype)

    return pl.pallas_call(
        body, out_shape=jax.ShapeDtypeStruct((T, F), BF16), grid=(T // tm,),
        in_specs=[pl.BlockSpec((tm, F), lambda i: (i, 0)), pl.BlockSpec((tm, F), lambda i: (i, 1))],
        out_specs=pl.BlockSpec((tm, F), lambda i: (i, 0)),
        compiler_params=_params(("parallel",)), name=name)(gu, gu)


def _swiglu_bwd(gu, dact, *, name):
    T, F2 = gu.shape
    F = F2 // 2
    tm = _tile(T, 384)

    def body(g_ref, u_ref, d_ref, o_ref):
        g = g_ref[...].astype(F32)
        u = u_ref[...].astype(F32)
        d = d_ref[...].astype(F32)
        sg = 1.0 / (1.0 + jnp.exp(-g))
        o_ref[:, :F] = (d * u * (sg * (1.0 + g * (1.0 - sg)))).astype(o_ref.dtype)
        o_ref[:, F:] = (d * (g * sg)).astype(o_ref.dtype)

    return pl.pallas_call(
        body, out_shape=jax.ShapeDtypeStruct((T, F2), BF16), grid=(T // tm,),
        in_specs=[pl.BlockSpec((tm, F), lambda i: (i, 0)), pl.BlockSpec((tm, F), lambda i: (i, 1)),
                  pl.BlockSpec((tm, F), lambda i: (i, 0))],
        out_specs=pl.BlockSpec((tm, F2), lambda i: (i, 0)),
        compiler_params=_params(("parallel",)), name=name)(gu, gu, dact)


def _fox_gates_fwd(f_t, b, *, name):
    H, T = f_t.shape
    nb = T // BLOCK

    def body(f_ref, b_ref, cum_ref, col_ref):
        f = f_ref[...] + b_ref[...]
        ls = jnp.minimum(f, 0.0) - jnp.log(1.0 + jnp.exp(-jnp.abs(f)))
        t = lax.broadcasted_iota(jnp.int32, (H, T), 1)
        ls = jnp.where(t >= PAD_ROWS, ls, 0.0)
        upper = (lax.broadcasted_iota(jnp.int32, (BLOCK, BLOCK), 0)
                 <= lax.broadcasted_iota(jnp.int32, (BLOCK, BLOCK), 1)).astype(F32)
        carry = jnp.zeros((H, 1), F32)
        for blk in range(nb):
            seg = ls[:, blk * BLOCK:(blk + 1) * BLOCK]
            pre = jnp.dot(seg, upper, precision=HIGHEST, preferred_element_type=F32) + carry
            cum_ref[:, blk * BLOCK:(blk + 1) * BLOCK] = pre
            col_ref[blk * BLOCK:(blk + 1) * BLOCK, :] = jnp.concatenate(
                [pre, jnp.zeros((BLOCK - H, BLOCK), F32)], axis=0).T
            carry = pre[:, BLOCK - 1:BLOCK]

    vm = pl.BlockSpec(memory_space=pltpu.VMEM)
    return pl.pallas_call(
        body, out_shape=(jax.ShapeDtypeStruct((H, T), F32), jax.ShapeDtypeStruct((T, BLOCK), F32)),
        in_specs=[vm, vm], out_specs=(vm, vm),
        compiler_params=_params(), name=name)(f_t, b)


def _fox_gates_bwd(dcq, dck, f_t, b, *, name):
    H, T = f_t.shape
    nb = T // BLOCK

    def body(dq_ref, d_ref, f_ref, b_ref, df_ref, db_ref):
        lower = (lax.broadcasted_iota(jnp.int32, (BLOCK, BLOCK), 0)
                 >= lax.broadcasted_iota(jnp.int32, (BLOCK, BLOCK), 1)).astype(F32)
        carry = jnp.zeros((H, 1), F32)
        for blk in range(nb - 1, -1, -1):
            seg = dq_ref[:, blk * BLOCK:(blk + 1) * BLOCK] - d_ref[:, blk * BLOCK:(blk + 1) * BLOCK]
            suf = jnp.dot(seg, lower, precision=HIGHEST, preferred_element_type=F32) + carry
            df_ref[:, blk * BLOCK:(blk + 1) * BLOCK] = suf
            carry = suf[:, 0:1]
        f = f_ref[...] + b_ref[...]
        t = lax.broadcasted_iota(jnp.int32, (H, T), 1)
        df = jnp.where(t >= PAD_ROWS, df_ref[...] / (1.0 + jnp.exp(f)), 0.0)
        df_ref[...] = df
        db_ref[...] = jnp.sum(df, axis=1, keepdims=True)

    vm = pl.BlockSpec(memory_space=pltpu.VMEM)
    return pl.pallas_call(
        body, out_shape=(jax.ShapeDtypeStruct((H, T), F32), jax.ShapeDtypeStruct((H, 1), F32)),
        in_specs=[vm, vm, vm, vm], out_specs=(vm, vm),
        compiler_params=_params(), name=name)(dcq, dck, f_t, b)


LANE_KC = HEAD_DIM
LANE_QC = HEAD_DIM + 3
LANE_END = HEAD_DIM + 6


def _split3(c):
    hi = c.astype(BF16).astype(F32)
    r = c - hi
    mid = r.astype(BF16).astype(F32)
    lo = (r - mid).astype(BF16).astype(F32)
    return hi, mid, lo


def _lanes(lane, data, start, terms, rest):
    out = rest
    for i, t in enumerate(terms):
        out = jnp.where(lane == start + i, t, out)
    return jnp.where(lane < HEAD_DIM, data, out)


def _fox_prep(proj, cum_col, *, name):
    T = proj.shape[0]
    tm = FOX_TILE
    nt = T // tm
    H = FOX_HEADS
    lanes = 2 * HEAD_DIM
    qb, kb, vb = 768 // lanes, 1280 // lanes, 1792 // lanes

    def body(q_ref, k_ref, v_ref, c_ref, qa_ref, ka_ref, va_ref):
        p = pl.program_id(0)
        i = pl.program_id(1)
        lane = lax.broadcasted_iota(jnp.int32, (tm, lanes), 1)
        rows = i * tm + lax.broadcasted_iota(jnp.int32, (tm, 1), 0)
        q2 = q_ref[...].astype(F32)
        k2 = k_ref[...].astype(F32)
        v2 = v_ref[...].astype(F32)
        cum = c_ref[...]
        for e in range(2):
            c = jnp.sum(jnp.where(lane == 2 * p + e, cum, 0.0), axis=1, keepdims=True)
            ck = jnp.where(rows >= PAD_ROWS, c, -NEG)
            qe, ke, ve = (q2, k2, v2) if e == 0 else tuple(pltpu.roll(a, HEAD_DIM, 1) for a in (q2, k2, v2))
            one = jnp.where(lane < LANE_END, 1.0, 0.0)
            qa = _lanes(lane, qe * SCALE, LANE_QC, _split3(c), jnp.where(lane < LANE_QC, -1.0, 0.0))
            ka = _lanes(lane, ke, LANE_KC, _split3(ck), one)
            va = jnp.where(lane < HEAD_DIM, ve, jnp.where(lane < LANE_QC, 1.0, 0.0))
            qa_ref[e] = qa.astype(BF16)
            ka_ref[e] = ka.astype(BF16)
            va_ref[e] = va.astype(BF16)

    def col(b):
        return pl.BlockSpec((tm, lanes), lambda p, i, b=b: (i, b + p))

    out = pl.BlockSpec((2, tm, lanes), lambda p, i: (p, i, 0))
    shp = jax.ShapeDtypeStruct((H, T, lanes), BF16)
    return pl.pallas_call(
        body, out_shape=(shp, shp, shp), grid=(H // 2, nt),
        in_specs=[col(qb), col(kb), col(vb), pl.BlockSpec((tm, lanes), lambda p, i: (i, 0))],
        out_specs=(out, out, out),
        compiler_params=_params(("parallel", "parallel")), name=name)(proj, proj, proj, cum_col)


def _fox_fwd(q_aug, k_aug, v_aug, *, name):
    H, T, lanes = q_aug.shape
    tq = FOX_TILE
    nq = T // tq
    G = FOX_GROUP

    def body(q_ref, k_ref, v_ref, o_ref, lse_ref, m_scr, acc_scr):
        i = pl.program_id(1)
        m_scr[...] = jnp.full(m_scr.shape, NEG, F32)
        acc_scr[...] = jnp.zeros(acc_scr.shape, F32)

        def step(g, kb, diag):
            off = pl.multiple_of(kb * tq, tq)
            s_t = lax.dot_general(k_ref[g, pl.ds(off, tq), :], q_ref[g], NT, preferred_element_type=F32)
            if diag:
                r = lax.broadcasted_iota(jnp.int32, (tq, tq), 0)
                c = lax.broadcasted_iota(jnp.int32, (tq, tq), 1)
                s_t = jnp.where(c >= r, s_t, NEG)
            m_prev = m_scr[g]
            m_new = jnp.maximum(m_prev, jnp.max(s_t, axis=0, keepdims=True))
            p_t = jnp.exp(s_t - m_new).astype(BF16)
            alpha = jnp.exp(m_prev - m_new)
            acc_scr[g] = alpha * acc_scr[g] + lax.dot_general(
                v_ref[g, pl.ds(off, tq), :], p_t, TN, preferred_element_type=F32)
            m_scr[g] = m_new

        def loop_body(kb, carry):
            for g in range(G):
                step(g, kb, False)
            return carry

        lax.fori_loop(0, i, loop_body, 0)
        for g in range(G):
            step(g, i, True)
            acc = acc_scr[g]
            lse_ref[g] = m_scr[g] + jnp.log(acc[HEAD_DIM:HEAD_DIM + 1, :])
            acc_t = acc.T
            o_ref[g] = (acc_t[:, :HEAD_DIM] / acc_t[:, HEAD_DIM:HEAD_DIM + 1]).astype(o_ref.dtype)

    blk = pl.BlockSpec((G, tq, lanes), lambda h, i: (h, i, 0))
    full = pl.BlockSpec((G, T, lanes), lambda h, i: (h, 0, 0))
    return pl.pallas_call(
        body,
        out_shape=(jax.ShapeDtypeStruct((H, T, HEAD_DIM), BF16), jax.ShapeDtypeStruct((H, nq, 1, tq), F32)),
        grid=(H // G, nq),
        in_specs=[blk, full, full],
        out_specs=(pl.BlockSpec((G, tq, HEAD_DIM), lambda h, i: (h, i, 0)),
                   pl.BlockSpec((G, None, 1, tq), lambda h, i: (h, i, 0, 0))),
        scratch_shapes=[pltpu.VMEM((G, 1, tq), F32), pltpu.VMEM((G, lanes, tq), F32)],
        compiler_params=_params(("parallel", "arbitrary")), name=name)(q_aug, k_aug, v_aug)


def _fox_prep_bwd(dmix, o, *, name):
    T = dmix.shape[0]
    H = o.shape[0]
    tm = FOX_TILE
    lanes = 2 * HEAD_DIM
    first = 512 // lanes

    def body(d_ref, o_ref, da_ref):
        lane = lax.broadcasted_iota(jnp.int32, (tm, lanes), 1)
        d2 = d_ref[...].astype(F32)
        for e in range(2):
            de = d2 if e == 0 else pltpu.roll(d2, HEAD_DIM, 1)
            d64 = d_ref[:, e * HEAD_DIM:(e + 1) * HEAD_DIM].astype(F32)
            delta = jnp.sum(d64 * o_ref[e].astype(F32), axis=1, keepdims=True)
            da_ref[e] = _lanes(lane, de, LANE_KC, _split3(-delta), jnp.zeros((), F32)).astype(BF16)

    return pl.pallas_call(
        body, out_shape=jax.ShapeDtypeStruct((H, T, lanes), BF16), grid=(H // 2, T // tm),
        in_specs=[pl.BlockSpec((tm, lanes), lambda p, i: (i, first + p)),
                  pl.BlockSpec((2, tm, HEAD_DIM), lambda p, i: (p, i, 0))],
        out_specs=pl.BlockSpec((2, tm, lanes), lambda p, i: (p, i, 0)),
        compiler_params=_params(("parallel", "parallel")), name=name)(dmix, o)


def _fox_bwd(q_aug, k_aug, v_aug, do_aug, lse_row, *, name):
    H, T, lanes = q_aug.shape
    tq = FOX_TILE
    nq = T // tq
    G = FOX_GROUP

    def body(q_ref, k_ref, v_ref, do_ref, lse_ref, dq_ref, dk_ref, dv_ref, dck_ref, dk_acc, dv_acc):
        j = pl.program_id(1)

        @pl.when(j == 0)
        def _():
            dq_ref[...] = jnp.zeros(dq_ref.shape, F32)

        dk_acc[...] = jnp.zeros(dk_acc.shape, F32)
        dv_acc[...] = jnp.zeros(dv_acc.shape, F32)

        def step(g, qb, diag):
            off = pl.multiple_of(qb * tq, tq)
            ka = k_ref[g]
            qa = q_ref[g, pl.ds(off, tq), :]
            da = do_ref[g, pl.ds(off, tq), :]
            s_t = lax.dot_general(ka, qa, NT, preferred_element_type=F32)
            p_t = jnp.exp(s_t - lse_ref[g, qb])
            if diag:
                r = lax.broadcasted_iota(jnp.int32, (tq, tq), 0)
                c = lax.broadcasted_iota(jnp.int32, (tq, tq), 1)
                p_t = jnp.where(c >= r, p_t, 0.0)
            dv_acc[g] += jnp.dot(p_t.astype(BF16), da, preferred_element_type=F32)
            dp_t = lax.dot_general(v_ref[g], da, NT, preferred_element_type=F32)
            dsb = (p_t * dp_t).astype(BF16)
            dk_acc[g] += jnp.dot(dsb, qa, preferred_element_type=F32)
            dq_ref[g, qb] += lax.dot_general(ka, dsb, TN, preferred_element_type=F32)

        for g in range(G):
            step(g, j, True)

        def loop_body(qb, carry):
            for g in range(G):
                step(g, qb, False)
            return carry

        lax.fori_loop(j + 1, nq, loop_body, 0)
        dk = dk_acc[...]
        dk_ref[...] = dk.astype(dk_ref.dtype)
        dck_ref[...] = -dk[:, :, LANE_KC:LANE_KC + 1]
        dv_ref[...] = dv_acc[...].astype(dv_ref.dtype)

    blk = pl.BlockSpec((G, tq, lanes), lambda h, j: (h, j, 0))
    full = pl.BlockSpec((G, T, lanes), lambda h, j: (h, 0, 0))
    return pl.pallas_call(
        body,
        out_shape=(jax.ShapeDtypeStruct((H, nq, lanes, tq), F32), jax.ShapeDtypeStruct((H, T, lanes), BF16),
                   jax.ShapeDtypeStruct((H, T, lanes), BF16), jax.ShapeDtypeStruct((H, T, 1), F32)),
        grid=(H // G, nq),
        in_specs=[full, blk, blk, full, pl.BlockSpec((G, nq, 1, tq), lambda h, j: (h, 0, 0, 0))],
        out_specs=(pl.BlockSpec((G, nq, lanes, tq), lambda h, j: (h, 0, 0, 0)), blk, blk,
                   pl.BlockSpec((G, tq, 1), lambda h, j: (h, j, 0))),
        scratch_shapes=[pltpu.VMEM((G, tq, lanes), F32), pltpu.VMEM((G, tq, lanes), F32)],
        compiler_params=_params(("parallel", "arbitrary")), name=name,
    )(q_aug, k_aug, v_aug, do_aug, lse_row)


def _t5_bucket_np(d):
    n = np.maximum(d, 0).astype(np.int32)
    max_exact = N_BUCKETS // 2
    nf = np.maximum(n, 1).astype(np.float32)
    large = max_exact + (np.log(nf / max_exact) / math.log(MAX_DISTANCE / max_exact)
                         * (N_BUCKETS - max_exact)).astype(np.int32)
    large = np.minimum(large, N_BUCKETS - 1)
    return np.where(n < max_exact, n, large)


def _bucket_onehots():
    r = np.arange(BLOCK)[:, None]
    c = np.arange(BLOCK)[None, :]
    eye = np.eye(N_BUCKETS, dtype=np.float32)
    cur = eye[_t5_bucket_np(r - c).reshape(-1)]
    prev = eye[_t5_bucket_np(BLOCK + r - c).reshape(-1)]
    return cur, prev


def _swa_probs(qs, kc, kp, km, bc, bp, far, sink, n):
    r = lax.broadcasted_iota(jnp.int32, (BLOCK, BLOCK), 0)
    c = lax.broadcasted_iota(jnp.int32, (BLOCK, BLOCK), 1)
    never = 2 * BLOCK
    s_c = lax.dot_general(qs, kc, NT, preferred_element_type=F32) + bc
    s_p = lax.dot_general(qs, kp, NT, preferred_element_type=F32) + bp
    s_m = lax.dot_general(qs, km, NT, preferred_element_type=F32) + jnp.where(n == 1, bp, far)
    s_c = jnp.where((c <= r) & (c >= jnp.where(n >= 1, 0, PAD_ROWS)), s_c, NEG)
    s_p = jnp.where(c > r + jnp.where(n >= 2, 0, never), s_p, NEG)
    s_m = jnp.where(c >= jnp.where(n >= 1, PAD_ROWS, never), s_m, NEG)
    m = jnp.maximum(jnp.maximum(jnp.max(s_c, axis=1, keepdims=True), jnp.max(s_p, axis=1, keepdims=True)),
                    jnp.maximum(jnp.max(s_m, axis=1, keepdims=True), sink))
    e_c = jnp.exp(s_c - m)
    e_p = jnp.exp(s_p - m)
    e_m = jnp.exp(s_m - m)
    e_s = jnp.exp(sink - m)
    l = (jnp.sum(e_c, axis=1, keepdims=True) + jnp.sum(e_p, axis=1, keepdims=True)
         + jnp.sum(e_m, axis=1, keepdims=True) + e_s)
    return e_c, e_p, e_m, e_s, l


def _swa_specs(T):
    G = SWA_GROUP
    qblk = pl.BlockSpec((G, BLOCK, HEAD_DIM), lambda kv, n: (kv, n, 0))
    cur = pl.BlockSpec((None, BLOCK, HEAD_DIM), lambda kv, n: (kv, n + 1, 0))
    prev = pl.BlockSpec((None, BLOCK, HEAD_DIM), lambda kv, n: (kv, n, 0))
    meta = pl.BlockSpec((None, BLOCK, HEAD_DIM), lambda kv, n: (kv, 1, 0))
    bias = pl.BlockSpec((G, BLOCK, BLOCK), lambda kv, n: (kv, 0, 0))
    smem = pl.BlockSpec(memory_space=pltpu.SMEM)
    return qblk, cur, prev, meta, bias, smem


def _swa_fwd(q, kpad, vpad, bc, bp, far, sinks, *, name):
    Hq, T, dh = q.shape
    nb = T // BLOCK
    G = SWA_GROUP

    def body(q_ref, kc_ref, kp_ref, km_ref, vc_ref, vp_ref, vm_ref, bc_ref, bp_ref, far_ref, sink_ref, o_ref):
        kv = pl.program_id(0)
        n = pl.program_id(1)
        kc, kp, km = kc_ref[...], kp_ref[...], km_ref[...]
        vc, vp, vm = vc_ref[...], vp_ref[...], vm_ref[...]
        for g in range(G):
            h = kv * G + g
            qs = q_ref[g] * SCALE
            e_c, e_p, e_m, _, l = _swa_probs(qs, kc, kp, km, bc_ref[g], bp_ref[g], far_ref[h], sink_ref[h], n)
            o = (jnp.dot(e_c.astype(BF16), vc, preferred_element_type=F32)
                 + jnp.dot(e_p.astype(BF16), vp, preferred_element_type=F32)
                 + jnp.dot(e_m.astype(BF16), vm, preferred_element_type=F32))
            o_ref[g] = (o / l).astype(o_ref.dtype)

    qblk, cur, prev, meta, bias, smem = _swa_specs(T)
    return pl.pallas_call(
        body, out_shape=jax.ShapeDtypeStruct((Hq, T, dh), BF16), grid=(SWA_KV_HEADS, nb),
        in_specs=[qblk, cur, prev, meta, cur, prev, meta, bias, bias, smem, smem],
        out_specs=qblk,
        compiler_params=_params(("parallel", "parallel")), name=name,
    )(q, kpad, kpad, kpad, vpad, vpad, vpad, bc, bp, far, sinks)


def _swa_bwd(q, kpad, vpad, do, bc, bp, far, sinks, *, name):
    Hq, T, dh = q.shape
    nb = T // BLOCK
    G = SWA_GROUP

    def body(q_ref, kc_ref, kp_ref, km_ref, vc_ref, vp_ref, vm_ref, do_ref, bc_ref, bp_ref, far_ref, sink_ref,
             dq_ref, dk_ref, dv_ref, dbc_ref, dbp_ref, dbf_ref, dsk_ref):
        kv = pl.program_id(0)
        n = pl.program_id(1)

        @pl.when(n == 0)
        def _():
            for ref in (dk_ref, dv_ref, dbc_ref, dbp_ref, dbf_ref, dsk_ref):
                ref[...] = jnp.zeros(ref.shape, F32)

        kc, kp, km = kc_ref[...], kp_ref[...], km_ref[...]
        vc, vp, vm = vc_ref[...], vp_ref[...], vm_ref[...]
        dkc = dkp = dkm = dvc = dvp = dvm = jnp.zeros((BLOCK, dh), F32)
        for g in range(G):
            h = kv * G + g
            qs = q_ref[g] * SCALE
            e_c, e_p, e_m, e_s, l = _swa_probs(qs, kc, kp, km, bc_ref[g], bp_ref[g], far_ref[h], sink_ref[h], n)
            inv = 1.0 / l
            p_c, p_p, p_m = e_c * inv, e_p * inv, e_m * inv
            dob = do_ref[g]
            dp_c = lax.dot_general(dob, vc, NT, preferred_element_type=F32)
            dp_p = lax.dot_general(dob, vp, NT, preferred_element_type=F32)
            dp_m = lax.dot_general(dob, vm, NT, preferred_element_type=F32)
            delta = (jnp.sum(p_c * dp_c, axis=1, keepdims=True) + jnp.sum(p_p * dp_p, axis=1, keepdims=True)
                     + jnp.sum(p_m * dp_m, axis=1, keepdims=True))
            ds_c = p_c * (dp_c - delta)
            ds_p = p_p * (dp_p - delta)
            ds_m = p_m * (dp_m - delta)
            dsk_ref[g] += -(e_s * inv) * delta
            dbc_ref[g] += ds_c
            dbp_ref[g] += ds_p + jnp.where(n == 1, ds_m, 0.0)
            dbf_ref[g] += jnp.where(n >= 2, ds_m, 0.0)
            bc16, bp16, bm16 = ds_c.astype(BF16), ds_p.astype(BF16), ds_m.astype(BF16)
            dq = (jnp.dot(bc16, kc, preferred_element_type=F32) + jnp.dot(bp16, kp, preferred_element_type=F32)
                  + jnp.dot(bm16, km, preferred_element_type=F32))
            dq_ref[g] = (dq * SCALE).astype(dq_ref.dtype)
            dkc += lax.dot_general(bc16, qs, TN, preferred_element_type=F32)
            dkp += lax.dot_general(bp16, qs, TN, preferred_element_type=F32)
            dkm += lax.dot_general(bm16, qs, TN, preferred_element_type=F32)
            dvc += lax.dot_general(p_c.astype(BF16), dob, TN, preferred_element_type=F32)
            dvp += lax.dot_general(p_p.astype(BF16), dob, TN, preferred_element_type=F32)
            dvm += lax.dot_general(p_m.astype(BF16), dob, TN, preferred_element_type=F32)
        cur_off = pl.multiple_of((n + 1) * BLOCK, BLOCK)
        prev_off = pl.multiple_of(n * BLOCK, BLOCK)
        dk_ref[pl.ds(cur_off, BLOCK), :] += dkc
        dk_ref[pl.ds(prev_off, BLOCK), :] += dkp
        dk_ref[BLOCK:2 * BLOCK, :] += dkm
        dv_ref[pl.ds(cur_off, BLOCK), :] += dvc
        dv_ref[pl.ds(prev_off, BLOCK), :] += dvp
        dv_ref[BLOCK:2 * BLOCK, :] += dvm

    qblk, cur, prev, meta, bias, smem = _swa_specs(T)
    kvfull = pl.BlockSpec((None, T + BLOCK, dh), lambda kv, n: (kv, 0, 0))
    dsk = pl.BlockSpec((G, BLOCK, 1), lambda kv, n: (kv, 0, 0))
    return pl.pallas_call(
        body,
        out_shape=(jax.ShapeDtypeStruct((Hq, T, dh), BF16),
                   jax.ShapeDtypeStruct((SWA_KV_HEADS, T + BLOCK, dh), F32),
                   jax.ShapeDtypeStruct((SWA_KV_HEADS, T + BLOCK, dh), F32),
                   jax.ShapeDtypeStruct((Hq, BLOCK, BLOCK), F32), jax.ShapeDtypeStruct((Hq, BLOCK, BLOCK), F32),
                   jax.ShapeDtypeStruct((Hq, BLOCK, BLOCK), F32), jax.ShapeDtypeStruct((Hq, BLOCK, 1), F32)),
        grid=(SWA_KV_HEADS, nb),
        in_specs=[qblk, cur, prev, meta, cur, prev, meta, qblk, bias, bias, smem, smem],
        out_specs=(qblk, kvfull, kvfull, bias, bias, bias, dsk),
        compiler_params=_params(("parallel", "arbitrary")), name=name,
    )(q, kpad, kpad, kpad, vpad, vpad, vpad, do, bc, bp, far, sinks)


def _small_grads(dbc, dbp, dbf, dsk, oh_cur, oh_prev, *, name):
    Hq = dbc.shape[0]

    def body(dbc_ref, dbp_ref, dbf_ref, dsk_ref, oc_ref, op_ref, tab_ref, sink_ref):
        tab = (jnp.dot(dbc_ref[...], oc_ref[...], precision=HIGHEST, preferred_element_type=F32)
               + jnp.dot(dbp_ref[...], op_ref[...], precision=HIGHEST, preferred_element_type=F32))
        far = jnp.sum(dbf_ref[...], axis=1, keepdims=True)
        last = lax.broadcasted_iota(jnp.int32, (Hq, N_BUCKETS), 1) == N_BUCKETS - 1
        tab_ref[...] = tab + jnp.where(last, far, 0.0)
        sink_ref[...] = jnp.sum(dsk_ref[...], axis=1, keepdims=True)

    vm = pl.BlockSpec(memory_space=pltpu.VMEM)
    return pl.pallas_call(
        body, out_shape=(jax.ShapeDtypeStruct((Hq, N_BUCKETS), F32), jax.ShapeDtypeStruct((Hq, 1), F32)),
        in_specs=[vm] * 6, out_specs=(vm, vm), compiler_params=_params(), name=name,
    )(dbc.reshape(Hq, -1), dbp.reshape(Hq, -1), dbf.reshape(Hq, -1), dsk.reshape(Hq, -1), oh_cur, oh_prev)


def _coords():
    return lax.axis_index("x"), lax.axis_index("y"), lax.axis_index("c")


def _all_gather(shards, *, name):
    nt = len(shards)

    def body(*refs):
        ins, outs = refs[:nt], refs[nt:2 * nt]
        send_sems, recv_sems, local_sems = refs[2 * nt:]
        x, y, c = _coords()
        me, sibling = (x, y, c), (x, y, 1 - c)
        chips = [(1 - x, y), (x, 1 - y), (1 - x, 1 - y)]

        def slot(t, dev):
            return outs[t].at[4 * dev[0] + 2 * dev[1] + dev[2]]

        def copy(t, k, block, to, src=None):
            dst = slot(t, block)
            return pltpu.make_async_remote_copy(
                src_ref=dst if src is None else src, dst_ref=dst,
                send_sem=send_sems.at[t, k], recv_sem=recv_sems.at[t, k], device_id=to, device_id_type=MESH)

        mine = [pltpu.make_async_copy(ins[t], slot(t, me), local_sems.at[t]) for t in range(nt)]
        for cp in mine:
            cp.start()
        first = []
        for t in range(nt):
            first.append(copy(t, 0, me, sibling, src=ins[t]))
            first += [copy(t, 1 + j, me, (*chip, c), src=ins[t]) for j, chip in enumerate(chips)]
        for cp in first:
            cp.start()
        passed = []
        for j, chip in enumerate(chips):
            for t in range(nt):
                copy(t, 1 + j, (*chip, c), me).wait_recv()
                cp = copy(t, 4 + j, (*chip, c), sibling)
                cp.start()
                passed.append(cp)
        for t in range(nt):
            copy(t, 0, sibling, me).wait_recv()
            for j, chip in enumerate(chips):
                copy(t, 4 + j, (*chip, 1 - c), me).wait_recv()
        for cp in first + passed:
            cp.wait_send()
        for cp in mine:
            cp.wait()

    hbm = pl.BlockSpec(memory_space=pl.ANY)
    return pl.pallas_call(
        body,
        out_shape=tuple(jax.ShapeDtypeStruct((N_DEV,) + s.shape, s.dtype) for s in shards),
        in_specs=[hbm] * nt, out_specs=tuple([hbm] * nt),
        scratch_shapes=[pltpu.SemaphoreType.DMA((nt, 7)), pltpu.SemaphoreType.DMA((nt, 7)),
                        pltpu.SemaphoreType.DMA((nt,))],
        compiler_params=_params(), name=name)(*shards)


def _exchange_cores(gs, *, name):
    nt = len(gs)

    def body(*refs):
        g_refs, o_refs = refs[:nt], refs[nt:2 * nt]
        send_sems, recv_sems = refs[2 * nt:]
        x, y, c = _coords()
        sibling = (x, y, 1 - c)

        def copy(t, j, core):
            return pltpu.make_async_remote_copy(
                src_ref=g_refs[t].at[2 * j + core], dst_ref=o_refs[t].at[j],
                send_sem=send_sems.at[t, j], recv_sem=recv_sems.at[t, j], device_id=sibling, device_id_type=MESH)

        sends = [copy(t, j, 1 - c) for t in range(nt) for j in range(4)]
        for cp in sends:
            cp.start()
        for cp in sends:
            cp.wait_recv()
        for cp in sends:
            cp.wait_send()

    hbm = pl.BlockSpec(memory_space=pl.ANY)
    return pl.pallas_call(
        body, out_shape=tuple(jax.ShapeDtypeStruct((4,) + g.shape[1:], g.dtype) for g in gs),
        in_specs=[hbm] * nt, out_specs=tuple([hbm] * nt),
        scratch_shapes=[pltpu.SemaphoreType.DMA((nt, 4)), pltpu.SemaphoreType.DMA((nt, 4))],
        compiler_params=_params(), name=name)(*gs)


def _exchange_chips(ps, *, name):
    nt = len(ps)

    def body(*refs):
        p_refs, o_refs = refs[:nt], refs[nt:2 * nt]
        send_sems, recv_sems = refs[2 * nt:]
        x, y, c = _coords()
        peers = [(1 - x, y), (x, 1 - y), (1 - x, 1 - y)]
        sends = []
        for t in range(nt):
            for k, (px, py) in enumerate(peers):
                sends.append(pltpu.make_async_remote_copy(
                    src_ref=p_refs[t].at[2 * px + py], dst_ref=o_refs[t].at[k],
                    send_sem=send_sems.at[t, k], recv_sem=recv_sems.at[t, k],
                    device_id=(px, py, c), device_id_type=MESH))
        for cp in sends:
            cp.start()
        for cp in sends:
            cp.wait_recv()
        for cp in sends:
            cp.wait_send()

    hbm = pl.BlockSpec(memory_space=pl.ANY)
    return pl.pallas_call(
        body, out_shape=tuple(jax.ShapeDtypeStruct((3,) + p.shape[1:], p.dtype) for p in ps),
        in_specs=[hbm] * nt, out_specs=tuple([hbm] * nt),
        scratch_shapes=[pltpu.SemaphoreType.DMA((nt, 3)), pltpu.SemaphoreType.DMA((nt, 3))],
        compiler_params=_params(), name=name)(*ps)


def _add_cores(g, r, core, *, name):
    _, A, B = g.shape
    ta = _tile(A, 512, 16)

    def body(core_ref, a_ref, b_ref, o_ref, o16_ref):
        s = a_ref[...] + b_ref[...]
        o_ref[...] = s
        o16_ref[...] = s.astype(BF16)

    blk = (None, ta, B)
    out = pl.BlockSpec(blk, lambda j, i, core_ref: (j, i, 0))
    return pl.pallas_call(
        body, out_shape=(jax.ShapeDtypeStruct((4, A, B), F32), jax.ShapeDtypeStruct((4, A, B), BF16)),
        grid_spec=pltpu.PrefetchScalarGridSpec(
            num_scalar_prefetch=1, grid=(4, A // ta),
            in_specs=[pl.BlockSpec(blk, lambda j, i, core_ref: (2 * j + core_ref[0], i, 0)),
                      pl.BlockSpec(blk, lambda j, i, core_ref: (j, i, 0))],
            out_specs=(out, out)),
        compiler_params=_params(("parallel", "parallel")), name=name)(core, g, r)


def _adamw_math(w, g, m, v):
    m = ADAM_B1 * m + (1.0 - ADAM_B1) * g
    v = ADAM_B2 * v + (1.0 - ADAM_B2) * (g * g)
    m_hat = m / (1.0 - ADAM_B1 ** ADAM_STEP)
    v_hat = v / (1.0 - ADAM_B2 ** ADAM_STEP)
    delta = -ADAM_LR * (m_hat / (jnp.sqrt(v_hat) + ADAM_EPS) + ADAM_WD * w)
    return delta, m, v


def _sum_adamw(p, r, chip, w, m, v, *, segs, ta, name):
    Aw, Bw = w.shape
    Bg = p.shape[2]
    assert Aw % ta == 0

    def body(chip_ref, p_ref, r0, r1, r2, w_ref, m_ref, v_ref, g_out, d_out, m_out, v_out):
        for gc, wc, n in segs:
            g = ((p_ref[:, gc:gc + n] + r0[:, gc:gc + n].astype(F32)) + r1[:, gc:gc + n].astype(F32)
                 ) + r2[:, gc:gc + n].astype(F32)
            delta, m_new, v_new = _adamw_math(w_ref[:, wc:wc + n], g, m_ref[:, wc:wc + n], v_ref[:, wc:wc + n])
            g_out[:, wc:wc + n] = g
            d_out[:, wc:wc + n] = delta
            m_out[:, wc:wc + n] = m_new
            v_out[:, wc:wc + n] = v_new

    gblk = (None, ta, Bg)
    row = pl.BlockSpec((ta, Bw), lambda i, chip_ref: (i, 0))
    rspecs = [pl.BlockSpec(gblk, (lambda i, chip_ref, k=k: (k, i, 0))) for k in range(3)]
    shp = jax.ShapeDtypeStruct((Aw, Bw), F32)
    return pl.pallas_call(
        body, out_shape=(shp, shp, shp, shp),
        grid_spec=pltpu.PrefetchScalarGridSpec(
            num_scalar_prefetch=1, grid=(Aw // ta,),
            in_specs=[pl.BlockSpec(gblk, lambda i, chip_ref: (chip_ref[0], i, 0))] + rspecs + [row, row, row],
            out_specs=(row, row, row, row)),
        compiler_params=_params(("parallel",)), name=name)(chip, p, r, r, r, w, m, v)


def _adamw(w, g, m, v, *, name):
    def body(w_ref, g_ref, m_ref, v_ref, d_out, m_out, v_out):
        delta, m_new, v_new = _adamw_math(w_ref[...], g_ref[...], m_ref[...], v_ref[...])
        d_out[...] = delta
        m_out[...] = m_new
        v_out[...] = v_new

    vm = pl.BlockSpec(memory_space=pltpu.VMEM)
    shp = jax.ShapeDtypeStruct(w.shape, F32)
    return pl.pallas_call(body, out_shape=(shp, shp, shp), in_specs=[vm] * 4, out_specs=(vm, vm, vm),
                          compiler_params=_params(), name=name)(w, g, m, v)


def _small_allreduce_adamw(s, w, m, v, *, name):
    R, W = s.shape

    def body(s_ref, w_ref, m_ref, v_ref, g_out, d_out, m_out, v_out, gath, send_sems, recv_sems):
        x, y, c = _coords()
        mine = 4 * x + 2 * y + c
        gath[mine] = s_ref[...]
        peers = [((1 - x) if k & 4 else x, (1 - y) if k & 2 else y, (1 - c) if k & 1 else c) for k in range(1, N_DEV)]
        sends = []
        for k in range(1, N_DEV):
            peer = peers[k - 1]
            sends.append(pltpu.make_async_remote_copy(
                src_ref=s_ref, dst_ref=gath.at[mine], send_sem=send_sems.at[k - 1], recv_sem=recv_sems.at[k - 1],
                device_id=peer, device_id_type=MESH))
        for cp in sends:
            cp.start()
        for k in range(1, N_DEV):
            peer = peers[k - 1]
            pltpu.make_async_remote_copy(
                src_ref=s_ref, dst_ref=gath.at[4 * peer[0] + 2 * peer[1] + peer[2]],
                send_sem=send_sems.at[k - 1], recv_sem=recv_sems.at[k - 1],
                device_id=peer, device_id_type=MESH).wait_recv()
        for cp in sends:
            cp.wait_send()
        g = gath[0]
        for d in range(1, N_DEV):
            g = g + gath[d]
        delta, m_new, v_new = _adamw_math(w_ref[...], g, m_ref[...], v_ref[...])
        g_out[...] = g
        d_out[...] = delta
        m_out[...] = m_new
        v_out[...] = v_new

    vm = pl.BlockSpec(memory_space=pltpu.VMEM)
    shp = jax.ShapeDtypeStruct((R, W), F32)
    return pl.pallas_call(
        body, out_shape=(shp, shp, shp, shp), in_specs=[vm] * 4, out_specs=(vm, vm, vm, vm),
        scratch_shapes=[pltpu.VMEM((N_DEV, R, W), F32), pltpu.SemaphoreType.DMA((N_DEV - 1,)),
                        pltpu.SemaphoreType.DMA((N_DEV - 1,))],
        compiler_params=_params(), name=name)(s, w, m, v)


def _pack_small(rel_bias, g1, g2, g3, g4, b_forget, sinks, extra=None, meta=None):
    misc = jnp.concatenate([rel_bias.reshape(-1), b_forget.reshape(-1), sinks.reshape(-1)])
    misc = jnp.concatenate([misc, jnp.zeros((D_MODEL - misc.shape[0],), F32)])[None]
    last = jnp.zeros((1, D_MODEL), F32) if extra is None else extra
    meta = jnp.zeros((N_META, D_MODEL), F32) if meta is None else meta
    return jnp.concatenate([g1, g2, g3, g4, misc, last, jnp.zeros((2, D_MODEL), F32), meta], axis=0)


def _unpack_small(p):
    nrb = N_BUCKETS * SWA_Q_HEADS
    misc = p[4]
    return dict(rel_bias=misc[:nrb].reshape(N_BUCKETS, SWA_Q_HEADS), ln_pre_mix=p[0:1], ln_post_mix=p[1:2],
                ln_pre_ffn=p[2:3], ln_post_ffn=p[3:4], b_forget=misc[nrb:nrb + 8].reshape(1, 8),
                sinks=misc[nrb + 8:nrb + 16].reshape(1, 8))


def _heads(a, n):
    return a.reshape(a.shape[0], n, HEAD_DIM).transpose(1, 0, 2)


def _unheads(a):
    return a.transpose(1, 0, 2).reshape(a.shape[1], -1)


def kernel(x, meta_tokens, rel_bias, ln_pre_mix, ln_post_mix, ln_pre_ffn, ln_post_ffn, w_in, b_forget, sinks, w_out, w_gate_up, w_down, loss_target, m_meta_tokens, m_rel_bias, m_ln_pre_mix, m_ln_post_mix, m_ln_pre_ffn, m_ln_post_ffn, m_w_in, m_b_forget, m_sinks, m_w_out, m_w_gate_up, m_w_down, v_meta_tokens, v_rel_bias, v_ln_pre_mix, v_ln_post_mix, v_ln_pre_ffn, v_ln_post_ffn, v_w_in, v_b_forget, v_sinks, v_w_out, v_w_gate_up, v_w_down):
    seq = x.shape[1]
    T = BLOCK + seq
    assert T % FOX_TILE == 0
    nq = T // FOX_TILE
    tm = _tile(T, 1056)
    cin = w_in.shape[2]
    hid = w_down.shape[1]
    assert w_gate_up.shape[2] == 2 * hid and cin <= W_IN_PAD and hid <= HID_PAD

    x_i, y_i, c_i = _coords()
    core = jnp.reshape(c_i, (1,)).astype(jnp.int32)
    chip = jnp.reshape(2 * x_i + y_i, (1,)).astype(jnp.int32)
    w_in_s = jnp.pad(w_in[0].astype(BF16), ((0, 0), (0, W_IN_PAD - cin)))
    w_gu_s = jnp.pad(w_gate_up[0].astype(BF16).reshape(D_MODEL, 2, hid), ((0, 0), (0, 0), (0, HID_PAD - hid)))
    w_gu_s = w_gu_s.reshape(D_MODEL, 2 * HID_PAD)
    w_down_s = jnp.pad(w_down[0].astype(BF16), ((0, HID_PAD - hid), (0, 0)))
    g_in, g_out, g_gu, g_down, g_meta = _all_gather(
        [w_in_s, w_out[0].astype(BF16), w_gu_s, w_down_s, meta_tokens], name="ag_weights")
    w_in_full = g_in[:, :, :cin].transpose(1, 0, 2).reshape(D_MODEL, N_DEV * cin)
    w_qkv = w_in_full[:, :D_QKV]
    w_f = jnp.pad(w_in_full[:, D_QKV:], ((0, 0), (0, BLOCK - FOX_HEADS)))
    w_in_cat = jnp.concatenate([w_qkv, w_f, jnp.zeros((D_MODEL, D_PROJ_PAD - D_QKV - BLOCK), BF16)], axis=1)
    w_out_full = g_out.reshape(D_MODEL, D_MODEL)
    w_down_full = g_down.reshape(N_DEV * HID_PAD, D_MODEL)
    meta_full = g_meta.transpose(1, 0, 2).reshape(N_META, D_MODEL)

    h0 = jnp.concatenate([jnp.zeros((PAD_ROWS, D_MODEL), F32), meta_full, x[0]], axis=0)
    target = jnp.concatenate([jnp.zeros((BLOCK, D_MODEL), F32), loss_target[0]], axis=0)
    hn1 = _rms_fwd(h0, ln_pre_mix, name="rms_pre_mix")
    proj = _matmul(hn1, w_qkv, out_dtype=BF16, tm=tm, tn=768, name="mm_in_proj")
    proj_f = _matmul(hn1, w_f, out_dtype=F32, tm=tm, tn=BLOCK, name="mm_in_proj_f")

    q_a = _heads(proj[:, 0:512], 8)
    k_a = jnp.pad(_heads(proj[:, 512:640], 2), ((0, 0), (BLOCK, 0), (0, 0)))
    v_a = jnp.pad(_heads(proj[:, 640:768], 2), ((0, 0), (BLOCK, 0), (0, 0)))
    f_t = proj_f[:, :FOX_HEADS].T
    bf_col = b_forget.reshape(FOX_HEADS, 1)

    oh_cur, oh_prev = _bucket_onehots()
    bias_c = jnp.einsum("pb,bh->hp", jnp.asarray(oh_cur), rel_bias, precision=HIGHEST).reshape(8, BLOCK, BLOCK)
    bias_p = jnp.einsum("pb,bh->hp", jnp.asarray(oh_prev), rel_bias, precision=HIGHEST).reshape(8, BLOCK, BLOCK)
    far = rel_bias[N_BUCKETS - 1]
    sink_v = sinks[0]
    o_a = _swa_fwd(q_a, k_a, v_a, bias_c, bias_p, far, sink_v, name="swa_fwd")

    _, cum_col = _fox_gates_fwd(f_t, bf_col, name="fox_gates_fwd")
    q_b, k_b, v_b = _fox_prep(proj, cum_col, name="fox_prep")
    o_b, lse_row = _fox_fwd(q_b, k_b, v_b, name="fox_fwd")

    mix = jnp.concatenate([_unheads(o_a), _unheads(o_b)], axis=1)
    a1 = _matmul(mix, w_out_full, out_dtype=F32, tm=tm, tn=512, name="mm_out_proj")
    h1 = _post_res(a1, ln_post_mix, h0, name="post_mix")
    hn2 = _rms_fwd(h1, ln_pre_ffn, name="rms_pre_ffn")
    gu = _matmul(hn2, g_gu, b_shards=True, out_dtype=BF16, tm=tm, name="mm_gate_up")
    act = _swiglu_fwd(gu, name="swiglu_fwd")
    ff = _matmul(act, w_down_full, out_dtype=F32, tm=tm, tn=512, name="mm_down")
    dh2, loss_acc = _loss_head(ff, ln_post_ffn, h1, target, name="loss_head")

    dff, dg_post_ffn = _rms_bwd(ff, ln_post_ffn, dh2, None, out_dtype=BF16, name="rms_bwd_post_ffn")
    dact = _matmul(dff, w_down_full, nt=True, out_dtype=BF16, tm=tm, tn=1536, name="mm_d_act")
    d_w_down = _matmul(act.T, dff, out_dtype=F32, tm=768, tn=512, name="mm_dw_down")
    dgu = _swiglu_bwd(gu, dact, name="swiglu_bwd")
    dhn2 = _matmul(dgu, g_gu, nt=True, b_shards=True, out_dtype=F32, tm=tm, tn=D_MODEL, name="mm_d_hn2")
    d_w_gu = _matmul(hn2.T, dgu, out_shards=True, out_dtype=F32, tm=512, tn=2 * HID_PAD, name="mm_dw_gate_up")
    dh1, dg_pre_ffn = _rms_bwd(h1, ln_pre_ffn, dhn2, dh2, out_dtype=F32, name="rms_bwd_pre_ffn")
    da1, dg_post_mix = _rms_bwd(a1, ln_post_mix, dh1, None, out_dtype=BF16, name="rms_bwd_post_mix")
    dmix = _matmul(da1, w_out_full, nt=True, out_dtype=BF16, tm=tm, tn=512, name="mm_d_mix")
    d_w_out = _matmul(mix.T, da1, out_dtype=F32, tm=512, tn=512, name="mm_dw_out")

    do_a = _heads(dmix[:, :512], 8)
    dq_a, dk_a, dv_a, dbc, dbp, dbf, dsk = _swa_bwd(q_a, k_a, v_a, do_a, bias_c, bias_p, far, sink_v, name="swa_bwd")
    d_tab, d_sink = _small_grads(dbc, dbp, dbf, dsk, jnp.asarray(oh_cur), jnp.asarray(oh_prev), name="small_grads")

    do_b = _fox_prep_bwd(dmix, o_b, name="fox_prep_bwd")
    dq_t, dk_b, dv_b, dck = _fox_bwd(q_b, k_b, v_b, do_b, lse_row, name="fox_bwd")
    dcq = dq_t[:, :, LANE_QC, :].reshape(FOX_HEADS, T)
    df_t, d_bf = _fox_gates_bwd(dcq, dck.reshape(FOX_HEADS, T), f_t, bf_col, name="fox_gates_bwd")
    dq_b = (dq_t[:, :, :HEAD_DIM, :].transpose(1, 3, 0, 2).reshape(T, FOX_W) * SCALE).astype(BF16)
    dk_b = dk_b[:, :, :HEAD_DIM].transpose(1, 0, 2).reshape(T, FOX_W)
    dv_b = dv_b[:, :, :HEAD_DIM].transpose(1, 0, 2).reshape(T, FOX_W)

    dproj = jnp.concatenate([
        _unheads(dq_a), _unheads(dk_a[:, BLOCK:]).astype(BF16), _unheads(dv_a[:, BLOCK:]).astype(BF16),
        dq_b, dk_b, dv_b, df_t.T.astype(BF16), jnp.zeros((T, D_PROJ_PAD - D_PROJ), BF16)], axis=1)
    dhn1 = _matmul(dproj, w_in_cat, nt=True, out_dtype=F32, tm=tm, tn=512, name="mm_d_hn1")
    dproj_s = jnp.pad(dproj[:, :N_DEV * cin].reshape(T, N_DEV, cin), ((0, 0), (0, 0), (0, W_IN_PAD - cin)))
    d_w_in = _matmul(hn1.T, dproj_s.reshape(T, N_DEV * W_IN_PAD), out_shards=True, out_dtype=F32, tm=512,
                     tn=W_IN_PAD, name="mm_dw_in")
    dh0, dg_pre_mix = _rms_bwd(h0, ln_pre_mix, dhn1, dh1, out_dtype=F32, name="rms_bwd_pre_mix")
    grad_x = dh0[BLOCK:][None]
    d_meta = dh0[PAD_ROWS:BLOCK]

    by_dev = [d_w_in, d_w_out.reshape(N_DEV, -1, D_MODEL), d_w_gu, d_w_down.reshape(N_DEV, HID_PAD, D_MODEL)]
    tags = ["w_in", "w_out", "w_gate_up", "w_down"]
    from_sibling = _exchange_cores(by_dev, name="rs_cores")
    sums = [_add_cores(g, r, core, name="rs_add_" + t) for g, r, t in zip(by_dev, from_sibling, tags)]
    chip_sum = [s[0] for s in sums]
    from_chips = _exchange_chips([s[1] for s in sums], name="rs_chips")
    shard_w = [(w_in, m_w_in, v_w_in), (w_out, m_w_out, v_w_out), (w_gate_up, m_w_gate_up, v_w_gate_up),
               (w_down, m_w_down, v_w_down)]
    segs = [[(0, 0, cin)], [(0, 0, D_MODEL)], [(0, 0, hid), (HID_PAD, hid, hid)], [(0, 0, D_MODEL)]]
    tas = [256, BLOCK, 256, hid]
    big = [{}, {}, {}, {}]
    for i, t in enumerate(tags):
        w_t, m_t, v_t = shard_w[i]
        res = _sum_adamw(chip_sum[i], from_chips[i], chip, w_t[0], m_t[0], v_t[0], segs=segs[i], ta=tas[i],
                         name="rs_adamw_" + t)
        for kind in range(4):
            big[kind][t] = res[kind][None]

    loss_row = jnp.pad(loss_acc[0:1, 0:1] * (0.5 / D_MODEL), ((0, 0), (0, D_MODEL - 1)))
    s_small = _pack_small(d_tab.T, dg_pre_mix, dg_post_mix, dg_pre_ffn, dg_post_ffn, d_bf, d_sink,
                          extra=loss_row, meta=d_meta)
    w_s = _pack_small(rel_bias, ln_pre_mix, ln_post_mix, ln_pre_ffn, ln_post_ffn, b_forget, sinks)
    m_s = _pack_small(m_rel_bias, m_ln_pre_mix, m_ln_post_mix, m_ln_pre_ffn, m_ln_post_ffn, m_b_forget, m_sinks)
    v_s = _pack_small(v_rel_bias, v_ln_pre_mix, v_ln_post_mix, v_ln_pre_ffn, v_ln_post_ffn, v_b_forget, v_sinks)
    small = _small_allreduce_adamw(s_small, w_s, m_s, v_s, name="small_allreduce_adamw")
    loss = small[0][5, 0]
    mcols = meta_tokens.shape[1]
    g_meta_mine = lax.dynamic_slice(small[0][8:8 + N_META], (0, (4 * x_i + 2 * y_i + c_i) * mcols), (N_META, mcols))
    big[0]["meta_tokens"] = g_meta_mine
    for kind, arr in enumerate(_adamw(meta_tokens, g_meta_mine, m_meta_tokens, v_meta_tokens, name="adamw_meta")):
        big[kind + 1]["meta_tokens"] = arr
    small = [_unpack_small(p) for p in small]

    names = ["meta_tokens", "rel_bias", "ln_pre_mix", "ln_post_mix", "ln_pre_ffn", "ln_post_ffn", "w_in",
             "b_forget", "sinks", "w_out", "w_gate_up", "w_down"]
    outs = [loss, grad_x]
    for kind in range(4):
        for nme in names:
            outs.append(big[kind][nme] if nme in big[kind] else small[kind][nme])
    return tuple(outs)
```

```python
import math

import numpy as np
import jax
import jax.numpy as jnp
from jax import lax
from jax.experimental import pallas as pl
from jax.experimental.pallas import tpu as pltpu

F32 = jnp.float32
BF16 = jnp.bfloat16
HIGHEST = lax.Precision.HIGHEST
MESH = pl.DeviceIdType.MESH

N_DEV = 8
D_MODEL = 1024
N_META = 16
HEAD_DIM = 64
SWA_Q_HEADS = 8
SWA_KV_HEADS = 2
SWA_GROUP = 4
FOX_HEADS = 8
FOX_W = FOX_HEADS * HEAD_DIM
BLOCK = 128
PAD_ROWS = BLOCK - N_META
N_BUCKETS = 32
MAX_DISTANCE = 128
D_FF = 2816
D_QKV = 2304
D_PROJ = D_QKV + FOX_HEADS
D_PROJ_PAD = 2560
EPS = 1e-6
NEG = -1e30
SCALE = HEAD_DIM ** -0.5
ADAM_LR, ADAM_B1, ADAM_B2, ADAM_EPS, ADAM_WD, ADAM_STEP = 0.001, 0.9, 0.999, 1e-08, 0.01, 10
VMEM_LIMIT = 48 * 1024 * 1024
FOX_TILE = 384
FOX_GROUP = 4
W_IN_PAD = 384
HID_PAD = 384

NT = (((1,), (1,)), ((), ()))
NN = (((1,), (0,)), ((), ()))
TN = (((0,), (0,)), ((), ()))


def _params(sem=None, **kw):
    if sem is not None:
        kw["dimension_semantics"] = sem
    return pltpu.CompilerParams(vmem_limit_bytes=VMEM_LIMIT, **kw)


def _tile(n, target, mult=16):
    best = None
    for t in range(mult, min(n, target) + 1, mult):
        if n % t == 0:
            best = t
    assert best is not None, (n, target)
    return best


def _matmul(a, b, *, nt=False, b_shards=False, out_shards=False, out_dtype, tm, tn=None, tk=None, name):
    M, K = a.shape
    if b_shards and nt:
        N, tk = b.shape[1], b.shape[2]
    elif b_shards:
        N, tn = b.shape[0] * b.shape[2], b.shape[2]
    else:
        N = b.shape[0] if nt else b.shape[1]
    tk = K if tk is None else tk
    assert M % tm == 0 and N % tn == 0 and K % tk == 0, (name, a.shape, b.shape, tm, tn, tk)
    nk = K // tk
    dn = NT if nt else NN

    def body(a_ref, b_ref, o_ref, *scr):
        part = lax.dot_general(a_ref[...], b_ref[...], dn, preferred_element_type=F32)
        if nk == 1:
            o_ref[...] = part.astype(o_ref.dtype)
        else:
            acc = scr[0]
            k = pl.program_id(2)

            @pl.when(k == 0)
            def _():
                acc[...] = part

            @pl.when(k > 0)
            def _():
                acc[...] += part

            @pl.when(k == nk - 1)
            def _():
                o_ref[...] = acc[...].astype(o_ref.dtype)

    if b_shards and nt:
        b_spec = pl.BlockSpec((None, tn, tk), lambda i, j, k: (k, j, 0))
    elif b_shards:
        b_spec = pl.BlockSpec((None, tk, tn), lambda i, j, k: (j, k, 0))
    elif nt:
        b_spec = pl.BlockSpec((tn, tk), lambda i, j, k: (j, k))
    else:
        b_spec = pl.BlockSpec((tk, tn), lambda i, j, k: (k, j))
    if out_shards:
        out_shape = jax.ShapeDtypeStruct((N // tn, M, tn), out_dtype)
        out_spec = pl.BlockSpec((None, tm, tn), lambda i, j, k: (j, i, 0))
    else:
        out_shape = jax.ShapeDtypeStruct((M, N), out_dtype)
        out_spec = pl.BlockSpec((tm, tn), lambda i, j, k: (i, j))
    return pl.pallas_call(
        body,
        out_shape=out_shape,
        grid=(M // tm, N // tn, nk),
        in_specs=[pl.BlockSpec((tm, tk), lambda i, j, k: (i, k)), b_spec],
        out_specs=out_spec,
        scratch_shapes=[pltpu.VMEM((tm, tn), F32)] if nk > 1 else [],
        compiler_params=_params(("parallel", "parallel", "arbitrary")),
        name=name,
    )(a, b)


def _rstd(x):
    return lax.rsqrt(jnp.mean(x * x, axis=-1, keepdims=True) + EPS)


def _rms_fwd(x, g, *, name):
    T, D = x.shape
    tm = _tile(T, 512)

    def body(x_ref, g_ref, o_ref):
        x = x_ref[...]
        o_ref[...] = (x * _rstd(x) * g_ref[...]).astype(o_ref.dtype)

    return pl.pallas_call(
        body, out_shape=jax.ShapeDtypeStruct((T, D), BF16), grid=(T // tm,),
        in_specs=[pl.BlockSpec((tm, D), lambda i: (i, 0)), pl.BlockSpec((1, D), lambda i: (0, 0))],
        out_specs=pl.BlockSpec((tm, D), lambda i: (i, 0)),
        compiler_params=_params(("parallel",)), name=name)(x, g)


def _post_res(a, g, h, *, name):
    T, D = a.shape
    tm = _tile(T, 512)

    def body(a_ref, g_ref, h_ref, o_ref):
        a = a_ref[...]
        o_ref[...] = h_ref[...] + a * _rstd(a) * g_ref[...]

    row = pl.BlockSpec((tm, D), lambda i: (i, 0))
    return pl.pallas_call(
        body, out_shape=jax.ShapeDtypeStruct((T, D), F32), grid=(T // tm,),
        in_specs=[row, pl.BlockSpec((1, D), lambda i: (0, 0)), row], out_specs=row,
        compiler_params=_params(("parallel",)), name=name)(a, g, h)


def _loss_head(a, g, h, target, *, name):
    T, D = a.shape
    tm = _tile(T, 512)

    def body(a_ref, g_ref, h_ref, t_ref, dy_ref, loss_ref):
        i = pl.program_id(0)
        a = a_ref[...]
        y = h_ref[...] + a * _rstd(a) * g_ref[...]
        rows = i * tm + lax.broadcasted_iota(jnp.int32, (tm, 1), 0)
        err = jnp.where(rows >= BLOCK, y - t_ref[...], 0.0)
        dy_ref[...] = err / D
        part = jnp.sum(jnp.sum(err * err, axis=1, keepdims=True), axis=0, keepdims=True)

        @pl.when(i == 0)
        def _():
            loss_ref[...] = jnp.zeros_like(loss_ref)

        loss_ref[...] += jnp.broadcast_to(part, loss_ref.shape)

    row = pl.BlockSpec((tm, D), lambda i: (i, 0))
    return pl.pallas_call(
        body, out_shape=(jax.ShapeDtypeStruct((T, D), F32), jax.ShapeDtypeStruct((8, 128), F32)),
        grid=(T // tm,),
        in_specs=[row, pl.BlockSpec((1, D), lambda i: (0, 0)), row, row],
        out_specs=(row, pl.BlockSpec((8, 128), lambda i: (0, 0))),
        compiler_params=_params(("arbitrary",)), name=name)(a, g, h, target)


def _rms_bwd(x, g, dy, res, *, out_dtype, name):
    T, D = x.shape
    tm = _tile(T, 512)
    has_res = res is not None

    def body(*refs):
        if has_res:
            x_ref, g_ref, dy_ref, r_ref, dx_ref, dg_ref = refs
        else:
            x_ref, g_ref, dy_ref, dx_ref, dg_ref = refs
        i = pl.program_id(0)
        x = x_ref[...]
        dy = dy_ref[...].astype(F32)
        r = _rstd(x)
        xh = x * r
        dxh = dy * g_ref[...]
        dx = r * (dxh - xh * jnp.mean(dxh * xh, axis=-1, keepdims=True))
        if has_res:
            dx = dx + r_ref[...]
        dx_ref[...] = dx.astype(dx_ref.dtype)

        @pl.when(i == 0)
        def _():
            dg_ref[...] = jnp.zeros_like(dg_ref)

        dg_ref[...] += jnp.sum(dy * xh, axis=0, keepdims=True)

    row = pl.BlockSpec((tm, D), lambda i: (i, 0))
    vec = pl.BlockSpec((1, D), lambda i: (0, 0))
    ins = [x, g, dy] + ([res] if has_res else [])
    return pl.pallas_call(
        body, out_shape=(jax.ShapeDtypeStruct((T, D), out_dtype), jax.ShapeDtypeStruct((1, D), F32)),
        grid=(T // tm,),
        in_specs=[row, vec, row] + ([row] if has_res else []),
        out_specs=(row, vec),
        compiler_params=_params(("arbitrary",)), name=name)(*ins)


def _swiglu_fwd(gu, *, name):
    T, F2 = gu.shape
    F = F2 // 2
    tm = _tile(T, 384)

    def body(g_ref, u_ref, o_ref):
        g = g_ref[...].astype(F32)
        o_ref[...] = (g / (1.0 + jnp.exp(-g)) * u_ref[...].astype(F32)).astype(o_ref.dtype)

    return pl.pallas_call(
        body, out_shape=jax.ShapeDtypeStruct((T, F), BF16), grid=(T // tm,),
        in_specs=[pl.BlockSpec((tm, F), lambda i: (i, 0)), pl.BlockSpec((tm, F), lambda i: (i, 1))],
        out_specs=pl.BlockSpec((tm, F), lambda i: (i, 0)),
        compiler_params=_params(("parallel",)), name=name)(gu, gu)


def _swiglu_bwd(gu, dact, *, name):
    T, F2 = gu.shape
    F = F2 // 2
    tm = _tile(T, 384)

    def body(g_ref, u_ref, d_ref, o_ref):
        g = g_ref[...].astype(F32)
        u = u_ref[...].astype(F32)
        d = d_ref[...].astype(F32)
        sg = 1.0 / (1.0 + jnp.exp(-g))
        o_ref[:, :F] = (d * u * (sg * (1.0 + g * (1.0 - sg)))).astype(o_ref.dtype)
        o_ref[:, F:] = (d * (g * sg)).astype(o_ref.dtype)

    return pl.pallas_call(
        body, out_shape=jax.ShapeDtypeStruct((T, F2), BF16), grid=(T // tm,),
        in_specs=[pl.BlockSpec((tm, F), lambda i: (i, 0)), pl.BlockSpec((tm, F), lambda i: (i, 1)),
                  pl.BlockSpec((tm, F), lambda i: (i, 0))],
        out_specs=pl.BlockSpec((tm, F2), lambda i: (i, 0)),
        compiler_params=_params(("parallel",)), name=name)(gu, gu, dact)


def _fox_gates_fwd(f_t, b, *, name):
    H, T = f_t.shape
    nb = T // BLOCK

    def body(f_ref, b_ref, cum_ref, col_ref):
        f = f_ref[...] + b_ref[...]
        ls = jnp.minimum(f, 0.0) - jnp.log(1.0 + jnp.exp(-jnp.abs(f)))
        t = lax.broadcasted_iota(jnp.int32, (H, T), 1)
        ls = jnp.where(t >= PAD_ROWS, ls, 0.0)
        upper = (lax.broadcasted_iota(jnp.int32, (BLOCK, BLOCK), 0)
                 <= lax.broadcasted_iota(jnp.int32, (BLOCK, BLOCK), 1)).astype(F32)
        carry = jnp.zeros((H, 1), F32)
        for blk in range(nb):
            seg = ls[:, blk * BLOCK:(blk + 1) * BLOCK]
            pre = jnp.dot(seg, upper, precision=HIGHEST, preferred_element_type=F32) + carry
            cum_ref[:, blk * BLOCK:(blk + 1) * BLOCK] = pre
            col_ref[blk * BLOCK:(blk + 1) * BLOCK, :] = jnp.concatenate(
                [pre, jnp.zeros((BLOCK - H, BLOCK), F32)], axis=0).T
            carry = pre[:, BLOCK - 1:BLOCK]

    vm = pl.BlockSpec(memory_space=pltpu.VMEM)
    return pl.pallas_call(
        body, out_shape=(jax.ShapeDtypeStruct((H, T), F32), jax.ShapeDtypeStruct((T, BLOCK), F32)),
        in_specs=[vm, vm], out_specs=(vm, vm),
        compiler_params=_params(), name=name)(f_t, b)


def _fox_gates_bwd(dcq, dck, f_t, b, *, name):
    H, T = f_t.shape
    nb = T // BLOCK

    def body(dq_ref, d_ref, f_ref, b_ref, df_ref, db_ref):
        lower = (lax.broadcasted_iota(jnp.int32, (BLOCK, BLOCK), 0)
                 >= lax.broadcasted_iota(jnp.int32, (BLOCK, BLOCK), 1)).astype(F32)
        carry = jnp.zeros((H, 1), F32)
        for blk in range(nb - 1, -1, -1):
            seg = dq_ref[:, blk * BLOCK:(blk + 1) * BLOCK] - d_ref[:, blk * BLOCK:(blk + 1) * BLOCK]
            suf = jnp.dot(seg, lower, precision=HIGHEST, preferred_element_type=F32) + carry
            df_ref[:, blk * BLOCK:(blk + 1) * BLOCK] = suf
            carry = suf[:, 0:1]
        f = f_ref[...] + b_ref[...]
        t = lax.broadcasted_iota(jnp.int32, (H, T), 1)
        df = jnp.where(t >= PAD_ROWS, df_ref[...] / (1.0 + jnp.exp(f)), 0.0)
        df_ref[...] = df
        db_ref[...] = jnp.sum(df, axis=1, keepdims=True)

    vm = pl.BlockSpec(memory_space=pltpu.VMEM)
    return pl.pallas_call(
        body, out_shape=(jax.ShapeDtypeStruct((H, T), F32), jax.ShapeDtypeStruct((H, 1), F32)),
        in_specs=[vm, vm, vm, vm], out_specs=(vm, vm),
        compiler_params=_params(), name=name)(dcq, dck, f_t, b)


LANE_KC = HEAD_DIM
LANE_QC = HEAD_DIM + 3
LANE_END = HEAD_DIM + 6


def _split3(c):
    hi = c.astype(BF16).astype(F32)
    r = c - hi
    mid = r.astype(BF16).astype(F32)
    lo = (r - mid).astype(BF16).astype(F32)
    return hi, mid, lo


def _lanes(lane, data, start, terms, rest):
    out = rest
    for i, t in enumerate(terms):
        out = jnp.where(lane == start + i, t, out)
    return jnp.where(lane < HEAD_DIM, data, out)


def _fox_prep(proj, cum_col, *, name):
    T = proj.shape[0]
    tm = FOX_TILE
    nt = T // tm
    H = FOX_HEADS
    lanes = 2 * HEAD_DIM
    qb, kb, vb = 768 // lanes, 1280 // lanes, 1792 // lanes

    def body(q_ref, k_ref, v_ref, c_ref, qa_ref, ka_ref, va_ref):
        p = pl.program_id(0)
        i = pl.program_id(1)
        lane = lax.broadcasted_iota(jnp.int32, (tm, lanes), 1)
        rows = i * tm + lax.broadcasted_iota(jnp.int32, (tm, 1), 0)
        q2 = q_ref[...].astype(F32)
        k2 = k_ref[...].astype(F32)
        v2 = v_ref[...].astype(F32)
        cum = c_ref[...]
        for e in range(2):
            c = jnp.sum(jnp.where(lane == 2 * p + e, cum, 0.0), axis=1, keepdims=True)
            ck = jnp.where(rows >= PAD_ROWS, c, -NEG)
            qe, ke, ve = (q2, k2, v2) if e == 0 else tuple(pltpu.roll(a, HEAD_DIM, 1) for a in (q2, k2, v2))
            one = jnp.where(lane < LANE_END, 1.0, 0.0)
            qa = _lanes(lane, qe * SCALE, LANE_QC, _split3(c), jnp.where(lane < LANE_QC, -1.0, 0.0))
            ka = _lanes(lane, ke, LANE_KC, _split3(ck), one)
            va = jnp.where(lane < HEAD_DIM, ve, jnp.where(lane < LANE_QC, 1.0, 0.0))
            qa_ref[e] = qa.astype(BF16)
            ka_ref[e] = ka.astype(BF16)
            va_ref[e] = va.astype(BF16)

    def col(b):
        return pl.BlockSpec((tm, lanes), lambda p, i, b=b: (i, b + p))

    out = pl.BlockSpec((2, tm, lanes), lambda p, i: (p, i, 0))
    shp = jax.ShapeDtypeStruct((H, T, lanes), BF16)
    return pl.pallas_call(
        body, out_shape=(shp, shp, shp), grid=(H // 2, nt),
        in_specs=[col(qb), col(kb), col(vb), pl.BlockSpec((tm, lanes), lambda p, i: (i, 0))],
        out_specs=(out, out, out),
        compiler_params=_params(("parallel", "parallel")), name=name)(proj, proj, proj, cum_col)


def _fox_fwd(q_aug, k_aug, v_aug, *, ex=None, name):
    H, T, lanes = q_aug.shape
    tq = FOX_TILE
    nq = T // tq
    G = FOX_GROUP

    def body(q_ref, k_ref, v_ref, o_ref, lse_ref, m_scr, acc_scr):
        i = pl.program_id(1)
        m_scr[...] = jnp.full(m_scr.shape, NEG, F32)
        acc_scr[...] = jnp.zeros(acc_scr.shape, F32)

        def step(g, kb, diag):
            off = pl.multiple_of(kb * tq, tq)
            s_t = lax.dot_general(k_ref[g, pl.ds(off, tq), :], q_ref[g], NT, preferred_element_type=F32)
            if diag:
                r = lax.broadcasted_iota(jnp.int32, (tq, tq), 0)
                c = lax.broadcasted_iota(jnp.int32, (tq, tq), 1)
                s_t = jnp.where(c >= r, s_t, NEG)
            m_prev = m_scr[g]
            m_new = jnp.maximum(m_prev, jnp.max(s_t, axis=0, keepdims=True))
            p_t = jnp.exp(s_t - m_new).astype(BF16)
            alpha = jnp.exp(m_prev - m_new)
            acc_scr[g] = alpha * acc_scr[g] + lax.dot_general(
                v_ref[g, pl.ds(off, tq), :], p_t, TN, preferred_element_type=F32)
            m_scr[g] = m_new

        def loop_body(kb, carry):
            for g in range(G):
                step(g, kb, False)
            return carry

        lax.fori_loop(0, i, loop_body, 0)
        for g in range(G):
            step(g, i, True)
            acc = acc_scr[g]
            lse_ref[g] = m_scr[g] + jnp.log(acc[HEAD_DIM:HEAD_DIM + 1, :])
            acc_t = acc.T
            o_ref[g] = (acc_t[:, :HEAD_DIM] / acc_t[:, HEAD_DIM:HEAD_DIM + 1]).astype(o_ref.dtype)

    blk = pl.BlockSpec((G, tq, lanes), lambda h, i: (h, i, 0))
    full = pl.BlockSpec((G, T, lanes), lambda h, i: (h, 0, 0))
    grid = (H // G, nq)
    body, x_in, x_in_specs, x_out, x_out_specs, x_scr = _carry(ex, grid, 3, 2, body)
    return pl.pallas_call(
        body,
        out_shape=(jax.ShapeDtypeStruct((H, T, HEAD_DIM), BF16), jax.ShapeDtypeStruct((H, nq, 1, tq), F32), *x_out),
        grid=grid,
        in_specs=[blk, full, full] + x_in_specs,
        out_specs=(pl.BlockSpec((G, tq, HEAD_DIM), lambda h, i: (h, i, 0)),
                   pl.BlockSpec((G, None, 1, tq), lambda h, i: (h, i, 0, 0)), *x_out_specs),
        scratch_shapes=[pltpu.VMEM((G, 1, tq), F32), pltpu.VMEM((G, lanes, tq), F32)] + x_scr,
        compiler_params=_params(("arbitrary", "arbitrary")), name=name)(q_aug, k_aug, v_aug, *x_in)


def _fox_prep_bwd(dmix, o, *, name):
    T = dmix.shape[0]
    H = o.shape[0]
    tm = FOX_TILE
    lanes = 2 * HEAD_DIM
    first = 512 // lanes

    def body(d_ref, o_ref, da_ref):
        lane = lax.broadcasted_iota(jnp.int32, (tm, lanes), 1)
        d2 = d_ref[...].astype(F32)
        for e in range(2):
            de = d2 if e == 0 else pltpu.roll(d2, HEAD_DIM, 1)
            d64 = d_ref[:, e * HEAD_DIM:(e + 1) * HEAD_DIM].astype(F32)
            delta = jnp.sum(d64 * o_ref[e].astype(F32), axis=1, keepdims=True)
            da_ref[e] = _lanes(lane, de, LANE_KC, _split3(-delta), jnp.zeros((), F32)).astype(BF16)

    return pl.pallas_call(
        body, out_shape=jax.ShapeDtypeStruct((H, T, lanes), BF16), grid=(H // 2, T // tm),
        in_specs=[pl.BlockSpec((tm, lanes), lambda p, i: (i, first + p)),
                  pl.BlockSpec((2, tm, HEAD_DIM), lambda p, i: (p, i, 0))],
        out_specs=pl.BlockSpec((2, tm, lanes), lambda p, i: (p, i, 0)),
        compiler_params=_params(("parallel", "parallel")), name=name)(dmix, o)


def _fox_bwd(q_aug, k_aug, v_aug, do_aug, lse_row, *, ex=None, name):
    H, T, lanes = q_aug.shape
    tq = FOX_TILE
    nq = T // tq
    G = FOX_GROUP

    def body(q_ref, k_ref, v_ref, do_ref, lse_ref, dq_ref, dk_ref, dv_ref, dck_ref, dk_acc, dv_acc):
        j = pl.program_id(1)

        @pl.when(j == 0)
        def _():
            dq_ref[...] = jnp.zeros(dq_ref.shape, F32)

        dk_acc[...] = jnp.zeros(dk_acc.shape, F32)
        dv_acc[...] = jnp.zeros(dv_acc.shape, F32)

        def step(g, qb, diag):
            off = pl.multiple_of(qb * tq, tq)
            ka = k_ref[g]
            qa = q_ref[g, pl.ds(off, tq), :]
            da = do_ref[g, pl.ds(off, tq), :]
            s_t = lax.dot_general(ka, qa, NT, preferred_element_type=F32)
            p_t = jnp.exp(s_t - lse_ref[g, qb])
            if diag:
                r = lax.broadcasted_iota(jnp.int32, (tq, tq), 0)
                c = lax.broadcasted_iota(jnp.int32, (tq, tq), 1)
                p_t = jnp.where(c >= r, p_t, 0.0)
            dv_acc[g] += jnp.dot(p_t.astype(BF16), da, preferred_element_type=F32)
            dp_t = lax.dot_general(v_ref[g], da, NT, preferred_element_type=F32)
            dsb = (p_t * dp_t).astype(BF16)
            dk_acc[g] += jnp.dot(dsb, qa, preferred_element_type=F32)
            dq_ref[g, qb] += lax.dot_general(ka, dsb, TN, preferred_element_type=F32)

        for g in range(G):
            step(g, j, True)

        def loop_body(qb, carry):
            for g in range(G):
                step(g, qb, False)
            return carry

        lax.fori_loop(j + 1, nq, loop_body, 0)
        dk = dk_acc[...]
        dk_ref[...] = dk.astype(dk_ref.dtype)
        dck_ref[...] = -dk[:, :, LANE_KC:LANE_KC + 1]
        dv_ref[...] = dv_acc[...].astype(dv_ref.dtype)

    blk = pl.BlockSpec((G, tq, lanes), lambda h, j: (h, j, 0))
    full = pl.BlockSpec((G, T, lanes), lambda h, j: (h, 0, 0))
    grid = (H // G, nq)
    body, x_in, x_in_specs, x_out, x_out_specs, x_scr = _carry(ex, grid, 5, 4, body)
    return pl.pallas_call(
        body,
        out_shape=(jax.ShapeDtypeStruct((H, nq, lanes, tq), F32), jax.ShapeDtypeStruct((H, T, lanes), BF16),
                   jax.ShapeDtypeStruct((H, T, lanes), BF16), jax.ShapeDtypeStruct((H, T, 1), F32), *x_out),
        grid=grid,
        in_specs=[full, blk, blk, full, pl.BlockSpec((G, nq, 1, tq), lambda h, j: (h, 0, 0, 0))] + x_in_specs,
        out_specs=(pl.BlockSpec((G, nq, lanes, tq), lambda h, j: (h, 0, 0, 0)), blk, blk,
                   pl.BlockSpec((G, tq, 1), lambda h, j: (h, j, 0)), *x_out_specs),
        scratch_shapes=[pltpu.VMEM((G, tq, lanes), F32), pltpu.VMEM((G, tq, lanes), F32)] + x_scr,
        compiler_params=_params(("arbitrary", "arbitrary")), name=name,
    )(q_aug, k_aug, v_aug, do_aug, lse_row, *x_in)


def _t5_bucket_np(d):
    n = np.maximum(d, 0).astype(np.int32)
    max_exact = N_BUCKETS // 2
    nf = np.maximum(n, 1).astype(np.float32)
    large = max_exact + (np.log(nf / max_exact) / math.log(MAX_DISTANCE / max_exact)
                         * (N_BUCKETS - max_exact)).astype(np.int32)
    large = np.minimum(large, N_BUCKETS - 1)
    return np.where(n < max_exact, n, large)


def _bucket_onehots():
    r = np.arange(BLOCK)[:, None]
    c = np.arange(BLOCK)[None, :]
    eye = np.eye(N_BUCKETS, dtype=np.float32)
    cur = eye[_t5_bucket_np(r - c).reshape(-1)]
    prev = eye[_t5_bucket_np(BLOCK + r - c).reshape(-1)]
    return cur, prev


def _swa_probs(qs, kc, kp, km, bc, bp, far, sink, n):
    r = lax.broadcasted_iota(jnp.int32, (BLOCK, BLOCK), 0)
    c = lax.broadcasted_iota(jnp.int32, (BLOCK, BLOCK), 1)
    never = 2 * BLOCK
    s_c = lax.dot_general(qs, kc, NT, preferred_element_type=F32) + bc
    s_p = lax.dot_general(qs, kp, NT, preferred_element_type=F32) + bp
    s_m = lax.dot_general(qs, km, NT, preferred_element_type=F32) + jnp.where(n == 1, bp, far)
    s_c = jnp.where((c <= r) & (c >= jnp.where(n >= 1, 0, PAD_ROWS)), s_c, NEG)
    s_p = jnp.where(c > r + jnp.where(n >= 2, 0, never), s_p, NEG)
    s_m = jnp.where(c >= jnp.where(n >= 1, PAD_ROWS, never), s_m, NEG)
    m = jnp.maximum(jnp.maximum(jnp.max(s_c, axis=1, keepdims=True), jnp.max(s_p, axis=1, keepdims=True)),
                    jnp.maximum(jnp.max(s_m, axis=1, keepdims=True), sink))
    e_c = jnp.exp(s_c - m)
    e_p = jnp.exp(s_p - m)
    e_m = jnp.exp(s_m - m)
    e_s = jnp.exp(sink - m)
    l = (jnp.sum(e_c, axis=1, keepdims=True) + jnp.sum(e_p, axis=1, keepdims=True)
         + jnp.sum(e_m, axis=1, keepdims=True) + e_s)
    return e_c, e_p, e_m, e_s, l


def _swa_specs(T):
    G = SWA_GROUP
    qblk = pl.BlockSpec((G, BLOCK, HEAD_DIM), lambda kv, n: (kv, n, 0))
    cur = pl.BlockSpec((None, BLOCK, HEAD_DIM), lambda kv, n: (kv, n + 1, 0))
    prev = pl.BlockSpec((None, BLOCK, HEAD_DIM), lambda kv, n: (kv, n, 0))
    meta = pl.BlockSpec((None, BLOCK, HEAD_DIM), lambda kv, n: (kv, 1, 0))
    bias = pl.BlockSpec((G, BLOCK, BLOCK), lambda kv, n: (kv, 0, 0))
    smem = pl.BlockSpec(memory_space=pltpu.SMEM)
    return qblk, cur, prev, meta, bias, smem


def _swa_fwd(q, kpad, vpad, bc, bp, far, sinks, *, name):
    Hq, T, dh = q.shape
    nb = T // BLOCK
    G = SWA_GROUP

    def body(q_ref, kc_ref, kp_ref, km_ref, vc_ref, vp_ref, vm_ref, bc_ref, bp_ref, far_ref, sink_ref, o_ref):
        kv = pl.program_id(0)
        n = pl.program_id(1)
        kc, kp, km = kc_ref[...], kp_ref[...], km_ref[...]
        vc, vp, vm = vc_ref[...], vp_ref[...], vm_ref[...]
        for g in range(G):
            h = kv * G + g
            qs = q_ref[g] * SCALE
            e_c, e_p, e_m, _, l = _swa_probs(qs, kc, kp, km, bc_ref[g], bp_ref[g], far_ref[h], sink_ref[h], n)
            o = (jnp.dot(e_c.astype(BF16), vc, preferred_element_type=F32)
                 + jnp.dot(e_p.astype(BF16), vp, preferred_element_type=F32)
                 + jnp.dot(e_m.astype(BF16), vm, preferred_element_type=F32))
            o_ref[g] = (o / l).astype(o_ref.dtype)

    qblk, cur, prev, meta, bias, smem = _swa_specs(T)
    return pl.pallas_call(
        body, out_shape=jax.ShapeDtypeStruct((Hq, T, dh), BF16), grid=(SWA_KV_HEADS, nb),
        in_specs=[qblk, cur, prev, meta, cur, prev, meta, bias, bias, smem, smem],
        out_specs=qblk,
        compiler_params=_params(("parallel", "parallel")), name=name,
    )(q, kpad, kpad, kpad, vpad, vpad, vpad, bc, bp, far, sinks)


def _swa_bwd(q, kpad, vpad, do, bc, bp, far, sinks, *, ex=None, name):
    Hq, T, dh = q.shape
    nb = T // BLOCK
    G = SWA_GROUP

    def body(q_ref, kc_ref, kp_ref, km_ref, vc_ref, vp_ref, vm_ref, do_ref, bc_ref, bp_ref, far_ref, sink_ref,
             dq_ref, dk_ref, dv_ref, dbc_ref, dbp_ref, dbf_ref, dsk_ref):
        kv = pl.program_id(0)
        n = pl.program_id(1)

        @pl.when(n == 0)
        def _():
            for ref in (dk_ref, dv_ref, dbc_ref, dbp_ref, dbf_ref, dsk_ref):
                ref[...] = jnp.zeros(ref.shape, F32)

        kc, kp, km = kc_ref[...], kp_ref[...], km_ref[...]
        vc, vp, vm = vc_ref[...], vp_ref[...], vm_ref[...]
        dkc = dkp = dkm = dvc = dvp = dvm = jnp.zeros((BLOCK, dh), F32)
        for g in range(G):
            h = kv * G + g
            qs = q_ref[g] * SCALE
            e_c, e_p, e_m, e_s, l = _swa_probs(qs, kc, kp, km, bc_ref[g], bp_ref[g], far_ref[h], sink_ref[h], n)
            inv = 1.0 / l
            p_c, p_p, p_m = e_c * inv, e_p * inv, e_m * inv
            dob = do_ref[g]
            dp_c = lax.dot_general(dob, vc, NT, preferred_element_type=F32)
            dp_p = lax.dot_general(dob, vp, NT, preferred_element_type=F32)
            dp_m = lax.dot_general(dob, vm, NT, preferred_element_type=F32)
            delta = (jnp.sum(p_c * dp_c, axis=1, keepdims=True) + jnp.sum(p_p * dp_p, axis=1, keepdims=True)
                     + jnp.sum(p_m * dp_m, axis=1, keepdims=True))
            ds_c = p_c * (dp_c - delta)
            ds_p = p_p * (dp_p - delta)
            ds_m = p_m * (dp_m - delta)
            dsk_ref[g] += -(e_s * inv) * delta
            dbc_ref[g] += ds_c
            dbp_ref[g] += ds_p + jnp.where(n == 1, ds_m, 0.0)
            dbf_ref[g] += jnp.where(n >= 2, ds_m, 0.0)
            bc16, bp16, bm16 = ds_c.astype(BF16), ds_p.astype(BF16), ds_m.astype(BF16)
            dq = (jnp.dot(bc16, kc, preferred_element_type=F32) + jnp.dot(bp16, kp, preferred_element_type=F32)
                  + jnp.dot(bm16, km, preferred_element_type=F32))
            dq_ref[g] = (dq * SCALE).astype(dq_ref.dtype)
            dkc += lax.dot_general(bc16, qs, TN, preferred_element_type=F32)
            dkp += lax.dot_general(bp16, qs, TN, preferred_element_type=F32)
            dkm += lax.dot_general(bm16, qs, TN, preferred_element_type=F32)
            dvc += lax.dot_general(p_c.astype(BF16), dob, TN, preferred_element_type=F32)
            dvp += lax.dot_general(p_p.astype(BF16), dob, TN, preferred_element_type=F32)
            dvm += lax.dot_general(p_m.astype(BF16), dob, TN, preferred_element_type=F32)
        cur_off = pl.multiple_of((n + 1) * BLOCK, BLOCK)
        prev_off = pl.multiple_of(n * BLOCK, BLOCK)
        dk_ref[pl.ds(cur_off, BLOCK), :] += dkc
        dk_ref[pl.ds(prev_off, BLOCK), :] += dkp
        dk_ref[BLOCK:2 * BLOCK, :] += dkm
        dv_ref[pl.ds(cur_off, BLOCK), :] += dvc
        dv_ref[pl.ds(prev_off, BLOCK), :] += dvp
        dv_ref[BLOCK:2 * BLOCK, :] += dvm

    qblk, cur, prev, meta, bias, smem = _swa_specs(T)
    kvfull = pl.BlockSpec((None, T + BLOCK, dh), lambda kv, n: (kv, 0, 0))
    dsk = pl.BlockSpec((G, BLOCK, 1), lambda kv, n: (kv, 0, 0))
    grid = (SWA_KV_HEADS, nb)
    body, x_in, x_in_specs, x_out, x_out_specs, x_scr = _carry(ex, grid, 12, 7, body)
    return pl.pallas_call(
        body,
        out_shape=(jax.ShapeDtypeStruct((Hq, T, dh), BF16),
                   jax.ShapeDtypeStruct((SWA_KV_HEADS, T + BLOCK, dh), F32),
                   jax.ShapeDtypeStruct((SWA_KV_HEADS, T + BLOCK, dh), F32),
                   jax.ShapeDtypeStruct((Hq, BLOCK, BLOCK), F32), jax.ShapeDtypeStruct((Hq, BLOCK, BLOCK), F32),
                   jax.ShapeDtypeStruct((Hq, BLOCK, BLOCK), F32), jax.ShapeDtypeStruct((Hq, BLOCK, 1), F32), *x_out),
        grid=grid,
        in_specs=[qblk, cur, prev, meta, cur, prev, meta, qblk, bias, bias, smem, smem] + x_in_specs,
        out_specs=(qblk, kvfull, kvfull, bias, bias, bias, dsk, *x_out_specs),
        scratch_shapes=x_scr,
        compiler_params=_params(("arbitrary", "arbitrary")), name=name,
    )(q, kpad, kpad, kpad, vpad, vpad, vpad, do, bc, bp, far, sinks, *x_in)


def _small_grads(dbc, dbp, dbf, dsk, oh_cur, oh_prev, *, name):
    Hq = dbc.shape[0]

    def body(dbc_ref, dbp_ref, dbf_ref, dsk_ref, oc_ref, op_ref, tab_ref, sink_ref):
        tab = (jnp.dot(dbc_ref[...], oc_ref[...], precision=HIGHEST, preferred_element_type=F32)
               + jnp.dot(dbp_ref[...], op_ref[...], precision=HIGHEST, preferred_element_type=F32))
        far = jnp.sum(dbf_ref[...], axis=1, keepdims=True)
        last = lax.broadcasted_iota(jnp.int32, (Hq, N_BUCKETS), 1) == N_BUCKETS - 1
        tab_ref[...] = tab + jnp.where(last, far, 0.0)
        sink_ref[...] = jnp.sum(dsk_ref[...], axis=1, keepdims=True)

    vm = pl.BlockSpec(memory_space=pltpu.VMEM)
    return pl.pallas_call(
        body, out_shape=(jax.ShapeDtypeStruct((Hq, N_BUCKETS), F32), jax.ShapeDtypeStruct((Hq, 1), F32)),
        in_specs=[vm] * 6, out_specs=(vm, vm), compiler_params=_params(), name=name,
    )(dbc.reshape(Hq, -1), dbp.reshape(Hq, -1), dbf.reshape(Hq, -1), dsk.reshape(Hq, -1), oh_cur, oh_prev)


def _coords():
    return lax.axis_index("x"), lax.axis_index("y"), lax.axis_index("c")


class _Exchange:
    def __init__(self, inputs, out_shapes, scratch, start, finish):
        self.inputs, self.out_shapes, self.scratch, self.start, self.finish = inputs, out_shapes, scratch, start, finish


def _carry(ex, grid, n_in, n_out, body):
    if ex is None:
        return body, [], [], [], [], []
    ni, no = len(ex.inputs), len(ex.out_shapes)

    def at_step(which):
        cond = None
        for axis, n in enumerate(grid):
            c = pl.program_id(axis) == (0 if which == "first" else n - 1)
            cond = c if cond is None else cond & c
        return cond

    def wrapped(*refs):
        refs = list(refs)
        n_own_scr = len(refs) - (n_in + ni + n_out + no) - len(ex.scratch)
        own_in, side_in = refs[:n_in], refs[n_in:n_in + ni]
        own_out = refs[n_in + ni:n_in + ni + n_out]
        side_out = refs[n_in + ni + n_out:n_in + ni + n_out + no]
        rest = refs[n_in + ni + n_out + no:]
        own_scr, sems = rest[:n_own_scr], rest[n_own_scr:]

        @pl.when(at_step("first"))
        def _():
            ex.start(side_in, side_out, sems)

        body(*own_in, *own_out, *own_scr)

        @pl.when(at_step("last"))
        def _():
            ex.finish(side_in, side_out, sems)

    hbm = pl.BlockSpec(memory_space=pl.ANY)
    return wrapped, list(ex.inputs), [hbm] * ni, list(ex.out_shapes), [hbm] * no, list(ex.scratch)


def _run_exchange(ex, *, name):
    ni, no = len(ex.inputs), len(ex.out_shapes)

    def body(*refs):
        ins, outs, sems = refs[:ni], refs[ni:ni + no], refs[ni + no:]
        ex.start(ins, outs, sems)
        ex.finish(ins, outs, sems)

    hbm = pl.BlockSpec(memory_space=pl.ANY)
    return pl.pallas_call(
        body, out_shape=tuple(ex.out_shapes), in_specs=[hbm] * ni, out_specs=tuple([hbm] * no),
        scratch_shapes=ex.scratch, compiler_params=_params(), name=name)(*ex.inputs)


def _gather_exchange(shards):
    nt = len(shards)

    def copies(ins, outs, sems):
        send_sems, recv_sems, local_sems = sems
        x, y, c = _coords()
        me, sibling = (x, y, c), (x, y, 1 - c)
        chips = [(1 - x, y), (x, 1 - y), (1 - x, 1 - y)]

        def slot(t, dev):
            return outs[t].at[4 * dev[0] + 2 * dev[1] + dev[2]]

        def copy(t, k, block, to, src=None):
            dst = slot(t, block)
            return pltpu.make_async_remote_copy(
                src_ref=dst if src is None else src, dst_ref=dst,
                send_sem=send_sems.at[t, k], recv_sem=recv_sems.at[t, k], device_id=to, device_id_type=MESH)

        mine = [pltpu.make_async_copy(ins[t], slot(t, me), local_sems.at[t]) for t in range(nt)]
        first = []
        for t in range(nt):
            first.append(copy(t, 0, me, sibling, src=ins[t]))
            first += [copy(t, 1 + j, me, (*chip, c), src=ins[t]) for j, chip in enumerate(chips)]
        return copy, mine, first, me, sibling, chips, c

    def start(ins, outs, sems):
        _, mine, first, *_ = copies(ins, outs, sems)
        for cp in mine + first:
            cp.start()

    def finish(ins, outs, sems):
        copy, mine, first, me, sibling, chips, c = copies(ins, outs, sems)
        passed = []
        for j, chip in enumerate(chips):
            for t in range(nt):
                copy(t, 1 + j, (*chip, c), me).wait_recv()
                cp = copy(t, 4 + j, (*chip, c), sibling)
                cp.start()
                passed.append(cp)
        for t in range(nt):
            copy(t, 0, sibling, me).wait_recv()
            for j, chip in enumerate(chips):
                copy(t, 4 + j, (*chip, 1 - c), me).wait_recv()
        for cp in first + passed:
            cp.wait_send()
        for cp in mine:
            cp.wait()

    return _Exchange(
        list(shards), [jax.ShapeDtypeStruct((N_DEV,) + s.shape, s.dtype) for s in shards],
        [pltpu.SemaphoreType.DMA((nt, 7)), pltpu.SemaphoreType.DMA((nt, 7)), pltpu.SemaphoreType.DMA((nt,))],
        start, finish)


def _swap_exchange(arrays, n_slices, copies):
    nt = len(arrays)

    def start(ins, outs, sems):
        for cp in copies(ins, outs, sems):
            cp.start()

    def finish(ins, outs, sems):
        sends = copies(ins, outs, sems)
        for cp in sends:
            cp.wait_recv()
        for cp in sends:
            cp.wait_send()

    return _Exchange(
        list(arrays), [jax.ShapeDtypeStruct((n_slices,) + a.shape[1:], a.dtype) for a in arrays],
        [pltpu.SemaphoreType.DMA((nt, n_slices)), pltpu.SemaphoreType.DMA((nt, n_slices))], start, finish)


def _cores_exchange(gs):
    def copies(ins, outs, sems):
        send_sems, recv_sems = sems
        x, y, c = _coords()
        return [pltpu.make_async_remote_copy(
            src_ref=ins[t].at[2 * j + (1 - c)], dst_ref=outs[t].at[j],
            send_sem=send_sems.at[t, j], recv_sem=recv_sems.at[t, j], device_id=(x, y, 1 - c), device_id_type=MESH)
            for t in range(len(gs)) for j in range(4)]

    return _swap_exchange(gs, 4, copies)


def _chips_exchange(ps):
    def copies(ins, outs, sems):
        send_sems, recv_sems = sems
        x, y, c = _coords()
        peers = [(1 - x, y), (x, 1 - y), (1 - x, 1 - y)]
        return [pltpu.make_async_remote_copy(
            src_ref=ins[t].at[2 * px + py], dst_ref=outs[t].at[k],
            send_sem=send_sems.at[t, k], recv_sem=recv_sems.at[t, k], device_id=(px, py, c), device_id_type=MESH)
            for t in range(len(ps)) for k, (px, py) in enumerate(peers)]

    return _swap_exchange(ps, 3, copies)


def _add_cores(g, r, core, *, name):
    _, A, B = g.shape
    ta = _tile(A, 512, 16)

    def body(core_ref, a_ref, b_ref, o_ref, o16_ref):
        s = a_ref[...] + b_ref[...]
        o_ref[...] = s
        o16_ref[...] = s.astype(BF16)

    blk = (None, ta, B)
    out = pl.BlockSpec(blk, lambda j, i, core_ref: (j, i, 0))
    return pl.pallas_call(
        body, out_shape=(jax.ShapeDtypeStruct((4, A, B), F32), jax.ShapeDtypeStruct((4, A, B), BF16)),
        grid_spec=pltpu.PrefetchScalarGridSpec(
            num_scalar_prefetch=1, grid=(4, A // ta),
            in_specs=[pl.BlockSpec(blk, lambda j, i, core_ref: (2 * j + core_ref[0], i, 0)),
                      pl.BlockSpec(blk, lambda j, i, core_ref: (j, i, 0))],
            out_specs=(out, out)),
        compiler_params=_params(("parallel", "parallel")), name=name)(core, g, r)


def _adamw_math(w, g, m, v):
    m = ADAM_B1 * m + (1.0 - ADAM_B1) * g
    v = ADAM_B2 * v + (1.0 - ADAM_B2) * (g * g)
    m_hat = m / (1.0 - ADAM_B1 ** ADAM_STEP)
    v_hat = v / (1.0 - ADAM_B2 ** ADAM_STEP)
    delta = -ADAM_LR * (m_hat / (jnp.sqrt(v_hat) + ADAM_EPS) + ADAM_WD * w)
    return delta, m, v


def _sum_adamw(p, r, chip, w, m, v, *, segs, ta, name):
    Aw, Bw = w.shape
    Bg = p.shape[2]
    assert Aw % ta == 0

    def body(chip_ref, p_ref, r0, r1, r2, w_ref, m_ref, v_ref, g_out, d_out, m_out, v_out):
        for gc, wc, n in segs:
            g = ((p_ref[:, gc:gc + n] + r0[:, gc:gc + n].astype(F32)) + r1[:, gc:gc + n].astype(F32)
                 ) + r2[:, gc:gc + n].astype(F32)
            delta, m_new, v_new = _adamw_math(w_ref[:, wc:wc + n], g, m_ref[:, wc:wc + n], v_ref[:, wc:wc + n])
            g_out[:, wc:wc + n] = g
            d_out[:, wc:wc + n] = delta
            m_out[:, wc:wc + n] = m_new
            v_out[:, wc:wc + n] = v_new

    gblk = (None, ta, Bg)
    row = pl.BlockSpec((ta, Bw), lambda i, chip_ref: (i, 0))
    rspecs = [pl.BlockSpec(gblk, (lambda i, chip_ref, k=k: (k, i, 0))) for k in range(3)]
    shp = jax.ShapeDtypeStruct((Aw, Bw), F32)
    return pl.pallas_call(
        body, out_shape=(shp, shp, shp, shp),
        grid_spec=pltpu.PrefetchScalarGridSpec(
            num_scalar_prefetch=1, grid=(Aw // ta,),
            in_specs=[pl.BlockSpec(gblk, lambda i, chip_ref: (chip_ref[0], i, 0))] + rspecs + [row, row, row],
            out_specs=(row, row, row, row)),
        compiler_params=_params(("parallel",)), name=name)(chip, p, r, r, r, w, m, v)


def _adamw(w, g, m, v, *, name):
    def body(w_ref, g_ref, m_ref, v_ref, d_out, m_out, v_out):
        delta, m_new, v_new = _adamw_math(w_ref[...], g_ref[...], m_ref[...], v_ref[...])
        d_out[...] = delta
        m_out[...] = m_new
        v_out[...] = v_new

    vm = pl.BlockSpec(memory_space=pltpu.VMEM)
    shp = jax.ShapeDtypeStruct(w.shape, F32)
    return pl.pallas_call(body, out_shape=(shp, shp, shp), in_specs=[vm] * 4, out_specs=(vm, vm, vm),
                          compiler_params=_params(), name=name)(w, g, m, v)


def _small_allreduce_adamw(s, w, m, v, *, name):
    R, W = s.shape

    def body(s_ref, w_ref, m_ref, v_ref, g_out, d_out, m_out, v_out, gath, send_sems, recv_sems):
        x, y, c = _coords()
        mine = 4 * x + 2 * y + c
        gath[mine] = s_ref[...]
        peers = [((1 - x) if k & 4 else x, (1 - y) if k & 2 else y, (1 - c) if k & 1 else c) for k in range(1, N_DEV)]
        sends = []
        for k in range(1, N_DEV):
            peer = peers[k - 1]
            sends.append(pltpu.make_async_remote_copy(
                src_ref=s_ref, dst_ref=gath.at[mine], send_sem=send_sems.at[k - 1], recv_sem=recv_sems.at[k - 1],
                device_id=peer, device_id_type=MESH))
        for cp in sends:
            cp.start()
        for k in range(1, N_DEV):
            peer = peers[k - 1]
            pltpu.make_async_remote_copy(
                src_ref=s_ref, dst_ref=gath.at[4 * peer[0] + 2 * peer[1] + peer[2]],
                send_sem=send_sems.at[k - 1], recv_sem=recv_sems.at[k - 1],
                device_id=peer, device_id_type=MESH).wait_recv()
        for cp in sends:
            cp.wait_send()
        g = gath[0]
        for d in range(1, N_DEV):
            g = g + gath[d]
        delta, m_new, v_new = _adamw_math(w_ref[...], g, m_ref[...], v_ref[...])
        g_out[...] = g
        d_out[...] = delta
        m_out[...] = m_new
        v_out[...] = v_new

    vm = pl.BlockSpec(memory_space=pltpu.VMEM)
    shp = jax.ShapeDtypeStruct((R, W), F32)
    return pl.pallas_call(
        body, out_shape=(shp, shp, shp, shp), in_specs=[vm] * 4, out_specs=(vm, vm, vm, vm),
        scratch_shapes=[pltpu.VMEM((N_DEV, R, W), F32), pltpu.SemaphoreType.DMA((N_DEV - 1,)),
                        pltpu.SemaphoreType.DMA((N_DEV - 1,))],
        compiler_params=_params(), name=name)(s, w, m, v)


def _pack_small(rel_bias, g1, g2, g3, g4, b_forget, sinks, extra=None, meta=None):
    misc = jnp.concatenate([rel_bias.reshape(-1), b_forget.reshape(-1), sinks.reshape(-1)])
    misc = jnp.concatenate([misc, jnp.zeros((D_MODEL - misc.shape[0],), F32)])[None]
    last = jnp.zeros((1, D_MODEL), F32) if extra is None else extra
    meta = jnp.zeros((N_META, D_MODEL), F32) if meta is None else meta
    return jnp.concatenate([g1, g2, g3, g4, misc, last, jnp.zeros((2, D_MODEL), F32), meta], axis=0)


def _unpack_small(p):
    nrb = N_BUCKETS * SWA_Q_HEADS
    misc = p[4]
    return dict(rel_bias=misc[:nrb].reshape(N_BUCKETS, SWA_Q_HEADS), ln_pre_mix=p[0:1], ln_post_mix=p[1:2],
                ln_pre_ffn=p[2:3], ln_post_ffn=p[3:4], b_forget=misc[nrb:nrb + 8].reshape(1, 8),
                sinks=misc[nrb + 8:nrb + 16].reshape(1, 8))


def _heads(a, n):
    return a.reshape(a.shape[0], n, HEAD_DIM).transpose(1, 0, 2)


def _unheads(a):
    return a.transpose(1, 0, 2).reshape(a.shape[1], -1)


def kernel(x, meta_tokens, rel_bias, ln_pre_mix, ln_post_mix, ln_pre_ffn, ln_post_ffn, w_in, b_forget, sinks, w_out, w_gate_up, w_down, loss_target, m_meta_tokens, m_rel_bias, m_ln_pre_mix, m_ln_post_mix, m_ln_pre_ffn, m_ln_post_ffn, m_w_in, m_b_forget, m_sinks, m_w_out, m_w_gate_up, m_w_down, v_meta_tokens, v_rel_bias, v_ln_pre_mix, v_ln_post_mix, v_ln_pre_ffn, v_ln_post_ffn, v_w_in, v_b_forget, v_sinks, v_w_out, v_w_gate_up, v_w_down):
    seq = x.shape[1]
    T = BLOCK + seq
    assert T % FOX_TILE == 0
    nq = T // FOX_TILE
    tm = _tile(T, 1056)
    cin = w_in.shape[2]
    hid = w_down.shape[1]
    assert w_gate_up.shape[2] == 2 * hid and cin <= W_IN_PAD and hid <= HID_PAD

    x_i, y_i, c_i = _coords()
    core = jnp.reshape(c_i, (1,)).astype(jnp.int32)
    chip = jnp.reshape(2 * x_i + y_i, (1,)).astype(jnp.int32)
    w_in_s = jnp.pad(w_in[0].astype(BF16), ((0, 0), (0, W_IN_PAD - cin)))
    w_gu_s = jnp.pad(w_gate_up[0].astype(BF16).reshape(D_MODEL, 2, hid), ((0, 0), (0, 0), (0, HID_PAD - hid)))
    w_gu_s = w_gu_s.reshape(D_MODEL, 2 * HID_PAD)
    w_down_s = jnp.pad(w_down[0].astype(BF16), ((0, HID_PAD - hid), (0, 0)))
    g_in, g_meta = _run_exchange(_gather_exchange([w_in_s, meta_tokens]), name="ag_w_in")
    gather_rest = _gather_exchange([w_out[0].astype(BF16), w_gu_s, w_down_s])
    w_in_full = g_in[:, :, :cin].transpose(1, 0, 2).reshape(D_MODEL, N_DEV * cin)
    w_qkv = w_in_full[:, :D_QKV]
    w_f = jnp.pad(w_in_full[:, D_QKV:], ((0, 0), (0, BLOCK - FOX_HEADS)))
    w_in_cat = jnp.concatenate([w_qkv, w_f, jnp.zeros((D_MODEL, D_PROJ_PAD - D_QKV - BLOCK), BF16)], axis=1)
    meta_full = g_meta.transpose(1, 0, 2).reshape(N_META, D_MODEL)

    h0 = jnp.concatenate([jnp.zeros((PAD_ROWS, D_MODEL), F32), meta_full, x[0]], axis=0)
    target = jnp.concatenate([jnp.zeros((BLOCK, D_MODEL), F32), loss_target[0]], axis=0)
    hn1 = _rms_fwd(h0, ln_pre_mix, name="rms_pre_mix")
    proj = _matmul(hn1, w_qkv, out_dtype=BF16, tm=tm, tn=768, name="mm_in_proj")
    proj_f = _matmul(hn1, w_f, out_dtype=F32, tm=tm, tn=BLOCK, name="mm_in_proj_f")

    q_a = _heads(proj[:, 0:512], 8)
    k_a = jnp.pad(_heads(proj[:, 512:640], 2), ((0, 0), (BLOCK, 0), (0, 0)))
    v_a = jnp.pad(_heads(proj[:, 640:768], 2), ((0, 0), (BLOCK, 0), (0, 0)))
    f_t = proj_f[:, :FOX_HEADS].T
    bf_col = b_forget.reshape(FOX_HEADS, 1)

    oh_cur, oh_prev = _bucket_onehots()
    bias_c = jnp.einsum("pb,bh->hp", jnp.asarray(oh_cur), rel_bias, precision=HIGHEST).reshape(8, BLOCK, BLOCK)
    bias_p = jnp.einsum("pb,bh->hp", jnp.asarray(oh_prev), rel_bias, precision=HIGHEST).reshape(8, BLOCK, BLOCK)
    far = rel_bias[N_BUCKETS - 1]
    sink_v = sinks[0]
    o_a = _swa_fwd(q_a, k_a, v_a, bias_c, bias_p, far, sink_v, name="swa_fwd")

    _, cum_col = _fox_gates_fwd(f_t, bf_col, name="fox_gates_fwd")
    q_b, k_b, v_b = _fox_prep(proj, cum_col, name="fox_prep")
    o_b, lse_row, g_out, g_gu, g_down = _fox_fwd(q_b, k_b, v_b, ex=gather_rest, name="fox_fwd")
    w_out_full = g_out.reshape(D_MODEL, D_MODEL)
    w_down_full = g_down.reshape(N_DEV * HID_PAD, D_MODEL)

    mix = jnp.concatenate([_unheads(o_a), _unheads(o_b)], axis=1)
    a1 = _matmul(mix, w_out_full, out_dtype=F32, tm=tm, tn=512, name="mm_out_proj")
    h1 = _post_res(a1, ln_post_mix, h0, name="post_mix")
    hn2 = _rms_fwd(h1, ln_pre_ffn, name="rms_pre_ffn")
    gu = _matmul(hn2, g_gu, b_shards=True, out_dtype=BF16, tm=tm, name="mm_gate_up")
    act = _swiglu_fwd(gu, name="swiglu_fwd")
    ff = _matmul(act, w_down_full, out_dtype=F32, tm=tm, tn=512, name="mm_down")
    dh2, loss_acc = _loss_head(ff, ln_post_ffn, h1, target, name="loss_head")

    dff, dg_post_ffn = _rms_bwd(ff, ln_post_ffn, dh2, None, out_dtype=BF16, name="rms_bwd_post_ffn")
    dact = _matmul(dff, w_down_full, nt=True, out_dtype=BF16, tm=tm, tn=1536, name="mm_d_act")
    d_w_down = _matmul(act.T, dff, out_dtype=F32, tm=768, tn=512, name="mm_dw_down")
    dgu = _swiglu_bwd(gu, dact, name="swiglu_bwd")
    dhn2 = _matmul(dgu, g_gu, nt=True, b_shards=True, out_dtype=F32, tm=tm, tn=D_MODEL, name="mm_d_hn2")
    d_w_gu = _matmul(hn2.T, dgu, out_shards=True, out_dtype=F32, tm=512, tn=2 * HID_PAD, name="mm_dw_gate_up")
    dh1, dg_pre_ffn = _rms_bwd(h1, ln_pre_ffn, dhn2, dh2, out_dtype=F32, name="rms_bwd_pre_ffn")
    da1, dg_post_mix = _rms_bwd(a1, ln_post_mix, dh1, None, out_dtype=BF16, name="rms_bwd_post_mix")
    dmix = _matmul(da1, w_out_full, nt=True, out_dtype=BF16, tm=tm, tn=512, name="mm_d_mix")
    d_w_out = _matmul(mix.T, da1, out_dtype=F32, tm=512, tn=512, name="mm_dw_out")

    ffn_grads = [d_w_gu, d_w_down.reshape(N_DEV, HID_PAD, D_MODEL)]
    do_a = _heads(dmix[:, :512], 8)
    dq_a, dk_a, dv_a, dbc, dbp, dbf, dsk, *ffn_sibling = _swa_bwd(
        q_a, k_a, v_a, do_a, bias_c, bias_p, far, sink_v, ex=_cores_exchange(ffn_grads), name="swa_bwd")
    d_tab, d_sink = _small_grads(dbc, dbp, dbf, dsk, jnp.asarray(oh_cur), jnp.asarray(oh_prev), name="small_grads")
    ffn_sums = [_add_cores(g, r, core, name="rs_add_" + t)
                for g, r, t in zip(ffn_grads, ffn_sibling, ["w_gate_up", "w_down"])]

    do_b = _fox_prep_bwd(dmix, o_b, name="fox_prep_bwd")
    dq_t, dk_b, dv_b, dck, *ffn_chips = _fox_bwd(
        q_b, k_b, v_b, do_b, lse_row, ex=_chips_exchange([s[1] for s in ffn_sums]), name="fox_bwd")
    dcq = dq_t[:, :, LANE_QC, :].reshape(FOX_HEADS, T)
    df_t, d_bf = _fox_gates_bwd(dcq, dck.reshape(FOX_HEADS, T), f_t, bf_col, name="fox_gates_bwd")
    dq_b = (dq_t[:, :, :HEAD_DIM, :].transpose(1, 3, 0, 2).reshape(T, FOX_W) * SCALE).astype(BF16)
    dk_b = dk_b[:, :, :HEAD_DIM].transpose(1, 0, 2).reshape(T, FOX_W)
    dv_b = dv_b[:, :, :HEAD_DIM].transpose(1, 0, 2).reshape(T, FOX_W)

    dproj = jnp.concatenate([
        _unheads(dq_a), _unheads(dk_a[:, BLOCK:]).astype(BF16), _unheads(dv_a[:, BLOCK:]).astype(BF16),
        dq_b, dk_b, dv_b, df_t.T.astype(BF16), jnp.zeros((T, D_PROJ_PAD - D_PROJ), BF16)], axis=1)
    dhn1 = _matmul(dproj, w_in_cat, nt=True, out_dtype=F32, tm=tm, tn=512, name="mm_d_hn1")
    dproj_s = jnp.pad(dproj[:, :N_DEV * cin].reshape(T, N_DEV, cin), ((0, 0), (0, 0), (0, W_IN_PAD - cin)))
    d_w_in = _matmul(hn1.T, dproj_s.reshape(T, N_DEV * W_IN_PAD), out_shards=True, out_dtype=F32, tm=512,
                     tn=W_IN_PAD, name="mm_dw_in")
    dh0, dg_pre_mix = _rms_bwd(h0, ln_pre_mix, dhn1, dh1, out_dtype=F32, name="rms_bwd_pre_mix")
    grad_x = dh0[BLOCK:][None]
    d_meta = dh0[PAD_ROWS:BLOCK]

    mix_grads = [d_w_in, d_w_out.reshape(N_DEV, -1, D_MODEL)]
    tags = ["w_in", "w_out", "w_gate_up", "w_down"]
    mix_sibling = _run_exchange(_cores_exchange(mix_grads), name="rs_cores")
    mix_sums = [_add_cores(g, r, core, name="rs_add_" + t) for g, r, t in zip(mix_grads, mix_sibling, tags[:2])]
    mix_chips = _run_exchange(_chips_exchange([s[1] for s in mix_sums]), name="rs_chips")
    chip_sum = [s[0] for s in mix_sums + ffn_sums]
    from_chips = list(mix_chips) + list(ffn_chips)
    shard_w = [(w_in, m_w_in, v_w_in), (w_out, m_w_out, v_w_out), (w_gate_up, m_w_gate_up, v_w_gate_up),
               (w_down, m_w_down, v_w_down)]
    segs = [[(0, 0, cin)], [(0, 0, D_MODEL)], [(0, 0, hid), (HID_PAD, hid, hid)], [(0, 0, D_MODEL)]]
    tas = [256, BLOCK, 256, hid]
    big = [{}, {}, {}, {}]
    for i, t in enumerate(tags):
        w_t, m_t, v_t = shard_w[i]
        res = _sum_adamw(chip_sum[i], from_chips[i], chip, w_t[0], m_t[0], v_t[0], segs=segs[i], ta=tas[i],
                         name="rs_adamw_" + t)
        for kind in range(4):
            big[kind][t] = res[kind][None]

    loss_row = jnp.pad(loss_acc[0:1, 0:1] * (0.5 / D_MODEL), ((0, 0), (0, D_MODEL - 1)))
    s_small = _pack_small(d_tab.T, dg_pre_mix, dg_post_mix, dg_pre_ffn, dg_post_ffn, d_bf, d_sink,
                          extra=loss_row, meta=d_meta)
    w_s = _pack_small(rel_bias, ln_pre_mix, ln_post_mix, ln_pre_ffn, ln_post_ffn, b_forget, sinks)
    m_s = _pack_small(m_rel_bias, m_ln_pre_mix, m_ln_post_mix, m_ln_pre_ffn, m_ln_post_ffn, m_b_forget, m_sinks)
    v_s = _pack_small(v_rel_bias, v_ln_pre_mix, v_ln_post_mix, v_ln_pre_ffn, v_ln_post_ffn, v_b_forget, v_sinks)
    small = _small_allreduce_adamw(s_small, w_s, m_s, v_s, name="small_allreduce_adamw")
    loss = small[0][5, 0]
    mcols = meta_tokens.shape[1]
    g_meta_mine = lax.dynamic_slice(small[0][8:8 + N_META], (0, (4 * x_i + 2 * y_i + c_i) * mcols), (N_META, mcols))
    big[0]["meta_tokens"] = g_meta_mine
    for kind, arr in enumerate(_adamw(meta_tokens, g_meta_mine, m_meta_tokens, v_meta_tokens, name="adamw_meta")):
        big[kind + 1]["meta_tokens"] = arr
    small = [_unpack_small(p) for p in small]

    names = ["meta_tokens", "rel_bias", "ln_pre_mix", "ln_post_mix", "ln_pre_ffn", "ln_post_ffn", "w_in",
             "b_forget", "sinks", "w_out", "w_gate_up", "w_down"]
    outs = [loss, grad_x]
    for kind in range(4):
        for nme in names:
            outs.append(big[kind][nme] if nme in big[kind] else small[kind][nme])
    return tuple(outs)
```

```python
import math

import numpy as np
import jax
import jax.numpy as jnp
from jax import lax
from jax.experimental import pallas as pl
from jax.experimental.pallas import tpu as pltpu

F32 = jnp.float32
BF16 = jnp.bfloat16
HIGHEST = lax.Precision.HIGHEST
MESH = pl.DeviceIdType.MESH

N_DEV = 8
D_MODEL = 1024
N_META = 16
HEAD_DIM = 64
SWA_Q_HEADS = 8
SWA_KV_HEADS = 2
SWA_GROUP = 4
FOX_HEADS = 8
FOX_W = FOX_HEADS * HEAD_DIM
BLOCK = 128
PAD_ROWS = BLOCK - N_META
N_BUCKETS = 32
MAX_DISTANCE = 128
D_FF = 2816
D_QKV = 2304
D_PROJ = D_QKV + FOX_HEADS
D_PROJ_PAD = 2560
EPS = 1e-6
NEG = -1e30
SCALE = HEAD_DIM ** -0.5
ADAM_LR, ADAM_B1, ADAM_B2, ADAM_EPS, ADAM_WD, ADAM_STEP = 0.001, 0.9, 0.999, 1e-08, 0.01, 10
VMEM_LIMIT = 48 * 1024 * 1024
FOX_TILE = 384
FOX_GROUP = 4
W_IN_PAD = 384
HID_PAD = 384

NT = (((1,), (1,)), ((), ()))
NN = (((1,), (0,)), ((), ()))
TN = (((0,), (0,)), ((), ()))


def _params(sem=None, **kw):
    if sem is not None:
        kw["dimension_semantics"] = sem
    return pltpu.CompilerParams(vmem_limit_bytes=VMEM_LIMIT, **kw)


def _tile(n, target, mult=16):
    best = None
    for t in range(mult, min(n, target) + 1, mult):
        if n % t == 0:
            best = t
    assert best is not None, (n, target)
    return best


def _matmul(a, b, *, nt=False, b_shards=False, out_shards=False, out_dtype, tm, tn=None, tk=None, name):
    M, K = a.shape
    if b_shards and nt:
        N, tk = b.shape[1], b.shape[2]
    elif b_shards:
        N, tn = b.shape[0] * b.shape[2], b.shape[2]
    else:
        N = b.shape[0] if nt else b.shape[1]
    tk = K if tk is None else tk
    assert M % tm == 0 and N % tn == 0 and K % tk == 0, (name, a.shape, b.shape, tm, tn, tk)
    nk = K // tk
    dn = NT if nt else NN

    def body(a_ref, b_ref, o_ref, *scr):
        part = lax.dot_general(a_ref[...], b_ref[...], dn, preferred_element_type=F32)
        if nk == 1:
            o_ref[...] = part.astype(o_ref.dtype)
        else:
            acc = scr[0]
            k = pl.program_id(2)

            @pl.when(k == 0)
            def _():
                acc[...] = part

            @pl.when(k > 0)
            def _():
                acc[...] += part

            @pl.when(k == nk - 1)
            def _():
                o_ref[...] = acc[...].astype(o_ref.dtype)

    if b_shards and nt:
        b_spec = pl.BlockSpec((None, tn, tk), lambda i, j, k: (k, j, 0))
    elif b_shards:
        b_spec = pl.BlockSpec((None, tk, tn), lambda i, j, k: (j, k, 0))
    elif nt:
        b_spec = pl.BlockSpec((tn, tk), lambda i, j, k: (j, k))
    else:
        b_spec = pl.BlockSpec((tk, tn), lambda i, j, k: (k, j))
    if out_shards:
        out_shape = jax.ShapeDtypeStruct((N // tn, M, tn), out_dtype)
        out_spec = pl.BlockSpec((None, tm, tn), lambda i, j, k: (j, i, 0))
    else:
        out_shape = jax.ShapeDtypeStruct((M, N), out_dtype)
        out_spec = pl.BlockSpec((tm, tn), lambda i, j, k: (i, j))
    return pl.pallas_call(
        body,
        out_shape=out_shape,
        grid=(M // tm, N // tn, nk),
        in_specs=[pl.BlockSpec((tm, tk), lambda i, j, k: (i, k)), b_spec],
        out_specs=out_spec,
        scratch_shapes=[pltpu.VMEM((tm, tn), F32)] if nk > 1 else [],
        compiler_params=_params(("parallel", "parallel", "arbitrary")),
        name=name,
    )(a, b)


def _rstd(x):
    return lax.rsqrt(jnp.mean(x * x, axis=-1, keepdims=True) + EPS)


def _rms_fwd(x, g, *, name):
    T, D = x.shape
    tm = _tile(T, 512)

    def body(x_ref, g_ref, o_ref):
        x = x_ref[...]
        o_ref[...] = (x * _rstd(x) * g_ref[...]).astype(o_ref.dtype)

    return pl.pallas_call(
        body, out_shape=jax.ShapeDtypeStruct((T, D), BF16), grid=(T // tm,),
        in_specs=[pl.BlockSpec((tm, D), lambda i: (i, 0)), pl.BlockSpec((1, D), lambda i: (0, 0))],
        out_specs=pl.BlockSpec((tm, D), lambda i: (i, 0)),
        compiler_params=_params(("parallel",)), name=name)(x, g)


def _post_res(a, g, h, *, name):
    T, D = a.shape
    tm = _tile(T, 512)

    def body(a_ref, g_ref, h_ref, o_ref):
        a = a_ref[...]
        o_ref[...] = h_ref[...] + a * _rstd(a) * g_ref[...]

    row = pl.BlockSpec((tm, D), lambda i: (i, 0))
    return pl.pallas_call(
        body, out_shape=jax.ShapeDtypeStruct((T, D), F32), grid=(T // tm,),
        in_specs=[row, pl.BlockSpec((1, D), lambda i: (0, 0)), row], out_specs=row,
        compiler_params=_params(("parallel",)), name=name)(a, g, h)


def _loss_head(a, g, h, target, *, name):
    T, D = a.shape
    tm = _tile(T, 512)

    def body(a_ref, g_ref, h_ref, t_ref, dy_ref, loss_ref):
        i = pl.program_id(0)
        a = a_ref[...]
        y = h_ref[...] + a * _rstd(a) * g_ref[...]
        rows = i * tm + lax.broadcasted_iota(jnp.int32, (tm, 1), 0)
        err = jnp.where(rows >= BLOCK, y - t_ref[...], 0.0)
        dy_ref[...] = err / D
        part = jnp.sum(jnp.sum(err * err, axis=1, keepdims=True), axis=0, keepdims=True)

        @pl.when(i == 0)
        def _():
            loss_ref[...] = jnp.zeros_like(loss_ref)

        loss_ref[...] += jnp.broadcast_to(part, loss_ref.shape)

    row = pl.BlockSpec((tm, D), lambda i: (i, 0))
    return pl.pallas_call(
        body, out_shape=(jax.ShapeDtypeStruct((T, D), F32), jax.ShapeDtypeStruct((8, 128), F32)),
        grid=(T // tm,),
        in_specs=[row, pl.BlockSpec((1, D), lambda i: (0, 0)), row, row],
        out_specs=(row, pl.BlockSpec((8, 128), lambda i: (0, 0))),
        compiler_params=_params(("arbitrary",)), name=name)(a, g, h, target)


def _rms_bwd(x, g, dy, res, *, out_dtype, name):
    T, D = x.shape
    tm = _tile(T, 512)
    has_res = res is not None

    def body(*refs):
        if has_res:
            x_ref, g_ref, dy_ref, r_ref, dx_ref, dg_ref = refs
        else:
            x_ref, g_ref, dy_ref, dx_ref, dg_ref = refs
        i = pl.program_id(0)
        x = x_ref[...]
        dy = dy_ref[...].astype(F32)
        r = _rstd(x)
        xh = x * r
        dxh = dy * g_ref[...]
        dx = r * (dxh - xh * jnp.mean(dxh * xh, axis=-1, keepdims=True))
        if has_res:
            dx = dx + r_ref[...]
        dx_ref[...] = dx.astype(dx_ref.dtype)

        @pl.when(i == 0)
        def _():
            dg_ref[...] = jnp.zeros_like(dg_ref)

        dg_ref[...] += jnp.sum(dy * xh, axis=0, keepdims=True)

    row = pl.BlockSpec((tm, D), lambda i: (i, 0))
    vec = pl.BlockSpec((1, D), lambda i: (0, 0))
    ins = [x, g, dy] + ([res] if has_res else [])
    return pl.pallas_call(
        body, out_shape=(jax.ShapeDtypeStruct((T, D), out_dtype), jax.ShapeDtypeStruct((1, D), F32)),
        grid=(T // tm,),
        in_specs=[row, vec, row] + ([row] if has_res else []),
        out_specs=(row, vec),
        compiler_params=_params(("arbitrary",)), name=name)(*ins)


def _swiglu_fwd(gu, *, name):
    T, F2 = gu.shape
    F = F2 // 2
    tm = _tile(T, 384)

    def body(g_ref, u_ref, o_ref):
        g = g_ref[...].astype(F32)
        o_ref[...] = (g / (1.0 + jnp.exp(-g)) * u_ref[...].astype(F32)).astype(o_ref.dtype)

    return pl.pallas_call(
        body, out_shape=jax.ShapeDtypeStruct((T, F), BF16), grid=(T // tm,),
        in_specs=[pl.BlockSpec((tm, F), lambda i: (i, 0)), pl.BlockSpec((tm, F), lambda i: (i, 1))],
        out_specs=pl.BlockSpec((tm, F), lambda i: (i, 0)),
        compiler_params=_params(("parallel",)), name=name)(gu, gu)


def _swiglu_bwd(gu, dact, *, name):
    T, F2 = gu.shape
    F = F2 // 2
    tm = _tile(T, 384)

    def body(g_ref, u_ref, d_ref, o_ref):
        g = g_ref[...].astype(F32)
        u = u_ref[...].astype(F32)
        d = d_ref[...].astype(F32)
        sg = 1.0 / (1.0 + jnp.exp(-g))
        o_ref[:, :F] = (d * u * (sg * (1.0 + g * (1.0 - sg)))).astype(o_ref.dtype)
        o_ref[:, F:] = (d * (g * sg)).astype(o_ref.dtype)

    return pl.pallas_call(
        body, out_shape=jax.ShapeDtypeStruct((T, F2), BF16), grid=(T // tm,),
        in_specs=[pl.BlockSpec((tm, F), lambda i: (i, 0)), pl.BlockSpec((tm, F), lambda i: (i, 1)),
                  pl.BlockSpec((tm, F), lambda i: (i, 0))],
        out_specs=pl.BlockSpec((tm, F2), lambda i: (i, 0)),
        compiler_params=_params(("parallel",)), name=name)(gu, gu, dact)


def _fox_gates_fwd(f_t, b, *, name):
    H, T = f_t.shape
    nb = T // BLOCK

    def body(f_ref, b_ref, cum_ref, col_ref):
        f = f_ref[...] + b_ref[...]
        ls = jnp.minimum(f, 0.0) - jnp.log(1.0 + jnp.exp(-jnp.abs(f)))
        t = lax.broadcasted_iota(jnp.int32, (H, T), 1)
        ls = jnp.where(t >= PAD_ROWS, ls, 0.0)
        upper = (lax.broadcasted_iota(jnp.int32, (BLOCK, BLOCK), 0)
                 <= lax.broadcasted_iota(jnp.int32, (BLOCK, BLOCK), 1)).astype(F32)
        carry = jnp.zeros((H, 1), F32)
        for blk in range(nb):
            seg = ls[:, blk * BLOCK:(blk + 1) * BLOCK]
            pre = jnp.dot(seg, upper, precision=HIGHEST, preferred_element_type=F32) + carry
            cum_ref[:, blk * BLOCK:(blk + 1) * BLOCK] = pre
            col_ref[blk * BLOCK:(blk + 1) * BLOCK, :] = jnp.concatenate(
                [pre, jnp.zeros((BLOCK - H, BLOCK), F32)], axis=0).T
            carry = pre[:, BLOCK - 1:BLOCK]

    vm = pl.BlockSpec(memory_space=pltpu.VMEM)
    return pl.pallas_call(
        body, out_shape=(jax.ShapeDtypeStruct((H, T), F32), jax.ShapeDtypeStruct((T, BLOCK), F32)),
        in_specs=[vm, vm], out_specs=(vm, vm),
        compiler_params=_params(), name=name)(f_t, b)


def _fox_gates_bwd(dcq, dck, f_t, b, *, name):
    H, T = f_t.shape
    nb = T // BLOCK

    def body(dq_ref, d_ref, f_ref, b_ref, df_ref, db_ref):
        lower = (lax.broadcasted_iota(jnp.int32, (BLOCK, BLOCK), 0)
                 >= lax.broadcasted_iota(jnp.int32, (BLOCK, BLOCK), 1)).astype(F32)
        carry = jnp.zeros((H, 1), F32)
        for blk in range(nb - 1, -1, -1):
            seg = dq_ref[:, blk * BLOCK:(blk + 1) * BLOCK] - d_ref[:, blk * BLOCK:(blk + 1) * BLOCK]
            suf = jnp.dot(seg, lower, precision=HIGHEST, preferred_element_type=F32) + carry
            df_ref[:, blk * BLOCK:(blk + 1) * BLOCK] = suf
            carry = suf[:, 0:1]
        f = f_ref[...] + b_ref[...]
        t = lax.broadcasted_iota(jnp.int32, (H, T), 1)
        df = jnp.where(t >= PAD_ROWS, df_ref[...] / (1.0 + jnp.exp(f)), 0.0)
        df_ref[...] = df
        db_ref[...] = jnp.sum(df, axis=1, keepdims=True)

    vm = pl.BlockSpec(memory_space=pltpu.VMEM)
    return pl.pallas_call(
        body, out_shape=(jax.ShapeDtypeStruct((H, T), F32), jax.ShapeDtypeStruct((H, 1), F32)),
        in_specs=[vm, vm, vm, vm], out_specs=(vm, vm),
        compiler_params=_params(), name=name)(dcq, dck, f_t, b)


LANE_KC = HEAD_DIM
LANE_QC = HEAD_DIM + 3
LANE_END = HEAD_DIM + 6


def _split3(c):
    hi = c.astype(BF16).astype(F32)
    r = c - hi
    mid = r.astype(BF16).astype(F32)
    lo = (r - mid).astype(BF16).astype(F32)
    return hi, mid, lo


def _lanes(lane, data, start, terms, rest):
    out = rest
    for i, t in enumerate(terms):
        out = jnp.where(lane == start + i, t, out)
    return jnp.where(lane < HEAD_DIM, data, out)


def _fox_prep(proj, cum_col, *, name):
    T = proj.shape[0]
    tm = FOX_TILE
    nt = T // tm
    H = FOX_HEADS
    lanes = 2 * HEAD_DIM
    qb, kb, vb = 768 // lanes, 1280 // lanes, 1792 // lanes

    def body(q_ref, k_ref, v_ref, c_ref, qa_ref, ka_ref, va_ref):
        p = pl.program_id(0)
        i = pl.program_id(1)
        lane = lax.broadcasted_iota(jnp.int32, (tm, lanes), 1)
        rows = i * tm + lax.broadcasted_iota(jnp.int32, (tm, 1), 0)
        q2 = q_ref[...].astype(F32)
        k2 = k_ref[...].astype(F32)
        v2 = v_ref[...].astype(F32)
        cum = c_ref[...]
        for e in range(2):
            c = jnp.sum(jnp.where(lane == 2 * p + e, cum, 0.0), axis=1, keepdims=True)
            ck = jnp.where(rows >= PAD_ROWS, c, -NEG)
            qe, ke, ve = (q2, k2, v2) if e == 0 else tuple(pltpu.roll(a, HEAD_DIM, 1) for a in (q2, k2, v2))
            one = jnp.where(lane < LANE_END, 1.0, 0.0)
            qa = _lanes(lane, qe * SCALE, LANE_QC, _split3(c), jnp.where(lane < LANE_QC, -1.0, 0.0))
            ka = _lanes(lane, ke, LANE_KC, _split3(ck), one)
            va = jnp.where(lane < HEAD_DIM, ve, jnp.where(lane < LANE_QC, 1.0, 0.0))
            qa_ref[e] = qa.astype(BF16)
            ka_ref[e] = ka.astype(BF16)
            va_ref[e] = va.astype(BF16)

    def col(b):
        return pl.BlockSpec((tm, lanes), lambda p, i, b=b: (i, b + p))

    out = pl.BlockSpec((2, tm, lanes), lambda p, i: (p, i, 0))
    shp = jax.ShapeDtypeStruct((H, T, lanes), BF16)
    return pl.pallas_call(
        body, out_shape=(shp, shp, shp), grid=(H // 2, nt),
        in_specs=[col(qb), col(kb), col(vb), pl.BlockSpec((tm, lanes), lambda p, i: (i, 0))],
        out_specs=(out, out, out),
        compiler_params=_params(("parallel", "parallel")), name=name)(proj, proj, proj, cum_col)


def _fox_fwd(q_aug, k_aug, v_aug, *, ex=None, name):
    H, T, lanes = q_aug.shape
    tq = FOX_TILE
    nq = T // tq
    G = FOX_GROUP

    def body(q_ref, k_ref, v_ref, o_ref, lse_ref, m_scr, acc_scr):
        i = pl.program_id(1)
        m_scr[...] = jnp.full(m_scr.shape, NEG, F32)
        acc_scr[...] = jnp.zeros(acc_scr.shape, F32)

        def step(g, kb, diag):
            off = pl.multiple_of(kb * tq, tq)
            s_t = lax.dot_general(k_ref[g, pl.ds(off, tq), :], q_ref[g], NT, preferred_element_type=F32)
            if diag:
                r = lax.broadcasted_iota(jnp.int32, (tq, tq), 0)
                c = lax.broadcasted_iota(jnp.int32, (tq, tq), 1)
                s_t = jnp.where(c >= r, s_t, NEG)
            m_prev = m_scr[g]
            m_new = jnp.maximum(m_prev, jnp.max(s_t, axis=0, keepdims=True))
            p_t = jnp.exp(s_t - m_new).astype(BF16)
            alpha = jnp.exp(m_prev - m_new)
            acc_scr[g] = alpha * acc_scr[g] + lax.dot_general(
                v_ref[g, pl.ds(off, tq), :], p_t, TN, preferred_element_type=F32)
            m_scr[g] = m_new

        def loop_body(kb, carry):
            for g in range(G):
                step(g, kb, False)
            return carry

        lax.fori_loop(0, i, loop_body, 0)
        for g in range(G):
            step(g, i, True)
            acc = acc_scr[g]
            lse_ref[g] = m_scr[g] + jnp.log(acc[HEAD_DIM:HEAD_DIM + 1, :])
            acc_t = acc.T
            o_ref[g] = (acc_t[:, :HEAD_DIM] / acc_t[:, HEAD_DIM:HEAD_DIM + 1]).astype(o_ref.dtype)

    blk = pl.BlockSpec((G, tq, lanes), lambda h, i: (h, i, 0))
    full = pl.BlockSpec((G, T, lanes), lambda h, i: (h, 0, 0))
    grid = (H // G, nq)
    body, x_in, x_in_specs, x_out, x_out_specs, x_scr = _carry(ex, grid, 3, 2, body)
    return pl.pallas_call(
        body,
        out_shape=(jax.ShapeDtypeStruct((H, T, HEAD_DIM), BF16), jax.ShapeDtypeStruct((H, nq, 1, tq), F32), *x_out),
        grid=grid,
        in_specs=[blk, full, full] + x_in_specs,
        out_specs=(pl.BlockSpec((G, tq, HEAD_DIM), lambda h, i: (h, i, 0)),
                   pl.BlockSpec((G, None, 1, tq), lambda h, i: (h, i, 0, 0)), *x_out_specs),
        scratch_shapes=[pltpu.VMEM((G, 1, tq), F32), pltpu.VMEM((G, lanes, tq), F32)] + x_scr,
        compiler_params=_params(("arbitrary", "arbitrary")), name=name)(q_aug, k_aug, v_aug, *x_in)


def _fox_prep_bwd(dmix, o, *, name):
    T = dmix.shape[0]
    H = o.shape[0]
    tm = FOX_TILE
    lanes = 2 * HEAD_DIM
    first = 512 // lanes

    def body(d_ref, o_ref, da_ref):
        lane = lax.broadcasted_iota(jnp.int32, (tm, lanes), 1)
        d2 = d_ref[...].astype(F32)
        for e in range(2):
            de = d2 if e == 0 else pltpu.roll(d2, HEAD_DIM, 1)
            d64 = d_ref[:, e * HEAD_DIM:(e + 1) * HEAD_DIM].astype(F32)
            delta = jnp.sum(d64 * o_ref[e].astype(F32), axis=1, keepdims=True)
            da_ref[e] = _lanes(lane, de, LANE_KC, _split3(-delta), jnp.zeros((), F32)).astype(BF16)

    return pl.pallas_call(
        body, out_shape=jax.ShapeDtypeStruct((H, T, lanes), BF16), grid=(H // 2, T // tm),
        in_specs=[pl.BlockSpec((tm, lanes), lambda p, i: (i, first + p)),
                  pl.BlockSpec((2, tm, HEAD_DIM), lambda p, i: (p, i, 0))],
        out_specs=pl.BlockSpec((2, tm, lanes), lambda p, i: (p, i, 0)),
        compiler_params=_params(("parallel", "parallel")), name=name)(dmix, o)


def _fox_bwd(q_aug, k_aug, v_aug, do_aug, lse_row, *, ex=None, name):
    H, T, lanes = q_aug.shape
    tq = FOX_TILE
    nq = T // tq
    G = FOX_GROUP

    def body(q_ref, k_ref, v_ref, do_ref, lse_ref, dq_ref, dk_ref, dv_ref, dck_ref, dk_acc, dv_acc):
        j = pl.program_id(1)

        @pl.when(j == 0)
        def _():
            dq_ref[...] = jnp.zeros(dq_ref.shape, F32)

        dk_acc[...] = jnp.zeros(dk_acc.shape, F32)
        dv_acc[...] = jnp.zeros(dv_acc.shape, F32)

        def step(g, qb, diag):
            off = pl.multiple_of(qb * tq, tq)
            ka = k_ref[g]
            qa = q_ref[g, pl.ds(off, tq), :]
            da = do_ref[g, pl.ds(off, tq), :]
            s_t = lax.dot_general(ka, qa, NT, preferred_element_type=F32)
            p_t = jnp.exp(s_t - lse_ref[g, qb])
            if diag:
                r = lax.broadcasted_iota(jnp.int32, (tq, tq), 0)
                c = lax.broadcasted_iota(jnp.int32, (tq, tq), 1)
                p_t = jnp.where(c >= r, p_t, 0.0)
            dv_acc[g] += jnp.dot(p_t.astype(BF16), da, preferred_element_type=F32)
            dp_t = lax.dot_general(v_ref[g], da, NT, preferred_element_type=F32)
            dsb = (p_t * dp_t).astype(BF16)
            dk_acc[g] += jnp.dot(dsb, qa, preferred_element_type=F32)
            dq_ref[g, qb] += lax.dot_general(ka, dsb, TN, preferred_element_type=F32)

        for g in range(G):
            step(g, j, True)

        def loop_body(qb, carry):
            for g in range(G):
                step(g, qb, False)
            return carry

        lax.fori_loop(j + 1, nq, loop_body, 0)
        dk = dk_acc[...]
        dk_ref[...] = dk.astype(dk_ref.dtype)
        dck_ref[...] = -dk[:, :, LANE_KC:LANE_KC + 1]
        dv_ref[...] = dv_acc[...].astype(dv_ref.dtype)

    blk = pl.BlockSpec((G, tq, lanes), lambda h, j: (h, j, 0))
    full = pl.BlockSpec((G, T, lanes), lambda h, j: (h, 0, 0))
    grid = (H // G, nq)
    body, x_in, x_in_specs, x_out, x_out_specs, x_scr = _carry(ex, grid, 5, 4, body)
    return pl.pallas_call(
        body,
        out_shape=(jax.ShapeDtypeStruct((H, nq, lanes, tq), F32), jax.ShapeDtypeStruct((H, T, lanes), BF16),
                   jax.ShapeDtypeStruct((H, T, lanes), BF16), jax.ShapeDtypeStruct((H, T, 1), F32), *x_out),
        grid=grid,
        in_specs=[full, blk, blk, full, pl.BlockSpec((G, nq, 1, tq), lambda h, j: (h, 0, 0, 0))] + x_in_specs,
        out_specs=(pl.BlockSpec((G, nq, lanes, tq), lambda h, j: (h, 0, 0, 0)), blk, blk,
                   pl.BlockSpec((G, tq, 1), lambda h, j: (h, j, 0)), *x_out_specs),
        scratch_shapes=[pltpu.VMEM((G, tq, lanes), F32), pltpu.VMEM((G, tq, lanes), F32)] + x_scr,
        compiler_params=_params(("arbitrary", "arbitrary")), name=name,
    )(q_aug, k_aug, v_aug, do_aug, lse_row, *x_in)


def _t5_bucket_np(d):
    n = np.maximum(d, 0).astype(np.int32)
    max_exact = N_BUCKETS // 2
    nf = np.maximum(n, 1).astype(np.float32)
    large = max_exact + (np.log(nf / max_exact) / math.log(MAX_DISTANCE / max_exact)
                         * (N_BUCKETS - max_exact)).astype(np.int32)
    large = np.minimum(large, N_BUCKETS - 1)
    return np.where(n < max_exact, n, large)


def _bucket_onehots():
    k = np.arange(BLOCK)[:, None]
    q = np.arange(BLOCK)[None, :]
    eye = np.eye(N_BUCKETS, dtype=np.float32)
    cur = eye[_t5_bucket_np(q - k).reshape(-1)]
    prev = eye[_t5_bucket_np(BLOCK + q - k).reshape(-1)]
    return cur, prev


SWA_K_COL = SWA_Q_HEADS * HEAD_DIM // (2 * HEAD_DIM)
SWA_V_COL = SWA_K_COL + 1


def _swa_terms(kk, qm, bc, bp, far, sink, n):
    k = lax.broadcasted_iota(jnp.int32, (BLOCK, BLOCK), 0)
    q = lax.broadcasted_iota(jnp.int32, (BLOCK, BLOCK), 1)
    never = 2 * BLOCK
    s_c = lax.dot_general(kk[0], qm, NT, preferred_element_type=F32) + bc
    s_p = lax.dot_general(kk[1], qm, NT, preferred_element_type=F32) + bp
    s_m = lax.dot_general(kk[2], qm, NT, preferred_element_type=F32) + jnp.where(n == 1, bp, far)
    s_c = jnp.where((k <= q) & (k >= jnp.where(n >= 1, 0, PAD_ROWS)), s_c, NEG)
    s_p = jnp.where(k > q + jnp.where(n >= 2, 0, never), s_p, NEG)
    s_m = jnp.where(k >= jnp.where(n >= 1, PAD_ROWS, never), s_m, NEG)
    m = jnp.maximum(jnp.maximum(jnp.max(s_c, axis=0, keepdims=True), jnp.max(s_p, axis=0, keepdims=True)),
                    jnp.maximum(jnp.max(s_m, axis=0, keepdims=True), sink))
    e = [jnp.exp(s_c - m), jnp.exp(s_p - m), jnp.exp(s_m - m)]
    e_s = jnp.exp(sink - m)
    l = (jnp.sum(e[0], axis=0, keepdims=True) + jnp.sum(e[1], axis=0, keepdims=True)
         + jnp.sum(e[2], axis=0, keepdims=True) + e_s)
    return e, e_s, l


def _swa_specs():
    G = SWA_GROUP
    width = G * HEAD_DIM

    def rows(which, col):
        if which == "cur":
            return pl.BlockSpec((BLOCK, BLOCK), lambda kv, n: (n, col))
        if which == "prev":
            return pl.BlockSpec((BLOCK, BLOCK), lambda kv, n: (jnp.maximum(n - 1, 0), col))
        return pl.BlockSpec((BLOCK, BLOCK), lambda kv, n: (0, col))

    qblk = pl.BlockSpec((BLOCK, width), lambda kv, n: (n, kv))
    keys = [rows(w, SWA_K_COL) for w in ("cur", "prev", "meta")]
    vals = [rows(w, SWA_V_COL) for w in ("cur", "prev", "meta")]
    bias = pl.BlockSpec((G, BLOCK, BLOCK), lambda kv, n: (kv, 0, 0))
    smem = pl.BlockSpec(memory_space=pltpu.SMEM)
    return qblk, keys, vals, bias, smem


def _swa_own_kv(tile_ref, kv):
    lane = lax.broadcasted_iota(jnp.int32, (BLOCK, 2 * HEAD_DIM), 1)
    t = tile_ref[...].astype(F32)
    return jnp.where(lane // HEAD_DIM == kv, t, pltpu.roll(t, HEAD_DIM, 1)).astype(BF16)


def _swa_fwd(proj, bc, bp, far, sinks, *, name):
    T = proj.shape[0]
    nb = T // BLOCK
    G = SWA_GROUP
    lanes = 2 * HEAD_DIM

    def body(q_ref, kc_ref, kp_ref, km_ref, vc_ref, vp_ref, vm_ref, bc_ref, bp_ref, far_ref, sink_ref, o_ref):
        kv = pl.program_id(0)
        n = pl.program_id(1)
        lane = lax.broadcasted_iota(jnp.int32, (BLOCK, lanes), 1)
        kk = [_swa_own_kv(r, kv) for r in (kc_ref, kp_ref, km_ref)]
        vv = [_swa_own_kv(r, kv) for r in (vc_ref, vp_ref, vm_ref)]
        for pair in range(G // 2):
            q2 = q_ref[:, pair * lanes:(pair + 1) * lanes].astype(F32) * SCALE
            outs = []
            for half in range(2):
                g = 2 * pair + half
                h = kv * G + g
                qm = jnp.where(lane // HEAD_DIM == half, q2, 0.0).astype(BF16)
                e, _, l = _swa_terms(kk, qm, bc_ref[g], bp_ref[g], far_ref[h], sink_ref[h], n)
                o_t = sum(lax.dot_general(vv[b], e[b].astype(BF16), TN, preferred_element_type=F32) for b in range(3))
                outs.append((o_t / l).T)
            o_ref[:, pair * lanes:(pair + 1) * lanes] = jnp.where(lane < HEAD_DIM, outs[0], outs[1]).astype(o_ref.dtype)

    qblk, keys, vals, bias, smem = _swa_specs()
    return pl.pallas_call(
        body, out_shape=jax.ShapeDtypeStruct((T, SWA_Q_HEADS * HEAD_DIM), BF16), grid=(SWA_KV_HEADS, nb),
        in_specs=[qblk] + keys + vals + [bias, bias, smem, smem],
        out_specs=qblk,
        compiler_params=_params(("parallel", "parallel")), name=name,
    )(proj, proj, proj, proj, proj, proj, proj, bc, bp, far, sinks)


def _swa_bwd(proj, dmix, bc, bp, far, sinks, *, ex=None, name):
    T = proj.shape[0]
    nb = T // BLOCK
    G = SWA_GROUP
    Hq = SWA_Q_HEADS
    lanes = 2 * HEAD_DIM

    def body(q_ref, kc_ref, kp_ref, km_ref, vc_ref, vp_ref, vm_ref, do_ref, bc_ref, bp_ref, far_ref, sink_ref,
             dq_ref, dk_ref, dv_ref, dbc_ref, dbp_ref, dbf_ref, dsk_ref):
        kv = pl.program_id(0)
        n = pl.program_id(1)

        @pl.when(n == 0)
        def _():
            for ref in (dk_ref, dv_ref, dbc_ref, dbp_ref, dbf_ref, dsk_ref):
                ref[...] = jnp.zeros(ref.shape, F32)

        lane = lax.broadcasted_iota(jnp.int32, (BLOCK, lanes), 1)
        kk = [_swa_own_kv(r, kv) for r in (kc_ref, kp_ref, km_ref)]
        vv = [_swa_own_kv(r, kv) for r in (vc_ref, vp_ref, vm_ref)]
        dk = [jnp.zeros((BLOCK, lanes), F32) for _ in range(3)]
        dv = [jnp.zeros((BLOCK, lanes), F32) for _ in range(3)]
        for pair in range(G // 2):
            q2 = q_ref[:, pair * lanes:(pair + 1) * lanes].astype(F32) * SCALE
            d2 = do_ref[:, pair * lanes:(pair + 1) * lanes]
            dqs = []
            for half in range(2):
                g = 2 * pair + half
                h = kv * G + g
                own = lane // HEAD_DIM == half
                qm = jnp.where(own, q2, 0.0).astype(BF16)
                dom = jnp.where(own, d2, jnp.zeros_like(d2))
                e, e_s, l = _swa_terms(kk, qm, bc_ref[g], bp_ref[g], far_ref[h], sink_ref[h], n)
                inv = 1.0 / l
                p = [e[b] * inv for b in range(3)]
                dp = [lax.dot_general(vv[b], dom, NT, preferred_element_type=F32) for b in range(3)]
                delta = sum(jnp.sum(p[b] * dp[b], axis=0, keepdims=True) for b in range(3))
                ds = [p[b] * (dp[b] - delta) for b in range(3)]
                dsk_ref[g] += -(e_s * inv) * delta
                dbc_ref[g] += ds[0]
                dbp_ref[g] += ds[1] + jnp.where(n == 1, ds[2], 0.0)
                dbf_ref[g] += jnp.where(n >= 2, ds[2], 0.0)
                ds16 = [d.astype(BF16) for d in ds]
                dq_t = sum(lax.dot_general(kk[b], ds16[b], TN, preferred_element_type=F32) for b in range(3))
                dqs.append(dq_t.T * SCALE)
                for b in range(3):
                    dk[b] += jnp.dot(ds16[b], qm, preferred_element_type=F32)
                    dv[b] += jnp.dot(p[b].astype(BF16), dom, preferred_element_type=F32)
            dq_ref[:, pair * lanes:(pair + 1) * lanes] = jnp.where(lane < HEAD_DIM, dqs[0], dqs[1]).astype(dq_ref.dtype)
        cur_off = pl.multiple_of(n * BLOCK, BLOCK)
        prev_off = pl.multiple_of(jnp.maximum(n - 1, 0) * BLOCK, BLOCK)
        for acc, ref in ((dk, dk_ref), (dv, dv_ref)):
            tot = [a + pltpu.roll(a, HEAD_DIM, 1) for a in acc]
            ref[pl.ds(cur_off, BLOCK), :] += tot[0]
            ref[pl.ds(prev_off, BLOCK), :] += tot[1]
            ref[0:BLOCK, :] += tot[2]

    qblk, keys, vals, bias, smem = _swa_specs()
    kvfull = pl.BlockSpec((None, T, lanes), lambda kv, n: (kv, 0, 0))
    dsk = pl.BlockSpec((G, 1, BLOCK), lambda kv, n: (kv, 0, 0))
    grid = (SWA_KV_HEADS, nb)
    body, x_in, x_in_specs, x_out, x_out_specs, x_scr = _carry(ex, grid, 12, 7, body)
    tile = jax.ShapeDtypeStruct((Hq, BLOCK, BLOCK), F32)
    return pl.pallas_call(
        body,
        out_shape=(jax.ShapeDtypeStruct((T, Hq * HEAD_DIM), BF16),
                   jax.ShapeDtypeStruct((SWA_KV_HEADS, T, lanes), F32),
                   jax.ShapeDtypeStruct((SWA_KV_HEADS, T, lanes), F32),
                   tile, tile, tile, jax.ShapeDtypeStruct((Hq, 1, BLOCK), F32), *x_out),
        grid=grid,
        in_specs=[qblk] + keys + vals + [qblk, bias, bias, smem, smem] + x_in_specs,
        out_specs=(qblk, kvfull, kvfull, bias, bias, bias, dsk, *x_out_specs),
        scratch_shapes=x_scr,
        compiler_params=_params(("arbitrary", "arbitrary")), name=name,
    )(proj, proj, proj, proj, proj, proj, proj, dmix, bc, bp, far, sinks, *x_in)


def _small_grads(dbc, dbp, dbf, dsk, oh_cur, oh_prev, *, name):
    Hq = dbc.shape[0]

    def body(dbc_ref, dbp_ref, dbf_ref, dsk_ref, oc_ref, op_ref, tab_ref, sink_ref):
        tab = (jnp.dot(dbc_ref[...], oc_ref[...], precision=HIGHEST, preferred_element_type=F32)
               + jnp.dot(dbp_ref[...], op_ref[...], precision=HIGHEST, preferred_element_type=F32))
        far = jnp.sum(dbf_ref[...], axis=1, keepdims=True)
        last = lax.broadcasted_iota(jnp.int32, (Hq, N_BUCKETS), 1) == N_BUCKETS - 1
        tab_ref[...] = tab + jnp.where(last, far, 0.0)
        sink_ref[...] = jnp.sum(dsk_ref[...], axis=1, keepdims=True)

    vm = pl.BlockSpec(memory_space=pltpu.VMEM)
    return pl.pallas_call(
        body, out_shape=(jax.ShapeDtypeStruct((Hq, N_BUCKETS), F32), jax.ShapeDtypeStruct((Hq, 1), F32)),
        in_specs=[vm] * 6, out_specs=(vm, vm), compiler_params=_params(), name=name,
    )(dbc.reshape(Hq, -1), dbp.reshape(Hq, -1), dbf.reshape(Hq, -1), dsk.reshape(Hq, -1), oh_cur, oh_prev)


def _coords():
    return lax.axis_index("x"), lax.axis_index("y"), lax.axis_index("c")


class _Exchange:
    def __init__(self, inputs, out_shapes, scratch, start, finish):
        self.inputs, self.out_shapes, self.scratch, self.start, self.finish = inputs, out_shapes, scratch, start, finish


def _carry(ex, grid, n_in, n_out, body):
    if ex is None:
        return body, [], [], [], [], []
    ni, no = len(ex.inputs), len(ex.out_shapes)

    def at_step(which):
        cond = None
        for axis, n in enumerate(grid):
            c = pl.program_id(axis) == (0 if which == "first" else n - 1)
            cond = c if cond is None else cond & c
        return cond

    def wrapped(*refs):
        refs = list(refs)
        n_own_scr = len(refs) - (n_in + ni + n_out + no) - len(ex.scratch)
        own_in, side_in = refs[:n_in], refs[n_in:n_in + ni]
        own_out = refs[n_in + ni:n_in + ni + n_out]
        side_out = refs[n_in + ni + n_out:n_in + ni + n_out + no]
        rest = refs[n_in + ni + n_out + no:]
        own_scr, sems = rest[:n_own_scr], rest[n_own_scr:]

        @pl.when(at_step("first"))
        def _():
            ex.start(side_in, side_out, sems)

        body(*own_in, *own_out, *own_scr)

        @pl.when(at_step("last"))
        def _():
            ex.finish(side_in, side_out, sems)

    hbm = pl.BlockSpec(memory_space=pl.ANY)
    return wrapped, list(ex.inputs), [hbm] * ni, list(ex.out_shapes), [hbm] * no, list(ex.scratch)


def _run_exchange(ex, *, name):
    ni, no = len(ex.inputs), len(ex.out_shapes)

    def body(*refs):
        ins, outs, sems = refs[:ni], refs[ni:ni + no], refs[ni + no:]
        ex.start(ins, outs, sems)
        ex.finish(ins, outs, sems)

    hbm = pl.BlockSpec(memory_space=pl.ANY)
    return pl.pallas_call(
        body, out_shape=tuple(ex.out_shapes), in_specs=[hbm] * ni, out_specs=tuple([hbm] * no),
        scratch_shapes=ex.scratch, compiler_params=_params(), name=name)(*ex.inputs)


def _gather_exchange(shards):
    nt = len(shards)

    def copies(ins, outs, sems):
        send_sems, recv_sems, local_sems = sems
        x, y, c = _coords()
        me, sibling = (x, y, c), (x, y, 1 - c)
        chips = [(1 - x, y), (x, 1 - y), (1 - x, 1 - y)]

        def slot(t, dev):
            return outs[t].at[4 * dev[0] + 2 * dev[1] + dev[2]]

        def copy(t, k, block, to, src=None):
            dst = slot(t, block)
            return pltpu.make_async_remote_copy(
                src_ref=dst if src is None else src, dst_ref=dst,
                send_sem=send_sems.at[t, k], recv_sem=recv_sems.at[t, k], device_id=to, device_id_type=MESH)

        mine = [pltpu.make_async_copy(ins[t], slot(t, me), local_sems.at[t]) for t in range(nt)]
        first = []
        for t in range(nt):
            first.append(copy(t, 0, me, sibling, src=ins[t]))
            first += [copy(t, 1 + j, me, (*chip, c), src=ins[t]) for j, chip in enumerate(chips)]
        return copy, mine, first, me, sibling, chips, c

    def start(ins, outs, sems):
        _, mine, first, *_ = copies(ins, outs, sems)
        for cp in mine + first:
            cp.start()

    def finish(ins, outs, sems):
        copy, mine, first, me, sibling, chips, c = copies(ins, outs, sems)
        passed = []
        for j, chip in enumerate(chips):
            for t in range(nt):
                copy(t, 1 + j, (*chip, c), me).wait_recv()
                cp = copy(t, 4 + j, (*chip, c), sibling)
                cp.start()
                passed.append(cp)
        for t in range(nt):
            copy(t, 0, sibling, me).wait_recv()
            for j, chip in enumerate(chips):
                copy(t, 4 + j, (*chip, 1 - c), me).wait_recv()
        for cp in first + passed:
            cp.wait_send()
        for cp in mine:
            cp.wait()

    return _Exchange(
        list(shards), [jax.ShapeDtypeStruct((N_DEV,) + s.shape, s.dtype) for s in shards],
        [pltpu.SemaphoreType.DMA((nt, 7)), pltpu.SemaphoreType.DMA((nt, 7)), pltpu.SemaphoreType.DMA((nt,))],
        start, finish)


def _swap_exchange(arrays, n_slices, copies):
    nt = len(arrays)

    def start(ins, outs, sems):
        for cp in copies(ins, outs, sems):
            cp.start()

    def finish(ins, outs, sems):
        sends = copies(ins, outs, sems)
        for cp in sends:
            cp.wait_recv()
        for cp in sends:
            cp.wait_send()

    return _Exchange(
        list(arrays), [jax.ShapeDtypeStruct((n_slices,) + a.shape[1:], a.dtype) for a in arrays],
        [pltpu.SemaphoreType.DMA((nt, n_slices)), pltpu.SemaphoreType.DMA((nt, n_slices))], start, finish)


def _cores_exchange(gs):
    def copies(ins, outs, sems):
        send_sems, recv_sems = sems
        x, y, c = _coords()
        return [pltpu.make_async_remote_copy(
            src_ref=ins[t].at[2 * j + (1 - c)], dst_ref=outs[t].at[j],
            send_sem=send_sems.at[t, j], recv_sem=recv_sems.at[t, j], device_id=(x, y, 1 - c), device_id_type=MESH)
            for t in range(len(gs)) for j in range(4)]

    return _swap_exchange(gs, 4, copies)


def _chips_exchange(ps):
    def copies(ins, outs, sems):
        send_sems, recv_sems = sems
        x, y, c = _coords()
        peers = [(1 - x, y), (x, 1 - y), (1 - x, 1 - y)]
        return [pltpu.make_async_remote_copy(
            src_ref=ins[t].at[2 * px + py], dst_ref=outs[t].at[k],
            send_sem=send_sems.at[t, k], recv_sem=recv_sems.at[t, k], device_id=(px, py, c), device_id_type=MESH)
            for t in range(len(ps)) for k, (px, py) in enumerate(peers)]

    return _swap_exchange(ps, 3, copies)


def _add_cores(g, r, core, *, name):
    _, A, B = g.shape
    ta = _tile(A, 512, 16)

    def body(core_ref, a_ref, b_ref, o_ref, o16_ref):
        s = a_ref[...] + b_ref[...]
        o_ref[...] = s
        o16_ref[...] = s.astype(BF16)

    blk = (None, ta, B)
    out = pl.BlockSpec(blk, lambda j, i, core_ref: (j, i, 0))
    return pl.pallas_call(
        body, out_shape=(jax.ShapeDtypeStruct((4, A, B), F32), jax.ShapeDtypeStruct((4, A, B), BF16)),
        grid_spec=pltpu.PrefetchScalarGridSpec(
            num_scalar_prefetch=1, grid=(4, A // ta),
            in_specs=[pl.BlockSpec(blk, lambda j, i, core_ref: (2 * j + core_ref[0], i, 0)),
                      pl.BlockSpec(blk, lambda j, i, core_ref: (j, i, 0))],
            out_specs=(out, out)),
        compiler_params=_params(("parallel", "parallel")), name=name)(core, g, r)


def _adamw_math(w, g, m, v):
    m = ADAM_B1 * m + (1.0 - ADAM_B1) * g
    v = ADAM_B2 * v + (1.0 - ADAM_B2) * (g * g)
    m_hat = m / (1.0 - ADAM_B1 ** ADAM_STEP)
    v_hat = v / (1.0 - ADAM_B2 ** ADAM_STEP)
    delta = -ADAM_LR * (m_hat / (jnp.sqrt(v_hat) + ADAM_EPS) + ADAM_WD * w)
    return delta, m, v


def _sum_adamw(p, r, chip, w, m, v, *, segs, ta, name):
    Aw, Bw = w.shape
    Bg = p.shape[2]
    assert Aw % ta == 0

    def body(chip_ref, p_ref, r0, r1, r2, w_ref, m_ref, v_ref, g_out, d_out, m_out, v_out):
        for gc, wc, n in segs:
            g = ((p_ref[:, gc:gc + n] + r0[:, gc:gc + n].astype(F32)) + r1[:, gc:gc + n].astype(F32)
                 ) + r2[:, gc:gc + n].astype(F32)
            delta, m_new, v_new = _adamw_math(w_ref[:, wc:wc + n], g, m_ref[:, wc:wc + n], v_ref[:, wc:wc + n])
            g_out[:, wc:wc + n] = g
            d_out[:, wc:wc + n] = delta
            m_out[:, wc:wc + n] = m_new
            v_out[:, wc:wc + n] = v_new

    gblk = (None, ta, Bg)
    row = pl.BlockSpec((ta, Bw), lambda i, chip_ref: (i, 0))
    rspecs = [pl.BlockSpec(gblk, (lambda i, chip_ref, k=k: (k, i, 0))) for k in range(3)]
    shp = jax.ShapeDtypeStruct((Aw, Bw), F32)
    return pl.pallas_call(
        body, out_shape=(shp, shp, shp, shp),
        grid_spec=pltpu.PrefetchScalarGridSpec(
            num_scalar_prefetch=1, grid=(Aw // ta,),
            in_specs=[pl.BlockSpec(gblk, lambda i, chip_ref: (chip_ref[0], i, 0))] + rspecs + [row, row, row],
            out_specs=(row, row, row, row)),
        compiler_params=_params(("parallel",)), name=name)(chip, p, r, r, r, w, m, v)


def _adamw(w, g, m, v, *, name):
    def body(w_ref, g_ref, m_ref, v_ref, d_out, m_out, v_out):
        delta, m_new, v_new = _adamw_math(w_ref[...], g_ref[...], m_ref[...], v_ref[...])
        d_out[...] = delta
        m_out[...] = m_new
        v_out[...] = v_new

    vm = pl.BlockSpec(memory_space=pltpu.VMEM)
    shp = jax.ShapeDtypeStruct(w.shape, F32)
    return pl.pallas_call(body, out_shape=(shp, shp, shp), in_specs=[vm] * 4, out_specs=(vm, vm, vm),
                          compiler_params=_params(), name=name)(w, g, m, v)


def _small_allreduce_adamw(s, w, m, v, *, name):
    R, W = s.shape

    def body(s_ref, w_ref, m_ref, v_ref, g_out, d_out, m_out, v_out, gath, send_sems, recv_sems):
        x, y, c = _coords()
        mine = 4 * x + 2 * y + c
        gath[mine] = s_ref[...]
        peers = [((1 - x) if k & 4 else x, (1 - y) if k & 2 else y, (1 - c) if k & 1 else c) for k in range(1, N_DEV)]
        sends = []
        for k in range(1, N_DEV):
            peer = peers[k - 1]
            sends.append(pltpu.make_async_remote_copy(
                src_ref=s_ref, dst_ref=gath.at[mine], send_sem=send_sems.at[k - 1], recv_sem=recv_sems.at[k - 1],
                device_id=peer, device_id_type=MESH))
        for cp in sends:
            cp.start()
        for k in range(1, N_DEV):
            peer = peers[k - 1]
            pltpu.make_async_remote_copy(
                src_ref=s_ref, dst_ref=gath.at[4 * peer[0] + 2 * peer[1] + peer[2]],
                send_sem=send_sems.at[k - 1], recv_sem=recv_sems.at[k - 1],
                device_id=peer, device_id_type=MESH).wait_recv()
        for cp in sends:
            cp.wait_send()
        g = gath[0]
        for d in range(1, N_DEV):
            g = g + gath[d]
        delta, m_new, v_new = _adamw_math(w_ref[...], g, m_ref[...], v_ref[...])
        g_out[...] = g
        d_out[...] = delta
        m_out[...] = m_new
        v_out[...] = v_new

    vm = pl.BlockSpec(memory_space=pltpu.VMEM)
    shp = jax.ShapeDtypeStruct((R, W), F32)
    return pl.pallas_call(
        body, out_shape=(shp, shp, shp, shp), in_specs=[vm] * 4, out_specs=(vm, vm, vm, vm),
        scratch_shapes=[pltpu.VMEM((N_DEV, R, W), F32), pltpu.SemaphoreType.DMA((N_DEV - 1,)),
                        pltpu.SemaphoreType.DMA((N_DEV - 1,))],
        compiler_params=_params(), name=name)(s, w, m, v)


def _pack_small(rel_bias, g1, g2, g3, g4, b_forget, sinks, extra=None, meta=None):
    misc = jnp.concatenate([rel_bias.reshape(-1), b_forget.reshape(-1), sinks.reshape(-1)])
    misc = jnp.concatenate([misc, jnp.zeros((D_MODEL - misc.shape[0],), F32)])[None]
    last = jnp.zeros((1, D_MODEL), F32) if extra is None else extra
    meta = jnp.zeros((N_META, D_MODEL), F32) if meta is None else meta
    return jnp.concatenate([g1, g2, g3, g4, misc, last, jnp.zeros((2, D_MODEL), F32), meta], axis=0)


def _unpack_small(p):
    nrb = N_BUCKETS * SWA_Q_HEADS
    misc = p[4]
    return dict(rel_bias=misc[:nrb].reshape(N_BUCKETS, SWA_Q_HEADS), ln_pre_mix=p[0:1], ln_post_mix=p[1:2],
                ln_pre_ffn=p[2:3], ln_post_ffn=p[3:4], b_forget=misc[nrb:nrb + 8].reshape(1, 8),
                sinks=misc[nrb + 8:nrb + 16].reshape(1, 8))


def _shard_order(pieces, shard, pad):
    T, dtype = pieces[0].shape[0], pieces[0].dtype
    total = sum(p.shape[1] for p in pieces)
    assert total % shard == 0
    out, zeros = [], jnp.zeros((T, pad), dtype)
    for s in range(total // shard):
        lo, hi, start = s * shard, (s + 1) * shard, 0
        for p in pieces:
            end = start + p.shape[1]
            if max(lo, start) < min(hi, end):
                out.append(p[:, max(lo, start) - start:min(hi, end) - start])
            start = end
        out.append(zeros)
    return jnp.concatenate(out, axis=1)


def _unheads(a):
    return a.transpose(1, 0, 2).reshape(a.shape[1], -1)


def kernel(x, meta_tokens, rel_bias, ln_pre_mix, ln_post_mix, ln_pre_ffn, ln_post_ffn, w_in, b_forget, sinks, w_out, w_gate_up, w_down, loss_target, m_meta_tokens, m_rel_bias, m_ln_pre_mix, m_ln_post_mix, m_ln_pre_ffn, m_ln_post_ffn, m_w_in, m_b_forget, m_sinks, m_w_out, m_w_gate_up, m_w_down, v_meta_tokens, v_rel_bias, v_ln_pre_mix, v_ln_post_mix, v_ln_pre_ffn, v_ln_post_ffn, v_w_in, v_b_forget, v_sinks, v_w_out, v_w_gate_up, v_w_down):
    seq = x.shape[1]
    T = BLOCK + seq
    assert T % FOX_TILE == 0
    nq = T // FOX_TILE
    tm = _tile(T, 1056)
    cin = w_in.shape[2]
    hid = w_down.shape[1]
    assert w_gate_up.shape[2] == 2 * hid and cin <= W_IN_PAD and hid <= HID_PAD

    x_i, y_i, c_i = _coords()
    core = jnp.reshape(c_i, (1,)).astype(jnp.int32)
    chip = jnp.reshape(2 * x_i + y_i, (1,)).astype(jnp.int32)
    w_in_s = jnp.pad(w_in[0].astype(BF16), ((0, 0), (0, W_IN_PAD - cin)))
    w_gu_s = jnp.pad(w_gate_up[0].astype(BF16).reshape(D_MODEL, 2, hid), ((0, 0), (0, 0), (0, HID_PAD - hid)))
    w_gu_s = w_gu_s.reshape(D_MODEL, 2 * HID_PAD)
    w_down_s = jnp.pad(w_down[0].astype(BF16), ((0, HID_PAD - hid), (0, 0)))
    g_in, g_meta = _run_exchange(_gather_exchange([w_in_s, meta_tokens]), name="ag_w_in")
    gather_rest = _gather_exchange([w_out[0].astype(BF16), w_gu_s, w_down_s])
    w_in_full = g_in[:, :, :cin].transpose(1, 0, 2).reshape(D_MODEL, N_DEV * cin)
    w_qkv = w_in_full[:, :D_QKV]
    w_f = jnp.pad(w_in_full[:, D_QKV:], ((0, 0), (0, BLOCK - FOX_HEADS)))
    meta_full = g_meta.transpose(1, 0, 2).reshape(N_META, D_MODEL)

    h0 = jnp.concatenate([jnp.zeros((PAD_ROWS, D_MODEL), F32), meta_full, x[0]], axis=0)
    target = jnp.concatenate([jnp.zeros((BLOCK, D_MODEL), F32), loss_target[0]], axis=0)
    hn1 = _rms_fwd(h0, ln_pre_mix, name="rms_pre_mix")
    proj = _matmul(hn1, w_qkv, out_dtype=BF16, tm=tm, tn=768, name="mm_in_proj")
    proj_f = _matmul(hn1, w_f, out_dtype=F32, tm=tm, tn=BLOCK, name="mm_in_proj_f")

    f_t = proj_f[:, :FOX_HEADS].T
    bf_col = b_forget.reshape(FOX_HEADS, 1)

    oh_cur, oh_prev = _bucket_onehots()
    bias_c = jnp.einsum("pb,bh->hp", jnp.asarray(oh_cur), rel_bias, precision=HIGHEST).reshape(8, BLOCK, BLOCK)
    bias_p = jnp.einsum("pb,bh->hp", jnp.asarray(oh_prev), rel_bias, precision=HIGHEST).reshape(8, BLOCK, BLOCK)
    far = rel_bias[N_BUCKETS - 1]
    sink_v = sinks[0]
    o_a = _swa_fwd(proj, bias_c, bias_p, far, sink_v, name="swa_fwd")

    _, cum_col = _fox_gates_fwd(f_t, bf_col, name="fox_gates_fwd")
    q_b, k_b, v_b = _fox_prep(proj, cum_col, name="fox_prep")
    o_b, lse_row, g_out, g_gu, g_down = _fox_fwd(q_b, k_b, v_b, ex=gather_rest, name="fox_fwd")
    w_out_full = g_out.reshape(D_MODEL, D_MODEL)
    w_down_full = g_down.reshape(N_DEV * HID_PAD, D_MODEL)

    mix = jnp.concatenate([o_a, _unheads(o_b)], axis=1)
    a1 = _matmul(mix, w_out_full, out_dtype=F32, tm=tm, tn=512, name="mm_out_proj")
    h1 = _post_res(a1, ln_post_mix, h0, name="post_mix")
    hn2 = _rms_fwd(h1, ln_pre_ffn, name="rms_pre_ffn")
    gu = _matmul(hn2, g_gu, b_shards=True, out_dtype=BF16, tm=tm, name="mm_gate_up")
    act = _swiglu_fwd(gu, name="swiglu_fwd")
    ff = _matmul(act, w_down_full, out_dtype=F32, tm=tm, tn=512, name="mm_down")
    dh2, loss_acc = _loss_head(ff, ln_post_ffn, h1, target, name="loss_head")

    dff, dg_post_ffn = _rms_bwd(ff, ln_post_ffn, dh2, None, out_dtype=BF16, name="rms_bwd_post_ffn")
    dact = _matmul(dff, w_down_full, nt=True, out_dtype=BF16, tm=tm, tn=1536, name="mm_d_act")
    d_w_down = _matmul(act.T, dff, out_dtype=F32, tm=768, tn=512, name="mm_dw_down")
    dgu = _swiglu_bwd(gu, dact, name="swiglu_bwd")
    dhn2 = _matmul(dgu, g_gu, nt=True, b_shards=True, out_dtype=F32, tm=tm, tn=D_MODEL, name="mm_d_hn2")
    d_w_gu = _matmul(hn2.T, dgu, out_shards=True, out_dtype=F32, tm=512, tn=2 * HID_PAD, name="mm_dw_gate_up")
    dh1, dg_pre_ffn = _rms_bwd(h1, ln_pre_ffn, dhn2, dh2, out_dtype=F32, name="rms_bwd_pre_ffn")
    da1, dg_post_mix = _rms_bwd(a1, ln_post_mix, dh1, None, out_dtype=BF16, name="rms_bwd_post_mix")
    dmix = _matmul(da1, w_out_full, nt=True, out_dtype=BF16, tm=tm, tn=512, name="mm_d_mix")
    d_w_out = _matmul(mix.T, da1, out_dtype=F32, tm=512, tn=512, name="mm_dw_out")

    ffn_grads = [d_w_gu, d_w_down.reshape(N_DEV, HID_PAD, D_MODEL)]
    dq_a, dk_a, dv_a, dbc, dbp, dbf, dsk, *ffn_sibling = _swa_bwd(
        proj, dmix, bias_c, bias_p, far, sink_v, ex=_cores_exchange(ffn_grads), name="swa_bwd")
    low = (jnp.arange(2 * HEAD_DIM) < HEAD_DIM)[None, :]
    dk_a = jnp.where(low, dk_a[0], dk_a[1]).astype(BF16)
    dv_a = jnp.where(low, dv_a[0], dv_a[1]).astype(BF16)
    d_tab, d_sink = _small_grads(dbc, dbp, dbf, dsk, jnp.asarray(oh_cur), jnp.asarray(oh_prev), name="small_grads")
    ffn_sums = [_add_cores(g, r, core, name="rs_add_" + t)
                for g, r, t in zip(ffn_grads, ffn_sibling, ["w_gate_up", "w_down"])]

    do_b = _fox_prep_bwd(dmix, o_b, name="fox_prep_bwd")
    dq_t, dk_b, dv_b, dck, *ffn_chips = _fox_bwd(
        q_b, k_b, v_b, do_b, lse_row, ex=_chips_exchange([s[1] for s in ffn_sums]), name="fox_bwd")
    dcq = dq_t[:, :, LANE_QC, :].reshape(FOX_HEADS, T)
    df_t, d_bf = _fox_gates_bwd(dcq, dck.reshape(FOX_HEADS, T), f_t, bf_col, name="fox_gates_bwd")
    dq_b = (dq_t[:, :, :HEAD_DIM, :].transpose(1, 3, 0, 2).reshape(T, FOX_W) * SCALE).astype(BF16)
    dk_b = dk_b[:, :, :HEAD_DIM].transpose(1, 0, 2).reshape(T, FOX_W)
    dv_b = dv_b[:, :, :HEAD_DIM].transpose(1, 0, 2).reshape(T, FOX_W)

    dproj_s = _shard_order([dq_a, dk_a, dv_a, dq_b, dk_b, dv_b, df_t.T.astype(BF16)], cin, W_IN_PAD - cin)
    dhn1 = _matmul(dproj_s, g_in, nt=True, b_shards=True, out_dtype=F32, tm=tm, tn=D_MODEL, name="mm_d_hn1")
    d_w_in = _matmul(hn1.T, dproj_s, out_shards=True, out_dtype=F32, tm=512, tn=W_IN_PAD, name="mm_dw_in")
    dh0, dg_pre_mix = _rms_bwd(h0, ln_pre_mix, dhn1, dh1, out_dtype=F32, name="rms_bwd_pre_mix")
    grad_x = dh0[BLOCK:][None]
    d_meta = dh0[PAD_ROWS:BLOCK]

    mix_grads = [d_w_in, d_w_out.reshape(N_DEV, -1, D_MODEL)]
    tags = ["w_in", "w_out", "w_gate_up", "w_down"]
    mix_sibling = _run_exchange(_cores_exchange(mix_grads), name="rs_cores")
    mix_sums = [_add_cores(g, r, core, name="rs_add_" + t) for g, r, t in zip(mix_grads, mix_sibling, tags[:2])]
    mix_chips = _run_exchange(_chips_exchange([s[1] for s in mix_sums]), name="rs_chips")
    chip_sum = [s[0] for s in mix_sums + ffn_sums]
    from_chips = list(mix_chips) + list(ffn_chips)
    shard_w = [(w_in, m_w_in, v_w_in), (w_out, m_w_out, v_w_out), (w_gate_up, m_w_gate_up, v_w_gate_up),
               (w_down, m_w_down, v_w_down)]
    segs = [[(0, 0, cin)], [(0, 0, D_MODEL)], [(0, 0, hid), (HID_PAD, hid, hid)], [(0, 0, D_MODEL)]]
    tas = [256, BLOCK, 256, hid]
    big = [{}, {}, {}, {}]
    for i, t in enumerate(tags):
        w_t, m_t, v_t = shard_w[i]
        res = _sum_adamw(chip_sum[i], from_chips[i], chip, w_t[0], m_t[0], v_t[0], segs=segs[i], ta=tas[i],
                         name="rs_adamw_" + t)
        for kind in range(4):
            big[kind][t] = res[kind][None]

    loss_row = jnp.pad(loss_acc[0:1, 0:1] * (0.5 / D_MODEL), ((0, 0), (0, D_MODEL - 1)))
    s_small = _pack_small(d_tab.T, dg_pre_mix, dg_post_mix, dg_pre_ffn, dg_post_ffn, d_bf, d_sink,
                          extra=loss_row, meta=d_meta)
    w_s = _pack_small(rel_bias, ln_pre_mix, ln_post_mix, ln_pre_ffn, ln_post_ffn, b_forget, sinks)
    m_s = _pack_small(m_rel_bias, m_ln_pre_mix, m_ln_post_mix, m_ln_pre_ffn, m_ln_post_ffn, m_b_forget, m_sinks)
    v_s = _pack_small(v_rel_bias, v_ln_pre_mix, v_ln_post_mix, v_ln_pre_ffn, v_ln_post_ffn, v_b_forget, v_sinks)
    small = _small_allreduce_adamw(s_small, w_s, m_s, v_s, name="small_allreduce_adamw")
    loss = small[0][5, 0]
    mcols = meta_tokens.shape[1]
    g_meta_mine = lax.dynamic_slice(small[0][8:8 + N_META], (0, (4 * x_i + 2 * y_i + c_i) * mcols), (N_META, mcols))
    big[0]["meta_tokens"] = g_meta_mine
    for kind, arr in enumerate(_adamw(meta_tokens, g_meta_mine, m_meta_tokens, v_meta_tokens, name="adamw_meta")):
        big[kind + 1]["meta_tokens"] = arr
    small = [_unpack_small(p) for p in small]

    names = ["meta_tokens", "rel_bias", "ln_pre_mix", "ln_post_mix", "ln_pre_ffn", "ln_post_ffn", "w_in",
             "b_forget", "sinks", "w_out", "w_gate_up", "w_down"]
    outs = [loss, grad_x]
    for kind in range(4):
        for nme in names:
            outs.append(big[kind][nme] if nme in big[kind] else small[kind][nme])
    return tuple(outs)
```

```python
import math

import numpy as np
import jax
import jax.numpy as jnp
from jax import lax
from jax.experimental import pallas as pl
from jax.experimental.pallas import tpu as pltpu

F32 = jnp.float32
BF16 = jnp.bfloat16
HIGHEST = lax.Precision.HIGHEST
MESH = pl.DeviceIdType.MESH

N_DEV = 8
D_MODEL = 1024
N_META = 16
HEAD_DIM = 64
SWA_Q_HEADS = 8
SWA_KV_HEADS = 2
SWA_GROUP = 4
FOX_HEADS = 8
FOX_W = FOX_HEADS * HEAD_DIM
BLOCK = 128
PAD_ROWS = BLOCK - N_META
N_BUCKETS = 32
MAX_DISTANCE = 128
D_FF = 2816
D_QKV = 2304
D_PROJ = D_QKV + FOX_HEADS
D_PROJ_PAD = 2560
EPS = 1e-6
NEG = -1e30
SCALE = HEAD_DIM ** -0.5
ADAM_LR, ADAM_B1, ADAM_B2, ADAM_EPS, ADAM_WD, ADAM_STEP = 0.001, 0.9, 0.999, 1e-08, 0.01, 10
VMEM_LIMIT = 48 * 1024 * 1024
FOX_TILE = 384
FOX_GROUP = 4
W_IN_PAD = 384
HID_PAD = 384

NT = (((1,), (1,)), ((), ()))
NN = (((1,), (0,)), ((), ()))
TN = (((0,), (0,)), ((), ()))


def _params(sem=None, **kw):
    if sem is not None:
        kw["dimension_semantics"] = sem
    return pltpu.CompilerParams(vmem_limit_bytes=VMEM_LIMIT, **kw)


def _tile(n, target, mult=16):
    best = None
    for t in range(mult, min(n, target) + 1, mult):
        if n % t == 0:
            best = t
    assert best is not None, (n, target)
    return best


def _matmul(a, b, *, nt=False, b_shards=False, out_shards=False, out_dtype, tm, tn=None, tk=None, name):
    M, K = a.shape
    k_shards = b.shape[0] if (b_shards and nt) else 0
    if k_shards:
        N, ks = b.shape[1], b.shape[2]
        assert tk is None and K == k_shards * ks
    elif b_shards:
        N, tn = b.shape[0] * b.shape[2], b.shape[2]
    else:
        N = b.shape[0] if nt else b.shape[1]
    tk = K if tk is None else tk
    assert M % tm == 0 and N % tn == 0 and K % tk == 0, (name, a.shape, b.shape, tm, tn, tk)
    nk = K // tk
    dn = NT if nt else NN

    def body(a_ref, b_ref, o_ref, *scr):
        if k_shards:
            part = sum(lax.dot_general(a_ref[:, s * ks:(s + 1) * ks], b_ref[s], NT, preferred_element_type=F32)
                       for s in range(k_shards))
        else:
            part = lax.dot_general(a_ref[...], b_ref[...], dn, preferred_element_type=F32)
        if nk == 1:
            o_ref[...] = part.astype(o_ref.dtype)
        else:
            acc = scr[0]
            k = pl.program_id(2)

            @pl.when(k == 0)
            def _():
                acc[...] = part

            @pl.when(k > 0)
            def _():
                acc[...] += part

            @pl.when(k == nk - 1)
            def _():
                o_ref[...] = acc[...].astype(o_ref.dtype)

    if k_shards:
        b_spec = pl.BlockSpec((k_shards, tn, ks), lambda i, j, k: (0, j, 0))
    elif b_shards:
        b_spec = pl.BlockSpec((None, tk, tn), lambda i, j, k: (j, k, 0))
    elif nt:
        b_spec = pl.BlockSpec((tn, tk), lambda i, j, k: (j, k))
    else:
        b_spec = pl.BlockSpec((tk, tn), lambda i, j, k: (k, j))
    if out_shards:
        out_shape = jax.ShapeDtypeStruct((N // tn, M, tn), out_dtype)
        out_spec = pl.BlockSpec((None, tm, tn), lambda i, j, k: (j, i, 0))
    else:
        out_shape = jax.ShapeDtypeStruct((M, N), out_dtype)
        out_spec = pl.BlockSpec((tm, tn), lambda i, j, k: (i, j))
    return pl.pallas_call(
        body,
        out_shape=out_shape,
        grid=(M // tm, N // tn, nk),
        in_specs=[pl.BlockSpec((tm, tk), lambda i, j, k: (i, k)), b_spec],
        out_specs=out_spec,
        scratch_shapes=[pltpu.VMEM((tm, tn), F32)] if nk > 1 else [],
        compiler_params=_params(("parallel", "parallel", "arbitrary")),
        name=name,
    )(a, b)


def _rstd(x):
    return lax.rsqrt(jnp.mean(x * x, axis=-1, keepdims=True) + EPS)


def _rms_fwd(x, g, *, name):
    T, D = x.shape
    tm = _tile(T, 512)

    def body(x_ref, g_ref, o_ref):
        x = x_ref[...]
        o_ref[...] = (x * _rstd(x) * g_ref[...]).astype(o_ref.dtype)

    return pl.pallas_call(
        body, out_shape=jax.ShapeDtypeStruct((T, D), BF16), grid=(T // tm,),
        in_specs=[pl.BlockSpec((tm, D), lambda i: (i, 0)), pl.BlockSpec((1, D), lambda i: (0, 0))],
        out_specs=pl.BlockSpec((tm, D), lambda i: (i, 0)),
        compiler_params=_params(("parallel",)), name=name)(x, g)


def _post_res(a, g, h, *, name):
    T, D = a.shape
    tm = _tile(T, 512)

    def body(a_ref, g_ref, h_ref, o_ref):
        a = a_ref[...]
        o_ref[...] = h_ref[...] + a * _rstd(a) * g_ref[...]

    row = pl.BlockSpec((tm, D), lambda i: (i, 0))
    return pl.pallas_call(
        body, out_shape=jax.ShapeDtypeStruct((T, D), F32), grid=(T // tm,),
        in_specs=[row, pl.BlockSpec((1, D), lambda i: (0, 0)), row], out_specs=row,
        compiler_params=_params(("parallel",)), name=name)(a, g, h)


def _loss_head(a, g, h, target, *, name):
    T, D = a.shape
    tm = _tile(T, 512)

    def body(a_ref, g_ref, h_ref, t_ref, dy_ref, loss_ref):
        i = pl.program_id(0)
        a = a_ref[...]
        y = h_ref[...] + a * _rstd(a) * g_ref[...]
        rows = i * tm + lax.broadcasted_iota(jnp.int32, (tm, 1), 0)
        err = jnp.where(rows >= BLOCK, y - t_ref[...], 0.0)
        dy_ref[...] = err / D
        part = jnp.sum(jnp.sum(err * err, axis=1, keepdims=True), axis=0, keepdims=True)

        @pl.when(i == 0)
        def _():
            loss_ref[...] = jnp.zeros_like(loss_ref)

        loss_ref[...] += jnp.broadcast_to(part, loss_ref.shape)

    row = pl.BlockSpec((tm, D), lambda i: (i, 0))
    return pl.pallas_call(
        body, out_shape=(jax.ShapeDtypeStruct((T, D), F32), jax.ShapeDtypeStruct((8, 128), F32)),
        grid=(T // tm,),
        in_specs=[row, pl.BlockSpec((1, D), lambda i: (0, 0)), row, row],
        out_specs=(row, pl.BlockSpec((8, 128), lambda i: (0, 0))),
        compiler_params=_params(("arbitrary",)), name=name)(a, g, h, target)


def _rms_bwd(x, g, dy, res, *, out_dtype, name):
    T, D = x.shape
    tm = _tile(T, 512)
    has_res = res is not None

    def body(*refs):
        if has_res:
            x_ref, g_ref, dy_ref, r_ref, dx_ref, dg_ref = refs
        else:
            x_ref, g_ref, dy_ref, dx_ref, dg_ref = refs
        i = pl.program_id(0)
        x = x_ref[...]
        dy = dy_ref[...].astype(F32)
        r = _rstd(x)
        xh = x * r
        dxh = dy * g_ref[...]
        dx = r * (dxh - xh * jnp.mean(dxh * xh, axis=-1, keepdims=True))
        if has_res:
            dx = dx + r_ref[...]
        dx_ref[...] = dx.astype(dx_ref.dtype)

        @pl.when(i == 0)
        def _():
            dg_ref[...] = jnp.zeros_like(dg_ref)

        dg_ref[...] += jnp.sum(dy * xh, axis=0, keepdims=True)

    row = pl.BlockSpec((tm, D), lambda i: (i, 0))
    vec = pl.BlockSpec((1, D), lambda i: (0, 0))
    ins = [x, g, dy] + ([res] if has_res else [])
    return pl.pallas_call(
        body, out_shape=(jax.ShapeDtypeStruct((T, D), out_dtype), jax.ShapeDtypeStruct((1, D), F32)),
        grid=(T // tm,),
        in_specs=[row, vec, row] + ([row] if has_res else []),
        out_specs=(row, vec),
        compiler_params=_params(("arbitrary",)), name=name)(*ins)


def _swiglu_fwd(gu, *, name):
    T, F2 = gu.shape
    F = F2 // 2
    tm = _tile(T, 384)

    def body(g_ref, u_ref, o_ref):
        g = g_ref[...].astype(F32)
        o_ref[...] = (g / (1.0 + jnp.exp(-g)) * u_ref[...].astype(F32)).astype(o_ref.dtype)

    return pl.pallas_call(
        body, out_shape=jax.ShapeDtypeStruct((T, F), BF16), grid=(T // tm,),
        in_specs=[pl.BlockSpec((tm, F), lambda i: (i, 0)), pl.BlockSpec((tm, F), lambda i: (i, 1))],
        out_specs=pl.BlockSpec((tm, F), lambda i: (i, 0)),
        compiler_params=_params(("parallel",)), name=name)(gu, gu)


def _swiglu_bwd(gu, dact, *, name):
    T, F2 = gu.shape
    F = F2 // 2
    tm = _tile(T, 384)

    def body(g_ref, u_ref, d_ref, o_ref):
        g = g_ref[...].astype(F32)
        u = u_ref[...].astype(F32)
        d = d_ref[...].astype(F32)
        sg = 1.0 / (1.0 + jnp.exp(-g))
        o_ref[:, :F] = (d * u * (sg * (1.0 + g * (1.0 - sg)))).astype(o_ref.dtype)
        o_ref[:, F:] = (d * (g * sg)).astype(o_ref.dtype)

    return pl.pallas_call(
        body, out_shape=jax.ShapeDtypeStruct((T, F2), BF16), grid=(T // tm,),
        in_specs=[pl.BlockSpec((tm, F), lambda i: (i, 0)), pl.BlockSpec((tm, F), lambda i: (i, 1)),
                  pl.BlockSpec((tm, F), lambda i: (i, 0))],
        out_specs=pl.BlockSpec((tm, F2), lambda i: (i, 0)),
        compiler_params=_params(("parallel",)), name=name)(gu, gu, dact)


def _fox_gates_fwd(f_t, b, *, name):
    H, T = f_t.shape
    nb = T // BLOCK

    def body(f_ref, b_ref, cum_ref, col_ref):
        f = f_ref[...] + b_ref[...]
        ls = jnp.minimum(f, 0.0) - jnp.log(1.0 + jnp.exp(-jnp.abs(f)))
        t = lax.broadcasted_iota(jnp.int32, (H, T), 1)
        ls = jnp.where(t >= PAD_ROWS, ls, 0.0)
        upper = (lax.broadcasted_iota(jnp.int32, (BLOCK, BLOCK), 0)
                 <= lax.broadcasted_iota(jnp.int32, (BLOCK, BLOCK), 1)).astype(F32)
        carry = jnp.zeros((H, 1), F32)
        for blk in range(nb):
            seg = ls[:, blk * BLOCK:(blk + 1) * BLOCK]
            pre = jnp.dot(seg, upper, precision=HIGHEST, preferred_element_type=F32) + carry
            cum_ref[:, blk * BLOCK:(blk + 1) * BLOCK] = pre
            col_ref[blk * BLOCK:(blk + 1) * BLOCK, :] = jnp.concatenate(
                [pre, jnp.zeros((BLOCK - H, BLOCK), F32)], axis=0).T
            carry = pre[:, BLOCK - 1:BLOCK]

    vm = pl.BlockSpec(memory_space=pltpu.VMEM)
    return pl.pallas_call(
        body, out_shape=(jax.ShapeDtypeStruct((H, T), F32), jax.ShapeDtypeStruct((T, BLOCK), F32)),
        in_specs=[vm, vm], out_specs=(vm, vm),
        compiler_params=_params(), name=name)(f_t, b)


def _fox_gates_bwd(dcq, dck, f_t, b, *, name):
    H, T = f_t.shape
    nb = T // BLOCK

    def body(dq_ref, d_ref, f_ref, b_ref, df_ref, db_ref):
        lower = (lax.broadcasted_iota(jnp.int32, (BLOCK, BLOCK), 0)
                 >= lax.broadcasted_iota(jnp.int32, (BLOCK, BLOCK), 1)).astype(F32)
        carry = jnp.zeros((H, 1), F32)
        for blk in range(nb - 1, -1, -1):
            seg = dq_ref[:, blk * BLOCK:(blk + 1) * BLOCK] - d_ref[:, blk * BLOCK:(blk + 1) * BLOCK]
            suf = jnp.dot(seg, lower, precision=HIGHEST, preferred_element_type=F32) + carry
            df_ref[:, blk * BLOCK:(blk + 1) * BLOCK] = suf
            carry = suf[:, 0:1]
        f = f_ref[...] + b_ref[...]
        t = lax.broadcasted_iota(jnp.int32, (H, T), 1)
        df = jnp.where(t >= PAD_ROWS, df_ref[...] / (1.0 + jnp.exp(f)), 0.0)
        df_ref[...] = df
        db_ref[...] = jnp.sum(df, axis=1, keepdims=True)

    vm = pl.BlockSpec(memory_space=pltpu.VMEM)
    return pl.pallas_call(
        body, out_shape=(jax.ShapeDtypeStruct((H, T), F32), jax.ShapeDtypeStruct((H, 1), F32)),
        in_specs=[vm, vm, vm, vm], out_specs=(vm, vm),
        compiler_params=_params(), name=name)(dcq, dck, f_t, b)


LANE_KC = HEAD_DIM
LANE_QC = HEAD_DIM + 3
LANE_END = HEAD_DIM + 6


def _split3(c):
    hi = c.astype(BF16).astype(F32)
    r = c - hi
    mid = r.astype(BF16).astype(F32)
    lo = (r - mid).astype(BF16).astype(F32)
    return hi, mid, lo


def _lanes(lane, data, start, terms, rest):
    out = rest
    for i, t in enumerate(terms):
        out = jnp.where(lane == start + i, t, out)
    return jnp.where(lane < HEAD_DIM, data, out)


def _fox_prep(proj, cum_col, *, name):
    T = proj.shape[0]
    tm = FOX_TILE
    nt = T // tm
    H = FOX_HEADS
    lanes = 2 * HEAD_DIM
    qb, kb, vb = 768 // lanes, 1280 // lanes, 1792 // lanes

    def body(q_ref, k_ref, v_ref, c_ref, qa_ref, ka_ref, va_ref):
        p = pl.program_id(0)
        i = pl.program_id(1)
        lane = lax.broadcasted_iota(jnp.int32, (tm, lanes), 1)
        rows = i * tm + lax.broadcasted_iota(jnp.int32, (tm, 1), 0)
        q2 = q_ref[...].astype(F32)
        k2 = k_ref[...].astype(F32)
        v2 = v_ref[...].astype(F32)
        cum = c_ref[...]
        for e in range(2):
            c = jnp.sum(jnp.where(lane == 2 * p + e, cum, 0.0), axis=1, keepdims=True)
            ck = jnp.where(rows >= PAD_ROWS, c, -NEG)
            qe, ke, ve = (q2, k2, v2) if e == 0 else tuple(pltpu.roll(a, HEAD_DIM, 1) for a in (q2, k2, v2))
            one = jnp.where(lane < LANE_END, 1.0, 0.0)
            qa = _lanes(lane, qe * SCALE, LANE_QC, _split3(c), jnp.where(lane < LANE_QC, -1.0, 0.0))
            ka = _lanes(lane, ke, LANE_KC, _split3(ck), one)
            va = jnp.where(lane < HEAD_DIM, ve, jnp.where(lane < LANE_QC, 1.0, 0.0))
            qa_ref[e] = qa.astype(BF16)
            ka_ref[e] = ka.astype(BF16)
            va_ref[e] = va.astype(BF16)

    def col(b):
        return pl.BlockSpec((tm, lanes), lambda p, i, b=b: (i, b + p))

    out = pl.BlockSpec((2, tm, lanes), lambda p, i: (p, i, 0))
    shp = jax.ShapeDtypeStruct((H, T, lanes), BF16)
    return pl.pallas_call(
        body, out_shape=(shp, shp, shp), grid=(H // 2, nt),
        in_specs=[col(qb), col(kb), col(vb), pl.BlockSpec((tm, lanes), lambda p, i: (i, 0))],
        out_specs=(out, out, out),
        compiler_params=_params(("parallel", "parallel")), name=name)(proj, proj, proj, cum_col)


def _fox_fwd(q_aug, k_aug, v_aug, *, ex=None, name):
    H, T, lanes = q_aug.shape
    tq = FOX_TILE
    nq = T // tq
    G = FOX_GROUP

    def body(q_ref, k_ref, v_ref, o_ref, lse_ref, m_scr, acc_scr):
        i = pl.program_id(1)
        m_scr[...] = jnp.full(m_scr.shape, NEG, F32)
        acc_scr[...] = jnp.zeros(acc_scr.shape, F32)

        def step(kb, diag):
            off = pl.multiple_of(kb * tq, tq)
            s_t = [lax.dot_general(k_ref[g, pl.ds(off, tq), :], q_ref[g], NT, preferred_element_type=F32)
                   for g in range(G)]
            if diag:
                r = lax.broadcasted_iota(jnp.int32, (tq, tq), 0)
                c = lax.broadcasted_iota(jnp.int32, (tq, tq), 1)
                s_t = [jnp.where(c >= r, s, NEG) for s in s_t]
            m_prev = [m_scr[g] for g in range(G)]
            m_new = [jnp.maximum(m_prev[g], jnp.max(s_t[g], axis=0, keepdims=True)) for g in range(G)]
            p_t = [jnp.exp(s_t[g] - m_new[g]).astype(BF16) for g in range(G)]
            pv = [lax.dot_general(v_ref[g, pl.ds(off, tq), :], p_t[g], TN, preferred_element_type=F32)
                  for g in range(G)]
            for g in range(G):
                acc_scr[g] = jnp.exp(m_prev[g] - m_new[g]) * acc_scr[g] + pv[g]
                m_scr[g] = m_new[g]

        def loop_body(kb, carry):
            step(kb, False)
            return carry

        lax.fori_loop(0, i, loop_body, 0)
        step(i, True)
        for g in range(G):
            acc = acc_scr[g]
            lse_ref[g] = m_scr[g] + jnp.log(acc[HEAD_DIM:HEAD_DIM + 1, :])
            acc_t = acc.T
            o_ref[g] = (acc_t[:, :HEAD_DIM] / acc_t[:, HEAD_DIM:HEAD_DIM + 1]).astype(o_ref.dtype)

    blk = pl.BlockSpec((G, tq, lanes), lambda h, i: (h, i, 0))
    full = pl.BlockSpec((G, T, lanes), lambda h, i: (h, 0, 0))
    grid = (H // G, nq)
    body, x_in, x_in_specs, x_out, x_out_specs, x_scr = _carry(ex, grid, 3, 2, body)
    return pl.pallas_call(
        body,
        out_shape=(jax.ShapeDtypeStruct((H, T, HEAD_DIM), BF16), jax.ShapeDtypeStruct((H, nq, 1, tq), F32), *x_out),
        grid=grid,
        in_specs=[blk, full, full] + x_in_specs,
        out_specs=(pl.BlockSpec((G, tq, HEAD_DIM), lambda h, i: (h, i, 0)),
                   pl.BlockSpec((G, None, 1, tq), lambda h, i: (h, i, 0, 0)), *x_out_specs),
        scratch_shapes=[pltpu.VMEM((G, 1, tq), F32), pltpu.VMEM((G, lanes, tq), F32)] + x_scr,
        compiler_params=_params(("arbitrary", "arbitrary")), name=name)(q_aug, k_aug, v_aug, *x_in)


def _fox_prep_bwd(dmix, o, *, name):
    T = dmix.shape[0]
    H = o.shape[0]
    tm = FOX_TILE
    lanes = 2 * HEAD_DIM
    first = 512 // lanes

    def body(d_ref, o_ref, da_ref):
        lane = lax.broadcasted_iota(jnp.int32, (tm, lanes), 1)
        d2 = d_ref[...].astype(F32)
        for e in range(2):
            de = d2 if e == 0 else pltpu.roll(d2, HEAD_DIM, 1)
            d64 = d_ref[:, e * HEAD_DIM:(e + 1) * HEAD_DIM].astype(F32)
            delta = jnp.sum(d64 * o_ref[e].astype(F32), axis=1, keepdims=True)
            da_ref[e] = _lanes(lane, de, LANE_KC, _split3(-delta), jnp.zeros((), F32)).astype(BF16)

    return pl.pallas_call(
        body, out_shape=jax.ShapeDtypeStruct((H, T, lanes), BF16), grid=(H // 2, T // tm),
        in_specs=[pl.BlockSpec((tm, lanes), lambda p, i: (i, first + p)),
                  pl.BlockSpec((2, tm, HEAD_DIM), lambda p, i: (p, i, 0))],
        out_specs=pl.BlockSpec((2, tm, lanes), lambda p, i: (p, i, 0)),
        compiler_params=_params(("parallel", "parallel")), name=name)(dmix, o)


def _fox_bwd(q_aug, k_aug, v_aug, do_aug, lse_row, *, ex=None, name):
    H, T, lanes = q_aug.shape
    tq = FOX_TILE
    nq = T // tq
    G = FOX_GROUP

    def body(q_ref, k_ref, v_ref, do_ref, lse_ref, dq_ref, dk_ref, dv_ref, dck_ref, dk_acc, dv_acc):
        j = pl.program_id(1)

        @pl.when(j == 0)
        def _():
            dq_ref[...] = jnp.zeros(dq_ref.shape, F32)

        dk_acc[...] = jnp.zeros(dk_acc.shape, F32)
        dv_acc[...] = jnp.zeros(dv_acc.shape, F32)

        def step(qb, diag):
            off = pl.multiple_of(qb * tq, tq)
            heads = range(G)
            qa = [q_ref[g, pl.ds(off, tq), :] for g in heads]
            da = [do_ref[g, pl.ds(off, tq), :] for g in heads]
            s_t = [lax.dot_general(k_ref[g], qa[g], NT, preferred_element_type=F32) for g in heads]
            dp_t = [lax.dot_general(v_ref[g], da[g], NT, preferred_element_type=F32) for g in heads]
            p_t = [jnp.exp(s_t[g] - lse_ref[g, qb]) for g in heads]
            if diag:
                r = lax.broadcasted_iota(jnp.int32, (tq, tq), 0)
                c = lax.broadcasted_iota(jnp.int32, (tq, tq), 1)
                p_t = [jnp.where(c >= r, p, 0.0) for p in p_t]
            dsb = [(p_t[g] * dp_t[g]).astype(BF16) for g in heads]
            dv = [jnp.dot(p_t[g].astype(BF16), da[g], preferred_element_type=F32) for g in heads]
            dk = [jnp.dot(dsb[g], qa[g], preferred_element_type=F32) for g in heads]
            dq = [jnp.dot(dsb[g].T, k_ref[g], preferred_element_type=F32) for g in heads]
            for g in heads:
                dv_acc[g] += dv[g]
                dk_acc[g] += dk[g]
                dq_ref[g, pl.ds(off, tq), :] += dq[g]

        step(j, True)

        def loop_body(qb, carry):
            step(qb, False)
            return carry

        lax.fori_loop(j + 1, nq, loop_body, 0)
        dk = dk_acc[...]
        dk_ref[...] = dk.astype(dk_ref.dtype)
        dck_ref[...] = -dk[:, :, LANE_KC:LANE_KC + 1]
        dv_ref[...] = dv_acc[...].astype(dv_ref.dtype)

    blk = pl.BlockSpec((G, tq, lanes), lambda h, j: (h, j, 0))
    full = pl.BlockSpec((G, T, lanes), lambda h, j: (h, 0, 0))
    grid = (H // G, nq)
    body, x_in, x_in_specs, x_out, x_out_specs, x_scr = _carry(ex, grid, 5, 4, body)
    return pl.pallas_call(
        body,
        out_shape=(jax.ShapeDtypeStruct((H, T, lanes), F32), jax.ShapeDtypeStruct((H, T, lanes), BF16),
                   jax.ShapeDtypeStruct((H, T, lanes), BF16), jax.ShapeDtypeStruct((H, T, 1), F32), *x_out),
        grid=grid,
        in_specs=[full, blk, blk, full, pl.BlockSpec((G, nq, 1, tq), lambda h, j: (h, 0, 0, 0))] + x_in_specs,
        out_specs=(full, blk, blk,
                   pl.BlockSpec((G, tq, 1), lambda h, j: (h, j, 0)), *x_out_specs),
        scratch_shapes=[pltpu.VMEM((G, tq, lanes), F32), pltpu.VMEM((G, tq, lanes), F32)] + x_scr,
        compiler_params=_params(("arbitrary", "arbitrary")), name=name,
    )(q_aug, k_aug, v_aug, do_aug, lse_row, *x_in)


def _t5_bucket_np(d):
    n = np.maximum(d, 0).astype(np.int32)
    max_exact = N_BUCKETS // 2
    nf = np.maximum(n, 1).astype(np.float32)
    large = max_exact + (np.log(nf / max_exact) / math.log(MAX_DISTANCE / max_exact)
                         * (N_BUCKETS - max_exact)).astype(np.int32)
    large = np.minimum(large, N_BUCKETS - 1)
    return np.where(n < max_exact, n, large)


def _bucket_onehots():
    k = np.arange(BLOCK)[:, None]
    q = np.arange(BLOCK)[None, :]
    eye = np.eye(N_BUCKETS, dtype=np.float32)
    cur = eye[_t5_bucket_np(q - k).reshape(-1)]
    prev = eye[_t5_bucket_np(BLOCK + q - k).reshape(-1)]
    return cur, prev


SWA_K_COL = SWA_Q_HEADS * HEAD_DIM // (2 * HEAD_DIM)
SWA_V_COL = SWA_K_COL + 1


def _swa_terms(kk, qm, bc, bp, far, sink, n):
    k = lax.broadcasted_iota(jnp.int32, (BLOCK, BLOCK), 0)
    q = lax.broadcasted_iota(jnp.int32, (BLOCK, BLOCK), 1)
    never = 2 * BLOCK
    s_c = lax.dot_general(kk[0], qm, NT, preferred_element_type=F32) + bc
    s_p = lax.dot_general(kk[1], qm, NT, preferred_element_type=F32) + bp
    s_m = lax.dot_general(kk[2], qm, NT, preferred_element_type=F32) + jnp.where(n == 1, bp, far)
    s_c = jnp.where((k <= q) & (k >= jnp.where(n >= 1, 0, PAD_ROWS)), s_c, NEG)
    s_p = jnp.where(k > q + jnp.where(n >= 2, 0, never), s_p, NEG)
    s_m = jnp.where(k >= jnp.where(n >= 1, PAD_ROWS, never), s_m, NEG)
    m = jnp.maximum(jnp.maximum(jnp.max(s_c, axis=0, keepdims=True), jnp.max(s_p, axis=0, keepdims=True)),
                    jnp.maximum(jnp.max(s_m, axis=0, keepdims=True), sink))
    e = [jnp.exp(s_c - m), jnp.exp(s_p - m), jnp.exp(s_m - m)]
    e_s = jnp.exp(sink - m)
    l = (jnp.sum(e[0], axis=0, keepdims=True) + jnp.sum(e[1], axis=0, keepdims=True)
         + jnp.sum(e[2], axis=0, keepdims=True) + e_s)
    return e, e_s, l


def _swa_specs():
    G = SWA_GROUP
    width = G * HEAD_DIM

    def rows(which, col):
        if which == "cur":
            return pl.BlockSpec((BLOCK, BLOCK), lambda kv, n: (n, col))
        if which == "prev":
            return pl.BlockSpec((BLOCK, BLOCK), lambda kv, n: (jnp.maximum(n - 1, 0), col))
        return pl.BlockSpec((BLOCK, BLOCK), lambda kv, n: (0, col))

    qblk = pl.BlockSpec((BLOCK, width), lambda kv, n: (n, kv))
    keys = [rows(w, SWA_K_COL) for w in ("cur", "prev", "meta")]
    vals = [rows(w, SWA_V_COL) for w in ("cur", "prev", "meta")]
    bias = pl.BlockSpec((G, BLOCK, BLOCK), lambda kv, n: (kv, 0, 0))
    smem = pl.BlockSpec(memory_space=pltpu.SMEM)
    return qblk, keys, vals, bias, smem


def _swa_own_kv(tile_ref, kv):
    lane = lax.broadcasted_iota(jnp.int32, (BLOCK, 2 * HEAD_DIM), 1)
    t = tile_ref[...].astype(F32)
    return jnp.where(lane // HEAD_DIM == kv, t, pltpu.roll(t, HEAD_DIM, 1)).astype(BF16)


def _swa_fwd(proj, bc, bp, far, sinks, *, name):
    T = proj.shape[0]
    nb = T // BLOCK
    G = SWA_GROUP
    lanes = 2 * HEAD_DIM

    def body(q_ref, kc_ref, kp_ref, km_ref, vc_ref, vp_ref, vm_ref, bc_ref, bp_ref, far_ref, sink_ref, o_ref):
        kv = pl.program_id(0)
        n = pl.program_id(1)
        lane = lax.broadcasted_iota(jnp.int32, (BLOCK, lanes), 1)
        kk = [_swa_own_kv(r, kv) for r in (kc_ref, kp_ref, km_ref)]
        vv = [_swa_own_kv(r, kv) for r in (vc_ref, vp_ref, vm_ref)]
        for pair in range(G // 2):
            q2 = q_ref[:, pair * lanes:(pair + 1) * lanes].astype(F32) * SCALE
            outs = []
            for half in range(2):
                g = 2 * pair + half
                h = kv * G + g
                qm = jnp.where(lane // HEAD_DIM == half, q2, 0.0).astype(BF16)
                e, _, l = _swa_terms(kk, qm, bc_ref[g], bp_ref[g], far_ref[h], sink_ref[h], n)
                o_t = sum(lax.dot_general(vv[b], e[b].astype(BF16), TN, preferred_element_type=F32) for b in range(3))
                outs.append((o_t / l).T)
            o_ref[:, pair * lanes:(pair + 1) * lanes] = jnp.where(lane < HEAD_DIM, outs[0], outs[1]).astype(o_ref.dtype)

    qblk, keys, vals, bias, smem = _swa_specs()
    return pl.pallas_call(
        body, out_shape=jax.ShapeDtypeStruct((T, SWA_Q_HEADS * HEAD_DIM), BF16), grid=(SWA_KV_HEADS, nb),
        in_specs=[qblk] + keys + vals + [bias, bias, smem, smem],
        out_specs=qblk,
        compiler_params=_params(("parallel", "parallel")), name=name,
    )(proj, proj, proj, proj, proj, proj, proj, bc, bp, far, sinks)


def _swa_bwd(proj, dmix, bc, bp, far, sinks, *, ex=None, name):
    T = proj.shape[0]
    nb = T // BLOCK
    G = SWA_GROUP
    Hq = SWA_Q_HEADS
    lanes = 2 * HEAD_DIM

    def body(q_ref, kc_ref, kp_ref, km_ref, vc_ref, vp_ref, vm_ref, do_ref, bc_ref, bp_ref, far_ref, sink_ref,
             dq_ref, dk_ref, dv_ref, dbc_ref, dbp_ref, dbf_ref, dsk_ref):
        kv = pl.program_id(0)
        n = pl.program_id(1)

        @pl.when(n == 0)
        def _():
            for ref in (dk_ref, dv_ref, dbc_ref, dbp_ref, dbf_ref, dsk_ref):
                ref[...] = jnp.zeros(ref.shape, F32)

        lane = lax.broadcasted_iota(jnp.int32, (BLOCK, lanes), 1)
        kk = [_swa_own_kv(r, kv) for r in (kc_ref, kp_ref, km_ref)]
        vv = [_swa_own_kv(r, kv) for r in (vc_ref, vp_ref, vm_ref)]
        dk = [jnp.zeros((BLOCK, lanes), F32) for _ in range(3)]
        dv = [jnp.zeros((BLOCK, lanes), F32) for _ in range(3)]
        for pair in range(G // 2):
            q2 = q_ref[:, pair * lanes:(pair + 1) * lanes].astype(F32) * SCALE
            d2 = do_ref[:, pair * lanes:(pair + 1) * lanes]
            dqs = []
            for half in range(2):
                g = 2 * pair + half
                h = kv * G + g
                own = lane // HEAD_DIM == half
                qm = jnp.where(own, q2, 0.0).astype(BF16)
                dom = jnp.where(own, d2, jnp.zeros_like(d2))
                e, e_s, l = _swa_terms(kk, qm, bc_ref[g], bp_ref[g], far_ref[h], sink_ref[h], n)
                inv = 1.0 / l
                p = [e[b] * inv for b in range(3)]
                dp = [lax.dot_general(vv[b], dom, NT, preferred_element_type=F32) for b in range(3)]
                delta = sum(jnp.sum(p[b] * dp[b], axis=0, keepdims=True) for b in range(3))
                ds = [p[b] * (dp[b] - delta) for b in range(3)]
                dsk_ref[g] += -(e_s * inv) * delta
                dbc_ref[g] += ds[0]
                dbp_ref[g] += ds[1] + jnp.where(n == 1, ds[2], 0.0)
                dbf_ref[g] += jnp.where(n >= 2, ds[2], 0.0)
                ds16 = [d.astype(BF16) for d in ds]
                dq_t = sum(lax.dot_general(kk[b], ds16[b], TN, preferred_element_type=F32) for b in range(3))
                dqs.append(dq_t.T * SCALE)
                for b in range(3):
                    dk[b] += jnp.dot(ds16[b], qm, preferred_element_type=F32)
                    dv[b] += jnp.dot(p[b].astype(BF16), dom, preferred_element_type=F32)
            dq_ref[:, pair * lanes:(pair + 1) * lanes] = jnp.where(lane < HEAD_DIM, dqs[0], dqs[1]).astype(dq_ref.dtype)
        cur_off = pl.multiple_of(n * BLOCK, BLOCK)
        prev_off = pl.multiple_of(jnp.maximum(n - 1, 0) * BLOCK, BLOCK)
        for acc, ref in ((dk, dk_ref), (dv, dv_ref)):
            tot = [a + pltpu.roll(a, HEAD_DIM, 1) for a in acc]
            ref[pl.ds(cur_off, BLOCK), :] += tot[0]
            ref[pl.ds(prev_off, BLOCK), :] += tot[1]
            ref[0:BLOCK, :] += tot[2]

    qblk, keys, vals, bias, smem = _swa_specs()
    kvfull = pl.BlockSpec((None, T, lanes), lambda kv, n: (kv, 0, 0))
    dsk = pl.BlockSpec((G, 1, BLOCK), lambda kv, n: (kv, 0, 0))
    grid = (SWA_KV_HEADS, nb)
    body, x_in, x_in_specs, x_out, x_out_specs, x_scr = _carry(ex, grid, 12, 7, body)
    tile = jax.ShapeDtypeStruct((Hq, BLOCK, BLOCK), F32)
    return pl.pallas_call(
        body,
        out_shape=(jax.ShapeDtypeStruct((T, Hq * HEAD_DIM), BF16),
                   jax.ShapeDtypeStruct((SWA_KV_HEADS, T, lanes), F32),
                   jax.ShapeDtypeStruct((SWA_KV_HEADS, T, lanes), F32),
                   tile, tile, tile, jax.ShapeDtypeStruct((Hq, 1, BLOCK), F32), *x_out),
        grid=grid,
        in_specs=[qblk] + keys + vals + [qblk, bias, bias, smem, smem] + x_in_specs,
        out_specs=(qblk, kvfull, kvfull, bias, bias, bias, dsk, *x_out_specs),
        scratch_shapes=x_scr,
        compiler_params=_params(("arbitrary", "arbitrary")), name=name,
    )(proj, proj, proj, proj, proj, proj, proj, dmix, bc, bp, far, sinks, *x_in)


def _small_grads(dbc, dbp, dbf, dsk, oh_cur, oh_prev, *, name):
    Hq = dbc.shape[0]

    def body(dbc_ref, dbp_ref, dbf_ref, dsk_ref, oc_ref, op_ref, tab_ref, sink_ref):
        tab = (jnp.dot(dbc_ref[...], oc_ref[...], precision=HIGHEST, preferred_element_type=F32)
               + jnp.dot(dbp_ref[...], op_ref[...], precision=HIGHEST, preferred_element_type=F32))
        far = jnp.sum(dbf_ref[...], axis=1, keepdims=True)
        last = lax.broadcasted_iota(jnp.int32, (Hq, N_BUCKETS), 1) == N_BUCKETS - 1
        tab_ref[...] = tab + jnp.where(last, far, 0.0)
        sink_ref[...] = jnp.sum(dsk_ref[...], axis=1, keepdims=True)

    vm = pl.BlockSpec(memory_space=pltpu.VMEM)
    return pl.pallas_call(
        body, out_shape=(jax.ShapeDtypeStruct((Hq, N_BUCKETS), F32), jax.ShapeDtypeStruct((Hq, 1), F32)),
        in_specs=[vm] * 6, out_specs=(vm, vm), compiler_params=_params(), name=name,
    )(dbc.reshape(Hq, -1), dbp.reshape(Hq, -1), dbf.reshape(Hq, -1), dsk.reshape(Hq, -1), oh_cur, oh_prev)


def _coords():
    return lax.axis_index("x"), lax.axis_index("y"), lax.axis_index("c")


class _Exchange:
    def __init__(self, inputs, out_shapes, scratch, start, finish):
        self.inputs, self.out_shapes, self.scratch, self.start, self.finish = inputs, out_shapes, scratch, start, finish


def _carry(ex, grid, n_in, n_out, body):
    if ex is None:
        return body, [], [], [], [], []
    ni, no = len(ex.inputs), len(ex.out_shapes)

    def at_step(which):
        cond = None
        for axis, n in enumerate(grid):
            c = pl.program_id(axis) == (0 if which == "first" else n - 1)
            cond = c if cond is None else cond & c
        return cond

    def wrapped(*refs):
        refs = list(refs)
        n_own_scr = len(refs) - (n_in + ni + n_out + no) - len(ex.scratch)
        own_in, side_in = refs[:n_in], refs[n_in:n_in + ni]
        own_out = refs[n_in + ni:n_in + ni + n_out]
        side_out = refs[n_in + ni + n_out:n_in + ni + n_out + no]
        rest = refs[n_in + ni + n_out + no:]
        own_scr, sems = rest[:n_own_scr], rest[n_own_scr:]

        @pl.when(at_step("first"))
        def _():
            ex.start(side_in, side_out, sems)

        body(*own_in, *own_out, *own_scr)

        @pl.when(at_step("last"))
        def _():
            ex.finish(side_in, side_out, sems)

    hbm = pl.BlockSpec(memory_space=pl.ANY)
    return wrapped, list(ex.inputs), [hbm] * ni, list(ex.out_shapes), [hbm] * no, list(ex.scratch)


def _run_exchange(ex, *, name):
    ni, no = len(ex.inputs), len(ex.out_shapes)

    def body(*refs):
        ins, outs, sems = refs[:ni], refs[ni:ni + no], refs[ni + no:]
        ex.start(ins, outs, sems)
        ex.finish(ins, outs, sems)

    hbm = pl.BlockSpec(memory_space=pl.ANY)
    return pl.pallas_call(
        body, out_shape=tuple(ex.out_shapes), in_specs=[hbm] * ni, out_specs=tuple([hbm] * no),
        scratch_shapes=ex.scratch, compiler_params=_params(), name=name)(*ex.inputs)


def _gather_exchange(shards):
    nt = len(shards)

    def copies(ins, outs, sems):
        send_sems, recv_sems, local_sems = sems
        x, y, c = _coords()
        me, sibling = (x, y, c), (x, y, 1 - c)
        chips = [(1 - x, y), (x, 1 - y), (1 - x, 1 - y)]

        def slot(t, dev):
            return outs[t].at[4 * dev[0] + 2 * dev[1] + dev[2]]

        def copy(t, k, block, to, src=None):
            dst = slot(t, block)
            return pltpu.make_async_remote_copy(
                src_ref=dst if src is None else src, dst_ref=dst,
                send_sem=send_sems.at[t, k], recv_sem=recv_sems.at[t, k], device_id=to, device_id_type=MESH)

        mine = [pltpu.make_async_copy(ins[t], slot(t, me), local_sems.at[t]) for t in range(nt)]
        first = []
        for t in range(nt):
            first.append(copy(t, 0, me, sibling, src=ins[t]))
            first += [copy(t, 1 + j, me, (*chip, c), src=ins[t]) for j, chip in enumerate(chips)]
        return copy, mine, first, me, sibling, chips, c

    def start(ins, outs, sems):
        _, mine, first, *_ = copies(ins, outs, sems)
        for cp in mine + first:
            cp.start()

    def finish(ins, outs, sems):
        copy, mine, first, me, sibling, chips, c = copies(ins, outs, sems)
        passed = []
        for j, chip in enumerate(chips):
            for t in range(nt):
                copy(t, 1 + j, (*chip, c), me).wait_recv()
                cp = copy(t, 4 + j, (*chip, c), sibling)
                cp.start()
                passed.append(cp)
        for t in range(nt):
            copy(t, 0, sibling, me).wait_recv()
            for j, chip in enumerate(chips):
                copy(t, 4 + j, (*chip, 1 - c), me).wait_recv()
        for cp in first + passed:
            cp.wait_send()
        for cp in mine:
            cp.wait()

    return _Exchange(
        list(shards), [jax.ShapeDtypeStruct((N_DEV,) + s.shape, s.dtype) for s in shards],
        [pltpu.SemaphoreType.DMA((nt, 7)), pltpu.SemaphoreType.DMA((nt, 7)), pltpu.SemaphoreType.DMA((nt,))],
        start, finish)


def _swap_exchange(arrays, n_slices, copies):
    nt = len(arrays)

    def start(ins, outs, sems):
        for cp in copies(ins, outs, sems):
            cp.start()

    def finish(ins, outs, sems):
        sends = copies(ins, outs, sems)
        for cp in sends:
            cp.wait_recv()
        for cp in sends:
            cp.wait_send()

    return _Exchange(
        list(arrays), [jax.ShapeDtypeStruct((n_slices,) + a.shape[1:], a.dtype) for a in arrays],
        [pltpu.SemaphoreType.DMA((nt, n_slices)), pltpu.SemaphoreType.DMA((nt, n_slices))], start, finish)


def _cores_exchange(gs):
    def copies(ins, outs, sems):
        send_sems, recv_sems = sems
        x, y, c = _coords()
        return [pltpu.make_async_remote_copy(
            src_ref=ins[t].at[2 * j + (1 - c)], dst_ref=outs[t].at[j],
            send_sem=send_sems.at[t, j], recv_sem=recv_sems.at[t, j], device_id=(x, y, 1 - c), device_id_type=MESH)
            for t in range(len(gs)) for j in range(4)]

    return _swap_exchange(gs, 4, copies)


def _chips_exchange(ps):
    def copies(ins, outs, sems):
        send_sems, recv_sems = sems
        x, y, c = _coords()
        peers = [(1 - x, y), (x, 1 - y), (1 - x, 1 - y)]
        return [pltpu.make_async_remote_copy(
            src_ref=ins[t].at[2 * px + py], dst_ref=outs[t].at[k],
            send_sem=send_sems.at[t, k], recv_sem=recv_sems.at[t, k], device_id=(px, py, c), device_id_type=MESH)
            for t in range(len(ps)) for k, (px, py) in enumerate(peers)]

    return _swap_exchange(ps, 3, copies)


def _add_cores(g, r, core, *, name):
    _, A, B = g.shape
    ta = _tile(A, 512, 16)

    def body(core_ref, a_ref, b_ref, o_ref, o16_ref):
        s = a_ref[...] + b_ref[...]
        o_ref[...] = s
        o16_ref[...] = s.astype(BF16)

    blk = (None, ta, B)
    out = pl.BlockSpec(blk, lambda j, i, core_ref: (j, i, 0))
    return pl.pallas_call(
        body, out_shape=(jax.ShapeDtypeStruct((4, A, B), F32), jax.ShapeDtypeStruct((4, A, B), BF16)),
        grid_spec=pltpu.PrefetchScalarGridSpec(
            num_scalar_prefetch=1, grid=(4, A // ta),
            in_specs=[pl.BlockSpec(blk, lambda j, i, core_ref: (2 * j + core_ref[0], i, 0)),
                      pl.BlockSpec(blk, lambda j, i, core_ref: (j, i, 0))],
            out_specs=(out, out)),
        compiler_params=_params(("parallel", "parallel")), name=name)(core, g, r)


def _adamw_math(w, g, m, v):
    m = ADAM_B1 * m + (1.0 - ADAM_B1) * g
    v = ADAM_B2 * v + (1.0 - ADAM_B2) * (g * g)
    m_hat = m / (1.0 - ADAM_B1 ** ADAM_STEP)
    v_hat = v / (1.0 - ADAM_B2 ** ADAM_STEP)
    delta = -ADAM_LR * (m_hat / (jnp.sqrt(v_hat) + ADAM_EPS) + ADAM_WD * w)
    return delta, m, v


def _sum_adamw(p, r, chip, w, m, v, *, segs, ta, name):
    Aw, Bw = w.shape
    Bg = p.shape[2]
    assert Aw % ta == 0

    def body(chip_ref, p_ref, r0, r1, r2, w_ref, m_ref, v_ref, g_out, d_out, m_out, v_out):
        for gc, wc, n in segs:
            g = ((p_ref[:, gc:gc + n] + r0[:, gc:gc + n].astype(F32)) + r1[:, gc:gc + n].astype(F32)
                 ) + r2[:, gc:gc + n].astype(F32)
            delta, m_new, v_new = _adamw_math(w_ref[:, wc:wc + n], g, m_ref[:, wc:wc + n], v_ref[:, wc:wc + n])
            g_out[:, wc:wc + n] = g
            d_out[:, wc:wc + n] = delta
            m_out[:, wc:wc + n] = m_new
            v_out[:, wc:wc + n] = v_new

    gblk = (None, ta, Bg)
    row = pl.BlockSpec((ta, Bw), lambda i, chip_ref: (i, 0))
    rspecs = [pl.BlockSpec(gblk, (lambda i, chip_ref, k=k: (k, i, 0))) for k in range(3)]
    shp = jax.ShapeDtypeStruct((Aw, Bw), F32)
    return pl.pallas_call(
        body, out_shape=(shp, shp, shp, shp),
        grid_spec=pltpu.PrefetchScalarGridSpec(
            num_scalar_prefetch=1, grid=(Aw // ta,),
            in_specs=[pl.BlockSpec(gblk, lambda i, chip_ref: (chip_ref[0], i, 0))] + rspecs + [row, row, row],
            out_specs=(row, row, row, row)),
        compiler_params=_params(("parallel",)), name=name)(chip, p, r, r, r, w, m, v)


def _adamw(w, g, m, v, *, name):
    def body(w_ref, g_ref, m_ref, v_ref, d_out, m_out, v_out):
        delta, m_new, v_new = _adamw_math(w_ref[...], g_ref[...], m_ref[...], v_ref[...])
        d_out[...] = delta
        m_out[...] = m_new
        v_out[...] = v_new

    vm = pl.BlockSpec(memory_space=pltpu.VMEM)
    shp = jax.ShapeDtypeStruct(w.shape, F32)
    return pl.pallas_call(body, out_shape=(shp, shp, shp), in_specs=[vm] * 4, out_specs=(vm, vm, vm),
                          compiler_params=_params(), name=name)(w, g, m, v)


def _small_allreduce_adamw(s, w, m, v, *, name):
    R, W = s.shape

    def body(s_ref, w_ref, m_ref, v_ref, g_out, d_out, m_out, v_out, gath, send_sems, recv_sems):
        x, y, c = _coords()
        mine = 4 * x + 2 * y + c
        gath[mine] = s_ref[...]
        peers = [((1 - x) if k & 4 else x, (1 - y) if k & 2 else y, (1 - c) if k & 1 else c) for k in range(1, N_DEV)]
        sends = []
        for k in range(1, N_DEV):
            peer = peers[k - 1]
            sends.append(pltpu.make_async_remote_copy(
                src_ref=s_ref, dst_ref=gath.at[mine], send_sem=send_sems.at[k - 1], recv_sem=recv_sems.at[k - 1],
                device_id=peer, device_id_type=MESH))
        for cp in sends:
            cp.start()
        for k in range(1, N_DEV):
            peer = peers[k - 1]
            pltpu.make_async_remote_copy(
                src_ref=s_ref, dst_ref=gath.at[4 * peer[0] + 2 * peer[1] + peer[2]],
                send_sem=send_sems.at[k - 1], recv_sem=recv_sems.at[k - 1],
                device_id=peer, device_id_type=MESH).wait_recv()
        for cp in sends:
            cp.wait_send()
        g = gath[0]
        for d in range(1, N_DEV):
            g = g + gath[d]
        delta, m_new, v_new = _adamw_math(w_ref[...], g, m_ref[...], v_ref[...])
        g_out[...] = g
        d_out[...] = delta
        m_out[...] = m_new
        v_out[...] = v_new

    vm = pl.BlockSpec(memory_space=pltpu.VMEM)
    shp = jax.ShapeDtypeStruct((R, W), F32)
    return pl.pallas_call(
        body, out_shape=(shp, shp, shp, shp), in_specs=[vm] * 4, out_specs=(vm, vm, vm, vm),
        scratch_shapes=[pltpu.VMEM((N_DEV, R, W), F32), pltpu.SemaphoreType.DMA((N_DEV - 1,)),
                        pltpu.SemaphoreType.DMA((N_DEV - 1,))],
        compiler_params=_params(), name=name)(s, w, m, v)


def _pack_small(rel_bias, g1, g2, g3, g4, b_forget, sinks, extra=None, meta=None):
    misc = jnp.concatenate([rel_bias.reshape(-1), b_forget.reshape(-1), sinks.reshape(-1)])
    misc = jnp.concatenate([misc, jnp.zeros((D_MODEL - misc.shape[0],), F32)])[None]
    last = jnp.zeros((1, D_MODEL), F32) if extra is None else extra
    meta = jnp.zeros((N_META, D_MODEL), F32) if meta is None else meta
    return jnp.concatenate([g1, g2, g3, g4, misc, last, jnp.zeros((2, D_MODEL), F32), meta], axis=0)


def _unpack_small(p):
    nrb = N_BUCKETS * SWA_Q_HEADS
    misc = p[4]
    return dict(rel_bias=misc[:nrb].reshape(N_BUCKETS, SWA_Q_HEADS), ln_pre_mix=p[0:1], ln_post_mix=p[1:2],
                ln_pre_ffn=p[2:3], ln_post_ffn=p[3:4], b_forget=misc[nrb:nrb + 8].reshape(1, 8),
                sinks=misc[nrb + 8:nrb + 16].reshape(1, 8))


def _shard_order(pieces, shard, pad):
    T, dtype = pieces[0].shape[0], pieces[0].dtype
    total = sum(p.shape[1] for p in pieces)
    assert total % shard == 0
    out, zeros = [], jnp.zeros((T, pad), dtype)
    for s in range(total // shard):
        lo, hi, start = s * shard, (s + 1) * shard, 0
        for p in pieces:
            end = start + p.shape[1]
            if max(lo, start) < min(hi, end):
                out.append(p[:, max(lo, start) - start:min(hi, end) - start])
            start = end
        out.append(zeros)
    return jnp.concatenate(out, axis=1)


def _unheads(a):
    return a.transpose(1, 0, 2).reshape(a.shape[1], -1)


def kernel(x, meta_tokens, rel_bias, ln_pre_mix, ln_post_mix, ln_pre_ffn, ln_post_ffn, w_in, b_forget, sinks, w_out, w_gate_up, w_down, loss_target, m_meta_tokens, m_rel_bias, m_ln_pre_mix, m_ln_post_mix, m_ln_pre_ffn, m_ln_post_ffn, m_w_in, m_b_forget, m_sinks, m_w_out, m_w_gate_up, m_w_down, v_meta_tokens, v_rel_bias, v_ln_pre_mix, v_ln_post_mix, v_ln_pre_ffn, v_ln_post_ffn, v_w_in, v_b_forget, v_sinks, v_w_out, v_w_gate_up, v_w_down):
    seq = x.shape[1]
    T = BLOCK + seq
    assert T % FOX_TILE == 0
    nq = T // FOX_TILE
    tm = _tile(T, 1056)
    cin = w_in.shape[2]
    hid = w_down.shape[1]
    assert w_gate_up.shape[2] == 2 * hid and cin <= W_IN_PAD and hid <= HID_PAD

    x_i, y_i, c_i = _coords()
    core = jnp.reshape(c_i, (1,)).astype(jnp.int32)
    chip = jnp.reshape(2 * x_i + y_i, (1,)).astype(jnp.int32)
    w_in_s = jnp.pad(w_in[0].astype(BF16), ((0, 0), (0, W_IN_PAD - cin)))
    w_gu_s = jnp.pad(w_gate_up[0].astype(BF16).reshape(D_MODEL, 2, hid), ((0, 0), (0, 0), (0, HID_PAD - hid)))
    w_gu_s = w_gu_s.reshape(D_MODEL, 2 * HID_PAD)
    w_down_s = jnp.pad(w_down[0].astype(BF16), ((0, HID_PAD - hid), (0, 0)))
    g_in, g_meta = _run_exchange(_gather_exchange([w_in_s, meta_tokens]), name="ag_w_in")
    gather_rest = _gather_exchange([w_out[0].astype(BF16), w_gu_s, w_down_s])
    w_in_full = g_in[:, :, :cin].transpose(1, 0, 2).reshape(D_MODEL, N_DEV * cin)
    w_qkv = w_in_full[:, :D_QKV]
    w_f = jnp.pad(w_in_full[:, D_QKV:], ((0, 0), (0, BLOCK - FOX_HEADS)))
    meta_full = g_meta.transpose(1, 0, 2).reshape(N_META, D_MODEL)

    h0 = jnp.concatenate([jnp.zeros((PAD_ROWS, D_MODEL), F32), meta_full, x[0]], axis=0)
    target = jnp.concatenate([jnp.zeros((BLOCK, D_MODEL), F32), loss_target[0]], axis=0)
    hn1 = _rms_fwd(h0, ln_pre_mix, name="rms_pre_mix")
    proj = _matmul(hn1, w_qkv, out_dtype=BF16, tm=tm, tn=768, name="mm_in_proj")
    proj_f = _matmul(hn1, w_f, out_dtype=F32, tm=tm, tn=BLOCK, name="mm_in_proj_f")

    f_t = proj_f[:, :FOX_HEADS].T
    bf_col = b_forget.reshape(FOX_HEADS, 1)

    oh_cur, oh_prev = _bucket_onehots()
    bias_c = jnp.einsum("pb,bh->hp", jnp.asarray(oh_cur), rel_bias, precision=HIGHEST).reshape(8, BLOCK, BLOCK)
    bias_p = jnp.einsum("pb,bh->hp", jnp.asarray(oh_prev), rel_bias, precision=HIGHEST).reshape(8, BLOCK, BLOCK)
    far = rel_bias[N_BUCKETS - 1]
    sink_v = sinks[0]
    o_a = _swa_fwd(proj, bias_c, bias_p, far, sink_v, name="swa_fwd")

    _, cum_col = _fox_gates_fwd(f_t, bf_col, name="fox_gates_fwd")
    q_b, k_b, v_b = _fox_prep(proj, cum_col, name="fox_prep")
    o_b, lse_row, g_out, g_gu, g_down = _fox_fwd(q_b, k_b, v_b, ex=gather_rest, name="fox_fwd")
    w_out_full = g_out.reshape(D_MODEL, D_MODEL)
    w_down_full = g_down.reshape(N_DEV * HID_PAD, D_MODEL)

    mix = jnp.concatenate([o_a, _unheads(o_b)], axis=1)
    a1 = _matmul(mix, w_out_full, out_dtype=F32, tm=tm, tn=512, name="mm_out_proj")
    h1 = _post_res(a1, ln_post_mix, h0, name="post_mix")
    hn2 = _rms_fwd(h1, ln_pre_ffn, name="rms_pre_ffn")
    gu = _matmul(hn2, g_gu, b_shards=True, out_dtype=BF16, tm=tm, name="mm_gate_up")
    act = _swiglu_fwd(gu, name="swiglu_fwd")
    ff = _matmul(act, w_down_full, out_dtype=F32, tm=tm, tn=512, name="mm_down")
    dh2, loss_acc = _loss_head(ff, ln_post_ffn, h1, target, name="loss_head")

    dff, dg_post_ffn = _rms_bwd(ff, ln_post_ffn, dh2, None, out_dtype=BF16, name="rms_bwd_post_ffn")
    dact = _matmul(dff, w_down_full, nt=True, out_dtype=BF16, tm=tm, tn=1536, name="mm_d_act")
    d_w_down = _matmul(act.T, dff, out_dtype=F32, tm=768, tn=512, name="mm_dw_down")
    dgu = _swiglu_bwd(gu, dact, name="swiglu_bwd")
    dhn2 = _matmul(dgu, g_gu, nt=True, b_shards=True, out_dtype=F32, tm=_tile(T, 528), tn=512, name="mm_d_hn2")
    d_w_gu = _matmul(hn2.T, dgu, out_shards=True, out_dtype=F32, tm=512, tn=2 * HID_PAD, name="mm_dw_gate_up")
    dh1, dg_pre_ffn = _rms_bwd(h1, ln_pre_ffn, dhn2, dh2, out_dtype=F32, name="rms_bwd_pre_ffn")
    da1, dg_post_mix = _rms_bwd(a1, ln_post_mix, dh1, None, out_dtype=BF16, name="rms_bwd_post_mix")
    dmix = _matmul(da1, w_out_full, nt=True, out_dtype=BF16, tm=tm, tn=512, name="mm_d_mix")
    d_w_out = _matmul(mix.T, da1, out_dtype=F32, tm=512, tn=512, name="mm_dw_out")

    ffn_grads = [d_w_gu, d_w_down.reshape(N_DEV, HID_PAD, D_MODEL)]
    dq_a, dk_a, dv_a, dbc, dbp, dbf, dsk, *ffn_sibling = _swa_bwd(
        proj, dmix, bias_c, bias_p, far, sink_v, ex=_cores_exchange(ffn_grads), name="swa_bwd")
    low = (jnp.arange(2 * HEAD_DIM) < HEAD_DIM)[None, :]
    dk_a = jnp.where(low, dk_a[0], dk_a[1]).astype(BF16)
    dv_a = jnp.where(low, dv_a[0], dv_a[1]).astype(BF16)
    d_tab, d_sink = _small_grads(dbc, dbp, dbf, dsk, jnp.asarray(oh_cur), jnp.asarray(oh_prev), name="small_grads")
    ffn_sums = [_add_cores(g, r, core, name="rs_add_" + t)
                for g, r, t in zip(ffn_grads, ffn_sibling, ["w_gate_up", "w_down"])]

    do_b = _fox_prep_bwd(dmix, o_b, name="fox_prep_bwd")
    dq_t, dk_b, dv_b, dck, *ffn_chips = _fox_bwd(
        q_b, k_b, v_b, do_b, lse_row, ex=_chips_exchange([s[1] for s in ffn_sums]), name="fox_bwd")
    dcq = dq_t[:, :, LANE_QC]
    df_t, d_bf = _fox_gates_bwd(dcq, dck.reshape(FOX_HEADS, T), f_t, bf_col, name="fox_gates_bwd")
    dq_b = (dq_t[:, :, :HEAD_DIM].transpose(1, 0, 2).reshape(T, FOX_W) * SCALE).astype(BF16)
    dk_b = dk_b[:, :, :HEAD_DIM].transpose(1, 0, 2).reshape(T, FOX_W)
    dv_b = dv_b[:, :, :HEAD_DIM].transpose(1, 0, 2).reshape(T, FOX_W)

    dproj_s = _shard_order([dq_a, dk_a, dv_a, dq_b, dk_b, dv_b, df_t.T.astype(BF16)], cin, W_IN_PAD - cin)
    dhn1 = _matmul(dproj_s, g_in, nt=True, b_shards=True, out_dtype=F32, tm=tm, tn=512, name="mm_d_hn1")
    d_w_in = _matmul(hn1.T, dproj_s, out_shards=True, out_dtype=F32, tm=512, tn=W_IN_PAD, name="mm_dw_in")
    dh0, dg_pre_mix = _rms_bwd(h0, ln_pre_mix, dhn1, dh1, out_dtype=F32, name="rms_bwd_pre_mix")
    grad_x = dh0[BLOCK:][None]
    d_meta = dh0[PAD_ROWS:BLOCK]

    mix_grads = [d_w_in, d_w_out.reshape(N_DEV, -1, D_MODEL)]
    tags = ["w_in", "w_out", "w_gate_up", "w_down"]
    mix_sibling = _run_exchange(_cores_exchange(mix_grads), name="rs_cores")
    mix_sums = [_add_cores(g, r, core, name="rs_add_" + t) for g, r, t in zip(mix_grads, mix_sibling, tags[:2])]
    mix_chips = _run_exchange(_chips_exchange([s[1] for s in mix_sums]), name="rs_chips")
    chip_sum = [s[0] for s in mix_sums + ffn_sums]
    from_chips = list(mix_chips) + list(ffn_chips)
    shard_w = [(w_in, m_w_in, v_w_in), (w_out, m_w_out, v_w_out), (w_gate_up, m_w_gate_up, v_w_gate_up),
               (w_down, m_w_down, v_w_down)]
    segs = [[(0, 0, cin)], [(0, 0, D_MODEL)], [(0, 0, hid), (HID_PAD, hid, hid)], [(0, 0, D_MODEL)]]
    tas = [256, BLOCK, 256, hid]
    big = [{}, {}, {}, {}]
    for i, t in enumerate(tags):
        w_t, m_t, v_t = shard_w[i]
        res = _sum_adamw(chip_sum[i], from_chips[i], chip, w_t[0], m_t[0], v_t[0], segs=segs[i], ta=tas[i],
                         name="rs_adamw_" + t)
        for kind in range(4):
            big[kind][t] = res[kind][None]

    loss_row = jnp.pad(loss_acc[0:1, 0:1] * (0.5 / D_MODEL), ((0, 0), (0, D_MODEL - 1)))
    s_small = _pack_small(d_tab.T, dg_pre_mix, dg_post_mix, dg_pre_ffn, dg_post_ffn, d_bf, d_sink,
                          extra=loss_row, meta=d_meta)
    w_s = _pack_small(rel_bias, ln_pre_mix, ln_post_mix, ln_pre_ffn, ln_post_ffn, b_forget, sinks)
    m_s = _pack_small(m_rel_bias, m_ln_pre_mix, m_ln_post_mix, m_ln_pre_ffn, m_ln_post_ffn, m_b_forget, m_sinks)
    v_s = _pack_small(v_rel_bias, v_ln_pre_mix, v_ln_post_mix, v_ln_pre_ffn, v_ln_post_ffn, v_b_forget, v_sinks)
    small = _small_allreduce_adamw(s_small, w_s, m_s, v_s, name="small_allreduce_adamw")
    loss = small[0][5, 0]
    mcols = meta_tokens.shape[1]
    g_meta_mine = lax.dynamic_slice(small[0][8:8 + N_META], (0, (4 * x_i + 2 * y_i + c_i) * mcols), (N_META, mcols))
    big[0]["meta_tokens"] = g_meta_mine
    for kind, arr in enumerate(_adamw(meta_tokens, g_meta_mine, m_meta_tokens, v_meta_tokens, name="adamw_meta")):
        big[kind + 1]["meta_tokens"] = arr
    small = [_unpack_small(p) for p in small]

    names = ["meta_tokens", "rel_bias", "ln_pre_mix", "ln_post_mix", "ln_pre_ffn", "ln_post_ffn", "w_in",
             "b_forget", "sinks", "w_out", "w_gate_up", "w_down"]
    outs = [loss, grad_x]
    for kind in range(4):
        for nme in names:
            outs.append(big[kind][nme] if nme in big[kind] else small[kind][nme])
    return tuple(outs)
```

```python
import math

import numpy as np
import jax
import jax.numpy as jnp
from jax import lax
from jax.experimental import pallas as pl
from jax.experimental.pallas import tpu as pltpu

F32 = jnp.float32
BF16 = jnp.bfloat16
HIGHEST = lax.Precision.HIGHEST
MESH = pl.DeviceIdType.MESH

N_DEV = 8
D_MODEL = 1024
N_META = 16
HEAD_DIM = 64
SWA_Q_HEADS = 8
SWA_KV_HEADS = 2
SWA_GROUP = 4
FOX_HEADS = 8
FOX_W = FOX_HEADS * HEAD_DIM
BLOCK = 128
PAD_ROWS = BLOCK - N_META
N_BUCKETS = 32
MAX_DISTANCE = 128
D_FF = 2816
D_QKV = 2304
D_PROJ = D_QKV + FOX_HEADS
D_PROJ_PAD = 2560
EPS = 1e-6
NEG = -1e30
SCALE = HEAD_DIM ** -0.5
ADAM_LR, ADAM_B1, ADAM_B2, ADAM_EPS, ADAM_WD, ADAM_STEP = 0.001, 0.9, 0.999, 1e-08, 0.01, 10
VMEM_LIMIT = 48 * 1024 * 1024
FOX_TILE = 384
FOX_GROUP = 4
W_IN_PAD = 384
HID_PAD = 384

NT = (((1,), (1,)), ((), ()))
NN = (((1,), (0,)), ((), ()))
TN = (((0,), (0,)), ((), ()))


def _params(sem=None, **kw):
    if sem is not None:
        kw["dimension_semantics"] = sem
    return pltpu.CompilerParams(vmem_limit_bytes=VMEM_LIMIT, **kw)


def _tile(n, target, mult=16):
    best = None
    for t in range(mult, min(n, target) + 1, mult):
        if n % t == 0:
            best = t
    assert best is not None, (n, target)
    return best


def _matmul(a, b, *, nt=False, b_shards=False, out_shards=False, out_dtype, tm, tn=None, tk=None, ex=None, name):
    M, K = a.shape
    k_shards = b.shape[0] if (b_shards and nt) else 0
    if k_shards:
        N, ks = b.shape[1], b.shape[2]
        assert tk is None and K == k_shards * ks
    elif b_shards:
        N, tn = b.shape[0] * b.shape[2], b.shape[2]
    else:
        N = b.shape[0] if nt else b.shape[1]
    tk = K if tk is None else tk
    assert M % tm == 0 and N % tn == 0 and K % tk == 0, (name, a.shape, b.shape, tm, tn, tk)
    nk = K // tk
    dn = NT if nt else NN

    def body(a_ref, b_ref, o_ref, *scr):
        if k_shards:
            part = sum(lax.dot_general(a_ref[:, s * ks:(s + 1) * ks], b_ref[s], NT, preferred_element_type=F32)
                       for s in range(k_shards))
        else:
            part = lax.dot_general(a_ref[...], b_ref[...], dn, preferred_element_type=F32)
        if nk == 1:
            o_ref[...] = part.astype(o_ref.dtype)
        else:
            acc = scr[0]
            k = pl.program_id(2)

            @pl.when(k == 0)
            def _():
                acc[...] = part

            @pl.when(k > 0)
            def _():
                acc[...] += part

            @pl.when(k == nk - 1)
            def _():
                o_ref[...] = acc[...].astype(o_ref.dtype)

    if k_shards:
        b_spec = pl.BlockSpec((k_shards, tn, ks), lambda i, j, k: (0, j, 0))
    elif b_shards:
        b_spec = pl.BlockSpec((None, tk, tn), lambda i, j, k: (j, k, 0))
    elif nt:
        b_spec = pl.BlockSpec((tn, tk), lambda i, j, k: (j, k))
    else:
        b_spec = pl.BlockSpec((tk, tn), lambda i, j, k: (k, j))
    if out_shards:
        out_shape = jax.ShapeDtypeStruct((N // tn, M, tn), out_dtype)
        out_spec = pl.BlockSpec((None, tm, tn), lambda i, j, k: (j, i, 0))
    else:
        out_shape = jax.ShapeDtypeStruct((M, N), out_dtype)
        out_spec = pl.BlockSpec((tm, tn), lambda i, j, k: (i, j))
    grid = (M // tm, N // tn, nk)
    body, x_in, x_in_specs, x_out, x_out_specs, x_scr = _carry(ex, grid, 2, 1, body)
    res = pl.pallas_call(
        body,
        out_shape=(out_shape, *x_out),
        grid=grid,
        in_specs=[pl.BlockSpec((tm, tk), lambda i, j, k: (i, k)), b_spec] + x_in_specs,
        out_specs=(out_spec, *x_out_specs),
        scratch_shapes=([pltpu.VMEM((tm, tn), F32)] if nk > 1 else []) + x_scr,
        compiler_params=_params(("parallel", "parallel", "arbitrary") if ex is None else ("arbitrary",) * 3),
        name=name,
    )(a, b, *x_in)
    return res[0] if ex is None else res


def _rstd(x):
    return lax.rsqrt(jnp.mean(x * x, axis=-1, keepdims=True) + EPS)


def _rms_fwd(x, g, *, name):
    T, D = x.shape
    tm = _tile(T, 512)

    def body(x_ref, g_ref, o_ref, ot_ref):
        x = x_ref[...]
        y = x * _rstd(x) * g_ref[...]
        o_ref[...] = y.astype(o_ref.dtype)
        ot_ref[...] = y.T.astype(ot_ref.dtype)

    return pl.pallas_call(
        body, out_shape=(jax.ShapeDtypeStruct((T, D), BF16), jax.ShapeDtypeStruct((D, T), BF16)), grid=(T // tm,),
        in_specs=[pl.BlockSpec((tm, D), lambda i: (i, 0)), pl.BlockSpec((1, D), lambda i: (0, 0))],
        out_specs=(pl.BlockSpec((tm, D), lambda i: (i, 0)), pl.BlockSpec((D, tm), lambda i: (0, i))),
        compiler_params=_params(("parallel",)), name=name)(x, g)


def _post_res(a, g, h, *, name):
    T, D = a.shape
    tm = _tile(T, 512)

    def body(a_ref, g_ref, h_ref, o_ref):
        a = a_ref[...]
        o_ref[...] = h_ref[...] + a * _rstd(a) * g_ref[...]

    row = pl.BlockSpec((tm, D), lambda i: (i, 0))
    return pl.pallas_call(
        body, out_shape=jax.ShapeDtypeStruct((T, D), F32), grid=(T // tm,),
        in_specs=[row, pl.BlockSpec((1, D), lambda i: (0, 0)), row], out_specs=row,
        compiler_params=_params(("parallel",)), name=name)(a, g, h)


def _loss_head(a, g, h, target, *, name):
    T, D = a.shape
    tm = _tile(T, 512)

    def body(a_ref, g_ref, h_ref, t_ref, dy_ref, loss_ref):
        i = pl.program_id(0)
        a = a_ref[...]
        y = h_ref[...] + a * _rstd(a) * g_ref[...]
        rows = i * tm + lax.broadcasted_iota(jnp.int32, (tm, 1), 0)
        err = jnp.where(rows >= BLOCK, y - t_ref[...], 0.0)
        dy_ref[...] = err / D
        part = jnp.sum(jnp.sum(err * err, axis=1, keepdims=True), axis=0, keepdims=True)

        @pl.when(i == 0)
        def _():
            loss_ref[...] = jnp.zeros_like(loss_ref)

        loss_ref[...] += jnp.broadcast_to(part, loss_ref.shape)

    row = pl.BlockSpec((tm, D), lambda i: (i, 0))
    return pl.pallas_call(
        body, out_shape=(jax.ShapeDtypeStruct((T, D), F32), jax.ShapeDtypeStruct((8, 128), F32)),
        grid=(T // tm,),
        in_specs=[row, pl.BlockSpec((1, D), lambda i: (0, 0)), row, row],
        out_specs=(row, pl.BlockSpec((8, 128), lambda i: (0, 0))),
        compiler_params=_params(("arbitrary",)), name=name)(a, g, h, target)


def _rms_bwd(x, g, dy, res, *, out_dtype, ex=None, name):
    T, D = x.shape
    tm = _tile(T, 512)
    has_res = res is not None

    def body(*refs):
        if has_res:
            x_ref, g_ref, dy_ref, r_ref, dx_ref, dg_ref = refs
        else:
            x_ref, g_ref, dy_ref, dx_ref, dg_ref = refs
        i = pl.program_id(0)
        x = x_ref[...]
        dy = dy_ref[...].astype(F32)
        r = _rstd(x)
        xh = x * r
        dxh = dy * g_ref[...]
        dx = r * (dxh - xh * jnp.mean(dxh * xh, axis=-1, keepdims=True))
        if has_res:
            dx = dx + r_ref[...]
        dx_ref[...] = dx.astype(dx_ref.dtype)

        @pl.when(i == 0)
        def _():
            dg_ref[...] = jnp.zeros_like(dg_ref)

        dg_ref[...] += jnp.sum(dy * xh, axis=0, keepdims=True)

    row = pl.BlockSpec((tm, D), lambda i: (i, 0))
    vec = pl.BlockSpec((1, D), lambda i: (0, 0))
    ins = [x, g, dy] + ([res] if has_res else [])
    grid = (T // tm,)
    body, x_in, x_in_specs, x_out, x_out_specs, x_scr = _carry(ex, grid, len(ins), 2, body)
    return pl.pallas_call(
        body, out_shape=(jax.ShapeDtypeStruct((T, D), out_dtype), jax.ShapeDtypeStruct((1, D), F32), *x_out),
        grid=grid,
        in_specs=[row, vec, row] + ([row] if has_res else []) + x_in_specs,
        out_specs=(row, vec, *x_out_specs), scratch_shapes=x_scr,
        compiler_params=_params(("arbitrary",)), name=name)(*ins, *x_in)


def _swiglu_fwd(gu, *, name):
    T, F2 = gu.shape
    F = F2 // 2
    tm = _tile(T, 384)

    def body(g_ref, u_ref, o_ref, ot_ref):
        g = g_ref[...].astype(F32)
        act = g / (1.0 + jnp.exp(-g)) * u_ref[...].astype(F32)
        o_ref[...] = act.astype(o_ref.dtype)
        ot_ref[...] = act.T.astype(ot_ref.dtype)

    return pl.pallas_call(
        body, out_shape=(jax.ShapeDtypeStruct((T, F), BF16), jax.ShapeDtypeStruct((F, T), BF16)), grid=(T // tm,),
        in_specs=[pl.BlockSpec((tm, F), lambda i: (i, 0)), pl.BlockSpec((tm, F), lambda i: (i, 1))],
        out_specs=(pl.BlockSpec((tm, F), lambda i: (i, 0)), pl.BlockSpec((F, tm), lambda i: (0, i))),
        compiler_params=_params(("parallel",)), name=name)(gu, gu)


def _swiglu_bwd(gu, dact, *, name):
    T, F2 = gu.shape
    F = F2 // 2
    tm = _tile(T, 384)

    def body(g_ref, u_ref, d_ref, o_ref):
        g = g_ref[...].astype(F32)
        u = u_ref[...].astype(F32)
        d = d_ref[...].astype(F32)
        sg = 1.0 / (1.0 + jnp.exp(-g))
        o_ref[:, :F] = (d * u * (sg * (1.0 + g * (1.0 - sg)))).astype(o_ref.dtype)
        o_ref[:, F:] = (d * (g * sg)).astype(o_ref.dtype)

    return pl.pallas_call(
        body, out_shape=jax.ShapeDtypeStruct((T, F2), BF16), grid=(T // tm,),
        in_specs=[pl.BlockSpec((tm, F), lambda i: (i, 0)), pl.BlockSpec((tm, F), lambda i: (i, 1)),
                  pl.BlockSpec((tm, F), lambda i: (i, 0))],
        out_specs=pl.BlockSpec((tm, F2), lambda i: (i, 0)),
        compiler_params=_params(("parallel",)), name=name)(gu, gu, dact)


def _fox_gates_fwd(f_t, b, *, name):
    H, T = f_t.shape
    nb = T // BLOCK

    def body(f_ref, b_ref, cum_ref, col_ref):
        f = f_ref[...] + b_ref[...]
        ls = jnp.minimum(f, 0.0) - jnp.log(1.0 + jnp.exp(-jnp.abs(f)))
        t = lax.broadcasted_iota(jnp.int32, (H, T), 1)
        ls = jnp.where(t >= PAD_ROWS, ls, 0.0)
        upper = (lax.broadcasted_iota(jnp.int32, (BLOCK, BLOCK), 0)
                 <= lax.broadcasted_iota(jnp.int32, (BLOCK, BLOCK), 1)).astype(F32)
        carry = jnp.zeros((H, 1), F32)
        for blk in range(nb):
            seg = ls[:, blk * BLOCK:(blk + 1) * BLOCK]
            pre = jnp.dot(seg, upper, precision=HIGHEST, preferred_element_type=F32) + carry
            cum_ref[:, blk * BLOCK:(blk + 1) * BLOCK] = pre
            col_ref[blk * BLOCK:(blk + 1) * BLOCK, :] = jnp.concatenate(
                [pre, jnp.zeros((BLOCK - H, BLOCK), F32)], axis=0).T
            carry = pre[:, BLOCK - 1:BLOCK]

    vm = pl.BlockSpec(memory_space=pltpu.VMEM)
    return pl.pallas_call(
        body, out_shape=(jax.ShapeDtypeStruct((H, T), F32), jax.ShapeDtypeStruct((T, BLOCK), F32)),
        in_specs=[vm, vm], out_specs=(vm, vm),
        compiler_params=_params(), name=name)(f_t, b)


def _fox_gates_bwd(dcq, dck, f_t, b, *, name):
    H, T = f_t.shape
    nb = T // BLOCK

    def body(dq_ref, d_ref, f_ref, b_ref, df_ref, db_ref):
        lower = (lax.broadcasted_iota(jnp.int32, (BLOCK, BLOCK), 0)
                 >= lax.broadcasted_iota(jnp.int32, (BLOCK, BLOCK), 1)).astype(F32)
        carry = jnp.zeros((H, 1), F32)
        for blk in range(nb - 1, -1, -1):
            seg = dq_ref[:, blk * BLOCK:(blk + 1) * BLOCK] - d_ref[:, blk * BLOCK:(blk + 1) * BLOCK]
            suf = jnp.dot(seg, lower, precision=HIGHEST, preferred_element_type=F32) + carry
            df_ref[:, blk * BLOCK:(blk + 1) * BLOCK] = suf
            carry = suf[:, 0:1]
        f = f_ref[...] + b_ref[...]
        t = lax.broadcasted_iota(jnp.int32, (H, T), 1)
        df = jnp.where(t >= PAD_ROWS, df_ref[...] / (1.0 + jnp.exp(f)), 0.0)
        df_ref[...] = df
        db_ref[...] = jnp.sum(df, axis=1, keepdims=True)

    vm = pl.BlockSpec(memory_space=pltpu.VMEM)
    return pl.pallas_call(
        body, out_shape=(jax.ShapeDtypeStruct((H, T), F32), jax.ShapeDtypeStruct((H, 1), F32)),
        in_specs=[vm, vm, vm, vm], out_specs=(vm, vm),
        compiler_params=_params(), name=name)(dcq, dck, f_t, b)


LANE_KC = HEAD_DIM
LANE_QC = HEAD_DIM + 3
LANE_END = HEAD_DIM + 6


def _split3(c):
    hi = c.astype(BF16).astype(F32)
    r = c - hi
    mid = r.astype(BF16).astype(F32)
    lo = (r - mid).astype(BF16).astype(F32)
    return hi, mid, lo


def _lanes(lane, data, start, terms, rest):
    out = rest
    for i, t in enumerate(terms):
        out = jnp.where(lane == start + i, t, out)
    return jnp.where(lane < HEAD_DIM, data, out)


def _fox_prep(proj, cum_col, *, name):
    T = proj.shape[0]
    tm = FOX_TILE
    nt = T // tm
    H = FOX_HEADS
    lanes = 2 * HEAD_DIM
    qb, kb, vb = 768 // lanes, 1280 // lanes, 1792 // lanes

    def body(q_ref, k_ref, v_ref, c_ref, qa_ref, ka_ref, va_ref):
        p = pl.program_id(0)
        i = pl.program_id(1)
        lane = lax.broadcasted_iota(jnp.int32, (tm, lanes), 1)
        rows = i * tm + lax.broadcasted_iota(jnp.int32, (tm, 1), 0)
        q2 = q_ref[...].astype(F32)
        k2 = k_ref[...].astype(F32)
        v2 = v_ref[...].astype(F32)
        cum = c_ref[...]
        for e in range(2):
            c = jnp.sum(jnp.where(lane == 2 * p + e, cum, 0.0), axis=1, keepdims=True)
            ck = jnp.where(rows >= PAD_ROWS, c, -NEG)
            qe, ke, ve = (q2, k2, v2) if e == 0 else tuple(pltpu.roll(a, HEAD_DIM, 1) for a in (q2, k2, v2))
            one = jnp.where(lane < LANE_END, 1.0, 0.0)
            qa = _lanes(lane, qe * SCALE, LANE_QC, _split3(c), jnp.where(lane < LANE_QC, -1.0, 0.0))
            ka = _lanes(lane, ke, LANE_KC, _split3(ck), one)
            va = jnp.where(lane < HEAD_DIM, ve, jnp.where(lane < LANE_QC, 1.0, 0.0))
            qa_ref[e] = qa.astype(BF16)
            ka_ref[e] = ka.astype(BF16)
            va_ref[e] = va.astype(BF16)

    def col(b):
        return pl.BlockSpec((tm, lanes), lambda p, i, b=b: (i, b + p))

    out = pl.BlockSpec((2, tm, lanes), lambda p, i: (p, i, 0))
    shp = jax.ShapeDtypeStruct((H, T, lanes), BF16)
    return pl.pallas_call(
        body, out_shape=(shp, shp, shp), grid=(H // 2, nt),
        in_specs=[col(qb), col(kb), col(vb), pl.BlockSpec((tm, lanes), lambda p, i: (i, 0))],
        out_specs=(out, out, out),
        compiler_params=_params(("parallel", "parallel")), name=name)(proj, proj, proj, cum_col)


def _fox_fwd(q_aug, k_aug, v_aug, *, ex=None, name):
    H, T, lanes = q_aug.shape
    tq = FOX_TILE
    nq = T // tq
    G = FOX_GROUP

    def body(q_ref, k_ref, v_ref, o_ref, lse_ref, m_scr, acc_scr):
        i = pl.program_id(1)
        m_scr[...] = jnp.full(m_scr.shape, NEG, F32)
        acc_scr[...] = jnp.zeros(acc_scr.shape, F32)

        def step(kb, diag):
            off = pl.multiple_of(kb * tq, tq)
            s_t = [lax.dot_general(k_ref[g, pl.ds(off, tq), :], q_ref[g], NT, preferred_element_type=F32)
                   for g in range(G)]
            if diag:
                r = lax.broadcasted_iota(jnp.int32, (tq, tq), 0)
                c = lax.broadcasted_iota(jnp.int32, (tq, tq), 1)
                s_t = [jnp.where(c >= r, s, NEG) for s in s_t]
            m_prev = [m_scr[g] for g in range(G)]
            m_new = [jnp.maximum(m_prev[g], jnp.max(s_t[g], axis=0, keepdims=True)) for g in range(G)]
            p_t = [jnp.exp(s_t[g] - m_new[g]).astype(BF16) for g in range(G)]
            pv = [lax.dot_general(v_ref[g, pl.ds(off, tq), :], p_t[g], TN, preferred_element_type=F32)
                  for g in range(G)]
            for g in range(G):
                acc_scr[g] = jnp.exp(m_prev[g] - m_new[g]) * acc_scr[g] + pv[g]
                m_scr[g] = m_new[g]

        def loop_body(kb, carry):
            step(kb, False)
            return carry

        lax.fori_loop(0, i, loop_body, 0)
        step(i, True)
        for g in range(G):
            acc = acc_scr[g]
            lse_ref[g] = m_scr[g] + jnp.log(acc[HEAD_DIM:HEAD_DIM + 1, :])
            acc_t = acc.T
            o_ref[g] = (acc_t[:, :HEAD_DIM] / acc_t[:, HEAD_DIM:HEAD_DIM + 1]).astype(o_ref.dtype)

    blk = pl.BlockSpec((G, tq, lanes), lambda h, i: (h, i, 0))
    full = pl.BlockSpec((G, T, lanes), lambda h, i: (h, 0, 0))
    grid = (H // G, nq)
    body, x_in, x_in_specs, x_out, x_out_specs, x_scr = _carry(ex, grid, 3, 2, body)
    return pl.pallas_call(
        body,
        out_shape=(jax.ShapeDtypeStruct((H, T, HEAD_DIM), BF16), jax.ShapeDtypeStruct((H, nq, 1, tq), F32), *x_out),
        grid=grid,
        in_specs=[blk, full, full] + x_in_specs,
        out_specs=(pl.BlockSpec((G, tq, HEAD_DIM), lambda h, i: (h, i, 0)),
                   pl.BlockSpec((G, None, 1, tq), lambda h, i: (h, i, 0, 0)), *x_out_specs),
        scratch_shapes=[pltpu.VMEM((G, 1, tq), F32), pltpu.VMEM((G, lanes, tq), F32)] + x_scr,
        compiler_params=_params(("arbitrary", "arbitrary")), name=name)(q_aug, k_aug, v_aug, *x_in)


def _fox_prep_bwd(dmix, o, *, name):
    T = dmix.shape[0]
    H = o.shape[0]
    tm = FOX_TILE
    lanes = 2 * HEAD_DIM
    first = 512 // lanes

    def body(d_ref, o_ref, da_ref):
        lane = lax.broadcasted_iota(jnp.int32, (tm, lanes), 1)
        d2 = d_ref[...].astype(F32)
        for e in range(2):
            de = d2 if e == 0 else pltpu.roll(d2, HEAD_DIM, 1)
            d64 = d_ref[:, e * HEAD_DIM:(e + 1) * HEAD_DIM].astype(F32)
            delta = jnp.sum(d64 * o_ref[e].astype(F32), axis=1, keepdims=True)
            da_ref[e] = _lanes(lane, de, LANE_KC, _split3(-delta), jnp.zeros((), F32)).astype(BF16)

    return pl.pallas_call(
        body, out_shape=jax.ShapeDtypeStruct((H, T, lanes), BF16), grid=(H // 2, T // tm),
        in_specs=[pl.BlockSpec((tm, lanes), lambda p, i: (i, first + p)),
                  pl.BlockSpec((2, tm, HEAD_DIM), lambda p, i: (p, i, 0))],
        out_specs=pl.BlockSpec((2, tm, lanes), lambda p, i: (p, i, 0)),
        compiler_params=_params(("parallel", "parallel")), name=name)(dmix, o)


def _fox_bwd(q_aug, k_aug, v_aug, do_aug, lse_row, *, ex=None, name):
    H, T, lanes = q_aug.shape
    tq = FOX_TILE
    nq = T // tq
    G = FOX_GROUP

    def body(q_ref, k_ref, v_ref, do_ref, lse_ref, dq_ref, dk_ref, dv_ref, dck_ref, dk_acc, dv_acc):
        j = pl.program_id(1)

        @pl.when(j == 0)
        def _():
            dq_ref[...] = jnp.zeros(dq_ref.shape, F32)

        dk_acc[...] = jnp.zeros(dk_acc.shape, F32)
        dv_acc[...] = jnp.zeros(dv_acc.shape, F32)

        def step(qb, diag):
            off = pl.multiple_of(qb * tq, tq)
            heads = range(G)
            qa = [q_ref[g, pl.ds(off, tq), :] for g in heads]
            da = [do_ref[g, pl.ds(off, tq), :] for g in heads]
            s_t = [lax.dot_general(k_ref[g], qa[g], NT, preferred_element_type=F32) for g in heads]
            dp_t = [lax.dot_general(v_ref[g], da[g], NT, preferred_element_type=F32) for g in heads]
            p_t = [jnp.exp(s_t[g] - lse_ref[g, qb]) for g in heads]
            if diag:
                r = lax.broadcasted_iota(jnp.int32, (tq, tq), 0)
                c = lax.broadcasted_iota(jnp.int32, (tq, tq), 1)
                p_t = [jnp.where(c >= r, p, 0.0) for p in p_t]
            dsb = [(p_t[g] * dp_t[g]).astype(BF16) for g in heads]
            dv = [jnp.dot(p_t[g].astype(BF16), da[g], preferred_element_type=F32) for g in heads]
            dk = [jnp.dot(dsb[g], qa[g], preferred_element_type=F32) for g in heads]
            dq = [jnp.dot(dsb[g].T, k_ref[g], preferred_element_type=F32) for g in heads]
            for g in heads:
                dv_acc[g] += dv[g]
                dk_acc[g] += dk[g]
                dq_ref[g, pl.ds(off, tq), :] += dq[g]

        step(j, True)

        def loop_body(qb, carry):
            step(qb, False)
            return carry

        lax.fori_loop(j + 1, nq, loop_body, 0)
        dk = dk_acc[...]
        dk_ref[...] = dk.astype(dk_ref.dtype)
        dck_ref[...] = -dk[:, :, LANE_KC:LANE_KC + 1]
        dv_ref[...] = dv_acc[...].astype(dv_ref.dtype)

    blk = pl.BlockSpec((G, tq, lanes), lambda h, j: (h, j, 0))
    full = pl.BlockSpec((G, T, lanes), lambda h, j: (h, 0, 0))
    grid = (H // G, nq)
    body, x_in, x_in_specs, x_out, x_out_specs, x_scr = _carry(ex, grid, 5, 4, body)
    return pl.pallas_call(
        body,
        out_shape=(jax.ShapeDtypeStruct((H, T, lanes), F32), jax.ShapeDtypeStruct((H, T, lanes), BF16),
                   jax.ShapeDtypeStruct((H, T, lanes), BF16), jax.ShapeDtypeStruct((H, T, 1), F32), *x_out),
        grid=grid,
        in_specs=[full, blk, blk, full, pl.BlockSpec((G, nq, 1, tq), lambda h, j: (h, 0, 0, 0))] + x_in_specs,
        out_specs=(full, blk, blk,
                   pl.BlockSpec((G, tq, 1), lambda h, j: (h, j, 0)), *x_out_specs),
        scratch_shapes=[pltpu.VMEM((G, tq, lanes), F32), pltpu.VMEM((G, tq, lanes), F32)] + x_scr,
        compiler_params=_params(("arbitrary", "arbitrary")), name=name,
    )(q_aug, k_aug, v_aug, do_aug, lse_row, *x_in)


def _t5_bucket_np(d):
    n = np.maximum(d, 0).astype(np.int32)
    max_exact = N_BUCKETS // 2
    nf = np.maximum(n, 1).astype(np.float32)
    large = max_exact + (np.log(nf / max_exact) / math.log(MAX_DISTANCE / max_exact)
                         * (N_BUCKETS - max_exact)).astype(np.int32)
    large = np.minimum(large, N_BUCKETS - 1)
    return np.where(n < max_exact, n, large)


def _bucket_onehots():
    k = np.arange(BLOCK)[:, None]
    q = np.arange(BLOCK)[None, :]
    eye = np.eye(N_BUCKETS, dtype=np.float32)
    cur = eye[_t5_bucket_np(q - k).reshape(-1)]
    prev = eye[_t5_bucket_np(BLOCK + q - k).reshape(-1)]
    return cur, prev


SWA_K_COL = SWA_Q_HEADS * HEAD_DIM // (2 * HEAD_DIM)
SWA_V_COL = SWA_K_COL + 1


def _swa_terms(raw, bc, bp, far, sink, n):
    k = lax.broadcasted_iota(jnp.int32, (BLOCK, BLOCK), 0)
    q = lax.broadcasted_iota(jnp.int32, (BLOCK, BLOCK), 1)
    never = 2 * BLOCK
    s_c = raw[0] + bc
    s_p = raw[1] + bp
    s_m = raw[2] + jnp.where(n == 1, bp, far)
    s_c = jnp.where((k <= q) & (k >= jnp.where(n >= 1, 0, PAD_ROWS)), s_c, NEG)
    s_p = jnp.where(k > q + jnp.where(n >= 2, 0, never), s_p, NEG)
    s_m = jnp.where(k >= jnp.where(n >= 1, PAD_ROWS, never), s_m, NEG)
    m = jnp.maximum(jnp.maximum(jnp.max(s_c, axis=0, keepdims=True), jnp.max(s_p, axis=0, keepdims=True)),
                    jnp.maximum(jnp.max(s_m, axis=0, keepdims=True), sink))
    e = [jnp.exp(s_c - m), jnp.exp(s_p - m), jnp.exp(s_m - m)]
    e_s = jnp.exp(sink - m)
    l = (jnp.sum(e[0], axis=0, keepdims=True) + jnp.sum(e[1], axis=0, keepdims=True)
         + jnp.sum(e[2], axis=0, keepdims=True) + e_s)
    return e, e_s, l


def _swa_specs():
    G = SWA_GROUP
    width = G * HEAD_DIM

    def rows(which, col):
        if which == "cur":
            return pl.BlockSpec((BLOCK, BLOCK), lambda kv, n: (n, col))
        if which == "prev":
            return pl.BlockSpec((BLOCK, BLOCK), lambda kv, n: (jnp.maximum(n - 1, 0), col))
        return pl.BlockSpec((BLOCK, BLOCK), lambda kv, n: (0, col))

    qblk = pl.BlockSpec((BLOCK, width), lambda kv, n: (n, kv))
    keys = [rows(w, SWA_K_COL) for w in ("cur", "prev", "meta")]
    vals = [rows(w, SWA_V_COL) for w in ("cur", "prev", "meta")]
    bias = pl.BlockSpec((G, BLOCK, BLOCK), lambda kv, n: (kv, 0, 0))
    smem = pl.BlockSpec(memory_space=pltpu.SMEM)
    return qblk, keys, vals, bias, smem


def _swa_own_kv(tile_ref, kv):
    lane = lax.broadcasted_iota(jnp.int32, (BLOCK, 2 * HEAD_DIM), 1)
    t = tile_ref[...].astype(F32)
    return jnp.where(lane // HEAD_DIM == kv, t, pltpu.roll(t, HEAD_DIM, 1)).astype(BF16)


def _swa_fwd(proj, bc, bp, far, sinks, *, name):
    T = proj.shape[0]
    nb = T // BLOCK
    G = SWA_GROUP
    lanes = 2 * HEAD_DIM

    def body(q_ref, kc_ref, kp_ref, km_ref, vc_ref, vp_ref, vm_ref, bc_ref, bp_ref, far_ref, sink_ref, o_ref):
        kv = pl.program_id(0)
        n = pl.program_id(1)
        lane = lax.broadcasted_iota(jnp.int32, (BLOCK, lanes), 1)
        kk = [_swa_own_kv(r, kv) for r in (kc_ref, kp_ref, km_ref)]
        vv = [_swa_own_kv(r, kv) for r in (vc_ref, vp_ref, vm_ref)]
        heads, blocks = range(G), range(3)
        q2 = [q_ref[:, pair * lanes:(pair + 1) * lanes].astype(F32) * SCALE for pair in range(G // 2)]
        qm = [jnp.where(lane // HEAD_DIM == g % 2, q2[g // 2], 0.0).astype(BF16) for g in heads]
        raw = [[lax.dot_general(kk[b], qm[g], NT, preferred_element_type=F32) for b in blocks] for g in heads]
        terms = [_swa_terms(raw[g], bc_ref[g], bp_ref[g], far_ref[kv * G + g], sink_ref[kv * G + g], n) for g in heads]
        o_t = [sum(lax.dot_general(vv[b], terms[g][0][b].astype(BF16), TN, preferred_element_type=F32) for b in blocks)
               for g in heads]
        outs = [(o_t[g] / terms[g][2]).T for g in heads]
        for pair in range(G // 2):
            o_ref[:, pair * lanes:(pair + 1) * lanes] = jnp.where(
                lane < HEAD_DIM, outs[2 * pair], outs[2 * pair + 1]).astype(o_ref.dtype)

    qblk, keys, vals, bias, smem = _swa_specs()
    return pl.pallas_call(
        body, out_shape=jax.ShapeDtypeStruct((T, SWA_Q_HEADS * HEAD_DIM), BF16), grid=(SWA_KV_HEADS, nb),
        in_specs=[qblk] + keys + vals + [bias, bias, smem, smem],
        out_specs=qblk,
        compiler_params=_params(("parallel", "parallel")), name=name,
    )(proj, proj, proj, proj, proj, proj, proj, bc, bp, far, sinks)


def _swa_bwd(proj, dmix, bc, bp, far, sinks, *, ex=None, name):
    T = proj.shape[0]
    nb = T // BLOCK
    G = SWA_GROUP
    Hq = SWA_Q_HEADS
    lanes = 2 * HEAD_DIM

    def body(q_ref, kc_ref, kp_ref, km_ref, vc_ref, vp_ref, vm_ref, do_ref, bc_ref, bp_ref, far_ref, sink_ref,
             dq_ref, dk_ref, dv_ref, dbc_ref, dbp_ref, dbf_ref, dsk_ref):
        kv = pl.program_id(0)
        n = pl.program_id(1)

        @pl.when(n == 0)
        def _():
            for ref in (dk_ref, dv_ref, dbc_ref, dbp_ref, dbf_ref, dsk_ref):
                ref[...] = jnp.zeros(ref.shape, F32)

        lane = lax.broadcasted_iota(jnp.int32, (BLOCK, lanes), 1)
        kk = [_swa_own_kv(r, kv) for r in (kc_ref, kp_ref, km_ref)]
        vv = [_swa_own_kv(r, kv) for r in (vc_ref, vp_ref, vm_ref)]
        heads, blocks = range(G), range(3)
        q2 = [q_ref[:, pair * lanes:(pair + 1) * lanes].astype(F32) * SCALE for pair in range(G // 2)]
        d2 = [do_ref[:, pair * lanes:(pair + 1) * lanes] for pair in range(G // 2)]
        own = [lane // HEAD_DIM == g % 2 for g in heads]
        qm = [jnp.where(own[g], q2[g // 2], 0.0).astype(BF16) for g in heads]
        dom = [jnp.where(own[g], d2[g // 2], jnp.zeros_like(d2[0])) for g in heads]
        raw = [[lax.dot_general(kk[b], qm[g], NT, preferred_element_type=F32) for b in blocks] for g in heads]
        dp = [[lax.dot_general(vv[b], dom[g], NT, preferred_element_type=F32) for b in blocks] for g in heads]
        p, ds16 = [], []
        for g in heads:
            e, e_s, l = _swa_terms(raw[g], bc_ref[g], bp_ref[g], far_ref[kv * G + g], sink_ref[kv * G + g], n)
            inv = 1.0 / l
            pg = [e[b] * inv for b in blocks]
            delta = sum(jnp.sum(pg[b] * dp[g][b], axis=0, keepdims=True) for b in blocks)
            ds = [pg[b] * (dp[g][b] - delta) for b in blocks]
            dsk_ref[g] += -(e_s * inv) * delta
            dbc_ref[g] += ds[0]
            dbp_ref[g] += ds[1] + jnp.where(n == 1, ds[2], 0.0)
            dbf_ref[g] += jnp.where(n >= 2, ds[2], 0.0)
            p.append([x.astype(BF16) for x in pg])
            ds16.append([x.astype(BF16) for x in ds])
        dq_t = [sum(lax.dot_general(kk[b], ds16[g][b], TN, preferred_element_type=F32) for b in blocks) for g in heads]
        dk = [sum(jnp.dot(ds16[g][b], qm[g], preferred_element_type=F32) for g in heads) for b in blocks]
        dv = [sum(jnp.dot(p[g][b], dom[g], preferred_element_type=F32) for g in heads) for b in blocks]
        dqs = [dq_t[g].T * SCALE for g in heads]
        for pair in range(G // 2):
            dq_ref[:, pair * lanes:(pair + 1) * lanes] = jnp.where(
                lane < HEAD_DIM, dqs[2 * pair], dqs[2 * pair + 1]).astype(dq_ref.dtype)
        cur_off = pl.multiple_of(n * BLOCK, BLOCK)
        prev_off = pl.multiple_of(jnp.maximum(n - 1, 0) * BLOCK, BLOCK)
        for acc, ref in ((dk, dk_ref), (dv, dv_ref)):
            tot = [a + pltpu.roll(a, HEAD_DIM, 1) for a in acc]
            ref[pl.ds(cur_off, BLOCK), :] += tot[0]
            ref[pl.ds(prev_off, BLOCK), :] += tot[1]
            ref[0:BLOCK, :] += tot[2]

    qblk, keys, vals, bias, smem = _swa_specs()
    kvfull = pl.BlockSpec((None, T, lanes), lambda kv, n: (kv, 0, 0))
    dsk = pl.BlockSpec((G, 1, BLOCK), lambda kv, n: (kv, 0, 0))
    grid = (SWA_KV_HEADS, nb)
    body, x_in, x_in_specs, x_out, x_out_specs, x_scr = _carry(ex, grid, 12, 7, body)
    tile = jax.ShapeDtypeStruct((Hq, BLOCK, BLOCK), F32)
    return pl.pallas_call(
        body,
        out_shape=(jax.ShapeDtypeStruct((T, Hq * HEAD_DIM), BF16),
                   jax.ShapeDtypeStruct((SWA_KV_HEADS, T, lanes), F32),
                   jax.ShapeDtypeStruct((SWA_KV_HEADS, T, lanes), F32),
                   tile, tile, tile, jax.ShapeDtypeStruct((Hq, 1, BLOCK), F32), *x_out),
        grid=grid,
        in_specs=[qblk] + keys + vals + [qblk, bias, bias, smem, smem] + x_in_specs,
        out_specs=(qblk, kvfull, kvfull, bias, bias, bias, dsk, *x_out_specs),
        scratch_shapes=x_scr,
        compiler_params=_params(("arbitrary", "arbitrary")), name=name,
    )(proj, proj, proj, proj, proj, proj, proj, dmix, bc, bp, far, sinks, *x_in)


def _small_grads(dbc, dbp, dbf, dsk, oh_cur, oh_prev, *, name):
    Hq = dbc.shape[0]

    def body(dbc_ref, dbp_ref, dbf_ref, dsk_ref, oc_ref, op_ref, tab_ref, sink_ref):
        tab = (jnp.dot(dbc_ref[...], oc_ref[...], precision=HIGHEST, preferred_element_type=F32)
               + jnp.dot(dbp_ref[...], op_ref[...], precision=HIGHEST, preferred_element_type=F32))
        far = jnp.sum(dbf_ref[...], axis=1, keepdims=True)
        last = lax.broadcasted_iota(jnp.int32, (Hq, N_BUCKETS), 1) == N_BUCKETS - 1
        tab_ref[...] = tab + jnp.where(last, far, 0.0)
        sink_ref[...] = jnp.sum(dsk_ref[...], axis=1, keepdims=True)

    vm = pl.BlockSpec(memory_space=pltpu.VMEM)
    return pl.pallas_call(
        body, out_shape=(jax.ShapeDtypeStruct((Hq, N_BUCKETS), F32), jax.ShapeDtypeStruct((Hq, 1), F32)),
        in_specs=[vm] * 6, out_specs=(vm, vm), compiler_params=_params(), name=name,
    )(dbc.reshape(Hq, -1), dbp.reshape(Hq, -1), dbf.reshape(Hq, -1), dsk.reshape(Hq, -1), oh_cur, oh_prev)


def _coords():
    return lax.axis_index("x"), lax.axis_index("y"), lax.axis_index("c")


class _Exchange:
    def __init__(self, inputs, out_shapes, scratch, start, finish):
        self.inputs, self.out_shapes, self.scratch, self.start, self.finish = inputs, out_shapes, scratch, start, finish


def _carry(ex, grid, n_in, n_out, body):
    if ex is None:
        return body, [], [], [], [], []
    ni, no = len(ex.inputs), len(ex.out_shapes)

    def at_step(which):
        cond = None
        for axis, n in enumerate(grid):
            c = pl.program_id(axis) == (0 if which == "first" else n - 1)
            cond = c if cond is None else cond & c
        return cond

    def wrapped(*refs):
        refs = list(refs)
        n_own_scr = len(refs) - (n_in + ni + n_out + no) - len(ex.scratch)
        own_in, side_in = refs[:n_in], refs[n_in:n_in + ni]
        own_out = refs[n_in + ni:n_in + ni + n_out]
        side_out = refs[n_in + ni + n_out:n_in + ni + n_out + no]
        rest = refs[n_in + ni + n_out + no:]
        own_scr, sems = rest[:n_own_scr], rest[n_own_scr:]

        @pl.when(at_step("first"))
        def _():
            ex.start(side_in, side_out, sems)

        body(*own_in, *own_out, *own_scr)

        @pl.when(at_step("last"))
        def _():
            ex.finish(side_in, side_out, sems)

    hbm = pl.BlockSpec(memory_space=pl.ANY)
    return wrapped, list(ex.inputs), [hbm] * ni, list(ex.out_shapes), [hbm] * no, list(ex.scratch)


def _run_exchange(ex, *, name):
    ni, no = len(ex.inputs), len(ex.out_shapes)

    def body(*refs):
        ins, outs, sems = refs[:ni], refs[ni:ni + no], refs[ni + no:]
        ex.start(ins, outs, sems)
        ex.finish(ins, outs, sems)

    hbm = pl.BlockSpec(memory_space=pl.ANY)
    return pl.pallas_call(
        body, out_shape=tuple(ex.out_shapes), in_specs=[hbm] * ni, out_specs=tuple([hbm] * no),
        scratch_shapes=ex.scratch, compiler_params=_params(), name=name)(*ex.inputs)


def _gather_exchange(shards):
    nt = len(shards)

    def copies(ins, outs, sems):
        send_sems, recv_sems, local_sems = sems
        x, y, c = _coords()
        me, sibling = (x, y, c), (x, y, 1 - c)
        chips = [(1 - x, y), (x, 1 - y), (1 - x, 1 - y)]

        def slot(t, dev):
            return outs[t].at[4 * dev[0] + 2 * dev[1] + dev[2]]

        def copy(t, k, block, to, src=None):
            dst = slot(t, block)
            return pltpu.make_async_remote_copy(
                src_ref=dst if src is None else src, dst_ref=dst,
                send_sem=send_sems.at[t, k], recv_sem=recv_sems.at[t, k], device_id=to, device_id_type=MESH)

        mine = [pltpu.make_async_copy(ins[t], slot(t, me), local_sems.at[t]) for t in range(nt)]
        first = []
        for t in range(nt):
            first.append(copy(t, 0, me, sibling, src=ins[t]))
            first += [copy(t, 1 + j, me, (*chip, c), src=ins[t]) for j, chip in enumerate(chips)]
        return copy, mine, first, me, sibling, chips, c

    def start(ins, outs, sems):
        _, mine, first, *_ = copies(ins, outs, sems)
        for cp in mine + first:
            cp.start()

    def finish(ins, outs, sems):
        copy, mine, first, me, sibling, chips, c = copies(ins, outs, sems)
        passed = []
        for j, chip in enumerate(chips):
            for t in range(nt):
                copy(t, 1 + j, (*chip, c), me).wait_recv()
                cp = copy(t, 4 + j, (*chip, c), sibling)
                cp.start()
                passed.append(cp)
        for t in range(nt):
            copy(t, 0, sibling, me).wait_recv()
            for j, chip in enumerate(chips):
                copy(t, 4 + j, (*chip, 1 - c), me).wait_recv()
        for cp in first + passed:
            cp.wait_send()
        for cp in mine:
            cp.wait()

    return _Exchange(
        list(shards), [jax.ShapeDtypeStruct((N_DEV,) + s.shape, s.dtype) for s in shards],
        [pltpu.SemaphoreType.DMA((nt, 7)), pltpu.SemaphoreType.DMA((nt, 7)), pltpu.SemaphoreType.DMA((nt,))],
        start, finish)


def _swap_exchange(arrays, n_slices, copies):
    nt = len(arrays)

    def start(ins, outs, sems):
        for cp in copies(ins, outs, sems):
            cp.start()

    def finish(ins, outs, sems):
        sends = copies(ins, outs, sems)
        for cp in sends:
            cp.wait_recv()
        for cp in sends:
            cp.wait_send()

    return _Exchange(
        list(arrays), [jax.ShapeDtypeStruct((n_slices,) + a.shape[1:], a.dtype) for a in arrays],
        [pltpu.SemaphoreType.DMA((nt, n_slices)), pltpu.SemaphoreType.DMA((nt, n_slices))], start, finish)


def _cores_exchange(gs):
    def copies(ins, outs, sems):
        send_sems, recv_sems = sems
        x, y, c = _coords()
        return [pltpu.make_async_remote_copy(
            src_ref=ins[t].at[2 * j + (1 - c)], dst_ref=outs[t].at[j],
            send_sem=send_sems.at[t, j], recv_sem=recv_sems.at[t, j], device_id=(x, y, 1 - c), device_id_type=MESH)
            for t in range(len(gs)) for j in range(4)]

    return _swap_exchange(gs, 4, copies)


def _chips_exchange(ps):
    def copies(ins, outs, sems):
        send_sems, recv_sems = sems
        x, y, c = _coords()
        peers = [(1 - x, y), (x, 1 - y), (1 - x, 1 - y)]
        return [pltpu.make_async_remote_copy(
            src_ref=ins[t].at[2 * px + py], dst_ref=outs[t].at[k],
            send_sem=send_sems.at[t, k], recv_sem=recv_sems.at[t, k], device_id=(px, py, c), device_id_type=MESH)
            for t in range(len(ps)) for k, (px, py) in enumerate(peers)]

    return _swap_exchange(ps, 3, copies)


def _add_cores(g, r, core, *, name):
    _, A, B = g.shape
    ta = _tile(A, 512, 16)

    def body(core_ref, a_ref, b_ref, o_ref, o16_ref):
        s = a_ref[...] + b_ref[...]
        o_ref[...] = s
        o16_ref[...] = s.astype(BF16)

    blk = (None, ta, B)
    out = pl.BlockSpec(blk, lambda j, i, core_ref: (j, i, 0))
    return pl.pallas_call(
        body, out_shape=(jax.ShapeDtypeStruct((4, A, B), F32), jax.ShapeDtypeStruct((4, A, B), BF16)),
        grid_spec=pltpu.PrefetchScalarGridSpec(
            num_scalar_prefetch=1, grid=(4, A // ta),
            in_specs=[pl.BlockSpec(blk, lambda j, i, core_ref: (2 * j + core_ref[0], i, 0)),
                      pl.BlockSpec(blk, lambda j, i, core_ref: (j, i, 0))],
            out_specs=(out, out)),
        compiler_params=_params(("parallel", "parallel")), name=name)(core, g, r)


def _adamw_math(w, g, m, v):
    m = ADAM_B1 * m + (1.0 - ADAM_B1) * g
    v = ADAM_B2 * v + (1.0 - ADAM_B2) * (g * g)
    m_hat = m / (1.0 - ADAM_B1 ** ADAM_STEP)
    v_hat = v / (1.0 - ADAM_B2 ** ADAM_STEP)
    delta = -ADAM_LR * (m_hat / (jnp.sqrt(v_hat) + ADAM_EPS) + ADAM_WD * w)
    return delta, m, v


def _sum_adamw(p, r, chip, w, m, v, *, segs, ta, name):
    Aw, Bw = w.shape
    Bg = p.shape[2]
    assert Aw % ta == 0

    def body(chip_ref, p_ref, r0, r1, r2, w_ref, m_ref, v_ref, g_out, d_out, m_out, v_out):
        for gc, wc, n in segs:
            g = ((p_ref[:, gc:gc + n] + r0[:, gc:gc + n].astype(F32)) + r1[:, gc:gc + n].astype(F32)
                 ) + r2[:, gc:gc + n].astype(F32)
            delta, m_new, v_new = _adamw_math(w_ref[:, wc:wc + n], g, m_ref[:, wc:wc + n], v_ref[:, wc:wc + n])
            g_out[:, wc:wc + n] = g
            d_out[:, wc:wc + n] = delta
            m_out[:, wc:wc + n] = m_new
            v_out[:, wc:wc + n] = v_new

    gblk = (None, ta, Bg)
    row = pl.BlockSpec((ta, Bw), lambda i, chip_ref: (i, 0))
    rspecs = [pl.BlockSpec(gblk, (lambda i, chip_ref, k=k: (k, i, 0))) for k in range(3)]
    shp = jax.ShapeDtypeStruct((Aw, Bw), F32)
    return pl.pallas_call(
        body, out_shape=(shp, shp, shp, shp),
        grid_spec=pltpu.PrefetchScalarGridSpec(
            num_scalar_prefetch=1, grid=(Aw // ta,),
            in_specs=[pl.BlockSpec(gblk, lambda i, chip_ref: (chip_ref[0], i, 0))] + rspecs + [row, row, row],
            out_specs=(row, row, row, row)),
        compiler_params=_params(("parallel",)), name=name)(chip, p, r, r, r, w, m, v)


def _adamw(w, g, m, v, *, name):
    def body(w_ref, g_ref, m_ref, v_ref, d_out, m_out, v_out):
        delta, m_new, v_new = _adamw_math(w_ref[...], g_ref[...], m_ref[...], v_ref[...])
        d_out[...] = delta
        m_out[...] = m_new
        v_out[...] = v_new

    vm = pl.BlockSpec(memory_space=pltpu.VMEM)
    shp = jax.ShapeDtypeStruct(w.shape, F32)
    return pl.pallas_call(body, out_shape=(shp, shp, shp), in_specs=[vm] * 4, out_specs=(vm, vm, vm),
                          compiler_params=_params(), name=name)(w, g, m, v)


def _small_allreduce_adamw(s, w, m, v, *, name):
    R, W = s.shape

    def body(s_ref, w_ref, m_ref, v_ref, g_out, d_out, m_out, v_out, gath, send_sems, recv_sems):
        x, y, c = _coords()
        mine = 4 * x + 2 * y + c
        gath[mine] = s_ref[...]
        peers = [((1 - x) if k & 4 else x, (1 - y) if k & 2 else y, (1 - c) if k & 1 else c) for k in range(1, N_DEV)]
        sends = []
        for k in range(1, N_DEV):
            peer = peers[k - 1]
            sends.append(pltpu.make_async_remote_copy(
                src_ref=s_ref, dst_ref=gath.at[mine], send_sem=send_sems.at[k - 1], recv_sem=recv_sems.at[k - 1],
                device_id=peer, device_id_type=MESH))
        for cp in sends:
            cp.start()
        for k in range(1, N_DEV):
            peer = peers[k - 1]
            pltpu.make_async_remote_copy(
                src_ref=s_ref, dst_ref=gath.at[4 * peer[0] + 2 * peer[1] + peer[2]],
                send_sem=send_sems.at[k - 1], recv_sem=recv_sems.at[k - 1],
                device_id=peer, device_id_type=MESH).wait_recv()
        for cp in sends:
            cp.wait_send()
        g = gath[0]
        for d in range(1, N_DEV):
            g = g + gath[d]
        delta, m_new, v_new = _adamw_math(w_ref[...], g, m_ref[...], v_ref[...])
        g_out[...] = g
        d_out[...] = delta
        m_out[...] = m_new
        v_out[...] = v_new

    vm = pl.BlockSpec(memory_space=pltpu.VMEM)
    shp = jax.ShapeDtypeStruct((R, W), F32)
    return pl.pallas_call(
        body, out_shape=(shp, shp, shp, shp), in_specs=[vm] * 4, out_specs=(vm, vm, vm, vm),
        scratch_shapes=[pltpu.VMEM((N_DEV, R, W), F32), pltpu.SemaphoreType.DMA((N_DEV - 1,)),
                        pltpu.SemaphoreType.DMA((N_DEV - 1,))],
        compiler_params=_params(), name=name)(s, w, m, v)


def _pack_small(rel_bias, g1, g2, g3, g4, b_forget, sinks, extra=None, meta=None):
    misc = jnp.concatenate([rel_bias.reshape(-1), b_forget.reshape(-1), sinks.reshape(-1)])
    misc = jnp.concatenate([misc, jnp.zeros((D_MODEL - misc.shape[0],), F32)])[None]
    last = jnp.zeros((1, D_MODEL), F32) if extra is None else extra
    meta = jnp.zeros((N_META, D_MODEL), F32) if meta is None else meta
    return jnp.concatenate([g1, g2, g3, g4, misc, last, jnp.zeros((2, D_MODEL), F32), meta], axis=0)


def _unpack_small(p):
    nrb = N_BUCKETS * SWA_Q_HEADS
    misc = p[4]
    return dict(rel_bias=misc[:nrb].reshape(N_BUCKETS, SWA_Q_HEADS), ln_pre_mix=p[0:1], ln_post_mix=p[1:2],
                ln_pre_ffn=p[2:3], ln_post_ffn=p[3:4], b_forget=misc[nrb:nrb + 8].reshape(1, 8),
                sinks=misc[nrb + 8:nrb + 16].reshape(1, 8))


def _shard_order(pieces, shard, pad):
    T, dtype = pieces[0].shape[0], pieces[0].dtype
    total = sum(p.shape[1] for p in pieces)
    assert total % shard == 0
    out, zeros = [], jnp.zeros((T, pad), dtype)
    for s in range(total // shard):
        lo, hi, start = s * shard, (s + 1) * shard, 0
        for p in pieces:
            end = start + p.shape[1]
            if max(lo, start) < min(hi, end):
                out.append(p[:, max(lo, start) - start:min(hi, end) - start])
            start = end
        out.append(zeros)
    return jnp.concatenate(out, axis=1)


def _unheads(a):
    return a.transpose(1, 0, 2).reshape(a.shape[1], -1)


def kernel(x, meta_tokens, rel_bias, ln_pre_mix, ln_post_mix, ln_pre_ffn, ln_post_ffn, w_in, b_forget, sinks, w_out, w_gate_up, w_down, loss_target, m_meta_tokens, m_rel_bias, m_ln_pre_mix, m_ln_post_mix, m_ln_pre_ffn, m_ln_post_ffn, m_w_in, m_b_forget, m_sinks, m_w_out, m_w_gate_up, m_w_down, v_meta_tokens, v_rel_bias, v_ln_pre_mix, v_ln_post_mix, v_ln_pre_ffn, v_ln_post_ffn, v_w_in, v_b_forget, v_sinks, v_w_out, v_w_gate_up, v_w_down):
    seq = x.shape[1]
    T = BLOCK + seq
    assert T % FOX_TILE == 0
    nq = T // FOX_TILE
    tm = _tile(T, 1056)
    cin = w_in.shape[2]
    hid = w_down.shape[1]
    assert w_gate_up.shape[2] == 2 * hid and cin <= W_IN_PAD and hid <= HID_PAD

    x_i, y_i, c_i = _coords()
    core = jnp.reshape(c_i, (1,)).astype(jnp.int32)
    chip = jnp.reshape(2 * x_i + y_i, (1,)).astype(jnp.int32)
    w_in_s = jnp.pad(w_in[0].astype(BF16), ((0, 0), (0, W_IN_PAD - cin)))
    w_gu_s = jnp.pad(w_gate_up[0].astype(BF16).reshape(D_MODEL, 2, hid), ((0, 0), (0, 0), (0, HID_PAD - hid)))
    w_gu_s = w_gu_s.reshape(D_MODEL, 2 * HID_PAD)
    w_down_s = jnp.pad(w_down[0].astype(BF16), ((0, HID_PAD - hid), (0, 0)))
    g_in, g_meta = _run_exchange(_gather_exchange([w_in_s, meta_tokens]), name="ag_w_in")
    gather_rest = _gather_exchange([w_out[0].astype(BF16), w_gu_s, w_down_s])
    w_in_full = g_in[:, :, :cin].transpose(1, 0, 2).reshape(D_MODEL, N_DEV * cin)
    w_qkv = w_in_full[:, :D_QKV]
    w_f = jnp.pad(w_in_full[:, D_QKV:], ((0, 0), (0, BLOCK - FOX_HEADS)))
    meta_full = g_meta.transpose(1, 0, 2).reshape(N_META, D_MODEL)

    h0 = jnp.concatenate([jnp.zeros((PAD_ROWS, D_MODEL), F32), meta_full, x[0]], axis=0)
    target = jnp.concatenate([jnp.zeros((BLOCK, D_MODEL), F32), loss_target[0]], axis=0)
    hn1, hn1_t = _rms_fwd(h0, ln_pre_mix, name="rms_pre_mix")
    proj = _matmul(hn1, w_qkv, out_dtype=BF16, tm=tm, tn=768, name="mm_in_proj")
    proj_f = _matmul(hn1, w_f, out_dtype=F32, tm=tm, tn=BLOCK, name="mm_in_proj_f")

    f_t = proj_f[:, :FOX_HEADS].T
    bf_col = b_forget.reshape(FOX_HEADS, 1)

    oh_cur, oh_prev = _bucket_onehots()
    bias_c = jnp.einsum("pb,bh->hp", jnp.asarray(oh_cur), rel_bias, precision=HIGHEST).reshape(8, BLOCK, BLOCK)
    bias_p = jnp.einsum("pb,bh->hp", jnp.asarray(oh_prev), rel_bias, precision=HIGHEST).reshape(8, BLOCK, BLOCK)
    far = rel_bias[N_BUCKETS - 1]
    sink_v = sinks[0]
    o_a = _swa_fwd(proj, bias_c, bias_p, far, sink_v, name="swa_fwd")

    _, cum_col = _fox_gates_fwd(f_t, bf_col, name="fox_gates_fwd")
    q_b, k_b, v_b = _fox_prep(proj, cum_col, name="fox_prep")
    o_b, lse_row, g_out, g_gu, g_down = _fox_fwd(q_b, k_b, v_b, ex=gather_rest, name="fox_fwd")
    w_out_full = g_out.reshape(D_MODEL, D_MODEL)
    w_down_full = g_down.reshape(N_DEV * HID_PAD, D_MODEL)

    mix = jnp.concatenate([o_a, _unheads(o_b)], axis=1)
    a1 = _matmul(mix, w_out_full, out_dtype=F32, tm=tm, tn=512, name="mm_out_proj")
    h1 = _post_res(a1, ln_post_mix, h0, name="post_mix")
    hn2, hn2_t = _rms_fwd(h1, ln_pre_ffn, name="rms_pre_ffn")
    gu = _matmul(hn2, g_gu, b_shards=True, out_dtype=BF16, tm=tm, name="mm_gate_up")
    act, act_t = _swiglu_fwd(gu, name="swiglu_fwd")
    ff = _matmul(act, w_down_full, out_dtype=F32, tm=tm, tn=512, name="mm_down")
    dh2, loss_acc = _loss_head(ff, ln_post_ffn, h1, target, name="loss_head")

    dff, dg_post_ffn = _rms_bwd(ff, ln_post_ffn, dh2, None, out_dtype=BF16, name="rms_bwd_post_ffn")
    dact = _matmul(dff, w_down_full, nt=True, out_dtype=BF16, tm=tm, tn=1536, name="mm_d_act")
    d_w_down = _matmul(act_t, dff, out_dtype=F32, tm=768, tn=512, name="mm_dw_down")
    dgu = _swiglu_bwd(gu, dact, name="swiglu_bwd")
    dhn2 = _matmul(dgu, g_gu, nt=True, b_shards=True, out_dtype=F32, tm=_tile(T, 528), tn=512, name="mm_d_hn2")
    d_w_gu = _matmul(hn2_t, dgu, out_shards=True, out_dtype=F32, tm=512, tn=2 * HID_PAD, name="mm_dw_gate_up")
    dh1, dg_pre_ffn = _rms_bwd(h1, ln_pre_ffn, dhn2, dh2, out_dtype=F32, name="rms_bwd_pre_ffn")
    da1, dg_post_mix = _rms_bwd(a1, ln_post_mix, dh1, None, out_dtype=BF16, name="rms_bwd_post_mix")
    dmix = _matmul(da1, w_out_full, nt=True, out_dtype=BF16, tm=tm, tn=512, name="mm_d_mix")
    d_w_out = _matmul(mix.T, da1, out_dtype=F32, tm=512, tn=512, name="mm_dw_out")

    ffn_grads = [d_w_out.reshape(N_DEV, -1, D_MODEL), d_w_gu, d_w_down.reshape(N_DEV, HID_PAD, D_MODEL)]
    dq_a, dk_a, dv_a, dbc, dbp, dbf, dsk, *ffn_sibling = _swa_bwd(
        proj, dmix, bias_c, bias_p, far, sink_v, ex=_cores_exchange(ffn_grads), name="swa_bwd")
    low = (jnp.arange(2 * HEAD_DIM) < HEAD_DIM)[None, :]
    dk_a = jnp.where(low, dk_a[0], dk_a[1]).astype(BF16)
    dv_a = jnp.where(low, dv_a[0], dv_a[1]).astype(BF16)
    d_tab, d_sink = _small_grads(dbc, dbp, dbf, dsk, jnp.asarray(oh_cur), jnp.asarray(oh_prev), name="small_grads")
    ffn_sums = [_add_cores(g, r, core, name="rs_add_" + t)
                for g, r, t in zip(ffn_grads, ffn_sibling, ["w_out", "w_gate_up", "w_down"])]

    do_b = _fox_prep_bwd(dmix, o_b, name="fox_prep_bwd")
    dq_t, dk_b, dv_b, dck, *ffn_chips = _fox_bwd(
        q_b, k_b, v_b, do_b, lse_row, ex=_chips_exchange([s[1] for s in ffn_sums]), name="fox_bwd")
    dcq = dq_t[:, :, LANE_QC]
    df_t, d_bf = _fox_gates_bwd(dcq, dck.reshape(FOX_HEADS, T), f_t, bf_col, name="fox_gates_bwd")
    dq_b = (dq_t[:, :, :HEAD_DIM].transpose(1, 0, 2).reshape(T, FOX_W) * SCALE).astype(BF16)
    dk_b = dk_b[:, :, :HEAD_DIM].transpose(1, 0, 2).reshape(T, FOX_W)
    dv_b = dv_b[:, :, :HEAD_DIM].transpose(1, 0, 2).reshape(T, FOX_W)

    dproj_s = _shard_order([dq_a, dk_a, dv_a, dq_b, dk_b, dv_b, df_t.T.astype(BF16)], cin, W_IN_PAD - cin)
    d_w_in = _matmul(hn1_t, dproj_s, out_shards=True, out_dtype=F32, tm=512, tn=W_IN_PAD, name="mm_dw_in")
    dhn1, in_sibling = _matmul(dproj_s, g_in, nt=True, b_shards=True, out_dtype=F32, tm=tm, tn=512,
                               ex=_cores_exchange([d_w_in]), name="mm_d_hn1")
    in_sum = _add_cores(d_w_in, in_sibling, core, name="rs_add_w_in")
    dh0, dg_pre_mix, in_chips = _rms_bwd(h0, ln_pre_mix, dhn1, dh1, out_dtype=F32,
                                         ex=_chips_exchange([in_sum[1]]), name="rms_bwd_pre_mix")
    grad_x = dh0[BLOCK:][None]
    d_meta = dh0[PAD_ROWS:BLOCK]

    tags = ["w_in", "w_out", "w_gate_up", "w_down"]
    chip_sum = [in_sum[0]] + [s[0] for s in ffn_sums]
    from_chips = [in_chips] + list(ffn_chips)
    shard_w = [(w_in, m_w_in, v_w_in), (w_out, m_w_out, v_w_out), (w_gate_up, m_w_gate_up, v_w_gate_up),
               (w_down, m_w_down, v_w_down)]
    segs = [[(0, 0, cin)], [(0, 0, D_MODEL)], [(0, 0, hid), (HID_PAD, hid, hid)], [(0, 0, D_MODEL)]]
    tas = [256, BLOCK, 256, hid]
    big = [{}, {}, {}, {}]
    for i, t in enumerate(tags):
        w_t, m_t, v_t = shard_w[i]
        res = _sum_adamw(chip_sum[i], from_chips[i], chip, w_t[0], m_t[0], v_t[0], segs=segs[i], ta=tas[i],
                         name="rs_adamw_" + t)
        for kind in range(4):
            big[kind][t] = res[kind][None]

    loss_row = jnp.pad(loss_acc[0:1, 0:1] * (0.5 / D_MODEL), ((0, 0), (0, D_MODEL - 1)))
    s_small = _pack_small(d_tab.T, dg_pre_mix, dg_post_mix, dg_pre_ffn, dg_post_ffn, d_bf, d_sink,
                          extra=loss_row, meta=d_meta)
    w_s = _pack_small(rel_bias, ln_pre_mix, ln_post_mix, ln_pre_ffn, ln_post_ffn, b_forget, sinks)
    m_s = _pack_small(m_rel_bias, m_ln_pre_mix, m_ln_post_mix, m_ln_pre_ffn, m_ln_post_ffn, m_b_forget, m_sinks)
    v_s = _pack_small(v_rel_bias, v_ln_pre_mix, v_ln_post_mix, v_ln_pre_ffn, v_ln_post_ffn, v_b_forget, v_sinks)
    small = _small_allreduce_adamw(s_small, w_s, m_s, v_s, name="small_allreduce_adamw")
    loss = small[0][5, 0]
    mcols = meta_tokens.shape[1]
    g_meta_mine = lax.dynamic_slice(small[0][8:8 + N_META], (0, (4 * x_i + 2 * y_i + c_i) * mcols), (N_META, mcols))
    big[0]["meta_tokens"] = g_meta_mine
    for kind, arr in enumerate(_adamw(meta_tokens, g_meta_mine, m_meta_tokens, v_meta_tokens, name="adamw_meta")):
        big[kind + 1]["meta_tokens"] = arr
    small = [_unpack_small(p) for p in small]

    names = ["meta_tokens", "rel_bias", "ln_pre_mix", "ln_post_mix", "ln_pre_ffn", "ln_post_ffn", "w_in",
             "b_forget", "sinks", "w_out", "w_gate_up", "w_down"]
    outs = [loss, grad_x]
    for kind in range(4):
        for nme in names:
            outs.append(big[kind][nme] if nme in big[kind] else small[kind][nme])
    return tuple(outs)
```

```python
import math

import numpy as np
import jax
import jax.numpy as jnp
from jax import lax
from jax.experimental import pallas as pl
from jax.experimental.pallas import tpu as pltpu

F32 = jnp.float32
BF16 = jnp.bfloat16
HIGHEST = lax.Precision.HIGHEST
MESH = pl.DeviceIdType.MESH

N_DEV = 8
D_MODEL = 1024
N_META = 16
HEAD_DIM = 64
SWA_Q_HEADS = 8
SWA_KV_HEADS = 2
SWA_GROUP = 4
FOX_HEADS = 8
FOX_W = FOX_HEADS * HEAD_DIM
BLOCK = 128
PAD_ROWS = BLOCK - N_META
N_BUCKETS = 32
MAX_DISTANCE = 128
D_FF = 2816
D_QKV = 2304
D_PROJ = D_QKV + FOX_HEADS
D_PROJ_PAD = 2560
EPS = 1e-6
NEG = -1e30
SCALE = HEAD_DIM ** -0.5
ADAM_LR, ADAM_B1, ADAM_B2, ADAM_EPS, ADAM_WD, ADAM_STEP = 0.001, 0.9, 0.999, 1e-08, 0.01, 10
VMEM_LIMIT = 48 * 1024 * 1024
FOX_TILE = 384
FOX_GROUP = 4
W_IN_PAD = 384
HID_PAD = 384

NT = (((1,), (1,)), ((), ()))
NN = (((1,), (0,)), ((), ()))
TN = (((0,), (0,)), ((), ()))


def _params(sem=None, **kw):
    if sem is not None:
        kw["dimension_semantics"] = sem
    return pltpu.CompilerParams(vmem_limit_bytes=VMEM_LIMIT, **kw)


def _tile(n, target, mult=16):
    best = None
    for t in range(mult, min(n, target) + 1, mult):
        if n % t == 0:
            best = t
    assert best is not None, (n, target)
    return best


def _matmul(a, b, *, nt=False, b_shards=False, out_shards=False, out_dtype, tm, tn=None, tk=None, ex=None, name):
    M, K = a.shape
    k_shards = b.shape[0] if (b_shards and nt) else 0
    if k_shards:
        N, ks = b.shape[1], b.shape[2]
        assert tk is None and K == k_shards * ks
    elif b_shards:
        N, tn = b.shape[0] * b.shape[2], b.shape[2]
    else:
        N = b.shape[0] if nt else b.shape[1]
    tk = K if tk is None else tk
    assert M % tm == 0 and N % tn == 0 and K % tk == 0, (name, a.shape, b.shape, tm, tn, tk)
    nk = K // tk
    dn = NT if nt else NN

    def body(a_ref, b_ref, o_ref, *scr):
        if k_shards:
            part = sum(lax.dot_general(a_ref[:, s * ks:(s + 1) * ks], b_ref[s], NT, preferred_element_type=F32)
                       for s in range(k_shards))
        else:
            part = lax.dot_general(a_ref[...], b_ref[...], dn, preferred_element_type=F32)
        if nk == 1:
            o_ref[...] = part.astype(o_ref.dtype)
        else:
            acc = scr[0]
            k = pl.program_id(2)

            @pl.when(k == 0)
            def _():
                acc[...] = part

            @pl.when(k > 0)
            def _():
                acc[...] += part

            @pl.when(k == nk - 1)
            def _():
                o_ref[...] = acc[...].astype(o_ref.dtype)

    if k_shards:
        b_spec = pl.BlockSpec((k_shards, tn, ks), lambda i, j, k: (0, j, 0))
    elif b_shards:
        b_spec = pl.BlockSpec((None, tk, tn), lambda i, j, k: (j, k, 0))
    elif nt:
        b_spec = pl.BlockSpec((tn, tk), lambda i, j, k: (j, k))
    else:
        b_spec = pl.BlockSpec((tk, tn), lambda i, j, k: (k, j))
    if out_shards:
        out_shape = jax.ShapeDtypeStruct((N // tn, M, tn), out_dtype)
        out_spec = pl.BlockSpec((None, tm, tn), lambda i, j, k: (j, i, 0))
    else:
        out_shape = jax.ShapeDtypeStruct((M, N), out_dtype)
        out_spec = pl.BlockSpec((tm, tn), lambda i, j, k: (i, j))
    grid = (M // tm, N // tn, nk)
    body, x_in, x_in_specs, x_out, x_out_specs, x_scr = _carry(ex, grid, 2, 1, body)
    res = pl.pallas_call(
        body,
        out_shape=(out_shape, *x_out),
        grid=grid,
        in_specs=[pl.BlockSpec((tm, tk), lambda i, j, k: (i, k)), b_spec] + x_in_specs,
        out_specs=(out_spec, *x_out_specs),
        scratch_shapes=([pltpu.VMEM((tm, tn), F32)] if nk > 1 else []) + x_scr,
        compiler_params=_params(("parallel", "parallel", "arbitrary") if ex is None else ("arbitrary",) * 3),
        name=name,
    )(a, b, *x_in)
    return res[0] if ex is None else res


def _rstd(x):
    return lax.rsqrt(jnp.mean(x * x, axis=-1, keepdims=True) + EPS)


def _rms_fwd(x, g, *, name):
    T, D = x.shape
    tm = _tile(T, 512)

    def body(x_ref, g_ref, o_ref, ot_ref):
        x = x_ref[...]
        y = x * _rstd(x) * g_ref[...]
        o_ref[...] = y.astype(o_ref.dtype)
        ot_ref[...] = y.T.astype(ot_ref.dtype)

    return pl.pallas_call(
        body, out_shape=(jax.ShapeDtypeStruct((T, D), BF16), jax.ShapeDtypeStruct((D, T), BF16)), grid=(T // tm,),
        in_specs=[pl.BlockSpec((tm, D), lambda i: (i, 0)), pl.BlockSpec((1, D), lambda i: (0, 0))],
        out_specs=(pl.BlockSpec((tm, D), lambda i: (i, 0)), pl.BlockSpec((D, tm), lambda i: (0, i))),
        compiler_params=_params(("parallel",)), name=name)(x, g)


def _post_res_norm(a, g_post, h, g_pre, *, name):
    T, D = a.shape
    tm = _tile(T, 384, BLOCK)

    def body(a_ref, gp_ref, h_ref, gn_ref, h1_ref, o_ref, ot_ref):
        a = a_ref[...]
        h1 = h_ref[...] + a * _rstd(a) * gp_ref[...]
        h1_ref[...] = h1
        y = h1 * _rstd(h1) * gn_ref[...]
        o_ref[...] = y.astype(o_ref.dtype)
        ot_ref[...] = y.T.astype(ot_ref.dtype)

    row = pl.BlockSpec((tm, D), lambda i: (i, 0))
    vec = pl.BlockSpec((1, D), lambda i: (0, 0))
    return pl.pallas_call(
        body, out_shape=(jax.ShapeDtypeStruct((T, D), F32), jax.ShapeDtypeStruct((T, D), BF16),
                         jax.ShapeDtypeStruct((D, T), BF16)), grid=(T // tm,),
        in_specs=[row, vec, row, vec], out_specs=(row, row, pl.BlockSpec((D, tm), lambda i: (0, i))),
        compiler_params=_params(("parallel",)), name=name)(a, g_post, h, g_pre)


def _loss_head(a, g, h, target, *, name):
    T, D = a.shape
    tm = _tile(T, 512)

    def body(a_ref, g_ref, h_ref, t_ref, dy_ref, da_ref, dg_ref, loss_ref):
        i = pl.program_id(0)
        a = a_ref[...]
        r = _rstd(a)
        ah = a * r
        y = h_ref[...] + ah * g_ref[...]
        rows = i * tm + lax.broadcasted_iota(jnp.int32, (tm, 1), 0)
        err = jnp.where(rows >= BLOCK, y - t_ref[...], 0.0)
        dy = err / D
        dy_ref[...] = dy
        dah = dy * g_ref[...]
        da_ref[...] = (r * (dah - ah * jnp.mean(dah * ah, axis=-1, keepdims=True))).astype(da_ref.dtype)
        part = jnp.sum(jnp.sum(err * err, axis=1, keepdims=True), axis=0, keepdims=True)

        @pl.when(i == 0)
        def _():
            loss_ref[...] = jnp.zeros_like(loss_ref)
            dg_ref[...] = jnp.zeros_like(dg_ref)

        loss_ref[...] += jnp.broadcast_to(part, loss_ref.shape)
        dg_ref[...] += jnp.sum(dy * ah, axis=0, keepdims=True)

    row = pl.BlockSpec((tm, D), lambda i: (i, 0))
    vec = pl.BlockSpec((1, D), lambda i: (0, 0))
    return pl.pallas_call(
        body, out_shape=(jax.ShapeDtypeStruct((T, D), F32), jax.ShapeDtypeStruct((T, D), BF16),
                         jax.ShapeDtypeStruct((1, D), F32), jax.ShapeDtypeStruct((8, 128), F32)),
        grid=(T // tm,),
        in_specs=[row, vec, row, row],
        out_specs=(row, row, vec, pl.BlockSpec((8, 128), lambda i: (0, 0))),
        compiler_params=_params(("arbitrary",)), name=name)(a, g, h, target)


def _rms_bwd(x, g, dy, res, *, out_dtype, then=None, ex=None, name):
    T, D = x.shape
    tm = _tile(T, 512)
    has_res = res is not None
    n_in = 3 + has_res + (2 if then is not None else 0)
    n_out = 2 + (2 if then is not None else 0)

    def pull_back(x, g, dy):
        r = _rstd(x)
        xh = x * r
        dxh = dy * g
        return r * (dxh - xh * jnp.mean(dxh * xh, axis=-1, keepdims=True)), jnp.sum(dy * xh, axis=0, keepdims=True)

    def body(*refs):
        ins, outs = refs[:n_in], refs[n_in:]
        i = pl.program_id(0)

        @pl.when(i == 0)
        def _():
            for ref in outs[1::2]:
                ref[...] = jnp.zeros_like(ref)

        dx, dg = pull_back(ins[0][...], ins[1][...], ins[2][...].astype(F32))
        if has_res:
            dx = dx + ins[3][...]
        outs[0][...] = dx.astype(outs[0].dtype)
        outs[1][...] += dg
        if then is not None:
            dx2, dg2 = pull_back(ins[n_in - 2][...], ins[n_in - 1][...], dx)
            outs[2][...] = dx2.astype(outs[2].dtype)
            outs[3][...] += dg2

    row = pl.BlockSpec((tm, D), lambda i: (i, 0))
    vec = pl.BlockSpec((1, D), lambda i: (0, 0))
    ins = [x, g, dy] + ([res] if has_res else []) + (list(then) if then is not None else [])
    in_specs = [row, vec, row] + ([row] if has_res else []) + ([row, vec] if then is not None else [])
    out_shape = [jax.ShapeDtypeStruct((T, D), out_dtype), jax.ShapeDtypeStruct((1, D), F32)]
    out_specs = [row, vec]
    if then is not None:
        out_shape += [jax.ShapeDtypeStruct((T, D), BF16), jax.ShapeDtypeStruct((1, D), F32)]
        out_specs += [row, vec]
    grid = (T // tm,)
    body, x_in, x_in_specs, x_out, x_out_specs, x_scr = _carry(ex, grid, n_in, n_out, body)
    return pl.pallas_call(
        body, out_shape=(*out_shape, *x_out), grid=grid,
        in_specs=in_specs + x_in_specs, out_specs=(*out_specs, *x_out_specs), scratch_shapes=x_scr,
        compiler_params=_params(("arbitrary",)), name=name)(*ins, *x_in)


def _gate_up_swiglu(a, w, *, name):
    T, D = a.shape
    S, n = w.shape[0] // 2, w.shape[2]
    tm = _tile(T, 384, BLOCK)

    def body(a_ref, wg_ref, wu_ref, g_ref, u_ref, o_ref, ot_ref):
        x = a_ref[...]
        g = jnp.dot(x, wg_ref[...], preferred_element_type=F32)
        u = jnp.dot(x, wu_ref[...], preferred_element_type=F32)
        g16, u16 = g.astype(BF16), u.astype(BF16)
        g_ref[...] = g16
        u_ref[...] = u16
        gr = g16.astype(F32)
        act = gr / (1.0 + jnp.exp(-gr)) * u16.astype(F32)
        o_ref[...] = act.astype(o_ref.dtype)
        ot_ref[...] = act.T.astype(ot_ref.dtype)

    tile = pl.BlockSpec((tm, n), lambda i, j: (i, j))
    shp = jax.ShapeDtypeStruct((T, S * n), BF16)
    return pl.pallas_call(
        body, out_shape=(shp, shp, shp, jax.ShapeDtypeStruct((S * n, T), BF16)), grid=(T // tm, S),
        in_specs=[pl.BlockSpec((tm, D), lambda i, j: (i, 0)),
                  pl.BlockSpec((None, D, n), lambda i, j: (j, 0, 0)),
                  pl.BlockSpec((None, D, n), lambda i, j: (j + S, 0, 0))],
        out_specs=(tile, tile, tile, pl.BlockSpec((n, tm), lambda i, j: (j, i))),
        compiler_params=_params(("parallel", "parallel")), name=name)(a, w, w)


def _d_act_swiglu(dff, w_down, gate, up, *, name):
    T, D = dff.shape
    F = w_down.shape[0]
    tm = _tile(T, 384)
    tf = _tile(F, 768, BLOCK)

    def body(d_ref, w_ref, g_ref, u_ref, o_ref):
        dy = d_ref[...]
        for c in range(0, F, tf):
            d = lax.dot_general(dy, w_ref[c:c + tf, :], NT, preferred_element_type=F32)
            g = g_ref[:, c:c + tf].astype(F32)
            u = u_ref[:, c:c + tf].astype(F32)
            sg = 1.0 / (1.0 + jnp.exp(-g))
            o_ref[:, c:c + tf] = (d * u * (sg * (1.0 + g * (1.0 - sg)))).astype(o_ref.dtype)
            o_ref[:, F + c:F + c + tf] = (d * (g * sg)).astype(o_ref.dtype)

    row = pl.BlockSpec((tm, F), lambda i: (i, 0))
    return pl.pallas_call(
        body, out_shape=jax.ShapeDtypeStruct((T, 2 * F), BF16), grid=(T // tm,),
        in_specs=[pl.BlockSpec((tm, D), lambda i: (i, 0)), pl.BlockSpec((F, D), lambda i: (0, 0)), row, row],
        out_specs=pl.BlockSpec((tm, 2 * F), lambda i: (i, 0)),
        compiler_params=_params(("parallel",)), name=name)(dff, w_down, gate, up)


def _fox_gates_fwd(f_t, b, *, name):
    H, T = f_t.shape
    nb = T // BLOCK

    def body(f_ref, b_ref, cum_ref, col_ref):
        f = f_ref[...] + b_ref[...]
        ls = jnp.minimum(f, 0.0) - jnp.log(1.0 + jnp.exp(-jnp.abs(f)))
        t = lax.broadcasted_iota(jnp.int32, (H, T), 1)
        ls = jnp.where(t >= PAD_ROWS, ls, 0.0)
        upper = (lax.broadcasted_iota(jnp.int32, (BLOCK, BLOCK), 0)
                 <= lax.broadcasted_iota(jnp.int32, (BLOCK, BLOCK), 1)).astype(F32)
        carry = jnp.zeros((H, 1), F32)
        for blk in range(nb):
            seg = ls[:, blk * BLOCK:(blk + 1) * BLOCK]
            pre = jnp.dot(seg, upper, precision=HIGHEST, preferred_element_type=F32) + carry
            cum_ref[:, blk * BLOCK:(blk + 1) * BLOCK] = pre
            col_ref[blk * BLOCK:(blk + 1) * BLOCK, :] = jnp.concatenate(
                [pre, jnp.zeros((BLOCK - H, BLOCK), F32)], axis=0).T
            carry = pre[:, BLOCK - 1:BLOCK]

    vm = pl.BlockSpec(memory_space=pltpu.VMEM)
    return pl.pallas_call(
        body, out_shape=(jax.ShapeDtypeStruct((H, T), F32), jax.ShapeDtypeStruct((T, BLOCK), F32)),
        in_specs=[vm, vm], out_specs=(vm, vm),
        compiler_params=_params(), name=name)(f_t, b)


def _fox_gates_bwd(dcq, dck, f_t, b, *, name):
    H, T = f_t.shape
    nb = T // BLOCK

    def body(dq_ref, d_ref, f_ref, b_ref, df_ref, db_ref):
        lower = (lax.broadcasted_iota(jnp.int32, (BLOCK, BLOCK), 0)
                 >= lax.broadcasted_iota(jnp.int32, (BLOCK, BLOCK), 1)).astype(F32)
        carry = jnp.zeros((H, 1), F32)
        for blk in range(nb - 1, -1, -1):
            seg = dq_ref[:, blk * BLOCK:(blk + 1) * BLOCK] - d_ref[:, blk * BLOCK:(blk + 1) * BLOCK]
            suf = jnp.dot(seg, lower, precision=HIGHEST, preferred_element_type=F32) + carry
            df_ref[:, blk * BLOCK:(blk + 1) * BLOCK] = suf
            carry = suf[:, 0:1]
        f = f_ref[...] + b_ref[...]
        t = lax.broadcasted_iota(jnp.int32, (H, T), 1)
        df = jnp.where(t >= PAD_ROWS, df_ref[...] / (1.0 + jnp.exp(f)), 0.0)
        df_ref[...] = df
        db_ref[...] = jnp.sum(df, axis=1, keepdims=True)

    vm = pl.BlockSpec(memory_space=pltpu.VMEM)
    return pl.pallas_call(
        body, out_shape=(jax.ShapeDtypeStruct((H, T), F32), jax.ShapeDtypeStruct((H, 1), F32)),
        in_specs=[vm, vm, vm, vm], out_specs=(vm, vm),
        compiler_params=_params(), name=name)(dcq, dck, f_t, b)


LANE_KC = HEAD_DIM
LANE_QC = HEAD_DIM + 3
LANE_END = HEAD_DIM + 6


def _split3(c):
    hi = c.astype(BF16).astype(F32)
    r = c - hi
    mid = r.astype(BF16).astype(F32)
    lo = (r - mid).astype(BF16).astype(F32)
    return hi, mid, lo


def _lanes(lane, data, start, terms, rest):
    out = rest
    for i, t in enumerate(terms):
        out = jnp.where(lane == start + i, t, out)
    return jnp.where(lane < HEAD_DIM, data, out)


def _fox_prep(proj, cum_col, *, name):
    T = proj.shape[0]
    tm = FOX_TILE
    nt = T // tm
    H = FOX_HEADS
    lanes = 2 * HEAD_DIM
    qb, kb, vb = 768 // lanes, 1280 // lanes, 1792 // lanes

    def body(q_ref, k_ref, v_ref, c_ref, qa_ref, ka_ref, va_ref):
        p = pl.program_id(0)
        i = pl.program_id(1)
        lane = lax.broadcasted_iota(jnp.int32, (tm, lanes), 1)
        rows = i * tm + lax.broadcasted_iota(jnp.int32, (tm, 1), 0)
        q2 = q_ref[...].astype(F32)
        k2 = k_ref[...].astype(F32)
        v2 = v_ref[...].astype(F32)
        cum = c_ref[...]
        for e in range(2):
            c = jnp.sum(jnp.where(lane == 2 * p + e, cum, 0.0), axis=1, keepdims=True)
            ck = jnp.where(rows >= PAD_ROWS, c, -NEG)
            qe, ke, ve = (q2, k2, v2) if e == 0 else tuple(pltpu.roll(a, HEAD_DIM, 1) for a in (q2, k2, v2))
            one = jnp.where(lane < LANE_END, 1.0, 0.0)
            qa = _lanes(lane, qe * SCALE, LANE_QC, _split3(c), jnp.where(lane < LANE_QC, -1.0, 0.0))
            ka = _lanes(lane, ke, LANE_KC, _split3(ck), one)
            va = jnp.where(lane < HEAD_DIM, ve, jnp.where(lane < LANE_QC, 1.0, 0.0))
            qa_ref[e] = qa.astype(BF16)
            ka_ref[e] = ka.astype(BF16)
            va_ref[e] = va.astype(BF16)

    def col(b):
        return pl.BlockSpec((tm, lanes), lambda p, i, b=b: (i, b + p))

    out = pl.BlockSpec((2, tm, lanes), lambda p, i: (p, i, 0))
    shp = jax.ShapeDtypeStruct((H, T, lanes), BF16)
    return pl.pallas_call(
        body, out_shape=(shp, shp, shp), grid=(H // 2, nt),
        in_specs=[col(qb), col(kb), col(vb), pl.BlockSpec((tm, lanes), lambda p, i: (i, 0))],
        out_specs=(out, out, out),
        compiler_params=_params(("parallel", "parallel")), name=name)(proj, proj, proj, cum_col)


def _fox_fwd(q_aug, k_aug, v_aug, *, ex=None, name):
    H, T, lanes = q_aug.shape
    tq = FOX_TILE
    nq = T // tq
    G = FOX_GROUP

    def body(q_ref, k_ref, v_ref, o_ref, lse_ref, m_scr, acc_scr):
        i = pl.program_id(1)
        m_scr[...] = jnp.full(m_scr.shape, NEG, F32)
        acc_scr[...] = jnp.zeros(acc_scr.shape, F32)

        def step(kb, diag):
            off = pl.multiple_of(kb * tq, tq)
            s_t = [lax.dot_general(k_ref[g, pl.ds(off, tq), :], q_ref[g], NT, preferred_element_type=F32)
                   for g in range(G)]
            if diag:
                r = lax.broadcasted_iota(jnp.int32, (tq, tq), 0)
                c = lax.broadcasted_iota(jnp.int32, (tq, tq), 1)
                s_t = [jnp.where(c >= r, s, NEG) for s in s_t]
            m_prev = [m_scr[g] for g in range(G)]
            m_new = [jnp.maximum(m_prev[g], jnp.max(s_t[g], axis=0, keepdims=True)) for g in range(G)]
            p_t = [jnp.exp(s_t[g] - m_new[g]).astype(BF16) for g in range(G)]
            pv = [lax.dot_general(v_ref[g, pl.ds(off, tq), :], p_t[g], TN, preferred_element_type=F32)
                  for g in range(G)]
            for g in range(G):
                acc_scr[g] = jnp.exp(m_prev[g] - m_new[g]) * acc_scr[g] + pv[g]
                m_scr[g] = m_new[g]

        def loop_body(kb, carry):
            step(kb, False)
            return carry

        lax.fori_loop(0, i, loop_body, 0)
        step(i, True)
        for g in range(G):
            acc = acc_scr[g]
            lse_ref[g] = m_scr[g] + jnp.log(acc[HEAD_DIM:HEAD_DIM + 1, :])
            acc_t = acc.T
            o_ref[g] = (acc_t[:, :HEAD_DIM] / acc_t[:, HEAD_DIM:HEAD_DIM + 1]).astype(o_ref.dtype)

    blk = pl.BlockSpec((G, tq, lanes), lambda h, i: (h, i, 0))
    full = pl.BlockSpec((G, T, lanes), lambda h, i: (h, 0, 0))
    grid = (H // G, nq)
    body, x_in, x_in_specs, x_out, x_out_specs, x_scr = _carry(ex, grid, 3, 2, body)
    return pl.pallas_call(
        body,
        out_shape=(jax.ShapeDtypeStruct((H, T, HEAD_DIM), BF16), jax.ShapeDtypeStruct((H, nq, 1, tq), F32), *x_out),
        grid=grid,
        in_specs=[blk, full, full] + x_in_specs,
        out_specs=(pl.BlockSpec((G, tq, HEAD_DIM), lambda h, i: (h, i, 0)),
                   pl.BlockSpec((G, None, 1, tq), lambda h, i: (h, i, 0, 0)), *x_out_specs),
        scratch_shapes=[pltpu.VMEM((G, 1, tq), F32), pltpu.VMEM((G, lanes, tq), F32)] + x_scr,
        compiler_params=_params(("arbitrary", "arbitrary")), name=name)(q_aug, k_aug, v_aug, *x_in)


def _fox_prep_bwd(dmix, o, *, name):
    T = dmix.shape[0]
    H = o.shape[0]
    tm = FOX_TILE
    lanes = 2 * HEAD_DIM
    first = 512 // lanes

    def body(d_ref, o_ref, da_ref):
        lane = lax.broadcasted_iota(jnp.int32, (tm, lanes), 1)
        d2 = d_ref[...].astype(F32)
        for e in range(2):
            de = d2 if e == 0 else pltpu.roll(d2, HEAD_DIM, 1)
            d64 = d_ref[:, e * HEAD_DIM:(e + 1) * HEAD_DIM].astype(F32)
            delta = jnp.sum(d64 * o_ref[e].astype(F32), axis=1, keepdims=True)
            da_ref[e] = _lanes(lane, de, LANE_KC, _split3(-delta), jnp.zeros((), F32)).astype(BF16)

    return pl.pallas_call(
        body, out_shape=jax.ShapeDtypeStruct((H, T, lanes), BF16), grid=(H // 2, T // tm),
        in_specs=[pl.BlockSpec((tm, lanes), lambda p, i: (i, first + p)),
                  pl.BlockSpec((2, tm, HEAD_DIM), lambda p, i: (p, i, 0))],
        out_specs=pl.BlockSpec((2, tm, lanes), lambda p, i: (p, i, 0)),
        compiler_params=_params(("parallel", "parallel")), name=name)(dmix, o)


def _fox_bwd(q_aug, k_aug, v_aug, do_aug, lse_row, *, ex=None, name):
    H, T, lanes = q_aug.shape
    tq = FOX_TILE
    nq = T // tq
    G = FOX_GROUP

    def body(q_ref, k_ref, v_ref, do_ref, lse_ref, dq_ref, dk_ref, dv_ref, dck_ref, dk_acc, dv_acc):
        j = pl.program_id(1)

        @pl.when(j == 0)
        def _():
            dq_ref[...] = jnp.zeros(dq_ref.shape, F32)

        dk_acc[...] = jnp.zeros(dk_acc.shape, F32)
        dv_acc[...] = jnp.zeros(dv_acc.shape, F32)

        def step(qb, diag):
            off = pl.multiple_of(qb * tq, tq)
            heads = range(G)
            qa = [q_ref[g, pl.ds(off, tq), :] for g in heads]
            da = [do_ref[g, pl.ds(off, tq), :] for g in heads]
            s_t = [lax.dot_general(k_ref[g], qa[g], NT, preferred_element_type=F32) for g in heads]
            dp_t = [lax.dot_general(v_ref[g], da[g], NT, preferred_element_type=F32) for g in heads]
            p_t = [jnp.exp(s_t[g] - lse_ref[g, qb]) for g in heads]
            if diag:
                r = lax.broadcasted_iota(jnp.int32, (tq, tq), 0)
                c = lax.broadcasted_iota(jnp.int32, (tq, tq), 1)
                p_t = [jnp.where(c >= r, p, 0.0) for p in p_t]
            dsb = [(p_t[g] * dp_t[g]).astype(BF16) for g in heads]
            dv = [jnp.dot(p_t[g].astype(BF16), da[g], preferred_element_type=F32) for g in heads]
            dk = [jnp.dot(dsb[g], qa[g], preferred_element_type=F32) for g in heads]
            dq = [jnp.dot(dsb[g].T, k_ref[g], preferred_element_type=F32) for g in heads]
            for g in heads:
                dv_acc[g] += dv[g]
                dk_acc[g] += dk[g]
                dq_ref[g, pl.ds(off, tq), :] += dq[g]

        step(j, True)

        def loop_body(qb, carry):
            step(qb, False)
            return carry

        lax.fori_loop(j + 1, nq, loop_body, 0)
        dk = dk_acc[...]
        dk_ref[...] = dk.astype(dk_ref.dtype)
        dck_ref[...] = -dk[:, :, LANE_KC:LANE_KC + 1]
        dv_ref[...] = dv_acc[...].astype(dv_ref.dtype)

    blk = pl.BlockSpec((G, tq, lanes), lambda h, j: (h, j, 0))
    full = pl.BlockSpec((G, T, lanes), lambda h, j: (h, 0, 0))
    grid = (H // G, nq)
    body, x_in, x_in_specs, x_out, x_out_specs, x_scr = _carry(ex, grid, 5, 4, body)
    return pl.pallas_call(
        body,
        out_shape=(jax.ShapeDtypeStruct((H, T, lanes), F32), jax.ShapeDtypeStruct((H, T, lanes), BF16),
                   jax.ShapeDtypeStruct((H, T, lanes), BF16), jax.ShapeDtypeStruct((H, T, 1), F32), *x_out),
        grid=grid,
        in_specs=[full, blk, blk, full, pl.BlockSpec((G, nq, 1, tq), lambda h, j: (h, 0, 0, 0))] + x_in_specs,
        out_specs=(full, blk, blk,
                   pl.BlockSpec((G, tq, 1), lambda h, j: (h, j, 0)), *x_out_specs),
        scratch_shapes=[pltpu.VMEM((G, tq, lanes), F32), pltpu.VMEM((G, tq, lanes), F32)] + x_scr,
        compiler_params=_params(("arbitrary", "arbitrary")), name=name,
    )(q_aug, k_aug, v_aug, do_aug, lse_row, *x_in)


def _t5_bucket_np(d):
    n = np.maximum(d, 0).astype(np.int32)
    max_exact = N_BUCKETS // 2
    nf = np.maximum(n, 1).astype(np.float32)
    large = max_exact + (np.log(nf / max_exact) / math.log(MAX_DISTANCE / max_exact)
                         * (N_BUCKETS - max_exact)).astype(np.int32)
    large = np.minimum(large, N_BUCKETS - 1)
    return np.where(n < max_exact, n, large)


def _bucket_onehots():
    k = np.arange(BLOCK)[:, None]
    q = np.arange(BLOCK)[None, :]
    eye = np.eye(N_BUCKETS, dtype=np.float32)
    cur = eye[_t5_bucket_np(q - k).reshape(-1)]
    prev = eye[_t5_bucket_np(BLOCK + q - k).reshape(-1)]
    return cur, prev


SWA_K_COL = SWA_Q_HEADS * HEAD_DIM // (2 * HEAD_DIM)
SWA_V_COL = SWA_K_COL + 1


def _swa_terms(raw, bc, bp, far, sink, n):
    k = lax.broadcasted_iota(jnp.int32, (BLOCK, BLOCK), 0)
    q = lax.broadcasted_iota(jnp.int32, (BLOCK, BLOCK), 1)
    never = 2 * BLOCK
    s_c = raw[0] + bc
    s_p = raw[1] + bp
    s_m = raw[2] + jnp.where(n == 1, bp, far)
    s_c = jnp.where((k <= q) & (k >= jnp.where(n >= 1, 0, PAD_ROWS)), s_c, NEG)
    s_p = jnp.where(k > q + jnp.where(n >= 2, 0, never), s_p, NEG)
    s_m = jnp.where(k >= jnp.where(n >= 1, PAD_ROWS, never), s_m, NEG)
    m = jnp.maximum(jnp.maximum(jnp.max(s_c, axis=0, keepdims=True), jnp.max(s_p, axis=0, keepdims=True)),
                    jnp.maximum(jnp.max(s_m, axis=0, keepdims=True), sink))
    e = [jnp.exp(s_c - m), jnp.exp(s_p - m), jnp.exp(s_m - m)]
    e_s = jnp.exp(sink - m)
    l = (jnp.sum(e[0], axis=0, keepdims=True) + jnp.sum(e[1], axis=0, keepdims=True)
         + jnp.sum(e[2], axis=0, keepdims=True) + e_s)
    return e, e_s, l


def _swa_specs():
    G = SWA_GROUP
    width = G * HEAD_DIM

    def rows(which, col):
        if which == "cur":
            return pl.BlockSpec((BLOCK, BLOCK), lambda kv, n: (n, col))
        if which == "prev":
            return pl.BlockSpec((BLOCK, BLOCK), lambda kv, n: (jnp.maximum(n - 1, 0), col))
        return pl.BlockSpec((BLOCK, BLOCK), lambda kv, n: (0, col))

    qblk = pl.BlockSpec((BLOCK, width), lambda kv, n: (n, kv))
    keys = [rows(w, SWA_K_COL) for w in ("cur", "prev", "meta")]
    vals = [rows(w, SWA_V_COL) for w in ("cur", "prev", "meta")]
    bias = pl.BlockSpec((G, BLOCK, BLOCK), lambda kv, n: (kv, 0, 0))
    smem = pl.BlockSpec(memory_space=pltpu.SMEM)
    return qblk, keys, vals, bias, smem


def _swa_own_kv(tile_ref, kv):
    lane = lax.broadcasted_iota(jnp.int32, (BLOCK, 2 * HEAD_DIM), 1)
    t = tile_ref[...].astype(F32)
    return jnp.where(lane // HEAD_DIM == kv, t, pltpu.roll(t, HEAD_DIM, 1)).astype(BF16)


def _swa_fwd(proj, bc, bp, far, sinks, *, name):
    T = proj.shape[0]
    nb = T // BLOCK
    G = SWA_GROUP
    lanes = 2 * HEAD_DIM

    def body(q_ref, kc_ref, kp_ref, km_ref, vc_ref, vp_ref, vm_ref, bc_ref, bp_ref, far_ref, sink_ref, o_ref):
        kv = pl.program_id(0)
        n = pl.program_id(1)
        lane = lax.broadcasted_iota(jnp.int32, (BLOCK, lanes), 1)
        kk = [_swa_own_kv(r, kv) for r in (kc_ref, kp_ref, km_ref)]
        vv = [_swa_own_kv(r, kv) for r in (vc_ref, vp_ref, vm_ref)]
        heads, blocks = range(G), range(3)
        q2 = [q_ref[:, pair * lanes:(pair + 1) * lanes].astype(F32) * SCALE for pair in range(G // 2)]
        qm = [jnp.where(lane // HEAD_DIM == g % 2, q2[g // 2], 0.0).astype(BF16) for g in heads]
        raw = [[lax.dot_general(kk[b], qm[g], NT, preferred_element_type=F32) for b in blocks] for g in heads]
        terms = [_swa_terms(raw[g], bc_ref[g], bp_ref[g], far_ref[kv * G + g], sink_ref[kv * G + g], n) for g in heads]
        o_t = [sum(lax.dot_general(vv[b], terms[g][0][b].astype(BF16), TN, preferred_element_type=F32) for b in blocks)
               for g in heads]
        outs = [(o_t[g] / terms[g][2]).T for g in heads]
        for pair in range(G // 2):
            o_ref[:, pair * lanes:(pair + 1) * lanes] = jnp.where(
                lane < HEAD_DIM, outs[2 * pair], outs[2 * pair + 1]).astype(o_ref.dtype)

    qblk, keys, vals, bias, smem = _swa_specs()
    return pl.pallas_call(
        body, out_shape=jax.ShapeDtypeStruct((T, SWA_Q_HEADS * HEAD_DIM), BF16), grid=(SWA_KV_HEADS, nb),
        in_specs=[qblk] + keys + vals + [bias, bias, smem, smem],
        out_specs=qblk,
        compiler_params=_params(("parallel", "parallel")), name=name,
    )(proj, proj, proj, proj, proj, proj, proj, bc, bp, far, sinks)


def _swa_bwd(proj, dmix, bc, bp, far, sinks, *, ex=None, name):
    T = proj.shape[0]
    nb = T // BLOCK
    G = SWA_GROUP
    Hq = SWA_Q_HEADS
    lanes = 2 * HEAD_DIM

    def body(q_ref, kc_ref, kp_ref, km_ref, vc_ref, vp_ref, vm_ref, do_ref, bc_ref, bp_ref, far_ref, sink_ref,
             dq_ref, dk_ref, dv_ref, dbc_ref, dbp_ref, dbf_ref, dsk_ref):
        kv = pl.program_id(0)
        n = pl.program_id(1)

        @pl.when(n == 0)
        def _():
            for ref in (dk_ref, dv_ref, dbc_ref, dbp_ref, dbf_ref, dsk_ref):
                ref[...] = jnp.zeros(ref.shape, F32)

        lane = lax.broadcasted_iota(jnp.int32, (BLOCK, lanes), 1)
        kk = [_swa_own_kv(r, kv) for r in (kc_ref, kp_ref, km_ref)]
        vv = [_swa_own_kv(r, kv) for r in (vc_ref, vp_ref, vm_ref)]
        heads, blocks = range(G), range(3)
        q2 = [q_ref[:, pair * lanes:(pair + 1) * lanes].astype(F32) * SCALE for pair in range(G // 2)]
        d2 = [do_ref[:, pair * lanes:(pair + 1) * lanes] for pair in range(G // 2)]
        own = [lane // HEAD_DIM == g % 2 for g in heads]
        qm = [jnp.where(own[g], q2[g // 2], 0.0).astype(BF16) for g in heads]
        dom = [jnp.where(own[g], d2[g // 2], jnp.zeros_like(d2[0])) for g in heads]
        raw = [[lax.dot_general(kk[b], qm[g], NT, preferred_element_type=F32) for b in blocks] for g in heads]
        dp = [[lax.dot_general(vv[b], dom[g], NT, preferred_element_type=F32) for b in blocks] for g in heads]
        p, ds16 = [], []
        for g in heads:
            e, e_s, l = _swa_terms(raw[g], bc_ref[g], bp_ref[g], far_ref[kv * G + g], sink_ref[kv * G + g], n)
            inv = 1.0 / l
            pg = [e[b] * inv for b in blocks]
            delta = sum(jnp.sum(pg[b] * dp[g][b], axis=0, keepdims=True) for b in blocks)
            ds = [pg[b] * (dp[g][b] - delta) for b in blocks]
            dsk_ref[g] += -(e_s * inv) * delta
            dbc_ref[g] += ds[0]
            dbp_ref[g] += ds[1] + jnp.where(n == 1, ds[2], 0.0)
            dbf_ref[g] += jnp.where(n >= 2, ds[2], 0.0)
            p.append([x.astype(BF16) for x in pg])
            ds16.append([x.astype(BF16) for x in ds])
        dq_t = [sum(lax.dot_general(kk[b], ds16[g][b], TN, preferred_element_type=F32) for b in blocks) for g in heads]
        dk = [sum(jnp.dot(ds16[g][b], qm[g], preferred_element_type=F32) for g in heads) for b in blocks]
        dv = [sum(jnp.dot(p[g][b], dom[g], preferred_element_type=F32) for g in heads) for b in blocks]
        dqs = [dq_t[g].T * SCALE for g in heads]
        for pair in range(G // 2):
            dq_ref[:, pair * lanes:(pair + 1) * lanes] = jnp.where(
                lane < HEAD_DIM, dqs[2 * pair], dqs[2 * pair + 1]).astype(dq_ref.dtype)
        cur_off = pl.multiple_of(n * BLOCK, BLOCK)
        prev_off = pl.multiple_of(jnp.maximum(n - 1, 0) * BLOCK, BLOCK)
        for acc, ref in ((dk, dk_ref), (dv, dv_ref)):
            tot = [a + pltpu.roll(a, HEAD_DIM, 1) for a in acc]
            ref[pl.ds(cur_off, BLOCK), :] += tot[0]
            ref[pl.ds(prev_off, BLOCK), :] += tot[1]
            ref[0:BLOCK, :] += tot[2]

    qblk, keys, vals, bias, smem = _swa_specs()
    kvfull = pl.BlockSpec((None, T, lanes), lambda kv, n: (kv, 0, 0))
    dsk = pl.BlockSpec((G, 1, BLOCK), lambda kv, n: (kv, 0, 0))
    grid = (SWA_KV_HEADS, nb)
    body, x_in, x_in_specs, x_out, x_out_specs, x_scr = _carry(ex, grid, 12, 7, body)
    tile = jax.ShapeDtypeStruct((Hq, BLOCK, BLOCK), F32)
    return pl.pallas_call(
        body,
        out_shape=(jax.ShapeDtypeStruct((T, Hq * HEAD_DIM), BF16),
                   jax.ShapeDtypeStruct((SWA_KV_HEADS, T, lanes), F32),
                   jax.ShapeDtypeStruct((SWA_KV_HEADS, T, lanes), F32),
                   tile, tile, tile, jax.ShapeDtypeStruct((Hq, 1, BLOCK), F32), *x_out),
        grid=grid,
        in_specs=[qblk] + keys + vals + [qblk, bias, bias, smem, smem] + x_in_specs,
        out_specs=(qblk, kvfull, kvfull, bias, bias, bias, dsk, *x_out_specs),
        scratch_shapes=x_scr,
        compiler_params=_params(("arbitrary", "arbitrary")), name=name,
    )(proj, proj, proj, proj, proj, proj, proj, dmix, bc, bp, far, sinks, *x_in)


def _small_grads(dbc, dbp, dbf, dsk, oh_cur, oh_prev, *, name):
    Hq = dbc.shape[0]

    def body(dbc_ref, dbp_ref, dbf_ref, dsk_ref, oc_ref, op_ref, tab_ref, sink_ref):
        tab = (jnp.dot(dbc_ref[...], oc_ref[...], precision=HIGHEST, preferred_element_type=F32)
               + jnp.dot(dbp_ref[...], op_ref[...], precision=HIGHEST, preferred_element_type=F32))
        far = jnp.sum(dbf_ref[...], axis=1, keepdims=True)
        last = lax.broadcasted_iota(jnp.int32, (Hq, N_BUCKETS), 1) == N_BUCKETS - 1
        tab_ref[...] = tab + jnp.where(last, far, 0.0)
        sink_ref[...] = jnp.sum(dsk_ref[...], axis=1, keepdims=True)

    vm = pl.BlockSpec(memory_space=pltpu.VMEM)
    return pl.pallas_call(
        body, out_shape=(jax.ShapeDtypeStruct((Hq, N_BUCKETS), F32), jax.ShapeDtypeStruct((Hq, 1), F32)),
        in_specs=[vm] * 6, out_specs=(vm, vm), compiler_params=_params(), name=name,
    )(dbc.reshape(Hq, -1), dbp.reshape(Hq, -1), dbf.reshape(Hq, -1), dsk.reshape(Hq, -1), oh_cur, oh_prev)


def _coords():
    return lax.axis_index("x"), lax.axis_index("y"), lax.axis_index("c")


class _Exchange:
    def __init__(self, inputs, out_shapes, scratch, start, finish):
        self.inputs, self.out_shapes, self.scratch, self.start, self.finish = inputs, out_shapes, scratch, start, finish


def _carry(ex, grid, n_in, n_out, body):
    if ex is None:
        return body, [], [], [], [], []
    ni, no = len(ex.inputs), len(ex.out_shapes)

    def at_step(which):
        cond = None
        for axis, n in enumerate(grid):
            c = pl.program_id(axis) == (0 if which == "first" else n - 1)
            cond = c if cond is None else cond & c
        return cond

    def wrapped(*refs):
        refs = list(refs)
        n_own_scr = len(refs) - (n_in + ni + n_out + no) - len(ex.scratch)
        own_in, side_in = refs[:n_in], refs[n_in:n_in + ni]
        own_out = refs[n_in + ni:n_in + ni + n_out]
        side_out = refs[n_in + ni + n_out:n_in + ni + n_out + no]
        rest = refs[n_in + ni + n_out + no:]
        own_scr, sems = rest[:n_own_scr], rest[n_own_scr:]

        @pl.when(at_step("first"))
        def _():
            ex.start(side_in, side_out, sems)

        body(*own_in, *own_out, *own_scr)

        @pl.when(at_step("last"))
        def _():
            ex.finish(side_in, side_out, sems)

    hbm = pl.BlockSpec(memory_space=pl.ANY)
    return wrapped, list(ex.inputs), [hbm] * ni, list(ex.out_shapes), [hbm] * no, list(ex.scratch)


def _run_exchange(ex, *, name):
    ni, no = len(ex.inputs), len(ex.out_shapes)

    def body(*refs):
        ins, outs, sems = refs[:ni], refs[ni:ni + no], refs[ni + no:]
        ex.start(ins, outs, sems)
        ex.finish(ins, outs, sems)

    hbm = pl.BlockSpec(memory_space=pl.ANY)
    return pl.pallas_call(
        body, out_shape=tuple(ex.out_shapes), in_specs=[hbm] * ni, out_specs=tuple([hbm] * no),
        scratch_shapes=ex.scratch, compiler_params=_params(), name=name)(*ex.inputs)


def _gather_exchange(shards):
    nt = len(shards)

    def copies(ins, outs, sems):
        send_sems, recv_sems, local_sems = sems
        x, y, c = _coords()
        me, sibling = (x, y, c), (x, y, 1 - c)
        chips = [(1 - x, y), (x, 1 - y), (1 - x, 1 - y)]

        def slot(t, dev):
            return outs[t].at[4 * dev[0] + 2 * dev[1] + dev[2]]

        def copy(t, k, block, to, src=None):
            dst = slot(t, block)
            return pltpu.make_async_remote_copy(
                src_ref=dst if src is None else src, dst_ref=dst,
                send_sem=send_sems.at[t, k], recv_sem=recv_sems.at[t, k], device_id=to, device_id_type=MESH)

        mine = [pltpu.make_async_copy(ins[t], slot(t, me), local_sems.at[t]) for t in range(nt)]
        first = []
        for t in range(nt):
            first.append(copy(t, 0, me, sibling, src=ins[t]))
            first += [copy(t, 1 + j, me, (*chip, c), src=ins[t]) for j, chip in enumerate(chips)]
        return copy, mine, first, me, sibling, chips, c

    def start(ins, outs, sems):
        _, mine, first, *_ = copies(ins, outs, sems)
        for cp in mine + first:
            cp.start()

    def finish(ins, outs, sems):
        copy, mine, first, me, sibling, chips, c = copies(ins, outs, sems)
        passed = []
        for j, chip in enumerate(chips):
            for t in range(nt):
                copy(t, 1 + j, (*chip, c), me).wait_recv()
                cp = copy(t, 4 + j, (*chip, c), sibling)
                cp.start()
                passed.append(cp)
        for t in range(nt):
            copy(t, 0, sibling, me).wait_recv()
            for j, chip in enumerate(chips):
                copy(t, 4 + j, (*chip, 1 - c), me).wait_recv()
        for cp in first + passed:
            cp.wait_send()
        for cp in mine:
            cp.wait()

    return _Exchange(
        list(shards), [jax.ShapeDtypeStruct((N_DEV,) + s.shape, s.dtype) for s in shards],
        [pltpu.SemaphoreType.DMA((nt, 7)), pltpu.SemaphoreType.DMA((nt, 7)), pltpu.SemaphoreType.DMA((nt,))],
        start, finish)


def _swap_exchange(arrays, n_slices, copies):
    nt = len(arrays)

    def start(ins, outs, sems):
        for cp in copies(ins, outs, sems):
            cp.start()

    def finish(ins, outs, sems):
        sends = copies(ins, outs, sems)
        for cp in sends:
            cp.wait_recv()
        for cp in sends:
            cp.wait_send()

    return _Exchange(
        list(arrays), [jax.ShapeDtypeStruct((n_slices,) + a.shape[1:], a.dtype) for a in arrays],
        [pltpu.SemaphoreType.DMA((nt, n_slices)), pltpu.SemaphoreType.DMA((nt, n_slices))], start, finish)


def _cores_exchange(gs):
    def copies(ins, outs, sems):
        send_sems, recv_sems = sems
        x, y, c = _coords()
        return [pltpu.make_async_remote_copy(
            src_ref=ins[t].at[2 * j + (1 - c)], dst_ref=outs[t].at[j],
            send_sem=send_sems.at[t, j], recv_sem=recv_sems.at[t, j], device_id=(x, y, 1 - c), device_id_type=MESH)
            for t in range(len(gs)) for j in range(4)]

    return _swap_exchange(gs, 4, copies)


def _chips_exchange(ps):
    def copies(ins, outs, sems):
        send_sems, recv_sems = sems
        x, y, c = _coords()
        peers = [(1 - x, y), (x, 1 - y), (1 - x, 1 - y)]
        return [pltpu.make_async_remote_copy(
            src_ref=ins[t].at[2 * px + py], dst_ref=outs[t].at[k],
            send_sem=send_sems.at[t, k], recv_sem=recv_sems.at[t, k], device_id=(px, py, c), device_id_type=MESH)
            for t in range(len(ps)) for k, (px, py) in enumerate(peers)]

    return _swap_exchange(ps, 3, copies)


def _add_cores(g, r, core, *, name):
    _, A, B = g.shape
    ta = _tile(A, 512, 16)

    def body(core_ref, a_ref, b_ref, o_ref, o16_ref):
        s = a_ref[...] + b_ref[...]
        o_ref[...] = s
        o16_ref[...] = s.astype(BF16)

    blk = (None, ta, B)
    out = pl.BlockSpec(blk, lambda j, i, core_ref: (j, i, 0))
    return pl.pallas_call(
        body, out_shape=(jax.ShapeDtypeStruct((4, A, B), F32), jax.ShapeDtypeStruct((4, A, B), BF16)),
        grid_spec=pltpu.PrefetchScalarGridSpec(
            num_scalar_prefetch=1, grid=(4, A // ta),
            in_specs=[pl.BlockSpec(blk, lambda j, i, core_ref: (2 * j + core_ref[0], i, 0)),
                      pl.BlockSpec(blk, lambda j, i, core_ref: (j, i, 0))],
            out_specs=(out, out)),
        compiler_params=_params(("parallel", "parallel")), name=name)(core, g, r)


def _adamw_math(w, g, m, v):
    m = ADAM_B1 * m + (1.0 - ADAM_B1) * g
    v = ADAM_B2 * v + (1.0 - ADAM_B2) * (g * g)
    m_hat = m / (1.0 - ADAM_B1 ** ADAM_STEP)
    v_hat = v / (1.0 - ADAM_B2 ** ADAM_STEP)
    delta = -ADAM_LR * (m_hat / (jnp.sqrt(v_hat) + ADAM_EPS) + ADAM_WD * w)
    return delta, m, v


def _sum_adamw(p, r, chip, w, m, v, *, segs, ta, name):
    Aw, Bw = w.shape
    Bg = p.shape[2]
    assert Aw % ta == 0

    def body(chip_ref, p_ref, r0, r1, r2, w_ref, m_ref, v_ref, g_out, d_out, m_out, v_out):
        for gc, wc, n in segs:
            g = ((p_ref[:, gc:gc + n] + r0[:, gc:gc + n].astype(F32)) + r1[:, gc:gc + n].astype(F32)
                 ) + r2[:, gc:gc + n].astype(F32)
            delta, m_new, v_new = _adamw_math(w_ref[:, wc:wc + n], g, m_ref[:, wc:wc + n], v_ref[:, wc:wc + n])
            g_out[:, wc:wc + n] = g
            d_out[:, wc:wc + n] = delta
            m_out[:, wc:wc + n] = m_new
            v_out[:, wc:wc + n] = v_new

    gblk = (None, ta, Bg)
    row = pl.BlockSpec((ta, Bw), lambda i, chip_ref: (i, 0))
    rspecs = [pl.BlockSpec(gblk, (lambda i, chip_ref, k=k: (k, i, 0))) for k in range(3)]
    shp = jax.ShapeDtypeStruct((Aw, Bw), F32)
    return pl.pallas_call(
        body, out_shape=(shp, shp, shp, shp),
        grid_spec=pltpu.PrefetchScalarGridSpec(
            num_scalar_prefetch=1, grid=(Aw // ta,),
            in_specs=[pl.BlockSpec(gblk, lambda i, chip_ref: (chip_ref[0], i, 0))] + rspecs + [row, row, row],
            out_specs=(row, row, row, row)),
        compiler_params=_params(("parallel",)), name=name)(chip, p, r, r, r, w, m, v)


def _adamw(w, g, m, v, *, name):
    def body(w_ref, g_ref, m_ref, v_ref, d_out, m_out, v_out):
        delta, m_new, v_new = _adamw_math(w_ref[...], g_ref[...], m_ref[...], v_ref[...])
        d_out[...] = delta
        m_out[...] = m_new
        v_out[...] = v_new

    vm = pl.BlockSpec(memory_space=pltpu.VMEM)
    shp = jax.ShapeDtypeStruct(w.shape, F32)
    return pl.pallas_call(body, out_shape=(shp, shp, shp), in_specs=[vm] * 4, out_specs=(vm, vm, vm),
                          compiler_params=_params(), name=name)(w, g, m, v)


def _small_allreduce_adamw(s, w, m, v, *, name):
    R, W = s.shape

    def body(s_ref, w_ref, m_ref, v_ref, g_out, d_out, m_out, v_out, gath, send_sems, recv_sems):
        x, y, c = _coords()
        mine = 4 * x + 2 * y + c
        gath[mine] = s_ref[...]
        peers = [((1 - x) if k & 4 else x, (1 - y) if k & 2 else y, (1 - c) if k & 1 else c) for k in range(1, N_DEV)]
        sends = []
        for k in range(1, N_DEV):
            peer = peers[k - 1]
            sends.append(pltpu.make_async_remote_copy(
                src_ref=s_ref, dst_ref=gath.at[mine], send_sem=send_sems.at[k - 1], recv_sem=recv_sems.at[k - 1],
                device_id=peer, device_id_type=MESH))
        for cp in sends:
            cp.start()
        for k in range(1, N_DEV):
            peer = peers[k - 1]
            pltpu.make_async_remote_copy(
                src_ref=s_ref, dst_ref=gath.at[4 * peer[0] + 2 * peer[1] + peer[2]],
                send_sem=send_sems.at[k - 1], recv_sem=recv_sems.at[k - 1],
                device_id=peer, device_id_type=MESH).wait_recv()
        for cp in sends:
            cp.wait_send()
        g = gath[0]
        for d in range(1, N_DEV):
            g = g + gath[d]
        delta, m_new, v_new = _adamw_math(w_ref[...], g, m_ref[...], v_ref[...])
        g_out[...] = g
        d_out[...] = delta
        m_out[...] = m_new
        v_out[...] = v_new

    vm = pl.BlockSpec(memory_space=pltpu.VMEM)
    shp = jax.ShapeDtypeStruct((R, W), F32)
    return pl.pallas_call(
        body, out_shape=(shp, shp, shp, shp), in_specs=[vm] * 4, out_specs=(vm, vm, vm, vm),
        scratch_shapes=[pltpu.VMEM((N_DEV, R, W), F32), pltpu.SemaphoreType.DMA((N_DEV - 1,)),
                        pltpu.SemaphoreType.DMA((N_DEV - 1,))],
        compiler_params=_params(), name=name)(s, w, m, v)


def _pack_small(rel_bias, g1, g2, g3, g4, b_forget, sinks, extra=None, meta=None):
    misc = jnp.concatenate([rel_bias.reshape(-1), b_forget.reshape(-1), sinks.reshape(-1)])
    misc = jnp.concatenate([misc, jnp.zeros((D_MODEL - misc.shape[0],), F32)])[None]
    last = jnp.zeros((1, D_MODEL), F32) if extra is None else extra
    meta = jnp.zeros((N_META, D_MODEL), F32) if meta is None else meta
    return jnp.concatenate([g1, g2, g3, g4, misc, last, jnp.zeros((2, D_MODEL), F32), meta], axis=0)


def _unpack_small(p):
    nrb = N_BUCKETS * SWA_Q_HEADS
    misc = p[4]
    return dict(rel_bias=misc[:nrb].reshape(N_BUCKETS, SWA_Q_HEADS), ln_pre_mix=p[0:1], ln_post_mix=p[1:2],
                ln_pre_ffn=p[2:3], ln_post_ffn=p[3:4], b_forget=misc[nrb:nrb + 8].reshape(1, 8),
                sinks=misc[nrb + 8:nrb + 16].reshape(1, 8))


def _shard_order(pieces, shard, pad):
    T, dtype = pieces[0].shape[0], pieces[0].dtype
    total = sum(p.shape[1] for p in pieces)
    assert total % shard == 0
    out, zeros = [], jnp.zeros((T, pad), dtype)
    for s in range(total // shard):
        lo, hi, start = s * shard, (s + 1) * shard, 0
        for p in pieces:
            end = start + p.shape[1]
            if max(lo, start) < min(hi, end):
                out.append(p[:, max(lo, start) - start:min(hi, end) - start])
            start = end
        out.append(zeros)
    return jnp.concatenate(out, axis=1)


def _unheads(a):
    return a.transpose(1, 0, 2).reshape(a.shape[1], -1)


def kernel(x, meta_tokens, rel_bias, ln_pre_mix, ln_post_mix, ln_pre_ffn, ln_post_ffn, w_in, b_forget, sinks, w_out, w_gate_up, w_down, loss_target, m_meta_tokens, m_rel_bias, m_ln_pre_mix, m_ln_post_mix, m_ln_pre_ffn, m_ln_post_ffn, m_w_in, m_b_forget, m_sinks, m_w_out, m_w_gate_up, m_w_down, v_meta_tokens, v_rel_bias, v_ln_pre_mix, v_ln_post_mix, v_ln_pre_ffn, v_ln_post_ffn, v_w_in, v_b_forget, v_sinks, v_w_out, v_w_gate_up, v_w_down):
    seq = x.shape[1]
    T = BLOCK + seq
    assert T % FOX_TILE == 0
    nq = T // FOX_TILE
    tm = _tile(T, 1056)
    cin = w_in.shape[2]
    hid = w_down.shape[1]
    assert w_gate_up.shape[2] == 2 * hid and cin <= W_IN_PAD and hid <= HID_PAD

    x_i, y_i, c_i = _coords()
    core = jnp.reshape(c_i, (1,)).astype(jnp.int32)
    chip = jnp.reshape(2 * x_i + y_i, (1,)).astype(jnp.int32)
    w_in_s = jnp.pad(w_in[0].astype(BF16), ((0, 0), (0, W_IN_PAD - cin)))
    w_gu_s = jnp.pad(w_gate_up[0].astype(BF16).reshape(D_MODEL, 2, hid), ((0, 0), (0, 0), (0, HID_PAD - hid)))
    w_gu_s = w_gu_s.reshape(D_MODEL, 2 * HID_PAD)
    w_down_s = jnp.pad(w_down[0].astype(BF16), ((0, HID_PAD - hid), (0, 0)))
    g_in, g_meta = _run_exchange(_gather_exchange([w_in_s, meta_tokens]), name="ag_w_in")
    gather_rest = _gather_exchange([w_out[0].astype(BF16), w_gu_s, w_down_s])
    w_in_full = g_in[:, :, :cin].transpose(1, 0, 2).reshape(D_MODEL, N_DEV * cin)
    w_qkv = w_in_full[:, :D_QKV]
    w_f = jnp.pad(w_in_full[:, D_QKV:], ((0, 0), (0, BLOCK - FOX_HEADS)))
    meta_full = g_meta.transpose(1, 0, 2).reshape(N_META, D_MODEL)

    h0 = jnp.concatenate([jnp.zeros((PAD_ROWS, D_MODEL), F32), meta_full, x[0]], axis=0)
    target = jnp.concatenate([jnp.zeros((BLOCK, D_MODEL), F32), loss_target[0]], axis=0)
    hn1, hn1_t = _rms_fwd(h0, ln_pre_mix, name="rms_pre_mix")
    proj = _matmul(hn1, w_qkv, out_dtype=BF16, tm=tm, tn=768, name="mm_in_proj")
    proj_f = _matmul(hn1, w_f, out_dtype=F32, tm=tm, tn=BLOCK, name="mm_in_proj_f")

    f_t = proj_f[:, :FOX_HEADS].T
    bf_col = b_forget.reshape(FOX_HEADS, 1)

    oh_cur, oh_prev = _bucket_onehots()
    bias_c = jnp.einsum("pb,bh->hp", jnp.asarray(oh_cur), rel_bias, precision=HIGHEST).reshape(8, BLOCK, BLOCK)
    bias_p = jnp.einsum("pb,bh->hp", jnp.asarray(oh_prev), rel_bias, precision=HIGHEST).reshape(8, BLOCK, BLOCK)
    far = rel_bias[N_BUCKETS - 1]
    sink_v = sinks[0]
    o_a = _swa_fwd(proj, bias_c, bias_p, far, sink_v, name="swa_fwd")

    _, cum_col = _fox_gates_fwd(f_t, bf_col, name="fox_gates_fwd")
    q_b, k_b, v_b = _fox_prep(proj, cum_col, name="fox_prep")
    o_b, lse_row, g_out, g_gu, g_down = _fox_fwd(q_b, k_b, v_b, ex=gather_rest, name="fox_fwd")
    w_out_full = g_out.reshape(D_MODEL, D_MODEL)
    w_down_full = g_down.reshape(N_DEV * HID_PAD, D_MODEL)

    mix = jnp.concatenate([o_a, _unheads(o_b)], axis=1)
    a1 = _matmul(mix, w_out_full, out_dtype=F32, tm=tm, tn=512, name="mm_out_proj")
    h1, hn2, hn2_t = _post_res_norm(a1, ln_post_mix, h0, ln_pre_ffn, name="post_mix_pre_ffn")
    gate, up, act, act_t = _gate_up_swiglu(hn2, g_gu, name="mm_gate_up")
    ff = _matmul(act, w_down_full, out_dtype=F32, tm=tm, tn=512, name="mm_down")
    dh2, dff, dg_post_ffn, loss_acc = _loss_head(ff, ln_post_ffn, h1, target, name="loss_head")

    dgu = _d_act_swiglu(dff, w_down_full, gate, up, name="mm_d_act")
    d_w_down = _matmul(act_t, dff, out_dtype=F32, tm=768, tn=512, name="mm_dw_down")
    dhn2 = _matmul(dgu, g_gu, nt=True, b_shards=True, out_dtype=F32, tm=_tile(T, 528), tn=512, name="mm_d_hn2")
    d_w_gu = _matmul(hn2_t, dgu, out_shards=True, out_dtype=F32, tm=512, tn=2 * HID_PAD, name="mm_dw_gate_up")
    dh1, dg_pre_ffn, da1, dg_post_mix = _rms_bwd(h1, ln_pre_ffn, dhn2, dh2, out_dtype=F32,
                                                 then=(a1, ln_post_mix), name="rms_bwd_pre_ffn_post_mix")
    dmix = _matmul(da1, w_out_full, nt=True, out_dtype=BF16, tm=tm, tn=512, name="mm_d_mix")
    d_w_out = _matmul(mix.T, da1, out_dtype=F32, tm=512, tn=512, name="mm_dw_out")

    ffn_grads = [d_w_out.reshape(N_DEV, -1, D_MODEL), d_w_gu, d_w_down.reshape(N_DEV, HID_PAD, D_MODEL)]
    dq_a, dk_a, dv_a, dbc, dbp, dbf, dsk, *ffn_sibling = _swa_bwd(
        proj, dmix, bias_c, bias_p, far, sink_v, ex=_cores_exchange(ffn_grads), name="swa_bwd")
    low = (jnp.arange(2 * HEAD_DIM) < HEAD_DIM)[None, :]
    dk_a = jnp.where(low, dk_a[0], dk_a[1]).astype(BF16)
    dv_a = jnp.where(low, dv_a[0], dv_a[1]).astype(BF16)
    d_tab, d_sink = _small_grads(dbc, dbp, dbf, dsk, jnp.asarray(oh_cur), jnp.asarray(oh_prev), name="small_grads")
    ffn_sums = [_add_cores(g, r, core, name="rs_add_" + t)
                for g, r, t in zip(ffn_grads, ffn_sibling, ["w_out", "w_gate_up", "w_down"])]

    do_b = _fox_prep_bwd(dmix, o_b, name="fox_prep_bwd")
    dq_t, dk_b, dv_b, dck, *ffn_chips = _fox_bwd(
        q_b, k_b, v_b, do_b, lse_row, ex=_chips_exchange([s[1] for s in ffn_sums]), name="fox_bwd")
    dcq = dq_t[:, :, LANE_QC]
    df_t, d_bf = _fox_gates_bwd(dcq, dck.reshape(FOX_HEADS, T), f_t, bf_col, name="fox_gates_bwd")
    dq_b = (dq_t[:, :, :HEAD_DIM].transpose(1, 0, 2).reshape(T, FOX_W) * SCALE).astype(BF16)
    dk_b = dk_b[:, :, :HEAD_DIM].transpose(1, 0, 2).reshape(T, FOX_W)
    dv_b = dv_b[:, :, :HEAD_DIM].transpose(1, 0, 2).reshape(T, FOX_W)

    dproj_s = _shard_order([dq_a, dk_a, dv_a, dq_b, dk_b, dv_b, df_t.T.astype(BF16)], cin, W_IN_PAD - cin)
    d_w_in = _matmul(hn1_t, dproj_s, out_shards=True, out_dtype=F32, tm=512, tn=W_IN_PAD, name="mm_dw_in")
    dhn1, in_sibling = _matmul(dproj_s, g_in, nt=True, b_shards=True, out_dtype=F32, tm=tm, tn=512,
                               ex=_cores_exchange([d_w_in]), name="mm_d_hn1")
    in_sum = _add_cores(d_w_in, in_sibling, core, name="rs_add_w_in")
    dh0, dg_pre_mix, in_chips = _rms_bwd(h0, ln_pre_mix, dhn1, dh1, out_dtype=F32,
                                         ex=_chips_exchange([in_sum[1]]), name="rms_bwd_pre_mix")
    grad_x = dh0[BLOCK:][None]
    d_meta = dh0[PAD_ROWS:BLOCK]

    tags = ["w_in", "w_out", "w_gate_up", "w_down"]
    chip_sum = [in_sum[0]] + [s[0] for s in ffn_sums]
    from_chips = [in_chips] + list(ffn_chips)
    shard_w = [(w_in, m_w_in, v_w_in), (w_out, m_w_out, v_w_out), (w_gate_up, m_w_gate_up, v_w_gate_up),
               (w_down, m_w_down, v_w_down)]
    segs = [[(0, 0, cin)], [(0, 0, D_MODEL)], [(0, 0, hid), (HID_PAD, hid, hid)], [(0, 0, D_MODEL)]]
    tas = [256, BLOCK, 256, hid]
    big = [{}, {}, {}, {}]
    for i, t in enumerate(tags):
        w_t, m_t, v_t = shard_w[i]
        res = _sum_adamw(chip_sum[i], from_chips[i], chip, w_t[0], m_t[0], v_t[0], segs=segs[i], ta=tas[i],
                         name="rs_adamw_" + t)
        for kind in range(4):
            big[kind][t] = res[kind][None]

    loss_row = jnp.pad(loss_acc[0:1, 0:1] * (0.5 / D_MODEL), ((0, 0), (0, D_MODEL - 1)))
    s_small = _pack_small(d_tab.T, dg_pre_mix, dg_post_mix, dg_pre_ffn, dg_post_ffn, d_bf, d_sink,
                          extra=loss_row, meta=d_meta)
    w_s = _pack_small(rel_bias, ln_pre_mix, ln_post_mix, ln_pre_ffn, ln_post_ffn, b_forget, sinks)
    m_s = _pack_small(m_rel_bias, m_ln_pre_mix, m_ln_post_mix, m_ln_pre_ffn, m_ln_post_ffn, m_b_forget, m_sinks)
    v_s = _pack_small(v_rel_bias, v_ln_pre_mix, v_ln_post_mix, v_ln_pre_ffn, v_ln_post_ffn, v_b_forget, v_sinks)
    small = _small_allreduce_adamw(s_small, w_s, m_s, v_s, name="small_allreduce_adamw")
    loss = small[0][5, 0]
    mcols = meta_tokens.shape[1]
    g_meta_mine = lax.dynamic_slice(small[0][8:8 + N_META], (0, (4 * x_i + 2 * y_i + c_i) * mcols), (N_META, mcols))
    big[0]["meta_tokens"] = g_meta_mine
    for kind, arr in enumerate(_adamw(meta_tokens, g_meta_mine, m_meta_tokens, v_meta_tokens, name="adamw_meta")):
        big[kind + 1]["meta_tokens"] = arr
    small = [_unpack_small(p) for p in small]

    names = ["meta_tokens", "rel_bias", "ln_pre_mix", "ln_post_mix", "ln_pre_ffn", "ln_post_ffn", "w_in",
             "b_forget", "sinks", "w_out", "w_gate_up", "w_down"]
    outs = [loss, grad_x]
    for kind in range(4):
        for nme in names:
            outs.append(big[kind][nme] if nme in big[kind] else small[kind][nme])
    return tuple(outs)
```

```python
import math

import numpy as np
import jax
import jax.numpy as jnp
from jax import lax
from jax.experimental import pallas as pl
from jax.experimental.pallas import tpu as pltpu

F32 = jnp.float32
BF16 = jnp.bfloat16
HIGHEST = lax.Precision.HIGHEST
MESH = pl.DeviceIdType.MESH

N_DEV = 8
D_MODEL = 1024
N_META = 16
HEAD_DIM = 64
SWA_Q_HEADS = 8
SWA_KV_HEADS = 2
SWA_GROUP = 4
FOX_HEADS = 8
FOX_W = FOX_HEADS * HEAD_DIM
BLOCK = 128
PAD_ROWS = BLOCK - N_META
N_BUCKETS = 32
MAX_DISTANCE = 128
D_FF = 2816
D_QKV = 2304
D_PROJ = D_QKV + FOX_HEADS
D_PROJ_PAD = 2560
EPS = 1e-6
NEG = -1e30
SCALE = HEAD_DIM ** -0.5
ADAM_LR, ADAM_B1, ADAM_B2, ADAM_EPS, ADAM_WD, ADAM_STEP = 0.001, 0.9, 0.999, 1e-08, 0.01, 10
VMEM_LIMIT = 56 * 1024 * 1024
FOX_TILE = 384
FOX_GROUP = 4
W_IN_PAD = 384
HID_PAD = 384

NT = (((1,), (1,)), ((), ()))
NN = (((1,), (0,)), ((), ()))
TN = (((0,), (0,)), ((), ()))


def _params(sem=None, **kw):
    if sem is not None:
        kw["dimension_semantics"] = sem
    return pltpu.CompilerParams(vmem_limit_bytes=VMEM_LIMIT, **kw)


def _tile(n, target, mult=16):
    best = None
    for t in range(mult, min(n, target) + 1, mult):
        if n % t == 0:
            best = t
    assert best is not None, (n, target)
    return best


def _matmul(a, b, *, nt=False, b_shards=False, out_shards=False, out_dtype, tm, tn=None, tk=None, ex=None, name):
    M, K = a.shape
    k_shards = b.shape[0] if (b_shards and nt) else 0
    if k_shards:
        N, ks = b.shape[1], b.shape[2]
        assert tk is None and K == k_shards * ks
    elif b_shards:
        N, tn = b.shape[0] * b.shape[2], b.shape[2]
    else:
        N = b.shape[0] if nt else b.shape[1]
    tk = K if tk is None else tk
    assert M % tm == 0 and N % tn == 0 and K % tk == 0, (name, a.shape, b.shape, tm, tn, tk)
    nk = K // tk
    dn = NT if nt else NN

    def body(a_ref, b_ref, o_ref, *scr):
        if k_shards:
            part = sum(lax.dot_general(a_ref[:, s * ks:(s + 1) * ks], b_ref[s], NT, preferred_element_type=F32)
                       for s in range(k_shards))
        else:
            part = lax.dot_general(a_ref[...], b_ref[...], dn, preferred_element_type=F32)
        if nk == 1:
            o_ref[...] = part.astype(o_ref.dtype)
        else:
            acc = scr[0]
            k = pl.program_id(2)

            @pl.when(k == 0)
            def _():
                acc[...] = part

            @pl.when(k > 0)
            def _():
                acc[...] += part

            @pl.when(k == nk - 1)
            def _():
                o_ref[...] = acc[...].astype(o_ref.dtype)

    if k_shards:
        b_spec = pl.BlockSpec((k_shards, tn, ks), lambda i, j, k: (0, j, 0))
    elif b_shards:
        b_spec = pl.BlockSpec((None, tk, tn), lambda i, j, k: (j, k, 0))
    elif nt:
        b_spec = pl.BlockSpec((tn, tk), lambda i, j, k: (j, k))
    else:
        b_spec = pl.BlockSpec((tk, tn), lambda i, j, k: (k, j))
    if out_shards:
        out_shape = jax.ShapeDtypeStruct((N // tn, M, tn), out_dtype)
        out_spec = pl.BlockSpec((None, tm, tn), lambda i, j, k: (j, i, 0))
    else:
        out_shape = jax.ShapeDtypeStruct((M, N), out_dtype)
        out_spec = pl.BlockSpec((tm, tn), lambda i, j, k: (i, j))
    grid = (M // tm, N // tn, nk)
    body, x_in, x_in_specs, x_out, x_out_specs, x_scr = _carry(ex, grid, 2, 1, body)
    res = pl.pallas_call(
        body,
        out_shape=(out_shape, *x_out),
        grid=grid,
        in_specs=[pl.BlockSpec((tm, tk), lambda i, j, k: (i, k)), b_spec] + x_in_specs,
        out_specs=(out_spec, *x_out_specs),
        scratch_shapes=([pltpu.VMEM((tm, tn), F32)] if nk > 1 else []) + x_scr,
        compiler_params=_params(("parallel", "parallel", "arbitrary") if ex is None else ("arbitrary",) * 3),
        name=name,
    )(a, b, *x_in)
    return res[0] if ex is None else res


def _rstd(x):
    return lax.rsqrt(jnp.mean(x * x, axis=-1, keepdims=True) + EPS)


def _rms_fwd(x, g, *, name):
    T, D = x.shape
    tm = _tile(T, 512)

    def body(x_ref, g_ref, o_ref, ot_ref):
        x = x_ref[...]
        y = x * _rstd(x) * g_ref[...]
        o_ref[...] = y.astype(o_ref.dtype)
        ot_ref[...] = y.T.astype(ot_ref.dtype)

    return pl.pallas_call(
        body, out_shape=(jax.ShapeDtypeStruct((T, D), BF16), jax.ShapeDtypeStruct((D, T), BF16)), grid=(T // tm,),
        in_specs=[pl.BlockSpec((tm, D), lambda i: (i, 0)), pl.BlockSpec((1, D), lambda i: (0, 0))],
        out_specs=(pl.BlockSpec((tm, D), lambda i: (i, 0)), pl.BlockSpec((D, tm), lambda i: (0, i))),
        compiler_params=_params(("parallel",)), name=name)(x, g)


def _post_res_norm(a, g_post, h, g_pre, *, name):
    T, D = a.shape
    tm = _tile(T, 384, BLOCK)

    def body(a_ref, gp_ref, h_ref, gn_ref, h1_ref, o_ref, ot_ref):
        a = a_ref[...]
        h1 = h_ref[...] + a * _rstd(a) * gp_ref[...]
        h1_ref[...] = h1
        y = h1 * _rstd(h1) * gn_ref[...]
        o_ref[...] = y.astype(o_ref.dtype)
        ot_ref[...] = y.T.astype(ot_ref.dtype)

    row = pl.BlockSpec((tm, D), lambda i: (i, 0))
    vec = pl.BlockSpec((1, D), lambda i: (0, 0))
    return pl.pallas_call(
        body, out_shape=(jax.ShapeDtypeStruct((T, D), F32), jax.ShapeDtypeStruct((T, D), BF16),
                         jax.ShapeDtypeStruct((D, T), BF16)), grid=(T // tm,),
        in_specs=[row, vec, row, vec], out_specs=(row, row, pl.BlockSpec((D, tm), lambda i: (0, i))),
        compiler_params=_params(("parallel",)), name=name)(a, g_post, h, g_pre)


def _loss_head(a, g, h, target, *, name):
    T, D = a.shape
    tm = _tile(T, 512)

    def body(a_ref, g_ref, h_ref, t_ref, dy_ref, da_ref, dg_ref, loss_ref):
        i = pl.program_id(0)
        a = a_ref[...]
        r = _rstd(a)
        ah = a * r
        y = h_ref[...] + ah * g_ref[...]
        rows = i * tm + lax.broadcasted_iota(jnp.int32, (tm, 1), 0)
        err = jnp.where(rows >= BLOCK, y - t_ref[...], 0.0)
        dy = err / D
        dy_ref[...] = dy
        dah = dy * g_ref[...]
        da_ref[...] = (r * (dah - ah * jnp.mean(dah * ah, axis=-1, keepdims=True))).astype(da_ref.dtype)
        part = jnp.sum(jnp.sum(err * err, axis=1, keepdims=True), axis=0, keepdims=True)

        @pl.when(i == 0)
        def _():
            loss_ref[...] = jnp.zeros_like(loss_ref)
            dg_ref[...] = jnp.zeros_like(dg_ref)

        loss_ref[...] += jnp.broadcast_to(part, loss_ref.shape)
        dg_ref[...] += jnp.sum(dy * ah, axis=0, keepdims=True)

    row = pl.BlockSpec((tm, D), lambda i: (i, 0))
    vec = pl.BlockSpec((1, D), lambda i: (0, 0))
    return pl.pallas_call(
        body, out_shape=(jax.ShapeDtypeStruct((T, D), F32), jax.ShapeDtypeStruct((T, D), BF16),
                         jax.ShapeDtypeStruct((1, D), F32), jax.ShapeDtypeStruct((8, 128), F32)),
        grid=(T // tm,),
        in_specs=[row, vec, row, row],
        out_specs=(row, row, vec, pl.BlockSpec((8, 128), lambda i: (0, 0))),
        compiler_params=_params(("arbitrary",)), name=name)(a, g, h, target)


def _rms_bwd(x, g, dy, res, *, out_dtype, then=None, ex=None, name):
    T, D = x.shape
    tm = _tile(T, 512)
    has_res = res is not None
    n_in = 3 + has_res + (2 if then is not None else 0)
    n_out = 2 + (2 if then is not None else 0)

    def pull_back(x, g, dy):
        r = _rstd(x)
        xh = x * r
        dxh = dy * g
        return r * (dxh - xh * jnp.mean(dxh * xh, axis=-1, keepdims=True)), jnp.sum(dy * xh, axis=0, keepdims=True)

    def body(*refs):
        ins, outs = refs[:n_in], refs[n_in:]
        i = pl.program_id(0)

        @pl.when(i == 0)
        def _():
            for ref in outs[1::2]:
                ref[...] = jnp.zeros_like(ref)

        dx, dg = pull_back(ins[0][...], ins[1][...], ins[2][...].astype(F32))
        if has_res:
            dx = dx + ins[3][...]
        outs[0][...] = dx.astype(outs[0].dtype)
        outs[1][...] += dg
        if then is not None:
            dx2, dg2 = pull_back(ins[n_in - 2][...], ins[n_in - 1][...], dx)
            outs[2][...] = dx2.astype(outs[2].dtype)
            outs[3][...] += dg2

    row = pl.BlockSpec((tm, D), lambda i: (i, 0))
    vec = pl.BlockSpec((1, D), lambda i: (0, 0))
    ins = [x, g, dy] + ([res] if has_res else []) + (list(then) if then is not None else [])
    in_specs = [row, vec, row] + ([row] if has_res else []) + ([row, vec] if then is not None else [])
    out_shape = [jax.ShapeDtypeStruct((T, D), out_dtype), jax.ShapeDtypeStruct((1, D), F32)]
    out_specs = [row, vec]
    if then is not None:
        out_shape += [jax.ShapeDtypeStruct((T, D), BF16), jax.ShapeDtypeStruct((1, D), F32)]
        out_specs += [row, vec]
    grid = (T // tm,)
    body, x_in, x_in_specs, x_out, x_out_specs, x_scr = _carry(ex, grid, n_in, n_out, body)
    return pl.pallas_call(
        body, out_shape=(*out_shape, *x_out), grid=grid,
        in_specs=in_specs + x_in_specs, out_specs=(*out_specs, *x_out_specs), scratch_shapes=x_scr,
        compiler_params=_params(("arbitrary",)), name=name)(*ins, *x_in)


def _gate_up_swiglu(a, w, *, name):
    T, D = a.shape
    S, n = w.shape[0] // 2, w.shape[2]
    tm = _tile(T, 384, BLOCK)

    def body(a_ref, wg_ref, wu_ref, g_ref, u_ref, o_ref, ot_ref):
        x = a_ref[...]
        g = jnp.dot(x, wg_ref[...], preferred_element_type=F32)
        u = jnp.dot(x, wu_ref[...], preferred_element_type=F32)
        g16, u16 = g.astype(BF16), u.astype(BF16)
        g_ref[...] = g16
        u_ref[...] = u16
        gr = g16.astype(F32)
        act = gr / (1.0 + jnp.exp(-gr)) * u16.astype(F32)
        o_ref[...] = act.astype(o_ref.dtype)
        ot_ref[...] = act.T.astype(ot_ref.dtype)

    tile = pl.BlockSpec((tm, n), lambda i, j: (i, j))
    shp = jax.ShapeDtypeStruct((T, S * n), BF16)
    return pl.pallas_call(
        body, out_shape=(shp, shp, shp, jax.ShapeDtypeStruct((S * n, T), BF16)), grid=(T // tm, S),
        in_specs=[pl.BlockSpec((tm, D), lambda i, j: (i, 0)),
                  pl.BlockSpec((None, D, n), lambda i, j: (j, 0, 0)),
                  pl.BlockSpec((None, D, n), lambda i, j: (j + S, 0, 0))],
        out_specs=(tile, tile, tile, pl.BlockSpec((n, tm), lambda i, j: (j, i))),
        compiler_params=_params(("parallel", "parallel")), name=name)(a, w, w)


def _d_act_swiglu(dff, w_down, gate, up, *, name):
    T, D = dff.shape
    F = w_down.shape[0]
    tm = _tile(T, 384)
    tf = _tile(F, 768, BLOCK)

    def body(d_ref, w_ref, g_ref, u_ref, o_ref):
        dy = d_ref[...]
        for c in range(0, F, tf):
            d = lax.dot_general(dy, w_ref[c:c + tf, :], NT, preferred_element_type=F32)
            g = g_ref[:, c:c + tf].astype(F32)
            u = u_ref[:, c:c + tf].astype(F32)
            sg = 1.0 / (1.0 + jnp.exp(-g))
            o_ref[:, c:c + tf] = (d * u * (sg * (1.0 + g * (1.0 - sg)))).astype(o_ref.dtype)
            o_ref[:, F + c:F + c + tf] = (d * (g * sg)).astype(o_ref.dtype)

    row = pl.BlockSpec((tm, F), lambda i: (i, 0))
    return pl.pallas_call(
        body, out_shape=jax.ShapeDtypeStruct((T, 2 * F), BF16), grid=(T // tm,),
        in_specs=[pl.BlockSpec((tm, D), lambda i: (i, 0)), pl.BlockSpec((F, D), lambda i: (0, 0)), row, row],
        out_specs=pl.BlockSpec((tm, 2 * F), lambda i: (i, 0)),
        compiler_params=_params(("parallel",)), name=name)(dff, w_down, gate, up)


def _fox_gates_fwd(f_t, b, *, name):
    H, T = f_t.shape
    nb = T // BLOCK

    def body(f_ref, b_ref, cum_ref, col_ref):
        f = f_ref[...] + b_ref[...]
        ls = jnp.minimum(f, 0.0) - jnp.log(1.0 + jnp.exp(-jnp.abs(f)))
        t = lax.broadcasted_iota(jnp.int32, (H, T), 1)
        ls = jnp.where(t >= PAD_ROWS, ls, 0.0)
        upper = (lax.broadcasted_iota(jnp.int32, (BLOCK, BLOCK), 0)
                 <= lax.broadcasted_iota(jnp.int32, (BLOCK, BLOCK), 1)).astype(F32)
        carry = jnp.zeros((H, 1), F32)
        for blk in range(nb):
            seg = ls[:, blk * BLOCK:(blk + 1) * BLOCK]
            pre = jnp.dot(seg, upper, precision=HIGHEST, preferred_element_type=F32) + carry
            cum_ref[:, blk * BLOCK:(blk + 1) * BLOCK] = pre
            col_ref[blk * BLOCK:(blk + 1) * BLOCK, :] = jnp.concatenate(
                [pre, jnp.zeros((BLOCK - H, BLOCK), F32)], axis=0).T
            carry = pre[:, BLOCK - 1:BLOCK]

    vm = pl.BlockSpec(memory_space=pltpu.VMEM)
    return pl.pallas_call(
        body, out_shape=(jax.ShapeDtypeStruct((H, T), F32), jax.ShapeDtypeStruct((T, BLOCK), F32)),
        in_specs=[vm, vm], out_specs=(vm, vm),
        compiler_params=_params(), name=name)(f_t, b)


def _fox_gates_bwd(dcq, dck, f_t, b, *, name):
    H, T = f_t.shape
    nb = T // BLOCK

    def body(dq_ref, d_ref, f_ref, b_ref, df_ref, db_ref):
        lower = (lax.broadcasted_iota(jnp.int32, (BLOCK, BLOCK), 0)
                 >= lax.broadcasted_iota(jnp.int32, (BLOCK, BLOCK), 1)).astype(F32)
        carry = jnp.zeros((H, 1), F32)
        for blk in range(nb - 1, -1, -1):
            seg = dq_ref[:, blk * BLOCK:(blk + 1) * BLOCK] - d_ref[:, blk * BLOCK:(blk + 1) * BLOCK]
            suf = jnp.dot(seg, lower, precision=HIGHEST, preferred_element_type=F32) + carry
            df_ref[:, blk * BLOCK:(blk + 1) * BLOCK] = suf
            carry = suf[:, 0:1]
        f = f_ref[...] + b_ref[...]
        t = lax.broadcasted_iota(jnp.int32, (H, T), 1)
        df = jnp.where(t >= PAD_ROWS, df_ref[...] / (1.0 + jnp.exp(f)), 0.0)
        df_ref[...] = df
        db_ref[...] = jnp.sum(df, axis=1, keepdims=True)

    vm = pl.BlockSpec(memory_space=pltpu.VMEM)
    return pl.pallas_call(
        body, out_shape=(jax.ShapeDtypeStruct((H, T), F32), jax.ShapeDtypeStruct((H, 1), F32)),
        in_specs=[vm, vm, vm, vm], out_specs=(vm, vm),
        compiler_params=_params(), name=name)(dcq, dck, f_t, b)


LANE_KC = HEAD_DIM
LANE_QC = HEAD_DIM + 3
LANE_END = HEAD_DIM + 6


def _split3(c):
    hi = c.astype(BF16).astype(F32)
    r = c - hi
    mid = r.astype(BF16).astype(F32)
    lo = (r - mid).astype(BF16).astype(F32)
    return hi, mid, lo


def _lanes(lane, data, start, terms, rest):
    out = rest
    for i, t in enumerate(terms):
        out = jnp.where(lane == start + i, t, out)
    return jnp.where(lane < HEAD_DIM, data, out)


def _fox_prep(proj, cum_col, *, name):
    T = proj.shape[0]
    tm = FOX_TILE
    nt = T // tm
    H = FOX_HEADS
    lanes = 2 * HEAD_DIM
    qb, kb, vb = 768 // lanes, 1280 // lanes, 1792 // lanes

    def body(q_ref, k_ref, v_ref, c_ref, qa_ref, ka_ref, va_ref):
        p = pl.program_id(0)
        i = pl.program_id(1)
        lane = lax.broadcasted_iota(jnp.int32, (tm, lanes), 1)
        rows = i * tm + lax.broadcasted_iota(jnp.int32, (tm, 1), 0)
        q2 = q_ref[...].astype(F32)
        k2 = k_ref[...].astype(F32)
        v2 = v_ref[...].astype(F32)
        cum = c_ref[...]
        for e in range(2):
            c = jnp.sum(jnp.where(lane == 2 * p + e, cum, 0.0), axis=1, keepdims=True)
            ck = jnp.where(rows >= PAD_ROWS, c, -NEG)
            qe, ke, ve = (q2, k2, v2) if e == 0 else tuple(pltpu.roll(a, HEAD_DIM, 1) for a in (q2, k2, v2))
            one = jnp.where(lane < LANE_END, 1.0, 0.0)
            qa = _lanes(lane, qe * SCALE, LANE_QC, _split3(c), jnp.where(lane < LANE_QC, -1.0, 0.0))
            ka = _lanes(lane, ke, LANE_KC, _split3(ck), one)
            va = jnp.where(lane < HEAD_DIM, ve, jnp.where(lane < LANE_QC, 1.0, 0.0))
            qa_ref[e] = qa.astype(BF16)
            ka_ref[e] = ka.astype(BF16)
            va_ref[e] = va.astype(BF16)

    def col(b):
        return pl.BlockSpec((tm, lanes), lambda p, i, b=b: (i, b + p))

    out = pl.BlockSpec((2, tm, lanes), lambda p, i: (p, i, 0))
    shp = jax.ShapeDtypeStruct((H, T, lanes), BF16)
    return pl.pallas_call(
        body, out_shape=(shp, shp, shp), grid=(H // 2, nt),
        in_specs=[col(qb), col(kb), col(vb), pl.BlockSpec((tm, lanes), lambda p, i: (i, 0))],
        out_specs=(out, out, out),
        compiler_params=_params(("parallel", "parallel")), name=name)(proj, proj, proj, cum_col)


def _fox_fwd(q_aug, k_aug, v_aug, mix, *, ex=None, name):
    H, T, lanes = q_aug.shape
    tq = FOX_TILE
    nq = T // tq
    G = FOX_GROUP

    def body(q_ref, k_ref, v_ref, mix_ref, o_ref, lse_ref, m_scr, acc_scr):
        i = pl.program_id(1)
        m_scr[...] = jnp.full(m_scr.shape, NEG, F32)
        acc_scr[...] = jnp.zeros(acc_scr.shape, F32)

        def step(kb, diag):
            off = pl.multiple_of(kb * tq, tq)
            s_t = [lax.dot_general(k_ref[g, pl.ds(off, tq), :], q_ref[g], NT, preferred_element_type=F32)
                   for g in range(G)]
            if diag:
                r = lax.broadcasted_iota(jnp.int32, (tq, tq), 0)
                c = lax.broadcasted_iota(jnp.int32, (tq, tq), 1)
                s_t = [jnp.where(c >= r, s, NEG) for s in s_t]
            m_prev = [m_scr[g] for g in range(G)]
            m_new = [jnp.maximum(m_prev[g], jnp.max(s_t[g], axis=0, keepdims=True)) for g in range(G)]
            p_t = [jnp.exp(s_t[g] - m_new[g]).astype(BF16) for g in range(G)]
            pv = [lax.dot_general(v_ref[g, pl.ds(off, tq), :], p_t[g], TN, preferred_element_type=F32)
                  for g in range(G)]
            for g in range(G):
                acc_scr[g] = jnp.exp(m_prev[g] - m_new[g]) * acc_scr[g] + pv[g]
                m_scr[g] = m_new[g]

        def loop_body(kb, carry):
            step(kb, False)
            return carry

        lax.fori_loop(0, i, loop_body, 0)
        step(i, True)
        lane = lax.broadcasted_iota(jnp.int32, (tq, lanes), 1)
        outs = []
        for g in range(G):
            acc = acc_scr[g]
            lse_ref[g] = m_scr[g] + jnp.log(acc[HEAD_DIM:HEAD_DIM + 1, :])
            acc_t = acc.T
            outs.append(acc_t / acc_t[:, HEAD_DIM:HEAD_DIM + 1])
        for pair in range(G // 2):
            o_ref[:, pair * lanes:(pair + 1) * lanes] = jnp.where(
                lane < HEAD_DIM, outs[2 * pair], pltpu.roll(outs[2 * pair + 1], HEAD_DIM, 1)).astype(o_ref.dtype)

    blk = pl.BlockSpec((G, tq, lanes), lambda h, i: (h, i, 0))
    full = pl.BlockSpec((G, T, lanes), lambda h, i: (h, 0, 0))
    grid = (H // G, nq)
    first = mix.shape[1] // (G * HEAD_DIM) - H // G
    body, x_in, x_in_specs, x_out, x_out_specs, x_scr = _carry(ex, grid, 4, 2, body)
    return pl.pallas_call(
        body,
        out_shape=(jax.ShapeDtypeStruct(mix.shape, mix.dtype), jax.ShapeDtypeStruct((H, nq, 1, tq), F32), *x_out),
        grid=grid,
        in_specs=[blk, full, full, pl.BlockSpec(memory_space=pl.ANY)] + x_in_specs,
        out_specs=(pl.BlockSpec((tq, G * HEAD_DIM), lambda h, i: (i, first + h)),
                   pl.BlockSpec((G, None, 1, tq), lambda h, i: (h, i, 0, 0)), *x_out_specs),
        input_output_aliases={3: 0},
        scratch_shapes=[pltpu.VMEM((G, 1, tq), F32), pltpu.VMEM((G, lanes, tq), F32)] + x_scr,
        compiler_params=_params(("arbitrary", "arbitrary")), name=name)(q_aug, k_aug, v_aug, mix, *x_in)


def _fox_prep_bwd(dmix, mix, *, name):
    T = dmix.shape[0]
    H = FOX_HEADS
    tm = FOX_TILE
    lanes = 2 * HEAD_DIM
    first = mix.shape[1] // lanes - H // 2

    def body(d_ref, o_ref, da_ref):
        lane = lax.broadcasted_iota(jnp.int32, (tm, lanes), 1)
        d2 = d_ref[...].astype(F32)
        prod = d2 * o_ref[...].astype(F32)
        for e in range(2):
            de = d2 if e == 0 else pltpu.roll(d2, HEAD_DIM, 1)
            delta = jnp.sum(jnp.where(lane // HEAD_DIM == e, prod, 0.0), axis=1, keepdims=True)
            da_ref[e] = _lanes(lane, de, LANE_KC, _split3(-delta), jnp.zeros((), F32)).astype(BF16)

    pair = pl.BlockSpec((tm, lanes), lambda p, i: (i, first + p))
    return pl.pallas_call(
        body, out_shape=jax.ShapeDtypeStruct((H, T, lanes), BF16), grid=(H // 2, T // tm),
        in_specs=[pair, pair],
        out_specs=pl.BlockSpec((2, tm, lanes), lambda p, i: (p, i, 0)),
        compiler_params=_params(("parallel", "parallel")), name=name)(dmix, mix)


def _fox_bwd(q_aug, k_aug, v_aug, do_aug, lse_row, *, ex=None, name):
    H, T, lanes = q_aug.shape
    tq = FOX_TILE
    nq = T // tq
    G = FOX_GROUP

    def side_by_side(tiles, scale=None):
        lane = lax.broadcasted_iota(jnp.int32, tiles[0].shape, 1)
        out = [jnp.where(lane < HEAD_DIM, tiles[2 * p], pltpu.roll(tiles[2 * p + 1], HEAD_DIM, 1))
               for p in range(G // 2)]
        out = jnp.concatenate(out, axis=1)
        return out if scale is None else out * scale

    def body(q_ref, k_ref, v_ref, do_ref, lse_ref, dqb_ref, dkb_ref, dvb_ref, dq_ref, dck_ref, dk_acc, dv_acc):
        j = pl.program_id(1)

        @pl.when(j == 0)
        def _():
            dq_ref[...] = jnp.zeros(dq_ref.shape, F32)

        dk_acc[...] = jnp.zeros(dk_acc.shape, F32)
        dv_acc[...] = jnp.zeros(dv_acc.shape, F32)

        def step(qb, diag):
            off = pl.multiple_of(qb * tq, tq)
            heads = range(G)
            qa = [q_ref[g, pl.ds(off, tq), :] for g in heads]
            da = [do_ref[g, pl.ds(off, tq), :] for g in heads]
            s_t = [lax.dot_general(k_ref[g], qa[g], NT, preferred_element_type=F32) for g in heads]
            dp_t = [lax.dot_general(v_ref[g], da[g], NT, preferred_element_type=F32) for g in heads]
            p_t = [jnp.exp(s_t[g] - lse_ref[g, qb]) for g in heads]
            if diag:
                r = lax.broadcasted_iota(jnp.int32, (tq, tq), 0)
                c = lax.broadcasted_iota(jnp.int32, (tq, tq), 1)
                p_t = [jnp.where(c >= r, p, 0.0) for p in p_t]
            dsb = [(p_t[g] * dp_t[g]).astype(BF16) for g in heads]
            dv = [jnp.dot(p_t[g].astype(BF16), da[g], preferred_element_type=F32) for g in heads]
            dk = [jnp.dot(dsb[g], qa[g], preferred_element_type=F32) for g in heads]
            dq = [jnp.dot(dsb[g].T, k_ref[g], preferred_element_type=F32) for g in heads]
            for g in heads:
                dv_acc[g] += dv[g]
                dk_acc[g] += dk[g]
                dq_ref[g, pl.ds(off, tq), :] += dq[g]

        step(j, True)

        def loop_body(qb, carry):
            step(qb, False)
            return carry

        lax.fori_loop(j + 1, nq, loop_body, 0)
        dk = [dk_acc[g] for g in range(G)]
        dkb_ref[...] = side_by_side(dk).astype(dkb_ref.dtype)
        dvb_ref[...] = side_by_side([dv_acc[g] for g in range(G)]).astype(dvb_ref.dtype)
        for g in range(G):
            dck_ref[g] = -dk[g][:, LANE_KC:LANE_KC + 1]

        @pl.when(j == nq - 1)
        def _():
            for qb in range(nq):
                rows = pl.ds(qb * tq, tq)
                dqb_ref[rows, :] = side_by_side([dq_ref[g, rows, :] for g in range(G)], SCALE).astype(dqb_ref.dtype)

    blk = pl.BlockSpec((G, tq, lanes), lambda h, j: (h, j, 0))
    full = pl.BlockSpec((G, T, lanes), lambda h, j: (h, 0, 0))
    wide = G * HEAD_DIM
    grid = (H // G, nq)
    body, x_in, x_in_specs, x_out, x_out_specs, x_scr = _carry(ex, grid, 5, 5, body)
    flat = jax.ShapeDtypeStruct((T, H * HEAD_DIM), BF16)
    return pl.pallas_call(
        body,
        out_shape=(flat, flat, flat, jax.ShapeDtypeStruct((H, T, lanes), F32),
                   jax.ShapeDtypeStruct((H, T, 1), F32), *x_out),
        grid=grid,
        in_specs=[full, blk, blk, full, pl.BlockSpec((G, nq, 1, tq), lambda h, j: (h, 0, 0, 0))] + x_in_specs,
        out_specs=(pl.BlockSpec((T, wide), lambda h, j: (0, h)), pl.BlockSpec((tq, wide), lambda h, j: (j, h)),
                   pl.BlockSpec((tq, wide), lambda h, j: (j, h)), full,
                   pl.BlockSpec((G, tq, 1), lambda h, j: (h, j, 0)), *x_out_specs),
        scratch_shapes=[pltpu.VMEM((G, tq, lanes), F32), pltpu.VMEM((G, tq, lanes), F32)] + x_scr,
        compiler_params=_params(("arbitrary", "arbitrary")), name=name,
    )(q_aug, k_aug, v_aug, do_aug, lse_row, *x_in)


def _t5_bucket_np(d):
    n = np.maximum(d, 0).astype(np.int32)
    max_exact = N_BUCKETS // 2
    nf = np.maximum(n, 1).astype(np.float32)
    large = max_exact + (np.log(nf / max_exact) / math.log(MAX_DISTANCE / max_exact)
                         * (N_BUCKETS - max_exact)).astype(np.int32)
    large = np.minimum(large, N_BUCKETS - 1)
    return np.where(n < max_exact, n, large)


def _bucket_onehots():
    k = np.arange(BLOCK)[:, None]
    q = np.arange(BLOCK)[None, :]
    eye = np.eye(N_BUCKETS, dtype=np.float32)
    cur = eye[_t5_bucket_np(q - k).reshape(-1)]
    prev = eye[_t5_bucket_np(BLOCK + q - k).reshape(-1)]
    return cur, prev


SWA_K_COL = SWA_Q_HEADS * HEAD_DIM // (2 * HEAD_DIM)
SWA_V_COL = SWA_K_COL + 1


def _swa_terms(raw, bc, bp, far, sink, n):
    k = lax.broadcasted_iota(jnp.int32, (BLOCK, BLOCK), 0)
    q = lax.broadcasted_iota(jnp.int32, (BLOCK, BLOCK), 1)
    never = 2 * BLOCK
    s_c = raw[0] + bc
    s_p = raw[1] + bp
    s_m = raw[2] + jnp.where(n == 1, bp, far)
    s_c = jnp.where((k <= q) & (k >= jnp.where(n >= 1, 0, PAD_ROWS)), s_c, NEG)
    s_p = jnp.where(k > q + jnp.where(n >= 2, 0, never), s_p, NEG)
    s_m = jnp.where(k >= jnp.where(n >= 1, PAD_ROWS, never), s_m, NEG)
    m = jnp.maximum(jnp.maximum(jnp.max(s_c, axis=0, keepdims=True), jnp.max(s_p, axis=0, keepdims=True)),
                    jnp.maximum(jnp.max(s_m, axis=0, keepdims=True), sink))
    e = [jnp.exp(s_c - m), jnp.exp(s_p - m), jnp.exp(s_m - m)]
    e_s = jnp.exp(sink - m)
    l = (jnp.sum(e[0], axis=0, keepdims=True) + jnp.sum(e[1], axis=0, keepdims=True)
         + jnp.sum(e[2], axis=0, keepdims=True) + e_s)
    return e, e_s, l


def _swa_specs():
    G = SWA_GROUP
    width = G * HEAD_DIM

    def rows(which, col):
        if which == "cur":
            return pl.BlockSpec((BLOCK, BLOCK), lambda kv, n: (n, col))
        if which == "prev":
            return pl.BlockSpec((BLOCK, BLOCK), lambda kv, n: (jnp.maximum(n - 1, 0), col))
        return pl.BlockSpec((BLOCK, BLOCK), lambda kv, n: (0, col))

    qblk = pl.BlockSpec((BLOCK, width), lambda kv, n: (n, kv))
    keys = [rows(w, SWA_K_COL) for w in ("cur", "prev", "meta")]
    vals = [rows(w, SWA_V_COL) for w in ("cur", "prev", "meta")]
    bias = pl.BlockSpec((G, BLOCK, BLOCK), lambda kv, n: (kv, 0, 0))
    smem = pl.BlockSpec(memory_space=pltpu.SMEM)
    return qblk, keys, vals, bias, smem


def _swa_own_kv(tile_ref, kv):
    lane = lax.broadcasted_iota(jnp.int32, (BLOCK, 2 * HEAD_DIM), 1)
    t = tile_ref[...].astype(F32)
    return jnp.where(lane // HEAD_DIM == kv, t, pltpu.roll(t, HEAD_DIM, 1)).astype(BF16)


def _swa_fwd(proj, bc, bp, far, sinks, *, name):
    T = proj.shape[0]
    nb = T // BLOCK
    G = SWA_GROUP
    lanes = 2 * HEAD_DIM

    def body(q_ref, kc_ref, kp_ref, km_ref, vc_ref, vp_ref, vm_ref, bc_ref, bp_ref, far_ref, sink_ref, o_ref):
        kv = pl.program_id(0)
        n = pl.program_id(1)
        lane = lax.broadcasted_iota(jnp.int32, (BLOCK, lanes), 1)
        kk = [_swa_own_kv(r, kv) for r in (kc_ref, kp_ref, km_ref)]
        vv = [_swa_own_kv(r, kv) for r in (vc_ref, vp_ref, vm_ref)]
        heads, blocks = range(G), range(3)
        q2 = [q_ref[:, pair * lanes:(pair + 1) * lanes].astype(F32) * SCALE for pair in range(G // 2)]
        qm = [jnp.where(lane // HEAD_DIM == g % 2, q2[g // 2], 0.0).astype(BF16) for g in heads]
        raw = [[lax.dot_general(kk[b], qm[g], NT, preferred_element_type=F32) for b in blocks] for g in heads]
        terms = [_swa_terms(raw[g], bc_ref[g], bp_ref[g], far_ref[kv * G + g], sink_ref[kv * G + g], n) for g in heads]
        o_t = [sum(lax.dot_general(vv[b], terms[g][0][b].astype(BF16), TN, preferred_element_type=F32) for b in blocks)
               for g in heads]
        outs = [(o_t[g] / terms[g][2]).T for g in heads]
        for pair in range(G // 2):
            o_ref[:, pair * lanes:(pair + 1) * lanes] = jnp.where(
                lane < HEAD_DIM, outs[2 * pair], outs[2 * pair + 1]).astype(o_ref.dtype)

    qblk, keys, vals, bias, smem = _swa_specs()
    return pl.pallas_call(
        body, out_shape=jax.ShapeDtypeStruct((T, D_MODEL), BF16), grid=(SWA_KV_HEADS, nb),
        in_specs=[qblk] + keys + vals + [bias, bias, smem, smem],
        out_specs=qblk,
        compiler_params=_params(("parallel", "parallel")), name=name,
    )(proj, proj, proj, proj, proj, proj, proj, bc, bp, far, sinks)


def _swa_bwd(proj, dmix, bc, bp, far, sinks, *, ex=None, name):
    T = proj.shape[0]
    nb = T // BLOCK
    G = SWA_GROUP
    Hq = SWA_Q_HEADS
    lanes = 2 * HEAD_DIM

    def body(q_ref, kc_ref, kp_ref, km_ref, vc_ref, vp_ref, vm_ref, do_ref, bc_ref, bp_ref, far_ref, sink_ref,
             dq_ref, dk_ref, dv_ref, dbc_ref, dbp_ref, dbf_ref, dsk_ref):
        kv = pl.program_id(0)
        n = pl.program_id(1)

        @pl.when(n == 0)
        def _():
            for ref in (dk_ref, dv_ref, dbc_ref, dbp_ref, dbf_ref, dsk_ref):
                ref[...] = jnp.zeros(ref.shape, F32)

        lane = lax.broadcasted_iota(jnp.int32, (BLOCK, lanes), 1)
        kk = [_swa_own_kv(r, kv) for r in (kc_ref, kp_ref, km_ref)]
        vv = [_swa_own_kv(r, kv) for r in (vc_ref, vp_ref, vm_ref)]
        heads, blocks = range(G), range(3)
        q2 = [q_ref[:, pair * lanes:(pair + 1) * lanes].astype(F32) * SCALE for pair in range(G // 2)]
        d2 = [do_ref[:, pair * lanes:(pair + 1) * lanes] for pair in range(G // 2)]
        own = [lane // HEAD_DIM == g % 2 for g in heads]
        qm = [jnp.where(own[g], q2[g // 2], 0.0).astype(BF16) for g in heads]
        dom = [jnp.where(own[g], d2[g // 2], jnp.zeros_like(d2[0])) for g in heads]
        raw = [[lax.dot_general(kk[b], qm[g], NT, preferred_element_type=F32) for b in blocks] for g in heads]
        dp = [[lax.dot_general(vv[b], dom[g], NT, preferred_element_type=F32) for b in blocks] for g in heads]
        p, ds16 = [], []
        for g in heads:
            e, e_s, l = _swa_terms(raw[g], bc_ref[g], bp_ref[g], far_ref[kv * G + g], sink_ref[kv * G + g], n)
            inv = 1.0 / l
            pg = [e[b] * inv for b in blocks]
            delta = sum(jnp.sum(pg[b] * dp[g][b], axis=0, keepdims=True) for b in blocks)
            ds = [pg[b] * (dp[g][b] - delta) for b in blocks]
            dsk_ref[g] += -(e_s * inv) * delta
            dbc_ref[g] += ds[0]
            dbp_ref[g] += ds[1] + jnp.where(n == 1, ds[2], 0.0)
            dbf_ref[g] += jnp.where(n >= 2, ds[2], 0.0)
            p.append([x.astype(BF16) for x in pg])
            ds16.append([x.astype(BF16) for x in ds])
        dq_t = [sum(lax.dot_general(kk[b], ds16[g][b], TN, preferred_element_type=F32) for b in blocks) for g in heads]
        dk = [sum(jnp.dot(ds16[g][b], qm[g], preferred_element_type=F32) for g in heads) for b in blocks]
        dv = [sum(jnp.dot(p[g][b], dom[g], preferred_element_type=F32) for g in heads) for b in blocks]
        dqs = [dq_t[g].T * SCALE for g in heads]
        for pair in range(G // 2):
            dq_ref[:, pair * lanes:(pair + 1) * lanes] = jnp.where(
                lane < HEAD_DIM, dqs[2 * pair], dqs[2 * pair + 1]).astype(dq_ref.dtype)
        cur_off = pl.multiple_of(n * BLOCK, BLOCK)
        prev_off = pl.multiple_of(jnp.maximum(n - 1, 0) * BLOCK, BLOCK)
        for acc, ref in ((dk, dk_ref), (dv, dv_ref)):
            tot = [a + pltpu.roll(a, HEAD_DIM, 1) for a in acc]
            ref[pl.ds(cur_off, BLOCK), :] += tot[0]
            ref[pl.ds(prev_off, BLOCK), :] += tot[1]
            ref[0:BLOCK, :] += tot[2]

    qblk, keys, vals, bias, smem = _swa_specs()
    kvfull = pl.BlockSpec((None, T, lanes), lambda kv, n: (kv, 0, 0))
    dsk = pl.BlockSpec((G, 1, BLOCK), lambda kv, n: (kv, 0, 0))
    grid = (SWA_KV_HEADS, nb)
    body, x_in, x_in_specs, x_out, x_out_specs, x_scr = _carry(ex, grid, 12, 7, body)
    tile = jax.ShapeDtypeStruct((Hq, BLOCK, BLOCK), F32)
    return pl.pallas_call(
        body,
        out_shape=(jax.ShapeDtypeStruct((T, Hq * HEAD_DIM), BF16),
                   jax.ShapeDtypeStruct((SWA_KV_HEADS, T, lanes), F32),
                   jax.ShapeDtypeStruct((SWA_KV_HEADS, T, lanes), F32),
                   tile, tile, tile, jax.ShapeDtypeStruct((Hq, 1, BLOCK), F32), *x_out),
        grid=grid,
        in_specs=[qblk] + keys + vals + [qblk, bias, bias, smem, smem] + x_in_specs,
        out_specs=(qblk, kvfull, kvfull, bias, bias, bias, dsk, *x_out_specs),
        scratch_shapes=x_scr,
        compiler_params=_params(("arbitrary", "arbitrary")), name=name,
    )(proj, proj, proj, proj, proj, proj, proj, dmix, bc, bp, far, sinks, *x_in)


def _small_grads(dbc, dbp, dbf, dsk, oh_cur, oh_prev, *, name):
    Hq = dbc.shape[0]

    def body(dbc_ref, dbp_ref, dbf_ref, dsk_ref, oc_ref, op_ref, tab_ref, sink_ref):
        tab = (jnp.dot(dbc_ref[...], oc_ref[...], precision=HIGHEST, preferred_element_type=F32)
               + jnp.dot(dbp_ref[...], op_ref[...], precision=HIGHEST, preferred_element_type=F32))
        far = jnp.sum(dbf_ref[...], axis=1, keepdims=True)
        last = lax.broadcasted_iota(jnp.int32, (Hq, N_BUCKETS), 1) == N_BUCKETS - 1
        tab_ref[...] = tab + jnp.where(last, far, 0.0)
        sink_ref[...] = jnp.sum(dsk_ref[...], axis=1, keepdims=True)

    vm = pl.BlockSpec(memory_space=pltpu.VMEM)
    return pl.pallas_call(
        body, out_shape=(jax.ShapeDtypeStruct((Hq, N_BUCKETS), F32), jax.ShapeDtypeStruct((Hq, 1), F32)),
        in_specs=[vm] * 6, out_specs=(vm, vm), compiler_params=_params(), name=name,
    )(dbc.reshape(Hq, -1), dbp.reshape(Hq, -1), dbf.reshape(Hq, -1), dsk.reshape(Hq, -1), oh_cur, oh_prev)


def _coords():
    return lax.axis_index("x"), lax.axis_index("y"), lax.axis_index("c")


class _Exchange:
    def __init__(self, inputs, out_shapes, scratch, start, finish):
        self.inputs, self.out_shapes, self.scratch, self.start, self.finish = inputs, out_shapes, scratch, start, finish


def _carry(ex, grid, n_in, n_out, body):
    if ex is None:
        return body, [], [], [], [], []
    ni, no = len(ex.inputs), len(ex.out_shapes)

    def at_step(which):
        cond = None
        for axis, n in enumerate(grid):
            c = pl.program_id(axis) == (0 if which == "first" else n - 1)
            cond = c if cond is None else cond & c
        return cond

    def wrapped(*refs):
        refs = list(refs)
        n_own_scr = len(refs) - (n_in + ni + n_out + no) - len(ex.scratch)
        own_in, side_in = refs[:n_in], refs[n_in:n_in + ni]
        own_out = refs[n_in + ni:n_in + ni + n_out]
        side_out = refs[n_in + ni + n_out:n_in + ni + n_out + no]
        rest = refs[n_in + ni + n_out + no:]
        own_scr, sems = rest[:n_own_scr], rest[n_own_scr:]

        @pl.when(at_step("first"))
        def _():
            ex.start(side_in, side_out, sems)

        body(*own_in, *own_out, *own_scr)

        @pl.when(at_step("last"))
        def _():
            ex.finish(side_in, side_out, sems)

    hbm = pl.BlockSpec(memory_space=pl.ANY)
    return wrapped, list(ex.inputs), [hbm] * ni, list(ex.out_shapes), [hbm] * no, list(ex.scratch)


def _run_exchange(ex, *, name):
    ni, no = len(ex.inputs), len(ex.out_shapes)

    def body(*refs):
        ins, outs, sems = refs[:ni], refs[ni:ni + no], refs[ni + no:]
        ex.start(ins, outs, sems)
        ex.finish(ins, outs, sems)

    hbm = pl.BlockSpec(memory_space=pl.ANY)
    return pl.pallas_call(
        body, out_shape=tuple(ex.out_shapes), in_specs=[hbm] * ni, out_specs=tuple([hbm] * no),
        scratch_shapes=ex.scratch, compiler_params=_params(), name=name)(*ex.inputs)


def _gather_exchange(shards):
    nt = len(shards)

    def copies(ins, outs, sems):
        send_sems, recv_sems, local_sems = sems
        x, y, c = _coords()
        me, sibling = (x, y, c), (x, y, 1 - c)
        chips = [(1 - x, y), (x, 1 - y), (1 - x, 1 - y)]

        def slot(t, dev):
            return outs[t].at[4 * dev[0] + 2 * dev[1] + dev[2]]

        def copy(t, k, block, to, src=None):
            dst = slot(t, block)
            return pltpu.make_async_remote_copy(
                src_ref=dst if src is None else src, dst_ref=dst,
                send_sem=send_sems.at[t, k], recv_sem=recv_sems.at[t, k], device_id=to, device_id_type=MESH)

        mine = [pltpu.make_async_copy(ins[t], slot(t, me), local_sems.at[t]) for t in range(nt)]
        first = []
        for t in range(nt):
            first.append(copy(t, 0, me, sibling, src=ins[t]))
            first += [copy(t, 1 + j, me, (*chip, c), src=ins[t]) for j, chip in enumerate(chips)]
        return copy, mine, first, me, sibling, chips, c

    def start(ins, outs, sems):
        _, mine, first, *_ = copies(ins, outs, sems)
        for cp in mine + first:
            cp.start()

    def finish(ins, outs, sems):
        copy, mine, first, me, sibling, chips, c = copies(ins, outs, sems)
        passed = []
        for j, chip in enumerate(chips):
            for t in range(nt):
                copy(t, 1 + j, (*chip, c), me).wait_recv()
                cp = copy(t, 4 + j, (*chip, c), sibling)
                cp.start()
                passed.append(cp)
        for t in range(nt):
            copy(t, 0, sibling, me).wait_recv()
            for j, chip in enumerate(chips):
                copy(t, 4 + j, (*chip, 1 - c), me).wait_recv()
        for cp in first + passed:
            cp.wait_send()
        for cp in mine:
            cp.wait()

    return _Exchange(
        list(shards), [jax.ShapeDtypeStruct((N_DEV,) + s.shape, s.dtype) for s in shards],
        [pltpu.SemaphoreType.DMA((nt, 7)), pltpu.SemaphoreType.DMA((nt, 7)), pltpu.SemaphoreType.DMA((nt,))],
        start, finish)


def _swap_exchange(arrays, n_slices, copies):
    nt = len(arrays)

    def start(ins, outs, sems):
        for cp in copies(ins, outs, sems):
            cp.start()

    def finish(ins, outs, sems):
        sends = copies(ins, outs, sems)
        for cp in sends:
            cp.wait_recv()
        for cp in sends:
            cp.wait_send()

    return _Exchange(
        list(arrays), [jax.ShapeDtypeStruct((n_slices,) + a.shape[1:], a.dtype) for a in arrays],
        [pltpu.SemaphoreType.DMA((nt, n_slices)), pltpu.SemaphoreType.DMA((nt, n_slices))], start, finish)


def _cores_exchange(gs):
    def copies(ins, outs, sems):
        send_sems, recv_sems = sems
        x, y, c = _coords()
        return [pltpu.make_async_remote_copy(
            src_ref=ins[t].at[2 * j + (1 - c)], dst_ref=outs[t].at[j],
            send_sem=send_sems.at[t, j], recv_sem=recv_sems.at[t, j], device_id=(x, y, 1 - c), device_id_type=MESH)
            for t in range(len(gs)) for j in range(4)]

    return _swap_exchange(gs, 4, copies)


def _chips_exchange(ps):
    def copies(ins, outs, sems):
        send_sems, recv_sems = sems
        x, y, c = _coords()
        peers = [(1 - x, y), (x, 1 - y), (1 - x, 1 - y)]
        return [pltpu.make_async_remote_copy(
            src_ref=ins[t].at[2 * px + py], dst_ref=outs[t].at[k],
            send_sem=send_sems.at[t, k], recv_sem=recv_sems.at[t, k], device_id=(px, py, c), device_id_type=MESH)
            for t in range(len(ps)) for k, (px, py) in enumerate(peers)]

    return _swap_exchange(ps, 3, copies)


def _add_cores(g, r, core, *, name):
    _, A, B = g.shape
    ta = _tile(A, 512, 16)

    def body(core_ref, a_ref, b_ref, o_ref, o16_ref):
        s = a_ref[...] + b_ref[...]
        o_ref[...] = s
        o16_ref[...] = s.astype(BF16)

    blk = (None, ta, B)
    out = pl.BlockSpec(blk, lambda j, i, core_ref: (j, i, 0))
    return pl.pallas_call(
        body, out_shape=(jax.ShapeDtypeStruct((4, A, B), F32), jax.ShapeDtypeStruct((4, A, B), BF16)),
        grid_spec=pltpu.PrefetchScalarGridSpec(
            num_scalar_prefetch=1, grid=(4, A // ta),
            in_specs=[pl.BlockSpec(blk, lambda j, i, core_ref: (2 * j + core_ref[0], i, 0)),
                      pl.BlockSpec(blk, lambda j, i, core_ref: (j, i, 0))],
            out_specs=(out, out)),
        compiler_params=_params(("parallel", "parallel")), name=name)(core, g, r)


def _adamw_math(w, g, m, v):
    m = ADAM_B1 * m + (1.0 - ADAM_B1) * g
    v = ADAM_B2 * v + (1.0 - ADAM_B2) * (g * g)
    m_hat = m / (1.0 - ADAM_B1 ** ADAM_STEP)
    v_hat = v / (1.0 - ADAM_B2 ** ADAM_STEP)
    delta = -ADAM_LR * (m_hat / (jnp.sqrt(v_hat) + ADAM_EPS) + ADAM_WD * w)
    return delta, m, v


def _sum_adamw(p, r, chip, w, m, v, *, segs, ta, name):
    Aw, Bw = w.shape
    Bg = p.shape[2]
    assert Aw % ta == 0

    def body(chip_ref, p_ref, r0, r1, r2, w_ref, m_ref, v_ref, g_out, d_out, m_out, v_out):
        for gc, wc, n in segs:
            g = ((p_ref[:, gc:gc + n] + r0[:, gc:gc + n].astype(F32)) + r1[:, gc:gc + n].astype(F32)
                 ) + r2[:, gc:gc + n].astype(F32)
            delta, m_new, v_new = _adamw_math(w_ref[:, wc:wc + n], g, m_ref[:, wc:wc + n], v_ref[:, wc:wc + n])
            g_out[:, wc:wc + n] = g
            d_out[:, wc:wc + n] = delta
            m_out[:, wc:wc + n] = m_new
            v_out[:, wc:wc + n] = v_new

    gblk = (None, ta, Bg)
    row = pl.BlockSpec((ta, Bw), lambda i, chip_ref: (i, 0))
    rspecs = [pl.BlockSpec(gblk, (lambda i, chip_ref, k=k: (k, i, 0))) for k in range(3)]
    shp = jax.ShapeDtypeStruct((Aw, Bw), F32)
    return pl.pallas_call(
        body, out_shape=(shp, shp, shp, shp),
        grid_spec=pltpu.PrefetchScalarGridSpec(
            num_scalar_prefetch=1, grid=(Aw // ta,),
            in_specs=[pl.BlockSpec(gblk, lambda i, chip_ref: (chip_ref[0], i, 0))] + rspecs + [row, row, row],
            out_specs=(row, row, row, row)),
        compiler_params=_params(("parallel",)), name=name)(chip, p, r, r, r, w, m, v)


def _adamw(w, g, m, v, *, name):
    def body(w_ref, g_ref, m_ref, v_ref, d_out, m_out, v_out):
        delta, m_new, v_new = _adamw_math(w_ref[...], g_ref[...], m_ref[...], v_ref[...])
        d_out[...] = delta
        m_out[...] = m_new
        v_out[...] = v_new

    vm = pl.BlockSpec(memory_space=pltpu.VMEM)
    shp = jax.ShapeDtypeStruct(w.shape, F32)
    return pl.pallas_call(body, out_shape=(shp, shp, shp), in_specs=[vm] * 4, out_specs=(vm, vm, vm),
                          compiler_params=_params(), name=name)(w, g, m, v)


def _small_allreduce_adamw(s, w, m, v, *, name):
    R, W = s.shape

    def body(s_ref, w_ref, m_ref, v_ref, g_out, d_out, m_out, v_out, gath, send_sems, recv_sems):
        x, y, c = _coords()
        mine = 4 * x + 2 * y + c
        gath[mine] = s_ref[...]
        peers = [((1 - x) if k & 4 else x, (1 - y) if k & 2 else y, (1 - c) if k & 1 else c) for k in range(1, N_DEV)]
        sends = []
        for k in range(1, N_DEV):
            peer = peers[k - 1]
            sends.append(pltpu.make_async_remote_copy(
                src_ref=s_ref, dst_ref=gath.at[mine], send_sem=send_sems.at[k - 1], recv_sem=recv_sems.at[k - 1],
                device_id=peer, device_id_type=MESH))
        for cp in sends:
            cp.start()
        for k in range(1, N_DEV):
            peer = peers[k - 1]
            pltpu.make_async_remote_copy(
                src_ref=s_ref, dst_ref=gath.at[4 * peer[0] + 2 * peer[1] + peer[2]],
                send_sem=send_sems.at[k - 1], recv_sem=recv_sems.at[k - 1],
                device_id=peer, device_id_type=MESH).wait_recv()
        for cp in sends:
            cp.wait_send()
        g = gath[0]
        for d in range(1, N_DEV):
            g = g + gath[d]
        delta, m_new, v_new = _adamw_math(w_ref[...], g, m_ref[...], v_ref[...])
        g_out[...] = g
        d_out[...] = delta
        m_out[...] = m_new
        v_out[...] = v_new

    vm = pl.BlockSpec(memory_space=pltpu.VMEM)
    shp = jax.ShapeDtypeStruct((R, W), F32)
    return pl.pallas_call(
        body, out_shape=(shp, shp, shp, shp), in_specs=[vm] * 4, out_specs=(vm, vm, vm, vm),
        scratch_shapes=[pltpu.VMEM((N_DEV, R, W), F32), pltpu.SemaphoreType.DMA((N_DEV - 1,)),
                        pltpu.SemaphoreType.DMA((N_DEV - 1,))],
        compiler_params=_params(), name=name)(s, w, m, v)


def _pack_small(rel_bias, g1, g2, g3, g4, b_forget, sinks, extra=None, meta=None):
    misc = jnp.concatenate([rel_bias.reshape(-1), b_forget.reshape(-1), sinks.reshape(-1)])
    misc = jnp.concatenate([misc, jnp.zeros((D_MODEL - misc.shape[0],), F32)])[None]
    last = jnp.zeros((1, D_MODEL), F32) if extra is None else extra
    meta = jnp.zeros((N_META, D_MODEL), F32) if meta is None else meta
    return jnp.concatenate([g1, g2, g3, g4, misc, last, jnp.zeros((2, D_MODEL), F32), meta], axis=0)


def _unpack_small(p):
    nrb = N_BUCKETS * SWA_Q_HEADS
    misc = p[4]
    return dict(rel_bias=misc[:nrb].reshape(N_BUCKETS, SWA_Q_HEADS), ln_pre_mix=p[0:1], ln_post_mix=p[1:2],
                ln_pre_ffn=p[2:3], ln_post_ffn=p[3:4], b_forget=misc[nrb:nrb + 8].reshape(1, 8),
                sinks=misc[nrb + 8:nrb + 16].reshape(1, 8))


def _shard_order(pieces, shard, pad):
    T, dtype = pieces[0].shape[0], pieces[0].dtype
    total = sum(p.shape[1] for p in pieces)
    assert total % shard == 0
    out, zeros = [], jnp.zeros((T, pad), dtype)
    for s in range(total // shard):
        lo, hi, start = s * shard, (s + 1) * shard, 0
        for p in pieces:
            end = start + p.shape[1]
            if max(lo, start) < min(hi, end):
                out.append(p[:, max(lo, start) - start:min(hi, end) - start])
            start = end
        out.append(zeros)
    return jnp.concatenate(out, axis=1)


def _unheads(a):
    return a.transpose(1, 0, 2).reshape(a.shape[1], -1)


def kernel(x, meta_tokens, rel_bias, ln_pre_mix, ln_post_mix, ln_pre_ffn, ln_post_ffn, w_in, b_forget, sinks, w_out, w_gate_up, w_down, loss_target, m_meta_tokens, m_rel_bias, m_ln_pre_mix, m_ln_post_mix, m_ln_pre_ffn, m_ln_post_ffn, m_w_in, m_b_forget, m_sinks, m_w_out, m_w_gate_up, m_w_down, v_meta_tokens, v_rel_bias, v_ln_pre_mix, v_ln_post_mix, v_ln_pre_ffn, v_ln_post_ffn, v_w_in, v_b_forget, v_sinks, v_w_out, v_w_gate_up, v_w_down):
    seq = x.shape[1]
    T = BLOCK + seq
    assert T % FOX_TILE == 0
    nq = T // FOX_TILE
    tm = _tile(T, 1056)
    cin = w_in.shape[2]
    hid = w_down.shape[1]
    assert w_gate_up.shape[2] == 2 * hid and cin <= W_IN_PAD and hid <= HID_PAD

    x_i, y_i, c_i = _coords()
    core = jnp.reshape(c_i, (1,)).astype(jnp.int32)
    chip = jnp.reshape(2 * x_i + y_i, (1,)).astype(jnp.int32)
    w_in_s = jnp.pad(w_in[0].astype(BF16), ((0, 0), (0, W_IN_PAD - cin)))
    w_gu_s = jnp.pad(w_gate_up[0].astype(BF16).reshape(D_MODEL, 2, hid), ((0, 0), (0, 0), (0, HID_PAD - hid)))
    w_gu_s = w_gu_s.reshape(D_MODEL, 2 * HID_PAD)
    w_down_s = jnp.pad(w_down[0].astype(BF16), ((0, HID_PAD - hid), (0, 0)))
    g_in, g_meta = _run_exchange(_gather_exchange([w_in_s, meta_tokens]), name="ag_w_in")
    gather_rest = _gather_exchange([w_out[0].astype(BF16), w_gu_s, w_down_s])
    w_in_full = g_in[:, :, :cin].transpose(1, 0, 2).reshape(D_MODEL, N_DEV * cin)
    w_qkv = w_in_full[:, :D_QKV]
    w_f = jnp.pad(w_in_full[:, D_QKV:], ((0, 0), (0, BLOCK - FOX_HEADS)))
    meta_full = g_meta.transpose(1, 0, 2).reshape(N_META, D_MODEL)

    h0 = jnp.concatenate([jnp.zeros((PAD_ROWS, D_MODEL), F32), meta_full, x[0]], axis=0)
    target = jnp.concatenate([jnp.zeros((BLOCK, D_MODEL), F32), loss_target[0]], axis=0)
    hn1, hn1_t = _rms_fwd(h0, ln_pre_mix, name="rms_pre_mix")
    proj = _matmul(hn1, w_qkv, out_dtype=BF16, tm=tm, tn=768, name="mm_in_proj")
    proj_f = _matmul(hn1, w_f, out_dtype=F32, tm=tm, tn=BLOCK, name="mm_in_proj_f")

    f_t = proj_f[:, :FOX_HEADS].T
    bf_col = b_forget.reshape(FOX_HEADS, 1)

    oh_cur, oh_prev = _bucket_onehots()
    bias_c = jnp.einsum("pb,bh->hp", jnp.asarray(oh_cur), rel_bias, precision=HIGHEST).reshape(8, BLOCK, BLOCK)
    bias_p = jnp.einsum("pb,bh->hp", jnp.asarray(oh_prev), rel_bias, precision=HIGHEST).reshape(8, BLOCK, BLOCK)
    far = rel_bias[N_BUCKETS - 1]
    sink_v = sinks[0]
    mix_a = _swa_fwd(proj, bias_c, bias_p, far, sink_v, name="swa_fwd")

    _, cum_col = _fox_gates_fwd(f_t, bf_col, name="fox_gates_fwd")
    q_b, k_b, v_b = _fox_prep(proj, cum_col, name="fox_prep")
    mix, lse_row, g_out, g_gu, g_down = _fox_fwd(q_b, k_b, v_b, mix_a, ex=gather_rest, name="fox_fwd")
    w_out_full = g_out.reshape(D_MODEL, D_MODEL)
    w_down_full = g_down.reshape(N_DEV * HID_PAD, D_MODEL)

    a1 = _matmul(mix, w_out_full, out_dtype=F32, tm=tm, tn=512, name="mm_out_proj")
    h1, hn2, hn2_t = _post_res_norm(a1, ln_post_mix, h0, ln_pre_ffn, name="post_mix_pre_ffn")
    gate, up, act, act_t = _gate_up_swiglu(hn2, g_gu, name="mm_gate_up")
    ff = _matmul(act, w_down_full, out_dtype=F32, tm=tm, tn=512, name="mm_down")
    dh2, dff, dg_post_ffn, loss_acc = _loss_head(ff, ln_post_ffn, h1, target, name="loss_head")

    dgu = _d_act_swiglu(dff, w_down_full, gate, up, name="mm_d_act")
    d_w_down = _matmul(act_t, dff, out_dtype=F32, tm=768, tn=512, name="mm_dw_down")
    dhn2 = _matmul(dgu, g_gu, nt=True, b_shards=True, out_dtype=F32, tm=_tile(T, 528), tn=512, name="mm_d_hn2")
    d_w_gu = _matmul(hn2_t, dgu, out_shards=True, out_dtype=F32, tm=512, tn=2 * HID_PAD, name="mm_dw_gate_up")
    dh1, dg_pre_ffn, da1, dg_post_mix = _rms_bwd(h1, ln_pre_ffn, dhn2, dh2, out_dtype=F32,
                                                 then=(a1, ln_post_mix), name="rms_bwd_pre_ffn_post_mix")
    dmix = _matmul(da1, w_out_full, nt=True, out_dtype=BF16, tm=tm, tn=512, name="mm_d_mix")
    d_w_out = _matmul(mix.T, da1, out_dtype=F32, tm=512, tn=512, name="mm_dw_out")

    ffn_grads = [d_w_out.reshape(N_DEV, -1, D_MODEL), d_w_gu, d_w_down.reshape(N_DEV, HID_PAD, D_MODEL)]
    dq_a, dk_a, dv_a, dbc, dbp, dbf, dsk, *ffn_sibling = _swa_bwd(
        proj, dmix, bias_c, bias_p, far, sink_v, ex=_cores_exchange(ffn_grads), name="swa_bwd")
    low = (jnp.arange(2 * HEAD_DIM) < HEAD_DIM)[None, :]
    dk_a = jnp.where(low, dk_a[0], dk_a[1]).astype(BF16)
    dv_a = jnp.where(low, dv_a[0], dv_a[1]).astype(BF16)
    d_tab, d_sink = _small_grads(dbc, dbp, dbf, dsk, jnp.asarray(oh_cur), jnp.asarray(oh_prev), name="small_grads")
    ffn_sums = [_add_cores(g, r, core, name="rs_add_" + t)
                for g, r, t in zip(ffn_grads, ffn_sibling, ["w_out", "w_gate_up", "w_down"])]

    do_b = _fox_prep_bwd(dmix, mix, name="fox_prep_bwd")
    dq_b, dk_b, dv_b, dq_acc, dck, *ffn_chips = _fox_bwd(
        q_b, k_b, v_b, do_b, lse_row, ex=_chips_exchange([s[1] for s in ffn_sums]), name="fox_bwd")
    dcq = dq_acc[:, :, LANE_QC]
    df_t, d_bf = _fox_gates_bwd(dcq, dck.reshape(FOX_HEADS, T), f_t, bf_col, name="fox_gates_bwd")

    dproj_s = _shard_order([dq_a, dk_a, dv_a, dq_b, dk_b, dv_b, df_t.T.astype(BF16)], cin, W_IN_PAD - cin)
    d_w_in = _matmul(hn1_t, dproj_s, out_shards=True, out_dtype=F32, tm=512, tn=W_IN_PAD, name="mm_dw_in")
    dhn1, in_sibling = _matmul(dproj_s, g_in, nt=True, b_shards=True, out_dtype=F32, tm=tm, tn=512,
                               ex=_cores_exchange([d_w_in]), name="mm_d_hn1")
    in_sum = _add_cores(d_w_in, in_sibling, core, name="rs_add_w_in")
    dh0, dg_pre_mix, in_chips = _rms_bwd(h0, ln_pre_mix, dhn1, dh1, out_dtype=F32,
                                         ex=_chips_exchange([in_sum[1]]), name="rms_bwd_pre_mix")
    grad_x = dh0[BLOCK:][None]
    d_meta = dh0[PAD_ROWS:BLOCK]

    tags = ["w_in", "w_out", "w_gate_up", "w_down"]
    chip_sum = [in_sum[0]] + [s[0] for s in ffn_sums]
    from_chips = [in_chips] + list(ffn_chips)
    shard_w = [(w_in, m_w_in, v_w_in), (w_out, m_w_out, v_w_out), (w_gate_up, m_w_gate_up, v_w_gate_up),
               (w_down, m_w_down, v_w_down)]
    segs = [[(0, 0, cin)], [(0, 0, D_MODEL)], [(0, 0, hid), (HID_PAD, hid, hid)], [(0, 0, D_MODEL)]]
    tas = [256, BLOCK, 256, hid]
    big = [{}, {}, {}, {}]
    for i, t in enumerate(tags):
        w_t, m_t, v_t = shard_w[i]
        res = _sum_adamw(chip_sum[i], from_chips[i], chip, w_t[0], m_t[0], v_t[0], segs=segs[i], ta=tas[i],
                         name="rs_adamw_" + t)
        for kind in range(4):
            big[kind][t] = res[kind][None]

    loss_row = jnp.pad(loss_acc[0:1, 0:1] * (0.5 / D_MODEL), ((0, 0), (0, D_MODEL - 1)))
    s_small = _pack_small(d_tab.T, dg_pre_mix, dg_post_mix, dg_pre_ffn, dg_post_ffn, d_bf, d_sink,
                          extra=loss_row, meta=d_meta)
    w_s = _pack_small(rel_bias, ln_pre_mix, ln_post_mix, ln_pre_ffn, ln_post_ffn, b_forget, sinks)
    m_s = _pack_small(m_rel_bias, m_ln_pre_mix, m_ln_post_mix, m_ln_pre_ffn, m_ln_post_ffn, m_b_forget, m_sinks)
    v_s = _pack_small(v_rel_bias, v_ln_pre_mix, v_ln_post_mix, v_ln_pre_ffn, v_ln_post_ffn, v_b_forget, v_sinks)
    small = _small_allreduce_adamw(s_small, w_s, m_s, v_s, name="small_allreduce_adamw")
    loss = small[0][5, 0]
    mcols = meta_tokens.shape[1]
    g_meta_mine = lax.dynamic_slice(small[0][8:8 + N_META], (0, (4 * x_i + 2 * y_i + c_i) * mcols), (N_META, mcols))
    big[0]["meta_tokens"] = g_meta_mine
    for kind, arr in enumerate(_adamw(meta_tokens, g_meta_mine, m_meta_tokens, v_meta_tokens, name="adamw_meta")):
        big[kind + 1]["meta_tokens"] = arr
    small = [_unpack_small(p) for p in small]

    names = ["meta_tokens", "rel_bias", "ln_pre_mix", "ln_post_mix", "ln_pre_ffn", "ln_post_ffn", "w_in",
             "b_forget", "sinks", "w_out", "w_gate_up", "w_down"]
    outs = [loss, grad_x]
    for kind in range(4):
        for nme in names:
            outs.append(big[kind][nme] if nme in big[kind] else small[kind][nme])
    return tuple(outs)
```

```python
import math

import numpy as np
import jax
import jax.numpy as jnp
from jax import lax
from jax.experimental import pallas as pl
from jax.experimental.pallas import tpu as pltpu

F32 = jnp.float32
BF16 = jnp.bfloat16
HIGHEST = lax.Precision.HIGHEST
MESH = pl.DeviceIdType.MESH

N_DEV = 8
D_MODEL = 1024
N_META = 16
HEAD_DIM = 64
SWA_Q_HEADS = 8
SWA_KV_HEADS = 2
SWA_GROUP = 4
FOX_HEADS = 8
FOX_W = FOX_HEADS * HEAD_DIM
BLOCK = 128
PAD_ROWS = BLOCK - N_META
N_BUCKETS = 32
MAX_DISTANCE = 128
D_FF = 2816
D_QKV = 2304
D_PROJ = D_QKV + FOX_HEADS
D_PROJ_PAD = 2560
EPS = 1e-6
NEG = -1e30
SCALE = HEAD_DIM ** -0.5
ADAM_LR, ADAM_B1, ADAM_B2, ADAM_EPS, ADAM_WD, ADAM_STEP = 0.001, 0.9, 0.999, 1e-08, 0.01, 10
VMEM_LIMIT = 56 * 1024 * 1024
FOX_TILE = 384
FOX_GROUP = 4
W_IN_PAD = 384
HID_PAD = 384

NT = (((1,), (1,)), ((), ()))
NN = (((1,), (0,)), ((), ()))
TN = (((0,), (0,)), ((), ()))


def _params(sem=None, **kw):
    if sem is not None:
        kw["dimension_semantics"] = sem
    return pltpu.CompilerParams(vmem_limit_bytes=VMEM_LIMIT, **kw)


def _tile(n, target, mult=16):
    best = None
    for t in range(mult, min(n, target) + 1, mult):
        if n % t == 0:
            best = t
    assert best is not None, (n, target)
    return best


def _matmul(a, b, *, nt=False, b_shards=False, out_shards=False, out_dtype, tm, tn=None, tk=None, ex=None, name):
    M, K = a.shape
    k_shards = b.shape[0] if (b_shards and nt) else 0
    if k_shards:
        N, ks = b.shape[1], b.shape[2]
        assert tk is None and K == k_shards * ks
    elif b_shards:
        N, tn = b.shape[0] * b.shape[2], b.shape[2]
    else:
        N = b.shape[0] if nt else b.shape[1]
    tk = K if tk is None else tk
    assert M % tm == 0 and N % tn == 0 and K % tk == 0, (name, a.shape, b.shape, tm, tn, tk)
    nk = K // tk
    dn = NT if nt else NN

    def body(a_ref, b_ref, o_ref, *scr):
        if k_shards:
            part = sum(lax.dot_general(a_ref[:, s * ks:(s + 1) * ks], b_ref[s], NT, preferred_element_type=F32)
                       for s in range(k_shards))
        else:
            part = lax.dot_general(a_ref[...], b_ref[...], dn, preferred_element_type=F32)
        if nk == 1:
            o_ref[...] = part.astype(o_ref.dtype)
        else:
            acc = scr[0]
            k = pl.program_id(2)

            @pl.when(k == 0)
            def _():
                acc[...] = part

            @pl.when(k > 0)
            def _():
                acc[...] += part

            @pl.when(k == nk - 1)
            def _():
                o_ref[...] = acc[...].astype(o_ref.dtype)

    if k_shards:
        b_spec = pl.BlockSpec((k_shards, tn, ks), lambda i, j, k: (0, j, 0))
    elif b_shards:
        b_spec = pl.BlockSpec((None, tk, tn), lambda i, j, k: (j, k, 0))
    elif nt:
        b_spec = pl.BlockSpec((tn, tk), lambda i, j, k: (j, k))
    else:
        b_spec = pl.BlockSpec((tk, tn), lambda i, j, k: (k, j))
    if out_shards:
        out_shape = jax.ShapeDtypeStruct((N // tn, M, tn), out_dtype)
        out_spec = pl.BlockSpec((None, tm, tn), lambda i, j, k: (j, i, 0))
    else:
        out_shape = jax.ShapeDtypeStruct((M, N), out_dtype)
        out_spec = pl.BlockSpec((tm, tn), lambda i, j, k: (i, j))
    grid = (M // tm, N // tn, nk)
    body, x_in, x_in_specs, x_out, x_out_specs, x_scr = _carry(ex, grid, 2, 1, body)
    res = pl.pallas_call(
        body,
        out_shape=(out_shape, *x_out),
        grid=grid,
        in_specs=[pl.BlockSpec((tm, tk), lambda i, j, k: (i, k)), b_spec] + x_in_specs,
        out_specs=(out_spec, *x_out_specs),
        scratch_shapes=([pltpu.VMEM((tm, tn), F32)] if nk > 1 else []) + x_scr,
        compiler_params=_params(("parallel", "parallel", "arbitrary") if ex is None else ("arbitrary",) * 3),
        name=name,
    )(a, b, *x_in)
    return res[0] if ex is None else res


def _rstd(x):
    return lax.rsqrt(jnp.mean(x * x, axis=-1, keepdims=True) + EPS)


def _rms_fwd(x, g, *, name):
    T, D = x.shape
    tm = _tile(T, 512)

    def body(x_ref, g_ref, o_ref, ot_ref):
        x = x_ref[...]
        y = x * _rstd(x) * g_ref[...]
        o_ref[...] = y.astype(o_ref.dtype)
        ot_ref[...] = y.T.astype(ot_ref.dtype)

    return pl.pallas_call(
        body, out_shape=(jax.ShapeDtypeStruct((T, D), BF16), jax.ShapeDtypeStruct((D, T), BF16)), grid=(T // tm,),
        in_specs=[pl.BlockSpec((tm, D), lambda i: (i, 0)), pl.BlockSpec((1, D), lambda i: (0, 0))],
        out_specs=(pl.BlockSpec((tm, D), lambda i: (i, 0)), pl.BlockSpec((D, tm), lambda i: (0, i))),
        compiler_params=_params(("parallel",)), name=name)(x, g)


def _post_res_norm(a, g_post, h, g_pre, *, name):
    T, D = a.shape
    tm = _tile(T, 384, BLOCK)

    def body(a_ref, gp_ref, h_ref, gn_ref, h1_ref, o_ref, ot_ref):
        a = a_ref[...]
        h1 = h_ref[...] + a * _rstd(a) * gp_ref[...]
        h1_ref[...] = h1
        y = h1 * _rstd(h1) * gn_ref[...]
        o_ref[...] = y.astype(o_ref.dtype)
        ot_ref[...] = y.T.astype(ot_ref.dtype)

    row = pl.BlockSpec((tm, D), lambda i: (i, 0))
    vec = pl.BlockSpec((1, D), lambda i: (0, 0))
    return pl.pallas_call(
        body, out_shape=(jax.ShapeDtypeStruct((T, D), F32), jax.ShapeDtypeStruct((T, D), BF16),
                         jax.ShapeDtypeStruct((D, T), BF16)), grid=(T // tm,),
        in_specs=[row, vec, row, vec], out_specs=(row, row, pl.BlockSpec((D, tm), lambda i: (0, i))),
        compiler_params=_params(("parallel",)), name=name)(a, g_post, h, g_pre)


def _loss_head(a, g, h, target, *, name):
    T, D = a.shape
    tm = _tile(T, 512)

    def body(a_ref, g_ref, h_ref, t_ref, dy_ref, da_ref, dg_ref, loss_ref):
        i = pl.program_id(0)
        a = a_ref[...]
        r = _rstd(a)
        ah = a * r
        y = h_ref[...] + ah * g_ref[...]
        rows = i * tm + lax.broadcasted_iota(jnp.int32, (tm, 1), 0)
        err = jnp.where(rows >= BLOCK, y - t_ref[...], 0.0)
        dy = err / D
        dy_ref[...] = dy
        dah = dy * g_ref[...]
        da_ref[...] = (r * (dah - ah * jnp.mean(dah * ah, axis=-1, keepdims=True))).astype(da_ref.dtype)
        part = jnp.sum(jnp.sum(err * err, axis=1, keepdims=True), axis=0, keepdims=True)

        @pl.when(i == 0)
        def _():
            loss_ref[...] = jnp.zeros_like(loss_ref)
            dg_ref[...] = jnp.zeros_like(dg_ref)

        loss_ref[...] += jnp.broadcast_to(part, loss_ref.shape)
        dg_ref[...] += jnp.sum(dy * ah, axis=0, keepdims=True)

    row = pl.BlockSpec((tm, D), lambda i: (i, 0))
    vec = pl.BlockSpec((1, D), lambda i: (0, 0))
    return pl.pallas_call(
        body, out_shape=(jax.ShapeDtypeStruct((T, D), F32), jax.ShapeDtypeStruct((T, D), BF16),
                         jax.ShapeDtypeStruct((1, D), F32), jax.ShapeDtypeStruct((8, 128), F32)),
        grid=(T // tm,),
        in_specs=[row, vec, row, row],
        out_specs=(row, row, vec, pl.BlockSpec((8, 128), lambda i: (0, 0))),
        compiler_params=_params(("arbitrary",)), name=name)(a, g, h, target)


def _rms_bwd(x, g, dy, res, *, out_dtype, then=None, ex=None, name):
    T, D = x.shape
    tm = _tile(T, 512)
    has_res = res is not None
    n_in = 3 + has_res + (2 if then is not None else 0)
    n_out = 2 + (2 if then is not None else 0)

    def pull_back(x, g, dy):
        r = _rstd(x)
        xh = x * r
        dxh = dy * g
        return r * (dxh - xh * jnp.mean(dxh * xh, axis=-1, keepdims=True)), jnp.sum(dy * xh, axis=0, keepdims=True)

    def body(*refs):
        ins, outs = refs[:n_in], refs[n_in:]
        i = pl.program_id(0)

        @pl.when(i == 0)
        def _():
            for ref in outs[1::2]:
                ref[...] = jnp.zeros_like(ref)

        dx, dg = pull_back(ins[0][...], ins[1][...], ins[2][...].astype(F32))
        if has_res:
            dx = dx + ins[3][...]
        outs[0][...] = dx.astype(outs[0].dtype)
        outs[1][...] += dg
        if then is not None:
            dx2, dg2 = pull_back(ins[n_in - 2][...], ins[n_in - 1][...], dx)
            outs[2][...] = dx2.astype(outs[2].dtype)
            outs[3][...] += dg2

    row = pl.BlockSpec((tm, D), lambda i: (i, 0))
    vec = pl.BlockSpec((1, D), lambda i: (0, 0))
    ins = [x, g, dy] + ([res] if has_res else []) + (list(then) if then is not None else [])
    in_specs = [row, vec, row] + ([row] if has_res else []) + ([row, vec] if then is not None else [])
    out_shape = [jax.ShapeDtypeStruct((T, D), out_dtype), jax.ShapeDtypeStruct((1, D), F32)]
    out_specs = [row, vec]
    if then is not None:
        out_shape += [jax.ShapeDtypeStruct((T, D), BF16), jax.ShapeDtypeStruct((1, D), F32)]
        out_specs += [row, vec]
    grid = (T // tm,)
    body, x_in, x_in_specs, x_out, x_out_specs, x_scr = _carry(ex, grid, n_in, n_out, body)
    return pl.pallas_call(
        body, out_shape=(*out_shape, *x_out), grid=grid,
        in_specs=in_specs + x_in_specs, out_specs=(*out_specs, *x_out_specs), scratch_shapes=x_scr,
        compiler_params=_params(("arbitrary",)), name=name)(*ins, *x_in)


def _gate_up_swiglu(a, w, *, name):
    T, D = a.shape
    S, n = w.shape[0] // 2, w.shape[2]
    tm = _tile(T, 1408, BLOCK)

    def body(a_ref, wg_ref, wu_ref, g_ref, u_ref, o_ref, ot_ref):
        x = a_ref[...]
        g = jnp.dot(x, wg_ref[...], preferred_element_type=F32)
        u = jnp.dot(x, wu_ref[...], preferred_element_type=F32)
        g16, u16 = g.astype(BF16), u.astype(BF16)
        g_ref[...] = g16
        u_ref[...] = u16
        gr = g16.astype(F32)
        act = gr / (1.0 + jnp.exp(-gr)) * u16.astype(F32)
        o_ref[...] = act.astype(o_ref.dtype)
        ot_ref[...] = act.T.astype(ot_ref.dtype)

    tile = pl.BlockSpec((tm, n), lambda i, j: (i, j))
    shp = jax.ShapeDtypeStruct((T, S * n), BF16)
    return pl.pallas_call(
        body, out_shape=(shp, shp, shp, jax.ShapeDtypeStruct((S * n, T), BF16)), grid=(T // tm, S),
        in_specs=[pl.BlockSpec((tm, D), lambda i, j: (i, 0)),
                  pl.BlockSpec((None, D, n), lambda i, j: (j, 0, 0)),
                  pl.BlockSpec((None, D, n), lambda i, j: (j + S, 0, 0))],
        out_specs=(tile, tile, tile, pl.BlockSpec((n, tm), lambda i, j: (j, i))),
        compiler_params=_params(("parallel", "parallel")), name=name)(a, w, w)


def _d_act_swiglu(dff, w_down, gate, up, *, name):
    T, D = dff.shape
    F = w_down.shape[0]
    tm = _tile(T, 384)
    tf = _tile(F, 768, BLOCK)

    def body(d_ref, w_ref, g_ref, u_ref, o_ref):
        dy = d_ref[...]
        for c in range(0, F, tf):
            d = lax.dot_general(dy, w_ref[c:c + tf, :], NT, preferred_element_type=F32)
            g = g_ref[:, c:c + tf].astype(F32)
            u = u_ref[:, c:c + tf].astype(F32)
            sg = 1.0 / (1.0 + jnp.exp(-g))
            o_ref[:, c:c + tf] = (d * u * (sg * (1.0 + g * (1.0 - sg)))).astype(o_ref.dtype)
            o_ref[:, F + c:F + c + tf] = (d * (g * sg)).astype(o_ref.dtype)

    row = pl.BlockSpec((tm, F), lambda i: (i, 0))
    return pl.pallas_call(
        body, out_shape=jax.ShapeDtypeStruct((T, 2 * F), BF16), grid=(T // tm,),
        in_specs=[pl.BlockSpec((tm, D), lambda i: (i, 0)), pl.BlockSpec((F, D), lambda i: (0, 0)), row, row],
        out_specs=pl.BlockSpec((tm, 2 * F), lambda i: (i, 0)),
        compiler_params=_params(("parallel",)), name=name)(dff, w_down, gate, up)


def _fox_gates_fwd(f_t, b, *, name):
    H, T = f_t.shape
    nb = T // BLOCK

    def body(f_ref, b_ref, cum_ref, col_ref):
        f = f_ref[...] + b_ref[...]
        ls = jnp.minimum(f, 0.0) - jnp.log(1.0 + jnp.exp(-jnp.abs(f)))
        t = lax.broadcasted_iota(jnp.int32, (H, T), 1)
        ls = jnp.where(t >= PAD_ROWS, ls, 0.0)
        upper = (lax.broadcasted_iota(jnp.int32, (BLOCK, BLOCK), 0)
                 <= lax.broadcasted_iota(jnp.int32, (BLOCK, BLOCK), 1)).astype(F32)
        carry = jnp.zeros((H, 1), F32)
        for blk in range(nb):
            seg = ls[:, blk * BLOCK:(blk + 1) * BLOCK]
            pre = jnp.dot(seg, upper, precision=HIGHEST, preferred_element_type=F32) + carry
            cum_ref[:, blk * BLOCK:(blk + 1) * BLOCK] = pre
            col_ref[blk * BLOCK:(blk + 1) * BLOCK, :] = jnp.concatenate(
                [pre, jnp.zeros((BLOCK - H, BLOCK), F32)], axis=0).T
            carry = pre[:, BLOCK - 1:BLOCK]

    vm = pl.BlockSpec(memory_space=pltpu.VMEM)
    return pl.pallas_call(
        body, out_shape=(jax.ShapeDtypeStruct((H, T), F32), jax.ShapeDtypeStruct((T, BLOCK), F32)),
        in_specs=[vm, vm], out_specs=(vm, vm),
        compiler_params=_params(), name=name)(f_t, b)


def _fox_gates_bwd(dcq, dck, f_t, b, *, name):
    H, T = f_t.shape
    nb = T // BLOCK

    def body(dq_ref, d_ref, f_ref, b_ref, df_ref, db_ref):
        lower = (lax.broadcasted_iota(jnp.int32, (BLOCK, BLOCK), 0)
                 >= lax.broadcasted_iota(jnp.int32, (BLOCK, BLOCK), 1)).astype(F32)
        carry = jnp.zeros((H, 1), F32)
        for blk in range(nb - 1, -1, -1):
            seg = dq_ref[:, blk * BLOCK:(blk + 1) * BLOCK] - d_ref[:, blk * BLOCK:(blk + 1) * BLOCK]
            suf = jnp.dot(seg, lower, precision=HIGHEST, preferred_element_type=F32) + carry
            df_ref[:, blk * BLOCK:(blk + 1) * BLOCK] = suf
            carry = suf[:, 0:1]
        f = f_ref[...] + b_ref[...]
        t = lax.broadcasted_iota(jnp.int32, (H, T), 1)
        df = jnp.where(t >= PAD_ROWS, df_ref[...] / (1.0 + jnp.exp(f)), 0.0)
        df_ref[...] = df
        db_ref[...] = jnp.sum(df, axis=1, keepdims=True)

    vm = pl.BlockSpec(memory_space=pltpu.VMEM)
    return pl.pallas_call(
        body, out_shape=(jax.ShapeDtypeStruct((H, T), F32), jax.ShapeDtypeStruct((H, 1), F32)),
        in_specs=[vm, vm, vm, vm], out_specs=(vm, vm),
        compiler_params=_params(), name=name)(dcq, dck, f_t, b)


LANE_KC = HEAD_DIM
LANE_QC = HEAD_DIM + 3
LANE_END = HEAD_DIM + 6


def _split3(c):
    hi = c.astype(BF16).astype(F32)
    r = c - hi
    mid = r.astype(BF16).astype(F32)
    lo = (r - mid).astype(BF16).astype(F32)
    return hi, mid, lo


def _lanes(lane, data, start, terms, rest):
    out = rest
    for i, t in enumerate(terms):
        out = jnp.where(lane == start + i, t, out)
    return jnp.where(lane < HEAD_DIM, data, out)


def _fox_prep(proj, cum_col, *, name):
    T = proj.shape[0]
    tm = FOX_TILE
    nt = T // tm
    H = FOX_HEADS
    lanes = 2 * HEAD_DIM
    qb, kb, vb = 768 // lanes, 1280 // lanes, 1792 // lanes

    def body(q_ref, k_ref, v_ref, c_ref, qa_ref, ka_ref, va_ref):
        p = pl.program_id(0)
        i = pl.program_id(1)
        lane = lax.broadcasted_iota(jnp.int32, (tm, lanes), 1)
        rows = i * tm + lax.broadcasted_iota(jnp.int32, (tm, 1), 0)
        q2 = q_ref[...].astype(F32)
        k2 = k_ref[...].astype(F32)
        v2 = v_ref[...].astype(F32)
        cum = c_ref[...]
        for e in range(2):
            c = jnp.sum(jnp.where(lane == 2 * p + e, cum, 0.0), axis=1, keepdims=True)
            ck = jnp.where(rows >= PAD_ROWS, c, -NEG)
            qe, ke, ve = (q2, k2, v2) if e == 0 else tuple(pltpu.roll(a, HEAD_DIM, 1) for a in (q2, k2, v2))
            one = jnp.where(lane < LANE_END, 1.0, 0.0)
            qa = _lanes(lane, qe * SCALE, LANE_QC, _split3(c), jnp.where(lane < LANE_QC, -1.0, 0.0))
            ka = _lanes(lane, ke, LANE_KC, _split3(ck), one)
            va = jnp.where(lane < HEAD_DIM, ve, jnp.where(lane < LANE_QC, 1.0, 0.0))
            qa_ref[e] = qa.astype(BF16)
            ka_ref[e] = ka.astype(BF16)
            va_ref[e] = va.astype(BF16)

    def col(b):
        return pl.BlockSpec((tm, lanes), lambda p, i, b=b: (i, b + p))

    out = pl.BlockSpec((2, tm, lanes), lambda p, i: (p, i, 0))
    shp = jax.ShapeDtypeStruct((H, T, lanes), BF16)
    return pl.pallas_call(
        body, out_shape=(shp, shp, shp), grid=(H // 2, nt),
        in_specs=[col(qb), col(kb), col(vb), pl.BlockSpec((tm, lanes), lambda p, i: (i, 0))],
        out_specs=(out, out, out),
        compiler_params=_params(("parallel", "parallel")), name=name)(proj, proj, proj, cum_col)


def _fox_fwd(q_aug, k_aug, v_aug, mix, *, ex=None, name):
    H, T, lanes = q_aug.shape
    tq = FOX_TILE
    nq = T // tq
    G = FOX_GROUP

    def body(q_ref, k_ref, v_ref, mix_ref, o_ref, lse_ref, m_scr, acc_scr):
        i = pl.program_id(1)
        m_scr[...] = jnp.full(m_scr.shape, NEG, F32)
        acc_scr[...] = jnp.zeros(acc_scr.shape, F32)

        def step(kb, diag):
            off = pl.multiple_of(kb * tq, tq)
            s_t = [lax.dot_general(k_ref[g, pl.ds(off, tq), :], q_ref[g], NT, preferred_element_type=F32)
                   for g in range(G)]
            if diag:
                r = lax.broadcasted_iota(jnp.int32, (tq, tq), 0)
                c = lax.broadcasted_iota(jnp.int32, (tq, tq), 1)
                s_t = [jnp.where(c >= r, s, NEG) for s in s_t]
            m_prev = [m_scr[g] for g in range(G)]
            m_new = [jnp.maximum(m_prev[g], jnp.max(s_t[g], axis=0, keepdims=True)) for g in range(G)]
            p_t = [jnp.exp(s_t[g] - m_new[g]).astype(BF16) for g in range(G)]
            pv = [lax.dot_general(v_ref[g, pl.ds(off, tq), :], p_t[g], TN, preferred_element_type=F32)
                  for g in range(G)]
            for g in range(G):
                acc_scr[g] = jnp.exp(m_prev[g] - m_new[g]) * acc_scr[g] + pv[g]
                m_scr[g] = m_new[g]

        def loop_body(kb, carry):
            step(kb, False)
            return carry

        lax.fori_loop(0, i, loop_body, 0)
        step(i, True)
        lane = lax.broadcasted_iota(jnp.int32, (tq, lanes), 1)
        outs = []
        for g in range(G):
            acc = acc_scr[g]
            lse_ref[g] = m_scr[g] + jnp.log(acc[HEAD_DIM:HEAD_DIM + 1, :])
            acc_t = acc.T
            outs.append(acc_t / acc_t[:, HEAD_DIM:HEAD_DIM + 1])
        for pair in range(G // 2):
            o_ref[:, pair * lanes:(pair + 1) * lanes] = jnp.where(
                lane < HEAD_DIM, outs[2 * pair], pltpu.roll(outs[2 * pair + 1], HEAD_DIM, 1)).astype(o_ref.dtype)

    blk = pl.BlockSpec((G, tq, lanes), lambda h, i: (h, i, 0))
    full = pl.BlockSpec((G, T, lanes), lambda h, i: (h, 0, 0))
    grid = (H // G, nq)
    first = mix.shape[1] // (G * HEAD_DIM) - H // G
    body, x_in, x_in_specs, x_out, x_out_specs, x_scr = _carry(ex, grid, 4, 2, body)
    return pl.pallas_call(
        body,
        out_shape=(jax.ShapeDtypeStruct(mix.shape, mix.dtype), jax.ShapeDtypeStruct((H, nq, 1, tq), F32), *x_out),
        grid=grid,
        in_specs=[blk, full, full, pl.BlockSpec(memory_space=pl.ANY)] + x_in_specs,
        out_specs=(pl.BlockSpec((tq, G * HEAD_DIM), lambda h, i: (i, first + h)),
                   pl.BlockSpec((G, None, 1, tq), lambda h, i: (h, i, 0, 0)), *x_out_specs),
        input_output_aliases={3: 0},
        scratch_shapes=[pltpu.VMEM((G, 1, tq), F32), pltpu.VMEM((G, lanes, tq), F32)] + x_scr,
        compiler_params=_params(("arbitrary", "arbitrary")), name=name)(q_aug, k_aug, v_aug, mix, *x_in)


def _fox_prep_bwd(dmix, mix, *, name):
    T = dmix.shape[0]
    H = FOX_HEADS
    tm = FOX_TILE
    lanes = 2 * HEAD_DIM
    first = mix.shape[1] // lanes - H // 2

    def body(d_ref, o_ref, da_ref):
        lane = lax.broadcasted_iota(jnp.int32, (tm, lanes), 1)
        d2 = d_ref[...].astype(F32)
        prod = d2 * o_ref[...].astype(F32)
        for e in range(2):
            de = d2 if e == 0 else pltpu.roll(d2, HEAD_DIM, 1)
            delta = jnp.sum(jnp.where(lane // HEAD_DIM == e, prod, 0.0), axis=1, keepdims=True)
            da_ref[e] = _lanes(lane, de, LANE_KC, _split3(-delta), jnp.zeros((), F32)).astype(BF16)

    pair = pl.BlockSpec((tm, lanes), lambda p, i: (i, first + p))
    return pl.pallas_call(
        body, out_shape=jax.ShapeDtypeStruct((H, T, lanes), BF16), grid=(H // 2, T // tm),
        in_specs=[pair, pair],
        out_specs=pl.BlockSpec((2, tm, lanes), lambda p, i: (p, i, 0)),
        compiler_params=_params(("parallel", "parallel")), name=name)(dmix, mix)


def _fox_bwd(q_aug, k_aug, v_aug, do_aug, lse_row, *, ex=None, name):
    H, T, lanes = q_aug.shape
    tq = FOX_TILE
    nq = T // tq
    G = FOX_GROUP

    def side_by_side(tiles, scale=None):
        lane = lax.broadcasted_iota(jnp.int32, tiles[0].shape, 1)
        out = [jnp.where(lane < HEAD_DIM, tiles[2 * p], pltpu.roll(tiles[2 * p + 1], HEAD_DIM, 1))
               for p in range(G // 2)]
        out = jnp.concatenate(out, axis=1)
        return out if scale is None else out * scale

    def body(q_ref, k_ref, v_ref, do_ref, lse_ref, dqb_ref, dkb_ref, dvb_ref, dcq_ref, dck_ref, dk_acc, dv_acc, dq_ref):
        j = pl.program_id(1)

        @pl.when(j == 0)
        def _():
            dq_ref[...] = jnp.zeros(dq_ref.shape, F32)
            dcq_ref[...] = jnp.zeros(dcq_ref.shape, F32)

        dk_acc[...] = jnp.zeros(dk_acc.shape, F32)
        dv_acc[...] = jnp.zeros(dv_acc.shape, F32)

        def step(qb, diag):
            off = pl.multiple_of(qb * tq, tq)
            heads = range(G)
            qa = [q_ref[g, pl.ds(off, tq), :] for g in heads]
            da = [do_ref[g, pl.ds(off, tq), :] for g in heads]
            s_t = [lax.dot_general(k_ref[g], qa[g], NT, preferred_element_type=F32) for g in heads]
            dp_t = [lax.dot_general(v_ref[g], da[g], NT, preferred_element_type=F32) for g in heads]
            p_t = [jnp.exp(s_t[g] - lse_ref[g, qb]) for g in heads]
            if diag:
                r = lax.broadcasted_iota(jnp.int32, (tq, tq), 0)
                c = lax.broadcasted_iota(jnp.int32, (tq, tq), 1)
                p_t = [jnp.where(c >= r, p, 0.0) for p in p_t]
            dsb = [(p_t[g] * dp_t[g]).astype(BF16) for g in heads]
            dv = [jnp.dot(p_t[g].astype(BF16), da[g], preferred_element_type=F32) for g in heads]
            dk = [jnp.dot(dsb[g], qa[g], preferred_element_type=F32) for g in heads]
            dq = [jnp.dot(dsb[g].T, k_ref[g], preferred_element_type=F32) for g in heads]
            for g in heads:
                dv_acc[g] += dv[g]
                dk_acc[g] += dk[g]
                dq_ref[g, pl.ds(off, tq), :] += dq[g]
                dcq_ref[g, qb] += jnp.sum(dsb[g].astype(F32), axis=0, keepdims=True)

        step(j, True)

        def loop_body(qb, carry):
            step(qb, False)
            return carry

        lax.fori_loop(j + 1, nq, loop_body, 0)
        dk = [dk_acc[g] for g in range(G)]
        dkb_ref[...] = side_by_side(dk).astype(dkb_ref.dtype)
        dvb_ref[...] = side_by_side([dv_acc[g] for g in range(G)]).astype(dvb_ref.dtype)
        for g in range(G):
            dck_ref[g] = -dk[g].T[LANE_KC:LANE_KC + 1, :]

        @pl.when(j == nq - 1)
        def _():
            for qb in range(nq):
                rows = pl.ds(qb * tq, tq)
                dqb_ref[rows, :] = side_by_side([dq_ref[g, rows, :] for g in range(G)], SCALE).astype(dqb_ref.dtype)

    blk = pl.BlockSpec((G, tq, lanes), lambda h, j: (h, j, 0))
    full = pl.BlockSpec((G, T, lanes), lambda h, j: (h, 0, 0))
    wide = G * HEAD_DIM
    grid = (H // G, nq)
    body, x_in, x_in_specs, x_out, x_out_specs, x_scr = _carry(ex, grid, 5, 5, body)
    flat = jax.ShapeDtypeStruct((T, H * HEAD_DIM), BF16)
    rows = jax.ShapeDtypeStruct((H, nq, 1, tq), F32)
    all_rows = pl.BlockSpec((G, nq, 1, tq), lambda h, j: (h, 0, 0, 0))
    return pl.pallas_call(
        body,
        out_shape=(flat, flat, flat, rows, rows, *x_out),
        grid=grid,
        in_specs=[full, blk, blk, full, all_rows] + x_in_specs,
        out_specs=(pl.BlockSpec((T, wide), lambda h, j: (0, h)), pl.BlockSpec((tq, wide), lambda h, j: (j, h)),
                   pl.BlockSpec((tq, wide), lambda h, j: (j, h)), all_rows,
                   pl.BlockSpec((G, None, 1, tq), lambda h, j: (h, j, 0, 0)), *x_out_specs),
        scratch_shapes=[pltpu.VMEM((G, tq, lanes), F32), pltpu.VMEM((G, tq, lanes), F32),
                        pltpu.VMEM((G, T, lanes), F32)] + x_scr,
        compiler_params=_params(("arbitrary", "arbitrary")), name=name,
    )(q_aug, k_aug, v_aug, do_aug, lse_row, *x_in)


def _t5_bucket_np(d):
    n = np.maximum(d, 0).astype(np.int32)
    max_exact = N_BUCKETS // 2
    nf = np.maximum(n, 1).astype(np.float32)
    large = max_exact + (np.log(nf / max_exact) / math.log(MAX_DISTANCE / max_exact)
                         * (N_BUCKETS - max_exact)).astype(np.int32)
    large = np.minimum(large, N_BUCKETS - 1)
    return np.where(n < max_exact, n, large)


def _bucket_onehots():
    k = np.arange(BLOCK)[:, None]
    q = np.arange(BLOCK)[None, :]
    eye = np.eye(N_BUCKETS, dtype=np.float32)
    cur = eye[_t5_bucket_np(q - k).reshape(-1)]
    prev = eye[_t5_bucket_np(BLOCK + q - k).reshape(-1)]
    return cur, prev


SWA_K_COL = SWA_Q_HEADS * HEAD_DIM // (2 * HEAD_DIM)
SWA_V_COL = SWA_K_COL + 1


def _swa_terms(raw, bc, bp, far, sink, n):
    k = lax.broadcasted_iota(jnp.int32, (BLOCK, BLOCK), 0)
    q = lax.broadcasted_iota(jnp.int32, (BLOCK, BLOCK), 1)
    never = 2 * BLOCK
    s_c = raw[0] + bc
    s_p = raw[1] + bp
    s_m = raw[2] + jnp.where(n == 1, bp, far)
    s_c = jnp.where((k <= q) & (k >= jnp.where(n >= 1, 0, PAD_ROWS)), s_c, NEG)
    s_p = jnp.where(k > q + jnp.where(n >= 2, 0, never), s_p, NEG)
    s_m = jnp.where(k >= jnp.where(n >= 1, PAD_ROWS, never), s_m, NEG)
    m = jnp.maximum(jnp.maximum(jnp.max(s_c, axis=0, keepdims=True), jnp.max(s_p, axis=0, keepdims=True)),
                    jnp.maximum(jnp.max(s_m, axis=0, keepdims=True), sink))
    e = [jnp.exp(s_c - m), jnp.exp(s_p - m), jnp.exp(s_m - m)]
    e_s = jnp.exp(sink - m)
    l = (jnp.sum(e[0], axis=0, keepdims=True) + jnp.sum(e[1], axis=0, keepdims=True)
         + jnp.sum(e[2], axis=0, keepdims=True) + e_s)
    return e, e_s, l


def _swa_specs():
    G = SWA_GROUP
    width = G * HEAD_DIM

    def rows(which, col):
        if which == "cur":
            return pl.BlockSpec((BLOCK, BLOCK), lambda kv, n: (n, col))
        if which == "prev":
            return pl.BlockSpec((BLOCK, BLOCK), lambda kv, n: (jnp.maximum(n - 1, 0), col))
        return pl.BlockSpec((BLOCK, BLOCK), lambda kv, n: (0, col))

    qblk = pl.BlockSpec((BLOCK, width), lambda kv, n: (n, kv))
    keys = [rows(w, SWA_K_COL) for w in ("cur", "prev", "meta")]
    vals = [rows(w, SWA_V_COL) for w in ("cur", "prev", "meta")]
    bias = pl.BlockSpec((G, BLOCK, BLOCK), lambda kv, n: (kv, 0, 0))
    smem = pl.BlockSpec(memory_space=pltpu.SMEM)
    return qblk, keys, vals, bias, smem


def _swa_own_kv(tile_ref, kv):
    lane = lax.broadcasted_iota(jnp.int32, (BLOCK, 2 * HEAD_DIM), 1)
    t = tile_ref[...].astype(F32)
    return jnp.where(lane // HEAD_DIM == kv, t, pltpu.roll(t, HEAD_DIM, 1)).astype(BF16)


def _swa_fwd(proj, bc, bp, far, sinks, *, name):
    T = proj.shape[0]
    nb = T // BLOCK
    G = SWA_GROUP
    lanes = 2 * HEAD_DIM

    def body(q_ref, kc_ref, kp_ref, km_ref, vc_ref, vp_ref, vm_ref, bc_ref, bp_ref, far_ref, sink_ref, o_ref):
        kv = pl.program_id(0)
        n = pl.program_id(1)
        lane = lax.broadcasted_iota(jnp.int32, (BLOCK, lanes), 1)
        kk = [_swa_own_kv(r, kv) for r in (kc_ref, kp_ref, km_ref)]
        vv = [_swa_own_kv(r, kv) for r in (vc_ref, vp_ref, vm_ref)]
        heads, blocks = range(G), range(3)
        q2 = [q_ref[:, pair * lanes:(pair + 1) * lanes].astype(F32) * SCALE for pair in range(G // 2)]
        qm = [jnp.where(lane // HEAD_DIM == g % 2, q2[g // 2], 0.0).astype(BF16) for g in heads]
        raw = [[lax.dot_general(kk[b], qm[g], NT, preferred_element_type=F32) for b in blocks] for g in heads]
        terms = [_swa_terms(raw[g], bc_ref[g], bp_ref[g], far_ref[kv * G + g], sink_ref[kv * G + g], n) for g in heads]
        o_t = [sum(lax.dot_general(vv[b], terms[g][0][b].astype(BF16), TN, preferred_element_type=F32) for b in blocks)
               for g in heads]
        outs = [(o_t[g] / terms[g][2]).T for g in heads]
        for pair in range(G // 2):
            o_ref[:, pair * lanes:(pair + 1) * lanes] = jnp.where(
                lane < HEAD_DIM, outs[2 * pair], outs[2 * pair + 1]).astype(o_ref.dtype)

    qblk, keys, vals, bias, smem = _swa_specs()
    return pl.pallas_call(
        body, out_shape=jax.ShapeDtypeStruct((T, D_MODEL), BF16), grid=(SWA_KV_HEADS, nb),
        in_specs=[qblk] + keys + vals + [bias, bias, smem, smem],
        out_specs=qblk,
        compiler_params=_params(("parallel", "parallel")), name=name,
    )(proj, proj, proj, proj, proj, proj, proj, bc, bp, far, sinks)


def _swa_bwd(proj, dmix, bc, bp, far, sinks, *, ex=None, name):
    T = proj.shape[0]
    nb = T // BLOCK
    G = SWA_GROUP
    Hq = SWA_Q_HEADS
    lanes = 2 * HEAD_DIM

    def body(q_ref, kc_ref, kp_ref, km_ref, vc_ref, vp_ref, vm_ref, do_ref, bc_ref, bp_ref, far_ref, sink_ref,
             dq_ref, dk_ref, dv_ref, dbc_ref, dbp_ref, dbf_ref, dsk_ref):
        kv = pl.program_id(0)
        n = pl.program_id(1)

        @pl.when(n == 0)
        def _():
            for ref in (dk_ref, dv_ref, dbc_ref, dbp_ref, dbf_ref, dsk_ref):
                ref[...] = jnp.zeros(ref.shape, F32)

        lane = lax.broadcasted_iota(jnp.int32, (BLOCK, lanes), 1)
        kk = [_swa_own_kv(r, kv) for r in (kc_ref, kp_ref, km_ref)]
        vv = [_swa_own_kv(r, kv) for r in (vc_ref, vp_ref, vm_ref)]
        heads, blocks = range(G), range(3)
        q2 = [q_ref[:, pair * lanes:(pair + 1) * lanes].astype(F32) * SCALE for pair in range(G // 2)]
        d2 = [do_ref[:, pair * lanes:(pair + 1) * lanes] for pair in range(G // 2)]
        own = [lane // HEAD_DIM == g % 2 for g in heads]
        qm = [jnp.where(own[g], q2[g // 2], 0.0).astype(BF16) for g in heads]
        dom = [jnp.where(own[g], d2[g // 2], jnp.zeros_like(d2[0])) for g in heads]
        raw = [[lax.dot_general(kk[b], qm[g], NT, preferred_element_type=F32) for b in blocks] for g in heads]
        dp = [[lax.dot_general(vv[b], dom[g], NT, preferred_element_type=F32) for b in blocks] for g in heads]
        p, ds16 = [], []
        for g in heads:
            e, e_s, l = _swa_terms(raw[g], bc_ref[g], bp_ref[g], far_ref[kv * G + g], sink_ref[kv * G + g], n)
            inv = 1.0 / l
            pg = [e[b] * inv for b in blocks]
            delta = sum(jnp.sum(pg[b] * dp[g][b], axis=0, keepdims=True) for b in blocks)
            ds = [pg[b] * (dp[g][b] - delta) for b in blocks]
            dsk_ref[g] += -(e_s * inv) * delta
            dbc_ref[g] += ds[0]
            dbp_ref[g] += ds[1] + jnp.where(n == 1, ds[2], 0.0)
            dbf_ref[g] += jnp.where(n >= 2, ds[2], 0.0)
            p.append([x.astype(BF16) for x in pg])
            ds16.append([x.astype(BF16) for x in ds])
        dq_t = [sum(lax.dot_general(kk[b], ds16[g][b], TN, preferred_element_type=F32) for b in blocks) for g in heads]
        dk = [sum(jnp.dot(ds16[g][b], qm[g], preferred_element_type=F32) for g in heads) for b in blocks]
        dv = [sum(jnp.dot(p[g][b], dom[g], preferred_element_type=F32) for g in heads) for b in blocks]
        dqs = [dq_t[g].T * SCALE for g in heads]
        for pair in range(G // 2):
            dq_ref[:, pair * lanes:(pair + 1) * lanes] = jnp.where(
                lane < HEAD_DIM, dqs[2 * pair], dqs[2 * pair + 1]).astype(dq_ref.dtype)
        cur_off = pl.multiple_of(n * BLOCK, BLOCK)
        prev_off = pl.multiple_of(jnp.maximum(n - 1, 0) * BLOCK, BLOCK)
        for acc, ref in ((dk, dk_ref), (dv, dv_ref)):
            tot = [a + pltpu.roll(a, HEAD_DIM, 1) for a in acc]
            ref[pl.ds(cur_off, BLOCK), :] += tot[0]
            ref[pl.ds(prev_off, BLOCK), :] += tot[1]
            ref[0:BLOCK, :] += tot[2]

    qblk, keys, vals, bias, smem = _swa_specs()
    kvfull = pl.BlockSpec((None, T, lanes), lambda kv, n: (kv, 0, 0))
    dsk = pl.BlockSpec((G, 1, BLOCK), lambda kv, n: (kv, 0, 0))
    grid = (SWA_KV_HEADS, nb)
    body, x_in, x_in_specs, x_out, x_out_specs, x_scr = _carry(ex, grid, 12, 7, body)
    tile = jax.ShapeDtypeStruct((Hq, BLOCK, BLOCK), F32)
    return pl.pallas_call(
        body,
        out_shape=(jax.ShapeDtypeStruct((T, Hq * HEAD_DIM), BF16),
                   jax.ShapeDtypeStruct((SWA_KV_HEADS, T, lanes), F32),
                   jax.ShapeDtypeStruct((SWA_KV_HEADS, T, lanes), F32),
                   tile, tile, tile, jax.ShapeDtypeStruct((Hq, 1, BLOCK), F32), *x_out),
        grid=grid,
        in_specs=[qblk] + keys + vals + [qblk, bias, bias, smem, smem] + x_in_specs,
        out_specs=(qblk, kvfull, kvfull, bias, bias, bias, dsk, *x_out_specs),
        scratch_shapes=x_scr,
        compiler_params=_params(("arbitrary", "arbitrary")), name=name,
    )(proj, proj, proj, proj, proj, proj, proj, dmix, bc, bp, far, sinks, *x_in)


def _small_grads(dbc, dbp, dbf, dsk, oh_cur, oh_prev, *, name):
    Hq = dbc.shape[0]

    def body(dbc_ref, dbp_ref, dbf_ref, dsk_ref, oc_ref, op_ref, tab_ref, sink_ref):
        tab = (jnp.dot(dbc_ref[...], oc_ref[...], precision=HIGHEST, preferred_element_type=F32)
               + jnp.dot(dbp_ref[...], op_ref[...], precision=HIGHEST, preferred_element_type=F32))
        far = jnp.sum(dbf_ref[...], axis=1, keepdims=True)
        last = lax.broadcasted_iota(jnp.int32, (Hq, N_BUCKETS), 1) == N_BUCKETS - 1
        tab_ref[...] = tab + jnp.where(last, far, 0.0)
        sink_ref[...] = jnp.sum(dsk_ref[...], axis=1, keepdims=True)

    vm = pl.BlockSpec(memory_space=pltpu.VMEM)
    return pl.pallas_call(
        body, out_shape=(jax.ShapeDtypeStruct((Hq, N_BUCKETS), F32), jax.ShapeDtypeStruct((Hq, 1), F32)),
        in_specs=[vm] * 6, out_specs=(vm, vm), compiler_params=_params(), name=name,
    )(dbc.reshape(Hq, -1), dbp.reshape(Hq, -1), dbf.reshape(Hq, -1), dsk.reshape(Hq, -1), oh_cur, oh_prev)


def _coords():
    return lax.axis_index("x"), lax.axis_index("y"), lax.axis_index("c")


class _Exchange:
    def __init__(self, inputs, out_shapes, scratch, start, finish):
        self.inputs, self.out_shapes, self.scratch, self.start, self.finish = inputs, out_shapes, scratch, start, finish


def _carry(ex, grid, n_in, n_out, body):
    if ex is None:
        return body, [], [], [], [], []
    ni, no = len(ex.inputs), len(ex.out_shapes)

    def at_step(which):
        cond = None
        for axis, n in enumerate(grid):
            c = pl.program_id(axis) == (0 if which == "first" else n - 1)
            cond = c if cond is None else cond & c
        return cond

    def wrapped(*refs):
        refs = list(refs)
        n_own_scr = len(refs) - (n_in + ni + n_out + no) - len(ex.scratch)
        own_in, side_in = refs[:n_in], refs[n_in:n_in + ni]
        own_out = refs[n_in + ni:n_in + ni + n_out]
        side_out = refs[n_in + ni + n_out:n_in + ni + n_out + no]
        rest = refs[n_in + ni + n_out + no:]
        own_scr, sems = rest[:n_own_scr], rest[n_own_scr:]

        @pl.when(at_step("first"))
        def _():
            ex.start(side_in, side_out, sems)

        body(*own_in, *own_out, *own_scr)

        @pl.when(at_step("last"))
        def _():
            ex.finish(side_in, side_out, sems)

    hbm = pl.BlockSpec(memory_space=pl.ANY)
    return wrapped, list(ex.inputs), [hbm] * ni, list(ex.out_shapes), [hbm] * no, list(ex.scratch)


def _run_exchange(ex, *, name):
    ni, no = len(ex.inputs), len(ex.out_shapes)

    def body(*refs):
        ins, outs, sems = refs[:ni], refs[ni:ni + no], refs[ni + no:]
        ex.start(ins, outs, sems)
        ex.finish(ins, outs, sems)

    hbm = pl.BlockSpec(memory_space=pl.ANY)
    return pl.pallas_call(
        body, out_shape=tuple(ex.out_shapes), in_specs=[hbm] * ni, out_specs=tuple([hbm] * no),
        scratch_shapes=ex.scratch, compiler_params=_params(), name=name)(*ex.inputs)


def _gather_exchange(shards):
    nt = len(shards)

    def copies(ins, outs, sems):
        send_sems, recv_sems, local_sems = sems
        x, y, c = _coords()
        me, sibling = (x, y, c), (x, y, 1 - c)
        chips = [(1 - x, y), (x, 1 - y), (1 - x, 1 - y)]

        def slot(t, dev):
            return outs[t].at[4 * dev[0] + 2 * dev[1] + dev[2]]

        def copy(t, k, block, to, src=None):
            dst = slot(t, block)
            return pltpu.make_async_remote_copy(
                src_ref=dst if src is None else src, dst_ref=dst,
                send_sem=send_sems.at[t, k], recv_sem=recv_sems.at[t, k], device_id=to, device_id_type=MESH)

        mine = [pltpu.make_async_copy(ins[t], slot(t, me), local_sems.at[t]) for t in range(nt)]
        first = []
        for t in range(nt):
            first.append(copy(t, 0, me, sibling, src=ins[t]))
            first += [copy(t, 1 + j, me, (*chip, c), src=ins[t]) for j, chip in enumerate(chips)]
        return copy, mine, first, me, sibling, chips, c

    def start(ins, outs, sems):
        _, mine, first, *_ = copies(ins, outs, sems)
        for cp in mine + first:
            cp.start()

    def finish(ins, outs, sems):
        copy, mine, first, me, sibling, chips, c = copies(ins, outs, sems)
        passed = []
        for j, chip in enumerate(chips):
            for t in range(nt):
                copy(t, 1 + j, (*chip, c), me).wait_recv()
                cp = copy(t, 4 + j, (*chip, c), sibling)
                cp.start()
                passed.append(cp)
        for t in range(nt):
            copy(t, 0, sibling, me).wait_recv()
            for j, chip in enumerate(chips):
                copy(t, 4 + j, (*chip, 1 - c), me).wait_recv()
        for cp in first + passed:
            cp.wait_send()
        for cp in mine:
            cp.wait()

    return _Exchange(
        list(shards), [jax.ShapeDtypeStruct((N_DEV,) + s.shape, s.dtype) for s in shards],
        [pltpu.SemaphoreType.DMA((nt, 7)), pltpu.SemaphoreType.DMA((nt, 7)), pltpu.SemaphoreType.DMA((nt,))],
        start, finish)


def _swap_exchange(arrays, n_slices, copies):
    nt = len(arrays)

    def start(ins, outs, sems):
        for cp in copies(ins, outs, sems):
            cp.start()

    def finish(ins, outs, sems):
        sends = copies(ins, outs, sems)
        for cp in sends:
            cp.wait_recv()
        for cp in sends:
            cp.wait_send()

    return _Exchange(
        list(arrays), [jax.ShapeDtypeStruct((n_slices,) + a.shape[1:], a.dtype) for a in arrays],
        [pltpu.SemaphoreType.DMA((nt, n_slices)), pltpu.SemaphoreType.DMA((nt, n_slices))], start, finish)


def _cores_exchange(gs):
    def copies(ins, outs, sems):
        send_sems, recv_sems = sems
        x, y, c = _coords()
        return [pltpu.make_async_remote_copy(
            src_ref=ins[t].at[2 * j + (1 - c)], dst_ref=outs[t].at[j],
            send_sem=send_sems.at[t, j], recv_sem=recv_sems.at[t, j], device_id=(x, y, 1 - c), device_id_type=MESH)
            for t in range(len(gs)) for j in range(4)]

    return _swap_exchange(gs, 4, copies)


def _chips_exchange(ps):
    def copies(ins, outs, sems):
        send_sems, recv_sems = sems
        x, y, c = _coords()
        peers = [(1 - x, y), (x, 1 - y), (1 - x, 1 - y)]
        return [pltpu.make_async_remote_copy(
            src_ref=ins[t].at[2 * px + py], dst_ref=outs[t].at[k],
            send_sem=send_sems.at[t, k], recv_sem=recv_sems.at[t, k], device_id=(px, py, c), device_id_type=MESH)
            for t in range(len(ps)) for k, (px, py) in enumerate(peers)]

    return _swap_exchange(ps, 3, copies)


def _add_cores(g, r, core, *, name):
    _, A, B = g.shape
    ta = _tile(A, 512, 16)

    def body(core_ref, a_ref, b_ref, o_ref, o16_ref):
        s = a_ref[...] + b_ref[...]
        o_ref[...] = s
        o16_ref[...] = s.astype(BF16)

    blk = (None, ta, B)
    out = pl.BlockSpec(blk, lambda j, i, core_ref: (j, i, 0))
    return pl.pallas_call(
        body, out_shape=(jax.ShapeDtypeStruct((4, A, B), F32), jax.ShapeDtypeStruct((4, A, B), BF16)),
        grid_spec=pltpu.PrefetchScalarGridSpec(
            num_scalar_prefetch=1, grid=(4, A // ta),
            in_specs=[pl.BlockSpec(blk, lambda j, i, core_ref: (2 * j + core_ref[0], i, 0)),
                      pl.BlockSpec(blk, lambda j, i, core_ref: (j, i, 0))],
            out_specs=(out, out)),
        compiler_params=_params(("parallel", "parallel")), name=name)(core, g, r)


def _adamw_math(w, g, m, v):
    m = ADAM_B1 * m + (1.0 - ADAM_B1) * g
    v = ADAM_B2 * v + (1.0 - ADAM_B2) * (g * g)
    m_hat = m / (1.0 - ADAM_B1 ** ADAM_STEP)
    v_hat = v / (1.0 - ADAM_B2 ** ADAM_STEP)
    delta = -ADAM_LR * (m_hat / (jnp.sqrt(v_hat) + ADAM_EPS) + ADAM_WD * w)
    return delta, m, v


def _sum_adamw(p, r, chip, w, m, v, *, segs, ta, name):
    Aw, Bw = w.shape
    Bg = p.shape[2]
    assert Aw % ta == 0

    def body(chip_ref, p_ref, r0, r1, r2, w_ref, m_ref, v_ref, g_out, d_out, m_out, v_out):
        for gc, wc, n in segs:
            g = ((p_ref[:, gc:gc + n] + r0[:, gc:gc + n].astype(F32)) + r1[:, gc:gc + n].astype(F32)
                 ) + r2[:, gc:gc + n].astype(F32)
            delta, m_new, v_new = _adamw_math(w_ref[:, wc:wc + n], g, m_ref[:, wc:wc + n], v_ref[:, wc:wc + n])
            g_out[:, wc:wc + n] = g
            d_out[:, wc:wc + n] = delta
            m_out[:, wc:wc + n] = m_new
            v_out[:, wc:wc + n] = v_new

    gblk = (None, ta, Bg)
    row = pl.BlockSpec((ta, Bw), lambda i, chip_ref: (i, 0))
    rspecs = [pl.BlockSpec(gblk, (lambda i, chip_ref, k=k: (k, i, 0))) for k in range(3)]
    shp = jax.ShapeDtypeStruct((Aw, Bw), F32)
    return pl.pallas_call(
        body, out_shape=(shp, shp, shp, shp),
        grid_spec=pltpu.PrefetchScalarGridSpec(
            num_scalar_prefetch=1, grid=(Aw // ta,),
            in_specs=[pl.BlockSpec(gblk, lambda i, chip_ref: (chip_ref[0], i, 0))] + rspecs + [row, row, row],
            out_specs=(row, row, row, row)),
        compiler_params=_params(("parallel",)), name=name)(chip, p, r, r, r, w, m, v)


def _adamw(w, g, m, v, *, name):
    def body(w_ref, g_ref, m_ref, v_ref, d_out, m_out, v_out):
        delta, m_new, v_new = _adamw_math(w_ref[...], g_ref[...], m_ref[...], v_ref[...])
        d_out[...] = delta
        m_out[...] = m_new
        v_out[...] = v_new

    vm = pl.BlockSpec(memory_space=pltpu.VMEM)
    shp = jax.ShapeDtypeStruct(w.shape, F32)
    return pl.pallas_call(body, out_shape=(shp, shp, shp), in_specs=[vm] * 4, out_specs=(vm, vm, vm),
                          compiler_params=_params(), name=name)(w, g, m, v)


def _small_allreduce_adamw(s, w, m, v, *, name):
    R, W = s.shape

    def body(s_ref, w_ref, m_ref, v_ref, g_out, d_out, m_out, v_out, gath, send_sems, recv_sems):
        x, y, c = _coords()
        mine = 4 * x + 2 * y + c
        gath[mine] = s_ref[...]
        peers = [((1 - x) if k & 4 else x, (1 - y) if k & 2 else y, (1 - c) if k & 1 else c) for k in range(1, N_DEV)]
        sends = []
        for k in range(1, N_DEV):
            peer = peers[k - 1]
            sends.append(pltpu.make_async_remote_copy(
                src_ref=s_ref, dst_ref=gath.at[mine], send_sem=send_sems.at[k - 1], recv_sem=recv_sems.at[k - 1],
                device_id=peer, device_id_type=MESH))
        for cp in sends:
            cp.start()
        for k in range(1, N_DEV):
            peer = peers[k - 1]
            pltpu.make_async_remote_copy(
                src_ref=s_ref, dst_ref=gath.at[4 * peer[0] + 2 * peer[1] + peer[2]],
                send_sem=send_sems.at[k - 1], recv_sem=recv_sems.at[k - 1],
                device_id=peer, device_id_type=MESH).wait_recv()
        for cp in sends:
            cp.wait_send()
        g = gath[0]
        for d in range(1, N_DEV):
            g = g + gath[d]
        delta, m_new, v_new = _adamw_math(w_ref[...], g, m_ref[...], v_ref[...])
        g_out[...] = g
        d_out[...] = delta
        m_out[...] = m_new
        v_out[...] = v_new

    vm = pl.BlockSpec(memory_space=pltpu.VMEM)
    shp = jax.ShapeDtypeStruct((R, W), F32)
    return pl.pallas_call(
        body, out_shape=(shp, shp, shp, shp), in_specs=[vm] * 4, out_specs=(vm, vm, vm, vm),
        scratch_shapes=[pltpu.VMEM((N_DEV, R, W), F32), pltpu.SemaphoreType.DMA((N_DEV - 1,)),
                        pltpu.SemaphoreType.DMA((N_DEV - 1,))],
        compiler_params=_params(), name=name)(s, w, m, v)


def _pack_small(rel_bias, g1, g2, g3, g4, b_forget, sinks, extra=None, meta=None):
    misc = jnp.concatenate([rel_bias.reshape(-1), b_forget.reshape(-1), sinks.reshape(-1)])
    misc = jnp.concatenate([misc, jnp.zeros((D_MODEL - misc.shape[0],), F32)])[None]
    last = jnp.zeros((1, D_MODEL), F32) if extra is None else extra
    meta = jnp.zeros((N_META, D_MODEL), F32) if meta is None else meta
    return jnp.concatenate([g1, g2, g3, g4, misc, last, jnp.zeros((2, D_MODEL), F32), meta], axis=0)


def _unpack_small(p):
    nrb = N_BUCKETS * SWA_Q_HEADS
    misc = p[4]
    return dict(rel_bias=misc[:nrb].reshape(N_BUCKETS, SWA_Q_HEADS), ln_pre_mix=p[0:1], ln_post_mix=p[1:2],
                ln_pre_ffn=p[2:3], ln_post_ffn=p[3:4], b_forget=misc[nrb:nrb + 8].reshape(1, 8),
                sinks=misc[nrb + 8:nrb + 16].reshape(1, 8))


def _shard_order(pieces, shard, pad):
    T, dtype = pieces[0].shape[0], pieces[0].dtype
    total = sum(p.shape[1] for p in pieces)
    assert total % shard == 0
    out, zeros = [], jnp.zeros((T, pad), dtype)
    for s in range(total // shard):
        lo, hi, start = s * shard, (s + 1) * shard, 0
        for p in pieces:
            end = start + p.shape[1]
            if max(lo, start) < min(hi, end):
                out.append(p[:, max(lo, start) - start:min(hi, end) - start])
            start = end
        out.append(zeros)
    return jnp.concatenate(out, axis=1)


def _unheads(a):
    return a.transpose(1, 0, 2).reshape(a.shape[1], -1)


def kernel(x, meta_tokens, rel_bias, ln_pre_mix, ln_post_mix, ln_pre_ffn, ln_post_ffn, w_in, b_forget, sinks, w_out, w_gate_up, w_down, loss_target, m_meta_tokens, m_rel_bias, m_ln_pre_mix, m_ln_post_mix, m_ln_pre_ffn, m_ln_post_ffn, m_w_in, m_b_forget, m_sinks, m_w_out, m_w_gate_up, m_w_down, v_meta_tokens, v_rel_bias, v_ln_pre_mix, v_ln_post_mix, v_ln_pre_ffn, v_ln_post_ffn, v_w_in, v_b_forget, v_sinks, v_w_out, v_w_gate_up, v_w_down):
    seq = x.shape[1]
    T = BLOCK + seq
    assert T % FOX_TILE == 0
    nq = T // FOX_TILE
    tm = _tile(T, 1056)
    cin = w_in.shape[2]
    hid = w_down.shape[1]
    assert w_gate_up.shape[2] == 2 * hid and cin <= W_IN_PAD and hid <= HID_PAD

    x_i, y_i, c_i = _coords()
    core = jnp.reshape(c_i, (1,)).astype(jnp.int32)
    chip = jnp.reshape(2 * x_i + y_i, (1,)).astype(jnp.int32)
    w_in_s = jnp.pad(w_in[0].astype(BF16), ((0, 0), (0, W_IN_PAD - cin)))
    w_gu_s = jnp.pad(w_gate_up[0].astype(BF16).reshape(D_MODEL, 2, hid), ((0, 0), (0, 0), (0, HID_PAD - hid)))
    w_gu_s = w_gu_s.reshape(D_MODEL, 2 * HID_PAD)
    w_down_s = jnp.pad(w_down[0].astype(BF16), ((0, HID_PAD - hid), (0, 0)))
    g_in, g_meta = _run_exchange(_gather_exchange([w_in_s, meta_tokens]), name="ag_w_in")
    gather_rest = _gather_exchange([w_out[0].astype(BF16), w_gu_s, w_down_s])
    w_in_full = g_in[:, :, :cin].transpose(1, 0, 2).reshape(D_MODEL, N_DEV * cin)
    w_qkv = w_in_full[:, :D_QKV]
    w_f = jnp.pad(w_in_full[:, D_QKV:], ((0, 0), (0, BLOCK - FOX_HEADS)))
    meta_full = g_meta.transpose(1, 0, 2).reshape(N_META, D_MODEL)

    h0 = jnp.concatenate([jnp.zeros((PAD_ROWS, D_MODEL), F32), meta_full, x[0]], axis=0)
    target = jnp.concatenate([jnp.zeros((BLOCK, D_MODEL), F32), loss_target[0]], axis=0)
    hn1, hn1_t = _rms_fwd(h0, ln_pre_mix, name="rms_pre_mix")
    proj = _matmul(hn1, w_qkv, out_dtype=BF16, tm=tm, tn=768, name="mm_in_proj")
    proj_f = _matmul(hn1, w_f, out_dtype=F32, tm=tm, tn=BLOCK, name="mm_in_proj_f")

    f_t = proj_f[:, :FOX_HEADS].T
    bf_col = b_forget.reshape(FOX_HEADS, 1)

    oh_cur, oh_prev = _bucket_onehots()
    bias_c = jnp.einsum("pb,bh->hp", jnp.asarray(oh_cur), rel_bias, precision=HIGHEST).reshape(8, BLOCK, BLOCK)
    bias_p = jnp.einsum("pb,bh->hp", jnp.asarray(oh_prev), rel_bias, precision=HIGHEST).reshape(8, BLOCK, BLOCK)
    far = rel_bias[N_BUCKETS - 1]
    sink_v = sinks[0]
    mix_a = _swa_fwd(proj, bias_c, bias_p, far, sink_v, name="swa_fwd")

    _, cum_col = _fox_gates_fwd(f_t, bf_col, name="fox_gates_fwd")
    q_b, k_b, v_b = _fox_prep(proj, cum_col, name="fox_prep")
    mix, lse_row, g_out, g_gu, g_down = _fox_fwd(q_b, k_b, v_b, mix_a, ex=gather_rest, name="fox_fwd")
    w_out_full = g_out.reshape(D_MODEL, D_MODEL)
    w_down_full = g_down.reshape(N_DEV * HID_PAD, D_MODEL)

    a1 = _matmul(mix, w_out_full, out_dtype=F32, tm=tm, tn=512, name="mm_out_proj")
    h1, hn2, hn2_t = _post_res_norm(a1, ln_post_mix, h0, ln_pre_ffn, name="post_mix_pre_ffn")
    gate, up, act, act_t = _gate_up_swiglu(hn2, g_gu, name="mm_gate_up")
    ff = _matmul(act, w_down_full, out_dtype=F32, tm=tm, tn=512, name="mm_down")
    dh2, dff, dg_post_ffn, loss_acc = _loss_head(ff, ln_post_ffn, h1, target, name="loss_head")

    dgu = _d_act_swiglu(dff, w_down_full, gate, up, name="mm_d_act")
    d_w_down = _matmul(act_t, dff, out_dtype=F32, tm=768, tn=512, name="mm_dw_down")
    dhn2 = _matmul(dgu, g_gu, nt=True, b_shards=True, out_dtype=F32, tm=_tile(T, 528), tn=512, name="mm_d_hn2")
    d_w_gu = _matmul(hn2_t, dgu, out_shards=True, out_dtype=F32, tm=512, tn=2 * HID_PAD, name="mm_dw_gate_up")
    dh1, dg_pre_ffn, da1, dg_post_mix = _rms_bwd(h1, ln_pre_ffn, dhn2, dh2, out_dtype=F32,
                                                 then=(a1, ln_post_mix), name="rms_bwd_pre_ffn_post_mix")
    dmix = _matmul(da1, w_out_full, nt=True, out_dtype=BF16, tm=tm, tn=512, name="mm_d_mix")
    d_w_out = _matmul(mix.T, da1, out_dtype=F32, tm=512, tn=512, name="mm_dw_out")

    ffn_grads = [d_w_out.reshape(N_DEV, -1, D_MODEL), d_w_gu, d_w_down.reshape(N_DEV, HID_PAD, D_MODEL)]
    dq_a, dk_a, dv_a, dbc, dbp, dbf, dsk, *ffn_sibling = _swa_bwd(
        proj, dmix, bias_c, bias_p, far, sink_v, ex=_cores_exchange(ffn_grads), name="swa_bwd")
    low = (jnp.arange(2 * HEAD_DIM) < HEAD_DIM)[None, :]
    dk_a = jnp.where(low, dk_a[0], dk_a[1]).astype(BF16)
    dv_a = jnp.where(low, dv_a[0], dv_a[1]).astype(BF16)
    d_tab, d_sink = _small_grads(dbc, dbp, dbf, dsk, jnp.asarray(oh_cur), jnp.asarray(oh_prev), name="small_grads")
    ffn_sums = [_add_cores(g, r, core, name="rs_add_" + t)
                for g, r, t in zip(ffn_grads, ffn_sibling, ["w_out", "w_gate_up", "w_down"])]

    do_b = _fox_prep_bwd(dmix, mix, name="fox_prep_bwd")
    dq_b, dk_b, dv_b, dcq, dck, *ffn_chips = _fox_bwd(
        q_b, k_b, v_b, do_b, lse_row, ex=_chips_exchange([s[1] for s in ffn_sums]), name="fox_bwd")
    df_t, d_bf = _fox_gates_bwd(dcq.reshape(FOX_HEADS, T), dck.reshape(FOX_HEADS, T), f_t, bf_col,
                                name="fox_gates_bwd")

    dproj_s = _shard_order([dq_a, dk_a, dv_a, dq_b, dk_b, dv_b, df_t.T.astype(BF16)], cin, W_IN_PAD - cin)
    d_w_in = _matmul(hn1_t, dproj_s, out_shards=True, out_dtype=F32, tm=512, tn=W_IN_PAD, name="mm_dw_in")
    dhn1, in_sibling = _matmul(dproj_s, g_in, nt=True, b_shards=True, out_dtype=F32, tm=tm, tn=512,
                               ex=_cores_exchange([d_w_in]), name="mm_d_hn1")
    in_sum = _add_cores(d_w_in, in_sibling, core, name="rs_add_w_in")
    dh0, dg_pre_mix, in_chips = _rms_bwd(h0, ln_pre_mix, dhn1, dh1, out_dtype=F32,
                                         ex=_chips_exchange([in_sum[1]]), name="rms_bwd_pre_mix")
    grad_x = dh0[BLOCK:][None]
    d_meta = dh0[PAD_ROWS:BLOCK]

    tags = ["w_in", "w_out", "w_gate_up", "w_down"]
    chip_sum = [in_sum[0]] + [s[0] for s in ffn_sums]
    from_chips = [in_chips] + list(ffn_chips)
    shard_w = [(w_in, m_w_in, v_w_in), (w_out, m_w_out, v_w_out), (w_gate_up, m_w_gate_up, v_w_gate_up),
               (w_down, m_w_down, v_w_down)]
    segs = [[(0, 0, cin)], [(0, 0, D_MODEL)], [(0, 0, hid), (HID_PAD, hid, hid)], [(0, 0, D_MODEL)]]
    tas = [256, BLOCK, 256, hid]
    big = [{}, {}, {}, {}]
    for i, t in enumerate(tags):
        w_t, m_t, v_t = shard_w[i]
        res = _sum_adamw(chip_sum[i], from_chips[i], chip, w_t[0], m_t[0], v_t[0], segs=segs[i], ta=tas[i],
                         name="rs_adamw_" + t)
        for kind in range(4):
            big[kind][t] = res[kind][None]

    loss_row = jnp.pad(loss_acc[0:1, 0:1] * (0.5 / D_MODEL), ((0, 0), (0, D_MODEL - 1)))
    s_small = _pack_small(d_tab.T, dg_pre_mix, dg_post_mix, dg_pre_ffn, dg_post_ffn, d_bf, d_sink,
                          extra=loss_row, meta=d_meta)
    w_s = _pack_small(rel_bias, ln_pre_mix, ln_post_mix, ln_pre_ffn, ln_post_ffn, b_forget, sinks)
    m_s = _pack_small(m_rel_bias, m_ln_pre_mix, m_ln_post_mix, m_ln_pre_ffn, m_ln_post_ffn, m_b_forget, m_sinks)
    v_s = _pack_small(v_rel_bias, v_ln_pre_mix, v_ln_post_mix, v_ln_pre_ffn, v_ln_post_ffn, v_b_forget, v_sinks)
    small = _small_allreduce_adamw(s_small, w_s, m_s, v_s, name="small_allreduce_adamw")
    loss = small[0][5, 0]
    mcols = meta_tokens.shape[1]
    g_meta_mine = lax.dynamic_slice(small[0][8:8 + N_META], (0, (4 * x_i + 2 * y_i + c_i) * mcols), (N_META, mcols))
    big[0]["meta_tokens"] = g_meta_mine
    for kind, arr in enumerate(_adamw(meta_tokens, g_meta_mine, m_meta_tokens, v_meta_tokens, name="adamw_meta")):
        big[kind + 1]["meta_tokens"] = arr
    small = [_unpack_small(p) for p in small]

    names = ["meta_tokens", "rel_bias", "ln_pre_mix", "ln_post_mix", "ln_pre_ffn", "ln_post_ffn", "w_in",
             "b_forget", "sinks", "w_out", "w_gate_up", "w_down"]
    outs = [loss, grad_x]
    for kind in range(4):
        for nme in names:
            outs.append(big[kind][nme] if nme in big[kind] else small[kind][nme])
    return tuple(outs)
```

```python
import math

import numpy as np
import jax
import jax.numpy as jnp
from jax import lax
from jax.experimental import pallas as pl
from jax.experimental.pallas import tpu as pltpu

F32 = jnp.float32
BF16 = jnp.bfloat16
HIGHEST = lax.Precision.HIGHEST
MESH = pl.DeviceIdType.MESH

N_DEV = 8
D_MODEL = 1024
N_META = 16
HEAD_DIM = 64
SWA_Q_HEADS = 8
SWA_KV_HEADS = 2
SWA_GROUP = 4
FOX_HEADS = 8
FOX_W = FOX_HEADS * HEAD_DIM
SWA_Q_W = SWA_Q_HEADS * HEAD_DIM
BLOCK = 128
PAD_ROWS = BLOCK - N_META
N_BUCKETS = 32
MAX_DISTANCE = 128
D_FF = 2816
D_QKV = 2304
D_PROJ = D_QKV + FOX_HEADS
D_PROJ_PAD = 2560
EPS = 1e-6
NEG = -1e30
SCALE = HEAD_DIM ** -0.5
ADAM_LR, ADAM_B1, ADAM_B2, ADAM_EPS, ADAM_WD, ADAM_STEP = 0.001, 0.9, 0.999, 1e-08, 0.01, 10
VMEM_LIMIT = 56 * 1024 * 1024
FOX_TILE = 384
FOX_GROUP = 4
W_IN_PAD = 384
HID_PAD = 384

NT = (((1,), (1,)), ((), ()))
NN = (((1,), (0,)), ((), ()))
TN = (((0,), (0,)), ((), ()))


def _params(sem=None, **kw):
    if sem is not None:
        kw["dimension_semantics"] = sem
    return pltpu.CompilerParams(vmem_limit_bytes=VMEM_LIMIT, **kw)


def _tile(n, target, mult=16):
    best = None
    for t in range(mult, min(n, target) + 1, mult):
        if n % t == 0:
            best = t
    assert best is not None, (n, target)
    return best


def _matmul(a, b, *, nt=False, b_shards=False, out_shards=False, out_dtype, tm, tn=None, tk=None, ex=None, name):
    M, K = a.shape
    k_shards = b.shape[0] if (b_shards and nt) else 0
    if k_shards:
        N, ks = b.shape[1], b.shape[2]
        assert tk is None and K == k_shards * ks
    elif b_shards:
        N, tn = b.shape[0] * b.shape[2], b.shape[2]
    else:
        N = b.shape[0] if nt else b.shape[1]
    tk = K if tk is None else tk
    assert M % tm == 0 and N % tn == 0 and K % tk == 0, (name, a.shape, b.shape, tm, tn, tk)
    nk = K // tk
    dn = NT if nt else NN

    def body(a_ref, b_ref, o_ref, *scr):
        if k_shards:
            part = sum(lax.dot_general(a_ref[:, s * ks:(s + 1) * ks], b_ref[s], NT, preferred_element_type=F32)
                       for s in range(k_shards))
        else:
            part = lax.dot_general(a_ref[...], b_ref[...], dn, preferred_element_type=F32)
        if nk == 1:
            o_ref[...] = part.astype(o_ref.dtype)
        else:
            acc = scr[0]
            k = pl.program_id(2)

            @pl.when(k == 0)
            def _():
                acc[...] = part

            @pl.when(k > 0)
            def _():
                acc[...] += part

            @pl.when(k == nk - 1)
            def _():
                o_ref[...] = acc[...].astype(o_ref.dtype)

    if k_shards:
        b_spec = pl.BlockSpec((k_shards, tn, ks), lambda i, j, k: (0, j, 0))
    elif b_shards:
        b_spec = pl.BlockSpec((None, tk, tn), lambda i, j, k: (j, k, 0))
    elif nt:
        b_spec = pl.BlockSpec((tn, tk), lambda i, j, k: (j, k))
    else:
        b_spec = pl.BlockSpec((tk, tn), lambda i, j, k: (k, j))
    if out_shards:
        out_shape = jax.ShapeDtypeStruct((N // tn, M, tn), out_dtype)
        out_spec = pl.BlockSpec((None, tm, tn), lambda i, j, k: (j, i, 0))
    else:
        out_shape = jax.ShapeDtypeStruct((M, N), out_dtype)
        out_spec = pl.BlockSpec((tm, tn), lambda i, j, k: (i, j))
    grid = (M // tm, N // tn, nk)
    body, x_in, x_in_specs, x_out, x_out_specs, x_scr = _carry(ex, grid, 2, 1, body)
    res = pl.pallas_call(
        body,
        out_shape=(out_shape, *x_out),
        grid=grid,
        in_specs=[pl.BlockSpec((tm, tk), lambda i, j, k: (i, k)), b_spec] + x_in_specs,
        out_specs=(out_spec, *x_out_specs),
        scratch_shapes=([pltpu.VMEM((tm, tn), F32)] if nk > 1 else []) + x_scr,
        compiler_params=_params(("parallel", "parallel", "arbitrary") if ex is None else ("arbitrary",) * 3),
        name=name,
    )(a, b, *x_in)
    return res[0] if ex is None else res


def _rstd(x):
    return lax.rsqrt(jnp.mean(x * x, axis=-1, keepdims=True) + EPS)


def _rms_fwd(x, g, *, name):
    T, D = x.shape
    tm = _tile(T, 512)

    def body(x_ref, g_ref, o_ref, ot_ref):
        x = x_ref[...]
        y = x * _rstd(x) * g_ref[...]
        o_ref[...] = y.astype(o_ref.dtype)
        ot_ref[...] = y.T.astype(ot_ref.dtype)

    return pl.pallas_call(
        body, out_shape=(jax.ShapeDtypeStruct((T, D), BF16), jax.ShapeDtypeStruct((D, T), BF16)), grid=(T // tm,),
        in_specs=[pl.BlockSpec((tm, D), lambda i: (i, 0)), pl.BlockSpec((1, D), lambda i: (0, 0))],
        out_specs=(pl.BlockSpec((tm, D), lambda i: (i, 0)), pl.BlockSpec((D, tm), lambda i: (0, i))),
        compiler_params=_params(("parallel",)), name=name)(x, g)


def _post_res_norm(a, g_post, h, g_pre, *, name):
    T, D = a.shape
    tm = _tile(T, 384, BLOCK)

    def body(a_ref, gp_ref, h_ref, gn_ref, h1_ref, o_ref, ot_ref):
        a = a_ref[...]
        h1 = h_ref[...] + a * _rstd(a) * gp_ref[...]
        h1_ref[...] = h1
        y = h1 * _rstd(h1) * gn_ref[...]
        o_ref[...] = y.astype(o_ref.dtype)
        ot_ref[...] = y.T.astype(ot_ref.dtype)

    row = pl.BlockSpec((tm, D), lambda i: (i, 0))
    vec = pl.BlockSpec((1, D), lambda i: (0, 0))
    return pl.pallas_call(
        body, out_shape=(jax.ShapeDtypeStruct((T, D), F32), jax.ShapeDtypeStruct((T, D), BF16),
                         jax.ShapeDtypeStruct((D, T), BF16)), grid=(T // tm,),
        in_specs=[row, vec, row, vec], out_specs=(row, row, pl.BlockSpec((D, tm), lambda i: (0, i))),
        compiler_params=_params(("parallel",)), name=name)(a, g_post, h, g_pre)


def _loss_head(a, g, h, target, *, name):
    T, D = a.shape
    tm = _tile(T, 512)

    def body(a_ref, g_ref, h_ref, t_ref, dy_ref, da_ref, dg_ref, loss_ref):
        i = pl.program_id(0)
        a = a_ref[...]
        r = _rstd(a)
        ah = a * r
        y = h_ref[...] + ah * g_ref[...]
        rows = i * tm + lax.broadcasted_iota(jnp.int32, (tm, 1), 0)
        err = jnp.where(rows >= BLOCK, y - t_ref[...], 0.0)
        dy = err / D
        dy_ref[...] = dy
        dah = dy * g_ref[...]
        da_ref[...] = (r * (dah - ah * jnp.mean(dah * ah, axis=-1, keepdims=True))).astype(da_ref.dtype)
        part = jnp.sum(jnp.sum(err * err, axis=1, keepdims=True), axis=0, keepdims=True)

        @pl.when(i == 0)
        def _():
            loss_ref[...] = jnp.zeros_like(loss_ref)
            dg_ref[...] = jnp.zeros_like(dg_ref)

        loss_ref[...] += jnp.broadcast_to(part, loss_ref.shape)
        dg_ref[...] += jnp.sum(dy * ah, axis=0, keepdims=True)

    row = pl.BlockSpec((tm, D), lambda i: (i, 0))
    vec = pl.BlockSpec((1, D), lambda i: (0, 0))
    return pl.pallas_call(
        body, out_shape=(jax.ShapeDtypeStruct((T, D), F32), jax.ShapeDtypeStruct((T, D), BF16),
                         jax.ShapeDtypeStruct((1, D), F32), jax.ShapeDtypeStruct((8, 128), F32)),
        grid=(T // tm,),
        in_specs=[row, vec, row, row],
        out_specs=(row, row, vec, pl.BlockSpec((8, 128), lambda i: (0, 0))),
        compiler_params=_params(("arbitrary",)), name=name)(a, g, h, target)


def _rms_bwd(x, g, dy, res, *, out_dtype, dy2=None, then=None, ex=None, name):
    T, D = x.shape
    tm = _tile(T, 512)
    has_res = res is not None
    has_dy2 = dy2 is not None
    n_in = 3 + has_dy2 + has_res + (2 if then is not None else 0)
    n_out = 2 + (2 if then is not None else 0)

    def pull_back(x, g, dy):
        r = _rstd(x)
        xh = x * r
        dxh = dy * g
        return r * (dxh - xh * jnp.mean(dxh * xh, axis=-1, keepdims=True)), jnp.sum(dy * xh, axis=0, keepdims=True)

    def body(*refs):
        ins, outs = refs[:n_in], refs[n_in:]
        i = pl.program_id(0)

        @pl.when(i == 0)
        def _():
            for ref in outs[1::2]:
                ref[...] = jnp.zeros_like(ref)

        dy_all = ins[2][...].astype(F32)
        if has_dy2:
            dy_all = dy_all + ins[3][...].astype(F32)
        dx, dg = pull_back(ins[0][...], ins[1][...], dy_all)
        if has_res:
            dx = dx + ins[3 + has_dy2][...]
        outs[0][...] = dx.astype(outs[0].dtype)
        outs[1][...] += dg
        if then is not None:
            dx2, dg2 = pull_back(ins[n_in - 2][...], ins[n_in - 1][...], dx)
            outs[2][...] = dx2.astype(outs[2].dtype)
            outs[3][...] += dg2

    row = pl.BlockSpec((tm, D), lambda i: (i, 0))
    vec = pl.BlockSpec((1, D), lambda i: (0, 0))
    ins = [x, g, dy] + ([dy2] if has_dy2 else []) + ([res] if has_res else []) + (list(then) if then is not None else [])
    in_specs = ([row, vec, row] + ([row] if has_dy2 else []) + ([row] if has_res else [])
                + ([row, vec] if then is not None else []))
    out_shape = [jax.ShapeDtypeStruct((T, D), out_dtype), jax.ShapeDtypeStruct((1, D), F32)]
    out_specs = [row, vec]
    if then is not None:
        out_shape += [jax.ShapeDtypeStruct((T, D), BF16), jax.ShapeDtypeStruct((1, D), F32)]
        out_specs += [row, vec]
    grid = (T // tm,)
    body, x_in, x_in_specs, x_out, x_out_specs, x_scr = _carry(ex, grid, n_in, n_out, body)
    return pl.pallas_call(
        body, out_shape=(*out_shape, *x_out), grid=grid,
        in_specs=in_specs + x_in_specs, out_specs=(*out_specs, *x_out_specs), scratch_shapes=x_scr,
        compiler_params=_params(("arbitrary",)), name=name)(*ins, *x_in)


def _gate_up_swiglu(a, w, *, name):
    T, D = a.shape
    S, n = w.shape[0] // 2, w.shape[2]
    tm = _tile(T, 1408, BLOCK)

    def body(a_ref, wg_ref, wu_ref, g_ref, u_ref, o_ref, ot_ref):
        x = a_ref[...]
        g = jnp.dot(x, wg_ref[...], preferred_element_type=F32)
        u = jnp.dot(x, wu_ref[...], preferred_element_type=F32)
        g16, u16 = g.astype(BF16), u.astype(BF16)
        g_ref[...] = g16
        u_ref[...] = u16
        gr = g16.astype(F32)
        act = gr / (1.0 + jnp.exp(-gr)) * u16.astype(F32)
        o_ref[...] = act.astype(o_ref.dtype)
        ot_ref[...] = act.T.astype(ot_ref.dtype)

    tile = pl.BlockSpec((tm, n), lambda i, j: (i, j))
    shp = jax.ShapeDtypeStruct((T, S * n), BF16)
    return pl.pallas_call(
        body, out_shape=(shp, shp, shp, jax.ShapeDtypeStruct((S * n, T), BF16)), grid=(T // tm, S),
        in_specs=[pl.BlockSpec((tm, D), lambda i, j: (i, 0)),
                  pl.BlockSpec((None, D, n), lambda i, j: (j, 0, 0)),
                  pl.BlockSpec((None, D, n), lambda i, j: (j + S, 0, 0))],
        out_specs=(tile, tile, tile, pl.BlockSpec((n, tm), lambda i, j: (j, i))),
        compiler_params=_params(("parallel", "parallel")), name=name)(a, w, w)


def _d_act_swiglu(dff, w_down, gate, up, *, name):
    T, D = dff.shape
    F = w_down.shape[0]
    tm = _tile(T, 384)
    tf = _tile(F, 768, BLOCK)

    def body(d_ref, w_ref, g_ref, u_ref, o_ref):
        dy = d_ref[...]
        for c in range(0, F, tf):
            d = lax.dot_general(dy, w_ref[c:c + tf, :], NT, preferred_element_type=F32)
            g = g_ref[:, c:c + tf].astype(F32)
            u = u_ref[:, c:c + tf].astype(F32)
            sg = 1.0 / (1.0 + jnp.exp(-g))
            o_ref[:, c:c + tf] = (d * u * (sg * (1.0 + g * (1.0 - sg)))).astype(o_ref.dtype)
            o_ref[:, F + c:F + c + tf] = (d * (g * sg)).astype(o_ref.dtype)

    row = pl.BlockSpec((tm, F), lambda i: (i, 0))
    return pl.pallas_call(
        body, out_shape=jax.ShapeDtypeStruct((T, 2 * F), BF16), grid=(T // tm,),
        in_specs=[pl.BlockSpec((tm, D), lambda i: (i, 0)), pl.BlockSpec((F, D), lambda i: (0, 0)), row, row],
        out_specs=pl.BlockSpec((tm, 2 * F), lambda i: (i, 0)),
        compiler_params=_params(("parallel",)), name=name)(dff, w_down, gate, up)


def _fox_gates_fwd(f_t, b, *, name):
    H, T = f_t.shape
    nb = T // BLOCK

    def body(f_ref, b_ref, cum_ref, col_ref):
        f = f_ref[...] + b_ref[...]
        ls = jnp.minimum(f, 0.0) - jnp.log(1.0 + jnp.exp(-jnp.abs(f)))
        t = lax.broadcasted_iota(jnp.int32, (H, T), 1)
        ls = jnp.where(t >= PAD_ROWS, ls, 0.0)
        upper = (lax.broadcasted_iota(jnp.int32, (BLOCK, BLOCK), 0)
                 <= lax.broadcasted_iota(jnp.int32, (BLOCK, BLOCK), 1)).astype(F32)
        carry = jnp.zeros((H, 1), F32)
        for blk in range(nb):
            seg = ls[:, blk * BLOCK:(blk + 1) * BLOCK]
            pre = jnp.dot(seg, upper, precision=HIGHEST, preferred_element_type=F32) + carry
            cum_ref[:, blk * BLOCK:(blk + 1) * BLOCK] = pre
            col_ref[blk * BLOCK:(blk + 1) * BLOCK, :] = jnp.concatenate(
                [pre, jnp.zeros((BLOCK - H, BLOCK), F32)], axis=0).T
            carry = pre[:, BLOCK - 1:BLOCK]

    vm = pl.BlockSpec(memory_space=pltpu.VMEM)
    return pl.pallas_call(
        body, out_shape=(jax.ShapeDtypeStruct((H, T), F32), jax.ShapeDtypeStruct((T, BLOCK), F32)),
        in_specs=[vm, vm], out_specs=(vm, vm),
        compiler_params=_params(), name=name)(f_t, b)


def _fox_gates_bwd(dcq, dck, f_t, b, *, name):
    H, T = f_t.shape
    nb = T // BLOCK

    def body(dq_ref, d_ref, f_ref, b_ref, df_ref, db_ref):
        lower = (lax.broadcasted_iota(jnp.int32, (BLOCK, BLOCK), 0)
                 >= lax.broadcasted_iota(jnp.int32, (BLOCK, BLOCK), 1)).astype(F32)
        carry = jnp.zeros((H, 1), F32)
        for blk in range(nb - 1, -1, -1):
            seg = dq_ref[:, blk * BLOCK:(blk + 1) * BLOCK] - d_ref[:, blk * BLOCK:(blk + 1) * BLOCK]
            suf = jnp.dot(seg, lower, precision=HIGHEST, preferred_element_type=F32) + carry
            df_ref[:, blk * BLOCK:(blk + 1) * BLOCK] = suf
            carry = suf[:, 0:1]
        f = f_ref[...] + b_ref[...]
        t = lax.broadcasted_iota(jnp.int32, (H, T), 1)
        df = jnp.where(t >= PAD_ROWS, df_ref[...] / (1.0 + jnp.exp(f)), 0.0)
        df_ref[...] = df
        db_ref[...] = jnp.sum(df, axis=1, keepdims=True)

    vm = pl.BlockSpec(memory_space=pltpu.VMEM)
    return pl.pallas_call(
        body, out_shape=(jax.ShapeDtypeStruct((H, T), F32), jax.ShapeDtypeStruct((H, 1), F32)),
        in_specs=[vm, vm, vm, vm], out_specs=(vm, vm),
        compiler_params=_params(), name=name)(dcq, dck, f_t, b)


LANE_KC = HEAD_DIM
LANE_QC = HEAD_DIM + 3
LANE_END = HEAD_DIM + 6


def _split3(c):
    hi = c.astype(BF16).astype(F32)
    r = c - hi
    mid = r.astype(BF16).astype(F32)
    lo = (r - mid).astype(BF16).astype(F32)
    return hi, mid, lo


def _lanes(lane, data, start, terms, rest):
    out = rest
    for i, t in enumerate(terms):
        out = jnp.where(lane == start + i, t, out)
    return jnp.where(lane < HEAD_DIM, data, out)


def _fox_prep(proj, cum_col, *, name):
    T = proj.shape[0]
    tm = FOX_TILE
    nt = T // tm
    H = FOX_HEADS
    lanes = 2 * HEAD_DIM
    first = (proj.shape[1] - 3 * H * HEAD_DIM) // lanes

    def body(q_ref, k_ref, v_ref, c_ref, qa_ref, ka_ref, va_ref):
        p = pl.program_id(0)
        i = pl.program_id(1)
        lane = lax.broadcasted_iota(jnp.int32, (tm, lanes), 1)
        rows = i * tm + lax.broadcasted_iota(jnp.int32, (tm, 1), 0)
        q2 = q_ref[...].astype(F32)
        k2 = k_ref[...].astype(F32)
        v2 = v_ref[...].astype(F32)
        cum = c_ref[...]
        for e in range(2):
            c = jnp.sum(jnp.where(lane == 2 * p + e, cum, 0.0), axis=1, keepdims=True)
            ck = jnp.where(rows >= PAD_ROWS, c, -NEG)
            qe, ke, ve = (q2, k2, v2) if e == 0 else tuple(pltpu.roll(a, HEAD_DIM, 1) for a in (q2, k2, v2))
            one = jnp.where(lane < LANE_END, 1.0, 0.0)
            qa = _lanes(lane, qe * SCALE, LANE_QC, _split3(c), jnp.where(lane < LANE_QC, -1.0, 0.0))
            ka = _lanes(lane, ke, LANE_KC, _split3(ck), one)
            va = jnp.where(lane < HEAD_DIM, ve, jnp.where(lane < LANE_QC, 1.0, 0.0))
            qa_ref[e] = qa.astype(BF16)
            ka_ref[e] = ka.astype(BF16)
            va_ref[e] = va.astype(BF16)

    pairs = FOX_GROUP // 2

    def col(part):
        return pl.BlockSpec((tm, lanes),
                            lambda p, i: (i, first + 3 * pairs * (p // pairs) + part * pairs + p % pairs))

    out = pl.BlockSpec((2, tm, lanes), lambda p, i: (p, i, 0))
    shp = jax.ShapeDtypeStruct((H, T, lanes), BF16)
    return pl.pallas_call(
        body, out_shape=(shp, shp, shp), grid=(H // 2, nt),
        in_specs=[col(0), col(1), col(2), pl.BlockSpec((tm, lanes), lambda p, i: (i, 0))],
        out_specs=(out, out, out),
        compiler_params=_params(("parallel", "parallel")), name=name)(proj, proj, proj, cum_col)


def _fox_fwd(q_aug, k_aug, v_aug, mix, *, ex=None, name):
    H, T, lanes = q_aug.shape
    tq = FOX_TILE
    nq = T // tq
    G = FOX_GROUP

    def body(q_ref, k_ref, v_ref, mix_ref, o_ref, lse_ref, m_scr, acc_scr):
        i = pl.program_id(1)
        m_scr[...] = jnp.full(m_scr.shape, NEG, F32)
        acc_scr[...] = jnp.zeros(acc_scr.shape, F32)

        def step(kb, diag):
            off = pl.multiple_of(kb * tq, tq)
            s_t = [lax.dot_general(k_ref[g, pl.ds(off, tq), :], q_ref[g], NT, preferred_element_type=F32)
                   for g in range(G)]
            if diag:
                r = lax.broadcasted_iota(jnp.int32, (tq, tq), 0)
                c = lax.broadcasted_iota(jnp.int32, (tq, tq), 1)
                s_t = [jnp.where(c >= r, s, NEG) for s in s_t]
            m_prev = [m_scr[g] for g in range(G)]
            m_new = [jnp.maximum(m_prev[g], jnp.max(s_t[g], axis=0, keepdims=True)) for g in range(G)]
            p_t = [jnp.exp(s_t[g] - m_new[g]).astype(BF16) for g in range(G)]
            pv = [lax.dot_general(v_ref[g, pl.ds(off, tq), :], p_t[g], TN, preferred_element_type=F32)
                  for g in range(G)]
            for g in range(G):
                acc_scr[g] = jnp.exp(m_prev[g] - m_new[g]) * acc_scr[g] + pv[g]
                m_scr[g] = m_new[g]

        def loop_body(kb, carry):
            step(kb, False)
            return carry

        lax.fori_loop(0, i, loop_body, 0)
        step(i, True)
        lane = lax.broadcasted_iota(jnp.int32, (tq, lanes), 1)
        outs = []
        for g in range(G):
            acc = acc_scr[g]
            lse_ref[g] = m_scr[g] + jnp.log(acc[HEAD_DIM:HEAD_DIM + 1, :])
            acc_t = acc.T
            outs.append(acc_t / acc_t[:, HEAD_DIM:HEAD_DIM + 1])
        for pair in range(G // 2):
            o_ref[:, pair * lanes:(pair + 1) * lanes] = jnp.where(
                lane < HEAD_DIM, outs[2 * pair], pltpu.roll(outs[2 * pair + 1], HEAD_DIM, 1)).astype(o_ref.dtype)

    blk = pl.BlockSpec((G, tq, lanes), lambda h, i: (h, i, 0))
    full = pl.BlockSpec((G, T, lanes), lambda h, i: (h, 0, 0))
    grid = (H // G, nq)
    first = mix.shape[1] // (G * HEAD_DIM) - H // G
    body, x_in, x_in_specs, x_out, x_out_specs, x_scr = _carry(ex, grid, 4, 2, body)
    return pl.pallas_call(
        body,
        out_shape=(jax.ShapeDtypeStruct(mix.shape, mix.dtype), jax.ShapeDtypeStruct((H, nq, 1, tq), F32), *x_out),
        grid=grid,
        in_specs=[blk, full, full, pl.BlockSpec(memory_space=pl.ANY)] + x_in_specs,
        out_specs=(pl.BlockSpec((tq, G * HEAD_DIM), lambda h, i: (i, first + h)),
                   pl.BlockSpec((G, None, 1, tq), lambda h, i: (h, i, 0, 0)), *x_out_specs),
        input_output_aliases={3: 0},
        scratch_shapes=[pltpu.VMEM((G, 1, tq), F32), pltpu.VMEM((G, lanes, tq), F32)] + x_scr,
        compiler_params=_params(("arbitrary", "arbitrary")), name=name)(q_aug, k_aug, v_aug, mix, *x_in)


def _fox_prep_bwd(dmix, mix, *, name):
    T = dmix.shape[0]
    H = FOX_HEADS
    tm = FOX_TILE
    lanes = 2 * HEAD_DIM
    first = mix.shape[1] // lanes - H // 2

    def body(d_ref, o_ref, da_ref):
        lane = lax.broadcasted_iota(jnp.int32, (tm, lanes), 1)
        d2 = d_ref[...].astype(F32)
        prod = d2 * o_ref[...].astype(F32)
        for e in range(2):
            de = d2 if e == 0 else pltpu.roll(d2, HEAD_DIM, 1)
            delta = jnp.sum(jnp.where(lane // HEAD_DIM == e, prod, 0.0), axis=1, keepdims=True)
            da_ref[e] = _lanes(lane, de, LANE_KC, _split3(-delta), jnp.zeros((), F32)).astype(BF16)

    pair = pl.BlockSpec((tm, lanes), lambda p, i: (i, first + p))
    return pl.pallas_call(
        body, out_shape=jax.ShapeDtypeStruct((H, T, lanes), BF16), grid=(H // 2, T // tm),
        in_specs=[pair, pair],
        out_specs=pl.BlockSpec((2, tm, lanes), lambda p, i: (p, i, 0)),
        compiler_params=_params(("parallel", "parallel")), name=name)(dmix, mix)


def _fox_bwd(q_aug, k_aug, v_aug, do_aug, lse_row, dproj, *, ex=None, name):
    H, T, lanes = q_aug.shape
    tq = FOX_TILE
    nq = T // tq
    G = FOX_GROUP

    def side_by_side(tiles, scale=None):
        lane = lax.broadcasted_iota(jnp.int32, tiles[0].shape, 1)
        out = [jnp.where(lane < HEAD_DIM, tiles[2 * p], pltpu.roll(tiles[2 * p + 1], HEAD_DIM, 1))
               for p in range(G // 2)]
        out = jnp.concatenate(out, axis=1)
        return out if scale is None else out * scale

    def body(q_ref, k_ref, v_ref, do_ref, lse_ref, dproj_in, out_ref, dcq_ref, dck_ref, dk_acc, dv_acc, dq_ref):
        j = pl.program_id(1)

        @pl.when(j == 0)
        def _():
            dq_ref[...] = jnp.zeros(dq_ref.shape, F32)
            dcq_ref[...] = jnp.zeros(dcq_ref.shape, F32)

        dk_acc[...] = jnp.zeros(dk_acc.shape, F32)
        dv_acc[...] = jnp.zeros(dv_acc.shape, F32)

        def step(qb, diag):
            off = pl.multiple_of(qb * tq, tq)
            heads = range(G)
            qa = [q_ref[g, pl.ds(off, tq), :] for g in heads]
            da = [do_ref[g, pl.ds(off, tq), :] for g in heads]
            s_t = [lax.dot_general(k_ref[g], qa[g], NT, preferred_element_type=F32) for g in heads]
            dp_t = [lax.dot_general(v_ref[g], da[g], NT, preferred_element_type=F32) for g in heads]
            p_t = [jnp.exp(s_t[g] - lse_ref[g, qb]) for g in heads]
            if diag:
                r = lax.broadcasted_iota(jnp.int32, (tq, tq), 0)
                c = lax.broadcasted_iota(jnp.int32, (tq, tq), 1)
                p_t = [jnp.where(c >= r, p, 0.0) for p in p_t]
            dsb = [(p_t[g] * dp_t[g]).astype(BF16) for g in heads]
            dv = [jnp.dot(p_t[g].astype(BF16), da[g], preferred_element_type=F32) for g in heads]
            dk = [jnp.dot(dsb[g], qa[g], preferred_element_type=F32) for g in heads]
            dq = [jnp.dot(dsb[g].T, k_ref[g], preferred_element_type=F32) for g in heads]
            for g in heads:
                dv_acc[g] += dv[g]
                dk_acc[g] += dk[g]
                dq_ref[g, pl.ds(off, tq), :] += dq[g]
                dcq_ref[g, qb] += jnp.sum(dsb[g].astype(F32), axis=0, keepdims=True)

        step(j, True)

        def loop_body(qb, carry):
            step(qb, False)
            return carry

        lax.fori_loop(j + 1, nq, loop_body, 0)
        dk = [dk_acc[g] for g in range(G)]
        rows = pl.ds(pl.multiple_of(j * tq, tq), tq)
        out_ref[:, 0:wide] = side_by_side([dq_ref[g, rows, :] for g in range(G)], SCALE).astype(out_ref.dtype)
        out_ref[:, wide:2 * wide] = side_by_side(dk).astype(out_ref.dtype)
        out_ref[:, 2 * wide:3 * wide] = side_by_side([dv_acc[g] for g in range(G)]).astype(out_ref.dtype)
        for g in range(G):
            dck_ref[g] = -dk[g].T[LANE_KC:LANE_KC + 1, :]

    blk = pl.BlockSpec((G, tq, lanes), lambda h, j: (h, j, 0))
    full = pl.BlockSpec((G, T, lanes), lambda h, j: (h, 0, 0))
    wide = G * HEAD_DIM
    first = dproj.shape[1] // (3 * wide) - H // G
    grid = (H // G, nq)
    body, x_in, x_in_specs, x_out, x_out_specs, x_scr = _carry(ex, grid, 6, 3, body)
    rows = jax.ShapeDtypeStruct((H, nq, 1, tq), F32)
    all_rows = pl.BlockSpec((G, nq, 1, tq), lambda h, j: (h, 0, 0, 0))
    return pl.pallas_call(
        body,
        out_shape=(jax.ShapeDtypeStruct(dproj.shape, dproj.dtype), rows, rows, *x_out),
        grid=grid,
        in_specs=[full, blk, blk, full, all_rows, pl.BlockSpec(memory_space=pl.ANY)] + x_in_specs,
        out_specs=(pl.BlockSpec((tq, 3 * wide), lambda h, j: (j, first + h)), all_rows,
                   pl.BlockSpec((G, None, 1, tq), lambda h, j: (h, j, 0, 0)), *x_out_specs),
        input_output_aliases={5: 0},
        scratch_shapes=[pltpu.VMEM((G, tq, lanes), F32), pltpu.VMEM((G, tq, lanes), F32),
                        pltpu.VMEM((G, T, lanes), F32)] + x_scr,
        compiler_params=_params(("arbitrary", "arbitrary")), name=name,
    )(q_aug, k_aug, v_aug, do_aug, lse_row, dproj, *x_in)


def _t5_bucket_np(d):
    n = np.maximum(d, 0).astype(np.int32)
    max_exact = N_BUCKETS // 2
    nf = np.maximum(n, 1).astype(np.float32)
    large = max_exact + (np.log(nf / max_exact) / math.log(MAX_DISTANCE / max_exact)
                         * (N_BUCKETS - max_exact)).astype(np.int32)
    large = np.minimum(large, N_BUCKETS - 1)
    return np.where(n < max_exact, n, large)


def _bucket_onehots():
    k = np.arange(BLOCK)[:, None]
    q = np.arange(BLOCK)[None, :]
    eye = np.eye(N_BUCKETS, dtype=np.float32)
    cur = eye[_t5_bucket_np(q - k).reshape(-1)]
    prev = eye[_t5_bucket_np(BLOCK + q - k).reshape(-1)]
    return cur, prev


SWA_K_COL = SWA_Q_HEADS * HEAD_DIM // (2 * HEAD_DIM)
SWA_V_COL = SWA_K_COL + 1


def _swa_terms(raw, bc, bp, far, sink, n):
    k = lax.broadcasted_iota(jnp.int32, (BLOCK, BLOCK), 0)
    q = lax.broadcasted_iota(jnp.int32, (BLOCK, BLOCK), 1)
    never = 2 * BLOCK
    s_c = raw[0] + bc
    s_p = raw[1] + bp
    s_m = raw[2] + jnp.where(n == 1, bp, far)
    s_c = jnp.where((k <= q) & (k >= jnp.where(n >= 1, 0, PAD_ROWS)), s_c, NEG)
    s_p = jnp.where(k > q + jnp.where(n >= 2, 0, never), s_p, NEG)
    s_m = jnp.where(k >= jnp.where(n >= 1, PAD_ROWS, never), s_m, NEG)
    m = jnp.maximum(jnp.maximum(jnp.max(s_c, axis=0, keepdims=True), jnp.max(s_p, axis=0, keepdims=True)),
                    jnp.maximum(jnp.max(s_m, axis=0, keepdims=True), sink))
    e = [jnp.exp(s_c - m), jnp.exp(s_p - m), jnp.exp(s_m - m)]
    e_s = jnp.exp(sink - m)
    l = (jnp.sum(e[0], axis=0, keepdims=True) + jnp.sum(e[1], axis=0, keepdims=True)
         + jnp.sum(e[2], axis=0, keepdims=True) + e_s)
    return e, e_s, l


def _swa_specs():
    def rows(which, col):
        if which == "cur":
            return pl.BlockSpec((BLOCK, BLOCK), lambda n: (n, col))
        if which == "prev":
            return pl.BlockSpec((BLOCK, BLOCK), lambda n: (jnp.maximum(n - 1, 0), col))
        return pl.BlockSpec((BLOCK, BLOCK), lambda n: (0, col))

    qblk = pl.BlockSpec((BLOCK, SWA_Q_HEADS * HEAD_DIM), lambda n: (n, 0))
    keys = [rows(w, SWA_K_COL) for w in ("cur", "prev", "meta")]
    vals = [rows(w, SWA_V_COL) for w in ("cur", "prev", "meta")]
    bias = pl.BlockSpec((SWA_Q_HEADS, BLOCK, BLOCK), lambda n: (0, 0, 0))
    smem = pl.BlockSpec(memory_space=pltpu.SMEM)
    return qblk, keys, vals, bias, smem


def _swa_own_kv(tile_ref, kv):
    lane = lax.broadcasted_iota(jnp.int32, (BLOCK, 2 * HEAD_DIM), 1)
    t = tile_ref[...].astype(F32)
    return jnp.where(lane // HEAD_DIM == kv, t, pltpu.roll(t, HEAD_DIM, 1)).astype(BF16)


def _swa_fwd(proj, bc, bp, far, sinks, *, name):
    T = proj.shape[0]
    nb = T // BLOCK
    G = SWA_GROUP
    Hq = SWA_Q_HEADS
    lanes = 2 * HEAD_DIM

    def body(q_ref, kc_ref, kp_ref, km_ref, vc_ref, vp_ref, vm_ref, bc_ref, bp_ref, far_ref, sink_ref, o_ref):
        n = pl.program_id(0)
        lane = lax.broadcasted_iota(jnp.int32, (BLOCK, lanes), 1)
        kk = [[_swa_own_kv(r, kv) for r in (kc_ref, kp_ref, km_ref)] for kv in range(SWA_KV_HEADS)]
        vv = [[_swa_own_kv(r, kv) for r in (vc_ref, vp_ref, vm_ref)] for kv in range(SWA_KV_HEADS)]
        heads, blocks = range(Hq), range(3)
        q2 = [q_ref[:, pair * lanes:(pair + 1) * lanes].astype(F32) * SCALE for pair in range(Hq // 2)]
        qm = [jnp.where(lane // HEAD_DIM == h % 2, q2[h // 2], 0.0).astype(BF16) for h in heads]
        raw = [[lax.dot_general(kk[h // G][b], qm[h], NT, preferred_element_type=F32) for b in blocks] for h in heads]
        terms = [_swa_terms(raw[h], bc_ref[h], bp_ref[h], far_ref[h], sink_ref[h], n) for h in heads]
        o_t = [sum(lax.dot_general(vv[h // G][b], terms[h][0][b].astype(BF16), TN, preferred_element_type=F32)
                   for b in blocks) for h in heads]
        outs = [(o_t[h] / terms[h][2]).T for h in heads]
        for pair in range(Hq // 2):
            o_ref[:, pair * lanes:(pair + 1) * lanes] = jnp.where(
                lane < HEAD_DIM, outs[2 * pair], outs[2 * pair + 1]).astype(o_ref.dtype)

    qblk, keys, vals, bias, smem = _swa_specs()
    return pl.pallas_call(
        body, out_shape=jax.ShapeDtypeStruct((T, D_MODEL), BF16), grid=(nb,),
        in_specs=[qblk] + keys + vals + [bias, bias, smem, smem],
        out_specs=qblk,
        compiler_params=_params(("parallel",)), name=name,
    )(proj, proj, proj, proj, proj, proj, proj, bc, bp, far, sinks)


def _swa_bwd(proj, dmix, bc, bp, far, sinks, *, ex=None, name):
    T, width = proj.shape
    nb = T // BLOCK
    G = SWA_GROUP
    Hq = SWA_Q_HEADS
    lanes = 2 * HEAD_DIM
    qw = Hq * HEAD_DIM
    own_w = qw + 2 * lanes

    def body(q_ref, kc_ref, kp_ref, km_ref, vc_ref, vp_ref, vm_ref, do_ref, bc_ref, bp_ref, far_ref, sink_ref,
             dp_ref, dbc_ref, dbp_ref, dbf_ref, dsk_ref, dk_acc, dv_acc):
        n = pl.program_id(0)

        @pl.when(n == 0)
        def _():
            for ref in (dk_acc, dv_acc, dbc_ref, dbp_ref, dbf_ref, dsk_ref):
                ref[...] = jnp.zeros(ref.shape, F32)

        lane = lax.broadcasted_iota(jnp.int32, (BLOCK, lanes), 1)
        kvs = range(SWA_KV_HEADS)
        kk = [[_swa_own_kv(r, kv) for r in (kc_ref, kp_ref, km_ref)] for kv in kvs]
        vv = [[_swa_own_kv(r, kv) for r in (vc_ref, vp_ref, vm_ref)] for kv in kvs]
        heads, blocks = range(Hq), range(3)
        q2 = [q_ref[:, pair * lanes:(pair + 1) * lanes].astype(F32) * SCALE for pair in range(Hq // 2)]
        d2 = [do_ref[:, pair * lanes:(pair + 1) * lanes] for pair in range(Hq // 2)]
        own = [lane // HEAD_DIM == h % 2 for h in heads]
        qm = [jnp.where(own[h], q2[h // 2], 0.0).astype(BF16) for h in heads]
        dom = [jnp.where(own[h], d2[h // 2], jnp.zeros_like(d2[0])) for h in heads]
        raw = [[lax.dot_general(kk[h // G][b], qm[h], NT, preferred_element_type=F32) for b in blocks] for h in heads]
        dp = [[lax.dot_general(vv[h // G][b], dom[h], NT, preferred_element_type=F32) for b in blocks] for h in heads]
        p, ds16 = [], []
        for h in heads:
            e, e_s, l = _swa_terms(raw[h], bc_ref[h], bp_ref[h], far_ref[h], sink_ref[h], n)
            inv = 1.0 / l
            ph = [e[b] * inv for b in blocks]
            delta = sum(jnp.sum(ph[b] * dp[h][b], axis=0, keepdims=True) for b in blocks)
            ds = [ph[b] * (dp[h][b] - delta) for b in blocks]
            dsk_ref[h] += -(e_s * inv) * delta
            dbc_ref[h] += ds[0]
            dbp_ref[h] += ds[1] + jnp.where(n == 1, ds[2], 0.0)
            dbf_ref[h] += jnp.where(n >= 2, ds[2], 0.0)
            p.append([x.astype(BF16) for x in ph])
            ds16.append([x.astype(BF16) for x in ds])
        dq_t = [sum(lax.dot_general(kk[h // G][b], ds16[h][b], TN, preferred_element_type=F32) for b in blocks)
                for h in heads]
        group = [range(kv * G, (kv + 1) * G) for kv in kvs]
        dk = [[sum(jnp.dot(ds16[h][b], qm[h], preferred_element_type=F32) for h in group[kv]) for b in blocks]
              for kv in kvs]
        dv = [[sum(jnp.dot(p[h][b], dom[h], preferred_element_type=F32) for h in group[kv]) for b in blocks]
              for kv in kvs]
        dqs = [dq_t[h].T * SCALE for h in heads]
        rows = pl.ds(pl.multiple_of(n * BLOCK, BLOCK), BLOCK)
        for pair in range(Hq // 2):
            dp_ref[rows, pair * lanes:(pair + 1) * lanes] = jnp.where(
                lane < HEAD_DIM, dqs[2 * pair], dqs[2 * pair + 1]).astype(dp_ref.dtype)
        prev_rows = pl.ds(pl.multiple_of(jnp.maximum(n - 1, 0) * BLOCK, BLOCK), BLOCK)
        for acc, ref in ((dk, dk_acc), (dv, dv_acc)):
            tot = [[a + pltpu.roll(a, HEAD_DIM, 1) for a in acc[kv]] for kv in kvs]
            both = [jnp.where(lane < HEAD_DIM, tot[0][b], tot[1][b]) for b in blocks]
            ref[rows, :] += both[0]
            ref[prev_rows, :] += both[1]
            ref[0:BLOCK, :] += both[2]

        @pl.when(n == nb - 1)
        def _():
            dp_ref[:, qw:qw + lanes] = dk_acc[...].astype(dp_ref.dtype)
            dp_ref[:, qw + lanes:own_w] = dv_acc[...].astype(dp_ref.dtype)

    qblk, keys, vals, bias, smem = _swa_specs()
    dsk = pl.BlockSpec((Hq, 1, BLOCK), lambda n: (0, 0, 0))
    grid = (nb,)
    body, x_in, x_in_specs, x_out, x_out_specs, x_scr = _carry(ex, grid, 12, 5, body)
    tile = jax.ShapeDtypeStruct((Hq, BLOCK, BLOCK), F32)
    return pl.pallas_call(
        body,
        out_shape=(jax.ShapeDtypeStruct((T, width), BF16), tile, tile, tile,
                   jax.ShapeDtypeStruct((Hq, 1, BLOCK), F32), *x_out),
        grid=grid,
        in_specs=[qblk] + keys + vals + [qblk, bias, bias, smem, smem] + x_in_specs,
        out_specs=(pl.BlockSpec((T, own_w), lambda n: (0, 0)), bias, bias, bias, dsk, *x_out_specs),
        scratch_shapes=[pltpu.VMEM((T, lanes), F32), pltpu.VMEM((T, lanes), F32)] + x_scr,
        compiler_params=_params(("arbitrary",)), name=name,
    )(proj, proj, proj, proj, proj, proj, proj, dmix, bc, bp, far, sinks, *x_in)


def _small_grads(dbc, dbp, dbf, dsk, oh_cur, oh_prev, *, name):
    Hq = dbc.shape[0]

    def body(dbc_ref, dbp_ref, dbf_ref, dsk_ref, oc_ref, op_ref, tab_ref, sink_ref):
        tab = (jnp.dot(dbc_ref[...], oc_ref[...], precision=HIGHEST, preferred_element_type=F32)
               + jnp.dot(dbp_ref[...], op_ref[...], precision=HIGHEST, preferred_element_type=F32))
        far = jnp.sum(dbf_ref[...], axis=1, keepdims=True)
        last = lax.broadcasted_iota(jnp.int32, (Hq, N_BUCKETS), 1) == N_BUCKETS - 1
        tab_ref[...] = tab + jnp.where(last, far, 0.0)
        sink_ref[...] = jnp.sum(dsk_ref[...], axis=1, keepdims=True)

    vm = pl.BlockSpec(memory_space=pltpu.VMEM)
    return pl.pallas_call(
        body, out_shape=(jax.ShapeDtypeStruct((Hq, N_BUCKETS), F32), jax.ShapeDtypeStruct((Hq, 1), F32)),
        in_specs=[vm] * 6, out_specs=(vm, vm), compiler_params=_params(), name=name,
    )(dbc.reshape(Hq, -1), dbp.reshape(Hq, -1), dbf.reshape(Hq, -1), dsk.reshape(Hq, -1), oh_cur, oh_prev)


def _coords():
    return lax.axis_index("x"), lax.axis_index("y"), lax.axis_index("c")


class _Exchange:
    def __init__(self, inputs, out_shapes, scratch, start, finish):
        self.inputs, self.out_shapes, self.scratch, self.start, self.finish = inputs, out_shapes, scratch, start, finish


def _carry(ex, grid, n_in, n_out, body):
    if ex is None:
        return body, [], [], [], [], []
    ni, no = len(ex.inputs), len(ex.out_shapes)

    def at_step(which):
        cond = None
        for axis, n in enumerate(grid):
            c = pl.program_id(axis) == (0 if which == "first" else n - 1)
            cond = c if cond is None else cond & c
        return cond

    def wrapped(*refs):
        refs = list(refs)
        n_own_scr = len(refs) - (n_in + ni + n_out + no) - len(ex.scratch)
        own_in, side_in = refs[:n_in], refs[n_in:n_in + ni]
        own_out = refs[n_in + ni:n_in + ni + n_out]
        side_out = refs[n_in + ni + n_out:n_in + ni + n_out + no]
        rest = refs[n_in + ni + n_out + no:]
        own_scr, sems = rest[:n_own_scr], rest[n_own_scr:]

        @pl.when(at_step("first"))
        def _():
            ex.start(side_in, side_out, sems)

        body(*own_in, *own_out, *own_scr)

        @pl.when(at_step("last"))
        def _():
            ex.finish(side_in, side_out, sems)

    hbm = pl.BlockSpec(memory_space=pl.ANY)
    return wrapped, list(ex.inputs), [hbm] * ni, list(ex.out_shapes), [hbm] * no, list(ex.scratch)


def _run_exchange(ex, *, name):
    ni, no = len(ex.inputs), len(ex.out_shapes)

    def body(*refs):
        ins, outs, sems = refs[:ni], refs[ni:ni + no], refs[ni + no:]
        ex.start(ins, outs, sems)
        ex.finish(ins, outs, sems)

    hbm = pl.BlockSpec(memory_space=pl.ANY)
    return pl.pallas_call(
        body, out_shape=tuple(ex.out_shapes), in_specs=[hbm] * ni, out_specs=tuple([hbm] * no),
        scratch_shapes=ex.scratch, compiler_params=_params(), name=name)(*ex.inputs)


def _gather_exchange(shards):
    nt = len(shards)

    def copies(ins, outs, sems):
        send_sems, recv_sems, local_sems = sems
        x, y, c = _coords()
        me, sibling = (x, y, c), (x, y, 1 - c)
        chips = [(1 - x, y), (x, 1 - y), (1 - x, 1 - y)]

        def slot(t, dev):
            return outs[t].at[4 * dev[0] + 2 * dev[1] + dev[2]]

        def copy(t, k, block, to, src=None):
            dst = slot(t, block)
            return pltpu.make_async_remote_copy(
                src_ref=dst if src is None else src, dst_ref=dst,
                send_sem=send_sems.at[t, k], recv_sem=recv_sems.at[t, k], device_id=to, device_id_type=MESH)

        mine = [pltpu.make_async_copy(ins[t], slot(t, me), local_sems.at[t]) for t in range(nt)]
        first = []
        for t in range(nt):
            first.append(copy(t, 0, me, sibling, src=ins[t]))
            first += [copy(t, 1 + j, me, (*chip, c), src=ins[t]) for j, chip in enumerate(chips)]
        return copy, mine, first, me, sibling, chips, c

    def start(ins, outs, sems):
        _, mine, first, *_ = copies(ins, outs, sems)
        for cp in mine + first:
            cp.start()

    def finish(ins, outs, sems):
        copy, mine, first, me, sibling, chips, c = copies(ins, outs, sems)
        passed = []
        for j, chip in enumerate(chips):
            for t in range(nt):
                copy(t, 1 + j, (*chip, c), me).wait_recv()
                cp = copy(t, 4 + j, (*chip, c), sibling)
                cp.start()
                passed.append(cp)
        for t in range(nt):
            copy(t, 0, sibling, me).wait_recv()
            for j, chip in enumerate(chips):
                copy(t, 4 + j, (*chip, 1 - c), me).wait_recv()
        for cp in first + passed:
            cp.wait_send()
        for cp in mine:
            cp.wait()

    return _Exchange(
        list(shards), [jax.ShapeDtypeStruct((N_DEV,) + s.shape, s.dtype) for s in shards],
        [pltpu.SemaphoreType.DMA((nt, 7)), pltpu.SemaphoreType.DMA((nt, 7)), pltpu.SemaphoreType.DMA((nt,))],
        start, finish)


def _swap_exchange(arrays, n_slices, copies):
    nt = len(arrays)

    def start(ins, outs, sems):
        for cp in copies(ins, outs, sems):
            cp.start()

    def finish(ins, outs, sems):
        sends = copies(ins, outs, sems)
        for cp in sends:
            cp.wait_recv()
        for cp in sends:
            cp.wait_send()

    return _Exchange(
        list(arrays), [jax.ShapeDtypeStruct((n_slices,) + a.shape[1:], a.dtype) for a in arrays],
        [pltpu.SemaphoreType.DMA((nt, n_slices)), pltpu.SemaphoreType.DMA((nt, n_slices))], start, finish)


def _cores_exchange(gs):
    def copies(ins, outs, sems):
        send_sems, recv_sems = sems
        x, y, c = _coords()
        return [pltpu.make_async_remote_copy(
            src_ref=ins[t].at[2 * j + (1 - c)], dst_ref=outs[t].at[j],
            send_sem=send_sems.at[t, j], recv_sem=recv_sems.at[t, j], device_id=(x, y, 1 - c), device_id_type=MESH)
            for t in range(len(gs)) for j in range(4)]

    return _swap_exchange(gs, 4, copies)


def _chips_exchange(ps):
    def copies(ins, outs, sems):
        send_sems, recv_sems = sems
        x, y, c = _coords()
        peers = [(1 - x, y), (x, 1 - y), (1 - x, 1 - y)]
        return [pltpu.make_async_remote_copy(
            src_ref=ins[t].at[2 * px + py], dst_ref=outs[t].at[k],
            send_sem=send_sems.at[t, k], recv_sem=recv_sems.at[t, k], device_id=(px, py, c), device_id_type=MESH)
            for t in range(len(ps)) for k, (px, py) in enumerate(peers)]

    return _swap_exchange(ps, 3, copies)


def _add_cores(g, r, core, *, name):
    _, A, B = g.shape
    ta = _tile(A, 512, 16)

    def body(core_ref, a_ref, b_ref, o_ref, o16_ref):
        s = a_ref[...] + b_ref[...]
        o_ref[...] = s
        o16_ref[...] = s.astype(BF16)

    blk = (None, ta, B)
    out = pl.BlockSpec(blk, lambda j, i, core_ref: (j, i, 0))
    return pl.pallas_call(
        body, out_shape=(jax.ShapeDtypeStruct((4, A, B), F32), jax.ShapeDtypeStruct((4, A, B), BF16)),
        grid_spec=pltpu.PrefetchScalarGridSpec(
            num_scalar_prefetch=1, grid=(4, A // ta),
            in_specs=[pl.BlockSpec(blk, lambda j, i, core_ref: (2 * j + core_ref[0], i, 0)),
                      pl.BlockSpec(blk, lambda j, i, core_ref: (j, i, 0))],
            out_specs=(out, out)),
        compiler_params=_params(("parallel", "parallel")), name=name)(core, g, r)


def _adamw_math(w, g, m, v):
    m = ADAM_B1 * m + (1.0 - ADAM_B1) * g
    v = ADAM_B2 * v + (1.0 - ADAM_B2) * (g * g)
    m_hat = m / (1.0 - ADAM_B1 ** ADAM_STEP)
    v_hat = v / (1.0 - ADAM_B2 ** ADAM_STEP)
    delta = -ADAM_LR * (m_hat / (jnp.sqrt(v_hat) + ADAM_EPS) + ADAM_WD * w)
    return delta, m, v


def _sum_adamw(p, r, chip, w, m, v, *, segs, ta, name):
    Aw, Bw = w.shape
    Bg = p.shape[2]
    assert Aw % ta == 0

    def body(chip_ref, p_ref, r0, r1, r2, w_ref, m_ref, v_ref, g_out, d_out, m_out, v_out):
        for gc, wc, n in segs:
            g = ((p_ref[:, gc:gc + n] + r0[:, gc:gc + n].astype(F32)) + r1[:, gc:gc + n].astype(F32)
                 ) + r2[:, gc:gc + n].astype(F32)
            delta, m_new, v_new = _adamw_math(w_ref[:, wc:wc + n], g, m_ref[:, wc:wc + n], v_ref[:, wc:wc + n])
            g_out[:, wc:wc + n] = g
            d_out[:, wc:wc + n] = delta
            m_out[:, wc:wc + n] = m_new
            v_out[:, wc:wc + n] = v_new

    gblk = (None, ta, Bg)
    row = pl.BlockSpec((ta, Bw), lambda i, chip_ref: (i, 0))
    rspecs = [pl.BlockSpec(gblk, (lambda i, chip_ref, k=k: (k, i, 0))) for k in range(3)]
    shp = jax.ShapeDtypeStruct((Aw, Bw), F32)
    return pl.pallas_call(
        body, out_shape=(shp, shp, shp, shp),
        grid_spec=pltpu.PrefetchScalarGridSpec(
            num_scalar_prefetch=1, grid=(Aw // ta,),
            in_specs=[pl.BlockSpec(gblk, lambda i, chip_ref: (chip_ref[0], i, 0))] + rspecs + [row, row, row],
            out_specs=(row, row, row, row)),
        compiler_params=_params(("parallel",)), name=name)(chip, p, r, r, r, w, m, v)


def _adamw(w, g, m, v, *, name):
    def body(w_ref, g_ref, m_ref, v_ref, d_out, m_out, v_out):
        delta, m_new, v_new = _adamw_math(w_ref[...], g_ref[...], m_ref[...], v_ref[...])
        d_out[...] = delta
        m_out[...] = m_new
        v_out[...] = v_new

    vm = pl.BlockSpec(memory_space=pltpu.VMEM)
    shp = jax.ShapeDtypeStruct(w.shape, F32)
    return pl.pallas_call(body, out_shape=(shp, shp, shp), in_specs=[vm] * 4, out_specs=(vm, vm, vm),
                          compiler_params=_params(), name=name)(w, g, m, v)


def _small_allreduce_adamw(s, w, m, v, *, name):
    R, W = s.shape

    def body(s_ref, w_ref, m_ref, v_ref, g_out, d_out, m_out, v_out, gath, send_sems, recv_sems):
        x, y, c = _coords()
        mine = 4 * x + 2 * y + c
        gath[mine] = s_ref[...]
        peers = [((1 - x) if k & 4 else x, (1 - y) if k & 2 else y, (1 - c) if k & 1 else c) for k in range(1, N_DEV)]
        sends = []
        for k in range(1, N_DEV):
            peer = peers[k - 1]
            sends.append(pltpu.make_async_remote_copy(
                src_ref=s_ref, dst_ref=gath.at[mine], send_sem=send_sems.at[k - 1], recv_sem=recv_sems.at[k - 1],
                device_id=peer, device_id_type=MESH))
        for cp in sends:
            cp.start()
        for k in range(1, N_DEV):
            peer = peers[k - 1]
            pltpu.make_async_remote_copy(
                src_ref=s_ref, dst_ref=gath.at[4 * peer[0] + 2 * peer[1] + peer[2]],
                send_sem=send_sems.at[k - 1], recv_sem=recv_sems.at[k - 1],
                device_id=peer, device_id_type=MESH).wait_recv()
        for cp in sends:
            cp.wait_send()
        g = gath[0]
        for d in range(1, N_DEV):
            g = g + gath[d]
        delta, m_new, v_new = _adamw_math(w_ref[...], g, m_ref[...], v_ref[...])
        g_out[...] = g
        d_out[...] = delta
        m_out[...] = m_new
        v_out[...] = v_new

    vm = pl.BlockSpec(memory_space=pltpu.VMEM)
    shp = jax.ShapeDtypeStruct((R, W), F32)
    return pl.pallas_call(
        body, out_shape=(shp, shp, shp, shp), in_specs=[vm] * 4, out_specs=(vm, vm, vm, vm),
        scratch_shapes=[pltpu.VMEM((N_DEV, R, W), F32), pltpu.SemaphoreType.DMA((N_DEV - 1,)),
                        pltpu.SemaphoreType.DMA((N_DEV - 1,))],
        compiler_params=_params(), name=name)(s, w, m, v)


def _pack_small(rel_bias, g1, g2, g3, g4, b_forget, sinks, extra=None, meta=None):
    misc = jnp.concatenate([rel_bias.reshape(-1), b_forget.reshape(-1), sinks.reshape(-1)])
    misc = jnp.concatenate([misc, jnp.zeros((D_MODEL - misc.shape[0],), F32)])[None]
    last = jnp.zeros((1, D_MODEL), F32) if extra is None else extra
    meta = jnp.zeros((N_META, D_MODEL), F32) if meta is None else meta
    return jnp.concatenate([g1, g2, g3, g4, misc, last, jnp.zeros((2, D_MODEL), F32), meta], axis=0)


def _unpack_small(p):
    nrb = N_BUCKETS * SWA_Q_HEADS
    misc = p[4]
    return dict(rel_bias=misc[:nrb].reshape(N_BUCKETS, SWA_Q_HEADS), ln_pre_mix=p[0:1], ln_post_mix=p[1:2],
                ln_pre_ffn=p[2:3], ln_post_ffn=p[3:4], b_forget=misc[nrb:nrb + 8].reshape(1, 8),
                sinks=misc[nrb + 8:nrb + 16].reshape(1, 8))


def _proj_runs():
    gw = FOX_GROUP * HEAD_DIM
    swa = SWA_Q_W + 2 * SWA_KV_HEADS * HEAD_DIM
    runs = [(0, swa)]
    for grp in range(FOX_HEADS // FOX_GROUP):
        runs += [(swa + part * FOX_W + grp * gw, swa + part * FOX_W + (grp + 1) * gw) for part in range(3)]
    return runs


def _natural_order(a):
    pos, placed = 0, []
    for start, stop in _proj_runs():
        placed.append((start, a[:, pos:pos + stop - start]))
        pos += stop - start
    return jnp.concatenate([piece for _, piece in sorted(placed, key=lambda sp: sp[0])], axis=1)


def kernel(x, meta_tokens, rel_bias, ln_pre_mix, ln_post_mix, ln_pre_ffn, ln_post_ffn, w_in, b_forget, sinks, w_out, w_gate_up, w_down, loss_target, m_meta_tokens, m_rel_bias, m_ln_pre_mix, m_ln_post_mix, m_ln_pre_ffn, m_ln_post_ffn, m_w_in, m_b_forget, m_sinks, m_w_out, m_w_gate_up, m_w_down, v_meta_tokens, v_rel_bias, v_ln_pre_mix, v_ln_post_mix, v_ln_pre_ffn, v_ln_post_ffn, v_w_in, v_b_forget, v_sinks, v_w_out, v_w_gate_up, v_w_down):
    seq = x.shape[1]
    T = BLOCK + seq
    assert T % FOX_TILE == 0
    nq = T // FOX_TILE
    tm = _tile(T, 1056)
    cin = w_in.shape[2]
    hid = w_down.shape[1]
    assert w_gate_up.shape[2] == 2 * hid and cin <= W_IN_PAD and hid <= HID_PAD

    x_i, y_i, c_i = _coords()
    core = jnp.reshape(c_i, (1,)).astype(jnp.int32)
    chip = jnp.reshape(2 * x_i + y_i, (1,)).astype(jnp.int32)
    w_in_s = jnp.pad(w_in[0].astype(BF16), ((0, 0), (0, W_IN_PAD - cin)))
    w_gu_s = jnp.pad(w_gate_up[0].astype(BF16).reshape(D_MODEL, 2, hid), ((0, 0), (0, 0), (0, HID_PAD - hid)))
    w_gu_s = w_gu_s.reshape(D_MODEL, 2 * HID_PAD)
    w_down_s = jnp.pad(w_down[0].astype(BF16), ((0, HID_PAD - hid), (0, 0)))
    g_in, g_meta = _run_exchange(_gather_exchange([w_in_s, meta_tokens]), name="ag_w_in")
    gather_rest = _gather_exchange([w_out[0].astype(BF16), w_gu_s, w_down_s])
    w_in_full = g_in[:, :, :cin].transpose(1, 0, 2).reshape(D_MODEL, N_DEV * cin)
    w_qkv = jnp.concatenate([w_in_full[:, a:b] for a, b in _proj_runs()], axis=1)
    w_f = jnp.pad(w_in_full[:, D_QKV:], ((0, 0), (0, BLOCK - FOX_HEADS)))
    meta_full = g_meta.transpose(1, 0, 2).reshape(N_META, D_MODEL)

    h0 = jnp.concatenate([jnp.zeros((PAD_ROWS, D_MODEL), F32), meta_full, x[0]], axis=0)
    target = jnp.concatenate([jnp.zeros((BLOCK, D_MODEL), F32), loss_target[0]], axis=0)
    hn1, hn1_t = _rms_fwd(h0, ln_pre_mix, name="rms_pre_mix")
    proj = _matmul(hn1, w_qkv, out_dtype=BF16, tm=tm, tn=768, name="mm_in_proj")
    proj_f = _matmul(hn1, w_f, out_dtype=F32, tm=tm, tn=BLOCK, name="mm_in_proj_f")

    f_t = proj_f[:, :FOX_HEADS].T
    bf_col = b_forget.reshape(FOX_HEADS, 1)

    oh_cur, oh_prev = _bucket_onehots()
    bias_c = jnp.einsum("pb,bh->hp", jnp.asarray(oh_cur), rel_bias, precision=HIGHEST).reshape(8, BLOCK, BLOCK)
    bias_p = jnp.einsum("pb,bh->hp", jnp.asarray(oh_prev), rel_bias, precision=HIGHEST).reshape(8, BLOCK, BLOCK)
    far = rel_bias[N_BUCKETS - 1]
    sink_v = sinks[0]
    mix_a = _swa_fwd(proj, bias_c, bias_p, far, sink_v, name="swa_fwd")

    _, cum_col = _fox_gates_fwd(f_t, bf_col, name="fox_gates_fwd")
    q_b, k_b, v_b = _fox_prep(proj, cum_col, name="fox_prep")
    mix, lse_row, g_out, g_gu, g_down = _fox_fwd(q_b, k_b, v_b, mix_a, ex=gather_rest, name="fox_fwd")
    w_out_full = g_out.reshape(D_MODEL, D_MODEL)
    w_down_full = g_down.reshape(N_DEV * HID_PAD, D_MODEL)

    a1 = _matmul(mix, w_out_full, out_dtype=F32, tm=tm, tn=512, name="mm_out_proj")
    h1, hn2, hn2_t = _post_res_norm(a1, ln_post_mix, h0, ln_pre_ffn, name="post_mix_pre_ffn")
    gate, up, act, act_t = _gate_up_swiglu(hn2, g_gu, name="mm_gate_up")
    ff = _matmul(act, w_down_full, out_dtype=F32, tm=tm, tn=512, name="mm_down")
    dh2, dff, dg_post_ffn, loss_acc = _loss_head(ff, ln_post_ffn, h1, target, name="loss_head")

    dgu = _d_act_swiglu(dff, w_down_full, gate, up, name="mm_d_act")
    d_w_down = _matmul(act_t, dff, out_dtype=F32, tm=768, tn=512, name="mm_dw_down")
    dhn2 = _matmul(dgu, g_gu, nt=True, b_shards=True, out_dtype=F32, tm=_tile(T, 528), tn=512, name="mm_d_hn2")
    d_w_gu = _matmul(hn2_t, dgu, out_shards=True, out_dtype=F32, tm=512, tn=2 * HID_PAD, name="mm_dw_gate_up")
    dh1, dg_pre_ffn, da1, dg_post_mix = _rms_bwd(h1, ln_pre_ffn, dhn2, dh2, out_dtype=F32,
                                                 then=(a1, ln_post_mix), name="rms_bwd_pre_ffn_post_mix")
    dmix = _matmul(da1, w_out_full, nt=True, out_dtype=BF16, tm=tm, tn=512, name="mm_d_mix")
    d_w_out = _matmul(mix.T, da1, out_dtype=F32, tm=512, tn=512, name="mm_dw_out")

    ffn_grads = [d_w_out.reshape(N_DEV, -1, D_MODEL), d_w_gu, d_w_down.reshape(N_DEV, HID_PAD, D_MODEL)]
    dproj_a, dbc, dbp, dbf, dsk, *ffn_sibling = _swa_bwd(
        proj, dmix, bias_c, bias_p, far, sink_v, ex=_cores_exchange(ffn_grads), name="swa_bwd")
    d_tab, d_sink = _small_grads(dbc, dbp, dbf, dsk, jnp.asarray(oh_cur), jnp.asarray(oh_prev), name="small_grads")
    ffn_sums = [_add_cores(g, r, core, name="rs_add_" + t)
                for g, r, t in zip(ffn_grads, ffn_sibling, ["w_out", "w_gate_up", "w_down"])]

    do_b = _fox_prep_bwd(dmix, mix, name="fox_prep_bwd")
    dproj, dcq, dck, *ffn_chips = _fox_bwd(
        q_b, k_b, v_b, do_b, lse_row, dproj_a, ex=_chips_exchange([s[1] for s in ffn_sums]), name="fox_bwd")
    df_t, d_bf = _fox_gates_bwd(dcq.reshape(FOX_HEADS, T), dck.reshape(FOX_HEADS, T), f_t, bf_col,
                                name="fox_gates_bwd")
    df = jnp.pad(df_t.T.astype(BF16), ((0, 0), (0, BLOCK - FOX_HEADS)))

    d_w_qkv = _matmul(hn1_t, dproj, out_dtype=F32, tm=512, tn=768, name="mm_dw_in")
    d_w_f = _matmul(hn1_t, df, out_dtype=F32, tm=512, tn=BLOCK, name="mm_dw_in_f")
    d_w_nat = jnp.concatenate([_natural_order(d_w_qkv), d_w_f[:, :FOX_HEADS]], axis=1)
    d_w_in = jnp.pad(d_w_nat.reshape(D_MODEL, N_DEV, cin), ((0, 0), (0, 0), (0, W_IN_PAD - cin))).transpose(1, 0, 2)
    dhn1_f = _matmul(df, w_f, nt=True, out_dtype=F32, tm=tm, tn=512, name="mm_d_hn1_f")
    dhn1, in_sibling = _matmul(dproj, w_qkv, nt=True, out_dtype=F32, tm=tm, tn=512,
                               ex=_cores_exchange([d_w_in]), name="mm_d_hn1")
    in_sum = _add_cores(d_w_in, in_sibling, core, name="rs_add_w_in")
    dh0, dg_pre_mix, in_chips = _rms_bwd(h0, ln_pre_mix, dhn1, dh1, out_dtype=F32, dy2=dhn1_f,
                                         ex=_chips_exchange([in_sum[1]]), name="rms_bwd_pre_mix")
    grad_x = dh0[BLOCK:][None]
    d_meta = dh0[PAD_ROWS:BLOCK]

    tags = ["w_in", "w_out", "w_gate_up", "w_down"]
    chip_sum = [in_sum[0]] + [s[0] for s in ffn_sums]
    from_chips = [in_chips] + list(ffn_chips)
    shard_w = [(w_in, m_w_in, v_w_in), (w_out, m_w_out, v_w_out), (w_gate_up, m_w_gate_up, v_w_gate_up),
               (w_down, m_w_down, v_w_down)]
    segs = [[(0, 0, cin)], [(0, 0, D_MODEL)], [(0, 0, hid), (HID_PAD, hid, hid)], [(0, 0, D_MODEL)]]
    tas = [256, BLOCK, 256, hid]
    big = [{}, {}, {}, {}]
    for i, t in enumerate(tags):
        w_t, m_t, v_t = shard_w[i]
        res = _sum_adamw(chip_sum[i], from_chips[i], chip, w_t[0], m_t[0], v_t[0], segs=segs[i], ta=tas[i],
                         name="rs_adamw_" + t)
        for kind in range(4):
            big[kind][t] = res[kind][None]

    loss_row = jnp.pad(loss_acc[0:1, 0:1] * (0.5 / D_MODEL), ((0, 0), (0, D_MODEL - 1)))
    s_small = _pack_small(d_tab.T, dg_pre_mix, dg_post_mix, dg_pre_ffn, dg_post_ffn, d_bf, d_sink,
                          extra=loss_row, meta=d_meta)
    w_s = _pack_small(rel_bias, ln_pre_mix, ln_post_mix, ln_pre_ffn, ln_post_ffn, b_forget, sinks)
    m_s = _pack_small(m_rel_bias, m_ln_pre_mix, m_ln_post_mix, m_ln_pre_ffn, m_ln_post_ffn, m_b_forget, m_sinks)
    v_s = _pack_small(v_rel_bias, v_ln_pre_mix, v_ln_post_mix, v_ln_pre_ffn, v_ln_post_ffn, v_b_forget, v_sinks)
    small = _small_allreduce_adamw(s_small, w_s, m_s, v_s, name="small_allreduce_adamw")
    loss = small[0][5, 0]
    mcols = meta_tokens.shape[1]
    g_meta_mine = lax.dynamic_slice(small[0][8:8 + N_META], (0, (4 * x_i + 2 * y_i + c_i) * mcols), (N_META, mcols))
    big[0]["meta_tokens"] = g_meta_mine
    for kind, arr in enumerate(_adamw(meta_tokens, g_meta_mine, m_meta_tokens, v_meta_tokens, name="adamw_meta")):
        big[kind + 1]["meta_tokens"] = arr
    small = [_unpack_small(p) for p in small]

    names = ["meta_tokens", "rel_bias", "ln_pre_mix", "ln_post_mix", "ln_pre_ffn", "ln_post_ffn", "w_in",
             "b_forget", "sinks", "w_out", "w_gate_up", "w_down"]
    outs = [loss, grad_x]
    for kind in range(4):
        for nme in names:
            outs.append(big[kind][nme] if nme in big[kind] else small[kind][nme])
    return tuple(outs)
```

```python
import math

import numpy as np
import jax
import jax.numpy as jnp
from jax import lax
from jax.experimental import pallas as pl
from jax.experimental.pallas import tpu as pltpu

F32 = jnp.float32
BF16 = jnp.bfloat16
HIGHEST = lax.Precision.HIGHEST
MESH = pl.DeviceIdType.MESH

N_DEV = 8
D_MODEL = 1024
N_META = 16
HEAD_DIM = 64
SWA_Q_HEADS = 8
SWA_KV_HEADS = 2
SWA_GROUP = 4
FOX_HEADS = 8
FOX_W = FOX_HEADS * HEAD_DIM
SWA_Q_W = SWA_Q_HEADS * HEAD_DIM
BLOCK = 128
PAD_ROWS = BLOCK - N_META
N_BUCKETS = 32
MAX_DISTANCE = 128
D_FF = 2816
D_QKV = 2304
D_PROJ = D_QKV + FOX_HEADS
D_PROJ_PAD = 2560
EPS = 1e-6
NEG = -1e30
SCALE = HEAD_DIM ** -0.5
ADAM_LR, ADAM_B1, ADAM_B2, ADAM_EPS, ADAM_WD, ADAM_STEP = 0.001, 0.9, 0.999, 1e-08, 0.01, 10
VMEM_LIMIT = 56 * 1024 * 1024
FOX_TILE = 384
FOX_GROUP = 4
W_IN_PAD = 384
HID_PAD = 384

NT = (((1,), (1,)), ((), ()))
NN = (((1,), (0,)), ((), ()))
TN = (((0,), (0,)), ((), ()))


def _params(sem=None, **kw):
    if sem is not None:
        kw["dimension_semantics"] = sem
    return pltpu.CompilerParams(vmem_limit_bytes=VMEM_LIMIT, **kw)


def _tile(n, target, mult=16):
    best = None
    for t in range(mult, min(n, target) + 1, mult):
        if n % t == 0:
            best = t
    assert best is not None, (n, target)
    return best


def _matmul(a, b, *, nt=False, ta=False, b_shards=False, out_shards=False, out_dtype, tm, tn=None, tk=None,
            ex=None, name):
    M, K = a.shape[::-1] if ta else a.shape
    assert not (ta and (nt or b_shards))
    k_shards = b.shape[0] if (b_shards and nt) else 0
    if k_shards:
        N, ks = b.shape[1], b.shape[2]
        assert tk is None and K == k_shards * ks
    elif b_shards:
        N, tn = b.shape[0] * b.shape[2], b.shape[2]
    else:
        N = b.shape[0] if nt else b.shape[1]
    tk = K if tk is None else tk
    assert M % tm == 0 and N % tn == 0 and K % tk == 0, (name, a.shape, b.shape, tm, tn, tk)
    nk = K // tk
    dn = NT if nt else (TN if ta else NN)
    a_spec = pl.BlockSpec((tk, tm), lambda i, j, k: (k, i)) if ta else pl.BlockSpec((tm, tk), lambda i, j, k: (i, k))

    def body(a_ref, b_ref, o_ref, *scr):
        if k_shards:
            part = sum(lax.dot_general(a_ref[:, s * ks:(s + 1) * ks], b_ref[s], NT, preferred_element_type=F32)
                       for s in range(k_shards))
        else:
            part = lax.dot_general(a_ref[...], b_ref[...], dn, preferred_element_type=F32)
        if nk == 1:
            o_ref[...] = part.astype(o_ref.dtype)
        else:
            acc = scr[0]
            k = pl.program_id(2)

            @pl.when(k == 0)
            def _():
                acc[...] = part

            @pl.when(k > 0)
            def _():
                acc[...] += part

            @pl.when(k == nk - 1)
            def _():
                o_ref[...] = acc[...].astype(o_ref.dtype)

    if k_shards:
        b_spec = pl.BlockSpec((k_shards, tn, ks), lambda i, j, k: (0, j, 0))
    elif b_shards:
        b_spec = pl.BlockSpec((None, tk, tn), lambda i, j, k: (j, k, 0))
    elif nt:
        b_spec = pl.BlockSpec((tn, tk), lambda i, j, k: (j, k))
    else:
        b_spec = pl.BlockSpec((tk, tn), lambda i, j, k: (k, j))
    if out_shards:
        out_shape = jax.ShapeDtypeStruct((N // tn, M, tn), out_dtype)
        out_spec = pl.BlockSpec((None, tm, tn), lambda i, j, k: (j, i, 0))
    else:
        out_shape = jax.ShapeDtypeStruct((M, N), out_dtype)
        out_spec = pl.BlockSpec((tm, tn), lambda i, j, k: (i, j))
    grid = (M // tm, N // tn, nk)
    body, x_in, x_in_specs, x_out, x_out_specs, x_scr = _carry(ex, grid, 2, 1, body)
    res = pl.pallas_call(
        body,
        out_shape=(out_shape, *x_out),
        grid=grid,
        in_specs=[a_spec, b_spec] + x_in_specs,
        out_specs=(out_spec, *x_out_specs),
        scratch_shapes=([pltpu.VMEM((tm, tn), F32)] if nk > 1 else []) + x_scr,
        compiler_params=_params(("parallel", "parallel", "arbitrary") if ex is None else ("arbitrary",) * 3),
        name=name,
    )(a, b, *x_in)
    return res[0] if ex is None else res


def _rstd(x):
    return lax.rsqrt(jnp.mean(x * x, axis=-1, keepdims=True) + EPS)


def _rms_fwd(x, g, *, name):
    T, D = x.shape
    tm = _tile(T, 512)

    def body(x_ref, g_ref, o_ref, ot_ref):
        x = x_ref[...]
        y = x * _rstd(x) * g_ref[...]
        o_ref[...] = y.astype(o_ref.dtype)
        ot_ref[...] = y.T.astype(ot_ref.dtype)

    return pl.pallas_call(
        body, out_shape=(jax.ShapeDtypeStruct((T, D), BF16), jax.ShapeDtypeStruct((D, T), BF16)), grid=(T // tm,),
        in_specs=[pl.BlockSpec((tm, D), lambda i: (i, 0)), pl.BlockSpec((1, D), lambda i: (0, 0))],
        out_specs=(pl.BlockSpec((tm, D), lambda i: (i, 0)), pl.BlockSpec((D, tm), lambda i: (0, i))),
        compiler_params=_params(("parallel",)), name=name)(x, g)


def _post_res_norm(a, g_post, h, g_pre, *, name):
    T, D = a.shape
    tm = _tile(T, 384, BLOCK)

    def body(a_ref, gp_ref, h_ref, gn_ref, h1_ref, o_ref, ot_ref):
        a = a_ref[...]
        h1 = h_ref[...] + a * _rstd(a) * gp_ref[...]
        h1_ref[...] = h1
        y = h1 * _rstd(h1) * gn_ref[...]
        o_ref[...] = y.astype(o_ref.dtype)
        ot_ref[...] = y.T.astype(ot_ref.dtype)

    row = pl.BlockSpec((tm, D), lambda i: (i, 0))
    vec = pl.BlockSpec((1, D), lambda i: (0, 0))
    return pl.pallas_call(
        body, out_shape=(jax.ShapeDtypeStruct((T, D), F32), jax.ShapeDtypeStruct((T, D), BF16),
                         jax.ShapeDtypeStruct((D, T), BF16)), grid=(T // tm,),
        in_specs=[row, vec, row, vec], out_specs=(row, row, pl.BlockSpec((D, tm), lambda i: (0, i))),
        compiler_params=_params(("parallel",)), name=name)(a, g_post, h, g_pre)


def _loss_head(a, g, h, target, *, name):
    T, D = a.shape
    tm = _tile(T, 512)

    def body(a_ref, g_ref, h_ref, t_ref, dy_ref, da_ref, dg_ref, loss_ref):
        i = pl.program_id(0)
        a = a_ref[...]
        r = _rstd(a)
        ah = a * r
        y = h_ref[...] + ah * g_ref[...]
        rows = i * tm + lax.broadcasted_iota(jnp.int32, (tm, 1), 0)
        err = jnp.where(rows >= BLOCK, y - t_ref[...], 0.0)
        dy = err / D
        dy_ref[...] = dy
        dah = dy * g_ref[...]
        da_ref[...] = (r * (dah - ah * jnp.mean(dah * ah, axis=-1, keepdims=True))).astype(da_ref.dtype)
        part = jnp.sum(jnp.sum(err * err, axis=1, keepdims=True), axis=0, keepdims=True)

        @pl.when(i == 0)
        def _():
            loss_ref[...] = jnp.zeros_like(loss_ref)
            dg_ref[...] = jnp.zeros_like(dg_ref)

        loss_ref[...] += jnp.broadcast_to(part, loss_ref.shape)
        dg_ref[...] += jnp.sum(dy * ah, axis=0, keepdims=True)

    row = pl.BlockSpec((tm, D), lambda i: (i, 0))
    vec = pl.BlockSpec((1, D), lambda i: (0, 0))
    return pl.pallas_call(
        body, out_shape=(jax.ShapeDtypeStruct((T, D), F32), jax.ShapeDtypeStruct((T, D), BF16),
                         jax.ShapeDtypeStruct((1, D), F32), jax.ShapeDtypeStruct((8, 128), F32)),
        grid=(T // tm,),
        in_specs=[row, vec, row, row],
        out_specs=(row, row, vec, pl.BlockSpec((8, 128), lambda i: (0, 0))),
        compiler_params=_params(("arbitrary",)), name=name)(a, g, h, target)


def _rms_bwd(x, g, dy, res, *, out_dtype, dy2=None, then=None, ex=None, name):
    T, D = x.shape
    tm = _tile(T, 512)
    has_res = res is not None
    has_dy2 = 2 if dy2 is not None else 0
    n_in = 3 + has_dy2 + has_res + (2 if then is not None else 0)
    n_out = 2 + (2 if then is not None else 0)

    def pull_back(x, g, dy):
        r = _rstd(x)
        xh = x * r
        dxh = dy * g
        return r * (dxh - xh * jnp.mean(dxh * xh, axis=-1, keepdims=True)), jnp.sum(dy * xh, axis=0, keepdims=True)

    def body(*refs):
        ins, outs = refs[:n_in], refs[n_in:]
        i = pl.program_id(0)

        @pl.when(i == 0)
        def _():
            for ref in outs[1::2]:
                ref[...] = jnp.zeros_like(ref)

        dy_all = ins[2][...].astype(F32)
        if has_dy2:
            dy_all = dy_all + lax.dot_general(ins[3][...], ins[4][...], NT, preferred_element_type=F32)
        dx, dg = pull_back(ins[0][...], ins[1][...], dy_all)
        if has_res:
            dx = dx + ins[3 + has_dy2][...]
        outs[0][...] = dx.astype(outs[0].dtype)
        outs[1][...] += dg
        if then is not None:
            dx2, dg2 = pull_back(ins[n_in - 2][...], ins[n_in - 1][...], dx)
            outs[2][...] = dx2.astype(outs[2].dtype)
            outs[3][...] += dg2

    row = pl.BlockSpec((tm, D), lambda i: (i, 0))
    vec = pl.BlockSpec((1, D), lambda i: (0, 0))
    ins = [x, g, dy] + (list(dy2) if has_dy2 else []) + ([res] if has_res else []) + (list(then) if then is not None else [])
    dy2_specs = ([pl.BlockSpec((tm, dy2[0].shape[1]), lambda i: (i, 0)), pl.BlockSpec(dy2[1].shape, lambda i: (0, 0))]
                 if has_dy2 else [])
    in_specs = [row, vec, row] + dy2_specs + ([row] if has_res else []) + ([row, vec] if then is not None else [])
    out_shape = [jax.ShapeDtypeStruct((T, D), out_dtype), jax.ShapeDtypeStruct((1, D), F32)]
    out_specs = [row, vec]
    if then is not None:
        out_shape += [jax.ShapeDtypeStruct((T, D), BF16), jax.ShapeDtypeStruct((1, D), F32)]
        out_specs += [row, vec]
    grid = (T // tm,)
    body, x_in, x_in_specs, x_out, x_out_specs, x_scr = _carry(ex, grid, n_in, n_out, body)
    return pl.pallas_call(
        body, out_shape=(*out_shape, *x_out), grid=grid,
        in_specs=in_specs + x_in_specs, out_specs=(*out_specs, *x_out_specs), scratch_shapes=x_scr,
        compiler_params=_params(("arbitrary",)), name=name)(*ins, *x_in)


def _gate_up_swiglu(a, w, *, name):
    T, D = a.shape
    S, n = w.shape[0] // 2, w.shape[2]
    tm = _tile(T, 1408, BLOCK)

    def body(a_ref, wg_ref, wu_ref, g_ref, u_ref, o_ref, ot_ref):
        x = a_ref[...]
        g = jnp.dot(x, wg_ref[...], preferred_element_type=F32)
        u = jnp.dot(x, wu_ref[...], preferred_element_type=F32)
        g16, u16 = g.astype(BF16), u.astype(BF16)
        g_ref[...] = g16
        u_ref[...] = u16
        gr = g16.astype(F32)
        act = gr / (1.0 + jnp.exp(-gr)) * u16.astype(F32)
        o_ref[...] = act.astype(o_ref.dtype)
        ot_ref[...] = act.T.astype(ot_ref.dtype)

    tile = pl.BlockSpec((tm, n), lambda i, j: (i, j))
    shp = jax.ShapeDtypeStruct((T, S * n), BF16)
    return pl.pallas_call(
        body, out_shape=(shp, shp, shp, jax.ShapeDtypeStruct((S * n, T), BF16)), grid=(T // tm, S),
        in_specs=[pl.BlockSpec((tm, D), lambda i, j: (i, 0)),
                  pl.BlockSpec((None, D, n), lambda i, j: (j, 0, 0)),
                  pl.BlockSpec((None, D, n), lambda i, j: (j + S, 0, 0))],
        out_specs=(tile, tile, tile, pl.BlockSpec((n, tm), lambda i, j: (j, i))),
        compiler_params=_params(("parallel", "parallel")), name=name)(a, w, w)


def _d_act_swiglu(dff, w_down, gate, up, *, name):
    T, D = dff.shape
    F = w_down.shape[0]
    tm = _tile(T, 384)
    tf = _tile(F, 768, BLOCK)

    def body(d_ref, w_ref, g_ref, u_ref, o_ref):
        dy = d_ref[...]
        for c in range(0, F, tf):
            d = lax.dot_general(dy, w_ref[c:c + tf, :], NT, preferred_element_type=F32)
            g = g_ref[:, c:c + tf].astype(F32)
            u = u_ref[:, c:c + tf].astype(F32)
            sg = 1.0 / (1.0 + jnp.exp(-g))
            o_ref[:, c:c + tf] = (d * u * (sg * (1.0 + g * (1.0 - sg)))).astype(o_ref.dtype)
            o_ref[:, F + c:F + c + tf] = (d * (g * sg)).astype(o_ref.dtype)

    row = pl.BlockSpec((tm, F), lambda i: (i, 0))
    return pl.pallas_call(
        body, out_shape=jax.ShapeDtypeStruct((T, 2 * F), BF16), grid=(T // tm,),
        in_specs=[pl.BlockSpec((tm, D), lambda i: (i, 0)), pl.BlockSpec((F, D), lambda i: (0, 0)), row, row],
        out_specs=pl.BlockSpec((tm, 2 * F), lambda i: (i, 0)),
        compiler_params=_params(("parallel",)), name=name)(dff, w_down, gate, up)


def _fox_gates_fwd(f_t, b, *, name):
    H, T = f_t.shape
    nb = T // BLOCK

    def body(f_ref, b_ref, cum_ref, col_ref):
        f = f_ref[...] + b_ref[...]
        ls = jnp.minimum(f, 0.0) - jnp.log(1.0 + jnp.exp(-jnp.abs(f)))
        t = lax.broadcasted_iota(jnp.int32, (H, T), 1)
        ls = jnp.where(t >= PAD_ROWS, ls, 0.0)
        upper = (lax.broadcasted_iota(jnp.int32, (BLOCK, BLOCK), 0)
                 <= lax.broadcasted_iota(jnp.int32, (BLOCK, BLOCK), 1)).astype(F32)
        carry = jnp.zeros((H, 1), F32)
        for blk in range(nb):
            seg = ls[:, blk * BLOCK:(blk + 1) * BLOCK]
            pre = jnp.dot(seg, upper, precision=HIGHEST, preferred_element_type=F32) + carry
            cum_ref[:, blk * BLOCK:(blk + 1) * BLOCK] = pre
            col_ref[blk * BLOCK:(blk + 1) * BLOCK, :] = jnp.concatenate(
                [pre, jnp.zeros((BLOCK - H, BLOCK), F32)], axis=0).T
            carry = pre[:, BLOCK - 1:BLOCK]

    vm = pl.BlockSpec(memory_space=pltpu.VMEM)
    return pl.pallas_call(
        body, out_shape=(jax.ShapeDtypeStruct((H, T), F32), jax.ShapeDtypeStruct((T, BLOCK), F32)),
        in_specs=[vm, vm], out_specs=(vm, vm),
        compiler_params=_params(), name=name)(f_t, b)


def _fox_gates_bwd(dcq, dck, f_t, b, *, name):
    H, T = f_t.shape
    nb = T // BLOCK

    def body(dq_ref, d_ref, f_ref, b_ref, df_ref, db_ref):
        lower = (lax.broadcasted_iota(jnp.int32, (BLOCK, BLOCK), 0)
                 >= lax.broadcasted_iota(jnp.int32, (BLOCK, BLOCK), 1)).astype(F32)
        carry = jnp.zeros((H, 1), F32)
        for blk in range(nb - 1, -1, -1):
            seg = dq_ref[:, blk * BLOCK:(blk + 1) * BLOCK] - d_ref[:, blk * BLOCK:(blk + 1) * BLOCK]
            suf = jnp.dot(seg, lower, precision=HIGHEST, preferred_element_type=F32) + carry
            df_ref[:, blk * BLOCK:(blk + 1) * BLOCK] = suf
            carry = suf[:, 0:1]
        f = f_ref[...] + b_ref[...]
        t = lax.broadcasted_iota(jnp.int32, (H, T), 1)
        df = jnp.where(t >= PAD_ROWS, df_ref[...] / (1.0 + jnp.exp(f)), 0.0)
        df_ref[...] = df
        db_ref[...] = jnp.sum(df, axis=1, keepdims=True)

    vm = pl.BlockSpec(memory_space=pltpu.VMEM)
    return pl.pallas_call(
        body, out_shape=(jax.ShapeDtypeStruct((H, T), F32), jax.ShapeDtypeStruct((H, 1), F32)),
        in_specs=[vm, vm, vm, vm], out_specs=(vm, vm),
        compiler_params=_params(), name=name)(dcq, dck, f_t, b)


LANE_KC = HEAD_DIM
LANE_QC = HEAD_DIM + 3
LANE_END = HEAD_DIM + 6


def _split3(c):
    hi = c.astype(BF16).astype(F32)
    r = c - hi
    mid = r.astype(BF16).astype(F32)
    lo = (r - mid).astype(BF16).astype(F32)
    return hi, mid, lo


def _lanes(lane, data, start, terms, rest):
    out = rest
    for i, t in enumerate(terms):
        out = jnp.where(lane == start + i, t, out)
    return jnp.where(lane < HEAD_DIM, data, out)


def _fox_prep(proj, cum_col, *, name):
    T = proj.shape[0]
    tm = FOX_TILE
    nt = T // tm
    H = FOX_HEADS
    lanes = 2 * HEAD_DIM
    first = (proj.shape[1] - 3 * H * HEAD_DIM) // lanes

    def body(q_ref, k_ref, v_ref, c_ref, qa_ref, ka_ref, va_ref):
        p = pl.program_id(0)
        i = pl.program_id(1)
        lane = lax.broadcasted_iota(jnp.int32, (tm, lanes), 1)
        rows = i * tm + lax.broadcasted_iota(jnp.int32, (tm, 1), 0)
        q2 = q_ref[...].astype(F32)
        k2 = k_ref[...].astype(F32)
        v2 = v_ref[...].astype(F32)
        cum = c_ref[...]
        for e in range(2):
            c = jnp.sum(jnp.where(lane == 2 * p + e, cum, 0.0), axis=1, keepdims=True)
            ck = jnp.where(rows >= PAD_ROWS, c, -NEG)
            qe, ke, ve = (q2, k2, v2) if e == 0 else tuple(pltpu.roll(a, HEAD_DIM, 1) for a in (q2, k2, v2))
            one = jnp.where(lane < LANE_END, 1.0, 0.0)
            qa = _lanes(lane, qe * SCALE, LANE_QC, _split3(c), jnp.where(lane < LANE_QC, -1.0, 0.0))
            ka = _lanes(lane, ke, LANE_KC, _split3(ck), one)
            va = jnp.where(lane < HEAD_DIM, ve, jnp.where(lane < LANE_QC, 1.0, 0.0))
            qa_ref[e] = qa.astype(BF16)
            ka_ref[e] = ka.astype(BF16)
            va_ref[e] = va.astype(BF16)

    pairs = FOX_GROUP // 2

    def col(part):
        return pl.BlockSpec((tm, lanes),
                            lambda p, i: (i, first + 3 * pairs * (p // pairs) + part * pairs + p % pairs))

    out = pl.BlockSpec((2, tm, lanes), lambda p, i: (p, i, 0))
    shp = jax.ShapeDtypeStruct((H, T, lanes), BF16)
    return pl.pallas_call(
        body, out_shape=(shp, shp, shp), grid=(H // 2, nt),
        in_specs=[col(0), col(1), col(2), pl.BlockSpec((tm, lanes), lambda p, i: (i, 0))],
        out_specs=(out, out, out),
        compiler_params=_params(("parallel", "parallel")), name=name)(proj, proj, proj, cum_col)


def _fox_fwd(q_aug, k_aug, v_aug, mix, *, ex=None, name):
    H, T, lanes = q_aug.shape
    tq = FOX_TILE
    nq = T // tq
    G = FOX_GROUP

    def body(q_ref, k_ref, v_ref, mix_ref, o_ref, lse_ref, m_scr, acc_scr):
        i = pl.program_id(1)
        m_scr[...] = jnp.full(m_scr.shape, NEG, F32)
        acc_scr[...] = jnp.zeros(acc_scr.shape, F32)

        def step(kb, diag):
            off = pl.multiple_of(kb * tq, tq)
            s_t = [lax.dot_general(k_ref[g, pl.ds(off, tq), :], q_ref[g], NT, preferred_element_type=F32)
                   for g in range(G)]
            if diag:
                r = lax.broadcasted_iota(jnp.int32, (tq, tq), 0)
                c = lax.broadcasted_iota(jnp.int32, (tq, tq), 1)
                s_t = [jnp.where(c >= r, s, NEG) for s in s_t]
            m_prev = [m_scr[g] for g in range(G)]
            m_new = [jnp.maximum(m_prev[g], jnp.max(s_t[g], axis=0, keepdims=True)) for g in range(G)]
            p_t = [jnp.exp(s_t[g] - m_new[g]).astype(BF16) for g in range(G)]
            pv = [lax.dot_general(v_ref[g, pl.ds(off, tq), :], p_t[g], TN, preferred_element_type=F32)
                  for g in range(G)]
            for g in range(G):
                acc_scr[g] = jnp.exp(m_prev[g] - m_new[g]) * acc_scr[g] + pv[g]
                m_scr[g] = m_new[g]

        def loop_body(kb, carry):
            step(kb, False)
            return carry

        lax.fori_loop(0, i, loop_body, 0)
        step(i, True)
        lane = lax.broadcasted_iota(jnp.int32, (tq, lanes), 1)
        outs = []
        for g in range(G):
            acc = acc_scr[g]
            lse_ref[g] = m_scr[g] + jnp.log(acc[HEAD_DIM:HEAD_DIM + 1, :])
            acc_t = acc.T
            outs.append(acc_t / acc_t[:, HEAD_DIM:HEAD_DIM + 1])
        for pair in range(G // 2):
            o_ref[:, pair * lanes:(pair + 1) * lanes] = jnp.where(
                lane < HEAD_DIM, outs[2 * pair], pltpu.roll(outs[2 * pair + 1], HEAD_DIM, 1)).astype(o_ref.dtype)

    blk = pl.BlockSpec((G, tq, lanes), lambda h, i: (h, i, 0))
    full = pl.BlockSpec((G, T, lanes), lambda h, i: (h, 0, 0))
    grid = (H // G, nq)
    first = mix.shape[1] // (G * HEAD_DIM) - H // G
    body, x_in, x_in_specs, x_out, x_out_specs, x_scr = _carry(ex, grid, 4, 2, body)
    return pl.pallas_call(
        body,
        out_shape=(jax.ShapeDtypeStruct(mix.shape, mix.dtype), jax.ShapeDtypeStruct((H, nq, 1, tq), F32), *x_out),
        grid=grid,
        in_specs=[blk, full, full, pl.BlockSpec(memory_space=pl.ANY)] + x_in_specs,
        out_specs=(pl.BlockSpec((tq, G * HEAD_DIM), lambda h, i: (i, first + h)),
                   pl.BlockSpec((G, None, 1, tq), lambda h, i: (h, i, 0, 0)), *x_out_specs),
        input_output_aliases={3: 0},
        scratch_shapes=[pltpu.VMEM((G, 1, tq), F32), pltpu.VMEM((G, lanes, tq), F32)] + x_scr,
        compiler_params=_params(("arbitrary", "arbitrary")), name=name)(q_aug, k_aug, v_aug, mix, *x_in)


def _fox_prep_bwd(dmix, mix, *, name):
    T = dmix.shape[0]
    H = FOX_HEADS
    tm = FOX_TILE
    lanes = 2 * HEAD_DIM
    first = mix.shape[1] // lanes - H // 2

    def body(d_ref, o_ref, da_ref):
        lane = lax.broadcasted_iota(jnp.int32, (tm, lanes), 1)
        d2 = d_ref[...].astype(F32)
        prod = d2 * o_ref[...].astype(F32)
        for e in range(2):
            de = d2 if e == 0 else pltpu.roll(d2, HEAD_DIM, 1)
            delta = jnp.sum(jnp.where(lane // HEAD_DIM == e, prod, 0.0), axis=1, keepdims=True)
            da_ref[e] = _lanes(lane, de, LANE_KC, _split3(-delta), jnp.zeros((), F32)).astype(BF16)

    pair = pl.BlockSpec((tm, lanes), lambda p, i: (i, first + p))
    return pl.pallas_call(
        body, out_shape=jax.ShapeDtypeStruct((H, T, lanes), BF16), grid=(H // 2, T // tm),
        in_specs=[pair, pair],
        out_specs=pl.BlockSpec((2, tm, lanes), lambda p, i: (p, i, 0)),
        compiler_params=_params(("parallel", "parallel")), name=name)(dmix, mix)


def _fox_bwd(q_aug, k_aug, v_aug, do_aug, lse_row, dproj, *, ex=None, name):
    H, T, lanes = q_aug.shape
    tq = FOX_TILE
    nq = T // tq
    G = FOX_GROUP

    def side_by_side(tiles, scale=None):
        lane = lax.broadcasted_iota(jnp.int32, tiles[0].shape, 1)
        out = [jnp.where(lane < HEAD_DIM, tiles[2 * p], pltpu.roll(tiles[2 * p + 1], HEAD_DIM, 1))
               for p in range(G // 2)]
        out = jnp.concatenate(out, axis=1)
        return out if scale is None else out * scale

    def body(q_ref, k_ref, v_ref, do_ref, lse_ref, dproj_in, out_ref, dcq_ref, dck_ref, dk_acc, dv_acc, dq_ref):
        j = pl.program_id(1)

        @pl.when(j == 0)
        def _():
            dq_ref[...] = jnp.zeros(dq_ref.shape, F32)
            dcq_ref[...] = jnp.zeros(dcq_ref.shape, F32)

        dk_acc[...] = jnp.zeros(dk_acc.shape, F32)
        dv_acc[...] = jnp.zeros(dv_acc.shape, F32)

        def step(qb, diag):
            off = pl.multiple_of(qb * tq, tq)
            heads = range(G)
            qa = [q_ref[g, pl.ds(off, tq), :] for g in heads]
            da = [do_ref[g, pl.ds(off, tq), :] for g in heads]
            s_t = [lax.dot_general(k_ref[g], qa[g], NT, preferred_element_type=F32) for g in heads]
            dp_t = [lax.dot_general(v_ref[g], da[g], NT, preferred_element_type=F32) for g in heads]
            p_t = [jnp.exp(s_t[g] - lse_ref[g, qb]) for g in heads]
            if diag:
                r = lax.broadcasted_iota(jnp.int32, (tq, tq), 0)
                c = lax.broadcasted_iota(jnp.int32, (tq, tq), 1)
                p_t = [jnp.where(c >= r, p, 0.0) for p in p_t]
            dsb = [(p_t[g] * dp_t[g]).astype(BF16) for g in heads]
            dv = [jnp.dot(p_t[g].astype(BF16), da[g], preferred_element_type=F32) for g in heads]
            dk = [jnp.dot(dsb[g], qa[g], preferred_element_type=F32) for g in heads]
            dq = [jnp.dot(dsb[g].T, k_ref[g], preferred_element_type=F32) for g in heads]
            for g in heads:
                dv_acc[g] += dv[g]
                dk_acc[g] += dk[g]
                dq_ref[g, pl.ds(off, tq), :] += dq[g]
                dcq_ref[g, qb] += jnp.sum(dsb[g].astype(F32), axis=0, keepdims=True)

        step(j, True)

        def loop_body(qb, carry):
            step(qb, False)
            return carry

        lax.fori_loop(j + 1, nq, loop_body, 0)
        dk = [dk_acc[g] for g in range(G)]
        rows = pl.ds(pl.multiple_of(j * tq, tq), tq)
        out_ref[:, 0:wide] = side_by_side([dq_ref[g, rows, :] for g in range(G)], SCALE).astype(out_ref.dtype)
        out_ref[:, wide:2 * wide] = side_by_side(dk).astype(out_ref.dtype)
        out_ref[:, 2 * wide:3 * wide] = side_by_side([dv_acc[g] for g in range(G)]).astype(out_ref.dtype)
        for g in range(G):
            dck_ref[g] = -dk[g].T[LANE_KC:LANE_KC + 1, :]

    blk = pl.BlockSpec((G, tq, lanes), lambda h, j: (h, j, 0))
    full = pl.BlockSpec((G, T, lanes), lambda h, j: (h, 0, 0))
    wide = G * HEAD_DIM
    first = dproj.shape[1] // (3 * wide) - H // G
    grid = (H // G, nq)
    body, x_in, x_in_specs, x_out, x_out_specs, x_scr = _carry(ex, grid, 6, 3, body)
    rows = jax.ShapeDtypeStruct((H, nq, 1, tq), F32)
    all_rows = pl.BlockSpec((G, nq, 1, tq), lambda h, j: (h, 0, 0, 0))
    return pl.pallas_call(
        body,
        out_shape=(jax.ShapeDtypeStruct(dproj.shape, dproj.dtype), rows, rows, *x_out),
        grid=grid,
        in_specs=[full, blk, blk, full, all_rows, pl.BlockSpec(memory_space=pl.ANY)] + x_in_specs,
        out_specs=(pl.BlockSpec((tq, 3 * wide), lambda h, j: (j, first + h)), all_rows,
                   pl.BlockSpec((G, None, 1, tq), lambda h, j: (h, j, 0, 0)), *x_out_specs),
        input_output_aliases={5: 0},
        scratch_shapes=[pltpu.VMEM((G, tq, lanes), F32), pltpu.VMEM((G, tq, lanes), F32),
                        pltpu.VMEM((G, T, lanes), F32)] + x_scr,
        compiler_params=_params(("arbitrary", "arbitrary")), name=name,
    )(q_aug, k_aug, v_aug, do_aug, lse_row, dproj, *x_in)


def _t5_bucket_np(d):
    n = np.maximum(d, 0).astype(np.int32)
    max_exact = N_BUCKETS // 2
    nf = np.maximum(n, 1).astype(np.float32)
    large = max_exact + (np.log(nf / max_exact) / math.log(MAX_DISTANCE / max_exact)
                         * (N_BUCKETS - max_exact)).astype(np.int32)
    large = np.minimum(large, N_BUCKETS - 1)
    return np.where(n < max_exact, n, large)


def _bucket_onehots():
    k = np.arange(BLOCK)[:, None]
    q = np.arange(BLOCK)[None, :]
    eye = np.eye(N_BUCKETS, dtype=np.float32)
    cur = eye[_t5_bucket_np(q - k).reshape(-1)]
    prev = eye[_t5_bucket_np(BLOCK + q - k).reshape(-1)]
    return cur, prev


SWA_K_COL = SWA_Q_HEADS * HEAD_DIM // (2 * HEAD_DIM)
SWA_V_COL = SWA_K_COL + 1


def _swa_terms(raw, bc, bp, far, sink, n):
    k = lax.broadcasted_iota(jnp.int32, (BLOCK, BLOCK), 0)
    q = lax.broadcasted_iota(jnp.int32, (BLOCK, BLOCK), 1)
    never = 2 * BLOCK
    s_c = raw[0] + bc
    s_p = raw[1] + bp
    s_m = raw[2] + jnp.where(n == 1, bp, far)
    s_c = jnp.where((k <= q) & (k >= jnp.where(n >= 1, 0, PAD_ROWS)), s_c, NEG)
    s_p = jnp.where(k > q + jnp.where(n >= 2, 0, never), s_p, NEG)
    s_m = jnp.where(k >= jnp.where(n >= 1, PAD_ROWS, never), s_m, NEG)
    m = jnp.maximum(jnp.maximum(jnp.max(s_c, axis=0, keepdims=True), jnp.max(s_p, axis=0, keepdims=True)),
                    jnp.maximum(jnp.max(s_m, axis=0, keepdims=True), sink))
    e = [jnp.exp(s_c - m), jnp.exp(s_p - m), jnp.exp(s_m - m)]
    e_s = jnp.exp(sink - m)
    l = (jnp.sum(e[0], axis=0, keepdims=True) + jnp.sum(e[1], axis=0, keepdims=True)
         + jnp.sum(e[2], axis=0, keepdims=True) + e_s)
    return e, e_s, l


def _swa_specs():
    def rows(which, col):
        if which == "cur":
            return pl.BlockSpec((BLOCK, BLOCK), lambda n: (n, col))
        if which == "prev":
            return pl.BlockSpec((BLOCK, BLOCK), lambda n: (jnp.maximum(n - 1, 0), col))
        return pl.BlockSpec((BLOCK, BLOCK), lambda n: (0, col))

    qblk = pl.BlockSpec((BLOCK, SWA_Q_HEADS * HEAD_DIM), lambda n: (n, 0))
    keys = [rows(w, SWA_K_COL) for w in ("cur", "prev", "meta")]
    vals = [rows(w, SWA_V_COL) for w in ("cur", "prev", "meta")]
    bias = pl.BlockSpec((SWA_Q_HEADS, BLOCK, BLOCK), lambda n: (0, 0, 0))
    smem = pl.BlockSpec(memory_space=pltpu.SMEM)
    return qblk, keys, vals, bias, smem


def _swa_own_kv(tile_ref, kv):
    lane = lax.broadcasted_iota(jnp.int32, (BLOCK, 2 * HEAD_DIM), 1)
    t = tile_ref[...].astype(F32)
    return jnp.where(lane // HEAD_DIM == kv, t, pltpu.roll(t, HEAD_DIM, 1)).astype(BF16)


def _swa_fwd(proj, bc, bp, far, sinks, *, name):
    T = proj.shape[0]
    nb = T // BLOCK
    G = SWA_GROUP
    Hq = SWA_Q_HEADS
    lanes = 2 * HEAD_DIM

    def body(q_ref, kc_ref, kp_ref, km_ref, vc_ref, vp_ref, vm_ref, bc_ref, bp_ref, far_ref, sink_ref, o_ref):
        n = pl.program_id(0)
        lane = lax.broadcasted_iota(jnp.int32, (BLOCK, lanes), 1)
        kk = [[_swa_own_kv(r, kv) for r in (kc_ref, kp_ref, km_ref)] for kv in range(SWA_KV_HEADS)]
        vv = [[_swa_own_kv(r, kv) for r in (vc_ref, vp_ref, vm_ref)] for kv in range(SWA_KV_HEADS)]
        heads, blocks = range(Hq), range(3)
        q2 = [q_ref[:, pair * lanes:(pair + 1) * lanes].astype(F32) * SCALE for pair in range(Hq // 2)]
        qm = [jnp.where(lane // HEAD_DIM == h % 2, q2[h // 2], 0.0).astype(BF16) for h in heads]
        raw = [[lax.dot_general(kk[h // G][b], qm[h], NT, preferred_element_type=F32) for b in blocks] for h in heads]
        terms = [_swa_terms(raw[h], bc_ref[h], bp_ref[h], far_ref[h], sink_ref[h], n) for h in heads]
        o_t = [sum(lax.dot_general(vv[h // G][b], terms[h][0][b].astype(BF16), TN, preferred_element_type=F32)
                   for b in blocks) for h in heads]
        outs = [(o_t[h] / terms[h][2]).T for h in heads]
        for pair in range(Hq // 2):
            o_ref[:, pair * lanes:(pair + 1) * lanes] = jnp.where(
                lane < HEAD_DIM, outs[2 * pair], outs[2 * pair + 1]).astype(o_ref.dtype)

    qblk, keys, vals, bias, smem = _swa_specs()
    return pl.pallas_call(
        body, out_shape=jax.ShapeDtypeStruct((T, D_MODEL), BF16), grid=(nb,),
        in_specs=[qblk] + keys + vals + [bias, bias, smem, smem],
        out_specs=qblk,
        compiler_params=_params(("parallel",)), name=name,
    )(proj, proj, proj, proj, proj, proj, proj, bc, bp, far, sinks)


def _swa_bwd(proj, dmix, bc, bp, far, sinks, *, ex=None, name):
    T, width = proj.shape
    nb = T // BLOCK
    G = SWA_GROUP
    Hq = SWA_Q_HEADS
    lanes = 2 * HEAD_DIM
    qw = Hq * HEAD_DIM
    own_w = qw + 2 * lanes

    def body(q_ref, kc_ref, kp_ref, km_ref, vc_ref, vp_ref, vm_ref, do_ref, bc_ref, bp_ref, far_ref, sink_ref,
             dp_ref, dbc_ref, dbp_ref, dbf_ref, dsk_ref, dk_acc, dv_acc):
        n = pl.program_id(0)

        @pl.when(n == 0)
        def _():
            for ref in (dk_acc, dv_acc, dbc_ref, dbp_ref, dbf_ref, dsk_ref):
                ref[...] = jnp.zeros(ref.shape, F32)

        lane = lax.broadcasted_iota(jnp.int32, (BLOCK, lanes), 1)
        kvs = range(SWA_KV_HEADS)
        kk = [[_swa_own_kv(r, kv) for r in (kc_ref, kp_ref, km_ref)] for kv in kvs]
        vv = [[_swa_own_kv(r, kv) for r in (vc_ref, vp_ref, vm_ref)] for kv in kvs]
        heads, blocks = range(Hq), range(3)
        q2 = [q_ref[:, pair * lanes:(pair + 1) * lanes].astype(F32) * SCALE for pair in range(Hq // 2)]
        d2 = [do_ref[:, pair * lanes:(pair + 1) * lanes] for pair in range(Hq // 2)]
        own = [lane // HEAD_DIM == h % 2 for h in heads]
        qm = [jnp.where(own[h], q2[h // 2], 0.0).astype(BF16) for h in heads]
        dom = [jnp.where(own[h], d2[h // 2], jnp.zeros_like(d2[0])) for h in heads]
        raw = [[lax.dot_general(kk[h // G][b], qm[h], NT, preferred_element_type=F32) for b in blocks] for h in heads]
        dp = [[lax.dot_general(vv[h // G][b], dom[h], NT, preferred_element_type=F32) for b in blocks] for h in heads]
        p, ds16 = [], []
        for h in heads:
            e, e_s, l = _swa_terms(raw[h], bc_ref[h], bp_ref[h], far_ref[h], sink_ref[h], n)
            inv = 1.0 / l
            ph = [e[b] * inv for b in blocks]
            delta = sum(jnp.sum(ph[b] * dp[h][b], axis=0, keepdims=True) for b in blocks)
            ds = [ph[b] * (dp[h][b] - delta) for b in blocks]
            dsk_ref[h] += -(e_s * inv) * delta
            dbc_ref[h] += ds[0]
            dbp_ref[h] += ds[1] + jnp.where(n == 1, ds[2], 0.0)
            dbf_ref[h] += jnp.where(n >= 2, ds[2], 0.0)
            p.append([x.astype(BF16) for x in ph])
            ds16.append([x.astype(BF16) for x in ds])
        dq_t = [sum(lax.dot_general(kk[h // G][b], ds16[h][b], TN, preferred_element_type=F32) for b in blocks)
                for h in heads]
        group = [range(kv * G, (kv + 1) * G) for kv in kvs]
        dk = [[sum(jnp.dot(ds16[h][b], qm[h], preferred_element_type=F32) for h in group[kv]) for b in blocks]
              for kv in kvs]
        dv = [[sum(jnp.dot(p[h][b], dom[h], preferred_element_type=F32) for h in group[kv]) for b in blocks]
              for kv in kvs]
        dqs = [dq_t[h].T * SCALE for h in heads]
        rows = pl.ds(pl.multiple_of(n * BLOCK, BLOCK), BLOCK)
        for pair in range(Hq // 2):
            dp_ref[rows, pair * lanes:(pair + 1) * lanes] = jnp.where(
                lane < HEAD_DIM, dqs[2 * pair], dqs[2 * pair + 1]).astype(dp_ref.dtype)
        prev_rows = pl.ds(pl.multiple_of(jnp.maximum(n - 1, 0) * BLOCK, BLOCK), BLOCK)
        for acc, ref in ((dk, dk_acc), (dv, dv_acc)):
            tot = [[a + pltpu.roll(a, HEAD_DIM, 1) for a in acc[kv]] for kv in kvs]
            both = [jnp.where(lane < HEAD_DIM, tot[0][b], tot[1][b]) for b in blocks]
            ref[rows, :] += both[0]
            ref[prev_rows, :] += both[1]
            ref[0:BLOCK, :] += both[2]

        @pl.when(n == nb - 1)
        def _():
            dp_ref[:, qw:qw + lanes] = dk_acc[...].astype(dp_ref.dtype)
            dp_ref[:, qw + lanes:own_w] = dv_acc[...].astype(dp_ref.dtype)

    qblk, keys, vals, bias, smem = _swa_specs()
    dsk = pl.BlockSpec((Hq, 1, BLOCK), lambda n: (0, 0, 0))
    grid = (nb,)
    body, x_in, x_in_specs, x_out, x_out_specs, x_scr = _carry(ex, grid, 12, 5, body)
    tile = jax.ShapeDtypeStruct((Hq, BLOCK, BLOCK), F32)
    return pl.pallas_call(
        body,
        out_shape=(jax.ShapeDtypeStruct((T, width), BF16), tile, tile, tile,
                   jax.ShapeDtypeStruct((Hq, 1, BLOCK), F32), *x_out),
        grid=grid,
        in_specs=[qblk] + keys + vals + [qblk, bias, bias, smem, smem] + x_in_specs,
        out_specs=(pl.BlockSpec((T, own_w), lambda n: (0, 0)), bias, bias, bias, dsk, *x_out_specs),
        scratch_shapes=[pltpu.VMEM((T, lanes), F32), pltpu.VMEM((T, lanes), F32)] + x_scr,
        compiler_params=_params(("arbitrary",)), name=name,
    )(proj, proj, proj, proj, proj, proj, proj, dmix, bc, bp, far, sinks, *x_in)


def _small_grads(dbc, dbp, dbf, dsk, oh_cur, oh_prev, *, name):
    Hq = dbc.shape[0]

    def body(dbc_ref, dbp_ref, dbf_ref, dsk_ref, oc_ref, op_ref, tab_ref, sink_ref):
        tab = (jnp.dot(dbc_ref[...], oc_ref[...], precision=HIGHEST, preferred_element_type=F32)
               + jnp.dot(dbp_ref[...], op_ref[...], precision=HIGHEST, preferred_element_type=F32))
        far = jnp.sum(dbf_ref[...], axis=1, keepdims=True)
        last = lax.broadcasted_iota(jnp.int32, (Hq, N_BUCKETS), 1) == N_BUCKETS - 1
        tab_ref[...] = tab + jnp.where(last, far, 0.0)
        sink_ref[...] = jnp.sum(dsk_ref[...], axis=1, keepdims=True)

    vm = pl.BlockSpec(memory_space=pltpu.VMEM)
    return pl.pallas_call(
        body, out_shape=(jax.ShapeDtypeStruct((Hq, N_BUCKETS), F32), jax.ShapeDtypeStruct((Hq, 1), F32)),
        in_specs=[vm] * 6, out_specs=(vm, vm), compiler_params=_params(), name=name,
    )(dbc.reshape(Hq, -1), dbp.reshape(Hq, -1), dbf.reshape(Hq, -1), dsk.reshape(Hq, -1), oh_cur, oh_prev)


def _coords():
    return lax.axis_index("x"), lax.axis_index("y"), lax.axis_index("c")


class _Exchange:
    def __init__(self, inputs, out_shapes, scratch, start, finish):
        self.inputs, self.out_shapes, self.scratch, self.start, self.finish = inputs, out_shapes, scratch, start, finish


def _carry(ex, grid, n_in, n_out, body):
    if ex is None:
        return body, [], [], [], [], []
    ni, no = len(ex.inputs), len(ex.out_shapes)

    def at_step(which):
        cond = None
        for axis, n in enumerate(grid):
            c = pl.program_id(axis) == (0 if which == "first" else n - 1)
            cond = c if cond is None else cond & c
        return cond

    def wrapped(*refs):
        refs = list(refs)
        n_own_scr = len(refs) - (n_in + ni + n_out + no) - len(ex.scratch)
        own_in, side_in = refs[:n_in], refs[n_in:n_in + ni]
        own_out = refs[n_in + ni:n_in + ni + n_out]
        side_out = refs[n_in + ni + n_out:n_in + ni + n_out + no]
        rest = refs[n_in + ni + n_out + no:]
        own_scr, sems = rest[:n_own_scr], rest[n_own_scr:]

        @pl.when(at_step("first"))
        def _():
            ex.start(side_in, side_out, sems)

        body(*own_in, *own_out, *own_scr)

        @pl.when(at_step("last"))
        def _():
            ex.finish(side_in, side_out, sems)

    hbm = pl.BlockSpec(memory_space=pl.ANY)
    return wrapped, list(ex.inputs), [hbm] * ni, list(ex.out_shapes), [hbm] * no, list(ex.scratch)


def _run_exchange(ex, *, name):
    ni, no = len(ex.inputs), len(ex.out_shapes)

    def body(*refs):
        ins, outs, sems = refs[:ni], refs[ni:ni + no], refs[ni + no:]
        ex.start(ins, outs, sems)
        ex.finish(ins, outs, sems)

    hbm = pl.BlockSpec(memory_space=pl.ANY)
    return pl.pallas_call(
        body, out_shape=tuple(ex.out_shapes), in_specs=[hbm] * ni, out_specs=tuple([hbm] * no),
        scratch_shapes=ex.scratch, compiler_params=_params(), name=name)(*ex.inputs)


def _gather_exchange(shards):
    nt = len(shards)

    def copies(ins, outs, sems):
        send_sems, recv_sems, local_sems = sems
        x, y, c = _coords()
        me, sibling = (x, y, c), (x, y, 1 - c)
        chips = [(1 - x, y), (x, 1 - y), (1 - x, 1 - y)]

        def slot(t, dev):
            return outs[t].at[4 * dev[0] + 2 * dev[1] + dev[2]]

        def copy(t, k, block, to, src=None):
            dst = slot(t, block)
            return pltpu.make_async_remote_copy(
                src_ref=dst if src is None else src, dst_ref=dst,
                send_sem=send_sems.at[t, k], recv_sem=recv_sems.at[t, k], device_id=to, device_id_type=MESH)

        mine = [pltpu.make_async_copy(ins[t], slot(t, me), local_sems.at[t]) for t in range(nt)]
        first = []
        for t in range(nt):
            first.append(copy(t, 0, me, sibling, src=ins[t]))
            first += [copy(t, 1 + j, me, (*chip, c), src=ins[t]) for j, chip in enumerate(chips)]
        return copy, mine, first, me, sibling, chips, c

    def start(ins, outs, sems):
        _, mine, first, *_ = copies(ins, outs, sems)
        for cp in mine + first:
            cp.start()

    def finish(ins, outs, sems):
        copy, mine, first, me, sibling, chips, c = copies(ins, outs, sems)
        passed = []
        for j, chip in enumerate(chips):
            for t in range(nt):
                copy(t, 1 + j, (*chip, c), me).wait_recv()
                cp = copy(t, 4 + j, (*chip, c), sibling)
                cp.start()
                passed.append(cp)
        for t in range(nt):
            copy(t, 0, sibling, me).wait_recv()
            for j, chip in enumerate(chips):
                copy(t, 4 + j, (*chip, 1 - c), me).wait_recv()
        for cp in first + passed:
            cp.wait_send()
        for cp in mine:
            cp.wait()

    return _Exchange(
        list(shards), [jax.ShapeDtypeStruct((N_DEV,) + s.shape, s.dtype) for s in shards],
        [pltpu.SemaphoreType.DMA((nt, 7)), pltpu.SemaphoreType.DMA((nt, 7)), pltpu.SemaphoreType.DMA((nt,))],
        start, finish)


def _swap_exchange(arrays, n_slices, copies):
    nt = len(arrays)

    def start(ins, outs, sems):
        for cp in copies(ins, outs, sems):
            cp.start()

    def finish(ins, outs, sems):
        sends = copies(ins, outs, sems)
        for cp in sends:
            cp.wait_recv()
        for cp in sends:
            cp.wait_send()

    return _Exchange(
        list(arrays), [jax.ShapeDtypeStruct((n_slices,) + a.shape[1:], a.dtype) for a in arrays],
        [pltpu.SemaphoreType.DMA((nt, n_slices)), pltpu.SemaphoreType.DMA((nt, n_slices))], start, finish)


def _cores_exchange(gs):
    def copies(ins, outs, sems):
        send_sems, recv_sems = sems
        x, y, c = _coords()
        return [pltpu.make_async_remote_copy(
            src_ref=ins[t].at[2 * j + (1 - c)], dst_ref=outs[t].at[j],
            send_sem=send_sems.at[t, j], recv_sem=recv_sems.at[t, j], device_id=(x, y, 1 - c), device_id_type=MESH)
            for t in range(len(gs)) for j in range(4)]

    return _swap_exchange(gs, 4, copies)


def _chips_exchange(ps):
    def copies(ins, outs, sems):
        send_sems, recv_sems = sems
        x, y, c = _coords()
        peers = [(1 - x, y), (x, 1 - y), (1 - x, 1 - y)]
        return [pltpu.make_async_remote_copy(
            src_ref=ins[t].at[2 * px + py], dst_ref=outs[t].at[k],
            send_sem=send_sems.at[t, k], recv_sem=recv_sems.at[t, k], device_id=(px, py, c), device_id_type=MESH)
            for t in range(len(ps)) for k, (px, py) in enumerate(peers)]

    return _swap_exchange(ps, 3, copies)


def _add_cores(g, r, core, *, name):
    _, A, B = g.shape
    ta = _tile(A, 512, 16)

    def body(core_ref, a_ref, b_ref, o_ref, o16_ref):
        s = a_ref[...] + b_ref[...]
        o_ref[...] = s
        o16_ref[...] = s.astype(BF16)

    blk = (None, ta, B)
    out = pl.BlockSpec(blk, lambda j, i, core_ref: (j, i, 0))
    return pl.pallas_call(
        body, out_shape=(jax.ShapeDtypeStruct((4, A, B), F32), jax.ShapeDtypeStruct((4, A, B), BF16)),
        grid_spec=pltpu.PrefetchScalarGridSpec(
            num_scalar_prefetch=1, grid=(4, A // ta),
            in_specs=[pl.BlockSpec(blk, lambda j, i, core_ref: (2 * j + core_ref[0], i, 0)),
                      pl.BlockSpec(blk, lambda j, i, core_ref: (j, i, 0))],
            out_specs=(out, out)),
        compiler_params=_params(("parallel", "parallel")), name=name)(core, g, r)


def _adamw_math(w, g, m, v):
    m = ADAM_B1 * m + (1.0 - ADAM_B1) * g
    v = ADAM_B2 * v + (1.0 - ADAM_B2) * (g * g)
    m_hat = m / (1.0 - ADAM_B1 ** ADAM_STEP)
    v_hat = v / (1.0 - ADAM_B2 ** ADAM_STEP)
    delta = -ADAM_LR * (m_hat / (jnp.sqrt(v_hat) + ADAM_EPS) + ADAM_WD * w)
    return delta, m, v


def _sum_adamw(p, r, chip, w, m, v, *, segs, ta, name):
    Aw, Bw = w.shape
    Bg = p.shape[2]
    assert Aw % ta == 0

    def body(chip_ref, p_ref, r0, r1, r2, w_ref, m_ref, v_ref, g_out, d_out, m_out, v_out):
        for gc, wc, n in segs:
            g = ((p_ref[:, gc:gc + n] + r0[:, gc:gc + n].astype(F32)) + r1[:, gc:gc + n].astype(F32)
                 ) + r2[:, gc:gc + n].astype(F32)
            delta, m_new, v_new = _adamw_math(w_ref[:, wc:wc + n], g, m_ref[:, wc:wc + n], v_ref[:, wc:wc + n])
            g_out[:, wc:wc + n] = g
            d_out[:, wc:wc + n] = delta
            m_out[:, wc:wc + n] = m_new
            v_out[:, wc:wc + n] = v_new

    gblk = (None, ta, Bg)
    row = pl.BlockSpec((ta, Bw), lambda i, chip_ref: (i, 0))
    rspecs = [pl.BlockSpec(gblk, (lambda i, chip_ref, k=k: (k, i, 0))) for k in range(3)]
    shp = jax.ShapeDtypeStruct((Aw, Bw), F32)
    return pl.pallas_call(
        body, out_shape=(shp, shp, shp, shp),
        grid_spec=pltpu.PrefetchScalarGridSpec(
            num_scalar_prefetch=1, grid=(Aw // ta,),
            in_specs=[pl.BlockSpec(gblk, lambda i, chip_ref: (chip_ref[0], i, 0))] + rspecs + [row, row, row],
            out_specs=(row, row, row, row)),
        compiler_params=_params(("parallel",)), name=name)(chip, p, r, r, r, w, m, v)


def _adamw(w, g, m, v, *, name):
    def body(w_ref, g_ref, m_ref, v_ref, d_out, m_out, v_out):
        delta, m_new, v_new = _adamw_math(w_ref[...], g_ref[...], m_ref[...], v_ref[...])
        d_out[...] = delta
        m_out[...] = m_new
        v_out[...] = v_new

    vm = pl.BlockSpec(memory_space=pltpu.VMEM)
    shp = jax.ShapeDtypeStruct(w.shape, F32)
    return pl.pallas_call(body, out_shape=(shp, shp, shp), in_specs=[vm] * 4, out_specs=(vm, vm, vm),
                          compiler_params=_params(), name=name)(w, g, m, v)


def _small_allreduce_adamw(s, w, m, v, *, name):
    R, W = s.shape

    def body(s_ref, w_ref, m_ref, v_ref, g_out, d_out, m_out, v_out, gath, send_sems, recv_sems):
        x, y, c = _coords()
        mine = 4 * x + 2 * y + c
        gath[mine] = s_ref[...]
        peers = [((1 - x) if k & 4 else x, (1 - y) if k & 2 else y, (1 - c) if k & 1 else c) for k in range(1, N_DEV)]
        sends = []
        for k in range(1, N_DEV):
            peer = peers[k - 1]
            sends.append(pltpu.make_async_remote_copy(
                src_ref=s_ref, dst_ref=gath.at[mine], send_sem=send_sems.at[k - 1], recv_sem=recv_sems.at[k - 1],
                device_id=peer, device_id_type=MESH))
        for cp in sends:
            cp.start()
        for k in range(1, N_DEV):
            peer = peers[k - 1]
            pltpu.make_async_remote_copy(
                src_ref=s_ref, dst_ref=gath.at[4 * peer[0] + 2 * peer[1] + peer[2]],
                send_sem=send_sems.at[k - 1], recv_sem=recv_sems.at[k - 1],
                device_id=peer, device_id_type=MESH).wait_recv()
        for cp in sends:
            cp.wait_send()
        g = gath[0]
        for d in range(1, N_DEV):
            g = g + gath[d]
        delta, m_new, v_new = _adamw_math(w_ref[...], g, m_ref[...], v_ref[...])
        g_out[...] = g
        d_out[...] = delta
        m_out[...] = m_new
        v_out[...] = v_new

    vm = pl.BlockSpec(memory_space=pltpu.VMEM)
    shp = jax.ShapeDtypeStruct((R, W), F32)
    return pl.pallas_call(
        body, out_shape=(shp, shp, shp, shp), in_specs=[vm] * 4, out_specs=(vm, vm, vm, vm),
        scratch_shapes=[pltpu.VMEM((N_DEV, R, W), F32), pltpu.SemaphoreType.DMA((N_DEV - 1,)),
                        pltpu.SemaphoreType.DMA((N_DEV - 1,))],
        compiler_params=_params(), name=name)(s, w, m, v)


def _pack_small(rel_bias, g1, g2, g3, g4, b_forget, sinks, extra=None, meta=None):
    misc = jnp.concatenate([rel_bias.reshape(-1), b_forget.reshape(-1), sinks.reshape(-1)])
    misc = jnp.concatenate([misc, jnp.zeros((D_MODEL - misc.shape[0],), F32)])[None]
    last = jnp.zeros((1, D_MODEL), F32) if extra is None else extra
    meta = jnp.zeros((N_META, D_MODEL), F32) if meta is None else meta
    return jnp.concatenate([g1, g2, g3, g4, misc, last, jnp.zeros((2, D_MODEL), F32), meta], axis=0)


def _unpack_small(p):
    nrb = N_BUCKETS * SWA_Q_HEADS
    misc = p[4]
    return dict(rel_bias=misc[:nrb].reshape(N_BUCKETS, SWA_Q_HEADS), ln_pre_mix=p[0:1], ln_post_mix=p[1:2],
                ln_pre_ffn=p[2:3], ln_post_ffn=p[3:4], b_forget=misc[nrb:nrb + 8].reshape(1, 8),
                sinks=misc[nrb + 8:nrb + 16].reshape(1, 8))


def _proj_runs():
    gw = FOX_GROUP * HEAD_DIM
    swa = SWA_Q_W + 2 * SWA_KV_HEADS * HEAD_DIM
    runs = [(0, swa)]
    for grp in range(FOX_HEADS // FOX_GROUP):
        runs += [(swa + part * FOX_W + grp * gw, swa + part * FOX_W + (grp + 1) * gw) for part in range(3)]
    return runs


def _device_shards(qkv, gate, shard, padded):
    pos, segments = 0, []
    for start, stop in _proj_runs():
        segments.append((start, stop, qkv, pos))
        pos += stop - start
    segments.append((pos, pos + gate.shape[1], gate, 0))
    total = pos + gate.shape[1]
    assert total % shard == 0
    zeros = jnp.zeros((qkv.shape[0], padded - shard), qkv.dtype)
    out = []
    for d in range(total // shard):
        lo, hi = d * shard, (d + 1) * shard
        pieces = [arr[:, src + max(lo, s) - s:src + min(hi, e) - s]
                  for s, e, arr, src in sorted(segments, key=lambda seg: seg[0]) if max(lo, s) < min(hi, e)]
        out.append(jnp.concatenate(pieces + [zeros], axis=1))
    return jnp.stack(out)


def kernel(x, meta_tokens, rel_bias, ln_pre_mix, ln_post_mix, ln_pre_ffn, ln_post_ffn, w_in, b_forget, sinks, w_out, w_gate_up, w_down, loss_target, m_meta_tokens, m_rel_bias, m_ln_pre_mix, m_ln_post_mix, m_ln_pre_ffn, m_ln_post_ffn, m_w_in, m_b_forget, m_sinks, m_w_out, m_w_gate_up, m_w_down, v_meta_tokens, v_rel_bias, v_ln_pre_mix, v_ln_post_mix, v_ln_pre_ffn, v_ln_post_ffn, v_w_in, v_b_forget, v_sinks, v_w_out, v_w_gate_up, v_w_down):
    seq = x.shape[1]
    T = BLOCK + seq
    assert T % FOX_TILE == 0
    nq = T // FOX_TILE
    tm = _tile(T, 1056)
    cin = w_in.shape[2]
    hid = w_down.shape[1]
    assert w_gate_up.shape[2] == 2 * hid and cin <= W_IN_PAD and hid <= HID_PAD

    x_i, y_i, c_i = _coords()
    core = jnp.reshape(c_i, (1,)).astype(jnp.int32)
    chip = jnp.reshape(2 * x_i + y_i, (1,)).astype(jnp.int32)
    w_in_s = jnp.pad(w_in[0].astype(BF16), ((0, 0), (0, W_IN_PAD - cin)))
    w_gu_s = jnp.pad(w_gate_up[0].astype(BF16).reshape(D_MODEL, 2, hid), ((0, 0), (0, 0), (0, HID_PAD - hid)))
    w_gu_s = w_gu_s.reshape(D_MODEL, 2 * HID_PAD)
    w_down_s = jnp.pad(w_down[0].astype(BF16), ((0, HID_PAD - hid), (0, 0)))
    g_in, g_meta = _run_exchange(_gather_exchange([w_in_s, meta_tokens]), name="ag_w_in")
    gather_rest = _gather_exchange([w_out[0].astype(BF16), w_gu_s, w_down_s])
    w_in_full = g_in[:, :, :cin].transpose(1, 0, 2).reshape(D_MODEL, N_DEV * cin)
    w_qkv = jnp.concatenate([w_in_full[:, a:b] for a, b in _proj_runs()], axis=1)
    w_f = jnp.pad(w_in_full[:, D_QKV:], ((0, 0), (0, BLOCK - FOX_HEADS)))
    meta_full = g_meta.transpose(1, 0, 2).reshape(N_META, D_MODEL)

    h0 = jnp.concatenate([jnp.zeros((PAD_ROWS, D_MODEL), F32), meta_full, x[0]], axis=0)
    target = jnp.concatenate([jnp.zeros((BLOCK, D_MODEL), F32), loss_target[0]], axis=0)
    hn1, hn1_t = _rms_fwd(h0, ln_pre_mix, name="rms_pre_mix")
    proj = _matmul(hn1, w_qkv, out_dtype=BF16, tm=tm, tn=768, name="mm_in_proj")
    proj_f = _matmul(hn1, w_f, out_dtype=F32, tm=tm, tn=BLOCK, name="mm_in_proj_f")

    f_t = proj_f[:, :FOX_HEADS].T
    bf_col = b_forget.reshape(FOX_HEADS, 1)

    oh_cur, oh_prev = _bucket_onehots()
    bias_c = jnp.einsum("pb,bh->hp", jnp.asarray(oh_cur), rel_bias, precision=HIGHEST).reshape(8, BLOCK, BLOCK)
    bias_p = jnp.einsum("pb,bh->hp", jnp.asarray(oh_prev), rel_bias, precision=HIGHEST).reshape(8, BLOCK, BLOCK)
    far = rel_bias[N_BUCKETS - 1]
    sink_v = sinks[0]
    mix_a = _swa_fwd(proj, bias_c, bias_p, far, sink_v, name="swa_fwd")

    _, cum_col = _fox_gates_fwd(f_t, bf_col, name="fox_gates_fwd")
    q_b, k_b, v_b = _fox_prep(proj, cum_col, name="fox_prep")
    mix, lse_row, g_out, g_gu, g_down = _fox_fwd(q_b, k_b, v_b, mix_a, ex=gather_rest, name="fox_fwd")
    w_out_full = g_out.reshape(D_MODEL, D_MODEL)
    w_down_full = g_down.reshape(N_DEV * HID_PAD, D_MODEL)

    a1 = _matmul(mix, w_out_full, out_dtype=F32, tm=tm, tn=512, name="mm_out_proj")
    h1, hn2, hn2_t = _post_res_norm(a1, ln_post_mix, h0, ln_pre_ffn, name="post_mix_pre_ffn")
    gate, up, act, act_t = _gate_up_swiglu(hn2, g_gu, name="mm_gate_up")
    ff = _matmul(act, w_down_full, out_dtype=F32, tm=tm, tn=512, name="mm_down")
    dh2, dff, dg_post_ffn, loss_acc = _loss_head(ff, ln_post_ffn, h1, target, name="loss_head")

    dgu = _d_act_swiglu(dff, w_down_full, gate, up, name="mm_d_act")
    d_w_down = _matmul(act_t, dff, out_dtype=F32, tm=768, tn=512, name="mm_dw_down")
    dhn2 = _matmul(dgu, g_gu, nt=True, b_shards=True, out_dtype=F32, tm=_tile(T, 528), tn=512, name="mm_d_hn2")
    d_w_gu = _matmul(hn2_t, dgu, out_shards=True, out_dtype=F32, tm=512, tn=2 * HID_PAD, name="mm_dw_gate_up")
    dh1, dg_pre_ffn, da1, dg_post_mix = _rms_bwd(h1, ln_pre_ffn, dhn2, dh2, out_dtype=F32,
                                                 then=(a1, ln_post_mix), name="rms_bwd_pre_ffn_post_mix")
    dmix = _matmul(da1, w_out_full, nt=True, out_dtype=BF16, tm=tm, tn=512, name="mm_d_mix")
    d_w_out = _matmul(mix, da1, ta=True, out_dtype=F32, tm=512, tn=512, name="mm_dw_out")

    ffn_grads = [d_w_out.reshape(N_DEV, -1, D_MODEL), d_w_gu, d_w_down.reshape(N_DEV, HID_PAD, D_MODEL)]
    dproj_a, dbc, dbp, dbf, dsk, *ffn_sibling = _swa_bwd(
        proj, dmix, bias_c, bias_p, far, sink_v, ex=_cores_exchange(ffn_grads), name="swa_bwd")
    d_tab, d_sink = _small_grads(dbc, dbp, dbf, dsk, jnp.asarray(oh_cur), jnp.asarray(oh_prev), name="small_grads")
    ffn_sums = [_add_cores(g, r, core, name="rs_add_" + t)
                for g, r, t in zip(ffn_grads, ffn_sibling, ["w_out", "w_gate_up", "w_down"])]

    do_b = _fox_prep_bwd(dmix, mix, name="fox_prep_bwd")
    dproj, dcq, dck, *ffn_chips = _fox_bwd(
        q_b, k_b, v_b, do_b, lse_row, dproj_a, ex=_chips_exchange([s[1] for s in ffn_sums]), name="fox_bwd")
    df_t, d_bf = _fox_gates_bwd(dcq.reshape(FOX_HEADS, T), dck.reshape(FOX_HEADS, T), f_t, bf_col,
                                name="fox_gates_bwd")
    df = jnp.pad(df_t.T.astype(BF16), ((0, 0), (0, BLOCK - FOX_HEADS)))

    d_w_qkv = _matmul(hn1_t, dproj, out_dtype=F32, tm=512, tn=768, name="mm_dw_in")
    d_w_f = _matmul(hn1_t, df, out_dtype=F32, tm=512, tn=BLOCK, name="mm_dw_in_f")
    d_w_in = _device_shards(d_w_qkv, d_w_f[:, :FOX_HEADS], cin, W_IN_PAD)
    dhn1, in_sibling = _matmul(dproj, w_qkv, nt=True, out_dtype=F32, tm=tm, tn=512,
                               ex=_cores_exchange([d_w_in]), name="mm_d_hn1")
    in_sum = _add_cores(d_w_in, in_sibling, core, name="rs_add_w_in")
    dh0, dg_pre_mix, in_chips = _rms_bwd(h0, ln_pre_mix, dhn1, dh1, out_dtype=F32, dy2=(df, w_f),
                                         ex=_chips_exchange([in_sum[1]]), name="rms_bwd_pre_mix")
    grad_x = dh0[BLOCK:][None]
    d_meta = dh0[PAD_ROWS:BLOCK]

    tags = ["w_in", "w_out", "w_gate_up", "w_down"]
    chip_sum = [in_sum[0]] + [s[0] for s in ffn_sums]
    from_chips = [in_chips] + list(ffn_chips)
    shard_w = [(w_in, m_w_in, v_w_in), (w_out, m_w_out, v_w_out), (w_gate_up, m_w_gate_up, v_w_gate_up),
               (w_down, m_w_down, v_w_down)]
    segs = [[(0, 0, cin)], [(0, 0, D_MODEL)], [(0, 0, hid), (HID_PAD, hid, hid)], [(0, 0, D_MODEL)]]
    tas = [256, BLOCK, 256, hid]
    big = [{}, {}, {}, {}]
    for i, t in enumerate(tags):
        w_t, m_t, v_t = shard_w[i]
        res = _sum_adamw(chip_sum[i], from_chips[i], chip, w_t[0], m_t[0], v_t[0], segs=segs[i], ta=tas[i],
                         name="rs_adamw_" + t)
        for kind in range(4):
            big[kind][t] = res[kind][None]

    loss_row = jnp.pad(loss_acc[0:1, 0:1] * (0.5 / D_MODEL), ((0, 0), (0, D_MODEL - 1)))
    s_small = _pack_small(d_tab.T, dg_pre_mix, dg_post_mix, dg_pre_ffn, dg_post_ffn, d_bf, d_sink,
                          extra=loss_row, meta=d_meta)
    w_s = _pack_small(rel_bias, ln_pre_mix, ln_post_mix, ln_pre_ffn, ln_post_ffn, b_forget, sinks)
    m_s = _pack_small(m_rel_bias, m_ln_pre_mix, m_ln_post_mix, m_ln_pre_ffn, m_ln_post_ffn, m_b_forget, m_sinks)
    v_s = _pack_small(v_rel_bias, v_ln_pre_mix, v_ln_post_mix, v_ln_pre_ffn, v_ln_post_ffn, v_b_forget, v_sinks)
    small = _small_allreduce_adamw(s_small, w_s, m_s, v_s, name="small_allreduce_adamw")
    loss = small[0][5, 0]
    mcols = meta_tokens.shape[1]
    g_meta_mine = lax.dynamic_slice(small[0][8:8 + N_META], (0, (4 * x_i + 2 * y_i + c_i) * mcols), (N_META, mcols))
    big[0]["meta_tokens"] = g_meta_mine
    for kind, arr in enumerate(_adamw(meta_tokens, g_meta_mine, m_meta_tokens, v_meta_tokens, name="adamw_meta")):
        big[kind + 1]["meta_tokens"] = arr
    small = [_unpack_small(p) for p in small]

    names = ["meta_tokens", "rel_bias", "ln_pre_mix", "ln_post_mix", "ln_pre_ffn", "ln_post_ffn", "w_in",
             "b_forget", "sinks", "w_out", "w_gate_up", "w_down"]
    outs = [loss, grad_x]
    for kind in range(4):
        for nme in names:
            outs.append(big[kind][nme] if nme in big[kind] else small[kind][nme])
    return tuple(outs)
```

```python
import math

import numpy as np
import jax
import jax.numpy as jnp
from jax import lax
from jax.experimental import pallas as pl
from jax.experimental.pallas import tpu as pltpu

F32 = jnp.float32
BF16 = jnp.bfloat16
HIGHEST = lax.Precision.HIGHEST
MESH = pl.DeviceIdType.MESH

N_DEV = 8
D_MODEL = 1024
N_META = 16
HEAD_DIM = 64
SWA_Q_HEADS = 8
SWA_KV_HEADS = 2
SWA_GROUP = 4
FOX_HEADS = 8
FOX_W = FOX_HEADS * HEAD_DIM
SWA_Q_W = SWA_Q_HEADS * HEAD_DIM
BLOCK = 128
PAD_ROWS = BLOCK - N_META
N_BUCKETS = 32
MAX_DISTANCE = 128
D_FF = 2816
D_QKV = 2304
D_PROJ = D_QKV + FOX_HEADS
D_PROJ_PAD = 2560
EPS = 1e-6
NEG = -1e30
SCALE = HEAD_DIM ** -0.5
ADAM_LR, ADAM_B1, ADAM_B2, ADAM_EPS, ADAM_WD, ADAM_STEP = 0.001, 0.9, 0.999, 1e-08, 0.01, 10
VMEM_LIMIT = 56 * 1024 * 1024
FOX_TILE = 384
FOX_GROUP = 4
W_IN_PAD = 384
HID_PAD = 384

NT = (((1,), (1,)), ((), ()))
NN = (((1,), (0,)), ((), ()))
TN = (((0,), (0,)), ((), ()))


def _params(sem=None, **kw):
    if sem is not None:
        kw["dimension_semantics"] = sem
    return pltpu.CompilerParams(vmem_limit_bytes=VMEM_LIMIT, **kw)


def _tile(n, target, mult=16):
    best = None
    for t in range(mult, min(n, target) + 1, mult):
        if n % t == 0:
            best = t
    assert best is not None, (n, target)
    return best


def _matmul(a, b, *, nt=False, ta=False, b_shards=False, out_shards=False, out_dtype, tm, tn=None, tk=None,
            ex=None, name):
    M, K = a.shape[::-1] if ta else a.shape
    assert not (ta and (nt or b_shards))
    k_shards = b.shape[0] if (b_shards and nt) else 0
    if k_shards:
        N, ks = b.shape[1], b.shape[2]
        assert tk is None and K == k_shards * ks
    elif b_shards:
        N, tn = b.shape[0] * b.shape[2], b.shape[2]
    else:
        N = b.shape[0] if nt else b.shape[1]
    tk = K if tk is None else tk
    assert M % tm == 0 and N % tn == 0 and K % tk == 0, (name, a.shape, b.shape, tm, tn, tk)
    nk = K // tk
    dn = NT if nt else (TN if ta else NN)
    a_spec = pl.BlockSpec((tk, tm), lambda i, j, k: (k, i)) if ta else pl.BlockSpec((tm, tk), lambda i, j, k: (i, k))

    def body(a_ref, b_ref, o_ref, *scr):
        if k_shards:
            part = sum(lax.dot_general(a_ref[:, s * ks:(s + 1) * ks], b_ref[s], NT, preferred_element_type=F32)
                       for s in range(k_shards))
        else:
            part = lax.dot_general(a_ref[...], b_ref[...], dn, preferred_element_type=F32)
        if nk == 1:
            o_ref[...] = part.astype(o_ref.dtype)
        else:
            acc = scr[0]
            k = pl.program_id(2)

            @pl.when(k == 0)
            def _():
                acc[...] = part

            @pl.when(k > 0)
            def _():
                acc[...] += part

            @pl.when(k == nk - 1)
            def _():
                o_ref[...] = acc[...].astype(o_ref.dtype)

    if k_shards:
        b_spec = pl.BlockSpec((k_shards, tn, ks), lambda i, j, k: (0, j, 0))
    elif b_shards:
        b_spec = pl.BlockSpec((None, tk, tn), lambda i, j, k: (j, k, 0))
    elif nt:
        b_spec = pl.BlockSpec((tn, tk), lambda i, j, k: (j, k))
    else:
        b_spec = pl.BlockSpec((tk, tn), lambda i, j, k: (k, j))
    if out_shards:
        out_shape = jax.ShapeDtypeStruct((N // tn, M, tn), out_dtype)
        out_spec = pl.BlockSpec((None, tm, tn), lambda i, j, k: (j, i, 0))
    else:
        out_shape = jax.ShapeDtypeStruct((M, N), out_dtype)
        out_spec = pl.BlockSpec((tm, tn), lambda i, j, k: (i, j))
    grid = (M // tm, N // tn, nk)
    body, x_in, x_in_specs, x_out, x_out_specs, x_scr = _carry(ex, grid, 2, 1, body)
    res = pl.pallas_call(
        body,
        out_shape=(out_shape, *x_out),
        grid=grid,
        in_specs=[a_spec, b_spec] + x_in_specs,
        out_specs=(out_spec, *x_out_specs),
        scratch_shapes=([pltpu.VMEM((tm, tn), F32)] if nk > 1 else []) + x_scr,
        compiler_params=_params(("parallel", "parallel", "arbitrary") if ex is None else ("arbitrary",) * 3),
        name=name,
    )(a, b, *x_in)
    return res[0] if ex is None else res


def _rstd(x):
    return lax.rsqrt(jnp.mean(x * x, axis=-1, keepdims=True) + EPS)


def _rms_fwd(x, g, *, name):
    T, D = x.shape
    tm = _tile(T, 512)

    def body(x_ref, g_ref, o_ref, ot_ref):
        x = x_ref[...]
        y = x * _rstd(x) * g_ref[...]
        o_ref[...] = y.astype(o_ref.dtype)
        ot_ref[...] = y.T.astype(ot_ref.dtype)

    return pl.pallas_call(
        body, out_shape=(jax.ShapeDtypeStruct((T, D), BF16), jax.ShapeDtypeStruct((D, T), BF16)), grid=(T // tm,),
        in_specs=[pl.BlockSpec((tm, D), lambda i: (i, 0)), pl.BlockSpec((1, D), lambda i: (0, 0))],
        out_specs=(pl.BlockSpec((tm, D), lambda i: (i, 0)), pl.BlockSpec((D, tm), lambda i: (0, i))),
        compiler_params=_params(("parallel",)), name=name)(x, g)


def _post_res_norm(a, g_post, h, g_pre, *, name):
    T, D = a.shape
    tm = _tile(T, 384, BLOCK)

    def body(a_ref, gp_ref, h_ref, gn_ref, h1_ref, o_ref, ot_ref):
        a = a_ref[...]
        h1 = h_ref[...] + a * _rstd(a) * gp_ref[...]
        h1_ref[...] = h1
        y = h1 * _rstd(h1) * gn_ref[...]
        o_ref[...] = y.astype(o_ref.dtype)
        ot_ref[...] = y.T.astype(ot_ref.dtype)

    row = pl.BlockSpec((tm, D), lambda i: (i, 0))
    vec = pl.BlockSpec((1, D), lambda i: (0, 0))
    return pl.pallas_call(
        body, out_shape=(jax.ShapeDtypeStruct((T, D), F32), jax.ShapeDtypeStruct((T, D), BF16),
                         jax.ShapeDtypeStruct((D, T), BF16)), grid=(T // tm,),
        in_specs=[row, vec, row, vec], out_specs=(row, row, pl.BlockSpec((D, tm), lambda i: (0, i))),
        compiler_params=_params(("parallel",)), name=name)(a, g_post, h, g_pre)


def _loss_head(a, g, h, target, *, name):
    T, D = a.shape
    tm = _tile(T, 512)

    def body(a_ref, g_ref, h_ref, t_ref, dy_ref, da_ref, dg_ref, loss_ref):
        i = pl.program_id(0)
        a = a_ref[...]
        r = _rstd(a)
        ah = a * r
        y = h_ref[...] + ah * g_ref[...]
        rows = i * tm + lax.broadcasted_iota(jnp.int32, (tm, 1), 0)
        err = jnp.where(rows >= BLOCK, y - t_ref[...], 0.0)
        dy = err / D
        dy_ref[...] = dy
        dah = dy * g_ref[...]
        da_ref[...] = (r * (dah - ah * jnp.mean(dah * ah, axis=-1, keepdims=True))).astype(da_ref.dtype)
        part = jnp.sum(jnp.sum(err * err, axis=1, keepdims=True), axis=0, keepdims=True)

        @pl.when(i == 0)
        def _():
            loss_ref[...] = jnp.zeros_like(loss_ref)
            dg_ref[...] = jnp.zeros_like(dg_ref)

        loss_ref[...] += jnp.broadcast_to(part, loss_ref.shape)
        dg_ref[...] += jnp.sum(dy * ah, axis=0, keepdims=True)

    row = pl.BlockSpec((tm, D), lambda i: (i, 0))
    vec = pl.BlockSpec((1, D), lambda i: (0, 0))
    return pl.pallas_call(
        body, out_shape=(jax.ShapeDtypeStruct((T, D), F32), jax.ShapeDtypeStruct((T, D), BF16),
                         jax.ShapeDtypeStruct((1, D), F32), jax.ShapeDtypeStruct((8, 128), F32)),
        grid=(T // tm,),
        in_specs=[row, vec, row, row],
        out_specs=(row, row, vec, pl.BlockSpec((8, 128), lambda i: (0, 0))),
        compiler_params=_params(("arbitrary",)), name=name)(a, g, h, target)


def _rms_bwd(x, g, dy, res, *, out_dtype, dy2=None, then=None, ex=None, name):
    T, D = x.shape
    tm = _tile(T, 512)
    has_res = res is not None
    has_dy2 = 2 if dy2 is not None else 0
    n_in = 3 + has_dy2 + has_res + (2 if then is not None else 0)
    n_out = 2 + (2 if then is not None else 0)

    def pull_back(x, g, dy):
        r = _rstd(x)
        xh = x * r
        dxh = dy * g
        return r * (dxh - xh * jnp.mean(dxh * xh, axis=-1, keepdims=True)), jnp.sum(dy * xh, axis=0, keepdims=True)

    def body(*refs):
        ins, outs = refs[:n_in], refs[n_in:]
        i = pl.program_id(0)

        @pl.when(i == 0)
        def _():
            for ref in outs[1::2]:
                ref[...] = jnp.zeros_like(ref)

        dy_all = ins[2][...].astype(F32)
        if has_dy2:
            dy_all = dy_all + lax.dot_general(ins[3][...], ins[4][...], NT, preferred_element_type=F32)
        dx, dg = pull_back(ins[0][...], ins[1][...], dy_all)
        if has_res:
            dx = dx + ins[3 + has_dy2][...]
        outs[0][...] = dx.astype(outs[0].dtype)
        outs[1][...] += dg
        if then is not None:
            dx2, dg2 = pull_back(ins[n_in - 2][...], ins[n_in - 1][...], dx)
            outs[2][...] = dx2.astype(outs[2].dtype)
            outs[3][...] += dg2

    row = pl.BlockSpec((tm, D), lambda i: (i, 0))
    vec = pl.BlockSpec((1, D), lambda i: (0, 0))
    ins = [x, g, dy] + (list(dy2) if has_dy2 else []) + ([res] if has_res else []) + (list(then) if then is not None else [])
    dy2_specs = ([pl.BlockSpec((tm, dy2[0].shape[1]), lambda i: (i, 0)), pl.BlockSpec(dy2[1].shape, lambda i: (0, 0))]
                 if has_dy2 else [])
    in_specs = [row, vec, row] + dy2_specs + ([row] if has_res else []) + ([row, vec] if then is not None else [])
    out_shape = [jax.ShapeDtypeStruct((T, D), out_dtype), jax.ShapeDtypeStruct((1, D), F32)]
    out_specs = [row, vec]
    if then is not None:
        out_shape += [jax.ShapeDtypeStruct((T, D), BF16), jax.ShapeDtypeStruct((1, D), F32)]
        out_specs += [row, vec]
    grid = (T // tm,)
    body, x_in, x_in_specs, x_out, x_out_specs, x_scr = _carry(ex, grid, n_in, n_out, body)
    return pl.pallas_call(
        body, out_shape=(*out_shape, *x_out), grid=grid,
        in_specs=in_specs + x_in_specs, out_specs=(*out_specs, *x_out_specs), scratch_shapes=x_scr,
        compiler_params=_params(("arbitrary",)), name=name)(*ins, *x_in)


def _gate_up_swiglu(a, w, *, name):
    T, D = a.shape
    S, n = w.shape[0] // 2, w.shape[2]
    tm = _tile(T, 1408, BLOCK)

    def body(a_ref, wg_ref, wu_ref, g_ref, u_ref, o_ref, ot_ref):
        x = a_ref[...]
        g = jnp.dot(x, wg_ref[...], preferred_element_type=F32)
        u = jnp.dot(x, wu_ref[...], preferred_element_type=F32)
        g16, u16 = g.astype(BF16), u.astype(BF16)
        g_ref[...] = g16
        u_ref[...] = u16
        gr = g16.astype(F32)
        act = gr / (1.0 + jnp.exp(-gr)) * u16.astype(F32)
        o_ref[...] = act.astype(o_ref.dtype)
        ot_ref[...] = act.T.astype(ot_ref.dtype)

    tile = pl.BlockSpec((tm, n), lambda i, j: (i, j))
    shp = jax.ShapeDtypeStruct((T, S * n), BF16)
    return pl.pallas_call(
        body, out_shape=(shp, shp, shp, jax.ShapeDtypeStruct((S * n, T), BF16)), grid=(T // tm, S),
        in_specs=[pl.BlockSpec((tm, D), lambda i, j: (i, 0)),
                  pl.BlockSpec((None, D, n), lambda i, j: (j, 0, 0)),
                  pl.BlockSpec((None, D, n), lambda i, j: (j + S, 0, 0))],
        out_specs=(tile, tile, tile, pl.BlockSpec((n, tm), lambda i, j: (j, i))),
        compiler_params=_params(("parallel", "parallel")), name=name)(a, w, w)


def _d_act_swiglu(dff, w_down, gate, up, *, name):
    T, D = dff.shape
    F = w_down.shape[0]
    tm = _tile(T, 384)
    tf = _tile(F, 768, BLOCK)

    def body(d_ref, w_ref, g_ref, u_ref, o_ref):
        dy = d_ref[...]
        for c in range(0, F, tf):
            d = lax.dot_general(dy, w_ref[c:c + tf, :], NT, preferred_element_type=F32)
            g = g_ref[:, c:c + tf].astype(F32)
            u = u_ref[:, c:c + tf].astype(F32)
            sg = 1.0 / (1.0 + jnp.exp(-g))
            o_ref[:, c:c + tf] = (d * u * (sg * (1.0 + g * (1.0 - sg)))).astype(o_ref.dtype)
            o_ref[:, F + c:F + c + tf] = (d * (g * sg)).astype(o_ref.dtype)

    row = pl.BlockSpec((tm, F), lambda i: (i, 0))
    return pl.pallas_call(
        body, out_shape=jax.ShapeDtypeStruct((T, 2 * F), BF16), grid=(T // tm,),
        in_specs=[pl.BlockSpec((tm, D), lambda i: (i, 0)), pl.BlockSpec((F, D), lambda i: (0, 0)), row, row],
        out_specs=pl.BlockSpec((tm, 2 * F), lambda i: (i, 0)),
        compiler_params=_params(("parallel",)), name=name)(dff, w_down, gate, up)


def _fox_gates_fwd(f_t, b, *, name):
    H, T = f_t.shape
    nb = T // BLOCK

    def body(f_ref, b_ref, cum_ref, col_ref):
        f = f_ref[...] + b_ref[...]
        ls = jnp.minimum(f, 0.0) - jnp.log(1.0 + jnp.exp(-jnp.abs(f)))
        t = lax.broadcasted_iota(jnp.int32, (H, T), 1)
        ls = jnp.where(t >= PAD_ROWS, ls, 0.0)
        upper = (lax.broadcasted_iota(jnp.int32, (BLOCK, BLOCK), 0)
                 <= lax.broadcasted_iota(jnp.int32, (BLOCK, BLOCK), 1)).astype(F32)
        carry = jnp.zeros((H, 1), F32)
        for blk in range(nb):
            seg = ls[:, blk * BLOCK:(blk + 1) * BLOCK]
            pre = jnp.dot(seg, upper, precision=HIGHEST, preferred_element_type=F32) + carry
            cum_ref[:, blk * BLOCK:(blk + 1) * BLOCK] = pre
            col_ref[blk * BLOCK:(blk + 1) * BLOCK, :] = jnp.concatenate(
                [pre, jnp.zeros((BLOCK - H, BLOCK), F32)], axis=0).T
            carry = pre[:, BLOCK - 1:BLOCK]

    vm = pl.BlockSpec(memory_space=pltpu.VMEM)
    return pl.pallas_call(
        body, out_shape=(jax.ShapeDtypeStruct((H, T), F32), jax.ShapeDtypeStruct((T, BLOCK), F32)),
        in_specs=[vm, vm], out_specs=(vm, vm),
        compiler_params=_params(), name=name)(f_t, b)


def _fox_gates_bwd(dcq, dck, f_t, b, *, name):
    H, T = f_t.shape
    nb = T // BLOCK

    def body(dq_ref, d_ref, f_ref, b_ref, df_ref, db_ref):
        lower = (lax.broadcasted_iota(jnp.int32, (BLOCK, BLOCK), 0)
                 >= lax.broadcasted_iota(jnp.int32, (BLOCK, BLOCK), 1)).astype(F32)
        carry = jnp.zeros((H, 1), F32)
        for blk in range(nb - 1, -1, -1):
            seg = dq_ref[:, blk * BLOCK:(blk + 1) * BLOCK] - d_ref[:, blk * BLOCK:(blk + 1) * BLOCK]
            suf = jnp.dot(seg, lower, precision=HIGHEST, preferred_element_type=F32) + carry
            df_ref[:, blk * BLOCK:(blk + 1) * BLOCK] = suf
            carry = suf[:, 0:1]
        f = f_ref[...] + b_ref[...]
        t = lax.broadcasted_iota(jnp.int32, (H, T), 1)
        df = jnp.where(t >= PAD_ROWS, df_ref[...] / (1.0 + jnp.exp(f)), 0.0)
        df_ref[...] = df
        db_ref[...] = jnp.sum(df, axis=1, keepdims=True)

    vm = pl.BlockSpec(memory_space=pltpu.VMEM)
    return pl.pallas_call(
        body, out_shape=(jax.ShapeDtypeStruct((H, T), F32), jax.ShapeDtypeStruct((H, 1), F32)),
        in_specs=[vm, vm, vm, vm], out_specs=(vm, vm),
        compiler_params=_params(), name=name)(dcq, dck, f_t, b)


LANE_KC = HEAD_DIM
LANE_QC = HEAD_DIM + 3
LANE_END = HEAD_DIM + 6


def _split3(c):
    hi = c.astype(BF16).astype(F32)
    r = c - hi
    mid = r.astype(BF16).astype(F32)
    lo = (r - mid).astype(BF16).astype(F32)
    return hi, mid, lo


def _lanes(lane, data, start, terms, rest):
    out = rest
    for i, t in enumerate(terms):
        out = jnp.where(lane == start + i, t, out)
    return jnp.where(lane < HEAD_DIM, data, out)


def _fox_prep(proj, cum_col, *, name):
    T = proj.shape[0]
    tm = FOX_TILE
    nt = T // tm
    H = FOX_HEADS
    lanes = 2 * HEAD_DIM
    first = (proj.shape[1] - 3 * H * HEAD_DIM) // lanes

    def body(q_ref, k_ref, v_ref, c_ref, qa_ref, ka_ref, va_ref):
        p = pl.program_id(0)
        i = pl.program_id(1)
        lane = lax.broadcasted_iota(jnp.int32, (tm, lanes), 1)
        rows = i * tm + lax.broadcasted_iota(jnp.int32, (tm, 1), 0)
        q2 = q_ref[...].astype(F32)
        k2 = k_ref[...].astype(F32)
        v2 = v_ref[...].astype(F32)
        cum = c_ref[...]
        for e in range(2):
            c = jnp.sum(jnp.where(lane == 2 * p + e, cum, 0.0), axis=1, keepdims=True)
            ck = jnp.where(rows >= PAD_ROWS, c, -NEG)
            qe, ke, ve = (q2, k2, v2) if e == 0 else tuple(pltpu.roll(a, HEAD_DIM, 1) for a in (q2, k2, v2))
            one = jnp.where(lane < LANE_END, 1.0, 0.0)
            qa = _lanes(lane, qe * SCALE, LANE_QC, _split3(c), jnp.where(lane < LANE_QC, -1.0, 0.0))
            ka = _lanes(lane, ke, LANE_KC, _split3(ck), one)
            va = jnp.where(lane < HEAD_DIM, ve, jnp.where(lane < LANE_QC, 1.0, 0.0))
            qa_ref[e] = qa.astype(BF16)
            ka_ref[e] = ka.astype(BF16)
            va_ref[e] = va.astype(BF16)

    pairs = FOX_GROUP // 2

    def col(part):
        return pl.BlockSpec((tm, lanes),
                            lambda p, i: (i, first + 3 * pairs * (p // pairs) + part * pairs + p % pairs))

    out = pl.BlockSpec((2, tm, lanes), lambda p, i: (p, i, 0))
    shp = jax.ShapeDtypeStruct((H, T, lanes), BF16)
    return pl.pallas_call(
        body, out_shape=(shp, shp, shp), grid=(H // 2, nt),
        in_specs=[col(0), col(1), col(2), pl.BlockSpec((tm, lanes), lambda p, i: (i, 0))],
        out_specs=(out, out, out),
        compiler_params=_params(("parallel", "parallel")), name=name)(proj, proj, proj, cum_col)


def _fox_fwd(q_aug, k_aug, v_aug, mix, *, ex=None, name):
    H, T, lanes = q_aug.shape
    tq = FOX_TILE
    nq = T // tq
    G = FOX_GROUP

    def body(q_ref, k_ref, v_ref, mix_ref, o_ref, lse_ref, m_scr, acc_scr):
        i = pl.program_id(1)
        m_scr[...] = jnp.full(m_scr.shape, NEG, F32)
        acc_scr[...] = jnp.zeros(acc_scr.shape, F32)

        def step(kb, diag):
            off = pl.multiple_of(kb * tq, tq)
            s_t = [lax.dot_general(k_ref[g, pl.ds(off, tq), :], q_ref[g], NT, preferred_element_type=F32)
                   for g in range(G)]
            if diag:
                r = lax.broadcasted_iota(jnp.int32, (tq, tq), 0)
                c = lax.broadcasted_iota(jnp.int32, (tq, tq), 1)
                s_t = [jnp.where(c >= r, s, NEG) for s in s_t]
            m_prev = [m_scr[g] for g in range(G)]
            m_new = [jnp.maximum(m_prev[g], jnp.max(s_t[g], axis=0, keepdims=True)) for g in range(G)]
            p_t = [jnp.exp(s_t[g] - m_new[g]).astype(BF16) for g in range(G)]
            pv = [lax.dot_general(v_ref[g, pl.ds(off, tq), :], p_t[g], TN, preferred_element_type=F32)
                  for g in range(G)]
            for g in range(G):
                acc_scr[g] = jnp.exp(m_prev[g] - m_new[g]) * acc_scr[g] + pv[g]
                m_scr[g] = m_new[g]

        def loop_body(kb, carry):
            step(kb, False)
            return carry

        lax.fori_loop(0, i, loop_body, 0)
        step(i, True)
        lane = lax.broadcasted_iota(jnp.int32, (tq, lanes), 1)
        outs = []
        for g in range(G):
            acc = acc_scr[g]
            lse_ref[g] = m_scr[g] + jnp.log(acc[HEAD_DIM:HEAD_DIM + 1, :])
            acc_t = acc.T
            outs.append(acc_t / acc_t[:, HEAD_DIM:HEAD_DIM + 1])
        for pair in range(G // 2):
            o_ref[:, pair * lanes:(pair + 1) * lanes] = jnp.where(
                lane < HEAD_DIM, outs[2 * pair], pltpu.roll(outs[2 * pair + 1], HEAD_DIM, 1)).astype(o_ref.dtype)

    blk = pl.BlockSpec((G, tq, lanes), lambda h, i: (h, i, 0))
    full = pl.BlockSpec((G, T, lanes), lambda h, i: (h, 0, 0))
    grid = (H // G, nq)
    first = mix.shape[1] // (G * HEAD_DIM) - H // G
    body, x_in, x_in_specs, x_out, x_out_specs, x_scr = _carry(ex, grid, 4, 2, body)
    return pl.pallas_call(
        body,
        out_shape=(jax.ShapeDtypeStruct(mix.shape, mix.dtype), jax.ShapeDtypeStruct((H, nq, 1, tq), F32), *x_out),
        grid=grid,
        in_specs=[blk, full, full, pl.BlockSpec(memory_space=pl.ANY)] + x_in_specs,
        out_specs=(pl.BlockSpec((tq, G * HEAD_DIM), lambda h, i: (i, first + h)),
                   pl.BlockSpec((G, None, 1, tq), lambda h, i: (h, i, 0, 0)), *x_out_specs),
        input_output_aliases={3: 0},
        scratch_shapes=[pltpu.VMEM((G, 1, tq), F32), pltpu.VMEM((G, lanes, tq), F32)] + x_scr,
        compiler_params=_params(("arbitrary", "arbitrary")), name=name)(q_aug, k_aug, v_aug, mix, *x_in)


def _fox_prep_bwd(dmix, mix, *, name):
    T = dmix.shape[0]
    H = FOX_HEADS
    tm = FOX_TILE
    lanes = 2 * HEAD_DIM
    first = mix.shape[1] // lanes - H // 2

    def body(d_ref, o_ref, da_ref):
        lane = lax.broadcasted_iota(jnp.int32, (tm, lanes), 1)
        d2 = d_ref[...].astype(F32)
        prod = d2 * o_ref[...].astype(F32)
        for e in range(2):
            de = d2 if e == 0 else pltpu.roll(d2, HEAD_DIM, 1)
            delta = jnp.sum(jnp.where(lane // HEAD_DIM == e, prod, 0.0), axis=1, keepdims=True)
            da_ref[e] = _lanes(lane, de, LANE_KC, _split3(-delta), jnp.zeros((), F32)).astype(BF16)

    pair = pl.BlockSpec((tm, lanes), lambda p, i: (i, first + p))
    return pl.pallas_call(
        body, out_shape=jax.ShapeDtypeStruct((H, T, lanes), BF16), grid=(H // 2, T // tm),
        in_specs=[pair, pair],
        out_specs=pl.BlockSpec((2, tm, lanes), lambda p, i: (p, i, 0)),
        compiler_params=_params(("parallel", "parallel")), name=name)(dmix, mix)


def _fox_bwd(q_aug, k_aug, v_aug, do_aug, lse_row, dproj, *, ex=None, name):
    H, T, lanes = q_aug.shape
    tq = FOX_TILE
    nq = T // tq
    G = FOX_GROUP

    def side_by_side(tiles, scale=None):
        lane = lax.broadcasted_iota(jnp.int32, tiles[0].shape, 1)
        out = [jnp.where(lane < HEAD_DIM, tiles[2 * p], pltpu.roll(tiles[2 * p + 1], HEAD_DIM, 1))
               for p in range(G // 2)]
        out = jnp.concatenate(out, axis=1)
        return out if scale is None else out * scale

    def body(q_ref, k_ref, v_ref, do_ref, lse_ref, dproj_in, out_ref, dcq_ref, dck_ref, dk_acc, dv_acc, dq_ref):
        j = pl.program_id(1)

        @pl.when(j == 0)
        def _():
            dq_ref[...] = jnp.zeros(dq_ref.shape, F32)
            dcq_ref[...] = jnp.zeros(dcq_ref.shape, F32)

        dk_acc[...] = jnp.zeros(dk_acc.shape, F32)
        dv_acc[...] = jnp.zeros(dv_acc.shape, F32)

        def step(qb, diag):
            off = pl.multiple_of(qb * tq, tq)
            heads = range(G)
            qa = [q_ref[g, pl.ds(off, tq), :] for g in heads]
            da = [do_ref[g, pl.ds(off, tq), :] for g in heads]
            s_t = [lax.dot_general(k_ref[g], qa[g], NT, preferred_element_type=F32) for g in heads]
            dp_t = [lax.dot_general(v_ref[g], da[g], NT, preferred_element_type=F32) for g in heads]
            p_t = [jnp.exp(s_t[g] - lse_ref[g, qb]) for g in heads]
            if diag:
                r = lax.broadcasted_iota(jnp.int32, (tq, tq), 0)
                c = lax.broadcasted_iota(jnp.int32, (tq, tq), 1)
                p_t = [jnp.where(c >= r, p, 0.0) for p in p_t]
            dsb = [(p_t[g] * dp_t[g]).astype(BF16) for g in heads]
            dv = [jnp.dot(p_t[g].astype(BF16), da[g], preferred_element_type=F32) for g in heads]
            dk = [jnp.dot(dsb[g], qa[g], preferred_element_type=F32) for g in heads]
            dq = [jnp.dot(dsb[g].T, k_ref[g], preferred_element_type=F32) for g in heads]
            for g in heads:
                dv_acc[g] += dv[g]
                dk_acc[g] += dk[g]
                dq_ref[g, pl.ds(off, tq), :] += dq[g]
                dcq_ref[g, qb] += jnp.sum(dsb[g].astype(F32), axis=0, keepdims=True)

        step(j, True)

        def loop_body(qb, carry):
            step(qb, False)
            return carry

        lax.fori_loop(j + 1, nq, loop_body, 0)
        dk = [dk_acc[g] for g in range(G)]
        rows = pl.ds(pl.multiple_of(j * tq, tq), tq)
        out_ref[:, 0:wide] = side_by_side([dq_ref[g, rows, :] for g in range(G)], SCALE).astype(out_ref.dtype)
        out_ref[:, wide:2 * wide] = side_by_side(dk).astype(out_ref.dtype)
        out_ref[:, 2 * wide:3 * wide] = side_by_side([dv_acc[g] for g in range(G)]).astype(out_ref.dtype)
        for g in range(G):
            dck_ref[g] = -dk[g].T[LANE_KC:LANE_KC + 1, :]

    blk = pl.BlockSpec((G, tq, lanes), lambda h, j: (h, j, 0))
    full = pl.BlockSpec((G, T, lanes), lambda h, j: (h, 0, 0))
    wide = G * HEAD_DIM
    first = dproj.shape[1] // (3 * wide) - H // G
    grid = (H // G, nq)
    body, x_in, x_in_specs, x_out, x_out_specs, x_scr = _carry(ex, grid, 6, 3, body)
    rows = jax.ShapeDtypeStruct((H, nq, 1, tq), F32)
    all_rows = pl.BlockSpec((G, nq, 1, tq), lambda h, j: (h, 0, 0, 0))
    return pl.pallas_call(
        body,
        out_shape=(jax.ShapeDtypeStruct(dproj.shape, dproj.dtype), rows, rows, *x_out),
        grid=grid,
        in_specs=[full, blk, blk, full, all_rows, pl.BlockSpec(memory_space=pl.ANY)] + x_in_specs,
        out_specs=(pl.BlockSpec((tq, 3 * wide), lambda h, j: (j, first + h)), all_rows,
                   pl.BlockSpec((G, None, 1, tq), lambda h, j: (h, j, 0, 0)), *x_out_specs),
        input_output_aliases={5: 0},
        scratch_shapes=[pltpu.VMEM((G, tq, lanes), F32), pltpu.VMEM((G, tq, lanes), F32),
                        pltpu.VMEM((G, T, lanes), F32)] + x_scr,
        compiler_params=_params(("arbitrary", "arbitrary")), name=name,
    )(q_aug, k_aug, v_aug, do_aug, lse_row, dproj, *x_in)


def _t5_bucket_np(d):
    n = np.maximum(d, 0).astype(np.int32)
    max_exact = N_BUCKETS // 2
    nf = np.maximum(n, 1).astype(np.float32)
    large = max_exact + (np.log(nf / max_exact) / math.log(MAX_DISTANCE / max_exact)
                         * (N_BUCKETS - max_exact)).astype(np.int32)
    large = np.minimum(large, N_BUCKETS - 1)
    return np.where(n < max_exact, n, large)


def _bucket_onehots():
    k = np.arange(BLOCK)[:, None]
    q = np.arange(BLOCK)[None, :]
    eye = np.eye(N_BUCKETS, dtype=np.float32)
    cur = eye[_t5_bucket_np(q - k).reshape(-1)]
    prev = eye[_t5_bucket_np(BLOCK + q - k).reshape(-1)]
    return cur, prev


SWA_K_COL = SWA_Q_HEADS * HEAD_DIM // (2 * HEAD_DIM)
SWA_V_COL = SWA_K_COL + 1


def _swa_terms(raw, bc, bp, far, sink, n):
    k = lax.broadcasted_iota(jnp.int32, (BLOCK, BLOCK), 0)
    q = lax.broadcasted_iota(jnp.int32, (BLOCK, BLOCK), 1)
    never = 2 * BLOCK
    s_c = raw[0] + bc
    s_p = raw[1] + bp
    s_m = raw[2] + jnp.where(n == 1, bp, far)
    s_c = jnp.where((k <= q) & (k >= jnp.where(n >= 1, 0, PAD_ROWS)), s_c, NEG)
    s_p = jnp.where(k > q + jnp.where(n >= 2, 0, never), s_p, NEG)
    s_m = jnp.where(k >= jnp.where(n >= 1, PAD_ROWS, never), s_m, NEG)
    m = jnp.maximum(jnp.maximum(jnp.max(s_c, axis=0, keepdims=True), jnp.max(s_p, axis=0, keepdims=True)),
                    jnp.maximum(jnp.max(s_m, axis=0, keepdims=True), sink))
    e = [jnp.exp(s_c - m), jnp.exp(s_p - m), jnp.exp(s_m - m)]
    e_s = jnp.exp(sink - m)
    l = (jnp.sum(e[0], axis=0, keepdims=True) + jnp.sum(e[1], axis=0, keepdims=True)
         + jnp.sum(e[2], axis=0, keepdims=True) + e_s)
    return e, e_s, l


SWA_STEP = 3


def _swa_specs():
    R = SWA_STEP

    def window(col):
        return ([pl.BlockSpec((BLOCK, BLOCK), lambda s, w=w: (jnp.maximum(R * s - 1 + w, 0), col)) for w in range(R + 1)]
                + [pl.BlockSpec((BLOCK, BLOCK), lambda s: (0, col))])

    qblk = pl.BlockSpec((R * BLOCK, SWA_Q_HEADS * HEAD_DIM), lambda s: (s, 0))
    bias = pl.BlockSpec((SWA_Q_HEADS, BLOCK, BLOCK), lambda s: (0, 0, 0))
    smem = pl.BlockSpec(memory_space=pltpu.SMEM)
    return qblk, window(SWA_K_COL), window(SWA_V_COL), bias, smem


def _swa_own_kv(tile_ref, kv):
    lane = lax.broadcasted_iota(jnp.int32, (BLOCK, 2 * HEAD_DIM), 1)
    t = tile_ref[...].astype(F32)
    return jnp.where(lane // HEAD_DIM == kv, t, pltpu.roll(t, HEAD_DIM, 1)).astype(BF16)


def _swa_fwd(proj, bc, bp, far, sinks, *, name):
    T = proj.shape[0]
    nb = T // BLOCK
    G = SWA_GROUP
    Hq = SWA_Q_HEADS
    lanes = 2 * HEAD_DIM

    R = SWA_STEP
    assert nb % R == 0

    def body(*refs):
        q_ref, k_refs, v_refs = refs[0], refs[1:R + 3], refs[R + 3:2 * R + 5]
        bc_ref, bp_ref, far_ref, sink_ref, o_ref = refs[2 * R + 5:]
        s = pl.program_id(0)
        lane = lax.broadcasted_iota(jnp.int32, (BLOCK, lanes), 1)
        kvs = range(SWA_KV_HEADS)
        kk = [[_swa_own_kv(ref, kv) for ref in k_refs] for kv in kvs]
        vv = [[_swa_own_kv(ref, kv) for ref in v_refs] for kv in kvs]
        chains = [(r, h) for r in range(R) for h in range(Hq)]
        tiles = lambda r: (r + 1, r, R + 1)
        q2 = {(r, pair): q_ref[r * BLOCK:(r + 1) * BLOCK, pair * lanes:(pair + 1) * lanes].astype(F32) * SCALE
              for r in range(R) for pair in range(Hq // 2)}
        qm = {c: jnp.where(lane // HEAD_DIM == c[1] % 2, q2[c[0], c[1] // 2], 0.0).astype(BF16) for c in chains}
        raw = {c: [lax.dot_general(kk[c[1] // G][w], qm[c], NT, preferred_element_type=F32) for w in tiles(c[0])]
               for c in chains}
        terms = {c: _swa_terms(raw[c], bc_ref[c[1]], bp_ref[c[1]], far_ref[c[1]], sink_ref[c[1]], R * s + c[0])
                 for c in chains}
        o_t = {c: sum(lax.dot_general(vv[c[1] // G][w], terms[c][0][b].astype(BF16), TN, preferred_element_type=F32)
                      for b, w in enumerate(tiles(c[0]))) for c in chains}
        outs = {c: (o_t[c] / terms[c][2]).T for c in chains}
        for r in range(R):
            for pair in range(Hq // 2):
                o_ref[r * BLOCK:(r + 1) * BLOCK, pair * lanes:(pair + 1) * lanes] = jnp.where(
                    lane < HEAD_DIM, outs[r, 2 * pair], outs[r, 2 * pair + 1]).astype(o_ref.dtype)

    qblk, keys, vals, bias, smem = _swa_specs()
    return pl.pallas_call(
        body, out_shape=jax.ShapeDtypeStruct((T, D_MODEL), BF16), grid=(nb // R,),
        in_specs=[qblk] + keys + vals + [bias, bias, smem, smem],
        out_specs=qblk,
        compiler_params=_params(("parallel",)), name=name,
    )(proj, *([proj] * (2 * R + 4)), bc, bp, far, sinks)


def _swa_bwd(proj, dmix, bc, bp, far, sinks, *, ex=None, name):
    T, width = proj.shape
    nb = T // BLOCK
    G = SWA_GROUP
    Hq = SWA_Q_HEADS
    lanes = 2 * HEAD_DIM
    qw = Hq * HEAD_DIM
    own_w = qw + 2 * lanes

    R = SWA_STEP
    assert nb % R == 0
    n_in = 2 * R + 10

    def body(*refs):
        q_ref, k_refs, v_refs = refs[0], refs[1:R + 3], refs[R + 3:2 * R + 5]
        do_ref, bc_ref, bp_ref, far_ref, sink_ref = refs[2 * R + 5:n_in]
        dp_ref, dbc_ref, dbp_ref, dbf_ref, dsk_ref, dk_acc, dv_acc = refs[n_in:]
        s = pl.program_id(0)

        @pl.when(s == 0)
        def _():
            for ref in (dk_acc, dv_acc, dbc_ref, dbp_ref, dbf_ref, dsk_ref):
                ref[...] = jnp.zeros(ref.shape, F32)

        lane = lax.broadcasted_iota(jnp.int32, (BLOCK, lanes), 1)
        kvs = range(SWA_KV_HEADS)
        kk = [[_swa_own_kv(ref, kv) for ref in k_refs] for kv in kvs]
        vv = [[_swa_own_kv(ref, kv) for ref in v_refs] for kv in kvs]
        chains = [(r, h) for r in range(R) for h in range(Hq)]
        blocks = range(3)
        tiles = lambda r: (r + 1, r, R + 1)
        sub = lambda ref, r, pair: ref[r * BLOCK:(r + 1) * BLOCK, pair * lanes:(pair + 1) * lanes]
        q2 = {(r, pair): sub(q_ref, r, pair).astype(F32) * SCALE for r in range(R) for pair in range(Hq // 2)}
        d2 = {(r, pair): sub(do_ref, r, pair) for r in range(R) for pair in range(Hq // 2)}
        own = [lane // HEAD_DIM == half for half in range(2)]
        qm = {c: jnp.where(own[c[1] % 2], q2[c[0], c[1] // 2], 0.0).astype(BF16) for c in chains}
        dom = {c: jnp.where(own[c[1] % 2], d2[c[0], c[1] // 2], jnp.zeros_like(d2[0, 0])) for c in chains}
        raw = {c: [lax.dot_general(kk[c[1] // G][w], qm[c], NT, preferred_element_type=F32) for w in tiles(c[0])]
               for c in chains}
        dp = {c: [lax.dot_general(vv[c[1] // G][w], dom[c], NT, preferred_element_type=F32) for w in tiles(c[0])]
              for c in chains}
        p, ds16 = {}, {}
        for c in chains:
            r, h = c
            n = R * s + r
            e, e_s, l = _swa_terms(raw[c], bc_ref[h], bp_ref[h], far_ref[h], sink_ref[h], n)
            inv = 1.0 / l
            ph = [e[b] * inv for b in blocks]
            delta = sum(jnp.sum(ph[b] * dp[c][b], axis=0, keepdims=True) for b in blocks)
            ds = [ph[b] * (dp[c][b] - delta) for b in blocks]
            dsk_ref[h] += -(e_s * inv) * delta
            dbc_ref[h] += ds[0]
            dbp_ref[h] += ds[1] + jnp.where(n == 1, ds[2], 0.0)
            dbf_ref[h] += jnp.where(n >= 2, ds[2], 0.0)
            p[c] = [x.astype(BF16) for x in ph]
            ds16[c] = [x.astype(BF16) for x in ds]
        dq_t = {c: sum(lax.dot_general(kk[c[1] // G][w], ds16[c][b], TN, preferred_element_type=F32)
                       for b, w in enumerate(tiles(c[0]))) for c in chains}
        group = [range(kv * G, (kv + 1) * G) for kv in kvs]
        dk = {(r, kv): [sum(jnp.dot(ds16[r, h][b], qm[r, h], preferred_element_type=F32) for h in group[kv])
                        for b in blocks] for r in range(R) for kv in kvs}
        dv = {(r, kv): [sum(jnp.dot(p[r, h][b], dom[r, h], preferred_element_type=F32) for h in group[kv])
                        for b in blocks] for r in range(R) for kv in kvs}
        for r in range(R):
            n = R * s + r
            rows = pl.ds(pl.multiple_of(n * BLOCK, BLOCK), BLOCK)
            prev_rows = pl.ds(pl.multiple_of(jnp.maximum(n - 1, 0) * BLOCK, BLOCK), BLOCK)
            for pair in range(Hq // 2):
                dp_ref[rows, pair * lanes:(pair + 1) * lanes] = (jnp.where(
                    lane < HEAD_DIM, dq_t[r, 2 * pair].T, dq_t[r, 2 * pair + 1].T) * SCALE).astype(dp_ref.dtype)
            for acc, ref in ((dk, dk_acc), (dv, dv_acc)):
                tot = [[a + pltpu.roll(a, HEAD_DIM, 1) for a in acc[r, kv]] for kv in kvs]
                both = [jnp.where(lane < HEAD_DIM, tot[0][b], tot[1][b]) for b in blocks]
                ref[rows, :] += both[0]
                ref[prev_rows, :] += both[1]
                ref[0:BLOCK, :] += both[2]

        @pl.when(s == nb // R - 1)
        def _():
            dp_ref[:, qw:qw + lanes] = dk_acc[...].astype(dp_ref.dtype)
            dp_ref[:, qw + lanes:own_w] = dv_acc[...].astype(dp_ref.dtype)

    qblk, keys, vals, bias, smem = _swa_specs()
    dsk = pl.BlockSpec((Hq, 1, BLOCK), lambda s: (0, 0, 0))
    grid = (nb // R,)
    body, x_in, x_in_specs, x_out, x_out_specs, x_scr = _carry(ex, grid, n_in, 5, body)
    tile = jax.ShapeDtypeStruct((Hq, BLOCK, BLOCK), F32)
    return pl.pallas_call(
        body,
        out_shape=(jax.ShapeDtypeStruct((T, width), BF16), tile, tile, tile,
                   jax.ShapeDtypeStruct((Hq, 1, BLOCK), F32), *x_out),
        grid=grid,
        in_specs=[qblk] + keys + vals + [qblk, bias, bias, smem, smem] + x_in_specs,
        out_specs=(pl.BlockSpec((T, own_w), lambda s: (0, 0)), bias, bias, bias, dsk, *x_out_specs),
        scratch_shapes=[pltpu.VMEM((T, lanes), F32), pltpu.VMEM((T, lanes), F32)] + x_scr,
        compiler_params=_params(("arbitrary",)), name=name,
    )(proj, *([proj] * (2 * R + 4)), dmix, bc, bp, far, sinks, *x_in)


def _small_grads(dbc, dbp, dbf, dsk, oh_cur, oh_prev, *, name):
    Hq = dbc.shape[0]

    def body(dbc_ref, dbp_ref, dbf_ref, dsk_ref, oc_ref, op_ref, tab_ref, sink_ref):
        tab = (jnp.dot(dbc_ref[...], oc_ref[...], precision=HIGHEST, preferred_element_type=F32)
               + jnp.dot(dbp_ref[...], op_ref[...], precision=HIGHEST, preferred_element_type=F32))
        far = jnp.sum(dbf_ref[...], axis=1, keepdims=True)
        last = lax.broadcasted_iota(jnp.int32, (Hq, N_BUCKETS), 1) == N_BUCKETS - 1
        tab_ref[...] = tab + jnp.where(last, far, 0.0)
        sink_ref[...] = jnp.sum(dsk_ref[...], axis=1, keepdims=True)

    vm = pl.BlockSpec(memory_space=pltpu.VMEM)
    return pl.pallas_call(
        body, out_shape=(jax.ShapeDtypeStruct((Hq, N_BUCKETS), F32), jax.ShapeDtypeStruct((Hq, 1), F32)),
        in_specs=[vm] * 6, out_specs=(vm, vm), compiler_params=_params(), name=name,
    )(dbc.reshape(Hq, -1), dbp.reshape(Hq, -1), dbf.reshape(Hq, -1), dsk.reshape(Hq, -1), oh_cur, oh_prev)


def _coords():
    return lax.axis_index("x"), lax.axis_index("y"), lax.axis_index("c")


class _Exchange:
    def __init__(self, inputs, out_shapes, scratch, start, finish):
        self.inputs, self.out_shapes, self.scratch, self.start, self.finish = inputs, out_shapes, scratch, start, finish


def _carry(ex, grid, n_in, n_out, body):
    if ex is None:
        return body, [], [], [], [], []
    ni, no = len(ex.inputs), len(ex.out_shapes)

    def at_step(which):
        cond = None
        for axis, n in enumerate(grid):
            c = pl.program_id(axis) == (0 if which == "first" else n - 1)
            cond = c if cond is None else cond & c
        return cond

    def wrapped(*refs):
        refs = list(refs)
        n_own_scr = len(refs) - (n_in + ni + n_out + no) - len(ex.scratch)
        own_in, side_in = refs[:n_in], refs[n_in:n_in + ni]
        own_out = refs[n_in + ni:n_in + ni + n_out]
        side_out = refs[n_in + ni + n_out:n_in + ni + n_out + no]
        rest = refs[n_in + ni + n_out + no:]
        own_scr, sems = rest[:n_own_scr], rest[n_own_scr:]

        @pl.when(at_step("first"))
        def _():
            ex.start(side_in, side_out, sems)

        body(*own_in, *own_out, *own_scr)

        @pl.when(at_step("last"))
        def _():
            ex.finish(side_in, side_out, sems)

    hbm = pl.BlockSpec(memory_space=pl.ANY)
    return wrapped, list(ex.inputs), [hbm] * ni, list(ex.out_shapes), [hbm] * no, list(ex.scratch)


def _run_exchange(ex, *, name):
    ni, no = len(ex.inputs), len(ex.out_shapes)

    def body(*refs):
        ins, outs, sems = refs[:ni], refs[ni:ni + no], refs[ni + no:]
        ex.start(ins, outs, sems)
        ex.finish(ins, outs, sems)

    hbm = pl.BlockSpec(memory_space=pl.ANY)
    return pl.pallas_call(
        body, out_shape=tuple(ex.out_shapes), in_specs=[hbm] * ni, out_specs=tuple([hbm] * no),
        scratch_shapes=ex.scratch, compiler_params=_params(), name=name)(*ex.inputs)


def _gather_exchange(shards):
    nt = len(shards)

    def copies(ins, outs, sems):
        send_sems, recv_sems, local_sems = sems
        x, y, c = _coords()
        me, sibling = (x, y, c), (x, y, 1 - c)
        chips = [(1 - x, y), (x, 1 - y), (1 - x, 1 - y)]

        def slot(t, dev):
            return outs[t].at[4 * dev[0] + 2 * dev[1] + dev[2]]

        def copy(t, k, block, to, src=None):
            dst = slot(t, block)
            return pltpu.make_async_remote_copy(
                src_ref=dst if src is None else src, dst_ref=dst,
                send_sem=send_sems.at[t, k], recv_sem=recv_sems.at[t, k], device_id=to, device_id_type=MESH)

        mine = [pltpu.make_async_copy(ins[t], slot(t, me), local_sems.at[t]) for t in range(nt)]
        first = []
        for t in range(nt):
            first.append(copy(t, 0, me, sibling, src=ins[t]))
            first += [copy(t, 1 + j, me, (*chip, c), src=ins[t]) for j, chip in enumerate(chips)]
        return copy, mine, first, me, sibling, chips, c

    def start(ins, outs, sems):
        _, mine, first, *_ = copies(ins, outs, sems)
        for cp in mine + first:
            cp.start()

    def finish(ins, outs, sems):
        copy, mine, first, me, sibling, chips, c = copies(ins, outs, sems)
        passed = []
        for j, chip in enumerate(chips):
            for t in range(nt):
                copy(t, 1 + j, (*chip, c), me).wait_recv()
                cp = copy(t, 4 + j, (*chip, c), sibling)
                cp.start()
                passed.append(cp)
        for t in range(nt):
            copy(t, 0, sibling, me).wait_recv()
            for j, chip in enumerate(chips):
                copy(t, 4 + j, (*chip, 1 - c), me).wait_recv()
        for cp in first + passed:
            cp.wait_send()
        for cp in mine:
            cp.wait()

    return _Exchange(
        list(shards), [jax.ShapeDtypeStruct((N_DEV,) + s.shape, s.dtype) for s in shards],
        [pltpu.SemaphoreType.DMA((nt, 7)), pltpu.SemaphoreType.DMA((nt, 7)), pltpu.SemaphoreType.DMA((nt,))],
        start, finish)


def _swap_exchange(arrays, n_slices, copies):
    nt = len(arrays)

    def start(ins, outs, sems):
        for cp in copies(ins, outs, sems):
            cp.start()

    def finish(ins, outs, sems):
        sends = copies(ins, outs, sems)
        for cp in sends:
            cp.wait_recv()
        for cp in sends:
            cp.wait_send()

    return _Exchange(
        list(arrays), [jax.ShapeDtypeStruct((n_slices,) + a.shape[1:], a.dtype) for a in arrays],
        [pltpu.SemaphoreType.DMA((nt, n_slices)), pltpu.SemaphoreType.DMA((nt, n_slices))], start, finish)


def _cores_exchange(gs):
    def copies(ins, outs, sems):
        send_sems, recv_sems = sems
        x, y, c = _coords()
        return [pltpu.make_async_remote_copy(
            src_ref=ins[t].at[2 * j + (1 - c)], dst_ref=outs[t].at[j],
            send_sem=send_sems.at[t, j], recv_sem=recv_sems.at[t, j], device_id=(x, y, 1 - c), device_id_type=MESH)
            for t in range(len(gs)) for j in range(4)]

    return _swap_exchange(gs, 4, copies)


def _chips_exchange(ps):
    def copies(ins, outs, sems):
        send_sems, recv_sems = sems
        x, y, c = _coords()
        peers = [(1 - x, y), (x, 1 - y), (1 - x, 1 - y)]
        return [pltpu.make_async_remote_copy(
            src_ref=ins[t].at[2 * px + py], dst_ref=outs[t].at[k],
            send_sem=send_sems.at[t, k], recv_sem=recv_sems.at[t, k], device_id=(px, py, c), device_id_type=MESH)
            for t in range(len(ps)) for k, (px, py) in enumerate(peers)]

    return _swap_exchange(ps, 3, copies)


def _add_cores(g, r, core, *, name):
    _, A, B = g.shape
    ta = _tile(A, 512, 16)

    def body(core_ref, a_ref, b_ref, o_ref, o16_ref):
        s = a_ref[...] + b_ref[...]
        o_ref[...] = s
        o16_ref[...] = s.astype(BF16)

    blk = (None, ta, B)
    out = pl.BlockSpec(blk, lambda j, i, core_ref: (j, i, 0))
    return pl.pallas_call(
        body, out_shape=(jax.ShapeDtypeStruct((4, A, B), F32), jax.ShapeDtypeStruct((4, A, B), BF16)),
        grid_spec=pltpu.PrefetchScalarGridSpec(
            num_scalar_prefetch=1, grid=(4, A // ta),
            in_specs=[pl.BlockSpec(blk, lambda j, i, core_ref: (2 * j + core_ref[0], i, 0)),
                      pl.BlockSpec(blk, lambda j, i, core_ref: (j, i, 0))],
            out_specs=(out, out)),
        compiler_params=_params(("parallel", "parallel")), name=name)(core, g, r)


def _adamw_math(w, g, m, v):
    m = ADAM_B1 * m + (1.0 - ADAM_B1) * g
    v = ADAM_B2 * v + (1.0 - ADAM_B2) * (g * g)
    m_hat = m / (1.0 - ADAM_B1 ** ADAM_STEP)
    v_hat = v / (1.0 - ADAM_B2 ** ADAM_STEP)
    delta = -ADAM_LR * (m_hat / (jnp.sqrt(v_hat) + ADAM_EPS) + ADAM_WD * w)
    return delta, m, v


def _sum_adamw(p, r, chip, w, m, v, *, segs, ta, name):
    Aw, Bw = w.shape
    Bg = p.shape[2]
    assert Aw % ta == 0

    def body(chip_ref, p_ref, r0, r1, r2, w_ref, m_ref, v_ref, g_out, d_out, m_out, v_out):
        for gc, wc, n in segs:
            g = ((p_ref[:, gc:gc + n] + r0[:, gc:gc + n].astype(F32)) + r1[:, gc:gc + n].astype(F32)
                 ) + r2[:, gc:gc + n].astype(F32)
            delta, m_new, v_new = _adamw_math(w_ref[:, wc:wc + n], g, m_ref[:, wc:wc + n], v_ref[:, wc:wc + n])
            g_out[:, wc:wc + n] = g
            d_out[:, wc:wc + n] = delta
            m_out[:, wc:wc + n] = m_new
            v_out[:, wc:wc + n] = v_new

    gblk = (None, ta, Bg)
    row = pl.BlockSpec((ta, Bw), lambda i, chip_ref: (i, 0))
    rspecs = [pl.BlockSpec(gblk, (lambda i, chip_ref, k=k: (k, i, 0))) for k in range(3)]
    shp = jax.ShapeDtypeStruct((Aw, Bw), F32)
    return pl.pallas_call(
        body, out_shape=(shp, shp, shp, shp),
        grid_spec=pltpu.PrefetchScalarGridSpec(
            num_scalar_prefetch=1, grid=(Aw // ta,),
            in_specs=[pl.BlockSpec(gblk, lambda i, chip_ref: (chip_ref[0], i, 0))] + rspecs + [row, row, row],
            out_specs=(row, row, row, row)),
        compiler_params=_params(("parallel",)), name=name)(chip, p, r, r, r, w, m, v)


def _adamw(w, g, m, v, *, name):
    def body(w_ref, g_ref, m_ref, v_ref, d_out, m_out, v_out):
        delta, m_new, v_new = _adamw_math(w_ref[...], g_ref[...], m_ref[...], v_ref[...])
        d_out[...] = delta
        m_out[...] = m_new
        v_out[...] = v_new

    vm = pl.BlockSpec(memory_space=pltpu.VMEM)
    shp = jax.ShapeDtypeStruct(w.shape, F32)
    return pl.pallas_call(body, out_shape=(shp, shp, shp), in_specs=[vm] * 4, out_specs=(vm, vm, vm),
                          compiler_params=_params(), name=name)(w, g, m, v)


def _small_allreduce_adamw(s, w, m, v, *, name):
    R, W = s.shape

    def body(s_ref, w_ref, m_ref, v_ref, g_out, d_out, m_out, v_out, gath, send_sems, recv_sems):
        x, y, c = _coords()
        mine = 4 * x + 2 * y + c
        gath[mine] = s_ref[...]
        peers = [((1 - x) if k & 4 else x, (1 - y) if k & 2 else y, (1 - c) if k & 1 else c) for k in range(1, N_DEV)]
        sends = []
        for k in range(1, N_DEV):
            peer = peers[k - 1]
            sends.append(pltpu.make_async_remote_copy(
                src_ref=s_ref, dst_ref=gath.at[mine], send_sem=send_sems.at[k - 1], recv_sem=recv_sems.at[k - 1],
                device_id=peer, device_id_type=MESH))
        for cp in sends:
            cp.start()
        for k in range(1, N_DEV):
            peer = peers[k - 1]
            pltpu.make_async_remote_copy(
                src_ref=s_ref, dst_ref=gath.at[4 * peer[0] + 2 * peer[1] + peer[2]],
                send_sem=send_sems.at[k - 1], recv_sem=recv_sems.at[k - 1],
                device_id=peer, device_id_type=MESH).wait_recv()
        for cp in sends:
            cp.wait_send()
        g = gath[0]
        for d in range(1, N_DEV):
            g = g + gath[d]
        delta, m_new, v_new = _adamw_math(w_ref[...], g, m_ref[...], v_ref[...])
        g_out[...] = g
        d_out[...] = delta
        m_out[...] = m_new
        v_out[...] = v_new

    vm = pl.BlockSpec(memory_space=pltpu.VMEM)
    shp = jax.ShapeDtypeStruct((R, W), F32)
    return pl.pallas_call(
        body, out_shape=(shp, shp, shp, shp), in_specs=[vm] * 4, out_specs=(vm, vm, vm, vm),
        scratch_shapes=[pltpu.VMEM((N_DEV, R, W), F32), pltpu.SemaphoreType.DMA((N_DEV - 1,)),
                        pltpu.SemaphoreType.DMA((N_DEV - 1,))],
        compiler_params=_params(), name=name)(s, w, m, v)


def _pack_small(rel_bias, g1, g2, g3, g4, b_forget, sinks, extra=None, meta=None):
    misc = jnp.concatenate([rel_bias.reshape(-1), b_forget.reshape(-1), sinks.reshape(-1)])
    misc = jnp.concatenate([misc, jnp.zeros((D_MODEL - misc.shape[0],), F32)])[None]
    last = jnp.zeros((1, D_MODEL), F32) if extra is None else extra
    meta = jnp.zeros((N_META, D_MODEL), F32) if meta is None else meta
    return jnp.concatenate([g1, g2, g3, g4, misc, last, jnp.zeros((2, D_MODEL), F32), meta], axis=0)


def _unpack_small(p):
    nrb = N_BUCKETS * SWA_Q_HEADS
    misc = p[4]
    return dict(rel_bias=misc[:nrb].reshape(N_BUCKETS, SWA_Q_HEADS), ln_pre_mix=p[0:1], ln_post_mix=p[1:2],
                ln_pre_ffn=p[2:3], ln_post_ffn=p[3:4], b_forget=misc[nrb:nrb + 8].reshape(1, 8),
                sinks=misc[nrb + 8:nrb + 16].reshape(1, 8))


def _proj_runs():
    gw = FOX_GROUP * HEAD_DIM
    swa = SWA_Q_W + 2 * SWA_KV_HEADS * HEAD_DIM
    runs = [(0, swa)]
    for grp in range(FOX_HEADS // FOX_GROUP):
        runs += [(swa + part * FOX_W + grp * gw, swa + part * FOX_W + (grp + 1) * gw) for part in range(3)]
    return runs


def _device_shards(qkv, gate, shard, padded):
    pos, segments = 0, []
    for start, stop in _proj_runs():
        segments.append((start, stop, qkv, pos))
        pos += stop - start
    segments.append((pos, pos + gate.shape[1], gate, 0))
    total = pos + gate.shape[1]
    assert total % shard == 0
    zeros = jnp.zeros((qkv.shape[0], padded - shard), qkv.dtype)
    out = []
    for d in range(total // shard):
        lo, hi = d * shard, (d + 1) * shard
        pieces = [arr[:, src + max(lo, s) - s:src + min(hi, e) - s]
                  for s, e, arr, src in sorted(segments, key=lambda seg: seg[0]) if max(lo, s) < min(hi, e)]
        out.append(jnp.concatenate(pieces + [zeros], axis=1))
    return jnp.stack(out)


def kernel(x, meta_tokens, rel_bias, ln_pre_mix, ln_post_mix, ln_pre_ffn, ln_post_ffn, w_in, b_forget, sinks, w_out, w_gate_up, w_down, loss_target, m_meta_tokens, m_rel_bias, m_ln_pre_mix, m_ln_post_mix, m_ln_pre_ffn, m_ln_post_ffn, m_w_in, m_b_forget, m_sinks, m_w_out, m_w_gate_up, m_w_down, v_meta_tokens, v_rel_bias, v_ln_pre_mix, v_ln_post_mix, v_ln_pre_ffn, v_ln_post_ffn, v_w_in, v_b_forget, v_sinks, v_w_out, v_w_gate_up, v_w_down):
    seq = x.shape[1]
    T = BLOCK + seq
    assert T % FOX_TILE == 0
    nq = T // FOX_TILE
    tm = _tile(T, 1056)
    cin = w_in.shape[2]
    hid = w_down.shape[1]
    assert w_gate_up.shape[2] == 2 * hid and cin <= W_IN_PAD and hid <= HID_PAD

    x_i, y_i, c_i = _coords()
    core = jnp.reshape(c_i, (1,)).astype(jnp.int32)
    chip = jnp.reshape(2 * x_i + y_i, (1,)).astype(jnp.int32)
    w_in_s = jnp.pad(w_in[0].astype(BF16), ((0, 0), (0, W_IN_PAD - cin)))
    w_gu_s = jnp.pad(w_gate_up[0].astype(BF16).reshape(D_MODEL, 2, hid), ((0, 0), (0, 0), (0, HID_PAD - hid)))
    w_gu_s = w_gu_s.reshape(D_MODEL, 2 * HID_PAD)
    w_down_s = jnp.pad(w_down[0].astype(BF16), ((0, HID_PAD - hid), (0, 0)))
    g_in, g_meta = _run_exchange(_gather_exchange([w_in_s, meta_tokens]), name="ag_w_in")
    gather_rest = _gather_exchange([w_out[0].astype(BF16), w_gu_s, w_down_s])
    w_in_full = g_in[:, :, :cin].transpose(1, 0, 2).reshape(D_MODEL, N_DEV * cin)
    w_qkv = jnp.concatenate([w_in_full[:, a:b] for a, b in _proj_runs()], axis=1)
    w_f = jnp.pad(w_in_full[:, D_QKV:], ((0, 0), (0, BLOCK - FOX_HEADS)))
    meta_full = g_meta.transpose(1, 0, 2).reshape(N_META, D_MODEL)

    h0 = jnp.concatenate([jnp.zeros((PAD_ROWS, D_MODEL), F32), meta_full, x[0]], axis=0)
    target = jnp.concatenate([jnp.zeros((BLOCK, D_MODEL), F32), loss_target[0]], axis=0)
    hn1, hn1_t = _rms_fwd(h0, ln_pre_mix, name="rms_pre_mix")
    proj = _matmul(hn1, w_qkv, out_dtype=BF16, tm=tm, tn=D_QKV, name="mm_in_proj")
    proj_f = _matmul(hn1, w_f, out_dtype=F32, tm=tm, tn=BLOCK, name="mm_in_proj_f")

    f_t = proj_f[:, :FOX_HEADS].T
    bf_col = b_forget.reshape(FOX_HEADS, 1)

    oh_cur, oh_prev = _bucket_onehots()
    bias_c = jnp.einsum("pb,bh->hp", jnp.asarray(oh_cur), rel_bias, precision=HIGHEST).reshape(8, BLOCK, BLOCK)
    bias_p = jnp.einsum("pb,bh->hp", jnp.asarray(oh_prev), rel_bias, precision=HIGHEST).reshape(8, BLOCK, BLOCK)
    far = rel_bias[N_BUCKETS - 1]
    sink_v = sinks[0]
    mix_a = _swa_fwd(proj, bias_c, bias_p, far, sink_v, name="swa_fwd")

    _, cum_col = _fox_gates_fwd(f_t, bf_col, name="fox_gates_fwd")
    q_b, k_b, v_b = _fox_prep(proj, cum_col, name="fox_prep")
    mix, lse_row, g_out, g_gu, g_down = _fox_fwd(q_b, k_b, v_b, mix_a, ex=gather_rest, name="fox_fwd")
    w_out_full = g_out.reshape(D_MODEL, D_MODEL)
    w_down_full = g_down.reshape(N_DEV * HID_PAD, D_MODEL)

    a1 = _matmul(mix, w_out_full, out_dtype=F32, tm=tm, tn=512, name="mm_out_proj")
    h1, hn2, hn2_t = _post_res_norm(a1, ln_post_mix, h0, ln_pre_ffn, name="post_mix_pre_ffn")
    gate, up, act, act_t = _gate_up_swiglu(hn2, g_gu, name="mm_gate_up")
    ff = _matmul(act, w_down_full, out_dtype=F32, tm=tm, tn=512, name="mm_down")
    dh2, dff, dg_post_ffn, loss_acc = _loss_head(ff, ln_post_ffn, h1, target, name="loss_head")

    dgu = _d_act_swiglu(dff, w_down_full, gate, up, name="mm_d_act")
    d_w_down = _matmul(act_t, dff, out_dtype=F32, tm=768, tn=512, name="mm_dw_down")
    dhn2 = _matmul(dgu, g_gu, nt=True, b_shards=True, out_dtype=F32, tm=tm, tn=512, name="mm_d_hn2")
    d_w_gu = _matmul(hn2_t, dgu, out_shards=True, out_dtype=F32, tm=512, tn=2 * HID_PAD, name="mm_dw_gate_up")
    dh1, dg_pre_ffn, da1, dg_post_mix = _rms_bwd(h1, ln_pre_ffn, dhn2, dh2, out_dtype=F32,
                                                 then=(a1, ln_post_mix), name="rms_bwd_pre_ffn_post_mix")
    dmix = _matmul(da1, w_out_full, nt=True, out_dtype=BF16, tm=tm, tn=512, name="mm_d_mix")
    d_w_out = _matmul(mix, da1, ta=True, out_dtype=F32, tm=512, tn=512, name="mm_dw_out")

    ffn_grads = [d_w_out.reshape(N_DEV, -1, D_MODEL), d_w_gu, d_w_down.reshape(N_DEV, HID_PAD, D_MODEL)]
    dproj_a, dbc, dbp, dbf, dsk, *ffn_sibling = _swa_bwd(
        proj, dmix, bias_c, bias_p, far, sink_v, ex=_cores_exchange(ffn_grads), name="swa_bwd")
    d_tab, d_sink = _small_grads(dbc, dbp, dbf, dsk, jnp.asarray(oh_cur), jnp.asarray(oh_prev), name="small_grads")
    ffn_sums = [_add_cores(g, r, core, name="rs_add_" + t)
                for g, r, t in zip(ffn_grads, ffn_sibling, ["w_out", "w_gate_up", "w_down"])]

    do_b = _fox_prep_bwd(dmix, mix, name="fox_prep_bwd")
    dproj, dcq, dck, *ffn_chips = _fox_bwd(
        q_b, k_b, v_b, do_b, lse_row, dproj_a, ex=_chips_exchange([s[1] for s in ffn_sums]), name="fox_bwd")
    df_t, d_bf = _fox_gates_bwd(dcq.reshape(FOX_HEADS, T), dck.reshape(FOX_HEADS, T), f_t, bf_col,
                                name="fox_gates_bwd")
    df = jnp.pad(df_t.T.astype(BF16), ((0, 0), (0, BLOCK - FOX_HEADS)))

    d_w_qkv = _matmul(hn1_t, dproj, out_dtype=F32, tm=512, tn=768, name="mm_dw_in")
    d_w_f = _matmul(hn1_t, df, out_dtype=F32, tm=512, tn=BLOCK, name="mm_dw_in_f")
    d_w_in = _device_shards(d_w_qkv, d_w_f[:, :FOX_HEADS], cin, W_IN_PAD)
    dhn1, in_sibling = _matmul(dproj, w_qkv, nt=True, out_dtype=F32, tm=tm, tn=512,
                               ex=_cores_exchange([d_w_in]), name="mm_d_hn1")
    in_sum = _add_cores(d_w_in, in_sibling, core, name="rs_add_w_in")
    dh0, dg_pre_mix, in_chips = _rms_bwd(h0, ln_pre_mix, dhn1, dh1, out_dtype=F32, dy2=(df, w_f),
                                         ex=_chips_exchange([in_sum[1]]), name="rms_bwd_pre_mix")
    grad_x = dh0[BLOCK:][None]
    d_meta = dh0[PAD_ROWS:BLOCK]

    tags = ["w_in", "w_out", "w_gate_up", "w_down"]
    chip_sum = [in_sum[0]] + [s[0] for s in ffn_sums]
    from_chips = [in_chips] + list(ffn_chips)
    shard_w = [(w_in, m_w_in, v_w_in), (w_out, m_w_out, v_w_out), (w_gate_up, m_w_gate_up, v_w_gate_up),
               (w_down, m_w_down, v_w_down)]
    segs = [[(0, 0, cin)], [(0, 0, D_MODEL)], [(0, 0, hid), (HID_PAD, hid, hid)], [(0, 0, D_MODEL)]]
    tas = [256, BLOCK, 256, hid]
    big = [{}, {}, {}, {}]
    for i, t in enumerate(tags):
        w_t, m_t, v_t = shard_w[i]
        res = _sum_adamw(chip_sum[i], from_chips[i], chip, w_t[0], m_t[0], v_t[0], segs=segs[i], ta=tas[i],
                         name="rs_adamw_" + t)
        for kind in range(4):
            big[kind][t] = res[kind][None]

    loss_row = jnp.pad(loss_acc[0:1, 0:1] * (0.5 / D_MODEL), ((0, 0), (0, D_MODEL - 1)))
    s_small = _pack_small(d_tab.T, dg_pre_mix, dg_post_mix, dg_pre_ffn, dg_post_ffn, d_bf, d_sink,
                          extra=loss_row, meta=d_meta)
    w_s = _pack_small(rel_bias, ln_pre_mix, ln_post_mix, ln_pre_ffn, ln_post_ffn, b_forget, sinks)
    m_s = _pack_small(m_rel_bias, m_ln_pre_mix, m_ln_post_mix, m_ln_pre_ffn, m_ln_post_ffn, m_b_forget, m_sinks)
    v_s = _pack_small(v_rel_bias, v_ln_pre_mix, v_ln_post_mix, v_ln_pre_ffn, v_ln_post_ffn, v_b_forget, v_sinks)
    small = _small_allreduce_adamw(s_small, w_s, m_s, v_s, name="small_allreduce_adamw")
    loss = small[0][5, 0]
    mcols = meta_tokens.shape[1]
    g_meta_mine = lax.dynamic_slice(small[0][8:8 + N_META], (0, (4 * x_i + 2 * y_i + c_i) * mcols), (N_META, mcols))
    big[0]["meta_tokens"] = g_meta_mine
    for kind, arr in enumerate(_adamw(meta_tokens, g_meta_mine, m_meta_tokens, v_meta_tokens, name="adamw_meta")):
        big[kind + 1]["meta_tokens"] = arr
    small = [_unpack_small(p) for p in small]

    names = ["meta_tokens", "rel_bias", "ln_pre_mix", "ln_post_mix", "ln_pre_ffn", "ln_post_ffn", "w_in",
             "b_forget", "sinks", "w_out", "w_gate_up", "w_down"]
    outs = [loss, grad_x]
    for kind in range(4):
        for nme in names:
            outs.append(big[kind][nme] if nme in big[kind] else small[kind][nme])
    return tuple(outs)
```

```python
import math

import numpy as np
import jax
import jax.numpy as jnp
from jax import lax
from jax.experimental import pallas as pl
from jax.experimental.pallas import tpu as pltpu

F32 = jnp.float32
BF16 = jnp.bfloat16
HIGHEST = lax.Precision.HIGHEST
MESH = pl.DeviceIdType.MESH

N_DEV = 8
D_MODEL = 1024
N_META = 16
HEAD_DIM = 64
SWA_Q_HEADS = 8
SWA_KV_HEADS = 2
SWA_GROUP = 4
FOX_HEADS = 8
FOX_W = FOX_HEADS * HEAD_DIM
SWA_Q_W = SWA_Q_HEADS * HEAD_DIM
BLOCK = 128
PAD_ROWS = BLOCK - N_META
N_BUCKETS = 32
MAX_DISTANCE = 128
D_FF = 2816
D_QKV = 2304
D_PROJ = D_QKV + FOX_HEADS
D_PROJ_PAD = 2560
EPS = 1e-6
NEG = -1e30
SCALE = HEAD_DIM ** -0.5
ADAM_LR, ADAM_B1, ADAM_B2, ADAM_EPS, ADAM_WD, ADAM_STEP = 0.001, 0.9, 0.999, 1e-08, 0.01, 10
VMEM_LIMIT = 56 * 1024 * 1024
FOX_TILE = 384
FOX_GROUP = 4
W_IN_PAD = 384
HID_PAD = 384

NT = (((1,), (1,)), ((), ()))
NN = (((1,), (0,)), ((), ()))
TN = (((0,), (0,)), ((), ()))


def _params(sem=None, **kw):
    if sem is not None:
        kw["dimension_semantics"] = sem
    return pltpu.CompilerParams(vmem_limit_bytes=VMEM_LIMIT, **kw)


def _tile(n, target, mult=16):
    best = None
    for t in range(mult, min(n, target) + 1, mult):
        if n % t == 0:
            best = t
    assert best is not None, (n, target)
    return best


def _matmul(a, b, *, nt=False, ta=False, b_shards=False, out_shards=False, out_dtype, tm, tn=None, tk=None,
            ex=None, name):
    M, K = a.shape[::-1] if ta else a.shape
    assert not (ta and (nt or b_shards))
    k_shards = b.shape[0] if (b_shards and nt) else 0
    if k_shards:
        N, ks = b.shape[1], b.shape[2]
        assert tk is None and K == k_shards * ks
    elif b_shards:
        N, tn = b.shape[0] * b.shape[2], b.shape[2]
    else:
        N = b.shape[0] if nt else b.shape[1]
    tk = K if tk is None else tk
    assert M % tm == 0 and N % tn == 0 and K % tk == 0, (name, a.shape, b.shape, tm, tn, tk)
    nk = K // tk
    dn = NT if nt else (TN if ta else NN)
    a_spec = pl.BlockSpec((tk, tm), lambda i, j, k: (k, i)) if ta else pl.BlockSpec((tm, tk), lambda i, j, k: (i, k))

    def body(a_ref, b_ref, o_ref, *scr):
        if k_shards:
            part = sum(lax.dot_general(a_ref[:, s * ks:(s + 1) * ks], b_ref[s], NT, preferred_element_type=F32)
                       for s in range(k_shards))
        else:
            part = lax.dot_general(a_ref[...], b_ref[...], dn, preferred_element_type=F32)
        if nk == 1:
            o_ref[...] = part.astype(o_ref.dtype)
        else:
            acc = scr[0]
            k = pl.program_id(2)

            @pl.when(k == 0)
            def _():
                acc[...] = part

            @pl.when(k > 0)
            def _():
                acc[...] += part

            @pl.when(k == nk - 1)
            def _():
                o_ref[...] = acc[...].astype(o_ref.dtype)

    if k_shards:
        b_spec = pl.BlockSpec((k_shards, tn, ks), lambda i, j, k: (0, j, 0))
    elif b_shards:
        b_spec = pl.BlockSpec((None, tk, tn), lambda i, j, k: (j, k, 0))
    elif nt:
        b_spec = pl.BlockSpec((tn, tk), lambda i, j, k: (j, k))
    else:
        b_spec = pl.BlockSpec((tk, tn), lambda i, j, k: (k, j))
    if out_shards:
        out_shape = jax.ShapeDtypeStruct((N // tn, M, tn), out_dtype)
        out_spec = pl.BlockSpec((None, tm, tn), lambda i, j, k: (j, i, 0))
    else:
        out_shape = jax.ShapeDtypeStruct((M, N), out_dtype)
        out_spec = pl.BlockSpec((tm, tn), lambda i, j, k: (i, j))
    grid = (M // tm, N // tn, nk)
    body, x_in, x_in_specs, x_out, x_out_specs, x_scr = _carry(ex, grid, 2, 1, body)
    res = pl.pallas_call(
        body,
        out_shape=(out_shape, *x_out),
        grid=grid,
        in_specs=[a_spec, b_spec] + x_in_specs,
        out_specs=(out_spec, *x_out_specs),
        scratch_shapes=([pltpu.VMEM((tm, tn), F32)] if nk > 1 else []) + x_scr,
        compiler_params=_params(("parallel", "parallel", "arbitrary") if ex is None else ("arbitrary",) * 3),
        name=name,
    )(a, b, *x_in)
    return res[0] if ex is None else res


def _rstd(x):
    return lax.rsqrt(jnp.mean(x * x, axis=-1, keepdims=True) + EPS)


def _rms_fwd(x, g, *, name):
    T, D = x.shape
    tm = _tile(T, 512)

    def body(x_ref, g_ref, o_ref, ot_ref):
        x = x_ref[...]
        y = x * _rstd(x) * g_ref[...]
        o_ref[...] = y.astype(o_ref.dtype)
        ot_ref[...] = y.T.astype(ot_ref.dtype)

    return pl.pallas_call(
        body, out_shape=(jax.ShapeDtypeStruct((T, D), BF16), jax.ShapeDtypeStruct((D, T), BF16)), grid=(T // tm,),
        in_specs=[pl.BlockSpec((tm, D), lambda i: (i, 0)), pl.BlockSpec((1, D), lambda i: (0, 0))],
        out_specs=(pl.BlockSpec((tm, D), lambda i: (i, 0)), pl.BlockSpec((D, tm), lambda i: (0, i))),
        compiler_params=_params(("parallel",)), name=name)(x, g)


def _post_res_norm(a, g_post, h, g_pre, *, name):
    T, D = a.shape
    tm = _tile(T, 384, BLOCK)

    def body(a_ref, gp_ref, h_ref, gn_ref, h1_ref, o_ref, ot_ref):
        a = a_ref[...]
        h1 = h_ref[...] + a * _rstd(a) * gp_ref[...]
        h1_ref[...] = h1
        y = h1 * _rstd(h1) * gn_ref[...]
        o_ref[...] = y.astype(o_ref.dtype)
        ot_ref[...] = y.T.astype(ot_ref.dtype)

    row = pl.BlockSpec((tm, D), lambda i: (i, 0))
    vec = pl.BlockSpec((1, D), lambda i: (0, 0))
    return pl.pallas_call(
        body, out_shape=(jax.ShapeDtypeStruct((T, D), F32), jax.ShapeDtypeStruct((T, D), BF16),
                         jax.ShapeDtypeStruct((D, T), BF16)), grid=(T // tm,),
        in_specs=[row, vec, row, vec], out_specs=(row, row, pl.BlockSpec((D, tm), lambda i: (0, i))),
        compiler_params=_params(("parallel",)), name=name)(a, g_post, h, g_pre)


def _loss_head(a, g, h, target, *, name):
    T, D = a.shape
    tm = _tile(T, 512)

    def body(a_ref, g_ref, h_ref, t_ref, dy_ref, da_ref, dg_ref, loss_ref):
        i = pl.program_id(0)
        a = a_ref[...]
        r = _rstd(a)
        ah = a * r
        y = h_ref[...] + ah * g_ref[...]
        rows = i * tm + lax.broadcasted_iota(jnp.int32, (tm, 1), 0)
        err = jnp.where(rows >= BLOCK, y - t_ref[...], 0.0)
        dy = err / D
        dy_ref[...] = dy
        dah = dy * g_ref[...]
        da_ref[...] = (r * (dah - ah * jnp.mean(dah * ah, axis=-1, keepdims=True))).astype(da_ref.dtype)
        part = jnp.sum(jnp.sum(err * err, axis=1, keepdims=True), axis=0, keepdims=True)

        @pl.when(i == 0)
        def _():
            loss_ref[...] = jnp.zeros_like(loss_ref)
            dg_ref[...] = jnp.zeros_like(dg_ref)

        loss_ref[...] += jnp.broadcast_to(part, loss_ref.shape)
        dg_ref[...] += jnp.sum(dy * ah, axis=0, keepdims=True)

    row = pl.BlockSpec((tm, D), lambda i: (i, 0))
    vec = pl.BlockSpec((1, D), lambda i: (0, 0))
    return pl.pallas_call(
        body, out_shape=(jax.ShapeDtypeStruct((T, D), F32), jax.ShapeDtypeStruct((T, D), BF16),
                         jax.ShapeDtypeStruct((1, D), F32), jax.ShapeDtypeStruct((8, 128), F32)),
        grid=(T // tm,),
        in_specs=[row, vec, row, row],
        out_specs=(row, row, vec, pl.BlockSpec((8, 128), lambda i: (0, 0))),
        compiler_params=_params(("arbitrary",)), name=name)(a, g, h, target)


def _rms_bwd(x, g, dy, res, *, out_dtype, dy2=None, then=None, ex=None, name):
    T, D = x.shape
    tm = _tile(T, 512)
    has_res = res is not None
    has_dy2 = 2 if dy2 is not None else 0
    n_in = 3 + has_dy2 + has_res + (2 if then is not None else 0)
    n_out = 2 + (2 if then is not None else 0)

    def pull_back(x, g, dy):
        r = _rstd(x)
        xh = x * r
        dxh = dy * g
        return r * (dxh - xh * jnp.mean(dxh * xh, axis=-1, keepdims=True)), jnp.sum(dy * xh, axis=0, keepdims=True)

    def body(*refs):
        ins, outs = refs[:n_in], refs[n_in:]
        i = pl.program_id(0)

        @pl.when(i == 0)
        def _():
            for ref in outs[1::2]:
                ref[...] = jnp.zeros_like(ref)

        dy_all = ins[2][...].astype(F32)
        if has_dy2:
            dy_all = dy_all + lax.dot_general(ins[3][...], ins[4][...], NT, preferred_element_type=F32)
        dx, dg = pull_back(ins[0][...], ins[1][...], dy_all)
        if has_res:
            dx = dx + ins[3 + has_dy2][...]
        outs[0][...] = dx.astype(outs[0].dtype)
        outs[1][...] += dg
        if then is not None:
            dx2, dg2 = pull_back(ins[n_in - 2][...], ins[n_in - 1][...], dx)
            outs[2][...] = dx2.astype(outs[2].dtype)
            outs[3][...] += dg2

    row = pl.BlockSpec((tm, D), lambda i: (i, 0))
    vec = pl.BlockSpec((1, D), lambda i: (0, 0))
    ins = [x, g, dy] + (list(dy2) if has_dy2 else []) + ([res] if has_res else []) + (list(then) if then is not None else [])
    dy2_specs = ([pl.BlockSpec((tm, dy2[0].shape[1]), lambda i: (i, 0)), pl.BlockSpec(dy2[1].shape, lambda i: (0, 0))]
                 if has_dy2 else [])
    in_specs = [row, vec, row] + dy2_specs + ([row] if has_res else []) + ([row, vec] if then is not None else [])
    out_shape = [jax.ShapeDtypeStruct((T, D), out_dtype), jax.ShapeDtypeStruct((1, D), F32)]
    out_specs = [row, vec]
    if then is not None:
        out_shape += [jax.ShapeDtypeStruct((T, D), BF16), jax.ShapeDtypeStruct((1, D), F32)]
        out_specs += [row, vec]
    grid = (T // tm,)
    body, x_in, x_in_specs, x_out, x_out_specs, x_scr = _carry(ex, grid, n_in, n_out, body)
    return pl.pallas_call(
        body, out_shape=(*out_shape, *x_out), grid=grid,
        in_specs=in_specs + x_in_specs, out_specs=(*out_specs, *x_out_specs), scratch_shapes=x_scr,
        compiler_params=_params(("arbitrary",)), name=name)(*ins, *x_in)


def _gate_up_swiglu(a, w, *, name):
    T, D = a.shape
    S, n = w.shape[0] // 2, w.shape[2]
    tm = _tile(T, 1408, BLOCK)

    def body(a_ref, wg_ref, wu_ref, g_ref, u_ref, o_ref, ot_ref):
        x = a_ref[...]
        g = jnp.dot(x, wg_ref[...], preferred_element_type=F32)
        u = jnp.dot(x, wu_ref[...], preferred_element_type=F32)
        g16, u16 = g.astype(BF16), u.astype(BF16)
        g_ref[...] = g16
        u_ref[...] = u16
        gr = g16.astype(F32)
        act = gr / (1.0 + jnp.exp(-gr)) * u16.astype(F32)
        o_ref[...] = act.astype(o_ref.dtype)
        ot_ref[...] = act.T.astype(ot_ref.dtype)

    tile = pl.BlockSpec((tm, n), lambda i, j: (i, j))
    shp = jax.ShapeDtypeStruct((T, S * n), BF16)
    return pl.pallas_call(
        body, out_shape=(shp, shp, shp, jax.ShapeDtypeStruct((S * n, T), BF16)), grid=(T // tm, S),
        in_specs=[pl.BlockSpec((tm, D), lambda i, j: (i, 0)),
                  pl.BlockSpec((None, D, n), lambda i, j: (j, 0, 0)),
                  pl.BlockSpec((None, D, n), lambda i, j: (j + S, 0, 0))],
        out_specs=(tile, tile, tile, pl.BlockSpec((n, tm), lambda i, j: (j, i))),
        compiler_params=_params(("parallel", "parallel")), name=name)(a, w, w)


def _d_act_swiglu(dff, w_down, gate, up, *, name):
    T, D = dff.shape
    F = w_down.shape[0]
    tm = _tile(T, 384)
    tf = _tile(F, 768, BLOCK)

    def body(d_ref, w_ref, g_ref, u_ref, o_ref):
        dy = d_ref[...]
        for c in range(0, F, tf):
            d = lax.dot_general(dy, w_ref[c:c + tf, :], NT, preferred_element_type=F32)
            g = g_ref[:, c:c + tf].astype(F32)
            u = u_ref[:, c:c + tf].astype(F32)
            sg = 1.0 / (1.0 + jnp.exp(-g))
            o_ref[:, c:c + tf] = (d * u * (sg * (1.0 + g * (1.0 - sg)))).astype(o_ref.dtype)
            o_ref[:, F + c:F + c + tf] = (d * (g * sg)).astype(o_ref.dtype)

    row = pl.BlockSpec((tm, F), lambda i: (i, 0))
    return pl.pallas_call(
        body, out_shape=jax.ShapeDtypeStruct((T, 2 * F), BF16), grid=(T // tm,),
        in_specs=[pl.BlockSpec((tm, D), lambda i: (i, 0)), pl.BlockSpec((F, D), lambda i: (0, 0)), row, row],
        out_specs=pl.BlockSpec((tm, 2 * F), lambda i: (i, 0)),
        compiler_params=_params(("parallel",)), name=name)(dff, w_down, gate, up)


def _fox_gates_fwd(f_t, b, *, name):
    H, T = f_t.shape
    nb = T // BLOCK

    def body(f_ref, b_ref, cum_ref, col_ref):
        f = f_ref[...] + b_ref[...]
        ls = jnp.minimum(f, 0.0) - jnp.log(1.0 + jnp.exp(-jnp.abs(f)))
        t = lax.broadcasted_iota(jnp.int32, (H, T), 1)
        ls = jnp.where(t >= PAD_ROWS, ls, 0.0)
        upper = (lax.broadcasted_iota(jnp.int32, (BLOCK, BLOCK), 0)
                 <= lax.broadcasted_iota(jnp.int32, (BLOCK, BLOCK), 1)).astype(F32)
        carry = jnp.zeros((H, 1), F32)
        for blk in range(nb):
            seg = ls[:, blk * BLOCK:(blk + 1) * BLOCK]
            pre = jnp.dot(seg, upper, precision=HIGHEST, preferred_element_type=F32) + carry
            cum_ref[:, blk * BLOCK:(blk + 1) * BLOCK] = pre
            col_ref[blk * BLOCK:(blk + 1) * BLOCK, :] = jnp.concatenate(
                [pre, jnp.zeros((BLOCK - H, BLOCK), F32)], axis=0).T
            carry = pre[:, BLOCK - 1:BLOCK]

    vm = pl.BlockSpec(memory_space=pltpu.VMEM)
    return pl.pallas_call(
        body, out_shape=(jax.ShapeDtypeStruct((H, T), F32), jax.ShapeDtypeStruct((T, BLOCK), F32)),
        in_specs=[vm, vm], out_specs=(vm, vm),
        compiler_params=_params(), name=name)(f_t, b)


def _fox_gates_bwd(dcq, dck, f_t, b, *, name):
    H, T = f_t.shape
    nb = T // BLOCK

    def body(dq_ref, d_ref, f_ref, b_ref, df_ref, db_ref):
        lower = (lax.broadcasted_iota(jnp.int32, (BLOCK, BLOCK), 0)
                 >= lax.broadcasted_iota(jnp.int32, (BLOCK, BLOCK), 1)).astype(F32)
        carry = jnp.zeros((H, 1), F32)
        for blk in range(nb - 1, -1, -1):
            seg = dq_ref[:, blk * BLOCK:(blk + 1) * BLOCK] - d_ref[:, blk * BLOCK:(blk + 1) * BLOCK]
            suf = jnp.dot(seg, lower, precision=HIGHEST, preferred_element_type=F32) + carry
            df_ref[:, blk * BLOCK:(blk + 1) * BLOCK] = suf
            carry = suf[:, 0:1]
        f = f_ref[...] + b_ref[...]
        t = lax.broadcasted_iota(jnp.int32, (H, T), 1)
        df = jnp.where(t >= PAD_ROWS, df_ref[...] / (1.0 + jnp.exp(f)), 0.0)
        df_ref[...] = df
        db_ref[...] = jnp.sum(df, axis=1, keepdims=True)

    vm = pl.BlockSpec(memory_space=pltpu.VMEM)
    return pl.pallas_call(
        body, out_shape=(jax.ShapeDtypeStruct((H, T), F32), jax.ShapeDtypeStruct((H, 1), F32)),
        in_specs=[vm, vm, vm, vm], out_specs=(vm, vm),
        compiler_params=_params(), name=name)(dcq, dck, f_t, b)


LANE_KC = HEAD_DIM
LANE_QC = HEAD_DIM + 3
LANE_END = HEAD_DIM + 6


def _split3(c):
    hi = c.astype(BF16).astype(F32)
    r = c - hi
    mid = r.astype(BF16).astype(F32)
    lo = (r - mid).astype(BF16).astype(F32)
    return hi, mid, lo


def _lanes(lane, data, start, terms, rest):
    out = rest
    for i, t in enumerate(terms):
        out = jnp.where(lane == start + i, t, out)
    return jnp.where(lane < HEAD_DIM, data, out)


def _fox_prep(proj, cum_col, *, name):
    T = proj.shape[0]
    tm = FOX_TILE
    nt = T // tm
    H = FOX_HEADS
    lanes = 2 * HEAD_DIM
    first = (proj.shape[1] - 3 * H * HEAD_DIM) // lanes

    def body(q_ref, k_ref, v_ref, c_ref, qa_ref, ka_ref, va_ref):
        p = pl.program_id(0)
        i = pl.program_id(1)
        lane = lax.broadcasted_iota(jnp.int32, (tm, lanes), 1)
        rows = i * tm + lax.broadcasted_iota(jnp.int32, (tm, 1), 0)
        q2 = q_ref[...].astype(F32)
        k2 = k_ref[...].astype(F32)
        v2 = v_ref[...].astype(F32)
        cum = c_ref[...]
        for e in range(2):
            c = jnp.sum(jnp.where(lane == 2 * p + e, cum, 0.0), axis=1, keepdims=True)
            ck = jnp.where(rows >= PAD_ROWS, c, -NEG)
            qe, ke, ve = (q2, k2, v2) if e == 0 else tuple(pltpu.roll(a, HEAD_DIM, 1) for a in (q2, k2, v2))
            one = jnp.where(lane < LANE_END, 1.0, 0.0)
            qa = _lanes(lane, qe * SCALE, LANE_QC, _split3(c), jnp.where(lane < LANE_QC, -1.0, 0.0))
            ka = _lanes(lane, ke, LANE_KC, _split3(ck), one)
            va = jnp.where(lane < HEAD_DIM, ve, jnp.where(lane < LANE_QC, 1.0, 0.0))
            qa_ref[e] = qa.astype(BF16)
            ka_ref[e] = ka.astype(BF16)
            va_ref[e] = va.astype(BF16)

    pairs = FOX_GROUP // 2

    def col(part):
        return pl.BlockSpec((tm, lanes),
                            lambda p, i: (i, first + 3 * pairs * (p // pairs) + part * pairs + p % pairs))

    out = pl.BlockSpec((2, tm, lanes), lambda p, i: (p, i, 0))
    shp = jax.ShapeDtypeStruct((H, T, lanes), BF16)
    return pl.pallas_call(
        body, out_shape=(shp, shp, shp), grid=(H // 2, nt),
        in_specs=[col(0), col(1), col(2), pl.BlockSpec((tm, lanes), lambda p, i: (i, 0))],
        out_specs=(out, out, out),
        compiler_params=_params(("parallel", "parallel")), name=name)(proj, proj, proj, cum_col)


def _fox_fwd(q_aug, k_aug, v_aug, mix, *, ex=None, name):
    H, T, lanes = q_aug.shape
    tq = FOX_TILE
    nq = T // tq
    G = FOX_HEADS

    def body(q_ref, k_ref, v_ref, mix_ref, o_ref, lse_ref, m_scr, acc_scr):
        i = pl.program_id(1)
        m_scr[...] = jnp.full(m_scr.shape, NEG, F32)
        acc_scr[...] = jnp.zeros(acc_scr.shape, F32)

        def step(kb, diag):
            off = pl.multiple_of(kb * tq, tq)
            s_t = [lax.dot_general(k_ref[g, pl.ds(off, tq), :], q_ref[g], NT, preferred_element_type=F32)
                   for g in range(G)]
            if diag:
                r = lax.broadcasted_iota(jnp.int32, (tq, tq), 0)
                c = lax.broadcasted_iota(jnp.int32, (tq, tq), 1)
                s_t = [jnp.where(c >= r, s, NEG) for s in s_t]
            m_prev = [m_scr[g] for g in range(G)]
            m_new = [jnp.maximum(m_prev[g], jnp.max(s_t[g], axis=0, keepdims=True)) for g in range(G)]
            p_t = [jnp.exp(s_t[g] - m_new[g]).astype(BF16) for g in range(G)]
            pv = [lax.dot_general(v_ref[g, pl.ds(off, tq), :], p_t[g], TN, preferred_element_type=F32)
                  for g in range(G)]
            for g in range(G):
                acc_scr[g] = jnp.exp(m_prev[g] - m_new[g]) * acc_scr[g] + pv[g]
                m_scr[g] = m_new[g]

        def loop_body(kb, carry):
            step(kb, False)
            return carry

        lax.fori_loop(0, i, loop_body, 0)
        step(i, True)
        lane = lax.broadcasted_iota(jnp.int32, (tq, lanes), 1)
        outs = []
        for g in range(G):
            acc = acc_scr[g]
            lse_ref[g] = m_scr[g] + jnp.log(acc[HEAD_DIM:HEAD_DIM + 1, :])
            acc_t = acc.T
            outs.append(acc_t / acc_t[:, HEAD_DIM:HEAD_DIM + 1])
        for pair in range(G // 2):
            o_ref[:, pair * lanes:(pair + 1) * lanes] = jnp.where(
                lane < HEAD_DIM, outs[2 * pair], pltpu.roll(outs[2 * pair + 1], HEAD_DIM, 1)).astype(o_ref.dtype)

    blk = pl.BlockSpec((G, tq, lanes), lambda h, i: (h, i, 0))
    full = pl.BlockSpec((G, T, lanes), lambda h, i: (h, 0, 0))
    grid = (H // G, nq)
    first = mix.shape[1] // (G * HEAD_DIM) - H // G
    body, x_in, x_in_specs, x_out, x_out_specs, x_scr = _carry(ex, grid, 4, 2, body)
    return pl.pallas_call(
        body,
        out_shape=(jax.ShapeDtypeStruct(mix.shape, mix.dtype), jax.ShapeDtypeStruct((H, nq, 1, tq), F32), *x_out),
        grid=grid,
        in_specs=[blk, full, full, pl.BlockSpec(memory_space=pl.ANY)] + x_in_specs,
        out_specs=(pl.BlockSpec((tq, G * HEAD_DIM), lambda h, i: (i, first + h)),
                   pl.BlockSpec((G, None, 1, tq), lambda h, i: (h, i, 0, 0)), *x_out_specs),
        input_output_aliases={3: 0},
        scratch_shapes=[pltpu.VMEM((G, 1, tq), F32), pltpu.VMEM((G, lanes, tq), F32)] + x_scr,
        compiler_params=_params(("arbitrary", "arbitrary")), name=name)(q_aug, k_aug, v_aug, mix, *x_in)


def _fox_prep_bwd(dmix, mix, *, name):
    T = dmix.shape[0]
    H = FOX_HEADS
    tm = FOX_TILE
    lanes = 2 * HEAD_DIM
    first = mix.shape[1] // lanes - H // 2

    def body(d_ref, o_ref, da_ref):
        lane = lax.broadcasted_iota(jnp.int32, (tm, lanes), 1)
        d2 = d_ref[...].astype(F32)
        prod = d2 * o_ref[...].astype(F32)
        for e in range(2):
            de = d2 if e == 0 else pltpu.roll(d2, HEAD_DIM, 1)
            delta = jnp.sum(jnp.where(lane // HEAD_DIM == e, prod, 0.0), axis=1, keepdims=True)
            da_ref[e] = _lanes(lane, de, LANE_KC, _split3(-delta), jnp.zeros((), F32)).astype(BF16)

    pair = pl.BlockSpec((tm, lanes), lambda p, i: (i, first + p))
    return pl.pallas_call(
        body, out_shape=jax.ShapeDtypeStruct((H, T, lanes), BF16), grid=(H // 2, T // tm),
        in_specs=[pair, pair],
        out_specs=pl.BlockSpec((2, tm, lanes), lambda p, i: (p, i, 0)),
        compiler_params=_params(("parallel", "parallel")), name=name)(dmix, mix)


def _fox_bwd(q_aug, k_aug, v_aug, do_aug, lse_row, dproj, *, ex=None, name):
    H, T, lanes = q_aug.shape
    tq = FOX_TILE
    nq = T // tq
    G = FOX_GROUP

    def side_by_side(tiles, scale=None):
        lane = lax.broadcasted_iota(jnp.int32, tiles[0].shape, 1)
        out = [jnp.where(lane < HEAD_DIM, tiles[2 * p], pltpu.roll(tiles[2 * p + 1], HEAD_DIM, 1))
               for p in range(G // 2)]
        out = jnp.concatenate(out, axis=1)
        return out if scale is None else out * scale

    def body(q_ref, k_ref, v_ref, do_ref, lse_ref, dproj_in, out_ref, dcq_ref, dck_ref, dk_acc, dv_acc, dq_ref):
        j = pl.program_id(1)

        @pl.when(j == 0)
        def _():
            dq_ref[...] = jnp.zeros(dq_ref.shape, F32)
            dcq_ref[...] = jnp.zeros(dcq_ref.shape, F32)

        dk_acc[...] = jnp.zeros(dk_acc.shape, F32)
        dv_acc[...] = jnp.zeros(dv_acc.shape, F32)

        def step(qb, diag):
            off = pl.multiple_of(qb * tq, tq)
            heads = range(G)
            qa = [q_ref[g, pl.ds(off, tq), :] for g in heads]
            da = [do_ref[g, pl.ds(off, tq), :] for g in heads]
            s_t = [lax.dot_general(k_ref[g], qa[g], NT, preferred_element_type=F32) for g in heads]
            dp_t = [lax.dot_general(v_ref[g], da[g], NT, preferred_element_type=F32) for g in heads]
            p_t = [jnp.exp(s_t[g] - lse_ref[g, qb]) for g in heads]
            if diag:
                r = lax.broadcasted_iota(jnp.int32, (tq, tq), 0)
                c = lax.broadcasted_iota(jnp.int32, (tq, tq), 1)
                p_t = [jnp.where(c >= r, p, 0.0) for p in p_t]
            dsb = [(p_t[g] * dp_t[g]).astype(BF16) for g in heads]
            dv = [jnp.dot(p_t[g].astype(BF16), da[g], preferred_element_type=F32) for g in heads]
            dk = [jnp.dot(dsb[g], qa[g], preferred_element_type=F32) for g in heads]
            dq = [lax.dot_general(k_ref[g], dsb[g], TN, preferred_element_type=F32) for g in heads]
            for g in heads:
                dv_acc[g] += dv[g]
                dk_acc[g] += dk[g]
                dq_ref[g, qb] += dq[g]
                dcq_ref[g, qb] += jnp.sum(dsb[g].astype(F32), axis=0, keepdims=True)

        step(j, True)

        def loop_body(qb, carry):
            step(qb, False)
            return carry

        lax.fori_loop(j + 1, nq, loop_body, 0)
        dk = [dk_acc[g] for g in range(G)]
        out_ref[:, 0:wide] = side_by_side([dq_ref[g, j].T for g in range(G)], SCALE).astype(out_ref.dtype)
        out_ref[:, wide:2 * wide] = side_by_side(dk).astype(out_ref.dtype)
        out_ref[:, 2 * wide:3 * wide] = side_by_side([dv_acc[g] for g in range(G)]).astype(out_ref.dtype)
        for g in range(G):
            dck_ref[g] = -dk[g].T[LANE_KC:LANE_KC + 1, :]

    blk = pl.BlockSpec((G, tq, lanes), lambda h, j: (h, j, 0))
    full = pl.BlockSpec((G, T, lanes), lambda h, j: (h, 0, 0))
    wide = G * HEAD_DIM
    first = dproj.shape[1] // (3 * wide) - H // G
    grid = (H // G, nq)
    body, x_in, x_in_specs, x_out, x_out_specs, x_scr = _carry(ex, grid, 6, 3, body)
    rows = jax.ShapeDtypeStruct((H, nq, 1, tq), F32)
    all_rows = pl.BlockSpec((G, nq, 1, tq), lambda h, j: (h, 0, 0, 0))
    return pl.pallas_call(
        body,
        out_shape=(jax.ShapeDtypeStruct(dproj.shape, dproj.dtype), rows, rows, *x_out),
        grid=grid,
        in_specs=[full, blk, blk, full, all_rows, pl.BlockSpec(memory_space=pl.ANY)] + x_in_specs,
        out_specs=(pl.BlockSpec((tq, 3 * wide), lambda h, j: (j, first + h)), all_rows,
                   pl.BlockSpec((G, None, 1, tq), lambda h, j: (h, j, 0, 0)), *x_out_specs),
        input_output_aliases={5: 0},
        scratch_shapes=[pltpu.VMEM((G, tq, lanes), F32), pltpu.VMEM((G, tq, lanes), F32),
                        pltpu.VMEM((G, nq, lanes, tq), F32)] + x_scr,
        compiler_params=_params(("arbitrary", "arbitrary")), name=name,
    )(q_aug, k_aug, v_aug, do_aug, lse_row, dproj, *x_in)


def _t5_bucket_np(d):
    n = np.maximum(d, 0).astype(np.int32)
    max_exact = N_BUCKETS // 2
    nf = np.maximum(n, 1).astype(np.float32)
    large = max_exact + (np.log(nf / max_exact) / math.log(MAX_DISTANCE / max_exact)
                         * (N_BUCKETS - max_exact)).astype(np.int32)
    large = np.minimum(large, N_BUCKETS - 1)
    return np.where(n < max_exact, n, large)


def _bucket_onehots():
    k = np.arange(BLOCK)[:, None]
    q = np.arange(BLOCK)[None, :]
    eye = np.eye(N_BUCKETS, dtype=np.float32)
    cur = eye[_t5_bucket_np(q - k).reshape(-1)]
    prev = eye[_t5_bucket_np(BLOCK + q - k).reshape(-1)]
    return cur, prev


SWA_K_COL = SWA_Q_HEADS * HEAD_DIM // (2 * HEAD_DIM)
SWA_V_COL = SWA_K_COL + 1


def _swa_terms(raw, bc, bp, far, sink, n):
    k = lax.broadcasted_iota(jnp.int32, (BLOCK, BLOCK), 0)
    q = lax.broadcasted_iota(jnp.int32, (BLOCK, BLOCK), 1)
    never = 2 * BLOCK
    s_c = raw[0] + bc
    s_p = raw[1] + bp
    s_m = raw[2] + jnp.where(n == 1, bp, far)
    s_c = jnp.where((k <= q) & (k >= jnp.where(n >= 1, 0, PAD_ROWS)), s_c, NEG)
    s_p = jnp.where(k > q + jnp.where(n >= 2, 0, never), s_p, NEG)
    s_m = jnp.where(k >= jnp.where(n >= 1, PAD_ROWS, never), s_m, NEG)
    m = jnp.maximum(jnp.maximum(jnp.max(s_c, axis=0, keepdims=True), jnp.max(s_p, axis=0, keepdims=True)),
                    jnp.maximum(jnp.max(s_m, axis=0, keepdims=True), sink))
    e = [jnp.exp(s_c - m), jnp.exp(s_p - m), jnp.exp(s_m - m)]
    e_s = jnp.exp(sink - m)
    l = (jnp.sum(e[0], axis=0, keepdims=True) + jnp.sum(e[1], axis=0, keepdims=True)
         + jnp.sum(e[2], axis=0, keepdims=True) + e_s)
    return e, e_s, l


SWA_STEP = 3


def _swa_specs():
    R = SWA_STEP

    def window(col):
        return ([pl.BlockSpec((BLOCK, BLOCK), lambda s, w=w: (jnp.maximum(R * s - 1 + w, 0), col)) for w in range(R + 1)]
                + [pl.BlockSpec((BLOCK, BLOCK), lambda s: (0, col))])

    qblk = pl.BlockSpec((R * BLOCK, SWA_Q_HEADS * HEAD_DIM), lambda s: (s, 0))
    bias = pl.BlockSpec((SWA_Q_HEADS, BLOCK, BLOCK), lambda s: (0, 0, 0))
    smem = pl.BlockSpec(memory_space=pltpu.SMEM)
    return qblk, window(SWA_K_COL), window(SWA_V_COL), bias, smem


def _swa_own_kv(tile_ref, kv):
    lane = lax.broadcasted_iota(jnp.int32, (BLOCK, 2 * HEAD_DIM), 1)
    t = tile_ref[...].astype(F32)
    return jnp.where(lane // HEAD_DIM == kv, t, pltpu.roll(t, HEAD_DIM, 1)).astype(BF16)


def _swa_fwd(proj, bc, bp, far, sinks, *, name):
    T = proj.shape[0]
    nb = T // BLOCK
    G = SWA_GROUP
    Hq = SWA_Q_HEADS
    lanes = 2 * HEAD_DIM

    R = SWA_STEP
    assert nb % R == 0

    def body(*refs):
        q_ref, k_refs, v_refs = refs[0], refs[1:R + 3], refs[R + 3:2 * R + 5]
        bc_ref, bp_ref, far_ref, sink_ref, o_ref = refs[2 * R + 5:]
        s = pl.program_id(0)
        lane = lax.broadcasted_iota(jnp.int32, (BLOCK, lanes), 1)
        kvs = range(SWA_KV_HEADS)
        kk = [[_swa_own_kv(ref, kv) for ref in k_refs] for kv in kvs]
        vv = [[_swa_own_kv(ref, kv) for ref in v_refs] for kv in kvs]
        chains = [(r, h) for r in range(R) for h in range(Hq)]
        tiles = lambda r: (r + 1, r, R + 1)
        q2 = {(r, pair): q_ref[r * BLOCK:(r + 1) * BLOCK, pair * lanes:(pair + 1) * lanes].astype(F32) * SCALE
              for r in range(R) for pair in range(Hq // 2)}
        qm = {c: jnp.where(lane // HEAD_DIM == c[1] % 2, q2[c[0], c[1] // 2], 0.0).astype(BF16) for c in chains}
        raw = {c: [lax.dot_general(kk[c[1] // G][w], qm[c], NT, preferred_element_type=F32) for w in tiles(c[0])]
               for c in chains}
        terms = {c: _swa_terms(raw[c], bc_ref[c[1]], bp_ref[c[1]], far_ref[c[1]], sink_ref[c[1]], R * s + c[0])
                 for c in chains}
        o_t = {c: sum(lax.dot_general(vv[c[1] // G][w], terms[c][0][b].astype(BF16), TN, preferred_element_type=F32)
                      for b, w in enumerate(tiles(c[0]))) for c in chains}
        outs = {c: (o_t[c] / terms[c][2]).T for c in chains}
        for r in range(R):
            for pair in range(Hq // 2):
                o_ref[r * BLOCK:(r + 1) * BLOCK, pair * lanes:(pair + 1) * lanes] = jnp.where(
                    lane < HEAD_DIM, outs[r, 2 * pair], outs[r, 2 * pair + 1]).astype(o_ref.dtype)

    qblk, keys, vals, bias, smem = _swa_specs()
    return pl.pallas_call(
        body, out_shape=jax.ShapeDtypeStruct((T, D_MODEL), BF16), grid=(nb // R,),
        in_specs=[qblk] + keys + vals + [bias, bias, smem, smem],
        out_specs=qblk,
        compiler_params=_params(("parallel",)), name=name,
    )(proj, *([proj] * (2 * R + 4)), bc, bp, far, sinks)


def _swa_bwd(proj, dmix, bc, bp, far, sinks, *, ex=None, name):
    T, width = proj.shape
    nb = T // BLOCK
    G = SWA_GROUP
    Hq = SWA_Q_HEADS
    lanes = 2 * HEAD_DIM
    qw = Hq * HEAD_DIM
    own_w = qw + 2 * lanes

    R = SWA_STEP
    assert nb % R == 0
    n_in = 2 * R + 10

    def body(*refs):
        q_ref, k_refs, v_refs = refs[0], refs[1:R + 3], refs[R + 3:2 * R + 5]
        do_ref, bc_ref, bp_ref, far_ref, sink_ref = refs[2 * R + 5:n_in]
        dp_ref, dbc_ref, dbp_ref, dbf_ref, dsk_ref, dk_acc, dv_acc = refs[n_in:]
        s = pl.program_id(0)

        @pl.when(s == 0)
        def _():
            for ref in (dk_acc, dv_acc, dbc_ref, dbp_ref, dbf_ref, dsk_ref):
                ref[...] = jnp.zeros(ref.shape, F32)

        lane = lax.broadcasted_iota(jnp.int32, (BLOCK, lanes), 1)
        kvs = range(SWA_KV_HEADS)
        kk = [[_swa_own_kv(ref, kv) for ref in k_refs] for kv in kvs]
        vv = [[_swa_own_kv(ref, kv) for ref in v_refs] for kv in kvs]
        chains = [(r, h) for r in range(R) for h in range(Hq)]
        blocks = range(3)
        tiles = lambda r: (r + 1, r, R + 1)
        sub = lambda ref, r, pair: ref[r * BLOCK:(r + 1) * BLOCK, pair * lanes:(pair + 1) * lanes]
        q2 = {(r, pair): sub(q_ref, r, pair).astype(F32) * SCALE for r in range(R) for pair in range(Hq // 2)}
        d2 = {(r, pair): sub(do_ref, r, pair) for r in range(R) for pair in range(Hq // 2)}
        own = [lane // HEAD_DIM == half for half in range(2)]
        qm = {c: jnp.where(own[c[1] % 2], q2[c[0], c[1] // 2], 0.0).astype(BF16) for c in chains}
        dom = {c: jnp.where(own[c[1] % 2], d2[c[0], c[1] // 2], jnp.zeros_like(d2[0, 0])) for c in chains}
        raw = {c: [lax.dot_general(kk[c[1] // G][w], qm[c], NT, preferred_element_type=F32) for w in tiles(c[0])]
               for c in chains}
        dp = {c: [lax.dot_general(vv[c[1] // G][w], dom[c], NT, preferred_element_type=F32) for w in tiles(c[0])]
              for c in chains}
        p, ds16 = {}, {}
        for c in chains:
            r, h = c
            n = R * s + r
            e, e_s, l = _swa_terms(raw[c], bc_ref[h], bp_ref[h], far_ref[h], sink_ref[h], n)
            inv = 1.0 / l
            ph = [e[b] * inv for b in blocks]
            delta = sum(jnp.sum(ph[b] * dp[c][b], axis=0, keepdims=True) for b in blocks)
            ds = [ph[b] * (dp[c][b] - delta) for b in blocks]
            dsk_ref[h] += -(e_s * inv) * delta
            dbc_ref[h] += ds[0]
            dbp_ref[h] += ds[1] + jnp.where(n == 1, ds[2], 0.0)
            dbf_ref[h] += jnp.where(n >= 2, ds[2], 0.0)
            p[c] = [x.astype(BF16) for x in ph]
            ds16[c] = [x.astype(BF16) for x in ds]
        dq_t = {c: sum(lax.dot_general(kk[c[1] // G][w], ds16[c][b], TN, preferred_element_type=F32)
                       for b, w in enumerate(tiles(c[0]))) for c in chains}
        group = [range(kv * G, (kv + 1) * G) for kv in kvs]
        dk = {(r, kv): [sum(jnp.dot(ds16[r, h][b], qm[r, h], preferred_element_type=F32) for h in group[kv])
                        for b in blocks] for r in range(R) for kv in kvs}
        dv = {(r, kv): [sum(jnp.dot(p[r, h][b], dom[r, h], preferred_element_type=F32) for h in group[kv])
                        for b in blocks] for r in range(R) for kv in kvs}
        for r in range(R):
            n = R * s + r
            rows = pl.ds(pl.multiple_of(n * BLOCK, BLOCK), BLOCK)
            prev_rows = pl.ds(pl.multiple_of(jnp.maximum(n - 1, 0) * BLOCK, BLOCK), BLOCK)
            for pair in range(Hq // 2):
                dp_ref[rows, pair * lanes:(pair + 1) * lanes] = (jnp.where(
                    lane < HEAD_DIM, dq_t[r, 2 * pair].T, dq_t[r, 2 * pair + 1].T) * SCALE).astype(dp_ref.dtype)
            for acc, ref in ((dk, dk_acc), (dv, dv_acc)):
                tot = [[a + pltpu.roll(a, HEAD_DIM, 1) for a in acc[r, kv]] for kv in kvs]
                both = [jnp.where(lane < HEAD_DIM, tot[0][b], tot[1][b]) for b in blocks]
                ref[rows, :] += both[0]
                ref[prev_rows, :] += both[1]
                ref[0:BLOCK, :] += both[2]

        @pl.when(s == nb // R - 1)
        def _():
            dp_ref[:, qw:qw + lanes] = dk_acc[...].astype(dp_ref.dtype)
            dp_ref[:, qw + lanes:own_w] = dv_acc[...].astype(dp_ref.dtype)

    qblk, keys, vals, bias, smem = _swa_specs()
    dsk = pl.BlockSpec((Hq, 1, BLOCK), lambda s: (0, 0, 0))
    grid = (nb // R,)
    body, x_in, x_in_specs, x_out, x_out_specs, x_scr = _carry(ex, grid, n_in, 5, body)
    tile = jax.ShapeDtypeStruct((Hq, BLOCK, BLOCK), F32)
    return pl.pallas_call(
        body,
        out_shape=(jax.ShapeDtypeStruct((T, width), BF16), tile, tile, tile,
                   jax.ShapeDtypeStruct((Hq, 1, BLOCK), F32), *x_out),
        grid=grid,
        in_specs=[qblk] + keys + vals + [qblk, bias, bias, smem, smem] + x_in_specs,
        out_specs=(pl.BlockSpec((T, own_w), lambda s: (0, 0)), bias, bias, bias, dsk, *x_out_specs),
        scratch_shapes=[pltpu.VMEM((T, lanes), F32), pltpu.VMEM((T, lanes), F32)] + x_scr,
        compiler_params=_params(("arbitrary",)), name=name,
    )(proj, *([proj] * (2 * R + 4)), dmix, bc, bp, far, sinks, *x_in)


def _small_grads(dbc, dbp, dbf, dsk, oh_cur, oh_prev, *, name):
    Hq = dbc.shape[0]

    def body(dbc_ref, dbp_ref, dbf_ref, dsk_ref, oc_ref, op_ref, tab_ref, sink_ref):
        tab = (jnp.dot(dbc_ref[...], oc_ref[...], precision=HIGHEST, preferred_element_type=F32)
               + jnp.dot(dbp_ref[...], op_ref[...], precision=HIGHEST, preferred_element_type=F32))
        far = jnp.sum(dbf_ref[...], axis=1, keepdims=True)
        last = lax.broadcasted_iota(jnp.int32, (Hq, N_BUCKETS), 1) == N_BUCKETS - 1
        tab_ref[...] = tab + jnp.where(last, far, 0.0)
        sink_ref[...] = jnp.sum(dsk_ref[...], axis=1, keepdims=True)

    vm = pl.BlockSpec(memory_space=pltpu.VMEM)
    return pl.pallas_call(
        body, out_shape=(jax.ShapeDtypeStruct((Hq, N_BUCKETS), F32), jax.ShapeDtypeStruct((Hq, 1), F32)),
        in_specs=[vm] * 6, out_specs=(vm, vm), compiler_params=_params(), name=name,
    )(dbc.reshape(Hq, -1), dbp.reshape(Hq, -1), dbf.reshape(Hq, -1), dsk.reshape(Hq, -1), oh_cur, oh_prev)


def _coords():
    return lax.axis_index("x"), lax.axis_index("y"), lax.axis_index("c")


class _Exchange:
    def __init__(self, inputs, out_shapes, scratch, start, finish):
        self.inputs, self.out_shapes, self.scratch, self.start, self.finish = inputs, out_shapes, scratch, start, finish


def _carry(ex, grid, n_in, n_out, body):
    if ex is None:
        return body, [], [], [], [], []
    ni, no = len(ex.inputs), len(ex.out_shapes)

    def at_step(which):
        cond = None
        for axis, n in enumerate(grid):
            c = pl.program_id(axis) == (0 if which == "first" else n - 1)
            cond = c if cond is None else cond & c
        return cond

    def wrapped(*refs):
        refs = list(refs)
        n_own_scr = len(refs) - (n_in + ni + n_out + no) - len(ex.scratch)
        own_in, side_in = refs[:n_in], refs[n_in:n_in + ni]
        own_out = refs[n_in + ni:n_in + ni + n_out]
        side_out = refs[n_in + ni + n_out:n_in + ni + n_out + no]
        rest = refs[n_in + ni + n_out + no:]
        own_scr, sems = rest[:n_own_scr], rest[n_own_scr:]

        @pl.when(at_step("first"))
        def _():
            ex.start(side_in, side_out, sems)

        body(*own_in, *own_out, *own_scr)

        @pl.when(at_step("last"))
        def _():
            ex.finish(side_in, side_out, sems)

    hbm = pl.BlockSpec(memory_space=pl.ANY)
    return wrapped, list(ex.inputs), [hbm] * ni, list(ex.out_shapes), [hbm] * no, list(ex.scratch)


def _run_exchange(ex, *, name):
    ni, no = len(ex.inputs), len(ex.out_shapes)

    def body(*refs):
        ins, outs, sems = refs[:ni], refs[ni:ni + no], refs[ni + no:]
        ex.start(ins, outs, sems)
        ex.finish(ins, outs, sems)

    hbm = pl.BlockSpec(memory_space=pl.ANY)
    return pl.pallas_call(
        body, out_shape=tuple(ex.out_shapes), in_specs=[hbm] * ni, out_specs=tuple([hbm] * no),
        scratch_shapes=ex.scratch, compiler_params=_params(), name=name)(*ex.inputs)


def _gather_exchange(shards):
    nt = len(shards)

    def copies(ins, outs, sems):
        send_sems, recv_sems, local_sems = sems
        x, y, c = _coords()
        me, sibling = (x, y, c), (x, y, 1 - c)
        chips = [(1 - x, y), (x, 1 - y), (1 - x, 1 - y)]

        def slot(t, dev):
            return outs[t].at[4 * dev[0] + 2 * dev[1] + dev[2]]

        def copy(t, k, block, to, src=None):
            dst = slot(t, block)
            return pltpu.make_async_remote_copy(
                src_ref=dst if src is None else src, dst_ref=dst,
                send_sem=send_sems.at[t, k], recv_sem=recv_sems.at[t, k], device_id=to, device_id_type=MESH)

        mine = [pltpu.make_async_copy(ins[t], slot(t, me), local_sems.at[t]) for t in range(nt)]
        first = []
        for t in range(nt):
            first.append(copy(t, 0, me, sibling, src=ins[t]))
            first += [copy(t, 1 + j, me, (*chip, c), src=ins[t]) for j, chip in enumerate(chips)]
        return copy, mine, first, me, sibling, chips, c

    def start(ins, outs, sems):
        _, mine, first, *_ = copies(ins, outs, sems)
        for cp in mine + first:
            cp.start()

    def finish(ins, outs, sems):
        copy, mine, first, me, sibling, chips, c = copies(ins, outs, sems)
        passed = []
        for j, chip in enumerate(chips):
            for t in range(nt):
                copy(t, 1 + j, (*chip, c), me).wait_recv()
                cp = copy(t, 4 + j, (*chip, c), sibling)
                cp.start()
                passed.append(cp)
        for t in range(nt):
            copy(t, 0, sibling, me).wait_recv()
            for j, chip in enumerate(chips):
                copy(t, 4 + j, (*chip, 1 - c), me).wait_recv()
        for cp in first + passed:
            cp.wait_send()
        for cp in mine:
            cp.wait()

    return _Exchange(
        list(shards), [jax.ShapeDtypeStruct((N_DEV,) + s.shape, s.dtype) for s in shards],
        [pltpu.SemaphoreType.DMA((nt, 7)), pltpu.SemaphoreType.DMA((nt, 7)), pltpu.SemaphoreType.DMA((nt,))],
        start, finish)


def _swap_exchange(arrays, n_slices, copies):
    nt = len(arrays)

    def start(ins, outs, sems):
        for cp in copies(ins, outs, sems):
            cp.start()

    def finish(ins, outs, sems):
        sends = copies(ins, outs, sems)
        for cp in sends:
            cp.wait_recv()
        for cp in sends:
            cp.wait_send()

    return _Exchange(
        list(arrays), [jax.ShapeDtypeStruct((n_slices,) + a.shape[1:], a.dtype) for a in arrays],
        [pltpu.SemaphoreType.DMA((nt, n_slices)), pltpu.SemaphoreType.DMA((nt, n_slices))], start, finish)


def _cores_exchange(gs):
    def copies(ins, outs, sems):
        send_sems, recv_sems = sems
        x, y, c = _coords()
        return [pltpu.make_async_remote_copy(
            src_ref=ins[t].at[2 * j + (1 - c)], dst_ref=outs[t].at[j],
            send_sem=send_sems.at[t, j], recv_sem=recv_sems.at[t, j], device_id=(x, y, 1 - c), device_id_type=MESH)
            for t in range(len(gs)) for j in range(4)]

    return _swap_exchange(gs, 4, copies)


def _chips_exchange(ps):
    def copies(ins, outs, sems):
        send_sems, recv_sems = sems
        x, y, c = _coords()
        peers = [(1 - x, y), (x, 1 - y), (1 - x, 1 - y)]
        return [pltpu.make_async_remote_copy(
            src_ref=ins[t].at[2 * px + py], dst_ref=outs[t].at[k],
            send_sem=send_sems.at[t, k], recv_sem=recv_sems.at[t, k], device_id=(px, py, c), device_id_type=MESH)
            for t in range(len(ps)) for k, (px, py) in enumerate(peers)]

    return _swap_exchange(ps, 3, copies)


def _add_cores(g, r, core, *, name):
    _, A, B = g.shape
    ta = _tile(A, 512, 16)

    def body(core_ref, a_ref, b_ref, o_ref, o16_ref):
        s = a_ref[...] + b_ref[...]
        o_ref[...] = s
        o16_ref[...] = s.astype(BF16)

    blk = (None, ta, B)
    out = pl.BlockSpec(blk, lambda j, i, core_ref: (j, i, 0))
    return pl.pallas_call(
        body, out_shape=(jax.ShapeDtypeStruct((4, A, B), F32), jax.ShapeDtypeStruct((4, A, B), BF16)),
        grid_spec=pltpu.PrefetchScalarGridSpec(
            num_scalar_prefetch=1, grid=(4, A // ta),
            in_specs=[pl.BlockSpec(blk, lambda j, i, core_ref: (2 * j + core_ref[0], i, 0)),
                      pl.BlockSpec(blk, lambda j, i, core_ref: (j, i, 0))],
            out_specs=(out, out)),
        compiler_params=_params(("parallel", "parallel")), name=name)(core, g, r)


def _adamw_math(w, g, m, v):
    m = ADAM_B1 * m + (1.0 - ADAM_B1) * g
    v = ADAM_B2 * v + (1.0 - ADAM_B2) * (g * g)
    m_hat = m / (1.0 - ADAM_B1 ** ADAM_STEP)
    v_hat = v / (1.0 - ADAM_B2 ** ADAM_STEP)
    delta = -ADAM_LR * (m_hat / (jnp.sqrt(v_hat) + ADAM_EPS) + ADAM_WD * w)
    return delta, m, v


def _sum_adamw(p, r, chip, w, m, v, *, segs, ta, name):
    Aw, Bw = w.shape
    Bg = p.shape[2]
    assert Aw % ta == 0

    def body(chip_ref, p_ref, r0, r1, r2, w_ref, m_ref, v_ref, g_out, d_out, m_out, v_out):
        for gc, wc, n in segs:
            g = ((p_ref[:, gc:gc + n] + r0[:, gc:gc + n].astype(F32)) + r1[:, gc:gc + n].astype(F32)
                 ) + r2[:, gc:gc + n].astype(F32)
            delta, m_new, v_new = _adamw_math(w_ref[:, wc:wc + n], g, m_ref[:, wc:wc + n], v_ref[:, wc:wc + n])
            g_out[:, wc:wc + n] = g
            d_out[:, wc:wc + n] = delta
            m_out[:, wc:wc + n] = m_new
            v_out[:, wc:wc + n] = v_new

    gblk = (None, ta, Bg)
    row = pl.BlockSpec((ta, Bw), lambda i, chip_ref: (i, 0))
    rspecs = [pl.BlockSpec(gblk, (lambda i, chip_ref, k=k: (k, i, 0))) for k in range(3)]
    shp = jax.ShapeDtypeStruct((Aw, Bw), F32)
    return pl.pallas_call(
        body, out_shape=(shp, shp, shp, shp),
        grid_spec=pltpu.PrefetchScalarGridSpec(
            num_scalar_prefetch=1, grid=(Aw // ta,),
            in_specs=[pl.BlockSpec(gblk, lambda i, chip_ref: (chip_ref[0], i, 0))] + rspecs + [row, row, row],
            out_specs=(row, row, row, row)),
        compiler_params=_params(("parallel",)), name=name)(chip, p, r, r, r, w, m, v)


def _adamw(w, g, m, v, *, name):
    def body(w_ref, g_ref, m_ref, v_ref, d_out, m_out, v_out):
        delta, m_new, v_new = _adamw_math(w_ref[...], g_ref[...], m_ref[...], v_ref[...])
        d_out[...] = delta
        m_out[...] = m_new
        v_out[...] = v_new

    vm = pl.BlockSpec(memory_space=pltpu.VMEM)
    shp = jax.ShapeDtypeStruct(w.shape, F32)
    return pl.pallas_call(body, out_shape=(shp, shp, shp), in_specs=[vm] * 4, out_specs=(vm, vm, vm),
                          compiler_params=_params(), name=name)(w, g, m, v)


def _small_allreduce_adamw(s, w, m, v, *, name):
    R, W = s.shape

    def body(s_ref, w_ref, m_ref, v_ref, g_out, d_out, m_out, v_out, gath, send_sems, recv_sems):
        x, y, c = _coords()
        mine = 4 * x + 2 * y + c
        gath[mine] = s_ref[...]
        peers = [((1 - x) if k & 4 else x, (1 - y) if k & 2 else y, (1 - c) if k & 1 else c) for k in range(1, N_DEV)]
        sends = []
        for k in range(1, N_DEV):
            peer = peers[k - 1]
            sends.append(pltpu.make_async_remote_copy(
                src_ref=s_ref, dst_ref=gath.at[mine], send_sem=send_sems.at[k - 1], recv_sem=recv_sems.at[k - 1],
                device_id=peer, device_id_type=MESH))
        for cp in sends:
            cp.start()
        for k in range(1, N_DEV):
            peer = peers[k - 1]
            pltpu.make_async_remote_copy(
                src_ref=s_ref, dst_ref=gath.at[4 * peer[0] + 2 * peer[1] + peer[2]],
                send_sem=send_sems.at[k - 1], recv_sem=recv_sems.at[k - 1],
                device_id=peer, device_id_type=MESH).wait_recv()
        for cp in sends:
            cp.wait_send()
        g = gath[0]
        for d in range(1, N_DEV):
            g = g + gath[d]
        delta, m_new, v_new = _adamw_math(w_ref[...], g, m_ref[...], v_ref[...])
        g_out[...] = g
        d_out[...] = delta
        m_out[...] = m_new
        v_out[...] = v_new

    vm = pl.BlockSpec(memory_space=pltpu.VMEM)
    shp = jax.ShapeDtypeStruct((R, W), F32)
    return pl.pallas_call(
        body, out_shape=(shp, shp, shp, shp), in_specs=[vm] * 4, out_specs=(vm, vm, vm, vm),
        scratch_shapes=[pltpu.VMEM((N_DEV, R, W), F32), pltpu.SemaphoreType.DMA((N_DEV - 1,)),
                        pltpu.SemaphoreType.DMA((N_DEV - 1,))],
        compiler_params=_params(), name=name)(s, w, m, v)


def _pack_small(rel_bias, g1, g2, g3, g4, b_forget, sinks, extra=None, meta=None):
    misc = jnp.concatenate([rel_bias.reshape(-1), b_forget.reshape(-1), sinks.reshape(-1)])
    misc = jnp.concatenate([misc, jnp.zeros((D_MODEL - misc.shape[0],), F32)])[None]
    last = jnp.zeros((1, D_MODEL), F32) if extra is None else extra
    meta = jnp.zeros((N_META, D_MODEL), F32) if meta is None else meta
    return jnp.concatenate([g1, g2, g3, g4, misc, last, jnp.zeros((2, D_MODEL), F32), meta], axis=0)


def _unpack_small(p):
    nrb = N_BUCKETS * SWA_Q_HEADS
    misc = p[4]
    return dict(rel_bias=misc[:nrb].reshape(N_BUCKETS, SWA_Q_HEADS), ln_pre_mix=p[0:1], ln_post_mix=p[1:2],
                ln_pre_ffn=p[2:3], ln_post_ffn=p[3:4], b_forget=misc[nrb:nrb + 8].reshape(1, 8),
                sinks=misc[nrb + 8:nrb + 16].reshape(1, 8))


def _proj_runs():
    gw = FOX_GROUP * HEAD_DIM
    swa = SWA_Q_W + 2 * SWA_KV_HEADS * HEAD_DIM
    runs = [(0, swa)]
    for grp in range(FOX_HEADS // FOX_GROUP):
        runs += [(swa + part * FOX_W + grp * gw, swa + part * FOX_W + (grp + 1) * gw) for part in range(3)]
    return runs


def _device_shards(qkv, gate, shard, padded):
    pos, segments = 0, []
    for start, stop in _proj_runs():
        segments.append((start, stop, qkv, pos))
        pos += stop - start
    segments.append((pos, pos + gate.shape[1], gate, 0))
    total = pos + gate.shape[1]
    assert total % shard == 0
    zeros = jnp.zeros((qkv.shape[0], padded - shard), qkv.dtype)
    out = []
    for d in range(total // shard):
        lo, hi = d * shard, (d + 1) * shard
        pieces = [arr[:, src + max(lo, s) - s:src + min(hi, e) - s]
                  for s, e, arr, src in sorted(segments, key=lambda seg: seg[0]) if max(lo, s) < min(hi, e)]
        out.append(jnp.concatenate(pieces + [zeros], axis=1))
    return jnp.stack(out)


def kernel(x, meta_tokens, rel_bias, ln_pre_mix, ln_post_mix, ln_pre_ffn, ln_post_ffn, w_in, b_forget, sinks, w_out, w_gate_up, w_down, loss_target, m_meta_tokens, m_rel_bias, m_ln_pre_mix, m_ln_post_mix, m_ln_pre_ffn, m_ln_post_ffn, m_w_in, m_b_forget, m_sinks, m_w_out, m_w_gate_up, m_w_down, v_meta_tokens, v_rel_bias, v_ln_pre_mix, v_ln_post_mix, v_ln_pre_ffn, v_ln_post_ffn, v_w_in, v_b_forget, v_sinks, v_w_out, v_w_gate_up, v_w_down):
    seq = x.shape[1]
    T = BLOCK + seq
    assert T % FOX_TILE == 0
    nq = T // FOX_TILE
    tm = _tile(T, 1056)
    cin = w_in.shape[2]
    hid = w_down.shape[1]
    assert w_gate_up.shape[2] == 2 * hid and cin <= W_IN_PAD and hid <= HID_PAD

    x_i, y_i, c_i = _coords()
    core = jnp.reshape(c_i, (1,)).astype(jnp.int32)
    chip = jnp.reshape(2 * x_i + y_i, (1,)).astype(jnp.int32)
    w_in_s = jnp.pad(w_in[0].astype(BF16), ((0, 0), (0, W_IN_PAD - cin)))
    w_gu_s = jnp.pad(w_gate_up[0].astype(BF16).reshape(D_MODEL, 2, hid), ((0, 0), (0, 0), (0, HID_PAD - hid)))
    w_gu_s = w_gu_s.reshape(D_MODEL, 2 * HID_PAD)
    w_down_s = jnp.pad(w_down[0].astype(BF16), ((0, HID_PAD - hid), (0, 0)))
    g_in, g_meta = _run_exchange(_gather_exchange([w_in_s, meta_tokens]), name="ag_w_in")
    gather_rest = _gather_exchange([w_out[0].astype(BF16), w_gu_s, w_down_s])
    w_in_full = g_in[:, :, :cin].transpose(1, 0, 2).reshape(D_MODEL, N_DEV * cin)
    w_qkv = jnp.concatenate([w_in_full[:, a:b] for a, b in _proj_runs()], axis=1)
    w_f = jnp.pad(w_in_full[:, D_QKV:], ((0, 0), (0, BLOCK - FOX_HEADS)))
    meta_full = g_meta.transpose(1, 0, 2).reshape(N_META, D_MODEL)

    h0 = jnp.concatenate([jnp.zeros((PAD_ROWS, D_MODEL), F32), meta_full, x[0]], axis=0)
    target = jnp.concatenate([jnp.zeros((BLOCK, D_MODEL), F32), loss_target[0]], axis=0)
    hn1, hn1_t = _rms_fwd(h0, ln_pre_mix, name="rms_pre_mix")
    proj = _matmul(hn1, w_qkv, out_dtype=BF16, tm=tm, tn=D_QKV, name="mm_in_proj")
    proj_f = _matmul(hn1, w_f, out_dtype=F32, tm=tm, tn=BLOCK, name="mm_in_proj_f")

    f_t = proj_f[:, :FOX_HEADS].T
    bf_col = b_forget.reshape(FOX_HEADS, 1)

    oh_cur, oh_prev = _bucket_onehots()
    bias_c = jnp.einsum("pb,bh->hp", jnp.asarray(oh_cur), rel_bias, precision=HIGHEST).reshape(8, BLOCK, BLOCK)
    bias_p = jnp.einsum("pb,bh->hp", jnp.asarray(oh_prev), rel_bias, precision=HIGHEST).reshape(8, BLOCK, BLOCK)
    far = rel_bias[N_BUCKETS - 1]
    sink_v = sinks[0]
    mix_a = _swa_fwd(proj, bias_c, bias_p, far, sink_v, name="swa_fwd")

    _, cum_col = _fox_gates_fwd(f_t, bf_col, name="fox_gates_fwd")
    q_b, k_b, v_b = _fox_prep(proj, cum_col, name="fox_prep")
    mix, lse_row, g_out, g_gu, g_down = _fox_fwd(q_b, k_b, v_b, mix_a, ex=gather_rest, name="fox_fwd")
    w_out_full = g_out.reshape(D_MODEL, D_MODEL)
    w_down_full = g_down.reshape(N_DEV * HID_PAD, D_MODEL)

    a1 = _matmul(mix, w_out_full, out_dtype=F32, tm=tm, tn=512, name="mm_out_proj")
    h1, hn2, hn2_t = _post_res_norm(a1, ln_post_mix, h0, ln_pre_ffn, name="post_mix_pre_ffn")
    gate, up, act, act_t = _gate_up_swiglu(hn2, g_gu, name="mm_gate_up")
    ff = _matmul(act, w_down_full, out_dtype=F32, tm=tm, tn=512, name="mm_down")
    dh2, dff, dg_post_ffn, loss_acc = _loss_head(ff, ln_post_ffn, h1, target, name="loss_head")

    dgu = _d_act_swiglu(dff, w_down_full, gate, up, name="mm_d_act")
    d_w_down = _matmul(act_t, dff, out_dtype=F32, tm=768, tn=512, name="mm_dw_down")
    dhn2 = _matmul(dgu, g_gu, nt=True, b_shards=True, out_dtype=F32, tm=tm, tn=512, name="mm_d_hn2")
    d_w_gu = _matmul(hn2_t, dgu, out_shards=True, out_dtype=F32, tm=512, tn=2 * HID_PAD, name="mm_dw_gate_up")
    dh1, dg_pre_ffn, da1, dg_post_mix = _rms_bwd(h1, ln_pre_ffn, dhn2, dh2, out_dtype=F32,
                                                 then=(a1, ln_post_mix), name="rms_bwd_pre_ffn_post_mix")
    dmix = _matmul(da1, w_out_full, nt=True, out_dtype=BF16, tm=tm, tn=512, name="mm_d_mix")
    d_w_out = _matmul(mix, da1, ta=True, out_dtype=F32, tm=512, tn=512, name="mm_dw_out")

    ffn_grads = [d_w_out.reshape(N_DEV, -1, D_MODEL), d_w_gu, d_w_down.reshape(N_DEV, HID_PAD, D_MODEL)]
    dproj_a, dbc, dbp, dbf, dsk, *ffn_sibling = _swa_bwd(
        proj, dmix, bias_c, bias_p, far, sink_v, ex=_cores_exchange(ffn_grads), name="swa_bwd")
    d_tab, d_sink = _small_grads(dbc, dbp, dbf, dsk, jnp.asarray(oh_cur), jnp.asarray(oh_prev), name="small_grads")
    ffn_sums = [_add_cores(g, r, core, name="rs_add_" + t)
                for g, r, t in zip(ffn_grads, ffn_sibling, ["w_out", "w_gate_up", "w_down"])]

    do_b = _fox_prep_bwd(dmix, mix, name="fox_prep_bwd")
    dproj, dcq, dck, *ffn_chips = _fox_bwd(
        q_b, k_b, v_b, do_b, lse_row, dproj_a, ex=_chips_exchange([s[1] for s in ffn_sums]), name="fox_bwd")
    df_t, d_bf = _fox_gates_bwd(dcq.reshape(FOX_HEADS, T), dck.reshape(FOX_HEADS, T), f_t, bf_col,
                                name="fox_gates_bwd")
    df = jnp.pad(df_t.T.astype(BF16), ((0, 0), (0, BLOCK - FOX_HEADS)))

    d_w_qkv = _matmul(hn1_t, dproj, out_dtype=F32, tm=512, tn=768, name="mm_dw_in")
    d_w_f = _matmul(hn1_t, df, out_dtype=F32, tm=512, tn=BLOCK, name="mm_dw_in_f")
    d_w_in = _device_shards(d_w_qkv, d_w_f[:, :FOX_HEADS], cin, W_IN_PAD)
    dhn1, in_sibling = _matmul(dproj, w_qkv, nt=True, out_dtype=F32, tm=tm, tn=512,
                               ex=_cores_exchange([d_w_in]), name="mm_d_hn1")
    in_sum = _add_cores(d_w_in, in_sibling, core, name="rs_add_w_in")
    dh0, dg_pre_mix, in_chips = _rms_bwd(h0, ln_pre_mix, dhn1, dh1, out_dtype=F32, dy2=(df, w_f),
                                         ex=_chips_exchange([in_sum[1]]), name="rms_bwd_pre_mix")
    grad_x = dh0[BLOCK:][None]
    d_meta = dh0[PAD_ROWS:BLOCK]

    tags = ["w_in", "w_out", "w_gate_up", "w_down"]
    chip_sum = [in_sum[0]] + [s[0] for s in ffn_sums]
    from_chips = [in_chips] + list(ffn_chips)
    shard_w = [(w_in, m_w_in, v_w_in), (w_out, m_w_out, v_w_out), (w_gate_up, m_w_gate_up, v_w_gate_up),
               (w_down, m_w_down, v_w_down)]
    segs = [[(0, 0, cin)], [(0, 0, D_MODEL)], [(0, 0, hid), (HID_PAD, hid, hid)], [(0, 0, D_MODEL)]]
    tas = [256, BLOCK, 256, hid]
    big = [{}, {}, {}, {}]
    for i, t in enumerate(tags):
        w_t, m_t, v_t = shard_w[i]
        res = _sum_adamw(chip_sum[i], from_chips[i], chip, w_t[0], m_t[0], v_t[0], segs=segs[i], ta=tas[i],
                         name="rs_adamw_" + t)
        for kind in range(4):
            big[kind][t] = res[kind][None]

    loss_row = jnp.pad(loss_acc[0:1, 0:1] * (0.5 / D_MODEL), ((0, 0), (0, D_MODEL - 1)))
    s_small = _pack_small(d_tab.T, dg_pre_mix, dg_post_mix, dg_pre_ffn, dg_post_ffn, d_bf, d_sink,
                          extra=loss_row, meta=d_meta)
    w_s = _pack_small(rel_bias, ln_pre_mix, ln_post_mix, ln_pre_ffn, ln_post_ffn, b_forget, sinks)
    m_s = _pack_small(m_rel_bias, m_ln_pre_mix, m_ln_post_mix, m_ln_pre_ffn, m_ln_post_ffn, m_b_forget, m_sinks)
    v_s = _pack_small(v_rel_bias, v_ln_pre_mix, v_ln_post_mix, v_ln_pre_ffn, v_ln_post_ffn, v_b_forget, v_sinks)
    small = _small_allreduce_adamw(s_small, w_s, m_s, v_s, name="small_allreduce_adamw")
    loss = small[0][5, 0]
    mcols = meta_tokens.shape[1]
    g_meta_mine = lax.dynamic_slice(small[0][8:8 + N_META], (0, (4 * x_i + 2 * y_i + c_i) * mcols), (N_META, mcols))
    big[0]["meta_tokens"] = g_meta_mine
    for kind, arr in enumerate(_adamw(meta_tokens, g_meta_mine, m_meta_tokens, v_meta_tokens, name="adamw_meta")):
        big[kind + 1]["meta_tokens"] = arr
    small = [_unpack_small(p) for p in small]

    names = ["meta_tokens", "rel_bias", "ln_pre_mix", "ln_post_mix", "ln_pre_ffn", "ln_post_ffn", "w_in",
             "b_forget", "sinks", "w_out", "w_gate_up", "w_down"]
    outs = [loss, grad_x]
    for kind in range(4):
        for nme in names:
            outs.append(big[kind][nme] if nme in big[kind] else small[kind][nme])
    return tuple(outs)
```

```python
import math

import numpy as np
import jax
import jax.numpy as jnp
from jax import lax
from jax.experimental import pallas as pl
from jax.experimental.pallas import tpu as pltpu

F32 = jnp.float32
BF16 = jnp.bfloat16
HIGHEST = lax.Precision.HIGHEST
MESH = pl.DeviceIdType.MESH

N_DEV = 8
D_MODEL = 1024
N_META = 16
HEAD_DIM = 64
SWA_Q_HEADS = 8
SWA_KV_HEADS = 2
SWA_GROUP = 4
FOX_HEADS = 8
FOX_W = FOX_HEADS * HEAD_DIM
SWA_Q_W = SWA_Q_HEADS * HEAD_DIM
BLOCK = 128
PAD_ROWS = BLOCK - N_META
N_BUCKETS = 32
MAX_DISTANCE = 128
D_FF = 2816
D_QKV = 2304
D_PROJ = D_QKV + FOX_HEADS
D_PROJ_PAD = 2560
EPS = 1e-6
NEG = -1e30
SCALE = HEAD_DIM ** -0.5
ADAM_LR, ADAM_B1, ADAM_B2, ADAM_EPS, ADAM_WD, ADAM_STEP = 0.001, 0.9, 0.999, 1e-08, 0.01, 10
VMEM_LIMIT = 56 * 1024 * 1024
FOX_TILE = 384
FOX_GROUP = 4
W_IN_PAD = 384
HID_PAD = 384

NT = (((1,), (1,)), ((), ()))
NN = (((1,), (0,)), ((), ()))
TN = (((0,), (0,)), ((), ()))


def _params(sem=None, **kw):
    if sem is not None:
        kw["dimension_semantics"] = sem
    return pltpu.CompilerParams(vmem_limit_bytes=VMEM_LIMIT, **kw)


def _tile(n, target, mult=16):
    best = None
    for t in range(mult, min(n, target) + 1, mult):
        if n % t == 0:
            best = t
    assert best is not None, (n, target)
    return best


def _matmul(a, b, *, nt=False, ta=False, b_shards=False, out_shards=False, out_dtype, tm, tn=None, tk=None,
            ex=None, name):
    M, K = a.shape[::-1] if ta else a.shape
    assert not (ta and (nt or b_shards))
    k_shards = b.shape[0] if (b_shards and nt) else 0
    if k_shards:
        N, ks = b.shape[1], b.shape[2]
        assert tk is None and K == k_shards * ks
    elif b_shards:
        N, tn = b.shape[0] * b.shape[2], b.shape[2]
    else:
        N = b.shape[0] if nt else b.shape[1]
    tk = K if tk is None else tk
    assert M % tm == 0 and N % tn == 0 and K % tk == 0, (name, a.shape, b.shape, tm, tn, tk)
    nk = K // tk
    dn = NT if nt else (TN if ta else NN)
    a_spec = pl.BlockSpec((tk, tm), lambda i, j, k: (k, i)) if ta else pl.BlockSpec((tm, tk), lambda i, j, k: (i, k))

    def body(a_ref, b_ref, o_ref, *scr):
        if k_shards:
            part = sum(lax.dot_general(a_ref[:, s * ks:(s + 1) * ks], b_ref[s], NT, preferred_element_type=F32)
                       for s in range(k_shards))
        else:
            part = lax.dot_general(a_ref[...], b_ref[...], dn, preferred_element_type=F32)
        if nk == 1:
            o_ref[...] = part.astype(o_ref.dtype)
        else:
            acc = scr[0]
            k = pl.program_id(2)

            @pl.when(k == 0)
            def _():
                acc[...] = part

            @pl.when(k > 0)
            def _():
                acc[...] += part

            @pl.when(k == nk - 1)
            def _():
                o_ref[...] = acc[...].astype(o_ref.dtype)

    if k_shards:
        b_spec = pl.BlockSpec((k_shards, tn, ks), lambda i, j, k: (0, j, 0))
    elif b_shards:
        b_spec = pl.BlockSpec((None, tk, tn), lambda i, j, k: (j, k, 0))
    elif nt:
        b_spec = pl.BlockSpec((tn, tk), lambda i, j, k: (j, k))
    else:
        b_spec = pl.BlockSpec((tk, tn), lambda i, j, k: (k, j))
    if out_shards:
        out_shape = jax.ShapeDtypeStruct((N // tn, M, tn), out_dtype)
        out_spec = pl.BlockSpec((None, tm, tn), lambda i, j, k: (j, i, 0))
    else:
        out_shape = jax.ShapeDtypeStruct((M, N), out_dtype)
        out_spec = pl.BlockSpec((tm, tn), lambda i, j, k: (i, j))
    grid = (M // tm, N // tn, nk)
    body, x_in, x_in_specs, x_out, x_out_specs, x_scr = _carry(ex, grid, 2, 1, body)
    res = pl.pallas_call(
        body,
        out_shape=(out_shape, *x_out),
        grid=grid,
        in_specs=[a_spec, b_spec] + x_in_specs,
        out_specs=(out_spec, *x_out_specs),
        scratch_shapes=([pltpu.VMEM((tm, tn), F32)] if nk > 1 else []) + x_scr,
        compiler_params=_params(("parallel", "parallel", "arbitrary") if ex is None else ("arbitrary",) * 3),
        name=name,
    )(a, b, *x_in)
    return res[0] if ex is None else res


def _rstd(x):
    return lax.rsqrt(jnp.mean(x * x, axis=-1, keepdims=True) + EPS)


def _rms_fwd(x, g, *, name):
    T, D = x.shape
    tm = _tile(T, 512)

    def body(x_ref, g_ref, o_ref, ot_ref):
        x = x_ref[...]
        y = x * _rstd(x) * g_ref[...]
        o_ref[...] = y.astype(o_ref.dtype)
        ot_ref[...] = y.T.astype(ot_ref.dtype)

    return pl.pallas_call(
        body, out_shape=(jax.ShapeDtypeStruct((T, D), BF16), jax.ShapeDtypeStruct((D, T), BF16)), grid=(T // tm,),
        in_specs=[pl.BlockSpec((tm, D), lambda i: (i, 0)), pl.BlockSpec((1, D), lambda i: (0, 0))],
        out_specs=(pl.BlockSpec((tm, D), lambda i: (i, 0)), pl.BlockSpec((D, tm), lambda i: (0, i))),
        compiler_params=_params(("parallel",)), name=name)(x, g)


def _post_res_norm(a, g_post, h, g_pre, *, name):
    T, D = a.shape
    tm = _tile(T, 384, BLOCK)

    def body(a_ref, gp_ref, h_ref, gn_ref, h1_ref, o_ref, ot_ref):
        a = a_ref[...]
        h1 = h_ref[...] + a * _rstd(a) * gp_ref[...]
        h1_ref[...] = h1
        y = h1 * _rstd(h1) * gn_ref[...]
        o_ref[...] = y.astype(o_ref.dtype)
        ot_ref[...] = y.T.astype(ot_ref.dtype)

    row = pl.BlockSpec((tm, D), lambda i: (i, 0))
    vec = pl.BlockSpec((1, D), lambda i: (0, 0))
    return pl.pallas_call(
        body, out_shape=(jax.ShapeDtypeStruct((T, D), F32), jax.ShapeDtypeStruct((T, D), BF16),
                         jax.ShapeDtypeStruct((D, T), BF16)), grid=(T // tm,),
        in_specs=[row, vec, row, vec], out_specs=(row, row, pl.BlockSpec((D, tm), lambda i: (0, i))),
        compiler_params=_params(("parallel",)), name=name)(a, g_post, h, g_pre)


def _loss_head(a, g, h, target, *, name):
    T, D = a.shape
    tm = _tile(T, 512)

    def body(a_ref, g_ref, h_ref, t_ref, dy_ref, da_ref, dg_ref, loss_ref):
        i = pl.program_id(0)
        a = a_ref[...]
        r = _rstd(a)
        ah = a * r
        y = h_ref[...] + ah * g_ref[...]
        rows = i * tm + lax.broadcasted_iota(jnp.int32, (tm, 1), 0)
        err = jnp.where(rows >= BLOCK, y - t_ref[...], 0.0)
        dy = err / D
        dy_ref[...] = dy
        dah = dy * g_ref[...]
        da_ref[...] = (r * (dah - ah * jnp.mean(dah * ah, axis=-1, keepdims=True))).astype(da_ref.dtype)
        part = jnp.sum(jnp.sum(err * err, axis=1, keepdims=True), axis=0, keepdims=True)

        @pl.when(i == 0)
        def _():
            loss_ref[...] = jnp.zeros_like(loss_ref)
            dg_ref[...] = jnp.zeros_like(dg_ref)

        loss_ref[...] += jnp.broadcast_to(part, loss_ref.shape)
        dg_ref[...] += jnp.sum(dy * ah, axis=0, keepdims=True)

    row = pl.BlockSpec((tm, D), lambda i: (i, 0))
    vec = pl.BlockSpec((1, D), lambda i: (0, 0))
    return pl.pallas_call(
        body, out_shape=(jax.ShapeDtypeStruct((T, D), F32), jax.ShapeDtypeStruct((T, D), BF16),
                         jax.ShapeDtypeStruct((1, D), F32), jax.ShapeDtypeStruct((8, 128), F32)),
        grid=(T // tm,),
        in_specs=[row, vec, row, row],
        out_specs=(row, row, vec, pl.BlockSpec((8, 128), lambda i: (0, 0))),
        compiler_params=_params(("arbitrary",)), name=name)(a, g, h, target)


def _rms_bwd(x, g, dy, res, *, out_dtype, dy2=None, then=None, ex=None, name):
    T, D = x.shape
    tm = _tile(T, 512)
    has_res = res is not None
    has_dy2 = 2 if dy2 is not None else 0
    n_in = 3 + has_dy2 + has_res + (2 if then is not None else 0)
    n_out = 2 + (2 if then is not None else 0)

    def pull_back(x, g, dy):
        r = _rstd(x)
        xh = x * r
        dxh = dy * g
        return r * (dxh - xh * jnp.mean(dxh * xh, axis=-1, keepdims=True)), jnp.sum(dy * xh, axis=0, keepdims=True)

    def body(*refs):
        ins, outs = refs[:n_in], refs[n_in:]
        i = pl.program_id(0)

        @pl.when(i == 0)
        def _():
            for ref in outs[1::2]:
                ref[...] = jnp.zeros_like(ref)

        dy_all = ins[2][...].astype(F32)
        if has_dy2:
            dy_all = dy_all + lax.dot_general(ins[3][...], ins[4][...], NT, preferred_element_type=F32)
        dx, dg = pull_back(ins[0][...], ins[1][...], dy_all)
        if has_res:
            dx = dx + ins[3 + has_dy2][...]
        outs[0][...] = dx.astype(outs[0].dtype)
        outs[1][...] += dg
        if then is not None:
            dx2, dg2 = pull_back(ins[n_in - 2][...], ins[n_in - 1][...], dx)
            outs[2][...] = dx2.astype(outs[2].dtype)
            outs[3][...] += dg2

    row = pl.BlockSpec((tm, D), lambda i: (i, 0))
    vec = pl.BlockSpec((1, D), lambda i: (0, 0))
    ins = [x, g, dy] + (list(dy2) if has_dy2 else []) + ([res] if has_res else []) + (list(then) if then is not None else [])
    dy2_specs = ([pl.BlockSpec((tm, dy2[0].shape[1]), lambda i: (i, 0)), pl.BlockSpec(dy2[1].shape, lambda i: (0, 0))]
                 if has_dy2 else [])
    in_specs = [row, vec, row] + dy2_specs + ([row] if has_res else []) + ([row, vec] if then is not None else [])
    out_shape = [jax.ShapeDtypeStruct((T, D), out_dtype), jax.ShapeDtypeStruct((1, D), F32)]
    out_specs = [row, vec]
    if then is not None:
        out_shape += [jax.ShapeDtypeStruct((T, D), BF16), jax.ShapeDtypeStruct((1, D), F32)]
        out_specs += [row, vec]
    grid = (T // tm,)
    body, x_in, x_in_specs, x_out, x_out_specs, x_scr = _carry(ex, grid, n_in, n_out, body)
    return pl.pallas_call(
        body, out_shape=(*out_shape, *x_out), grid=grid,
        in_specs=in_specs + x_in_specs, out_specs=(*out_specs, *x_out_specs), scratch_shapes=x_scr,
        compiler_params=_params(("arbitrary",)), name=name)(*ins, *x_in)


def _gate_up_swiglu(a, w, *, name):
    T, D = a.shape
    S, n = w.shape[0] // 2, w.shape[2]
    tm = _tile(T, 1408, BLOCK)

    def body(a_ref, wg_ref, wu_ref, g_ref, u_ref, o_ref, ot_ref):
        x = a_ref[...]
        g = jnp.dot(x, wg_ref[...], preferred_element_type=F32)
        u = jnp.dot(x, wu_ref[...], preferred_element_type=F32)
        g16, u16 = g.astype(BF16), u.astype(BF16)
        g_ref[...] = g16
        u_ref[...] = u16
        gr = g16.astype(F32)
        act = gr / (1.0 + jnp.exp(-gr)) * u16.astype(F32)
        o_ref[...] = act.astype(o_ref.dtype)
        ot_ref[...] = act.T.astype(ot_ref.dtype)

    tile = pl.BlockSpec((tm, n), lambda i, j: (i, j))
    shp = jax.ShapeDtypeStruct((T, S * n), BF16)
    return pl.pallas_call(
        body, out_shape=(shp, shp, shp, jax.ShapeDtypeStruct((S * n, T), BF16)), grid=(T // tm, S),
        in_specs=[pl.BlockSpec((tm, D), lambda i, j: (i, 0)),
                  pl.BlockSpec((None, D, n), lambda i, j: (j, 0, 0)),
                  pl.BlockSpec((None, D, n), lambda i, j: (j + S, 0, 0))],
        out_specs=(tile, tile, tile, pl.BlockSpec((n, tm), lambda i, j: (j, i))),
        compiler_params=_params(("parallel", "parallel")), name=name)(a, w, w)


def _d_act_swiglu(dff, w_down, gate, up, *, name):
    T, D = dff.shape
    F = w_down.shape[0]
    tm = _tile(T, 384)
    tf = _tile(F, 768, BLOCK)

    def body(d_ref, w_ref, g_ref, u_ref, o_ref):
        dy = d_ref[...]
        for c in range(0, F, tf):
            d = lax.dot_general(dy, w_ref[c:c + tf, :], NT, preferred_element_type=F32)
            g = g_ref[:, c:c + tf].astype(F32)
            u = u_ref[:, c:c + tf].astype(F32)
            sg = 1.0 / (1.0 + jnp.exp(-g))
            o_ref[:, c:c + tf] = (d * u * (sg * (1.0 + g * (1.0 - sg)))).astype(o_ref.dtype)
            o_ref[:, F + c:F + c + tf] = (d * (g * sg)).astype(o_ref.dtype)

    row = pl.BlockSpec((tm, F), lambda i: (i, 0))
    return pl.pallas_call(
        body, out_shape=jax.ShapeDtypeStruct((T, 2 * F), BF16), grid=(T // tm,),
        in_specs=[pl.BlockSpec((tm, D), lambda i: (i, 0)), pl.BlockSpec((F, D), lambda i: (0, 0)), row, row],
        out_specs=pl.BlockSpec((tm, 2 * F), lambda i: (i, 0)),
        compiler_params=_params(("parallel",)), name=name)(dff, w_down, gate, up)


def _fox_gates_fwd(f_t, b, *, name):
    H, T = f_t.shape
    nb = T // BLOCK

    def body(f_ref, b_ref, cum_ref, col_ref):
        f = f_ref[...] + b_ref[...]
        ls = jnp.minimum(f, 0.0) - jnp.log(1.0 + jnp.exp(-jnp.abs(f)))
        t = lax.broadcasted_iota(jnp.int32, (H, T), 1)
        ls = jnp.where(t >= PAD_ROWS, ls, 0.0)
        upper = (lax.broadcasted_iota(jnp.int32, (BLOCK, BLOCK), 0)
                 <= lax.broadcasted_iota(jnp.int32, (BLOCK, BLOCK), 1)).astype(F32)
        carry = jnp.zeros((H, 1), F32)
        for blk in range(nb):
            seg = ls[:, blk * BLOCK:(blk + 1) * BLOCK]
            pre = jnp.dot(seg, upper, precision=HIGHEST, preferred_element_type=F32) + carry
            cum_ref[:, blk * BLOCK:(blk + 1) * BLOCK] = pre
            col_ref[blk * BLOCK:(blk + 1) * BLOCK, :] = jnp.concatenate(
                [pre, jnp.zeros((BLOCK - H, BLOCK), F32)], axis=0).T
            carry = pre[:, BLOCK - 1:BLOCK]

    vm = pl.BlockSpec(memory_space=pltpu.VMEM)
    return pl.pallas_call(
        body, out_shape=(jax.ShapeDtypeStruct((H, T), F32), jax.ShapeDtypeStruct((T, BLOCK), F32)),
        in_specs=[vm, vm], out_specs=(vm, vm),
        compiler_params=_params(), name=name)(f_t, b)


def _fox_gates_bwd(dcq, dck, f_t, b, *, name):
    H, T = f_t.shape
    nb = T // BLOCK

    def body(dq_ref, d_ref, f_ref, b_ref, df_ref, db_ref):
        lower = (lax.broadcasted_iota(jnp.int32, (BLOCK, BLOCK), 0)
                 >= lax.broadcasted_iota(jnp.int32, (BLOCK, BLOCK), 1)).astype(F32)
        carry = jnp.zeros((H, 1), F32)
        for blk in range(nb - 1, -1, -1):
            seg = dq_ref[:, blk * BLOCK:(blk + 1) * BLOCK] - d_ref[:, blk * BLOCK:(blk + 1) * BLOCK]
            suf = jnp.dot(seg, lower, precision=HIGHEST, preferred_element_type=F32) + carry
            df_ref[:, blk * BLOCK:(blk + 1) * BLOCK] = suf
            carry = suf[:, 0:1]
        f = f_ref[...] + b_ref[...]
        t = lax.broadcasted_iota(jnp.int32, (H, T), 1)
        df = jnp.where(t >= PAD_ROWS, df_ref[...] / (1.0 + jnp.exp(f)), 0.0)
        df_ref[...] = df
        db_ref[...] = jnp.sum(df, axis=1, keepdims=True)

    vm = pl.BlockSpec(memory_space=pltpu.VMEM)
    return pl.pallas_call(
        body, out_shape=(jax.ShapeDtypeStruct((H, T), F32), jax.ShapeDtypeStruct((H, 1), F32)),
        in_specs=[vm, vm, vm, vm], out_specs=(vm, vm),
        compiler_params=_params(), name=name)(dcq, dck, f_t, b)


LANE_KC = HEAD_DIM
LANE_QC = HEAD_DIM + 3
LANE_END = HEAD_DIM + 6


def _split3(c):
    hi = c.astype(BF16).astype(F32)
    r = c - hi
    mid = r.astype(BF16).astype(F32)
    lo = (r - mid).astype(BF16).astype(F32)
    return hi, mid, lo


def _lanes(lane, data, start, terms, rest):
    out = rest
    for i, t in enumerate(terms):
        out = jnp.where(lane == start + i, t, out)
    return jnp.where(lane < HEAD_DIM, data, out)


def _fox_prep(proj, cum_col, *, name):
    T = proj.shape[0]
    tm = FOX_TILE
    nt = T // tm
    H = FOX_HEADS
    lanes = 2 * HEAD_DIM
    first = (proj.shape[1] - 3 * H * HEAD_DIM) // lanes

    def body(q_ref, k_ref, v_ref, c_ref, qa_ref, ka_ref, va_ref):
        p = pl.program_id(0)
        i = pl.program_id(1)
        lane = lax.broadcasted_iota(jnp.int32, (tm, lanes), 1)
        rows = i * tm + lax.broadcasted_iota(jnp.int32, (tm, 1), 0)
        q2 = q_ref[...].astype(F32)
        k2 = k_ref[...].astype(F32)
        v2 = v_ref[...].astype(F32)
        cum = c_ref[...]
        for e in range(2):
            c = jnp.sum(jnp.where(lane == 2 * p + e, cum, 0.0), axis=1, keepdims=True)
            ck = jnp.where(rows >= PAD_ROWS, c, -NEG)
            qe, ke, ve = (q2, k2, v2) if e == 0 else tuple(pltpu.roll(a, HEAD_DIM, 1) for a in (q2, k2, v2))
            one = jnp.where(lane < LANE_END, 1.0, 0.0)
            qa = _lanes(lane, qe * SCALE, LANE_QC, _split3(c), jnp.where(lane < LANE_QC, -1.0, 0.0))
            ka = _lanes(lane, ke, LANE_KC, _split3(ck), one)
            va = jnp.where(lane < HEAD_DIM, ve, jnp.where(lane < LANE_QC, 1.0, 0.0))
            qa_ref[e] = qa.astype(BF16)
            ka_ref[e] = ka.astype(BF16)
            va_ref[e] = va.astype(BF16)

    pairs = FOX_GROUP // 2

    def col(part):
        return pl.BlockSpec((tm, lanes),
                            lambda p, i: (i, first + 3 * pairs * (p // pairs) + part * pairs + p % pairs))

    out = pl.BlockSpec((2, tm, lanes), lambda p, i: (p, i, 0))
    shp = jax.ShapeDtypeStruct((H, T, lanes), BF16)
    return pl.pallas_call(
        body, out_shape=(shp, shp, shp), grid=(H // 2, nt),
        in_specs=[col(0), col(1), col(2), pl.BlockSpec((tm, lanes), lambda p, i: (i, 0))],
        out_specs=(out, out, out),
        compiler_params=_params(("parallel", "parallel")), name=name)(proj, proj, proj, cum_col)


def _fox_fwd(q_aug, k_aug, v_aug, mix, *, ex=None, name):
    H, T, lanes = q_aug.shape
    tq = FOX_TILE
    nq = T // tq
    G = FOX_HEADS

    def body(q_ref, k_ref, v_ref, mix_ref, o_ref, lse_ref, m_scr, acc_scr):
        i = pl.program_id(1)
        m_scr[...] = jnp.full(m_scr.shape, NEG, F32)
        acc_scr[...] = jnp.zeros(acc_scr.shape, F32)

        def step(kb, diag):
            off = pl.multiple_of(kb * tq, tq)
            s_t = [lax.dot_general(k_ref[g, pl.ds(off, tq), :], q_ref[g], NT, preferred_element_type=F32)
                   for g in range(G)]
            if diag:
                r = lax.broadcasted_iota(jnp.int32, (tq, tq), 0)
                c = lax.broadcasted_iota(jnp.int32, (tq, tq), 1)
                s_t = [jnp.where(c >= r, s, NEG) for s in s_t]
            m_prev = [m_scr[g] for g in range(G)]
            m_new = [jnp.maximum(m_prev[g], jnp.max(s_t[g], axis=0, keepdims=True)) for g in range(G)]
            p_t = [jnp.exp(s_t[g] - m_new[g]).astype(BF16) for g in range(G)]
            pv = [lax.dot_general(v_ref[g, pl.ds(off, tq), :], p_t[g], TN, preferred_element_type=F32)
                  for g in range(G)]
            for g in range(G):
                acc_scr[g] = jnp.exp(m_prev[g] - m_new[g]) * acc_scr[g] + pv[g]
                m_scr[g] = m_new[g]

        def loop_body(kb, carry):
            step(kb, False)
            return carry

        lax.fori_loop(0, i, loop_body, 0)
        step(i, True)
        lane = lax.broadcasted_iota(jnp.int32, (tq, lanes), 1)
        outs = []
        for g in range(G):
            acc = acc_scr[g]
            lse_ref[g] = m_scr[g] + jnp.log(acc[HEAD_DIM:HEAD_DIM + 1, :])
            acc_t = acc.T
            outs.append(acc_t / acc_t[:, HEAD_DIM:HEAD_DIM + 1])
        for pair in range(G // 2):
            o_ref[:, pair * lanes:(pair + 1) * lanes] = jnp.where(
                lane < HEAD_DIM, outs[2 * pair], pltpu.roll(outs[2 * pair + 1], HEAD_DIM, 1)).astype(o_ref.dtype)

    blk = pl.BlockSpec((G, tq, lanes), lambda h, i: (h, i, 0))
    full = pl.BlockSpec((G, T, lanes), lambda h, i: (h, 0, 0))
    grid = (H // G, nq)
    first = mix.shape[1] // (G * HEAD_DIM) - H // G
    body, x_in, x_in_specs, x_out, x_out_specs, x_scr = _carry(ex, grid, 4, 2, body)
    return pl.pallas_call(
        body,
        out_shape=(jax.ShapeDtypeStruct(mix.shape, mix.dtype), jax.ShapeDtypeStruct((H, nq, 1, tq), F32), *x_out),
        grid=grid,
        in_specs=[blk, full, full, pl.BlockSpec(memory_space=pl.ANY)] + x_in_specs,
        out_specs=(pl.BlockSpec((tq, G * HEAD_DIM), lambda h, i: (i, first + h)),
                   pl.BlockSpec((G, None, 1, tq), lambda h, i: (h, i, 0, 0)), *x_out_specs),
        input_output_aliases={3: 0},
        scratch_shapes=[pltpu.VMEM((G, 1, tq), F32), pltpu.VMEM((G, lanes, tq), F32)] + x_scr,
        compiler_params=_params(("arbitrary", "arbitrary")), name=name)(q_aug, k_aug, v_aug, mix, *x_in)


def _fox_prep_bwd(dmix, mix, *, name):
    T = dmix.shape[0]
    H = FOX_HEADS
    tm = FOX_TILE
    lanes = 2 * HEAD_DIM
    first = mix.shape[1] // lanes - H // 2

    def body(d_ref, o_ref, da_ref):
        lane = lax.broadcasted_iota(jnp.int32, (tm, lanes), 1)
        d2 = d_ref[...].astype(F32)
        prod = d2 * o_ref[...].astype(F32)
        for e in range(2):
            de = d2 if e == 0 else pltpu.roll(d2, HEAD_DIM, 1)
            delta = jnp.sum(jnp.where(lane // HEAD_DIM == e, prod, 0.0), axis=1, keepdims=True)
            da_ref[e] = _lanes(lane, de, LANE_KC, _split3(-delta), jnp.zeros((), F32)).astype(BF16)

    pair = pl.BlockSpec((tm, lanes), lambda p, i: (i, first + p))
    return pl.pallas_call(
        body, out_shape=jax.ShapeDtypeStruct((H, T, lanes), BF16), grid=(H // 2, T // tm),
        in_specs=[pair, pair],
        out_specs=pl.BlockSpec((2, tm, lanes), lambda p, i: (p, i, 0)),
        compiler_params=_params(("parallel", "parallel")), name=name)(dmix, mix)


def _fox_bwd(q_aug, k_aug, v_aug, do_aug, lse_row, dproj, *, ex=None, name):
    H, T, lanes = q_aug.shape
    tq = FOX_TILE
    nq = T // tq
    G = FOX_GROUP

    def side_by_side(tiles, scale=None):
        lane = lax.broadcasted_iota(jnp.int32, tiles[0].shape, 1)
        out = [jnp.where(lane < HEAD_DIM, tiles[2 * p], pltpu.roll(tiles[2 * p + 1], HEAD_DIM, 1))
               for p in range(G // 2)]
        out = jnp.concatenate(out, axis=1)
        return out if scale is None else out * scale

    def body(q_ref, k_ref, v_ref, do_ref, lse_ref, dproj_in, out_ref, dcq_ref, dck_ref, dk_acc, dv_acc, dq_ref):
        j = pl.program_id(1)

        @pl.when(j == 0)
        def _():
            dq_ref[...] = jnp.zeros(dq_ref.shape, F32)
            dcq_ref[...] = jnp.zeros(dcq_ref.shape, F32)

        dk_acc[...] = jnp.zeros(dk_acc.shape, F32)
        dv_acc[...] = jnp.zeros(dv_acc.shape, F32)

        def step(qb, diag):
            off = pl.multiple_of(qb * tq, tq)
            heads = range(G)
            qa = [q_ref[g, pl.ds(off, tq), :] for g in heads]
            da = [do_ref[g, pl.ds(off, tq), :] for g in heads]
            s_t = [lax.dot_general(k_ref[g], qa[g], NT, preferred_element_type=F32) for g in heads]
            dp_t = [lax.dot_general(v_ref[g], da[g], NT, preferred_element_type=F32) for g in heads]
            p_t = [jnp.exp(s_t[g] - lse_ref[g, qb]) for g in heads]
            if diag:
                r = lax.broadcasted_iota(jnp.int32, (tq, tq), 0)
                c = lax.broadcasted_iota(jnp.int32, (tq, tq), 1)
                p_t = [jnp.where(c >= r, p, 0.0) for p in p_t]
            dsb = [(p_t[g] * dp_t[g]).astype(BF16) for g in heads]
            dv = [jnp.dot(p_t[g].astype(BF16), da[g], preferred_element_type=F32) for g in heads]
            dk = [jnp.dot(dsb[g], qa[g], preferred_element_type=F32) for g in heads]
            dq = [lax.dot_general(k_ref[g], dsb[g], TN, preferred_element_type=F32) for g in heads]
            for g in heads:
                dv_acc[g] += dv[g]
                dk_acc[g] += dk[g]
                dq_ref[g, qb] += dq[g]
                dcq_ref[g, qb] += jnp.sum(dsb[g].astype(F32), axis=0, keepdims=True)

        step(j, True)

        def loop_body(qb, carry):
            step(qb, False)
            return carry

        lax.fori_loop(j + 1, nq, loop_body, 0)
        dk = [dk_acc[g] for g in range(G)]
        out_ref[:, 0:wide] = side_by_side([dq_ref[g, j].T for g in range(G)], SCALE).astype(out_ref.dtype)
        out_ref[:, wide:2 * wide] = side_by_side(dk).astype(out_ref.dtype)
        out_ref[:, 2 * wide:3 * wide] = side_by_side([dv_acc[g] for g in range(G)]).astype(out_ref.dtype)
        for g in range(G):
            dck_ref[g] = -dk[g].T[LANE_KC:LANE_KC + 1, :]

    blk = pl.BlockSpec((G, tq, lanes), lambda h, j: (h, j, 0))
    full = pl.BlockSpec((G, T, lanes), lambda h, j: (h, 0, 0))
    wide = G * HEAD_DIM
    first = dproj.shape[1] // (3 * wide) - H // G
    grid = (H // G, nq)
    body, x_in, x_in_specs, x_out, x_out_specs, x_scr = _carry(ex, grid, 6, 3, body)
    rows = jax.ShapeDtypeStruct((H, nq, 1, tq), F32)
    all_rows = pl.BlockSpec((G, nq, 1, tq), lambda h, j: (h, 0, 0, 0))
    return pl.pallas_call(
        body,
        out_shape=(jax.ShapeDtypeStruct(dproj.shape, dproj.dtype), rows, rows, *x_out),
        grid=grid,
        in_specs=[full, blk, blk, full, all_rows, pl.BlockSpec(memory_space=pl.ANY)] + x_in_specs,
        out_specs=(pl.BlockSpec((tq, 3 * wide), lambda h, j: (j, first + h)), all_rows,
                   pl.BlockSpec((G, None, 1, tq), lambda h, j: (h, j, 0, 0)), *x_out_specs),
        input_output_aliases={5: 0},
        scratch_shapes=[pltpu.VMEM((G, tq, lanes), F32), pltpu.VMEM((G, tq, lanes), F32),
                        pltpu.VMEM((G, nq, lanes, tq), F32)] + x_scr,
        compiler_params=_params(("arbitrary", "arbitrary")), name=name,
    )(q_aug, k_aug, v_aug, do_aug, lse_row, dproj, *x_in)


def _t5_bucket_np(d):
    n = np.maximum(d, 0).astype(np.int32)
    max_exact = N_BUCKETS // 2
    nf = np.maximum(n, 1).astype(np.float32)
    large = max_exact + (np.log(nf / max_exact) / math.log(MAX_DISTANCE / max_exact)
                         * (N_BUCKETS - max_exact)).astype(np.int32)
    large = np.minimum(large, N_BUCKETS - 1)
    return np.where(n < max_exact, n, large)


def _bucket_onehots():
    k = np.arange(BLOCK)[:, None]
    q = np.arange(BLOCK)[None, :]
    eye = np.eye(N_BUCKETS, dtype=np.float32)
    cur = eye[_t5_bucket_np(q - k).reshape(-1)]
    prev = eye[_t5_bucket_np(BLOCK + q - k).reshape(-1)]
    return cur, prev


SWA_K_COL = SWA_Q_HEADS * HEAD_DIM // (2 * HEAD_DIM)
SWA_V_COL = SWA_K_COL + 1


def _swa_terms(raw, bc, bp, far, sink, n):
    k = lax.broadcasted_iota(jnp.int32, (BLOCK, BLOCK), 0)
    q = lax.broadcasted_iota(jnp.int32, (BLOCK, BLOCK), 1)
    never = 2 * BLOCK
    s_c = raw[0] + bc
    s_p = raw[1] + bp
    s_m = raw[2] + jnp.where(n == 1, bp, far)
    s_c = jnp.where((k <= q) & (k >= jnp.where(n >= 1, 0, PAD_ROWS)), s_c, NEG)
    s_p = jnp.where(k > q + jnp.where(n >= 2, 0, never), s_p, NEG)
    s_m = jnp.where(k >= jnp.where(n >= 1, PAD_ROWS, never), s_m, NEG)
    m = jnp.maximum(jnp.maximum(jnp.max(s_c, axis=0, keepdims=True), jnp.max(s_p, axis=0, keepdims=True)),
                    jnp.maximum(jnp.max(s_m, axis=0, keepdims=True), sink))
    e = [jnp.exp(s_c - m), jnp.exp(s_p - m), jnp.exp(s_m - m)]
    e_s = jnp.exp(sink - m)
    l = (jnp.sum(e[0], axis=0, keepdims=True) + jnp.sum(e[1], axis=0, keepdims=True)
         + jnp.sum(e[2], axis=0, keepdims=True) + e_s)
    return e, e_s, l


SWA_STEP = 3


def _swa_specs():
    R = SWA_STEP

    def window(col):
        return ([pl.BlockSpec((BLOCK, BLOCK), lambda s, w=w: (jnp.maximum(R * s - 1 + w, 0), col)) for w in range(R + 1)]
                + [pl.BlockSpec((BLOCK, BLOCK), lambda s: (0, col))])

    qblk = pl.BlockSpec((R * BLOCK, SWA_Q_HEADS * HEAD_DIM), lambda s: (s, 0))
    bias = pl.BlockSpec((SWA_Q_HEADS, BLOCK, BLOCK), lambda s: (0, 0, 0))
    smem = pl.BlockSpec(memory_space=pltpu.SMEM)
    return qblk, window(SWA_K_COL), window(SWA_V_COL), bias, smem


def _swa_own_kv(tile_ref, kv):
    lane = lax.broadcasted_iota(jnp.int32, (BLOCK, 2 * HEAD_DIM), 1)
    t = tile_ref[...].astype(F32)
    return jnp.where(lane // HEAD_DIM == kv, t, pltpu.roll(t, HEAD_DIM, 1)).astype(BF16)


def _swa_fwd(proj, bc, bp, far, sinks, *, name):
    T = proj.shape[0]
    nb = T // BLOCK
    G = SWA_GROUP
    Hq = SWA_Q_HEADS
    lanes = 2 * HEAD_DIM

    R = SWA_STEP
    assert nb % R == 0

    def body(*refs):
        q_ref, k_refs, v_refs = refs[0], refs[1:R + 3], refs[R + 3:2 * R + 5]
        bc_ref, bp_ref, far_ref, sink_ref, o_ref = refs[2 * R + 5:]
        s = pl.program_id(0)
        lane = lax.broadcasted_iota(jnp.int32, (BLOCK, lanes), 1)
        kvs = range(SWA_KV_HEADS)
        kk = [[_swa_own_kv(ref, kv) for ref in k_refs] for kv in kvs]
        vv = [[_swa_own_kv(ref, kv) for ref in v_refs] for kv in kvs]
        chains = [(r, h) for r in range(R) for h in range(Hq)]
        tiles = lambda r: (r + 1, r, R + 1)
        q2 = {(r, pair): q_ref[r * BLOCK:(r + 1) * BLOCK, pair * lanes:(pair + 1) * lanes].astype(F32) * SCALE
              for r in range(R) for pair in range(Hq // 2)}
        qm = {c: jnp.where(lane // HEAD_DIM == c[1] % 2, q2[c[0], c[1] // 2], 0.0).astype(BF16) for c in chains}
        raw = {c: [lax.dot_general(kk[c[1] // G][w], qm[c], NT, preferred_element_type=F32) for w in tiles(c[0])]
               for c in chains}
        terms = {c: _swa_terms(raw[c], bc_ref[c[1]], bp_ref[c[1]], far_ref[c[1]], sink_ref[c[1]], R * s + c[0])
                 for c in chains}
        o_t = {c: sum(lax.dot_general(vv[c[1] // G][w], terms[c][0][b].astype(BF16), TN, preferred_element_type=F32)
                      for b, w in enumerate(tiles(c[0]))) for c in chains}
        outs = {c: (o_t[c] / terms[c][2]).T for c in chains}
        for r in range(R):
            for pair in range(Hq // 2):
                o_ref[r * BLOCK:(r + 1) * BLOCK, pair * lanes:(pair + 1) * lanes] = jnp.where(
                    lane < HEAD_DIM, outs[r, 2 * pair], outs[r, 2 * pair + 1]).astype(o_ref.dtype)

    qblk, keys, vals, bias, smem = _swa_specs()
    return pl.pallas_call(
        body, out_shape=jax.ShapeDtypeStruct((T, D_MODEL), BF16), grid=(nb // R,),
        in_specs=[qblk] + keys + vals + [bias, bias, smem, smem],
        out_specs=qblk,
        compiler_params=_params(("parallel",)), name=name,
    )(proj, *([proj] * (2 * R + 4)), bc, bp, far, sinks)


def _swa_bwd(proj, dmix, bc, bp, far, sinks, *, ex=None, name):
    T, width = proj.shape
    nb = T // BLOCK
    G = SWA_GROUP
    Hq = SWA_Q_HEADS
    lanes = 2 * HEAD_DIM
    qw = Hq * HEAD_DIM
    own_w = qw + 2 * lanes

    R = SWA_STEP
    assert nb % R == 0
    n_in = 2 * R + 10

    def body(*refs):
        q_ref, k_refs, v_refs = refs[0], refs[1:R + 3], refs[R + 3:2 * R + 5]
        do_ref, bc_ref, bp_ref, far_ref, sink_ref = refs[2 * R + 5:n_in]
        dp_ref, dbc_ref, dbp_ref, dbf_ref, dsk_ref, dk_acc, dv_acc = refs[n_in:]
        s = pl.program_id(0)

        @pl.when(s == 0)
        def _():
            for ref in (dk_acc, dv_acc, dbc_ref, dbp_ref, dbf_ref, dsk_ref):
                ref[...] = jnp.zeros(ref.shape, F32)

        lane = lax.broadcasted_iota(jnp.int32, (BLOCK, lanes), 1)
        kvs = range(SWA_KV_HEADS)
        kk = [[_swa_own_kv(ref, kv) for ref in k_refs] for kv in kvs]
        vv = [[_swa_own_kv(ref, kv) for ref in v_refs] for kv in kvs]
        chains = [(r, h) for r in range(R) for h in range(Hq)]
        blocks = range(3)
        tiles = lambda r: (r + 1, r, R + 1)
        sub = lambda ref, r, pair: ref[r * BLOCK:(r + 1) * BLOCK, pair * lanes:(pair + 1) * lanes]
        q2 = {(r, pair): sub(q_ref, r, pair).astype(F32) * SCALE for r in range(R) for pair in range(Hq // 2)}
        d2 = {(r, pair): sub(do_ref, r, pair) for r in range(R) for pair in range(Hq // 2)}
        own = [lane // HEAD_DIM == half for half in range(2)]
        qm = {c: jnp.where(own[c[1] % 2], q2[c[0], c[1] // 2], 0.0).astype(BF16) for c in chains}
        dom = {c: jnp.where(own[c[1] % 2], d2[c[0], c[1] // 2], jnp.zeros_like(d2[0, 0])) for c in chains}
        raw = {c: [lax.dot_general(kk[c[1] // G][w], qm[c], NT, preferred_element_type=F32) for w in tiles(c[0])]
               for c in chains}
        dp = {c: [lax.dot_general(vv[c[1] // G][w], dom[c], NT, preferred_element_type=F32) for w in tiles(c[0])]
              for c in chains}
        p, ds16 = {}, {}
        for c in chains:
            r, h = c
            n = R * s + r
            e, e_s, l = _swa_terms(raw[c], bc_ref[h], bp_ref[h], far_ref[h], sink_ref[h], n)
            inv = 1.0 / l
            ph = [e[b] * inv for b in blocks]
            delta = sum(jnp.sum(ph[b] * dp[c][b], axis=0, keepdims=True) for b in blocks)
            ds = [ph[b] * (dp[c][b] - delta) for b in blocks]
            dsk_ref[h] += -(e_s * inv) * delta
            dbc_ref[h] += ds[0]
            dbp_ref[h] += ds[1] + jnp.where(n == 1, ds[2], 0.0)
            dbf_ref[h] += jnp.where(n >= 2, ds[2], 0.0)
            p[c] = [x.astype(BF16) for x in ph]
            ds16[c] = [x.astype(BF16) for x in ds]
        dq_t = {c: sum(lax.dot_general(kk[c[1] // G][w], ds16[c][b], TN, preferred_element_type=F32)
                       for b, w in enumerate(tiles(c[0]))) for c in chains}
        group = [range(kv * G, (kv + 1) * G) for kv in kvs]
        dk = {(r, kv): [sum(jnp.dot(ds16[r, h][b], qm[r, h], preferred_element_type=F32) for h in group[kv])
                        for b in blocks] for r in range(R) for kv in kvs}
        dv = {(r, kv): [sum(jnp.dot(p[r, h][b], dom[r, h], preferred_element_type=F32) for h in group[kv])
                        for b in blocks] for r in range(R) for kv in kvs}
        for r in range(R):
            n = R * s + r
            rows = pl.ds(pl.multiple_of(n * BLOCK, BLOCK), BLOCK)
            prev_rows = pl.ds(pl.multiple_of(jnp.maximum(n - 1, 0) * BLOCK, BLOCK), BLOCK)
            for pair in range(Hq // 2):
                dp_ref[rows, pair * lanes:(pair + 1) * lanes] = (jnp.where(
                    lane < HEAD_DIM, dq_t[r, 2 * pair].T, dq_t[r, 2 * pair + 1].T) * SCALE).astype(dp_ref.dtype)
            for acc, ref in ((dk, dk_acc), (dv, dv_acc)):
                tot = [[a + pltpu.roll(a, HEAD_DIM, 1) for a in acc[r, kv]] for kv in kvs]
                both = [jnp.where(lane < HEAD_DIM, tot[0][b], tot[1][b]) for b in blocks]
                ref[rows, :] += both[0]
                ref[prev_rows, :] += both[1]
                ref[0:BLOCK, :] += both[2]

        @pl.when(s == nb // R - 1)
        def _():
            dp_ref[:, qw:qw + lanes] = dk_acc[...].astype(dp_ref.dtype)
            dp_ref[:, qw + lanes:own_w] = dv_acc[...].astype(dp_ref.dtype)

    qblk, keys, vals, bias, smem = _swa_specs()
    dsk = pl.BlockSpec((Hq, 1, BLOCK), lambda s: (0, 0, 0))
    grid = (nb // R,)
    body, x_in, x_in_specs, x_out, x_out_specs, x_scr = _carry(ex, grid, n_in, 5, body)
    tile = jax.ShapeDtypeStruct((Hq, BLOCK, BLOCK), F32)
    return pl.pallas_call(
        body,
        out_shape=(jax.ShapeDtypeStruct((T, width), BF16), tile, tile, tile,
                   jax.ShapeDtypeStruct((Hq, 1, BLOCK), F32), *x_out),
        grid=grid,
        in_specs=[qblk] + keys + vals + [qblk, bias, bias, smem, smem] + x_in_specs,
        out_specs=(pl.BlockSpec((T, own_w), lambda s: (0, 0)), bias, bias, bias, dsk, *x_out_specs),
        scratch_shapes=[pltpu.VMEM((T, lanes), F32), pltpu.VMEM((T, lanes), F32)] + x_scr,
        compiler_params=_params(("arbitrary",)), name=name,
    )(proj, *([proj] * (2 * R + 4)), dmix, bc, bp, far, sinks, *x_in)


def _bias_tiles(tab_t, oh_cur_t, oh_prev_t, *, name):
    Hq = tab_t.shape[0]

    def body(t_ref, oc_ref, op_ref, bc_ref, bp_ref):
        bc_ref[...] = jnp.dot(t_ref[...], oc_ref[...], precision=HIGHEST, preferred_element_type=F32)
        bp_ref[...] = jnp.dot(t_ref[...], op_ref[...], precision=HIGHEST, preferred_element_type=F32)

    vm = pl.BlockSpec(memory_space=pltpu.VMEM)
    shp = jax.ShapeDtypeStruct((Hq, BLOCK * BLOCK), F32)
    bc, bp = pl.pallas_call(body, out_shape=(shp, shp), in_specs=[vm] * 3, out_specs=(vm, vm),
                            compiler_params=_params(), name=name)(tab_t, oh_cur_t, oh_prev_t)
    return bc.reshape(Hq, BLOCK, BLOCK), bp.reshape(Hq, BLOCK, BLOCK)


def _small_grads(dbc, dbp, dbf, dsk, oh_cur, oh_prev, *, name):
    Hq = dbc.shape[0]

    def body(dbc_ref, dbp_ref, dbf_ref, dsk_ref, oc_ref, op_ref, tab_ref, sink_ref):
        tab = (jnp.dot(dbc_ref[...], oc_ref[...], precision=HIGHEST, preferred_element_type=F32)
               + jnp.dot(dbp_ref[...], op_ref[...], precision=HIGHEST, preferred_element_type=F32))
        far = jnp.sum(dbf_ref[...], axis=1, keepdims=True)
        last = lax.broadcasted_iota(jnp.int32, (Hq, N_BUCKETS), 1) == N_BUCKETS - 1
        tab_ref[...] = tab + jnp.where(last, far, 0.0)
        sink_ref[...] = jnp.sum(dsk_ref[...], axis=1, keepdims=True)

    vm = pl.BlockSpec(memory_space=pltpu.VMEM)
    return pl.pallas_call(
        body, out_shape=(jax.ShapeDtypeStruct((Hq, N_BUCKETS), F32), jax.ShapeDtypeStruct((Hq, 1), F32)),
        in_specs=[vm] * 6, out_specs=(vm, vm), compiler_params=_params(), name=name,
    )(dbc.reshape(Hq, -1), dbp.reshape(Hq, -1), dbf.reshape(Hq, -1), dsk.reshape(Hq, -1), oh_cur, oh_prev)


def _coords():
    return lax.axis_index("x"), lax.axis_index("y"), lax.axis_index("c")


class _Exchange:
    def __init__(self, inputs, out_shapes, scratch, start, finish):
        self.inputs, self.out_shapes, self.scratch, self.start, self.finish = inputs, out_shapes, scratch, start, finish


def _carry(ex, grid, n_in, n_out, body):
    if ex is None:
        return body, [], [], [], [], []
    ni, no = len(ex.inputs), len(ex.out_shapes)

    def at_step(which):
        cond = None
        for axis, n in enumerate(grid):
            c = pl.program_id(axis) == (0 if which == "first" else n - 1)
            cond = c if cond is None else cond & c
        return cond

    def wrapped(*refs):
        refs = list(refs)
        n_own_scr = len(refs) - (n_in + ni + n_out + no) - len(ex.scratch)
        own_in, side_in = refs[:n_in], refs[n_in:n_in + ni]
        own_out = refs[n_in + ni:n_in + ni + n_out]
        side_out = refs[n_in + ni + n_out:n_in + ni + n_out + no]
        rest = refs[n_in + ni + n_out + no:]
        own_scr, sems = rest[:n_own_scr], rest[n_own_scr:]

        @pl.when(at_step("first"))
        def _():
            ex.start(side_in, side_out, sems)

        body(*own_in, *own_out, *own_scr)

        @pl.when(at_step("last"))
        def _():
            ex.finish(side_in, side_out, sems)

    hbm = pl.BlockSpec(memory_space=pl.ANY)
    return wrapped, list(ex.inputs), [hbm] * ni, list(ex.out_shapes), [hbm] * no, list(ex.scratch)


def _run_exchange(ex, *, name):
    ni, no = len(ex.inputs), len(ex.out_shapes)

    def body(*refs):
        ins, outs, sems = refs[:ni], refs[ni:ni + no], refs[ni + no:]
        ex.start(ins, outs, sems)
        ex.finish(ins, outs, sems)

    hbm = pl.BlockSpec(memory_space=pl.ANY)
    return pl.pallas_call(
        body, out_shape=tuple(ex.out_shapes), in_specs=[hbm] * ni, out_specs=tuple([hbm] * no),
        scratch_shapes=ex.scratch, compiler_params=_params(), name=name)(*ex.inputs)


def _gather_exchange(shards):
    nt = len(shards)

    def copies(ins, outs, sems):
        send_sems, recv_sems, local_sems = sems
        x, y, c = _coords()
        me, sibling = (x, y, c), (x, y, 1 - c)
        chips = [(1 - x, y), (x, 1 - y), (1 - x, 1 - y)]

        def slot(t, dev):
            return outs[t].at[4 * dev[0] + 2 * dev[1] + dev[2]]

        def copy(t, k, block, to, src=None):
            dst = slot(t, block)
            return pltpu.make_async_remote_copy(
                src_ref=dst if src is None else src, dst_ref=dst,
                send_sem=send_sems.at[t, k], recv_sem=recv_sems.at[t, k], device_id=to, device_id_type=MESH)

        mine = [pltpu.make_async_copy(ins[t], slot(t, me), local_sems.at[t]) for t in range(nt)]
        first = []
        for t in range(nt):
            first.append(copy(t, 0, me, sibling, src=ins[t]))
            first += [copy(t, 1 + j, me, (*chip, c), src=ins[t]) for j, chip in enumerate(chips)]
        return copy, mine, first, me, sibling, chips, c

    def start(ins, outs, sems):
        _, mine, first, *_ = copies(ins, outs, sems)
        for cp in mine + first:
            cp.start()

    def finish(ins, outs, sems):
        copy, mine, first, me, sibling, chips, c = copies(ins, outs, sems)
        passed = []
        for j, chip in enumerate(chips):
            for t in range(nt):
                copy(t, 1 + j, (*chip, c), me).wait_recv()
                cp = copy(t, 4 + j, (*chip, c), sibling)
                cp.start()
                passed.append(cp)
        for t in range(nt):
            copy(t, 0, sibling, me).wait_recv()
            for j, chip in enumerate(chips):
                copy(t, 4 + j, (*chip, 1 - c), me).wait_recv()
        for cp in first + passed:
            cp.wait_send()
        for cp in mine:
            cp.wait()

    return _Exchange(
        list(shards), [jax.ShapeDtypeStruct((N_DEV,) + s.shape, s.dtype) for s in shards],
        [pltpu.SemaphoreType.DMA((nt, 7)), pltpu.SemaphoreType.DMA((nt, 7)), pltpu.SemaphoreType.DMA((nt,))],
        start, finish)


def _swap_exchange(arrays, n_slices, copies):
    nt = len(arrays)

    def start(ins, outs, sems):
        for cp in copies(ins, outs, sems):
            cp.start()

    def finish(ins, outs, sems):
        sends = copies(ins, outs, sems)
        for cp in sends:
            cp.wait_recv()
        for cp in sends:
            cp.wait_send()

    return _Exchange(
        list(arrays), [jax.ShapeDtypeStruct((n_slices,) + a.shape[1:], a.dtype) for a in arrays],
        [pltpu.SemaphoreType.DMA((nt, n_slices)), pltpu.SemaphoreType.DMA((nt, n_slices))], start, finish)


def _cores_exchange(gs):
    def copies(ins, outs, sems):
        send_sems, recv_sems = sems
        x, y, c = _coords()
        return [pltpu.make_async_remote_copy(
            src_ref=ins[t].at[2 * j + (1 - c)], dst_ref=outs[t].at[j],
            send_sem=send_sems.at[t, j], recv_sem=recv_sems.at[t, j], device_id=(x, y, 1 - c), device_id_type=MESH)
            for t in range(len(gs)) for j in range(4)]

    return _swap_exchange(gs, 4, copies)


def _chips_exchange(ps):
    def copies(ins, outs, sems):
        send_sems, recv_sems = sems
        x, y, c = _coords()
        peers = [(1 - x, y), (x, 1 - y), (1 - x, 1 - y)]
        return [pltpu.make_async_remote_copy(
            src_ref=ins[t].at[2 * px + py], dst_ref=outs[t].at[k],
            send_sem=send_sems.at[t, k], recv_sem=recv_sems.at[t, k], device_id=(px, py, c), device_id_type=MESH)
            for t in range(len(ps)) for k, (px, py) in enumerate(peers)]

    return _swap_exchange(ps, 3, copies)


def _add_cores(g, r, core, *, name):
    _, A, B = g.shape
    ta = _tile(A, 512, 16)

    def body(core_ref, a_ref, b_ref, o_ref, o16_ref):
        s = a_ref[...] + b_ref[...]
        o_ref[...] = s
        o16_ref[...] = s.astype(BF16)

    blk = (None, ta, B)
    out = pl.BlockSpec(blk, lambda j, i, core_ref: (j, i, 0))
    return pl.pallas_call(
        body, out_shape=(jax.ShapeDtypeStruct((4, A, B), F32), jax.ShapeDtypeStruct((4, A, B), BF16)),
        grid_spec=pltpu.PrefetchScalarGridSpec(
            num_scalar_prefetch=1, grid=(4, A // ta),
            in_specs=[pl.BlockSpec(blk, lambda j, i, core_ref: (2 * j + core_ref[0], i, 0)),
                      pl.BlockSpec(blk, lambda j, i, core_ref: (j, i, 0))],
            out_specs=(out, out)),
        compiler_params=_params(("parallel", "parallel")), name=name)(core, g, r)


def _adamw_math(w, g, m, v):
    m = ADAM_B1 * m + (1.0 - ADAM_B1) * g
    v = ADAM_B2 * v + (1.0 - ADAM_B2) * (g * g)
    m_hat = m / (1.0 - ADAM_B1 ** ADAM_STEP)
    v_hat = v / (1.0 - ADAM_B2 ** ADAM_STEP)
    delta = -ADAM_LR * (m_hat / (jnp.sqrt(v_hat) + ADAM_EPS) + ADAM_WD * w)
    return delta, m, v


def _sum_adamw(p, r, chip, w, m, v, *, segs, ta, name):
    Aw, Bw = w.shape
    Bg = p.shape[2]
    assert Aw % ta == 0

    def body(chip_ref, p_ref, r0, r1, r2, w_ref, m_ref, v_ref, g_out, d_out, m_out, v_out):
        for gc, wc, n in segs:
            g = ((p_ref[:, gc:gc + n] + r0[:, gc:gc + n].astype(F32)) + r1[:, gc:gc + n].astype(F32)
                 ) + r2[:, gc:gc + n].astype(F32)
            delta, m_new, v_new = _adamw_math(w_ref[:, wc:wc + n], g, m_ref[:, wc:wc + n], v_ref[:, wc:wc + n])
            g_out[:, wc:wc + n] = g
            d_out[:, wc:wc + n] = delta
            m_out[:, wc:wc + n] = m_new
            v_out[:, wc:wc + n] = v_new

    gblk = (None, ta, Bg)
    row = pl.BlockSpec((ta, Bw), lambda i, chip_ref: (i, 0))
    rspecs = [pl.BlockSpec(gblk, (lambda i, chip_ref, k=k: (k, i, 0))) for k in range(3)]
    shp = jax.ShapeDtypeStruct((Aw, Bw), F32)
    return pl.pallas_call(
        body, out_shape=(shp, shp, shp, shp),
        grid_spec=pltpu.PrefetchScalarGridSpec(
            num_scalar_prefetch=1, grid=(Aw // ta,),
            in_specs=[pl.BlockSpec(gblk, lambda i, chip_ref: (chip_ref[0], i, 0))] + rspecs + [row, row, row],
            out_specs=(row, row, row, row)),
        compiler_params=_params(("parallel",)), name=name)(chip, p, r, r, r, w, m, v)


def _adamw(w, g, m, v, *, name):
    def body(w_ref, g_ref, m_ref, v_ref, d_out, m_out, v_out):
        delta, m_new, v_new = _adamw_math(w_ref[...], g_ref[...], m_ref[...], v_ref[...])
        d_out[...] = delta
        m_out[...] = m_new
        v_out[...] = v_new

    vm = pl.BlockSpec(memory_space=pltpu.VMEM)
    shp = jax.ShapeDtypeStruct(w.shape, F32)
    return pl.pallas_call(body, out_shape=(shp, shp, shp), in_specs=[vm] * 4, out_specs=(vm, vm, vm),
                          compiler_params=_params(), name=name)(w, g, m, v)


def _small_allreduce_adamw(s, w, m, v, *, name):
    R, W = s.shape

    def body(s_ref, w_ref, m_ref, v_ref, g_out, d_out, m_out, v_out, gath, send_sems, recv_sems):
        x, y, c = _coords()
        mine = 4 * x + 2 * y + c
        gath[mine] = s_ref[...]
        peers = [((1 - x) if k & 4 else x, (1 - y) if k & 2 else y, (1 - c) if k & 1 else c) for k in range(1, N_DEV)]
        sends = []
        for k in range(1, N_DEV):
            peer = peers[k - 1]
            sends.append(pltpu.make_async_remote_copy(
                src_ref=s_ref, dst_ref=gath.at[mine], send_sem=send_sems.at[k - 1], recv_sem=recv_sems.at[k - 1],
                device_id=peer, device_id_type=MESH))
        for cp in sends:
            cp.start()
        for k in range(1, N_DEV):
            peer = peers[k - 1]
            pltpu.make_async_remote_copy(
                src_ref=s_ref, dst_ref=gath.at[4 * peer[0] + 2 * peer[1] + peer[2]],
                send_sem=send_sems.at[k - 1], recv_sem=recv_sems.at[k - 1],
                device_id=peer, device_id_type=MESH).wait_recv()
        for cp in sends:
            cp.wait_send()
        g = gath[0]
        for d in range(1, N_DEV):
            g = g + gath[d]
        delta, m_new, v_new = _adamw_math(w_ref[...], g, m_ref[...], v_ref[...])
        g_out[...] = g
        d_out[...] = delta
        m_out[...] = m_new
        v_out[...] = v_new

    vm = pl.BlockSpec(memory_space=pltpu.VMEM)
    shp = jax.ShapeDtypeStruct((R, W), F32)
    return pl.pallas_call(
        body, out_shape=(shp, shp, shp, shp), in_specs=[vm] * 4, out_specs=(vm, vm, vm, vm),
        scratch_shapes=[pltpu.VMEM((N_DEV, R, W), F32), pltpu.SemaphoreType.DMA((N_DEV - 1,)),
                        pltpu.SemaphoreType.DMA((N_DEV - 1,))],
        compiler_params=_params(), name=name)(s, w, m, v)


def _pack_small(rel_bias, g1, g2, g3, g4, b_forget, sinks, extra=None, meta=None):
    misc = jnp.concatenate([rel_bias.reshape(-1), b_forget.reshape(-1), sinks.reshape(-1)])
    misc = jnp.concatenate([misc, jnp.zeros((D_MODEL - misc.shape[0],), F32)])[None]
    last = jnp.zeros((1, D_MODEL), F32) if extra is None else extra
    meta = jnp.zeros((N_META, D_MODEL), F32) if meta is None else meta
    return jnp.concatenate([g1, g2, g3, g4, misc, last, jnp.zeros((2, D_MODEL), F32), meta], axis=0)


def _unpack_small(p):
    nrb = N_BUCKETS * SWA_Q_HEADS
    misc = p[4]
    return dict(rel_bias=misc[:nrb].reshape(N_BUCKETS, SWA_Q_HEADS), ln_pre_mix=p[0:1], ln_post_mix=p[1:2],
                ln_pre_ffn=p[2:3], ln_post_ffn=p[3:4], b_forget=misc[nrb:nrb + 8].reshape(1, 8),
                sinks=misc[nrb + 8:nrb + 16].reshape(1, 8))


def _proj_runs():
    gw = FOX_GROUP * HEAD_DIM
    swa = SWA_Q_W + 2 * SWA_KV_HEADS * HEAD_DIM
    runs = [(0, swa)]
    for grp in range(FOX_HEADS // FOX_GROUP):
        runs += [(swa + part * FOX_W + grp * gw, swa + part * FOX_W + (grp + 1) * gw) for part in range(3)]
    return runs


def _columns_from_shards(gathered, runs, shard):
    pieces = []
    for start, stop in runs:
        for d in range(start // shard, (stop - 1) // shard + 1):
            lo = d * shard
            pieces.append(gathered[d][:, max(start, lo) - lo:min(stop, lo + shard) - lo])
    return jnp.concatenate(pieces, axis=1)


def _device_shards(qkv, gate, shard, padded):
    pos, segments = 0, []
    for start, stop in _proj_runs():
        segments.append((start, stop, qkv, pos))
        pos += stop - start
    segments.append((pos, pos + gate.shape[1], gate, 0))
    total = pos + gate.shape[1]
    assert total % shard == 0
    zeros = jnp.zeros((qkv.shape[0], padded - shard), qkv.dtype)
    out = []
    for d in range(total // shard):
        lo, hi = d * shard, (d + 1) * shard
        pieces = [arr[:, src + max(lo, s) - s:src + min(hi, e) - s]
                  for s, e, arr, src in sorted(segments, key=lambda seg: seg[0]) if max(lo, s) < min(hi, e)]
        out.append(jnp.concatenate(pieces + [zeros], axis=1))
    return jnp.stack(out)


def kernel(x, meta_tokens, rel_bias, ln_pre_mix, ln_post_mix, ln_pre_ffn, ln_post_ffn, w_in, b_forget, sinks, w_out, w_gate_up, w_down, loss_target, m_meta_tokens, m_rel_bias, m_ln_pre_mix, m_ln_post_mix, m_ln_pre_ffn, m_ln_post_ffn, m_w_in, m_b_forget, m_sinks, m_w_out, m_w_gate_up, m_w_down, v_meta_tokens, v_rel_bias, v_ln_pre_mix, v_ln_post_mix, v_ln_pre_ffn, v_ln_post_ffn, v_w_in, v_b_forget, v_sinks, v_w_out, v_w_gate_up, v_w_down):
    seq = x.shape[1]
    T = BLOCK + seq
    assert T % FOX_TILE == 0
    nq = T // FOX_TILE
    tm = _tile(T, 1056)
    cin = w_in.shape[2]
    hid = w_down.shape[1]
    assert w_gate_up.shape[2] == 2 * hid and cin <= W_IN_PAD and hid <= HID_PAD

    x_i, y_i, c_i = _coords()
    core = jnp.reshape(c_i, (1,)).astype(jnp.int32)
    chip = jnp.reshape(2 * x_i + y_i, (1,)).astype(jnp.int32)
    w_in_s = jnp.pad(w_in[0].astype(BF16), ((0, 0), (0, W_IN_PAD - cin)))
    w_gu_s = jnp.pad(w_gate_up[0].astype(BF16).reshape(D_MODEL, 2, hid), ((0, 0), (0, 0), (0, HID_PAD - hid)))
    w_gu_s = w_gu_s.reshape(D_MODEL, 2 * HID_PAD)
    w_down_s = jnp.pad(w_down[0].astype(BF16), ((0, HID_PAD - hid), (0, 0)))
    g_in, g_meta = _run_exchange(_gather_exchange([w_in_s, meta_tokens]), name="ag_w_in")
    gather_rest = _gather_exchange([w_out[0].astype(BF16), w_gu_s, w_down_s])
    w_qkv = _columns_from_shards(g_in, _proj_runs(), cin)
    w_f = jnp.pad(_columns_from_shards(g_in, [(D_QKV, D_PROJ)], cin), ((0, 0), (0, BLOCK - FOX_HEADS)))
    meta_full = g_meta.transpose(1, 0, 2).reshape(N_META, D_MODEL)

    h0 = jnp.concatenate([jnp.zeros((PAD_ROWS, D_MODEL), F32), meta_full, x[0]], axis=0)
    target = jnp.concatenate([jnp.zeros((BLOCK, D_MODEL), F32), loss_target[0]], axis=0)
    hn1, hn1_t = _rms_fwd(h0, ln_pre_mix, name="rms_pre_mix")
    proj = _matmul(hn1, w_qkv, out_dtype=BF16, tm=tm, tn=D_QKV, name="mm_in_proj")
    proj_f = _matmul(hn1, w_f, out_dtype=F32, tm=tm, tn=BLOCK, name="mm_in_proj_f")

    f_t = proj_f[:, :FOX_HEADS].T
    bf_col = b_forget.reshape(FOX_HEADS, 1)

    oh_cur, oh_prev = _bucket_onehots()
    bias_c, bias_p = _bias_tiles(rel_bias.T, jnp.asarray(oh_cur.T), jnp.asarray(oh_prev.T), name="bias_tiles")
    far = rel_bias[N_BUCKETS - 1]
    sink_v = sinks[0]
    mix_a = _swa_fwd(proj, bias_c, bias_p, far, sink_v, name="swa_fwd")

    _, cum_col = _fox_gates_fwd(f_t, bf_col, name="fox_gates_fwd")
    q_b, k_b, v_b = _fox_prep(proj, cum_col, name="fox_prep")
    mix, lse_row, g_out, g_gu, g_down = _fox_fwd(q_b, k_b, v_b, mix_a, ex=gather_rest, name="fox_fwd")
    w_out_full = g_out.reshape(D_MODEL, D_MODEL)
    w_down_full = g_down.reshape(N_DEV * HID_PAD, D_MODEL)

    a1 = _matmul(mix, w_out_full, out_dtype=F32, tm=tm, tn=D_MODEL, name="mm_out_proj")
    h1, hn2, hn2_t = _post_res_norm(a1, ln_post_mix, h0, ln_pre_ffn, name="post_mix_pre_ffn")
    gate, up, act, act_t = _gate_up_swiglu(hn2, g_gu, name="mm_gate_up")
    ff = _matmul(act, w_down_full, out_dtype=F32, tm=tm, tn=512, name="mm_down")
    dh2, dff, dg_post_ffn, loss_acc = _loss_head(ff, ln_post_ffn, h1, target, name="loss_head")

    dgu = _d_act_swiglu(dff, w_down_full, gate, up, name="mm_d_act")
    d_w_down = _matmul(act_t, dff, out_dtype=F32, tm=768, tn=512, name="mm_dw_down")
    dhn2 = _matmul(dgu, g_gu, nt=True, b_shards=True, out_dtype=F32, tm=tm, tn=512, name="mm_d_hn2")
    d_w_gu = _matmul(hn2_t, dgu, out_shards=True, out_dtype=F32, tm=512, tn=2 * HID_PAD, name="mm_dw_gate_up")
    dh1, dg_pre_ffn, da1, dg_post_mix = _rms_bwd(h1, ln_pre_ffn, dhn2, dh2, out_dtype=F32,
                                                 then=(a1, ln_post_mix), name="rms_bwd_pre_ffn_post_mix")
    dmix = _matmul(da1, w_out_full, nt=True, out_dtype=BF16, tm=tm, tn=D_MODEL, name="mm_d_mix")
    d_w_out = _matmul(mix, da1, ta=True, out_dtype=F32, tm=512, tn=D_MODEL, name="mm_dw_out")

    ffn_grads = [d_w_out.reshape(N_DEV, -1, D_MODEL), d_w_gu, d_w_down.reshape(N_DEV, HID_PAD, D_MODEL)]
    dproj_a, dbc, dbp, dbf, dsk, *ffn_sibling = _swa_bwd(
        proj, dmix, bias_c, bias_p, far, sink_v, ex=_cores_exchange(ffn_grads), name="swa_bwd")
    d_tab, d_sink = _small_grads(dbc, dbp, dbf, dsk, jnp.asarray(oh_cur), jnp.asarray(oh_prev), name="small_grads")
    ffn_sums = [_add_cores(g, r, core, name="rs_add_" + t)
                for g, r, t in zip(ffn_grads, ffn_sibling, ["w_out", "w_gate_up", "w_down"])]

    do_b = _fox_prep_bwd(dmix, mix, name="fox_prep_bwd")
    dproj, dcq, dck, *ffn_chips = _fox_bwd(
        q_b, k_b, v_b, do_b, lse_row, dproj_a, ex=_chips_exchange([s[1] for s in ffn_sums]), name="fox_bwd")
    df_t, d_bf = _fox_gates_bwd(dcq.reshape(FOX_HEADS, T), dck.reshape(FOX_HEADS, T), f_t, bf_col,
                                name="fox_gates_bwd")
    df = jnp.pad(df_t.T.astype(BF16), ((0, 0), (0, BLOCK - FOX_HEADS)))

    d_w_qkv = _matmul(hn1_t, dproj, out_dtype=F32, tm=512, tn=768, name="mm_dw_in")
    d_w_f = _matmul(hn1_t, df, out_dtype=F32, tm=512, tn=BLOCK, name="mm_dw_in_f")
    d_w_in = _device_shards(d_w_qkv, d_w_f[:, :FOX_HEADS], cin, W_IN_PAD)
    dhn1, in_sibling = _matmul(dproj, w_qkv, nt=True, out_dtype=F32, tm=tm, tn=512,
                               ex=_cores_exchange([d_w_in]), name="mm_d_hn1")
    in_sum = _add_cores(d_w_in, in_sibling, core, name="rs_add_w_in")
    dh0, dg_pre_mix, in_chips = _rms_bwd(h0, ln_pre_mix, dhn1, dh1, out_dtype=F32, dy2=(df, w_f),
                                         ex=_chips_exchange([in_sum[1]]), name="rms_bwd_pre_mix")
    grad_x = dh0[BLOCK:][None]
    d_meta = dh0[PAD_ROWS:BLOCK]

    tags = ["w_in", "w_out", "w_gate_up", "w_down"]
    chip_sum = [in_sum[0]] + [s[0] for s in ffn_sums]
    from_chips = [in_chips] + list(ffn_chips)
    shard_w = [(w_in, m_w_in, v_w_in), (w_out, m_w_out, v_w_out), (w_gate_up, m_w_gate_up, v_w_gate_up),
               (w_down, m_w_down, v_w_down)]
    segs = [[(0, 0, cin)], [(0, 0, D_MODEL)], [(0, 0, hid), (HID_PAD, hid, hid)], [(0, 0, D_MODEL)]]
    tas = [256, BLOCK, 256, hid]
    big = [{}, {}, {}, {}]
    for i, t in enumerate(tags):
        w_t, m_t, v_t = shard_w[i]
        res = _sum_adamw(chip_sum[i], from_chips[i], chip, w_t[0], m_t[0], v_t[0], segs=segs[i], ta=tas[i],
                         name="rs_adamw_" + t)
        for kind in range(4):
            big[kind][t] = res[kind][None]

    loss_row = jnp.pad(loss_acc[0:1, 0:1] * (0.5 / D_MODEL), ((0, 0), (0, D_MODEL - 1)))
    s_small = _pack_small(d_tab.T, dg_pre_mix, dg_post_mix, dg_pre_ffn, dg_post_ffn, d_bf, d_sink,
                          extra=loss_row, meta=d_meta)
    w_s = _pack_small(rel_bias, ln_pre_mix, ln_post_mix, ln_pre_ffn, ln_post_ffn, b_forget, sinks)
    m_s = _pack_small(m_rel_bias, m_ln_pre_mix, m_ln_post_mix, m_ln_pre_ffn, m_ln_post_ffn, m_b_forget, m_sinks)
    v_s = _pack_small(v_rel_bias, v_ln_pre_mix, v_ln_post_mix, v_ln_pre_ffn, v_ln_post_ffn, v_b_forget, v_sinks)
    small = _small_allreduce_adamw(s_small, w_s, m_s, v_s, name="small_allreduce_adamw")
    loss = small[0][5, 0]
    mcols = meta_tokens.shape[1]
    g_meta_mine = lax.dynamic_slice(small[0][8:8 + N_META], (0, (4 * x_i + 2 * y_i + c_i) * mcols), (N_META, mcols))
    big[0]["meta_tokens"] = g_meta_mine
    for kind, arr in enumerate(_adamw(meta_tokens, g_meta_mine, m_meta_tokens, v_meta_tokens, name="adamw_meta")):
        big[kind + 1]["meta_tokens"] = arr
    small = [_unpack_small(p) for p in small]

    names = ["meta_tokens", "rel_bias", "ln_pre_mix", "ln_post_mix", "ln_pre_ffn", "ln_post_ffn", "w_in",
             "b_forget", "sinks", "w_out", "w_gate_up", "w_down"]
    outs = [loss, grad_x]
    for kind in range(4):
        for nme in names:
            outs.append(big[kind][nme] if nme in big[kind] else small[kind][nme])
    return tuple(outs)
```

```python
import math

import numpy as np
import jax
import jax.numpy as jnp
from jax import lax
from jax.experimental import pallas as pl
from jax.experimental.pallas import tpu as pltpu

F32 = jnp.float32
BF16 = jnp.bfloat16
HIGHEST = lax.Precision.HIGHEST
MESH = pl.DeviceIdType.MESH

N_DEV = 8
D_MODEL = 1024
N_META = 16
HEAD_DIM = 64
SWA_Q_HEADS = 8
SWA_KV_HEADS = 2
SWA_GROUP = 4
FOX_HEADS = 8
FOX_W = FOX_HEADS * HEAD_DIM
SWA_Q_W = SWA_Q_HEADS * HEAD_DIM
BLOCK = 128
PAD_ROWS = BLOCK - N_META
N_BUCKETS = 32
MAX_DISTANCE = 128
D_FF = 2816
D_QKV = 2304
D_PROJ = D_QKV + FOX_HEADS
D_PROJ_PAD = 2560
EPS = 1e-6
NEG = -1e30
SCALE = HEAD_DIM ** -0.5
ADAM_LR, ADAM_B1, ADAM_B2, ADAM_EPS, ADAM_WD, ADAM_STEP = 0.001, 0.9, 0.999, 1e-08, 0.01, 10
VMEM_LIMIT = 56 * 1024 * 1024
FOX_TILE = 384
FOX_GROUP = 4
W_IN_PAD = 384
HID_PAD = 384

NT = (((1,), (1,)), ((), ()))
NN = (((1,), (0,)), ((), ()))
TN = (((0,), (0,)), ((), ()))


def _params(sem=None, **kw):
    if sem is not None:
        kw["dimension_semantics"] = sem
    return pltpu.CompilerParams(vmem_limit_bytes=VMEM_LIMIT, **kw)


def _tile(n, target, mult=16):
    best = None
    for t in range(mult, min(n, target) + 1, mult):
        if n % t == 0:
            best = t
    assert best is not None, (n, target)
    return best


def _matmul(a, b, *, nt=False, ta=False, b_shards=False, out_shards=False, out_dtype, tm, tn=None, tk=None,
            ex=None, name):
    M, K = a.shape[::-1] if ta else a.shape
    assert not (ta and (nt or b_shards))
    k_shards = b.shape[0] if (b_shards and nt) else 0
    if k_shards:
        N, ks = b.shape[1], b.shape[2]
        assert tk is None and K == k_shards * ks
    elif b_shards:
        N, tn = b.shape[0] * b.shape[2], b.shape[2]
    else:
        N = b.shape[0] if nt else b.shape[1]
    tk = K if tk is None else tk
    assert M % tm == 0 and N % tn == 0 and K % tk == 0, (name, a.shape, b.shape, tm, tn, tk)
    nk = K // tk
    dn = NT if nt else (TN if ta else NN)
    a_spec = pl.BlockSpec((tk, tm), lambda i, j, k: (k, i)) if ta else pl.BlockSpec((tm, tk), lambda i, j, k: (i, k))

    def body(a_ref, b_ref, o_ref, *scr):
        if k_shards:
            part = sum(lax.dot_general(a_ref[:, s * ks:(s + 1) * ks], b_ref[s], NT, preferred_element_type=F32)
                       for s in range(k_shards))
        else:
            part = lax.dot_general(a_ref[...], b_ref[...], dn, preferred_element_type=F32)
        if nk == 1:
            o_ref[...] = part.astype(o_ref.dtype)
        else:
            acc = scr[0]
            k = pl.program_id(2)

            @pl.when(k == 0)
            def _():
                acc[...] = part

            @pl.when(k > 0)
            def _():
                acc[...] += part

            @pl.when(k == nk - 1)
            def _():
                o_ref[...] = acc[...].astype(o_ref.dtype)

    if k_shards:
        b_spec = pl.BlockSpec((k_shards, tn, ks), lambda i, j, k: (0, j, 0))
    elif b_shards:
        b_spec = pl.BlockSpec((None, tk, tn), lambda i, j, k: (j, k, 0))
    elif nt:
        b_spec = pl.BlockSpec((tn, tk), lambda i, j, k: (j, k))
    else:
        b_spec = pl.BlockSpec((tk, tn), lambda i, j, k: (k, j))
    if out_shards:
        out_shape = jax.ShapeDtypeStruct((N // tn, M, tn), out_dtype)
        out_spec = pl.BlockSpec((None, tm, tn), lambda i, j, k: (j, i, 0))
    else:
        out_shape = jax.ShapeDtypeStruct((M, N), out_dtype)
        out_spec = pl.BlockSpec((tm, tn), lambda i, j, k: (i, j))
    grid = (M // tm, N // tn, nk)
    body, x_in, x_in_specs, x_out, x_out_specs, x_scr = _carry(ex, grid, 2, 1, body)
    res = pl.pallas_call(
        body,
        out_shape=(out_shape, *x_out),
        grid=grid,
        in_specs=[a_spec, b_spec] + x_in_specs,
        out_specs=(out_spec, *x_out_specs),
        scratch_shapes=([pltpu.VMEM((tm, tn), F32)] if nk > 1 else []) + x_scr,
        compiler_params=_params(("parallel", "parallel", "arbitrary") if ex is None else ("arbitrary",) * 3),
        name=name,
    )(a, b, *x_in)
    return res[0] if ex is None else res


def _rstd(x):
    return lax.rsqrt(jnp.mean(x * x, axis=-1, keepdims=True) + EPS)


def _rms_fwd(x, g, *, name):
    T, D = x.shape
    tm = _tile(T, 512)

    def body(x_ref, g_ref, o_ref, ot_ref):
        x = x_ref[...]
        y = x * _rstd(x) * g_ref[...]
        o_ref[...] = y.astype(o_ref.dtype)
        ot_ref[...] = y.T.astype(ot_ref.dtype)

    return pl.pallas_call(
        body, out_shape=(jax.ShapeDtypeStruct((T, D), BF16), jax.ShapeDtypeStruct((D, T), BF16)), grid=(T // tm,),
        in_specs=[pl.BlockSpec((tm, D), lambda i: (i, 0)), pl.BlockSpec((1, D), lambda i: (0, 0))],
        out_specs=(pl.BlockSpec((tm, D), lambda i: (i, 0)), pl.BlockSpec((D, tm), lambda i: (0, i))),
        compiler_params=_params(("parallel",)), name=name)(x, g)


def _post_res_norm(a, g_post, h, g_pre, *, name):
    T, D = a.shape
    tm = _tile(T, 384, BLOCK)

    def body(a_ref, gp_ref, h_ref, gn_ref, h1_ref, o_ref, ot_ref):
        a = a_ref[...]
        h1 = h_ref[...] + a * _rstd(a) * gp_ref[...]
        h1_ref[...] = h1
        y = h1 * _rstd(h1) * gn_ref[...]
        o_ref[...] = y.astype(o_ref.dtype)
        ot_ref[...] = y.T.astype(ot_ref.dtype)

    row = pl.BlockSpec((tm, D), lambda i: (i, 0))
    vec = pl.BlockSpec((1, D), lambda i: (0, 0))
    return pl.pallas_call(
        body, out_shape=(jax.ShapeDtypeStruct((T, D), F32), jax.ShapeDtypeStruct((T, D), BF16),
                         jax.ShapeDtypeStruct((D, T), BF16)), grid=(T // tm,),
        in_specs=[row, vec, row, vec], out_specs=(row, row, pl.BlockSpec((D, tm), lambda i: (0, i))),
        compiler_params=_params(("parallel",)), name=name)(a, g_post, h, g_pre)


def _loss_head(a, g, h, target, *, name):
    T, D = a.shape
    tm = _tile(T, 512)

    def body(a_ref, g_ref, h_ref, t_ref, dy_ref, da_ref, dg_ref, loss_ref):
        i = pl.program_id(0)
        a = a_ref[...]
        r = _rstd(a)
        ah = a * r
        y = h_ref[...] + ah * g_ref[...]
        rows = i * tm + lax.broadcasted_iota(jnp.int32, (tm, 1), 0)
        err = jnp.where(rows >= BLOCK, y - t_ref[...], 0.0)
        dy = err / D
        dy_ref[...] = dy
        dah = dy * g_ref[...]
        da_ref[...] = (r * (dah - ah * jnp.mean(dah * ah, axis=-1, keepdims=True))).astype(da_ref.dtype)
        part = jnp.sum(jnp.sum(err * err, axis=1, keepdims=True), axis=0, keepdims=True)

        @pl.when(i == 0)
        def _():
            loss_ref[...] = jnp.zeros_like(loss_ref)
            dg_ref[...] = jnp.zeros_like(dg_ref)

        loss_ref[...] += jnp.broadcast_to(part, loss_ref.shape)
        dg_ref[...] += jnp.sum(dy * ah, axis=0, keepdims=True)

    row = pl.BlockSpec((tm, D), lambda i: (i, 0))
    vec = pl.BlockSpec((1, D), lambda i: (0, 0))
    return pl.pallas_call(
        body, out_shape=(jax.ShapeDtypeStruct((T, D), F32), jax.ShapeDtypeStruct((T, D), BF16),
                         jax.ShapeDtypeStruct((1, D), F32), jax.ShapeDtypeStruct((8, 128), F32)),
        grid=(T // tm,),
        in_specs=[row, vec, row, row],
        out_specs=(row, row, vec, pl.BlockSpec((8, 128), lambda i: (0, 0))),
        compiler_params=_params(("arbitrary",)), name=name)(a, g, h, target)


def _rms_bwd(x, g, dy, res, *, out_dtype, dy2=None, then=None, ex=None, name):
    T, D = x.shape
    tm = _tile(T, 512)
    has_res = res is not None
    has_dy2 = 2 if dy2 is not None else 0
    n_in = 3 + has_dy2 + has_res + (2 if then is not None else 0)
    n_out = 2 + (2 if then is not None else 0)

    def pull_back(x, g, dy):
        r = _rstd(x)
        xh = x * r
        dxh = dy * g
        return r * (dxh - xh * jnp.mean(dxh * xh, axis=-1, keepdims=True)), jnp.sum(dy * xh, axis=0, keepdims=True)

    def body(*refs):
        ins, outs = refs[:n_in], refs[n_in:]
        i = pl.program_id(0)

        @pl.when(i == 0)
        def _():
            for ref in outs[1::2]:
                ref[...] = jnp.zeros_like(ref)

        dy_all = ins[2][...].astype(F32)
        if has_dy2:
            dy_all = dy_all + lax.dot_general(ins[3][...], ins[4][...], NT, preferred_element_type=F32)
        dx, dg = pull_back(ins[0][...], ins[1][...], dy_all)
        if has_res:
            dx = dx + ins[3 + has_dy2][...]
        outs[0][...] = dx.astype(outs[0].dtype)
        outs[1][...] += dg
        if then is not None:
            dx2, dg2 = pull_back(ins[n_in - 2][...], ins[n_in - 1][...], dx)
            outs[2][...] = dx2.astype(outs[2].dtype)
            outs[3][...] += dg2

    row = pl.BlockSpec((tm, D), lambda i: (i, 0))
    vec = pl.BlockSpec((1, D), lambda i: (0, 0))
    ins = [x, g, dy] + (list(dy2) if has_dy2 else []) + ([res] if has_res else []) + (list(then) if then is not None else [])
    dy2_specs = ([pl.BlockSpec((tm, dy2[0].shape[1]), lambda i: (i, 0)), pl.BlockSpec(dy2[1].shape, lambda i: (0, 0))]
                 if has_dy2 else [])
    in_specs = [row, vec, row] + dy2_specs + ([row] if has_res else []) + ([row, vec] if then is not None else [])
    out_shape = [jax.ShapeDtypeStruct((T, D), out_dtype), jax.ShapeDtypeStruct((1, D), F32)]
    out_specs = [row, vec]
    if then is not None:
        out_shape += [jax.ShapeDtypeStruct((T, D), BF16), jax.ShapeDtypeStruct((1, D), F32)]
        out_specs += [row, vec]
    grid = (T // tm,)
    body, x_in, x_in_specs, x_out, x_out_specs, x_scr = _carry(ex, grid, n_in, n_out, body)
    return pl.pallas_call(
        body, out_shape=(*out_shape, *x_out), grid=grid,
        in_specs=in_specs + x_in_specs, out_specs=(*out_specs, *x_out_specs), scratch_shapes=x_scr,
        compiler_params=_params(("arbitrary",)), name=name)(*ins, *x_in)


def _gate_up_swiglu(a, w, *, name):
    T, D = a.shape
    S, n = w.shape[0] // 2, w.shape[2]
    tm = _tile(T, 1408, BLOCK)

    def body(a_ref, wg_ref, wu_ref, g_ref, u_ref, o_ref, ot_ref):
        x = a_ref[...]
        g = jnp.dot(x, wg_ref[...], preferred_element_type=F32)
        u = jnp.dot(x, wu_ref[...], preferred_element_type=F32)
        g16, u16 = g.astype(BF16), u.astype(BF16)
        g_ref[...] = g16
        u_ref[...] = u16
        gr = g16.astype(F32)
        act = gr / (1.0 + jnp.exp(-gr)) * u16.astype(F32)
        o_ref[...] = act.astype(o_ref.dtype)
        ot_ref[...] = act.T.astype(ot_ref.dtype)

    tile = pl.BlockSpec((tm, n), lambda i, j: (i, j))
    shp = jax.ShapeDtypeStruct((T, S * n), BF16)
    return pl.pallas_call(
        body, out_shape=(shp, shp, shp, jax.ShapeDtypeStruct((S * n, T), BF16)), grid=(T // tm, S),
        in_specs=[pl.BlockSpec((tm, D), lambda i, j: (i, 0)),
                  pl.BlockSpec((None, D, n), lambda i, j: (j, 0, 0)),
                  pl.BlockSpec((None, D, n), lambda i, j: (j + S, 0, 0))],
        out_specs=(tile, tile, tile, pl.BlockSpec((n, tm), lambda i, j: (j, i))),
        compiler_params=_params(("parallel", "parallel")), name=name)(a, w, w)


def _d_act_swiglu(dff, w_down, gate, up, *, name):
    T, D = dff.shape
    F = w_down.shape[0]
    tm = _tile(T, 384)
    tf = _tile(F, 768, BLOCK)

    def body(d_ref, w_ref, g_ref, u_ref, o_ref):
        dy = d_ref[...]
        for c in range(0, F, tf):
            d = lax.dot_general(dy, w_ref[c:c + tf, :], NT, preferred_element_type=F32)
            g = g_ref[:, c:c + tf].astype(F32)
            u = u_ref[:, c:c + tf].astype(F32)
            sg = 1.0 / (1.0 + jnp.exp(-g))
            o_ref[:, c:c + tf] = (d * u * (sg * (1.0 + g * (1.0 - sg)))).astype(o_ref.dtype)
            o_ref[:, F + c:F + c + tf] = (d * (g * sg)).astype(o_ref.dtype)

    row = pl.BlockSpec((tm, F), lambda i: (i, 0))
    return pl.pallas_call(
        body, out_shape=jax.ShapeDtypeStruct((T, 2 * F), BF16), grid=(T // tm,),
        in_specs=[pl.BlockSpec((tm, D), lambda i: (i, 0)), pl.BlockSpec((F, D), lambda i: (0, 0)), row, row],
        out_specs=pl.BlockSpec((tm, 2 * F), lambda i: (i, 0)),
        compiler_params=_params(("parallel",)), name=name)(dff, w_down, gate, up)


def _fox_gates_fwd(f_t, b, *, name):
    H, T = f_t.shape
    nb = T // BLOCK

    def body(f_ref, b_ref, cum_ref, col_ref):
        f = f_ref[...] + b_ref[...]
        ls = jnp.minimum(f, 0.0) - jnp.log(1.0 + jnp.exp(-jnp.abs(f)))
        t = lax.broadcasted_iota(jnp.int32, (H, T), 1)
        ls = jnp.where(t >= PAD_ROWS, ls, 0.0)
        upper = (lax.broadcasted_iota(jnp.int32, (BLOCK, BLOCK), 0)
                 <= lax.broadcasted_iota(jnp.int32, (BLOCK, BLOCK), 1)).astype(F32)
        carry = jnp.zeros((H, 1), F32)
        for blk in range(nb):
            seg = ls[:, blk * BLOCK:(blk + 1) * BLOCK]
            pre = jnp.dot(seg, upper, precision=HIGHEST, preferred_element_type=F32) + carry
            cum_ref[:, blk * BLOCK:(blk + 1) * BLOCK] = pre
            col_ref[blk * BLOCK:(blk + 1) * BLOCK, :] = jnp.concatenate(
                [pre, jnp.zeros((BLOCK - H, BLOCK), F32)], axis=0).T
            carry = pre[:, BLOCK - 1:BLOCK]

    vm = pl.BlockSpec(memory_space=pltpu.VMEM)
    return pl.pallas_call(
        body, out_shape=(jax.ShapeDtypeStruct((H, T), F32), jax.ShapeDtypeStruct((T, BLOCK), F32)),
        in_specs=[vm, vm], out_specs=(vm, vm),
        compiler_params=_params(), name=name)(f_t, b)


def _fox_gates_bwd(dcq, dck, f_t, b, *, name):
    H, T = f_t.shape
    nb = T // BLOCK

    def body(dq_ref, d_ref, f_ref, b_ref, df_ref, db_ref):
        lower = (lax.broadcasted_iota(jnp.int32, (BLOCK, BLOCK), 0)
                 >= lax.broadcasted_iota(jnp.int32, (BLOCK, BLOCK), 1)).astype(F32)
        carry = jnp.zeros((H, 1), F32)
        for blk in range(nb - 1, -1, -1):
            seg = dq_ref[:, blk * BLOCK:(blk + 1) * BLOCK] - d_ref[:, blk * BLOCK:(blk + 1) * BLOCK]
            suf = jnp.dot(seg, lower, precision=HIGHEST, preferred_element_type=F32) + carry
            df_ref[:, blk * BLOCK:(blk + 1) * BLOCK] = suf
            carry = suf[:, 0:1]
        f = f_ref[...] + b_ref[...]
        t = lax.broadcasted_iota(jnp.int32, (H, T), 1)
        df = jnp.where(t >= PAD_ROWS, df_ref[...] / (1.0 + jnp.exp(f)), 0.0)
        df_ref[...] = df
        db_ref[...] = jnp.sum(df, axis=1, keepdims=True)

    vm = pl.BlockSpec(memory_space=pltpu.VMEM)
    return pl.pallas_call(
        body, out_shape=(jax.ShapeDtypeStruct((H, T), F32), jax.ShapeDtypeStruct((H, 1), F32)),
        in_specs=[vm, vm, vm, vm], out_specs=(vm, vm),
        compiler_params=_params(), name=name)(dcq, dck, f_t, b)


LANE_KC = HEAD_DIM
LANE_QC = HEAD_DIM + 3
LANE_END = HEAD_DIM + 6


def _split3(c):
    hi = c.astype(BF16).astype(F32)
    r = c - hi
    mid = r.astype(BF16).astype(F32)
    lo = (r - mid).astype(BF16).astype(F32)
    return hi, mid, lo


def _lanes(lane, data, start, terms, rest):
    out = rest
    for i, t in enumerate(terms):
        out = jnp.where(lane == start + i, t, out)
    return jnp.where(lane < HEAD_DIM, data, out)


def _fox_prep(proj, cum_col, *, name):
    T = proj.shape[0]
    tm = FOX_TILE
    nt = T // tm
    H = FOX_HEADS
    lanes = 2 * HEAD_DIM
    first = (proj.shape[1] - 3 * H * HEAD_DIM) // lanes

    def body(q_ref, k_ref, v_ref, c_ref, qa_ref, ka_ref, va_ref):
        p = pl.program_id(0)
        i = pl.program_id(1)
        lane = lax.broadcasted_iota(jnp.int32, (tm, lanes), 1)
        rows = i * tm + lax.broadcasted_iota(jnp.int32, (tm, 1), 0)
        q2 = q_ref[...].astype(F32)
        k2 = k_ref[...].astype(F32)
        v2 = v_ref[...].astype(F32)
        cum = c_ref[...]
        for e in range(2):
            c = jnp.sum(jnp.where(lane == 2 * p + e, cum, 0.0), axis=1, keepdims=True)
            ck = jnp.where(rows >= PAD_ROWS, c, -NEG)
            qe, ke, ve = (q2, k2, v2) if e == 0 else tuple(pltpu.roll(a, HEAD_DIM, 1) for a in (q2, k2, v2))
            one = jnp.where(lane < LANE_END, 1.0, 0.0)
            qa = _lanes(lane, qe * SCALE, LANE_QC, _split3(c), jnp.where(lane < LANE_QC, -1.0, 0.0))
            ka = _lanes(lane, ke, LANE_KC, _split3(ck), one)
            va = jnp.where(lane < HEAD_DIM, ve, jnp.where(lane < LANE_QC, 1.0, 0.0))
            qa_ref[e] = qa.astype(BF16)
            ka_ref[e] = ka.astype(BF16)
            va_ref[e] = va.astype(BF16)

    pairs = FOX_GROUP // 2

    def col(part):
        return pl.BlockSpec((tm, lanes),
                            lambda p, i: (i, first + 3 * pairs * (p // pairs) + part * pairs + p % pairs))

    out = pl.BlockSpec((2, tm, lanes), lambda p, i: (p, i, 0))
    shp = jax.ShapeDtypeStruct((H, T, lanes), BF16)
    return pl.pallas_call(
        body, out_shape=(shp, shp, shp), grid=(H // 2, nt),
        in_specs=[col(0), col(1), col(2), pl.BlockSpec((tm, lanes), lambda p, i: (i, 0))],
        out_specs=(out, out, out),
        compiler_params=_params(("parallel", "parallel")), name=name)(proj, proj, proj, cum_col)


def _fox_fwd(q_aug, k_aug, v_aug, mix, *, ex=None, name):
    H, T, lanes = q_aug.shape
    tq = FOX_TILE
    nq = T // tq
    G = FOX_HEADS

    def body(q_ref, k_ref, v_ref, mix_ref, o_ref, lse_ref, m_scr, acc_scr):
        i = pl.program_id(1)
        m_scr[...] = jnp.full(m_scr.shape, NEG, F32)
        acc_scr[...] = jnp.zeros(acc_scr.shape, F32)

        def step(kb, diag):
            off = pl.multiple_of(kb * tq, tq)
            s_t = [lax.dot_general(k_ref[g, pl.ds(off, tq), :], q_ref[g], NT, preferred_element_type=F32)
                   for g in range(G)]
            if diag:
                r = lax.broadcasted_iota(jnp.int32, (tq, tq), 0)
                c = lax.broadcasted_iota(jnp.int32, (tq, tq), 1)
                s_t = [jnp.where(c >= r, s, NEG) for s in s_t]
            m_prev = [m_scr[g] for g in range(G)]
            m_new = [jnp.maximum(m_prev[g], jnp.max(s_t[g], axis=0, keepdims=True)) for g in range(G)]
            p_t = [jnp.exp(s_t[g] - m_new[g]).astype(BF16) for g in range(G)]
            pv = [lax.dot_general(v_ref[g, pl.ds(off, tq), :], p_t[g], TN, preferred_element_type=F32)
                  for g in range(G)]
            for g in range(G):
                acc_scr[g] = jnp.exp(m_prev[g] - m_new[g]) * acc_scr[g] + pv[g]
                m_scr[g] = m_new[g]

        def loop_body(kb, carry):
            step(kb, False)
            return carry

        lax.fori_loop(0, i, loop_body, 0)
        step(i, True)
        lane = lax.broadcasted_iota(jnp.int32, (tq, lanes), 1)
        outs = []
        for g in range(G):
            acc = acc_scr[g]
            lse_ref[g] = m_scr[g] + jnp.log(acc[HEAD_DIM:HEAD_DIM + 1, :])
            acc_t = acc.T
            outs.append(acc_t / acc_t[:, HEAD_DIM:HEAD_DIM + 1])
        for pair in range(G // 2):
            o_ref[:, pair * lanes:(pair + 1) * lanes] = jnp.where(
                lane < HEAD_DIM, outs[2 * pair], pltpu.roll(outs[2 * pair + 1], HEAD_DIM, 1)).astype(o_ref.dtype)

    blk = pl.BlockSpec((G, tq, lanes), lambda h, i: (h, i, 0))
    full = pl.BlockSpec((G, T, lanes), lambda h, i: (h, 0, 0))
    grid = (H // G, nq)
    first = mix.shape[1] // (G * HEAD_DIM) - H // G
    body, x_in, x_in_specs, x_out, x_out_specs, x_scr = _carry(ex, grid, 4, 2, body)
    return pl.pallas_call(
        body,
        out_shape=(jax.ShapeDtypeStruct(mix.shape, mix.dtype), jax.ShapeDtypeStruct((H, nq, 1, tq), F32), *x_out),
        grid=grid,
        in_specs=[blk, full, full, pl.BlockSpec(memory_space=pl.ANY)] + x_in_specs,
        out_specs=(pl.BlockSpec((tq, G * HEAD_DIM), lambda h, i: (i, first + h)),
                   pl.BlockSpec((G, None, 1, tq), lambda h, i: (h, i, 0, 0)), *x_out_specs),
        input_output_aliases={3: 0},
        scratch_shapes=[pltpu.VMEM((G, 1, tq), F32), pltpu.VMEM((G, lanes, tq), F32)] + x_scr,
        compiler_params=_params(("arbitrary", "arbitrary")), name=name)(q_aug, k_aug, v_aug, mix, *x_in)


def _fox_prep_bwd(dmix, mix, *, name):
    T = dmix.shape[0]
    H = FOX_HEADS
    tm = FOX_TILE
    lanes = 2 * HEAD_DIM
    first = mix.shape[1] // lanes - H // 2

    def body(d_ref, o_ref, da_ref):
        lane = lax.broadcasted_iota(jnp.int32, (tm, lanes), 1)
        d2 = d_ref[...].astype(F32)
        prod = d2 * o_ref[...].astype(F32)
        for e in range(2):
            de = d2 if e == 0 else pltpu.roll(d2, HEAD_DIM, 1)
            delta = jnp.sum(jnp.where(lane // HEAD_DIM == e, prod, 0.0), axis=1, keepdims=True)
            da_ref[e] = _lanes(lane, de, LANE_KC, _split3(-delta), jnp.zeros((), F32)).astype(BF16)

    pair = pl.BlockSpec((tm, lanes), lambda p, i: (i, first + p))
    return pl.pallas_call(
        body, out_shape=jax.ShapeDtypeStruct((H, T, lanes), BF16), grid=(H // 2, T // tm),
        in_specs=[pair, pair],
        out_specs=pl.BlockSpec((2, tm, lanes), lambda p, i: (p, i, 0)),
        compiler_params=_params(("parallel", "parallel")), name=name)(dmix, mix)


def _fox_bwd(q_aug, k_aug, v_aug, do_aug, lse_row, dproj, *, ex=None, name):
    H, T, lanes = q_aug.shape
    tq = FOX_TILE
    nq = T // tq
    G = FOX_GROUP

    def side_by_side(tiles, scale=None):
        lane = lax.broadcasted_iota(jnp.int32, tiles[0].shape, 1)
        out = [jnp.where(lane < HEAD_DIM, tiles[2 * p], pltpu.roll(tiles[2 * p + 1], HEAD_DIM, 1))
               for p in range(G // 2)]
        out = jnp.concatenate(out, axis=1)
        return out if scale is None else out * scale

    def body(q_ref, k_ref, v_ref, do_ref, lse_ref, dproj_in, out_ref, dcq_ref, dck_ref, dk_acc, dv_acc, dq_ref):
        j = pl.program_id(1)

        @pl.when(j == 0)
        def _():
            dq_ref[...] = jnp.zeros(dq_ref.shape, F32)
            dcq_ref[...] = jnp.zeros(dcq_ref.shape, F32)

        dk_acc[...] = jnp.zeros(dk_acc.shape, F32)
        dv_acc[...] = jnp.zeros(dv_acc.shape, F32)

        def step(qb, diag):
            off = pl.multiple_of(qb * tq, tq)
            heads = range(G)
            qa = [q_ref[g, pl.ds(off, tq), :] for g in heads]
            da = [do_ref[g, pl.ds(off, tq), :] for g in heads]
            s_t = [lax.dot_general(k_ref[g], qa[g], NT, preferred_element_type=F32) for g in heads]
            dp_t = [lax.dot_general(v_ref[g], da[g], NT, preferred_element_type=F32) for g in heads]
            p_t = [jnp.exp(s_t[g] - lse_ref[g, qb]) for g in heads]
            if diag:
                r = lax.broadcasted_iota(jnp.int32, (tq, tq), 0)
                c = lax.broadcasted_iota(jnp.int32, (tq, tq), 1)
                p_t = [jnp.where(c >= r, p, 0.0) for p in p_t]
            dsb = [(p_t[g] * dp_t[g]).astype(BF16) for g in heads]
            dv = [jnp.dot(p_t[g].astype(BF16), da[g], preferred_element_type=F32) for g in heads]
            dk = [jnp.dot(dsb[g], qa[g], preferred_element_type=F32) for g in heads]
            dq = [lax.dot_general(k_ref[g], dsb[g], TN, preferred_element_type=F32) for g in heads]
            for g in heads:
                dv_acc[g] += dv[g]
                dk_acc[g] += dk[g]
                dq_ref[g, qb] += dq[g]
                dcq_ref[g, qb] += jnp.sum(dsb[g].astype(F32), axis=0, keepdims=True)

        step(j, True)

        def loop_body(qb, carry):
            step(qb, False)
            return carry

        lax.fori_loop(j + 1, nq, loop_body, 0)
        dk = [dk_acc[g] for g in range(G)]
        out_ref[:, 0:wide] = side_by_side([dq_ref[g, j].T for g in range(G)], SCALE).astype(out_ref.dtype)
        out_ref[:, wide:2 * wide] = side_by_side(dk).astype(out_ref.dtype)
        out_ref[:, 2 * wide:3 * wide] = side_by_side([dv_acc[g] for g in range(G)]).astype(out_ref.dtype)
        for g in range(G):
            dck_ref[g] = -dk[g].T[LANE_KC:LANE_KC + 1, :]

    blk = pl.BlockSpec((G, tq, lanes), lambda h, j: (h, j, 0))
    full = pl.BlockSpec((G, T, lanes), lambda h, j: (h, 0, 0))
    wide = G * HEAD_DIM
    first = dproj.shape[1] // (3 * wide) - H // G
    grid = (H // G, nq)
    body, x_in, x_in_specs, x_out, x_out_specs, x_scr = _carry(ex, grid, 6, 3, body)
    rows = jax.ShapeDtypeStruct((H, nq, 1, tq), F32)
    all_rows = pl.BlockSpec((G, nq, 1, tq), lambda h, j: (h, 0, 0, 0))
    return pl.pallas_call(
        body,
        out_shape=(jax.ShapeDtypeStruct(dproj.shape, dproj.dtype), rows, rows, *x_out),
        grid=grid,
        in_specs=[full, blk, blk, full, all_rows, pl.BlockSpec(memory_space=pl.ANY)] + x_in_specs,
        out_specs=(pl.BlockSpec((tq, 3 * wide), lambda h, j: (j, first + h)), all_rows,
                   pl.BlockSpec((G, None, 1, tq), lambda h, j: (h, j, 0, 0)), *x_out_specs),
        input_output_aliases={5: 0},
        scratch_shapes=[pltpu.VMEM((G, tq, lanes), F32), pltpu.VMEM((G, tq, lanes), F32),
                        pltpu.VMEM((G, nq, lanes, tq), F32)] + x_scr,
        compiler_params=_params(("arbitrary", "arbitrary")), name=name,
    )(q_aug, k_aug, v_aug, do_aug, lse_row, dproj, *x_in)


def _t5_bucket_np(d):
    n = np.maximum(d, 0).astype(np.int32)
    max_exact = N_BUCKETS // 2
    nf = np.maximum(n, 1).astype(np.float32)
    large = max_exact + (np.log(nf / max_exact) / math.log(MAX_DISTANCE / max_exact)
                         * (N_BUCKETS - max_exact)).astype(np.int32)
    large = np.minimum(large, N_BUCKETS - 1)
    return np.where(n < max_exact, n, large)


def _bucket_onehots():
    k = np.arange(BLOCK)[:, None]
    q = np.arange(BLOCK)[None, :]
    eye = np.eye(N_BUCKETS, dtype=np.float32)
    cur = eye[_t5_bucket_np(q - k).reshape(-1)]
    prev = eye[_t5_bucket_np(BLOCK + q - k).reshape(-1)]
    return cur, prev


SWA_K_COL = SWA_Q_HEADS * HEAD_DIM // (2 * HEAD_DIM)
SWA_V_COL = SWA_K_COL + 1


def _swa_terms(raw, bc, bp, far, sink, n):
    k = lax.broadcasted_iota(jnp.int32, (BLOCK, BLOCK), 0)
    q = lax.broadcasted_iota(jnp.int32, (BLOCK, BLOCK), 1)
    never = 2 * BLOCK
    s_c = raw[0] + bc
    s_p = raw[1] + bp
    s_m = raw[2] + jnp.where(n == 1, bp, far)
    s_c = jnp.where((k <= q) & (k >= jnp.where(n >= 1, 0, PAD_ROWS)), s_c, NEG)
    s_p = jnp.where(k > q + jnp.where(n >= 2, 0, never), s_p, NEG)
    s_m = jnp.where(k >= jnp.where(n >= 1, PAD_ROWS, never), s_m, NEG)
    m = jnp.maximum(jnp.maximum(jnp.max(s_c, axis=0, keepdims=True), jnp.max(s_p, axis=0, keepdims=True)),
                    jnp.maximum(jnp.max(s_m, axis=0, keepdims=True), sink))
    e = [jnp.exp(s_c - m), jnp.exp(s_p - m), jnp.exp(s_m - m)]
    e_s = jnp.exp(sink - m)
    l = (jnp.sum(e[0], axis=0, keepdims=True) + jnp.sum(e[1], axis=0, keepdims=True)
         + jnp.sum(e[2], axis=0, keepdims=True) + e_s)
    return e, e_s, l


SWA_STEP = 3


def _swa_specs():
    R = SWA_STEP

    def window(col):
        return ([pl.BlockSpec((BLOCK, BLOCK), lambda s, w=w: (jnp.maximum(R * s - 1 + w, 0), col)) for w in range(R + 1)]
                + [pl.BlockSpec((BLOCK, BLOCK), lambda s: (0, col))])

    qblk = pl.BlockSpec((R * BLOCK, SWA_Q_HEADS * HEAD_DIM), lambda s: (s, 0))
    bias = pl.BlockSpec((SWA_Q_HEADS, BLOCK, BLOCK), lambda s: (0, 0, 0))
    smem = pl.BlockSpec(memory_space=pltpu.SMEM)
    return qblk, window(SWA_K_COL), window(SWA_V_COL), bias, smem


def _swa_own_kv(tile_ref, kv):
    lane = lax.broadcasted_iota(jnp.int32, (BLOCK, 2 * HEAD_DIM), 1)
    t = tile_ref[...].astype(F32)
    return jnp.where(lane // HEAD_DIM == kv, t, pltpu.roll(t, HEAD_DIM, 1)).astype(BF16)


def _swa_fwd(proj, bc, bp, far, sinks, *, name):
    T = proj.shape[0]
    nb = T // BLOCK
    G = SWA_GROUP
    Hq = SWA_Q_HEADS
    lanes = 2 * HEAD_DIM

    R = SWA_STEP
    assert nb % R == 0

    def body(*refs):
        q_ref, k_refs, v_refs = refs[0], refs[1:R + 3], refs[R + 3:2 * R + 5]
        bc_ref, bp_ref, far_ref, sink_ref, o_ref = refs[2 * R + 5:]
        s = pl.program_id(0)
        lane = lax.broadcasted_iota(jnp.int32, (BLOCK, lanes), 1)
        kvs = range(SWA_KV_HEADS)
        kk = [[_swa_own_kv(ref, kv) for ref in k_refs] for kv in kvs]
        vv = [[_swa_own_kv(ref, kv) for ref in v_refs] for kv in kvs]
        chains = [(r, h) for r in range(R) for h in range(Hq)]
        tiles = lambda r: (r + 1, r, R + 1)
        q2 = {(r, pair): q_ref[r * BLOCK:(r + 1) * BLOCK, pair * lanes:(pair + 1) * lanes].astype(F32) * SCALE
              for r in range(R) for pair in range(Hq // 2)}
        qm = {c: jnp.where(lane // HEAD_DIM == c[1] % 2, q2[c[0], c[1] // 2], 0.0).astype(BF16) for c in chains}
        raw = {c: [lax.dot_general(kk[c[1] // G][w], qm[c], NT, preferred_element_type=F32) for w in tiles(c[0])]
               for c in chains}
        terms = {c: _swa_terms(raw[c], bc_ref[c[1]], bp_ref[c[1]], far_ref[c[1]], sink_ref[c[1]], R * s + c[0])
                 for c in chains}
        o_t = {c: sum(lax.dot_general(vv[c[1] // G][w], terms[c][0][b].astype(BF16), TN, preferred_element_type=F32)
                      for b, w in enumerate(tiles(c[0]))) for c in chains}
        outs = {c: (o_t[c] / terms[c][2]).T for c in chains}
        for r in range(R):
            for pair in range(Hq // 2):
                o_ref[r * BLOCK:(r + 1) * BLOCK, pair * lanes:(pair + 1) * lanes] = jnp.where(
                    lane < HEAD_DIM, outs[r, 2 * pair], outs[r, 2 * pair + 1]).astype(o_ref.dtype)

    qblk, keys, vals, bias, smem = _swa_specs()
    return pl.pallas_call(
        body, out_shape=jax.ShapeDtypeStruct((T, D_MODEL), BF16), grid=(nb // R,),
        in_specs=[qblk] + keys + vals + [bias, bias, smem, smem],
        out_specs=qblk,
        compiler_params=_params(("parallel",)), name=name,
    )(proj, *([proj] * (2 * R + 4)), bc, bp, far, sinks)


def _swa_bwd(proj, dmix, bc, bp, far, sinks, *, ex=None, name):
    T, width = proj.shape
    nb = T // BLOCK
    G = SWA_GROUP
    Hq = SWA_Q_HEADS
    lanes = 2 * HEAD_DIM
    qw = Hq * HEAD_DIM
    own_w = qw + 2 * lanes

    R = SWA_STEP
    assert nb % R == 0
    n_in = 2 * R + 10

    def body(*refs):
        q_ref, k_refs, v_refs = refs[0], refs[1:R + 3], refs[R + 3:2 * R + 5]
        do_ref, bc_ref, bp_ref, far_ref, sink_ref = refs[2 * R + 5:n_in]
        dp_ref, dbc_ref, dbp_ref, dbf_ref, dsk_ref, dk_acc, dv_acc = refs[n_in:]
        s = pl.program_id(0)

        @pl.when(s == 0)
        def _():
            for ref in (dk_acc, dv_acc, dbc_ref, dbp_ref, dbf_ref, dsk_ref):
                ref[...] = jnp.zeros(ref.shape, F32)

        lane = lax.broadcasted_iota(jnp.int32, (BLOCK, lanes), 1)
        kvs = range(SWA_KV_HEADS)
        kk = [[_swa_own_kv(ref, kv) for ref in k_refs] for kv in kvs]
        vv = [[_swa_own_kv(ref, kv) for ref in v_refs] for kv in kvs]
        chains = [(r, h) for r in range(R) for h in range(Hq)]
        blocks = range(3)
        tiles = lambda r: (r + 1, r, R + 1)
        sub = lambda ref, r, pair: ref[r * BLOCK:(r + 1) * BLOCK, pair * lanes:(pair + 1) * lanes]
        q2 = {(r, pair): sub(q_ref, r, pair).astype(F32) * SCALE for r in range(R) for pair in range(Hq // 2)}
        d2 = {(r, pair): sub(do_ref, r, pair) for r in range(R) for pair in range(Hq // 2)}
        own = [lane // HEAD_DIM == half for half in range(2)]
        qm = {c: jnp.where(own[c[1] % 2], q2[c[0], c[1] // 2], 0.0).astype(BF16) for c in chains}
        dom = {c: jnp.where(own[c[1] % 2], d2[c[0], c[1] // 2], jnp.zeros_like(d2[0, 0])) for c in chains}
        raw = {c: [lax.dot_general(kk[c[1] // G][w], qm[c], NT, preferred_element_type=F32) for w in tiles(c[0])]
               for c in chains}
        dp = {c: [lax.dot_general(vv[c[1] // G][w], dom[c], NT, preferred_element_type=F32) for w in tiles(c[0])]
              for c in chains}
        p, ds16 = {}, {}
        for c in chains:
            r, h = c
            n = R * s + r
            e, e_s, l = _swa_terms(raw[c], bc_ref[h], bp_ref[h], far_ref[h], sink_ref[h], n)
            inv = 1.0 / l
            ph = [e[b] * inv for b in blocks]
            delta = sum(jnp.sum(ph[b] * dp[c][b], axis=0, keepdims=True) for b in blocks)
            ds = [ph[b] * (dp[c][b] - delta) for b in blocks]
            dsk_ref[h] += -(e_s * inv) * delta
            dbc_ref[h] += ds[0]
            dbp_ref[h] += ds[1] + jnp.where(n == 1, ds[2], 0.0)
            dbf_ref[h] += jnp.where(n >= 2, ds[2], 0.0)
            p[c] = [x.astype(BF16) for x in ph]
            ds16[c] = [x.astype(BF16) for x in ds]
        dq_t = {c: sum(lax.dot_general(kk[c[1] // G][w], ds16[c][b], TN, preferred_element_type=F32)
                       for b, w in enumerate(tiles(c[0]))) for c in chains}
        group = [range(kv * G, (kv + 1) * G) for kv in kvs]
        dk = {(r, kv): [sum(jnp.dot(ds16[r, h][b], qm[r, h], preferred_element_type=F32) for h in group[kv])
                        for b in blocks] for r in range(R) for kv in kvs}
        dv = {(r, kv): [sum(jnp.dot(p[r, h][b], dom[r, h], preferred_element_type=F32) for h in group[kv])
                        for b in blocks] for r in range(R) for kv in kvs}
        for r in range(R):
            n = R * s + r
            rows = pl.ds(pl.multiple_of(n * BLOCK, BLOCK), BLOCK)
            prev_rows = pl.ds(pl.multiple_of(jnp.maximum(n - 1, 0) * BLOCK, BLOCK), BLOCK)
            for pair in range(Hq // 2):
                dp_ref[rows, pair * lanes:(pair + 1) * lanes] = (jnp.where(
                    lane < HEAD_DIM, dq_t[r, 2 * pair].T, dq_t[r, 2 * pair + 1].T) * SCALE).astype(dp_ref.dtype)
            for acc, ref in ((dk, dk_acc), (dv, dv_acc)):
                tot = [[a + pltpu.roll(a, HEAD_DIM, 1) for a in acc[r, kv]] for kv in kvs]
                both = [jnp.where(lane < HEAD_DIM, tot[0][b], tot[1][b]) for b in blocks]
                ref[rows, :] += both[0]
                ref[prev_rows, :] += both[1]
                ref[0:BLOCK, :] += both[2]

        @pl.when(s == nb // R - 1)
        def _():
            dp_ref[:, qw:qw + lanes] = dk_acc[...].astype(dp_ref.dtype)
            dp_ref[:, qw + lanes:own_w] = dv_acc[...].astype(dp_ref.dtype)

    qblk, keys, vals, bias, smem = _swa_specs()
    dsk = pl.BlockSpec((Hq, 1, BLOCK), lambda s: (0, 0, 0))
    grid = (nb // R,)
    body, x_in, x_in_specs, x_out, x_out_specs, x_scr = _carry(ex, grid, n_in, 5, body)
    tile = jax.ShapeDtypeStruct((Hq, BLOCK, BLOCK), F32)
    return pl.pallas_call(
        body,
        out_shape=(jax.ShapeDtypeStruct((T, width), BF16), tile, tile, tile,
                   jax.ShapeDtypeStruct((Hq, 1, BLOCK), F32), *x_out),
        grid=grid,
        in_specs=[qblk] + keys + vals + [qblk, bias, bias, smem, smem] + x_in_specs,
        out_specs=(pl.BlockSpec((T, own_w), lambda s: (0, 0)), bias, bias, bias, dsk, *x_out_specs),
        scratch_shapes=[pltpu.VMEM((T, lanes), F32), pltpu.VMEM((T, lanes), F32)] + x_scr,
        compiler_params=_params(("arbitrary",)), name=name,
    )(proj, *([proj] * (2 * R + 4)), dmix, bc, bp, far, sinks, *x_in)


def _bias_tiles(tab_t, oh_cur_t, oh_prev_t, *, name):
    Hq = tab_t.shape[0]

    def body(t_ref, oc_ref, op_ref, bc_ref, bp_ref):
        bc_ref[...] = jnp.dot(t_ref[...], oc_ref[...], precision=HIGHEST, preferred_element_type=F32)
        bp_ref[...] = jnp.dot(t_ref[...], op_ref[...], precision=HIGHEST, preferred_element_type=F32)

    vm = pl.BlockSpec(memory_space=pltpu.VMEM)
    shp = jax.ShapeDtypeStruct((Hq, BLOCK * BLOCK), F32)
    bc, bp = pl.pallas_call(body, out_shape=(shp, shp), in_specs=[vm] * 3, out_specs=(vm, vm),
                            compiler_params=_params(), name=name)(tab_t, oh_cur_t, oh_prev_t)
    return bc.reshape(Hq, BLOCK, BLOCK), bp.reshape(Hq, BLOCK, BLOCK)


def _small_grads(dbc, dbp, dbf, dsk, oh_cur, oh_prev, *, name):
    Hq = dbc.shape[0]

    def body(dbc_ref, dbp_ref, dbf_ref, dsk_ref, oc_ref, op_ref, tab_ref, sink_ref):
        tab = (jnp.dot(dbc_ref[...], oc_ref[...], precision=HIGHEST, preferred_element_type=F32)
               + jnp.dot(dbp_ref[...], op_ref[...], precision=HIGHEST, preferred_element_type=F32))
        far = jnp.sum(dbf_ref[...], axis=1, keepdims=True)
        last = lax.broadcasted_iota(jnp.int32, (Hq, N_BUCKETS), 1) == N_BUCKETS - 1
        tab_ref[...] = tab + jnp.where(last, far, 0.0)
        sink_ref[...] = jnp.sum(dsk_ref[...], axis=1, keepdims=True)

    vm = pl.BlockSpec(memory_space=pltpu.VMEM)
    return pl.pallas_call(
        body, out_shape=(jax.ShapeDtypeStruct((Hq, N_BUCKETS), F32), jax.ShapeDtypeStruct((Hq, 1), F32)),
        in_specs=[vm] * 6, out_specs=(vm, vm), compiler_params=_params(), name=name,
    )(dbc.reshape(Hq, -1), dbp.reshape(Hq, -1), dbf.reshape(Hq, -1), dsk.reshape(Hq, -1), oh_cur, oh_prev)


def _coords():
    return lax.axis_index("x"), lax.axis_index("y"), lax.axis_index("c")


class _Exchange:
    def __init__(self, inputs, out_shapes, scratch, start, finish):
        self.inputs, self.out_shapes, self.scratch, self.start, self.finish = inputs, out_shapes, scratch, start, finish


def _carry(ex, grid, n_in, n_out, body):
    if ex is None:
        return body, [], [], [], [], []
    ni, no = len(ex.inputs), len(ex.out_shapes)

    def at_step(which):
        cond = None
        for axis, n in enumerate(grid):
            c = pl.program_id(axis) == (0 if which == "first" else n - 1)
            cond = c if cond is None else cond & c
        return cond

    def wrapped(*refs):
        refs = list(refs)
        n_own_scr = len(refs) - (n_in + ni + n_out + no) - len(ex.scratch)
        own_in, side_in = refs[:n_in], refs[n_in:n_in + ni]
        own_out = refs[n_in + ni:n_in + ni + n_out]
        side_out = refs[n_in + ni + n_out:n_in + ni + n_out + no]
        rest = refs[n_in + ni + n_out + no:]
        own_scr, sems = rest[:n_own_scr], rest[n_own_scr:]

        @pl.when(at_step("first"))
        def _():
            ex.start(side_in, side_out, sems)

        body(*own_in, *own_out, *own_scr)

        @pl.when(at_step("last"))
        def _():
            ex.finish(side_in, side_out, sems)

    hbm = pl.BlockSpec(memory_space=pl.ANY)
    return wrapped, list(ex.inputs), [hbm] * ni, list(ex.out_shapes), [hbm] * no, list(ex.scratch)


def _run_exchange(ex, *, name):
    ni, no = len(ex.inputs), len(ex.out_shapes)

    def body(*refs):
        ins, outs, sems = refs[:ni], refs[ni:ni + no], refs[ni + no:]
        ex.start(ins, outs, sems)
        ex.finish(ins, outs, sems)

    hbm = pl.BlockSpec(memory_space=pl.ANY)
    return pl.pallas_call(
        body, out_shape=tuple(ex.out_shapes), in_specs=[hbm] * ni, out_specs=tuple([hbm] * no),
        scratch_shapes=ex.scratch, compiler_params=_params(), name=name)(*ex.inputs)


def _gather_exchange(shards):
    nt = len(shards)

    def copies(ins, outs, sems):
        send_sems, recv_sems, local_sems = sems
        x, y, c = _coords()
        me, sibling = (x, y, c), (x, y, 1 - c)
        chips = [(1 - x, y), (x, 1 - y), (1 - x, 1 - y)]

        def slot(t, dev):
            return outs[t].at[4 * dev[0] + 2 * dev[1] + dev[2]]

        def copy(t, k, block, to, src=None):
            dst = slot(t, block)
            return pltpu.make_async_remote_copy(
                src_ref=dst if src is None else src, dst_ref=dst,
                send_sem=send_sems.at[t, k], recv_sem=recv_sems.at[t, k], device_id=to, device_id_type=MESH)

        mine = [pltpu.make_async_copy(ins[t], slot(t, me), local_sems.at[t]) for t in range(nt)]
        first = []
        for t in range(nt):
            first.append(copy(t, 0, me, sibling, src=ins[t]))
            first += [copy(t, 1 + j, me, (*chip, c), src=ins[t]) for j, chip in enumerate(chips)]
        return copy, mine, first, me, sibling, chips, c

    def start(ins, outs, sems):
        _, mine, first, *_ = copies(ins, outs, sems)
        for cp in mine + first:
            cp.start()

    def finish(ins, outs, sems):
        copy, mine, first, me, sibling, chips, c = copies(ins, outs, sems)
        passed = []
        for j, chip in enumerate(chips):
            for t in range(nt):
                copy(t, 1 + j, (*chip, c), me).wait_recv()
                cp = copy(t, 4 + j, (*chip, c), sibling)
                cp.start()
                passed.append(cp)
        for t in range(nt):
            copy(t, 0, sibling, me).wait_recv()
            for j, chip in enumerate(chips):
                copy(t, 4 + j, (*chip, 1 - c), me).wait_recv()
        for cp in first + passed:
            cp.wait_send()
        for cp in mine:
            cp.wait()

    return _Exchange(
        list(shards), [jax.ShapeDtypeStruct((N_DEV,) + s.shape, s.dtype) for s in shards],
        [pltpu.SemaphoreType.DMA((nt, 7)), pltpu.SemaphoreType.DMA((nt, 7)), pltpu.SemaphoreType.DMA((nt,))],
        start, finish)


def _swap_exchange(arrays, n_slices, copies):
    nt = len(arrays)

    def start(ins, outs, sems):
        for cp in copies(ins, outs, sems):
            cp.start()

    def finish(ins, outs, sems):
        sends = copies(ins, outs, sems)
        for cp in sends:
            cp.wait_recv()
        for cp in sends:
            cp.wait_send()

    return _Exchange(
        list(arrays), [jax.ShapeDtypeStruct((n_slices,) + a.shape[1:], a.dtype) for a in arrays],
        [pltpu.SemaphoreType.DMA((nt, n_slices)), pltpu.SemaphoreType.DMA((nt, n_slices))], start, finish)


def _cores_exchange(gs):
    def copies(ins, outs, sems):
        send_sems, recv_sems = sems
        x, y, c = _coords()
        return [pltpu.make_async_remote_copy(
            src_ref=ins[t].at[2 * j + (1 - c)], dst_ref=outs[t].at[j],
            send_sem=send_sems.at[t, j], recv_sem=recv_sems.at[t, j], device_id=(x, y, 1 - c), device_id_type=MESH)
            for t in range(len(gs)) for j in range(4)]

    return _swap_exchange(gs, 4, copies)


def _chips_exchange(ps):
    def copies(ins, outs, sems):
        send_sems, recv_sems = sems
        x, y, c = _coords()
        peers = [(1 - x, y), (x, 1 - y), (1 - x, 1 - y)]
        return [pltpu.make_async_remote_copy(
            src_ref=ins[t].at[2 * px + py], dst_ref=outs[t].at[k],
            send_sem=send_sems.at[t, k], recv_sem=recv_sems.at[t, k], device_id=(px, py, c), device_id_type=MESH)
            for t in range(len(ps)) for k, (px, py) in enumerate(peers)]

    return _swap_exchange(ps, 3, copies)


def _add_cores(g, r, core, *, name):
    _, A, B = g.shape
    ta = _tile(A, 512, 16)

    def body(core_ref, a_ref, b_ref, o16_ref):
        o16_ref[...] = (a_ref[...] + b_ref[...]).astype(BF16)

    blk = (None, ta, B)
    return pl.pallas_call(
        body, out_shape=jax.ShapeDtypeStruct((4, A, B), BF16),
        grid_spec=pltpu.PrefetchScalarGridSpec(
            num_scalar_prefetch=1, grid=(4, A // ta),
            in_specs=[pl.BlockSpec(blk, lambda j, i, core_ref: (2 * j + core_ref[0], i, 0)),
                      pl.BlockSpec(blk, lambda j, i, core_ref: (j, i, 0))],
            out_specs=pl.BlockSpec(blk, lambda j, i, core_ref: (j, i, 0))),
        compiler_params=_params(("parallel", "parallel")), name=name)(core, g, r)


def _adamw_math(w, g, m, v):
    m = ADAM_B1 * m + (1.0 - ADAM_B1) * g
    v = ADAM_B2 * v + (1.0 - ADAM_B2) * (g * g)
    m_hat = m / (1.0 - ADAM_B1 ** ADAM_STEP)
    v_hat = v / (1.0 - ADAM_B2 ** ADAM_STEP)
    delta = -ADAM_LR * (m_hat / (jnp.sqrt(v_hat) + ADAM_EPS) + ADAM_WD * w)
    return delta, m, v


def _sum_adamw(mine, sib, r, where, w, m, v, *, segs, ta, name):
    Aw, Bw = w.shape
    Bg = mine.shape[2]
    assert Aw % ta == 0

    def body(where_ref, p_ref, s_ref, r0, r1, r2, w_ref, m_ref, v_ref, g_out, d_out, m_out, v_out):
        for gc, wc, n in segs:
            g = (((p_ref[:, gc:gc + n] + s_ref[:, gc:gc + n]) + r0[:, gc:gc + n].astype(F32))
                 + r1[:, gc:gc + n].astype(F32)) + r2[:, gc:gc + n].astype(F32)
            delta, m_new, v_new = _adamw_math(w_ref[:, wc:wc + n], g, m_ref[:, wc:wc + n], v_ref[:, wc:wc + n])
            g_out[:, wc:wc + n] = g
            d_out[:, wc:wc + n] = delta
            m_out[:, wc:wc + n] = m_new
            v_out[:, wc:wc + n] = v_new

    gblk = (None, ta, Bg)
    row = pl.BlockSpec((ta, Bw), lambda i, where_ref: (i, 0))
    rspecs = [pl.BlockSpec(gblk, (lambda i, where_ref, k=k: (k, i, 0))) for k in range(3)]
    shp = jax.ShapeDtypeStruct((Aw, Bw), F32)
    return pl.pallas_call(
        body, out_shape=(shp, shp, shp, shp),
        grid_spec=pltpu.PrefetchScalarGridSpec(
            num_scalar_prefetch=1, grid=(Aw // ta,),
            in_specs=[pl.BlockSpec(gblk, lambda i, where_ref: (2 * where_ref[0] + where_ref[1], i, 0)),
                      pl.BlockSpec(gblk, lambda i, where_ref: (where_ref[0], i, 0))] + rspecs + [row, row, row],
            out_specs=(row, row, row, row)),
        compiler_params=_params(("parallel",)), name=name)(where, mine, sib, r, r, r, w, m, v)


def _adamw(w, g, m, v, *, name):
    def body(w_ref, g_ref, m_ref, v_ref, d_out, m_out, v_out):
        delta, m_new, v_new = _adamw_math(w_ref[...], g_ref[...], m_ref[...], v_ref[...])
        d_out[...] = delta
        m_out[...] = m_new
        v_out[...] = v_new

    vm = pl.BlockSpec(memory_space=pltpu.VMEM)
    shp = jax.ShapeDtypeStruct(w.shape, F32)
    return pl.pallas_call(body, out_shape=(shp, shp, shp), in_specs=[vm] * 4, out_specs=(vm, vm, vm),
                          compiler_params=_params(), name=name)(w, g, m, v)


def _small_allreduce_adamw(s, w, m, v, *, name):
    R, W = s.shape

    def body(s_ref, w_ref, m_ref, v_ref, g_out, d_out, m_out, v_out, gath, send_sems, recv_sems):
        x, y, c = _coords()
        mine = 4 * x + 2 * y + c
        gath[mine] = s_ref[...]
        peers = [((1 - x) if k & 4 else x, (1 - y) if k & 2 else y, (1 - c) if k & 1 else c) for k in range(1, N_DEV)]
        sends = []
        for k in range(1, N_DEV):
            peer = peers[k - 1]
            sends.append(pltpu.make_async_remote_copy(
                src_ref=s_ref, dst_ref=gath.at[mine], send_sem=send_sems.at[k - 1], recv_sem=recv_sems.at[k - 1],
                device_id=peer, device_id_type=MESH))
        for cp in sends:
            cp.start()
        for k in range(1, N_DEV):
            peer = peers[k - 1]
            pltpu.make_async_remote_copy(
                src_ref=s_ref, dst_ref=gath.at[4 * peer[0] + 2 * peer[1] + peer[2]],
                send_sem=send_sems.at[k - 1], recv_sem=recv_sems.at[k - 1],
                device_id=peer, device_id_type=MESH).wait_recv()
        for cp in sends:
            cp.wait_send()
        g = gath[0]
        for d in range(1, N_DEV):
            g = g + gath[d]
        delta, m_new, v_new = _adamw_math(w_ref[...], g, m_ref[...], v_ref[...])
        g_out[...] = g
        d_out[...] = delta
        m_out[...] = m_new
        v_out[...] = v_new

    vm = pl.BlockSpec(memory_space=pltpu.VMEM)
    shp = jax.ShapeDtypeStruct((R, W), F32)
    return pl.pallas_call(
        body, out_shape=(shp, shp, shp, shp), in_specs=[vm] * 4, out_specs=(vm, vm, vm, vm),
        scratch_shapes=[pltpu.VMEM((N_DEV, R, W), F32), pltpu.SemaphoreType.DMA((N_DEV - 1,)),
                        pltpu.SemaphoreType.DMA((N_DEV - 1,))],
        compiler_params=_params(), name=name)(s, w, m, v)


def _pack_small(rel_bias, g1, g2, g3, g4, b_forget, sinks, extra=None, meta=None):
    misc = jnp.concatenate([rel_bias.reshape(-1), b_forget.reshape(-1), sinks.reshape(-1)])
    misc = jnp.concatenate([misc, jnp.zeros((D_MODEL - misc.shape[0],), F32)])[None]
    last = jnp.zeros((1, D_MODEL), F32) if extra is None else extra
    meta = jnp.zeros((N_META, D_MODEL), F32) if meta is None else meta
    return jnp.concatenate([g1, g2, g3, g4, misc, last, jnp.zeros((2, D_MODEL), F32), meta], axis=0)


def _unpack_small(p):
    nrb = N_BUCKETS * SWA_Q_HEADS
    misc = p[4]
    return dict(rel_bias=misc[:nrb].reshape(N_BUCKETS, SWA_Q_HEADS), ln_pre_mix=p[0:1], ln_post_mix=p[1:2],
                ln_pre_ffn=p[2:3], ln_post_ffn=p[3:4], b_forget=misc[nrb:nrb + 8].reshape(1, 8),
                sinks=misc[nrb + 8:nrb + 16].reshape(1, 8))


def _proj_runs():
    gw = FOX_GROUP * HEAD_DIM
    swa = SWA_Q_W + 2 * SWA_KV_HEADS * HEAD_DIM
    runs = [(0, swa)]
    for grp in range(FOX_HEADS // FOX_GROUP):
        runs += [(swa + part * FOX_W + grp * gw, swa + part * FOX_W + (grp + 1) * gw) for part in range(3)]
    return runs


def _columns_from_shards(gathered, runs, shard):
    pieces = []
    for start, stop in runs:
        for d in range(start // shard, (stop - 1) // shard + 1):
            lo = d * shard
            pieces.append(gathered[d][:, max(start, lo) - lo:min(stop, lo + shard) - lo])
    return jnp.concatenate(pieces, axis=1)


def _device_shards(qkv, gate, shard, padded):
    pos, segments = 0, []
    for start, stop in _proj_runs():
        segments.append((start, stop, qkv, pos))
        pos += stop - start
    segments.append((pos, pos + gate.shape[1], gate, 0))
    total = pos + gate.shape[1]
    assert total % shard == 0
    zeros = jnp.zeros((qkv.shape[0], padded - shard), qkv.dtype)
    out = []
    for d in range(total // shard):
        lo, hi = d * shard, (d + 1) * shard
        pieces = [arr[:, src + max(lo, s) - s:src + min(hi, e) - s]
                  for s, e, arr, src in sorted(segments, key=lambda seg: seg[0]) if max(lo, s) < min(hi, e)]
        out.append(jnp.concatenate(pieces + [zeros], axis=1))
    return jnp.stack(out)


def kernel(x, meta_tokens, rel_bias, ln_pre_mix, ln_post_mix, ln_pre_ffn, ln_post_ffn, w_in, b_forget, sinks, w_out, w_gate_up, w_down, loss_target, m_meta_tokens, m_rel_bias, m_ln_pre_mix, m_ln_post_mix, m_ln_pre_ffn, m_ln_post_ffn, m_w_in, m_b_forget, m_sinks, m_w_out, m_w_gate_up, m_w_down, v_meta_tokens, v_rel_bias, v_ln_pre_mix, v_ln_post_mix, v_ln_pre_ffn, v_ln_post_ffn, v_w_in, v_b_forget, v_sinks, v_w_out, v_w_gate_up, v_w_down):
    seq = x.shape[1]
    T = BLOCK + seq
    assert T % FOX_TILE == 0
    nq = T // FOX_TILE
    tm = _tile(T, 1056)
    cin = w_in.shape[2]
    hid = w_down.shape[1]
    assert w_gate_up.shape[2] == 2 * hid and cin <= W_IN_PAD and hid <= HID_PAD

    x_i, y_i, c_i = _coords()
    core = jnp.reshape(c_i, (1,)).astype(jnp.int32)
    where = jnp.stack([2 * x_i + y_i, c_i]).astype(jnp.int32)
    w_in_s = jnp.pad(w_in[0].astype(BF16), ((0, 0), (0, W_IN_PAD - cin)))
    w_gu_s = jnp.pad(w_gate_up[0].astype(BF16).reshape(D_MODEL, 2, hid), ((0, 0), (0, 0), (0, HID_PAD - hid)))
    w_gu_s = w_gu_s.reshape(D_MODEL, 2 * HID_PAD)
    w_down_s = jnp.pad(w_down[0].astype(BF16), ((0, HID_PAD - hid), (0, 0)))
    g_in, g_meta = _run_exchange(_gather_exchange([w_in_s, meta_tokens]), name="ag_w_in")
    gather_rest = _gather_exchange([w_out[0].astype(BF16), w_gu_s, w_down_s])
    w_qkv = _columns_from_shards(g_in, _proj_runs(), cin)
    w_f = jnp.pad(_columns_from_shards(g_in, [(D_QKV, D_PROJ)], cin), ((0, 0), (0, BLOCK - FOX_HEADS)))
    meta_full = g_meta.transpose(1, 0, 2).reshape(N_META, D_MODEL)

    h0 = jnp.concatenate([jnp.zeros((PAD_ROWS, D_MODEL), F32), meta_full, x[0]], axis=0)
    target = jnp.concatenate([jnp.zeros((BLOCK, D_MODEL), F32), loss_target[0]], axis=0)
    hn1, hn1_t = _rms_fwd(h0, ln_pre_mix, name="rms_pre_mix")
    proj = _matmul(hn1, w_qkv, out_dtype=BF16, tm=tm, tn=D_QKV, name="mm_in_proj")
    proj_f = _matmul(hn1, w_f, out_dtype=F32, tm=tm, tn=BLOCK, name="mm_in_proj_f")

    f_t = proj_f[:, :FOX_HEADS].T
    bf_col = b_forget.reshape(FOX_HEADS, 1)

    oh_cur, oh_prev = _bucket_onehots()
    bias_c, bias_p = _bias_tiles(rel_bias.T, jnp.asarray(oh_cur.T), jnp.asarray(oh_prev.T), name="bias_tiles")
    far = rel_bias[N_BUCKETS - 1]
    sink_v = sinks[0]
    mix_a = _swa_fwd(proj, bias_c, bias_p, far, sink_v, name="swa_fwd")

    _, cum_col = _fox_gates_fwd(f_t, bf_col, name="fox_gates_fwd")
    q_b, k_b, v_b = _fox_prep(proj, cum_col, name="fox_prep")
    mix, lse_row, g_out, g_gu, g_down = _fox_fwd(q_b, k_b, v_b, mix_a, ex=gather_rest, name="fox_fwd")
    w_out_full = g_out.reshape(D_MODEL, D_MODEL)
    w_down_full = g_down.reshape(N_DEV * HID_PAD, D_MODEL)

    a1 = _matmul(mix, w_out_full, out_dtype=F32, tm=tm, tn=D_MODEL, name="mm_out_proj")
    h1, hn2, hn2_t = _post_res_norm(a1, ln_post_mix, h0, ln_pre_ffn, name="post_mix_pre_ffn")
    gate, up, act, act_t = _gate_up_swiglu(hn2, g_gu, name="mm_gate_up")
    ff = _matmul(act, w_down_full, out_dtype=F32, tm=tm, tn=512, name="mm_down")
    dh2, dff, dg_post_ffn, loss_acc = _loss_head(ff, ln_post_ffn, h1, target, name="loss_head")

    dgu = _d_act_swiglu(dff, w_down_full, gate, up, name="mm_d_act")
    d_w_down = _matmul(act_t, dff, out_dtype=F32, tm=768, tn=512, name="mm_dw_down")
    dhn2 = _matmul(dgu, g_gu, nt=True, b_shards=True, out_dtype=F32, tm=tm, tn=512, name="mm_d_hn2")
    d_w_gu = _matmul(hn2_t, dgu, out_shards=True, out_dtype=F32, tm=512, tn=2 * HID_PAD, name="mm_dw_gate_up")
    dh1, dg_pre_ffn, da1, dg_post_mix = _rms_bwd(h1, ln_pre_ffn, dhn2, dh2, out_dtype=F32,
                                                 then=(a1, ln_post_mix), name="rms_bwd_pre_ffn_post_mix")
    dmix = _matmul(da1, w_out_full, nt=True, out_dtype=BF16, tm=tm, tn=D_MODEL, name="mm_d_mix")
    d_w_out = _matmul(mix, da1, ta=True, out_dtype=F32, tm=512, tn=D_MODEL, name="mm_dw_out")

    ffn_grads = [d_w_out.reshape(N_DEV, -1, D_MODEL), d_w_gu, d_w_down.reshape(N_DEV, HID_PAD, D_MODEL)]
    dproj_a, dbc, dbp, dbf, dsk, *ffn_sibling = _swa_bwd(
        proj, dmix, bias_c, bias_p, far, sink_v, ex=_cores_exchange(ffn_grads), name="swa_bwd")
    d_tab, d_sink = _small_grads(dbc, dbp, dbf, dsk, jnp.asarray(oh_cur), jnp.asarray(oh_prev), name="small_grads")
    ffn_sums = [_add_cores(g, r, core, name="rs_add_" + t)
                for g, r, t in zip(ffn_grads, ffn_sibling, ["w_out", "w_gate_up", "w_down"])]

    do_b = _fox_prep_bwd(dmix, mix, name="fox_prep_bwd")
    dproj, dcq, dck, *ffn_chips = _fox_bwd(
        q_b, k_b, v_b, do_b, lse_row, dproj_a, ex=_chips_exchange(ffn_sums), name="fox_bwd")
    df_t, d_bf = _fox_gates_bwd(dcq.reshape(FOX_HEADS, T), dck.reshape(FOX_HEADS, T), f_t, bf_col,
                                name="fox_gates_bwd")
    df = jnp.pad(df_t.T.astype(BF16), ((0, 0), (0, BLOCK - FOX_HEADS)))

    d_w_qkv = _matmul(hn1_t, dproj, out_dtype=F32, tm=512, tn=768, name="mm_dw_in")
    d_w_f = _matmul(hn1_t, df, out_dtype=F32, tm=512, tn=BLOCK, name="mm_dw_in_f")
    d_w_in = _device_shards(d_w_qkv, d_w_f[:, :FOX_HEADS], cin, W_IN_PAD)
    dhn1, in_sibling = _matmul(dproj, w_qkv, nt=True, out_dtype=F32, tm=tm, tn=512,
                               ex=_cores_exchange([d_w_in]), name="mm_d_hn1")
    in_sum = _add_cores(d_w_in, in_sibling, core, name="rs_add_w_in")
    dh0, dg_pre_mix, in_chips = _rms_bwd(h0, ln_pre_mix, dhn1, dh1, out_dtype=F32, dy2=(df, w_f),
                                         ex=_chips_exchange([in_sum]), name="rms_bwd_pre_mix")
    grad_x = dh0[BLOCK:][None]
    d_meta = dh0[PAD_ROWS:BLOCK]

    tags = ["w_in", "w_out", "w_gate_up", "w_down"]
    mine = [d_w_in] + ffn_grads
    from_sibling = [in_sibling] + list(ffn_sibling)
    from_chips = [in_chips] + list(ffn_chips)
    shard_w = [(w_in, m_w_in, v_w_in), (w_out, m_w_out, v_w_out), (w_gate_up, m_w_gate_up, v_w_gate_up),
               (w_down, m_w_down, v_w_down)]
    segs = [[(0, 0, cin)], [(0, 0, D_MODEL)], [(0, 0, hid), (HID_PAD, hid, hid)], [(0, 0, D_MODEL)]]
    tas = [256, BLOCK, 256, hid]
    big = [{}, {}, {}, {}]
    for i, t in enumerate(tags):
        w_t, m_t, v_t = shard_w[i]
        res = _sum_adamw(mine[i], from_sibling[i], from_chips[i], where, w_t[0], m_t[0], v_t[0], segs=segs[i],
                         ta=tas[i], name="rs_adamw_" + t)
        for kind in range(4):
            big[kind][t] = res[kind][None]

    loss_row = jnp.pad(loss_acc[0:1, 0:1] * (0.5 / D_MODEL), ((0, 0), (0, D_MODEL - 1)))
    s_small = _pack_small(d_tab.T, dg_pre_mix, dg_post_mix, dg_pre_ffn, dg_post_ffn, d_bf, d_sink,
                          extra=loss_row, meta=d_meta)
    w_s = _pack_small(rel_bias, ln_pre_mix, ln_post_mix, ln_pre_ffn, ln_post_ffn, b_forget, sinks)
    m_s = _pack_small(m_rel_bias, m_ln_pre_mix, m_ln_post_mix, m_ln_pre_ffn, m_ln_post_ffn, m_b_forget, m_sinks)
    v_s = _pack_small(v_rel_bias, v_ln_pre_mix, v_ln_post_mix, v_ln_pre_ffn, v_ln_post_ffn, v_b_forget, v_sinks)
    small = _small_allreduce_adamw(s_small, w_s, m_s, v_s, name="small_allreduce_adamw")
    loss = small[0][5, 0]
    mcols = meta_tokens.shape[1]
    g_meta_mine = lax.dynamic_slice(small[0][8:8 + N_META], (0, (4 * x_i + 2 * y_i + c_i) * mcols), (N_META, mcols))
    big[0]["meta_tokens"] = g_meta_mine
    for kind, arr in enumerate(_adamw(meta_tokens, g_meta_mine, m_meta_tokens, v_meta_tokens, name="adamw_meta")):
        big[kind + 1]["meta_tokens"] = arr
    small = [_unpack_small(p) for p in small]

    names = ["meta_tokens", "rel_bias", "ln_pre_mix", "ln_post_mix", "ln_pre_ffn", "ln_post_ffn", "w_in",
             "b_forget", "sinks", "w_out", "w_gate_up", "w_down"]
    outs = [loss, grad_x]
    for kind in range(4):
        for nme in names:
            outs.append(big[kind][nme] if nme in big[kind] else small[kind][nme])
    return tuple(outs)
```

```python
import math

import numpy as np
import jax
import jax.numpy as jnp
from jax import lax
from jax.experimental import pallas as pl
from jax.experimental.pallas import tpu as pltpu

F32 = jnp.float32
BF16 = jnp.bfloat16
HIGHEST = lax.Precision.HIGHEST
MESH = pl.DeviceIdType.MESH

N_DEV = 8
D_MODEL = 1024
N_META = 16
HEAD_DIM = 64
SWA_Q_HEADS = 8
SWA_KV_HEADS = 2
SWA_GROUP = 4
FOX_HEADS = 8
FOX_W = FOX_HEADS * HEAD_DIM
SWA_Q_W = SWA_Q_HEADS * HEAD_DIM
BLOCK = 128
PAD_ROWS = BLOCK - N_META
N_BUCKETS = 32
MAX_DISTANCE = 128
D_FF = 2816
D_QKV = 2304
D_PROJ = D_QKV + FOX_HEADS
D_PROJ_PAD = 2560
EPS = 1e-6
NEG = -1e30
SCALE = HEAD_DIM ** -0.5
ADAM_LR, ADAM_B1, ADAM_B2, ADAM_EPS, ADAM_WD, ADAM_STEP = 0.001, 0.9, 0.999, 1e-08, 0.01, 10
VMEM_LIMIT = 56 * 1024 * 1024
FOX_TILE = 384
FOX_GROUP = 4
W_IN_PAD = 384
HID_PAD = 384

NT = (((1,), (1,)), ((), ()))
NN = (((1,), (0,)), ((), ()))
TN = (((0,), (0,)), ((), ()))


def _params(sem=None, **kw):
    if sem is not None:
        kw["dimension_semantics"] = sem
    return pltpu.CompilerParams(vmem_limit_bytes=VMEM_LIMIT, **kw)


def _tile(n, target, mult=16):
    best = None
    for t in range(mult, min(n, target) + 1, mult):
        if n % t == 0:
            best = t
    assert best is not None, (n, target)
    return best


def _matmul(a, b, *, nt=False, ta=False, b_shards=False, out_shards=False, out_dtype, tm, tn=None, tk=None,
            ex=None, name):
    M, K = a.shape[::-1] if ta else a.shape
    assert not (ta and (nt or b_shards))
    k_shards = b.shape[0] if (b_shards and nt) else 0
    if k_shards:
        N, ks = b.shape[1], b.shape[2]
        assert tk is None and K == k_shards * ks
    elif b_shards:
        N, tn = b.shape[0] * b.shape[2], b.shape[2]
    else:
        N = b.shape[0] if nt else b.shape[1]
    tk = K if tk is None else tk
    assert M % tm == 0 and N % tn == 0 and K % tk == 0, (name, a.shape, b.shape, tm, tn, tk)
    nk = K // tk
    dn = NT if nt else (TN if ta else NN)
    a_spec = pl.BlockSpec((tk, tm), lambda i, j, k: (k, i)) if ta else pl.BlockSpec((tm, tk), lambda i, j, k: (i, k))

    def body(a_ref, b_ref, o_ref, *scr):
        if k_shards:
            part = sum(lax.dot_general(a_ref[:, s * ks:(s + 1) * ks], b_ref[s], NT, preferred_element_type=F32)
                       for s in range(k_shards))
        else:
            part = lax.dot_general(a_ref[...], b_ref[...], dn, preferred_element_type=F32)
        if nk == 1:
            o_ref[...] = part.astype(o_ref.dtype)
        else:
            acc = scr[0]
            k = pl.program_id(2)

            @pl.when(k == 0)
            def _():
                acc[...] = part

            @pl.when(k > 0)
            def _():
                acc[...] += part

            @pl.when(k == nk - 1)
            def _():
                o_ref[...] = acc[...].astype(o_ref.dtype)

    if k_shards:
        b_spec = pl.BlockSpec((k_shards, tn, ks), lambda i, j, k: (0, j, 0))
    elif b_shards:
        b_spec = pl.BlockSpec((None, tk, tn), lambda i, j, k: (j, k, 0))
    elif nt:
        b_spec = pl.BlockSpec((tn, tk), lambda i, j, k: (j, k))
    else:
        b_spec = pl.BlockSpec((tk, tn), lambda i, j, k: (k, j))
    if out_shards:
        out_shape = jax.ShapeDtypeStruct((N // tn, M, tn), out_dtype)
        out_spec = pl.BlockSpec((None, tm, tn), lambda i, j, k: (j, i, 0))
    else:
        out_shape = jax.ShapeDtypeStruct((M, N), out_dtype)
        out_spec = pl.BlockSpec((tm, tn), lambda i, j, k: (i, j))
    grid = (M // tm, N // tn, nk)
    body, x_in, x_in_specs, x_out, x_out_specs, x_scr = _carry(ex, grid, 2, 1, body)
    res = pl.pallas_call(
        body,
        out_shape=(out_shape, *x_out),
        grid=grid,
        in_specs=[a_spec, b_spec] + x_in_specs,
        out_specs=(out_spec, *x_out_specs),
        scratch_shapes=([pltpu.VMEM((tm, tn), F32)] if nk > 1 else []) + x_scr,
        compiler_params=_params(("parallel", "parallel", "arbitrary") if ex is None else ("arbitrary",) * 3),
        name=name,
    )(a, b, *x_in)
    return res[0] if ex is None else res


def _rstd(x):
    return lax.rsqrt(jnp.mean(x * x, axis=-1, keepdims=True) + EPS)


def _rms_fwd(x, g, *, name):
    T, D = x.shape
    tm = _tile(T, 512)

    def body(x_ref, g_ref, o_ref, ot_ref):
        x = x_ref[...]
        y = x * _rstd(x) * g_ref[...]
        o_ref[...] = y.astype(o_ref.dtype)
        ot_ref[...] = y.T.astype(ot_ref.dtype)

    return pl.pallas_call(
        body, out_shape=(jax.ShapeDtypeStruct((T, D), BF16), jax.ShapeDtypeStruct((D, T), BF16)), grid=(T // tm,),
        in_specs=[pl.BlockSpec((tm, D), lambda i: (i, 0)), pl.BlockSpec((1, D), lambda i: (0, 0))],
        out_specs=(pl.BlockSpec((tm, D), lambda i: (i, 0)), pl.BlockSpec((D, tm), lambda i: (0, i))),
        compiler_params=_params(("parallel",)), name=name)(x, g)


def _post_res_norm(a, g_post, h, g_pre, *, name):
    T, D = a.shape
    tm = _tile(T, 384, BLOCK)

    def body(a_ref, gp_ref, h_ref, gn_ref, h1_ref, o_ref, ot_ref):
        a = a_ref[...]
        h1 = h_ref[...] + a * _rstd(a) * gp_ref[...]
        h1_ref[...] = h1
        y = h1 * _rstd(h1) * gn_ref[...]
        o_ref[...] = y.astype(o_ref.dtype)
        ot_ref[...] = y.T.astype(ot_ref.dtype)

    row = pl.BlockSpec((tm, D), lambda i: (i, 0))
    vec = pl.BlockSpec((1, D), lambda i: (0, 0))
    return pl.pallas_call(
        body, out_shape=(jax.ShapeDtypeStruct((T, D), F32), jax.ShapeDtypeStruct((T, D), BF16),
                         jax.ShapeDtypeStruct((D, T), BF16)), grid=(T // tm,),
        in_specs=[row, vec, row, vec], out_specs=(row, row, pl.BlockSpec((D, tm), lambda i: (0, i))),
        compiler_params=_params(("parallel",)), name=name)(a, g_post, h, g_pre)


def _loss_head(a, g, h, target, *, name):
    T, D = a.shape
    tm = _tile(T, 512)

    def body(a_ref, g_ref, h_ref, t_ref, dy_ref, da_ref, dg_ref, loss_ref):
        i = pl.program_id(0)
        a = a_ref[...]
        r = _rstd(a)
        ah = a * r
        y = h_ref[...] + ah * g_ref[...]
        rows = i * tm + lax.broadcasted_iota(jnp.int32, (tm, 1), 0)
        err = jnp.where(rows >= BLOCK, y - t_ref[...], 0.0)
        dy = err / D
        dy_ref[...] = dy
        dah = dy * g_ref[...]
        da_ref[...] = (r * (dah - ah * jnp.mean(dah * ah, axis=-1, keepdims=True))).astype(da_ref.dtype)
        part = jnp.sum(jnp.sum(err * err, axis=1, keepdims=True), axis=0, keepdims=True)

        @pl.when(i == 0)
        def _():
            loss_ref[...] = jnp.zeros_like(loss_ref)
            dg_ref[...] = jnp.zeros_like(dg_ref)

        loss_ref[...] += jnp.broadcast_to(part, loss_ref.shape)
        dg_ref[...] += jnp.sum(dy * ah, axis=0, keepdims=True)

    row = pl.BlockSpec((tm, D), lambda i: (i, 0))
    vec = pl.BlockSpec((1, D), lambda i: (0, 0))
    return pl.pallas_call(
        body, out_shape=(jax.ShapeDtypeStruct((T, D), F32), jax.ShapeDtypeStruct((T, D), BF16),
                         jax.ShapeDtypeStruct((1, D), F32), jax.ShapeDtypeStruct((8, 128), F32)),
        grid=(T // tm,),
        in_specs=[row, vec, row, row],
        out_specs=(row, row, vec, pl.BlockSpec((8, 128), lambda i: (0, 0))),
        compiler_params=_params(("arbitrary",)), name=name)(a, g, h, target)


def _rms_bwd(x, g, dy, res, *, out_dtype, dy2=None, then=None, ex=None, name):
    T, D = x.shape
    tm = _tile(T, 512)
    has_res = res is not None
    has_dy2 = 2 if dy2 is not None else 0
    n_in = 3 + has_dy2 + has_res + (2 if then is not None else 0)
    n_out = 2 + (2 if then is not None else 0)

    def pull_back(x, g, dy):
        r = _rstd(x)
        xh = x * r
        dxh = dy * g
        return r * (dxh - xh * jnp.mean(dxh * xh, axis=-1, keepdims=True)), jnp.sum(dy * xh, axis=0, keepdims=True)

    def body(*refs):
        ins, outs = refs[:n_in], refs[n_in:]
        i = pl.program_id(0)

        @pl.when(i == 0)
        def _():
            for ref in outs[1::2]:
                ref[...] = jnp.zeros_like(ref)

        dy_all = ins[2][...].astype(F32)
        if has_dy2:
            dy_all = dy_all + lax.dot_general(ins[3][...], ins[4][...], NT, preferred_element_type=F32)
        dx, dg = pull_back(ins[0][...], ins[1][...], dy_all)
        if has_res:
            dx = dx + ins[3 + has_dy2][...]
        outs[0][...] = dx.astype(outs[0].dtype)
        outs[1][...] += dg
        if then is not None:
            dx2, dg2 = pull_back(ins[n_in - 2][...], ins[n_in - 1][...], dx)
            outs[2][...] = dx2.astype(outs[2].dtype)
            outs[3][...] += dg2

    row = pl.BlockSpec((tm, D), lambda i: (i, 0))
    vec = pl.BlockSpec((1, D), lambda i: (0, 0))
    ins = [x, g, dy] + (list(dy2) if has_dy2 else []) + ([res] if has_res else []) + (list(then) if then is not None else [])
    dy2_specs = ([pl.BlockSpec((tm, dy2[0].shape[1]), lambda i: (i, 0)), pl.BlockSpec(dy2[1].shape, lambda i: (0, 0))]
                 if has_dy2 else [])
    in_specs = [row, vec, row] + dy2_specs + ([row] if has_res else []) + ([row, vec] if then is not None else [])
    out_shape = [jax.ShapeDtypeStruct((T, D), out_dtype), jax.ShapeDtypeStruct((1, D), F32)]
    out_specs = [row, vec]
    if then is not None:
        out_shape += [jax.ShapeDtypeStruct((T, D), BF16), jax.ShapeDtypeStruct((1, D), F32)]
        out_specs += [row, vec]
    grid = (T // tm,)
    body, x_in, x_in_specs, x_out, x_out_specs, x_scr = _carry(ex, grid, n_in, n_out, body)
    return pl.pallas_call(
        body, out_shape=(*out_shape, *x_out), grid=grid,
        in_specs=in_specs + x_in_specs, out_specs=(*out_specs, *x_out_specs), scratch_shapes=x_scr,
        compiler_params=_params(("arbitrary",)), name=name)(*ins, *x_in)


def _gate_up_swiglu(a, w, *, name):
    T, D = a.shape
    S, n = w.shape[0] // 2, w.shape[2]
    tm = _tile(T, 1408, BLOCK)

    def body(a_ref, wg_ref, wu_ref, g_ref, u_ref, o_ref, ot_ref):
        x = a_ref[...]
        g = jnp.dot(x, wg_ref[...], preferred_element_type=F32)
        u = jnp.dot(x, wu_ref[...], preferred_element_type=F32)
        g16, u16 = g.astype(BF16), u.astype(BF16)
        g_ref[...] = g16
        u_ref[...] = u16
        gr = g16.astype(F32)
        act = gr / (1.0 + jnp.exp(-gr)) * u16.astype(F32)
        o_ref[...] = act.astype(o_ref.dtype)
        ot_ref[...] = act.T.astype(ot_ref.dtype)

    tile = pl.BlockSpec((tm, n), lambda i, j: (i, j))
    shp = jax.ShapeDtypeStruct((T, S * n), BF16)
    return pl.pallas_call(
        body, out_shape=(shp, shp, shp, jax.ShapeDtypeStruct((S * n, T), BF16)), grid=(T // tm, S),
        in_specs=[pl.BlockSpec((tm, D), lambda i, j: (i, 0)),
                  pl.BlockSpec((None, D, n), lambda i, j: (j, 0, 0)),
                  pl.BlockSpec((None, D, n), lambda i, j: (j + S, 0, 0))],
        out_specs=(tile, tile, tile, pl.BlockSpec((n, tm), lambda i, j: (j, i))),
        compiler_params=_params(("parallel", "parallel")), name=name)(a, w, w)


def _d_act_swiglu(dff, w_down, gate, up, *, name):
    T, D = dff.shape
    F = w_down.shape[0]
    tm = _tile(T, 384)
    tf = _tile(F, 768, BLOCK)

    def body(d_ref, w_ref, g_ref, u_ref, o_ref):
        dy = d_ref[...]
        for c in range(0, F, tf):
            d = lax.dot_general(dy, w_ref[c:c + tf, :], NT, preferred_element_type=F32)
            g = g_ref[:, c:c + tf].astype(F32)
            u = u_ref[:, c:c + tf].astype(F32)
            sg = 1.0 / (1.0 + jnp.exp(-g))
            o_ref[:, c:c + tf] = (d * u * (sg * (1.0 + g * (1.0 - sg)))).astype(o_ref.dtype)
            o_ref[:, F + c:F + c + tf] = (d * (g * sg)).astype(o_ref.dtype)

    row = pl.BlockSpec((tm, F), lambda i: (i, 0))
    return pl.pallas_call(
        body, out_shape=jax.ShapeDtypeStruct((T, 2 * F), BF16), grid=(T // tm,),
        in_specs=[pl.BlockSpec((tm, D), lambda i: (i, 0)), pl.BlockSpec((F, D), lambda i: (0, 0)), row, row],
        out_specs=pl.BlockSpec((tm, 2 * F), lambda i: (i, 0)),
        compiler_params=_params(("parallel",)), name=name)(dff, w_down, gate, up)


def _fox_gates_fwd(f_t, b, *, name):
    H, T = f_t.shape
    nb = T // BLOCK

    def body(f_ref, b_ref, col_ref):
        f = f_ref[...] + b_ref[...]
        ls = jnp.minimum(f, 0.0) - jnp.log(1.0 + jnp.exp(-jnp.abs(f)))
        t = lax.broadcasted_iota(jnp.int32, (H, T), 1)
        ls = jnp.where(t >= PAD_ROWS, ls, 0.0)
        upper = (lax.broadcasted_iota(jnp.int32, (BLOCK, BLOCK), 0)
                 <= lax.broadcasted_iota(jnp.int32, (BLOCK, BLOCK), 1)).astype(F32)
        carry = jnp.zeros((H, 1), F32)
        for blk in range(nb):
            seg = ls[:, blk * BLOCK:(blk + 1) * BLOCK]
            pre = jnp.dot(seg, upper, precision=HIGHEST, preferred_element_type=F32) + carry
            key_gate = jnp.where(t[:, blk * BLOCK:(blk + 1) * BLOCK] >= PAD_ROWS, pre, -NEG)
            terms = list(_split3(pre)) + list(_split3(key_gate))
            col_ref[blk * BLOCK:(blk + 1) * BLOCK, :] = jnp.concatenate(
                terms + [jnp.zeros((BLOCK - len(terms) * H, BLOCK), F32)], axis=0).T.astype(col_ref.dtype)
            carry = pre[:, BLOCK - 1:BLOCK]

    vm = pl.BlockSpec(memory_space=pltpu.VMEM)
    return pl.pallas_call(
        body, out_shape=jax.ShapeDtypeStruct((T, BLOCK), BF16),
        in_specs=[vm, vm], out_specs=vm,
        compiler_params=_params(), name=name)(f_t, b)


def _fox_gates_bwd(dcq, dck, f_t, b, *, name):
    H, T = f_t.shape
    nb = T // BLOCK

    def body(dq_ref, d_ref, f_ref, b_ref, df_ref, db_ref):
        lower = (lax.broadcasted_iota(jnp.int32, (BLOCK, BLOCK), 0)
                 >= lax.broadcasted_iota(jnp.int32, (BLOCK, BLOCK), 1)).astype(F32)
        carry = jnp.zeros((H, 1), F32)
        for blk in range(nb - 1, -1, -1):
            seg = dq_ref[:, blk * BLOCK:(blk + 1) * BLOCK] - d_ref[:, blk * BLOCK:(blk + 1) * BLOCK]
            suf = jnp.dot(seg, lower, precision=HIGHEST, preferred_element_type=F32) + carry
            df_ref[:, blk * BLOCK:(blk + 1) * BLOCK] = suf
            carry = suf[:, 0:1]
        f = f_ref[...] + b_ref[...]
        t = lax.broadcasted_iota(jnp.int32, (H, T), 1)
        df = jnp.where(t >= PAD_ROWS, df_ref[...] / (1.0 + jnp.exp(f)), 0.0)
        df_ref[...] = df
        db_ref[...] = jnp.sum(df, axis=1, keepdims=True)

    vm = pl.BlockSpec(memory_space=pltpu.VMEM)
    return pl.pallas_call(
        body, out_shape=(jax.ShapeDtypeStruct((H, T), F32), jax.ShapeDtypeStruct((H, 1), F32)),
        in_specs=[vm, vm, vm, vm], out_specs=(vm, vm),
        compiler_params=_params(), name=name)(dcq, dck, f_t, b)


def _fox_lanes(parity):
    base = HEAD_DIM * (1 - parity)
    return base, base + 3


def _split3(c):
    hi = c.astype(BF16).astype(F32)
    r = c - hi
    mid = r.astype(BF16).astype(F32)
    lo = (r - mid).astype(BF16).astype(F32)
    return hi, mid, lo


def _lanes(lane, parity, data, start, terms, ones_at=None, fill=1.0):
    out = jnp.zeros((), F32) if ones_at is None else jnp.where((lane >= ones_at) & (lane < ones_at + 3), fill, 0.0)
    for i, t in enumerate(terms):
        out = jnp.where(lane == start + i, t, out)
    return jnp.where(lane // HEAD_DIM == parity, data, out)


def _fox_prep(proj, cum_col, *, name):
    T = proj.shape[0]
    tm = FOX_TILE
    nt = T // tm
    H = FOX_HEADS
    lanes = 2 * HEAD_DIM
    first = (proj.shape[1] - 3 * H * HEAD_DIM) // lanes

    def body(q_ref, k_ref, v_ref, c_ref, qa_ref, ka_ref, va_ref):
        p = pl.program_id(0)
        i = pl.program_id(1)
        lane = lax.broadcasted_iota(jnp.int32, (1, lanes), 1)
        src = lax.broadcasted_iota(jnp.int32, (lanes, lanes), 0)
        dst = lax.broadcasted_iota(jnp.int32, (lanes, lanes), 1)
        q2 = q_ref[...].astype(F32) * SCALE
        k2 = k_ref[...].astype(F32)
        v2 = v_ref[...].astype(F32)
        gates = c_ref[...]
        def placed(h, first_term, start):
            pick = ((src % FOX_HEADS == h) & (src // FOX_HEADS - first_term == dst - start)
                    & (dst >= start) & (dst < start + 3))
            return jnp.dot(gates, pick.astype(BF16), preferred_element_type=F32)

        moved = [(placed(2 * p + e, 0, _fox_lanes(e)[1]), placed(2 * p + e, 3, _fox_lanes(e)[0])) for e in range(2)]
        for e in range(2):
            kc, qc = _fox_lanes(e)
            own = lane // HEAD_DIM == e
            minus = jnp.where((lane >= kc) & (lane < kc + 3), -1.0, 0.0)
            ones_q = jnp.where((lane >= qc) & (lane < qc + 3), 1.0, 0.0)
            ones_k = jnp.where((lane >= kc) & (lane < kc + 3), 1.0, 0.0)
            qa_ref[e] = jnp.where(own, q2, moved[e][0] + minus).astype(BF16)
            ka_ref[e] = jnp.where(own, k2, moved[e][1] + ones_q).astype(BF16)
            va_ref[e] = jnp.where(own, v2, ones_k).astype(BF16)

    pairs = FOX_GROUP // 2

    def col(part):
        return pl.BlockSpec((tm, lanes),
                            lambda p, i: (i, first + 3 * pairs * (p // pairs) + part * pairs + p % pairs))

    out = pl.BlockSpec((2, tm, lanes), lambda p, i: (p, i, 0))
    shp = jax.ShapeDtypeStruct((H, T, lanes), BF16)
    return pl.pallas_call(
        body, out_shape=(shp, shp, shp), grid=(H // 2, nt),
        in_specs=[col(0), col(1), col(2), pl.BlockSpec((tm, lanes), lambda p, i: (i, 0))],
        out_specs=(out, out, out),
        compiler_params=_params(("parallel", "parallel")), name=name)(proj, proj, proj, cum_col)


def _fox_fwd(q_aug, k_aug, v_aug, mix, *, ex=None, name):
    H, T, lanes = q_aug.shape
    tq = FOX_TILE
    nq = T // tq
    G = FOX_HEADS

    def body(q_ref, k_ref, v_ref, mix_ref, o_ref, lse_ref, m_scr, acc_scr):
        i = pl.program_id(1)
        m_scr[...] = jnp.full(m_scr.shape, NEG, F32)
        acc_scr[...] = jnp.zeros(acc_scr.shape, F32)

        def step(kb, diag):
            off = pl.multiple_of(kb * tq, tq)
            s_t = [lax.dot_general(k_ref[g, pl.ds(off, tq), :], q_ref[g], NT, preferred_element_type=F32)
                   for g in range(G)]
            if diag:
                r = lax.broadcasted_iota(jnp.int32, (tq, tq), 0)
                c = lax.broadcasted_iota(jnp.int32, (tq, tq), 1)
                s_t = [jnp.where(c >= r, s, NEG) for s in s_t]
            m_prev = [m_scr[g] for g in range(G)]
            m_new = [jnp.maximum(m_prev[g], jnp.max(s_t[g], axis=0, keepdims=True)) for g in range(G)]
            p_t = [jnp.exp(s_t[g] - m_new[g]).astype(BF16) for g in range(G)]
            pv = [lax.dot_general(v_ref[g, pl.ds(off, tq), :], p_t[g], TN, preferred_element_type=F32)
                  for g in range(G)]
            for g in range(G):
                acc_scr[g] = jnp.exp(m_prev[g] - m_new[g]) * acc_scr[g] + pv[g]
                m_scr[g] = m_new[g]

        def loop_body(kb, carry):
            step(kb, False)
            return carry

        lax.fori_loop(0, i, loop_body, 0)
        step(i, True)
        lane = lax.broadcasted_iota(jnp.int32, (tq, lanes), 1)
        outs = []
        for g in range(G):
            ones = _fox_lanes(g % 2)[0]
            acc = acc_scr[g]
            lse_ref[g] = m_scr[g] + jnp.log(acc[ones:ones + 1, :])
            acc_t = acc.T
            outs.append(acc_t / acc_t[:, ones:ones + 1])
        for pair in range(G // 2):
            o_ref[:, pair * lanes:(pair + 1) * lanes] = jnp.where(
                lane < HEAD_DIM, outs[2 * pair], outs[2 * pair + 1]).astype(o_ref.dtype)

    blk = pl.BlockSpec((G, tq, lanes), lambda h, i: (h, i, 0))
    full = pl.BlockSpec((G, T, lanes), lambda h, i: (h, 0, 0))
    grid = (H // G, nq)
    first = mix.shape[1] // (G * HEAD_DIM) - H // G
    body, x_in, x_in_specs, x_out, x_out_specs, x_scr = _carry(ex, grid, 4, 2, body)
    return pl.pallas_call(
        body,
        out_shape=(jax.ShapeDtypeStruct(mix.shape, mix.dtype), jax.ShapeDtypeStruct((H, nq, 1, tq), F32), *x_out),
        grid=grid,
        in_specs=[blk, full, full, pl.BlockSpec(memory_space=pl.ANY)] + x_in_specs,
        out_specs=(pl.BlockSpec((tq, G * HEAD_DIM), lambda h, i: (i, first + h)),
                   pl.BlockSpec((G, None, 1, tq), lambda h, i: (h, i, 0, 0)), *x_out_specs),
        input_output_aliases={3: 0},
        scratch_shapes=[pltpu.VMEM((G, 1, tq), F32), pltpu.VMEM((G, lanes, tq), F32)] + x_scr,
        compiler_params=_params(("arbitrary", "arbitrary")), name=name)(q_aug, k_aug, v_aug, mix, *x_in)


def _fox_prep_bwd(dmix, mix, *, name):
    T = dmix.shape[0]
    H = FOX_HEADS
    tm = FOX_TILE
    lanes = 2 * HEAD_DIM
    first = mix.shape[1] // lanes - H // 2

    def body(d_ref, o_ref, da_ref):
        lane = lax.broadcasted_iota(jnp.int32, (1, lanes), 1)
        d2 = d_ref[...].astype(F32)
        prod = d2 * o_ref[...].astype(F32)
        for e in range(2):
            delta = jnp.sum(jnp.where(lane // HEAD_DIM == e, prod, 0.0), axis=1, keepdims=True)
            da_ref[e] = _lanes(lane, e, d2, _fox_lanes(e)[0], _split3(-delta)).astype(BF16)

    pair = pl.BlockSpec((tm, lanes), lambda p, i: (i, first + p))
    return pl.pallas_call(
        body, out_shape=jax.ShapeDtypeStruct((H, T, lanes), BF16), grid=(H // 2, T // tm),
        in_specs=[pair, pair],
        out_specs=pl.BlockSpec((2, tm, lanes), lambda p, i: (p, i, 0)),
        compiler_params=_params(("parallel", "parallel")), name=name)(dmix, mix)


def _fox_bwd(q_aug, k_aug, v_aug, do_aug, lse_row, dproj, *, ex=None, name):
    H, T, lanes = q_aug.shape
    tq = FOX_TILE
    nq = T // tq
    G = FOX_GROUP

    def side_by_side(tiles, scale=None):
        lane = lax.broadcasted_iota(jnp.int32, tiles[0].shape, 1)
        out = [jnp.where(lane < HEAD_DIM, tiles[2 * p], tiles[2 * p + 1]) for p in range(G // 2)]
        out = jnp.concatenate(out, axis=1)
        return out if scale is None else out * scale

    def body(q_ref, k_ref, v_ref, do_ref, lse_ref, dproj_in, out_ref, dcq_ref, dck_ref, dk_acc, dv_acc, dq_ref):
        j = pl.program_id(1)

        @pl.when(j == 0)
        def _():
            dq_ref[...] = jnp.zeros(dq_ref.shape, F32)
            dcq_ref[...] = jnp.zeros(dcq_ref.shape, F32)

        dk_acc[...] = jnp.zeros(dk_acc.shape, F32)
        dv_acc[...] = jnp.zeros(dv_acc.shape, F32)

        def step(qb, diag):
            off = pl.multiple_of(qb * tq, tq)
            heads = range(G)
            qa = [q_ref[g, pl.ds(off, tq), :] for g in heads]
            da = [do_ref[g, pl.ds(off, tq), :] for g in heads]
            s_t = [lax.dot_general(k_ref[g], qa[g], NT, preferred_element_type=F32) for g in heads]
            dp_t = [lax.dot_general(v_ref[g], da[g], NT, preferred_element_type=F32) for g in heads]
            p_t = [jnp.exp(s_t[g] - lse_ref[g, qb]) for g in heads]
            if diag:
                r = lax.broadcasted_iota(jnp.int32, (tq, tq), 0)
                c = lax.broadcasted_iota(jnp.int32, (tq, tq), 1)
                p_t = [jnp.where(c >= r, p, 0.0) for p in p_t]
            dsb = [(p_t[g] * dp_t[g]).astype(BF16) for g in heads]
            dv = [jnp.dot(p_t[g].astype(BF16), da[g], preferred_element_type=F32) for g in heads]
            dk = [jnp.dot(dsb[g], qa[g], preferred_element_type=F32) for g in heads]
            dq = [lax.dot_general(k_ref[g], dsb[g], TN, preferred_element_type=F32) for g in heads]
            for g in heads:
                dv_acc[g] += dv[g]
                dk_acc[g] += dk[g]
                dq_ref[g, qb] += dq[g]
                dcq_ref[g, qb] += jnp.sum(dsb[g].astype(F32), axis=0, keepdims=True)

        step(j, True)

        def loop_body(qb, carry):
            step(qb, False)
            return carry

        lax.fori_loop(j + 1, nq, loop_body, 0)
        dk = [dk_acc[g] for g in range(G)]
        out_ref[:, 0:wide] = side_by_side([dq_ref[g, j].T for g in range(G)], SCALE).astype(out_ref.dtype)
        out_ref[:, wide:2 * wide] = side_by_side(dk).astype(out_ref.dtype)
        out_ref[:, 2 * wide:3 * wide] = side_by_side([dv_acc[g] for g in range(G)]).astype(out_ref.dtype)
        for g in range(G):
            kc = _fox_lanes(g % 2)[0]
            dck_ref[g] = -dk[g].T[kc:kc + 1, :]

    blk = pl.BlockSpec((G, tq, lanes), lambda h, j: (h, j, 0))
    full = pl.BlockSpec((G, T, lanes), lambda h, j: (h, 0, 0))
    wide = G * HEAD_DIM
    first = dproj.shape[1] // (3 * wide) - H // G
    grid = (H // G, nq)
    body, x_in, x_in_specs, x_out, x_out_specs, x_scr = _carry(ex, grid, 6, 3, body)
    rows = jax.ShapeDtypeStruct((H, nq, 1, tq), F32)
    all_rows = pl.BlockSpec((G, nq, 1, tq), lambda h, j: (h, 0, 0, 0))
    return pl.pallas_call(
        body,
        out_shape=(jax.ShapeDtypeStruct(dproj.shape, dproj.dtype), rows, rows, *x_out),
        grid=grid,
        in_specs=[full, blk, blk, full, all_rows, pl.BlockSpec(memory_space=pl.ANY)] + x_in_specs,
        out_specs=(pl.BlockSpec((tq, 3 * wide), lambda h, j: (j, first + h)), all_rows,
                   pl.BlockSpec((G, None, 1, tq), lambda h, j: (h, j, 0, 0)), *x_out_specs),
        input_output_aliases={5: 0},
        scratch_shapes=[pltpu.VMEM((G, tq, lanes), F32), pltpu.VMEM((G, tq, lanes), F32),
                        pltpu.VMEM((G, nq, lanes, tq), F32)] + x_scr,
        compiler_params=_params(("arbitrary", "arbitrary")), name=name,
    )(q_aug, k_aug, v_aug, do_aug, lse_row, dproj, *x_in)


def _t5_bucket_np(d):
    n = np.maximum(d, 0).astype(np.int32)
    max_exact = N_BUCKETS // 2
    nf = np.maximum(n, 1).astype(np.float32)
    large = max_exact + (np.log(nf / max_exact) / math.log(MAX_DISTANCE / max_exact)
                         * (N_BUCKETS - max_exact)).astype(np.int32)
    large = np.minimum(large, N_BUCKETS - 1)
    return np.where(n < max_exact, n, large)


def _bucket_onehots():
    k = np.arange(BLOCK)[:, None]
    q = np.arange(BLOCK)[None, :]
    eye = np.eye(N_BUCKETS, dtype=np.float32)
    cur = eye[_t5_bucket_np(q - k).reshape(-1)]
    prev = eye[_t5_bucket_np(BLOCK + q - k).reshape(-1)]
    return cur, prev


SWA_K_COL = SWA_Q_HEADS * HEAD_DIM // (2 * HEAD_DIM)
SWA_V_COL = SWA_K_COL + 1


def _swa_terms(raw, bc, bp, far, sink, n):
    k = lax.broadcasted_iota(jnp.int32, (BLOCK, BLOCK), 0)
    q = lax.broadcasted_iota(jnp.int32, (BLOCK, BLOCK), 1)
    never = 2 * BLOCK
    s_c = raw[0] + bc
    s_p = raw[1] + bp
    s_m = raw[2] + jnp.where(n == 1, bp, far)
    s_c = jnp.where((k <= q) & (k >= jnp.where(n >= 1, 0, PAD_ROWS)), s_c, NEG)
    s_p = jnp.where(k > q + jnp.where(n >= 2, 0, never), s_p, NEG)
    s_m = jnp.where(k >= jnp.where(n >= 1, PAD_ROWS, never), s_m, NEG)
    m = jnp.maximum(jnp.maximum(jnp.max(s_c, axis=0, keepdims=True), jnp.max(s_p, axis=0, keepdims=True)),
                    jnp.maximum(jnp.max(s_m, axis=0, keepdims=True), sink))
    e = [jnp.exp(s_c - m), jnp.exp(s_p - m), jnp.exp(s_m - m)]
    e_s = jnp.exp(sink - m)
    l = (jnp.sum(e[0], axis=0, keepdims=True) + jnp.sum(e[1], axis=0, keepdims=True)
         + jnp.sum(e[2], axis=0, keepdims=True) + e_s)
    return e, e_s, l


SWA_STEP = 3


def _swa_specs():
    R = SWA_STEP

    def window(col):
        return ([pl.BlockSpec((BLOCK, BLOCK), lambda s, w=w: (jnp.maximum(R * s - 1 + w, 0), col)) for w in range(R + 1)]
                + [pl.BlockSpec((BLOCK, BLOCK), lambda s: (0, col))])

    qblk = pl.BlockSpec((R * BLOCK, SWA_Q_HEADS * HEAD_DIM), lambda s: (s, 0))
    bias = pl.BlockSpec((SWA_Q_HEADS, BLOCK, BLOCK), lambda s: (0, 0, 0))
    smem = pl.BlockSpec(memory_space=pltpu.SMEM)
    return qblk, window(SWA_K_COL), window(SWA_V_COL), bias, smem


def _swa_own_kv(tile_ref, kv):
    lane = lax.broadcasted_iota(jnp.int32, (BLOCK, 2 * HEAD_DIM), 1)
    t = tile_ref[...].astype(F32)
    return jnp.where(lane // HEAD_DIM == kv, t, pltpu.roll(t, HEAD_DIM, 1)).astype(BF16)


def _swa_fwd(proj, bc, bp, far, sinks, *, name):
    T = proj.shape[0]
    nb = T // BLOCK
    G = SWA_GROUP
    Hq = SWA_Q_HEADS
    lanes = 2 * HEAD_DIM

    R = SWA_STEP
    assert nb % R == 0

    def body(*refs):
        q_ref, k_refs, v_refs = refs[0], refs[1:R + 3], refs[R + 3:2 * R + 5]
        bc_ref, bp_ref, far_ref, sink_ref, o_ref = refs[2 * R + 5:]
        s = pl.program_id(0)
        lane = lax.broadcasted_iota(jnp.int32, (BLOCK, lanes), 1)
        kvs = range(SWA_KV_HEADS)
        kk = [[_swa_own_kv(ref, kv) for ref in k_refs] for kv in kvs]
        vv = [[_swa_own_kv(ref, kv) for ref in v_refs] for kv in kvs]
        chains = [(r, h) for r in range(R) for h in range(Hq)]
        tiles = lambda r: (r + 1, r, R + 1)
        q2 = {(r, pair): q_ref[r * BLOCK:(r + 1) * BLOCK, pair * lanes:(pair + 1) * lanes].astype(F32) * SCALE
              for r in range(R) for pair in range(Hq // 2)}
        qm = {c: jnp.where(lane // HEAD_DIM == c[1] % 2, q2[c[0], c[1] // 2], 0.0).astype(BF16) for c in chains}
        raw = {c: [lax.dot_general(kk[c[1] // G][w], qm[c], NT, preferred_element_type=F32) for w in tiles(c[0])]
               for c in chains}
        terms = {c: _swa_terms(raw[c], bc_ref[c[1]], bp_ref[c[1]], far_ref[c[1]], sink_ref[c[1]], R * s + c[0])
                 for c in chains}
        o_t = {c: sum(lax.dot_general(vv[c[1] // G][w], terms[c][0][b].astype(BF16), TN, preferred_element_type=F32)
                      for b, w in enumerate(tiles(c[0]))) for c in chains}
        outs = {c: (o_t[c] / terms[c][2]).T for c in chains}
        for r in range(R):
            for pair in range(Hq // 2):
                o_ref[r * BLOCK:(r + 1) * BLOCK, pair * lanes:(pair + 1) * lanes] = jnp.where(
                    lane < HEAD_DIM, outs[r, 2 * pair], outs[r, 2 * pair + 1]).astype(o_ref.dtype)

    qblk, keys, vals, bias, smem = _swa_specs()
    return pl.pallas_call(
        body, out_shape=jax.ShapeDtypeStruct((T, D_MODEL), BF16), grid=(nb // R,),
        in_specs=[qblk] + keys + vals + [bias, bias, smem, smem],
        out_specs=qblk,
        compiler_params=_params(("parallel",)), name=name,
    )(proj, *([proj] * (2 * R + 4)), bc, bp, far, sinks)


def _swa_bwd(proj, dmix, bc, bp, far, sinks, *, ex=None, name):
    T, width = proj.shape
    nb = T // BLOCK
    G = SWA_GROUP
    Hq = SWA_Q_HEADS
    lanes = 2 * HEAD_DIM
    qw = Hq * HEAD_DIM
    own_w = qw + 2 * lanes

    R = SWA_STEP
    assert nb % R == 0
    n_in = 2 * R + 10

    def body(*refs):
        q_ref, k_refs, v_refs = refs[0], refs[1:R + 3], refs[R + 3:2 * R + 5]
        do_ref, bc_ref, bp_ref, far_ref, sink_ref = refs[2 * R + 5:n_in]
        dp_ref, dbc_ref, dbp_ref, dbf_ref, dsk_ref, dk_acc, dv_acc = refs[n_in:]
        s = pl.program_id(0)

        @pl.when(s == 0)
        def _():
            for ref in (dk_acc, dv_acc, dbc_ref, dbp_ref, dbf_ref, dsk_ref):
                ref[...] = jnp.zeros(ref.shape, F32)

        lane = lax.broadcasted_iota(jnp.int32, (BLOCK, lanes), 1)
        kvs = range(SWA_KV_HEADS)
        kk = [[_swa_own_kv(ref, kv) for ref in k_refs] for kv in kvs]
        vv = [[_swa_own_kv(ref, kv) for ref in v_refs] for kv in kvs]
        chains = [(r, h) for r in range(R) for h in range(Hq)]
        blocks = range(3)
        tiles = lambda r: (r + 1, r, R + 1)
        sub = lambda ref, r, pair: ref[r * BLOCK:(r + 1) * BLOCK, pair * lanes:(pair + 1) * lanes]
        q2 = {(r, pair): sub(q_ref, r, pair).astype(F32) * SCALE for r in range(R) for pair in range(Hq // 2)}
        d2 = {(r, pair): sub(do_ref, r, pair) for r in range(R) for pair in range(Hq // 2)}
        own = [lane // HEAD_DIM == half for half in range(2)]
        qm = {c: jnp.where(own[c[1] % 2], q2[c[0], c[1] // 2], 0.0).astype(BF16) for c in chains}
        dom = {c: jnp.where(own[c[1] % 2], d2[c[0], c[1] // 2], jnp.zeros_like(d2[0, 0])) for c in chains}
        raw = {c: [lax.dot_general(kk[c[1] // G][w], qm[c], NT, preferred_element_type=F32) for w in tiles(c[0])]
               for c in chains}
        dp = {c: [lax.dot_general(vv[c[1] // G][w], dom[c], NT, preferred_element_type=F32) for w in tiles(c[0])]
              for c in chains}
        p, ds16 = {}, {}
        for c in chains:
            r, h = c
            n = R * s + r
            e, e_s, l = _swa_terms(raw[c], bc_ref[h], bp_ref[h], far_ref[h], sink_ref[h], n)
            inv = 1.0 / l
            ph = [e[b] * inv for b in blocks]
            delta = sum(jnp.sum(ph[b] * dp[c][b], axis=0, keepdims=True) for b in blocks)
            ds = [ph[b] * (dp[c][b] - delta) for b in blocks]
            dsk_ref[h] += -(e_s * inv) * delta
            dbc_ref[h] += ds[0]
            dbp_ref[h] += ds[1] + jnp.where(n == 1, ds[2], 0.0)
            dbf_ref[h] += jnp.where(n >= 2, ds[2], 0.0)
            p[c] = [x.astype(BF16) for x in ph]
            ds16[c] = [x.astype(BF16) for x in ds]
        dq_t = {c: sum(lax.dot_general(kk[c[1] // G][w], ds16[c][b], TN, preferred_element_type=F32)
                       for b, w in enumerate(tiles(c[0]))) for c in chains}
        group = [range(kv * G, (kv + 1) * G) for kv in kvs]
        dk = {(r, kv): [sum(jnp.dot(ds16[r, h][b], qm[r, h], preferred_element_type=F32) for h in group[kv])
                        for b in blocks] for r in range(R) for kv in kvs}
        dv = {(r, kv): [sum(jnp.dot(p[r, h][b], dom[r, h], preferred_element_type=F32) for h in group[kv])
                        for b in blocks] for r in range(R) for kv in kvs}
        for r in range(R):
            n = R * s + r
            rows = pl.ds(pl.multiple_of(n * BLOCK, BLOCK), BLOCK)
            prev_rows = pl.ds(pl.multiple_of(jnp.maximum(n - 1, 0) * BLOCK, BLOCK), BLOCK)
            for pair in range(Hq // 2):
                dp_ref[rows, pair * lanes:(pair + 1) * lanes] = (jnp.where(
                    lane < HEAD_DIM, dq_t[r, 2 * pair].T, dq_t[r, 2 * pair + 1].T) * SCALE).astype(dp_ref.dtype)
            for acc, ref in ((dk, dk_acc), (dv, dv_acc)):
                tot = [[a + pltpu.roll(a, HEAD_DIM, 1) for a in acc[r, kv]] for kv in kvs]
                both = [jnp.where(lane < HEAD_DIM, tot[0][b], tot[1][b]) for b in blocks]
                ref[rows, :] += both[0]
                ref[prev_rows, :] += both[1]
                ref[0:BLOCK, :] += both[2]

        @pl.when(s == nb // R - 1)
        def _():
            dp_ref[:, qw:qw + lanes] = dk_acc[...].astype(dp_ref.dtype)
            dp_ref[:, qw + lanes:own_w] = dv_acc[...].astype(dp_ref.dtype)

    qblk, keys, vals, bias, smem = _swa_specs()
    dsk = pl.BlockSpec((Hq, 1, BLOCK), lambda s: (0, 0, 0))
    grid = (nb // R,)
    body, x_in, x_in_specs, x_out, x_out_specs, x_scr = _carry(ex, grid, n_in, 5, body)
    tile = jax.ShapeDtypeStruct((Hq, BLOCK, BLOCK), F32)
    return pl.pallas_call(
        body,
        out_shape=(jax.ShapeDtypeStruct((T, width), BF16), tile, tile, tile,
                   jax.ShapeDtypeStruct((Hq, 1, BLOCK), F32), *x_out),
        grid=grid,
        in_specs=[qblk] + keys + vals + [qblk, bias, bias, smem, smem] + x_in_specs,
        out_specs=(pl.BlockSpec((T, own_w), lambda s: (0, 0)), bias, bias, bias, dsk, *x_out_specs),
        scratch_shapes=[pltpu.VMEM((T, lanes), F32), pltpu.VMEM((T, lanes), F32)] + x_scr,
        compiler_params=_params(("arbitrary",)), name=name,
    )(proj, *([proj] * (2 * R + 4)), dmix, bc, bp, far, sinks, *x_in)


def _bias_tiles(tab_t, oh_cur_t, oh_prev_t, *, name):
    Hq = tab_t.shape[0]

    def body(t_ref, oc_ref, op_ref, bc_ref, bp_ref):
        bc_ref[...] = jnp.dot(t_ref[...], oc_ref[...], precision=HIGHEST, preferred_element_type=F32)
        bp_ref[...] = jnp.dot(t_ref[...], op_ref[...], precision=HIGHEST, preferred_element_type=F32)

    vm = pl.BlockSpec(memory_space=pltpu.VMEM)
    shp = jax.ShapeDtypeStruct((Hq, BLOCK * BLOCK), F32)
    bc, bp = pl.pallas_call(body, out_shape=(shp, shp), in_specs=[vm] * 3, out_specs=(vm, vm),
                            compiler_params=_params(), name=name)(tab_t, oh_cur_t, oh_prev_t)
    return bc.reshape(Hq, BLOCK, BLOCK), bp.reshape(Hq, BLOCK, BLOCK)


def _small_grads(dbc, dbp, dbf, dsk, oh_cur, oh_prev, *, name):
    Hq = dbc.shape[0]

    def body(dbc_ref, dbp_ref, dbf_ref, dsk_ref, oc_ref, op_ref, tab_ref, sink_ref):
        tab = (jnp.dot(dbc_ref[...], oc_ref[...], precision=HIGHEST, preferred_element_type=F32)
               + jnp.dot(dbp_ref[...], op_ref[...], precision=HIGHEST, preferred_element_type=F32))
        far = jnp.sum(dbf_ref[...], axis=1, keepdims=True)
        last = lax.broadcasted_iota(jnp.int32, (Hq, N_BUCKETS), 1) == N_BUCKETS - 1
        tab_ref[...] = tab + jnp.where(last, far, 0.0)
        sink_ref[...] = jnp.sum(dsk_ref[...], axis=1, keepdims=True)

    vm = pl.BlockSpec(memory_space=pltpu.VMEM)
    return pl.pallas_call(
        body, out_shape=(jax.ShapeDtypeStruct((Hq, N_BUCKETS), F32), jax.ShapeDtypeStruct((Hq, 1), F32)),
        in_specs=[vm] * 6, out_specs=(vm, vm), compiler_params=_params(), name=name,
    )(dbc.reshape(Hq, -1), dbp.reshape(Hq, -1), dbf.reshape(Hq, -1), dsk.reshape(Hq, -1), oh_cur, oh_prev)


def _coords():
    return lax.axis_index("x"), lax.axis_index("y"), lax.axis_index("c")


class _Exchange:
    def __init__(self, inputs, out_shapes, scratch, start, finish):
        self.inputs, self.out_shapes, self.scratch, self.start, self.finish = inputs, out_shapes, scratch, start, finish


def _carry(ex, grid, n_in, n_out, body):
    if ex is None:
        return body, [], [], [], [], []
    ni, no = len(ex.inputs), len(ex.out_shapes)

    def at_step(which):
        cond = None
        for axis, n in enumerate(grid):
            c = pl.program_id(axis) == (0 if which == "first" else n - 1)
            cond = c if cond is None else cond & c
        return cond

    def wrapped(*refs):
        refs = list(refs)
        n_own_scr = len(refs) - (n_in + ni + n_out + no) - len(ex.scratch)
        own_in, side_in = refs[:n_in], refs[n_in:n_in + ni]
        own_out = refs[n_in + ni:n_in + ni + n_out]
        side_out = refs[n_in + ni + n_out:n_in + ni + n_out + no]
        rest = refs[n_in + ni + n_out + no:]
        own_scr, sems = rest[:n_own_scr], rest[n_own_scr:]

        @pl.when(at_step("first"))
        def _():
            ex.start(side_in, side_out, sems)

        body(*own_in, *own_out, *own_scr)

        @pl.when(at_step("last"))
        def _():
            ex.finish(side_in, side_out, sems)

    hbm = pl.BlockSpec(memory_space=pl.ANY)
    return wrapped, list(ex.inputs), [hbm] * ni, list(ex.out_shapes), [hbm] * no, list(ex.scratch)


def _run_exchange(ex, *, name):
    ni, no = len(ex.inputs), len(ex.out_shapes)

    def body(*refs):
        ins, outs, sems = refs[:ni], refs[ni:ni + no], refs[ni + no:]
        ex.start(ins, outs, sems)
        ex.finish(ins, outs, sems)

    hbm = pl.BlockSpec(memory_space=pl.ANY)
    return pl.pallas_call(
        body, out_shape=tuple(ex.out_shapes), in_specs=[hbm] * ni, out_specs=tuple([hbm] * no),
        scratch_shapes=ex.scratch, compiler_params=_params(), name=name)(*ex.inputs)


def _gather_exchange(shards):
    nt = len(shards)

    def copies(ins, outs, sems):
        send_sems, recv_sems, local_sems = sems
        x, y, c = _coords()
        me, sibling = (x, y, c), (x, y, 1 - c)
        chips = [(1 - x, y), (x, 1 - y), (1 - x, 1 - y)]

        def slot(t, dev):
            return outs[t].at[4 * dev[0] + 2 * dev[1] + dev[2]]

        def copy(t, k, block, to, src=None):
            dst = slot(t, block)
            return pltpu.make_async_remote_copy(
                src_ref=dst if src is None else src, dst_ref=dst,
                send_sem=send_sems.at[t, k], recv_sem=recv_sems.at[t, k], device_id=to, device_id_type=MESH)

        mine = [pltpu.make_async_copy(ins[t], slot(t, me), local_sems.at[t]) for t in range(nt)]
        first = []
        for t in range(nt):
            first.append(copy(t, 0, me, sibling, src=ins[t]))
            first += [copy(t, 1 + j, me, (*chip, c), src=ins[t]) for j, chip in enumerate(chips)]
        return copy, mine, first, me, sibling, chips, c

    def start(ins, outs, sems):
        _, mine, first, *_ = copies(ins, outs, sems)
        for cp in mine + first:
            cp.start()

    def finish(ins, outs, sems):
        copy, mine, first, me, sibling, chips, c = copies(ins, outs, sems)
        passed = []
        for j, chip in enumerate(chips):
            for t in range(nt):
                copy(t, 1 + j, (*chip, c), me).wait_recv()
                cp = copy(t, 4 + j, (*chip, c), sibling)
                cp.start()
                passed.append(cp)
        for t in range(nt):
            copy(t, 0, sibling, me).wait_recv()
            for j, chip in enumerate(chips):
                copy(t, 4 + j, (*chip, 1 - c), me).wait_recv()
        for cp in first + passed:
            cp.wait_send()
        for cp in mine:
            cp.wait()

    return _Exchange(
        list(shards), [jax.ShapeDtypeStruct((N_DEV,) + s.shape, s.dtype) for s in shards],
        [pltpu.SemaphoreType.DMA((nt, 7)), pltpu.SemaphoreType.DMA((nt, 7)), pltpu.SemaphoreType.DMA((nt,))],
        start, finish)


def _swap_exchange(arrays, n_slices, copies):
    nt = len(arrays)

    def start(ins, outs, sems):
        for cp in copies(ins, outs, sems):
            cp.start()

    def finish(ins, outs, sems):
        sends = copies(ins, outs, sems)
        for cp in sends:
            cp.wait_recv()
        for cp in sends:
            cp.wait_send()

    return _Exchange(
        list(arrays), [jax.ShapeDtypeStruct((n_slices,) + a.shape[1:], a.dtype) for a in arrays],
        [pltpu.SemaphoreType.DMA((nt, n_slices)), pltpu.SemaphoreType.DMA((nt, n_slices))], start, finish)


def _cores_exchange(gs):
    def copies(ins, outs, sems):
        send_sems, recv_sems = sems
        x, y, c = _coords()
        return [pltpu.make_async_remote_copy(
            src_ref=ins[t].at[2 * j + (1 - c)], dst_ref=outs[t].at[j],
            send_sem=send_sems.at[t, j], recv_sem=recv_sems.at[t, j], device_id=(x, y, 1 - c), device_id_type=MESH)
            for t in range(len(gs)) for j in range(4)]

    return _swap_exchange(gs, 4, copies)


def _chips_exchange(ps):
    def copies(ins, outs, sems):
        send_sems, recv_sems = sems
        x, y, c = _coords()
        peers = [(1 - x, y), (x, 1 - y), (1 - x, 1 - y)]
        return [pltpu.make_async_remote_copy(
            src_ref=ins[t].at[2 * px + py], dst_ref=outs[t].at[k],
            send_sem=send_sems.at[t, k], recv_sem=recv_sems.at[t, k], device_id=(px, py, c), device_id_type=MESH)
            for t in range(len(ps)) for k, (px, py) in enumerate(peers)]

    return _swap_exchange(ps, 3, copies)


def _add_cores(g, r, core, *, name):
    _, A, B = g.shape
    ta = _tile(A, 512, 16)

    def body(core_ref, a_ref, b_ref, o16_ref):
        o16_ref[...] = (a_ref[...] + b_ref[...]).astype(BF16)

    blk = (None, ta, B)
    return pl.pallas_call(
        body, out_shape=jax.ShapeDtypeStruct((4, A, B), BF16),
        grid_spec=pltpu.PrefetchScalarGridSpec(
            num_scalar_prefetch=1, grid=(4, A // ta),
            in_specs=[pl.BlockSpec(blk, lambda j, i, core_ref: (2 * j + core_ref[0], i, 0)),
                      pl.BlockSpec(blk, lambda j, i, core_ref: (j, i, 0))],
            out_specs=pl.BlockSpec(blk, lambda j, i, core_ref: (j, i, 0))),
        compiler_params=_params(("parallel", "parallel")), name=name)(core, g, r)


def _adamw_math(w, g, m, v):
    m = ADAM_B1 * m + (1.0 - ADAM_B1) * g
    v = ADAM_B2 * v + (1.0 - ADAM_B2) * (g * g)
    m_hat = m / (1.0 - ADAM_B1 ** ADAM_STEP)
    v_hat = v / (1.0 - ADAM_B2 ** ADAM_STEP)
    delta = -ADAM_LR * (m_hat / (jnp.sqrt(v_hat) + ADAM_EPS) + ADAM_WD * w)
    return delta, m, v


def _sum_adamw(mine, sib, r, where, w, m, v, *, segs, ta, name):
    Aw, Bw = w.shape
    Bg = mine.shape[2]
    assert Aw % ta == 0

    def body(where_ref, p_ref, s_ref, r0, r1, r2, w_ref, m_ref, v_ref, g_out, d_out, m_out, v_out):
        for gc, wc, n in segs:
            g = (((p_ref[:, gc:gc + n] + s_ref[:, gc:gc + n]) + r0[:, gc:gc + n].astype(F32))
                 + r1[:, gc:gc + n].astype(F32)) + r2[:, gc:gc + n].astype(F32)
            delta, m_new, v_new = _adamw_math(w_ref[:, wc:wc + n], g, m_ref[:, wc:wc + n], v_ref[:, wc:wc + n])
            g_out[:, wc:wc + n] = g
            d_out[:, wc:wc + n] = delta
            m_out[:, wc:wc + n] = m_new
            v_out[:, wc:wc + n] = v_new

    gblk = (None, ta, Bg)
    row = pl.BlockSpec((ta, Bw), lambda i, where_ref: (i, 0))
    rspecs = [pl.BlockSpec(gblk, (lambda i, where_ref, k=k: (k, i, 0))) for k in range(3)]
    shp = jax.ShapeDtypeStruct((Aw, Bw), F32)
    return pl.pallas_call(
        body, out_shape=(shp, shp, shp, shp),
        grid_spec=pltpu.PrefetchScalarGridSpec(
            num_scalar_prefetch=1, grid=(Aw // ta,),
            in_specs=[pl.BlockSpec(gblk, lambda i, where_ref: (2 * where_ref[0] + where_ref[1], i, 0)),
                      pl.BlockSpec(gblk, lambda i, where_ref: (where_ref[0], i, 0))] + rspecs + [row, row, row],
            out_specs=(row, row, row, row)),
        compiler_params=_params(("parallel",)), name=name)(where, mine, sib, r, r, r, w, m, v)


def _adamw(w, g, m, v, *, name):
    def body(w_ref, g_ref, m_ref, v_ref, d_out, m_out, v_out):
        delta, m_new, v_new = _adamw_math(w_ref[...], g_ref[...], m_ref[...], v_ref[...])
        d_out[...] = delta
        m_out[...] = m_new
        v_out[...] = v_new

    vm = pl.BlockSpec(memory_space=pltpu.VMEM)
    shp = jax.ShapeDtypeStruct(w.shape, F32)
    return pl.pallas_call(body, out_shape=(shp, shp, shp), in_specs=[vm] * 4, out_specs=(vm, vm, vm),
                          compiler_params=_params(), name=name)(w, g, m, v)


def _small_allreduce_adamw(s, w, m, v, *, name):
    R, W = s.shape

    def body(s_ref, w_ref, m_ref, v_ref, g_out, d_out, m_out, v_out, gath, send_sems, recv_sems):
        x, y, c = _coords()
        mine = 4 * x + 2 * y + c
        gath[mine] = s_ref[...]
        peers = [((1 - x) if k & 4 else x, (1 - y) if k & 2 else y, (1 - c) if k & 1 else c) for k in range(1, N_DEV)]
        sends = []
        for k in range(1, N_DEV):
            peer = peers[k - 1]
            sends.append(pltpu.make_async_remote_copy(
                src_ref=s_ref, dst_ref=gath.at[mine], send_sem=send_sems.at[k - 1], recv_sem=recv_sems.at[k - 1],
                device_id=peer, device_id_type=MESH))
        for cp in sends:
            cp.start()
        for k in range(1, N_DEV):
            peer = peers[k - 1]
            pltpu.make_async_remote_copy(
                src_ref=s_ref, dst_ref=gath.at[4 * peer[0] + 2 * peer[1] + peer[2]],
                send_sem=send_sems.at[k - 1], recv_sem=recv_sems.at[k - 1],
                device_id=peer, device_id_type=MESH).wait_recv()
        for cp in sends:
            cp.wait_send()
        g = gath[0]
        for d in range(1, N_DEV):
            g = g + gath[d]
        delta, m_new, v_new = _adamw_math(w_ref[...], g, m_ref[...], v_ref[...])
        g_out[...] = g
        d_out[...] = delta
        m_out[...] = m_new
        v_out[...] = v_new

    vm = pl.BlockSpec(memory_space=pltpu.VMEM)
    shp = jax.ShapeDtypeStruct((R, W), F32)
    return pl.pallas_call(
        body, out_shape=(shp, shp, shp, shp), in_specs=[vm] * 4, out_specs=(vm, vm, vm, vm),
        scratch_shapes=[pltpu.VMEM((N_DEV, R, W), F32), pltpu.SemaphoreType.DMA((N_DEV - 1,)),
                        pltpu.SemaphoreType.DMA((N_DEV - 1,))],
        compiler_params=_params(), name=name)(s, w, m, v)


def _pack_small(rel_bias, g1, g2, g3, g4, b_forget, sinks, extra=None, meta=None):
    misc = jnp.concatenate([rel_bias.reshape(-1), b_forget.reshape(-1), sinks.reshape(-1)])
    misc = jnp.concatenate([misc, jnp.zeros((D_MODEL - misc.shape[0],), F32)])[None]
    last = jnp.zeros((1, D_MODEL), F32) if extra is None else extra
    meta = jnp.zeros((N_META, D_MODEL), F32) if meta is None else meta
    return jnp.concatenate([g1, g2, g3, g4, misc, last, jnp.zeros((2, D_MODEL), F32), meta], axis=0)


def _unpack_small(p):
    nrb = N_BUCKETS * SWA_Q_HEADS
    misc = p[4]
    return dict(rel_bias=misc[:nrb].reshape(N_BUCKETS, SWA_Q_HEADS), ln_pre_mix=p[0:1], ln_post_mix=p[1:2],
                ln_pre_ffn=p[2:3], ln_post_ffn=p[3:4], b_forget=misc[nrb:nrb + 8].reshape(1, 8),
                sinks=misc[nrb + 8:nrb + 16].reshape(1, 8))


def _proj_runs():
    gw = FOX_GROUP * HEAD_DIM
    swa = SWA_Q_W + 2 * SWA_KV_HEADS * HEAD_DIM
    runs = [(0, swa)]
    for grp in range(FOX_HEADS // FOX_GROUP):
        runs += [(swa + part * FOX_W + grp * gw, swa + part * FOX_W + (grp + 1) * gw) for part in range(3)]
    return runs


def _columns_from_shards(gathered, runs, shard):
    pieces = []
    for start, stop in runs:
        for d in range(start // shard, (stop - 1) // shard + 1):
            lo = d * shard
            pieces.append(gathered[d][:, max(start, lo) - lo:min(stop, lo + shard) - lo])
    return jnp.concatenate(pieces, axis=1)


def _device_shards(qkv, gate, shard, padded):
    pos, segments = 0, []
    for start, stop in _proj_runs():
        segments.append((start, stop, qkv, pos))
        pos += stop - start
    segments.append((pos, pos + gate.shape[1], gate, 0))
    total = pos + gate.shape[1]
    assert total % shard == 0
    zeros = jnp.zeros((qkv.shape[0], padded - shard), qkv.dtype)
    out = []
    for d in range(total // shard):
        lo, hi = d * shard, (d + 1) * shard
        pieces = [arr[:, src + max(lo, s) - s:src + min(hi, e) - s]
                  for s, e, arr, src in sorted(segments, key=lambda seg: seg[0]) if max(lo, s) < min(hi, e)]
        out.append(jnp.concatenate(pieces + [zeros], axis=1))
    return jnp.stack(out)


def kernel(x, meta_tokens, rel_bias, ln_pre_mix, ln_post_mix, ln_pre_ffn, ln_post_ffn, w_in, b_forget, sinks, w_out, w_gate_up, w_down, loss_target, m_meta_tokens, m_rel_bias, m_ln_pre_mix, m_ln_post_mix, m_ln_pre_ffn, m_ln_post_ffn, m_w_in, m_b_forget, m_sinks, m_w_out, m_w_gate_up, m_w_down, v_meta_tokens, v_rel_bias, v_ln_pre_mix, v_ln_post_mix, v_ln_pre_ffn, v_ln_post_ffn, v_w_in, v_b_forget, v_sinks, v_w_out, v_w_gate_up, v_w_down):
    seq = x.shape[1]
    T = BLOCK + seq
    assert T % FOX_TILE == 0
    nq = T // FOX_TILE
    tm = _tile(T, 1056)
    cin = w_in.shape[2]
    hid = w_down.shape[1]
    assert w_gate_up.shape[2] == 2 * hid and cin <= W_IN_PAD and hid <= HID_PAD

    x_i, y_i, c_i = _coords()
    core = jnp.reshape(c_i, (1,)).astype(jnp.int32)
    where = jnp.stack([2 * x_i + y_i, c_i]).astype(jnp.int32)
    w_in_s = jnp.pad(w_in[0].astype(BF16), ((0, 0), (0, W_IN_PAD - cin)))
    w_gu_s = jnp.pad(w_gate_up[0].astype(BF16).reshape(D_MODEL, 2, hid), ((0, 0), (0, 0), (0, HID_PAD - hid)))
    w_gu_s = w_gu_s.reshape(D_MODEL, 2 * HID_PAD)
    w_down_s = jnp.pad(w_down[0].astype(BF16), ((0, HID_PAD - hid), (0, 0)))
    g_in, g_meta = _run_exchange(_gather_exchange([w_in_s, meta_tokens]), name="ag_w_in")
    gather_rest = _gather_exchange([w_out[0].astype(BF16), w_gu_s, w_down_s])
    w_qkv = _columns_from_shards(g_in, _proj_runs(), cin)
    w_f = jnp.pad(_columns_from_shards(g_in, [(D_QKV, D_PROJ)], cin), ((0, 0), (0, BLOCK - FOX_HEADS)))
    meta_full = g_meta.transpose(1, 0, 2).reshape(N_META, D_MODEL)

    h0 = jnp.concatenate([jnp.zeros((PAD_ROWS, D_MODEL), F32), meta_full, x[0]], axis=0)
    target = jnp.concatenate([jnp.zeros((BLOCK, D_MODEL), F32), loss_target[0]], axis=0)
    hn1, hn1_t = _rms_fwd(h0, ln_pre_mix, name="rms_pre_mix")
    proj = _matmul(hn1, w_qkv, out_dtype=BF16, tm=tm, tn=D_QKV, name="mm_in_proj")
    proj_f = _matmul(hn1, w_f, out_dtype=F32, tm=tm, tn=BLOCK, name="mm_in_proj_f")

    f_t = proj_f[:, :FOX_HEADS].T
    bf_col = b_forget.reshape(FOX_HEADS, 1)

    oh_cur, oh_prev = _bucket_onehots()
    bias_c, bias_p = _bias_tiles(rel_bias.T, jnp.asarray(oh_cur.T), jnp.asarray(oh_prev.T), name="bias_tiles")
    far = rel_bias[N_BUCKETS - 1]
    sink_v = sinks[0]
    mix_a = _swa_fwd(proj, bias_c, bias_p, far, sink_v, name="swa_fwd")

    cum_col = _fox_gates_fwd(f_t, bf_col, name="fox_gates_fwd")
    q_b, k_b, v_b = _fox_prep(proj, cum_col, name="fox_prep")
    mix, lse_row, g_out, g_gu, g_down = _fox_fwd(q_b, k_b, v_b, mix_a, ex=gather_rest, name="fox_fwd")
    w_out_full = g_out.reshape(D_MODEL, D_MODEL)
    w_down_full = g_down.reshape(N_DEV * HID_PAD, D_MODEL)

    a1 = _matmul(mix, w_out_full, out_dtype=F32, tm=tm, tn=D_MODEL, name="mm_out_proj")
    h1, hn2, hn2_t = _post_res_norm(a1, ln_post_mix, h0, ln_pre_ffn, name="post_mix_pre_ffn")
    gate, up, act, act_t = _gate_up_swiglu(hn2, g_gu, name="mm_gate_up")
    ff = _matmul(act, w_down_full, out_dtype=F32, tm=tm, tn=512, name="mm_down")
    dh2, dff, dg_post_ffn, loss_acc = _loss_head(ff, ln_post_ffn, h1, target, name="loss_head")

    dgu = _d_act_swiglu(dff, w_down_full, gate, up, name="mm_d_act")
    d_w_down = _matmul(act_t, dff, out_dtype=F32, tm=768, tn=512, name="mm_dw_down")
    dhn2 = _matmul(dgu, g_gu, nt=True, b_shards=True, out_dtype=F32, tm=tm, tn=512, name="mm_d_hn2")
    d_w_gu = _matmul(hn2_t, dgu, out_shards=True, out_dtype=F32, tm=512, tn=2 * HID_PAD, name="mm_dw_gate_up")
    dh1, dg_pre_ffn, da1, dg_post_mix = _rms_bwd(h1, ln_pre_ffn, dhn2, dh2, out_dtype=F32,
                                                 then=(a1, ln_post_mix), name="rms_bwd_pre_ffn_post_mix")
    dmix = _matmul(da1, w_out_full, nt=True, out_dtype=BF16, tm=tm, tn=D_MODEL, name="mm_d_mix")
    d_w_out = _matmul(mix, da1, ta=True, out_dtype=F32, tm=512, tn=D_MODEL, name="mm_dw_out")

    ffn_grads = [d_w_out.reshape(N_DEV, -1, D_MODEL), d_w_gu, d_w_down.reshape(N_DEV, HID_PAD, D_MODEL)]
    dproj_a, dbc, dbp, dbf, dsk, *ffn_sibling = _swa_bwd(
        proj, dmix, bias_c, bias_p, far, sink_v, ex=_cores_exchange(ffn_grads), name="swa_bwd")
    d_tab, d_sink = _small_grads(dbc, dbp, dbf, dsk, jnp.asarray(oh_cur), jnp.asarray(oh_prev), name="small_grads")
    ffn_sums = [_add_cores(g, r, core, name="rs_add_" + t)
                for g, r, t in zip(ffn_grads, ffn_sibling, ["w_out", "w_gate_up", "w_down"])]

    do_b = _fox_prep_bwd(dmix, mix, name="fox_prep_bwd")
    dproj, dcq, dck, *ffn_chips = _fox_bwd(
        q_b, k_b, v_b, do_b, lse_row, dproj_a, ex=_chips_exchange(ffn_sums), name="fox_bwd")
    df_t, d_bf = _fox_gates_bwd(dcq.reshape(FOX_HEADS, T), dck.reshape(FOX_HEADS, T), f_t, bf_col,
                                name="fox_gates_bwd")
    df = jnp.pad(df_t.T.astype(BF16), ((0, 0), (0, BLOCK - FOX_HEADS)))

    d_w_qkv = _matmul(hn1_t, dproj, out_dtype=F32, tm=512, tn=768, name="mm_dw_in")
    d_w_f = _matmul(hn1_t, df, out_dtype=F32, tm=512, tn=BLOCK, name="mm_dw_in_f")
    d_w_in = _device_shards(d_w_qkv, d_w_f[:, :FOX_HEADS], cin, W_IN_PAD)
    dhn1, in_sibling = _matmul(dproj, w_qkv, nt=True, out_dtype=F32, tm=tm, tn=512,
                               ex=_cores_exchange([d_w_in]), name="mm_d_hn1")
    in_sum = _add_cores(d_w_in, in_sibling, core, name="rs_add_w_in")
    dh0, dg_pre_mix, in_chips = _rms_bwd(h0, ln_pre_mix, dhn1, dh1, out_dtype=F32, dy2=(df, w_f),
                                         ex=_chips_exchange([in_sum]), name="rms_bwd_pre_mix")
    grad_x = dh0[BLOCK:][None]
    d_meta = dh0[PAD_ROWS:BLOCK]

    tags = ["w_in", "w_out", "w_gate_up", "w_down"]
    mine = [d_w_in] + ffn_grads
    from_sibling = [in_sibling] + list(ffn_sibling)
    from_chips = [in_chips] + list(ffn_chips)
    shard_w = [(w_in, m_w_in, v_w_in), (w_out, m_w_out, v_w_out), (w_gate_up, m_w_gate_up, v_w_gate_up),
               (w_down, m_w_down, v_w_down)]
    segs = [[(0, 0, cin)], [(0, 0, D_MODEL)], [(0, 0, hid), (HID_PAD, hid, hid)], [(0, 0, D_MODEL)]]
    tas = [256, BLOCK, 256, hid]
    big = [{}, {}, {}, {}]
    for i, t in enumerate(tags):
        w_t, m_t, v_t = shard_w[i]
        res = _sum_adamw(mine[i], from_sibling[i], from_chips[i], where, w_t[0], m_t[0], v_t[0], segs=segs[i],
                         ta=tas[i], name="rs_adamw_" + t)
        for kind in range(4):
            big[kind][t] = res[kind][None]

    loss_row = jnp.pad(loss_acc[0:1, 0:1] * (0.5 / D_MODEL), ((0, 0), (0, D_MODEL - 1)))
    s_small = _pack_small(d_tab.T, dg_pre_mix, dg_post_mix, dg_pre_ffn, dg_post_ffn, d_bf, d_sink,
                          extra=loss_row, meta=d_meta)
    w_s = _pack_small(rel_bias, ln_pre_mix, ln_post_mix, ln_pre_ffn, ln_post_ffn, b_forget, sinks)
    m_s = _pack_small(m_rel_bias, m_ln_pre_mix, m_ln_post_mix, m_ln_pre_ffn, m_ln_post_ffn, m_b_forget, m_sinks)
    v_s = _pack_small(v_rel_bias, v_ln_pre_mix, v_ln_post_mix, v_ln_pre_ffn, v_ln_post_ffn, v_b_forget, v_sinks)
    small = _small_allreduce_adamw(s_small, w_s, m_s, v_s, name="small_allreduce_adamw")
    loss = small[0][5, 0]
    mcols = meta_tokens.shape[1]
    g_meta_mine = lax.dynamic_slice(small[0][8:8 + N_META], (0, (4 * x_i + 2 * y_i + c_i) * mcols), (N_META, mcols))
    big[0]["meta_tokens"] = g_meta_mine
    for kind, arr in enumerate(_adamw(meta_tokens, g_meta_mine, m_meta_tokens, v_meta_tokens, name="adamw_meta")):
        big[kind + 1]["meta_tokens"] = arr
    small = [_unpack_small(p) for p in small]

    names = ["meta_tokens", "rel_bias", "ln_pre_mix", "ln_post_mix", "ln_pre_ffn", "ln_post_ffn", "w_in",
             "b_forget", "sinks", "w_out", "w_gate_up", "w_down"]
    outs = [loss, grad_x]
    for kind in range(4):
        for nme in names:
            outs.append(big[kind][nme] if nme in big[kind] else small[kind][nme])
    return tuple(outs)
```

```python
import math

import numpy as np
import jax
import jax.numpy as jnp
from jax import lax
from jax.experimental import pallas as pl
from jax.experimental.pallas import tpu as pltpu

F32 = jnp.float32
BF16 = jnp.bfloat16
HIGHEST = lax.Precision.HIGHEST
MESH = pl.DeviceIdType.MESH

N_DEV = 8
D_MODEL = 1024
N_META = 16
HEAD_DIM = 64
SWA_Q_HEADS = 8
SWA_KV_HEADS = 2
SWA_GROUP = 4
FOX_HEADS = 8
FOX_W = FOX_HEADS * HEAD_DIM
SWA_Q_W = SWA_Q_HEADS * HEAD_DIM
BLOCK = 128
PAD_ROWS = BLOCK - N_META
N_BUCKETS = 32
MAX_DISTANCE = 128
D_FF = 2816
D_QKV = 2304
D_PROJ = D_QKV + FOX_HEADS
D_PROJ_PAD = 2560
EPS = 1e-6
NEG = -1e30
SCALE = HEAD_DIM ** -0.5
ADAM_LR, ADAM_B1, ADAM_B2, ADAM_EPS, ADAM_WD, ADAM_STEP = 0.001, 0.9, 0.999, 1e-08, 0.01, 10
VMEM_LIMIT = 56 * 1024 * 1024
FOX_TILE = 384
FOX_GROUP = 4
W_IN_PAD = 384
HID_PAD = 384

NT = (((1,), (1,)), ((), ()))
NN = (((1,), (0,)), ((), ()))
TN = (((0,), (0,)), ((), ()))


def _params(sem=None, **kw):
    if sem is not None:
        kw["dimension_semantics"] = sem
    return pltpu.CompilerParams(vmem_limit_bytes=VMEM_LIMIT, **kw)


def _tile(n, target, mult=16):
    best = None
    for t in range(mult, min(n, target) + 1, mult):
        if n % t == 0:
            best = t
    assert best is not None, (n, target)
    return best


def _matmul(a, b, *, nt=False, ta=False, b_shards=False, out_shards=False, out_dtype, tm, tn=None, tk=None,
            ex=None, name):
    M, K = a.shape[::-1] if ta else a.shape
    assert not (ta and (nt or b_shards))
    k_shards = b.shape[0] if (b_shards and nt) else 0
    if k_shards:
        N, ks = b.shape[1], b.shape[2]
        assert tk is None and K == k_shards * ks
    elif b_shards:
        N, tn = b.shape[0] * b.shape[2], b.shape[2]
    else:
        N = b.shape[0] if nt else b.shape[1]
    tk = K if tk is None else tk
    assert M % tm == 0 and N % tn == 0 and K % tk == 0, (name, a.shape, b.shape, tm, tn, tk)
    nk = K // tk
    dn = NT if nt else (TN if ta else NN)
    a_spec = pl.BlockSpec((tk, tm), lambda i, j, k: (k, i)) if ta else pl.BlockSpec((tm, tk), lambda i, j, k: (i, k))

    def body(a_ref, b_ref, o_ref, *scr):
        if k_shards:
            part = sum(lax.dot_general(a_ref[:, s * ks:(s + 1) * ks], b_ref[s], NT, preferred_element_type=F32)
                       for s in range(k_shards))
        else:
            part = lax.dot_general(a_ref[...], b_ref[...], dn, preferred_element_type=F32)
        if nk == 1:
            o_ref[...] = part.astype(o_ref.dtype)
        else:
            acc = scr[0]
            k = pl.program_id(2)

            @pl.when(k == 0)
            def _():
                acc[...] = part

            @pl.when(k > 0)
            def _():
                acc[...] += part

            @pl.when(k == nk - 1)
            def _():
                o_ref[...] = acc[...].astype(o_ref.dtype)

    if k_shards:
        b_spec = pl.BlockSpec((k_shards, tn, ks), lambda i, j, k: (0, j, 0))
    elif b_shards:
        b_spec = pl.BlockSpec((None, tk, tn), lambda i, j, k: (j, k, 0))
    elif nt:
        b_spec = pl.BlockSpec((tn, tk), lambda i, j, k: (j, k))
    else:
        b_spec = pl.BlockSpec((tk, tn), lambda i, j, k: (k, j))
    if out_shards:
        out_shape = jax.ShapeDtypeStruct((N // tn, M, tn), out_dtype)
        out_spec = pl.BlockSpec((None, tm, tn), lambda i, j, k: (j, i, 0))
    else:
        out_shape = jax.ShapeDtypeStruct((M, N), out_dtype)
        out_spec = pl.BlockSpec((tm, tn), lambda i, j, k: (i, j))
    grid = (M // tm, N // tn, nk)
    body, x_in, x_in_specs, x_out, x_out_specs, x_scr = _carry(ex, grid, 2, 1, body)
    res = pl.pallas_call(
        body,
        out_shape=(out_shape, *x_out),
        grid=grid,
        in_specs=[a_spec, b_spec] + x_in_specs,
        out_specs=(out_spec, *x_out_specs),
        scratch_shapes=([pltpu.VMEM((tm, tn), F32)] if nk > 1 else []) + x_scr,
        compiler_params=_params(("parallel", "parallel", "arbitrary") if ex is None else ("arbitrary",) * 3),
        name=name,
    )(a, b, *x_in)
    return res[0] if ex is None else res


def _rstd(x):
    return lax.rsqrt(jnp.mean(x * x, axis=-1, keepdims=True) + EPS)


def _rms_fwd(x, g, *, name):
    T, D = x.shape
    tm = _tile(T, 512)

    def body(x_ref, g_ref, o_ref, ot_ref):
        x = x_ref[...]
        y = x * _rstd(x) * g_ref[...]
        o_ref[...] = y.astype(o_ref.dtype)
        ot_ref[...] = y.T.astype(ot_ref.dtype)

    return pl.pallas_call(
        body, out_shape=(jax.ShapeDtypeStruct((T, D), BF16), jax.ShapeDtypeStruct((D, T), BF16)), grid=(T // tm,),
        in_specs=[pl.BlockSpec((tm, D), lambda i: (i, 0)), pl.BlockSpec((1, D), lambda i: (0, 0))],
        out_specs=(pl.BlockSpec((tm, D), lambda i: (i, 0)), pl.BlockSpec((D, tm), lambda i: (0, i))),
        compiler_params=_params(("parallel",)), name=name)(x, g)


def _post_res_norm(a, g_post, h, g_pre, *, name):
    T, D = a.shape
    tm = _tile(T, 384, BLOCK)

    def body(a_ref, gp_ref, h_ref, gn_ref, h1_ref, o_ref, ot_ref):
        a = a_ref[...]
        h1 = h_ref[...] + a * _rstd(a) * gp_ref[...]
        h1_ref[...] = h1
        y = h1 * _rstd(h1) * gn_ref[...]
        o_ref[...] = y.astype(o_ref.dtype)
        ot_ref[...] = y.T.astype(ot_ref.dtype)

    row = pl.BlockSpec((tm, D), lambda i: (i, 0))
    vec = pl.BlockSpec((1, D), lambda i: (0, 0))
    return pl.pallas_call(
        body, out_shape=(jax.ShapeDtypeStruct((T, D), F32), jax.ShapeDtypeStruct((T, D), BF16),
                         jax.ShapeDtypeStruct((D, T), BF16)), grid=(T // tm,),
        in_specs=[row, vec, row, vec], out_specs=(row, row, pl.BlockSpec((D, tm), lambda i: (0, i))),
        compiler_params=_params(("parallel",)), name=name)(a, g_post, h, g_pre)


def _loss_head(a, g, h, target, *, name):
    T, D = a.shape
    tm = _tile(T, 512)

    def body(a_ref, g_ref, h_ref, t_ref, dy_ref, da_ref, dg_ref, loss_ref):
        i = pl.program_id(0)
        a = a_ref[...]
        r = _rstd(a)
        ah = a * r
        y = h_ref[...] + ah * g_ref[...]
        rows = i * tm + lax.broadcasted_iota(jnp.int32, (tm, 1), 0)
        err = jnp.where(rows >= BLOCK, y - t_ref[...], 0.0)
        dy = err / D
        dy_ref[...] = dy
        dah = dy * g_ref[...]
        da_ref[...] = (r * (dah - ah * jnp.mean(dah * ah, axis=-1, keepdims=True))).astype(da_ref.dtype)
        part = jnp.sum(jnp.sum(err * err, axis=1, keepdims=True), axis=0, keepdims=True)

        @pl.when(i == 0)
        def _():
            loss_ref[...] = jnp.zeros_like(loss_ref)
            dg_ref[...] = jnp.zeros_like(dg_ref)

        loss_ref[...] += jnp.broadcast_to(part, loss_ref.shape)
        dg_ref[...] += jnp.sum(dy * ah, axis=0, keepdims=True)

    row = pl.BlockSpec((tm, D), lambda i: (i, 0))
    vec = pl.BlockSpec((1, D), lambda i: (0, 0))
    return pl.pallas_call(
        body, out_shape=(jax.ShapeDtypeStruct((T, D), F32), jax.ShapeDtypeStruct((T, D), BF16),
                         jax.ShapeDtypeStruct((1, D), F32), jax.ShapeDtypeStruct((8, 128), F32)),
        grid=(T // tm,),
        in_specs=[row, vec, row, row],
        out_specs=(row, row, vec, pl.BlockSpec((8, 128), lambda i: (0, 0))),
        compiler_params=_params(("arbitrary",)), name=name)(a, g, h, target)


def _rms_bwd(x, g, dy, res, *, out_dtype, dy2=None, then=None, ex=None, name):
    T, D = x.shape
    tm = _tile(T, 512)
    has_res = res is not None
    has_dy2 = 2 if dy2 is not None else 0
    n_in = 3 + has_dy2 + has_res + (2 if then is not None else 0)
    n_out = 2 + (2 if then is not None else 0)

    def pull_back(x, g, dy):
        r = _rstd(x)
        xh = x * r
        dxh = dy * g
        return r * (dxh - xh * jnp.mean(dxh * xh, axis=-1, keepdims=True)), jnp.sum(dy * xh, axis=0, keepdims=True)

    def body(*refs):
        ins, outs = refs[:n_in], refs[n_in:]
        i = pl.program_id(0)

        @pl.when(i == 0)
        def _():
            for ref in outs[1::2]:
                ref[...] = jnp.zeros_like(ref)

        dy_all = ins[2][...].astype(F32)
        if has_dy2:
            dy_all = dy_all + lax.dot_general(ins[3][...], ins[4][...], NT, preferred_element_type=F32)
        dx, dg = pull_back(ins[0][...], ins[1][...], dy_all)
        if has_res:
            dx = dx + ins[3 + has_dy2][...]
        outs[0][...] = dx.astype(outs[0].dtype)
        outs[1][...] += dg
        if then is not None:
            dx2, dg2 = pull_back(ins[n_in - 2][...], ins[n_in - 1][...], dx)
            outs[2][...] = dx2.astype(outs[2].dtype)
            outs[3][...] += dg2

    row = pl.BlockSpec((tm, D), lambda i: (i, 0))
    vec = pl.BlockSpec((1, D), lambda i: (0, 0))
    ins = [x, g, dy] + (list(dy2) if has_dy2 else []) + ([res] if has_res else []) + (list(then) if then is not None else [])
    dy2_specs = ([pl.BlockSpec((tm, dy2[0].shape[1]), lambda i: (i, 0)), pl.BlockSpec(dy2[1].shape, lambda i: (0, 0))]
                 if has_dy2 else [])
    in_specs = [row, vec, row] + dy2_specs + ([row] if has_res else []) + ([row, vec] if then is not None else [])
    out_shape = [jax.ShapeDtypeStruct((T, D), out_dtype), jax.ShapeDtypeStruct((1, D), F32)]
    out_specs = [row, vec]
    if then is not None:
        out_shape += [jax.ShapeDtypeStruct((T, D), BF16), jax.ShapeDtypeStruct((1, D), F32)]
        out_specs += [row, vec]
    grid = (T // tm,)
    body, x_in, x_in_specs, x_out, x_out_specs, x_scr = _carry(ex, grid, n_in, n_out, body)
    return pl.pallas_call(
        body, out_shape=(*out_shape, *x_out), grid=grid,
        in_specs=in_specs + x_in_specs, out_specs=(*out_specs, *x_out_specs), scratch_shapes=x_scr,
        compiler_params=_params(("arbitrary",)), name=name)(*ins, *x_in)


def _gate_up_swiglu(a, w, *, name):
    T, D = a.shape
    S, n = w.shape[0] // 2, w.shape[2]
    tm = _tile(T, 1408, BLOCK)

    def body(a_ref, wg_ref, wu_ref, g_ref, u_ref, o_ref, ot_ref):
        x = a_ref[...]
        g = jnp.dot(x, wg_ref[...], preferred_element_type=F32)
        u = jnp.dot(x, wu_ref[...], preferred_element_type=F32)
        g16, u16 = g.astype(BF16), u.astype(BF16)
        g_ref[...] = g16
        u_ref[...] = u16
        gr = g16.astype(F32)
        act = gr / (1.0 + jnp.exp(-gr)) * u16.astype(F32)
        o_ref[...] = act.astype(o_ref.dtype)
        ot_ref[...] = act.T.astype(ot_ref.dtype)

    tile = pl.BlockSpec((tm, n), lambda i, j: (i, j))
    shp = jax.ShapeDtypeStruct((T, S * n), BF16)
    return pl.pallas_call(
        body, out_shape=(shp, shp, shp, jax.ShapeDtypeStruct((S * n, T), BF16)), grid=(T // tm, S),
        in_specs=[pl.BlockSpec((tm, D), lambda i, j: (i, 0)),
                  pl.BlockSpec((None, D, n), lambda i, j: (j, 0, 0)),
                  pl.BlockSpec((None, D, n), lambda i, j: (j + S, 0, 0))],
        out_specs=(tile, tile, tile, pl.BlockSpec((n, tm), lambda i, j: (j, i))),
        compiler_params=_params(("parallel", "parallel")), name=name)(a, w, w)


def _d_act_swiglu(dff, w_down, gate, up, *, name):
    T, D = dff.shape
    F = w_down.shape[0]
    tm = _tile(T, 384)
    tf = _tile(F, 768, BLOCK)

    def body(d_ref, w_ref, g_ref, u_ref, o_ref):
        dy = d_ref[...]
        for c in range(0, F, tf):
            d = lax.dot_general(dy, w_ref[c:c + tf, :], NT, preferred_element_type=F32)
            g = g_ref[:, c:c + tf].astype(F32)
            u = u_ref[:, c:c + tf].astype(F32)
            sg = 1.0 / (1.0 + jnp.exp(-g))
            o_ref[:, c:c + tf] = (d * u * (sg * (1.0 + g * (1.0 - sg)))).astype(o_ref.dtype)
            o_ref[:, F + c:F + c + tf] = (d * (g * sg)).astype(o_ref.dtype)

    row = pl.BlockSpec((tm, F), lambda i: (i, 0))
    return pl.pallas_call(
        body, out_shape=jax.ShapeDtypeStruct((T, 2 * F), BF16), grid=(T // tm,),
        in_specs=[pl.BlockSpec((tm, D), lambda i: (i, 0)), pl.BlockSpec((F, D), lambda i: (0, 0)), row, row],
        out_specs=pl.BlockSpec((tm, 2 * F), lambda i: (i, 0)),
        compiler_params=_params(("parallel",)), name=name)(dff, w_down, gate, up)


def _fox_gates_fwd(f_t, b, *, name):
    H, T = f_t.shape
    nb = T // BLOCK

    def body(f_ref, b_ref, col_ref):
        f = f_ref[...] + b_ref[...]
        ls = jnp.minimum(f, 0.0) - jnp.log(1.0 + jnp.exp(-jnp.abs(f)))
        t = lax.broadcasted_iota(jnp.int32, (H, T), 1)
        ls = jnp.where(t >= PAD_ROWS, ls, 0.0)
        upper = (lax.broadcasted_iota(jnp.int32, (BLOCK, BLOCK), 0)
                 <= lax.broadcasted_iota(jnp.int32, (BLOCK, BLOCK), 1)).astype(F32)
        carry = jnp.zeros((H, 1), F32)
        for blk in range(nb):
            seg = ls[:, blk * BLOCK:(blk + 1) * BLOCK]
            pre = jnp.dot(seg, upper, precision=HIGHEST, preferred_element_type=F32) + carry
            key_gate = jnp.where(t[:, blk * BLOCK:(blk + 1) * BLOCK] >= PAD_ROWS, pre, -NEG)
            terms = list(_split3(pre)) + list(_split3(key_gate))
            col_ref[blk * BLOCK:(blk + 1) * BLOCK, :] = jnp.concatenate(
                terms + [jnp.zeros((BLOCK - len(terms) * H, BLOCK), F32)], axis=0).T.astype(col_ref.dtype)
            carry = pre[:, BLOCK - 1:BLOCK]

    vm = pl.BlockSpec(memory_space=pltpu.VMEM)
    return pl.pallas_call(
        body, out_shape=jax.ShapeDtypeStruct((T, BLOCK), BF16),
        in_specs=[vm, vm], out_specs=vm,
        compiler_params=_params(), name=name)(f_t, b)


def _fox_gates_bwd(dcq, dck, f_t, b, *, name):
    H, T = f_t.shape
    nb = T // BLOCK

    def body(dq_ref, d_ref, f_ref, b_ref, df_ref, db_ref):
        lower = (lax.broadcasted_iota(jnp.int32, (BLOCK, BLOCK), 0)
                 >= lax.broadcasted_iota(jnp.int32, (BLOCK, BLOCK), 1)).astype(F32)
        carry = jnp.zeros((H, 1), F32)
        for blk in range(nb - 1, -1, -1):
            seg = dq_ref[:, blk * BLOCK:(blk + 1) * BLOCK] - d_ref[:, blk * BLOCK:(blk + 1) * BLOCK]
            suf = jnp.dot(seg, lower, precision=HIGHEST, preferred_element_type=F32) + carry
            df_ref[:, blk * BLOCK:(blk + 1) * BLOCK] = suf
            carry = suf[:, 0:1]
        f = f_ref[...] + b_ref[...]
        t = lax.broadcasted_iota(jnp.int32, (H, T), 1)
        df = jnp.where(t >= PAD_ROWS, df_ref[...] / (1.0 + jnp.exp(f)), 0.0)
        df_ref[...] = df
        db_ref[...] = jnp.sum(df, axis=1, keepdims=True)

    vm = pl.BlockSpec(memory_space=pltpu.VMEM)
    return pl.pallas_call(
        body, out_shape=(jax.ShapeDtypeStruct((H, T), F32), jax.ShapeDtypeStruct((H, 1), F32)),
        in_specs=[vm, vm, vm, vm], out_specs=(vm, vm),
        compiler_params=_params(), name=name)(dcq, dck, f_t, b)


def _fox_lanes(parity):
    base = HEAD_DIM * (1 - parity)
    return base, base + 3


def _split3(c):
    hi = c.astype(BF16).astype(F32)
    r = c - hi
    mid = r.astype(BF16).astype(F32)
    lo = (r - mid).astype(BF16).astype(F32)
    return hi, mid, lo


def _lanes(lane, parity, data, start, terms, ones_at=None, fill=1.0):
    out = jnp.zeros((), F32) if ones_at is None else jnp.where((lane >= ones_at) & (lane < ones_at + 3), fill, 0.0)
    for i, t in enumerate(terms):
        out = jnp.where(lane == start + i, t, out)
    return jnp.where(lane // HEAD_DIM == parity, data, out)


def _fox_prep(proj, cum_col, *, name):
    T = proj.shape[0]
    tm = _tile(T, 1408, BLOCK)
    nt = T // tm
    H = FOX_HEADS
    lanes = 2 * HEAD_DIM
    first = (proj.shape[1] - 3 * H * HEAD_DIM) // lanes

    def body(q_ref, k_ref, v_ref, c_ref, qa_ref, ka_ref, va_ref):
        p = pl.program_id(0)
        i = pl.program_id(1)
        lane = lax.broadcasted_iota(jnp.int32, (1, lanes), 1)
        src = lax.broadcasted_iota(jnp.int32, (lanes, lanes), 0)
        dst = lax.broadcasted_iota(jnp.int32, (lanes, lanes), 1)
        q2 = q_ref[...].astype(F32) * SCALE
        k2 = k_ref[...].astype(F32)
        v2 = v_ref[...].astype(F32)
        gates = c_ref[...]
        def placed(h, first_term, start):
            pick = ((src % FOX_HEADS == h) & (src // FOX_HEADS - first_term == dst - start)
                    & (dst >= start) & (dst < start + 3))
            return jnp.dot(gates, pick.astype(BF16), preferred_element_type=F32)

        moved = [(placed(2 * p + e, 0, _fox_lanes(e)[1]), placed(2 * p + e, 3, _fox_lanes(e)[0])) for e in range(2)]
        for e in range(2):
            kc, qc = _fox_lanes(e)
            own = lane // HEAD_DIM == e
            minus = jnp.where((lane >= kc) & (lane < kc + 3), -1.0, 0.0)
            ones_q = jnp.where((lane >= qc) & (lane < qc + 3), 1.0, 0.0)
            ones_k = jnp.where((lane >= kc) & (lane < kc + 3), 1.0, 0.0)
            qa_ref[e] = jnp.where(own, q2, moved[e][0] + minus).astype(BF16)
            ka_ref[e] = jnp.where(own, k2, moved[e][1] + ones_q).astype(BF16)
            va_ref[e] = jnp.where(own, v2, ones_k).astype(BF16)

    pairs = FOX_GROUP // 2

    def col(part):
        return pl.BlockSpec((tm, lanes),
                            lambda p, i: (i, first + 3 * pairs * (p // pairs) + part * pairs + p % pairs))

    out = pl.BlockSpec((2, tm, lanes), lambda p, i: (p, i, 0))
    shp = jax.ShapeDtypeStruct((H, T, lanes), BF16)
    return pl.pallas_call(
        body, out_shape=(shp, shp, shp), grid=(H // 2, nt),
        in_specs=[col(0), col(1), col(2), pl.BlockSpec((tm, lanes), lambda p, i: (i, 0))],
        out_specs=(out, out, out),
        compiler_params=_params(("parallel", "parallel")), name=name)(proj, proj, proj, cum_col)


def _fox_fwd(q_aug, k_aug, v_aug, mix, *, ex=None, name):
    H, T, lanes = q_aug.shape
    tq = FOX_TILE
    nq = T // tq
    G = FOX_HEADS

    def body(q_ref, k_ref, v_ref, mix_ref, o_ref, lse_ref, m_scr, acc_scr):
        i = pl.program_id(1)
        m_scr[...] = jnp.full(m_scr.shape, NEG, F32)
        acc_scr[...] = jnp.zeros(acc_scr.shape, F32)

        def step(kb, diag):
            off = pl.multiple_of(kb * tq, tq)
            s_t = [lax.dot_general(k_ref[g, pl.ds(off, tq), :], q_ref[g], NT, preferred_element_type=F32)
                   for g in range(G)]
            if diag:
                r = lax.broadcasted_iota(jnp.int32, (tq, tq), 0)
                c = lax.broadcasted_iota(jnp.int32, (tq, tq), 1)
                s_t = [jnp.where(c >= r, s, NEG) for s in s_t]
            m_prev = [m_scr[g] for g in range(G)]
            m_new = [jnp.maximum(m_prev[g], jnp.max(s_t[g], axis=0, keepdims=True)) for g in range(G)]
            p_t = [jnp.exp(s_t[g] - m_new[g]).astype(BF16) for g in range(G)]
            pv = [lax.dot_general(v_ref[g, pl.ds(off, tq), :], p_t[g], TN, preferred_element_type=F32)
                  for g in range(G)]
            for g in range(G):
                acc_scr[g] = jnp.exp(m_prev[g] - m_new[g]) * acc_scr[g] + pv[g]
                m_scr[g] = m_new[g]

        def loop_body(kb, carry):
            step(kb, False)
            return carry

        lax.fori_loop(0, i, loop_body, 0)
        step(i, True)
        lane = lax.broadcasted_iota(jnp.int32, (tq, lanes), 1)
        outs = []
        for g in range(G):
            ones = _fox_lanes(g % 2)[0]
            acc = acc_scr[g]
            lse_ref[g] = m_scr[g] + jnp.log(acc[ones:ones + 1, :])
            acc_t = acc.T
            outs.append(acc_t / acc_t[:, ones:ones + 1])
        for pair in range(G // 2):
            o_ref[:, pair * lanes:(pair + 1) * lanes] = jnp.where(
                lane < HEAD_DIM, outs[2 * pair], outs[2 * pair + 1]).astype(o_ref.dtype)

    blk = pl.BlockSpec((G, tq, lanes), lambda h, i: (h, i, 0))
    full = pl.BlockSpec((G, T, lanes), lambda h, i: (h, 0, 0))
    grid = (H // G, nq)
    first = mix.shape[1] // (G * HEAD_DIM) - H // G
    body, x_in, x_in_specs, x_out, x_out_specs, x_scr = _carry(ex, grid, 4, 2, body)
    return pl.pallas_call(
        body,
        out_shape=(jax.ShapeDtypeStruct(mix.shape, mix.dtype), jax.ShapeDtypeStruct((H, nq, 1, tq), F32), *x_out),
        grid=grid,
        in_specs=[blk, full, full, pl.BlockSpec(memory_space=pl.ANY)] + x_in_specs,
        out_specs=(pl.BlockSpec((tq, G * HEAD_DIM), lambda h, i: (i, first + h)),
                   pl.BlockSpec((G, None, 1, tq), lambda h, i: (h, i, 0, 0)), *x_out_specs),
        input_output_aliases={3: 0},
        scratch_shapes=[pltpu.VMEM((G, 1, tq), F32), pltpu.VMEM((G, lanes, tq), F32)] + x_scr,
        compiler_params=_params(("arbitrary", "arbitrary")), name=name)(q_aug, k_aug, v_aug, mix, *x_in)


def _fox_prep_bwd(dmix, mix, *, name):
    T = dmix.shape[0]
    H = FOX_HEADS
    tm = _tile(T, 1408, BLOCK)
    lanes = 2 * HEAD_DIM
    first = mix.shape[1] // lanes - H // 2

    def body(d_ref, o_ref, da_ref):
        lane = lax.broadcasted_iota(jnp.int32, (1, lanes), 1)
        d2 = d_ref[...].astype(F32)
        prod = d2 * o_ref[...].astype(F32)
        for e in range(2):
            delta = jnp.sum(jnp.where(lane // HEAD_DIM == e, prod, 0.0), axis=1, keepdims=True)
            da_ref[e] = _lanes(lane, e, d2, _fox_lanes(e)[0], _split3(-delta)).astype(BF16)

    pair = pl.BlockSpec((tm, lanes), lambda p, i: (i, first + p))
    return pl.pallas_call(
        body, out_shape=jax.ShapeDtypeStruct((H, T, lanes), BF16), grid=(H // 2, T // tm),
        in_specs=[pair, pair],
        out_specs=pl.BlockSpec((2, tm, lanes), lambda p, i: (p, i, 0)),
        compiler_params=_params(("parallel", "parallel")), name=name)(dmix, mix)


def _fox_bwd(q_aug, k_aug, v_aug, do_aug, lse_row, dproj, *, ex=None, name):
    H, T, lanes = q_aug.shape
    tq = FOX_TILE
    nq = T // tq
    G = FOX_GROUP

    def side_by_side(tiles, scale=None):
        lane = lax.broadcasted_iota(jnp.int32, tiles[0].shape, 1)
        out = [jnp.where(lane < HEAD_DIM, tiles[2 * p], tiles[2 * p + 1]) for p in range(G // 2)]
        out = jnp.concatenate(out, axis=1)
        return out if scale is None else out * scale

    def body(q_ref, k_ref, v_ref, do_ref, lse_ref, dproj_in, out_ref, dcq_ref, dck_ref, dk_acc, dv_acc, dq_ref):
        j = pl.program_id(1)

        @pl.when(j == 0)
        def _():
            dq_ref[...] = jnp.zeros(dq_ref.shape, F32)
            dcq_ref[...] = jnp.zeros(dcq_ref.shape, F32)

        dk_acc[...] = jnp.zeros(dk_acc.shape, F32)
        dv_acc[...] = jnp.zeros(dv_acc.shape, F32)

        def step(qb, diag):
            off = pl.multiple_of(qb * tq, tq)
            heads = range(G)
            qa = [q_ref[g, pl.ds(off, tq), :] for g in heads]
            da = [do_ref[g, pl.ds(off, tq), :] for g in heads]
            s_t = [lax.dot_general(k_ref[g], qa[g], NT, preferred_element_type=F32) for g in heads]
            dp_t = [lax.dot_general(v_ref[g], da[g], NT, preferred_element_type=F32) for g in heads]
            p_t = [jnp.exp(s_t[g] - lse_ref[g, qb]) for g in heads]
            if diag:
                r = lax.broadcasted_iota(jnp.int32, (tq, tq), 0)
                c = lax.broadcasted_iota(jnp.int32, (tq, tq), 1)
                p_t = [jnp.where(c >= r, p, 0.0) for p in p_t]
            dsb = [(p_t[g] * dp_t[g]).astype(BF16) for g in heads]
            dv = [jnp.dot(p_t[g].astype(BF16), da[g], preferred_element_type=F32) for g in heads]
            dk = [jnp.dot(dsb[g], qa[g], preferred_element_type=F32) for g in heads]
            dq = [lax.dot_general(k_ref[g], dsb[g], TN, preferred_element_type=F32) for g in heads]
            for g in heads:
                dv_acc[g] += dv[g]
                dk_acc[g] += dk[g]
                dq_ref[g, qb] += dq[g]
                dcq_ref[g, qb] += jnp.sum(dsb[g].astype(F32), axis=0, keepdims=True)

        step(j, True)

        def loop_body(qb, carry):
            step(qb, False)
            return carry

        lax.fori_loop(j + 1, nq, loop_body, 0)
        dk = [dk_acc[g] for g in range(G)]
        out_ref[:, 0:wide] = side_by_side([dq_ref[g, j].T for g in range(G)], SCALE).astype(out_ref.dtype)
        out_ref[:, wide:2 * wide] = side_by_side(dk).astype(out_ref.dtype)
        out_ref[:, 2 * wide:3 * wide] = side_by_side([dv_acc[g] for g in range(G)]).astype(out_ref.dtype)
        for g in range(G):
            kc = _fox_lanes(g % 2)[0]
            dck_ref[g] = -dk[g].T[kc:kc + 1, :]

    blk = pl.BlockSpec((G, tq, lanes), lambda h, j: (h, j, 0))
    full = pl.BlockSpec((G, T, lanes), lambda h, j: (h, 0, 0))
    wide = G * HEAD_DIM
    first = dproj.shape[1] // (3 * wide) - H // G
    grid = (H // G, nq)
    body, x_in, x_in_specs, x_out, x_out_specs, x_scr = _carry(ex, grid, 6, 3, body)
    rows = jax.ShapeDtypeStruct((H, nq, 1, tq), F32)
    all_rows = pl.BlockSpec((G, nq, 1, tq), lambda h, j: (h, 0, 0, 0))
    return pl.pallas_call(
        body,
        out_shape=(jax.ShapeDtypeStruct(dproj.shape, dproj.dtype), rows, rows, *x_out),
        grid=grid,
        in_specs=[full, blk, blk, full, all_rows, pl.BlockSpec(memory_space=pl.ANY)] + x_in_specs,
        out_specs=(pl.BlockSpec((tq, 3 * wide), lambda h, j: (j, first + h)), all_rows,
                   pl.BlockSpec((G, None, 1, tq), lambda h, j: (h, j, 0, 0)), *x_out_specs),
        input_output_aliases={5: 0},
        scratch_shapes=[pltpu.VMEM((G, tq, lanes), F32), pltpu.VMEM((G, tq, lanes), F32),
                        pltpu.VMEM((G, nq, lanes, tq), F32)] + x_scr,
        compiler_params=_params(("arbitrary", "arbitrary")), name=name,
    )(q_aug, k_aug, v_aug, do_aug, lse_row, dproj, *x_in)


def _t5_bucket_np(d):
    n = np.maximum(d, 0).astype(np.int32)
    max_exact = N_BUCKETS // 2
    nf = np.maximum(n, 1).astype(np.float32)
    large = max_exact + (np.log(nf / max_exact) / math.log(MAX_DISTANCE / max_exact)
                         * (N_BUCKETS - max_exact)).astype(np.int32)
    large = np.minimum(large, N_BUCKETS - 1)
    return np.where(n < max_exact, n, large)


def _bucket_onehots():
    k = np.arange(BLOCK)[:, None]
    q = np.arange(BLOCK)[None, :]
    eye = np.eye(N_BUCKETS, dtype=np.float32)
    cur = eye[_t5_bucket_np(q - k).reshape(-1)]
    prev = eye[_t5_bucket_np(BLOCK + q - k).reshape(-1)]
    return cur, prev


SWA_K_COL = SWA_Q_HEADS * HEAD_DIM // (2 * HEAD_DIM)
SWA_V_COL = SWA_K_COL + 1


def _swa_terms(raw, bc, bp, far, sink, n):
    k = lax.broadcasted_iota(jnp.int32, (BLOCK, BLOCK), 0)
    q = lax.broadcasted_iota(jnp.int32, (BLOCK, BLOCK), 1)
    never = 2 * BLOCK
    s_c = raw[0] + bc
    s_p = raw[1] + bp
    s_m = raw[2] + jnp.where(n == 1, bp, far)
    s_c = jnp.where((k <= q) & (k >= jnp.where(n >= 1, 0, PAD_ROWS)), s_c, NEG)
    s_p = jnp.where(k > q + jnp.where(n >= 2, 0, never), s_p, NEG)
    s_m = jnp.where(k >= jnp.where(n >= 1, PAD_ROWS, never), s_m, NEG)
    m = jnp.maximum(jnp.maximum(jnp.max(s_c, axis=0, keepdims=True), jnp.max(s_p, axis=0, keepdims=True)),
                    jnp.maximum(jnp.max(s_m, axis=0, keepdims=True), sink))
    e = [jnp.exp(s_c - m), jnp.exp(s_p - m), jnp.exp(s_m - m)]
    e_s = jnp.exp(sink - m)
    l = (jnp.sum(e[0], axis=0, keepdims=True) + jnp.sum(e[1], axis=0, keepdims=True)
         + jnp.sum(e[2], axis=0, keepdims=True) + e_s)
    return e, e_s, l


SWA_STEP = 3


def _swa_specs():
    R = SWA_STEP

    def window(col):
        return ([pl.BlockSpec((BLOCK, BLOCK), lambda s, w=w: (jnp.maximum(R * s - 1 + w, 0), col)) for w in range(R + 1)]
                + [pl.BlockSpec((BLOCK, BLOCK), lambda s: (0, col))])

    qblk = pl.BlockSpec((R * BLOCK, SWA_Q_HEADS * HEAD_DIM), lambda s: (s, 0))
    bias = pl.BlockSpec((SWA_Q_HEADS, BLOCK, BLOCK), lambda s: (0, 0, 0))
    smem = pl.BlockSpec(memory_space=pltpu.SMEM)
    return qblk, window(SWA_K_COL), window(SWA_V_COL), bias, smem


def _swa_own_kv(tile_ref, kv):
    lane = lax.broadcasted_iota(jnp.int32, (BLOCK, 2 * HEAD_DIM), 1)
    t = tile_ref[...].astype(F32)
    return jnp.where(lane // HEAD_DIM == kv, t, pltpu.roll(t, HEAD_DIM, 1)).astype(BF16)


def _swa_fwd(proj, bc, bp, far, sinks, *, name):
    T = proj.shape[0]
    nb = T // BLOCK
    G = SWA_GROUP
    Hq = SWA_Q_HEADS
    lanes = 2 * HEAD_DIM

    R = SWA_STEP
    assert nb % R == 0

    def body(*refs):
        q_ref, k_refs, v_refs = refs[0], refs[1:R + 3], refs[R + 3:2 * R + 5]
        bc_ref, bp_ref, far_ref, sink_ref, o_ref = refs[2 * R + 5:]
        s = pl.program_id(0)
        lane = lax.broadcasted_iota(jnp.int32, (BLOCK, lanes), 1)
        kvs = range(SWA_KV_HEADS)
        kk = [[_swa_own_kv(ref, kv) for ref in k_refs] for kv in kvs]
        vv = [[_swa_own_kv(ref, kv) for ref in v_refs] for kv in kvs]
        chains = [(r, h) for r in range(R) for h in range(Hq)]
        tiles = lambda r: (r + 1, r, R + 1)
        q2 = {(r, pair): q_ref[r * BLOCK:(r + 1) * BLOCK, pair * lanes:(pair + 1) * lanes].astype(F32) * SCALE
              for r in range(R) for pair in range(Hq // 2)}
        qm = {c: jnp.where(lane // HEAD_DIM == c[1] % 2, q2[c[0], c[1] // 2], 0.0).astype(BF16) for c in chains}
        raw = {c: [lax.dot_general(kk[c[1] // G][w], qm[c], NT, preferred_element_type=F32) for w in tiles(c[0])]
               for c in chains}
        terms = {c: _swa_terms(raw[c], bc_ref[c[1]], bp_ref[c[1]], far_ref[c[1]], sink_ref[c[1]], R * s + c[0])
                 for c in chains}
        o_t = {c: sum(lax.dot_general(vv[c[1] // G][w], terms[c][0][b].astype(BF16), TN, preferred_element_type=F32)
                      for b, w in enumerate(tiles(c[0]))) for c in chains}
        outs = {c: (o_t[c] / terms[c][2]).T for c in chains}
        for r in range(R):
            for pair in range(Hq // 2):
                o_ref[r * BLOCK:(r + 1) * BLOCK, pair * lanes:(pair + 1) * lanes] = jnp.where(
                    lane < HEAD_DIM, outs[r, 2 * pair], outs[r, 2 * pair + 1]).astype(o_ref.dtype)

    qblk, keys, vals, bias, smem = _swa_specs()
    return pl.pallas_call(
        body, out_shape=jax.ShapeDtypeStruct((T, D_MODEL), BF16), grid=(nb // R,),
        in_specs=[qblk] + keys + vals + [bias, bias, smem, smem],
        out_specs=qblk,
        compiler_params=_params(("parallel",)), name=name,
    )(proj, *([proj] * (2 * R + 4)), bc, bp, far, sinks)


def _swa_bwd(proj, dmix, bc, bp, far, sinks, *, ex=None, name):
    T, width = proj.shape
    nb = T // BLOCK
    G = SWA_GROUP
    Hq = SWA_Q_HEADS
    lanes = 2 * HEAD_DIM
    qw = Hq * HEAD_DIM
    own_w = qw + 2 * lanes

    R = SWA_STEP
    assert nb % R == 0
    n_in = 2 * R + 10

    def body(*refs):
        q_ref, k_refs, v_refs = refs[0], refs[1:R + 3], refs[R + 3:2 * R + 5]
        do_ref, bc_ref, bp_ref, far_ref, sink_ref = refs[2 * R + 5:n_in]
        dp_ref, dbc_ref, dbp_ref, dbf_ref, dsk_ref, dk_acc, dv_acc = refs[n_in:]
        s = pl.program_id(0)

        @pl.when(s == 0)
        def _():
            for ref in (dk_acc, dv_acc, dbc_ref, dbp_ref, dbf_ref, dsk_ref):
                ref[...] = jnp.zeros(ref.shape, F32)

        lane = lax.broadcasted_iota(jnp.int32, (BLOCK, lanes), 1)
        kvs = range(SWA_KV_HEADS)
        kk = [[_swa_own_kv(ref, kv) for ref in k_refs] for kv in kvs]
        vv = [[_swa_own_kv(ref, kv) for ref in v_refs] for kv in kvs]
        chains = [(r, h) for r in range(R) for h in range(Hq)]
        blocks = range(3)
        tiles = lambda r: (r + 1, r, R + 1)
        sub = lambda ref, r, pair: ref[r * BLOCK:(r + 1) * BLOCK, pair * lanes:(pair + 1) * lanes]
        q2 = {(r, pair): sub(q_ref, r, pair).astype(F32) * SCALE for r in range(R) for pair in range(Hq // 2)}
        d2 = {(r, pair): sub(do_ref, r, pair) for r in range(R) for pair in range(Hq // 2)}
        own = [lane // HEAD_DIM == half for half in range(2)]
        qm = {c: jnp.where(own[c[1] % 2], q2[c[0], c[1] // 2], 0.0).astype(BF16) for c in chains}
        dom = {c: jnp.where(own[c[1] % 2], d2[c[0], c[1] // 2], jnp.zeros_like(d2[0, 0])) for c in chains}
        raw = {c: [lax.dot_general(kk[c[1] // G][w], qm[c], NT, preferred_element_type=F32) for w in tiles(c[0])]
               for c in chains}
        dp = {c: [lax.dot_general(vv[c[1] // G][w], dom[c], NT, preferred_element_type=F32) for w in tiles(c[0])]
              for c in chains}
        p, ds16 = {}, {}
        for c in chains:
            r, h = c
            n = R * s + r
            e, e_s, l = _swa_terms(raw[c], bc_ref[h], bp_ref[h], far_ref[h], sink_ref[h], n)
            inv = 1.0 / l
            ph = [e[b] * inv for b in blocks]
            delta = sum(jnp.sum(ph[b] * dp[c][b], axis=0, keepdims=True) for b in blocks)
            ds = [ph[b] * (dp[c][b] - delta) for b in blocks]
            dsk_ref[h] += -(e_s * inv) * delta
            dbc_ref[h] += ds[0]
            dbp_ref[h] += ds[1] + jnp.where(n == 1, ds[2], 0.0)
            dbf_ref[h] += jnp.where(n >= 2, ds[2], 0.0)
            p[c] = [x.astype(BF16) for x in ph]
            ds16[c] = [x.astype(BF16) for x in ds]
        dq_t = {c: sum(lax.dot_general(kk[c[1] // G][w], ds16[c][b], TN, preferred_element_type=F32)
                       for b, w in enumerate(tiles(c[0]))) for c in chains}
        group = [range(kv * G, (kv + 1) * G) for kv in kvs]
        dk = {(r, kv): [sum(jnp.dot(ds16[r, h][b], qm[r, h], preferred_element_type=F32) for h in group[kv])
                        for b in blocks] for r in range(R) for kv in kvs}
        dv = {(r, kv): [sum(jnp.dot(p[r, h][b], dom[r, h], preferred_element_type=F32) for h in group[kv])
                        for b in blocks] for r in range(R) for kv in kvs}
        for r in range(R):
            n = R * s + r
            rows = pl.ds(pl.multiple_of(n * BLOCK, BLOCK), BLOCK)
            prev_rows = pl.ds(pl.multiple_of(jnp.maximum(n - 1, 0) * BLOCK, BLOCK), BLOCK)
            for pair in range(Hq // 2):
                dp_ref[rows, pair * lanes:(pair + 1) * lanes] = (jnp.where(
                    lane < HEAD_DIM, dq_t[r, 2 * pair].T, dq_t[r, 2 * pair + 1].T) * SCALE).astype(dp_ref.dtype)
            for acc, ref in ((dk, dk_acc), (dv, dv_acc)):
                tot = [[a + pltpu.roll(a, HEAD_DIM, 1) for a in acc[r, kv]] for kv in kvs]
                both = [jnp.where(lane < HEAD_DIM, tot[0][b], tot[1][b]) for b in blocks]
                ref[rows, :] += both[0]
                ref[prev_rows, :] += both[1]
                ref[0:BLOCK, :] += both[2]

        @pl.when(s == nb // R - 1)
        def _():
            dp_ref[:, qw:qw + lanes] = dk_acc[...].astype(dp_ref.dtype)
            dp_ref[:, qw + lanes:own_w] = dv_acc[...].astype(dp_ref.dtype)

    qblk, keys, vals, bias, smem = _swa_specs()
    dsk = pl.BlockSpec((Hq, 1, BLOCK), lambda s: (0, 0, 0))
    grid = (nb // R,)
    body, x_in, x_in_specs, x_out, x_out_specs, x_scr = _carry(ex, grid, n_in, 5, body)
    tile = jax.ShapeDtypeStruct((Hq, BLOCK, BLOCK), F32)
    return pl.pallas_call(
        body,
        out_shape=(jax.ShapeDtypeStruct((T, width), BF16), tile, tile, tile,
                   jax.ShapeDtypeStruct((Hq, 1, BLOCK), F32), *x_out),
        grid=grid,
        in_specs=[qblk] + keys + vals + [qblk, bias, bias, smem, smem] + x_in_specs,
        out_specs=(pl.BlockSpec((T, own_w), lambda s: (0, 0)), bias, bias, bias, dsk, *x_out_specs),
        scratch_shapes=[pltpu.VMEM((T, lanes), F32), pltpu.VMEM((T, lanes), F32)] + x_scr,
        compiler_params=_params(("arbitrary",)), name=name,
    )(proj, *([proj] * (2 * R + 4)), dmix, bc, bp, far, sinks, *x_in)


def _bias_tiles(tab_t, oh_cur_t, oh_prev_t, *, name):
    Hq = tab_t.shape[0]

    def body(t_ref, oc_ref, op_ref, bc_ref, bp_ref):
        bc_ref[...] = jnp.dot(t_ref[...], oc_ref[...], precision=HIGHEST, preferred_element_type=F32)
        bp_ref[...] = jnp.dot(t_ref[...], op_ref[...], precision=HIGHEST, preferred_element_type=F32)

    vm = pl.BlockSpec(memory_space=pltpu.VMEM)
    shp = jax.ShapeDtypeStruct((Hq, BLOCK * BLOCK), F32)
    bc, bp = pl.pallas_call(body, out_shape=(shp, shp), in_specs=[vm] * 3, out_specs=(vm, vm),
                            compiler_params=_params(), name=name)(tab_t, oh_cur_t, oh_prev_t)
    return bc.reshape(Hq, BLOCK, BLOCK), bp.reshape(Hq, BLOCK, BLOCK)


def _small_grads(dbc, dbp, dbf, dsk, oh_cur, oh_prev, *, name):
    Hq = dbc.shape[0]

    def body(dbc_ref, dbp_ref, dbf_ref, dsk_ref, oc_ref, op_ref, tab_ref, sink_ref):
        tab = (jnp.dot(dbc_ref[...], oc_ref[...], precision=HIGHEST, preferred_element_type=F32)
               + jnp.dot(dbp_ref[...], op_ref[...], precision=HIGHEST, preferred_element_type=F32))
        far = jnp.sum(dbf_ref[...], axis=1, keepdims=True)
        last = lax.broadcasted_iota(jnp.int32, (Hq, N_BUCKETS), 1) == N_BUCKETS - 1
        tab_ref[...] = tab + jnp.where(last, far, 0.0)
        sink_ref[...] = jnp.sum(dsk_ref[...], axis=1, keepdims=True)

    vm = pl.BlockSpec(memory_space=pltpu.VMEM)
    return pl.pallas_call(
        body, out_shape=(jax.ShapeDtypeStruct((Hq, N_BUCKETS), F32), jax.ShapeDtypeStruct((Hq, 1), F32)),
        in_specs=[vm] * 6, out_specs=(vm, vm), compiler_params=_params(), name=name,
    )(dbc.reshape(Hq, -1), dbp.reshape(Hq, -1), dbf.reshape(Hq, -1), dsk.reshape(Hq, -1), oh_cur, oh_prev)


def _coords():
    return lax.axis_index("x"), lax.axis_index("y"), lax.axis_index("c")


class _Exchange:
    def __init__(self, inputs, out_shapes, scratch, start, finish):
        self.inputs, self.out_shapes, self.scratch, self.start, self.finish = inputs, out_shapes, scratch, start, finish


def _carry(ex, grid, n_in, n_out, body):
    if ex is None:
        return body, [], [], [], [], []
    ni, no = len(ex.inputs), len(ex.out_shapes)

    def at_step(which):
        cond = None
        for axis, n in enumerate(grid):
            c = pl.program_id(axis) == (0 if which == "first" else n - 1)
            cond = c if cond is None else cond & c
        return cond

    def wrapped(*refs):
        refs = list(refs)
        n_own_scr = len(refs) - (n_in + ni + n_out + no) - len(ex.scratch)
        own_in, side_in = refs[:n_in], refs[n_in:n_in + ni]
        own_out = refs[n_in + ni:n_in + ni + n_out]
        side_out = refs[n_in + ni + n_out:n_in + ni + n_out + no]
        rest = refs[n_in + ni + n_out + no:]
        own_scr, sems = rest[:n_own_scr], rest[n_own_scr:]

        @pl.when(at_step("first"))
        def _():
            ex.start(side_in, side_out, sems)

        body(*own_in, *own_out, *own_scr)

        @pl.when(at_step("last"))
        def _():
            ex.finish(side_in, side_out, sems)

    hbm = pl.BlockSpec(memory_space=pl.ANY)
    return wrapped, list(ex.inputs), [hbm] * ni, list(ex.out_shapes), [hbm] * no, list(ex.scratch)


def _run_exchange(ex, *, name):
    ni, no = len(ex.inputs), len(ex.out_shapes)

    def body(*refs):
        ins, outs, sems = refs[:ni], refs[ni:ni + no], refs[ni + no:]
        ex.start(ins, outs, sems)
        ex.finish(ins, outs, sems)

    hbm = pl.BlockSpec(memory_space=pl.ANY)
    return pl.pallas_call(
        body, out_shape=tuple(ex.out_shapes), in_specs=[hbm] * ni, out_specs=tuple([hbm] * no),
        scratch_shapes=ex.scratch, compiler_params=_params(), name=name)(*ex.inputs)


def _gather_exchange(shards):
    nt = len(shards)

    def copies(ins, outs, sems):
        send_sems, recv_sems, local_sems = sems
        x, y, c = _coords()
        me, sibling = (x, y, c), (x, y, 1 - c)
        chips = [(1 - x, y), (x, 1 - y), (1 - x, 1 - y)]

        def slot(t, dev):
            return outs[t].at[4 * dev[0] + 2 * dev[1] + dev[2]]

        def copy(t, k, block, to, src=None):
            dst = slot(t, block)
            return pltpu.make_async_remote_copy(
                src_ref=dst if src is None else src, dst_ref=dst,
                send_sem=send_sems.at[t, k], recv_sem=recv_sems.at[t, k], device_id=to, device_id_type=MESH)

        mine = [pltpu.make_async_copy(ins[t], slot(t, me), local_sems.at[t]) for t in range(nt)]
        first = []
        for t in range(nt):
            first.append(copy(t, 0, me, sibling, src=ins[t]))
            first += [copy(t, 1 + j, me, (*chip, c), src=ins[t]) for j, chip in enumerate(chips)]
        return copy, mine, first, me, sibling, chips, c

    def start(ins, outs, sems):
        _, mine, first, *_ = copies(ins, outs, sems)
        for cp in mine + first:
            cp.start()

    def finish(ins, outs, sems):
        copy, mine, first, me, sibling, chips, c = copies(ins, outs, sems)
        passed = []
        for j, chip in enumerate(chips):
            for t in range(nt):
                copy(t, 1 + j, (*chip, c), me).wait_recv()
                cp = copy(t, 4 + j, (*chip, c), sibling)
                cp.start()
                passed.append(cp)
        for t in range(nt):
            copy(t, 0, sibling, me).wait_recv()
            for j, chip in enumerate(chips):
                copy(t, 4 + j, (*chip, 1 - c), me).wait_recv()
        for cp in first + passed:
            cp.wait_send()
        for cp in mine:
            cp.wait()

    return _Exchange(
        list(shards), [jax.ShapeDtypeStruct((N_DEV,) + s.shape, s.dtype) for s in shards],
        [pltpu.SemaphoreType.DMA((nt, 7)), pltpu.SemaphoreType.DMA((nt, 7)), pltpu.SemaphoreType.DMA((nt,))],
        start, finish)


def _swap_exchange(arrays, n_slices, copies):
    nt = len(arrays)

    def start(ins, outs, sems):
        for cp in copies(ins, outs, sems):
            cp.start()

    def finish(ins, outs, sems):
        sends = copies(ins, outs, sems)
        for cp in sends:
            cp.wait_recv()
        for cp in sends:
            cp.wait_send()

    return _Exchange(
        list(arrays), [jax.ShapeDtypeStruct((n_slices,) + a.shape[1:], a.dtype) for a in arrays],
        [pltpu.SemaphoreType.DMA((nt, n_slices)), pltpu.SemaphoreType.DMA((nt, n_slices))], start, finish)


def _cores_exchange(gs):
    def copies(ins, outs, sems):
        send_sems, recv_sems = sems
        x, y, c = _coords()
        return [pltpu.make_async_remote_copy(
            src_ref=ins[t].at[2 * j + (1 - c)], dst_ref=outs[t].at[j],
            send_sem=send_sems.at[t, j], recv_sem=recv_sems.at[t, j], device_id=(x, y, 1 - c), device_id_type=MESH)
            for t in range(len(gs)) for j in range(4)]

    return _swap_exchange(gs, 4, copies)


def _chips_exchange(ps):
    def copies(ins, outs, sems):
        send_sems, recv_sems = sems
        x, y, c = _coords()
        peers = [(1 - x, y), (x, 1 - y), (1 - x, 1 - y)]
        return [pltpu.make_async_remote_copy(
            src_ref=ins[t].at[2 * px + py], dst_ref=outs[t].at[k],
            send_sem=send_sems.at[t, k], recv_sem=recv_sems.at[t, k], device_id=(px, py, c), device_id_type=MESH)
            for t in range(len(ps)) for k, (px, py) in enumerate(peers)]

    return _swap_exchange(ps, 3, copies)


def _add_cores(g, r, core, *, name):
    _, A, B = g.shape
    ta = _tile(A, 512, 16)

    def body(core_ref, a_ref, b_ref, o16_ref):
        o16_ref[...] = (a_ref[...] + b_ref[...]).astype(BF16)

    blk = (None, ta, B)
    return pl.pallas_call(
        body, out_shape=jax.ShapeDtypeStruct((4, A, B), BF16),
        grid_spec=pltpu.PrefetchScalarGridSpec(
            num_scalar_prefetch=1, grid=(4, A // ta),
            in_specs=[pl.BlockSpec(blk, lambda j, i, core_ref: (2 * j + core_ref[0], i, 0)),
                      pl.BlockSpec(blk, lambda j, i, core_ref: (j, i, 0))],
            out_specs=pl.BlockSpec(blk, lambda j, i, core_ref: (j, i, 0))),
        compiler_params=_params(("parallel", "parallel")), name=name)(core, g, r)


def _adamw_math(w, g, m, v):
    m = ADAM_B1 * m + (1.0 - ADAM_B1) * g
    v = ADAM_B2 * v + (1.0 - ADAM_B2) * (g * g)
    m_hat = m / (1.0 - ADAM_B1 ** ADAM_STEP)
    v_hat = v / (1.0 - ADAM_B2 ** ADAM_STEP)
    delta = -ADAM_LR * (m_hat / (jnp.sqrt(v_hat) + ADAM_EPS) + ADAM_WD * w)
    return delta, m, v


def _sum_adamw(mine, sib, r, where, w, m, v, *, segs, ta, name):
    Aw, Bw = w.shape
    Bg = mine.shape[2]
    assert Aw % ta == 0

    def body(where_ref, p_ref, s_ref, r0, r1, r2, w_ref, m_ref, v_ref, g_out, d_out, m_out, v_out):
        for gc, wc, n in segs:
            g = (((p_ref[:, gc:gc + n] + s_ref[:, gc:gc + n]) + r0[:, gc:gc + n].astype(F32))
                 + r1[:, gc:gc + n].astype(F32)) + r2[:, gc:gc + n].astype(F32)
            delta, m_new, v_new = _adamw_math(w_ref[:, wc:wc + n], g, m_ref[:, wc:wc + n], v_ref[:, wc:wc + n])
            g_out[:, wc:wc + n] = g
            d_out[:, wc:wc + n] = delta
            m_out[:, wc:wc + n] = m_new
            v_out[:, wc:wc + n] = v_new

    gblk = (None, ta, Bg)
    row = pl.BlockSpec((ta, Bw), lambda i, where_ref: (i, 0))
    rspecs = [pl.BlockSpec(gblk, (lambda i, where_ref, k=k: (k, i, 0))) for k in range(3)]
    shp = jax.ShapeDtypeStruct((Aw, Bw), F32)
    return pl.pallas_call(
        body, out_shape=(shp, shp, shp, shp),
        grid_spec=pltpu.PrefetchScalarGridSpec(
            num_scalar_prefetch=1, grid=(Aw // ta,),
            in_specs=[pl.BlockSpec(gblk, lambda i, where_ref: (2 * where_ref[0] + where_ref[1], i, 0)),
                      pl.BlockSpec(gblk, lambda i, where_ref: (where_ref[0], i, 0))] + rspecs + [row, row, row],
            out_specs=(row, row, row, row)),
        compiler_params=_params(("parallel",)), name=name)(where, mine, sib, r, r, r, w, m, v)


def _adamw(w, g, m, v, *, name):
    def body(w_ref, g_ref, m_ref, v_ref, d_out, m_out, v_out):
        delta, m_new, v_new = _adamw_math(w_ref[...], g_ref[...], m_ref[...], v_ref[...])
        d_out[...] = delta
        m_out[...] = m_new
        v_out[...] = v_new

    vm = pl.BlockSpec(memory_space=pltpu.VMEM)
    shp = jax.ShapeDtypeStruct(w.shape, F32)
    return pl.pallas_call(body, out_shape=(shp, shp, shp), in_specs=[vm] * 4, out_specs=(vm, vm, vm),
                          compiler_params=_params(), name=name)(w, g, m, v)


def _small_allreduce_adamw(s, w, m, v, *, name):
    R, W = s.shape

    def body(s_ref, w_ref, m_ref, v_ref, g_out, d_out, m_out, v_out, gath, send_sems, recv_sems):
        x, y, c = _coords()
        mine = 4 * x + 2 * y + c
        gath[mine] = s_ref[...]
        peers = [((1 - x) if k & 4 else x, (1 - y) if k & 2 else y, (1 - c) if k & 1 else c) for k in range(1, N_DEV)]
        sends = []
        for k in range(1, N_DEV):
            peer = peers[k - 1]
            sends.append(pltpu.make_async_remote_copy(
                src_ref=s_ref, dst_ref=gath.at[mine], send_sem=send_sems.at[k - 1], recv_sem=recv_sems.at[k - 1],
                device_id=peer, device_id_type=MESH))
        for cp in sends:
            cp.start()
        for k in range(1, N_DEV):
            peer = peers[k - 1]
            pltpu.make_async_remote_copy(
                src_ref=s_ref, dst_ref=gath.at[4 * peer[0] + 2 * peer[1] + peer[2]],
                send_sem=send_sems.at[k - 1], recv_sem=recv_sems.at[k - 1],
                device_id=peer, device_id_type=MESH).wait_recv()
        for cp in sends:
            cp.wait_send()
        g = gath[0]
        for d in range(1, N_DEV):
            g = g + gath[d]
        delta, m_new, v_new = _adamw_math(w_ref[...], g, m_ref[...], v_ref[...])
        g_out[...] = g
        d_out[...] = delta
        m_out[...] = m_new
        v_out[...] = v_new

    vm = pl.BlockSpec(memory_space=pltpu.VMEM)
    shp = jax.ShapeDtypeStruct((R, W), F32)
    return pl.pallas_call(
        body, out_shape=(shp, shp, shp, shp), in_specs=[vm] * 4, out_specs=(vm, vm, vm, vm),
        scratch_shapes=[pltpu.VMEM((N_DEV, R, W), F32), pltpu.SemaphoreType.DMA((N_DEV - 1,)),
                        pltpu.SemaphoreType.DMA((N_DEV - 1,))],
        compiler_params=_params(), name=name)(s, w, m, v)


def _pack_small(rel_bias, g1, g2, g3, g4, b_forget, sinks, extra=None, meta=None):
    misc = jnp.concatenate([rel_bias.reshape(-1), b_forget.reshape(-1), sinks.reshape(-1)])
    misc = jnp.concatenate([misc, jnp.zeros((D_MODEL - misc.shape[0],), F32)])[None]
    last = jnp.zeros((1, D_MODEL), F32) if extra is None else extra
    meta = jnp.zeros((N_META, D_MODEL), F32) if meta is None else meta
    return jnp.concatenate([g1, g2, g3, g4, misc, last, jnp.zeros((2, D_MODEL), F32), meta], axis=0)


def _unpack_small(p):
    nrb = N_BUCKETS * SWA_Q_HEADS
    misc = p[4]
    return dict(rel_bias=misc[:nrb].reshape(N_BUCKETS, SWA_Q_HEADS), ln_pre_mix=p[0:1], ln_post_mix=p[1:2],
                ln_pre_ffn=p[2:3], ln_post_ffn=p[3:4], b_forget=misc[nrb:nrb + 8].reshape(1, 8),
                sinks=misc[nrb + 8:nrb + 16].reshape(1, 8))


def _proj_runs():
    gw = FOX_GROUP * HEAD_DIM
    swa = SWA_Q_W + 2 * SWA_KV_HEADS * HEAD_DIM
    runs = [(0, swa)]
    for grp in range(FOX_HEADS // FOX_GROUP):
        runs += [(swa + part * FOX_W + grp * gw, swa + part * FOX_W + (grp + 1) * gw) for part in range(3)]
    return runs


def _columns_from_shards(gathered, runs, shard):
    pieces = []
    for start, stop in runs:
        for d in range(start // shard, (stop - 1) // shard + 1):
            lo = d * shard
            pieces.append(gathered[d][:, max(start, lo) - lo:min(stop, lo + shard) - lo])
    return jnp.concatenate(pieces, axis=1)


def _device_shards(qkv, gate, shard, padded):
    pos, segments = 0, []
    for start, stop in _proj_runs():
        segments.append((start, stop, qkv, pos))
        pos += stop - start
    segments.append((pos, pos + gate.shape[1], gate, 0))
    total = pos + gate.shape[1]
    assert total % shard == 0
    zeros = jnp.zeros((qkv.shape[0], padded - shard), qkv.dtype)
    out = []
    for d in range(total // shard):
        lo, hi = d * shard, (d + 1) * shard
        pieces = [arr[:, src + max(lo, s) - s:src + min(hi, e) - s]
                  for s, e, arr, src in sorted(segments, key=lambda seg: seg[0]) if max(lo, s) < min(hi, e)]
        out.append(jnp.concatenate(pieces + [zeros], axis=1))
    return jnp.stack(out)


def kernel(x, meta_tokens, rel_bias, ln_pre_mix, ln_post_mix, ln_pre_ffn, ln_post_ffn, w_in, b_forget, sinks, w_out, w_gate_up, w_down, loss_target, m_meta_tokens, m_rel_bias, m_ln_pre_mix, m_ln_post_mix, m_ln_pre_ffn, m_ln_post_ffn, m_w_in, m_b_forget, m_sinks, m_w_out, m_w_gate_up, m_w_down, v_meta_tokens, v_rel_bias, v_ln_pre_mix, v_ln_post_mix, v_ln_pre_ffn, v_ln_post_ffn, v_w_in, v_b_forget, v_sinks, v_w_out, v_w_gate_up, v_w_down):
    seq = x.shape[1]
    T = BLOCK + seq
    assert T % FOX_TILE == 0
    nq = T // FOX_TILE
    tm = _tile(T, 1056)
    cin = w_in.shape[2]
    hid = w_down.shape[1]
    assert w_gate_up.shape[2] == 2 * hid and cin <= W_IN_PAD and hid <= HID_PAD

    x_i, y_i, c_i = _coords()
    core = jnp.reshape(c_i, (1,)).astype(jnp.int32)
    where = jnp.stack([2 * x_i + y_i, c_i]).astype(jnp.int32)
    w_in_s = jnp.pad(w_in[0].astype(BF16), ((0, 0), (0, W_IN_PAD - cin)))
    w_gu_s = jnp.pad(w_gate_up[0].astype(BF16).reshape(D_MODEL, 2, hid), ((0, 0), (0, 0), (0, HID_PAD - hid)))
    w_gu_s = w_gu_s.reshape(D_MODEL, 2 * HID_PAD)
    w_down_s = jnp.pad(w_down[0].astype(BF16), ((0, HID_PAD - hid), (0, 0)))
    g_in, g_meta = _run_exchange(_gather_exchange([w_in_s, meta_tokens]), name="ag_w_in")
    gather_rest = _gather_exchange([w_out[0].astype(BF16), w_gu_s, w_down_s])
    w_qkv = _columns_from_shards(g_in, _proj_runs(), cin)
    w_f = jnp.pad(_columns_from_shards(g_in, [(D_QKV, D_PROJ)], cin), ((0, 0), (0, BLOCK - FOX_HEADS)))
    meta_full = g_meta.transpose(1, 0, 2).reshape(N_META, D_MODEL)

    h0 = jnp.concatenate([jnp.zeros((PAD_ROWS, D_MODEL), F32), meta_full, x[0]], axis=0)
    target = jnp.concatenate([jnp.zeros((BLOCK, D_MODEL), F32), loss_target[0]], axis=0)
    hn1, hn1_t = _rms_fwd(h0, ln_pre_mix, name="rms_pre_mix")
    proj = _matmul(hn1, w_qkv, out_dtype=BF16, tm=tm, tn=D_QKV, name="mm_in_proj")
    proj_f = _matmul(hn1, w_f, out_dtype=F32, tm=tm, tn=BLOCK, name="mm_in_proj_f")

    f_t = proj_f[:, :FOX_HEADS].T
    bf_col = b_forget.reshape(FOX_HEADS, 1)

    oh_cur, oh_prev = _bucket_onehots()
    bias_c, bias_p = _bias_tiles(rel_bias.T, jnp.asarray(oh_cur.T), jnp.asarray(oh_prev.T), name="bias_tiles")
    far = rel_bias[N_BUCKETS - 1]
    sink_v = sinks[0]
    mix_a = _swa_fwd(proj, bias_c, bias_p, far, sink_v, name="swa_fwd")

    cum_col = _fox_gates_fwd(f_t, bf_col, name="fox_gates_fwd")
    q_b, k_b, v_b = _fox_prep(proj, cum_col, name="fox_prep")
    mix, lse_row, g_out, g_gu, g_down = _fox_fwd(q_b, k_b, v_b, mix_a, ex=gather_rest, name="fox_fwd")
    w_out_full = g_out.reshape(D_MODEL, D_MODEL)
    w_down_full = g_down.reshape(N_DEV * HID_PAD, D_MODEL)

    a1 = _matmul(mix, w_out_full, out_dtype=F32, tm=tm, tn=D_MODEL, name="mm_out_proj")
    h1, hn2, hn2_t = _post_res_norm(a1, ln_post_mix, h0, ln_pre_ffn, name="post_mix_pre_ffn")
    gate, up, act, act_t = _gate_up_swiglu(hn2, g_gu, name="mm_gate_up")
    ff = _matmul(act, w_down_full, out_dtype=F32, tm=tm, tn=512, name="mm_down")
    dh2, dff, dg_post_ffn, loss_acc = _loss_head(ff, ln_post_ffn, h1, target, name="loss_head")

    dgu = _d_act_swiglu(dff, w_down_full, gate, up, name="mm_d_act")
    d_w_down = _matmul(act_t, dff, out_dtype=F32, tm=768, tn=512, name="mm_dw_down")
    dhn2 = _matmul(dgu, g_gu, nt=True, b_shards=True, out_dtype=F32, tm=tm, tn=512, name="mm_d_hn2")
    d_w_gu = _matmul(hn2_t, dgu, out_shards=True, out_dtype=F32, tm=512, tn=2 * HID_PAD, name="mm_dw_gate_up")
    dh1, dg_pre_ffn, da1, dg_post_mix = _rms_bwd(h1, ln_pre_ffn, dhn2, dh2, out_dtype=F32,
                                                 then=(a1, ln_post_mix), name="rms_bwd_pre_ffn_post_mix")
    dmix = _matmul(da1, w_out_full, nt=True, out_dtype=BF16, tm=tm, tn=D_MODEL, name="mm_d_mix")
    d_w_out = _matmul(mix, da1, ta=True, out_dtype=F32, tm=512, tn=D_MODEL, name="mm_dw_out")

    ffn_grads = [d_w_out.reshape(N_DEV, -1, D_MODEL), d_w_gu, d_w_down.reshape(N_DEV, HID_PAD, D_MODEL)]
    dproj_a, dbc, dbp, dbf, dsk, *ffn_sibling = _swa_bwd(
        proj, dmix, bias_c, bias_p, far, sink_v, ex=_cores_exchange(ffn_grads), name="swa_bwd")
    d_tab, d_sink = _small_grads(dbc, dbp, dbf, dsk, jnp.asarray(oh_cur), jnp.asarray(oh_prev), name="small_grads")
    ffn_sums = [_add_cores(g, r, core, name="rs_add_" + t)
                for g, r, t in zip(ffn_grads, ffn_sibling, ["w_out", "w_gate_up", "w_down"])]

    do_b = _fox_prep_bwd(dmix, mix, name="fox_prep_bwd")
    dproj, dcq, dck, *ffn_chips = _fox_bwd(
        q_b, k_b, v_b, do_b, lse_row, dproj_a, ex=_chips_exchange(ffn_sums), name="fox_bwd")
    df_t, d_bf = _fox_gates_bwd(dcq.reshape(FOX_HEADS, T), dck.reshape(FOX_HEADS, T), f_t, bf_col,
                                name="fox_gates_bwd")
    df = jnp.pad(df_t.T.astype(BF16), ((0, 0), (0, BLOCK - FOX_HEADS)))

    d_w_qkv = _matmul(hn1_t, dproj, out_dtype=F32, tm=512, tn=768, name="mm_dw_in")
    d_w_f = _matmul(hn1_t, df, out_dtype=F32, tm=512, tn=BLOCK, name="mm_dw_in_f")
    d_w_in = _device_shards(d_w_qkv, d_w_f[:, :FOX_HEADS], cin, W_IN_PAD)
    dhn1, in_sibling = _matmul(dproj, w_qkv, nt=True, out_dtype=F32, tm=tm, tn=512,
                               ex=_cores_exchange([d_w_in]), name="mm_d_hn1")
    in_sum = _add_cores(d_w_in, in_sibling, core, name="rs_add_w_in")
    dh0, dg_pre_mix, in_chips = _rms_bwd(h0, ln_pre_mix, dhn1, dh1, out_dtype=F32, dy2=(df, w_f),
                                         ex=_chips_exchange([in_sum]), name="rms_bwd_pre_mix")
    grad_x = dh0[BLOCK:][None]
    d_meta = dh0[PAD_ROWS:BLOCK]

    tags = ["w_in", "w_out", "w_gate_up", "w_down"]
    mine = [d_w_in] + ffn_grads
    from_sibling = [in_sibling] + list(ffn_sibling)
    from_chips = [in_chips] + list(ffn_chips)
    shard_w = [(w_in, m_w_in, v_w_in), (w_out, m_w_out, v_w_out), (w_gate_up, m_w_gate_up, v_w_gate_up),
               (w_down, m_w_down, v_w_down)]
    segs = [[(0, 0, cin)], [(0, 0, D_MODEL)], [(0, 0, hid), (HID_PAD, hid, hid)], [(0, 0, D_MODEL)]]
    tas = [256, BLOCK, 256, hid]
    big = [{}, {}, {}, {}]
    for i, t in enumerate(tags):
        w_t, m_t, v_t = shard_w[i]
        res = _sum_adamw(mine[i], from_sibling[i], from_chips[i], where, w_t[0], m_t[0], v_t[0], segs=segs[i],
                         ta=tas[i], name="rs_adamw_" + t)
        for kind in range(4):
            big[kind][t] = res[kind][None]

    loss_row = jnp.pad(loss_acc[0:1, 0:1] * (0.5 / D_MODEL), ((0, 0), (0, D_MODEL - 1)))
    s_small = _pack_small(d_tab.T, dg_pre_mix, dg_post_mix, dg_pre_ffn, dg_post_ffn, d_bf, d_sink,
                          extra=loss_row, meta=d_meta)
    w_s = _pack_small(rel_bias, ln_pre_mix, ln_post_mix, ln_pre_ffn, ln_post_ffn, b_forget, sinks)
    m_s = _pack_small(m_rel_bias, m_ln_pre_mix, m_ln_post_mix, m_ln_pre_ffn, m_ln_post_ffn, m_b_forget, m_sinks)
    v_s = _pack_small(v_rel_bias, v_ln_pre_mix, v_ln_post_mix, v_ln_pre_ffn, v_ln_post_ffn, v_b_forget, v_sinks)
    small = _small_allreduce_adamw(s_small, w_s, m_s, v_s, name="small_allreduce_adamw")
    loss = small[0][5, 0]
    mcols = meta_tokens.shape[1]
    g_meta_mine = lax.dynamic_slice(small[0][8:8 + N_META], (0, (4 * x_i + 2 * y_i + c_i) * mcols), (N_META, mcols))
    big[0]["meta_tokens"] = g_meta_mine
    for kind, arr in enumerate(_adamw(meta_tokens, g_meta_mine, m_meta_tokens, v_meta_tokens, name="adamw_meta")):
        big[kind + 1]["meta_tokens"] = arr
    small = [_unpack_small(p) for p in small]

    names = ["meta_tokens", "rel_bias", "ln_pre_mix", "ln_post_mix", "ln_pre_ffn", "ln_post_ffn", "w_in",
             "b_forget", "sinks", "w_out", "w_gate_up", "w_down"]
    outs = [loss, grad_x]
    for kind in range(4):
        for nme in names:
            outs.append(big[kind][nme] if nme in big[kind] else small[kind][nme])
    return tuple(outs)
```

```python
import math

import numpy as np
import jax
import jax.numpy as jnp
from jax import lax
from jax.experimental import pallas as pl
from jax.experimental.pallas import tpu as pltpu

F32 = jnp.float32
BF16 = jnp.bfloat16
HIGHEST = lax.Precision.HIGHEST
MESH = pl.DeviceIdType.MESH

N_DEV = 8
D_MODEL = 1024
N_META = 16
HEAD_DIM = 64
SWA_Q_HEADS = 8
SWA_KV_HEADS = 2
SWA_GROUP = 4
FOX_HEADS = 8
FOX_W = FOX_HEADS * HEAD_DIM
SWA_Q_W = SWA_Q_HEADS * HEAD_DIM
BLOCK = 128
PAD_ROWS = BLOCK - N_META
N_BUCKETS = 32
MAX_DISTANCE = 128
D_FF = 2816
D_QKV = 2304
D_PROJ = D_QKV + FOX_HEADS
D_PROJ_PAD = 2560
EPS = 1e-6
NEG = -1e30
SCALE = HEAD_DIM ** -0.5
ADAM_LR, ADAM_B1, ADAM_B2, ADAM_EPS, ADAM_WD, ADAM_STEP = 0.001, 0.9, 0.999, 1e-08, 0.01, 10
VMEM_LIMIT = 56 * 1024 * 1024
FOX_TILE = 384
FOX_GROUP = 4
W_IN_PAD = 384
HID_PAD = 384

NT = (((1,), (1,)), ((), ()))
NN = (((1,), (0,)), ((), ()))
TN = (((0,), (0,)), ((), ()))


def _params(sem=None, **kw):
    if sem is not None:
        kw["dimension_semantics"] = sem
    return pltpu.CompilerParams(vmem_limit_bytes=VMEM_LIMIT, **kw)


def _tile(n, target, mult=16):
    best = None
    for t in range(mult, min(n, target) + 1, mult):
        if n % t == 0:
            best = t
    assert best is not None, (n, target)
    return best


def _matmul(a, b, *, nt=False, ta=False, b_shards=False, out_shards=False, out_dtype, tm, tn=None, tk=None,
            ex=None, name):
    M, K = a.shape[::-1] if ta else a.shape
    assert not (ta and (nt or b_shards))
    k_shards = b.shape[0] if (b_shards and nt) else 0
    if k_shards:
        N, ks = b.shape[1], b.shape[2]
        assert tk is None and K == k_shards * ks
    elif b_shards:
        N, tn = b.shape[0] * b.shape[2], b.shape[2]
    else:
        N = b.shape[0] if nt else b.shape[1]
    tk = K if tk is None else tk
    assert M % tm == 0 and N % tn == 0 and K % tk == 0, (name, a.shape, b.shape, tm, tn, tk)
    nk = K // tk
    dn = NT if nt else (TN if ta else NN)
    a_spec = pl.BlockSpec((tk, tm), lambda i, j, k: (k, i)) if ta else pl.BlockSpec((tm, tk), lambda i, j, k: (i, k))

    def body(a_ref, b_ref, o_ref, *scr):
        if k_shards:
            part = sum(lax.dot_general(a_ref[:, s * ks:(s + 1) * ks], b_ref[s], NT, preferred_element_type=F32)
                       for s in range(k_shards))
        else:
            part = lax.dot_general(a_ref[...], b_ref[...], dn, preferred_element_type=F32)
        if nk == 1:
            o_ref[...] = part.astype(o_ref.dtype)
        else:
            acc = scr[0]
            k = pl.program_id(2)

            @pl.when(k == 0)
            def _():
                acc[...] = part

            @pl.when(k > 0)
            def _():
                acc[...] += part

            @pl.when(k == nk - 1)
            def _():
                o_ref[...] = acc[...].astype(o_ref.dtype)

    if k_shards:
        b_spec = pl.BlockSpec((k_shards, tn, ks), lambda i, j, k: (0, j, 0))
    elif b_shards:
        b_spec = pl.BlockSpec((None, tk, tn), lambda i, j, k: (j, k, 0))
    elif nt:
        b_spec = pl.BlockSpec((tn, tk), lambda i, j, k: (j, k))
    else:
        b_spec = pl.BlockSpec((tk, tn), lambda i, j, k: (k, j))
    if out_shards:
        out_shape = jax.ShapeDtypeStruct((N // tn, M, tn), out_dtype)
        out_spec = pl.BlockSpec((None, tm, tn), lambda i, j, k: (j, i, 0))
    else:
        out_shape = jax.ShapeDtypeStruct((M, N), out_dtype)
        out_spec = pl.BlockSpec((tm, tn), lambda i, j, k: (i, j))
    grid = (M // tm, N // tn, nk)
    body, x_in, x_in_specs, x_out, x_out_specs, x_scr = _carry(ex, grid, 2, 1, body)
    res = pl.pallas_call(
        body,
        out_shape=(out_shape, *x_out),
        grid=grid,
        in_specs=[a_spec, b_spec] + x_in_specs,
        out_specs=(out_spec, *x_out_specs),
        scratch_shapes=([pltpu.VMEM((tm, tn), F32)] if nk > 1 else []) + x_scr,
        compiler_params=_params(("parallel", "parallel", "arbitrary") if ex is None else ("arbitrary",) * 3),
        name=name,
    )(a, b, *x_in)
    return res[0] if ex is None else res


def _rstd(x):
    return lax.rsqrt(jnp.mean(x * x, axis=-1, keepdims=True) + EPS)


def _pad_rows(x, target, *, ex=None, name):
    S, D = x.shape
    nb = S // BLOCK + 1

    def body(x_ref, t_ref, xo_ref, to_ref):
        keep = pl.program_id(0) > 0
        xo_ref[...] = jnp.where(keep, x_ref[...], 0.0)
        to_ref[...] = jnp.where(keep, t_ref[...], 0.0)

    src = pl.BlockSpec((BLOCK, D), lambda i: (jnp.maximum(i - 1, 0), 0))
    dst = pl.BlockSpec((BLOCK, D), lambda i: (i, 0))
    shp = jax.ShapeDtypeStruct((BLOCK + S, D), x.dtype)
    grid = (nb,)
    body, x_in, x_in_specs, x_out, x_out_specs, x_scr = _carry(ex, grid, 2, 2, body)
    return pl.pallas_call(
        body, out_shape=(shp, shp, *x_out), grid=grid,
        in_specs=[src, src] + x_in_specs, out_specs=(dst, dst, *x_out_specs), scratch_shapes=x_scr,
        compiler_params=_params(("arbitrary",)), name=name)(x, target, *x_in)


def _rms_fwd(x, g, *, name):
    T, D = x.shape
    tm = _tile(T, 512)

    def body(x_ref, g_ref, o_ref, ot_ref):
        x = x_ref[...]
        y = x * _rstd(x) * g_ref[...]
        o_ref[...] = y.astype(o_ref.dtype)
        ot_ref[...] = y.T.astype(ot_ref.dtype)

    return pl.pallas_call(
        body, out_shape=(jax.ShapeDtypeStruct((T, D), BF16), jax.ShapeDtypeStruct((D, T), BF16)), grid=(T // tm,),
        in_specs=[pl.BlockSpec((tm, D), lambda i: (i, 0)), pl.BlockSpec((1, D), lambda i: (0, 0))],
        out_specs=(pl.BlockSpec((tm, D), lambda i: (i, 0)), pl.BlockSpec((D, tm), lambda i: (0, i))),
        compiler_params=_params(("parallel",)), name=name)(x, g)


def _post_res_norm(a, g_post, h, g_pre, *, name):
    T, D = a.shape
    tm = _tile(T, 384, BLOCK)

    def body(a_ref, gp_ref, h_ref, gn_ref, h1_ref, o_ref, ot_ref):
        a = a_ref[...]
        h1 = h_ref[...] + a * _rstd(a) * gp_ref[...]
        h1_ref[...] = h1
        y = h1 * _rstd(h1) * gn_ref[...]
        o_ref[...] = y.astype(o_ref.dtype)
        ot_ref[...] = y.T.astype(ot_ref.dtype)

    row = pl.BlockSpec((tm, D), lambda i: (i, 0))
    vec = pl.BlockSpec((1, D), lambda i: (0, 0))
    return pl.pallas_call(
        body, out_shape=(jax.ShapeDtypeStruct((T, D), F32), jax.ShapeDtypeStruct((T, D), BF16),
                         jax.ShapeDtypeStruct((D, T), BF16)), grid=(T // tm,),
        in_specs=[row, vec, row, vec], out_specs=(row, row, pl.BlockSpec((D, tm), lambda i: (0, i))),
        compiler_params=_params(("parallel",)), name=name)(a, g_post, h, g_pre)


def _loss_head(a, g, h, target, *, name):
    T, D = a.shape
    tm = _tile(T, 512)

    def body(a_ref, g_ref, h_ref, t_ref, dy_ref, da_ref, dg_ref, loss_ref):
        i = pl.program_id(0)
        a = a_ref[...]
        r = _rstd(a)
        ah = a * r
        y = h_ref[...] + ah * g_ref[...]
        rows = i * tm + lax.broadcasted_iota(jnp.int32, (tm, 1), 0)
        err = jnp.where(rows >= BLOCK, y - t_ref[...], 0.0)
        dy = err / D
        dy_ref[...] = dy
        dah = dy * g_ref[...]
        da_ref[...] = (r * (dah - ah * jnp.mean(dah * ah, axis=-1, keepdims=True))).astype(da_ref.dtype)
        part = jnp.sum(jnp.sum(err * err, axis=1, keepdims=True), axis=0, keepdims=True)

        @pl.when(i == 0)
        def _():
            loss_ref[...] = jnp.zeros_like(loss_ref)
            dg_ref[...] = jnp.zeros_like(dg_ref)

        loss_ref[...] += jnp.broadcast_to(part, loss_ref.shape)
        dg_ref[...] += jnp.sum(dy * ah, axis=0, keepdims=True)

    row = pl.BlockSpec((tm, D), lambda i: (i, 0))
    vec = pl.BlockSpec((1, D), lambda i: (0, 0))
    return pl.pallas_call(
        body, out_shape=(jax.ShapeDtypeStruct((T, D), F32), jax.ShapeDtypeStruct((T, D), BF16),
                         jax.ShapeDtypeStruct((1, D), F32), jax.ShapeDtypeStruct((8, 128), F32)),
        grid=(T // tm,),
        in_specs=[row, vec, row, row],
        out_specs=(row, row, vec, pl.BlockSpec((8, 128), lambda i: (0, 0))),
        compiler_params=_params(("arbitrary",)), name=name)(a, g, h, target)


def _rms_bwd(x, g, dy, res, *, out_dtype, dy2=None, then=None, ex=None, name):
    T, D = x.shape
    tm = _tile(T, 512)
    has_res = res is not None
    has_dy2 = 2 if dy2 is not None else 0
    n_in = 3 + has_dy2 + has_res + (2 if then is not None else 0)
    n_out = 2 + (2 if then is not None else 0)

    def pull_back(x, g, dy):
        r = _rstd(x)
        xh = x * r
        dxh = dy * g
        return r * (dxh - xh * jnp.mean(dxh * xh, axis=-1, keepdims=True)), jnp.sum(dy * xh, axis=0, keepdims=True)

    def body(*refs):
        ins, outs = refs[:n_in], refs[n_in:]
        i = pl.program_id(0)

        @pl.when(i == 0)
        def _():
            for ref in outs[1::2]:
                ref[...] = jnp.zeros_like(ref)

        dy_all = ins[2][...].astype(F32)
        if has_dy2:
            dy_all = dy_all + lax.dot_general(ins[3][...], ins[4][...], NT, preferred_element_type=F32)
        dx, dg = pull_back(ins[0][...], ins[1][...], dy_all)
        if has_res:
            dx = dx + ins[3 + has_dy2][...]
        outs[0][...] = dx.astype(outs[0].dtype)
        outs[1][...] += dg
        if then is not None:
            dx2, dg2 = pull_back(ins[n_in - 2][...], ins[n_in - 1][...], dx)
            outs[2][...] = dx2.astype(outs[2].dtype)
            outs[3][...] += dg2

    row = pl.BlockSpec((tm, D), lambda i: (i, 0))
    vec = pl.BlockSpec((1, D), lambda i: (0, 0))
    ins = [x, g, dy] + (list(dy2) if has_dy2 else []) + ([res] if has_res else []) + (list(then) if then is not None else [])
    dy2_specs = ([pl.BlockSpec((tm, dy2[0].shape[1]), lambda i: (i, 0)), pl.BlockSpec(dy2[1].shape, lambda i: (0, 0))]
                 if has_dy2 else [])
    in_specs = [row, vec, row] + dy2_specs + ([row] if has_res else []) + ([row, vec] if then is not None else [])
    out_shape = [jax.ShapeDtypeStruct((T, D), out_dtype), jax.ShapeDtypeStruct((1, D), F32)]
    out_specs = [row, vec]
    if then is not None:
        out_shape += [jax.ShapeDtypeStruct((T, D), BF16), jax.ShapeDtypeStruct((1, D), F32)]
        out_specs += [row, vec]
    grid = (T // tm,)
    body, x_in, x_in_specs, x_out, x_out_specs, x_scr = _carry(ex, grid, n_in, n_out, body)
    return pl.pallas_call(
        body, out_shape=(*out_shape, *x_out), grid=grid,
        in_specs=in_specs + x_in_specs, out_specs=(*out_specs, *x_out_specs), scratch_shapes=x_scr,
        compiler_params=_params(("arbitrary",)), name=name)(*ins, *x_in)


def _gate_up_swiglu(a, w, *, name):
    T, D = a.shape
    S, n = w.shape[0] // 2, w.shape[2]
    tm = _tile(T, 1408, BLOCK)

    def body(a_ref, wg_ref, wu_ref, g_ref, u_ref, o_ref, ot_ref):
        x = a_ref[...]
        g = jnp.dot(x, wg_ref[...], preferred_element_type=F32)
        u = jnp.dot(x, wu_ref[...], preferred_element_type=F32)
        g16, u16 = g.astype(BF16), u.astype(BF16)
        g_ref[...] = g16
        u_ref[...] = u16
        gr = g16.astype(F32)
        act = gr / (1.0 + jnp.exp(-gr)) * u16.astype(F32)
        o_ref[...] = act.astype(o_ref.dtype)
        ot_ref[...] = act.T.astype(ot_ref.dtype)

    tile = pl.BlockSpec((tm, n), lambda i, j: (i, j))
    shp = jax.ShapeDtypeStruct((T, S * n), BF16)
    return pl.pallas_call(
        body, out_shape=(shp, shp, shp, jax.ShapeDtypeStruct((S * n, T), BF16)), grid=(T // tm, S),
        in_specs=[pl.BlockSpec((tm, D), lambda i, j: (i, 0)),
                  pl.BlockSpec((None, D, n), lambda i, j: (j, 0, 0)),
                  pl.BlockSpec((None, D, n), lambda i, j: (j + S, 0, 0))],
        out_specs=(tile, tile, tile, pl.BlockSpec((n, tm), lambda i, j: (j, i))),
        compiler_params=_params(("parallel", "parallel")), name=name)(a, w, w)


def _d_act_swiglu(dff, w_down, gate, up, *, name):
    T, D = dff.shape
    F = w_down.shape[0]
    tm = _tile(T, 384)
    tf = _tile(F, 768, BLOCK)

    def body(d_ref, w_ref, g_ref, u_ref, o_ref):
        dy = d_ref[...]
        for c in range(0, F, tf):
            d = lax.dot_general(dy, w_ref[c:c + tf, :], NT, preferred_element_type=F32)
            g = g_ref[:, c:c + tf].astype(F32)
            u = u_ref[:, c:c + tf].astype(F32)
            sg = 1.0 / (1.0 + jnp.exp(-g))
            o_ref[:, c:c + tf] = (d * u * (sg * (1.0 + g * (1.0 - sg)))).astype(o_ref.dtype)
            o_ref[:, F + c:F + c + tf] = (d * (g * sg)).astype(o_ref.dtype)

    row = pl.BlockSpec((tm, F), lambda i: (i, 0))
    return pl.pallas_call(
        body, out_shape=jax.ShapeDtypeStruct((T, 2 * F), BF16), grid=(T // tm,),
        in_specs=[pl.BlockSpec((tm, D), lambda i: (i, 0)), pl.BlockSpec((F, D), lambda i: (0, 0)), row, row],
        out_specs=pl.BlockSpec((tm, 2 * F), lambda i: (i, 0)),
        compiler_params=_params(("parallel",)), name=name)(dff, w_down, gate, up)


def _fox_gates_fwd(f_t, b, *, name):
    H, T = f_t.shape
    nb = T // BLOCK

    def body(f_ref, b_ref, col_ref):
        f = f_ref[...] + b_ref[...]
        ls = jnp.minimum(f, 0.0) - jnp.log(1.0 + jnp.exp(-jnp.abs(f)))
        t = lax.broadcasted_iota(jnp.int32, (H, T), 1)
        ls = jnp.where(t >= PAD_ROWS, ls, 0.0)
        upper = (lax.broadcasted_iota(jnp.int32, (BLOCK, BLOCK), 0)
                 <= lax.broadcasted_iota(jnp.int32, (BLOCK, BLOCK), 1)).astype(F32)
        carry = jnp.zeros((H, 1), F32)
        for blk in range(nb):
            seg = ls[:, blk * BLOCK:(blk + 1) * BLOCK]
            pre = jnp.dot(seg, upper, precision=HIGHEST, preferred_element_type=F32) + carry
            key_gate = jnp.where(t[:, blk * BLOCK:(blk + 1) * BLOCK] >= PAD_ROWS, pre, -NEG)
            terms = list(_split3(pre)) + list(_split3(key_gate))
            col_ref[blk * BLOCK:(blk + 1) * BLOCK, :] = jnp.concatenate(
                terms + [jnp.zeros((BLOCK - len(terms) * H, BLOCK), F32)], axis=0).T.astype(col_ref.dtype)
            carry = pre[:, BLOCK - 1:BLOCK]

    vm = pl.BlockSpec(memory_space=pltpu.VMEM)
    return pl.pallas_call(
        body, out_shape=jax.ShapeDtypeStruct((T, BLOCK), BF16),
        in_specs=[vm, vm], out_specs=vm,
        compiler_params=_params(), name=name)(f_t, b)


def _fox_gates_bwd(dcq, dck, f_t, b, *, name):
    H, T = f_t.shape
    nb = T // BLOCK

    def body(dq_ref, d_ref, f_ref, b_ref, df_ref, db_ref):
        lower = (lax.broadcasted_iota(jnp.int32, (BLOCK, BLOCK), 0)
                 >= lax.broadcasted_iota(jnp.int32, (BLOCK, BLOCK), 1)).astype(F32)
        carry = jnp.zeros((H, 1), F32)
        for blk in range(nb - 1, -1, -1):
            seg = dq_ref[:, blk * BLOCK:(blk + 1) * BLOCK] - d_ref[:, blk * BLOCK:(blk + 1) * BLOCK]
            suf = jnp.dot(seg, lower, precision=HIGHEST, preferred_element_type=F32) + carry
            df_ref[:, blk * BLOCK:(blk + 1) * BLOCK] = suf
            carry = suf[:, 0:1]
        f = f_ref[...] + b_ref[...]
        t = lax.broadcasted_iota(jnp.int32, (H, T), 1)
        df = jnp.where(t >= PAD_ROWS, df_ref[...] / (1.0 + jnp.exp(f)), 0.0)
        df_ref[...] = df
        db_ref[...] = jnp.sum(df, axis=1, keepdims=True)

    vm = pl.BlockSpec(memory_space=pltpu.VMEM)
    return pl.pallas_call(
        body, out_shape=(jax.ShapeDtypeStruct((H, T), F32), jax.ShapeDtypeStruct((H, 1), F32)),
        in_specs=[vm, vm, vm, vm], out_specs=(vm, vm),
        compiler_params=_params(), name=name)(dcq, dck, f_t, b)


def _fox_lanes(parity):
    base = HEAD_DIM * (1 - parity)
    return base, base + 3


def _split3(c):
    hi = c.astype(BF16).astype(F32)
    r = c - hi
    mid = r.astype(BF16).astype(F32)
    lo = (r - mid).astype(BF16).astype(F32)
    return hi, mid, lo


def _lanes(lane, parity, data, start, terms, ones_at=None, fill=1.0):
    out = jnp.zeros((), F32) if ones_at is None else jnp.where((lane >= ones_at) & (lane < ones_at + 3), fill, 0.0)
    for i, t in enumerate(terms):
        out = jnp.where(lane == start + i, t, out)
    return jnp.where(lane // HEAD_DIM == parity, data, out)


def _fox_prep(proj, cum_col, *, name):
    T = proj.shape[0]
    tm = _tile(T, 1408, BLOCK)
    nt = T // tm
    H = FOX_HEADS
    lanes = 2 * HEAD_DIM
    first = (proj.shape[1] - 3 * H * HEAD_DIM) // lanes

    def body(q_ref, k_ref, v_ref, c_ref, qa_ref, ka_ref, va_ref):
        p = pl.program_id(0)
        i = pl.program_id(1)
        lane = lax.broadcasted_iota(jnp.int32, (1, lanes), 1)
        src = lax.broadcasted_iota(jnp.int32, (lanes, lanes), 0)
        dst = lax.broadcasted_iota(jnp.int32, (lanes, lanes), 1)
        q2 = q_ref[...].astype(F32) * SCALE
        k2 = k_ref[...].astype(F32)
        v2 = v_ref[...].astype(F32)
        gates = c_ref[...]
        def placed(h, first_term, start):
            pick = ((src % FOX_HEADS == h) & (src // FOX_HEADS - first_term == dst - start)
                    & (dst >= start) & (dst < start + 3))
            return jnp.dot(gates, pick.astype(BF16), preferred_element_type=F32)

        moved = [(placed(2 * p + e, 0, _fox_lanes(e)[1]), placed(2 * p + e, 3, _fox_lanes(e)[0])) for e in range(2)]
        for e in range(2):
            kc, qc = _fox_lanes(e)
            own = lane // HEAD_DIM == e
            minus = jnp.where((lane >= kc) & (lane < kc + 3), -1.0, 0.0)
            ones_q = jnp.where((lane >= qc) & (lane < qc + 3), 1.0, 0.0)
            ones_k = jnp.where((lane >= kc) & (lane < kc + 3), 1.0, 0.0)
            qa_ref[e] = jnp.where(own, q2, moved[e][0] + minus).astype(BF16)
            ka_ref[e] = jnp.where(own, k2, moved[e][1] + ones_q).astype(BF16)
            va_ref[e] = jnp.where(own, v2, ones_k).astype(BF16)

    pairs = FOX_GROUP // 2

    def col(part):
        return pl.BlockSpec((tm, lanes),
                            lambda p, i: (i, first + 3 * pairs * (p // pairs) + part * pairs + p % pairs))

    out = pl.BlockSpec((2, tm, lanes), lambda p, i: (p, i, 0))
    shp = jax.ShapeDtypeStruct((H, T, lanes), BF16)
    return pl.pallas_call(
        body, out_shape=(shp, shp, shp), grid=(H // 2, nt),
        in_specs=[col(0), col(1), col(2), pl.BlockSpec((tm, lanes), lambda p, i: (i, 0))],
        out_specs=(out, out, out),
        compiler_params=_params(("parallel", "parallel")), name=name)(proj, proj, proj, cum_col)


def _fox_fwd(q_aug, k_aug, v_aug, mix, *, ex=None, name):
    H, T, lanes = q_aug.shape
    tq = FOX_TILE
    nq = T // tq
    G = FOX_HEADS

    def body(q_ref, k_ref, v_ref, mix_ref, o_ref, lse_ref, m_scr, acc_scr):
        i = pl.program_id(1)
        m_scr[...] = jnp.full(m_scr.shape, NEG, F32)
        acc_scr[...] = jnp.zeros(acc_scr.shape, F32)

        def step(kb, diag):
            off = pl.multiple_of(kb * tq, tq)
            s_t = [lax.dot_general(k_ref[g, pl.ds(off, tq), :], q_ref[g], NT, preferred_element_type=F32)
                   for g in range(G)]
            if diag:
                r = lax.broadcasted_iota(jnp.int32, (tq, tq), 0)
                c = lax.broadcasted_iota(jnp.int32, (tq, tq), 1)
                s_t = [jnp.where(c >= r, s, NEG) for s in s_t]
            m_prev = [m_scr[g] for g in range(G)]
            m_new = [jnp.maximum(m_prev[g], jnp.max(s_t[g], axis=0, keepdims=True)) for g in range(G)]
            p_t = [jnp.exp(s_t[g] - m_new[g]).astype(BF16) for g in range(G)]
            pv = [lax.dot_general(v_ref[g, pl.ds(off, tq), :], p_t[g], TN, preferred_element_type=F32)
                  for g in range(G)]
            for g in range(G):
                acc_scr[g] = jnp.exp(m_prev[g] - m_new[g]) * acc_scr[g] + pv[g]
                m_scr[g] = m_new[g]

        def loop_body(kb, carry):
            step(kb, False)
            return carry

        lax.fori_loop(0, i, loop_body, 0)
        step(i, True)
        lane = lax.broadcasted_iota(jnp.int32, (tq, lanes), 1)
        outs = []
        for g in range(G):
            ones = _fox_lanes(g % 2)[0]
            acc = acc_scr[g]
            lse_ref[g] = m_scr[g] + jnp.log(acc[ones:ones + 1, :])
            acc_t = acc.T
            outs.append(acc_t / acc_t[:, ones:ones + 1])
        for pair in range(G // 2):
            o_ref[:, pair * lanes:(pair + 1) * lanes] = jnp.where(
                lane < HEAD_DIM, outs[2 * pair], outs[2 * pair + 1]).astype(o_ref.dtype)

    blk = pl.BlockSpec((G, tq, lanes), lambda h, i: (h, i, 0))
    full = pl.BlockSpec((G, T, lanes), lambda h, i: (h, 0, 0))
    grid = (H // G, nq)
    first = mix.shape[1] // (G * HEAD_DIM) - H // G
    body, x_in, x_in_specs, x_out, x_out_specs, x_scr = _carry(ex, grid, 4, 2, body)
    return pl.pallas_call(
        body,
        out_shape=(jax.ShapeDtypeStruct(mix.shape, mix.dtype), jax.ShapeDtypeStruct((H, nq, 1, tq), F32), *x_out),
        grid=grid,
        in_specs=[blk, full, full, pl.BlockSpec(memory_space=pl.ANY)] + x_in_specs,
        out_specs=(pl.BlockSpec((tq, G * HEAD_DIM), lambda h, i: (i, first + h)),
                   pl.BlockSpec((G, None, 1, tq), lambda h, i: (h, i, 0, 0)), *x_out_specs),
        input_output_aliases={3: 0},
        scratch_shapes=[pltpu.VMEM((G, 1, tq), F32), pltpu.VMEM((G, lanes, tq), F32)] + x_scr,
        compiler_params=_params(("arbitrary", "arbitrary")), name=name)(q_aug, k_aug, v_aug, mix, *x_in)


def _fox_prep_bwd(dmix, mix, *, name):
    T = dmix.shape[0]
    H = FOX_HEADS
    tm = _tile(T, 1408, BLOCK)
    lanes = 2 * HEAD_DIM
    first = mix.shape[1] // lanes - H // 2

    def body(d_ref, o_ref, da_ref):
        lane = lax.broadcasted_iota(jnp.int32, (1, lanes), 1)
        d2 = d_ref[...].astype(F32)
        prod = d2 * o_ref[...].astype(F32)
        for e in range(2):
            delta = jnp.sum(jnp.where(lane // HEAD_DIM == e, prod, 0.0), axis=1, keepdims=True)
            da_ref[e] = _lanes(lane, e, d2, _fox_lanes(e)[0], _split3(-delta)).astype(BF16)

    pair = pl.BlockSpec((tm, lanes), lambda p, i: (i, first + p))
    return pl.pallas_call(
        body, out_shape=jax.ShapeDtypeStruct((H, T, lanes), BF16), grid=(H // 2, T // tm),
        in_specs=[pair, pair],
        out_specs=pl.BlockSpec((2, tm, lanes), lambda p, i: (p, i, 0)),
        compiler_params=_params(("parallel", "parallel")), name=name)(dmix, mix)


def _fox_bwd(q_aug, k_aug, v_aug, do_aug, lse_row, dproj, *, ex=None, name):
    H, T, lanes = q_aug.shape
    tq = FOX_TILE
    nq = T // tq
    G = FOX_GROUP

    def side_by_side(tiles, scale=None):
        lane = lax.broadcasted_iota(jnp.int32, tiles[0].shape, 1)
        out = [jnp.where(lane < HEAD_DIM, tiles[2 * p], tiles[2 * p + 1]) for p in range(G // 2)]
        out = jnp.concatenate(out, axis=1)
        return out if scale is None else out * scale

    def body(q_ref, k_ref, v_ref, do_ref, lse_ref, dproj_in, out_ref, dcq_ref, dck_ref, dk_acc, dv_acc, dq_ref):
        j = pl.program_id(1)

        @pl.when(j == 0)
        def _():
            dq_ref[...] = jnp.zeros(dq_ref.shape, F32)
            dcq_ref[...] = jnp.zeros(dcq_ref.shape, F32)

        dk_acc[...] = jnp.zeros(dk_acc.shape, F32)
        dv_acc[...] = jnp.zeros(dv_acc.shape, F32)

        def step(qb, diag):
            off = pl.multiple_of(qb * tq, tq)
            heads = range(G)
            qa = [q_ref[g, pl.ds(off, tq), :] for g in heads]
            da = [do_ref[g, pl.ds(off, tq), :] for g in heads]
            s_t = [lax.dot_general(k_ref[g], qa[g], NT, preferred_element_type=F32) for g in heads]
            dp_t = [lax.dot_general(v_ref[g], da[g], NT, preferred_element_type=F32) for g in heads]
            p_t = [jnp.exp(s_t[g] - lse_ref[g, qb]) for g in heads]
            if diag:
                r = lax.broadcasted_iota(jnp.int32, (tq, tq), 0)
                c = lax.broadcasted_iota(jnp.int32, (tq, tq), 1)
                p_t = [jnp.where(c >= r, p, 0.0) for p in p_t]
            dsb = [(p_t[g] * dp_t[g]).astype(BF16) for g in heads]
            dv = [jnp.dot(p_t[g].astype(BF16), da[g], preferred_element_type=F32) for g in heads]
            dk = [jnp.dot(dsb[g], qa[g], preferred_element_type=F32) for g in heads]
            dq = [lax.dot_general(k_ref[g], dsb[g], TN, preferred_element_type=F32) for g in heads]
            for g in heads:
                dv_acc[g] += dv[g]
                dk_acc[g] += dk[g]
                dq_ref[g, qb] += dq[g]
                dcq_ref[g, qb] += jnp.sum(dsb[g].astype(F32), axis=0, keepdims=True)

        step(j, True)

        def loop_body(qb, carry):
            step(qb, False)
            return carry

        lax.fori_loop(j + 1, nq, loop_body, 0)
        dk = [dk_acc[g] for g in range(G)]
        out_ref[:, 0:wide] = side_by_side([dq_ref[g, j].T for g in range(G)], SCALE).astype(out_ref.dtype)
        out_ref[:, wide:2 * wide] = side_by_side(dk).astype(out_ref.dtype)
        out_ref[:, 2 * wide:3 * wide] = side_by_side([dv_acc[g] for g in range(G)]).astype(out_ref.dtype)
        for g in range(G):
            kc = _fox_lanes(g % 2)[0]
            dck_ref[g] = -dk[g].T[kc:kc + 1, :]

    blk = pl.BlockSpec((G, tq, lanes), lambda h, j: (h, j, 0))
    full = pl.BlockSpec((G, T, lanes), lambda h, j: (h, 0, 0))
    wide = G * HEAD_DIM
    first = dproj.shape[1] // (3 * wide) - H // G
    grid = (H // G, nq)
    body, x_in, x_in_specs, x_out, x_out_specs, x_scr = _carry(ex, grid, 6, 3, body)
    rows = jax.ShapeDtypeStruct((H, nq, 1, tq), F32)
    all_rows = pl.BlockSpec((G, nq, 1, tq), lambda h, j: (h, 0, 0, 0))
    return pl.pallas_call(
        body,
        out_shape=(jax.ShapeDtypeStruct(dproj.shape, dproj.dtype), rows, rows, *x_out),
        grid=grid,
        in_specs=[full, blk, blk, full, all_rows, pl.BlockSpec(memory_space=pl.ANY)] + x_in_specs,
        out_specs=(pl.BlockSpec((tq, 3 * wide), lambda h, j: (j, first + h)), all_rows,
                   pl.BlockSpec((G, None, 1, tq), lambda h, j: (h, j, 0, 0)), *x_out_specs),
        input_output_aliases={5: 0},
        scratch_shapes=[pltpu.VMEM((G, tq, lanes), F32), pltpu.VMEM((G, tq, lanes), F32),
                        pltpu.VMEM((G, nq, lanes, tq), F32)] + x_scr,
        compiler_params=_params(("arbitrary", "arbitrary")), name=name,
    )(q_aug, k_aug, v_aug, do_aug, lse_row, dproj, *x_in)


def _t5_bucket_np(d):
    n = np.maximum(d, 0).astype(np.int32)
    max_exact = N_BUCKETS // 2
    nf = np.maximum(n, 1).astype(np.float32)
    large = max_exact + (np.log(nf / max_exact) / math.log(MAX_DISTANCE / max_exact)
                         * (N_BUCKETS - max_exact)).astype(np.int32)
    large = np.minimum(large, N_BUCKETS - 1)
    return np.where(n < max_exact, n, large)


def _bucket_onehots():
    k = np.arange(BLOCK)[:, None]
    q = np.arange(BLOCK)[None, :]
    eye = np.eye(N_BUCKETS, dtype=np.float32)
    cur = eye[_t5_bucket_np(q - k).reshape(-1)]
    prev = eye[_t5_bucket_np(BLOCK + q - k).reshape(-1)]
    return cur, prev


SWA_K_COL = SWA_Q_HEADS * HEAD_DIM // (2 * HEAD_DIM)
SWA_V_COL = SWA_K_COL + 1


def _swa_terms(raw, bc, bp, far, sink, n):
    k = lax.broadcasted_iota(jnp.int32, (BLOCK, BLOCK), 0)
    q = lax.broadcasted_iota(jnp.int32, (BLOCK, BLOCK), 1)
    never = 2 * BLOCK
    s_c = raw[0] + bc
    s_p = raw[1] + bp
    s_m = raw[2] + jnp.where(n == 1, bp, far)
    s_c = jnp.where((k <= q) & (k >= jnp.where(n >= 1, 0, PAD_ROWS)), s_c, NEG)
    s_p = jnp.where(k > q + jnp.where(n >= 2, 0, never), s_p, NEG)
    s_m = jnp.where(k >= jnp.where(n >= 1, PAD_ROWS, never), s_m, NEG)
    m = jnp.maximum(jnp.maximum(jnp.max(s_c, axis=0, keepdims=True), jnp.max(s_p, axis=0, keepdims=True)),
                    jnp.maximum(jnp.max(s_m, axis=0, keepdims=True), sink))
    e = [jnp.exp(s_c - m), jnp.exp(s_p - m), jnp.exp(s_m - m)]
    e_s = jnp.exp(sink - m)
    l = (jnp.sum(e[0], axis=0, keepdims=True) + jnp.sum(e[1], axis=0, keepdims=True)
         + jnp.sum(e[2], axis=0, keepdims=True) + e_s)
    return e, e_s, l


SWA_STEP = 3


def _swa_specs():
    R = SWA_STEP

    def window(col):
        return ([pl.BlockSpec((BLOCK, BLOCK), lambda s, w=w: (jnp.maximum(R * s - 1 + w, 0), col)) for w in range(R + 1)]
                + [pl.BlockSpec((BLOCK, BLOCK), lambda s: (0, col))])

    qblk = pl.BlockSpec((R * BLOCK, SWA_Q_HEADS * HEAD_DIM), lambda s: (s, 0))
    bias = pl.BlockSpec((SWA_Q_HEADS, BLOCK, BLOCK), lambda s: (0, 0, 0))
    smem = pl.BlockSpec(memory_space=pltpu.SMEM)
    return qblk, window(SWA_K_COL), window(SWA_V_COL), bias, smem


def _swa_own_kv(tile_ref, kv):
    lane = lax.broadcasted_iota(jnp.int32, (BLOCK, 2 * HEAD_DIM), 1)
    t = tile_ref[...].astype(F32)
    return jnp.where(lane // HEAD_DIM == kv, t, pltpu.roll(t, HEAD_DIM, 1)).astype(BF16)


def _swa_fwd(proj, bc, bp, far, sinks, *, name):
    T = proj.shape[0]
    nb = T // BLOCK
    G = SWA_GROUP
    Hq = SWA_Q_HEADS
    lanes = 2 * HEAD_DIM

    R = SWA_STEP
    assert nb % R == 0

    def body(*refs):
        q_ref, k_refs, v_refs = refs[0], refs[1:R + 3], refs[R + 3:2 * R + 5]
        bc_ref, bp_ref, far_ref, sink_ref, o_ref = refs[2 * R + 5:]
        s = pl.program_id(0)
        lane = lax.broadcasted_iota(jnp.int32, (BLOCK, lanes), 1)
        kvs = range(SWA_KV_HEADS)
        kk = [[_swa_own_kv(ref, kv) for ref in k_refs] for kv in kvs]
        vv = [[_swa_own_kv(ref, kv) for ref in v_refs] for kv in kvs]
        chains = [(r, h) for r in range(R) for h in range(Hq)]
        tiles = lambda r: (r + 1, r, R + 1)
        q2 = {(r, pair): q_ref[r * BLOCK:(r + 1) * BLOCK, pair * lanes:(pair + 1) * lanes].astype(F32) * SCALE
              for r in range(R) for pair in range(Hq // 2)}
        qm = {c: jnp.where(lane // HEAD_DIM == c[1] % 2, q2[c[0], c[1] // 2], 0.0).astype(BF16) for c in chains}
        raw = {c: [lax.dot_general(kk[c[1] // G][w], qm[c], NT, preferred_element_type=F32) for w in tiles(c[0])]
               for c in chains}
        terms = {c: _swa_terms(raw[c], bc_ref[c[1]], bp_ref[c[1]], far_ref[c[1]], sink_ref[c[1]], R * s + c[0])
                 for c in chains}
        o_t = {c: sum(lax.dot_general(vv[c[1] // G][w], terms[c][0][b].astype(BF16), TN, preferred_element_type=F32)
                      for b, w in enumerate(tiles(c[0]))) for c in chains}
        outs = {c: (o_t[c] / terms[c][2]).T for c in chains}
        for r in range(R):
            for pair in range(Hq // 2):
                o_ref[r * BLOCK:(r + 1) * BLOCK, pair * lanes:(pair + 1) * lanes] = jnp.where(
                    lane < HEAD_DIM, outs[r, 2 * pair], outs[r, 2 * pair + 1]).astype(o_ref.dtype)

    qblk, keys, vals, bias, smem = _swa_specs()
    return pl.pallas_call(
        body, out_shape=jax.ShapeDtypeStruct((T, D_MODEL), BF16), grid=(nb // R,),
        in_specs=[qblk] + keys + vals + [bias, bias, smem, smem],
        out_specs=qblk,
        compiler_params=_params(("parallel",)), name=name,
    )(proj, *([proj] * (2 * R + 4)), bc, bp, far, sinks)


def _swa_bwd(proj, dmix, bc, bp, far, sinks, *, ex=None, name):
    T, width = proj.shape
    nb = T // BLOCK
    G = SWA_GROUP
    Hq = SWA_Q_HEADS
    lanes = 2 * HEAD_DIM
    qw = Hq * HEAD_DIM
    own_w = qw + 2 * lanes

    R = SWA_STEP
    assert nb % R == 0
    n_in = 2 * R + 10

    def body(*refs):
        q_ref, k_refs, v_refs = refs[0], refs[1:R + 3], refs[R + 3:2 * R + 5]
        do_ref, bc_ref, bp_ref, far_ref, sink_ref = refs[2 * R + 5:n_in]
        dp_ref, dbc_ref, dbp_ref, dbf_ref, dsk_ref, dk_acc, dv_acc = refs[n_in:]
        s = pl.program_id(0)

        @pl.when(s == 0)
        def _():
            for ref in (dk_acc, dv_acc, dbc_ref, dbp_ref, dbf_ref, dsk_ref):
                ref[...] = jnp.zeros(ref.shape, F32)

        lane = lax.broadcasted_iota(jnp.int32, (BLOCK, lanes), 1)
        kvs = range(SWA_KV_HEADS)
        kk = [[_swa_own_kv(ref, kv) for ref in k_refs] for kv in kvs]
        vv = [[_swa_own_kv(ref, kv) for ref in v_refs] for kv in kvs]
        chains = [(r, h) for r in range(R) for h in range(Hq)]
        blocks = range(3)
        tiles = lambda r: (r + 1, r, R + 1)
        sub = lambda ref, r, pair: ref[r * BLOCK:(r + 1) * BLOCK, pair * lanes:(pair + 1) * lanes]
        q2 = {(r, pair): sub(q_ref, r, pair).astype(F32) * SCALE for r in range(R) for pair in range(Hq // 2)}
        d2 = {(r, pair): sub(do_ref, r, pair) for r in range(R) for pair in range(Hq // 2)}
        own = [lane // HEAD_DIM == half for half in range(2)]
        qm = {c: jnp.where(own[c[1] % 2], q2[c[0], c[1] // 2], 0.0).astype(BF16) for c in chains}
        dom = {c: jnp.where(own[c[1] % 2], d2[c[0], c[1] // 2], jnp.zeros_like(d2[0, 0])) for c in chains}
        raw = {c: [lax.dot_general(kk[c[1] // G][w], qm[c], NT, preferred_element_type=F32) for w in tiles(c[0])]
               for c in chains}
        dp = {c: [lax.dot_general(vv[c[1] // G][w], dom[c], NT, preferred_element_type=F32) for w in tiles(c[0])]
              for c in chains}
        p, ds16 = {}, {}
        for c in chains:
            r, h = c
            n = R * s + r
            e, e_s, l = _swa_terms(raw[c], bc_ref[h], bp_ref[h], far_ref[h], sink_ref[h], n)
            inv = 1.0 / l
            ph = [e[b] * inv for b in blocks]
            delta = sum(jnp.sum(ph[b] * dp[c][b], axis=0, keepdims=True) for b in blocks)
            ds = [ph[b] * (dp[c][b] - delta) for b in blocks]
            dsk_ref[h] += -(e_s * inv) * delta
            dbc_ref[h] += ds[0]
            dbp_ref[h] += ds[1] + jnp.where(n == 1, ds[2], 0.0)
            dbf_ref[h] += jnp.where(n >= 2, ds[2], 0.0)
            p[c] = [x.astype(BF16) for x in ph]
            ds16[c] = [x.astype(BF16) for x in ds]
        dq_t = {c: sum(lax.dot_general(kk[c[1] // G][w], ds16[c][b], TN, preferred_element_type=F32)
                       for b, w in enumerate(tiles(c[0]))) for c in chains}
        group = [range(kv * G, (kv + 1) * G) for kv in kvs]
        dk = {(r, kv): [sum(jnp.dot(ds16[r, h][b], qm[r, h], preferred_element_type=F32) for h in group[kv])
                        for b in blocks] for r in range(R) for kv in kvs}
        dv = {(r, kv): [sum(jnp.dot(p[r, h][b], dom[r, h], preferred_element_type=F32) for h in group[kv])
                        for b in blocks] for r in range(R) for kv in kvs}
        for r in range(R):
            n = R * s + r
            rows = pl.ds(pl.multiple_of(n * BLOCK, BLOCK), BLOCK)
            prev_rows = pl.ds(pl.multiple_of(jnp.maximum(n - 1, 0) * BLOCK, BLOCK), BLOCK)
            for pair in range(Hq // 2):
                dp_ref[rows, pair * lanes:(pair + 1) * lanes] = (jnp.where(
                    lane < HEAD_DIM, dq_t[r, 2 * pair].T, dq_t[r, 2 * pair + 1].T) * SCALE).astype(dp_ref.dtype)
            for acc, ref in ((dk, dk_acc), (dv, dv_acc)):
                tot = [[a + pltpu.roll(a, HEAD_DIM, 1) for a in acc[r, kv]] for kv in kvs]
                both = [jnp.where(lane < HEAD_DIM, tot[0][b], tot[1][b]) for b in blocks]
                ref[rows, :] += both[0]
                ref[prev_rows, :] += both[1]
                ref[0:BLOCK, :] += both[2]

        @pl.when(s == nb // R - 1)
        def _():
            dp_ref[:, qw:qw + lanes] = dk_acc[...].astype(dp_ref.dtype)
            dp_ref[:, qw + lanes:own_w] = dv_acc[...].astype(dp_ref.dtype)

    qblk, keys, vals, bias, smem = _swa_specs()
    dsk = pl.BlockSpec((Hq, 1, BLOCK), lambda s: (0, 0, 0))
    grid = (nb // R,)
    body, x_in, x_in_specs, x_out, x_out_specs, x_scr = _carry(ex, grid, n_in, 5, body)
    tile = jax.ShapeDtypeStruct((Hq, BLOCK, BLOCK), F32)
    return pl.pallas_call(
        body,
        out_shape=(jax.ShapeDtypeStruct((T, width), BF16), tile, tile, tile,
                   jax.ShapeDtypeStruct((Hq, 1, BLOCK), F32), *x_out),
        grid=grid,
        in_specs=[qblk] + keys + vals + [qblk, bias, bias, smem, smem] + x_in_specs,
        out_specs=(pl.BlockSpec((T, own_w), lambda s: (0, 0)), bias, bias, bias, dsk, *x_out_specs),
        scratch_shapes=[pltpu.VMEM((T, lanes), F32), pltpu.VMEM((T, lanes), F32)] + x_scr,
        compiler_params=_params(("arbitrary",)), name=name,
    )(proj, *([proj] * (2 * R + 4)), dmix, bc, bp, far, sinks, *x_in)


def _bias_tiles(tab_t, oh_cur_t, oh_prev_t, *, name):
    Hq = tab_t.shape[0]

    def body(t_ref, oc_ref, op_ref, bc_ref, bp_ref):
        bc_ref[...] = jnp.dot(t_ref[...], oc_ref[...], precision=HIGHEST, preferred_element_type=F32)
        bp_ref[...] = jnp.dot(t_ref[...], op_ref[...], precision=HIGHEST, preferred_element_type=F32)

    vm = pl.BlockSpec(memory_space=pltpu.VMEM)
    shp = jax.ShapeDtypeStruct((Hq, BLOCK * BLOCK), F32)
    bc, bp = pl.pallas_call(body, out_shape=(shp, shp), in_specs=[vm] * 3, out_specs=(vm, vm),
                            compiler_params=_params(), name=name)(tab_t, oh_cur_t, oh_prev_t)
    return bc.reshape(Hq, BLOCK, BLOCK), bp.reshape(Hq, BLOCK, BLOCK)


def _small_grads(dbc, dbp, dbf, dsk, oh_cur, oh_prev, *, name):
    Hq = dbc.shape[0]

    def body(dbc_ref, dbp_ref, dbf_ref, dsk_ref, oc_ref, op_ref, tab_ref, sink_ref):
        tab = (jnp.dot(dbc_ref[...], oc_ref[...], precision=HIGHEST, preferred_element_type=F32)
               + jnp.dot(dbp_ref[...], op_ref[...], precision=HIGHEST, preferred_element_type=F32))
        far = jnp.sum(dbf_ref[...], axis=1, keepdims=True)
        last = lax.broadcasted_iota(jnp.int32, (Hq, N_BUCKETS), 1) == N_BUCKETS - 1
        tab_ref[...] = tab + jnp.where(last, far, 0.0)
        sink_ref[...] = jnp.sum(dsk_ref[...], axis=1, keepdims=True)

    vm = pl.BlockSpec(memory_space=pltpu.VMEM)
    return pl.pallas_call(
        body, out_shape=(jax.ShapeDtypeStruct((Hq, N_BUCKETS), F32), jax.ShapeDtypeStruct((Hq, 1), F32)),
        in_specs=[vm] * 6, out_specs=(vm, vm), compiler_params=_params(), name=name,
    )(dbc.reshape(Hq, -1), dbp.reshape(Hq, -1), dbf.reshape(Hq, -1), dsk.reshape(Hq, -1), oh_cur, oh_prev)


def _coords():
    return lax.axis_index("x"), lax.axis_index("y"), lax.axis_index("c")


class _Exchange:
    def __init__(self, inputs, out_shapes, scratch, start, finish):
        self.inputs, self.out_shapes, self.scratch, self.start, self.finish = inputs, out_shapes, scratch, start, finish


def _carry(ex, grid, n_in, n_out, body):
    if ex is None:
        return body, [], [], [], [], []
    ni, no = len(ex.inputs), len(ex.out_shapes)

    def at_step(which):
        cond = None
        for axis, n in enumerate(grid):
            c = pl.program_id(axis) == (0 if which == "first" else n - 1)
            cond = c if cond is None else cond & c
        return cond

    def wrapped(*refs):
        refs = list(refs)
        n_own_scr = len(refs) - (n_in + ni + n_out + no) - len(ex.scratch)
        own_in, side_in = refs[:n_in], refs[n_in:n_in + ni]
        own_out = refs[n_in + ni:n_in + ni + n_out]
        side_out = refs[n_in + ni + n_out:n_in + ni + n_out + no]
        rest = refs[n_in + ni + n_out + no:]
        own_scr, sems = rest[:n_own_scr], rest[n_own_scr:]

        @pl.when(at_step("first"))
        def _():
            ex.start(side_in, side_out, sems)

        body(*own_in, *own_out, *own_scr)

        @pl.when(at_step("last"))
        def _():
            ex.finish(side_in, side_out, sems)

    hbm = pl.BlockSpec(memory_space=pl.ANY)
    return wrapped, list(ex.inputs), [hbm] * ni, list(ex.out_shapes), [hbm] * no, list(ex.scratch)


def _run_exchange(ex, *, name):
    ni, no = len(ex.inputs), len(ex.out_shapes)

    def body(*refs):
        ins, outs, sems = refs[:ni], refs[ni:ni + no], refs[ni + no:]
        ex.start(ins, outs, sems)
        ex.finish(ins, outs, sems)

    hbm = pl.BlockSpec(memory_space=pl.ANY)
    return pl.pallas_call(
        body, out_shape=tuple(ex.out_shapes), in_specs=[hbm] * ni, out_specs=tuple([hbm] * no),
        scratch_shapes=ex.scratch, compiler_params=_params(), name=name)(*ex.inputs)


def _gather_exchange(shards):
    nt = len(shards)

    def copies(ins, outs, sems):
        send_sems, recv_sems, local_sems = sems
        x, y, c = _coords()
        me, sibling = (x, y, c), (x, y, 1 - c)
        chips = [(1 - x, y), (x, 1 - y), (1 - x, 1 - y)]

        def slot(t, dev):
            return outs[t].at[4 * dev[0] + 2 * dev[1] + dev[2]]

        def copy(t, k, block, to, src=None):
            dst = slot(t, block)
            return pltpu.make_async_remote_copy(
                src_ref=dst if src is None else src, dst_ref=dst,
                send_sem=send_sems.at[t, k], recv_sem=recv_sems.at[t, k], device_id=to, device_id_type=MESH)

        mine = [pltpu.make_async_copy(ins[t], slot(t, me), local_sems.at[t]) for t in range(nt)]
        first = []
        for t in range(nt):
            first.append(copy(t, 0, me, sibling, src=ins[t]))
            first += [copy(t, 1 + j, me, (*chip, c), src=ins[t]) for j, chip in enumerate(chips)]
        return copy, mine, first, me, sibling, chips, c

    def start(ins, outs, sems):
        _, mine, first, *_ = copies(ins, outs, sems)
        for cp in mine + first:
            cp.start()

    def finish(ins, outs, sems):
        copy, mine, first, me, sibling, chips, c = copies(ins, outs, sems)
        passed = []
        for j, chip in enumerate(chips):
            for t in range(nt):
                copy(t, 1 + j, (*chip, c), me).wait_recv()
                cp = copy(t, 4 + j, (*chip, c), sibling)
                cp.start()
                passed.append(cp)
        for t in range(nt):
            copy(t, 0, sibling, me).wait_recv()
            for j, chip in enumerate(chips):
                copy(t, 4 + j, (*chip, 1 - c), me).wait_recv()
        for cp in first + passed:
            cp.wait_send()
        for cp in mine:
            cp.wait()

    return _Exchange(
        list(shards), [jax.ShapeDtypeStruct((N_DEV,) + s.shape, s.dtype) for s in shards],
        [pltpu.SemaphoreType.DMA((nt, 7)), pltpu.SemaphoreType.DMA((nt, 7)), pltpu.SemaphoreType.DMA((nt,))],
        start, finish)


def _swap_exchange(arrays, n_slices, copies):
    nt = len(arrays)

    def start(ins, outs, sems):
        for cp in copies(ins, outs, sems):
            cp.start()

    def finish(ins, outs, sems):
        sends = copies(ins, outs, sems)
        for cp in sends:
            cp.wait_recv()
        for cp in sends:
            cp.wait_send()

    return _Exchange(
        list(arrays), [jax.ShapeDtypeStruct((n_slices,) + a.shape[1:], a.dtype) for a in arrays],
        [pltpu.SemaphoreType.DMA((nt, n_slices)), pltpu.SemaphoreType.DMA((nt, n_slices))], start, finish)


def _cores_exchange(gs):
    def copies(ins, outs, sems):
        send_sems, recv_sems = sems
        x, y, c = _coords()
        return [pltpu.make_async_remote_copy(
            src_ref=ins[t].at[2 * j + (1 - c)], dst_ref=outs[t].at[j],
            send_sem=send_sems.at[t, j], recv_sem=recv_sems.at[t, j], device_id=(x, y, 1 - c), device_id_type=MESH)
            for t in range(len(gs)) for j in range(4)]

    return _swap_exchange(gs, 4, copies)


def _chips_exchange(ps):
    def copies(ins, outs, sems):
        send_sems, recv_sems = sems
        x, y, c = _coords()
        peers = [(1 - x, y), (x, 1 - y), (1 - x, 1 - y)]
        return [pltpu.make_async_remote_copy(
            src_ref=ins[t].at[2 * px + py], dst_ref=outs[t].at[k],
            send_sem=send_sems.at[t, k], recv_sem=recv_sems.at[t, k], device_id=(px, py, c), device_id_type=MESH)
            for t in range(len(ps)) for k, (px, py) in enumerate(peers)]

    return _swap_exchange(ps, 3, copies)


def _add_cores(g, r, core, *, name):
    _, A, B = g.shape
    ta = _tile(A, 512, 16)

    def body(core_ref, a_ref, b_ref, o16_ref):
        o16_ref[...] = (a_ref[...] + b_ref[...]).astype(BF16)

    blk = (None, ta, B)
    return pl.pallas_call(
        body, out_shape=jax.ShapeDtypeStruct((4, A, B), BF16),
        grid_spec=pltpu.PrefetchScalarGridSpec(
            num_scalar_prefetch=1, grid=(4, A // ta),
            in_specs=[pl.BlockSpec(blk, lambda j, i, core_ref: (2 * j + core_ref[0], i, 0)),
                      pl.BlockSpec(blk, lambda j, i, core_ref: (j, i, 0))],
            out_specs=pl.BlockSpec(blk, lambda j, i, core_ref: (j, i, 0))),
        compiler_params=_params(("parallel", "parallel")), name=name)(core, g, r)


def _adamw_math(w, g, m, v):
    m = ADAM_B1 * m + (1.0 - ADAM_B1) * g
    v = ADAM_B2 * v + (1.0 - ADAM_B2) * (g * g)
    m_hat = m / (1.0 - ADAM_B1 ** ADAM_STEP)
    v_hat = v / (1.0 - ADAM_B2 ** ADAM_STEP)
    delta = -ADAM_LR * (m_hat / (jnp.sqrt(v_hat) + ADAM_EPS) + ADAM_WD * w)
    return delta, m, v


def _sum_adamw(mine, sib, r, where, w, m, v, *, segs, ta, name):
    Aw, Bw = w.shape
    Bg = mine.shape[2]
    assert Aw % ta == 0

    def body(where_ref, p_ref, s_ref, r0, r1, r2, w_ref, m_ref, v_ref, g_out, d_out, m_out, v_out):
        for gc, wc, n in segs:
            g = (((p_ref[:, gc:gc + n] + s_ref[:, gc:gc + n]) + r0[:, gc:gc + n].astype(F32))
                 + r1[:, gc:gc + n].astype(F32)) + r2[:, gc:gc + n].astype(F32)
            delta, m_new, v_new = _adamw_math(w_ref[:, wc:wc + n], g, m_ref[:, wc:wc + n], v_ref[:, wc:wc + n])
            g_out[:, wc:wc + n] = g
            d_out[:, wc:wc + n] = delta
            m_out[:, wc:wc + n] = m_new
            v_out[:, wc:wc + n] = v_new

    gblk = (None, ta, Bg)
    row = pl.BlockSpec((ta, Bw), lambda i, where_ref: (i, 0))
    rspecs = [pl.BlockSpec(gblk, (lambda i, where_ref, k=k: (k, i, 0))) for k in range(3)]
    shp = jax.ShapeDtypeStruct((Aw, Bw), F32)
    return pl.pallas_call(
        body, out_shape=(shp, shp, shp, shp),
        grid_spec=pltpu.PrefetchScalarGridSpec(
            num_scalar_prefetch=1, grid=(Aw // ta,),
            in_specs=[pl.BlockSpec(gblk, lambda i, where_ref: (2 * where_ref[0] + where_ref[1], i, 0)),
                      pl.BlockSpec(gblk, lambda i, where_ref: (where_ref[0], i, 0))] + rspecs + [row, row, row],
            out_specs=(row, row, row, row)),
        compiler_params=_params(("parallel",)), name=name)(where, mine, sib, r, r, r, w, m, v)


def _adamw(w, g, m, v, *, name):
    def body(w_ref, g_ref, m_ref, v_ref, d_out, m_out, v_out):
        delta, m_new, v_new = _adamw_math(w_ref[...], g_ref[...], m_ref[...], v_ref[...])
        d_out[...] = delta
        m_out[...] = m_new
        v_out[...] = v_new

    vm = pl.BlockSpec(memory_space=pltpu.VMEM)
    shp = jax.ShapeDtypeStruct(w.shape, F32)
    return pl.pallas_call(body, out_shape=(shp, shp, shp), in_specs=[vm] * 4, out_specs=(vm, vm, vm),
                          compiler_params=_params(), name=name)(w, g, m, v)


def _small_allreduce_adamw(s, w, m, v, *, name):
    R, W = s.shape

    def body(s_ref, w_ref, m_ref, v_ref, g_out, d_out, m_out, v_out, gath, send_sems, recv_sems):
        x, y, c = _coords()
        mine = 4 * x + 2 * y + c
        gath[mine] = s_ref[...]
        peers = [((1 - x) if k & 4 else x, (1 - y) if k & 2 else y, (1 - c) if k & 1 else c) for k in range(1, N_DEV)]
        sends = []
        for k in range(1, N_DEV):
            peer = peers[k - 1]
            sends.append(pltpu.make_async_remote_copy(
                src_ref=s_ref, dst_ref=gath.at[mine], send_sem=send_sems.at[k - 1], recv_sem=recv_sems.at[k - 1],
                device_id=peer, device_id_type=MESH))
        for cp in sends:
            cp.start()
        for k in range(1, N_DEV):
            peer = peers[k - 1]
            pltpu.make_async_remote_copy(
                src_ref=s_ref, dst_ref=gath.at[4 * peer[0] + 2 * peer[1] + peer[2]],
                send_sem=send_sems.at[k - 1], recv_sem=recv_sems.at[k - 1],
                device_id=peer, device_id_type=MESH).wait_recv()
        for cp in sends:
            cp.wait_send()
        g = gath[0]
        for d in range(1, N_DEV):
            g = g + gath[d]
        delta, m_new, v_new = _adamw_math(w_ref[...], g, m_ref[...], v_ref[...])
        g_out[...] = g
        d_out[...] = delta
        m_out[...] = m_new
        v_out[...] = v_new

    vm = pl.BlockSpec(memory_space=pltpu.VMEM)
    shp = jax.ShapeDtypeStruct((R, W), F32)
    return pl.pallas_call(
        body, out_shape=(shp, shp, shp, shp), in_specs=[vm] * 4, out_specs=(vm, vm, vm, vm),
        scratch_shapes=[pltpu.VMEM((N_DEV, R, W), F32), pltpu.SemaphoreType.DMA((N_DEV - 1,)),
                        pltpu.SemaphoreType.DMA((N_DEV - 1,))],
        compiler_params=_params(), name=name)(s, w, m, v)


def _pack_small(rel_bias, g1, g2, g3, g4, b_forget, sinks, extra=None, meta=None):
    misc = jnp.concatenate([rel_bias.reshape(-1), b_forget.reshape(-1), sinks.reshape(-1)])
    misc = jnp.concatenate([misc, jnp.zeros((D_MODEL - misc.shape[0],), F32)])[None]
    last = jnp.zeros((1, D_MODEL), F32) if extra is None else extra
    meta = jnp.zeros((N_META, D_MODEL), F32) if meta is None else meta
    return jnp.concatenate([g1, g2, g3, g4, misc, last, jnp.zeros((2, D_MODEL), F32), meta], axis=0)


def _unpack_small(p):
    nrb = N_BUCKETS * SWA_Q_HEADS
    misc = p[4]
    return dict(rel_bias=misc[:nrb].reshape(N_BUCKETS, SWA_Q_HEADS), ln_pre_mix=p[0:1], ln_post_mix=p[1:2],
                ln_pre_ffn=p[2:3], ln_post_ffn=p[3:4], b_forget=misc[nrb:nrb + 8].reshape(1, 8),
                sinks=misc[nrb + 8:nrb + 16].reshape(1, 8))


def _proj_runs():
    gw = FOX_GROUP * HEAD_DIM
    swa = SWA_Q_W + 2 * SWA_KV_HEADS * HEAD_DIM
    runs = [(0, swa)]
    for grp in range(FOX_HEADS // FOX_GROUP):
        runs += [(swa + part * FOX_W + grp * gw, swa + part * FOX_W + (grp + 1) * gw) for part in range(3)]
    return runs


def _columns_from_shards(gathered, runs, shard):
    pieces = []
    for start, stop in runs:
        for d in range(start // shard, (stop - 1) // shard + 1):
            lo = d * shard
            pieces.append(gathered[d][:, max(start, lo) - lo:min(stop, lo + shard) - lo])
    return jnp.concatenate(pieces, axis=1)


def _device_shards(qkv, gate, shard, padded):
    pos, segments = 0, []
    for start, stop in _proj_runs():
        segments.append((start, stop, qkv, pos))
        pos += stop - start
    segments.append((pos, pos + gate.shape[1], gate, 0))
    total = pos + gate.shape[1]
    assert total % shard == 0
    zeros = jnp.zeros((qkv.shape[0], padded - shard), qkv.dtype)
    out = []
    for d in range(total // shard):
        lo, hi = d * shard, (d + 1) * shard
        pieces = [arr[:, src + max(lo, s) - s:src + min(hi, e) - s]
                  for s, e, arr, src in sorted(segments, key=lambda seg: seg[0]) if max(lo, s) < min(hi, e)]
        out.append(jnp.concatenate(pieces + [zeros], axis=1))
    return jnp.stack(out)


def kernel(x, meta_tokens, rel_bias, ln_pre_mix, ln_post_mix, ln_pre_ffn, ln_post_ffn, w_in, b_forget, sinks, w_out, w_gate_up, w_down, loss_target, m_meta_tokens, m_rel_bias, m_ln_pre_mix, m_ln_post_mix, m_ln_pre_ffn, m_ln_post_ffn, m_w_in, m_b_forget, m_sinks, m_w_out, m_w_gate_up, m_w_down, v_meta_tokens, v_rel_bias, v_ln_pre_mix, v_ln_post_mix, v_ln_pre_ffn, v_ln_post_ffn, v_w_in, v_b_forget, v_sinks, v_w_out, v_w_gate_up, v_w_down):
    seq = x.shape[1]
    T = BLOCK + seq
    assert T % FOX_TILE == 0
    nq = T // FOX_TILE
    tm = _tile(T, 1056)
    cin = w_in.shape[2]
    hid = w_down.shape[1]
    assert w_gate_up.shape[2] == 2 * hid and cin <= W_IN_PAD and hid <= HID_PAD

    x_i, y_i, c_i = _coords()
    core = jnp.reshape(c_i, (1,)).astype(jnp.int32)
    where = jnp.stack([2 * x_i + y_i, c_i]).astype(jnp.int32)
    w_in_s = jnp.pad(w_in[0].astype(BF16), ((0, 0), (0, W_IN_PAD - cin)))
    w_gu_s = jnp.pad(w_gate_up[0].astype(BF16).reshape(D_MODEL, 2, hid), ((0, 0), (0, 0), (0, HID_PAD - hid)))
    w_gu_s = w_gu_s.reshape(D_MODEL, 2 * HID_PAD)
    w_down_s = jnp.pad(w_down[0].astype(BF16), ((0, HID_PAD - hid), (0, 0)))
    x_rows, target, g_in, g_meta = _pad_rows(x[0], loss_target[0], ex=_gather_exchange([w_in_s, meta_tokens]),
                                             name="ag_w_in_pad_rows")
    gather_rest = _gather_exchange([w_out[0].astype(BF16), w_gu_s, w_down_s])
    w_qkv = _columns_from_shards(g_in, _proj_runs(), cin)
    w_f = jnp.pad(_columns_from_shards(g_in, [(D_QKV, D_PROJ)], cin), ((0, 0), (0, BLOCK - FOX_HEADS)))
    meta_full = g_meta.transpose(1, 0, 2).reshape(N_META, D_MODEL)

    h0 = lax.dynamic_update_slice(x_rows, meta_full, (PAD_ROWS, 0))
    hn1, hn1_t = _rms_fwd(h0, ln_pre_mix, name="rms_pre_mix")
    proj = _matmul(hn1, w_qkv, out_dtype=BF16, tm=tm, tn=D_QKV, name="mm_in_proj")
    proj_f = _matmul(hn1, w_f, out_dtype=F32, tm=tm, tn=BLOCK, name="mm_in_proj_f")

    f_t = proj_f[:, :FOX_HEADS].T
    bf_col = b_forget.reshape(FOX_HEADS, 1)

    oh_cur, oh_prev = _bucket_onehots()
    bias_c, bias_p = _bias_tiles(rel_bias.T, jnp.asarray(oh_cur.T), jnp.asarray(oh_prev.T), name="bias_tiles")
    far = rel_bias[N_BUCKETS - 1]
    sink_v = sinks[0]
    mix_a = _swa_fwd(proj, bias_c, bias_p, far, sink_v, name="swa_fwd")

    cum_col = _fox_gates_fwd(f_t, bf_col, name="fox_gates_fwd")
    q_b, k_b, v_b = _fox_prep(proj, cum_col, name="fox_prep")
    mix, lse_row, g_out, g_gu, g_down = _fox_fwd(q_b, k_b, v_b, mix_a, ex=gather_rest, name="fox_fwd")
    w_out_full = g_out.reshape(D_MODEL, D_MODEL)
    w_down_full = g_down.reshape(N_DEV * HID_PAD, D_MODEL)

    a1 = _matmul(mix, w_out_full, out_dtype=F32, tm=tm, tn=D_MODEL, name="mm_out_proj")
    h1, hn2, hn2_t = _post_res_norm(a1, ln_post_mix, h0, ln_pre_ffn, name="post_mix_pre_ffn")
    gate, up, act, act_t = _gate_up_swiglu(hn2, g_gu, name="mm_gate_up")
    ff = _matmul(act, w_down_full, out_dtype=F32, tm=tm, tn=512, name="mm_down")
    dh2, dff, dg_post_ffn, loss_acc = _loss_head(ff, ln_post_ffn, h1, target, name="loss_head")

    dgu = _d_act_swiglu(dff, w_down_full, gate, up, name="mm_d_act")
    d_w_down = _matmul(act_t, dff, out_dtype=F32, tm=768, tn=512, name="mm_dw_down")
    dhn2 = _matmul(dgu, g_gu, nt=True, b_shards=True, out_dtype=F32, tm=tm, tn=512, name="mm_d_hn2")
    d_w_gu = _matmul(hn2_t, dgu, out_shards=True, out_dtype=F32, tm=512, tn=2 * HID_PAD, name="mm_dw_gate_up")
    dh1, dg_pre_ffn, da1, dg_post_mix = _rms_bwd(h1, ln_pre_ffn, dhn2, dh2, out_dtype=F32,
                                                 then=(a1, ln_post_mix), name="rms_bwd_pre_ffn_post_mix")
    dmix = _matmul(da1, w_out_full, nt=True, out_dtype=BF16, tm=tm, tn=D_MODEL, name="mm_d_mix")
    d_w_out = _matmul(mix, da1, ta=True, out_dtype=F32, tm=512, tn=D_MODEL, name="mm_dw_out")

    ffn_grads = [d_w_out.reshape(N_DEV, -1, D_MODEL), d_w_gu, d_w_down.reshape(N_DEV, HID_PAD, D_MODEL)]
    dproj_a, dbc, dbp, dbf, dsk, *ffn_sibling = _swa_bwd(
        proj, dmix, bias_c, bias_p, far, sink_v, ex=_cores_exchange(ffn_grads), name="swa_bwd")
    d_tab, d_sink = _small_grads(dbc, dbp, dbf, dsk, jnp.asarray(oh_cur), jnp.asarray(oh_prev), name="small_grads")
    ffn_sums = [_add_cores(g, r, core, name="rs_add_" + t)
                for g, r, t in zip(ffn_grads, ffn_sibling, ["w_out", "w_gate_up", "w_down"])]

    do_b = _fox_prep_bwd(dmix, mix, name="fox_prep_bwd")
    dproj, dcq, dck, *ffn_chips = _fox_bwd(
        q_b, k_b, v_b, do_b, lse_row, dproj_a, ex=_chips_exchange(ffn_sums), name="fox_bwd")
    df_t, d_bf = _fox_gates_bwd(dcq.reshape(FOX_HEADS, T), dck.reshape(FOX_HEADS, T), f_t, bf_col,
                                name="fox_gates_bwd")
    df = jnp.pad(df_t.T.astype(BF16), ((0, 0), (0, BLOCK - FOX_HEADS)))

    d_w_qkv = _matmul(hn1_t, dproj, out_dtype=F32, tm=512, tn=768, name="mm_dw_in")
    d_w_f = _matmul(hn1_t, df, out_dtype=F32, tm=512, tn=BLOCK, name="mm_dw_in_f")
    d_w_in = _device_shards(d_w_qkv, d_w_f[:, :FOX_HEADS], cin, W_IN_PAD)
    dhn1, in_sibling = _matmul(dproj, w_qkv, nt=True, out_dtype=F32, tm=tm, tn=512,
                               ex=_cores_exchange([d_w_in]), name="mm_d_hn1")
    in_sum = _add_cores(d_w_in, in_sibling, core, name="rs_add_w_in")
    dh0, dg_pre_mix, in_chips = _rms_bwd(h0, ln_pre_mix, dhn1, dh1, out_dtype=F32, dy2=(df, w_f),
                                         ex=_chips_exchange([in_sum]), name="rms_bwd_pre_mix")
    grad_x = dh0[BLOCK:][None]
    d_meta = dh0[PAD_ROWS:BLOCK]

    tags = ["w_in", "w_out", "w_gate_up", "w_down"]
    mine = [d_w_in] + ffn_grads
    from_sibling = [in_sibling] + list(ffn_sibling)
    from_chips = [in_chips] + list(ffn_chips)
    shard_w = [(w_in, m_w_in, v_w_in), (w_out, m_w_out, v_w_out), (w_gate_up, m_w_gate_up, v_w_gate_up),
               (w_down, m_w_down, v_w_down)]
    segs = [[(0, 0, cin)], [(0, 0, D_MODEL)], [(0, 0, hid), (HID_PAD, hid, hid)], [(0, 0, D_MODEL)]]
    tas = [256, BLOCK, 256, hid]
    big = [{}, {}, {}, {}]
    for i, t in enumerate(tags):
        w_t, m_t, v_t = shard_w[i]
        res = _sum_adamw(mine[i], from_sibling[i], from_chips[i], where, w_t[0], m_t[0], v_t[0], segs=segs[i],
                         ta=tas[i], name="rs_adamw_" + t)
        for kind in range(4):
            big[kind][t] = res[kind][None]

    loss_row = jnp.pad(loss_acc[0:1, 0:1] * (0.5 / D_MODEL), ((0, 0), (0, D_MODEL - 1)))
    s_small = _pack_small(d_tab.T, dg_pre_mix, dg_post_mix, dg_pre_ffn, dg_post_ffn, d_bf, d_sink,
                          extra=loss_row, meta=d_meta)
    w_s = _pack_small(rel_bias, ln_pre_mix, ln_post_mix, ln_pre_ffn, ln_post_ffn, b_forget, sinks)
    m_s = _pack_small(m_rel_bias, m_ln_pre_mix, m_ln_post_mix, m_ln_pre_ffn, m_ln_post_ffn, m_b_forget, m_sinks)
    v_s = _pack_small(v_rel_bias, v_ln_pre_mix, v_ln_post_mix, v_ln_pre_ffn, v_ln_post_ffn, v_b_forget, v_sinks)
    small = _small_allreduce_adamw(s_small, w_s, m_s, v_s, name="small_allreduce_adamw")
    loss = small[0][5, 0]
    mcols = meta_tokens.shape[1]
    g_meta_mine = lax.dynamic_slice(small[0][8:8 + N_META], (0, (4 * x_i + 2 * y_i + c_i) * mcols), (N_META, mcols))
    big[0]["meta_tokens"] = g_meta_mine
    for kind, arr in enumerate(_adamw(meta_tokens, g_meta_mine, m_meta_tokens, v_meta_tokens, name="adamw_meta")):
        big[kind + 1]["meta_tokens"] = arr
    small = [_unpack_small(p) for p in small]

    names = ["meta_tokens", "rel_bias", "ln_pre_mix", "ln_post_mix", "ln_pre_ffn", "ln_post_ffn", "w_in",
             "b_forget", "sinks", "w_out", "w_gate_up", "w_down"]
    outs = [loss, grad_x]
    for kind in range(4):
        for nme in names:
            outs.append(big[kind][nme] if nme in big[kind] else small[kind][nme])
    return tuple(outs)
```

```python
import math

import numpy as np
import jax
import jax.numpy as jnp
from jax import lax
from jax.experimental import pallas as pl
from jax.experimental.pallas import tpu as pltpu

F32 = jnp.float32
BF16 = jnp.bfloat16
HIGHEST = lax.Precision.HIGHEST
MESH = pl.DeviceIdType.MESH

N_DEV = 8
D_MODEL = 1024
N_META = 16
HEAD_DIM = 64
SWA_Q_HEADS = 8
SWA_KV_HEADS = 2
SWA_GROUP = 4
FOX_HEADS = 8
FOX_W = FOX_HEADS * HEAD_DIM
SWA_Q_W = SWA_Q_HEADS * HEAD_DIM
BLOCK = 128
PAD_ROWS = BLOCK - N_META
N_BUCKETS = 32
MAX_DISTANCE = 128
D_FF = 2816
D_QKV = 2304
D_PROJ = D_QKV + FOX_HEADS
D_PROJ_PAD = 2560
EPS = 1e-6
NEG = -1e30
SCALE = HEAD_DIM ** -0.5
ADAM_LR, ADAM_B1, ADAM_B2, ADAM_EPS, ADAM_WD, ADAM_STEP = 0.001, 0.9, 0.999, 1e-08, 0.01, 10
VMEM_LIMIT = 56 * 1024 * 1024
FOX_TILE = 384
FOX_GROUP = 4
W_IN_PAD = 384

NT = (((1,), (1,)), ((), ()))
NN = (((1,), (0,)), ((), ()))
TN = (((0,), (0,)), ((), ()))


def _params(sem=None, **kw):
    if sem is not None:
        kw["dimension_semantics"] = sem
    return pltpu.CompilerParams(vmem_limit_bytes=VMEM_LIMIT, **kw)


def _tile(n, target, mult=16):
    best = None
    for t in range(mult, min(n, target) + 1, mult):
        if n % t == 0:
            best = t
    assert best is not None, (n, target)
    return best


def _matmul(a, b, *, nt=False, ta=False, out_dtype, tm, tn, tk=None, ex=None, name):
    M, K = a.shape[::-1] if ta else a.shape
    assert not (ta and nt)
    N = b.shape[0] if nt else b.shape[1]
    tk = K if tk is None else tk
    assert M % tm == 0 and N % tn == 0 and K % tk == 0, (name, a.shape, b.shape, tm, tn, tk)
    nk = K // tk
    dn = NT if nt else (TN if ta else NN)
    a_spec = pl.BlockSpec((tk, tm), lambda i, j, k: (k, i)) if ta else pl.BlockSpec((tm, tk), lambda i, j, k: (i, k))

    def body(a_ref, b_ref, o_ref, *scr):
        part = lax.dot_general(a_ref[...], b_ref[...], dn, preferred_element_type=F32)
        if nk == 1:
            o_ref[...] = part.astype(o_ref.dtype)
        else:
            acc = scr[0]
            k = pl.program_id(2)

            @pl.when(k == 0)
            def _():
                acc[...] = part

            @pl.when(k > 0)
            def _():
                acc[...] += part

            @pl.when(k == nk - 1)
            def _():
                o_ref[...] = acc[...].astype(o_ref.dtype)

    if nt:
        b_spec = pl.BlockSpec((tn, tk), lambda i, j, k: (j, k))
    else:
        b_spec = pl.BlockSpec((tk, tn), lambda i, j, k: (k, j))
    out_shape = jax.ShapeDtypeStruct((M, N), out_dtype)
    out_spec = pl.BlockSpec((tm, tn), lambda i, j, k: (i, j))
    grid = (M // tm, N // tn, nk)
    body, x_in, x_in_specs, x_out, x_out_specs, x_scr = _carry(ex, grid, 2, 1, body)
    res = pl.pallas_call(
        body,
        out_shape=(out_shape, *x_out),
        grid=grid,
        in_specs=[a_spec, b_spec] + x_in_specs,
        out_specs=(out_spec, *x_out_specs),
        scratch_shapes=([pltpu.VMEM((tm, tn), F32)] if nk > 1 else []) + x_scr,
        compiler_params=_params(("parallel", "parallel", "arbitrary") if ex is None else ("arbitrary",) * 3),
        name=name,
    )(a, b, *x_in)
    return res[0] if ex is None else res


def _rstd(x):
    return lax.rsqrt(jnp.mean(x * x, axis=-1, keepdims=True) + EPS)


def _pad_rows(x, target, *, ex=None, name):
    S, D = x.shape
    nb = S // BLOCK + 1

    def body(x_ref, t_ref, xo_ref, to_ref):
        keep = pl.program_id(0) > 0
        xo_ref[...] = jnp.where(keep, x_ref[...], 0.0)
        to_ref[...] = jnp.where(keep, t_ref[...], 0.0)

    src = pl.BlockSpec((BLOCK, D), lambda i: (jnp.maximum(i - 1, 0), 0))
    dst = pl.BlockSpec((BLOCK, D), lambda i: (i, 0))
    shp = jax.ShapeDtypeStruct((BLOCK + S, D), x.dtype)
    grid = (nb,)
    body, x_in, x_in_specs, x_out, x_out_specs, x_scr = _carry(ex, grid, 2, 2, body)
    return pl.pallas_call(
        body, out_shape=(shp, shp, *x_out), grid=grid,
        in_specs=[src, src] + x_in_specs, out_specs=(dst, dst, *x_out_specs), scratch_shapes=x_scr,
        compiler_params=_params(("arbitrary",)), name=name)(x, target, *x_in)


def _rms_fwd(x, g, *, name):
    T, D = x.shape
    tm = _tile(T, 512)

    def body(x_ref, g_ref, o_ref, ot_ref):
        x = x_ref[...]
        y = x * _rstd(x) * g_ref[...]
        o_ref[...] = y.astype(o_ref.dtype)
        ot_ref[...] = y.T.astype(ot_ref.dtype)

    return pl.pallas_call(
        body, out_shape=(jax.ShapeDtypeStruct((T, D), BF16), jax.ShapeDtypeStruct((D, T), BF16)), grid=(T // tm,),
        in_specs=[pl.BlockSpec((tm, D), lambda i: (i, 0)), pl.BlockSpec((1, D), lambda i: (0, 0))],
        out_specs=(pl.BlockSpec((tm, D), lambda i: (i, 0)), pl.BlockSpec((D, tm), lambda i: (0, i))),
        compiler_params=_params(("parallel",)), name=name)(x, g)


def _post_res_norm(a, g_post, h, g_pre, *, name):
    T, D = a.shape
    tm = _tile(T, 384, BLOCK)

    def body(a_ref, gp_ref, h_ref, gn_ref, h1_ref, o_ref):
        a = a_ref[...]
        h1 = h_ref[...] + a * _rstd(a) * gp_ref[...]
        h1_ref[...] = h1
        o_ref[...] = (h1 * _rstd(h1) * gn_ref[...]).astype(o_ref.dtype)

    row = pl.BlockSpec((tm, D), lambda i: (i, 0))
    vec = pl.BlockSpec((1, D), lambda i: (0, 0))
    return pl.pallas_call(
        body, out_shape=(jax.ShapeDtypeStruct((T, D), F32), jax.ShapeDtypeStruct((T, D), BF16)), grid=(T // tm,),
        in_specs=[row, vec, row, vec], out_specs=(row, row),
        compiler_params=_params(("parallel",)), name=name)(a, g_post, h, g_pre)


def _loss_head(a, g, h, target, *, name):
    T, D = a.shape
    tm = _tile(T, 512)

    def body(a_ref, g_ref, h_ref, t_ref, dy_ref, da_ref, dg_ref, loss_ref):
        i = pl.program_id(0)
        a = a_ref[...]
        r = _rstd(a)
        ah = a * r
        y = h_ref[...] + ah * g_ref[...]
        rows = i * tm + lax.broadcasted_iota(jnp.int32, (tm, 1), 0)
        err = jnp.where(rows >= BLOCK, y - t_ref[...], 0.0)
        dy = err / D
        dy_ref[...] = dy
        dah = dy * g_ref[...]
        da_ref[...] = (r * (dah - ah * jnp.mean(dah * ah, axis=-1, keepdims=True))).astype(da_ref.dtype)
        part = jnp.sum(jnp.sum(err * err, axis=1, keepdims=True), axis=0, keepdims=True)

        @pl.when(i == 0)
        def _():
            loss_ref[...] = jnp.zeros_like(loss_ref)
            dg_ref[...] = jnp.zeros_like(dg_ref)

        loss_ref[...] += jnp.broadcast_to(part, loss_ref.shape)
        dg_ref[...] += jnp.sum(dy * ah, axis=0, keepdims=True)

    row = pl.BlockSpec((tm, D), lambda i: (i, 0))
    vec = pl.BlockSpec((1, D), lambda i: (0, 0))
    return pl.pallas_call(
        body, out_shape=(jax.ShapeDtypeStruct((T, D), F32), jax.ShapeDtypeStruct((T, D), BF16),
                         jax.ShapeDtypeStruct((1, D), F32), jax.ShapeDtypeStruct((8, 128), F32)),
        grid=(T // tm,),
        in_specs=[row, vec, row, row],
        out_specs=(row, row, vec, pl.BlockSpec((8, 128), lambda i: (0, 0))),
        compiler_params=_params(("arbitrary",)), name=name)(a, g, h, target)


def _rms_bwd(x, g, dy, res, *, out_dtype, dy2=None, then=None, ex=None, name):
    T, D = x.shape
    tm = _tile(T, 512)
    has_res = res is not None
    has_dy2 = 2 if dy2 is not None else 0
    n_in = 3 + has_dy2 + has_res + (2 if then is not None else 0)
    n_out = 2 + (2 if then is not None else 0)

    def pull_back(x, g, dy):
        r = _rstd(x)
        xh = x * r
        dxh = dy * g
        return r * (dxh - xh * jnp.mean(dxh * xh, axis=-1, keepdims=True)), jnp.sum(dy * xh, axis=0, keepdims=True)

    def body(*refs):
        ins, outs = refs[:n_in], refs[n_in:]
        i = pl.program_id(0)

        @pl.when(i == 0)
        def _():
            for ref in outs[1::2]:
                ref[...] = jnp.zeros_like(ref)

        dy_all = ins[2][...].astype(F32)
        if has_dy2:
            dy_all = dy_all + lax.dot_general(ins[3][...], ins[4][...], NT, preferred_element_type=F32)
        dx, dg = pull_back(ins[0][...], ins[1][...], dy_all)
        if has_res:
            dx = dx + ins[3 + has_dy2][...]
        outs[0][...] = dx.astype(outs[0].dtype)
        outs[1][...] += dg
        if then is not None:
            dx2, dg2 = pull_back(ins[n_in - 2][...], ins[n_in - 1][...], dx)
            outs[2][...] = dx2.astype(outs[2].dtype)
            outs[3][...] += dg2

    row = pl.BlockSpec((tm, D), lambda i: (i, 0))
    vec = pl.BlockSpec((1, D), lambda i: (0, 0))
    ins = [x, g, dy] + (list(dy2) if has_dy2 else []) + ([res] if has_res else []) + (list(then) if then is not None else [])
    dy2_specs = ([pl.BlockSpec((tm, dy2[0].shape[1]), lambda i: (i, 0)), pl.BlockSpec(dy2[1].shape, lambda i: (0, 0))]
                 if has_dy2 else [])
    in_specs = [row, vec, row] + dy2_specs + ([row] if has_res else []) + ([row, vec] if then is not None else [])
    out_shape = [jax.ShapeDtypeStruct((T, D), out_dtype), jax.ShapeDtypeStruct((1, D), F32)]
    out_specs = [row, vec]
    if then is not None:
        out_shape += [jax.ShapeDtypeStruct((T, D), BF16), jax.ShapeDtypeStruct((1, D), F32)]
        out_specs += [row, vec]
    grid = (T // tm,)
    body, x_in, x_in_specs, x_out, x_out_specs, x_scr = _carry(ex, grid, n_in, n_out, body)
    return pl.pallas_call(
        body, out_shape=(*out_shape, *x_out), grid=grid,
        in_specs=in_specs + x_in_specs, out_specs=(*out_specs, *x_out_specs), scratch_shapes=x_scr,
        compiler_params=_params(("arbitrary",)), name=name)(*ins, *x_in)


def _gate_up_swiglu(a, w_t, *, name):
    T, D = a.shape
    F = w_t.shape[0] // 2
    tm = _tile(T, 1408, BLOCK)
    n = _tile(F, 256, BLOCK)

    def body(a_ref, wg_ref, wu_ref, g_ref, u_ref, o_ref, ot_ref):
        x = a_ref[...]
        g = lax.dot_general(x, wg_ref[...], NT, preferred_element_type=F32)
        u = lax.dot_general(x, wu_ref[...], NT, preferred_element_type=F32)
        g16, u16 = g.astype(BF16), u.astype(BF16)
        g_ref[...] = g16
        u_ref[...] = u16
        gr = g16.astype(F32)
        act = gr / (1.0 + jnp.exp(-gr)) * u16.astype(F32)
        o_ref[...] = act.astype(o_ref.dtype)
        ot_ref[...] = act.T.astype(ot_ref.dtype)

    tile = pl.BlockSpec((tm, n), lambda i, j: (i, j))
    shp = jax.ShapeDtypeStruct((T, F), BF16)
    return pl.pallas_call(
        body, out_shape=(shp, shp, shp, jax.ShapeDtypeStruct((F, T), BF16)), grid=(T // tm, F // n),
        in_specs=[pl.BlockSpec((tm, D), lambda i, j: (i, 0)),
                  pl.BlockSpec((n, D), lambda i, j: (j, 0)),
                  pl.BlockSpec((n, D), lambda i, j: (j + F // n, 0))],
        out_specs=(tile, tile, tile, pl.BlockSpec((n, tm), lambda i, j: (j, i))),
        compiler_params=_params(("parallel", "parallel")), name=name)(a, w_t, w_t)


def _d_act_swiglu(dff, w_down, gate, up, *, name):
    T, D = dff.shape
    F = w_down.shape[0]
    tm = _tile(T, 384)
    chunk = 768
    assert F % BLOCK == 0

    def body(d_ref, w_ref, g_ref, u_ref, o_ref):
        dy = d_ref[...]
        for c in range(0, F, chunk):
            e = min(c + chunk, F)
            d = lax.dot_general(dy, w_ref[c:e, :], NT, preferred_element_type=F32)
            g = g_ref[:, c:e].astype(F32)
            u = u_ref[:, c:e].astype(F32)
            sg = 1.0 / (1.0 + jnp.exp(-g))
            o_ref[:, c:e] = (d * u * (sg * (1.0 + g * (1.0 - sg)))).astype(o_ref.dtype)
            o_ref[:, F + c:F + e] = (d * (g * sg)).astype(o_ref.dtype)

    row = pl.BlockSpec((tm, F), lambda i: (i, 0))
    return pl.pallas_call(
        body, out_shape=jax.ShapeDtypeStruct((T, 2 * F), BF16), grid=(T // tm,),
        in_specs=[pl.BlockSpec((tm, D), lambda i: (i, 0)), pl.BlockSpec((F, D), lambda i: (0, 0)), row, row],
        out_specs=pl.BlockSpec((tm, 2 * F), lambda i: (i, 0)),
        compiler_params=_params(("parallel",)), name=name)(dff, w_down, gate, up)


def _fox_gates_fwd(f_t, b, *, name):
    H, T = f_t.shape
    nb = T // BLOCK

    def body(f_ref, b_ref, col_ref):
        f = f_ref[...] + b_ref[...]
        ls = jnp.minimum(f, 0.0) - jnp.log(1.0 + jnp.exp(-jnp.abs(f)))
        t = lax.broadcasted_iota(jnp.int32, (H, T), 1)
        ls = jnp.where(t >= PAD_ROWS, ls, 0.0)
        upper = (lax.broadcasted_iota(jnp.int32, (BLOCK, BLOCK), 0)
                 <= lax.broadcasted_iota(jnp.int32, (BLOCK, BLOCK), 1)).astype(F32)
        carry = jnp.zeros((H, 1), F32)
        for blk in range(nb):
            seg = ls[:, blk * BLOCK:(blk + 1) * BLOCK]
            pre = jnp.dot(seg, upper, precision=HIGHEST, preferred_element_type=F32) + carry
            key_gate = jnp.where(t[:, blk * BLOCK:(blk + 1) * BLOCK] >= PAD_ROWS, pre, -NEG)
            terms = list(_split3(pre)) + list(_split3(key_gate))
            col_ref[blk * BLOCK:(blk + 1) * BLOCK, :] = jnp.concatenate(
                terms + [jnp.zeros((BLOCK - len(terms) * H, BLOCK), F32)], axis=0).T.astype(col_ref.dtype)
            carry = pre[:, BLOCK - 1:BLOCK]

    vm = pl.BlockSpec(memory_space=pltpu.VMEM)
    return pl.pallas_call(
        body, out_shape=jax.ShapeDtypeStruct((T, BLOCK), BF16),
        in_specs=[vm, vm], out_specs=vm,
        compiler_params=_params(), name=name)(f_t, b)


def _fox_gates_bwd(dcq, dck, f_t, b, *, name):
    H, T = f_t.shape
    nb = T // BLOCK

    def body(dq_ref, d_ref, f_ref, b_ref, df_ref, db_ref):
        lower = (lax.broadcasted_iota(jnp.int32, (BLOCK, BLOCK), 0)
                 >= lax.broadcasted_iota(jnp.int32, (BLOCK, BLOCK), 1)).astype(F32)
        carry = jnp.zeros((H, 1), F32)
        for blk in range(nb - 1, -1, -1):
            seg = dq_ref[:, blk * BLOCK:(blk + 1) * BLOCK] - d_ref[:, blk * BLOCK:(blk + 1) * BLOCK]
            suf = jnp.dot(seg, lower, precision=HIGHEST, preferred_element_type=F32) + carry
            df_ref[:, blk * BLOCK:(blk + 1) * BLOCK] = suf
            carry = suf[:, 0:1]
        f = f_ref[...] + b_ref[...]
        t = lax.broadcasted_iota(jnp.int32, (H, T), 1)
        df = jnp.where(t >= PAD_ROWS, df_ref[...] / (1.0 + jnp.exp(f)), 0.0)
        df_ref[...] = df
        db_ref[...] = jnp.sum(df, axis=1, keepdims=True)

    vm = pl.BlockSpec(memory_space=pltpu.VMEM)
    return pl.pallas_call(
        body, out_shape=(jax.ShapeDtypeStruct((H, T), F32), jax.ShapeDtypeStruct((H, 1), F32)),
        in_specs=[vm, vm, vm, vm], out_specs=(vm, vm),
        compiler_params=_params(), name=name)(dcq, dck, f_t, b)


def _fox_lanes(parity):
    base = HEAD_DIM * (1 - parity)
    return base, base + 3


def _split3(c):
    hi = c.astype(BF16).astype(F32)
    r = c - hi
    mid = r.astype(BF16).astype(F32)
    lo = (r - mid).astype(BF16).astype(F32)
    return hi, mid, lo


def _lanes(lane, parity, data, start, terms, ones_at=None, fill=1.0):
    out = jnp.zeros((), F32) if ones_at is None else jnp.where((lane >= ones_at) & (lane < ones_at + 3), fill, 0.0)
    for i, t in enumerate(terms):
        out = jnp.where(lane == start + i, t, out)
    return jnp.where(lane // HEAD_DIM == parity, data, out)


def _fox_prep(proj, cum_col, *, name):
    T = proj.shape[0]
    tm = _tile(T, 1408, BLOCK)
    nt = T // tm
    H = FOX_HEADS
    lanes = 2 * HEAD_DIM
    first = (proj.shape[1] - 3 * H * HEAD_DIM) // lanes

    def body(q_ref, k_ref, v_ref, c_ref, qa_ref, ka_ref, va_ref):
        p = pl.program_id(0)
        i = pl.program_id(1)
        lane = lax.broadcasted_iota(jnp.int32, (1, lanes), 1)
        src = lax.broadcasted_iota(jnp.int32, (lanes, lanes), 0)
        dst = lax.broadcasted_iota(jnp.int32, (lanes, lanes), 1)
        q2 = q_ref[...].astype(F32) * SCALE
        k2 = k_ref[...].astype(F32)
        v2 = v_ref[...].astype(F32)
        gates = c_ref[...]
        def placed(h, first_term, start):
            pick = ((src % FOX_HEADS == h) & (src // FOX_HEADS - first_term == dst - start)
                    & (dst >= start) & (dst < start + 3))
            return jnp.dot(gates, pick.astype(BF16), preferred_element_type=F32)

        moved = [(placed(2 * p + e, 0, _fox_lanes(e)[1]), placed(2 * p + e, 3, _fox_lanes(e)[0])) for e in range(2)]
        for e in range(2):
            kc, qc = _fox_lanes(e)
            own = lane // HEAD_DIM == e
            minus = jnp.where((lane >= kc) & (lane < kc + 3), -1.0, 0.0)
            ones_q = jnp.where((lane >= qc) & (lane < qc + 3), 1.0, 0.0)
            ones_k = jnp.where((lane >= kc) & (lane < kc + 3), 1.0, 0.0)
            qa_ref[e] = jnp.where(own, q2, moved[e][0] + minus).astype(BF16)
            ka_ref[e] = jnp.where(own, k2, moved[e][1] + ones_q).astype(BF16)
            va_ref[e] = jnp.where(own, v2, ones_k).astype(BF16)

    pairs = FOX_GROUP // 2

    def col(part):
        return pl.BlockSpec((tm, lanes),
                            lambda p, i: (i, first + 3 * pairs * (p // pairs) + part * pairs + p % pairs))

    out = pl.BlockSpec((2, tm, lanes), lambda p, i: (p, i, 0))
    shp = jax.ShapeDtypeStruct((H, T, lanes), BF16)
    return pl.pallas_call(
        body, out_shape=(shp, shp, shp), grid=(H // 2, nt),
        in_specs=[col(0), col(1), col(2), pl.BlockSpec((tm, lanes), lambda p, i: (i, 0))],
        out_specs=(out, out, out),
        compiler_params=_params(("parallel", "parallel")), name=name)(proj, proj, proj, cum_col)


def _fox_fwd(q_aug, k_aug, v_aug, mix, *, ex=None, name):
    H, T, lanes = q_aug.shape
    tq = FOX_TILE
    nq = T // tq
    G = FOX_HEADS

    def body(q_ref, k_ref, v_ref, mix_ref, o_ref, lse_ref, m_scr, acc_scr):
        i = pl.program_id(1)
        m_scr[...] = jnp.full(m_scr.shape, NEG, F32)
        acc_scr[...] = jnp.zeros(acc_scr.shape, F32)

        def step(kb, diag):
            off = pl.multiple_of(kb * tq, tq)
            s_t = [lax.dot_general(k_ref[g, pl.ds(off, tq), :], q_ref[g], NT, preferred_element_type=F32)
                   for g in range(G)]
            if diag:
                r = lax.broadcasted_iota(jnp.int32, (tq, tq), 0)
                c = lax.broadcasted_iota(jnp.int32, (tq, tq), 1)
                s_t = [jnp.where(c >= r, s, NEG) for s in s_t]
            m_prev = [m_scr[g] for g in range(G)]
            m_new = [jnp.maximum(m_prev[g], jnp.max(s_t[g], axis=0, keepdims=True)) for g in range(G)]
            p_t = [jnp.exp(s_t[g] - m_new[g]).astype(BF16) for g in range(G)]
            pv = [lax.dot_general(v_ref[g, pl.ds(off, tq), :], p_t[g], TN, preferred_element_type=F32)
                  for g in range(G)]
            for g in range(G):
                acc_scr[g] = jnp.exp(m_prev[g] - m_new[g]) * acc_scr[g] + pv[g]
                m_scr[g] = m_new[g]

        def loop_body(kb, carry):
            step(kb, False)
            return carry

        lax.fori_loop(0, i, loop_body, 0)
        step(i, True)
        lane = lax.broadcasted_iota(jnp.int32, (tq, lanes), 1)
        outs = []
        for g in range(G):
            ones = _fox_lanes(g % 2)[0]
            acc = acc_scr[g]
            lse_ref[g] = m_scr[g] + jnp.log(acc[ones:ones + 1, :])
            acc_t = acc.T
            outs.append(acc_t / acc_t[:, ones:ones + 1])
        for pair in range(G // 2):
            o_ref[:, pair * lanes:(pair + 1) * lanes] = jnp.where(
                lane < HEAD_DIM, outs[2 * pair], outs[2 * pair + 1]).astype(o_ref.dtype)

    blk = pl.BlockSpec((G, tq, lanes), lambda h, i: (h, i, 0))
    full = pl.BlockSpec((G, T, lanes), lambda h, i: (h, 0, 0))
    grid = (H // G, nq)
    first = mix.shape[1] // (G * HEAD_DIM) - H // G
    body, x_in, x_in_specs, x_out, x_out_specs, x_scr = _carry(ex, grid, 4, 2, body)
    return pl.pallas_call(
        body,
        out_shape=(jax.ShapeDtypeStruct(mix.shape, mix.dtype), jax.ShapeDtypeStruct((H, nq, 1, tq), F32), *x_out),
        grid=grid,
        in_specs=[blk, full, full, pl.BlockSpec(memory_space=pl.ANY)] + x_in_specs,
        out_specs=(pl.BlockSpec((tq, G * HEAD_DIM), lambda h, i: (i, first + h)),
                   pl.BlockSpec((G, None, 1, tq), lambda h, i: (h, i, 0, 0)), *x_out_specs),
        input_output_aliases={3: 0},
        scratch_shapes=[pltpu.VMEM((G, 1, tq), F32), pltpu.VMEM((G, lanes, tq), F32)] + x_scr,
        compiler_params=_params(("arbitrary", "arbitrary")), name=name)(q_aug, k_aug, v_aug, mix, *x_in)


def _fox_prep_bwd(dmix, mix, *, name):
    T = dmix.shape[0]
    H = FOX_HEADS
    tm = _tile(T, 1408, BLOCK)
    lanes = 2 * HEAD_DIM
    first = mix.shape[1] // lanes - H // 2

    def body(d_ref, o_ref, da_ref):
        lane = lax.broadcasted_iota(jnp.int32, (1, lanes), 1)
        d2 = d_ref[...].astype(F32)
        prod = d2 * o_ref[...].astype(F32)
        for e in range(2):
            delta = jnp.sum(jnp.where(lane // HEAD_DIM == e, prod, 0.0), axis=1, keepdims=True)
            da_ref[e] = _lanes(lane, e, d2, _fox_lanes(e)[0], _split3(-delta)).astype(BF16)

    pair = pl.BlockSpec((tm, lanes), lambda p, i: (i, first + p))
    return pl.pallas_call(
        body, out_shape=jax.ShapeDtypeStruct((H, T, lanes), BF16), grid=(H // 2, T // tm),
        in_specs=[pair, pair],
        out_specs=pl.BlockSpec((2, tm, lanes), lambda p, i: (p, i, 0)),
        compiler_params=_params(("parallel", "parallel")), name=name)(dmix, mix)


def _fox_bwd(q_aug, k_aug, v_aug, do_aug, lse_row, dproj, *, ex=None, name):
    H, T, lanes = q_aug.shape
    tq = FOX_TILE
    nq = T // tq
    G = FOX_GROUP

    def side_by_side(tiles, scale=None):
        lane = lax.broadcasted_iota(jnp.int32, tiles[0].shape, 1)
        out = [jnp.where(lane < HEAD_DIM, tiles[2 * p], tiles[2 * p + 1]) for p in range(G // 2)]
        out = jnp.concatenate(out, axis=1)
        return out if scale is None else out * scale

    def body(q_ref, k_ref, v_ref, do_ref, lse_ref, dproj_in, out_ref, dcq_ref, dck_ref, dk_acc, dv_acc, dq_ref):
        j = pl.program_id(1)

        @pl.when(j == 0)
        def _():
            dq_ref[...] = jnp.zeros(dq_ref.shape, F32)
            dcq_ref[...] = jnp.zeros(dcq_ref.shape, F32)

        dk_acc[...] = jnp.zeros(dk_acc.shape, F32)
        dv_acc[...] = jnp.zeros(dv_acc.shape, F32)

        def step(qb, diag):
            off = pl.multiple_of(qb * tq, tq)
            heads = range(G)
            qa = [q_ref[g, pl.ds(off, tq), :] for g in heads]
            da = [do_ref[g, pl.ds(off, tq), :] for g in heads]
            s_t = [lax.dot_general(k_ref[g], qa[g], NT, preferred_element_type=F32) for g in heads]
            dp_t = [lax.dot_general(v_ref[g], da[g], NT, preferred_element_type=F32) for g in heads]
            p_t = [jnp.exp(s_t[g] - lse_ref[g, qb]) for g in heads]
            if diag:
                r = lax.broadcasted_iota(jnp.int32, (tq, tq), 0)
                c = lax.broadcasted_iota(jnp.int32, (tq, tq), 1)
                p_t = [jnp.where(c >= r, p, 0.0) for p in p_t]
            dsb = [(p_t[g] * dp_t[g]).astype(BF16) for g in heads]
            dv = [jnp.dot(p_t[g].astype(BF16), da[g], preferred_element_type=F32) for g in heads]
            dk = [jnp.dot(dsb[g], qa[g], preferred_element_type=F32) for g in heads]
            dq = [lax.dot_general(k_ref[g], dsb[g], TN, preferred_element_type=F32) for g in heads]
            for g in heads:
                dv_acc[g] += dv[g]
                dk_acc[g] += dk[g]
                dq_ref[g, qb] += dq[g]
                dcq_ref[g, qb] += jnp.sum(dsb[g].astype(F32), axis=0, keepdims=True)

        step(j, True)

        def loop_body(qb, carry):
            step(qb, False)
            return carry

        lax.fori_loop(j + 1, nq, loop_body, 0)
        dk = [dk_acc[g] for g in range(G)]
        out_ref[:, 0:wide] = side_by_side([dq_ref[g, j].T for g in range(G)], SCALE).astype(out_ref.dtype)
        out_ref[:, wide:2 * wide] = side_by_side(dk).astype(out_ref.dtype)
        out_ref[:, 2 * wide:3 * wide] = side_by_side([dv_acc[g] for g in range(G)]).astype(out_ref.dtype)
        for g in range(G):
            kc = _fox_lanes(g % 2)[0]
            dck_ref[g] = -dk[g].T[kc:kc + 1, :]

    blk = pl.BlockSpec((G, tq, lanes), lambda h, j: (h, j, 0))
    full = pl.BlockSpec((G, T, lanes), lambda h, j: (h, 0, 0))
    wide = G * HEAD_DIM
    first = dproj.shape[1] // (3 * wide) - H // G
    grid = (H // G, nq)
    body, x_in, x_in_specs, x_out, x_out_specs, x_scr = _carry(ex, grid, 6, 3, body)
    rows = jax.ShapeDtypeStruct((H, nq, 1, tq), F32)
    all_rows = pl.BlockSpec((G, nq, 1, tq), lambda h, j: (h, 0, 0, 0))
    return pl.pallas_call(
        body,
        out_shape=(jax.ShapeDtypeStruct(dproj.shape, dproj.dtype), rows, rows, *x_out),
        grid=grid,
        in_specs=[full, blk, blk, full, all_rows, pl.BlockSpec(memory_space=pl.ANY)] + x_in_specs,
        out_specs=(pl.BlockSpec((tq, 3 * wide), lambda h, j: (j, first + h)), all_rows,
                   pl.BlockSpec((G, None, 1, tq), lambda h, j: (h, j, 0, 0)), *x_out_specs),
        input_output_aliases={5: 0},
        scratch_shapes=[pltpu.VMEM((G, tq, lanes), F32), pltpu.VMEM((G, tq, lanes), F32),
                        pltpu.VMEM((G, nq, lanes, tq), F32)] + x_scr,
        compiler_params=_params(("arbitrary", "arbitrary")), name=name,
    )(q_aug, k_aug, v_aug, do_aug, lse_row, dproj, *x_in)


def _t5_bucket_np(d):
    n = np.maximum(d, 0).astype(np.int32)
    max_exact = N_BUCKETS // 2
    nf = np.maximum(n, 1).astype(np.float32)
    large = max_exact + (np.log(nf / max_exact) / math.log(MAX_DISTANCE / max_exact)
                         * (N_BUCKETS - max_exact)).astype(np.int32)
    large = np.minimum(large, N_BUCKETS - 1)
    return np.where(n < max_exact, n, large)


def _bucket_onehots():
    k = np.arange(BLOCK)[:, None]
    q = np.arange(BLOCK)[None, :]
    eye = np.eye(N_BUCKETS, dtype=np.float32)
    cur = eye[_t5_bucket_np(q - k).reshape(-1)]
    prev = eye[_t5_bucket_np(BLOCK + q - k).reshape(-1)]
    return cur, prev


SWA_K_COL = SWA_Q_HEADS * HEAD_DIM // (2 * HEAD_DIM)
SWA_V_COL = SWA_K_COL + 1


def _swa_terms(raw, bc, bp, far, sink, n):
    k = lax.broadcasted_iota(jnp.int32, (BLOCK, BLOCK), 0)
    q = lax.broadcasted_iota(jnp.int32, (BLOCK, BLOCK), 1)
    never = 2 * BLOCK
    s_c = raw[0] + bc
    s_p = raw[1] + bp
    s_m = raw[2] + jnp.where(n == 1, bp, far)
    s_c = jnp.where((k <= q) & (k >= jnp.where(n >= 1, 0, PAD_ROWS)), s_c, NEG)
    s_p = jnp.where(k > q + jnp.where(n >= 2, 0, never), s_p, NEG)
    s_m = jnp.where(k >= jnp.where(n >= 1, PAD_ROWS, never), s_m, NEG)
    m = jnp.maximum(jnp.maximum(jnp.max(s_c, axis=0, keepdims=True), jnp.max(s_p, axis=0, keepdims=True)),
                    jnp.maximum(jnp.max(s_m, axis=0, keepdims=True), sink))
    e = [jnp.exp(s_c - m), jnp.exp(s_p - m), jnp.exp(s_m - m)]
    e_s = jnp.exp(sink - m)
    l = (jnp.sum(e[0], axis=0, keepdims=True) + jnp.sum(e[1], axis=0, keepdims=True)
         + jnp.sum(e[2], axis=0, keepdims=True) + e_s)
    return e, e_s, l


SWA_STEP = 3


def _swa_specs():
    R = SWA_STEP

    def window(col):
        return ([pl.BlockSpec((BLOCK, BLOCK), lambda s, w=w: (jnp.maximum(R * s - 1 + w, 0), col)) for w in range(R + 1)]
                + [pl.BlockSpec((BLOCK, BLOCK), lambda s: (0, col))])

    qblk = pl.BlockSpec((R * BLOCK, SWA_Q_HEADS * HEAD_DIM), lambda s: (s, 0))
    bias = pl.BlockSpec((SWA_Q_HEADS, BLOCK, BLOCK), lambda s: (0, 0, 0))
    smem = pl.BlockSpec(memory_space=pltpu.SMEM)
    return qblk, window(SWA_K_COL), window(SWA_V_COL), bias, smem


def _swa_own_kv(tile_ref, kv):
    lane = lax.broadcasted_iota(jnp.int32, (BLOCK, 2 * HEAD_DIM), 1)
    t = tile_ref[...].astype(F32)
    return jnp.where(lane // HEAD_DIM == kv, t, pltpu.roll(t, HEAD_DIM, 1)).astype(BF16)


def _swa_fwd(proj, bc, bp, far, sinks, *, name):
    T = proj.shape[0]
    nb = T // BLOCK
    G = SWA_GROUP
    Hq = SWA_Q_HEADS
    lanes = 2 * HEAD_DIM

    R = SWA_STEP
    assert nb % R == 0

    def body(*refs):
        q_ref, k_refs, v_refs = refs[0], refs[1:R + 3], refs[R + 3:2 * R + 5]
        bc_ref, bp_ref, far_ref, sink_ref, o_ref = refs[2 * R + 5:]
        s = pl.program_id(0)
        lane = lax.broadcasted_iota(jnp.int32, (BLOCK, lanes), 1)
        kvs = range(SWA_KV_HEADS)
        kk = [[_swa_own_kv(ref, kv) for ref in k_refs] for kv in kvs]
        vv = [[_swa_own_kv(ref, kv) for ref in v_refs] for kv in kvs]
        chains = [(r, h) for r in range(R) for h in range(Hq)]
        tiles = lambda r: (r + 1, r, R + 1)
        q2 = {(r, pair): q_ref[r * BLOCK:(r + 1) * BLOCK, pair * lanes:(pair + 1) * lanes].astype(F32) * SCALE
              for r in range(R) for pair in range(Hq // 2)}
        qm = {c: jnp.where(lane // HEAD_DIM == c[1] % 2, q2[c[0], c[1] // 2], 0.0).astype(BF16) for c in chains}
        raw = {c: [lax.dot_general(kk[c[1] // G][w], qm[c], NT, preferred_element_type=F32) for w in tiles(c[0])]
               for c in chains}
        terms = {c: _swa_terms(raw[c], bc_ref[c[1]], bp_ref[c[1]], far_ref[c[1]], sink_ref[c[1]], R * s + c[0])
                 for c in chains}
        o_t = {c: sum(lax.dot_general(vv[c[1] // G][w], terms[c][0][b].astype(BF16), TN, preferred_element_type=F32)
                      for b, w in enumerate(tiles(c[0]))) for c in chains}
        outs = {c: (o_t[c] / terms[c][2]).T for c in chains}
        for r in range(R):
            for pair in range(Hq // 2):
                o_ref[r * BLOCK:(r + 1) * BLOCK, pair * lanes:(pair + 1) * lanes] = jnp.where(
                    lane < HEAD_DIM, outs[r, 2 * pair], outs[r, 2 * pair + 1]).astype(o_ref.dtype)

    qblk, keys, vals, bias, smem = _swa_specs()
    return pl.pallas_call(
        body, out_shape=jax.ShapeDtypeStruct((T, D_MODEL), BF16), grid=(nb // R,),
        in_specs=[qblk] + keys + vals + [bias, bias, smem, smem],
        out_specs=qblk,
        compiler_params=_params(("parallel",)), name=name,
    )(proj, *([proj] * (2 * R + 4)), bc, bp, far, sinks)


def _swa_bwd(proj, dmix, bc, bp, far, sinks, *, ex=None, name):
    T, width = proj.shape
    nb = T // BLOCK
    G = SWA_GROUP
    Hq = SWA_Q_HEADS
    lanes = 2 * HEAD_DIM
    qw = Hq * HEAD_DIM
    own_w = qw + 2 * lanes

    R = SWA_STEP
    assert nb % R == 0
    n_in = 2 * R + 10

    def body(*refs):
        q_ref, k_refs, v_refs = refs[0], refs[1:R + 3], refs[R + 3:2 * R + 5]
        do_ref, bc_ref, bp_ref, far_ref, sink_ref = refs[2 * R + 5:n_in]
        dp_ref, dbc_ref, dbp_ref, dbf_ref, dsk_ref, dk_acc, dv_acc = refs[n_in:]
        s = pl.program_id(0)

        @pl.when(s == 0)
        def _():
            for ref in (dk_acc, dv_acc, dbc_ref, dbp_ref, dbf_ref, dsk_ref):
                ref[...] = jnp.zeros(ref.shape, F32)

        lane = lax.broadcasted_iota(jnp.int32, (BLOCK, lanes), 1)
        kvs = range(SWA_KV_HEADS)
        kk = [[_swa_own_kv(ref, kv) for ref in k_refs] for kv in kvs]
        vv = [[_swa_own_kv(ref, kv) for ref in v_refs] for kv in kvs]
        chains = [(r, h) for r in range(R) for h in range(Hq)]
        blocks = range(3)
        tiles = lambda r: (r + 1, r, R + 1)
        sub = lambda ref, r, pair: ref[r * BLOCK:(r + 1) * BLOCK, pair * lanes:(pair + 1) * lanes]
        q2 = {(r, pair): sub(q_ref, r, pair).astype(F32) * SCALE for r in range(R) for pair in range(Hq // 2)}
        d2 = {(r, pair): sub(do_ref, r, pair) for r in range(R) for pair in range(Hq // 2)}
        own = [lane // HEAD_DIM == half for half in range(2)]
        qm = {c: jnp.where(own[c[1] % 2], q2[c[0], c[1] // 2], 0.0).astype(BF16) for c in chains}
        dom = {c: jnp.where(own[c[1] % 2], d2[c[0], c[1] // 2], jnp.zeros_like(d2[0, 0])) for c in chains}
        raw = {c: [lax.dot_general(kk[c[1] // G][w], qm[c], NT, preferred_element_type=F32) for w in tiles(c[0])]
               for c in chains}
        dp = {c: [lax.dot_general(vv[c[1] // G][w], dom[c], NT, preferred_element_type=F32) for w in tiles(c[0])]
              for c in chains}
        p, ds16 = {}, {}
        for c in chains:
            r, h = c
            n = R * s + r
            e, e_s, l = _swa_terms(raw[c], bc_ref[h], bp_ref[h], far_ref[h], sink_ref[h], n)
            inv = 1.0 / l
            ph = [e[b] * inv for b in blocks]
            delta = sum(jnp.sum(ph[b] * dp[c][b], axis=0, keepdims=True) for b in blocks)
            ds = [ph[b] * (dp[c][b] - delta) for b in blocks]
            dsk_ref[h] += -(e_s * inv) * delta
            dbc_ref[h] += ds[0]
            dbp_ref[h] += ds[1] + jnp.where(n == 1, ds[2], 0.0)
            dbf_ref[h] += jnp.where(n >= 2, ds[2], 0.0)
            p[c] = [x.astype(BF16) for x in ph]
            ds16[c] = [x.astype(BF16) for x in ds]
        dq_t = {c: sum(lax.dot_general(kk[c[1] // G][w], ds16[c][b], TN, preferred_element_type=F32)
                       for b, w in enumerate(tiles(c[0]))) for c in chains}
        group = [range(kv * G, (kv + 1) * G) for kv in kvs]
        dk = {(r, kv): [sum(jnp.dot(ds16[r, h][b], qm[r, h], preferred_element_type=F32) for h in group[kv])
                        for b in blocks] for r in range(R) for kv in kvs}
        dv = {(r, kv): [sum(jnp.dot(p[r, h][b], dom[r, h], preferred_element_type=F32) for h in group[kv])
                        for b in blocks] for r in range(R) for kv in kvs}
        for r in range(R):
            n = R * s + r
            rows = pl.ds(pl.multiple_of(n * BLOCK, BLOCK), BLOCK)
            prev_rows = pl.ds(pl.multiple_of(jnp.maximum(n - 1, 0) * BLOCK, BLOCK), BLOCK)
            for pair in range(Hq // 2):
                dp_ref[rows, pair * lanes:(pair + 1) * lanes] = (jnp.where(
                    lane < HEAD_DIM, dq_t[r, 2 * pair].T, dq_t[r, 2 * pair + 1].T) * SCALE).astype(dp_ref.dtype)
            for acc, ref in ((dk, dk_acc), (dv, dv_acc)):
                tot = [[a + pltpu.roll(a, HEAD_DIM, 1) for a in acc[r, kv]] for kv in kvs]
                both = [jnp.where(lane < HEAD_DIM, tot[0][b], tot[1][b]) for b in blocks]
                ref[rows, :] += both[0]
                ref[prev_rows, :] += both[1]
                ref[0:BLOCK, :] += both[2]

        @pl.when(s == nb // R - 1)
        def _():
            dp_ref[:, qw:qw + lanes] = dk_acc[...].astype(dp_ref.dtype)
            dp_ref[:, qw + lanes:own_w] = dv_acc[...].astype(dp_ref.dtype)

    qblk, keys, vals, bias, smem = _swa_specs()
    dsk = pl.BlockSpec((Hq, 1, BLOCK), lambda s: (0, 0, 0))
    grid = (nb // R,)
    body, x_in, x_in_specs, x_out, x_out_specs, x_scr = _carry(ex, grid, n_in, 5, body)
    tile = jax.ShapeDtypeStruct((Hq, BLOCK, BLOCK), F32)
    return pl.pallas_call(
        body,
        out_shape=(jax.ShapeDtypeStruct((T, width), BF16), tile, tile, tile,
                   jax.ShapeDtypeStruct((Hq, 1, BLOCK), F32), *x_out),
        grid=grid,
        in_specs=[qblk] + keys + vals + [qblk, bias, bias, smem, smem] + x_in_specs,
        out_specs=(pl.BlockSpec((T, own_w), lambda s: (0, 0)), bias, bias, bias, dsk, *x_out_specs),
        scratch_shapes=[pltpu.VMEM((T, lanes), F32), pltpu.VMEM((T, lanes), F32)] + x_scr,
        compiler_params=_params(("arbitrary",)), name=name,
    )(proj, *([proj] * (2 * R + 4)), dmix, bc, bp, far, sinks, *x_in)


def _bias_tiles(tab_t, oh_cur_t, oh_prev_t, *, name):
    Hq = tab_t.shape[0]

    def body(t_ref, oc_ref, op_ref, bc_ref, bp_ref):
        bc_ref[...] = jnp.dot(t_ref[...], oc_ref[...], precision=HIGHEST, preferred_element_type=F32)
        bp_ref[...] = jnp.dot(t_ref[...], op_ref[...], precision=HIGHEST, preferred_element_type=F32)

    vm = pl.BlockSpec(memory_space=pltpu.VMEM)
    shp = jax.ShapeDtypeStruct((Hq, BLOCK * BLOCK), F32)
    bc, bp = pl.pallas_call(body, out_shape=(shp, shp), in_specs=[vm] * 3, out_specs=(vm, vm),
                            compiler_params=_params(), name=name)(tab_t, oh_cur_t, oh_prev_t)
    return bc.reshape(Hq, BLOCK, BLOCK), bp.reshape(Hq, BLOCK, BLOCK)


def _small_grads(dbc, dbp, dbf, dsk, oh_cur, oh_prev, *, name):
    Hq = dbc.shape[0]

    def body(dbc_ref, dbp_ref, dbf_ref, dsk_ref, oc_ref, op_ref, tab_ref, sink_ref):
        tab = (jnp.dot(dbc_ref[...], oc_ref[...], precision=HIGHEST, preferred_element_type=F32)
               + jnp.dot(dbp_ref[...], op_ref[...], precision=HIGHEST, preferred_element_type=F32))
        far = jnp.sum(dbf_ref[...], axis=1, keepdims=True)
        last = lax.broadcasted_iota(jnp.int32, (Hq, N_BUCKETS), 1) == N_BUCKETS - 1
        tab_ref[...] = tab + jnp.where(last, far, 0.0)
        sink_ref[...] = jnp.sum(dsk_ref[...], axis=1, keepdims=True)

    vm = pl.BlockSpec(memory_space=pltpu.VMEM)
    return pl.pallas_call(
        body, out_shape=(jax.ShapeDtypeStruct((Hq, N_BUCKETS), F32), jax.ShapeDtypeStruct((Hq, 1), F32)),
        in_specs=[vm] * 6, out_specs=(vm, vm), compiler_params=_params(), name=name,
    )(dbc.reshape(Hq, -1), dbp.reshape(Hq, -1), dbf.reshape(Hq, -1), dsk.reshape(Hq, -1), oh_cur, oh_prev)


def _coords():
    return lax.axis_index("x"), lax.axis_index("y"), lax.axis_index("c")


class _Exchange:
    def __init__(self, inputs, out_shapes, scratch, start, finish):
        self.inputs, self.out_shapes, self.scratch, self.start, self.finish = inputs, out_shapes, scratch, start, finish


def _carry(ex, grid, n_in, n_out, body):
    if ex is None:
        return body, [], [], [], [], []
    ni, no = len(ex.inputs), len(ex.out_shapes)

    def at_step(which):
        cond = None
        for axis, n in enumerate(grid):
            c = pl.program_id(axis) == (0 if which == "first" else n - 1)
            cond = c if cond is None else cond & c
        return cond

    def wrapped(*refs):
        refs = list(refs)
        n_own_scr = len(refs) - (n_in + ni + n_out + no) - len(ex.scratch)
        own_in, side_in = refs[:n_in], refs[n_in:n_in + ni]
        own_out = refs[n_in + ni:n_in + ni + n_out]
        side_out = refs[n_in + ni + n_out:n_in + ni + n_out + no]
        rest = refs[n_in + ni + n_out + no:]
        own_scr, sems = rest[:n_own_scr], rest[n_own_scr:]

        @pl.when(at_step("first"))
        def _():
            ex.start(side_in, side_out, sems)

        body(*own_in, *own_out, *own_scr)

        @pl.when(at_step("last"))
        def _():
            ex.finish(side_in, side_out, sems)

    hbm = pl.BlockSpec(memory_space=pl.ANY)
    return wrapped, list(ex.inputs), [hbm] * ni, list(ex.out_shapes), [hbm] * no, list(ex.scratch)


def _run_exchange(ex, *, name):
    ni, no = len(ex.inputs), len(ex.out_shapes)

    def body(*refs):
        ins, outs, sems = refs[:ni], refs[ni:ni + no], refs[ni + no:]
        ex.start(ins, outs, sems)
        ex.finish(ins, outs, sems)

    hbm = pl.BlockSpec(memory_space=pl.ANY)
    return pl.pallas_call(
        body, out_shape=tuple(ex.out_shapes), in_specs=[hbm] * ni, out_specs=tuple([hbm] * no),
        scratch_shapes=ex.scratch, compiler_params=_params(), name=name)(*ex.inputs)


def _gather_exchange(shards):
    nt = len(shards)

    def copies(ins, outs, sems):
        send_sems, recv_sems, local_sems = sems
        x, y, c = _coords()
        me, sibling = (x, y, c), (x, y, 1 - c)
        chips = [(1 - x, y), (x, 1 - y), (1 - x, 1 - y)]

        def slot(t, dev):
            return outs[t].at[4 * dev[0] + 2 * dev[1] + dev[2]]

        def copy(t, k, block, to, src=None):
            dst = slot(t, block)
            return pltpu.make_async_remote_copy(
                src_ref=dst if src is None else src, dst_ref=dst,
                send_sem=send_sems.at[t, k], recv_sem=recv_sems.at[t, k], device_id=to, device_id_type=MESH)

        mine = [pltpu.make_async_copy(ins[t], slot(t, me), local_sems.at[t]) for t in range(nt)]
        first = []
        for t in range(nt):
            first.append(copy(t, 0, me, sibling, src=ins[t]))
            first += [copy(t, 1 + j, me, (*chip, c), src=ins[t]) for j, chip in enumerate(chips)]
        return copy, mine, first, me, sibling, chips, c

    def start(ins, outs, sems):
        _, mine, first, *_ = copies(ins, outs, sems)
        for cp in mine + first:
            cp.start()

    def finish(ins, outs, sems):
        copy, mine, first, me, sibling, chips, c = copies(ins, outs, sems)
        passed = []
        for j, chip in enumerate(chips):
            for t in range(nt):
                copy(t, 1 + j, (*chip, c), me).wait_recv()
                cp = copy(t, 4 + j, (*chip, c), sibling)
                cp.start()
                passed.append(cp)
        for t in range(nt):
            copy(t, 0, sibling, me).wait_recv()
            for j, chip in enumerate(chips):
                copy(t, 4 + j, (*chip, 1 - c), me).wait_recv()
        for cp in first + passed:
            cp.wait_send()
        for cp in mine:
            cp.wait()

    return _Exchange(
        list(shards), [jax.ShapeDtypeStruct((N_DEV,) + s.shape, s.dtype) for s in shards],
        [pltpu.SemaphoreType.DMA((nt, 7)), pltpu.SemaphoreType.DMA((nt, 7)), pltpu.SemaphoreType.DMA((nt,))],
        start, finish)


def _swap_exchange(arrays, n_slices, copies):
    nt = len(arrays)

    def start(ins, outs, sems):
        for cp in copies(ins, outs, sems):
            cp.start()

    def finish(ins, outs, sems):
        sends = copies(ins, outs, sems)
        for cp in sends:
            cp.wait_recv()
        for cp in sends:
            cp.wait_send()

    return _Exchange(
        list(arrays), [jax.ShapeDtypeStruct((n_slices,) + a.shape[1:], a.dtype) for a in arrays],
        [pltpu.SemaphoreType.DMA((nt, n_slices)), pltpu.SemaphoreType.DMA((nt, n_slices))], start, finish)


def _cores_exchange(gs):
    def copies(ins, outs, sems):
        send_sems, recv_sems = sems
        x, y, c = _coords()
        return [pltpu.make_async_remote_copy(
            src_ref=ins[t].at[2 * j + (1 - c)], dst_ref=outs[t].at[j],
            send_sem=send_sems.at[t, j], recv_sem=recv_sems.at[t, j], device_id=(x, y, 1 - c), device_id_type=MESH)
            for t in range(len(gs)) for j in range(4)]

    return _swap_exchange(gs, 4, copies)


def _chips_exchange(ps):
    def copies(ins, outs, sems):
        send_sems, recv_sems = sems
        x, y, c = _coords()
        peers = [(1 - x, y), (x, 1 - y), (1 - x, 1 - y)]
        return [pltpu.make_async_remote_copy(
            src_ref=ins[t].at[2 * px + py], dst_ref=outs[t].at[k],
            send_sem=send_sems.at[t, k], recv_sem=recv_sems.at[t, k], device_id=(px, py, c), device_id_type=MESH)
            for t in range(len(ps)) for k, (px, py) in enumerate(peers)]

    return _swap_exchange(ps, 3, copies)


def _add_cores(g, r, core, *, name):
    _, A, B = g.shape
    ta = _tile(A, 512, 16)

    def body(core_ref, a_ref, b_ref, o16_ref):
        o16_ref[...] = (a_ref[...] + b_ref[...]).astype(BF16)

    blk = (None, ta, B)
    return pl.pallas_call(
        body, out_shape=jax.ShapeDtypeStruct((4, A, B), BF16),
        grid_spec=pltpu.PrefetchScalarGridSpec(
            num_scalar_prefetch=1, grid=(4, A // ta),
            in_specs=[pl.BlockSpec(blk, lambda j, i, core_ref: (2 * j + core_ref[0], i, 0)),
                      pl.BlockSpec(blk, lambda j, i, core_ref: (j, i, 0))],
            out_specs=pl.BlockSpec(blk, lambda j, i, core_ref: (j, i, 0))),
        compiler_params=_params(("parallel", "parallel")), name=name)(core, g, r)


def _adamw_math(w, g, m, v):
    m = ADAM_B1 * m + (1.0 - ADAM_B1) * g
    v = ADAM_B2 * v + (1.0 - ADAM_B2) * (g * g)
    m_hat = m / (1.0 - ADAM_B1 ** ADAM_STEP)
    v_hat = v / (1.0 - ADAM_B2 ** ADAM_STEP)
    delta = -ADAM_LR * (m_hat / (jnp.sqrt(v_hat) + ADAM_EPS) + ADAM_WD * w)
    return delta, m, v


def _sum_adamw(mine, sib, r, where, w, m, v, *, ta, name):
    Aw, Bw = w.shape
    Bg = mine.shape[2]
    assert Aw % ta == 0 and Bw <= Bg and mine.shape[1] == Aw

    def body(where_ref, p_ref, s_ref, r0, r1, r2, w_ref, m_ref, v_ref, g_out, d_out, m_out, v_out):
        g = (((p_ref[:, :Bw] + s_ref[:, :Bw]) + r0[:, :Bw].astype(F32))
             + r1[:, :Bw].astype(F32)) + r2[:, :Bw].astype(F32)
        delta, m_new, v_new = _adamw_math(w_ref[...], g, m_ref[...], v_ref[...])
        g_out[...] = g
        d_out[...] = delta
        m_out[...] = m_new
        v_out[...] = v_new

    gblk = (None, ta, Bg)
    row = pl.BlockSpec((ta, Bw), lambda i, where_ref: (i, 0))
    rspecs = [pl.BlockSpec(gblk, (lambda i, where_ref, k=k: (k, i, 0))) for k in range(3)]
    shp = jax.ShapeDtypeStruct((Aw, Bw), F32)
    return pl.pallas_call(
        body, out_shape=(shp, shp, shp, shp),
        grid_spec=pltpu.PrefetchScalarGridSpec(
            num_scalar_prefetch=1, grid=(Aw // ta,),
            in_specs=[pl.BlockSpec(gblk, lambda i, where_ref: (2 * where_ref[0] + where_ref[1], i, 0)),
                      pl.BlockSpec(gblk, lambda i, where_ref: (where_ref[0], i, 0))] + rspecs + [row, row, row],
            out_specs=(row, row, row, row)),
        compiler_params=_params(("parallel",)), name=name)(where, mine, sib, r, r, r, w, m, v)


def _adamw(w, g, m, v, *, name):
    def body(w_ref, g_ref, m_ref, v_ref, d_out, m_out, v_out):
        delta, m_new, v_new = _adamw_math(w_ref[...], g_ref[...], m_ref[...], v_ref[...])
        d_out[...] = delta
        m_out[...] = m_new
        v_out[...] = v_new

    vm = pl.BlockSpec(memory_space=pltpu.VMEM)
    shp = jax.ShapeDtypeStruct(w.shape, F32)
    return pl.pallas_call(body, out_shape=(shp, shp, shp), in_specs=[vm] * 4, out_specs=(vm, vm, vm),
                          compiler_params=_params(), name=name)(w, g, m, v)


def _small_allreduce_adamw(s, w, m, v, *, name):
    R, W = s.shape

    def body(s_ref, w_ref, m_ref, v_ref, g_out, d_out, m_out, v_out, gath, send_sems, recv_sems):
        x, y, c = _coords()
        mine = 4 * x + 2 * y + c
        gath[mine] = s_ref[...]
        peers = [((1 - x) if k & 4 else x, (1 - y) if k & 2 else y, (1 - c) if k & 1 else c) for k in range(1, N_DEV)]
        sends = []
        for k in range(1, N_DEV):
            peer = peers[k - 1]
            sends.append(pltpu.make_async_remote_copy(
                src_ref=s_ref, dst_ref=gath.at[mine], send_sem=send_sems.at[k - 1], recv_sem=recv_sems.at[k - 1],
                device_id=peer, device_id_type=MESH))
        for cp in sends:
            cp.start()
        for k in range(1, N_DEV):
            peer = peers[k - 1]
            pltpu.make_async_remote_copy(
                src_ref=s_ref, dst_ref=gath.at[4 * peer[0] + 2 * peer[1] + peer[2]],
                send_sem=send_sems.at[k - 1], recv_sem=recv_sems.at[k - 1],
                device_id=peer, device_id_type=MESH).wait_recv()
        for cp in sends:
            cp.wait_send()
        g = gath[0]
        for d in range(1, N_DEV):
            g = g + gath[d]
        delta, m_new, v_new = _adamw_math(w_ref[...], g, m_ref[...], v_ref[...])
        g_out[...] = g
        d_out[...] = delta
        m_out[...] = m_new
        v_out[...] = v_new

    vm = pl.BlockSpec(memory_space=pltpu.VMEM)
    shp = jax.ShapeDtypeStruct((R, W), F32)
    return pl.pallas_call(
        body, out_shape=(shp, shp, shp, shp), in_specs=[vm] * 4, out_specs=(vm, vm, vm, vm),
        scratch_shapes=[pltpu.VMEM((N_DEV, R, W), F32), pltpu.SemaphoreType.DMA((N_DEV - 1,)),
                        pltpu.SemaphoreType.DMA((N_DEV - 1,))],
        compiler_params=_params(), name=name)(s, w, m, v)


def _pack_small(rel_bias, g1, g2, g3, g4, b_forget, sinks, extra=None, meta=None):
    misc = jnp.concatenate([rel_bias.reshape(-1), b_forget.reshape(-1), sinks.reshape(-1)])
    misc = jnp.concatenate([misc, jnp.zeros((D_MODEL - misc.shape[0],), F32)])[None]
    last = jnp.zeros((1, D_MODEL), F32) if extra is None else extra
    meta = jnp.zeros((N_META, D_MODEL), F32) if meta is None else meta
    return jnp.concatenate([g1, g2, g3, g4, misc, last, jnp.zeros((2, D_MODEL), F32), meta], axis=0)


def _unpack_small(p):
    nrb = N_BUCKETS * SWA_Q_HEADS
    misc = p[4]
    return dict(rel_bias=misc[:nrb].reshape(N_BUCKETS, SWA_Q_HEADS), ln_pre_mix=p[0:1], ln_post_mix=p[1:2],
                ln_pre_ffn=p[2:3], ln_post_ffn=p[3:4], b_forget=misc[nrb:nrb + 8].reshape(1, 8),
                sinks=misc[nrb + 8:nrb + 16].reshape(1, 8))


def _proj_runs():
    gw = FOX_GROUP * HEAD_DIM
    swa = SWA_Q_W + 2 * SWA_KV_HEADS * HEAD_DIM
    runs = [(0, swa)]
    for grp in range(FOX_HEADS // FOX_GROUP):
        runs += [(swa + part * FOX_W + grp * gw, swa + part * FOX_W + (grp + 1) * gw) for part in range(3)]
    return runs


def _columns_from_shards(gathered, runs, shard):
    pieces = []
    for start, stop in runs:
        for d in range(start // shard, (stop - 1) // shard + 1):
            lo = d * shard
            pieces.append(gathered[d][:, max(start, lo) - lo:min(stop, lo + shard) - lo])
    return jnp.concatenate(pieces, axis=1)


def _device_shards(qkv, gate, shard, padded):
    pos, segments = 0, []
    for start, stop in _proj_runs():
        segments.append((start, stop, qkv, pos))
        pos += stop - start
    segments.append((pos, pos + gate.shape[1], gate, 0))
    total = pos + gate.shape[1]
    assert total % shard == 0
    zeros = jnp.zeros((qkv.shape[0], padded - shard), qkv.dtype)
    out = []
    for d in range(total // shard):
        lo, hi = d * shard, (d + 1) * shard
        pieces = [arr[:, src + max(lo, s) - s:src + min(hi, e) - s]
                  for s, e, arr, src in sorted(segments, key=lambda seg: seg[0]) if max(lo, s) < min(hi, e)]
        out.append(jnp.concatenate(pieces + [zeros], axis=1))
    return jnp.stack(out)


def kernel(x, meta_tokens, rel_bias, ln_pre_mix, ln_post_mix, ln_pre_ffn, ln_post_ffn, w_in, b_forget, sinks, w_out, w_gate_up, w_down, loss_target, m_meta_tokens, m_rel_bias, m_ln_pre_mix, m_ln_post_mix, m_ln_pre_ffn, m_ln_post_ffn, m_w_in, m_b_forget, m_sinks, m_w_out, m_w_gate_up, m_w_down, v_meta_tokens, v_rel_bias, v_ln_pre_mix, v_ln_post_mix, v_ln_pre_ffn, v_ln_post_ffn, v_w_in, v_b_forget, v_sinks, v_w_out, v_w_gate_up, v_w_down):
    seq = x.shape[1]
    T = BLOCK + seq
    assert T % FOX_TILE == 0
    nq = T // FOX_TILE
    tm = _tile(T, 1056)
    cin = w_in.shape[2]
    hid = w_down.shape[1]
    F = N_DEV * hid
    assert w_gate_up.shape[2] == 2 * hid and cin <= W_IN_PAD and hid % 16 == 0

    x_i, y_i, c_i = _coords()
    core = jnp.reshape(c_i, (1,)).astype(jnp.int32)
    where = jnp.stack([2 * x_i + y_i, c_i]).astype(jnp.int32)
    w_in_s = jnp.pad(w_in[0].astype(BF16), ((0, 0), (0, W_IN_PAD - cin)))
    w_gu_t = w_gate_up[0].T
    x_rows, target, g_in, g_meta = _pad_rows(x[0], loss_target[0], ex=_gather_exchange([w_in_s, meta_tokens]),
                                             name="ag_w_in_pad_rows")
    gather_rest = _gather_exchange([w_out[0].astype(BF16), w_gu_t.astype(BF16), w_down[0].astype(BF16)])
    w_qkv = _columns_from_shards(g_in, _proj_runs(), cin)
    w_f = jnp.pad(_columns_from_shards(g_in, [(D_QKV, D_PROJ)], cin), ((0, 0), (0, BLOCK - FOX_HEADS)))
    meta_full = g_meta.transpose(1, 0, 2).reshape(N_META, D_MODEL)

    h0 = lax.dynamic_update_slice(x_rows, meta_full, (PAD_ROWS, 0))
    hn1, hn1_t = _rms_fwd(h0, ln_pre_mix, name="rms_pre_mix")
    proj = _matmul(hn1, w_qkv, out_dtype=BF16, tm=tm, tn=D_QKV, name="mm_in_proj")
    proj_f = _matmul(hn1, w_f, out_dtype=F32, tm=tm, tn=BLOCK, name="mm_in_proj_f")

    f_t = proj_f[:, :FOX_HEADS].T
    bf_col = b_forget.reshape(FOX_HEADS, 1)

    oh_cur, oh_prev = _bucket_onehots()
    bias_c, bias_p = _bias_tiles(rel_bias.T, jnp.asarray(oh_cur.T), jnp.asarray(oh_prev.T), name="bias_tiles")
    far = rel_bias[N_BUCKETS - 1]
    sink_v = sinks[0]
    mix_a = _swa_fwd(proj, bias_c, bias_p, far, sink_v, name="swa_fwd")

    cum_col = _fox_gates_fwd(f_t, bf_col, name="fox_gates_fwd")
    q_b, k_b, v_b = _fox_prep(proj, cum_col, name="fox_prep")
    mix, lse_row, g_out, g_gu, g_down = _fox_fwd(q_b, k_b, v_b, mix_a, ex=gather_rest, name="fox_fwd")
    w_out_full = g_out.reshape(D_MODEL, D_MODEL)
    w_gu_full_t = g_gu.reshape(2 * F, D_MODEL)
    w_down_full = g_down.reshape(F, D_MODEL)

    a1 = _matmul(mix, w_out_full, out_dtype=F32, tm=tm, tn=D_MODEL, name="mm_out_proj")
    h1, hn2 = _post_res_norm(a1, ln_post_mix, h0, ln_pre_ffn, name="post_mix_pre_ffn")
    gate, up, act, act_t = _gate_up_swiglu(hn2, w_gu_full_t, name="mm_gate_up")
    ff = _matmul(act, w_down_full, out_dtype=F32, tm=tm, tn=512, name="mm_down")
    dh2, dff, dg_post_ffn, loss_acc = _loss_head(ff, ln_post_ffn, h1, target, name="loss_head")

    dgu = _d_act_swiglu(dff, w_down_full, gate, up, name="mm_d_act")
    d_w_down = _matmul(act_t, dff, out_dtype=F32, tm=_tile(F, 768), tn=512, name="mm_dw_down")
    dhn2 = _matmul(dgu, w_gu_full_t, out_dtype=F32, tm=tm, tn=512, tk=F, name="mm_d_hn2")
    d_w_gu_t = _matmul(dgu, hn2, ta=True, out_dtype=F32, tm=256, tn=D_MODEL, name="mm_dw_gate_up")
    dh1, dg_pre_ffn, da1, dg_post_mix = _rms_bwd(h1, ln_pre_ffn, dhn2, dh2, out_dtype=F32,
                                                 then=(a1, ln_post_mix), name="rms_bwd_pre_ffn_post_mix")
    dmix = _matmul(da1, w_out_full, nt=True, out_dtype=BF16, tm=tm, tn=D_MODEL, name="mm_d_mix")
    d_w_out = _matmul(mix, da1, ta=True, out_dtype=F32, tm=512, tn=D_MODEL, name="mm_dw_out")

    ffn_grads = [g.reshape(N_DEV, -1, D_MODEL) for g in (d_w_out, d_w_gu_t, d_w_down)]
    dproj_a, dbc, dbp, dbf, dsk, *ffn_sibling = _swa_bwd(
        proj, dmix, bias_c, bias_p, far, sink_v, ex=_cores_exchange(ffn_grads), name="swa_bwd")
    d_tab, d_sink = _small_grads(dbc, dbp, dbf, dsk, jnp.asarray(oh_cur), jnp.asarray(oh_prev), name="small_grads")
    ffn_sums = [_add_cores(g, r, core, name="rs_add_" + t)
                for g, r, t in zip(ffn_grads, ffn_sibling, ["w_out", "w_gate_up", "w_down"])]

    do_b = _fox_prep_bwd(dmix, mix, name="fox_prep_bwd")
    dproj, dcq, dck, *ffn_chips = _fox_bwd(
        q_b, k_b, v_b, do_b, lse_row, dproj_a, ex=_chips_exchange(ffn_sums), name="fox_bwd")
    df_t, d_bf = _fox_gates_bwd(dcq.reshape(FOX_HEADS, T), dck.reshape(FOX_HEADS, T), f_t, bf_col,
                                name="fox_gates_bwd")
    df = jnp.pad(df_t.T.astype(BF16), ((0, 0), (0, BLOCK - FOX_HEADS)))

    d_w_qkv = _matmul(hn1_t, dproj, out_dtype=F32, tm=512, tn=768, name="mm_dw_in")
    d_w_f = _matmul(hn1_t, df, out_dtype=F32, tm=512, tn=BLOCK, name="mm_dw_in_f")
    d_w_in = _device_shards(d_w_qkv, d_w_f[:, :FOX_HEADS], cin, W_IN_PAD)
    dhn1, in_sibling = _matmul(dproj, w_qkv, nt=True, out_dtype=F32, tm=tm, tn=512,
                               ex=_cores_exchange([d_w_in]), name="mm_d_hn1")
    in_sum = _add_cores(d_w_in, in_sibling, core, name="rs_add_w_in")
    dh0, dg_pre_mix, in_chips = _rms_bwd(h0, ln_pre_mix, dhn1, dh1, out_dtype=F32, dy2=(df, w_f),
                                         ex=_chips_exchange([in_sum]), name="rms_bwd_pre_mix")
    grad_x = dh0[BLOCK:][None]
    d_meta = dh0[PAD_ROWS:BLOCK]

    rs_out, rs_gu, rs_down = zip(ffn_grads, ffn_sibling, ffn_chips)
    updates = [("w_in", (d_w_in, in_sibling, in_chips), (w_in[0], m_w_in[0], v_w_in[0]), 256),
               ("w_out", rs_out, (w_out[0], m_w_out[0], v_w_out[0]), BLOCK),
               ("w_gate_up", rs_gu, (w_gu_t, m_w_gate_up[0].T, v_w_gate_up[0].T), hid),
               ("w_down", rs_down, (w_down[0], m_w_down[0], v_w_down[0]), hid)]
    big = [{}, {}, {}, {}]
    for t, grads, shard, ta in updates:
        res = _sum_adamw(*grads, where, *shard, ta=ta, name="rs_adamw_" + t)
        for kind in range(4):
            big[kind][t] = (res[kind].T if t == "w_gate_up" else res[kind])[None]

    loss_row = jnp.pad(loss_acc[0:1, 0:1] * (0.5 / D_MODEL), ((0, 0), (0, D_MODEL - 1)))
    s_small = _pack_small(d_tab.T, dg_pre_mix, dg_post_mix, dg_pre_ffn, dg_post_ffn, d_bf, d_sink,
                          extra=loss_row, meta=d_meta)
    w_s = _pack_small(rel_bias, ln_pre_mix, ln_post_mix, ln_pre_ffn, ln_post_ffn, b_forget, sinks)
    m_s = _pack_small(m_rel_bias, m_ln_pre_mix, m_ln_post_mix, m_ln_pre_ffn, m_ln_post_ffn, m_b_forget, m_sinks)
    v_s = _pack_small(v_rel_bias, v_ln_pre_mix, v_ln_post_mix, v_ln_pre_ffn, v_ln_post_ffn, v_b_forget, v_sinks)
    small = _small_allreduce_adamw(s_small, w_s, m_s, v_s, name="small_allreduce_adamw")
    loss = small[0][5, 0]
    mcols = meta_tokens.shape[1]
    g_meta_mine = lax.dynamic_slice(small[0][8:8 + N_META], (0, (4 * x_i + 2 * y_i + c_i) * mcols), (N_META, mcols))
    big[0]["meta_tokens"] = g_meta_mine
    for kind, arr in enumerate(_adamw(meta_tokens, g_meta_mine, m_meta_tokens, v_meta_tokens, name="adamw_meta")):
        big[kind + 1]["meta_tokens"] = arr
    small = [_unpack_small(p) for p in small]

    names = ["meta_tokens", "rel_bias", "ln_pre_mix", "ln_post_mix", "ln_pre_ffn", "ln_post_ffn", "w_in",
             "b_forget", "sinks", "w_out", "w_gate_up", "w_down"]
    outs = [loss, grad_x]
    for kind in range(4):
        for nme in names:
            outs.append(big[kind][nme] if nme in big[kind] else small[kind][nme])
    return tuple(outs)
```

```python
import math

import numpy as np
import jax
import jax.numpy as jnp
from jax import lax
from jax.experimental import pallas as pl
from jax.experimental.pallas import tpu as pltpu

F32 = jnp.float32
BF16 = jnp.bfloat16
HIGHEST = lax.Precision.HIGHEST
MESH = pl.DeviceIdType.MESH

N_DEV = 8
D_MODEL = 1024
N_META = 16
HEAD_DIM = 64
SWA_Q_HEADS = 8
SWA_KV_HEADS = 2
SWA_GROUP = 4
FOX_HEADS = 8
FOX_W = FOX_HEADS * HEAD_DIM
SWA_Q_W = SWA_Q_HEADS * HEAD_DIM
BLOCK = 128
PAD_ROWS = BLOCK - N_META
N_BUCKETS = 32
MAX_DISTANCE = 128
D_FF = 2816
D_QKV = 2304
D_PROJ = D_QKV + FOX_HEADS
D_PROJ_PAD = 2560
EPS = 1e-6
NEG = -1e30
SCALE = HEAD_DIM ** -0.5
ADAM_LR, ADAM_B1, ADAM_B2, ADAM_EPS, ADAM_WD, ADAM_STEP = 0.001, 0.9, 0.999, 1e-08, 0.01, 10
VMEM_LIMIT = 56 * 1024 * 1024
FOX_TILE = 384
FOX_GROUP = 4
W_IN_PAD = 384

NT = (((1,), (1,)), ((), ()))
NN = (((1,), (0,)), ((), ()))
TN = (((0,), (0,)), ((), ()))


def _params(sem=None, **kw):
    if sem is not None:
        kw["dimension_semantics"] = sem
    return pltpu.CompilerParams(vmem_limit_bytes=VMEM_LIMIT, **kw)


def _tile(n, target, mult=16):
    best = None
    for t in range(mult, min(n, target) + 1, mult):
        if n % t == 0:
            best = t
    assert best is not None, (n, target)
    return best


def _matmul(a, b, *, nt=False, ta=False, out_dtype, tm, tn, tk=None, ex=None, name):
    M, K = a.shape[::-1] if ta else a.shape
    assert not (ta and nt)
    N = b.shape[0] if nt else b.shape[1]
    tk = K if tk is None else tk
    assert M % tm == 0 and N % tn == 0 and K % tk == 0, (name, a.shape, b.shape, tm, tn, tk)
    nk = K // tk
    dn = NT if nt else (TN if ta else NN)
    a_spec = pl.BlockSpec((tk, tm), lambda i, j, k: (k, i)) if ta else pl.BlockSpec((tm, tk), lambda i, j, k: (i, k))

    def body(a_ref, b_ref, o_ref, *scr):
        part = lax.dot_general(a_ref[...], b_ref[...], dn, preferred_element_type=F32)
        if nk == 1:
            o_ref[...] = part.astype(o_ref.dtype)
        else:
            acc = scr[0]
            k = pl.program_id(2)

            @pl.when(k == 0)
            def _():
                acc[...] = part

            @pl.when(k > 0)
            def _():
                acc[...] += part

            @pl.when(k == nk - 1)
            def _():
                o_ref[...] = acc[...].astype(o_ref.dtype)

    if nt:
        b_spec = pl.BlockSpec((tn, tk), lambda i, j, k: (j, k))
    else:
        b_spec = pl.BlockSpec((tk, tn), lambda i, j, k: (k, j))
    out_shape = jax.ShapeDtypeStruct((M, N), out_dtype)
    out_spec = pl.BlockSpec((tm, tn), lambda i, j, k: (i, j))
    grid = (M // tm, N // tn, nk)
    body, x_in, x_in_specs, x_out, x_out_specs, x_scr = _carry(ex, grid, 2, 1, body)
    res = pl.pallas_call(
        body,
        out_shape=(out_shape, *x_out),
        grid=grid,
        in_specs=[a_spec, b_spec] + x_in_specs,
        out_specs=(out_spec, *x_out_specs),
        scratch_shapes=([pltpu.VMEM((tm, tn), F32)] if nk > 1 else []) + x_scr,
        compiler_params=_params(("parallel", "parallel", "arbitrary") if ex is None else ("arbitrary",) * 3),
        name=name,
    )(a, b, *x_in)
    return res[0] if ex is None else res


def _rstd(x):
    return lax.rsqrt(jnp.mean(x * x, axis=-1, keepdims=True) + EPS)


def _pad_rows_rms(x, target, g, ex, *, name):
    S, D = x.shape
    nb = S // BLOCK + 1
    ni, no = len(ex.inputs), len(ex.out_shapes)
    mcols = D // N_DEV

    def body(x_ref, t_ref, g_ref, *rest):
        side_in, (h_ref, to_ref, y_ref, yt_ref) = rest[:ni], rest[ni:ni + 4]
        side_out = rest[ni + 4:ni + 4 + no]
        meta_buf, meta_sems, *sems = rest[ni + 4 + no:]
        i = pl.program_id(0)

        @pl.when(i == 0)
        def _():
            ex.start(side_in, side_out, sems)

        def norm():
            h = h_ref[...]
            y = h * _rstd(h) * g_ref[...]
            y_ref[...] = y.astype(y_ref.dtype)
            yt_ref[...] = y.T.astype(yt_ref.dtype)

        @pl.when(i < nb - 1)
        def _():
            h_ref[...] = x_ref[...]
            to_ref[...] = t_ref[...]
            norm()

        @pl.when(i == nb - 1)
        def _():
            ex.finish(side_in, side_out, sems)
            copies = [pltpu.make_async_copy(side_out[-1].at[d], meta_buf.at[:, d * mcols:(d + 1) * mcols],
                                            meta_sems.at[d]) for d in range(N_DEV)]
            for cp in copies:
                cp.start()
            for cp in copies:
                cp.wait()
            h_ref[:PAD_ROWS, :] = jnp.zeros((PAD_ROWS, D), F32)
            h_ref[PAD_ROWS:, :] = meta_buf[...]
            to_ref[...] = jnp.zeros_like(to_ref)
            norm()

    src = pl.BlockSpec((BLOCK, D), lambda i: (jnp.minimum(i, nb - 2), 0))
    dst = pl.BlockSpec((BLOCK, D), lambda i: ((i + 1) % nb, 0))
    hbm = pl.BlockSpec(memory_space=pl.ANY)
    rows = jax.ShapeDtypeStruct((BLOCK + S, D), F32)
    return pl.pallas_call(
        body,
        out_shape=(rows, rows, jax.ShapeDtypeStruct((BLOCK + S, D), BF16), jax.ShapeDtypeStruct((D, BLOCK + S), BF16),
                   *ex.out_shapes),
        grid=(nb,),
        in_specs=[src, src, pl.BlockSpec((1, D), lambda i: (0, 0))] + [hbm] * ni,
        out_specs=(dst, dst, dst, pl.BlockSpec((D, BLOCK), lambda i: (0, (i + 1) % nb)), *([hbm] * no)),
        scratch_shapes=[pltpu.VMEM((N_META, D), F32), pltpu.SemaphoreType.DMA((N_DEV,))] + list(ex.scratch),
        compiler_params=_params(("arbitrary",)), name=name)(x, target, g, *ex.inputs)


def _post_res_norm(a, g_post, h, g_pre, *, name):
    T, D = a.shape
    tm = _tile(T, 384, BLOCK)

    def body(a_ref, gp_ref, h_ref, gn_ref, h1_ref, o_ref):
        a = a_ref[...]
        h1 = h_ref[...] + a * _rstd(a) * gp_ref[...]
        h1_ref[...] = h1
        o_ref[...] = (h1 * _rstd(h1) * gn_ref[...]).astype(o_ref.dtype)

    row = pl.BlockSpec((tm, D), lambda i: (i, 0))
    vec = pl.BlockSpec((1, D), lambda i: (0, 0))
    return pl.pallas_call(
        body, out_shape=(jax.ShapeDtypeStruct((T, D), F32), jax.ShapeDtypeStruct((T, D), BF16)), grid=(T // tm,),
        in_specs=[row, vec, row, vec], out_specs=(row, row),
        compiler_params=_params(("parallel",)), name=name)(a, g_post, h, g_pre)


def _loss_head(a, g, h, target, *, name):
    T, D = a.shape
    tm = _tile(T, 512)

    def body(a_ref, g_ref, h_ref, t_ref, dy_ref, da_ref, dg_ref, loss_ref):
        i = pl.program_id(0)
        a = a_ref[...]
        r = _rstd(a)
        ah = a * r
        y = h_ref[...] + ah * g_ref[...]
        rows = i * tm + lax.broadcasted_iota(jnp.int32, (tm, 1), 0)
        err = jnp.where(rows >= BLOCK, y - t_ref[...], 0.0)
        dy = err / D
        dy_ref[...] = dy
        dah = dy * g_ref[...]
        da_ref[...] = (r * (dah - ah * jnp.mean(dah * ah, axis=-1, keepdims=True))).astype(da_ref.dtype)
        part = jnp.sum(jnp.sum(err * err, axis=1, keepdims=True), axis=0, keepdims=True)

        @pl.when(i == 0)
        def _():
            loss_ref[...] = jnp.zeros_like(loss_ref)
            dg_ref[...] = jnp.zeros_like(dg_ref)

        loss_ref[...] += jnp.broadcast_to(part, loss_ref.shape)
        dg_ref[...] += jnp.sum(dy * ah, axis=0, keepdims=True)

    row = pl.BlockSpec((tm, D), lambda i: (i, 0))
    vec = pl.BlockSpec((1, D), lambda i: (0, 0))
    return pl.pallas_call(
        body, out_shape=(jax.ShapeDtypeStruct((T, D), F32), jax.ShapeDtypeStruct((T, D), BF16),
                         jax.ShapeDtypeStruct((1, D), F32), jax.ShapeDtypeStruct((8, 128), F32)),
        grid=(T // tm,),
        in_specs=[row, vec, row, row],
        out_specs=(row, row, vec, pl.BlockSpec((8, 128), lambda i: (0, 0))),
        compiler_params=_params(("arbitrary",)), name=name)(a, g, h, target)


def _rms_bwd(x, g, dy, res, *, out_dtype, dy2=None, then=None, split_head=False, ex=None, name):
    T, D = x.shape
    tm = BLOCK if split_head else _tile(T, 512)
    assert not (split_head and then is not None)
    has_res = res is not None
    has_dy2 = 2 if dy2 is not None else 0
    n_in = 3 + has_dy2 + has_res + (2 if then is not None else 0)
    n_out = 2 + (2 if then is not None else 0) + split_head

    def pull_back(x, g, dy):
        r = _rstd(x)
        xh = x * r
        dxh = dy * g
        return r * (dxh - xh * jnp.mean(dxh * xh, axis=-1, keepdims=True)), jnp.sum(dy * xh, axis=0, keepdims=True)

    def body(*refs):
        ins, outs = refs[:n_in], refs[n_in:]
        i = pl.program_id(0)

        @pl.when(i == 0)
        def _():
            for ref in outs[1::2]:
                ref[...] = jnp.zeros_like(ref)

        dy_all = ins[2][...].astype(F32)
        if has_dy2:
            dy_all = dy_all + lax.dot_general(ins[3][...], ins[4][...], NT, preferred_element_type=F32)
        dx, dg = pull_back(ins[0][...], ins[1][...], dy_all)
        if has_res:
            dx = dx + ins[3 + has_dy2][...]
        if split_head:
            @pl.when(i == 0)
            def _():
                outs[2][...] = dx.astype(outs[2].dtype)

            @pl.when(i > 0)
            def _():
                outs[0][...] = dx.astype(outs[0].dtype)
        else:
            outs[0][...] = dx.astype(outs[0].dtype)
        outs[1][...] += dg
        if then is not None:
            dx2, dg2 = pull_back(ins[n_in - 2][...], ins[n_in - 1][...], dx)
            outs[2][...] = dx2.astype(outs[2].dtype)
            outs[3][...] += dg2

    row = pl.BlockSpec((tm, D), lambda i: (i, 0))
    vec = pl.BlockSpec((1, D), lambda i: (0, 0))
    ins = [x, g, dy] + (list(dy2) if has_dy2 else []) + ([res] if has_res else []) + (list(then) if then is not None else [])
    dy2_specs = ([pl.BlockSpec((tm, dy2[0].shape[1]), lambda i: (i, 0)), pl.BlockSpec(dy2[1].shape, lambda i: (0, 0))]
                 if has_dy2 else [])
    in_specs = [row, vec, row] + dy2_specs + ([row] if has_res else []) + ([row, vec] if then is not None else [])
    out_shape = [jax.ShapeDtypeStruct((T, D), out_dtype), jax.ShapeDtypeStruct((1, D), F32)]
    out_specs = [row, vec]
    if then is not None:
        out_shape += [jax.ShapeDtypeStruct((T, D), BF16), jax.ShapeDtypeStruct((1, D), F32)]
        out_specs += [row, vec]
    if split_head:
        out_shape[0] = jax.ShapeDtypeStruct((T - BLOCK, D), out_dtype)
        out_specs[0] = pl.BlockSpec((BLOCK, D), lambda i: (jnp.maximum(i - 1, 0), 0))
        out_shape.append(jax.ShapeDtypeStruct((BLOCK, D), out_dtype))
        out_specs.append(pl.BlockSpec((BLOCK, D), lambda i: (0, 0)))
    grid = (T // tm,)
    body, x_in, x_in_specs, x_out, x_out_specs, x_scr = _carry(ex, grid, n_in, n_out, body)
    return pl.pallas_call(
        body, out_shape=(*out_shape, *x_out), grid=grid,
        in_specs=in_specs + x_in_specs, out_specs=(*out_specs, *x_out_specs), scratch_shapes=x_scr,
        compiler_params=_params(("arbitrary",)), name=name)(*ins, *x_in)


def _gate_up_swiglu(a, w_t, *, name):
    T, D = a.shape
    F = w_t.shape[0] // 2
    tm = _tile(T, 1408, BLOCK)
    n = _tile(F, 256, BLOCK)

    def body(a_ref, wg_ref, wu_ref, g_ref, u_ref, o_ref, ot_ref):
        x = a_ref[...]
        g = lax.dot_general(x, wg_ref[...], NT, preferred_element_type=F32)
        u = lax.dot_general(x, wu_ref[...], NT, preferred_element_type=F32)
        g16, u16 = g.astype(BF16), u.astype(BF16)
        g_ref[...] = g16
        u_ref[...] = u16
        gr = g16.astype(F32)
        act = gr / (1.0 + jnp.exp(-gr)) * u16.astype(F32)
        o_ref[...] = act.astype(o_ref.dtype)
        ot_ref[...] = act.T.astype(ot_ref.dtype)

    tile = pl.BlockSpec((tm, n), lambda i, j: (i, j))
    shp = jax.ShapeDtypeStruct((T, F), BF16)
    return pl.pallas_call(
        body, out_shape=(shp, shp, shp, jax.ShapeDtypeStruct((F, T), BF16)), grid=(T // tm, F // n),
        in_specs=[pl.BlockSpec((tm, D), lambda i, j: (i, 0)),
                  pl.BlockSpec((n, D), lambda i, j: (j, 0)),
                  pl.BlockSpec((n, D), lambda i, j: (j + F // n, 0))],
        out_specs=(tile, tile, tile, pl.BlockSpec((n, tm), lambda i, j: (j, i))),
        compiler_params=_params(("parallel", "parallel")), name=name)(a, w_t, w_t)


def _d_act_swiglu(dff, w_down, gate, up, *, name):
    T, D = dff.shape
    F = w_down.shape[0]
    tm = _tile(T, 384)
    chunk = 768
    assert F % BLOCK == 0

    def body(d_ref, w_ref, g_ref, u_ref, o_ref):
        dy = d_ref[...]
        for c in range(0, F, chunk):
            e = min(c + chunk, F)
            d = lax.dot_general(dy, w_ref[c:e, :], NT, preferred_element_type=F32)
            g = g_ref[:, c:e].astype(F32)
            u = u_ref[:, c:e].astype(F32)
            sg = 1.0 / (1.0 + jnp.exp(-g))
            o_ref[:, c:e] = (d * u * (sg * (1.0 + g * (1.0 - sg)))).astype(o_ref.dtype)
            o_ref[:, F + c:F + e] = (d * (g * sg)).astype(o_ref.dtype)

    row = pl.BlockSpec((tm, F), lambda i: (i, 0))
    return pl.pallas_call(
        body, out_shape=jax.ShapeDtypeStruct((T, 2 * F), BF16), grid=(T // tm,),
        in_specs=[pl.BlockSpec((tm, D), lambda i: (i, 0)), pl.BlockSpec((F, D), lambda i: (0, 0)), row, row],
        out_specs=pl.BlockSpec((tm, 2 * F), lambda i: (i, 0)),
        compiler_params=_params(("parallel",)), name=name)(dff, w_down, gate, up)


def _fox_gates_fwd(f_t, b, *, name):
    H, T = f_t.shape
    nb = T // BLOCK

    def body(f_ref, b_ref, col_ref):
        f = f_ref[...] + b_ref[...]
        ls = jnp.minimum(f, 0.0) - jnp.log(1.0 + jnp.exp(-jnp.abs(f)))
        t = lax.broadcasted_iota(jnp.int32, (H, T), 1)
        ls = jnp.where(t >= PAD_ROWS, ls, 0.0)
        upper = (lax.broadcasted_iota(jnp.int32, (BLOCK, BLOCK), 0)
                 <= lax.broadcasted_iota(jnp.int32, (BLOCK, BLOCK), 1)).astype(F32)
        carry = jnp.zeros((H, 1), F32)
        for blk in range(nb):
            seg = ls[:, blk * BLOCK:(blk + 1) * BLOCK]
            pre = jnp.dot(seg, upper, precision=HIGHEST, preferred_element_type=F32) + carry
            key_gate = jnp.where(t[:, blk * BLOCK:(blk + 1) * BLOCK] >= PAD_ROWS, pre, -NEG)
            terms = list(_split3(pre)) + list(_split3(key_gate))
            col_ref[blk * BLOCK:(blk + 1) * BLOCK, :] = jnp.concatenate(
                terms + [jnp.zeros((BLOCK - len(terms) * H, BLOCK), F32)], axis=0).T.astype(col_ref.dtype)
            carry = pre[:, BLOCK - 1:BLOCK]

    vm = pl.BlockSpec(memory_space=pltpu.VMEM)
    return pl.pallas_call(
        body, out_shape=jax.ShapeDtypeStruct((T, BLOCK), BF16),
        in_specs=[vm, vm], out_specs=vm,
        compiler_params=_params(), name=name)(f_t, b)


def _fox_gates_bwd(dcq, dck, f_t, b, *, name):
    H, T = f_t.shape
    nb = T // BLOCK

    def body(dq_ref, d_ref, f_ref, b_ref, df_ref, db_ref):
        lower = (lax.broadcasted_iota(jnp.int32, (BLOCK, BLOCK), 0)
                 >= lax.broadcasted_iota(jnp.int32, (BLOCK, BLOCK), 1)).astype(F32)
        carry = jnp.zeros((H, 1), F32)
        for blk in range(nb - 1, -1, -1):
            seg = dq_ref[:, blk * BLOCK:(blk + 1) * BLOCK] - d_ref[:, blk * BLOCK:(blk + 1) * BLOCK]
            suf = jnp.dot(seg, lower, precision=HIGHEST, preferred_element_type=F32) + carry
            df_ref[:, blk * BLOCK:(blk + 1) * BLOCK] = suf
            carry = suf[:, 0:1]
        f = f_ref[...] + b_ref[...]
        t = lax.broadcasted_iota(jnp.int32, (H, T), 1)
        df = jnp.where(t >= PAD_ROWS, df_ref[...] / (1.0 + jnp.exp(f)), 0.0)
        df_ref[...] = df
        db_ref[...] = jnp.sum(df, axis=1, keepdims=True)

    vm = pl.BlockSpec(memory_space=pltpu.VMEM)
    return pl.pallas_call(
        body, out_shape=(jax.ShapeDtypeStruct((H, T), F32), jax.ShapeDtypeStruct((H, 1), F32)),
        in_specs=[vm, vm, vm, vm], out_specs=(vm, vm),
        compiler_params=_params(), name=name)(dcq, dck, f_t, b)


def _fox_lanes(parity):
    base = HEAD_DIM * (1 - parity)
    return base, base + 3


def _split3(c):
    hi = c.astype(BF16).astype(F32)
    r = c - hi
    mid = r.astype(BF16).astype(F32)
    lo = (r - mid).astype(BF16).astype(F32)
    return hi, mid, lo


def _lanes(lane, parity, data, start, terms, ones_at=None, fill=1.0):
    out = jnp.zeros((), F32) if ones_at is None else jnp.where((lane >= ones_at) & (lane < ones_at + 3), fill, 0.0)
    for i, t in enumerate(terms):
        out = jnp.where(lane == start + i, t, out)
    return jnp.where(lane // HEAD_DIM == parity, data, out)


def _fox_prep(proj, cum_col, *, name):
    T = proj.shape[0]
    tm = _tile(T, 1408, BLOCK)
    nt = T // tm
    H = FOX_HEADS
    lanes = 2 * HEAD_DIM
    first = (proj.shape[1] - 3 * H * HEAD_DIM) // lanes

    def body(q_ref, k_ref, v_ref, c_ref, qa_ref, ka_ref, va_ref):
        p = pl.program_id(0)
        i = pl.program_id(1)
        lane = lax.broadcasted_iota(jnp.int32, (1, lanes), 1)
        src = lax.broadcasted_iota(jnp.int32, (lanes, lanes), 0)
        dst = lax.broadcasted_iota(jnp.int32, (lanes, lanes), 1)
        q2 = q_ref[...].astype(F32) * SCALE
        k2 = k_ref[...].astype(F32)
        v2 = v_ref[...].astype(F32)
        gates = c_ref[...]
        def placed(h, first_term, start):
            pick = ((src % FOX_HEADS == h) & (src // FOX_HEADS - first_term == dst - start)
                    & (dst >= start) & (dst < start + 3))
            return jnp.dot(gates, pick.astype(BF16), preferred_element_type=F32)

        moved = [(placed(2 * p + e, 0, _fox_lanes(e)[1]), placed(2 * p + e, 3, _fox_lanes(e)[0])) for e in range(2)]
        for e in range(2):
            kc, qc = _fox_lanes(e)
            own = lane // HEAD_DIM == e
            minus = jnp.where((lane >= kc) & (lane < kc + 3), -1.0, 0.0)
            ones_q = jnp.where((lane >= qc) & (lane < qc + 3), 1.0, 0.0)
            ones_k = jnp.where((lane >= kc) & (lane < kc + 3), 1.0, 0.0)
            qa_ref[e] = jnp.where(own, q2, moved[e][0] + minus).astype(BF16)
            ka_ref[e] = jnp.where(own, k2, moved[e][1] + ones_q).astype(BF16)
            va_ref[e] = jnp.where(own, v2, ones_k).astype(BF16)

    pairs = FOX_GROUP // 2

    def col(part):
        return pl.BlockSpec((tm, lanes),
                            lambda p, i: (i, first + 3 * pairs * (p // pairs) + part * pairs + p % pairs))

    out = pl.BlockSpec((2, tm, lanes), lambda p, i: (p, i, 0))
    shp = jax.ShapeDtypeStruct((H, T, lanes), BF16)
    return pl.pallas_call(
        body, out_shape=(shp, shp, shp), grid=(H // 2, nt),
        in_specs=[col(0), col(1), col(2), pl.BlockSpec((tm, lanes), lambda p, i: (i, 0))],
        out_specs=(out, out, out),
        compiler_params=_params(("parallel", "parallel")), name=name)(proj, proj, proj, cum_col)


def _fox_fwd(q_aug, k_aug, v_aug, mix, *, ex=None, name):
    H, T, lanes = q_aug.shape
    tq = FOX_TILE
    nq = T // tq
    G = FOX_HEADS

    def body(q_ref, k_ref, v_ref, mix_ref, o_ref, lse_ref, m_scr, acc_scr):
        i = pl.program_id(1)
        m_scr[...] = jnp.full(m_scr.shape, NEG, F32)
        acc_scr[...] = jnp.zeros(acc_scr.shape, F32)

        def step(kb, diag):
            off = pl.multiple_of(kb * tq, tq)
            s_t = [lax.dot_general(k_ref[g, pl.ds(off, tq), :], q_ref[g], NT, preferred_element_type=F32)
                   for g in range(G)]
            if diag:
                r = lax.broadcasted_iota(jnp.int32, (tq, tq), 0)
                c = lax.broadcasted_iota(jnp.int32, (tq, tq), 1)
                s_t = [jnp.where(c >= r, s, NEG) for s in s_t]
            m_prev = [m_scr[g] for g in range(G)]
            m_new = [jnp.maximum(m_prev[g], jnp.max(s_t[g], axis=0, keepdims=True)) for g in range(G)]
            p_t = [jnp.exp(s_t[g] - m_new[g]).astype(BF16) for g in range(G)]
            pv = [lax.dot_general(v_ref[g, pl.ds(off, tq), :], p_t[g], TN, preferred_element_type=F32)
                  for g in range(G)]
            for g in range(G):
                acc_scr[g] = jnp.exp(m_prev[g] - m_new[g]) * acc_scr[g] + pv[g]
                m_scr[g] = m_new[g]

        def loop_body(kb, carry):
            step(kb, False)
            return carry

        lax.fori_loop(0, i, loop_body, 0)
        step(i, True)
        lane = lax.broadcasted_iota(jnp.int32, (tq, lanes), 1)
        outs = []
        for g in range(G):
            ones = _fox_lanes(g % 2)[0]
            acc = acc_scr[g]
            lse_ref[g] = m_scr[g] + jnp.log(acc[ones:ones + 1, :])
            acc_t = acc.T
            outs.append(acc_t / acc_t[:, ones:ones + 1])
        for pair in range(G // 2):
            o_ref[:, pair * lanes:(pair + 1) * lanes] = jnp.where(
                lane < HEAD_DIM, outs[2 * pair], outs[2 * pair + 1]).astype(o_ref.dtype)

    blk = pl.BlockSpec((G, tq, lanes), lambda h, i: (h, i, 0))
    full = pl.BlockSpec((G, T, lanes), lambda h, i: (h, 0, 0))
    grid = (H // G, nq)
    first = mix.shape[1] // (G * HEAD_DIM) - H // G
    body, x_in, x_in_specs, x_out, x_out_specs, x_scr = _carry(ex, grid, 4, 2, body)
    return pl.pallas_call(
        body,
        out_shape=(jax.ShapeDtypeStruct(mix.shape, mix.dtype), jax.ShapeDtypeStruct((H, nq, 1, tq), F32), *x_out),
        grid=grid,
        in_specs=[blk, full, full, pl.BlockSpec(memory_space=pl.ANY)] + x_in_specs,
        out_specs=(pl.BlockSpec((tq, G * HEAD_DIM), lambda h, i: (i, first + h)),
                   pl.BlockSpec((G, None, 1, tq), lambda h, i: (h, i, 0, 0)), *x_out_specs),
        input_output_aliases={3: 0},
        scratch_shapes=[pltpu.VMEM((G, 1, tq), F32), pltpu.VMEM((G, lanes, tq), F32)] + x_scr,
        compiler_params=_params(("arbitrary", "arbitrary")), name=name)(q_aug, k_aug, v_aug, mix, *x_in)


def _fox_prep_bwd(dmix, mix, *, name):
    T = dmix.shape[0]
    H = FOX_HEADS
    tm = _tile(T, 1408, BLOCK)
    lanes = 2 * HEAD_DIM
    first = mix.shape[1] // lanes - H // 2

    def body(d_ref, o_ref, da_ref):
        lane = lax.broadcasted_iota(jnp.int32, (1, lanes), 1)
        d2 = d_ref[...].astype(F32)
        prod = d2 * o_ref[...].astype(F32)
        for e in range(2):
            delta = jnp.sum(jnp.where(lane // HEAD_DIM == e, prod, 0.0), axis=1, keepdims=True)
            da_ref[e] = _lanes(lane, e, d2, _fox_lanes(e)[0], _split3(-delta)).astype(BF16)

    pair = pl.BlockSpec((tm, lanes), lambda p, i: (i, first + p))
    return pl.pallas_call(
        body, out_shape=jax.ShapeDtypeStruct((H, T, lanes), BF16), grid=(H // 2, T // tm),
        in_specs=[pair, pair],
        out_specs=pl.BlockSpec((2, tm, lanes), lambda p, i: (p, i, 0)),
        compiler_params=_params(("parallel", "parallel")), name=name)(dmix, mix)


def _fox_bwd(q_aug, k_aug, v_aug, do_aug, lse_row, dproj, *, ex=None, name):
    H, T, lanes = q_aug.shape
    tq = FOX_TILE
    nq = T // tq
    G = FOX_GROUP

    def side_by_side(tiles, scale=None):
        lane = lax.broadcasted_iota(jnp.int32, tiles[0].shape, 1)
        out = [jnp.where(lane < HEAD_DIM, tiles[2 * p], tiles[2 * p + 1]) for p in range(G // 2)]
        out = jnp.concatenate(out, axis=1)
        return out if scale is None else out * scale

    def body(q_ref, k_ref, v_ref, do_ref, lse_ref, dproj_in, out_ref, dcq_ref, dck_ref, dk_acc, dv_acc, dq_ref):
        j = pl.program_id(1)

        @pl.when(j == 0)
        def _():
            dq_ref[...] = jnp.zeros(dq_ref.shape, F32)
            dcq_ref[...] = jnp.zeros(dcq_ref.shape, F32)

        dk_acc[...] = jnp.zeros(dk_acc.shape, F32)
        dv_acc[...] = jnp.zeros(dv_acc.shape, F32)

        def step(qb, diag):
            off = pl.multiple_of(qb * tq, tq)
            heads = range(G)
            qa = [q_ref[g, pl.ds(off, tq), :] for g in heads]
            da = [do_ref[g, pl.ds(off, tq), :] for g in heads]
            s_t = [lax.dot_general(k_ref[g], qa[g], NT, preferred_element_type=F32) for g in heads]
            dp_t = [lax.dot_general(v_ref[g], da[g], NT, preferred_element_type=F32) for g in heads]
            p_t = [jnp.exp(s_t[g] - lse_ref[g, qb]) for g in heads]
            if diag:
                r = lax.broadcasted_iota(jnp.int32, (tq, tq), 0)
                c = lax.broadcasted_iota(jnp.int32, (tq, tq), 1)
                p_t = [jnp.where(c >= r, p, 0.0) for p in p_t]
            dsb = [(p_t[g] * dp_t[g]).astype(BF16) for g in heads]
            dv = [jnp.dot(p_t[g].astype(BF16), da[g], preferred_element_type=F32) for g in heads]
            dk = [jnp.dot(dsb[g], qa[g], preferred_element_type=F32) for g in heads]
            dq = [lax.dot_general(k_ref[g], dsb[g], TN, preferred_element_type=F32) for g in heads]
            for g in heads:
                dv_acc[g] += dv[g]
                dk_acc[g] += dk[g]
                dq_ref[g, qb] += dq[g]
                dcq_ref[g, qb] += jnp.sum(dsb[g].astype(F32), axis=0, keepdims=True)

        step(j, True)

        def loop_body(qb, carry):
            step(qb, False)
            return carry

        lax.fori_loop(j + 1, nq, loop_body, 0)
        dk = [dk_acc[g] for g in range(G)]
        out_ref[:, 0:wide] = side_by_side([dq_ref[g, j].T for g in range(G)], SCALE).astype(out_ref.dtype)
        out_ref[:, wide:2 * wide] = side_by_side(dk).astype(out_ref.dtype)
        out_ref[:, 2 * wide:3 * wide] = side_by_side([dv_acc[g] for g in range(G)]).astype(out_ref.dtype)
        for g in range(G):
            kc = _fox_lanes(g % 2)[0]
            dck_ref[g] = -dk[g].T[kc:kc + 1, :]

    blk = pl.BlockSpec((G, tq, lanes), lambda h, j: (h, j, 0))
    full = pl.BlockSpec((G, T, lanes), lambda h, j: (h, 0, 0))
    wide = G * HEAD_DIM
    first = dproj.shape[1] // (3 * wide) - H // G
    grid = (H // G, nq)
    body, x_in, x_in_specs, x_out, x_out_specs, x_scr = _carry(ex, grid, 6, 3, body)
    rows = jax.ShapeDtypeStruct((H, nq, 1, tq), F32)
    all_rows = pl.BlockSpec((G, nq, 1, tq), lambda h, j: (h, 0, 0, 0))
    return pl.pallas_call(
        body,
        out_shape=(jax.ShapeDtypeStruct(dproj.shape, dproj.dtype), rows, rows, *x_out),
        grid=grid,
        in_specs=[full, blk, blk, full, all_rows, pl.BlockSpec(memory_space=pl.ANY)] + x_in_specs,
        out_specs=(pl.BlockSpec((tq, 3 * wide), lambda h, j: (j, first + h)), all_rows,
                   pl.BlockSpec((G, None, 1, tq), lambda h, j: (h, j, 0, 0)), *x_out_specs),
        input_output_aliases={5: 0},
        scratch_shapes=[pltpu.VMEM((G, tq, lanes), F32), pltpu.VMEM((G, tq, lanes), F32),
                        pltpu.VMEM((G, nq, lanes, tq), F32)] + x_scr,
        compiler_params=_params(("arbitrary", "arbitrary")), name=name,
    )(q_aug, k_aug, v_aug, do_aug, lse_row, dproj, *x_in)


def _t5_bucket_np(d):
    n = np.maximum(d, 0).astype(np.int32)
    max_exact = N_BUCKETS // 2
    nf = np.maximum(n, 1).astype(np.float32)
    large = max_exact + (np.log(nf / max_exact) / math.log(MAX_DISTANCE / max_exact)
                         * (N_BUCKETS - max_exact)).astype(np.int32)
    large = np.minimum(large, N_BUCKETS - 1)
    return np.where(n < max_exact, n, large)


def _bucket_onehots():
    k = np.arange(BLOCK)[:, None]
    q = np.arange(BLOCK)[None, :]
    eye = np.eye(N_BUCKETS, dtype=np.float32)
    cur = eye[_t5_bucket_np(q - k).reshape(-1)]
    prev = eye[_t5_bucket_np(BLOCK + q - k).reshape(-1)]
    return cur, prev


SWA_K_COL = SWA_Q_HEADS * HEAD_DIM // (2 * HEAD_DIM)
SWA_V_COL = SWA_K_COL + 1


def _swa_terms(raw, bc, bp, far, sink, n):
    k = lax.broadcasted_iota(jnp.int32, (BLOCK, BLOCK), 0)
    q = lax.broadcasted_iota(jnp.int32, (BLOCK, BLOCK), 1)
    never = 2 * BLOCK
    s_c = raw[0] + bc
    s_p = raw[1] + bp
    s_m = raw[2] + jnp.where(n == 1, bp, far)
    s_c = jnp.where((k <= q) & (k >= jnp.where(n >= 1, 0, PAD_ROWS)), s_c, NEG)
    s_p = jnp.where(k > q + jnp.where(n >= 2, 0, never), s_p, NEG)
    s_m = jnp.where(k >= jnp.where(n >= 1, PAD_ROWS, never), s_m, NEG)
    m = jnp.maximum(jnp.maximum(jnp.max(s_c, axis=0, keepdims=True), jnp.max(s_p, axis=0, keepdims=True)),
                    jnp.maximum(jnp.max(s_m, axis=0, keepdims=True), sink))
    e = [jnp.exp(s_c - m), jnp.exp(s_p - m), jnp.exp(s_m - m)]
    e_s = jnp.exp(sink - m)
    l = (jnp.sum(e[0], axis=0, keepdims=True) + jnp.sum(e[1], axis=0, keepdims=True)
         + jnp.sum(e[2], axis=0, keepdims=True) + e_s)
    return e, e_s, l


SWA_STEP = 3


def _swa_specs():
    R = SWA_STEP

    def window(col):
        return ([pl.BlockSpec((BLOCK, BLOCK), lambda s, w=w: (jnp.maximum(R * s - 1 + w, 0), col)) for w in range(R + 1)]
                + [pl.BlockSpec((BLOCK, BLOCK), lambda s: (0, col))])

    qblk = pl.BlockSpec((R * BLOCK, SWA_Q_HEADS * HEAD_DIM), lambda s: (s, 0))
    bias = pl.BlockSpec((SWA_Q_HEADS, BLOCK, BLOCK), lambda s: (0, 0, 0))
    smem = pl.BlockSpec(memory_space=pltpu.SMEM)
    return qblk, window(SWA_K_COL), window(SWA_V_COL), bias, smem


def _swa_own_kv(tile_ref, kv):
    lane = lax.broadcasted_iota(jnp.int32, (BLOCK, 2 * HEAD_DIM), 1)
    t = tile_ref[...].astype(F32)
    return jnp.where(lane // HEAD_DIM == kv, t, pltpu.roll(t, HEAD_DIM, 1)).astype(BF16)


def _swa_fwd(proj, bc, bp, far, sinks, *, name):
    T = proj.shape[0]
    nb = T // BLOCK
    G = SWA_GROUP
    Hq = SWA_Q_HEADS
    lanes = 2 * HEAD_DIM

    R = SWA_STEP
    assert nb % R == 0

    def body(*refs):
        q_ref, k_refs, v_refs = refs[0], refs[1:R + 3], refs[R + 3:2 * R + 5]
        bc_ref, bp_ref, far_ref, sink_ref, o_ref = refs[2 * R + 5:]
        s = pl.program_id(0)
        lane = lax.broadcasted_iota(jnp.int32, (BLOCK, lanes), 1)
        kvs = range(SWA_KV_HEADS)
        kk = [[_swa_own_kv(ref, kv) for ref in k_refs] for kv in kvs]
        vv = [[_swa_own_kv(ref, kv) for ref in v_refs] for kv in kvs]
        chains = [(r, h) for r in range(R) for h in range(Hq)]
        tiles = lambda r: (r + 1, r, R + 1)
        q2 = {(r, pair): q_ref[r * BLOCK:(r + 1) * BLOCK, pair * lanes:(pair + 1) * lanes].astype(F32) * SCALE
              for r in range(R) for pair in range(Hq // 2)}
        qm = {c: jnp.where(lane // HEAD_DIM == c[1] % 2, q2[c[0], c[1] // 2], 0.0).astype(BF16) for c in chains}
        raw = {c: [lax.dot_general(kk[c[1] // G][w], qm[c], NT, preferred_element_type=F32) for w in tiles(c[0])]
               for c in chains}
        terms = {c: _swa_terms(raw[c], bc_ref[c[1]], bp_ref[c[1]], far_ref[c[1]], sink_ref[c[1]], R * s + c[0])
                 for c in chains}
        o_t = {c: sum(lax.dot_general(vv[c[1] // G][w], terms[c][0][b].astype(BF16), TN, preferred_element_type=F32)
                      for b, w in enumerate(tiles(c[0]))) for c in chains}
        outs = {c: (o_t[c] / terms[c][2]).T for c in chains}
        for r in range(R):
            for pair in range(Hq // 2):
                o_ref[r * BLOCK:(r + 1) * BLOCK, pair * lanes:(pair + 1) * lanes] = jnp.where(
                    lane < HEAD_DIM, outs[r, 2 * pair], outs[r, 2 * pair + 1]).astype(o_ref.dtype)

    qblk, keys, vals, bias, smem = _swa_specs()
    return pl.pallas_call(
        body, out_shape=jax.ShapeDtypeStruct((T, D_MODEL), BF16), grid=(nb // R,),
        in_specs=[qblk] + keys + vals + [bias, bias, smem, smem],
        out_specs=qblk,
        compiler_params=_params(("parallel",)), name=name,
    )(proj, *([proj] * (2 * R + 4)), bc, bp, far, sinks)


def _swa_bwd(proj, dmix, bc, bp, far, sinks, *, ex=None, name):
    T, width = proj.shape
    nb = T // BLOCK
    G = SWA_GROUP
    Hq = SWA_Q_HEADS
    lanes = 2 * HEAD_DIM
    qw = Hq * HEAD_DIM
    own_w = qw + 2 * lanes

    R = SWA_STEP
    assert nb % R == 0
    n_in = 2 * R + 10

    def body(*refs):
        q_ref, k_refs, v_refs = refs[0], refs[1:R + 3], refs[R + 3:2 * R + 5]
        do_ref, bc_ref, bp_ref, far_ref, sink_ref = refs[2 * R + 5:n_in]
        dp_ref, dbc_ref, dbp_ref, dbf_ref, dsk_ref, dk_acc, dv_acc = refs[n_in:]
        s = pl.program_id(0)

        @pl.when(s == 0)
        def _():
            for ref in (dk_acc, dv_acc, dbc_ref, dbp_ref, dbf_ref, dsk_ref):
                ref[...] = jnp.zeros(ref.shape, F32)

        lane = lax.broadcasted_iota(jnp.int32, (BLOCK, lanes), 1)
        kvs = range(SWA_KV_HEADS)
        kk = [[_swa_own_kv(ref, kv) for ref in k_refs] for kv in kvs]
        vv = [[_swa_own_kv(ref, kv) for ref in v_refs] for kv in kvs]
        chains = [(r, h) for r in range(R) for h in range(Hq)]
        blocks = range(3)
        tiles = lambda r: (r + 1, r, R + 1)
        sub = lambda ref, r, pair: ref[r * BLOCK:(r + 1) * BLOCK, pair * lanes:(pair + 1) * lanes]
        q2 = {(r, pair): sub(q_ref, r, pair).astype(F32) * SCALE for r in range(R) for pair in range(Hq // 2)}
        d2 = {(r, pair): sub(do_ref, r, pair) for r in range(R) for pair in range(Hq // 2)}
        own = [lane // HEAD_DIM == half for half in range(2)]
        qm = {c: jnp.where(own[c[1] % 2], q2[c[0], c[1] // 2], 0.0).astype(BF16) for c in chains}
        dom = {c: jnp.where(own[c[1] % 2], d2[c[0], c[1] // 2], jnp.zeros_like(d2[0, 0])) for c in chains}
        raw = {c: [lax.dot_general(kk[c[1] // G][w], qm[c], NT, preferred_element_type=F32) for w in tiles(c[0])]
               for c in chains}
        dp = {c: [lax.dot_general(vv[c[1] // G][w], dom[c], NT, preferred_element_type=F32) for w in tiles(c[0])]
              for c in chains}
        p, ds16 = {}, {}
        for c in chains:
            r, h = c
            n = R * s + r
            e, e_s, l = _swa_terms(raw[c], bc_ref[h], bp_ref[h], far_ref[h], sink_ref[h], n)
            inv = 1.0 / l
            ph = [e[b] * inv for b in blocks]
            delta = sum(jnp.sum(ph[b] * dp[c][b], axis=0, keepdims=True) for b in blocks)
            ds = [ph[b] * (dp[c][b] - delta) for b in blocks]
            dsk_ref[h] += -(e_s * inv) * delta
            dbc_ref[h] += ds[0]
            dbp_ref[h] += ds[1] + jnp.where(n == 1, ds[2], 0.0)
            dbf_ref[h] += jnp.where(n >= 2, ds[2], 0.0)
            p[c] = [x.astype(BF16) for x in ph]
            ds16[c] = [x.astype(BF16) for x in ds]
        dq_t = {c: sum(lax.dot_general(kk[c[1] // G][w], ds16[c][b], TN, preferred_element_type=F32)
                       for b, w in enumerate(tiles(c[0]))) for c in chains}
        group = [range(kv * G, (kv + 1) * G) for kv in kvs]
        dk = {(r, kv): [sum(jnp.dot(ds16[r, h][b], qm[r, h], preferred_element_type=F32) for h in group[kv])
                        for b in blocks] for r in range(R) for kv in kvs}
        dv = {(r, kv): [sum(jnp.dot(p[r, h][b], dom[r, h], preferred_element_type=F32) for h in group[kv])
                        for b in blocks] for r in range(R) for kv in kvs}
        for r in range(R):
            n = R * s + r
            rows = pl.ds(pl.multiple_of(n * BLOCK, BLOCK), BLOCK)
            prev_rows = pl.ds(pl.multiple_of(jnp.maximum(n - 1, 0) * BLOCK, BLOCK), BLOCK)
            for pair in range(Hq // 2):
                dp_ref[rows, pair * lanes:(pair + 1) * lanes] = (jnp.where(
                    lane < HEAD_DIM, dq_t[r, 2 * pair].T, dq_t[r, 2 * pair + 1].T) * SCALE).astype(dp_ref.dtype)
            for acc, ref in ((dk, dk_acc), (dv, dv_acc)):
                tot = [[a + pltpu.roll(a, HEAD_DIM, 1) for a in acc[r, kv]] for kv in kvs]
                both = [jnp.where(lane < HEAD_DIM, tot[0][b], tot[1][b]) for b in blocks]
                ref[rows, :] += both[0]
                ref[prev_rows, :] += both[1]
                ref[0:BLOCK, :] += both[2]

        @pl.when(s == nb // R - 1)
        def _():
            dp_ref[:, qw:qw + lanes] = dk_acc[...].astype(dp_ref.dtype)
            dp_ref[:, qw + lanes:own_w] = dv_acc[...].astype(dp_ref.dtype)

    qblk, keys, vals, bias, smem = _swa_specs()
    dsk = pl.BlockSpec((Hq, 1, BLOCK), lambda s: (0, 0, 0))
    grid = (nb // R,)
    body, x_in, x_in_specs, x_out, x_out_specs, x_scr = _carry(ex, grid, n_in, 5, body)
    tile = jax.ShapeDtypeStruct((Hq, BLOCK, BLOCK), F32)
    return pl.pallas_call(
        body,
        out_shape=(jax.ShapeDtypeStruct((T, width), BF16), tile, tile, tile,
                   jax.ShapeDtypeStruct((Hq, 1, BLOCK), F32), *x_out),
        grid=grid,
        in_specs=[qblk] + keys + vals + [qblk, bias, bias, smem, smem] + x_in_specs,
        out_specs=(pl.BlockSpec((T, own_w), lambda s: (0, 0)), bias, bias, bias, dsk, *x_out_specs),
        scratch_shapes=[pltpu.VMEM((T, lanes), F32), pltpu.VMEM((T, lanes), F32)] + x_scr,
        compiler_params=_params(("arbitrary",)), name=name,
    )(proj, *([proj] * (2 * R + 4)), dmix, bc, bp, far, sinks, *x_in)


def _bias_tiles(tab_t, oh_cur_t, oh_prev_t, *, name):
    Hq = tab_t.shape[0]

    def body(t_ref, oc_ref, op_ref, bc_ref, bp_ref):
        bc_ref[...] = jnp.dot(t_ref[...], oc_ref[...], precision=HIGHEST, preferred_element_type=F32)
        bp_ref[...] = jnp.dot(t_ref[...], op_ref[...], precision=HIGHEST, preferred_element_type=F32)

    vm = pl.BlockSpec(memory_space=pltpu.VMEM)
    shp = jax.ShapeDtypeStruct((Hq, BLOCK * BLOCK), F32)
    bc, bp = pl.pallas_call(body, out_shape=(shp, shp), in_specs=[vm] * 3, out_specs=(vm, vm),
                            compiler_params=_params(), name=name)(tab_t, oh_cur_t, oh_prev_t)
    return bc.reshape(Hq, BLOCK, BLOCK), bp.reshape(Hq, BLOCK, BLOCK)


def _small_grads(dbc, dbp, dbf, dsk, oh_cur, oh_prev, *, name):
    Hq = dbc.shape[0]

    def body(dbc_ref, dbp_ref, dbf_ref, dsk_ref, oc_ref, op_ref, tab_ref, sink_ref):
        tab = (jnp.dot(dbc_ref[...], oc_ref[...], precision=HIGHEST, preferred_element_type=F32)
               + jnp.dot(dbp_ref[...], op_ref[...], precision=HIGHEST, preferred_element_type=F32))
        far = jnp.sum(dbf_ref[...], axis=1, keepdims=True)
        last = lax.broadcasted_iota(jnp.int32, (Hq, N_BUCKETS), 1) == N_BUCKETS - 1
        tab_ref[...] = tab + jnp.where(last, far, 0.0)
        sink_ref[...] = jnp.sum(dsk_ref[...], axis=1, keepdims=True)

    vm = pl.BlockSpec(memory_space=pltpu.VMEM)
    return pl.pallas_call(
        body, out_shape=(jax.ShapeDtypeStruct((Hq, N_BUCKETS), F32), jax.ShapeDtypeStruct((Hq, 1), F32)),
        in_specs=[vm] * 6, out_specs=(vm, vm), compiler_params=_params(), name=name,
    )(dbc.reshape(Hq, -1), dbp.reshape(Hq, -1), dbf.reshape(Hq, -1), dsk.reshape(Hq, -1), oh_cur, oh_prev)


def _coords():
    return lax.axis_index("x"), lax.axis_index("y"), lax.axis_index("c")


class _Exchange:
    def __init__(self, inputs, out_shapes, scratch, start, finish):
        self.inputs, self.out_shapes, self.scratch, self.start, self.finish = inputs, out_shapes, scratch, start, finish


def _carry(ex, grid, n_in, n_out, body):
    if ex is None:
        return body, [], [], [], [], []
    ni, no = len(ex.inputs), len(ex.out_shapes)

    def at_step(which):
        cond = None
        for axis, n in enumerate(grid):
            c = pl.program_id(axis) == (0 if which == "first" else n - 1)
            cond = c if cond is None else cond & c
        return cond

    def wrapped(*refs):
        refs = list(refs)
        n_own_scr = len(refs) - (n_in + ni + n_out + no) - len(ex.scratch)
        own_in, side_in = refs[:n_in], refs[n_in:n_in + ni]
        own_out = refs[n_in + ni:n_in + ni + n_out]
        side_out = refs[n_in + ni + n_out:n_in + ni + n_out + no]
        rest = refs[n_in + ni + n_out + no:]
        own_scr, sems = rest[:n_own_scr], rest[n_own_scr:]

        @pl.when(at_step("first"))
        def _():
            ex.start(side_in, side_out, sems)

        body(*own_in, *own_out, *own_scr)

        @pl.when(at_step("last"))
        def _():
            ex.finish(side_in, side_out, sems)

    hbm = pl.BlockSpec(memory_space=pl.ANY)
    return wrapped, list(ex.inputs), [hbm] * ni, list(ex.out_shapes), [hbm] * no, list(ex.scratch)


def _run_exchange(ex, *, name):
    ni, no = len(ex.inputs), len(ex.out_shapes)

    def body(*refs):
        ins, outs, sems = refs[:ni], refs[ni:ni + no], refs[ni + no:]
        ex.start(ins, outs, sems)
        ex.finish(ins, outs, sems)

    hbm = pl.BlockSpec(memory_space=pl.ANY)
    return pl.pallas_call(
        body, out_shape=tuple(ex.out_shapes), in_specs=[hbm] * ni, out_specs=tuple([hbm] * no),
        scratch_shapes=ex.scratch, compiler_params=_params(), name=name)(*ex.inputs)


def _gather_exchange(shards):
    nt = len(shards)

    def copies(ins, outs, sems):
        send_sems, recv_sems, local_sems = sems
        x, y, c = _coords()
        me, sibling = (x, y, c), (x, y, 1 - c)
        chips = [(1 - x, y), (x, 1 - y), (1 - x, 1 - y)]

        def slot(t, dev):
            return outs[t].at[4 * dev[0] + 2 * dev[1] + dev[2]]

        def copy(t, k, block, to, src=None):
            dst = slot(t, block)
            return pltpu.make_async_remote_copy(
                src_ref=dst if src is None else src, dst_ref=dst,
                send_sem=send_sems.at[t, k], recv_sem=recv_sems.at[t, k], device_id=to, device_id_type=MESH)

        mine = [pltpu.make_async_copy(ins[t], slot(t, me), local_sems.at[t]) for t in range(nt)]
        first = []
        for t in range(nt):
            first.append(copy(t, 0, me, sibling, src=ins[t]))
            first += [copy(t, 1 + j, me, (*chip, c), src=ins[t]) for j, chip in enumerate(chips)]
        return copy, mine, first, me, sibling, chips, c

    def start(ins, outs, sems):
        _, mine, first, *_ = copies(ins, outs, sems)
        for cp in mine + first:
            cp.start()

    def finish(ins, outs, sems):
        copy, mine, first, me, sibling, chips, c = copies(ins, outs, sems)
        passed = []
        for j, chip in enumerate(chips):
            for t in range(nt):
                copy(t, 1 + j, (*chip, c), me).wait_recv()
                cp = copy(t, 4 + j, (*chip, c), sibling)
                cp.start()
                passed.append(cp)
        for t in range(nt):
            copy(t, 0, sibling, me).wait_recv()
            for j, chip in enumerate(chips):
                copy(t, 4 + j, (*chip, 1 - c), me).wait_recv()
        for cp in first + passed:
            cp.wait_send()
        for cp in mine:
            cp.wait()

    return _Exchange(
        list(shards), [jax.ShapeDtypeStruct((N_DEV,) + s.shape, s.dtype) for s in shards],
        [pltpu.SemaphoreType.DMA((nt, 7)), pltpu.SemaphoreType.DMA((nt, 7)), pltpu.SemaphoreType.DMA((nt,))],
        start, finish)


def _swap_exchange(arrays, n_slices, copies):
    nt = len(arrays)

    def start(ins, outs, sems):
        for cp in copies(ins, outs, sems):
            cp.start()

    def finish(ins, outs, sems):
        sends = copies(ins, outs, sems)
        for cp in sends:
            cp.wait_recv()
        for cp in sends:
            cp.wait_send()

    return _Exchange(
        list(arrays), [jax.ShapeDtypeStruct((n_slices,) + a.shape[1:], a.dtype) for a in arrays],
        [pltpu.SemaphoreType.DMA((nt, n_slices)), pltpu.SemaphoreType.DMA((nt, n_slices))], start, finish)


def _cores_exchange(gs):
    def copies(ins, outs, sems):
        send_sems, recv_sems = sems
        x, y, c = _coords()
        return [pltpu.make_async_remote_copy(
            src_ref=ins[t].at[2 * j + (1 - c)], dst_ref=outs[t].at[j],
            send_sem=send_sems.at[t, j], recv_sem=recv_sems.at[t, j], device_id=(x, y, 1 - c), device_id_type=MESH)
            for t in range(len(gs)) for j in range(4)]

    return _swap_exchange(gs, 4, copies)


def _chips_exchange(ps):
    def copies(ins, outs, sems):
        send_sems, recv_sems = sems
        x, y, c = _coords()
        peers = [(1 - x, y), (x, 1 - y), (1 - x, 1 - y)]
        return [pltpu.make_async_remote_copy(
            src_ref=ins[t].at[2 * px + py], dst_ref=outs[t].at[k],
            send_sem=send_sems.at[t, k], recv_sem=recv_sems.at[t, k], device_id=(px, py, c), device_id_type=MESH)
            for t in range(len(ps)) for k, (px, py) in enumerate(peers)]

    return _swap_exchange(ps, 3, copies)


def _add_cores(g, r, core, *, name):
    _, A, B = g.shape
    ta = _tile(A, 512, 16)

    def body(core_ref, a_ref, b_ref, o16_ref):
        o16_ref[...] = (a_ref[...] + b_ref[...]).astype(BF16)

    blk = (None, ta, B)
    return pl.pallas_call(
        body, out_shape=jax.ShapeDtypeStruct((4, A, B), BF16),
        grid_spec=pltpu.PrefetchScalarGridSpec(
            num_scalar_prefetch=1, grid=(4, A // ta),
            in_specs=[pl.BlockSpec(blk, lambda j, i, core_ref: (2 * j + core_ref[0], i, 0)),
                      pl.BlockSpec(blk, lambda j, i, core_ref: (j, i, 0))],
            out_specs=pl.BlockSpec(blk, lambda j, i, core_ref: (j, i, 0))),
        compiler_params=_params(("parallel", "parallel")), name=name)(core, g, r)


def _adamw_math(w, g, m, v):
    m = ADAM_B1 * m + (1.0 - ADAM_B1) * g
    v = ADAM_B2 * v + (1.0 - ADAM_B2) * (g * g)
    m_hat = m / (1.0 - ADAM_B1 ** ADAM_STEP)
    v_hat = v / (1.0 - ADAM_B2 ** ADAM_STEP)
    delta = -ADAM_LR * (m_hat / (jnp.sqrt(v_hat) + ADAM_EPS) + ADAM_WD * w)
    return delta, m, v


def _sum_adamw(mine, sib, r, where, w, m, v, *, ta, name):
    Aw, Bw = w.shape
    Bg = mine.shape[2]
    assert Aw % ta == 0 and Bw <= Bg and mine.shape[1] == Aw

    def body(where_ref, p_ref, s_ref, r0, r1, r2, w_ref, m_ref, v_ref, g_out, d_out, m_out, v_out):
        g = (((p_ref[:, :Bw] + s_ref[:, :Bw]) + r0[:, :Bw].astype(F32))
             + r1[:, :Bw].astype(F32)) + r2[:, :Bw].astype(F32)
        delta, m_new, v_new = _adamw_math(w_ref[...], g, m_ref[...], v_ref[...])
        g_out[...] = g
        d_out[...] = delta
        m_out[...] = m_new
        v_out[...] = v_new

    gblk = (None, ta, Bg)
    row = pl.BlockSpec((ta, Bw), lambda i, where_ref: (i, 0))
    rspecs = [pl.BlockSpec(gblk, (lambda i, where_ref, k=k: (k, i, 0))) for k in range(3)]
    shp = jax.ShapeDtypeStruct((Aw, Bw), F32)
    return pl.pallas_call(
        body, out_shape=(shp, shp, shp, shp),
        grid_spec=pltpu.PrefetchScalarGridSpec(
            num_scalar_prefetch=1, grid=(Aw // ta,),
            in_specs=[pl.BlockSpec(gblk, lambda i, where_ref: (2 * where_ref[0] + where_ref[1], i, 0)),
                      pl.BlockSpec(gblk, lambda i, where_ref: (where_ref[0], i, 0))] + rspecs + [row, row, row],
            out_specs=(row, row, row, row)),
        compiler_params=_params(("parallel",)), name=name)(where, mine, sib, r, r, r, w, m, v)


def _adamw(w, g, m, v, *, name):
    def body(w_ref, g_ref, m_ref, v_ref, d_out, m_out, v_out):
        delta, m_new, v_new = _adamw_math(w_ref[...], g_ref[...], m_ref[...], v_ref[...])
        d_out[...] = delta
        m_out[...] = m_new
        v_out[...] = v_new

    vm = pl.BlockSpec(memory_space=pltpu.VMEM)
    shp = jax.ShapeDtypeStruct(w.shape, F32)
    return pl.pallas_call(body, out_shape=(shp, shp, shp), in_specs=[vm] * 4, out_specs=(vm, vm, vm),
                          compiler_params=_params(), name=name)(w, g, m, v)


def _small_allreduce_adamw(s, w, m, v, *, name):
    R, W = s.shape

    def body(s_ref, w_ref, m_ref, v_ref, g_out, d_out, m_out, v_out, gath, send_sems, recv_sems):
        x, y, c = _coords()
        mine = 4 * x + 2 * y + c
        gath[mine] = s_ref[...]
        peers = [((1 - x) if k & 4 else x, (1 - y) if k & 2 else y, (1 - c) if k & 1 else c) for k in range(1, N_DEV)]
        sends = []
        for k in range(1, N_DEV):
            peer = peers[k - 1]
            sends.append(pltpu.make_async_remote_copy(
                src_ref=s_ref, dst_ref=gath.at[mine], send_sem=send_sems.at[k - 1], recv_sem=recv_sems.at[k - 1],
                device_id=peer, device_id_type=MESH))
        for cp in sends:
            cp.start()
        for k in range(1, N_DEV):
            peer = peers[k - 1]
            pltpu.make_async_remote_copy(
                src_ref=s_ref, dst_ref=gath.at[4 * peer[0] + 2 * peer[1] + peer[2]],
                send_sem=send_sems.at[k - 1], recv_sem=recv_sems.at[k - 1],
                device_id=peer, device_id_type=MESH).wait_recv()
        for cp in sends:
            cp.wait_send()
        g = gath[0]
        for d in range(1, N_DEV):
            g = g + gath[d]
        delta, m_new, v_new = _adamw_math(w_ref[...], g, m_ref[...], v_ref[...])
        g_out[...] = g
        d_out[...] = delta
        m_out[...] = m_new
        v_out[...] = v_new

    vm = pl.BlockSpec(memory_space=pltpu.VMEM)
    shp = jax.ShapeDtypeStruct((R, W), F32)
    return pl.pallas_call(
        body, out_shape=(shp, shp, shp, shp), in_specs=[vm] * 4, out_specs=(vm, vm, vm, vm),
        scratch_shapes=[pltpu.VMEM((N_DEV, R, W), F32), pltpu.SemaphoreType.DMA((N_DEV - 1,)),
                        pltpu.SemaphoreType.DMA((N_DEV - 1,))],
        compiler_params=_params(), name=name)(s, w, m, v)


def _pack_small(rel_bias, g1, g2, g3, g4, b_forget, sinks, extra=None, meta=None):
    misc = jnp.concatenate([rel_bias.reshape(-1), b_forget.reshape(-1), sinks.reshape(-1)])
    misc = jnp.concatenate([misc, jnp.zeros((D_MODEL - misc.shape[0],), F32)])[None]
    last = jnp.zeros((1, D_MODEL), F32) if extra is None else extra
    meta = jnp.zeros((N_META, D_MODEL), F32) if meta is None else meta
    return jnp.concatenate([g1, g2, g3, g4, misc, last, jnp.zeros((2, D_MODEL), F32), meta], axis=0)


def _unpack_small(p):
    nrb = N_BUCKETS * SWA_Q_HEADS
    misc = p[4]
    return dict(rel_bias=misc[:nrb].reshape(N_BUCKETS, SWA_Q_HEADS), ln_pre_mix=p[0:1], ln_post_mix=p[1:2],
                ln_pre_ffn=p[2:3], ln_post_ffn=p[3:4], b_forget=misc[nrb:nrb + 8].reshape(1, 8),
                sinks=misc[nrb + 8:nrb + 16].reshape(1, 8))


def _proj_runs():
    gw = FOX_GROUP * HEAD_DIM
    swa = SWA_Q_W + 2 * SWA_KV_HEADS * HEAD_DIM
    runs = [(0, swa)]
    for grp in range(FOX_HEADS // FOX_GROUP):
        runs += [(swa + part * FOX_W + grp * gw, swa + part * FOX_W + (grp + 1) * gw) for part in range(3)]
    return runs


def _columns_from_shards(gathered, runs, shard):
    pieces = []
    for start, stop in runs:
        for d in range(start // shard, (stop - 1) // shard + 1):
            lo = d * shard
            pieces.append(gathered[d][:, max(start, lo) - lo:min(stop, lo + shard) - lo])
    return jnp.concatenate(pieces, axis=1)


def _device_shards(qkv, gate, shard, padded):
    pos, segments = 0, []
    for start, stop in _proj_runs():
        segments.append((start, stop, qkv, pos))
        pos += stop - start
    segments.append((pos, pos + gate.shape[1], gate, 0))
    total = pos + gate.shape[1]
    assert total % shard == 0
    zeros = jnp.zeros((qkv.shape[0], padded - shard), qkv.dtype)
    out = []
    for d in range(total // shard):
        lo, hi = d * shard, (d + 1) * shard
        pieces = [arr[:, src + max(lo, s) - s:src + min(hi, e) - s]
                  for s, e, arr, src in sorted(segments, key=lambda seg: seg[0]) if max(lo, s) < min(hi, e)]
        out.append(jnp.concatenate(pieces + [zeros], axis=1))
    return jnp.stack(out)


def kernel(x, meta_tokens, rel_bias, ln_pre_mix, ln_post_mix, ln_pre_ffn, ln_post_ffn, w_in, b_forget, sinks, w_out, w_gate_up, w_down, loss_target, m_meta_tokens, m_rel_bias, m_ln_pre_mix, m_ln_post_mix, m_ln_pre_ffn, m_ln_post_ffn, m_w_in, m_b_forget, m_sinks, m_w_out, m_w_gate_up, m_w_down, v_meta_tokens, v_rel_bias, v_ln_pre_mix, v_ln_post_mix, v_ln_pre_ffn, v_ln_post_ffn, v_w_in, v_b_forget, v_sinks, v_w_out, v_w_gate_up, v_w_down):
    seq = x.shape[1]
    T = BLOCK + seq
    assert T % FOX_TILE == 0
    nq = T // FOX_TILE
    tm = _tile(T, 1056)
    cin = w_in.shape[2]
    hid = w_down.shape[1]
    F = N_DEV * hid
    assert w_gate_up.shape[2] == 2 * hid and cin <= W_IN_PAD and hid % 16 == 0

    x_i, y_i, c_i = _coords()
    core = jnp.reshape(c_i, (1,)).astype(jnp.int32)
    where = jnp.stack([2 * x_i + y_i, c_i]).astype(jnp.int32)
    w_in_s = jnp.pad(w_in[0].astype(BF16), ((0, 0), (0, W_IN_PAD - cin)))
    w_gu_t = w_gate_up[0].T
    h0, target, hn1, hn1_t, g_in, _ = _pad_rows_rms(x[0], loss_target[0], ln_pre_mix,
                                                    _gather_exchange([w_in_s, meta_tokens]), name="ag_w_in_rms_pre_mix")
    gather_rest = _gather_exchange([w_out[0].astype(BF16), w_gu_t.astype(BF16), w_down[0].astype(BF16)])
    w_qkv = _columns_from_shards(g_in, _proj_runs(), cin)
    w_f = jnp.pad(_columns_from_shards(g_in, [(D_QKV, D_PROJ)], cin), ((0, 0), (0, BLOCK - FOX_HEADS)))

    proj = _matmul(hn1, w_qkv, out_dtype=BF16, tm=tm, tn=D_QKV, name="mm_in_proj")
    proj_f = _matmul(hn1, w_f, out_dtype=F32, tm=tm, tn=BLOCK, name="mm_in_proj_f")

    f_t = proj_f[:, :FOX_HEADS].T
    bf_col = b_forget.reshape(FOX_HEADS, 1)

    oh_cur, oh_prev = _bucket_onehots()
    bias_c, bias_p = _bias_tiles(rel_bias.T, jnp.asarray(oh_cur.T), jnp.asarray(oh_prev.T), name="bias_tiles")
    far = rel_bias[N_BUCKETS - 1]
    sink_v = sinks[0]
    mix_a = _swa_fwd(proj, bias_c, bias_p, far, sink_v, name="swa_fwd")

    cum_col = _fox_gates_fwd(f_t, bf_col, name="fox_gates_fwd")
    q_b, k_b, v_b = _fox_prep(proj, cum_col, name="fox_prep")
    mix, lse_row, g_out, g_gu, g_down = _fox_fwd(q_b, k_b, v_b, mix_a, ex=gather_rest, name="fox_fwd")
    w_out_full = g_out.reshape(D_MODEL, D_MODEL)
    w_gu_full_t = g_gu.reshape(2 * F, D_MODEL)
    w_down_full = g_down.reshape(F, D_MODEL)

    a1 = _matmul(mix, w_out_full, out_dtype=F32, tm=tm, tn=D_MODEL, name="mm_out_proj")
    h1, hn2 = _post_res_norm(a1, ln_post_mix, h0, ln_pre_ffn, name="post_mix_pre_ffn")
    gate, up, act, act_t = _gate_up_swiglu(hn2, w_gu_full_t, name="mm_gate_up")
    ff = _matmul(act, w_down_full, out_dtype=F32, tm=tm, tn=512, name="mm_down")
    dh2, dff, dg_post_ffn, loss_acc = _loss_head(ff, ln_post_ffn, h1, target, name="loss_head")

    dgu = _d_act_swiglu(dff, w_down_full, gate, up, name="mm_d_act")
    d_w_down = _matmul(act_t, dff, out_dtype=F32, tm=_tile(F, 768), tn=512, name="mm_dw_down")
    dhn2 = _matmul(dgu, w_gu_full_t, out_dtype=F32, tm=tm, tn=512, tk=F, name="mm_d_hn2")
    d_w_gu_t = _matmul(dgu, hn2, ta=True, out_dtype=F32, tm=256, tn=D_MODEL, name="mm_dw_gate_up")
    dh1, dg_pre_ffn, da1, dg_post_mix = _rms_bwd(h1, ln_pre_ffn, dhn2, dh2, out_dtype=F32,
                                                 then=(a1, ln_post_mix), name="rms_bwd_pre_ffn_post_mix")
    dmix = _matmul(da1, w_out_full, nt=True, out_dtype=BF16, tm=tm, tn=D_MODEL, name="mm_d_mix")
    d_w_out = _matmul(mix, da1, ta=True, out_dtype=F32, tm=512, tn=D_MODEL, name="mm_dw_out")

    ffn_grads = [g.reshape(N_DEV, -1, D_MODEL) for g in (d_w_out, d_w_gu_t, d_w_down)]
    dproj_a, dbc, dbp, dbf, dsk, *ffn_sibling = _swa_bwd(
        proj, dmix, bias_c, bias_p, far, sink_v, ex=_cores_exchange(ffn_grads), name="swa_bwd")
    d_tab, d_sink = _small_grads(dbc, dbp, dbf, dsk, jnp.asarray(oh_cur), jnp.asarray(oh_prev), name="small_grads")
    ffn_sums = [_add_cores(g, r, core, name="rs_add_" + t)
                for g, r, t in zip(ffn_grads, ffn_sibling, ["w_out", "w_gate_up", "w_down"])]

    do_b = _fox_prep_bwd(dmix, mix, name="fox_prep_bwd")
    dproj, dcq, dck, *ffn_chips = _fox_bwd(
        q_b, k_b, v_b, do_b, lse_row, dproj_a, ex=_chips_exchange(ffn_sums), name="fox_bwd")
    df_t, d_bf = _fox_gates_bwd(dcq.reshape(FOX_HEADS, T), dck.reshape(FOX_HEADS, T), f_t, bf_col,
                                name="fox_gates_bwd")
    df = jnp.pad(df_t.T.astype(BF16), ((0, 0), (0, BLOCK - FOX_HEADS)))

    d_w_qkv = _matmul(hn1_t, dproj, out_dtype=F32, tm=512, tn=768, name="mm_dw_in")
    d_w_f = _matmul(hn1_t, df, out_dtype=F32, tm=512, tn=BLOCK, name="mm_dw_in_f")
    d_w_in = _device_shards(d_w_qkv, d_w_f[:, :FOX_HEADS], cin, W_IN_PAD)
    dhn1, in_sibling = _matmul(dproj, w_qkv, nt=True, out_dtype=F32, tm=tm, tn=512,
                               ex=_cores_exchange([d_w_in]), name="mm_d_hn1")
    in_sum = _add_cores(d_w_in, in_sibling, core, name="rs_add_w_in")
    dx_rows, dg_pre_mix, dh0_head, in_chips = _rms_bwd(
        h0, ln_pre_mix, dhn1, dh1, out_dtype=F32, dy2=(df, w_f), split_head=True,
        ex=_chips_exchange([in_sum]), name="rms_bwd_pre_mix")
    grad_x = dx_rows[None]
    d_meta = dh0_head[PAD_ROWS:]

    rs_out, rs_gu, rs_down = zip(ffn_grads, ffn_sibling, ffn_chips)
    updates = [("w_in", (d_w_in, in_sibling, in_chips), (w_in[0], m_w_in[0], v_w_in[0]), 256),
               ("w_out", rs_out, (w_out[0], m_w_out[0], v_w_out[0]), BLOCK),
               ("w_gate_up", rs_gu, (w_gu_t, m_w_gate_up[0].T, v_w_gate_up[0].T), hid),
               ("w_down", rs_down, (w_down[0], m_w_down[0], v_w_down[0]), hid)]
    big = [{}, {}, {}, {}]
    for t, grads, shard, ta in updates:
        res = _sum_adamw(*grads, where, *shard, ta=ta, name="rs_adamw_" + t)
        for kind in range(4):
            big[kind][t] = (res[kind].T if t == "w_gate_up" else res[kind])[None]

    loss_row = jnp.pad(loss_acc[0:1, 0:1] * (0.5 / D_MODEL), ((0, 0), (0, D_MODEL - 1)))
    s_small = _pack_small(d_tab.T, dg_pre_mix, dg_post_mix, dg_pre_ffn, dg_post_ffn, d_bf, d_sink,
                          extra=loss_row, meta=d_meta)
    w_s = _pack_small(rel_bias, ln_pre_mix, ln_post_mix, ln_pre_ffn, ln_post_ffn, b_forget, sinks)
    m_s = _pack_small(m_rel_bias, m_ln_pre_mix, m_ln_post_mix, m_ln_pre_ffn, m_ln_post_ffn, m_b_forget, m_sinks)
    v_s = _pack_small(v_rel_bias, v_ln_pre_mix, v_ln_post_mix, v_ln_pre_ffn, v_ln_post_ffn, v_b_forget, v_sinks)
    small = _small_allreduce_adamw(s_small, w_s, m_s, v_s, name="small_allreduce_adamw")
    loss = small[0][5, 0]
    mcols = meta_tokens.shape[1]
    g_meta_mine = lax.dynamic_slice(small[0][8:8 + N_META], (0, (4 * x_i + 2 * y_i + c_i) * mcols), (N_META, mcols))
    big[0]["meta_tokens"] = g_meta_mine
    for kind, arr in enumerate(_adamw(meta_tokens, g_meta_mine, m_meta_tokens, v_meta_tokens, name="adamw_meta")):
        big[kind + 1]["meta_tokens"] = arr
    small = [_unpack_small(p) for p in small]

    names = ["meta_tokens", "rel_bias", "ln_pre_mix", "ln_post_mix", "ln_pre_ffn", "ln_post_ffn", "w_in",
             "b_forget", "sinks", "w_out", "w_gate_up", "w_down"]
    outs = [loss, grad_x]
    for kind in range(4):
        for nme in names:
            outs.append(big[kind][nme] if nme in big[kind] else small[kind][nme])
    return tuple(outs)
```

```python
import math

import numpy as np
import jax
import jax.numpy as jnp
from jax import lax
from jax.experimental import pallas as pl
from jax.experimental.pallas import tpu as pltpu

F32 = jnp.float32
BF16 = jnp.bfloat16
HIGHEST = lax.Precision.HIGHEST
MESH = pl.DeviceIdType.MESH

N_DEV = 8
D_MODEL = 1024
N_META = 16
HEAD_DIM = 64
SWA_Q_HEADS = 8
SWA_KV_HEADS = 2
SWA_GROUP = 4
FOX_HEADS = 8
FOX_W = FOX_HEADS * HEAD_DIM
SWA_Q_W = SWA_Q_HEADS * HEAD_DIM
BLOCK = 128
PAD_ROWS = BLOCK - N_META
N_BUCKETS = 32
MAX_DISTANCE = 128
D_FF = 2816
D_QKV = 2304
D_PROJ = D_QKV + FOX_HEADS
D_PROJ_PAD = 2560
EPS = 1e-6
NEG = -1e30
SCALE = HEAD_DIM ** -0.5
ADAM_LR, ADAM_B1, ADAM_B2, ADAM_EPS, ADAM_WD, ADAM_STEP = 0.001, 0.9, 0.999, 1e-08, 0.01, 10
VMEM_LIMIT = 56 * 1024 * 1024
FOX_TILE = 384
FOX_GROUP = 4
W_IN_PAD = 384

NT = (((1,), (1,)), ((), ()))
NN = (((1,), (0,)), ((), ()))
TN = (((0,), (0,)), ((), ()))


def _params(sem=None, **kw):
    if sem is not None:
        kw["dimension_semantics"] = sem
    return pltpu.CompilerParams(vmem_limit_bytes=VMEM_LIMIT, **kw)


def _tile(n, target, mult=16):
    best = None
    for t in range(mult, min(n, target) + 1, mult):
        if n % t == 0:
            best = t
    assert best is not None, (n, target)
    return best


def _matmul(a, b, *, nt=False, ta=False, out_dtype, tm, tn, tk=None, ex=None, name):
    M, K = a.shape[::-1] if ta else a.shape
    assert not (ta and nt)
    N = b.shape[0] if nt else b.shape[1]
    tk = K if tk is None else tk
    assert M % tm == 0 and N % tn == 0 and K % tk == 0, (name, a.shape, b.shape, tm, tn, tk)
    nk = K // tk
    dn = NT if nt else (TN if ta else NN)
    a_spec = pl.BlockSpec((tk, tm), lambda i, j, k: (k, i)) if ta else pl.BlockSpec((tm, tk), lambda i, j, k: (i, k))

    def body(a_ref, b_ref, o_ref, *scr):
        part = lax.dot_general(a_ref[...], b_ref[...], dn, preferred_element_type=F32)
        if nk == 1:
            o_ref[...] = part.astype(o_ref.dtype)
        else:
            acc = scr[0]
            k = pl.program_id(2)

            @pl.when(k == 0)
            def _():
                acc[...] = part

            @pl.when(k > 0)
            def _():
                acc[...] += part

            @pl.when(k == nk - 1)
            def _():
                o_ref[...] = acc[...].astype(o_ref.dtype)

    if nt:
        b_spec = pl.BlockSpec((tn, tk), lambda i, j, k: (j, k))
    else:
        b_spec = pl.BlockSpec((tk, tn), lambda i, j, k: (k, j))
    out_shape = jax.ShapeDtypeStruct((M, N), out_dtype)
    out_spec = pl.BlockSpec((tm, tn), lambda i, j, k: (i, j))
    grid = (M // tm, N // tn, nk)
    body, x_in, x_in_specs, x_out, x_out_specs, x_scr = _carry(ex, grid, 2, 1, body)
    res = pl.pallas_call(
        body,
        out_shape=(out_shape, *x_out),
        grid=grid,
        in_specs=[a_spec, b_spec] + x_in_specs,
        out_specs=(out_spec, *x_out_specs),
        scratch_shapes=([pltpu.VMEM((tm, tn), F32)] if nk > 1 else []) + x_scr,
        compiler_params=_params(("parallel", "parallel", "arbitrary") if ex is None else ("arbitrary",) * 3),
        name=name,
    )(a, b, *x_in)
    return res[0] if ex is None else res


def _rstd(x):
    return lax.rsqrt(jnp.mean(x * x, axis=-1, keepdims=True) + EPS)


def _pad_rows_rms(x, target, g, ex, *, name):
    S, D = x.shape
    nb = S // BLOCK + 1
    ni, no = len(ex.inputs), len(ex.out_shapes)
    mcols = D // N_DEV

    def body(x_ref, t_ref, g_ref, *rest):
        side_in, (h_ref, to_ref, y_ref, yt_ref) = rest[:ni], rest[ni:ni + 4]
        side_out = rest[ni + 4:ni + 4 + no]
        meta_buf, meta_sems, *sems = rest[ni + 4 + no:]
        i = pl.program_id(0)

        @pl.when(i == 0)
        def _():
            ex.start(side_in, side_out, sems)

        def norm():
            h = h_ref[...]
            y = h * _rstd(h) * g_ref[...]
            y_ref[...] = y.astype(y_ref.dtype)
            yt_ref[...] = y.T.astype(yt_ref.dtype)

        @pl.when(i < nb - 1)
        def _():
            h_ref[...] = x_ref[...]
            to_ref[...] = t_ref[...]
            norm()

        @pl.when(i == nb - 1)
        def _():
            ex.finish(side_in, side_out, sems)
            copies = [pltpu.make_async_copy(side_out[-1].at[d], meta_buf.at[:, d * mcols:(d + 1) * mcols],
                                            meta_sems.at[d]) for d in range(N_DEV)]
            for cp in copies:
                cp.start()
            for cp in copies:
                cp.wait()
            h_ref[:PAD_ROWS, :] = jnp.zeros((PAD_ROWS, D), F32)
            h_ref[PAD_ROWS:, :] = meta_buf[...]
            to_ref[...] = jnp.zeros_like(to_ref)
            norm()

    src = pl.BlockSpec((BLOCK, D), lambda i: (jnp.minimum(i, nb - 2), 0))
    dst = pl.BlockSpec((BLOCK, D), lambda i: ((i + 1) % nb, 0))
    hbm = pl.BlockSpec(memory_space=pl.ANY)
    rows = jax.ShapeDtypeStruct((BLOCK + S, D), F32)
    return pl.pallas_call(
        body,
        out_shape=(rows, rows, jax.ShapeDtypeStruct((BLOCK + S, D), BF16), jax.ShapeDtypeStruct((D, BLOCK + S), BF16),
                   *ex.out_shapes),
        grid=(nb,),
        in_specs=[src, src, pl.BlockSpec((1, D), lambda i: (0, 0))] + [hbm] * ni,
        out_specs=(dst, dst, dst, pl.BlockSpec((D, BLOCK), lambda i: (0, (i + 1) % nb)), *([hbm] * no)),
        scratch_shapes=[pltpu.VMEM((N_META, D), F32), pltpu.SemaphoreType.DMA((N_DEV,))] + list(ex.scratch),
        compiler_params=_params(("arbitrary",)), name=name)(x, target, g, *ex.inputs)


def _post_res_norm(a, g_post, h, g_pre, *, name):
    T, D = a.shape
    tm = _tile(T, 384, BLOCK)

    def body(a_ref, gp_ref, h_ref, gn_ref, h1_ref, o_ref):
        a = a_ref[...]
        h1 = h_ref[...] + a * _rstd(a) * gp_ref[...]
        h1_ref[...] = h1
        o_ref[...] = (h1 * _rstd(h1) * gn_ref[...]).astype(o_ref.dtype)

    row = pl.BlockSpec((tm, D), lambda i: (i, 0))
    vec = pl.BlockSpec((1, D), lambda i: (0, 0))
    return pl.pallas_call(
        body, out_shape=(jax.ShapeDtypeStruct((T, D), F32), jax.ShapeDtypeStruct((T, D), BF16)), grid=(T // tm,),
        in_specs=[row, vec, row, vec], out_specs=(row, row),
        compiler_params=_params(("parallel",)), name=name)(a, g_post, h, g_pre)


def _loss_head(a, g, h, target, *, name):
    T, D = a.shape
    tm = _tile(T, 512)

    def body(a_ref, g_ref, h_ref, t_ref, dy_ref, da_ref, dg_ref, loss_ref):
        i = pl.program_id(0)
        a = a_ref[...]
        r = _rstd(a)
        ah = a * r
        y = h_ref[...] + ah * g_ref[...]
        rows = i * tm + lax.broadcasted_iota(jnp.int32, (tm, 1), 0)
        err = jnp.where(rows >= BLOCK, y - t_ref[...], 0.0)
        dy = err / D
        dy_ref[...] = dy
        dah = dy * g_ref[...]
        da_ref[...] = (r * (dah - ah * jnp.mean(dah * ah, axis=-1, keepdims=True))).astype(da_ref.dtype)
        part = jnp.sum(jnp.sum(err * err, axis=1, keepdims=True), axis=0, keepdims=True)

        @pl.when(i == 0)
        def _():
            loss_ref[...] = jnp.zeros_like(loss_ref)
            dg_ref[...] = jnp.zeros_like(dg_ref)

        loss_ref[...] += jnp.broadcast_to(part, loss_ref.shape)
        dg_ref[...] += jnp.sum(dy * ah, axis=0, keepdims=True)

    row = pl.BlockSpec((tm, D), lambda i: (i, 0))
    vec = pl.BlockSpec((1, D), lambda i: (0, 0))
    return pl.pallas_call(
        body, out_shape=(jax.ShapeDtypeStruct((T, D), F32), jax.ShapeDtypeStruct((T, D), BF16),
                         jax.ShapeDtypeStruct((1, D), F32), jax.ShapeDtypeStruct((8, 128), F32)),
        grid=(T // tm,),
        in_specs=[row, vec, row, row],
        out_specs=(row, row, vec, pl.BlockSpec((8, 128), lambda i: (0, 0))),
        compiler_params=_params(("arbitrary",)), name=name)(a, g, h, target)


def _rms_bwd(x, g, dy, res, *, out_dtype, dy2=None, then=None, split_head=False, ex=None, name):
    T, D = x.shape
    tm = BLOCK if split_head else _tile(T, 512)
    assert not (split_head and then is not None)
    has_res = res is not None
    has_dy2 = 2 if dy2 is not None else 0
    n_in = 3 + has_dy2 + has_res + (2 if then is not None else 0)
    n_out = 2 + (2 if then is not None else 0) + split_head

    def pull_back(x, g, dy):
        r = _rstd(x)
        xh = x * r
        dxh = dy * g
        return r * (dxh - xh * jnp.mean(dxh * xh, axis=-1, keepdims=True)), jnp.sum(dy * xh, axis=0, keepdims=True)

    def body(*refs):
        ins, outs = refs[:n_in], refs[n_in:]
        i = pl.program_id(0)

        @pl.when(i == 0)
        def _():
            for ref in outs[1::2]:
                ref[...] = jnp.zeros_like(ref)

        dy_all = ins[2][...].astype(F32)
        if has_dy2:
            dy_all = dy_all + lax.dot_general(ins[3][...], ins[4][...], NT, preferred_element_type=F32)
        dx, dg = pull_back(ins[0][...], ins[1][...], dy_all)
        if has_res:
            dx = dx + ins[3 + has_dy2][...]
        if split_head:
            @pl.when(i == 0)
            def _():
                outs[2][...] = dx.astype(outs[2].dtype)

            @pl.when(i > 0)
            def _():
                outs[0][...] = dx.astype(outs[0].dtype)
        else:
            outs[0][...] = dx.astype(outs[0].dtype)
        outs[1][...] += dg
        if then is not None:
            dx2, dg2 = pull_back(ins[n_in - 2][...], ins[n_in - 1][...], dx)
            outs[2][...] = dx2.astype(outs[2].dtype)
            outs[3][...] += dg2

    row = pl.BlockSpec((tm, D), lambda i: (i, 0))
    vec = pl.BlockSpec((1, D), lambda i: (0, 0))
    ins = [x, g, dy] + (list(dy2) if has_dy2 else []) + ([res] if has_res else []) + (list(then) if then is not None else [])
    dy2_specs = ([pl.BlockSpec((tm, dy2[0].shape[1]), lambda i: (i, 0)), pl.BlockSpec(dy2[1].shape, lambda i: (0, 0))]
                 if has_dy2 else [])
    in_specs = [row, vec, row] + dy2_specs + ([row] if has_res else []) + ([row, vec] if then is not None else [])
    out_shape = [jax.ShapeDtypeStruct((T, D), out_dtype), jax.ShapeDtypeStruct((1, D), F32)]
    out_specs = [row, vec]
    if then is not None:
        out_shape += [jax.ShapeDtypeStruct((T, D), BF16), jax.ShapeDtypeStruct((1, D), F32)]
        out_specs += [row, vec]
    if split_head:
        out_shape[0] = jax.ShapeDtypeStruct((T - BLOCK, D), out_dtype)
        out_specs[0] = pl.BlockSpec((BLOCK, D), lambda i: (jnp.maximum(i - 1, 0), 0))
        out_shape.append(jax.ShapeDtypeStruct((BLOCK, D), out_dtype))
        out_specs.append(pl.BlockSpec((BLOCK, D), lambda i: (0, 0)))
    grid = (T // tm,)
    body, x_in, x_in_specs, x_out, x_out_specs, x_scr = _carry(ex, grid, n_in, n_out, body)
    return pl.pallas_call(
        body, out_shape=(*out_shape, *x_out), grid=grid,
        in_specs=in_specs + x_in_specs, out_specs=(*out_specs, *x_out_specs), scratch_shapes=x_scr,
        compiler_params=_params(("arbitrary",)), name=name)(*ins, *x_in)


def _gate_up_swiglu(a, w_t, *, name):
    T, D = a.shape
    F = w_t.shape[0] // 2
    tm = _tile(T, 1408, BLOCK)
    n = _tile(F, 256, BLOCK)

    def body(a_ref, wg_ref, wu_ref, g_ref, u_ref, o_ref, ot_ref):
        x = a_ref[...]
        g = lax.dot_general(x, wg_ref[...], NT, preferred_element_type=F32)
        u = lax.dot_general(x, wu_ref[...], NT, preferred_element_type=F32)
        g16, u16 = g.astype(BF16), u.astype(BF16)
        g_ref[...] = g16
        u_ref[...] = u16
        gr = g16.astype(F32)
        act = gr / (1.0 + jnp.exp(-gr)) * u16.astype(F32)
        o_ref[...] = act.astype(o_ref.dtype)
        ot_ref[...] = act.T.astype(ot_ref.dtype)

    tile = pl.BlockSpec((tm, n), lambda i, j: (i, j))
    shp = jax.ShapeDtypeStruct((T, F), BF16)
    return pl.pallas_call(
        body, out_shape=(shp, shp, shp, jax.ShapeDtypeStruct((F, T), BF16)), grid=(T // tm, F // n),
        in_specs=[pl.BlockSpec((tm, D), lambda i, j: (i, 0)),
                  pl.BlockSpec((n, D), lambda i, j: (j, 0)),
                  pl.BlockSpec((n, D), lambda i, j: (j + F // n, 0))],
        out_specs=(tile, tile, tile, pl.BlockSpec((n, tm), lambda i, j: (j, i))),
        compiler_params=_params(("parallel", "parallel")), name=name)(a, w_t, w_t)


def _d_act_swiglu(dff, w_down, gate, up, *, name):
    T, D = dff.shape
    F = w_down.shape[0]
    tm = _tile(T, 384)
    chunk = 768
    assert F % BLOCK == 0

    def body(d_ref, w_ref, g_ref, u_ref, o_ref):
        dy = d_ref[...]
        for c in range(0, F, chunk):
            e = min(c + chunk, F)
            d = lax.dot_general(dy, w_ref[c:e, :], NT, preferred_element_type=F32)
            g = g_ref[:, c:e].astype(F32)
            u = u_ref[:, c:e].astype(F32)
            sg = 1.0 / (1.0 + jnp.exp(-g))
            o_ref[:, c:e] = (d * u * (sg * (1.0 + g * (1.0 - sg)))).astype(o_ref.dtype)
            o_ref[:, F + c:F + e] = (d * (g * sg)).astype(o_ref.dtype)

    row = pl.BlockSpec((tm, F), lambda i: (i, 0))
    return pl.pallas_call(
        body, out_shape=jax.ShapeDtypeStruct((T, 2 * F), BF16), grid=(T // tm,),
        in_specs=[pl.BlockSpec((tm, D), lambda i: (i, 0)), pl.BlockSpec((F, D), lambda i: (0, 0)), row, row],
        out_specs=pl.BlockSpec((tm, 2 * F), lambda i: (i, 0)),
        compiler_params=_params(("parallel",)), name=name)(dff, w_down, gate, up)


def _fox_gates_fwd(f_t, b, *, name):
    H, T = f_t.shape
    nb = T // BLOCK

    def body(f_ref, b_ref, col_ref):
        f = f_ref[...] + b_ref[...]
        ls = jnp.minimum(f, 0.0) - jnp.log(1.0 + jnp.exp(-jnp.abs(f)))
        t = lax.broadcasted_iota(jnp.int32, (H, T), 1)
        ls = jnp.where(t >= PAD_ROWS, ls, 0.0)
        upper = (lax.broadcasted_iota(jnp.int32, (BLOCK, BLOCK), 0)
                 <= lax.broadcasted_iota(jnp.int32, (BLOCK, BLOCK), 1)).astype(F32)
        carry = jnp.zeros((H, 1), F32)
        for blk in range(nb):
            seg = ls[:, blk * BLOCK:(blk + 1) * BLOCK]
            pre = jnp.dot(seg, upper, precision=HIGHEST, preferred_element_type=F32) + carry
            key_gate = jnp.where(t[:, blk * BLOCK:(blk + 1) * BLOCK] >= PAD_ROWS, pre, -NEG)
            terms = list(_split3(pre)) + list(_split3(key_gate))
            col_ref[blk * BLOCK:(blk + 1) * BLOCK, :] = jnp.concatenate(
                terms + [jnp.zeros((BLOCK - len(terms) * H, BLOCK), F32)], axis=0).T.astype(col_ref.dtype)
            carry = pre[:, BLOCK - 1:BLOCK]

    vm = pl.BlockSpec(memory_space=pltpu.VMEM)
    return pl.pallas_call(
        body, out_shape=jax.ShapeDtypeStruct((T, BLOCK), BF16),
        in_specs=[vm, vm], out_specs=vm,
        compiler_params=_params(), name=name)(f_t, b)


def _fox_gates_bwd(dcq, dck, f_t, b, *, name):
    H, T = f_t.shape
    nb = T // BLOCK

    def body(dq_ref, d_ref, f_ref, b_ref, df_ref, db_ref):
        lower = (lax.broadcasted_iota(jnp.int32, (BLOCK, BLOCK), 0)
                 >= lax.broadcasted_iota(jnp.int32, (BLOCK, BLOCK), 1)).astype(F32)
        carry = jnp.zeros((H, 1), F32)
        for blk in range(nb - 1, -1, -1):
            seg = dq_ref[:, blk * BLOCK:(blk + 1) * BLOCK] - d_ref[:, blk * BLOCK:(blk + 1) * BLOCK]
            suf = jnp.dot(seg, lower, precision=HIGHEST, preferred_element_type=F32) + carry
            df_ref[:, blk * BLOCK:(blk + 1) * BLOCK] = suf
            carry = suf[:, 0:1]
        f = f_ref[...] + b_ref[...]
        t = lax.broadcasted_iota(jnp.int32, (H, T), 1)
        df = jnp.where(t >= PAD_ROWS, df_ref[...] / (1.0 + jnp.exp(f)), 0.0)
        df_ref[...] = df
        db_ref[...] = jnp.sum(df, axis=1, keepdims=True)

    vm = pl.BlockSpec(memory_space=pltpu.VMEM)
    return pl.pallas_call(
        body, out_shape=(jax.ShapeDtypeStruct((H, T), F32), jax.ShapeDtypeStruct((H, 1), F32)),
        in_specs=[vm, vm, vm, vm], out_specs=(vm, vm),
        compiler_params=_params(), name=name)(dcq, dck, f_t, b)


def _fox_lanes(parity):
    base = HEAD_DIM * (1 - parity)
    return base, base + 3


def _split3(c):
    hi = c.astype(BF16).astype(F32)
    r = c - hi
    mid = r.astype(BF16).astype(F32)
    lo = (r - mid).astype(BF16).astype(F32)
    return hi, mid, lo


def _lanes(lane, parity, data, start, terms, ones_at=None, fill=1.0):
    out = jnp.zeros((), F32) if ones_at is None else jnp.where((lane >= ones_at) & (lane < ones_at + 3), fill, 0.0)
    for i, t in enumerate(terms):
        out = jnp.where(lane == start + i, t, out)
    return jnp.where(lane // HEAD_DIM == parity, data, out)


def _fox_prep(proj, cum_col, *, name):
    T = proj.shape[0]
    tm = _tile(T, 1408, BLOCK)
    nt = T // tm
    H = FOX_HEADS
    lanes = 2 * HEAD_DIM
    first = (proj.shape[1] - 3 * H * HEAD_DIM) // lanes

    def body(q_ref, k_ref, v_ref, c_ref, qa_ref, ka_ref, va_ref):
        p = pl.program_id(0)
        i = pl.program_id(1)
        lane = lax.broadcasted_iota(jnp.int32, (1, lanes), 1)
        src = lax.broadcasted_iota(jnp.int32, (lanes, lanes), 0)
        dst = lax.broadcasted_iota(jnp.int32, (lanes, lanes), 1)
        q2 = q_ref[...].astype(F32) * SCALE
        k2 = k_ref[...].astype(F32)
        v2 = v_ref[...].astype(F32)
        gates = c_ref[...]
        def placed(h, first_term, start):
            pick = ((src % FOX_HEADS == h) & (src // FOX_HEADS - first_term == dst - start)
                    & (dst >= start) & (dst < start + 3))
            return jnp.dot(gates, pick.astype(BF16), preferred_element_type=F32)

        moved = [(placed(2 * p + e, 0, _fox_lanes(e)[1]), placed(2 * p + e, 3, _fox_lanes(e)[0])) for e in range(2)]
        for e in range(2):
            kc, qc = _fox_lanes(e)
            own = lane // HEAD_DIM == e
            minus = jnp.where((lane >= kc) & (lane < kc + 3), -1.0, 0.0)
            ones_q = jnp.where((lane >= qc) & (lane < qc + 3), 1.0, 0.0)
            ones_k = jnp.where((lane >= kc) & (lane < kc + 3), 1.0, 0.0)
            qa_ref[e] = jnp.where(own, q2, moved[e][0] + minus).astype(BF16)
            ka_ref[e] = jnp.where(own, k2, moved[e][1] + ones_q).astype(BF16)
            va_ref[e] = jnp.where(own, v2, ones_k).astype(BF16)

    pairs = FOX_GROUP // 2

    def col(part):
        return pl.BlockSpec((tm, lanes),
                            lambda p, i: (i, first + 3 * pairs * (p // pairs) + part * pairs + p % pairs))

    out = pl.BlockSpec((2, tm, lanes), lambda p, i: (p, i, 0))
    shp = jax.ShapeDtypeStruct((H, T, lanes), BF16)
    return pl.pallas_call(
        body, out_shape=(shp, shp, shp), grid=(H // 2, nt),
        in_specs=[col(0), col(1), col(2), pl.BlockSpec((tm, lanes), lambda p, i: (i, 0))],
        out_specs=(out, out, out),
        compiler_params=_params(("parallel", "parallel")), name=name)(proj, proj, proj, cum_col)


def _fox_fwd(q_aug, k_aug, v_aug, mix, *, ex=None, name):
    H, T, lanes = q_aug.shape
    tq = FOX_TILE
    nq = T // tq
    G = FOX_HEADS

    def body(q_ref, k_ref, v_ref, mix_ref, o_ref, lse_ref, m_scr, acc_scr):
        i = pl.program_id(1)
        m_scr[...] = jnp.full(m_scr.shape, NEG, F32)
        acc_scr[...] = jnp.zeros(acc_scr.shape, F32)

        def step(kb, diag):
            off = pl.multiple_of(kb * tq, tq)
            s_t = [lax.dot_general(k_ref[g, pl.ds(off, tq), :], q_ref[g], NT, preferred_element_type=F32)
                   for g in range(G)]
            if diag:
                r = lax.broadcasted_iota(jnp.int32, (tq, tq), 0)
                c = lax.broadcasted_iota(jnp.int32, (tq, tq), 1)
                s_t = [jnp.where(c >= r, s, NEG) for s in s_t]
            m_prev = [m_scr[g] for g in range(G)]
            m_new = [jnp.maximum(m_prev[g], jnp.max(s_t[g], axis=0, keepdims=True)) for g in range(G)]
            p_t = [jnp.exp(s_t[g] - m_new[g]).astype(BF16) for g in range(G)]
            pv = [lax.dot_general(v_ref[g, pl.ds(off, tq), :], p_t[g], TN, preferred_element_type=F32)
                  for g in range(G)]
            for g in range(G):
                acc_scr[g] = jnp.exp(m_prev[g] - m_new[g]) * acc_scr[g] + pv[g]
                m_scr[g] = m_new[g]

        def loop_body(kb, carry):
            step(kb, False)
            return carry

        lax.fori_loop(0, i, loop_body, 0)
        step(i, True)
        lane = lax.broadcasted_iota(jnp.int32, (tq, lanes), 1)
        outs = []
        for g in range(G):
            ones = _fox_lanes(g % 2)[0]
            acc = acc_scr[g]
            lse_ref[g] = m_scr[g] + jnp.log(acc[ones:ones + 1, :])
            acc_t = acc.T
            outs.append(acc_t / acc_t[:, ones:ones + 1])
        for pair in range(G // 2):
            o_ref[:, pair * lanes:(pair + 1) * lanes] = jnp.where(
                lane < HEAD_DIM, outs[2 * pair], outs[2 * pair + 1]).astype(o_ref.dtype)

    blk = pl.BlockSpec((G, tq, lanes), lambda h, i: (h, i, 0))
    full = pl.BlockSpec((G, T, lanes), lambda h, i: (h, 0, 0))
    grid = (H // G, nq)
    first = mix.shape[1] // (G * HEAD_DIM) - H // G
    body, x_in, x_in_specs, x_out, x_out_specs, x_scr = _carry(ex, grid, 4, 2, body)
    return pl.pallas_call(
        body,
        out_shape=(jax.ShapeDtypeStruct(mix.shape, mix.dtype), jax.ShapeDtypeStruct((H, nq, 1, tq), F32), *x_out),
        grid=grid,
        in_specs=[blk, full, full, pl.BlockSpec(memory_space=pl.ANY)] + x_in_specs,
        out_specs=(pl.BlockSpec((tq, G * HEAD_DIM), lambda h, i: (i, first + h)),
                   pl.BlockSpec((G, None, 1, tq), lambda h, i: (h, i, 0, 0)), *x_out_specs),
        input_output_aliases={3: 0},
        scratch_shapes=[pltpu.VMEM((G, 1, tq), F32), pltpu.VMEM((G, lanes, tq), F32)] + x_scr,
        compiler_params=_params(("arbitrary", "arbitrary")), name=name)(q_aug, k_aug, v_aug, mix, *x_in)


def _fox_prep_bwd(dmix, mix, *, name):
    T = dmix.shape[0]
    H = FOX_HEADS
    tm = _tile(T, 1408, BLOCK)
    lanes = 2 * HEAD_DIM
    first = mix.shape[1] // lanes - H // 2

    def body(d_ref, o_ref, da_ref):
        lane = lax.broadcasted_iota(jnp.int32, (1, lanes), 1)
        d2 = d_ref[...].astype(F32)
        prod = d2 * o_ref[...].astype(F32)
        for e in range(2):
            delta = jnp.sum(jnp.where(lane // HEAD_DIM == e, prod, 0.0), axis=1, keepdims=True)
            da_ref[e] = _lanes(lane, e, d2, _fox_lanes(e)[0], _split3(-delta)).astype(BF16)

    pair = pl.BlockSpec((tm, lanes), lambda p, i: (i, first + p))
    return pl.pallas_call(
        body, out_shape=jax.ShapeDtypeStruct((H, T, lanes), BF16), grid=(H // 2, T // tm),
        in_specs=[pair, pair],
        out_specs=pl.BlockSpec((2, tm, lanes), lambda p, i: (p, i, 0)),
        compiler_params=_params(("parallel", "parallel")), name=name)(dmix, mix)


def _fox_bwd(q_aug, k_aug, v_aug, do_aug, lse_row, dproj, *, ex=None, name):
    H, T, lanes = q_aug.shape
    tq = FOX_TILE
    nq = T // tq
    G = FOX_GROUP

    def side_by_side(tiles, scale=None):
        lane = lax.broadcasted_iota(jnp.int32, tiles[0].shape, 1)
        out = [jnp.where(lane < HEAD_DIM, tiles[2 * p], tiles[2 * p + 1]) for p in range(G // 2)]
        out = jnp.concatenate(out, axis=1)
        return out if scale is None else out * scale

    def body(q_ref, k_ref, v_ref, do_ref, lse_ref, dproj_in, out_ref, dcq_ref, dck_ref, dk_acc, dv_acc, dq_ref):
        j = pl.program_id(1)

        @pl.when(j == 0)
        def _():
            dq_ref[...] = jnp.zeros(dq_ref.shape, F32)
            dcq_ref[...] = jnp.zeros(dcq_ref.shape, F32)

        dk_acc[...] = jnp.zeros(dk_acc.shape, F32)
        dv_acc[...] = jnp.zeros(dv_acc.shape, F32)

        def step(qb, diag):
            off = pl.multiple_of(qb * tq, tq)
            heads = range(G)
            qa = [q_ref[g, pl.ds(off, tq), :] for g in heads]
            da = [do_ref[g, pl.ds(off, tq), :] for g in heads]
            s_t = [lax.dot_general(k_ref[g], qa[g], NT, preferred_element_type=F32) for g in heads]
            dp_t = [lax.dot_general(v_ref[g], da[g], NT, preferred_element_type=F32) for g in heads]
            p_t = [jnp.exp(s_t[g] - lse_ref[g, qb]) for g in heads]
            if diag:
                r = lax.broadcasted_iota(jnp.int32, (tq, tq), 0)
                c = lax.broadcasted_iota(jnp.int32, (tq, tq), 1)
                p_t = [jnp.where(c >= r, p, 0.0) for p in p_t]
            dsb = [(p_t[g] * dp_t[g]).astype(BF16) for g in heads]
            dv = [jnp.dot(p_t[g].astype(BF16), da[g], preferred_element_type=F32) for g in heads]
            dk = [jnp.dot(dsb[g], qa[g], preferred_element_type=F32) for g in heads]
            dq = [lax.dot_general(k_ref[g], dsb[g], TN, preferred_element_type=F32) for g in heads]
            for g in heads:
                dv_acc[g] += dv[g]
                dk_acc[g] += dk[g]
                dq_ref[g, qb] += dq[g]
                dcq_ref[g, qb] += jnp.sum(dsb[g].astype(F32), axis=0, keepdims=True)

        step(j, True)

        def loop_body(qb, carry):
            step(qb, False)
            return carry

        lax.fori_loop(j + 1, nq, loop_body, 0)
        dk = [dk_acc[g] for g in range(G)]
        out_ref[:, 0:wide] = side_by_side([dq_ref[g, j].T for g in range(G)], SCALE).astype(out_ref.dtype)
        out_ref[:, wide:2 * wide] = side_by_side(dk).astype(out_ref.dtype)
        out_ref[:, 2 * wide:3 * wide] = side_by_side([dv_acc[g] for g in range(G)]).astype(out_ref.dtype)
        for g in range(G):
            kc = _fox_lanes(g % 2)[0]
            dck_ref[g] = -dk[g].T[kc:kc + 1, :]

    blk = pl.BlockSpec((G, tq, lanes), lambda h, j: (h, j, 0))
    full = pl.BlockSpec((G, T, lanes), lambda h, j: (h, 0, 0))
    wide = G * HEAD_DIM
    first = dproj.shape[1] // (3 * wide) - H // G
    grid = (H // G, nq)
    body, x_in, x_in_specs, x_out, x_out_specs, x_scr = _carry(ex, grid, 6, 3, body)
    rows = jax.ShapeDtypeStruct((H, nq, 1, tq), F32)
    all_rows = pl.BlockSpec((G, nq, 1, tq), lambda h, j: (h, 0, 0, 0))
    return pl.pallas_call(
        body,
        out_shape=(jax.ShapeDtypeStruct(dproj.shape, dproj.dtype), rows, rows, *x_out),
        grid=grid,
        in_specs=[full, blk, blk, full, all_rows, pl.BlockSpec(memory_space=pl.ANY)] + x_in_specs,
        out_specs=(pl.BlockSpec((tq, 3 * wide), lambda h, j: (j, first + h)), all_rows,
                   pl.BlockSpec((G, None, 1, tq), lambda h, j: (h, j, 0, 0)), *x_out_specs),
        input_output_aliases={5: 0},
        scratch_shapes=[pltpu.VMEM((G, tq, lanes), F32), pltpu.VMEM((G, tq, lanes), F32),
                        pltpu.VMEM((G, nq, lanes, tq), F32)] + x_scr,
        compiler_params=_params(("arbitrary", "arbitrary")), name=name,
    )(q_aug, k_aug, v_aug, do_aug, lse_row, dproj, *x_in)


def _t5_bucket_np(d):
    n = np.maximum(d, 0).astype(np.int32)
    max_exact = N_BUCKETS // 2
    nf = np.maximum(n, 1).astype(np.float32)
    large = max_exact + (np.log(nf / max_exact) / math.log(MAX_DISTANCE / max_exact)
                         * (N_BUCKETS - max_exact)).astype(np.int32)
    large = np.minimum(large, N_BUCKETS - 1)
    return np.where(n < max_exact, n, large)


def _bucket_onehots():
    k = np.arange(BLOCK)[:, None]
    q = np.arange(BLOCK)[None, :]
    eye = np.eye(N_BUCKETS, dtype=np.float32)
    cur = eye[_t5_bucket_np(q - k).reshape(-1)]
    prev = eye[_t5_bucket_np(BLOCK + q - k).reshape(-1)]
    return cur, prev


SWA_K_COL = SWA_Q_HEADS * HEAD_DIM // (2 * HEAD_DIM)
SWA_V_COL = SWA_K_COL + 1


def _swa_terms(raw, bc, bp, far, sink, n):
    k = lax.broadcasted_iota(jnp.int32, (BLOCK, BLOCK), 0)
    q = lax.broadcasted_iota(jnp.int32, (BLOCK, BLOCK), 1)
    never = 2 * BLOCK
    s_c = raw[0] + bc
    s_p = raw[1] + bp
    s_m = raw[2] + jnp.where(n == 1, bp, far)
    s_c = jnp.where((k <= q) & (k >= jnp.where(n >= 1, 0, PAD_ROWS)), s_c, NEG)
    s_p = jnp.where(k > q + jnp.where(n >= 2, 0, never), s_p, NEG)
    s_m = jnp.where(k >= jnp.where(n >= 1, PAD_ROWS, never), s_m, NEG)
    m = jnp.maximum(jnp.maximum(jnp.max(s_c, axis=0, keepdims=True), jnp.max(s_p, axis=0, keepdims=True)),
                    jnp.maximum(jnp.max(s_m, axis=0, keepdims=True), sink))
    e = [jnp.exp(s_c - m), jnp.exp(s_p - m), jnp.exp(s_m - m)]
    e_s = jnp.exp(sink - m)
    l = (jnp.sum(e[0], axis=0, keepdims=True) + jnp.sum(e[1], axis=0, keepdims=True)
         + jnp.sum(e[2], axis=0, keepdims=True) + e_s)
    return e, e_s, l


SWA_STEP = 3


def _swa_specs():
    R = SWA_STEP

    def window(col):
        return ([pl.BlockSpec((BLOCK, BLOCK), lambda s, w=w: (jnp.maximum(R * s - 1 + w, 0), col)) for w in range(R + 1)]
                + [pl.BlockSpec((BLOCK, BLOCK), lambda s: (0, col))])

    qblk = pl.BlockSpec((R * BLOCK, SWA_Q_HEADS * HEAD_DIM), lambda s: (s, 0))
    bias = pl.BlockSpec((SWA_Q_HEADS, BLOCK, BLOCK), lambda s: (0, 0, 0))
    smem = pl.BlockSpec(memory_space=pltpu.SMEM)
    return qblk, window(SWA_K_COL), window(SWA_V_COL), bias, smem


def _swa_own_kv(tile_ref, kv):
    lane = lax.broadcasted_iota(jnp.int32, (BLOCK, 2 * HEAD_DIM), 1)
    t = tile_ref[...].astype(F32)
    return jnp.where(lane // HEAD_DIM == kv, t, pltpu.roll(t, HEAD_DIM, 1)).astype(BF16)


def _swa_fwd(proj, bc, bp, far, sinks, *, name):
    T = proj.shape[0]
    nb = T // BLOCK
    G = SWA_GROUP
    Hq = SWA_Q_HEADS
    lanes = 2 * HEAD_DIM

    R = SWA_STEP
    assert nb % R == 0

    def body(*refs):
        q_ref, k_refs, v_refs = refs[0], refs[1:R + 3], refs[R + 3:2 * R + 5]
        bc_ref, bp_ref, far_ref, sink_ref, o_ref = refs[2 * R + 5:]
        s = pl.program_id(0)
        lane = lax.broadcasted_iota(jnp.int32, (BLOCK, lanes), 1)
        kvs = range(SWA_KV_HEADS)
        kk = [[_swa_own_kv(ref, kv) for ref in k_refs] for kv in kvs]
        vv = [[_swa_own_kv(ref, kv) for ref in v_refs] for kv in kvs]
        chains = [(r, h) for r in range(R) for h in range(Hq)]
        tiles = lambda r: (r + 1, r, R + 1)
        q2 = {(r, pair): q_ref[r * BLOCK:(r + 1) * BLOCK, pair * lanes:(pair + 1) * lanes].astype(F32) * SCALE
              for r in range(R) for pair in range(Hq // 2)}
        qm = {c: jnp.where(lane // HEAD_DIM == c[1] % 2, q2[c[0], c[1] // 2], 0.0).astype(BF16) for c in chains}
        raw = {c: [lax.dot_general(kk[c[1] // G][w], qm[c], NT, preferred_element_type=F32) for w in tiles(c[0])]
               for c in chains}
        terms = {c: _swa_terms(raw[c], bc_ref[c[1]], bp_ref[c[1]], far_ref[c[1]], sink_ref[c[1]], R * s + c[0])
                 for c in chains}
        o_t = {c: sum(lax.dot_general(vv[c[1] // G][w], terms[c][0][b].astype(BF16), TN, preferred_element_type=F32)
                      for b, w in enumerate(tiles(c[0]))) for c in chains}
        outs = {c: (o_t[c] / terms[c][2]).T for c in chains}
        for r in range(R):
            for pair in range(Hq // 2):
                o_ref[r * BLOCK:(r + 1) * BLOCK, pair * lanes:(pair + 1) * lanes] = jnp.where(
                    lane < HEAD_DIM, outs[r, 2 * pair], outs[r, 2 * pair + 1]).astype(o_ref.dtype)

    qblk, keys, vals, bias, smem = _swa_specs()
    return pl.pallas_call(
        body, out_shape=jax.ShapeDtypeStruct((T, D_MODEL), BF16), grid=(nb // R,),
        in_specs=[qblk] + keys + vals + [bias, bias, smem, smem],
        out_specs=qblk,
        compiler_params=_params(("parallel",)), name=name,
    )(proj, *([proj] * (2 * R + 4)), bc, bp, far, sinks)


def _swa_bwd(proj, dmix, bc, bp, far, sinks, *, ex=None, name):
    T, width = proj.shape
    nb = T // BLOCK
    G = SWA_GROUP
    Hq = SWA_Q_HEADS
    lanes = 2 * HEAD_DIM
    qw = Hq * HEAD_DIM
    own_w = qw + 2 * lanes

    R = SWA_STEP
    assert nb % R == 0
    n_in = 2 * R + 10

    def body(*refs):
        q_ref, k_refs, v_refs = refs[0], refs[1:R + 3], refs[R + 3:2 * R + 5]
        do_ref, bc_ref, bp_ref, far_ref, sink_ref = refs[2 * R + 5:n_in]
        dp_ref, dbc_ref, dbp_ref, dbf_ref, dsk_ref, dk_acc, dv_acc = refs[n_in:]
        s = pl.program_id(0)

        @pl.when(s == 0)
        def _():
            for ref in (dk_acc, dv_acc, dbc_ref, dbp_ref, dbf_ref, dsk_ref):
                ref[...] = jnp.zeros(ref.shape, F32)

        lane = lax.broadcasted_iota(jnp.int32, (BLOCK, lanes), 1)
        kvs = range(SWA_KV_HEADS)
        kk = [[_swa_own_kv(ref, kv) for ref in k_refs] for kv in kvs]
        vv = [[_swa_own_kv(ref, kv) for ref in v_refs] for kv in kvs]
        chains = [(r, h) for r in range(R) for h in range(Hq)]
        blocks = range(3)
        tiles = lambda r: (r + 1, r, R + 1)
        sub = lambda ref, r, pair: ref[r * BLOCK:(r + 1) * BLOCK, pair * lanes:(pair + 1) * lanes]
        q2 = {(r, pair): sub(q_ref, r, pair).astype(F32) * SCALE for r in range(R) for pair in range(Hq // 2)}
        d2 = {(r, pair): sub(do_ref, r, pair) for r in range(R) for pair in range(Hq // 2)}
        own = [lane // HEAD_DIM == half for half in range(2)]
        qm = {c: jnp.where(own[c[1] % 2], q2[c[0], c[1] // 2], 0.0).astype(BF16) for c in chains}
        dom = {c: jnp.where(own[c[1] % 2], d2[c[0], c[1] // 2], jnp.zeros_like(d2[0, 0])) for c in chains}
        raw = {c: [lax.dot_general(kk[c[1] // G][w], qm[c], NT, preferred_element_type=F32) for w in tiles(c[0])]
               for c in chains}
        dp = {c: [lax.dot_general(vv[c[1] // G][w], dom[c], NT, preferred_element_type=F32) for w in tiles(c[0])]
              for c in chains}
        p, ds16 = {}, {}
        for c in chains:
            r, h = c
            n = R * s + r
            e, e_s, l = _swa_terms(raw[c], bc_ref[h], bp_ref[h], far_ref[h], sink_ref[h], n)
            inv = 1.0 / l
            ph = [e[b] * inv for b in blocks]
            delta = sum(jnp.sum(ph[b] * dp[c][b], axis=0, keepdims=True) for b in blocks)
            ds = [ph[b] * (dp[c][b] - delta) for b in blocks]
            dsk_ref[h] += -(e_s * inv) * delta
            dbc_ref[h] += ds[0]
            dbp_ref[h] += ds[1] + jnp.where(n == 1, ds[2], 0.0)
            dbf_ref[h] += jnp.where(n >= 2, ds[2], 0.0)
            p[c] = [x.astype(BF16) for x in ph]
            ds16[c] = [x.astype(BF16) for x in ds]
        dq_t = {c: sum(lax.dot_general(kk[c[1] // G][w], ds16[c][b], TN, preferred_element_type=F32)
                       for b, w in enumerate(tiles(c[0]))) for c in chains}
        group = [range(kv * G, (kv + 1) * G) for kv in kvs]
        dk = {(r, kv): [sum(jnp.dot(ds16[r, h][b], qm[r, h], preferred_element_type=F32) for h in group[kv])
                        for b in blocks] for r in range(R) for kv in kvs}
        dv = {(r, kv): [sum(jnp.dot(p[r, h][b], dom[r, h], preferred_element_type=F32) for h in group[kv])
                        for b in blocks] for r in range(R) for kv in kvs}
        for r in range(R):
            n = R * s + r
            rows = pl.ds(pl.multiple_of(n * BLOCK, BLOCK), BLOCK)
            prev_rows = pl.ds(pl.multiple_of(jnp.maximum(n - 1, 0) * BLOCK, BLOCK), BLOCK)
            for pair in range(Hq // 2):
                dp_ref[rows, pair * lanes:(pair + 1) * lanes] = (jnp.where(
                    lane < HEAD_DIM, dq_t[r, 2 * pair].T, dq_t[r, 2 * pair + 1].T) * SCALE).astype(dp_ref.dtype)
            for acc, ref in ((dk, dk_acc), (dv, dv_acc)):
                tot = [[a + pltpu.roll(a, HEAD_DIM, 1) for a in acc[r, kv]] for kv in kvs]
                both = [jnp.where(lane < HEAD_DIM, tot[0][b], tot[1][b]) for b in blocks]
                ref[rows, :] += both[0]
                ref[prev_rows, :] += both[1]
                ref[0:BLOCK, :] += both[2]

        @pl.when(s == nb // R - 1)
        def _():
            dp_ref[:, qw:qw + lanes] = dk_acc[...].astype(dp_ref.dtype)
            dp_ref[:, qw + lanes:own_w] = dv_acc[...].astype(dp_ref.dtype)

    qblk, keys, vals, bias, smem = _swa_specs()
    dsk = pl.BlockSpec((Hq, 1, BLOCK), lambda s: (0, 0, 0))
    grid = (nb // R,)
    body, x_in, x_in_specs, x_out, x_out_specs, x_scr = _carry(ex, grid, n_in, 5, body)
    tile = jax.ShapeDtypeStruct((Hq, BLOCK, BLOCK), F32)
    return pl.pallas_call(
        body,
        out_shape=(jax.ShapeDtypeStruct((T, width), BF16), tile, tile, tile,
                   jax.ShapeDtypeStruct((Hq, 1, BLOCK), F32), *x_out),
        grid=grid,
        in_specs=[qblk] + keys + vals + [qblk, bias, bias, smem, smem] + x_in_specs,
        out_specs=(pl.BlockSpec((T, own_w), lambda s: (0, 0)), bias, bias, bias, dsk, *x_out_specs),
        scratch_shapes=[pltpu.VMEM((T, lanes), F32), pltpu.VMEM((T, lanes), F32)] + x_scr,
        compiler_params=_params(("arbitrary",)), name=name,
    )(proj, *([proj] * (2 * R + 4)), dmix, bc, bp, far, sinks, *x_in)


def _bias_tiles(tab_t, oh_cur_t, oh_prev_t, *, name):
    Hq = tab_t.shape[0]

    def body(t_ref, oc_ref, op_ref, bc_ref, bp_ref):
        bc_ref[...] = jnp.dot(t_ref[...], oc_ref[...], precision=HIGHEST, preferred_element_type=F32)
        bp_ref[...] = jnp.dot(t_ref[...], op_ref[...], precision=HIGHEST, preferred_element_type=F32)

    vm = pl.BlockSpec(memory_space=pltpu.VMEM)
    shp = jax.ShapeDtypeStruct((Hq, BLOCK * BLOCK), F32)
    bc, bp = pl.pallas_call(body, out_shape=(shp, shp), in_specs=[vm] * 3, out_specs=(vm, vm),
                            compiler_params=_params(), name=name)(tab_t, oh_cur_t, oh_prev_t)
    return bc.reshape(Hq, BLOCK, BLOCK), bp.reshape(Hq, BLOCK, BLOCK)


def _small_grads(dbc, dbp, dbf, dsk, oh_cur, oh_prev, *, ex=None, name):
    Hq = dbc.shape[0]

    def body(dbc_ref, dbp_ref, dbf_ref, dsk_ref, oc_ref, op_ref, tab_ref, sink_ref):
        tab = (jnp.dot(dbc_ref[...], oc_ref[...], precision=HIGHEST, preferred_element_type=F32)
               + jnp.dot(dbp_ref[...], op_ref[...], precision=HIGHEST, preferred_element_type=F32))
        far = jnp.sum(dbf_ref[...], axis=1, keepdims=True)
        last = lax.broadcasted_iota(jnp.int32, (Hq, N_BUCKETS), 1) == N_BUCKETS - 1
        tab_ref[...] = tab + jnp.where(last, far, 0.0)
        sink_ref[...] = jnp.sum(dsk_ref[...], axis=1, keepdims=True)

    vm = pl.BlockSpec(memory_space=pltpu.VMEM)
    body, x_in, x_in_specs, x_out, x_out_specs, x_scr = _carry(ex, (), 6, 2, body)
    return pl.pallas_call(
        body, out_shape=(jax.ShapeDtypeStruct((Hq, N_BUCKETS), F32), jax.ShapeDtypeStruct((Hq, 1), F32), *x_out),
        in_specs=[vm] * 6 + x_in_specs, out_specs=(vm, vm, *x_out_specs), scratch_shapes=x_scr,
        compiler_params=_params(), name=name,
    )(dbc.reshape(Hq, -1), dbp.reshape(Hq, -1), dbf.reshape(Hq, -1), dsk.reshape(Hq, -1), oh_cur, oh_prev, *x_in)


def _coords():
    return lax.axis_index("x"), lax.axis_index("y"), lax.axis_index("c")


class _Exchange:
    def __init__(self, inputs, out_shapes, scratch, start, finish):
        self.inputs, self.out_shapes, self.scratch, self.start, self.finish = inputs, out_shapes, scratch, start, finish


def _carry(ex, grid, n_in, n_out, body):
    if ex is None:
        return body, [], [], [], [], []
    ni, no = len(ex.inputs), len(ex.out_shapes)

    def at_step(which):
        cond = jnp.bool_(True)
        for axis, n in enumerate(grid):
            cond = cond & (pl.program_id(axis) == (0 if which == "first" else n - 1))
        return cond

    def wrapped(*refs):
        refs = list(refs)
        n_own_scr = len(refs) - (n_in + ni + n_out + no) - len(ex.scratch)
        own_in, side_in = refs[:n_in], refs[n_in:n_in + ni]
        own_out = refs[n_in + ni:n_in + ni + n_out]
        side_out = refs[n_in + ni + n_out:n_in + ni + n_out + no]
        rest = refs[n_in + ni + n_out + no:]
        own_scr, sems = rest[:n_own_scr], rest[n_own_scr:]

        @pl.when(at_step("first"))
        def _():
            ex.start(side_in, side_out, sems)

        body(*own_in, *own_out, *own_scr)

        @pl.when(at_step("last"))
        def _():
            ex.finish(side_in, side_out, sems)

    hbm = pl.BlockSpec(memory_space=pl.ANY)
    return wrapped, list(ex.inputs), [hbm] * ni, list(ex.out_shapes), [hbm] * no, list(ex.scratch)


def _gather_exchange(shards):
    nt = len(shards)

    def copies(ins, outs, sems):
        send_sems, recv_sems, local_sems = sems
        x, y, c = _coords()
        me, sibling = (x, y, c), (x, y, 1 - c)
        chips = [(1 - x, y), (x, 1 - y), (1 - x, 1 - y)]

        def slot(t, dev):
            return outs[t].at[4 * dev[0] + 2 * dev[1] + dev[2]]

        def copy(t, k, block, to, src=None):
            dst = slot(t, block)
            return pltpu.make_async_remote_copy(
                src_ref=dst if src is None else src, dst_ref=dst,
                send_sem=send_sems.at[t, k], recv_sem=recv_sems.at[t, k], device_id=to, device_id_type=MESH)

        mine = [pltpu.make_async_copy(ins[t], slot(t, me), local_sems.at[t]) for t in range(nt)]
        first = []
        for t in range(nt):
            first.append(copy(t, 0, me, sibling, src=ins[t]))
            first += [copy(t, 1 + j, me, (*chip, c), src=ins[t]) for j, chip in enumerate(chips)]
        return copy, mine, first, me, sibling, chips, c

    def start(ins, outs, sems):
        _, mine, first, *_ = copies(ins, outs, sems)
        for cp in mine + first:
            cp.start()

    def finish(ins, outs, sems):
        copy, mine, first, me, sibling, chips, c = copies(ins, outs, sems)
        passed = []
        for j, chip in enumerate(chips):
            for t in range(nt):
                copy(t, 1 + j, (*chip, c), me).wait_recv()
                cp = copy(t, 4 + j, (*chip, c), sibling)
                cp.start()
                passed.append(cp)
        for t in range(nt):
            copy(t, 0, sibling, me).wait_recv()
            for j, chip in enumerate(chips):
                copy(t, 4 + j, (*chip, 1 - c), me).wait_recv()
        for cp in first + passed:
            cp.wait_send()
        for cp in mine:
            cp.wait()

    return _Exchange(
        list(shards), [jax.ShapeDtypeStruct((N_DEV,) + s.shape, s.dtype) for s in shards],
        [pltpu.SemaphoreType.DMA((nt, 7)), pltpu.SemaphoreType.DMA((nt, 7)), pltpu.SemaphoreType.DMA((nt,))],
        start, finish)


def _swap_exchange(arrays, n_slices, copies):
    nt = len(arrays)

    def start(ins, outs, sems):
        for cp in copies(ins, outs, sems):
            cp.start()

    def finish(ins, outs, sems):
        sends = copies(ins, outs, sems)
        for cp in sends:
            cp.wait_recv()
        for cp in sends:
            cp.wait_send()

    return _Exchange(
        list(arrays), [jax.ShapeDtypeStruct((n_slices,) + a.shape[1:], a.dtype) for a in arrays],
        [pltpu.SemaphoreType.DMA((nt, n_slices)), pltpu.SemaphoreType.DMA((nt, n_slices))], start, finish)


def _cores_exchange(gs):
    def copies(ins, outs, sems):
        send_sems, recv_sems = sems
        x, y, c = _coords()
        return [pltpu.make_async_remote_copy(
            src_ref=ins[t].at[2 * j + (1 - c)], dst_ref=outs[t].at[j],
            send_sem=send_sems.at[t, j], recv_sem=recv_sems.at[t, j], device_id=(x, y, 1 - c), device_id_type=MESH)
            for t in range(len(gs)) for j in range(4)]

    return _swap_exchange(gs, 4, copies)


def _chips_exchange(ps):
    def copies(ins, outs, sems):
        send_sems, recv_sems = sems
        x, y, c = _coords()
        peers = [(1 - x, y), (x, 1 - y), (1 - x, 1 - y)]
        return [pltpu.make_async_remote_copy(
            src_ref=ins[t].at[2 * px + py], dst_ref=outs[t].at[k],
            send_sem=send_sems.at[t, k], recv_sem=recv_sems.at[t, k], device_id=(px, py, c), device_id_type=MESH)
            for t in range(len(ps)) for k, (px, py) in enumerate(peers)]

    return _swap_exchange(ps, 3, copies)


def _add_cores(g, r, core, *, name):
    _, A, B = g.shape
    ta = _tile(A, 512, 16)

    def body(core_ref, a_ref, b_ref, o16_ref):
        o16_ref[...] = (a_ref[...] + b_ref[...]).astype(BF16)

    blk = (None, ta, B)
    return pl.pallas_call(
        body, out_shape=jax.ShapeDtypeStruct((4, A, B), BF16),
        grid_spec=pltpu.PrefetchScalarGridSpec(
            num_scalar_prefetch=1, grid=(4, A // ta),
            in_specs=[pl.BlockSpec(blk, lambda j, i, core_ref: (2 * j + core_ref[0], i, 0)),
                      pl.BlockSpec(blk, lambda j, i, core_ref: (j, i, 0))],
            out_specs=pl.BlockSpec(blk, lambda j, i, core_ref: (j, i, 0))),
        compiler_params=_params(("parallel", "parallel")), name=name)(core, g, r)


def _adamw_math(w, g, m, v):
    m = ADAM_B1 * m + (1.0 - ADAM_B1) * g
    v = ADAM_B2 * v + (1.0 - ADAM_B2) * (g * g)
    m_hat = m / (1.0 - ADAM_B1 ** ADAM_STEP)
    v_hat = v / (1.0 - ADAM_B2 ** ADAM_STEP)
    delta = -ADAM_LR * (m_hat / (jnp.sqrt(v_hat) + ADAM_EPS) + ADAM_WD * w)
    return delta, m, v


def _sum_adamw(mine, sib, r, where, w, m, v, *, ta, name):
    Aw, Bw = w.shape
    Bg = mine.shape[2]
    assert Aw % ta == 0 and Bw <= Bg and mine.shape[1] == Aw

    def body(where_ref, p_ref, s_ref, r0, r1, r2, w_ref, m_ref, v_ref, g_out, d_out, m_out, v_out):
        g = (((p_ref[:, :Bw] + s_ref[:, :Bw]) + r0[:, :Bw].astype(F32))
             + r1[:, :Bw].astype(F32)) + r2[:, :Bw].astype(F32)
        delta, m_new, v_new = _adamw_math(w_ref[...], g, m_ref[...], v_ref[...])
        g_out[...] = g
        d_out[...] = delta
        m_out[...] = m_new
        v_out[...] = v_new

    gblk = (None, ta, Bg)
    row = pl.BlockSpec((ta, Bw), lambda i, where_ref: (i, 0))
    rspecs = [pl.BlockSpec(gblk, (lambda i, where_ref, k=k: (k, i, 0))) for k in range(3)]
    shp = jax.ShapeDtypeStruct((Aw, Bw), F32)
    return pl.pallas_call(
        body, out_shape=(shp, shp, shp, shp),
        grid_spec=pltpu.PrefetchScalarGridSpec(
            num_scalar_prefetch=1, grid=(Aw // ta,),
            in_specs=[pl.BlockSpec(gblk, lambda i, where_ref: (2 * where_ref[0] + where_ref[1], i, 0)),
                      pl.BlockSpec(gblk, lambda i, where_ref: (where_ref[0], i, 0))] + rspecs + [row, row, row],
            out_specs=(row, row, row, row)),
        compiler_params=_params(("parallel",)), name=name)(where, mine, sib, r, r, r, w, m, v)


def _adamw(w, g, m, v, *, name):
    def body(w_ref, g_ref, m_ref, v_ref, d_out, m_out, v_out):
        delta, m_new, v_new = _adamw_math(w_ref[...], g_ref[...], m_ref[...], v_ref[...])
        d_out[...] = delta
        m_out[...] = m_new
        v_out[...] = v_new

    vm = pl.BlockSpec(memory_space=pltpu.VMEM)
    shp = jax.ShapeDtypeStruct(w.shape, F32)
    return pl.pallas_call(body, out_shape=(shp, shp, shp), in_specs=[vm] * 4, out_specs=(vm, vm, vm),
                          compiler_params=_params(), name=name)(w, g, m, v)


def _small_allreduce_adamw(s, w, m, v, *, name):
    R, W = s.shape

    def body(s_ref, w_ref, m_ref, v_ref, g_out, d_out, m_out, v_out, gath, send_sems, recv_sems):
        x, y, c = _coords()
        mine = 4 * x + 2 * y + c
        gath[mine] = s_ref[...]
        peers = [((1 - x) if k & 4 else x, (1 - y) if k & 2 else y, (1 - c) if k & 1 else c) for k in range(1, N_DEV)]
        sends = []
        for k in range(1, N_DEV):
            peer = peers[k - 1]
            sends.append(pltpu.make_async_remote_copy(
                src_ref=s_ref, dst_ref=gath.at[mine], send_sem=send_sems.at[k - 1], recv_sem=recv_sems.at[k - 1],
                device_id=peer, device_id_type=MESH))
        for cp in sends:
            cp.start()
        for k in range(1, N_DEV):
            peer = peers[k - 1]
            pltpu.make_async_remote_copy(
                src_ref=s_ref, dst_ref=gath.at[4 * peer[0] + 2 * peer[1] + peer[2]],
                send_sem=send_sems.at[k - 1], recv_sem=recv_sems.at[k - 1],
                device_id=peer, device_id_type=MESH).wait_recv()
        for cp in sends:
            cp.wait_send()
        g = gath[0]
        for d in range(1, N_DEV):
            g = g + gath[d]
        delta, m_new, v_new = _adamw_math(w_ref[...], g, m_ref[...], v_ref[...])
        g_out[...] = g
        d_out[...] = delta
        m_out[...] = m_new
        v_out[...] = v_new

    vm = pl.BlockSpec(memory_space=pltpu.VMEM)
    shp = jax.ShapeDtypeStruct((R, W), F32)
    return pl.pallas_call(
        body, out_shape=(shp, shp, shp, shp), in_specs=[vm] * 4, out_specs=(vm, vm, vm, vm),
        scratch_shapes=[pltpu.VMEM((N_DEV, R, W), F32), pltpu.SemaphoreType.DMA((N_DEV - 1,)),
                        pltpu.SemaphoreType.DMA((N_DEV - 1,))],
        compiler_params=_params(), name=name)(s, w, m, v)


def _pack_small(rel_bias, g1, g2, g3, g4, b_forget, sinks, extra=None, meta=None):
    misc = jnp.concatenate([rel_bias.reshape(-1), b_forget.reshape(-1), sinks.reshape(-1)])
    misc = jnp.concatenate([misc, jnp.zeros((D_MODEL - misc.shape[0],), F32)])[None]
    last = jnp.zeros((1, D_MODEL), F32) if extra is None else extra
    meta = jnp.zeros((N_META, D_MODEL), F32) if meta is None else meta
    return jnp.concatenate([g1, g2, g3, g4, misc, last, jnp.zeros((2, D_MODEL), F32), meta], axis=0)


def _unpack_small(p):
    nrb = N_BUCKETS * SWA_Q_HEADS
    misc = p[4]
    return dict(rel_bias=misc[:nrb].reshape(N_BUCKETS, SWA_Q_HEADS), ln_pre_mix=p[0:1], ln_post_mix=p[1:2],
                ln_pre_ffn=p[2:3], ln_post_ffn=p[3:4], b_forget=misc[nrb:nrb + 8].reshape(1, 8),
                sinks=misc[nrb + 8:nrb + 16].reshape(1, 8))


def _proj_runs():
    gw = FOX_GROUP * HEAD_DIM
    swa = SWA_Q_W + 2 * SWA_KV_HEADS * HEAD_DIM
    runs = [(0, swa)]
    for grp in range(FOX_HEADS // FOX_GROUP):
        runs += [(swa + part * FOX_W + grp * gw, swa + part * FOX_W + (grp + 1) * gw) for part in range(3)]
    return runs


def _columns_from_shards(gathered, runs, shard):
    pieces = []
    for start, stop in runs:
        for d in range(start // shard, (stop - 1) // shard + 1):
            lo = d * shard
            pieces.append(gathered[d][:, max(start, lo) - lo:min(stop, lo + shard) - lo])
    return jnp.concatenate(pieces, axis=1)


def _device_shards(qkv, gate, shard, padded):
    pos, segments = 0, []
    for start, stop in _proj_runs():
        segments.append((start, stop, qkv, pos))
        pos += stop - start
    segments.append((pos, pos + gate.shape[1], gate, 0))
    total = pos + gate.shape[1]
    assert total % shard == 0
    zeros = jnp.zeros((qkv.shape[0], padded - shard), qkv.dtype)
    out = []
    for d in range(total // shard):
        lo, hi = d * shard, (d + 1) * shard
        pieces = [arr[:, src + max(lo, s) - s:src + min(hi, e) - s]
                  for s, e, arr, src in sorted(segments, key=lambda seg: seg[0]) if max(lo, s) < min(hi, e)]
        out.append(jnp.concatenate(pieces + [zeros], axis=1))
    return jnp.stack(out)


def kernel(x, meta_tokens, rel_bias, ln_pre_mix, ln_post_mix, ln_pre_ffn, ln_post_ffn, w_in, b_forget, sinks, w_out, w_gate_up, w_down, loss_target, m_meta_tokens, m_rel_bias, m_ln_pre_mix, m_ln_post_mix, m_ln_pre_ffn, m_ln_post_ffn, m_w_in, m_b_forget, m_sinks, m_w_out, m_w_gate_up, m_w_down, v_meta_tokens, v_rel_bias, v_ln_pre_mix, v_ln_post_mix, v_ln_pre_ffn, v_ln_post_ffn, v_w_in, v_b_forget, v_sinks, v_w_out, v_w_gate_up, v_w_down):
    seq = x.shape[1]
    T = BLOCK + seq
    assert T % FOX_TILE == 0
    nq = T // FOX_TILE
    tm = _tile(T, 1056)
    cin = w_in.shape[2]
    hid = w_down.shape[1]
    F = N_DEV * hid
    assert w_gate_up.shape[2] == 2 * hid and cin <= W_IN_PAD and hid % 16 == 0

    x_i, y_i, c_i = _coords()
    core = jnp.reshape(c_i, (1,)).astype(jnp.int32)
    where = jnp.stack([2 * x_i + y_i, c_i]).astype(jnp.int32)
    w_in_s = jnp.pad(w_in[0].astype(BF16), ((0, 0), (0, W_IN_PAD - cin)))
    w_gu_t = w_gate_up[0].T
    h0, target, hn1, hn1_t, g_in, _ = _pad_rows_rms(x[0], loss_target[0], ln_pre_mix,
                                                    _gather_exchange([w_in_s, meta_tokens]), name="ag_w_in_rms_pre_mix")
    gather_rest = _gather_exchange([w_out[0].astype(BF16), w_gu_t.astype(BF16), w_down[0].astype(BF16)])
    w_qkv = _columns_from_shards(g_in, _proj_runs(), cin)
    w_f = jnp.pad(_columns_from_shards(g_in, [(D_QKV, D_PROJ)], cin), ((0, 0), (0, BLOCK - FOX_HEADS)))

    proj = _matmul(hn1, w_qkv, out_dtype=BF16, tm=tm, tn=D_QKV, name="mm_in_proj")
    proj_f = _matmul(hn1, w_f, out_dtype=F32, tm=tm, tn=BLOCK, name="mm_in_proj_f")

    f_t = proj_f[:, :FOX_HEADS].T
    bf_col = b_forget.reshape(FOX_HEADS, 1)

    oh_cur, oh_prev = _bucket_onehots()
    bias_c, bias_p = _bias_tiles(rel_bias.T, jnp.asarray(oh_cur.T), jnp.asarray(oh_prev.T), name="bias_tiles")
    far = rel_bias[N_BUCKETS - 1]
    sink_v = sinks[0]
    mix_a = _swa_fwd(proj, bias_c, bias_p, far, sink_v, name="swa_fwd")

    cum_col = _fox_gates_fwd(f_t, bf_col, name="fox_gates_fwd")
    q_b, k_b, v_b = _fox_prep(proj, cum_col, name="fox_prep")
    mix, lse_row, g_out, g_gu, g_down = _fox_fwd(q_b, k_b, v_b, mix_a, ex=gather_rest, name="fox_fwd")
    w_out_full = g_out.reshape(D_MODEL, D_MODEL)
    w_gu_full_t = g_gu.reshape(2 * F, D_MODEL)
    w_down_full = g_down.reshape(F, D_MODEL)

    a1 = _matmul(mix, w_out_full, out_dtype=F32, tm=tm, tn=D_MODEL, name="mm_out_proj")
    h1, hn2 = _post_res_norm(a1, ln_post_mix, h0, ln_pre_ffn, name="post_mix_pre_ffn")
    gate, up, act, act_t = _gate_up_swiglu(hn2, w_gu_full_t, name="mm_gate_up")
    ff = _matmul(act, w_down_full, out_dtype=F32, tm=tm, tn=512, name="mm_down")
    dh2, dff, dg_post_ffn, loss_acc = _loss_head(ff, ln_post_ffn, h1, target, name="loss_head")

    dgu = _d_act_swiglu(dff, w_down_full, gate, up, name="mm_d_act")
    d_w_down = _matmul(act_t, dff, out_dtype=F32, tm=_tile(F, 768), tn=512, name="mm_dw_down")
    dhn2 = _matmul(dgu, w_gu_full_t, out_dtype=F32, tm=tm, tn=512, tk=F, name="mm_d_hn2")
    d_w_gu_t = _matmul(dgu, hn2, ta=True, out_dtype=F32, tm=256, tn=D_MODEL, name="mm_dw_gate_up")
    dh1, dg_pre_ffn, da1, dg_post_mix = _rms_bwd(h1, ln_pre_ffn, dhn2, dh2, out_dtype=F32,
                                                 then=(a1, ln_post_mix), name="rms_bwd_pre_ffn_post_mix")
    dmix = _matmul(da1, w_out_full, nt=True, out_dtype=BF16, tm=tm, tn=D_MODEL, name="mm_d_mix")
    d_w_out = _matmul(mix, da1, ta=True, out_dtype=F32, tm=512, tn=D_MODEL, name="mm_dw_out")

    ffn_grads = [g.reshape(N_DEV, -1, D_MODEL) for g in (d_w_out, d_w_gu_t, d_w_down)]
    dproj_a, dbc, dbp, dbf, dsk, *ffn_sibling = _swa_bwd(
        proj, dmix, bias_c, bias_p, far, sink_v, ex=_cores_exchange(ffn_grads), name="swa_bwd")
    ffn_sums = [_add_cores(g, r, core, name="rs_add_" + t)
                for g, r, t in zip(ffn_grads, ffn_sibling, ["w_out", "w_gate_up", "w_down"])]

    do_b = _fox_prep_bwd(dmix, mix, name="fox_prep_bwd")
    dproj, dcq, dck, *ffn_chips = _fox_bwd(
        q_b, k_b, v_b, do_b, lse_row, dproj_a, ex=_chips_exchange(ffn_sums), name="fox_bwd")
    df_t, d_bf = _fox_gates_bwd(dcq.reshape(FOX_HEADS, T), dck.reshape(FOX_HEADS, T), f_t, bf_col,
                                name="fox_gates_bwd")
    df = jnp.pad(df_t.T.astype(BF16), ((0, 0), (0, BLOCK - FOX_HEADS)))

    d_w_qkv = _matmul(hn1_t, dproj, out_dtype=F32, tm=512, tn=768, name="mm_dw_in")
    d_w_f = _matmul(hn1_t, df, out_dtype=F32, tm=512, tn=BLOCK, name="mm_dw_in_f")
    d_w_in = _device_shards(d_w_qkv, d_w_f[:, :FOX_HEADS], cin, W_IN_PAD)
    d_tab, d_sink, in_sibling = _small_grads(dbc, dbp, dbf, dsk, jnp.asarray(oh_cur), jnp.asarray(oh_prev),
                                             ex=_cores_exchange([d_w_in]), name="small_grads")
    in_sum = _add_cores(d_w_in, in_sibling, core, name="rs_add_w_in")
    dhn1, in_chips = _matmul(dproj, w_qkv, nt=True, out_dtype=F32, tm=tm, tn=512,
                             ex=_chips_exchange([in_sum]), name="mm_d_hn1")
    dx_rows, dg_pre_mix, dh0_head = _rms_bwd(
        h0, ln_pre_mix, dhn1, dh1, out_dtype=F32, dy2=(df, w_f), split_head=True, name="rms_bwd_pre_mix")
    grad_x = dx_rows[None]
    d_meta = dh0_head[PAD_ROWS:]

    rs_out, rs_gu, rs_down = zip(ffn_grads, ffn_sibling, ffn_chips)
    updates = [("w_in", (d_w_in, in_sibling, in_chips), (w_in[0], m_w_in[0], v_w_in[0]), 256),
               ("w_out", rs_out, (w_out[0], m_w_out[0], v_w_out[0]), BLOCK),
               ("w_gate_up", rs_gu, (w_gu_t, m_w_gate_up[0].T, v_w_gate_up[0].T), hid),
               ("w_down", rs_down, (w_down[0], m_w_down[0], v_w_down[0]), hid)]
    big = [{}, {}, {}, {}]
    for t, grads, shard, ta in updates:
        res = _sum_adamw(*grads, where, *shard, ta=ta, name="rs_adamw_" + t)
        for kind in range(4):
            big[kind][t] = (res[kind].T if t == "w_gate_up" else res[kind])[None]

    loss_row = jnp.pad(loss_acc[0:1, 0:1] * (0.5 / D_MODEL), ((0, 0), (0, D_MODEL - 1)))
    s_small = _pack_small(d_tab.T, dg_pre_mix, dg_post_mix, dg_pre_ffn, dg_post_ffn, d_bf, d_sink,
                          extra=loss_row, meta=d_meta)
    w_s = _pack_small(rel_bias, ln_pre_mix, ln_post_mix, ln_pre_ffn, ln_post_ffn, b_forget, sinks)
    m_s = _pack_small(m_rel_bias, m_ln_pre_mix, m_ln_post_mix, m_ln_pre_ffn, m_ln_post_ffn, m_b_forget, m_sinks)
    v_s = _pack_small(v_rel_bias, v_ln_pre_mix, v_ln_post_mix, v_ln_pre_ffn, v_ln_post_ffn, v_b_forget, v_sinks)
    small = _small_allreduce_adamw(s_small, w_s, m_s, v_s, name="small_allreduce_adamw")
    loss = small[0][5, 0]
    mcols = meta_tokens.shape[1]
    g_meta_mine = lax.dynamic_slice(small[0][8:8 + N_META], (0, (4 * x_i + 2 * y_i + c_i) * mcols), (N_META, mcols))
    big[0]["meta_tokens"] = g_meta_mine
    for kind, arr in enumerate(_adamw(meta_tokens, g_meta_mine, m_meta_tokens, v_meta_tokens, name="adamw_meta")):
        big[kind + 1]["meta_tokens"] = arr
    small = [_unpack_small(p) for p in small]

    names = ["meta_tokens", "rel_bias", "ln_pre_mix", "ln_post_mix", "ln_pre_ffn", "ln_post_ffn", "w_in",
             "b_forget", "sinks", "w_out", "w_gate_up", "w_down"]
    outs = [loss, grad_x]
    for kind in range(4):
        for nme in names:
            outs.append(big[kind][nme] if nme in big[kind] else small[kind][nme])
    return tuple(outs)
```

```python
import math

import numpy as np
import jax
import jax.numpy as jnp
from jax import lax
from jax.experimental import pallas as pl
from jax.experimental.pallas import tpu as pltpu

F32 = jnp.float32
BF16 = jnp.bfloat16
HIGHEST = lax.Precision.HIGHEST
MESH = pl.DeviceIdType.MESH

N_DEV = 8
D_MODEL = 1024
N_META = 16
HEAD_DIM = 64
SWA_Q_HEADS = 8
SWA_KV_HEADS = 2
SWA_GROUP = 4
FOX_HEADS = 8
FOX_W = FOX_HEADS * HEAD_DIM
SWA_Q_W = SWA_Q_HEADS * HEAD_DIM
BLOCK = 128
PAD_ROWS = BLOCK - N_META
N_BUCKETS = 32
MAX_DISTANCE = 128
D_FF = 2816
D_QKV = 2304
D_PROJ = D_QKV + FOX_HEADS
D_PROJ_PAD = 2560
EPS = 1e-6
NEG = -1e30
SCALE = HEAD_DIM ** -0.5
ADAM_LR, ADAM_B1, ADAM_B2, ADAM_EPS, ADAM_WD, ADAM_STEP = 0.001, 0.9, 0.999, 1e-08, 0.01, 10
VMEM_LIMIT = 56 * 1024 * 1024
FOX_TILE = 384
FOX_GROUP = 4
W_IN_PAD = 384

NT = (((1,), (1,)), ((), ()))
NN = (((1,), (0,)), ((), ()))
TN = (((0,), (0,)), ((), ()))


def _params(sem=None, **kw):
    if sem is not None:
        kw["dimension_semantics"] = sem
    return pltpu.CompilerParams(vmem_limit_bytes=VMEM_LIMIT, **kw)


def _tile(n, target, mult=16):
    best = None
    for t in range(mult, min(n, target) + 1, mult):
        if n % t == 0:
            best = t
    assert best is not None, (n, target)
    return best


def _matmul(a, b, *, nt=False, ta=False, out_dtype, tm, tn, tk=None, ex=None, name):
    M, K = a.shape[::-1] if ta else a.shape
    assert not (ta and nt)
    N = b.shape[0] if nt else b.shape[1]
    tk = K if tk is None else tk
    assert M % tm == 0 and N % tn == 0 and K % tk == 0, (name, a.shape, b.shape, tm, tn, tk)
    nk = K // tk
    dn = NT if nt else (TN if ta else NN)
    a_spec = pl.BlockSpec((tk, tm), lambda i, j, k: (k, i)) if ta else pl.BlockSpec((tm, tk), lambda i, j, k: (i, k))

    def body(a_ref, b_ref, o_ref, *scr):
        part = lax.dot_general(a_ref[...], b_ref[...], dn, preferred_element_type=F32)
        if nk == 1:
            o_ref[...] = part.astype(o_ref.dtype)
        else:
            acc = scr[0]
            k = pl.program_id(2)

            @pl.when(k == 0)
            def _():
                acc[...] = part

            @pl.when(k > 0)
            def _():
                acc[...] += part

            @pl.when(k == nk - 1)
            def _():
                o_ref[...] = acc[...].astype(o_ref.dtype)

    if nt:
        b_spec = pl.BlockSpec((tn, tk), lambda i, j, k: (j, k))
    else:
        b_spec = pl.BlockSpec((tk, tn), lambda i, j, k: (k, j))
    out_shape = jax.ShapeDtypeStruct((M, N), out_dtype)
    out_spec = pl.BlockSpec((tm, tn), lambda i, j, k: (i, j))
    grid = (M // tm, N // tn, nk)
    body, x_in, x_in_specs, x_out, x_out_specs, x_scr = _carry(ex, grid, 2, 1, body)
    res = pl.pallas_call(
        body,
        out_shape=(out_shape, *x_out),
        grid=grid,
        in_specs=[a_spec, b_spec] + x_in_specs,
        out_specs=(out_spec, *x_out_specs),
        scratch_shapes=([pltpu.VMEM((tm, tn), F32)] if nk > 1 else []) + x_scr,
        compiler_params=_params(("parallel", "parallel", "arbitrary") if ex is None else ("arbitrary",) * 3),
        name=name,
    )(a, b, *x_in)
    return res[0] if ex is None else res


def _rstd(x):
    return lax.rsqrt(jnp.mean(x * x, axis=-1, keepdims=True) + EPS)


def _pad_rows_rms(x, target, g, ex, *, name):
    S, D = x.shape
    nb = S // BLOCK + 1
    ni, no = len(ex.inputs), len(ex.out_shapes)
    mcols = D // N_DEV

    def body(x_ref, t_ref, g_ref, *rest):
        side_in, (h_ref, to_ref, y_ref, yt_ref) = rest[:ni], rest[ni:ni + 4]
        side_out = rest[ni + 4:ni + 4 + no]
        meta_buf, meta_sems, *sems = rest[ni + 4 + no:]
        i = pl.program_id(0)

        @pl.when(i == 0)
        def _():
            ex.start(side_in, side_out, sems)

        def norm():
            h = h_ref[...]
            y = h * _rstd(h) * g_ref[...]
            y_ref[...] = y.astype(y_ref.dtype)
            yt_ref[...] = y.T.astype(yt_ref.dtype)

        @pl.when(i < nb - 1)
        def _():
            h_ref[...] = x_ref[...]
            to_ref[...] = t_ref[...]
            norm()

        @pl.when(i == nb - 1)
        def _():
            ex.finish(side_in, side_out, sems)
            copies = [pltpu.make_async_copy(side_out[-1].at[d], meta_buf.at[:, d * mcols:(d + 1) * mcols],
                                            meta_sems.at[d]) for d in range(N_DEV)]
            for cp in copies:
                cp.start()
            for cp in copies:
                cp.wait()
            h_ref[:PAD_ROWS, :] = jnp.zeros((PAD_ROWS, D), F32)
            h_ref[PAD_ROWS:, :] = meta_buf[...]
            to_ref[...] = jnp.zeros_like(to_ref)
            norm()

    src = pl.BlockSpec((BLOCK, D), lambda i: (jnp.minimum(i, nb - 2), 0))
    dst = pl.BlockSpec((BLOCK, D), lambda i: ((i + 1) % nb, 0))
    hbm = pl.BlockSpec(memory_space=pl.ANY)
    rows = jax.ShapeDtypeStruct((BLOCK + S, D), F32)
    return pl.pallas_call(
        body,
        out_shape=(rows, rows, jax.ShapeDtypeStruct((BLOCK + S, D), BF16), jax.ShapeDtypeStruct((D, BLOCK + S), BF16),
                   *ex.out_shapes),
        grid=(nb,),
        in_specs=[src, src, pl.BlockSpec((1, D), lambda i: (0, 0))] + [hbm] * ni,
        out_specs=(dst, dst, dst, pl.BlockSpec((D, BLOCK), lambda i: (0, (i + 1) % nb)), *([hbm] * no)),
        scratch_shapes=[pltpu.VMEM((N_META, D), F32), pltpu.SemaphoreType.DMA((N_DEV,))] + list(ex.scratch),
        compiler_params=_params(("arbitrary",)), name=name)(x, target, g, *ex.inputs)


def _post_res_norm(a, g_post, h, g_pre, *, name):
    T, D = a.shape
    tm = _tile(T, 384, BLOCK)

    def body(a_ref, gp_ref, h_ref, gn_ref, h1_ref, o_ref):
        a = a_ref[...]
        h1 = h_ref[...] + a * _rstd(a) * gp_ref[...]
        h1_ref[...] = h1
        o_ref[...] = (h1 * _rstd(h1) * gn_ref[...]).astype(o_ref.dtype)

    row = pl.BlockSpec((tm, D), lambda i: (i, 0))
    vec = pl.BlockSpec((1, D), lambda i: (0, 0))
    return pl.pallas_call(
        body, out_shape=(jax.ShapeDtypeStruct((T, D), F32), jax.ShapeDtypeStruct((T, D), BF16)), grid=(T // tm,),
        in_specs=[row, vec, row, vec], out_specs=(row, row),
        compiler_params=_params(("parallel",)), name=name)(a, g_post, h, g_pre)


def _loss_head(a, g, h, target, *, name):
    T, D = a.shape
    tm = _tile(T, 512)

    def body(a_ref, g_ref, h_ref, t_ref, dy_ref, da_ref, dg_ref, loss_ref):
        i = pl.program_id(0)
        a = a_ref[...]
        r = _rstd(a)
        ah = a * r
        y = h_ref[...] + ah * g_ref[...]
        rows = i * tm + lax.broadcasted_iota(jnp.int32, (tm, 1), 0)
        err = jnp.where(rows >= BLOCK, y - t_ref[...], 0.0)
        dy = err / D
        dy_ref[...] = dy
        dah = dy * g_ref[...]
        da_ref[...] = (r * (dah - ah * jnp.mean(dah * ah, axis=-1, keepdims=True))).astype(da_ref.dtype)
        part = jnp.sum(jnp.sum(err * err, axis=1, keepdims=True), axis=0, keepdims=True)

        @pl.when(i == 0)
        def _():
            loss_ref[...] = jnp.zeros_like(loss_ref)
            dg_ref[...] = jnp.zeros_like(dg_ref)

        loss_ref[...] += jnp.broadcast_to(part, loss_ref.shape)
        dg_ref[...] += jnp.sum(dy * ah, axis=0, keepdims=True)

    row = pl.BlockSpec((tm, D), lambda i: (i, 0))
    vec = pl.BlockSpec((1, D), lambda i: (0, 0))
    return pl.pallas_call(
        body, out_shape=(jax.ShapeDtypeStruct((T, D), F32), jax.ShapeDtypeStruct((T, D), BF16),
                         jax.ShapeDtypeStruct((1, D), F32), jax.ShapeDtypeStruct((8, 128), F32)),
        grid=(T // tm,),
        in_specs=[row, vec, row, row],
        out_specs=(row, row, vec, pl.BlockSpec((8, 128), lambda i: (0, 0))),
        compiler_params=_params(("arbitrary",)), name=name)(a, g, h, target)


def _rms_pull_back(x, g, dy):
    r = _rstd(x)
    xh = x * r
    dxh = dy * g
    return r * (dxh - xh * jnp.mean(dxh * xh, axis=-1, keepdims=True)), jnp.sum(dy * xh, axis=0, keepdims=True)


def _rms_bwd_twice(x, g, dy, res, x2, g2, *, name):
    T, D = x.shape
    tm = _tile(T, 512)

    def body(x_ref, g_ref, dy_ref, res_ref, x2_ref, g2_ref, dx_ref, dg_ref, dx2_ref, dg2_ref):
        @pl.when(pl.program_id(0) == 0)
        def _():
            dg_ref[...] = jnp.zeros_like(dg_ref)
            dg2_ref[...] = jnp.zeros_like(dg2_ref)

        dx, dg = _rms_pull_back(x_ref[...], g_ref[...], dy_ref[...].astype(F32))
        dx = dx + res_ref[...]
        dx_ref[...] = dx
        dg_ref[...] += dg
        dx2, dg2 = _rms_pull_back(x2_ref[...], g2_ref[...], dx)
        dx2_ref[...] = dx2.astype(dx2_ref.dtype)
        dg2_ref[...] += dg2

    row = pl.BlockSpec((tm, D), lambda i: (i, 0))
    vec = pl.BlockSpec((1, D), lambda i: (0, 0))
    gain = jax.ShapeDtypeStruct((1, D), F32)
    return pl.pallas_call(
        body, out_shape=(jax.ShapeDtypeStruct((T, D), F32), gain, jax.ShapeDtypeStruct((T, D), BF16), gain),
        grid=(T // tm,), in_specs=[row, vec, row, row, row, vec], out_specs=(row, vec, row, vec),
        compiler_params=_params(("arbitrary",)), name=name)(x, g, dy, res, x2, g2)


def _rms_bwd_rows(x, g, dy, a, b, res, *, name):
    T, D = x.shape
    n = a.shape[1]
    n_tail = T // BLOCK - 1
    per_step = max(p for p in (4, 3, 2, 1) if n_tail % p == 0)
    steps = n_tail // per_step
    assert T == BLOCK * (1 + n_tail)
    n_rows = 4 * (per_step + 1)

    def body(*refs):
        rows, (g_ref, b_ref), (tail_ref, dg_ref, head_ref) = refs[:n_rows], refs[n_rows:n_rows + 2], refs[n_rows + 2:]

        def block(s):
            x_ref, dy_ref, a_ref, res_ref = rows[4 * s:4 * s + 4]
            dy_all = dy_ref[...] + lax.dot_general(a_ref[...], b_ref[...], NT, preferred_element_type=F32)
            dx, dg = _rms_pull_back(x_ref[...], g_ref[...], dy_all)
            return dx + res_ref[...], dg

        @pl.when(pl.program_id(0) == 0)
        def _():
            dx, dg = block(per_step)
            head_ref[...] = dx
            dg_ref[...] = dg

        for s in range(per_step):
            dx, dg = block(s)
            tail_ref[s * BLOCK:(s + 1) * BLOCK, :] = dx
            dg_ref[...] += dg

    def blocks(width):
        tail = [pl.BlockSpec((BLOCK, width), lambda i, s=s: (per_step * i + s + 1, 0)) for s in range(per_step)]
        return tail + [pl.BlockSpec((BLOCK, width), lambda i: (0, 0))]

    specs, args = [], []
    for bx, bdy, ba, bres in zip(blocks(D), blocks(D), blocks(n), blocks(D)):
        specs += [bx, bdy, ba, bres]
        args += [x, dy, a, res]
    vec = pl.BlockSpec((1, D), lambda i: (0, 0))
    return pl.pallas_call(
        body,
        out_shape=(jax.ShapeDtypeStruct((T - BLOCK, D), F32), jax.ShapeDtypeStruct((1, D), F32),
                   jax.ShapeDtypeStruct((BLOCK, D), F32)),
        grid=(steps,), in_specs=specs + [vec, pl.BlockSpec(b.shape, lambda i: (0, 0))],
        out_specs=(pl.BlockSpec((per_step * BLOCK, D), lambda i: (i, 0)), vec, pl.BlockSpec((BLOCK, D), lambda i: (0, 0))),
        compiler_params=_params(("arbitrary",)), name=name)(*args, g, b)


def _gate_up_swiglu(a, w_t, *, name):
    T, D = a.shape
    F = w_t.shape[0] // 2
    tm = _tile(T, 1408, BLOCK)
    n = _tile(F, 256, BLOCK)

    def body(a_ref, wg_ref, wu_ref, g_ref, u_ref, o_ref, ot_ref):
        x = a_ref[...]
        g = lax.dot_general(x, wg_ref[...], NT, preferred_element_type=F32)
        u = lax.dot_general(x, wu_ref[...], NT, preferred_element_type=F32)
        g16, u16 = g.astype(BF16), u.astype(BF16)
        g_ref[...] = g16
        u_ref[...] = u16
        gr = g16.astype(F32)
        act = gr / (1.0 + jnp.exp(-gr)) * u16.astype(F32)
        o_ref[...] = act.astype(o_ref.dtype)
        ot_ref[...] = act.T.astype(ot_ref.dtype)

    tile = pl.BlockSpec((tm, n), lambda i, j: (i, j))
    shp = jax.ShapeDtypeStruct((T, F), BF16)
    return pl.pallas_call(
        body, out_shape=(shp, shp, shp, jax.ShapeDtypeStruct((F, T), BF16)), grid=(T // tm, F // n),
        in_specs=[pl.BlockSpec((tm, D), lambda i, j: (i, 0)),
                  pl.BlockSpec((n, D), lambda i, j: (j, 0)),
                  pl.BlockSpec((n, D), lambda i, j: (j + F // n, 0))],
        out_specs=(tile, tile, tile, pl.BlockSpec((n, tm), lambda i, j: (j, i))),
        compiler_params=_params(("parallel", "parallel")), name=name)(a, w_t, w_t)


def _d_act_swiglu(dff, w_down, gate, up, *, name):
    T, D = dff.shape
    F = w_down.shape[0]
    tm = _tile(T, 384)
    chunk = 768
    assert F % BLOCK == 0

    def body(d_ref, w_ref, g_ref, u_ref, o_ref):
        dy = d_ref[...]
        for c in range(0, F, chunk):
            e = min(c + chunk, F)
            d = lax.dot_general(dy, w_ref[c:e, :], NT, preferred_element_type=F32)
            g = g_ref[:, c:e].astype(F32)
            u = u_ref[:, c:e].astype(F32)
            sg = 1.0 / (1.0 + jnp.exp(-g))
            o_ref[:, c:e] = (d * u * (sg * (1.0 + g * (1.0 - sg)))).astype(o_ref.dtype)
            o_ref[:, F + c:F + e] = (d * (g * sg)).astype(o_ref.dtype)

    row = pl.BlockSpec((tm, F), lambda i: (i, 0))
    return pl.pallas_call(
        body, out_shape=jax.ShapeDtypeStruct((T, 2 * F), BF16), grid=(T // tm,),
        in_specs=[pl.BlockSpec((tm, D), lambda i: (i, 0)), pl.BlockSpec((F, D), lambda i: (0, 0)), row, row],
        out_specs=pl.BlockSpec((tm, 2 * F), lambda i: (i, 0)),
        compiler_params=_params(("parallel",)), name=name)(dff, w_down, gate, up)


def _fox_gates_fwd(f_t, b, *, name):
    H, T = f_t.shape
    nb = T // BLOCK

    def body(f_ref, b_ref, col_ref):
        f = f_ref[...] + b_ref[...]
        ls = jnp.minimum(f, 0.0) - jnp.log(1.0 + jnp.exp(-jnp.abs(f)))
        t = lax.broadcasted_iota(jnp.int32, (H, T), 1)
        ls = jnp.where(t >= PAD_ROWS, ls, 0.0)
        upper = (lax.broadcasted_iota(jnp.int32, (BLOCK, BLOCK), 0)
                 <= lax.broadcasted_iota(jnp.int32, (BLOCK, BLOCK), 1)).astype(F32)
        carry = jnp.zeros((H, 1), F32)
        for blk in range(nb):
            seg = ls[:, blk * BLOCK:(blk + 1) * BLOCK]
            pre = jnp.dot(seg, upper, precision=HIGHEST, preferred_element_type=F32) + carry
            key_gate = jnp.where(t[:, blk * BLOCK:(blk + 1) * BLOCK] >= PAD_ROWS, pre, -NEG)
            terms = list(_split3(pre)) + list(_split3(key_gate))
            col_ref[blk * BLOCK:(blk + 1) * BLOCK, :] = jnp.concatenate(
                terms + [jnp.zeros((BLOCK - len(terms) * H, BLOCK), F32)], axis=0).T.astype(col_ref.dtype)
            carry = pre[:, BLOCK - 1:BLOCK]

    vm = pl.BlockSpec(memory_space=pltpu.VMEM)
    return pl.pallas_call(
        body, out_shape=jax.ShapeDtypeStruct((T, BLOCK), BF16),
        in_specs=[vm, vm], out_specs=vm,
        compiler_params=_params(), name=name)(f_t, b)


def _fox_gates_bwd(dcq, dck, f_t, b, *, name):
    H, T = f_t.shape
    nb = T // BLOCK

    def body(dq_ref, d_ref, f_ref, b_ref, df_ref, db_ref):
        lower = (lax.broadcasted_iota(jnp.int32, (BLOCK, BLOCK), 0)
                 >= lax.broadcasted_iota(jnp.int32, (BLOCK, BLOCK), 1)).astype(F32)
        carry = jnp.zeros((H, 1), F32)
        for blk in range(nb - 1, -1, -1):
            seg = dq_ref[:, blk * BLOCK:(blk + 1) * BLOCK] - d_ref[:, blk * BLOCK:(blk + 1) * BLOCK]
            suf = jnp.dot(seg, lower, precision=HIGHEST, preferred_element_type=F32) + carry
            df_ref[:, blk * BLOCK:(blk + 1) * BLOCK] = suf
            carry = suf[:, 0:1]
        f = f_ref[...] + b_ref[...]
        t = lax.broadcasted_iota(jnp.int32, (H, T), 1)
        df = jnp.where(t >= PAD_ROWS, df_ref[...] / (1.0 + jnp.exp(f)), 0.0)
        df_ref[...] = df
        db_ref[...] = jnp.sum(df, axis=1, keepdims=True)

    vm = pl.BlockSpec(memory_space=pltpu.VMEM)
    return pl.pallas_call(
        body, out_shape=(jax.ShapeDtypeStruct((H, T), F32), jax.ShapeDtypeStruct((H, 1), F32)),
        in_specs=[vm, vm, vm, vm], out_specs=(vm, vm),
        compiler_params=_params(), name=name)(dcq, dck, f_t, b)


def _fox_lanes(parity):
    base = HEAD_DIM * (1 - parity)
    return base, base + 3


def _split3(c):
    hi = c.astype(BF16).astype(F32)
    r = c - hi
    mid = r.astype(BF16).astype(F32)
    lo = (r - mid).astype(BF16).astype(F32)
    return hi, mid, lo


def _lanes(lane, parity, data, start, terms, ones_at=None, fill=1.0):
    out = jnp.zeros((), F32) if ones_at is None else jnp.where((lane >= ones_at) & (lane < ones_at + 3), fill, 0.0)
    for i, t in enumerate(terms):
        out = jnp.where(lane == start + i, t, out)
    return jnp.where(lane // HEAD_DIM == parity, data, out)


def _fox_prep(proj, cum_col, *, name):
    T = proj.shape[0]
    tm = _tile(T, 1408, BLOCK)
    nt = T // tm
    H = FOX_HEADS
    lanes = 2 * HEAD_DIM
    first = (proj.shape[1] - 3 * H * HEAD_DIM) // lanes

    def body(q_ref, k_ref, v_ref, c_ref, qa_ref, ka_ref, va_ref):
        p = pl.program_id(0)
        i = pl.program_id(1)
        lane = lax.broadcasted_iota(jnp.int32, (1, lanes), 1)
        src = lax.broadcasted_iota(jnp.int32, (lanes, lanes), 0)
        dst = lax.broadcasted_iota(jnp.int32, (lanes, lanes), 1)
        q2 = q_ref[...].astype(F32) * SCALE
        k2 = k_ref[...].astype(F32)
        v2 = v_ref[...].astype(F32)
        gates = c_ref[...]
        def placed(h, first_term, start):
            pick = ((src % FOX_HEADS == h) & (src // FOX_HEADS - first_term == dst - start)
                    & (dst >= start) & (dst < start + 3))
            return jnp.dot(gates, pick.astype(BF16), preferred_element_type=F32)

        moved = [(placed(2 * p + e, 0, _fox_lanes(e)[1]), placed(2 * p + e, 3, _fox_lanes(e)[0])) for e in range(2)]
        for e in range(2):
            kc, qc = _fox_lanes(e)
            own = lane // HEAD_DIM == e
            minus = jnp.where((lane >= kc) & (lane < kc + 3), -1.0, 0.0)
            ones_q = jnp.where((lane >= qc) & (lane < qc + 3), 1.0, 0.0)
            ones_k = jnp.where((lane >= kc) & (lane < kc + 3), 1.0, 0.0)
            qa_ref[e] = jnp.where(own, q2, moved[e][0] + minus).astype(BF16)
            ka_ref[e] = jnp.where(own, k2, moved[e][1] + ones_q).astype(BF16)
            va_ref[e] = jnp.where(own, v2, ones_k).astype(BF16)

    pairs = FOX_GROUP // 2

    def col(part):
        return pl.BlockSpec((tm, lanes),
                            lambda p, i: (i, first + 3 * pairs * (p // pairs) + part * pairs + p % pairs))

    out = pl.BlockSpec((2, tm, lanes), lambda p, i: (p, i, 0))
    shp = jax.ShapeDtypeStruct((H, T, lanes), BF16)
    return pl.pallas_call(
        body, out_shape=(shp, shp, shp), grid=(H // 2, nt),
        in_specs=[col(0), col(1), col(2), pl.BlockSpec((tm, lanes), lambda p, i: (i, 0))],
        out_specs=(out, out, out),
        compiler_params=_params(("parallel", "parallel")), name=name)(proj, proj, proj, cum_col)


def _fox_fwd(q_aug, k_aug, v_aug, mix, *, ex=None, name):
    H, T, lanes = q_aug.shape
    tq = FOX_TILE
    nq = T // tq
    G = FOX_HEADS

    def body(q_ref, k_ref, v_ref, mix_ref, o_ref, lse_ref, m_scr, acc_scr):
        i = pl.program_id(1)
        m_scr[...] = jnp.full(m_scr.shape, NEG, F32)
        acc_scr[...] = jnp.zeros(acc_scr.shape, F32)

        def step(kb, diag):
            off = pl.multiple_of(kb * tq, tq)
            s_t = [lax.dot_general(k_ref[g, pl.ds(off, tq), :], q_ref[g], NT, preferred_element_type=F32)
                   for g in range(G)]
            if diag:
                r = lax.broadcasted_iota(jnp.int32, (tq, tq), 0)
                c = lax.broadcasted_iota(jnp.int32, (tq, tq), 1)
                s_t = [jnp.where(c >= r, s, NEG) for s in s_t]
            m_prev = [m_scr[g] for g in range(G)]
            m_new = [jnp.maximum(m_prev[g], jnp.max(s_t[g], axis=0, keepdims=True)) for g in range(G)]
            p_t = [jnp.exp(s_t[g] - m_new[g]).astype(BF16) for g in range(G)]
            pv = [lax.dot_general(v_ref[g, pl.ds(off, tq), :], p_t[g], TN, preferred_element_type=F32)
                  for g in range(G)]
            for g in range(G):
                acc_scr[g] = jnp.exp(m_prev[g] - m_new[g]) * acc_scr[g] + pv[g]
                m_scr[g] = m_new[g]

        def loop_body(kb, carry):
            step(kb, False)
            return carry

        lax.fori_loop(0, i, loop_body, 0)
        step(i, True)
        lane = lax.broadcasted_iota(jnp.int32, (tq, lanes), 1)
        outs = []
        for g in range(G):
            ones = _fox_lanes(g % 2)[0]
            acc = acc_scr[g]
            lse_ref[g] = m_scr[g] + jnp.log(acc[ones:ones + 1, :])
            acc_t = acc.T
            outs.append(acc_t / acc_t[:, ones:ones + 1])
        for pair in range(G // 2):
            o_ref[:, pair * lanes:(pair + 1) * lanes] = jnp.where(
                lane < HEAD_DIM, outs[2 * pair], outs[2 * pair + 1]).astype(o_ref.dtype)

    blk = pl.BlockSpec((G, tq, lanes), lambda h, i: (h, i, 0))
    full = pl.BlockSpec((G, T, lanes), lambda h, i: (h, 0, 0))
    grid = (H // G, nq)
    first = mix.shape[1] // (G * HEAD_DIM) - H // G
    body, x_in, x_in_specs, x_out, x_out_specs, x_scr = _carry(ex, grid, 4, 2, body)
    return pl.pallas_call(
        body,
        out_shape=(jax.ShapeDtypeStruct(mix.shape, mix.dtype), jax.ShapeDtypeStruct((H, nq, 1, tq), F32), *x_out),
        grid=grid,
        in_specs=[blk, full, full, pl.BlockSpec(memory_space=pl.ANY)] + x_in_specs,
        out_specs=(pl.BlockSpec((tq, G * HEAD_DIM), lambda h, i: (i, first + h)),
                   pl.BlockSpec((G, None, 1, tq), lambda h, i: (h, i, 0, 0)), *x_out_specs),
        input_output_aliases={3: 0},
        scratch_shapes=[pltpu.VMEM((G, 1, tq), F32), pltpu.VMEM((G, lanes, tq), F32)] + x_scr,
        compiler_params=_params(("arbitrary", "arbitrary")), name=name)(q_aug, k_aug, v_aug, mix, *x_in)


def _fox_prep_bwd(dmix, mix, *, name):
    T = dmix.shape[0]
    H = FOX_HEADS
    tm = _tile(T, 1408, BLOCK)
    lanes = 2 * HEAD_DIM
    first = mix.shape[1] // lanes - H // 2

    def body(d_ref, o_ref, da_ref):
        lane = lax.broadcasted_iota(jnp.int32, (1, lanes), 1)
        d2 = d_ref[...].astype(F32)
        prod = d2 * o_ref[...].astype(F32)
        for e in range(2):
            delta = jnp.sum(jnp.where(lane // HEAD_DIM == e, prod, 0.0), axis=1, keepdims=True)
            da_ref[e] = _lanes(lane, e, d2, _fox_lanes(e)[0], _split3(-delta)).astype(BF16)

    pair = pl.BlockSpec((tm, lanes), lambda p, i: (i, first + p))
    return pl.pallas_call(
        body, out_shape=jax.ShapeDtypeStruct((H, T, lanes), BF16), grid=(H // 2, T // tm),
        in_specs=[pair, pair],
        out_specs=pl.BlockSpec((2, tm, lanes), lambda p, i: (p, i, 0)),
        compiler_params=_params(("parallel", "parallel")), name=name)(dmix, mix)


def _fox_bwd(q_aug, k_aug, v_aug, do_aug, lse_row, dproj, *, ex=None, name):
    H, T, lanes = q_aug.shape
    tq = FOX_TILE
    nq = T // tq
    G = FOX_GROUP

    def side_by_side(tiles, scale=None):
        lane = lax.broadcasted_iota(jnp.int32, tiles[0].shape, 1)
        out = [jnp.where(lane < HEAD_DIM, tiles[2 * p], tiles[2 * p + 1]) for p in range(G // 2)]
        out = jnp.concatenate(out, axis=1)
        return out if scale is None else out * scale

    def body(q_ref, k_ref, v_ref, do_ref, lse_ref, dproj_in, out_ref, dcq_ref, dck_ref, dk_acc, dv_acc, dq_ref):
        j = pl.program_id(1)

        @pl.when(j == 0)
        def _():
            dq_ref[...] = jnp.zeros(dq_ref.shape, F32)
            dcq_ref[...] = jnp.zeros(dcq_ref.shape, F32)

        dk_acc[...] = jnp.zeros(dk_acc.shape, F32)
        dv_acc[...] = jnp.zeros(dv_acc.shape, F32)

        def step(qb, diag):
            off = pl.multiple_of(qb * tq, tq)
            heads = range(G)
            qa = [q_ref[g, pl.ds(off, tq), :] for g in heads]
            da = [do_ref[g, pl.ds(off, tq), :] for g in heads]
            s_t = [lax.dot_general(k_ref[g], qa[g], NT, preferred_element_type=F32) for g in heads]
            dp_t = [lax.dot_general(v_ref[g], da[g], NT, preferred_element_type=F32) for g in heads]
            p_t = [jnp.exp(s_t[g] - lse_ref[g, qb]) for g in heads]
            if diag:
                r = lax.broadcasted_iota(jnp.int32, (tq, tq), 0)
                c = lax.broadcasted_iota(jnp.int32, (tq, tq), 1)
                p_t = [jnp.where(c >= r, p, 0.0) for p in p_t]
            dsb = [(p_t[g] * dp_t[g]).astype(BF16) for g in heads]
            dv = [jnp.dot(p_t[g].astype(BF16), da[g], preferred_element_type=F32) for g in heads]
            dk = [jnp.dot(dsb[g], qa[g], preferred_element_type=F32) for g in heads]
            dq = [lax.dot_general(k_ref[g], dsb[g], TN, preferred_element_type=F32) for g in heads]
            for g in heads:
                dv_acc[g] += dv[g]
                dk_acc[g] += dk[g]
                dq_ref[g, qb] += dq[g]
                dcq_ref[g, qb] += jnp.sum(dsb[g].astype(F32), axis=0, keepdims=True)

        step(j, True)

        def loop_body(qb, carry):
            step(qb, False)
            return carry

        lax.fori_loop(j + 1, nq, loop_body, 0)
        dk = [dk_acc[g] for g in range(G)]
        out_ref[:, 0:wide] = side_by_side([dq_ref[g, j].T for g in range(G)], SCALE).astype(out_ref.dtype)
        out_ref[:, wide:2 * wide] = side_by_side(dk).astype(out_ref.dtype)
        out_ref[:, 2 * wide:3 * wide] = side_by_side([dv_acc[g] for g in range(G)]).astype(out_ref.dtype)
        for g in range(G):
            kc = _fox_lanes(g % 2)[0]
            dck_ref[g] = -dk[g].T[kc:kc + 1, :]

    blk = pl.BlockSpec((G, tq, lanes), lambda h, j: (h, j, 0))
    full = pl.BlockSpec((G, T, lanes), lambda h, j: (h, 0, 0))
    wide = G * HEAD_DIM
    first = dproj.shape[1] // (3 * wide) - H // G
    grid = (H // G, nq)
    body, x_in, x_in_specs, x_out, x_out_specs, x_scr = _carry(ex, grid, 6, 3, body)
    rows = jax.ShapeDtypeStruct((H, nq, 1, tq), F32)
    all_rows = pl.BlockSpec((G, nq, 1, tq), lambda h, j: (h, 0, 0, 0))
    return pl.pallas_call(
        body,
        out_shape=(jax.ShapeDtypeStruct(dproj.shape, dproj.dtype), rows, rows, *x_out),
        grid=grid,
        in_specs=[full, blk, blk, full, all_rows, pl.BlockSpec(memory_space=pl.ANY)] + x_in_specs,
        out_specs=(pl.BlockSpec((tq, 3 * wide), lambda h, j: (j, first + h)), all_rows,
                   pl.BlockSpec((G, None, 1, tq), lambda h, j: (h, j, 0, 0)), *x_out_specs),
        input_output_aliases={5: 0},
        scratch_shapes=[pltpu.VMEM((G, tq, lanes), F32), pltpu.VMEM((G, tq, lanes), F32),
                        pltpu.VMEM((G, nq, lanes, tq), F32)] + x_scr,
        compiler_params=_params(("arbitrary", "arbitrary")), name=name,
    )(q_aug, k_aug, v_aug, do_aug, lse_row, dproj, *x_in)


def _t5_bucket_np(d):
    n = np.maximum(d, 0).astype(np.int32)
    max_exact = N_BUCKETS // 2
    nf = np.maximum(n, 1).astype(np.float32)
    large = max_exact + (np.log(nf / max_exact) / math.log(MAX_DISTANCE / max_exact)
                         * (N_BUCKETS - max_exact)).astype(np.int32)
    large = np.minimum(large, N_BUCKETS - 1)
    return np.where(n < max_exact, n, large)


def _bucket_onehots():
    k = np.arange(BLOCK)[:, None]
    q = np.arange(BLOCK)[None, :]
    eye = np.eye(N_BUCKETS, dtype=np.float32)
    cur = eye[_t5_bucket_np(q - k).reshape(-1)]
    prev = eye[_t5_bucket_np(BLOCK + q - k).reshape(-1)]
    return cur, prev


SWA_K_COL = SWA_Q_HEADS * HEAD_DIM // (2 * HEAD_DIM)
SWA_V_COL = SWA_K_COL + 1


def _swa_terms(raw, bc, bp, far, sink, n):
    k = lax.broadcasted_iota(jnp.int32, (BLOCK, BLOCK), 0)
    q = lax.broadcasted_iota(jnp.int32, (BLOCK, BLOCK), 1)
    never = 2 * BLOCK
    s_c = raw[0] + bc
    s_p = raw[1] + bp
    s_m = raw[2] + jnp.where(n == 1, bp, far)
    s_c = jnp.where((k <= q) & (k >= jnp.where(n >= 1, 0, PAD_ROWS)), s_c, NEG)
    s_p = jnp.where(k > q + jnp.where(n >= 2, 0, never), s_p, NEG)
    s_m = jnp.where(k >= jnp.where(n >= 1, PAD_ROWS, never), s_m, NEG)
    m = jnp.maximum(jnp.maximum(jnp.max(s_c, axis=0, keepdims=True), jnp.max(s_p, axis=0, keepdims=True)),
                    jnp.maximum(jnp.max(s_m, axis=0, keepdims=True), sink))
    e = [jnp.exp(s_c - m), jnp.exp(s_p - m), jnp.exp(s_m - m)]
    e_s = jnp.exp(sink - m)
    l = (jnp.sum(e[0], axis=0, keepdims=True) + jnp.sum(e[1], axis=0, keepdims=True)
         + jnp.sum(e[2], axis=0, keepdims=True) + e_s)
    return e, e_s, l


SWA_STEP = 3


def _swa_specs():
    R = SWA_STEP

    def window(col):
        return ([pl.BlockSpec((BLOCK, BLOCK), lambda s, w=w: (jnp.maximum(R * s - 1 + w, 0), col)) for w in range(R + 1)]
                + [pl.BlockSpec((BLOCK, BLOCK), lambda s: (0, col))])

    qblk = pl.BlockSpec((R * BLOCK, SWA_Q_HEADS * HEAD_DIM), lambda s: (s, 0))
    bias = pl.BlockSpec((SWA_Q_HEADS, BLOCK, BLOCK), lambda s: (0, 0, 0))
    smem = pl.BlockSpec(memory_space=pltpu.SMEM)
    return qblk, window(SWA_K_COL), window(SWA_V_COL), bias, smem


def _swa_own_kv(tile_ref, kv):
    lane = lax.broadcasted_iota(jnp.int32, (BLOCK, 2 * HEAD_DIM), 1)
    t = tile_ref[...].astype(F32)
    return jnp.where(lane // HEAD_DIM == kv, t, pltpu.roll(t, HEAD_DIM, 1)).astype(BF16)


def _swa_fwd(proj, bc, bp, far, sinks, *, name):
    T = proj.shape[0]
    nb = T // BLOCK
    G = SWA_GROUP
    Hq = SWA_Q_HEADS
    lanes = 2 * HEAD_DIM

    R = SWA_STEP
    assert nb % R == 0

    def body(*refs):
        q_ref, k_refs, v_refs = refs[0], refs[1:R + 3], refs[R + 3:2 * R + 5]
        bc_ref, bp_ref, far_ref, sink_ref, o_ref = refs[2 * R + 5:]
        s = pl.program_id(0)
        lane = lax.broadcasted_iota(jnp.int32, (BLOCK, lanes), 1)
        kvs = range(SWA_KV_HEADS)
        kk = [[_swa_own_kv(ref, kv) for ref in k_refs] for kv in kvs]
        vv = [[_swa_own_kv(ref, kv) for ref in v_refs] for kv in kvs]
        chains = [(r, h) for r in range(R) for h in range(Hq)]
        tiles = lambda r: (r + 1, r, R + 1)
        q2 = {(r, pair): q_ref[r * BLOCK:(r + 1) * BLOCK, pair * lanes:(pair + 1) * lanes].astype(F32) * SCALE
              for r in range(R) for pair in range(Hq // 2)}
        qm = {c: jnp.where(lane // HEAD_DIM == c[1] % 2, q2[c[0], c[1] // 2], 0.0).astype(BF16) for c in chains}
        raw = {c: [lax.dot_general(kk[c[1] // G][w], qm[c], NT, preferred_element_type=F32) for w in tiles(c[0])]
               for c in chains}
        terms = {c: _swa_terms(raw[c], bc_ref[c[1]], bp_ref[c[1]], far_ref[c[1]], sink_ref[c[1]], R * s + c[0])
                 for c in chains}
        o_t = {c: sum(lax.dot_general(vv[c[1] // G][w], terms[c][0][b].astype(BF16), TN, preferred_element_type=F32)
                      for b, w in enumerate(tiles(c[0]))) for c in chains}
        outs = {c: (o_t[c] / terms[c][2]).T for c in chains}
        for r in range(R):
            for pair in range(Hq // 2):
                o_ref[r * BLOCK:(r + 1) * BLOCK, pair * lanes:(pair + 1) * lanes] = jnp.where(
                    lane < HEAD_DIM, outs[r, 2 * pair], outs[r, 2 * pair + 1]).astype(o_ref.dtype)

    qblk, keys, vals, bias, smem = _swa_specs()
    return pl.pallas_call(
        body, out_shape=jax.ShapeDtypeStruct((T, D_MODEL), BF16), grid=(nb // R,),
        in_specs=[qblk] + keys + vals + [bias, bias, smem, smem],
        out_specs=qblk,
        compiler_params=_params(("parallel",)), name=name,
    )(proj, *([proj] * (2 * R + 4)), bc, bp, far, sinks)


def _swa_bwd(proj, dmix, bc, bp, far, sinks, *, ex=None, name):
    T, width = proj.shape
    nb = T // BLOCK
    G = SWA_GROUP
    Hq = SWA_Q_HEADS
    lanes = 2 * HEAD_DIM
    qw = Hq * HEAD_DIM
    own_w = qw + 2 * lanes

    R = SWA_STEP
    assert nb % R == 0
    n_in = 2 * R + 10

    def body(*refs):
        q_ref, k_refs, v_refs = refs[0], refs[1:R + 3], refs[R + 3:2 * R + 5]
        do_ref, bc_ref, bp_ref, far_ref, sink_ref = refs[2 * R + 5:n_in]
        dp_ref, dbc_ref, dbp_ref, dbf_ref, dsk_ref, dk_acc, dv_acc = refs[n_in:]
        s = pl.program_id(0)

        @pl.when(s == 0)
        def _():
            for ref in (dk_acc, dv_acc, dbc_ref, dbp_ref, dbf_ref, dsk_ref):
                ref[...] = jnp.zeros(ref.shape, F32)

        lane = lax.broadcasted_iota(jnp.int32, (BLOCK, lanes), 1)
        kvs = range(SWA_KV_HEADS)
        kk = [[_swa_own_kv(ref, kv) for ref in k_refs] for kv in kvs]
        vv = [[_swa_own_kv(ref, kv) for ref in v_refs] for kv in kvs]
        chains = [(r, h) for r in range(R) for h in range(Hq)]
        blocks = range(3)
        tiles = lambda r: (r + 1, r, R + 1)
        sub = lambda ref, r, pair: ref[r * BLOCK:(r + 1) * BLOCK, pair * lanes:(pair + 1) * lanes]
        q2 = {(r, pair): sub(q_ref, r, pair).astype(F32) * SCALE for r in range(R) for pair in range(Hq // 2)}
        d2 = {(r, pair): sub(do_ref, r, pair) for r in range(R) for pair in range(Hq // 2)}
        own = [lane // HEAD_DIM == half for half in range(2)]
        qm = {c: jnp.where(own[c[1] % 2], q2[c[0], c[1] // 2], 0.0).astype(BF16) for c in chains}
        dom = {c: jnp.where(own[c[1] % 2], d2[c[0], c[1] // 2], jnp.zeros_like(d2[0, 0])) for c in chains}
        raw = {c: [lax.dot_general(kk[c[1] // G][w], qm[c], NT, preferred_element_type=F32) for w in tiles(c[0])]
               for c in chains}
        dp = {c: [lax.dot_general(vv[c[1] // G][w], dom[c], NT, preferred_element_type=F32) for w in tiles(c[0])]
              for c in chains}
        p, ds16 = {}, {}
        for c in chains:
            r, h = c
            n = R * s + r
            e, e_s, l = _swa_terms(raw[c], bc_ref[h], bp_ref[h], far_ref[h], sink_ref[h], n)
            inv = 1.0 / l
            ph = [e[b] * inv for b in blocks]
            delta = sum(jnp.sum(ph[b] * dp[c][b], axis=0, keepdims=True) for b in blocks)
            ds = [ph[b] * (dp[c][b] - delta) for b in blocks]
            dsk_ref[h] += -(e_s * inv) * delta
            dbc_ref[h] += ds[0]
            dbp_ref[h] += ds[1] + jnp.where(n == 1, ds[2], 0.0)
            dbf_ref[h] += jnp.where(n >= 2, ds[2], 0.0)
            p[c] = [x.astype(BF16) for x in ph]
            ds16[c] = [x.astype(BF16) for x in ds]
        dq_t = {c: sum(lax.dot_general(kk[c[1] // G][w], ds16[c][b], TN, preferred_element_type=F32)
                       for b, w in enumerate(tiles(c[0]))) for c in chains}
        group = [range(kv * G, (kv + 1) * G) for kv in kvs]
        dk = {(r, kv): [sum(jnp.dot(ds16[r, h][b], qm[r, h], preferred_element_type=F32) for h in group[kv])
                        for b in blocks] for r in range(R) for kv in kvs}
        dv = {(r, kv): [sum(jnp.dot(p[r, h][b], dom[r, h], preferred_element_type=F32) for h in group[kv])
                        for b in blocks] for r in range(R) for kv in kvs}
        for r in range(R):
            n = R * s + r
            rows = pl.ds(pl.multiple_of(n * BLOCK, BLOCK), BLOCK)
            prev_rows = pl.ds(pl.multiple_of(jnp.maximum(n - 1, 0) * BLOCK, BLOCK), BLOCK)
            for pair in range(Hq // 2):
                dp_ref[rows, pair * lanes:(pair + 1) * lanes] = (jnp.where(
                    lane < HEAD_DIM, dq_t[r, 2 * pair].T, dq_t[r, 2 * pair + 1].T) * SCALE).astype(dp_ref.dtype)
            for acc, ref in ((dk, dk_acc), (dv, dv_acc)):
                tot = [[a + pltpu.roll(a, HEAD_DIM, 1) for a in acc[r, kv]] for kv in kvs]
                both = [jnp.where(lane < HEAD_DIM, tot[0][b], tot[1][b]) for b in blocks]
                ref[rows, :] += both[0]
                ref[prev_rows, :] += both[1]
                ref[0:BLOCK, :] += both[2]

        @pl.when(s == nb // R - 1)
        def _():
            dp_ref[:, qw:qw + lanes] = dk_acc[...].astype(dp_ref.dtype)
            dp_ref[:, qw + lanes:own_w] = dv_acc[...].astype(dp_ref.dtype)

    qblk, keys, vals, bias, smem = _swa_specs()
    dsk = pl.BlockSpec((Hq, 1, BLOCK), lambda s: (0, 0, 0))
    grid = (nb // R,)
    body, x_in, x_in_specs, x_out, x_out_specs, x_scr = _carry(ex, grid, n_in, 5, body)
    tile = jax.ShapeDtypeStruct((Hq, BLOCK, BLOCK), F32)
    return pl.pallas_call(
        body,
        out_shape=(jax.ShapeDtypeStruct((T, width), BF16), tile, tile, tile,
                   jax.ShapeDtypeStruct((Hq, 1, BLOCK), F32), *x_out),
        grid=grid,
        in_specs=[qblk] + keys + vals + [qblk, bias, bias, smem, smem] + x_in_specs,
        out_specs=(pl.BlockSpec((T, own_w), lambda s: (0, 0)), bias, bias, bias, dsk, *x_out_specs),
        scratch_shapes=[pltpu.VMEM((T, lanes), F32), pltpu.VMEM((T, lanes), F32)] + x_scr,
        compiler_params=_params(("arbitrary",)), name=name,
    )(proj, *([proj] * (2 * R + 4)), dmix, bc, bp, far, sinks, *x_in)


def _bias_tiles(tab_t, oh_cur_t, oh_prev_t, *, name):
    Hq = tab_t.shape[0]

    def body(t_ref, oc_ref, op_ref, bc_ref, bp_ref):
        bc_ref[...] = jnp.dot(t_ref[...], oc_ref[...], precision=HIGHEST, preferred_element_type=F32)
        bp_ref[...] = jnp.dot(t_ref[...], op_ref[...], precision=HIGHEST, preferred_element_type=F32)

    vm = pl.BlockSpec(memory_space=pltpu.VMEM)
    shp = jax.ShapeDtypeStruct((Hq, BLOCK * BLOCK), F32)
    bc, bp = pl.pallas_call(body, out_shape=(shp, shp), in_specs=[vm] * 3, out_specs=(vm, vm),
                            compiler_params=_params(), name=name)(tab_t, oh_cur_t, oh_prev_t)
    return bc.reshape(Hq, BLOCK, BLOCK), bp.reshape(Hq, BLOCK, BLOCK)


def _small_grads(dbc, dbp, dbf, dsk, oh_cur, oh_prev, *, ex=None, name):
    Hq = dbc.shape[0]

    def body(dbc_ref, dbp_ref, dbf_ref, dsk_ref, oc_ref, op_ref, tab_ref, sink_ref):
        tab = (jnp.dot(dbc_ref[...], oc_ref[...], precision=HIGHEST, preferred_element_type=F32)
               + jnp.dot(dbp_ref[...], op_ref[...], precision=HIGHEST, preferred_element_type=F32))
        far = jnp.sum(dbf_ref[...], axis=1, keepdims=True)
        last = lax.broadcasted_iota(jnp.int32, (Hq, N_BUCKETS), 1) == N_BUCKETS - 1
        tab_ref[...] = tab + jnp.where(last, far, 0.0)
        sink_ref[...] = jnp.sum(dsk_ref[...], axis=1, keepdims=True)

    vm = pl.BlockSpec(memory_space=pltpu.VMEM)
    body, x_in, x_in_specs, x_out, x_out_specs, x_scr = _carry(ex, (), 6, 2, body)
    return pl.pallas_call(
        body, out_shape=(jax.ShapeDtypeStruct((Hq, N_BUCKETS), F32), jax.ShapeDtypeStruct((Hq, 1), F32), *x_out),
        in_specs=[vm] * 6 + x_in_specs, out_specs=(vm, vm, *x_out_specs), scratch_shapes=x_scr,
        compiler_params=_params(), name=name,
    )(dbc.reshape(Hq, -1), dbp.reshape(Hq, -1), dbf.reshape(Hq, -1), dsk.reshape(Hq, -1), oh_cur, oh_prev, *x_in)


def _coords():
    return lax.axis_index("x"), lax.axis_index("y"), lax.axis_index("c")


class _Exchange:
    def __init__(self, inputs, out_shapes, scratch, start, finish):
        self.inputs, self.out_shapes, self.scratch, self.start, self.finish = inputs, out_shapes, scratch, start, finish


def _carry(ex, grid, n_in, n_out, body):
    if ex is None:
        return body, [], [], [], [], []
    ni, no = len(ex.inputs), len(ex.out_shapes)

    def at_step(which):
        cond = jnp.bool_(True)
        for axis, n in enumerate(grid):
            cond = cond & (pl.program_id(axis) == (0 if which == "first" else n - 1))
        return cond

    def wrapped(*refs):
        refs = list(refs)
        n_own_scr = len(refs) - (n_in + ni + n_out + no) - len(ex.scratch)
        own_in, side_in = refs[:n_in], refs[n_in:n_in + ni]
        own_out = refs[n_in + ni:n_in + ni + n_out]
        side_out = refs[n_in + ni + n_out:n_in + ni + n_out + no]
        rest = refs[n_in + ni + n_out + no:]
        own_scr, sems = rest[:n_own_scr], rest[n_own_scr:]

        @pl.when(at_step("first"))
        def _():
            ex.start(side_in, side_out, sems)

        body(*own_in, *own_out, *own_scr)

        @pl.when(at_step("last"))
        def _():
            ex.finish(side_in, side_out, sems)

    hbm = pl.BlockSpec(memory_space=pl.ANY)
    return wrapped, list(ex.inputs), [hbm] * ni, list(ex.out_shapes), [hbm] * no, list(ex.scratch)


def _gather_exchange(shards):
    nt = len(shards)

    def copies(ins, outs, sems):
        send_sems, recv_sems, local_sems = sems
        x, y, c = _coords()
        me, sibling = (x, y, c), (x, y, 1 - c)
        chips = [(1 - x, y), (x, 1 - y), (1 - x, 1 - y)]

        def slot(t, dev):
            return outs[t].at[4 * dev[0] + 2 * dev[1] + dev[2]]

        def copy(t, k, block, to, src=None):
            dst = slot(t, block)
            return pltpu.make_async_remote_copy(
                src_ref=dst if src is None else src, dst_ref=dst,
                send_sem=send_sems.at[t, k], recv_sem=recv_sems.at[t, k], device_id=to, device_id_type=MESH)

        mine = [pltpu.make_async_copy(ins[t], slot(t, me), local_sems.at[t]) for t in range(nt)]
        first = []
        for t in range(nt):
            first.append(copy(t, 0, me, sibling, src=ins[t]))
            first += [copy(t, 1 + j, me, (*chip, c), src=ins[t]) for j, chip in enumerate(chips)]
        return copy, mine, first, me, sibling, chips, c

    def start(ins, outs, sems):
        _, mine, first, *_ = copies(ins, outs, sems)
        for cp in mine + first:
            cp.start()

    def finish(ins, outs, sems):
        copy, mine, first, me, sibling, chips, c = copies(ins, outs, sems)
        passed = []
        for j, chip in enumerate(chips):
            for t in range(nt):
                copy(t, 1 + j, (*chip, c), me).wait_recv()
                cp = copy(t, 4 + j, (*chip, c), sibling)
                cp.start()
                passed.append(cp)
        for t in range(nt):
            copy(t, 0, sibling, me).wait_recv()
            for j, chip in enumerate(chips):
                copy(t, 4 + j, (*chip, 1 - c), me).wait_recv()
        for cp in first + passed:
            cp.wait_send()
        for cp in mine:
            cp.wait()

    return _Exchange(
        list(shards), [jax.ShapeDtypeStruct((N_DEV,) + s.shape, s.dtype) for s in shards],
        [pltpu.SemaphoreType.DMA((nt, 7)), pltpu.SemaphoreType.DMA((nt, 7)), pltpu.SemaphoreType.DMA((nt,))],
        start, finish)


def _swap_exchange(arrays, n_slices, copies):
    nt = len(arrays)

    def start(ins, outs, sems):
        for cp in copies(ins, outs, sems):
            cp.start()

    def finish(ins, outs, sems):
        sends = copies(ins, outs, sems)
        for cp in sends:
            cp.wait_recv()
        for cp in sends:
            cp.wait_send()

    return _Exchange(
        list(arrays), [jax.ShapeDtypeStruct((n_slices,) + a.shape[1:], a.dtype) for a in arrays],
        [pltpu.SemaphoreType.DMA((nt, n_slices)), pltpu.SemaphoreType.DMA((nt, n_slices))], start, finish)


def _cores_exchange(gs):
    def copies(ins, outs, sems):
        send_sems, recv_sems = sems
        x, y, c = _coords()
        return [pltpu.make_async_remote_copy(
            src_ref=ins[t].at[2 * j + (1 - c)], dst_ref=outs[t].at[j],
            send_sem=send_sems.at[t, j], recv_sem=recv_sems.at[t, j], device_id=(x, y, 1 - c), device_id_type=MESH)
            for t in range(len(gs)) for j in range(4)]

    return _swap_exchange(gs, 4, copies)


def _chips_exchange(ps):
    def copies(ins, outs, sems):
        send_sems, recv_sems = sems
        x, y, c = _coords()
        peers = [(1 - x, y), (x, 1 - y), (1 - x, 1 - y)]
        return [pltpu.make_async_remote_copy(
            src_ref=ins[t].at[2 * px + py], dst_ref=outs[t].at[k],
            send_sem=send_sems.at[t, k], recv_sem=recv_sems.at[t, k], device_id=(px, py, c), device_id_type=MESH)
            for t in range(len(ps)) for k, (px, py) in enumerate(peers)]

    return _swap_exchange(ps, 3, copies)


def _add_cores(g, r, core, *, name):
    _, A, B = g.shape
    ta = _tile(A, 512, 16)

    def body(core_ref, a_ref, b_ref, o16_ref):
        o16_ref[...] = (a_ref[...] + b_ref[...]).astype(BF16)

    blk = (None, ta, B)
    return pl.pallas_call(
        body, out_shape=jax.ShapeDtypeStruct((4, A, B), BF16),
        grid_spec=pltpu.PrefetchScalarGridSpec(
            num_scalar_prefetch=1, grid=(4, A // ta),
            in_specs=[pl.BlockSpec(blk, lambda j, i, core_ref: (2 * j + core_ref[0], i, 0)),
                      pl.BlockSpec(blk, lambda j, i, core_ref: (j, i, 0))],
            out_specs=pl.BlockSpec(blk, lambda j, i, core_ref: (j, i, 0))),
        compiler_params=_params(("parallel", "parallel")), name=name)(core, g, r)


def _adamw_math(w, g, m, v):
    m = ADAM_B1 * m + (1.0 - ADAM_B1) * g
    v = ADAM_B2 * v + (1.0 - ADAM_B2) * (g * g)
    m_hat = m / (1.0 - ADAM_B1 ** ADAM_STEP)
    v_hat = v / (1.0 - ADAM_B2 ** ADAM_STEP)
    delta = -ADAM_LR * (m_hat / (jnp.sqrt(v_hat) + ADAM_EPS) + ADAM_WD * w)
    return delta, m, v


def _sum_adamw(mine, sib, r, where, w, m, v, *, ta, name):
    Aw, Bw = w.shape
    Bg = mine.shape[2]
    assert Aw % ta == 0 and Bw <= Bg and mine.shape[1] == Aw

    def body(where_ref, p_ref, s_ref, r0, r1, r2, w_ref, m_ref, v_ref, g_out, d_out, m_out, v_out):
        g = (((p_ref[:, :Bw] + s_ref[:, :Bw]) + r0[:, :Bw].astype(F32))
             + r1[:, :Bw].astype(F32)) + r2[:, :Bw].astype(F32)
        delta, m_new, v_new = _adamw_math(w_ref[...], g, m_ref[...], v_ref[...])
        g_out[...] = g
        d_out[...] = delta
        m_out[...] = m_new
        v_out[...] = v_new

    gblk = (None, ta, Bg)
    row = pl.BlockSpec((ta, Bw), lambda i, where_ref: (i, 0))
    rspecs = [pl.BlockSpec(gblk, (lambda i, where_ref, k=k: (k, i, 0))) for k in range(3)]
    shp = jax.ShapeDtypeStruct((Aw, Bw), F32)
    return pl.pallas_call(
        body, out_shape=(shp, shp, shp, shp),
        grid_spec=pltpu.PrefetchScalarGridSpec(
            num_scalar_prefetch=1, grid=(Aw // ta,),
            in_specs=[pl.BlockSpec(gblk, lambda i, where_ref: (2 * where_ref[0] + where_ref[1], i, 0)),
                      pl.BlockSpec(gblk, lambda i, where_ref: (where_ref[0], i, 0))] + rspecs + [row, row, row],
            out_specs=(row, row, row, row)),
        compiler_params=_params(("parallel",)), name=name)(where, mine, sib, r, r, r, w, m, v)


def _adamw(w, g, m, v, *, name):
    def body(w_ref, g_ref, m_ref, v_ref, d_out, m_out, v_out):
        delta, m_new, v_new = _adamw_math(w_ref[...], g_ref[...], m_ref[...], v_ref[...])
        d_out[...] = delta
        m_out[...] = m_new
        v_out[...] = v_new

    vm = pl.BlockSpec(memory_space=pltpu.VMEM)
    shp = jax.ShapeDtypeStruct(w.shape, F32)
    return pl.pallas_call(body, out_shape=(shp, shp, shp), in_specs=[vm] * 4, out_specs=(vm, vm, vm),
                          compiler_params=_params(), name=name)(w, g, m, v)


def _small_allreduce_adamw(s, w, m, v, *, name):
    R, W = s.shape

    def body(s_ref, w_ref, m_ref, v_ref, g_out, d_out, m_out, v_out, gath, send_sems, recv_sems):
        x, y, c = _coords()
        mine = 4 * x + 2 * y + c
        gath[mine] = s_ref[...]
        peers = [((1 - x) if k & 4 else x, (1 - y) if k & 2 else y, (1 - c) if k & 1 else c) for k in range(1, N_DEV)]
        sends = []
        for k in range(1, N_DEV):
            peer = peers[k - 1]
            sends.append(pltpu.make_async_remote_copy(
                src_ref=s_ref, dst_ref=gath.at[mine], send_sem=send_sems.at[k - 1], recv_sem=recv_sems.at[k - 1],
                device_id=peer, device_id_type=MESH))
        for cp in sends:
            cp.start()
        for k in range(1, N_DEV):
            peer = peers[k - 1]
            pltpu.make_async_remote_copy(
                src_ref=s_ref, dst_ref=gath.at[4 * peer[0] + 2 * peer[1] + peer[2]],
                send_sem=send_sems.at[k - 1], recv_sem=recv_sems.at[k - 1],
                device_id=peer, device_id_type=MESH).wait_recv()
        for cp in sends:
            cp.wait_send()
        g = gath[0]
        for d in range(1, N_DEV):
            g = g + gath[d]
        delta, m_new, v_new = _adamw_math(w_ref[...], g, m_ref[...], v_ref[...])
        g_out[...] = g
        d_out[...] = delta
        m_out[...] = m_new
        v_out[...] = v_new

    vm = pl.BlockSpec(memory_space=pltpu.VMEM)
    shp = jax.ShapeDtypeStruct((R, W), F32)
    return pl.pallas_call(
        body, out_shape=(shp, shp, shp, shp), in_specs=[vm] * 4, out_specs=(vm, vm, vm, vm),
        scratch_shapes=[pltpu.VMEM((N_DEV, R, W), F32), pltpu.SemaphoreType.DMA((N_DEV - 1,)),
                        pltpu.SemaphoreType.DMA((N_DEV - 1,))],
        compiler_params=_params(), name=name)(s, w, m, v)


def _pack_small(rel_bias, g1, g2, g3, g4, b_forget, sinks, extra=None, meta=None):
    misc = jnp.concatenate([rel_bias.reshape(-1), b_forget.reshape(-1), sinks.reshape(-1)])
    misc = jnp.concatenate([misc, jnp.zeros((D_MODEL - misc.shape[0],), F32)])[None]
    last = jnp.zeros((1, D_MODEL), F32) if extra is None else extra
    meta = jnp.zeros((N_META, D_MODEL), F32) if meta is None else meta
    return jnp.concatenate([g1, g2, g3, g4, misc, last, jnp.zeros((2, D_MODEL), F32), meta], axis=0)


def _unpack_small(p):
    nrb = N_BUCKETS * SWA_Q_HEADS
    misc = p[4]
    return dict(rel_bias=misc[:nrb].reshape(N_BUCKETS, SWA_Q_HEADS), ln_pre_mix=p[0:1], ln_post_mix=p[1:2],
                ln_pre_ffn=p[2:3], ln_post_ffn=p[3:4], b_forget=misc[nrb:nrb + 8].reshape(1, 8),
                sinks=misc[nrb + 8:nrb + 16].reshape(1, 8))


def _proj_runs():
    gw = FOX_GROUP * HEAD_DIM
    swa = SWA_Q_W + 2 * SWA_KV_HEADS * HEAD_DIM
    runs = [(0, swa)]
    for grp in range(FOX_HEADS // FOX_GROUP):
        runs += [(swa + part * FOX_W + grp * gw, swa + part * FOX_W + (grp + 1) * gw) for part in range(3)]
    return runs


def _columns_from_shards(gathered, runs, shard):
    pieces = []
    for start, stop in runs:
        for d in range(start // shard, (stop - 1) // shard + 1):
            lo = d * shard
            pieces.append(gathered[d][:, max(start, lo) - lo:min(stop, lo + shard) - lo])
    return jnp.concatenate(pieces, axis=1)


def _device_shards(qkv, gate, shard, padded):
    pos, segments = 0, []
    for start, stop in _proj_runs():
        segments.append((start, stop, qkv, pos))
        pos += stop - start
    segments.append((pos, pos + gate.shape[1], gate, 0))
    total = pos + gate.shape[1]
    assert total % shard == 0
    zeros = jnp.zeros((qkv.shape[0], padded - shard), qkv.dtype)
    out = []
    for d in range(total // shard):
        lo, hi = d * shard, (d + 1) * shard
        pieces = [arr[:, src + max(lo, s) - s:src + min(hi, e) - s]
                  for s, e, arr, src in sorted(segments, key=lambda seg: seg[0]) if max(lo, s) < min(hi, e)]
        out.append(jnp.concatenate(pieces + [zeros], axis=1))
    return jnp.stack(out)


def kernel(x, meta_tokens, rel_bias, ln_pre_mix, ln_post_mix, ln_pre_ffn, ln_post_ffn, w_in, b_forget, sinks, w_out, w_gate_up, w_down, loss_target, m_meta_tokens, m_rel_bias, m_ln_pre_mix, m_ln_post_mix, m_ln_pre_ffn, m_ln_post_ffn, m_w_in, m_b_forget, m_sinks, m_w_out, m_w_gate_up, m_w_down, v_meta_tokens, v_rel_bias, v_ln_pre_mix, v_ln_post_mix, v_ln_pre_ffn, v_ln_post_ffn, v_w_in, v_b_forget, v_sinks, v_w_out, v_w_gate_up, v_w_down):
    seq = x.shape[1]
    T = BLOCK + seq
    assert T % FOX_TILE == 0
    nq = T // FOX_TILE
    tm = _tile(T, 1056)
    cin = w_in.shape[2]
    hid = w_down.shape[1]
    F = N_DEV * hid
    assert w_gate_up.shape[2] == 2 * hid and cin <= W_IN_PAD and hid % 16 == 0

    x_i, y_i, c_i = _coords()
    core = jnp.reshape(c_i, (1,)).astype(jnp.int32)
    where = jnp.stack([2 * x_i + y_i, c_i]).astype(jnp.int32)
    w_in_s = jnp.pad(w_in[0].astype(BF16), ((0, 0), (0, W_IN_PAD - cin)))
    w_gu_t = w_gate_up[0].T
    h0, target, hn1, hn1_t, g_in, _ = _pad_rows_rms(x[0], loss_target[0], ln_pre_mix,
                                                    _gather_exchange([w_in_s, meta_tokens]), name="ag_w_in_rms_pre_mix")
    gather_rest = _gather_exchange([w_out[0].astype(BF16), w_gu_t.astype(BF16), w_down[0].astype(BF16)])
    w_qkv = _columns_from_shards(g_in, _proj_runs(), cin)
    w_f = jnp.pad(_columns_from_shards(g_in, [(D_QKV, D_PROJ)], cin), ((0, 0), (0, BLOCK - FOX_HEADS)))

    proj = _matmul(hn1, w_qkv, out_dtype=BF16, tm=tm, tn=D_QKV, name="mm_in_proj")
    proj_f = _matmul(hn1, w_f, out_dtype=F32, tm=tm, tn=BLOCK, name="mm_in_proj_f")

    f_t = proj_f[:, :FOX_HEADS].T
    bf_col = b_forget.reshape(FOX_HEADS, 1)

    oh_cur, oh_prev = _bucket_onehots()
    bias_c, bias_p = _bias_tiles(rel_bias.T, jnp.asarray(oh_cur.T), jnp.asarray(oh_prev.T), name="bias_tiles")
    far = rel_bias[N_BUCKETS - 1]
    sink_v = sinks[0]
    mix_a = _swa_fwd(proj, bias_c, bias_p, far, sink_v, name="swa_fwd")

    cum_col = _fox_gates_fwd(f_t, bf_col, name="fox_gates_fwd")
    q_b, k_b, v_b = _fox_prep(proj, cum_col, name="fox_prep")
    mix, lse_row, g_out, g_gu, g_down = _fox_fwd(q_b, k_b, v_b, mix_a, ex=gather_rest, name="fox_fwd")
    w_out_full = g_out.reshape(D_MODEL, D_MODEL)
    w_gu_full_t = g_gu.reshape(2 * F, D_MODEL)
    w_down_full = g_down.reshape(F, D_MODEL)

    a1 = _matmul(mix, w_out_full, out_dtype=F32, tm=tm, tn=D_MODEL, name="mm_out_proj")
    h1, hn2 = _post_res_norm(a1, ln_post_mix, h0, ln_pre_ffn, name="post_mix_pre_ffn")
    gate, up, act, act_t = _gate_up_swiglu(hn2, w_gu_full_t, name="mm_gate_up")
    ff = _matmul(act, w_down_full, out_dtype=F32, tm=tm, tn=512, name="mm_down")
    dh2, dff, dg_post_ffn, loss_acc = _loss_head(ff, ln_post_ffn, h1, target, name="loss_head")

    dgu = _d_act_swiglu(dff, w_down_full, gate, up, name="mm_d_act")
    d_w_down = _matmul(act_t, dff, out_dtype=F32, tm=_tile(F, 768), tn=512, name="mm_dw_down")
    dhn2 = _matmul(dgu, w_gu_full_t, out_dtype=F32, tm=tm, tn=512, tk=F, name="mm_d_hn2")
    d_w_gu_t = _matmul(dgu, hn2, ta=True, out_dtype=F32, tm=256, tn=D_MODEL, name="mm_dw_gate_up")
    dh1, dg_pre_ffn, da1, dg_post_mix = _rms_bwd_twice(h1, ln_pre_ffn, dhn2, dh2, a1, ln_post_mix,
                                                       name="rms_bwd_pre_ffn_post_mix")
    dmix = _matmul(da1, w_out_full, nt=True, out_dtype=BF16, tm=tm, tn=D_MODEL, name="mm_d_mix")
    d_w_out = _matmul(mix, da1, ta=True, out_dtype=F32, tm=512, tn=D_MODEL, name="mm_dw_out")

    ffn_grads = [g.reshape(N_DEV, -1, D_MODEL) for g in (d_w_out, d_w_gu_t, d_w_down)]
    dproj_a, dbc, dbp, dbf, dsk, *ffn_sibling = _swa_bwd(
        proj, dmix, bias_c, bias_p, far, sink_v, ex=_cores_exchange(ffn_grads), name="swa_bwd")
    ffn_sums = [_add_cores(g, r, core, name="rs_add_" + t)
                for g, r, t in zip(ffn_grads, ffn_sibling, ["w_out", "w_gate_up", "w_down"])]

    do_b = _fox_prep_bwd(dmix, mix, name="fox_prep_bwd")
    dproj, dcq, dck, *ffn_chips = _fox_bwd(
        q_b, k_b, v_b, do_b, lse_row, dproj_a, ex=_chips_exchange(ffn_sums), name="fox_bwd")
    df_t, d_bf = _fox_gates_bwd(dcq.reshape(FOX_HEADS, T), dck.reshape(FOX_HEADS, T), f_t, bf_col,
                                name="fox_gates_bwd")
    df = jnp.pad(df_t.T.astype(BF16), ((0, 0), (0, BLOCK - FOX_HEADS)))

    d_w_qkv = _matmul(hn1_t, dproj, out_dtype=F32, tm=512, tn=768, name="mm_dw_in")
    d_w_f = _matmul(hn1_t, df, out_dtype=F32, tm=512, tn=BLOCK, name="mm_dw_in_f")
    d_w_in = _device_shards(d_w_qkv, d_w_f[:, :FOX_HEADS], cin, W_IN_PAD)
    d_tab, d_sink, in_sibling = _small_grads(dbc, dbp, dbf, dsk, jnp.asarray(oh_cur), jnp.asarray(oh_prev),
                                             ex=_cores_exchange([d_w_in]), name="small_grads")
    in_sum = _add_cores(d_w_in, in_sibling, core, name="rs_add_w_in")
    dhn1, in_chips = _matmul(dproj, w_qkv, nt=True, out_dtype=F32, tm=tm, tn=512,
                             ex=_chips_exchange([in_sum]), name="mm_d_hn1")
    dx_rows, dg_pre_mix, dh0_head = _rms_bwd_rows(h0, ln_pre_mix, dhn1, df, w_f, dh1, name="rms_bwd_pre_mix")
    grad_x = dx_rows[None]
    d_meta = dh0_head[PAD_ROWS:]

    rs_out, rs_gu, rs_down = zip(ffn_grads, ffn_sibling, ffn_chips)
    updates = [("w_in", (d_w_in, in_sibling, in_chips), (w_in[0], m_w_in[0], v_w_in[0]), 256),
               ("w_out", rs_out, (w_out[0], m_w_out[0], v_w_out[0]), BLOCK),
               ("w_gate_up", rs_gu, (w_gu_t, m_w_gate_up[0].T, v_w_gate_up[0].T), hid),
               ("w_down", rs_down, (w_down[0], m_w_down[0], v_w_down[0]), hid)]
    big = [{}, {}, {}, {}]
    for t, grads, shard, ta in updates:
        res = _sum_adamw(*grads, where, *shard, ta=ta, name="rs_adamw_" + t)
        for kind in range(4):
            big[kind][t] = (res[kind].T if t == "w_gate_up" else res[kind])[None]

    loss_row = jnp.pad(loss_acc[0:1, 0:1] * (0.5 / D_MODEL), ((0, 0), (0, D_MODEL - 1)))
    s_small = _pack_small(d_tab.T, dg_pre_mix, dg_post_mix, dg_pre_ffn, dg_post_ffn, d_bf, d_sink,
                          extra=loss_row, meta=d_meta)
    w_s = _pack_small(rel_bias, ln_pre_mix, ln_post_mix, ln_pre_ffn, ln_post_ffn, b_forget, sinks)
    m_s = _pack_small(m_rel_bias, m_ln_pre_mix, m_ln_post_mix, m_ln_pre_ffn, m_ln_post_ffn, m_b_forget, m_sinks)
    v_s = _pack_small(v_rel_bias, v_ln_pre_mix, v_ln_post_mix, v_ln_pre_ffn, v_ln_post_ffn, v_b_forget, v_sinks)
    small = _small_allreduce_adamw(s_small, w_s, m_s, v_s, name="small_allreduce_adamw")
    loss = small[0][5, 0]
    mcols = meta_tokens.shape[1]
    g_meta_mine = lax.dynamic_slice(small[0][8:8 + N_META], (0, (4 * x_i + 2 * y_i + c_i) * mcols), (N_META, mcols))
    big[0]["meta_tokens"] = g_meta_mine
    for kind, arr in enumerate(_adamw(meta_tokens, g_meta_mine, m_meta_tokens, v_meta_tokens, name="adamw_meta")):
        big[kind + 1]["meta_tokens"] = arr
    small = [_unpack_small(p) for p in small]

    names = ["meta_tokens", "rel_bias", "ln_pre_mix", "ln_post_mix", "ln_pre_ffn", "ln_post_ffn", "w_in",
             "b_forget", "sinks", "w_out", "w_gate_up", "w_down"]
    outs = [loss, grad_x]
    for kind in range(4):
        for nme in names:
            outs.append(big[kind][nme] if nme in big[kind] else small[kind][nme])
    return tuple(outs)
```

```python
import math

import numpy as np
import jax
import jax.numpy as jnp
from jax import lax
from jax.experimental import pallas as pl
from jax.experimental.pallas import tpu as pltpu

F32 = jnp.float32
BF16 = jnp.bfloat16
HIGHEST = lax.Precision.HIGHEST
MESH = pl.DeviceIdType.MESH

N_DEV = 8
D_MODEL = 1024
N_META = 16
HEAD_DIM = 64
SWA_Q_HEADS = 8
SWA_KV_HEADS = 2
SWA_GROUP = 4
FOX_HEADS = 8
FOX_W = FOX_HEADS * HEAD_DIM
SWA_Q_W = SWA_Q_HEADS * HEAD_DIM
BLOCK = 128
PAD_ROWS = BLOCK - N_META
N_BUCKETS = 32
MAX_DISTANCE = 128
D_FF = 2816
D_QKV = 2304
D_PROJ = D_QKV + FOX_HEADS
D_PROJ_PAD = 2560
EPS = 1e-6
NEG = -1e30
SCALE = HEAD_DIM ** -0.5
ADAM_LR, ADAM_B1, ADAM_B2, ADAM_EPS, ADAM_WD, ADAM_STEP = 0.001, 0.9, 0.999, 1e-08, 0.01, 10
VMEM_LIMIT = 56 * 1024 * 1024
FOX_TILE = 384
FOX_GROUP = 4
W_IN_PAD = 384

NT = (((1,), (1,)), ((), ()))
NN = (((1,), (0,)), ((), ()))
TN = (((0,), (0,)), ((), ()))


def _params(sem=None, **kw):
    if sem is not None:
        kw["dimension_semantics"] = sem
    return pltpu.CompilerParams(vmem_limit_bytes=VMEM_LIMIT, **kw)


def _tile(n, target, mult=16):
    best = None
    for t in range(mult, min(n, target) + 1, mult):
        if n % t == 0:
            best = t
    assert best is not None, (n, target)
    return best


def _matmul(a, b, *, nt=False, ta=False, out_dtype, tm, tn, tk=None, ex=None, name):
    M, K = a.shape[::-1] if ta else a.shape
    assert not (ta and nt)
    N = b.shape[0] if nt else b.shape[1]
    tk = K if tk is None else tk
    assert M % tm == 0 and N % tn == 0 and K % tk == 0, (name, a.shape, b.shape, tm, tn, tk)
    nk = K // tk
    dn = NT if nt else (TN if ta else NN)
    a_spec = pl.BlockSpec((tk, tm), lambda i, j, k: (k, i)) if ta else pl.BlockSpec((tm, tk), lambda i, j, k: (i, k))

    def body(a_ref, b_ref, o_ref, *scr):
        part = lax.dot_general(a_ref[...], b_ref[...], dn, preferred_element_type=F32)
        if nk == 1:
            o_ref[...] = part.astype(o_ref.dtype)
        else:
            acc = scr[0]
            k = pl.program_id(2)

            @pl.when(k == 0)
            def _():
                acc[...] = part

            @pl.when(k > 0)
            def _():
                acc[...] += part

            @pl.when(k == nk - 1)
            def _():
                o_ref[...] = acc[...].astype(o_ref.dtype)

    if nt:
        b_spec = pl.BlockSpec((tn, tk), lambda i, j, k: (j, k))
    else:
        b_spec = pl.BlockSpec((tk, tn), lambda i, j, k: (k, j))
    out_shape = jax.ShapeDtypeStruct((M, N), out_dtype)
    out_spec = pl.BlockSpec((tm, tn), lambda i, j, k: (i, j))
    grid = (M // tm, N // tn, nk)
    body, x_in, x_in_specs, x_out, x_out_specs, x_scr = _carry(ex, grid, 2, 1, body)
    res = pl.pallas_call(
        body,
        out_shape=(out_shape, *x_out),
        grid=grid,
        in_specs=[a_spec, b_spec] + x_in_specs,
        out_specs=(out_spec, *x_out_specs),
        scratch_shapes=([pltpu.VMEM((tm, tn), F32)] if nk > 1 else []) + x_scr,
        compiler_params=_params(("parallel", "parallel", "arbitrary") if ex is None else ("arbitrary",) * 3),
        name=name,
    )(a, b, *x_in)
    return res[0] if ex is None else res


def _rstd(x):
    return lax.rsqrt(jnp.mean(x * x, axis=-1, keepdims=True) + EPS)


def _pad_rows_rms(x, target, g, ex, *, name):
    S, D = x.shape
    nb = S // BLOCK + 1
    ni, no = len(ex.inputs), len(ex.out_shapes)
    mcols = D // N_DEV

    def body(x_ref, t_ref, g_ref, *rest):
        side_in, (h_ref, to_ref, y_ref, yt_ref) = rest[:ni], rest[ni:ni + 4]
        side_out = rest[ni + 4:ni + 4 + no]
        meta_buf, meta_sems, *sems = rest[ni + 4 + no:]
        i = pl.program_id(0)

        @pl.when(i == 0)
        def _():
            ex.start(side_in, side_out, sems)

        def norm():
            h = h_ref[...]
            y = h * _rstd(h) * g_ref[...]
            y_ref[...] = y.astype(y_ref.dtype)
            yt_ref[...] = y.T.astype(yt_ref.dtype)

        @pl.when(i < nb - 1)
        def _():
            h_ref[...] = x_ref[...]
            to_ref[...] = t_ref[...]
            norm()

        @pl.when(i == nb - 1)
        def _():
            ex.finish(side_in, side_out, sems)
            copies = [pltpu.make_async_copy(side_out[-1].at[d], meta_buf.at[:, d * mcols:(d + 1) * mcols],
                                            meta_sems.at[d]) for d in range(N_DEV)]
            for cp in copies:
                cp.start()
            for cp in copies:
                cp.wait()
            h_ref[:PAD_ROWS, :] = jnp.zeros((PAD_ROWS, D), F32)
            h_ref[PAD_ROWS:, :] = meta_buf[...]
            to_ref[...] = jnp.zeros_like(to_ref)
            norm()

    src = pl.BlockSpec((BLOCK, D), lambda i: (jnp.minimum(i, nb - 2), 0))
    dst = pl.BlockSpec((BLOCK, D), lambda i: ((i + 1) % nb, 0))
    hbm = pl.BlockSpec(memory_space=pl.ANY)
    rows = jax.ShapeDtypeStruct((BLOCK + S, D), F32)
    return pl.pallas_call(
        body,
        out_shape=(rows, rows, jax.ShapeDtypeStruct((BLOCK + S, D), BF16), jax.ShapeDtypeStruct((D, BLOCK + S), BF16),
                   *ex.out_shapes),
        grid=(nb,),
        in_specs=[src, src, pl.BlockSpec((1, D), lambda i: (0, 0))] + [hbm] * ni,
        out_specs=(dst, dst, dst, pl.BlockSpec((D, BLOCK), lambda i: (0, (i + 1) % nb)), *([hbm] * no)),
        scratch_shapes=[pltpu.VMEM((N_META, D), F32), pltpu.SemaphoreType.DMA((N_DEV,))] + list(ex.scratch),
        compiler_params=_params(("arbitrary",)), name=name)(x, target, g, *ex.inputs)


def _post_res_norm(a, g_post, h, g_pre, *, name):
    T, D = a.shape
    tm = _tile(T, 384, BLOCK)

    def body(a_ref, gp_ref, h_ref, gn_ref, h1_ref, o_ref):
        a = a_ref[...]
        h1 = h_ref[...] + a * _rstd(a) * gp_ref[...]
        h1_ref[...] = h1
        o_ref[...] = (h1 * _rstd(h1) * gn_ref[...]).astype(o_ref.dtype)

    row = pl.BlockSpec((tm, D), lambda i: (i, 0))
    vec = pl.BlockSpec((1, D), lambda i: (0, 0))
    return pl.pallas_call(
        body, out_shape=(jax.ShapeDtypeStruct((T, D), F32), jax.ShapeDtypeStruct((T, D), BF16)), grid=(T // tm,),
        in_specs=[row, vec, row, vec], out_specs=(row, row),
        compiler_params=_params(("parallel",)), name=name)(a, g_post, h, g_pre)


def _loss_head(a, g, h, target, *, name):
    T, D = a.shape
    tm = _tile(T, 512)

    def body(a_ref, g_ref, h_ref, t_ref, dy_ref, da_ref, dg_ref, loss_ref):
        i = pl.program_id(0)
        a = a_ref[...]
        r = _rstd(a)
        ah = a * r
        y = h_ref[...] + ah * g_ref[...]
        rows = i * tm + lax.broadcasted_iota(jnp.int32, (tm, 1), 0)
        err = jnp.where(rows >= BLOCK, y - t_ref[...], 0.0)
        dy = err / D
        dy_ref[...] = dy
        dah = dy * g_ref[...]
        da_ref[...] = (r * (dah - ah * jnp.mean(dah * ah, axis=-1, keepdims=True))).astype(da_ref.dtype)
        part = jnp.sum(jnp.sum(err * err, axis=1, keepdims=True), axis=0, keepdims=True)

        @pl.when(i == 0)
        def _():
            loss_ref[...] = jnp.zeros_like(loss_ref)
            dg_ref[...] = jnp.zeros_like(dg_ref)

        loss_ref[...] += jnp.broadcast_to(part, loss_ref.shape)
        dg_ref[...] += jnp.sum(dy * ah, axis=0, keepdims=True)

    row = pl.BlockSpec((tm, D), lambda i: (i, 0))
    vec = pl.BlockSpec((1, D), lambda i: (0, 0))
    return pl.pallas_call(
        body, out_shape=(jax.ShapeDtypeStruct((T, D), F32), jax.ShapeDtypeStruct((T, D), BF16),
                         jax.ShapeDtypeStruct((1, D), F32), jax.ShapeDtypeStruct((8, 128), F32)),
        grid=(T // tm,),
        in_specs=[row, vec, row, row],
        out_specs=(row, row, vec, pl.BlockSpec((8, 128), lambda i: (0, 0))),
        compiler_params=_params(("arbitrary",)), name=name)(a, g, h, target)


def _rms_pull_back(x, g, dy):
    r = _rstd(x)
    xh = x * r
    dxh = dy * g
    return r * (dxh - xh * jnp.mean(dxh * xh, axis=-1, keepdims=True)), jnp.sum(dy * xh, axis=0, keepdims=True)


def _rms_bwd_twice(x, g, dy, res, x2, g2, *, name):
    T, D = x.shape
    tm = _tile(T, 512)

    def body(x_ref, g_ref, dy_ref, res_ref, x2_ref, g2_ref, dx_ref, dg_ref, dx2_ref, dg2_ref):
        @pl.when(pl.program_id(0) == 0)
        def _():
            dg_ref[...] = jnp.zeros_like(dg_ref)
            dg2_ref[...] = jnp.zeros_like(dg2_ref)

        dx, dg = _rms_pull_back(x_ref[...], g_ref[...], dy_ref[...].astype(F32))
        dx = dx + res_ref[...]
        dx_ref[...] = dx
        dg_ref[...] += dg
        dx2, dg2 = _rms_pull_back(x2_ref[...], g2_ref[...], dx)
        dx2_ref[...] = dx2.astype(dx2_ref.dtype)
        dg2_ref[...] += dg2

    row = pl.BlockSpec((tm, D), lambda i: (i, 0))
    vec = pl.BlockSpec((1, D), lambda i: (0, 0))
    gain = jax.ShapeDtypeStruct((1, D), F32)
    return pl.pallas_call(
        body, out_shape=(jax.ShapeDtypeStruct((T, D), F32), gain, jax.ShapeDtypeStruct((T, D), BF16), gain),
        grid=(T // tm,), in_specs=[row, vec, row, row, row, vec], out_specs=(row, vec, row, vec),
        compiler_params=_params(("arbitrary",)), name=name)(x, g, dy, res, x2, g2)


def _rms_bwd_rows(x, g, dy, a, b, res, *, name):
    T, D = x.shape
    n = a.shape[1]
    n_tail = T // BLOCK - 1
    per_step = max(p for p in (4, 3, 2, 1) if n_tail % p == 0)
    steps = n_tail // per_step
    assert T == BLOCK * (1 + n_tail)
    n_rows = 4 * (per_step + 1)

    def body(*refs):
        rows, (g_ref, b_ref), (tail_ref, dg_ref, head_ref) = refs[:n_rows], refs[n_rows:n_rows + 2], refs[n_rows + 2:]

        def block(s):
            x_ref, dy_ref, a_ref, res_ref = rows[4 * s:4 * s + 4]
            dy_all = dy_ref[...] + lax.dot_general(a_ref[...], b_ref[...], NT, preferred_element_type=F32)
            dx, dg = _rms_pull_back(x_ref[...], g_ref[...], dy_all)
            return dx + res_ref[...], dg

        @pl.when(pl.program_id(0) == 0)
        def _():
            dx, dg = block(per_step)
            head_ref[...] = dx
            dg_ref[...] = dg

        for s in range(per_step):
            dx, dg = block(s)
            tail_ref[s * BLOCK:(s + 1) * BLOCK, :] = dx
            dg_ref[...] += dg

    def blocks(width):
        tail = [pl.BlockSpec((BLOCK, width), lambda i, s=s: (per_step * i + s + 1, 0)) for s in range(per_step)]
        return tail + [pl.BlockSpec((BLOCK, width), lambda i: (0, 0))]

    specs, args = [], []
    for bx, bdy, ba, bres in zip(blocks(D), blocks(D), blocks(n), blocks(D)):
        specs += [bx, bdy, ba, bres]
        args += [x, dy, a, res]
    vec = pl.BlockSpec((1, D), lambda i: (0, 0))
    return pl.pallas_call(
        body,
        out_shape=(jax.ShapeDtypeStruct((T - BLOCK, D), F32), jax.ShapeDtypeStruct((1, D), F32),
                   jax.ShapeDtypeStruct((BLOCK, D), F32)),
        grid=(steps,), in_specs=specs + [vec, pl.BlockSpec(b.shape, lambda i: (0, 0))],
        out_specs=(pl.BlockSpec((per_step * BLOCK, D), lambda i: (i, 0)), vec, pl.BlockSpec((BLOCK, D), lambda i: (0, 0))),
        compiler_params=_params(("arbitrary",)), name=name)(*args, g, b)


def _gate_up_swiglu(a, w_t, *, name):
    T, D = a.shape
    F = w_t.shape[0] // 2
    tm = _tile(T, 1408, BLOCK)
    n = _tile(F, 256, BLOCK)
    rows = 3 * BLOCK

    def body(a_ref, wg_ref, wu_ref, g_ref, u_ref, o_ref, ot_ref):
        wg, wu = wg_ref[...], wu_ref[...]
        for r in range(0, tm, rows):
            e = min(r + rows, tm)
            x = a_ref[r:e, :]
            g = lax.dot_general(x, wg, NT, preferred_element_type=F32)
            u = lax.dot_general(x, wu, NT, preferred_element_type=F32)
            g16, u16 = g.astype(BF16), u.astype(BF16)
            g_ref[r:e, :] = g16
            u_ref[r:e, :] = u16
            gr = g16.astype(F32)
            act = gr / (1.0 + jnp.exp(-gr)) * u16.astype(F32)
            o_ref[r:e, :] = act.astype(o_ref.dtype)
            ot_ref[:, r:e] = act.T.astype(ot_ref.dtype)

    tile = pl.BlockSpec((tm, n), lambda i, j: (i, j))
    shp = jax.ShapeDtypeStruct((T, F), BF16)
    return pl.pallas_call(
        body, out_shape=(shp, shp, shp, jax.ShapeDtypeStruct((F, T), BF16)), grid=(T // tm, F // n),
        in_specs=[pl.BlockSpec((tm, D), lambda i, j: (i, 0)),
                  pl.BlockSpec((n, D), lambda i, j: (j, 0)),
                  pl.BlockSpec((n, D), lambda i, j: (j + F // n, 0))],
        out_specs=(tile, tile, tile, pl.BlockSpec((n, tm), lambda i, j: (j, i))),
        compiler_params=_params(("parallel", "parallel")), name=name)(a, w_t, w_t)


def _d_act_swiglu(dff, w_down, gate, up, *, name):
    T, D = dff.shape
    F = w_down.shape[0]
    tm = _tile(T, 384)
    chunk = 768
    assert F % BLOCK == 0

    def body(d_ref, w_ref, g_ref, u_ref, o_ref):
        dy = d_ref[...]
        for c in range(0, F, chunk):
            e = min(c + chunk, F)
            d = lax.dot_general(dy, w_ref[c:e, :], NT, preferred_element_type=F32)
            g = g_ref[:, c:e].astype(F32)
            u = u_ref[:, c:e].astype(F32)
            sg = 1.0 / (1.0 + jnp.exp(-g))
            o_ref[:, c:e] = (d * u * (sg * (1.0 + g * (1.0 - sg)))).astype(o_ref.dtype)
            o_ref[:, F + c:F + e] = (d * (g * sg)).astype(o_ref.dtype)

    row = pl.BlockSpec((tm, F), lambda i: (i, 0))
    return pl.pallas_call(
        body, out_shape=jax.ShapeDtypeStruct((T, 2 * F), BF16), grid=(T // tm,),
        in_specs=[pl.BlockSpec((tm, D), lambda i: (i, 0)), pl.BlockSpec((F, D), lambda i: (0, 0)), row, row],
        out_specs=pl.BlockSpec((tm, 2 * F), lambda i: (i, 0)),
        compiler_params=_params(("parallel",)), name=name)(dff, w_down, gate, up)


def _fox_gates_fwd(f_t, b, *, name):
    H, T = f_t.shape
    nb = T // BLOCK

    def body(f_ref, b_ref, col_ref):
        f = f_ref[...] + b_ref[...]
        ls = jnp.minimum(f, 0.0) - jnp.log(1.0 + jnp.exp(-jnp.abs(f)))
        t = lax.broadcasted_iota(jnp.int32, (H, T), 1)
        ls = jnp.where(t >= PAD_ROWS, ls, 0.0)
        upper = (lax.broadcasted_iota(jnp.int32, (BLOCK, BLOCK), 0)
                 <= lax.broadcasted_iota(jnp.int32, (BLOCK, BLOCK), 1)).astype(F32)
        carry = jnp.zeros((H, 1), F32)
        for blk in range(nb):
            seg = ls[:, blk * BLOCK:(blk + 1) * BLOCK]
            pre = jnp.dot(seg, upper, precision=HIGHEST, preferred_element_type=F32) + carry
            key_gate = jnp.where(t[:, blk * BLOCK:(blk + 1) * BLOCK] >= PAD_ROWS, pre, -NEG)
            terms = list(_split3(pre)) + list(_split3(key_gate))
            col_ref[blk * BLOCK:(blk + 1) * BLOCK, :] = jnp.concatenate(
                terms + [jnp.zeros((BLOCK - len(terms) * H, BLOCK), F32)], axis=0).T.astype(col_ref.dtype)
            carry = pre[:, BLOCK - 1:BLOCK]

    vm = pl.BlockSpec(memory_space=pltpu.VMEM)
    return pl.pallas_call(
        body, out_shape=jax.ShapeDtypeStruct((T, BLOCK), BF16),
        in_specs=[vm, vm], out_specs=vm,
        compiler_params=_params(), name=name)(f_t, b)


def _fox_gates_bwd(dcq, dck, f_t, b, *, name):
    H, T = f_t.shape
    nb = T // BLOCK

    def body(dq_ref, d_ref, f_ref, b_ref, df_ref, db_ref):
        lower = (lax.broadcasted_iota(jnp.int32, (BLOCK, BLOCK), 0)
                 >= lax.broadcasted_iota(jnp.int32, (BLOCK, BLOCK), 1)).astype(F32)
        carry = jnp.zeros((H, 1), F32)
        for blk in range(nb - 1, -1, -1):
            seg = dq_ref[:, blk * BLOCK:(blk + 1) * BLOCK] - d_ref[:, blk * BLOCK:(blk + 1) * BLOCK]
            suf = jnp.dot(seg, lower, precision=HIGHEST, preferred_element_type=F32) + carry
            df_ref[:, blk * BLOCK:(blk + 1) * BLOCK] = suf
            carry = suf[:, 0:1]
        f = f_ref[...] + b_ref[...]
        t = lax.broadcasted_iota(jnp.int32, (H, T), 1)
        df = jnp.where(t >= PAD_ROWS, df_ref[...] / (1.0 + jnp.exp(f)), 0.0)
        df_ref[...] = df
        db_ref[...] = jnp.sum(df, axis=1, keepdims=True)

    vm = pl.BlockSpec(memory_space=pltpu.VMEM)
    return pl.pallas_call(
        body, out_shape=(jax.ShapeDtypeStruct((H, T), F32), jax.ShapeDtypeStruct((H, 1), F32)),
        in_specs=[vm, vm, vm, vm], out_specs=(vm, vm),
        compiler_params=_params(), name=name)(dcq, dck, f_t, b)


def _fox_lanes(parity):
    base = HEAD_DIM * (1 - parity)
    return base, base + 3


def _split3(c):
    hi = c.astype(BF16).astype(F32)
    r = c - hi
    mid = r.astype(BF16).astype(F32)
    lo = (r - mid).astype(BF16).astype(F32)
    return hi, mid, lo


def _lanes(lane, parity, data, start, terms, ones_at=None, fill=1.0):
    out = jnp.zeros((), F32) if ones_at is None else jnp.where((lane >= ones_at) & (lane < ones_at + 3), fill, 0.0)
    for i, t in enumerate(terms):
        out = jnp.where(lane == start + i, t, out)
    return jnp.where(lane // HEAD_DIM == parity, data, out)


def _fox_prep(proj, cum_col, *, name):
    T = proj.shape[0]
    tm = _tile(T, 1408, BLOCK)
    nt = T // tm
    H = FOX_HEADS
    lanes = 2 * HEAD_DIM
    first = (proj.shape[1] - 3 * H * HEAD_DIM) // lanes

    def body(q_ref, k_ref, v_ref, c_ref, qa_ref, ka_ref, va_ref):
        p = pl.program_id(0)
        i = pl.program_id(1)
        lane = lax.broadcasted_iota(jnp.int32, (1, lanes), 1)
        src = lax.broadcasted_iota(jnp.int32, (lanes, lanes), 0)
        dst = lax.broadcasted_iota(jnp.int32, (lanes, lanes), 1)
        q2 = q_ref[...].astype(F32) * SCALE
        k2 = k_ref[...].astype(F32)
        v2 = v_ref[...].astype(F32)
        gates = c_ref[...]
        def placed(h, first_term, start):
            pick = ((src % FOX_HEADS == h) & (src // FOX_HEADS - first_term == dst - start)
                    & (dst >= start) & (dst < start + 3))
            return jnp.dot(gates, pick.astype(BF16), preferred_element_type=F32)

        moved = [(placed(2 * p + e, 0, _fox_lanes(e)[1]), placed(2 * p + e, 3, _fox_lanes(e)[0])) for e in range(2)]
        for e in range(2):
            kc, qc = _fox_lanes(e)
            own = lane // HEAD_DIM == e
            minus = jnp.where((lane >= kc) & (lane < kc + 3), -1.0, 0.0)
            ones_q = jnp.where((lane >= qc) & (lane < qc + 3), 1.0, 0.0)
            ones_k = jnp.where((lane >= kc) & (lane < kc + 3), 1.0, 0.0)
            qa_ref[e] = jnp.where(own, q2, moved[e][0] + minus).astype(BF16)
            ka_ref[e] = jnp.where(own, k2, moved[e][1] + ones_q).astype(BF16)
            va_ref[e] = jnp.where(own, v2, ones_k).astype(BF16)

    pairs = FOX_GROUP // 2

    def col(part):
        return pl.BlockSpec((tm, lanes),
                            lambda p, i: (i, first + 3 * pairs * (p // pairs) + part * pairs + p % pairs))

    out = pl.BlockSpec((2, tm, lanes), lambda p, i: (p, i, 0))
    shp = jax.ShapeDtypeStruct((H, T, lanes), BF16)
    return pl.pallas_call(
        body, out_shape=(shp, shp, shp), grid=(H // 2, nt),
        in_specs=[col(0), col(1), col(2), pl.BlockSpec((tm, lanes), lambda p, i: (i, 0))],
        out_specs=(out, out, out),
        compiler_params=_params(("parallel", "parallel")), name=name)(proj, proj, proj, cum_col)


def _fox_fwd(q_aug, k_aug, v_aug, mix, *, ex=None, name):
    H, T, lanes = q_aug.shape
    tq = FOX_TILE
    nq = T // tq
    G = FOX_HEADS

    def body(q_ref, k_ref, v_ref, mix_ref, o_ref, lse_ref, m_scr, acc_scr):
        i = pl.program_id(1)
        m_scr[...] = jnp.full(m_scr.shape, NEG, F32)
        acc_scr[...] = jnp.zeros(acc_scr.shape, F32)

        def step(kb, diag):
            off = pl.multiple_of(kb * tq, tq)
            s_t = [lax.dot_general(k_ref[g, pl.ds(off, tq), :], q_ref[g], NT, preferred_element_type=F32)
                   for g in range(G)]
            if diag:
                r = lax.broadcasted_iota(jnp.int32, (tq, tq), 0)
                c = lax.broadcasted_iota(jnp.int32, (tq, tq), 1)
                s_t = [jnp.where(c >= r, s, NEG) for s in s_t]
            m_prev = [m_scr[g] for g in range(G)]
            m_new = [jnp.maximum(m_prev[g], jnp.max(s_t[g], axis=0, keepdims=True)) for g in range(G)]
            p_t = [jnp.exp(s_t[g] - m_new[g]).astype(BF16) for g in range(G)]
            pv = [lax.dot_general(v_ref[g, pl.ds(off, tq), :], p_t[g], TN, preferred_element_type=F32)
                  for g in range(G)]
            for g in range(G):
                acc_scr[g] = jnp.exp(m_prev[g] - m_new[g]) * acc_scr[g] + pv[g]
                m_scr[g] = m_new[g]

        def loop_body(kb, carry):
            step(kb, False)
            return carry

        lax.fori_loop(0, i, loop_body, 0)
        step(i, True)
        lane = lax.broadcasted_iota(jnp.int32, (tq, lanes), 1)
        outs = []
        for g in range(G):
            ones = _fox_lanes(g % 2)[0]
            acc = acc_scr[g]
            lse_ref[g] = m_scr[g] + jnp.log(acc[ones:ones + 1, :])
            acc_t = acc.T
            outs.append(acc_t / acc_t[:, ones:ones + 1])
        for pair in range(G // 2):
            o_ref[:, pair * lanes:(pair + 1) * lanes] = jnp.where(
                lane < HEAD_DIM, outs[2 * pair], outs[2 * pair + 1]).astype(o_ref.dtype)

    blk = pl.BlockSpec((G, tq, lanes), lambda h, i: (h, i, 0))
    full = pl.BlockSpec((G, T, lanes), lambda h, i: (h, 0, 0))
    grid = (H // G, nq)
    first = mix.shape[1] // (G * HEAD_DIM) - H // G
    body, x_in, x_in_specs, x_out, x_out_specs, x_scr = _carry(ex, grid, 4, 2, body)
    return pl.pallas_call(
        body,
        out_shape=(jax.ShapeDtypeStruct(mix.shape, mix.dtype), jax.ShapeDtypeStruct((H, nq, 1, tq), F32), *x_out),
        grid=grid,
        in_specs=[blk, full, full, pl.BlockSpec(memory_space=pl.ANY)] + x_in_specs,
        out_specs=(pl.BlockSpec((tq, G * HEAD_DIM), lambda h, i: (i, first + h)),
                   pl.BlockSpec((G, None, 1, tq), lambda h, i: (h, i, 0, 0)), *x_out_specs),
        input_output_aliases={3: 0},
        scratch_shapes=[pltpu.VMEM((G, 1, tq), F32), pltpu.VMEM((G, lanes, tq), F32)] + x_scr,
        compiler_params=_params(("arbitrary", "arbitrary")), name=name)(q_aug, k_aug, v_aug, mix, *x_in)


def _fox_prep_bwd(dmix, mix, *, name):
    T = dmix.shape[0]
    H = FOX_HEADS
    tm = _tile(T, 1408, BLOCK)
    lanes = 2 * HEAD_DIM
    first = mix.shape[1] // lanes - H // 2

    def body(d_ref, o_ref, da_ref):
        lane = lax.broadcasted_iota(jnp.int32, (1, lanes), 1)
        d2 = d_ref[...].astype(F32)
        prod = d2 * o_ref[...].astype(F32)
        for e in range(2):
            delta = jnp.sum(jnp.where(lane // HEAD_DIM == e, prod, 0.0), axis=1, keepdims=True)
            da_ref[e] = _lanes(lane, e, d2, _fox_lanes(e)[0], _split3(-delta)).astype(BF16)

    pair = pl.BlockSpec((tm, lanes), lambda p, i: (i, first + p))
    return pl.pallas_call(
        body, out_shape=jax.ShapeDtypeStruct((H, T, lanes), BF16), grid=(H // 2, T // tm),
        in_specs=[pair, pair],
        out_specs=pl.BlockSpec((2, tm, lanes), lambda p, i: (p, i, 0)),
        compiler_params=_params(("parallel", "parallel")), name=name)(dmix, mix)


def _fox_bwd(q_aug, k_aug, v_aug, do_aug, lse_row, dproj, *, ex=None, name):
    H, T, lanes = q_aug.shape
    tq = FOX_TILE
    nq = T // tq
    G = FOX_GROUP

    def side_by_side(tiles, scale=None):
        lane = lax.broadcasted_iota(jnp.int32, tiles[0].shape, 1)
        out = [jnp.where(lane < HEAD_DIM, tiles[2 * p], tiles[2 * p + 1]) for p in range(G // 2)]
        out = jnp.concatenate(out, axis=1)
        return out if scale is None else out * scale

    def body(q_ref, k_ref, v_ref, do_ref, lse_ref, dproj_in, out_ref, dcq_ref, dck_ref, dk_acc, dv_acc, dq_ref):
        j = pl.program_id(1)

        @pl.when(j == 0)
        def _():
            dq_ref[...] = jnp.zeros(dq_ref.shape, F32)
            dcq_ref[...] = jnp.zeros(dcq_ref.shape, F32)

        dk_acc[...] = jnp.zeros(dk_acc.shape, F32)
        dv_acc[...] = jnp.zeros(dv_acc.shape, F32)

        def step(qb, diag):
            off = pl.multiple_of(qb * tq, tq)
            heads = range(G)
            qa = [q_ref[g, pl.ds(off, tq), :] for g in heads]
            da = [do_ref[g, pl.ds(off, tq), :] for g in heads]
            s_t = [lax.dot_general(k_ref[g], qa[g], NT, preferred_element_type=F32) for g in heads]
            dp_t = [lax.dot_general(v_ref[g], da[g], NT, preferred_element_type=F32) for g in heads]
            p_t = [jnp.exp(s_t[g] - lse_ref[g, qb]) for g in heads]
            if diag:
                r = lax.broadcasted_iota(jnp.int32, (tq, tq), 0)
                c = lax.broadcasted_iota(jnp.int32, (tq, tq), 1)
                p_t = [jnp.where(c >= r, p, 0.0) for p in p_t]
            dsb = [(p_t[g] * dp_t[g]).astype(BF16) for g in heads]
            dv = [jnp.dot(p_t[g].astype(BF16), da[g], preferred_element_type=F32) for g in heads]
            dk = [jnp.dot(dsb[g], qa[g], preferred_element_type=F32) for g in heads]
            dq = [lax.dot_general(k_ref[g], dsb[g], TN, preferred_element_type=F32) for g in heads]
            for g in heads:
                dv_acc[g] += dv[g]
                dk_acc[g] += dk[g]
                dq_ref[g, qb] += dq[g]
                dcq_ref[g, qb] += jnp.sum(dsb[g].astype(F32), axis=0, keepdims=True)

        step(j, True)

        def loop_body(qb, carry):
            step(qb, False)
            return carry

        lax.fori_loop(j + 1, nq, loop_body, 0)
        dk = [dk_acc[g] for g in range(G)]
        out_ref[:, 0:wide] = side_by_side([dq_ref[g, j].T for g in range(G)], SCALE).astype(out_ref.dtype)
        out_ref[:, wide:2 * wide] = side_by_side(dk).astype(out_ref.dtype)
        out_ref[:, 2 * wide:3 * wide] = side_by_side([dv_acc[g] for g in range(G)]).astype(out_ref.dtype)
        for g in range(G):
            kc = _fox_lanes(g % 2)[0]
            dck_ref[g] = -dk[g].T[kc:kc + 1, :]

    blk = pl.BlockSpec((G, tq, lanes), lambda h, j: (h, j, 0))
    full = pl.BlockSpec((G, T, lanes), lambda h, j: (h, 0, 0))
    wide = G * HEAD_DIM
    first = dproj.shape[1] // (3 * wide) - H // G
    grid = (H // G, nq)
    body, x_in, x_in_specs, x_out, x_out_specs, x_scr = _carry(ex, grid, 6, 3, body)
    rows = jax.ShapeDtypeStruct((H, nq, 1, tq), F32)
    all_rows = pl.BlockSpec((G, nq, 1, tq), lambda h, j: (h, 0, 0, 0))
    return pl.pallas_call(
        body,
        out_shape=(jax.ShapeDtypeStruct(dproj.shape, dproj.dtype), rows, rows, *x_out),
        grid=grid,
        in_specs=[full, blk, blk, full, all_rows, pl.BlockSpec(memory_space=pl.ANY)] + x_in_specs,
        out_specs=(pl.BlockSpec((tq, 3 * wide), lambda h, j: (j, first + h)), all_rows,
                   pl.BlockSpec((G, None, 1, tq), lambda h, j: (h, j, 0, 0)), *x_out_specs),
        input_output_aliases={5: 0},
        scratch_shapes=[pltpu.VMEM((G, tq, lanes), F32), pltpu.VMEM((G, tq, lanes), F32),
                        pltpu.VMEM((G, nq, lanes, tq), F32)] + x_scr,
        compiler_params=_params(("arbitrary", "arbitrary")), name=name,
    )(q_aug, k_aug, v_aug, do_aug, lse_row, dproj, *x_in)


def _t5_bucket_np(d):
    n = np.maximum(d, 0).astype(np.int32)
    max_exact = N_BUCKETS // 2
    nf = np.maximum(n, 1).astype(np.float32)
    large = max_exact + (np.log(nf / max_exact) / math.log(MAX_DISTANCE / max_exact)
                         * (N_BUCKETS - max_exact)).astype(np.int32)
    large = np.minimum(large, N_BUCKETS - 1)
    return np.where(n < max_exact, n, large)


def _bucket_onehots():
    k = np.arange(BLOCK)[:, None]
    q = np.arange(BLOCK)[None, :]
    eye = np.eye(N_BUCKETS, dtype=np.float32)
    cur = eye[_t5_bucket_np(q - k).reshape(-1)]
    prev = eye[_t5_bucket_np(BLOCK + q - k).reshape(-1)]
    return cur, prev


SWA_K_COL = SWA_Q_HEADS * HEAD_DIM // (2 * HEAD_DIM)
SWA_V_COL = SWA_K_COL + 1


def _swa_terms(raw, bc, bp, far, sink, n):
    k = lax.broadcasted_iota(jnp.int32, (BLOCK, BLOCK), 0)
    q = lax.broadcasted_iota(jnp.int32, (BLOCK, BLOCK), 1)
    never = 2 * BLOCK
    s_c = raw[0] + bc
    s_p = raw[1] + bp
    s_m = raw[2] + jnp.where(n == 1, bp, far)
    s_c = jnp.where((k <= q) & (k >= jnp.where(n >= 1, 0, PAD_ROWS)), s_c, NEG)
    s_p = jnp.where(k > q + jnp.where(n >= 2, 0, never), s_p, NEG)
    s_m = jnp.where(k >= jnp.where(n >= 1, PAD_ROWS, never), s_m, NEG)
    m = jnp.maximum(jnp.maximum(jnp.max(s_c, axis=0, keepdims=True), jnp.max(s_p, axis=0, keepdims=True)),
                    jnp.maximum(jnp.max(s_m, axis=0, keepdims=True), sink))
    e = [jnp.exp(s_c - m), jnp.exp(s_p - m), jnp.exp(s_m - m)]
    e_s = jnp.exp(sink - m)
    l = (jnp.sum(e[0], axis=0, keepdims=True) + jnp.sum(e[1], axis=0, keepdims=True)
         + jnp.sum(e[2], axis=0, keepdims=True) + e_s)
    return e, e_s, l


SWA_STEP = 3


def _swa_specs():
    R = SWA_STEP

    def window(col):
        return ([pl.BlockSpec((BLOCK, BLOCK), lambda s, w=w: (jnp.maximum(R * s - 1 + w, 0), col)) for w in range(R + 1)]
                + [pl.BlockSpec((BLOCK, BLOCK), lambda s: (0, col))])

    qblk = pl.BlockSpec((R * BLOCK, SWA_Q_HEADS * HEAD_DIM), lambda s: (s, 0))
    bias = pl.BlockSpec((SWA_Q_HEADS, BLOCK, BLOCK), lambda s: (0, 0, 0))
    smem = pl.BlockSpec(memory_space=pltpu.SMEM)
    return qblk, window(SWA_K_COL), window(SWA_V_COL), bias, smem


def _swa_own_kv(tile_ref, kv):
    lane = lax.broadcasted_iota(jnp.int32, (BLOCK, 2 * HEAD_DIM), 1)
    t = tile_ref[...].astype(F32)
    return jnp.where(lane // HEAD_DIM == kv, t, pltpu.roll(t, HEAD_DIM, 1)).astype(BF16)


def _swa_fwd(proj, bc, bp, far, sinks, *, name):
    T = proj.shape[0]
    nb = T // BLOCK
    G = SWA_GROUP
    Hq = SWA_Q_HEADS
    lanes = 2 * HEAD_DIM

    R = SWA_STEP
    assert nb % R == 0

    def body(*refs):
        q_ref, k_refs, v_refs = refs[0], refs[1:R + 3], refs[R + 3:2 * R + 5]
        bc_ref, bp_ref, far_ref, sink_ref, o_ref = refs[2 * R + 5:]
        s = pl.program_id(0)
        lane = lax.broadcasted_iota(jnp.int32, (BLOCK, lanes), 1)
        kvs = range(SWA_KV_HEADS)
        kk = [[_swa_own_kv(ref, kv) for ref in k_refs] for kv in kvs]
        vv = [[_swa_own_kv(ref, kv) for ref in v_refs] for kv in kvs]
        chains = [(r, h) for r in range(R) for h in range(Hq)]
        tiles = lambda r: (r + 1, r, R + 1)
        q2 = {(r, pair): q_ref[r * BLOCK:(r + 1) * BLOCK, pair * lanes:(pair + 1) * lanes].astype(F32) * SCALE
              for r in range(R) for pair in range(Hq // 2)}
        qm = {c: jnp.where(lane // HEAD_DIM == c[1] % 2, q2[c[0], c[1] // 2], 0.0).astype(BF16) for c in chains}
        raw = {c: [lax.dot_general(kk[c[1] // G][w], qm[c], NT, preferred_element_type=F32) for w in tiles(c[0])]
               for c in chains}
        terms = {c: _swa_terms(raw[c], bc_ref[c[1]], bp_ref[c[1]], far_ref[c[1]], sink_ref[c[1]], R * s + c[0])
                 for c in chains}
        o_t = {c: sum(lax.dot_general(vv[c[1] // G][w], terms[c][0][b].astype(BF16), TN, preferred_element_type=F32)
                      for b, w in enumerate(tiles(c[0]))) for c in chains}
        outs = {c: (o_t[c] / terms[c][2]).T for c in chains}
        for r in range(R):
            for pair in range(Hq // 2):
                o_ref[r * BLOCK:(r + 1) * BLOCK, pair * lanes:(pair + 1) * lanes] = jnp.where(
                    lane < HEAD_DIM, outs[r, 2 * pair], outs[r, 2 * pair + 1]).astype(o_ref.dtype)

    qblk, keys, vals, bias, smem = _swa_specs()
    return pl.pallas_call(
        body, out_shape=jax.ShapeDtypeStruct((T, D_MODEL), BF16), grid=(nb // R,),
        in_specs=[qblk] + keys + vals + [bias, bias, smem, smem],
        out_specs=qblk,
        compiler_params=_params(("parallel",)), name=name,
    )(proj, *([proj] * (2 * R + 4)), bc, bp, far, sinks)


def _swa_bwd(proj, dmix, bc, bp, far, sinks, *, ex=None, name):
    T, width = proj.shape
    nb = T // BLOCK
    G = SWA_GROUP
    Hq = SWA_Q_HEADS
    lanes = 2 * HEAD_DIM
    qw = Hq * HEAD_DIM
    own_w = qw + 2 * lanes

    R = SWA_STEP
    assert nb % R == 0
    n_in = 2 * R + 10

    def body(*refs):
        q_ref, k_refs, v_refs = refs[0], refs[1:R + 3], refs[R + 3:2 * R + 5]
        do_ref, bc_ref, bp_ref, far_ref, sink_ref = refs[2 * R + 5:n_in]
        dp_ref, dbc_ref, dbp_ref, dbf_ref, dsk_ref, dk_acc, dv_acc = refs[n_in:]
        s = pl.program_id(0)

        @pl.when(s == 0)
        def _():
            for ref in (dk_acc, dv_acc, dbc_ref, dbp_ref, dbf_ref, dsk_ref):
                ref[...] = jnp.zeros(ref.shape, F32)

        lane = lax.broadcasted_iota(jnp.int32, (BLOCK, lanes), 1)
        kvs = range(SWA_KV_HEADS)
        kk = [[_swa_own_kv(ref, kv) for ref in k_refs] for kv in kvs]
        vv = [[_swa_own_kv(ref, kv) for ref in v_refs] for kv in kvs]
        chains = [(r, h) for r in range(R) for h in range(Hq)]
        blocks = range(3)
        tiles = lambda r: (r + 1, r, R + 1)
        sub = lambda ref, r, pair: ref[r * BLOCK:(r + 1) * BLOCK, pair * lanes:(pair + 1) * lanes]
        q2 = {(r, pair): sub(q_ref, r, pair).astype(F32) * SCALE for r in range(R) for pair in range(Hq // 2)}
        d2 = {(r, pair): sub(do_ref, r, pair) for r in range(R) for pair in range(Hq // 2)}
        own = [lane // HEAD_DIM == half for half in range(2)]
        qm = {c: jnp.where(own[c[1] % 2], q2[c[0], c[1] // 2], 0.0).astype(BF16) for c in chains}
        dom = {c: jnp.where(own[c[1] % 2], d2[c[0], c[1] // 2], jnp.zeros_like(d2[0, 0])) for c in chains}
        raw = {c: [lax.dot_general(kk[c[1] // G][w], qm[c], NT, preferred_element_type=F32) for w in tiles(c[0])]
               for c in chains}
        dp = {c: [lax.dot_general(vv[c[1] // G][w], dom[c], NT, preferred_element_type=F32) for w in tiles(c[0])]
              for c in chains}
        p, ds16 = {}, {}
        for c in chains:
            r, h = c
            n = R * s + r
            e, e_s, l = _swa_terms(raw[c], bc_ref[h], bp_ref[h], far_ref[h], sink_ref[h], n)
            inv = 1.0 / l
            ph = [e[b] * inv for b in blocks]
            delta = sum(jnp.sum(ph[b] * dp[c][b], axis=0, keepdims=True) for b in blocks)
            ds = [ph[b] * (dp[c][b] - delta) for b in blocks]
            dsk_ref[h] += -(e_s * inv) * delta
            dbc_ref[h] += ds[0]
            dbp_ref[h] += ds[1] + jnp.where(n == 1, ds[2], 0.0)
            dbf_ref[h] += jnp.where(n >= 2, ds[2], 0.0)
            p[c] = [x.astype(BF16) for x in ph]
            ds16[c] = [x.astype(BF16) for x in ds]
        dq_t = {c: sum(lax.dot_general(kk[c[1] // G][w], ds16[c][b], TN, preferred_element_type=F32)
                       for b, w in enumerate(tiles(c[0]))) for c in chains}
        group = [range(kv * G, (kv + 1) * G) for kv in kvs]
        dk = {(r, kv): [sum(jnp.dot(ds16[r, h][b], qm[r, h], preferred_element_type=F32) for h in group[kv])
                        for b in blocks] for r in range(R) for kv in kvs}
        dv = {(r, kv): [sum(jnp.dot(p[r, h][b], dom[r, h], preferred_element_type=F32) for h in group[kv])
                        for b in blocks] for r in range(R) for kv in kvs}
        for r in range(R):
            n = R * s + r
            rows = pl.ds(pl.multiple_of(n * BLOCK, BLOCK), BLOCK)
            prev_rows = pl.ds(pl.multiple_of(jnp.maximum(n - 1, 0) * BLOCK, BLOCK), BLOCK)
            for pair in range(Hq // 2):
                dp_ref[rows, pair * lanes:(pair + 1) * lanes] = (jnp.where(
                    lane < HEAD_DIM, dq_t[r, 2 * pair].T, dq_t[r, 2 * pair + 1].T) * SCALE).astype(dp_ref.dtype)
            for acc, ref in ((dk, dk_acc), (dv, dv_acc)):
                tot = [[a + pltpu.roll(a, HEAD_DIM, 1) for a in acc[r, kv]] for kv in kvs]
                both = [jnp.where(lane < HEAD_DIM, tot[0][b], tot[1][b]) for b in blocks]
                ref[rows, :] += both[0]
                ref[prev_rows, :] += both[1]
                ref[0:BLOCK, :] += both[2]

        @pl.when(s == nb // R - 1)
        def _():
            dp_ref[:, qw:qw + lanes] = dk_acc[...].astype(dp_ref.dtype)
            dp_ref[:, qw + lanes:own_w] = dv_acc[...].astype(dp_ref.dtype)

    qblk, keys, vals, bias, smem = _swa_specs()
    dsk = pl.BlockSpec((Hq, 1, BLOCK), lambda s: (0, 0, 0))
    grid = (nb // R,)
    body, x_in, x_in_specs, x_out, x_out_specs, x_scr = _carry(ex, grid, n_in, 5, body)
    tile = jax.ShapeDtypeStruct((Hq, BLOCK, BLOCK), F32)
    return pl.pallas_call(
        body,
        out_shape=(jax.ShapeDtypeStruct((T, width), BF16), tile, tile, tile,
                   jax.ShapeDtypeStruct((Hq, 1, BLOCK), F32), *x_out),
        grid=grid,
        in_specs=[qblk] + keys + vals + [qblk, bias, bias, smem, smem] + x_in_specs,
        out_specs=(pl.BlockSpec((T, own_w), lambda s: (0, 0)), bias, bias, bias, dsk, *x_out_specs),
        scratch_shapes=[pltpu.VMEM((T, lanes), F32), pltpu.VMEM((T, lanes), F32)] + x_scr,
        compiler_params=_params(("arbitrary",)), name=name,
    )(proj, *([proj] * (2 * R + 4)), dmix, bc, bp, far, sinks, *x_in)


def _bias_tiles(tab_t, oh_cur_t, oh_prev_t, *, name):
    Hq = tab_t.shape[0]

    def body(t_ref, oc_ref, op_ref, bc_ref, bp_ref):
        bc_ref[...] = jnp.dot(t_ref[...], oc_ref[...], precision=HIGHEST, preferred_element_type=F32)
        bp_ref[...] = jnp.dot(t_ref[...], op_ref[...], precision=HIGHEST, preferred_element_type=F32)

    vm = pl.BlockSpec(memory_space=pltpu.VMEM)
    shp = jax.ShapeDtypeStruct((Hq, BLOCK * BLOCK), F32)
    bc, bp = pl.pallas_call(body, out_shape=(shp, shp), in_specs=[vm] * 3, out_specs=(vm, vm),
                            compiler_params=_params(), name=name)(tab_t, oh_cur_t, oh_prev_t)
    return bc.reshape(Hq, BLOCK, BLOCK), bp.reshape(Hq, BLOCK, BLOCK)


def _small_grads(dbc, dbp, dbf, dsk, oh_cur, oh_prev, *, ex=None, name):
    Hq = dbc.shape[0]

    def body(dbc_ref, dbp_ref, dbf_ref, dsk_ref, oc_ref, op_ref, tab_ref, sink_ref):
        tab = (jnp.dot(dbc_ref[...], oc_ref[...], precision=HIGHEST, preferred_element_type=F32)
               + jnp.dot(dbp_ref[...], op_ref[...], precision=HIGHEST, preferred_element_type=F32))
        far = jnp.sum(dbf_ref[...], axis=1, keepdims=True)
        last = lax.broadcasted_iota(jnp.int32, (Hq, N_BUCKETS), 1) == N_BUCKETS - 1
        tab_ref[...] = tab + jnp.where(last, far, 0.0)
        sink_ref[...] = jnp.sum(dsk_ref[...], axis=1, keepdims=True)

    vm = pl.BlockSpec(memory_space=pltpu.VMEM)
    body, x_in, x_in_specs, x_out, x_out_specs, x_scr = _carry(ex, (), 6, 2, body)
    return pl.pallas_call(
        body, out_shape=(jax.ShapeDtypeStruct((Hq, N_BUCKETS), F32), jax.ShapeDtypeStruct((Hq, 1), F32), *x_out),
        in_specs=[vm] * 6 + x_in_specs, out_specs=(vm, vm, *x_out_specs), scratch_shapes=x_scr,
        compiler_params=_params(), name=name,
    )(dbc.reshape(Hq, -1), dbp.reshape(Hq, -1), dbf.reshape(Hq, -1), dsk.reshape(Hq, -1), oh_cur, oh_prev, *x_in)


def _coords():
    return lax.axis_index("x"), lax.axis_index("y"), lax.axis_index("c")


class _Exchange:
    def __init__(self, inputs, out_shapes, scratch, start, finish):
        self.inputs, self.out_shapes, self.scratch, self.start, self.finish = inputs, out_shapes, scratch, start, finish


def _carry(ex, grid, n_in, n_out, body):
    if ex is None:
        return body, [], [], [], [], []
    ni, no = len(ex.inputs), len(ex.out_shapes)

    def at_step(which):
        cond = jnp.bool_(True)
        for axis, n in enumerate(grid):
            cond = cond & (pl.program_id(axis) == (0 if which == "first" else n - 1))
        return cond

    def wrapped(*refs):
        refs = list(refs)
        n_own_scr = len(refs) - (n_in + ni + n_out + no) - len(ex.scratch)
        own_in, side_in = refs[:n_in], refs[n_in:n_in + ni]
        own_out = refs[n_in + ni:n_in + ni + n_out]
        side_out = refs[n_in + ni + n_out:n_in + ni + n_out + no]
        rest = refs[n_in + ni + n_out + no:]
        own_scr, sems = rest[:n_own_scr], rest[n_own_scr:]

        @pl.when(at_step("first"))
        def _():
            ex.start(side_in, side_out, sems)

        body(*own_in, *own_out, *own_scr)

        @pl.when(at_step("last"))
        def _():
            ex.finish(side_in, side_out, sems)

    hbm = pl.BlockSpec(memory_space=pl.ANY)
    return wrapped, list(ex.inputs), [hbm] * ni, list(ex.out_shapes), [hbm] * no, list(ex.scratch)


def _gather_exchange(shards):
    nt = len(shards)

    def copies(ins, outs, sems):
        send_sems, recv_sems, local_sems = sems
        x, y, c = _coords()
        me, sibling = (x, y, c), (x, y, 1 - c)
        chips = [(1 - x, y), (x, 1 - y), (1 - x, 1 - y)]

        def slot(t, dev):
            return outs[t].at[4 * dev[0] + 2 * dev[1] + dev[2]]

        def copy(t, k, block, to, src=None):
            dst = slot(t, block)
            return pltpu.make_async_remote_copy(
                src_ref=dst if src is None else src, dst_ref=dst,
                send_sem=send_sems.at[t, k], recv_sem=recv_sems.at[t, k], device_id=to, device_id_type=MESH)

        mine = [pltpu.make_async_copy(ins[t], slot(t, me), local_sems.at[t]) for t in range(nt)]
        first = []
        for t in range(nt):
            first.append(copy(t, 0, me, sibling, src=ins[t]))
            first += [copy(t, 1 + j, me, (*chip, c), src=ins[t]) for j, chip in enumerate(chips)]
        return copy, mine, first, me, sibling, chips, c

    def start(ins, outs, sems):
        _, mine, first, *_ = copies(ins, outs, sems)
        for cp in mine + first:
            cp.start()

    def finish(ins, outs, sems):
        copy, mine, first, me, sibling, chips, c = copies(ins, outs, sems)
        passed = []
        for j, chip in enumerate(chips):
            for t in range(nt):
                copy(t, 1 + j, (*chip, c), me).wait_recv()
                cp = copy(t, 4 + j, (*chip, c), sibling)
                cp.start()
                passed.append(cp)
        for t in range(nt):
            copy(t, 0, sibling, me).wait_recv()
            for j, chip in enumerate(chips):
                copy(t, 4 + j, (*chip, 1 - c), me).wait_recv()
        for cp in first + passed:
            cp.wait_send()
        for cp in mine:
            cp.wait()

    return _Exchange(
        list(shards), [jax.ShapeDtypeStruct((N_DEV,) + s.shape, s.dtype) for s in shards],
        [pltpu.SemaphoreType.DMA((nt, 7)), pltpu.SemaphoreType.DMA((nt, 7)), pltpu.SemaphoreType.DMA((nt,))],
        start, finish)


def _swap_exchange(arrays, n_slices, copies):
    nt = len(arrays)

    def start(ins, outs, sems):
        for cp in copies(ins, outs, sems):
            cp.start()

    def finish(ins, outs, sems):
        sends = copies(ins, outs, sems)
        for cp in sends:
            cp.wait_recv()
        for cp in sends:
            cp.wait_send()

    return _Exchange(
        list(arrays), [jax.ShapeDtypeStruct((n_slices,) + a.shape[1:], a.dtype) for a in arrays],
        [pltpu.SemaphoreType.DMA((nt, n_slices)), pltpu.SemaphoreType.DMA((nt, n_slices))], start, finish)


def _cores_exchange(gs):
    def copies(ins, outs, sems):
        send_sems, recv_sems = sems
        x, y, c = _coords()
        return [pltpu.make_async_remote_copy(
            src_ref=ins[t].at[2 * j + (1 - c)], dst_ref=outs[t].at[j],
            send_sem=send_sems.at[t, j], recv_sem=recv_sems.at[t, j], device_id=(x, y, 1 - c), device_id_type=MESH)
            for t in range(len(gs)) for j in range(4)]

    return _swap_exchange(gs, 4, copies)


def _chips_exchange(ps):
    def copies(ins, outs, sems):
        send_sems, recv_sems = sems
        x, y, c = _coords()
        peers = [(1 - x, y), (x, 1 - y), (1 - x, 1 - y)]
        return [pltpu.make_async_remote_copy(
            src_ref=ins[t].at[2 * px + py], dst_ref=outs[t].at[k],
            send_sem=send_sems.at[t, k], recv_sem=recv_sems.at[t, k], device_id=(px, py, c), device_id_type=MESH)
            for t in range(len(ps)) for k, (px, py) in enumerate(peers)]

    return _swap_exchange(ps, 3, copies)


def _add_cores(g, r, core, *, name):
    _, A, B = g.shape
    ta = _tile(A, 512, 16)

    def body(core_ref, a_ref, b_ref, o16_ref):
        o16_ref[...] = (a_ref[...] + b_ref[...]).astype(BF16)

    blk = (None, ta, B)
    return pl.pallas_call(
        body, out_shape=jax.ShapeDtypeStruct((4, A, B), BF16),
        grid_spec=pltpu.PrefetchScalarGridSpec(
            num_scalar_prefetch=1, grid=(4, A // ta),
            in_specs=[pl.BlockSpec(blk, lambda j, i, core_ref: (2 * j + core_ref[0], i, 0)),
                      pl.BlockSpec(blk, lambda j, i, core_ref: (j, i, 0))],
            out_specs=pl.BlockSpec(blk, lambda j, i, core_ref: (j, i, 0))),
        compiler_params=_params(("parallel", "parallel")), name=name)(core, g, r)


def _adamw_math(w, g, m, v):
    m = ADAM_B1 * m + (1.0 - ADAM_B1) * g
    v = ADAM_B2 * v + (1.0 - ADAM_B2) * (g * g)
    m_hat = m / (1.0 - ADAM_B1 ** ADAM_STEP)
    v_hat = v / (1.0 - ADAM_B2 ** ADAM_STEP)
    delta = -ADAM_LR * (m_hat / (jnp.sqrt(v_hat) + ADAM_EPS) + ADAM_WD * w)
    return delta, m, v


def _sum_adamw(mine, sib, r, where, w, m, v, *, ta, name):
    Aw, Bw = w.shape
    Bg = mine.shape[2]
    assert Aw % ta == 0 and Bw <= Bg and mine.shape[1] == Aw

    def body(where_ref, p_ref, s_ref, r0, r1, r2, w_ref, m_ref, v_ref, g_out, d_out, m_out, v_out):
        g = (((p_ref[:, :Bw] + s_ref[:, :Bw]) + r0[:, :Bw].astype(F32))
             + r1[:, :Bw].astype(F32)) + r2[:, :Bw].astype(F32)
        delta, m_new, v_new = _adamw_math(w_ref[...], g, m_ref[...], v_ref[...])
        g_out[...] = g
        d_out[...] = delta
        m_out[...] = m_new
        v_out[...] = v_new

    gblk = (None, ta, Bg)
    row = pl.BlockSpec((ta, Bw), lambda i, where_ref: (i, 0))
    rspecs = [pl.BlockSpec(gblk, (lambda i, where_ref, k=k: (k, i, 0))) for k in range(3)]
    shp = jax.ShapeDtypeStruct((Aw, Bw), F32)
    return pl.pallas_call(
        body, out_shape=(shp, shp, shp, shp),
        grid_spec=pltpu.PrefetchScalarGridSpec(
            num_scalar_prefetch=1, grid=(Aw // ta,),
            in_specs=[pl.BlockSpec(gblk, lambda i, where_ref: (2 * where_ref[0] + where_ref[1], i, 0)),
                      pl.BlockSpec(gblk, lambda i, where_ref: (where_ref[0], i, 0))] + rspecs + [row, row, row],
            out_specs=(row, row, row, row)),
        compiler_params=_params(("parallel",)), name=name)(where, mine, sib, r, r, r, w, m, v)


def _adamw(w, g, m, v, *, name):
    def body(w_ref, g_ref, m_ref, v_ref, d_out, m_out, v_out):
        delta, m_new, v_new = _adamw_math(w_ref[...], g_ref[...], m_ref[...], v_ref[...])
        d_out[...] = delta
        m_out[...] = m_new
        v_out[...] = v_new

    vm = pl.BlockSpec(memory_space=pltpu.VMEM)
    shp = jax.ShapeDtypeStruct(w.shape, F32)
    return pl.pallas_call(body, out_shape=(shp, shp, shp), in_specs=[vm] * 4, out_specs=(vm, vm, vm),
                          compiler_params=_params(), name=name)(w, g, m, v)


def _small_allreduce_adamw(s, w, m, v, *, name):
    R, W = s.shape

    def body(s_ref, w_ref, m_ref, v_ref, g_out, d_out, m_out, v_out, gath, send_sems, recv_sems):
        x, y, c = _coords()
        mine = 4 * x + 2 * y + c
        gath[mine] = s_ref[...]
        peers = [((1 - x) if k & 4 else x, (1 - y) if k & 2 else y, (1 - c) if k & 1 else c) for k in range(1, N_DEV)]
        sends = []
        for k in range(1, N_DEV):
            peer = peers[k - 1]
            sends.append(pltpu.make_async_remote_copy(
                src_ref=s_ref, dst_ref=gath.at[mine], send_sem=send_sems.at[k - 1], recv_sem=recv_sems.at[k - 1],
                device_id=peer, device_id_type=MESH))
        for cp in sends:
            cp.start()
        for k in range(1, N_DEV):
            peer = peers[k - 1]
            pltpu.make_async_remote_copy(
                src_ref=s_ref, dst_ref=gath.at[4 * peer[0] + 2 * peer[1] + peer[2]],
                send_sem=send_sems.at[k - 1], recv_sem=recv_sems.at[k - 1],
                device_id=peer, device_id_type=MESH).wait_recv()
        for cp in sends:
            cp.wait_send()
        g = gath[0]
        for d in range(1, N_DEV):
            g = g + gath[d]
        delta, m_new, v_new = _adamw_math(w_ref[...], g, m_ref[...], v_ref[...])
        g_out[...] = g
        d_out[...] = delta
        m_out[...] = m_new
        v_out[...] = v_new

    vm = pl.BlockSpec(memory_space=pltpu.VMEM)
    shp = jax.ShapeDtypeStruct((R, W), F32)
    return pl.pallas_call(
        body, out_shape=(shp, shp, shp, shp), in_specs=[vm] * 4, out_specs=(vm, vm, vm, vm),
        scratch_shapes=[pltpu.VMEM((N_DEV, R, W), F32), pltpu.SemaphoreType.DMA((N_DEV - 1,)),
                        pltpu.SemaphoreType.DMA((N_DEV - 1,))],
        compiler_params=_params(), name=name)(s, w, m, v)


def _pack_small(rel_bias, g1, g2, g3, g4, b_forget, sinks, extra=None, meta=None):
    misc = jnp.concatenate([rel_bias.reshape(-1), b_forget.reshape(-1), sinks.reshape(-1)])
    misc = jnp.concatenate([misc, jnp.zeros((D_MODEL - misc.shape[0],), F32)])[None]
    last = jnp.zeros((1, D_MODEL), F32) if extra is None else extra
    meta = jnp.zeros((N_META, D_MODEL), F32) if meta is None else meta
    return jnp.concatenate([g1, g2, g3, g4, misc, last, jnp.zeros((2, D_MODEL), F32), meta], axis=0)


def _unpack_small(p):
    nrb = N_BUCKETS * SWA_Q_HEADS
    misc = p[4]
    return dict(rel_bias=misc[:nrb].reshape(N_BUCKETS, SWA_Q_HEADS), ln_pre_mix=p[0:1], ln_post_mix=p[1:2],
                ln_pre_ffn=p[2:3], ln_post_ffn=p[3:4], b_forget=misc[nrb:nrb + 8].reshape(1, 8),
                sinks=misc[nrb + 8:nrb + 16].reshape(1, 8))


def _proj_runs():
    gw = FOX_GROUP * HEAD_DIM
    swa = SWA_Q_W + 2 * SWA_KV_HEADS * HEAD_DIM
    runs = [(0, swa)]
    for grp in range(FOX_HEADS // FOX_GROUP):
        runs += [(swa + part * FOX_W + grp * gw, swa + part * FOX_W + (grp + 1) * gw) for part in range(3)]
    return runs


def _columns_from_shards(gathered, runs, shard):
    pieces = []
    for start, stop in runs:
        for d in range(start // shard, (stop - 1) // shard + 1):
            lo = d * shard
            pieces.append(gathered[d][:, max(start, lo) - lo:min(stop, lo + shard) - lo])
    return jnp.concatenate(pieces, axis=1)


def _device_shards(qkv, gate, shard, padded):
    pos, segments = 0, []
    for start, stop in _proj_runs():
        segments.append((start, stop, qkv, pos))
        pos += stop - start
    segments.append((pos, pos + gate.shape[1], gate, 0))
    total = pos + gate.shape[1]
    assert total % shard == 0
    zeros = jnp.zeros((qkv.shape[0], padded - shard), qkv.dtype)
    out = []
    for d in range(total // shard):
        lo, hi = d * shard, (d + 1) * shard
        pieces = [arr[:, src + max(lo, s) - s:src + min(hi, e) - s]
                  for s, e, arr, src in sorted(segments, key=lambda seg: seg[0]) if max(lo, s) < min(hi, e)]
        out.append(jnp.concatenate(pieces + [zeros], axis=1))
    return jnp.stack(out)


def kernel(x, meta_tokens, rel_bias, ln_pre_mix, ln_post_mix, ln_pre_ffn, ln_post_ffn, w_in, b_forget, sinks, w_out, w_gate_up, w_down, loss_target, m_meta_tokens, m_rel_bias, m_ln_pre_mix, m_ln_post_mix, m_ln_pre_ffn, m_ln_post_ffn, m_w_in, m_b_forget, m_sinks, m_w_out, m_w_gate_up, m_w_down, v_meta_tokens, v_rel_bias, v_ln_pre_mix, v_ln_post_mix, v_ln_pre_ffn, v_ln_post_ffn, v_w_in, v_b_forget, v_sinks, v_w_out, v_w_gate_up, v_w_down):
    seq = x.shape[1]
    T = BLOCK + seq
    assert T % FOX_TILE == 0
    nq = T // FOX_TILE
    tm = _tile(T, 1056)
    cin = w_in.shape[2]
    hid = w_down.shape[1]
    F = N_DEV * hid
    assert w_gate_up.shape[2] == 2 * hid and cin <= W_IN_PAD and hid % 16 == 0

    x_i, y_i, c_i = _coords()
    core = jnp.reshape(c_i, (1,)).astype(jnp.int32)
    where = jnp.stack([2 * x_i + y_i, c_i]).astype(jnp.int32)
    w_in_s = jnp.pad(w_in[0].astype(BF16), ((0, 0), (0, W_IN_PAD - cin)))
    w_gu_t = w_gate_up[0].T
    h0, target, hn1, hn1_t, g_in, _ = _pad_rows_rms(x[0], loss_target[0], ln_pre_mix,
                                                    _gather_exchange([w_in_s, meta_tokens]), name="ag_w_in_rms_pre_mix")
    gather_rest = _gather_exchange([w_out[0].astype(BF16), w_gu_t.astype(BF16), w_down[0].astype(BF16)])
    w_qkv = _columns_from_shards(g_in, _proj_runs(), cin)
    w_f = jnp.pad(_columns_from_shards(g_in, [(D_QKV, D_PROJ)], cin), ((0, 0), (0, BLOCK - FOX_HEADS)))

    proj = _matmul(hn1, w_qkv, out_dtype=BF16, tm=tm, tn=D_QKV, name="mm_in_proj")
    proj_f = _matmul(hn1, w_f, out_dtype=F32, tm=tm, tn=BLOCK, name="mm_in_proj_f")

    f_t = proj_f[:, :FOX_HEADS].T
    bf_col = b_forget.reshape(FOX_HEADS, 1)

    oh_cur, oh_prev = _bucket_onehots()
    bias_c, bias_p = _bias_tiles(rel_bias.T, jnp.asarray(oh_cur.T), jnp.asarray(oh_prev.T), name="bias_tiles")
    far = rel_bias[N_BUCKETS - 1]
    sink_v = sinks[0]
    mix_a = _swa_fwd(proj, bias_c, bias_p, far, sink_v, name="swa_fwd")

    cum_col = _fox_gates_fwd(f_t, bf_col, name="fox_gates_fwd")
    q_b, k_b, v_b = _fox_prep(proj, cum_col, name="fox_prep")
    mix, lse_row, g_out, g_gu, g_down = _fox_fwd(q_b, k_b, v_b, mix_a, ex=gather_rest, name="fox_fwd")
    w_out_full = g_out.reshape(D_MODEL, D_MODEL)
    w_gu_full_t = g_gu.reshape(2 * F, D_MODEL)
    w_down_full = g_down.reshape(F, D_MODEL)

    a1 = _matmul(mix, w_out_full, out_dtype=F32, tm=tm, tn=D_MODEL, name="mm_out_proj")
    h1, hn2 = _post_res_norm(a1, ln_post_mix, h0, ln_pre_ffn, name="post_mix_pre_ffn")
    gate, up, act, act_t = _gate_up_swiglu(hn2, w_gu_full_t, name="mm_gate_up")
    ff = _matmul(act, w_down_full, out_dtype=F32, tm=tm, tn=512, name="mm_down")
    dh2, dff, dg_post_ffn, loss_acc = _loss_head(ff, ln_post_ffn, h1, target, name="loss_head")

    dgu = _d_act_swiglu(dff, w_down_full, gate, up, name="mm_d_act")
    d_w_down = _matmul(act_t, dff, out_dtype=F32, tm=_tile(F, 768), tn=512, name="mm_dw_down")
    dhn2 = _matmul(dgu, w_gu_full_t, out_dtype=F32, tm=tm, tn=512, tk=F, name="mm_d_hn2")
    d_w_gu_t = _matmul(dgu, hn2, ta=True, out_dtype=F32, tm=256, tn=D_MODEL, name="mm_dw_gate_up")
    dh1, dg_pre_ffn, da1, dg_post_mix = _rms_bwd_twice(h1, ln_pre_ffn, dhn2, dh2, a1, ln_post_mix,
                                                       name="rms_bwd_pre_ffn_post_mix")
    dmix = _matmul(da1, w_out_full, nt=True, out_dtype=BF16, tm=tm, tn=D_MODEL, name="mm_d_mix")
    d_w_out = _matmul(mix, da1, ta=True, out_dtype=F32, tm=512, tn=D_MODEL, name="mm_dw_out")

    ffn_grads = [g.reshape(N_DEV, -1, D_MODEL) for g in (d_w_out, d_w_gu_t, d_w_down)]
    dproj_a, dbc, dbp, dbf, dsk, *ffn_sibling = _swa_bwd(
        proj, dmix, bias_c, bias_p, far, sink_v, ex=_cores_exchange(ffn_grads), name="swa_bwd")
    ffn_sums = [_add_cores(g, r, core, name="rs_add_" + t)
                for g, r, t in zip(ffn_grads, ffn_sibling, ["w_out", "w_gate_up", "w_down"])]

    do_b = _fox_prep_bwd(dmix, mix, name="fox_prep_bwd")
    dproj, dcq, dck, *ffn_chips = _fox_bwd(
        q_b, k_b, v_b, do_b, lse_row, dproj_a, ex=_chips_exchange(ffn_sums), name="fox_bwd")
    df_t, d_bf = _fox_gates_bwd(dcq.reshape(FOX_HEADS, T), dck.reshape(FOX_HEADS, T), f_t, bf_col,
                                name="fox_gates_bwd")
    df = jnp.pad(df_t.T.astype(BF16), ((0, 0), (0, BLOCK - FOX_HEADS)))

    d_w_qkv = _matmul(hn1_t, dproj, out_dtype=F32, tm=512, tn=768, name="mm_dw_in")
    d_w_f = _matmul(hn1_t, df, out_dtype=F32, tm=512, tn=BLOCK, name="mm_dw_in_f")
    d_w_in = _device_shards(d_w_qkv, d_w_f[:, :FOX_HEADS], cin, W_IN_PAD)
    d_tab, d_sink, in_sibling = _small_grads(dbc, dbp, dbf, dsk, jnp.asarray(oh_cur), jnp.asarray(oh_prev),
                                             ex=_cores_exchange([d_w_in]), name="small_grads")
    in_sum = _add_cores(d_w_in, in_sibling, core, name="rs_add_w_in")
    dhn1, in_chips = _matmul(dproj, w_qkv, nt=True, out_dtype=F32, tm=tm, tn=512,
                             ex=_chips_exchange([in_sum]), name="mm_d_hn1")
    dx_rows, dg_pre_mix, dh0_head = _rms_bwd_rows(h0, ln_pre_mix, dhn1, df, w_f, dh1, name="rms_bwd_pre_mix")
    grad_x = dx_rows[None]
    d_meta = dh0_head[PAD_ROWS:]

    rs_out, rs_gu, rs_down = zip(ffn_grads, ffn_sibling, ffn_chips)
    updates = [("w_in", (d_w_in, in_sibling, in_chips), (w_in[0], m_w_in[0], v_w_in[0]), 256),
               ("w_out", rs_out, (w_out[0], m_w_out[0], v_w_out[0]), BLOCK),
               ("w_gate_up", rs_gu, (w_gu_t, m_w_gate_up[0].T, v_w_gate_up[0].T), hid),
               ("w_down", rs_down, (w_down[0], m_w_down[0], v_w_down[0]), hid)]
    big = [{}, {}, {}, {}]
    for t, grads, shard, ta in updates:
        res = _sum_adamw(*grads, where, *shard, ta=ta, name="rs_adamw_" + t)
        for kind in range(4):
            big[kind][t] = (res[kind].T if t == "w_gate_up" else res[kind])[None]

    loss_row = jnp.pad(loss_acc[0:1, 0:1] * (0.5 / D_MODEL), ((0, 0), (0, D_MODEL - 1)))
    s_small = _pack_small(d_tab.T, dg_pre_mix, dg_post_mix, dg_pre_ffn, dg_post_ffn, d_bf, d_sink,
                          extra=loss_row, meta=d_meta)
    w_s = _pack_small(rel_bias, ln_pre_mix, ln_post_mix, ln_pre_ffn, ln_post_ffn, b_forget, sinks)
    m_s = _pack_small(m_rel_bias, m_ln_pre_mix, m_ln_post_mix, m_ln_pre_ffn, m_ln_post_ffn, m_b_forget, m_sinks)
    v_s = _pack_small(v_rel_bias, v_ln_pre_mix, v_ln_post_mix, v_ln_pre_ffn, v_ln_post_ffn, v_b_forget, v_sinks)
    small = _small_allreduce_adamw(s_small, w_s, m_s, v_s, name="small_allreduce_adamw")
    loss = small[0][5, 0]
    mcols = meta_tokens.shape[1]
    g_meta_mine = lax.dynamic_slice(small[0][8:8 + N_META], (0, (4 * x_i + 2 * y_i + c_i) * mcols), (N_META, mcols))
    big[0]["meta_tokens"] = g_meta_mine
    for kind, arr in enumerate(_adamw(meta_tokens, g_meta_mine, m_meta_tokens, v_meta_tokens, name="adamw_meta")):
        big[kind + 1]["meta_tokens"] = arr
    small = [_unpack_small(p) for p in small]

    names = ["meta_tokens", "rel_bias", "ln_pre_mix", "ln_post_mix", "ln_pre_ffn", "ln_post_ffn", "w_in",
             "b_forget", "sinks", "w_out", "w_gate_up", "w_down"]
    outs = [loss, grad_x]
    for kind in range(4):
        for nme in names:
            outs.append(big[kind][nme] if nme in big[kind] else small[kind][nme])
    return tuple(outs)
```

```python
import math

import numpy as np
import jax
import jax.numpy as jnp
from jax import lax
from jax.experimental import pallas as pl
from jax.experimental.pallas import tpu as pltpu

F32 = jnp.float32
BF16 = jnp.bfloat16
HIGHEST = lax.Precision.HIGHEST
MESH = pl.DeviceIdType.MESH

N_DEV = 8
D_MODEL = 1024
N_META = 16
HEAD_DIM = 64
SWA_Q_HEADS = 8
SWA_KV_HEADS = 2
SWA_GROUP = 4
FOX_HEADS = 8
FOX_W = FOX_HEADS * HEAD_DIM
SWA_Q_W = SWA_Q_HEADS * HEAD_DIM
BLOCK = 128
PAD_ROWS = BLOCK - N_META
N_BUCKETS = 32
MAX_DISTANCE = 128
D_FF = 2816
D_QKV = 2304
D_PROJ = D_QKV + FOX_HEADS
D_PROJ_PAD = 2560
EPS = 1e-6
NEG = -1e30
SCALE = HEAD_DIM ** -0.5
ADAM_LR, ADAM_B1, ADAM_B2, ADAM_EPS, ADAM_WD, ADAM_STEP = 0.001, 0.9, 0.999, 1e-08, 0.01, 10
VMEM_LIMIT = 56 * 1024 * 1024
FOX_TILE = 384
FOX_GROUP = 4
W_IN_PAD = 384

NT = (((1,), (1,)), ((), ()))
NN = (((1,), (0,)), ((), ()))
TN = (((0,), (0,)), ((), ()))


def _params(sem=None, **kw):
    if sem is not None:
        kw["dimension_semantics"] = sem
    return pltpu.CompilerParams(vmem_limit_bytes=VMEM_LIMIT, **kw)


def _tile(n, target, mult=16):
    best = None
    for t in range(mult, min(n, target) + 1, mult):
        if n % t == 0:
            best = t
    assert best is not None, (n, target)
    return best


def _matmul(a, b, *, nt=False, ta=False, out_dtype, tm, tn, tk=None, ex=None, name):
    M, K = a.shape[::-1] if ta else a.shape
    assert not (ta and nt)
    N = b.shape[0] if nt else b.shape[1]
    tk = K if tk is None else tk
    assert M % tm == 0 and N % tn == 0 and K % tk == 0, (name, a.shape, b.shape, tm, tn, tk)
    nk = K // tk
    dn = NT if nt else (TN if ta else NN)
    a_spec = pl.BlockSpec((tk, tm), lambda i, j, k: (k, i)) if ta else pl.BlockSpec((tm, tk), lambda i, j, k: (i, k))

    def body(a_ref, b_ref, o_ref, *scr):
        part = lax.dot_general(a_ref[...], b_ref[...], dn, preferred_element_type=F32)
        if nk == 1:
            o_ref[...] = part.astype(o_ref.dtype)
        else:
            acc = scr[0]
            k = pl.program_id(2)

            @pl.when(k == 0)
            def _():
                acc[...] = part

            @pl.when(k > 0)
            def _():
                acc[...] += part

            @pl.when(k == nk - 1)
            def _():
                o_ref[...] = acc[...].astype(o_ref.dtype)

    if nt:
        b_spec = pl.BlockSpec((tn, tk), lambda i, j, k: (j, k))
    else:
        b_spec = pl.BlockSpec((tk, tn), lambda i, j, k: (k, j))
    out_shape = jax.ShapeDtypeStruct((M, N), out_dtype)
    out_spec = pl.BlockSpec((tm, tn), lambda i, j, k: (i, j))
    grid = (M // tm, N // tn, nk)
    body, x_in, x_in_specs, x_out, x_out_specs, x_scr = _carry(ex, grid, 2, 1, body)
    res = pl.pallas_call(
        body,
        out_shape=(out_shape, *x_out),
        grid=grid,
        in_specs=[a_spec, b_spec] + x_in_specs,
        out_specs=(out_spec, *x_out_specs),
        scratch_shapes=([pltpu.VMEM((tm, tn), F32)] if nk > 1 else []) + x_scr,
        compiler_params=_params(("parallel", "parallel", "arbitrary") if ex is None else ("arbitrary",) * 3),
        name=name,
    )(a, b, *x_in)
    return res[0] if ex is None else res


def _rstd(x):
    return lax.rsqrt(jnp.mean(x * x, axis=-1, keepdims=True) + EPS)


def _pad_rows_rms(x, target, g, ex, *, name):
    S, D = x.shape
    nb = S // BLOCK + 1
    ni, no = len(ex.inputs), len(ex.out_shapes)
    mcols = D // N_DEV

    def body(x_ref, t_ref, g_ref, *rest):
        side_in, (h_ref, to_ref, y_ref, yt_ref) = rest[:ni], rest[ni:ni + 4]
        side_out = rest[ni + 4:ni + 4 + no]
        meta_buf, meta_sems, *sems = rest[ni + 4 + no:]
        i = pl.program_id(0)

        @pl.when(i == 0)
        def _():
            ex.start(side_in, side_out, sems)

        def norm():
            h = h_ref[...]
            y = h * _rstd(h) * g_ref[...]
            y_ref[...] = y.astype(y_ref.dtype)
            yt_ref[...] = y.T.astype(yt_ref.dtype)

        @pl.when(i < nb - 1)
        def _():
            h_ref[...] = x_ref[...]
            to_ref[...] = t_ref[...]
            norm()

        @pl.when(i == nb - 1)
        def _():
            ex.finish(side_in, side_out, sems)
            copies = [pltpu.make_async_copy(side_out[-1].at[d], meta_buf.at[:, d * mcols:(d + 1) * mcols],
                                            meta_sems.at[d]) for d in range(N_DEV)]
            for cp in copies:
                cp.start()
            for cp in copies:
                cp.wait()
            h_ref[:PAD_ROWS, :] = jnp.zeros((PAD_ROWS, D), F32)
            h_ref[PAD_ROWS:, :] = meta_buf[...]
            to_ref[...] = jnp.zeros_like(to_ref)
            norm()

    src = pl.BlockSpec((BLOCK, D), lambda i: (jnp.minimum(i, nb - 2), 0))
    dst = pl.BlockSpec((BLOCK, D), lambda i: ((i + 1) % nb, 0))
    hbm = pl.BlockSpec(memory_space=pl.ANY)
    rows = jax.ShapeDtypeStruct((BLOCK + S, D), F32)
    return pl.pallas_call(
        body,
        out_shape=(rows, rows, jax.ShapeDtypeStruct((BLOCK + S, D), BF16), jax.ShapeDtypeStruct((D, BLOCK + S), BF16),
                   *ex.out_shapes),
        grid=(nb,),
        in_specs=[src, src, pl.BlockSpec((1, D), lambda i: (0, 0))] + [hbm] * ni,
        out_specs=(dst, dst, dst, pl.BlockSpec((D, BLOCK), lambda i: (0, (i + 1) % nb)), *([hbm] * no)),
        scratch_shapes=[pltpu.VMEM((N_META, D), F32), pltpu.SemaphoreType.DMA((N_DEV,))] + list(ex.scratch),
        compiler_params=_params(("arbitrary",)), name=name)(x, target, g, *ex.inputs)


def _post_res_norm(a, g_post, h, g_pre, *, name):
    T, D = a.shape
    tm = _tile(T, 384, BLOCK)

    def body(a_ref, gp_ref, h_ref, gn_ref, h1_ref, o_ref):
        a = a_ref[...]
        h1 = h_ref[...] + a * _rstd(a) * gp_ref[...]
        h1_ref[...] = h1
        o_ref[...] = (h1 * _rstd(h1) * gn_ref[...]).astype(o_ref.dtype)

    row = pl.BlockSpec((tm, D), lambda i: (i, 0))
    vec = pl.BlockSpec((1, D), lambda i: (0, 0))
    return pl.pallas_call(
        body, out_shape=(jax.ShapeDtypeStruct((T, D), F32), jax.ShapeDtypeStruct((T, D), BF16)), grid=(T // tm,),
        in_specs=[row, vec, row, vec], out_specs=(row, row),
        compiler_params=_params(("parallel",)), name=name)(a, g_post, h, g_pre)


def _loss_head(a, g, h, target, *, name):
    T, D = a.shape
    tm = _tile(T, 512)

    def body(a_ref, g_ref, h_ref, t_ref, dy_ref, da_ref, dg_ref, loss_ref):
        i = pl.program_id(0)
        a = a_ref[...]
        r = _rstd(a)
        ah = a * r
        y = h_ref[...] + ah * g_ref[...]
        rows = i * tm + lax.broadcasted_iota(jnp.int32, (tm, 1), 0)
        err = jnp.where(rows >= BLOCK, y - t_ref[...], 0.0)
        dy = err / D
        dy_ref[...] = dy
        dah = dy * g_ref[...]
        da_ref[...] = (r * (dah - ah * jnp.mean(dah * ah, axis=-1, keepdims=True))).astype(da_ref.dtype)
        part = jnp.sum(jnp.sum(err * err, axis=1, keepdims=True), axis=0, keepdims=True)

        @pl.when(i == 0)
        def _():
            loss_ref[...] = jnp.zeros_like(loss_ref)
            dg_ref[...] = jnp.zeros_like(dg_ref)

        loss_ref[...] += jnp.broadcast_to(part, loss_ref.shape)
        dg_ref[...] += jnp.sum(dy * ah, axis=0, keepdims=True)

    row = pl.BlockSpec((tm, D), lambda i: (i, 0))
    vec = pl.BlockSpec((1, D), lambda i: (0, 0))
    return pl.pallas_call(
        body, out_shape=(jax.ShapeDtypeStruct((T, D), F32), jax.ShapeDtypeStruct((T, D), BF16),
                         jax.ShapeDtypeStruct((1, D), F32), jax.ShapeDtypeStruct((8, 128), F32)),
        grid=(T // tm,),
        in_specs=[row, vec, row, row],
        out_specs=(row, row, vec, pl.BlockSpec((8, 128), lambda i: (0, 0))),
        compiler_params=_params(("arbitrary",)), name=name)(a, g, h, target)


def _rms_pull_back(x, g, dy):
    r = _rstd(x)
    xh = x * r
    dxh = dy * g
    return r * (dxh - xh * jnp.mean(dxh * xh, axis=-1, keepdims=True)), jnp.sum(dy * xh, axis=0, keepdims=True)


def _rms_bwd_twice(x, g, dy, res, x2, g2, *, name):
    T, D = x.shape
    tm = _tile(T, 512)

    def body(x_ref, g_ref, dy_ref, res_ref, x2_ref, g2_ref, dx_ref, dg_ref, dx2_ref, dg2_ref):
        @pl.when(pl.program_id(0) == 0)
        def _():
            dg_ref[...] = jnp.zeros_like(dg_ref)
            dg2_ref[...] = jnp.zeros_like(dg2_ref)

        dx, dg = _rms_pull_back(x_ref[...], g_ref[...], dy_ref[...].astype(F32))
        dx = dx + res_ref[...]
        dx_ref[...] = dx
        dg_ref[...] += dg
        dx2, dg2 = _rms_pull_back(x2_ref[...], g2_ref[...], dx)
        dx2_ref[...] = dx2.astype(dx2_ref.dtype)
        dg2_ref[...] += dg2

    row = pl.BlockSpec((tm, D), lambda i: (i, 0))
    vec = pl.BlockSpec((1, D), lambda i: (0, 0))
    gain = jax.ShapeDtypeStruct((1, D), F32)
    return pl.pallas_call(
        body, out_shape=(jax.ShapeDtypeStruct((T, D), F32), gain, jax.ShapeDtypeStruct((T, D), BF16), gain),
        grid=(T // tm,), in_specs=[row, vec, row, row, row, vec], out_specs=(row, vec, row, vec),
        compiler_params=_params(("arbitrary",)), name=name)(x, g, dy, res, x2, g2)


def _rms_bwd_rows(x, g, dy, a, b, res, *, name):
    T, D = x.shape
    n = a.shape[1]
    n_tail = T // BLOCK - 1
    per_step = max(p for p in (4, 3, 2, 1) if n_tail % p == 0)
    steps = n_tail // per_step
    assert T == BLOCK * (1 + n_tail)
    n_rows = 4 * (per_step + 1)

    def body(*refs):
        rows, (g_ref, b_ref), (tail_ref, dg_ref, head_ref) = refs[:n_rows], refs[n_rows:n_rows + 2], refs[n_rows + 2:]

        def block(s):
            x_ref, dy_ref, a_ref, res_ref = rows[4 * s:4 * s + 4]
            dy_all = dy_ref[...] + lax.dot_general(a_ref[...], b_ref[...], NT, preferred_element_type=F32)
            dx, dg = _rms_pull_back(x_ref[...], g_ref[...], dy_all)
            return dx + res_ref[...], dg

        @pl.when(pl.program_id(0) == 0)
        def _():
            dx, dg = block(per_step)
            head_ref[...] = dx
            dg_ref[...] = dg

        for s in range(per_step):
            dx, dg = block(s)
            tail_ref[s * BLOCK:(s + 1) * BLOCK, :] = dx
            dg_ref[...] += dg

    def blocks(width):
        tail = [pl.BlockSpec((BLOCK, width), lambda i, s=s: (per_step * i + s + 1, 0)) for s in range(per_step)]
        return tail + [pl.BlockSpec((BLOCK, width), lambda i: (0, 0))]

    specs, args = [], []
    for bx, bdy, ba, bres in zip(blocks(D), blocks(D), blocks(n), blocks(D)):
        specs += [bx, bdy, ba, bres]
        args += [x, dy, a, res]
    vec = pl.BlockSpec((1, D), lambda i: (0, 0))
    return pl.pallas_call(
        body,
        out_shape=(jax.ShapeDtypeStruct((T - BLOCK, D), F32), jax.ShapeDtypeStruct((1, D), F32),
                   jax.ShapeDtypeStruct((BLOCK, D), F32)),
        grid=(steps,), in_specs=specs + [vec, pl.BlockSpec(b.shape, lambda i: (0, 0))],
        out_specs=(pl.BlockSpec((per_step * BLOCK, D), lambda i: (i, 0)), vec, pl.BlockSpec((BLOCK, D), lambda i: (0, 0))),
        compiler_params=_params(("arbitrary",)), name=name)(*args, g, b)


def _gate_up_swiglu(a, w_t, *, name):
    T, D = a.shape
    F = w_t.shape[0] // 2
    tm = _tile(T, 1408, BLOCK)
    n = _tile(F, 256, BLOCK)
    rows = 3 * BLOCK

    def body(a_ref, wg_ref, wu_ref, g_ref, u_ref, o_ref, ot_ref):
        wg, wu = wg_ref[...], wu_ref[...]
        for r in range(0, tm, rows):
            e = min(r + rows, tm)
            x = a_ref[r:e, :]
            g = lax.dot_general(x, wg, NT, preferred_element_type=F32)
            u = lax.dot_general(x, wu, NT, preferred_element_type=F32)
            g16, u16 = g.astype(BF16), u.astype(BF16)
            g_ref[r:e, :] = g16
            u_ref[r:e, :] = u16
            gr = g16.astype(F32)
            act = gr / (1.0 + jnp.exp(-gr)) * u16.astype(F32)
            o_ref[r:e, :] = act.astype(o_ref.dtype)
            ot_ref[:, r:e] = act.T.astype(ot_ref.dtype)

    tile = pl.BlockSpec((tm, n), lambda i, j: (i, j))
    shp = jax.ShapeDtypeStruct((T, F), BF16)
    return pl.pallas_call(
        body, out_shape=(shp, shp, shp, jax.ShapeDtypeStruct((F, T), BF16)), grid=(T // tm, F // n),
        in_specs=[pl.BlockSpec((tm, D), lambda i, j: (i, 0)),
                  pl.BlockSpec((n, D), lambda i, j: (j, 0)),
                  pl.BlockSpec((n, D), lambda i, j: (j + F // n, 0))],
        out_specs=(tile, tile, tile, pl.BlockSpec((n, tm), lambda i, j: (j, i))),
        compiler_params=_params(("parallel", "parallel")), name=name)(a, w_t, w_t)


def _d_act_swiglu(dff, w_down, gate, up, *, name):
    T, D = dff.shape
    F = w_down.shape[0]
    tm = _tile(T, 384)
    chunk = 768
    assert F % BLOCK == 0

    def body(d_ref, w_ref, g_ref, u_ref, o_ref):
        dy = d_ref[...]
        for c in range(0, F, chunk):
            e = min(c + chunk, F)
            d = lax.dot_general(dy, w_ref[c:e, :], NT, preferred_element_type=F32)
            g = g_ref[:, c:e].astype(F32)
            u = u_ref[:, c:e].astype(F32)
            sg = 1.0 / (1.0 + jnp.exp(-g))
            o_ref[:, c:e] = (d * u * (sg * (1.0 + g * (1.0 - sg)))).astype(o_ref.dtype)
            o_ref[:, F + c:F + e] = (d * (g * sg)).astype(o_ref.dtype)

    row = pl.BlockSpec((tm, F), lambda i: (i, 0))
    return pl.pallas_call(
        body, out_shape=jax.ShapeDtypeStruct((T, 2 * F), BF16), grid=(T // tm,),
        in_specs=[pl.BlockSpec((tm, D), lambda i: (i, 0)), pl.BlockSpec((F, D), lambda i: (0, 0)), row, row],
        out_specs=pl.BlockSpec((tm, 2 * F), lambda i: (i, 0)),
        compiler_params=_params(("parallel",)), name=name)(dff, w_down, gate, up)


def _fox_gates_fwd(f_t, b, *, name):
    H, T = f_t.shape
    nb = T // BLOCK

    def body(f_ref, b_ref, col_ref):
        f = f_ref[...] + b_ref[...]
        ls = jnp.minimum(f, 0.0) - jnp.log(1.0 + jnp.exp(-jnp.abs(f)))
        t = lax.broadcasted_iota(jnp.int32, (H, T), 1)
        ls = jnp.where(t >= PAD_ROWS, ls, 0.0)
        upper = (lax.broadcasted_iota(jnp.int32, (BLOCK, BLOCK), 0)
                 <= lax.broadcasted_iota(jnp.int32, (BLOCK, BLOCK), 1)).astype(F32)
        carry = jnp.zeros((H, 1), F32)
        for blk in range(nb):
            seg = ls[:, blk * BLOCK:(blk + 1) * BLOCK]
            pre = jnp.dot(seg, upper, precision=HIGHEST, preferred_element_type=F32) + carry
            key_gate = jnp.where(t[:, blk * BLOCK:(blk + 1) * BLOCK] >= PAD_ROWS, pre, -NEG)
            terms = list(_split3(pre)) + list(_split3(key_gate))
            col_ref[blk * BLOCK:(blk + 1) * BLOCK, :] = jnp.concatenate(
                terms + [jnp.zeros((BLOCK - len(terms) * H, BLOCK), F32)], axis=0).T.astype(col_ref.dtype)
            carry = pre[:, BLOCK - 1:BLOCK]

    vm = pl.BlockSpec(memory_space=pltpu.VMEM)
    return pl.pallas_call(
        body, out_shape=jax.ShapeDtypeStruct((T, BLOCK), BF16),
        in_specs=[vm, vm], out_specs=vm,
        compiler_params=_params(), name=name)(f_t, b)


def _fox_gates_bwd(dcq, dck, f_t, b, *, name):
    H, T = f_t.shape
    nb = T // BLOCK

    def body(dq_ref, d_ref, f_ref, b_ref, df_ref, db_ref):
        lower = (lax.broadcasted_iota(jnp.int32, (BLOCK, BLOCK), 0)
                 >= lax.broadcasted_iota(jnp.int32, (BLOCK, BLOCK), 1)).astype(F32)
        carry = jnp.zeros((H, 1), F32)
        for blk in range(nb - 1, -1, -1):
            seg = dq_ref[:, blk * BLOCK:(blk + 1) * BLOCK] - d_ref[:, blk * BLOCK:(blk + 1) * BLOCK]
            suf = jnp.dot(seg, lower, precision=HIGHEST, preferred_element_type=F32) + carry
            df_ref[:, blk * BLOCK:(blk + 1) * BLOCK] = suf
            carry = suf[:, 0:1]
        f = f_ref[...] + b_ref[...]
        t = lax.broadcasted_iota(jnp.int32, (H, T), 1)
        df = jnp.where(t >= PAD_ROWS, df_ref[...] / (1.0 + jnp.exp(f)), 0.0)
        df_ref[...] = df
        db_ref[...] = jnp.sum(df, axis=1, keepdims=True)

    vm = pl.BlockSpec(memory_space=pltpu.VMEM)
    return pl.pallas_call(
        body, out_shape=(jax.ShapeDtypeStruct((H, T), F32), jax.ShapeDtypeStruct((H, 1), F32)),
        in_specs=[vm, vm, vm, vm], out_specs=(vm, vm),
        compiler_params=_params(), name=name)(dcq, dck, f_t, b)


def _fox_lanes(parity):
    base = HEAD_DIM * (1 - parity)
    return base, base + 3


def _split3(c):
    hi = c.astype(BF16).astype(F32)
    r = c - hi
    mid = r.astype(BF16).astype(F32)
    lo = (r - mid).astype(BF16).astype(F32)
    return hi, mid, lo


def _lanes(lane, parity, data, start, terms, ones_at=None, fill=1.0):
    out = jnp.zeros((), F32) if ones_at is None else jnp.where((lane >= ones_at) & (lane < ones_at + 3), fill, 0.0)
    for i, t in enumerate(terms):
        out = jnp.where(lane == start + i, t, out)
    return jnp.where(lane // HEAD_DIM == parity, data, out)


def _fox_prep(proj, cum_col, *, name):
    T = proj.shape[0]
    tm = _tile(T, 1408, BLOCK)
    nt = T // tm
    H = FOX_HEADS
    lanes = 2 * HEAD_DIM
    first = (proj.shape[1] - 3 * H * HEAD_DIM) // lanes

    def body(q_ref, k_ref, v_ref, c_ref, qa_ref, ka_ref, va_ref):
        p = pl.program_id(0)
        i = pl.program_id(1)
        lane = lax.broadcasted_iota(jnp.int32, (1, lanes), 1)
        src = lax.broadcasted_iota(jnp.int32, (lanes, lanes), 0)
        dst = lax.broadcasted_iota(jnp.int32, (lanes, lanes), 1)
        q2 = q_ref[...].astype(F32) * SCALE
        k2 = k_ref[...].astype(F32)
        v2 = v_ref[...].astype(F32)
        gates = c_ref[...]
        def placed(h, first_term, start):
            pick = ((src % FOX_HEADS == h) & (src // FOX_HEADS - first_term == dst - start)
                    & (dst >= start) & (dst < start + 3))
            return jnp.dot(gates, pick.astype(BF16), preferred_element_type=F32)

        moved = [(placed(2 * p + e, 0, _fox_lanes(e)[1]), placed(2 * p + e, 3, _fox_lanes(e)[0])) for e in range(2)]
        for e in range(2):
            kc, qc = _fox_lanes(e)
            own = lane // HEAD_DIM == e
            minus = jnp.where((lane >= kc) & (lane < kc + 3), -1.0, 0.0)
            ones_q = jnp.where((lane >= qc) & (lane < qc + 3), 1.0, 0.0)
            ones_k = jnp.where((lane >= kc) & (lane < kc + 3), 1.0, 0.0)
            qa_ref[e] = jnp.where(own, q2, moved[e][0] + minus).astype(BF16)
            ka_ref[e] = jnp.where(own, k2, moved[e][1] + ones_q).astype(BF16)
            va_ref[e] = jnp.where(own, v2, ones_k).astype(BF16)

    pairs = FOX_GROUP // 2

    def col(part):
        return pl.BlockSpec((tm, lanes),
                            lambda p, i: (i, first + 3 * pairs * (p // pairs) + part * pairs + p % pairs))

    out = pl.BlockSpec((2, tm, lanes), lambda p, i: (p, i, 0))
    shp = jax.ShapeDtypeStruct((H, T, lanes), BF16)
    return pl.pallas_call(
        body, out_shape=(shp, shp, shp), grid=(H // 2, nt),
        in_specs=[col(0), col(1), col(2), pl.BlockSpec((tm, lanes), lambda p, i: (i, 0))],
        out_specs=(out, out, out),
        compiler_params=_params(("parallel", "parallel")), name=name)(proj, proj, proj, cum_col)


def _fox_fwd(q_aug, k_aug, v_aug, mix, *, ex=None, name):
    H, T, lanes = q_aug.shape
    tq = FOX_TILE
    nq = T // tq
    G = FOX_HEADS

    def body(q_ref, k_ref, v_ref, mix_ref, o_ref, lse_ref, m_scr, acc_scr):
        i = pl.program_id(1)
        m_scr[...] = jnp.full(m_scr.shape, NEG, F32)
        acc_scr[...] = jnp.zeros(acc_scr.shape, F32)

        def step(kb, diag):
            off = pl.multiple_of(kb * tq, tq)
            s_t = [lax.dot_general(k_ref[g, pl.ds(off, tq), :], q_ref[g], NT, preferred_element_type=F32)
                   for g in range(G)]
            if diag:
                r = lax.broadcasted_iota(jnp.int32, (tq, tq), 0)
                c = lax.broadcasted_iota(jnp.int32, (tq, tq), 1)
                s_t = [jnp.where(c >= r, s, NEG) for s in s_t]
            m_prev = [m_scr[g] for g in range(G)]
            m_new = [jnp.maximum(m_prev[g], jnp.max(s_t[g], axis=0, keepdims=True)) for g in range(G)]
            p_t = [jnp.exp(s_t[g] - m_new[g]).astype(BF16) for g in range(G)]
            pv = [lax.dot_general(v_ref[g, pl.ds(off, tq), :], p_t[g], TN, preferred_element_type=F32)
                  for g in range(G)]
            for g in range(G):
                acc_scr[g] = jnp.exp(m_prev[g] - m_new[g]) * acc_scr[g] + pv[g]
                m_scr[g] = m_new[g]

        def loop_body(kb, carry):
            step(kb, False)
            return carry

        lax.fori_loop(0, i, loop_body, 0)
        step(i, True)
        lane = lax.broadcasted_iota(jnp.int32, (tq, lanes), 1)
        outs = []
        for g in range(G):
            ones = _fox_lanes(g % 2)[0]
            acc = acc_scr[g]
            lse_ref[g] = m_scr[g] + jnp.log(acc[ones:ones + 1, :])
            acc_t = acc.T
            outs.append(acc_t / acc_t[:, ones:ones + 1])
        for pair in range(G // 2):
            o_ref[:, pair * lanes:(pair + 1) * lanes] = jnp.where(
                lane < HEAD_DIM, outs[2 * pair], outs[2 * pair + 1]).astype(o_ref.dtype)

    blk = pl.BlockSpec((G, tq, lanes), lambda h, i: (h, i, 0))
    full = pl.BlockSpec((G, T, lanes), lambda h, i: (h, 0, 0))
    grid = (H // G, nq)
    first = mix.shape[1] // (G * HEAD_DIM) - H // G
    body, x_in, x_in_specs, x_out, x_out_specs, x_scr = _carry(ex, grid, 4, 2, body)
    return pl.pallas_call(
        body,
        out_shape=(jax.ShapeDtypeStruct(mix.shape, mix.dtype), jax.ShapeDtypeStruct((H, nq, 1, tq), F32), *x_out),
        grid=grid,
        in_specs=[blk, full, full, pl.BlockSpec(memory_space=pl.ANY)] + x_in_specs,
        out_specs=(pl.BlockSpec((tq, G * HEAD_DIM), lambda h, i: (i, first + h)),
                   pl.BlockSpec((G, None, 1, tq), lambda h, i: (h, i, 0, 0)), *x_out_specs),
        input_output_aliases={3: 0},
        scratch_shapes=[pltpu.VMEM((G, 1, tq), F32), pltpu.VMEM((G, lanes, tq), F32)] + x_scr,
        compiler_params=_params(("arbitrary", "arbitrary")), name=name)(q_aug, k_aug, v_aug, mix, *x_in)


def _fox_prep_bwd(dmix, mix, *, name):
    T = dmix.shape[0]
    H = FOX_HEADS
    tm = _tile(T, 1408, BLOCK)
    lanes = 2 * HEAD_DIM
    first = mix.shape[1] // lanes - H // 2

    def body(d_ref, o_ref, da_ref):
        lane = lax.broadcasted_iota(jnp.int32, (1, lanes), 1)
        d2 = d_ref[...].astype(F32)
        prod = d2 * o_ref[...].astype(F32)
        for e in range(2):
            delta = jnp.sum(jnp.where(lane // HEAD_DIM == e, prod, 0.0), axis=1, keepdims=True)
            da_ref[e] = _lanes(lane, e, d2, _fox_lanes(e)[0], _split3(-delta)).astype(BF16)

    pair = pl.BlockSpec((tm, lanes), lambda p, i: (i, first + p))
    return pl.pallas_call(
        body, out_shape=jax.ShapeDtypeStruct((H, T, lanes), BF16), grid=(H // 2, T // tm),
        in_specs=[pair, pair],
        out_specs=pl.BlockSpec((2, tm, lanes), lambda p, i: (p, i, 0)),
        compiler_params=_params(("parallel", "parallel")), name=name)(dmix, mix)


def _fox_bwd(q_aug, k_aug, v_aug, do_aug, lse_row, dproj, *, ex=None, name):
    H, T, lanes = q_aug.shape
    tq = FOX_TILE
    nq = T // tq
    G = FOX_GROUP

    def side_by_side(tiles, scale=None):
        lane = lax.broadcasted_iota(jnp.int32, tiles[0].shape, 1)
        out = [jnp.where(lane < HEAD_DIM, tiles[2 * p], tiles[2 * p + 1]) for p in range(G // 2)]
        out = jnp.concatenate(out, axis=1)
        return out if scale is None else out * scale

    def body(q_ref, k_ref, v_ref, do_ref, lse_ref, dproj_in, out_ref, dcq_ref, dck_ref, dk_acc, dv_acc, dq_ref):
        j = pl.program_id(1)

        @pl.when(j == 0)
        def _():
            dq_ref[...] = jnp.zeros(dq_ref.shape, F32)
            dcq_ref[...] = jnp.zeros(dcq_ref.shape, F32)

        dk_acc[...] = jnp.zeros(dk_acc.shape, F32)
        dv_acc[...] = jnp.zeros(dv_acc.shape, F32)

        def step(qb, diag):
            off = pl.multiple_of(qb * tq, tq)
            heads = range(G)
            qa = [q_ref[g, pl.ds(off, tq), :] for g in heads]
            da = [do_ref[g, pl.ds(off, tq), :] for g in heads]
            s_t = [lax.dot_general(k_ref[g], qa[g], NT, preferred_element_type=F32) for g in heads]
            dp_t = [lax.dot_general(v_ref[g], da[g], NT, preferred_element_type=F32) for g in heads]
            p_t = [jnp.exp(s_t[g] - lse_ref[g, qb]) for g in heads]
            if diag:
                r = lax.broadcasted_iota(jnp.int32, (tq, tq), 0)
                c = lax.broadcasted_iota(jnp.int32, (tq, tq), 1)
                p_t = [jnp.where(c >= r, p, 0.0) for p in p_t]
            dsb = [(p_t[g] * dp_t[g]).astype(BF16) for g in heads]
            dv = [jnp.dot(p_t[g].astype(BF16), da[g], preferred_element_type=F32) for g in heads]
            dk = [jnp.dot(dsb[g], qa[g], preferred_element_type=F32) for g in heads]
            dq = [lax.dot_general(k_ref[g], dsb[g], TN, preferred_element_type=F32) for g in heads]
            for g in heads:
                dv_acc[g] += dv[g]
                dk_acc[g] += dk[g]
                dq_ref[g, qb] += dq[g]
                dcq_ref[g, qb] += jnp.sum(dsb[g].astype(F32), axis=0, keepdims=True)

        step(j, True)

        def loop_body(qb, carry):
            step(qb, False)
            return carry

        lax.fori_loop(j + 1, nq, loop_body, 0)
        dk = [dk_acc[g] for g in range(G)]
        out_ref[:, 0:wide] = side_by_side([dq_ref[g, j].T for g in range(G)], SCALE).astype(out_ref.dtype)
        out_ref[:, wide:2 * wide] = side_by_side(dk).astype(out_ref.dtype)
        out_ref[:, 2 * wide:3 * wide] = side_by_side([dv_acc[g] for g in range(G)]).astype(out_ref.dtype)
        for g in range(G):
            kc = _fox_lanes(g % 2)[0]
            dck_ref[g] = -dk[g].T[kc:kc + 1, :]

    blk = pl.BlockSpec((G, tq, lanes), lambda h, j: (h, j, 0))
    full = pl.BlockSpec((G, T, lanes), lambda h, j: (h, 0, 0))
    wide = G * HEAD_DIM
    first = dproj.shape[1] // (3 * wide) - H // G
    grid = (H // G, nq)
    body, x_in, x_in_specs, x_out, x_out_specs, x_scr = _carry(ex, grid, 6, 3, body)
    rows = jax.ShapeDtypeStruct((H, nq, 1, tq), F32)
    all_rows = pl.BlockSpec((G, nq, 1, tq), lambda h, j: (h, 0, 0, 0))
    return pl.pallas_call(
        body,
        out_shape=(jax.ShapeDtypeStruct(dproj.shape, dproj.dtype), rows, rows, *x_out),
        grid=grid,
        in_specs=[full, blk, blk, full, all_rows, pl.BlockSpec(memory_space=pl.ANY)] + x_in_specs,
        out_specs=(pl.BlockSpec((tq, 3 * wide), lambda h, j: (j, first + h)), all_rows,
                   pl.BlockSpec((G, None, 1, tq), lambda h, j: (h, j, 0, 0)), *x_out_specs),
        input_output_aliases={5: 0},
        scratch_shapes=[pltpu.VMEM((G, tq, lanes), F32), pltpu.VMEM((G, tq, lanes), F32),
                        pltpu.VMEM((G, nq, lanes, tq), F32)] + x_scr,
        compiler_params=_params(("arbitrary", "arbitrary")), name=name,
    )(q_aug, k_aug, v_aug, do_aug, lse_row, dproj, *x_in)


def _t5_bucket_np(d):
    n = np.maximum(d, 0).astype(np.int32)
    max_exact = N_BUCKETS // 2
    nf = np.maximum(n, 1).astype(np.float32)
    large = max_exact + (np.log(nf / max_exact) / math.log(MAX_DISTANCE / max_exact)
                         * (N_BUCKETS - max_exact)).astype(np.int32)
    large = np.minimum(large, N_BUCKETS - 1)
    return np.where(n < max_exact, n, large)


def _bucket_onehots():
    k = np.arange(BLOCK)[:, None]
    q = np.arange(BLOCK)[None, :]
    eye = np.eye(N_BUCKETS, dtype=np.float32)
    cur = eye[_t5_bucket_np(q - k).reshape(-1)]
    prev = eye[_t5_bucket_np(BLOCK + q - k).reshape(-1)]
    return cur, prev


SWA_K_COL = SWA_Q_HEADS * HEAD_DIM // (2 * HEAD_DIM)
SWA_V_COL = SWA_K_COL + 1


def _swa_terms(raw, bc, bp, far, sink, n):
    k = lax.broadcasted_iota(jnp.int32, (BLOCK, BLOCK), 0)
    q = lax.broadcasted_iota(jnp.int32, (BLOCK, BLOCK), 1)
    never = 2 * BLOCK
    s_c = raw[0] + bc
    s_p = raw[1] + bp
    s_m = raw[2] + jnp.where(n == 1, bp, far)
    s_c = jnp.where((k <= q) & (k >= jnp.where(n >= 1, 0, PAD_ROWS)), s_c, NEG)
    s_p = jnp.where(k > q + jnp.where(n >= 2, 0, never), s_p, NEG)
    s_m = jnp.where(k >= jnp.where(n >= 1, PAD_ROWS, never), s_m, NEG)
    m = jnp.maximum(jnp.maximum(jnp.max(s_c, axis=0, keepdims=True), jnp.max(s_p, axis=0, keepdims=True)),
                    jnp.maximum(jnp.max(s_m, axis=0, keepdims=True), sink))
    e = [jnp.exp(s_c - m), jnp.exp(s_p - m), jnp.exp(s_m - m)]
    e_s = jnp.exp(sink - m)
    l = (jnp.sum(e[0], axis=0, keepdims=True) + jnp.sum(e[1], axis=0, keepdims=True)
         + jnp.sum(e[2], axis=0, keepdims=True) + e_s)
    return e, e_s, l


SWA_STEP = 3


def _swa_specs():
    R = SWA_STEP

    def window(col):
        return ([pl.BlockSpec((BLOCK, BLOCK), lambda s, w=w: (jnp.maximum(R * s - 1 + w, 0), col)) for w in range(R + 1)]
                + [pl.BlockSpec((BLOCK, BLOCK), lambda s: (0, col))])

    qblk = pl.BlockSpec((R * BLOCK, SWA_Q_HEADS * HEAD_DIM), lambda s: (s, 0))
    bias = pl.BlockSpec((SWA_Q_HEADS, BLOCK, BLOCK), lambda s: (0, 0, 0))
    smem = pl.BlockSpec(memory_space=pltpu.SMEM)
    return qblk, window(SWA_K_COL), window(SWA_V_COL), bias, smem


def _swa_own_kv(tile_ref, kv):
    lane = lax.broadcasted_iota(jnp.int32, (BLOCK, 2 * HEAD_DIM), 1)
    t = tile_ref[...].astype(F32)
    return jnp.where(lane // HEAD_DIM == kv, t, pltpu.roll(t, HEAD_DIM, 1)).astype(BF16)


def _swa_fwd(proj, bc, bp, far, sinks, *, name):
    T = proj.shape[0]
    nb = T // BLOCK
    G = SWA_GROUP
    Hq = SWA_Q_HEADS
    lanes = 2 * HEAD_DIM

    R = SWA_STEP
    assert nb % R == 0

    def body(*refs):
        q_ref, k_refs, v_refs = refs[0], refs[1:R + 3], refs[R + 3:2 * R + 5]
        bc_ref, bp_ref, far_ref, sink_ref, o_ref = refs[2 * R + 5:]
        s = pl.program_id(0)
        lane = lax.broadcasted_iota(jnp.int32, (BLOCK, lanes), 1)
        kvs = range(SWA_KV_HEADS)
        kk = [[_swa_own_kv(ref, kv) for ref in k_refs] for kv in kvs]
        vv = [[_swa_own_kv(ref, kv) for ref in v_refs] for kv in kvs]
        chains = [(r, h) for r in range(R) for h in range(Hq)]
        tiles = lambda r: (r + 1, r, R + 1)
        q2 = {(r, pair): q_ref[r * BLOCK:(r + 1) * BLOCK, pair * lanes:(pair + 1) * lanes].astype(F32) * SCALE
              for r in range(R) for pair in range(Hq // 2)}
        qm = {c: jnp.where(lane // HEAD_DIM == c[1] % 2, q2[c[0], c[1] // 2], 0.0).astype(BF16) for c in chains}
        raw = {c: [lax.dot_general(kk[c[1] // G][w], qm[c], NT, preferred_element_type=F32) for w in tiles(c[0])]
               for c in chains}
        terms = {c: _swa_terms(raw[c], bc_ref[c[1]], bp_ref[c[1]], far_ref[c[1]], sink_ref[c[1]], R * s + c[0])
                 for c in chains}
        o_t = {c: sum(lax.dot_general(vv[c[1] // G][w], terms[c][0][b].astype(BF16), TN, preferred_element_type=F32)
                      for b, w in enumerate(tiles(c[0]))) for c in chains}
        outs = {c: (o_t[c] / terms[c][2]).T for c in chains}
        for r in range(R):
            for pair in range(Hq // 2):
                o_ref[r * BLOCK:(r + 1) * BLOCK, pair * lanes:(pair + 1) * lanes] = jnp.where(
                    lane < HEAD_DIM, outs[r, 2 * pair], outs[r, 2 * pair + 1]).astype(o_ref.dtype)

    qblk, keys, vals, bias, smem = _swa_specs()
    return pl.pallas_call(
        body, out_shape=jax.ShapeDtypeStruct((T, D_MODEL), BF16), grid=(nb // R,),
        in_specs=[qblk] + keys + vals + [bias, bias, smem, smem],
        out_specs=qblk,
        compiler_params=_params(("parallel",)), name=name,
    )(proj, *([proj] * (2 * R + 4)), bc, bp, far, sinks)


def _swa_bwd(proj, dmix, bc, bp, far, sinks, *, ex=None, name):
    T, width = proj.shape
    nb = T // BLOCK
    G = SWA_GROUP
    Hq = SWA_Q_HEADS
    lanes = 2 * HEAD_DIM
    qw = Hq * HEAD_DIM
    own_w = qw + 2 * lanes

    R = SWA_STEP
    assert nb % R == 0
    n_in = 2 * R + 10

    def body(*refs):
        q_ref, k_refs, v_refs = refs[0], refs[1:R + 3], refs[R + 3:2 * R + 5]
        do_ref, bc_ref, bp_ref, far_ref, sink_ref = refs[2 * R + 5:n_in]
        dp_ref, dbc_ref, dbp_ref, dbf_ref, dsk_ref, dk_acc, dv_acc = refs[n_in:]
        s = pl.program_id(0)

        @pl.when(s == 0)
        def _():
            for ref in (dk_acc, dv_acc, dbc_ref, dbp_ref, dbf_ref, dsk_ref):
                ref[...] = jnp.zeros(ref.shape, F32)

        lane = lax.broadcasted_iota(jnp.int32, (BLOCK, lanes), 1)
        kvs = range(SWA_KV_HEADS)
        kk = [[_swa_own_kv(ref, kv) for ref in k_refs] for kv in kvs]
        vv = [[_swa_own_kv(ref, kv) for ref in v_refs] for kv in kvs]
        chains = [(r, h) for r in range(R) for h in range(Hq)]
        blocks = range(3)
        tiles = lambda r: (r + 1, r, R + 1)
        sub = lambda ref, r, pair: ref[r * BLOCK:(r + 1) * BLOCK, pair * lanes:(pair + 1) * lanes]
        q2 = {(r, pair): sub(q_ref, r, pair).astype(F32) * SCALE for r in range(R) for pair in range(Hq // 2)}
        d2 = {(r, pair): sub(do_ref, r, pair) for r in range(R) for pair in range(Hq // 2)}
        own = [lane // HEAD_DIM == half for half in range(2)]
        qm = {c: jnp.where(own[c[1] % 2], q2[c[0], c[1] // 2], 0.0).astype(BF16) for c in chains}
        dom = {c: jnp.where(own[c[1] % 2], d2[c[0], c[1] // 2], jnp.zeros_like(d2[0, 0])) for c in chains}
        raw = {c: [lax.dot_general(kk[c[1] // G][w], qm[c], NT, preferred_element_type=F32) for w in tiles(c[0])]
               for c in chains}
        dp = {c: [lax.dot_general(vv[c[1] // G][w], dom[c], NT, preferred_element_type=F32) for w in tiles(c[0])]
              for c in chains}
        p, ds16 = {}, {}
        for c in chains:
            r, h = c
            n = R * s + r
            e, e_s, l = _swa_terms(raw[c], bc_ref[h], bp_ref[h], far_ref[h], sink_ref[h], n)
            inv = 1.0 / l
            ph = [e[b] * inv for b in blocks]
            delta = sum(jnp.sum(ph[b] * dp[c][b], axis=0, keepdims=True) for b in blocks)
            ds = [ph[b] * (dp[c][b] - delta) for b in blocks]
            dsk_ref[h] += -(e_s * inv) * delta
            dbc_ref[h] += ds[0]
            dbp_ref[h] += ds[1] + jnp.where(n == 1, ds[2], 0.0)
            dbf_ref[h] += jnp.where(n >= 2, ds[2], 0.0)
            p[c] = [x.astype(BF16) for x in ph]
            ds16[c] = [x.astype(BF16) for x in ds]
        dq_t = {c: sum(lax.dot_general(kk[c[1] // G][w], ds16[c][b], TN, preferred_element_type=F32)
                       for b, w in enumerate(tiles(c[0]))) for c in chains}
        group = [range(kv * G, (kv + 1) * G) for kv in kvs]
        dk = {(r, kv): [sum(jnp.dot(ds16[r, h][b], qm[r, h], preferred_element_type=F32) for h in group[kv])
                        for b in blocks] for r in range(R) for kv in kvs}
        dv = {(r, kv): [sum(jnp.dot(p[r, h][b], dom[r, h], preferred_element_type=F32) for h in group[kv])
                        for b in blocks] for r in range(R) for kv in kvs}
        for r in range(R):
            n = R * s + r
            rows = pl.ds(pl.multiple_of(n * BLOCK, BLOCK), BLOCK)
            prev_rows = pl.ds(pl.multiple_of(jnp.maximum(n - 1, 0) * BLOCK, BLOCK), BLOCK)
            for pair in range(Hq // 2):
                dp_ref[rows, pair * lanes:(pair + 1) * lanes] = (jnp.where(
                    lane < HEAD_DIM, dq_t[r, 2 * pair].T, dq_t[r, 2 * pair + 1].T) * SCALE).astype(dp_ref.dtype)
            for acc, ref in ((dk, dk_acc), (dv, dv_acc)):
                tot = [[a + pltpu.roll(a, HEAD_DIM, 1) for a in acc[r, kv]] for kv in kvs]
                both = [jnp.where(lane < HEAD_DIM, tot[0][b], tot[1][b]) for b in blocks]
                ref[rows, :] += both[0]
                ref[prev_rows, :] += both[1]
                ref[0:BLOCK, :] += both[2]

        @pl.when(s == nb // R - 1)
        def _():
            dp_ref[:, qw:qw + lanes] = dk_acc[...].astype(dp_ref.dtype)
            dp_ref[:, qw + lanes:own_w] = dv_acc[...].astype(dp_ref.dtype)

    qblk, keys, vals, bias, smem = _swa_specs()
    dsk = pl.BlockSpec((Hq, 1, BLOCK), lambda s: (0, 0, 0))
    grid = (nb // R,)
    body, x_in, x_in_specs, x_out, x_out_specs, x_scr = _carry(ex, grid, n_in, 5, body)
    tile = jax.ShapeDtypeStruct((Hq, BLOCK, BLOCK), F32)
    return pl.pallas_call(
        body,
        out_shape=(jax.ShapeDtypeStruct((T, width), BF16), tile, tile, tile,
                   jax.ShapeDtypeStruct((Hq, 1, BLOCK), F32), *x_out),
        grid=grid,
        in_specs=[qblk] + keys + vals + [qblk, bias, bias, smem, smem] + x_in_specs,
        out_specs=(pl.BlockSpec((T, own_w), lambda s: (0, 0)), bias, bias, bias, dsk, *x_out_specs),
        scratch_shapes=[pltpu.VMEM((T, lanes), F32), pltpu.VMEM((T, lanes), F32)] + x_scr,
        compiler_params=_params(("arbitrary",)), name=name,
    )(proj, *([proj] * (2 * R + 4)), dmix, bc, bp, far, sinks, *x_in)


def _bias_tiles(tab_t, oh_cur_t, oh_prev_t, *, name):
    Hq = tab_t.shape[0]

    def body(t_ref, oc_ref, op_ref, bc_ref, bp_ref):
        bc_ref[...] = jnp.dot(t_ref[...], oc_ref[...], precision=HIGHEST, preferred_element_type=F32)
        bp_ref[...] = jnp.dot(t_ref[...], op_ref[...], precision=HIGHEST, preferred_element_type=F32)

    vm = pl.BlockSpec(memory_space=pltpu.VMEM)
    shp = jax.ShapeDtypeStruct((Hq, BLOCK * BLOCK), F32)
    bc, bp = pl.pallas_call(body, out_shape=(shp, shp), in_specs=[vm] * 3, out_specs=(vm, vm),
                            compiler_params=_params(), name=name)(tab_t, oh_cur_t, oh_prev_t)
    return bc.reshape(Hq, BLOCK, BLOCK), bp.reshape(Hq, BLOCK, BLOCK)


def _small_grads(dbc, dbp, dbf, dsk, oh_cur, oh_prev, *, ex=None, name):
    Hq = dbc.shape[0]

    def body(dbc_ref, dbp_ref, dbf_ref, dsk_ref, oc_ref, op_ref, tab_ref, sink_ref):
        tab = (jnp.dot(dbc_ref[...], oc_ref[...], precision=HIGHEST, preferred_element_type=F32)
               + jnp.dot(dbp_ref[...], op_ref[...], precision=HIGHEST, preferred_element_type=F32))
        far = jnp.sum(dbf_ref[...], axis=1, keepdims=True)
        last = lax.broadcasted_iota(jnp.int32, (Hq, N_BUCKETS), 1) == N_BUCKETS - 1
        tab_ref[...] = tab + jnp.where(last, far, 0.0)
        sink_ref[...] = jnp.sum(dsk_ref[...], axis=1, keepdims=True)

    vm = pl.BlockSpec(memory_space=pltpu.VMEM)
    body, x_in, x_in_specs, x_out, x_out_specs, x_scr = _carry(ex, (), 6, 2, body)
    return pl.pallas_call(
        body, out_shape=(jax.ShapeDtypeStruct((Hq, N_BUCKETS), F32), jax.ShapeDtypeStruct((Hq, 1), F32), *x_out),
        in_specs=[vm] * 6 + x_in_specs, out_specs=(vm, vm, *x_out_specs), scratch_shapes=x_scr,
        compiler_params=_params(), name=name,
    )(dbc.reshape(Hq, -1), dbp.reshape(Hq, -1), dbf.reshape(Hq, -1), dsk.reshape(Hq, -1), oh_cur, oh_prev, *x_in)


def _coords():
    return lax.axis_index("x"), lax.axis_index("y"), lax.axis_index("c")


class _Exchange:
    def __init__(self, inputs, out_shapes, scratch, start, finish):
        self.inputs, self.out_shapes, self.scratch, self.start, self.finish = inputs, out_shapes, scratch, start, finish


def _carry(ex, grid, n_in, n_out, body):
    if ex is None:
        return body, [], [], [], [], []
    ni, no = len(ex.inputs), len(ex.out_shapes)

    def at_step(which):
        cond = jnp.bool_(True)
        for axis, n in enumerate(grid):
            cond = cond & (pl.program_id(axis) == (0 if which == "first" else n - 1))
        return cond

    def wrapped(*refs):
        refs = list(refs)
        n_own_scr = len(refs) - (n_in + ni + n_out + no) - len(ex.scratch)
        own_in, side_in = refs[:n_in], refs[n_in:n_in + ni]
        own_out = refs[n_in + ni:n_in + ni + n_out]
        side_out = refs[n_in + ni + n_out:n_in + ni + n_out + no]
        rest = refs[n_in + ni + n_out + no:]
        own_scr, sems = rest[:n_own_scr], rest[n_own_scr:]

        @pl.when(at_step("first"))
        def _():
            ex.start(side_in, side_out, sems)

        body(*own_in, *own_out, *own_scr)

        @pl.when(at_step("last"))
        def _():
            ex.finish(side_in, side_out, sems)

    hbm = pl.BlockSpec(memory_space=pl.ANY)
    return wrapped, list(ex.inputs), [hbm] * ni, list(ex.out_shapes), [hbm] * no, list(ex.scratch)


def _gather_exchange(shards):
    nt = len(shards)

    def copies(ins, outs, sems):
        send_sems, recv_sems, local_sems = sems
        x, y, c = _coords()
        me, sibling = (x, y, c), (x, y, 1 - c)
        chips = [(1 - x, y), (x, 1 - y), (1 - x, 1 - y)]

        def slot(t, dev):
            return outs[t].at[4 * dev[0] + 2 * dev[1] + dev[2]]

        def copy(t, k, block, to, src=None):
            dst = slot(t, block)
            return pltpu.make_async_remote_copy(
                src_ref=dst if src is None else src, dst_ref=dst,
                send_sem=send_sems.at[t, k], recv_sem=recv_sems.at[t, k], device_id=to, device_id_type=MESH)

        mine = [pltpu.make_async_copy(ins[t], slot(t, me), local_sems.at[t]) for t in range(nt)]
        first = []
        for t in range(nt):
            first.append(copy(t, 0, me, sibling, src=ins[t]))
            first += [copy(t, 1 + j, me, (*chip, c), src=ins[t]) for j, chip in enumerate(chips)]
        return copy, mine, first, me, sibling, chips, c

    def start(ins, outs, sems):
        _, mine, first, *_ = copies(ins, outs, sems)
        for cp in mine + first:
            cp.start()

    def finish(ins, outs, sems):
        copy, mine, first, me, sibling, chips, c = copies(ins, outs, sems)
        passed = []
        for j, chip in enumerate(chips):
            for t in range(nt):
                copy(t, 1 + j, (*chip, c), me).wait_recv()
                cp = copy(t, 4 + j, (*chip, c), sibling)
                cp.start()
                passed.append(cp)
        for t in range(nt):
            copy(t, 0, sibling, me).wait_recv()
            for j, chip in enumerate(chips):
                copy(t, 4 + j, (*chip, 1 - c), me).wait_recv()
        for cp in first + passed:
            cp.wait_send()
        for cp in mine:
            cp.wait()

    return _Exchange(
        list(shards), [jax.ShapeDtypeStruct((N_DEV,) + s.shape, s.dtype) for s in shards],
        [pltpu.SemaphoreType.DMA((nt, 7)), pltpu.SemaphoreType.DMA((nt, 7)), pltpu.SemaphoreType.DMA((nt,))],
        start, finish)


def _swap_exchange(arrays, n_slices, copies):
    nt = len(arrays)

    def start(ins, outs, sems):
        for cp in copies(ins, outs, sems):
            cp.start()

    def finish(ins, outs, sems):
        sends = copies(ins, outs, sems)
        for cp in sends:
            cp.wait_recv()
        for cp in sends:
            cp.wait_send()

    return _Exchange(
        list(arrays), [jax.ShapeDtypeStruct((n_slices,) + a.shape[1:], a.dtype) for a in arrays],
        [pltpu.SemaphoreType.DMA((nt, n_slices)), pltpu.SemaphoreType.DMA((nt, n_slices))], start, finish)


def _cores_exchange(gs):
    def copies(ins, outs, sems):
        send_sems, recv_sems = sems
        x, y, c = _coords()
        return [pltpu.make_async_remote_copy(
            src_ref=ins[t].at[2 * j + (1 - c)], dst_ref=outs[t].at[j],
            send_sem=send_sems.at[t, j], recv_sem=recv_sems.at[t, j], device_id=(x, y, 1 - c), device_id_type=MESH)
            for t in range(len(gs)) for j in range(4)]

    return _swap_exchange(gs, 4, copies)


def _chips_exchange(ps):
    def copies(ins, outs, sems):
        send_sems, recv_sems = sems
        x, y, c = _coords()
        peers = [(1 - x, y), (x, 1 - y), (1 - x, 1 - y)]
        return [pltpu.make_async_remote_copy(
            src_ref=ins[t].at[2 * px + py], dst_ref=outs[t].at[k],
            send_sem=send_sems.at[t, k], recv_sem=recv_sems.at[t, k], device_id=(px, py, c), device_id_type=MESH)
            for t in range(len(ps)) for k, (px, py) in enumerate(peers)]

    return _swap_exchange(ps, 3, copies)


def _add_cores(g, r, core, *, name):
    _, A, B = g.shape
    ta = _tile(A, 512, 16)

    def body(core_ref, a_ref, b_ref, o16_ref):
        o16_ref[...] = (a_ref[...] + b_ref[...]).astype(BF16)

    blk = (None, ta, B)
    return pl.pallas_call(
        body, out_shape=jax.ShapeDtypeStruct((4, A, B), BF16),
        grid_spec=pltpu.PrefetchScalarGridSpec(
            num_scalar_prefetch=1, grid=(4, A // ta),
            in_specs=[pl.BlockSpec(blk, lambda j, i, core_ref: (2 * j + core_ref[0], i, 0)),
                      pl.BlockSpec(blk, lambda j, i, core_ref: (j, i, 0))],
            out_specs=pl.BlockSpec(blk, lambda j, i, core_ref: (j, i, 0))),
        compiler_params=_params(("parallel", "parallel")), name=name)(core, g, r)


def _adamw_math(w, g, m, v):
    m = ADAM_B1 * m + (1.0 - ADAM_B1) * g
    v = ADAM_B2 * v + (1.0 - ADAM_B2) * (g * g)
    m_hat = m / (1.0 - ADAM_B1 ** ADAM_STEP)
    v_hat = v / (1.0 - ADAM_B2 ** ADAM_STEP)
    delta = -ADAM_LR * (m_hat / (jnp.sqrt(v_hat) + ADAM_EPS) + ADAM_WD * w)
    return delta, m, v


def _sum_adamw(mine, sib, r, where, w, m, v, *, ta, name):
    Aw, Bw = w.shape
    Bg = mine.shape[2]
    assert Aw % ta == 0 and Bw <= Bg and mine.shape[1] == Aw

    def body(where_ref, p_ref, s_ref, r0, r1, r2, w_ref, m_ref, v_ref, g_out, d_out, m_out, v_out):
        g = (((p_ref[:, :Bw] + s_ref[:, :Bw]) + r0[:, :Bw].astype(F32))
             + r1[:, :Bw].astype(F32)) + r2[:, :Bw].astype(F32)
        delta, m_new, v_new = _adamw_math(w_ref[...], g, m_ref[...], v_ref[...])
        g_out[...] = g
        d_out[...] = delta
        m_out[...] = m_new
        v_out[...] = v_new

    gblk = (None, ta, Bg)
    row = pl.BlockSpec((ta, Bw), lambda i, where_ref: (i, 0))
    rspecs = [pl.BlockSpec(gblk, (lambda i, where_ref, k=k: (k, i, 0))) for k in range(3)]
    shp = jax.ShapeDtypeStruct((Aw, Bw), F32)
    return pl.pallas_call(
        body, out_shape=(shp, shp, shp, shp),
        grid_spec=pltpu.PrefetchScalarGridSpec(
            num_scalar_prefetch=1, grid=(Aw // ta,),
            in_specs=[pl.BlockSpec(gblk, lambda i, where_ref: (2 * where_ref[0] + where_ref[1], i, 0)),
                      pl.BlockSpec(gblk, lambda i, where_ref: (where_ref[0], i, 0))] + rspecs + [row, row, row],
            out_specs=(row, row, row, row)),
        compiler_params=_params(("parallel",)), name=name)(where, mine, sib, r, r, r, w, m, v)


def _adamw(w, g, m, v, *, name):
    def body(w_ref, g_ref, m_ref, v_ref, d_out, m_out, v_out):
        delta, m_new, v_new = _adamw_math(w_ref[...], g_ref[...], m_ref[...], v_ref[...])
        d_out[...] = delta
        m_out[...] = m_new
        v_out[...] = v_new

    vm = pl.BlockSpec(memory_space=pltpu.VMEM)
    shp = jax.ShapeDtypeStruct(w.shape, F32)
    return pl.pallas_call(body, out_shape=(shp, shp, shp), in_specs=[vm] * 4, out_specs=(vm, vm, vm),
                          compiler_params=_params(), name=name)(w, g, m, v)


def _small_allreduce_adamw(s, w, m, v, *, name):
    R, W = s.shape

    def body(s_ref, w_ref, m_ref, v_ref, g_out, d_out, m_out, v_out, gath, send_sems, recv_sems):
        x, y, c = _coords()
        mine = 4 * x + 2 * y + c
        gath[mine] = s_ref[...]
        peers = [((1 - x) if k & 4 else x, (1 - y) if k & 2 else y, (1 - c) if k & 1 else c) for k in range(1, N_DEV)]
        sends = []
        for k in range(1, N_DEV):
            peer = peers[k - 1]
            sends.append(pltpu.make_async_remote_copy(
                src_ref=s_ref, dst_ref=gath.at[mine], send_sem=send_sems.at[k - 1], recv_sem=recv_sems.at[k - 1],
                device_id=peer, device_id_type=MESH))
        for cp in sends:
            cp.start()
        for k in range(1, N_DEV):
            peer = peers[k - 1]
            pltpu.make_async_remote_copy(
                src_ref=s_ref, dst_ref=gath.at[4 * peer[0] + 2 * peer[1] + peer[2]],
                send_sem=send_sems.at[k - 1], recv_sem=recv_sems.at[k - 1],
                device_id=peer, device_id_type=MESH).wait_recv()
        for cp in sends:
            cp.wait_send()
        g = gath[0]
        for d in range(1, N_DEV):
            g = g + gath[d]
        delta, m_new, v_new = _adamw_math(w_ref[...], g, m_ref[...], v_ref[...])
        g_out[...] = g
        d_out[...] = delta
        m_out[...] = m_new
        v_out[...] = v_new

    vm = pl.BlockSpec(memory_space=pltpu.VMEM)
    shp = jax.ShapeDtypeStruct((R, W), F32)
    return pl.pallas_call(
        body, out_shape=(shp, shp, shp, shp), in_specs=[vm] * 4, out_specs=(vm, vm, vm, vm),
        scratch_shapes=[pltpu.VMEM((N_DEV, R, W), F32), pltpu.SemaphoreType.DMA((N_DEV - 1,)),
                        pltpu.SemaphoreType.DMA((N_DEV - 1,))],
        compiler_params=_params(), name=name)(s, w, m, v)


def _pack_small(rel_bias, g1, g2, g3, g4, b_forget, sinks, extra=None, meta=None):
    misc = jnp.concatenate([rel_bias.reshape(-1), b_forget.reshape(-1), sinks.reshape(-1)])
    misc = jnp.concatenate([misc, jnp.zeros((D_MODEL - misc.shape[0],), F32)])[None]
    last = jnp.zeros((1, D_MODEL), F32) if extra is None else extra
    meta = jnp.zeros((N_META, D_MODEL), F32) if meta is None else meta
    return jnp.concatenate([g1, g2, g3, g4, misc, last, jnp.zeros((2, D_MODEL), F32), meta], axis=0)


def _unpack_small(p):
    nrb = N_BUCKETS * SWA_Q_HEADS
    misc = p[4]
    return dict(rel_bias=misc[:nrb].reshape(N_BUCKETS, SWA_Q_HEADS), ln_pre_mix=p[0:1], ln_post_mix=p[1:2],
                ln_pre_ffn=p[2:3], ln_post_ffn=p[3:4], b_forget=misc[nrb:nrb + 8].reshape(1, 8),
                sinks=misc[nrb + 8:nrb + 16].reshape(1, 8))


def _proj_runs():
    gw = FOX_GROUP * HEAD_DIM
    swa = SWA_Q_W + 2 * SWA_KV_HEADS * HEAD_DIM
    runs = [(0, swa)]
    for grp in range(FOX_HEADS // FOX_GROUP):
        runs += [(swa + part * FOX_W + grp * gw, swa + part * FOX_W + (grp + 1) * gw) for part in range(3)]
    return runs


def _columns_from_shards(gathered, runs, shard):
    pieces = []
    for start, stop in runs:
        for d in range(start // shard, (stop - 1) // shard + 1):
            lo = d * shard
            pieces.append(gathered[d][:, max(start, lo) - lo:min(stop, lo + shard) - lo])
    return jnp.concatenate(pieces, axis=1)


def _device_shards(qkv, gate, shard, padded):
    pos, segments = 0, []
    for start, stop in _proj_runs():
        segments.append((start, stop, qkv, pos))
        pos += stop - start
    segments.append((pos, pos + gate.shape[1], gate, 0))
    total = pos + gate.shape[1]
    assert total % shard == 0
    zeros = jnp.zeros((qkv.shape[0], padded - shard), qkv.dtype)
    out = []
    for d in range(total // shard):
        lo, hi = d * shard, (d + 1) * shard
        pieces = [arr[:, src + max(lo, s) - s:src + min(hi, e) - s]
                  for s, e, arr, src in sorted(segments, key=lambda seg: seg[0]) if max(lo, s) < min(hi, e)]
        out.append(jnp.concatenate(pieces + [zeros], axis=1))
    return jnp.stack(out)


def kernel(x, meta_tokens, rel_bias, ln_pre_mix, ln_post_mix, ln_pre_ffn, ln_post_ffn, w_in, b_forget, sinks, w_out, w_gate_up, w_down, loss_target, m_meta_tokens, m_rel_bias, m_ln_pre_mix, m_ln_post_mix, m_ln_pre_ffn, m_ln_post_ffn, m_w_in, m_b_forget, m_sinks, m_w_out, m_w_gate_up, m_w_down, v_meta_tokens, v_rel_bias, v_ln_pre_mix, v_ln_post_mix, v_ln_pre_ffn, v_ln_post_ffn, v_w_in, v_b_forget, v_sinks, v_w_out, v_w_gate_up, v_w_down):
    seq = x.shape[1]
    T = BLOCK + seq
    assert T % FOX_TILE == 0
    nq = T // FOX_TILE
    tm = _tile(T, 1056)
    cin = w_in.shape[2]
    hid = w_down.shape[1]
    F = N_DEV * hid
    assert w_gate_up.shape[2] == 2 * hid and cin <= W_IN_PAD and hid % 16 == 0

    x_i, y_i, c_i = _coords()
    core = jnp.reshape(c_i, (1,)).astype(jnp.int32)
    where = jnp.stack([2 * x_i + y_i, c_i]).astype(jnp.int32)
    w_in_s = jnp.pad(w_in[0].astype(BF16), ((0, 0), (0, W_IN_PAD - cin)))
    w_gu_t = w_gate_up[0].T
    h0, target, hn1, hn1_t, g_in, _ = _pad_rows_rms(x[0], loss_target[0], ln_pre_mix,
                                                    _gather_exchange([w_in_s, meta_tokens]), name="ag_w_in_rms_pre_mix")
    gather_rest = _gather_exchange([w_out[0].astype(BF16), w_gu_t.astype(BF16), w_down[0].astype(BF16)])
    w_qkv = _columns_from_shards(g_in, _proj_runs(), cin)
    w_f = jnp.pad(_columns_from_shards(g_in, [(D_QKV, D_PROJ)], cin), ((0, 0), (0, BLOCK - FOX_HEADS)))

    proj = _matmul(hn1, w_qkv, out_dtype=BF16, tm=tm, tn=D_QKV, name="mm_in_proj")
    proj_f = _matmul(hn1, w_f, out_dtype=F32, tm=tm, tn=BLOCK, name="mm_in_proj_f")

    f_t = proj_f[:, :FOX_HEADS].T
    bf_col = b_forget.reshape(FOX_HEADS, 1)

    oh_cur, oh_prev = _bucket_onehots()
    bias_c, bias_p = _bias_tiles(rel_bias.T, jnp.asarray(oh_cur.T), jnp.asarray(oh_prev.T), name="bias_tiles")
    far = rel_bias[N_BUCKETS - 1]
    sink_v = sinks[0]
    mix_a = _swa_fwd(proj, bias_c, bias_p, far, sink_v, name="swa_fwd")

    cum_col = _fox_gates_fwd(f_t, bf_col, name="fox_gates_fwd")
    q_b, k_b, v_b = _fox_prep(proj, cum_col, name="fox_prep")
    mix, lse_row, g_out, g_gu, g_down = _fox_fwd(q_b, k_b, v_b, mix_a, ex=gather_rest, name="fox_fwd")
    w_out_full = g_out.reshape(D_MODEL, D_MODEL)
    w_gu_full_t = g_gu.reshape(2 * F, D_MODEL)
    w_down_full = g_down.reshape(F, D_MODEL)

    a1 = _matmul(mix, w_out_full, out_dtype=F32, tm=tm, tn=D_MODEL, name="mm_out_proj")
    h1, hn2 = _post_res_norm(a1, ln_post_mix, h0, ln_pre_ffn, name="post_mix_pre_ffn")
    gate, up, act, act_t = _gate_up_swiglu(hn2, w_gu_full_t, name="mm_gate_up")
    ff = _matmul(act, w_down_full, out_dtype=F32, tm=tm, tn=512, name="mm_down")
    dh2, dff, dg_post_ffn, loss_acc = _loss_head(ff, ln_post_ffn, h1, target, name="loss_head")

    dgu = _d_act_swiglu(dff, w_down_full, gate, up, name="mm_d_act")
    d_w_down = _matmul(act_t, dff, out_dtype=F32, tm=_tile(F, 768), tn=512, name="mm_dw_down")
    dhn2 = _matmul(dgu, w_gu_full_t, out_dtype=F32, tm=tm, tn=512, name="mm_d_hn2")
    d_w_gu_t = _matmul(dgu, hn2, ta=True, out_dtype=F32, tm=256, tn=D_MODEL, name="mm_dw_gate_up")
    dh1, dg_pre_ffn, da1, dg_post_mix = _rms_bwd_twice(h1, ln_pre_ffn, dhn2, dh2, a1, ln_post_mix,
                                                       name="rms_bwd_pre_ffn_post_mix")
    dmix = _matmul(da1, w_out_full, nt=True, out_dtype=BF16, tm=tm, tn=D_MODEL, name="mm_d_mix")
    d_w_out = _matmul(mix, da1, ta=True, out_dtype=F32, tm=512, tn=D_MODEL, name="mm_dw_out")

    ffn_grads = [g.reshape(N_DEV, -1, D_MODEL) for g in (d_w_out, d_w_gu_t, d_w_down)]
    dproj_a, dbc, dbp, dbf, dsk, *ffn_sibling = _swa_bwd(
        proj, dmix, bias_c, bias_p, far, sink_v, ex=_cores_exchange(ffn_grads), name="swa_bwd")
    ffn_sums = [_add_cores(g, r, core, name="rs_add_" + t)
                for g, r, t in zip(ffn_grads, ffn_sibling, ["w_out", "w_gate_up", "w_down"])]

    do_b = _fox_prep_bwd(dmix, mix, name="fox_prep_bwd")
    dproj, dcq, dck, *ffn_chips = _fox_bwd(
        q_b, k_b, v_b, do_b, lse_row, dproj_a, ex=_chips_exchange(ffn_sums), name="fox_bwd")
    df_t, d_bf = _fox_gates_bwd(dcq.reshape(FOX_HEADS, T), dck.reshape(FOX_HEADS, T), f_t, bf_col,
                                name="fox_gates_bwd")
    df = jnp.pad(df_t.T.astype(BF16), ((0, 0), (0, BLOCK - FOX_HEADS)))

    d_w_qkv = _matmul(hn1_t, dproj, out_dtype=F32, tm=512, tn=768, name="mm_dw_in")
    d_w_f = _matmul(hn1_t, df, out_dtype=F32, tm=512, tn=BLOCK, name="mm_dw_in_f")
    d_w_in = _device_shards(d_w_qkv, d_w_f[:, :FOX_HEADS], cin, W_IN_PAD)
    d_tab, d_sink, in_sibling = _small_grads(dbc, dbp, dbf, dsk, jnp.asarray(oh_cur), jnp.asarray(oh_prev),
                                             ex=_cores_exchange([d_w_in]), name="small_grads")
    in_sum = _add_cores(d_w_in, in_sibling, core, name="rs_add_w_in")
    dhn1, in_chips = _matmul(dproj, w_qkv, nt=True, out_dtype=F32, tm=tm, tn=512,
                             ex=_chips_exchange([in_sum]), name="mm_d_hn1")
    dx_rows, dg_pre_mix, dh0_head = _rms_bwd_rows(h0, ln_pre_mix, dhn1, df, w_f, dh1, name="rms_bwd_pre_mix")
    grad_x = dx_rows[None]
    d_meta = dh0_head[PAD_ROWS:]

    rs_out, rs_gu, rs_down = zip(ffn_grads, ffn_sibling, ffn_chips)
    updates = [("w_in", (d_w_in, in_sibling, in_chips), (w_in[0], m_w_in[0], v_w_in[0]), 256),
               ("w_out", rs_out, (w_out[0], m_w_out[0], v_w_out[0]), BLOCK),
               ("w_gate_up", rs_gu, (w_gu_t, m_w_gate_up[0].T, v_w_gate_up[0].T), hid),
               ("w_down", rs_down, (w_down[0], m_w_down[0], v_w_down[0]), hid)]
    big = [{}, {}, {}, {}]
    for t, grads, shard, ta in updates:
        res = _sum_adamw(*grads, where, *shard, ta=ta, name="rs_adamw_" + t)
        for kind in range(4):
            big[kind][t] = (res[kind].T if t == "w_gate_up" else res[kind])[None]

    loss_row = jnp.pad(loss_acc[0:1, 0:1] * (0.5 / D_MODEL), ((0, 0), (0, D_MODEL - 1)))
    s_small = _pack_small(d_tab.T, dg_pre_mix, dg_post_mix, dg_pre_ffn, dg_post_ffn, d_bf, d_sink,
                          extra=loss_row, meta=d_meta)
    w_s = _pack_small(rel_bias, ln_pre_mix, ln_post_mix, ln_pre_ffn, ln_post_ffn, b_forget, sinks)
    m_s = _pack_small(m_rel_bias, m_ln_pre_mix, m_ln_post_mix, m_ln_pre_ffn, m_ln_post_ffn, m_b_forget, m_sinks)
    v_s = _pack_small(v_rel_bias, v_ln_pre_mix, v_ln_post_mix, v_ln_pre_ffn, v_ln_post_ffn, v_b_forget, v_sinks)
    small = _small_allreduce_adamw(s_small, w_s, m_s, v_s, name="small_allreduce_adamw")
    loss = small[0][5, 0]
    mcols = meta_tokens.shape[1]
    g_meta_mine = lax.dynamic_slice(small[0][8:8 + N_META], (0, (4 * x_i + 2 * y_i + c_i) * mcols), (N_META, mcols))
    big[0]["meta_tokens"] = g_meta_mine
    for kind, arr in enumerate(_adamw(meta_tokens, g_meta_mine, m_meta_tokens, v_meta_tokens, name="adamw_meta")):
        big[kind + 1]["meta_tokens"] = arr
    small = [_unpack_small(p) for p in small]

    names = ["meta_tokens", "rel_bias", "ln_pre_mix", "ln_post_mix", "ln_pre_ffn", "ln_post_ffn", "w_in",
             "b_forget", "sinks", "w_out", "w_gate_up", "w_down"]
    outs = [loss, grad_x]
    for kind in range(4):
        for nme in names:
            outs.append(big[kind][nme] if nme in big[kind] else small[kind][nme])
    return tuple(outs)
```

```python
import math

import numpy as np
import jax
import jax.numpy as jnp
from jax import lax
from jax.experimental import pallas as pl
from jax.experimental.pallas import tpu as pltpu

F32 = jnp.float32
BF16 = jnp.bfloat16
HIGHEST = lax.Precision.HIGHEST
MESH = pl.DeviceIdType.MESH

N_DEV = 8
D_MODEL = 1024
N_META = 16
HEAD_DIM = 64
SWA_Q_HEADS = 8
SWA_KV_HEADS = 2
SWA_GROUP = 4
FOX_HEADS = 8
FOX_W = FOX_HEADS * HEAD_DIM
SWA_Q_W = SWA_Q_HEADS * HEAD_DIM
BLOCK = 128
PAD_ROWS = BLOCK - N_META
N_BUCKETS = 32
MAX_DISTANCE = 128
D_FF = 2816
D_QKV = 2304
D_PROJ = D_QKV + FOX_HEADS
D_PROJ_PAD = 2560
EPS = 1e-6
NEG = -1e30
SCALE = HEAD_DIM ** -0.5
ADAM_LR, ADAM_B1, ADAM_B2, ADAM_EPS, ADAM_WD, ADAM_STEP = 0.001, 0.9, 0.999, 1e-08, 0.01, 10
VMEM_LIMIT = 56 * 1024 * 1024
FOX_TILE = 384
FOX_GROUP = 4
W_IN_PAD = 384

NT = (((1,), (1,)), ((), ()))
NN = (((1,), (0,)), ((), ()))
TN = (((0,), (0,)), ((), ()))


def _params(sem=None, **kw):
    if sem is not None:
        kw["dimension_semantics"] = sem
    return pltpu.CompilerParams(vmem_limit_bytes=VMEM_LIMIT, **kw)


def _tile(n, target, mult=16):
    best = None
    for t in range(mult, min(n, target) + 1, mult):
        if n % t == 0:
            best = t
    assert best is not None, (n, target)
    return best


def _matmul(a, b, *, nt=False, ta=False, out_dtype, tm, tn, tk=None, ex=None, name):
    M, K = a.shape[::-1] if ta else a.shape
    assert not (ta and nt)
    N = b.shape[0] if nt else b.shape[1]
    tk = K if tk is None else tk
    assert M % tm == 0 and N % tn == 0 and K % tk == 0, (name, a.shape, b.shape, tm, tn, tk)
    nk = K // tk
    dn = NT if nt else (TN if ta else NN)
    a_spec = pl.BlockSpec((tk, tm), lambda i, j, k: (k, i)) if ta else pl.BlockSpec((tm, tk), lambda i, j, k: (i, k))

    def body(a_ref, b_ref, o_ref, *scr):
        part = lax.dot_general(a_ref[...], b_ref[...], dn, preferred_element_type=F32)
        if nk == 1:
            o_ref[...] = part.astype(o_ref.dtype)
        else:
            acc = scr[0]
            k = pl.program_id(2)

            @pl.when(k == 0)
            def _():
                acc[...] = part

            @pl.when(k > 0)
            def _():
                acc[...] += part

            @pl.when(k == nk - 1)
            def _():
                o_ref[...] = acc[...].astype(o_ref.dtype)

    if nt:
        b_spec = pl.BlockSpec((tn, tk), lambda i, j, k: (j, k))
    else:
        b_spec = pl.BlockSpec((tk, tn), lambda i, j, k: (k, j))
    out_shape = jax.ShapeDtypeStruct((M, N), out_dtype)
    out_spec = pl.BlockSpec((tm, tn), lambda i, j, k: (i, j))
    grid = (M // tm, N // tn, nk)
    body, x_in, x_in_specs, x_out, x_out_specs, x_scr = _carry(ex, grid, 2, 1, body)
    res = pl.pallas_call(
        body,
        out_shape=(out_shape, *x_out),
        grid=grid,
        in_specs=[a_spec, b_spec] + x_in_specs,
        out_specs=(out_spec, *x_out_specs),
        scratch_shapes=([pltpu.VMEM((tm, tn), F32)] if nk > 1 else []) + x_scr,
        compiler_params=_params(("parallel", "parallel", "arbitrary") if ex is None else ("arbitrary",) * 3),
        name=name,
    )(a, b, *x_in)
    return res[0] if ex is None else res


def _rstd(x):
    return lax.rsqrt(jnp.mean(x * x, axis=-1, keepdims=True) + EPS)


def _pad_rows_rms(x, target, g, ex, *, name):
    S, D = x.shape
    nb = S // BLOCK + 1
    ni, no = len(ex.inputs), len(ex.out_shapes)
    mcols = D // N_DEV

    def body(x_ref, t_ref, g_ref, *rest):
        side_in, (h_ref, to_ref, y_ref, yt_ref) = rest[:ni], rest[ni:ni + 4]
        side_out = rest[ni + 4:ni + 4 + no]
        meta_buf, meta_sems, *sems = rest[ni + 4 + no:]
        i = pl.program_id(0)

        @pl.when(i == 0)
        def _():
            ex.start(side_in, side_out, sems)

        def norm():
            h = h_ref[...]
            y = h * _rstd(h) * g_ref[...]
            y_ref[...] = y.astype(y_ref.dtype)
            yt_ref[...] = y.T.astype(yt_ref.dtype)

        @pl.when(i < nb - 1)
        def _():
            h_ref[...] = x_ref[...]
            to_ref[...] = t_ref[...]
            norm()

        @pl.when(i == nb - 1)
        def _():
            ex.finish(side_in, side_out, sems)
            copies = [pltpu.make_async_copy(side_out[-1].at[d], meta_buf.at[:, d * mcols:(d + 1) * mcols],
                                            meta_sems.at[d]) for d in range(N_DEV)]
            for cp in copies:
                cp.start()
            for cp in copies:
                cp.wait()
            h_ref[:PAD_ROWS, :] = jnp.zeros((PAD_ROWS, D), F32)
            h_ref[PAD_ROWS:, :] = meta_buf[...]
            to_ref[...] = jnp.zeros_like(to_ref)
            norm()

    src = pl.BlockSpec((BLOCK, D), lambda i: (jnp.minimum(i, nb - 2), 0))
    dst = pl.BlockSpec((BLOCK, D), lambda i: ((i + 1) % nb, 0))
    hbm = pl.BlockSpec(memory_space=pl.ANY)
    rows = jax.ShapeDtypeStruct((BLOCK + S, D), F32)
    return pl.pallas_call(
        body,
        out_shape=(rows, rows, jax.ShapeDtypeStruct((BLOCK + S, D), BF16), jax.ShapeDtypeStruct((D, BLOCK + S), BF16),
                   *ex.out_shapes),
        grid=(nb,),
        in_specs=[src, src, pl.BlockSpec((1, D), lambda i: (0, 0))] + [hbm] * ni,
        out_specs=(dst, dst, dst, pl.BlockSpec((D, BLOCK), lambda i: (0, (i + 1) % nb)), *([hbm] * no)),
        scratch_shapes=[pltpu.VMEM((N_META, D), F32), pltpu.SemaphoreType.DMA((N_DEV,))] + list(ex.scratch),
        compiler_params=_params(("arbitrary",)), name=name)(x, target, g, *ex.inputs)


def _post_res_norm(a, g_post, h, g_pre, *, name):
    T, D = a.shape
    tm = _tile(T, 384, BLOCK)

    def body(a_ref, gp_ref, h_ref, gn_ref, h1_ref, o_ref):
        a = a_ref[...]
        h1 = h_ref[...] + a * _rstd(a) * gp_ref[...]
        h1_ref[...] = h1
        o_ref[...] = (h1 * _rstd(h1) * gn_ref[...]).astype(o_ref.dtype)

    row = pl.BlockSpec((tm, D), lambda i: (i, 0))
    vec = pl.BlockSpec((1, D), lambda i: (0, 0))
    return pl.pallas_call(
        body, out_shape=(jax.ShapeDtypeStruct((T, D), F32), jax.ShapeDtypeStruct((T, D), BF16)), grid=(T // tm,),
        in_specs=[row, vec, row, vec], out_specs=(row, row),
        compiler_params=_params(("parallel",)), name=name)(a, g_post, h, g_pre)


def _loss_head(a, g, h, target, *, name):
    T, D = a.shape
    tm = _tile(T, 512)

    def body(a_ref, g_ref, h_ref, t_ref, dy_ref, da_ref, dg_ref, loss_ref):
        i = pl.program_id(0)
        a = a_ref[...]
        r = _rstd(a)
        ah = a * r
        y = h_ref[...] + ah * g_ref[...]
        rows = i * tm + lax.broadcasted_iota(jnp.int32, (tm, 1), 0)
        err = jnp.where(rows >= BLOCK, y - t_ref[...], 0.0)
        dy = err / D
        dy_ref[...] = dy
        dah = dy * g_ref[...]
        da_ref[...] = (r * (dah - ah * jnp.mean(dah * ah, axis=-1, keepdims=True))).astype(da_ref.dtype)
        part = jnp.sum(jnp.sum(err * err, axis=1, keepdims=True), axis=0, keepdims=True)

        @pl.when(i == 0)
        def _():
            loss_ref[...] = jnp.zeros_like(loss_ref)
            dg_ref[...] = jnp.zeros_like(dg_ref)

        loss_ref[...] += jnp.broadcast_to(part, loss_ref.shape)
        dg_ref[...] += jnp.sum(dy * ah, axis=0, keepdims=True)

    row = pl.BlockSpec((tm, D), lambda i: (i, 0))
    vec = pl.BlockSpec((1, D), lambda i: (0, 0))
    return pl.pallas_call(
        body, out_shape=(jax.ShapeDtypeStruct((T, D), F32), jax.ShapeDtypeStruct((T, D), BF16),
                         jax.ShapeDtypeStruct((1, D), F32), jax.ShapeDtypeStruct((8, 128), F32)),
        grid=(T // tm,),
        in_specs=[row, vec, row, row],
        out_specs=(row, row, vec, pl.BlockSpec((8, 128), lambda i: (0, 0))),
        compiler_params=_params(("arbitrary",)), name=name)(a, g, h, target)


def _rms_pull_back(x, g, dy):
    r = _rstd(x)
    xh = x * r
    dxh = dy * g
    return r * (dxh - xh * jnp.mean(dxh * xh, axis=-1, keepdims=True)), jnp.sum(dy * xh, axis=0, keepdims=True)


def _rms_bwd_twice(x, g, dy, res, x2, g2, *, name):
    T, D = x.shape
    tm = _tile(T, 512)

    def body(x_ref, g_ref, dy_ref, res_ref, x2_ref, g2_ref, dx_ref, dg_ref, dx2_ref, dg2_ref):
        @pl.when(pl.program_id(0) == 0)
        def _():
            dg_ref[...] = jnp.zeros_like(dg_ref)
            dg2_ref[...] = jnp.zeros_like(dg2_ref)

        dx, dg = _rms_pull_back(x_ref[...], g_ref[...], dy_ref[...].astype(F32))
        dx = dx + res_ref[...]
        dx_ref[...] = dx
        dg_ref[...] += dg
        dx2, dg2 = _rms_pull_back(x2_ref[...], g2_ref[...], dx)
        dx2_ref[...] = dx2.astype(dx2_ref.dtype)
        dg2_ref[...] += dg2

    row = pl.BlockSpec((tm, D), lambda i: (i, 0))
    vec = pl.BlockSpec((1, D), lambda i: (0, 0))
    gain = jax.ShapeDtypeStruct((1, D), F32)
    return pl.pallas_call(
        body, out_shape=(jax.ShapeDtypeStruct((T, D), F32), gain, jax.ShapeDtypeStruct((T, D), BF16), gain),
        grid=(T // tm,), in_specs=[row, vec, row, row, row, vec], out_specs=(row, vec, row, vec),
        compiler_params=_params(("arbitrary",)), name=name)(x, g, dy, res, x2, g2)


def _rms_bwd_rows(x, g, dy, a, b, res, *, name):
    T, D = x.shape
    n = a.shape[1]
    n_tail = T // BLOCK - 1
    per_step = max(p for p in (4, 3, 2, 1) if n_tail % p == 0)
    steps = n_tail // per_step
    assert T == BLOCK * (1 + n_tail)
    n_rows = 4 * (per_step + 1)

    def body(*refs):
        rows, (g_ref, b_ref), (tail_ref, dg_ref, head_ref) = refs[:n_rows], refs[n_rows:n_rows + 2], refs[n_rows + 2:]

        def block(s):
            x_ref, dy_ref, a_ref, res_ref = rows[4 * s:4 * s + 4]
            dy_all = dy_ref[...] + lax.dot_general(a_ref[...], b_ref[...], NT, preferred_element_type=F32)
            dx, dg = _rms_pull_back(x_ref[...], g_ref[...], dy_all)
            return dx + res_ref[...], dg

        @pl.when(pl.program_id(0) == 0)
        def _():
            dx, dg = block(per_step)
            head_ref[...] = dx
            dg_ref[...] = dg

        for s in range(per_step):
            dx, dg = block(s)
            tail_ref[s * BLOCK:(s + 1) * BLOCK, :] = dx
            dg_ref[...] += dg

    def blocks(width):
        tail = [pl.BlockSpec((BLOCK, width), lambda i, s=s: (per_step * i + s + 1, 0)) for s in range(per_step)]
        return tail + [pl.BlockSpec((BLOCK, width), lambda i: (0, 0))]

    specs, args = [], []
    for bx, bdy, ba, bres in zip(blocks(D), blocks(D), blocks(n), blocks(D)):
        specs += [bx, bdy, ba, bres]
        args += [x, dy, a, res]
    vec = pl.BlockSpec((1, D), lambda i: (0, 0))
    return pl.pallas_call(
        body,
        out_shape=(jax.ShapeDtypeStruct((T - BLOCK, D), F32), jax.ShapeDtypeStruct((1, D), F32),
                   jax.ShapeDtypeStruct((BLOCK, D), F32)),
        grid=(steps,), in_specs=specs + [vec, pl.BlockSpec(b.shape, lambda i: (0, 0))],
        out_specs=(pl.BlockSpec((per_step * BLOCK, D), lambda i: (i, 0)), vec, pl.BlockSpec((BLOCK, D), lambda i: (0, 0))),
        compiler_params=_params(("arbitrary",)), name=name)(*args, g, b)


def _gate_up_swiglu(a, w_t, *, name):
    T, D = a.shape
    F = w_t.shape[0] // 2
    tm = _tile(T, 1408, BLOCK)
    n = _tile(F, 256, BLOCK)
    rows = 3 * BLOCK

    def body(a_ref, wg_ref, wu_ref, g_ref, u_ref, o_ref, ot_ref):
        wg, wu = wg_ref[...], wu_ref[...]
        for r in range(0, tm, rows):
            e = min(r + rows, tm)
            x = a_ref[r:e, :]
            g = lax.dot_general(x, wg, NT, preferred_element_type=F32)
            u = lax.dot_general(x, wu, NT, preferred_element_type=F32)
            g16, u16 = g.astype(BF16), u.astype(BF16)
            g_ref[r:e, :] = g16
            u_ref[r:e, :] = u16
            gr = g16.astype(F32)
            act = gr / (1.0 + jnp.exp(-gr)) * u16.astype(F32)
            o_ref[r:e, :] = act.astype(o_ref.dtype)
            ot_ref[:, r:e] = act.T.astype(ot_ref.dtype)

    tile = pl.BlockSpec((tm, n), lambda i, j: (i, j))
    shp = jax.ShapeDtypeStruct((T, F), BF16)
    return pl.pallas_call(
        body, out_shape=(shp, shp, shp, jax.ShapeDtypeStruct((F, T), BF16)), grid=(T // tm, F // n),
        in_specs=[pl.BlockSpec((tm, D), lambda i, j: (i, 0)),
                  pl.BlockSpec((n, D), lambda i, j: (j, 0)),
                  pl.BlockSpec((n, D), lambda i, j: (j + F // n, 0))],
        out_specs=(tile, tile, tile, pl.BlockSpec((n, tm), lambda i, j: (j, i))),
        compiler_params=_params(("parallel", "parallel")), name=name)(a, w_t, w_t)


def _d_act_swiglu(dff, w_down, gate, up, *, name):
    T, D = dff.shape
    F = w_down.shape[0]
    tm = _tile(T, 384)
    chunk = 768
    assert F % BLOCK == 0

    def body(d_ref, w_ref, g_ref, u_ref, o_ref):
        dy = d_ref[...]
        for c in range(0, F, chunk):
            e = min(c + chunk, F)
            d = lax.dot_general(dy, w_ref[c:e, :], NT, preferred_element_type=F32)
            g = g_ref[:, c:e].astype(F32)
            u = u_ref[:, c:e].astype(F32)
            sg = 1.0 / (1.0 + jnp.exp(-g))
            o_ref[:, c:e] = (d * u * (sg * (1.0 + g * (1.0 - sg)))).astype(o_ref.dtype)
            o_ref[:, F + c:F + e] = (d * (g * sg)).astype(o_ref.dtype)

    row = pl.BlockSpec((tm, F), lambda i: (i, 0))
    return pl.pallas_call(
        body, out_shape=jax.ShapeDtypeStruct((T, 2 * F), BF16), grid=(T // tm,),
        in_specs=[pl.BlockSpec((tm, D), lambda i: (i, 0)), pl.BlockSpec((F, D), lambda i: (0, 0)), row, row],
        out_specs=pl.BlockSpec((tm, 2 * F), lambda i: (i, 0)),
        compiler_params=_params(("parallel",)), name=name)(dff, w_down, gate, up)


def _fox_gates_fwd(f_t, b, *, name):
    H, T = f_t.shape
    nb = T // BLOCK

    def body(f_ref, b_ref, col_ref):
        f = f_ref[...] + b_ref[...]
        ls = jnp.minimum(f, 0.0) - jnp.log(1.0 + jnp.exp(-jnp.abs(f)))
        t = lax.broadcasted_iota(jnp.int32, (H, T), 1)
        ls = jnp.where(t >= PAD_ROWS, ls, 0.0)
        upper = (lax.broadcasted_iota(jnp.int32, (BLOCK, BLOCK), 0)
                 <= lax.broadcasted_iota(jnp.int32, (BLOCK, BLOCK), 1)).astype(F32)
        carry = jnp.zeros((H, 1), F32)
        for blk in range(nb):
            seg = ls[:, blk * BLOCK:(blk + 1) * BLOCK]
            pre = jnp.dot(seg, upper, precision=HIGHEST, preferred_element_type=F32) + carry
            key_gate = jnp.where(t[:, blk * BLOCK:(blk + 1) * BLOCK] >= PAD_ROWS, pre, -NEG)
            terms = list(_split3(pre)) + list(_split3(key_gate))
            col_ref[blk * BLOCK:(blk + 1) * BLOCK, :] = jnp.concatenate(
                terms + [jnp.zeros((BLOCK - len(terms) * H, BLOCK), F32)], axis=0).T.astype(col_ref.dtype)
            carry = pre[:, BLOCK - 1:BLOCK]

    vm = pl.BlockSpec(memory_space=pltpu.VMEM)
    return pl.pallas_call(
        body, out_shape=jax.ShapeDtypeStruct((T, BLOCK), BF16),
        in_specs=[vm, vm], out_specs=vm,
        compiler_params=_params(), name=name)(f_t, b)


def _fox_gates_bwd(dcq, dck, f_t, b, *, name):
    H, T = f_t.shape
    nb = T // BLOCK

    def body(dq_ref, d_ref, f_ref, b_ref, df_ref, db_ref):
        lower = (lax.broadcasted_iota(jnp.int32, (BLOCK, BLOCK), 0)
                 >= lax.broadcasted_iota(jnp.int32, (BLOCK, BLOCK), 1)).astype(F32)
        carry = jnp.zeros((H, 1), F32)
        for blk in range(nb - 1, -1, -1):
            seg = dq_ref[:, blk * BLOCK:(blk + 1) * BLOCK] - d_ref[:, blk * BLOCK:(blk + 1) * BLOCK]
            suf = jnp.dot(seg, lower, precision=HIGHEST, preferred_element_type=F32) + carry
            df_ref[:, blk * BLOCK:(blk + 1) * BLOCK] = suf
            carry = suf[:, 0:1]
        f = f_ref[...] + b_ref[...]
        t = lax.broadcasted_iota(jnp.int32, (H, T), 1)
        df = jnp.where(t >= PAD_ROWS, df_ref[...] / (1.0 + jnp.exp(f)), 0.0)
        df_ref[...] = df
        db_ref[...] = jnp.sum(df, axis=1, keepdims=True)

    vm = pl.BlockSpec(memory_space=pltpu.VMEM)
    return pl.pallas_call(
        body, out_shape=(jax.ShapeDtypeStruct((H, T), F32), jax.ShapeDtypeStruct((H, 1), F32)),
        in_specs=[vm, vm, vm, vm], out_specs=(vm, vm),
        compiler_params=_params(), name=name)(dcq, dck, f_t, b)


def _fox_lanes(parity):
    base = HEAD_DIM * (1 - parity)
    return base, base + 3


def _split3(c):
    hi = c.astype(BF16).astype(F32)
    r = c - hi
    mid = r.astype(BF16).astype(F32)
    lo = (r - mid).astype(BF16).astype(F32)
    return hi, mid, lo


def _lanes(lane, parity, data, start, terms, ones_at=None, fill=1.0):
    out = jnp.zeros((), F32) if ones_at is None else jnp.where((lane >= ones_at) & (lane < ones_at + 3), fill, 0.0)
    for i, t in enumerate(terms):
        out = jnp.where(lane == start + i, t, out)
    return jnp.where(lane // HEAD_DIM == parity, data, out)


def _fox_prep(proj, cum_col, *, name):
    T = proj.shape[0]
    tm = _tile(T, 1408, BLOCK)
    nt = T // tm
    H = FOX_HEADS
    lanes = 2 * HEAD_DIM
    first = (proj.shape[1] - 3 * H * HEAD_DIM) // lanes

    def body(q_ref, k_ref, v_ref, c_ref, qa_ref, ka_ref, va_ref):
        p = pl.program_id(0)
        i = pl.program_id(1)
        lane = lax.broadcasted_iota(jnp.int32, (1, lanes), 1)
        src = lax.broadcasted_iota(jnp.int32, (lanes, lanes), 0)
        dst = lax.broadcasted_iota(jnp.int32, (lanes, lanes), 1)
        q2 = q_ref[...].astype(F32) * SCALE
        k2 = k_ref[...].astype(F32)
        v2 = v_ref[...].astype(F32)
        gates = c_ref[...]
        def placed(h, first_term, start):
            pick = ((src % FOX_HEADS == h) & (src // FOX_HEADS - first_term == dst - start)
                    & (dst >= start) & (dst < start + 3))
            return jnp.dot(gates, pick.astype(BF16), preferred_element_type=F32)

        moved = [(placed(2 * p + e, 0, _fox_lanes(e)[1]), placed(2 * p + e, 3, _fox_lanes(e)[0])) for e in range(2)]
        for e in range(2):
            kc, qc = _fox_lanes(e)
            own = lane // HEAD_DIM == e
            minus = jnp.where((lane >= kc) & (lane < kc + 3), -1.0, 0.0)
            ones_q = jnp.where((lane >= qc) & (lane < qc + 3), 1.0, 0.0)
            ones_k = jnp.where((lane >= kc) & (lane < kc + 3), 1.0, 0.0)
            qa_ref[e] = jnp.where(own, q2, moved[e][0] + minus).astype(BF16)
            ka_ref[e] = jnp.where(own, k2, moved[e][1] + ones_q).astype(BF16)
            va_ref[e] = jnp.where(own, v2, ones_k).astype(BF16)

    pairs = FOX_GROUP // 2

    def col(part):
        return pl.BlockSpec((tm, lanes),
                            lambda p, i: (i, first + 3 * pairs * (p // pairs) + part * pairs + p % pairs))

    out = pl.BlockSpec((2, tm, lanes), lambda p, i: (p, i, 0))
    shp = jax.ShapeDtypeStruct((H, T, lanes), BF16)
    return pl.pallas_call(
        body, out_shape=(shp, shp, shp), grid=(H // 2, nt),
        in_specs=[col(0), col(1), col(2), pl.BlockSpec((tm, lanes), lambda p, i: (i, 0))],
        out_specs=(out, out, out),
        compiler_params=_params(("parallel", "parallel")), name=name)(proj, proj, proj, cum_col)


def _fox_fwd(q_aug, k_aug, v_aug, mix, *, ex=None, name):
    H, T, lanes = q_aug.shape
    tq = FOX_TILE
    nq = T // tq
    G = FOX_HEADS

    def body(q_ref, k_ref, v_ref, mix_ref, o_ref, lse_ref, m_scr, acc_scr):
        i = pl.program_id(1)
        m_scr[...] = jnp.full(m_scr.shape, NEG, F32)
        acc_scr[...] = jnp.zeros(acc_scr.shape, F32)

        def step(kb, diag):
            off = pl.multiple_of(kb * tq, tq)
            s_t = [lax.dot_general(k_ref[g, pl.ds(off, tq), :], q_ref[g], NT, preferred_element_type=F32)
                   for g in range(G)]
            if diag:
                r = lax.broadcasted_iota(jnp.int32, (tq, tq), 0)
                c = lax.broadcasted_iota(jnp.int32, (tq, tq), 1)
                s_t = [jnp.where(c >= r, s, NEG) for s in s_t]
            m_prev = [m_scr[g] for g in range(G)]
            m_new = [jnp.maximum(m_prev[g], jnp.max(s_t[g], axis=0, keepdims=True)) for g in range(G)]
            p_t = [jnp.exp(s_t[g] - m_new[g]).astype(BF16) for g in range(G)]
            pv = [lax.dot_general(v_ref[g, pl.ds(off, tq), :], p_t[g], TN, preferred_element_type=F32)
                  for g in range(G)]
            for g in range(G):
                acc_scr[g] = jnp.exp(m_prev[g] - m_new[g]) * acc_scr[g] + pv[g]
                m_scr[g] = m_new[g]

        def loop_body(kb, carry):
            step(kb, False)
            return carry

        lax.fori_loop(0, i, loop_body, 0)
        step(i, True)
        lane = lax.broadcasted_iota(jnp.int32, (tq, lanes), 1)
        outs = []
        for g in range(G):
            ones = _fox_lanes(g % 2)[0]
            acc = acc_scr[g]
            lse_ref[g] = m_scr[g] + jnp.log(acc[ones:ones + 1, :])
            acc_t = acc.T
            outs.append(acc_t / acc_t[:, ones:ones + 1])
        for pair in range(G // 2):
            o_ref[:, pair * lanes:(pair + 1) * lanes] = jnp.where(
                lane < HEAD_DIM, outs[2 * pair], outs[2 * pair + 1]).astype(o_ref.dtype)

    blk = pl.BlockSpec((G, tq, lanes), lambda h, i: (h, i, 0))
    full = pl.BlockSpec((G, T, lanes), lambda h, i: (h, 0, 0))
    grid = (H // G, nq)
    first = mix.shape[1] // (G * HEAD_DIM) - H // G
    body, x_in, x_in_specs, x_out, x_out_specs, x_scr = _carry(ex, grid, 4, 2, body)
    return pl.pallas_call(
        body,
        out_shape=(jax.ShapeDtypeStruct(mix.shape, mix.dtype), jax.ShapeDtypeStruct((H, nq, 1, tq), F32), *x_out),
        grid=grid,
        in_specs=[blk, full, full, pl.BlockSpec(memory_space=pl.ANY)] + x_in_specs,
        out_specs=(pl.BlockSpec((tq, G * HEAD_DIM), lambda h, i: (i, first + h)),
                   pl.BlockSpec((G, None, 1, tq), lambda h, i: (h, i, 0, 0)), *x_out_specs),
        input_output_aliases={3: 0},
        scratch_shapes=[pltpu.VMEM((G, 1, tq), F32), pltpu.VMEM((G, lanes, tq), F32)] + x_scr,
        compiler_params=_params(("arbitrary", "arbitrary")), name=name)(q_aug, k_aug, v_aug, mix, *x_in)


def _fox_prep_bwd(dmix, mix, *, name):
    T = dmix.shape[0]
    H = FOX_HEADS
    tm = _tile(T, 1408, BLOCK)
    lanes = 2 * HEAD_DIM
    first = mix.shape[1] // lanes - H // 2

    def body(d_ref, o_ref, da_ref):
        lane = lax.broadcasted_iota(jnp.int32, (1, lanes), 1)
        d2 = d_ref[...].astype(F32)
        prod = d2 * o_ref[...].astype(F32)
        for e in range(2):
            delta = jnp.sum(jnp.where(lane // HEAD_DIM == e, prod, 0.0), axis=1, keepdims=True)
            da_ref[e] = _lanes(lane, e, d2, _fox_lanes(e)[0], _split3(-delta)).astype(BF16)

    pair = pl.BlockSpec((tm, lanes), lambda p, i: (i, first + p))
    return pl.pallas_call(
        body, out_shape=jax.ShapeDtypeStruct((H, T, lanes), BF16), grid=(H // 2, T // tm),
        in_specs=[pair, pair],
        out_specs=pl.BlockSpec((2, tm, lanes), lambda p, i: (p, i, 0)),
        compiler_params=_params(("parallel", "parallel")), name=name)(dmix, mix)


def _fox_bwd(q_aug, k_aug, v_aug, do_aug, lse_row, dproj, *, ex=None, name):
    H, T, lanes = q_aug.shape
    tq = FOX_TILE
    nq = T // tq
    G = FOX_GROUP

    def side_by_side(tiles, scale=None):
        lane = lax.broadcasted_iota(jnp.int32, tiles[0].shape, 1)
        out = [jnp.where(lane < HEAD_DIM, tiles[2 * p], tiles[2 * p + 1]) for p in range(G // 2)]
        out = jnp.concatenate(out, axis=1)
        return out if scale is None else out * scale

    def body(q_ref, k_ref, v_ref, do_ref, lse_ref, dproj_in, out_ref, dcq_ref, dck_ref, dk_acc, dv_acc, dq_ref):
        j = pl.program_id(1)

        @pl.when(j == 0)
        def _():
            dq_ref[...] = jnp.zeros(dq_ref.shape, F32)
            dcq_ref[...] = jnp.zeros(dcq_ref.shape, F32)

        dk_acc[...] = jnp.zeros(dk_acc.shape, F32)
        dv_acc[...] = jnp.zeros(dv_acc.shape, F32)

        def step(qb, diag):
            off = pl.multiple_of(qb * tq, tq)
            heads = range(G)
            qa = [q_ref[g, pl.ds(off, tq), :] for g in heads]
            da = [do_ref[g, pl.ds(off, tq), :] for g in heads]
            s_t = [lax.dot_general(k_ref[g], qa[g], NT, preferred_element_type=F32) for g in heads]
            dp_t = [lax.dot_general(v_ref[g], da[g], NT, preferred_element_type=F32) for g in heads]
            p_t = [jnp.exp(s_t[g] - lse_ref[g, qb]) for g in heads]
            if diag:
                r = lax.broadcasted_iota(jnp.int32, (tq, tq), 0)
                c = lax.broadcasted_iota(jnp.int32, (tq, tq), 1)
                p_t = [jnp.where(c >= r, p, 0.0) for p in p_t]
            dsb = [(p_t[g] * dp_t[g]).astype(BF16) for g in heads]
            dv = [jnp.dot(p_t[g].astype(BF16), da[g], preferred_element_type=F32) for g in heads]
            dk = [jnp.dot(dsb[g], qa[g], preferred_element_type=F32) for g in heads]
            dq = [lax.dot_general(k_ref[g], dsb[g], TN, preferred_element_type=F32) for g in heads]
            for g in heads:
                dv_acc[g] += dv[g]
                dk_acc[g] += dk[g]
                dq_ref[g, qb] += dq[g]
                dcq_ref[g, qb] += jnp.sum(dsb[g].astype(F32), axis=0, keepdims=True)

        step(j, True)

        def loop_body(qb, carry):
            step(qb, False)
            return carry

        lax.fori_loop(j + 1, nq, loop_body, 0)
        dk = [dk_acc[g] for g in range(G)]
        out_ref[:, 0:wide] = side_by_side([dq_ref[g, j].T for g in range(G)], SCALE).astype(out_ref.dtype)
        out_ref[:, wide:2 * wide] = side_by_side(dk).astype(out_ref.dtype)
        out_ref[:, 2 * wide:3 * wide] = side_by_side([dv_acc[g] for g in range(G)]).astype(out_ref.dtype)
        for g in range(G):
            kc = _fox_lanes(g % 2)[0]
            dck_ref[g] = -dk[g].T[kc:kc + 1, :]

    blk = pl.BlockSpec((G, tq, lanes), lambda h, j: (h, j, 0))
    full = pl.BlockSpec((G, T, lanes), lambda h, j: (h, 0, 0))
    wide = G * HEAD_DIM
    first = dproj.shape[1] // (3 * wide) - H // G
    grid = (H // G, nq)
    body, x_in, x_in_specs, x_out, x_out_specs, x_scr = _carry(ex, grid, 6, 3, body)
    rows = jax.ShapeDtypeStruct((H, nq, 1, tq), F32)
    all_rows = pl.BlockSpec((G, nq, 1, tq), lambda h, j: (h, 0, 0, 0))
    return pl.pallas_call(
        body,
        out_shape=(jax.ShapeDtypeStruct(dproj.shape, dproj.dtype), rows, rows, *x_out),
        grid=grid,
        in_specs=[full, blk, blk, full, all_rows, pl.BlockSpec(memory_space=pl.ANY)] + x_in_specs,
        out_specs=(pl.BlockSpec((tq, 3 * wide), lambda h, j: (j, first + h)), all_rows,
                   pl.BlockSpec((G, None, 1, tq), lambda h, j: (h, j, 0, 0)), *x_out_specs),
        input_output_aliases={5: 0},
        scratch_shapes=[pltpu.VMEM((G, tq, lanes), F32), pltpu.VMEM((G, tq, lanes), F32),
                        pltpu.VMEM((G, nq, lanes, tq), F32)] + x_scr,
        compiler_params=_params(("arbitrary", "arbitrary")), name=name,
    )(q_aug, k_aug, v_aug, do_aug, lse_row, dproj, *x_in)


def _t5_bucket_np(d):
    n = np.maximum(d, 0).astype(np.int32)
    max_exact = N_BUCKETS // 2
    nf = np.maximum(n, 1).astype(np.float32)
    large = max_exact + (np.log(nf / max_exact) / math.log(MAX_DISTANCE / max_exact)
                         * (N_BUCKETS - max_exact)).astype(np.int32)
    large = np.minimum(large, N_BUCKETS - 1)
    return np.where(n < max_exact, n, large)


def _bucket_onehots():
    k = np.arange(BLOCK)[:, None]
    q = np.arange(BLOCK)[None, :]
    eye = np.eye(N_BUCKETS, dtype=np.float32)
    cur = eye[_t5_bucket_np(q - k).reshape(-1)]
    prev = eye[_t5_bucket_np(BLOCK + q - k).reshape(-1)]
    return cur, prev


SWA_K_COL = SWA_Q_HEADS * HEAD_DIM // (2 * HEAD_DIM)
SWA_V_COL = SWA_K_COL + 1


def _swa_terms(raw, bc, bp, far, sink, n):
    k = lax.broadcasted_iota(jnp.int32, (BLOCK, BLOCK), 0)
    q = lax.broadcasted_iota(jnp.int32, (BLOCK, BLOCK), 1)
    never = 2 * BLOCK
    s_c = raw[0] + bc
    s_p = raw[1] + bp
    s_m = raw[2] + jnp.where(n == 1, bp, far)
    s_c = jnp.where((k <= q) & (k >= jnp.where(n >= 1, 0, PAD_ROWS)), s_c, NEG)
    s_p = jnp.where(k > q + jnp.where(n >= 2, 0, never), s_p, NEG)
    s_m = jnp.where(k >= jnp.where(n >= 1, PAD_ROWS, never), s_m, NEG)
    m = jnp.maximum(jnp.maximum(jnp.max(s_c, axis=0, keepdims=True), jnp.max(s_p, axis=0, keepdims=True)),
                    jnp.maximum(jnp.max(s_m, axis=0, keepdims=True), sink))
    e = [jnp.exp(s_c - m), jnp.exp(s_p - m), jnp.exp(s_m - m)]
    e_s = jnp.exp(sink - m)
    l = (jnp.sum(e[0], axis=0, keepdims=True) + jnp.sum(e[1], axis=0, keepdims=True)
         + jnp.sum(e[2], axis=0, keepdims=True) + e_s)
    return e, e_s, l


SWA_STEP = 3


def _swa_specs():
    R = SWA_STEP

    def window(col):
        return ([pl.BlockSpec((BLOCK, BLOCK), lambda s, w=w: (jnp.maximum(R * s - 1 + w, 0), col)) for w in range(R + 1)]
                + [pl.BlockSpec((BLOCK, BLOCK), lambda s: (0, col))])

    qblk = pl.BlockSpec((R * BLOCK, SWA_Q_HEADS * HEAD_DIM), lambda s: (s, 0))
    bias = pl.BlockSpec((SWA_Q_HEADS, BLOCK, BLOCK), lambda s: (0, 0, 0))
    smem = pl.BlockSpec(memory_space=pltpu.SMEM)
    return qblk, window(SWA_K_COL), window(SWA_V_COL), bias, smem


def _swa_own_kv(tile_ref, kv):
    lane = lax.broadcasted_iota(jnp.int32, (BLOCK, 2 * HEAD_DIM), 1)
    t = tile_ref[...].astype(F32)
    return jnp.where(lane // HEAD_DIM == kv, t, pltpu.roll(t, HEAD_DIM, 1)).astype(BF16)


def _swa_fwd(proj, bc, bp, far, sinks, *, name):
    T = proj.shape[0]
    nb = T // BLOCK
    G = SWA_GROUP
    Hq = SWA_Q_HEADS
    lanes = 2 * HEAD_DIM

    R = SWA_STEP
    assert nb % R == 0

    def body(*refs):
        q_ref, k_refs, v_refs = refs[0], refs[1:R + 3], refs[R + 3:2 * R + 5]
        bc_ref, bp_ref, far_ref, sink_ref, o_ref = refs[2 * R + 5:]
        s = pl.program_id(0)
        lane = lax.broadcasted_iota(jnp.int32, (BLOCK, lanes), 1)
        kvs = range(SWA_KV_HEADS)
        kk = [[_swa_own_kv(ref, kv) for ref in k_refs] for kv in kvs]
        vv = [[_swa_own_kv(ref, kv) for ref in v_refs] for kv in kvs]
        chains = [(r, h) for r in range(R) for h in range(Hq)]
        tiles = lambda r: (r + 1, r, R + 1)
        q2 = {(r, pair): q_ref[r * BLOCK:(r + 1) * BLOCK, pair * lanes:(pair + 1) * lanes].astype(F32) * SCALE
              for r in range(R) for pair in range(Hq // 2)}
        qm = {c: jnp.where(lane // HEAD_DIM == c[1] % 2, q2[c[0], c[1] // 2], 0.0).astype(BF16) for c in chains}
        raw = {c: [lax.dot_general(kk[c[1] // G][w], qm[c], NT, preferred_element_type=F32) for w in tiles(c[0])]
               for c in chains}
        terms = {c: _swa_terms(raw[c], bc_ref[c[1]], bp_ref[c[1]], far_ref[c[1]], sink_ref[c[1]], R * s + c[0])
                 for c in chains}
        o_t = {c: sum(lax.dot_general(vv[c[1] // G][w], terms[c][0][b].astype(BF16), TN, preferred_element_type=F32)
                      for b, w in enumerate(tiles(c[0]))) for c in chains}
        outs = {c: (o_t[c] / terms[c][2]).T for c in chains}
        for r in range(R):
            for pair in range(Hq // 2):
                o_ref[r * BLOCK:(r + 1) * BLOCK, pair * lanes:(pair + 1) * lanes] = jnp.where(
                    lane < HEAD_DIM, outs[r, 2 * pair], outs[r, 2 * pair + 1]).astype(o_ref.dtype)

    qblk, keys, vals, bias, smem = _swa_specs()
    return pl.pallas_call(
        body, out_shape=jax.ShapeDtypeStruct((T, D_MODEL), BF16), grid=(nb // R,),
        in_specs=[qblk] + keys + vals + [bias, bias, smem, smem],
        out_specs=qblk,
        compiler_params=_params(("parallel",)), name=name,
    )(proj, *([proj] * (2 * R + 4)), bc, bp, far, sinks)


def _swa_bwd(proj, dmix, bc, bp, far, sinks, *, ex=None, name):
    T, width = proj.shape
    nb = T // BLOCK
    G = SWA_GROUP
    Hq = SWA_Q_HEADS
    lanes = 2 * HEAD_DIM
    qw = Hq * HEAD_DIM
    own_w = qw + 2 * lanes

    R = SWA_STEP
    assert nb % R == 0
    n_in = 2 * R + 10

    def body(*refs):
        q_ref, k_refs, v_refs = refs[0], refs[1:R + 3], refs[R + 3:2 * R + 5]
        do_ref, bc_ref, bp_ref, far_ref, sink_ref = refs[2 * R + 5:n_in]
        dp_ref, dbc_ref, dbp_ref, dbf_ref, dsk_ref, dk_acc, dv_acc = refs[n_in:]
        s = pl.program_id(0)

        @pl.when(s == 0)
        def _():
            for ref in (dk_acc, dv_acc, dbc_ref, dbp_ref, dbf_ref, dsk_ref):
                ref[...] = jnp.zeros(ref.shape, F32)

        lane = lax.broadcasted_iota(jnp.int32, (BLOCK, lanes), 1)
        kvs = range(SWA_KV_HEADS)
        kk = [[_swa_own_kv(ref, kv) for ref in k_refs] for kv in kvs]
        vv = [[_swa_own_kv(ref, kv) for ref in v_refs] for kv in kvs]
        chains = [(r, h) for r in range(R) for h in range(Hq)]
        blocks = range(3)
        tiles = lambda r: (r + 1, r, R + 1)
        sub = lambda ref, r, pair: ref[r * BLOCK:(r + 1) * BLOCK, pair * lanes:(pair + 1) * lanes]
        q2 = {(r, pair): sub(q_ref, r, pair).astype(F32) * SCALE for r in range(R) for pair in range(Hq // 2)}
        d2 = {(r, pair): sub(do_ref, r, pair) for r in range(R) for pair in range(Hq // 2)}
        own = [lane // HEAD_DIM == half for half in range(2)]
        qm = {c: jnp.where(own[c[1] % 2], q2[c[0], c[1] // 2], 0.0).astype(BF16) for c in chains}
        dom = {c: jnp.where(own[c[1] % 2], d2[c[0], c[1] // 2], jnp.zeros_like(d2[0, 0])) for c in chains}
        raw = {c: [lax.dot_general(kk[c[1] // G][w], qm[c], NT, preferred_element_type=F32) for w in tiles(c[0])]
               for c in chains}
        dp = {c: [lax.dot_general(vv[c[1] // G][w], dom[c], NT, preferred_element_type=F32) for w in tiles(c[0])]
              for c in chains}
        p, ds16 = {}, {}
        for c in chains:
            r, h = c
            n = R * s + r
            e, e_s, l = _swa_terms(raw[c], bc_ref[h], bp_ref[h], far_ref[h], sink_ref[h], n)
            inv = 1.0 / l
            ph = [e[b] * inv for b in blocks]
            delta = sum(jnp.sum(ph[b] * dp[c][b], axis=0, keepdims=True) for b in blocks)
            ds = [ph[b] * (dp[c][b] - delta) for b in blocks]
            dsk_ref[h] += -(e_s * inv) * delta
            dbc_ref[h] += ds[0]
            dbp_ref[h] += ds[1] + jnp.where(n == 1, ds[2], 0.0)
            dbf_ref[h] += jnp.where(n >= 2, ds[2], 0.0)
            p[c] = [x.astype(BF16) for x in ph]
            ds16[c] = [x.astype(BF16) for x in ds]
        dq_t = {c: sum(lax.dot_general(kk[c[1] // G][w], ds16[c][b], TN, preferred_element_type=F32)
                       for b, w in enumerate(tiles(c[0]))) for c in chains}
        group = [range(kv * G, (kv + 1) * G) for kv in kvs]
        dk = {(r, kv): [sum(jnp.dot(ds16[r, h][b], qm[r, h], preferred_element_type=F32) for h in group[kv])
                        for b in blocks] for r in range(R) for kv in kvs}
        dv = {(r, kv): [sum(jnp.dot(p[r, h][b], dom[r, h], preferred_element_type=F32) for h in group[kv])
                        for b in blocks] for r in range(R) for kv in kvs}
        for r in range(R):
            n = R * s + r
            rows = pl.ds(pl.multiple_of(n * BLOCK, BLOCK), BLOCK)
            prev_rows = pl.ds(pl.multiple_of(jnp.maximum(n - 1, 0) * BLOCK, BLOCK), BLOCK)
            for pair in range(Hq // 2):
                dp_ref[rows, pair * lanes:(pair + 1) * lanes] = (jnp.where(
                    lane < HEAD_DIM, dq_t[r, 2 * pair].T, dq_t[r, 2 * pair + 1].T) * SCALE).astype(dp_ref.dtype)
            for acc, ref in ((dk, dk_acc), (dv, dv_acc)):
                tot = [[a + pltpu.roll(a, HEAD_DIM, 1) for a in acc[r, kv]] for kv in kvs]
                both = [jnp.where(lane < HEAD_DIM, tot[0][b], tot[1][b]) for b in blocks]
                ref[rows, :] += both[0]
                ref[prev_rows, :] += both[1]
                ref[0:BLOCK, :] += both[2]

        @pl.when(s == nb // R - 1)
        def _():
            dp_ref[:, qw:qw + lanes] = dk_acc[...].astype(dp_ref.dtype)
            dp_ref[:, qw + lanes:own_w] = dv_acc[...].astype(dp_ref.dtype)

    qblk, keys, vals, bias, smem = _swa_specs()
    dsk = pl.BlockSpec((Hq, 1, BLOCK), lambda s: (0, 0, 0))
    grid = (nb // R,)
    body, x_in, x_in_specs, x_out, x_out_specs, x_scr = _carry(ex, grid, n_in, 5, body)
    tile = jax.ShapeDtypeStruct((Hq, BLOCK, BLOCK), F32)
    return pl.pallas_call(
        body,
        out_shape=(jax.ShapeDtypeStruct((T, width), BF16), tile, tile, tile,
                   jax.ShapeDtypeStruct((Hq, 1, BLOCK), F32), *x_out),
        grid=grid,
        in_specs=[qblk] + keys + vals + [qblk, bias, bias, smem, smem] + x_in_specs,
        out_specs=(pl.BlockSpec((T, own_w), lambda s: (0, 0)), bias, bias, bias, dsk, *x_out_specs),
        scratch_shapes=[pltpu.VMEM((T, lanes), F32), pltpu.VMEM((T, lanes), F32)] + x_scr,
        compiler_params=_params(("arbitrary",)), name=name,
    )(proj, *([proj] * (2 * R + 4)), dmix, bc, bp, far, sinks, *x_in)


def _bias_tiles(tab_t, oh_cur_t, oh_prev_t, *, name):
    Hq = tab_t.shape[0]

    def body(t_ref, oc_ref, op_ref, bc_ref, bp_ref):
        bc_ref[...] = jnp.dot(t_ref[...], oc_ref[...], precision=HIGHEST, preferred_element_type=F32)
        bp_ref[...] = jnp.dot(t_ref[...], op_ref[...], precision=HIGHEST, preferred_element_type=F32)

    vm = pl.BlockSpec(memory_space=pltpu.VMEM)
    shp = jax.ShapeDtypeStruct((Hq, BLOCK * BLOCK), F32)
    bc, bp = pl.pallas_call(body, out_shape=(shp, shp), in_specs=[vm] * 3, out_specs=(vm, vm),
                            compiler_params=_params(), name=name)(tab_t, oh_cur_t, oh_prev_t)
    return bc.reshape(Hq, BLOCK, BLOCK), bp.reshape(Hq, BLOCK, BLOCK)


def _small_grads(dbc, dbp, dbf, dsk, oh_cur, oh_prev, *, ex=None, name):
    Hq = dbc.shape[0]

    def body(dbc_ref, dbp_ref, dbf_ref, dsk_ref, oc_ref, op_ref, tab_ref, sink_ref):
        tab = (jnp.dot(dbc_ref[...], oc_ref[...], precision=HIGHEST, preferred_element_type=F32)
               + jnp.dot(dbp_ref[...], op_ref[...], precision=HIGHEST, preferred_element_type=F32))
        far = jnp.sum(dbf_ref[...], axis=1, keepdims=True)
        last = lax.broadcasted_iota(jnp.int32, (Hq, N_BUCKETS), 1) == N_BUCKETS - 1
        tab_ref[...] = tab + jnp.where(last, far, 0.0)
        sink_ref[...] = jnp.sum(dsk_ref[...], axis=1, keepdims=True)

    vm = pl.BlockSpec(memory_space=pltpu.VMEM)
    body, x_in, x_in_specs, x_out, x_out_specs, x_scr = _carry(ex, (), 6, 2, body)
    return pl.pallas_call(
        body, out_shape=(jax.ShapeDtypeStruct((Hq, N_BUCKETS), F32), jax.ShapeDtypeStruct((Hq, 1), F32), *x_out),
        in_specs=[vm] * 6 + x_in_specs, out_specs=(vm, vm, *x_out_specs), scratch_shapes=x_scr,
        compiler_params=_params(), name=name,
    )(dbc.reshape(Hq, -1), dbp.reshape(Hq, -1), dbf.reshape(Hq, -1), dsk.reshape(Hq, -1), oh_cur, oh_prev, *x_in)


def _coords():
    return lax.axis_index("x"), lax.axis_index("y"), lax.axis_index("c")


class _Exchange:
    def __init__(self, inputs, out_shapes, scratch, start, finish):
        self.inputs, self.out_shapes, self.scratch, self.start, self.finish = inputs, out_shapes, scratch, start, finish


def _carry(ex, grid, n_in, n_out, body):
    if ex is None:
        return body, [], [], [], [], []
    ni, no = len(ex.inputs), len(ex.out_shapes)

    def at_step(which):
        cond = jnp.bool_(True)
        for axis, n in enumerate(grid):
            cond = cond & (pl.program_id(axis) == (0 if which == "first" else n - 1))
        return cond

    def wrapped(*refs):
        refs = list(refs)
        n_own_scr = len(refs) - (n_in + ni + n_out + no) - len(ex.scratch)
        own_in, side_in = refs[:n_in], refs[n_in:n_in + ni]
        own_out = refs[n_in + ni:n_in + ni + n_out]
        side_out = refs[n_in + ni + n_out:n_in + ni + n_out + no]
        rest = refs[n_in + ni + n_out + no:]
        own_scr, sems = rest[:n_own_scr], rest[n_own_scr:]

        @pl.when(at_step("first"))
        def _():
            ex.start(side_in, side_out, sems)

        body(*own_in, *own_out, *own_scr)

        @pl.when(at_step("last"))
        def _():
            ex.finish(side_in, side_out, sems)

    hbm = pl.BlockSpec(memory_space=pl.ANY)
    return wrapped, list(ex.inputs), [hbm] * ni, list(ex.out_shapes), [hbm] * no, list(ex.scratch)


def _gather_exchange(shards):
    nt = len(shards)

    def copies(ins, outs, sems):
        send_sems, recv_sems, local_sems = sems
        x, y, c = _coords()
        me, sibling = (x, y, c), (x, y, 1 - c)
        chips = [(1 - x, y), (x, 1 - y), (1 - x, 1 - y)]

        def slot(t, dev):
            return outs[t].at[4 * dev[0] + 2 * dev[1] + dev[2]]

        def copy(t, k, block, to, src=None):
            dst = slot(t, block)
            return pltpu.make_async_remote_copy(
                src_ref=dst if src is None else src, dst_ref=dst,
                send_sem=send_sems.at[t, k], recv_sem=recv_sems.at[t, k], device_id=to, device_id_type=MESH)

        mine = [pltpu.make_async_copy(ins[t], slot(t, me), local_sems.at[t]) for t in range(nt)]
        first = []
        for t in range(nt):
            first.append(copy(t, 0, me, sibling, src=ins[t]))
            first += [copy(t, 1 + j, me, (*chip, c), src=ins[t]) for j, chip in enumerate(chips)]
        return copy, mine, first, me, sibling, chips, c

    def start(ins, outs, sems):
        _, mine, first, *_ = copies(ins, outs, sems)
        for cp in mine + first:
            cp.start()

    def finish(ins, outs, sems):
        copy, mine, first, me, sibling, chips, c = copies(ins, outs, sems)
        passed = []
        for j, chip in enumerate(chips):
            for t in range(nt):
                copy(t, 1 + j, (*chip, c), me).wait_recv()
                cp = copy(t, 4 + j, (*chip, c), sibling)
                cp.start()
                passed.append(cp)
        for t in range(nt):
            copy(t, 0, sibling, me).wait_recv()
            for j, chip in enumerate(chips):
                copy(t, 4 + j, (*chip, 1 - c), me).wait_recv()
        for cp in first + passed:
            cp.wait_send()
        for cp in mine:
            cp.wait()

    return _Exchange(
        list(shards), [jax.ShapeDtypeStruct((N_DEV,) + s.shape, s.dtype) for s in shards],
        [pltpu.SemaphoreType.DMA((nt, 7)), pltpu.SemaphoreType.DMA((nt, 7)), pltpu.SemaphoreType.DMA((nt,))],
        start, finish)


def _swap_exchange(arrays, n_slices, copies):
    nt = len(arrays)

    def start(ins, outs, sems):
        for cp in copies(ins, outs, sems):
            cp.start()

    def finish(ins, outs, sems):
        sends = copies(ins, outs, sems)
        for cp in sends:
            cp.wait_recv()
        for cp in sends:
            cp.wait_send()

    return _Exchange(
        list(arrays), [jax.ShapeDtypeStruct((n_slices,) + a.shape[1:], a.dtype) for a in arrays],
        [pltpu.SemaphoreType.DMA((nt, n_slices)), pltpu.SemaphoreType.DMA((nt, n_slices))], start, finish)


def _cores_exchange(gs):
    def copies(ins, outs, sems):
        send_sems, recv_sems = sems
        x, y, c = _coords()
        return [pltpu.make_async_remote_copy(
            src_ref=ins[t].at[2 * j + (1 - c)], dst_ref=outs[t].at[j],
            send_sem=send_sems.at[t, j], recv_sem=recv_sems.at[t, j], device_id=(x, y, 1 - c), device_id_type=MESH)
            for t in range(len(gs)) for j in range(4)]

    return _swap_exchange(gs, 4, copies)


def _chips_exchange(ps):
    def copies(ins, outs, sems):
        send_sems, recv_sems = sems
        x, y, c = _coords()
        peers = [(1 - x, y), (x, 1 - y), (1 - x, 1 - y)]
        return [pltpu.make_async_remote_copy(
            src_ref=ins[t].at[2 * px + py], dst_ref=outs[t].at[k],
            send_sem=send_sems.at[t, k], recv_sem=recv_sems.at[t, k], device_id=(px, py, c), device_id_type=MESH)
            for t in range(len(ps)) for k, (px, py) in enumerate(peers)]

    return _swap_exchange(ps, 3, copies)


def _add_cores(g, r, core, *, name):
    _, A, B = g.shape
    ta = _tile(A, 512, 16)

    def body(core_ref, a_ref, b_ref, o16_ref):
        o16_ref[...] = (a_ref[...] + b_ref[...]).astype(BF16)

    blk = (None, ta, B)
    return pl.pallas_call(
        body, out_shape=jax.ShapeDtypeStruct((4, A, B), BF16),
        grid_spec=pltpu.PrefetchScalarGridSpec(
            num_scalar_prefetch=1, grid=(4, A // ta),
            in_specs=[pl.BlockSpec(blk, lambda j, i, core_ref: (2 * j + core_ref[0], i, 0)),
                      pl.BlockSpec(blk, lambda j, i, core_ref: (j, i, 0))],
            out_specs=pl.BlockSpec(blk, lambda j, i, core_ref: (j, i, 0))),
        compiler_params=_params(("parallel", "parallel")), name=name)(core, g, r)


def _adamw_math(w, g, m, v):
    m = ADAM_B1 * m + (1.0 - ADAM_B1) * g
    v = ADAM_B2 * v + (1.0 - ADAM_B2) * (g * g)
    m_hat = m / (1.0 - ADAM_B1 ** ADAM_STEP)
    v_hat = v / (1.0 - ADAM_B2 ** ADAM_STEP)
    delta = -ADAM_LR * (m_hat / (jnp.sqrt(v_hat) + ADAM_EPS) + ADAM_WD * w)
    return delta, m, v


def _sum_adamw(mine, sib, r, where, w, m, v, *, ta, name):
    Aw, Bw = w.shape
    Bg = mine.shape[2]
    assert Aw % ta == 0 and Bw <= Bg and mine.shape[1] == Aw

    def body(where_ref, p_ref, s_ref, r0, r1, r2, w_ref, m_ref, v_ref, g_out, d_out, m_out, v_out):
        g = (((p_ref[:, :Bw] + s_ref[:, :Bw]) + r0[:, :Bw].astype(F32))
             + r1[:, :Bw].astype(F32)) + r2[:, :Bw].astype(F32)
        delta, m_new, v_new = _adamw_math(w_ref[...], g, m_ref[...], v_ref[...])
        g_out[...] = g
        d_out[...] = delta
        m_out[...] = m_new
        v_out[...] = v_new

    gblk = (None, ta, Bg)
    row = pl.BlockSpec((ta, Bw), lambda i, where_ref: (i, 0))
    rspecs = [pl.BlockSpec(gblk, (lambda i, where_ref, k=k: (k, i, 0))) for k in range(3)]
    shp = jax.ShapeDtypeStruct((Aw, Bw), F32)
    return pl.pallas_call(
        body, out_shape=(shp, shp, shp, shp),
        grid_spec=pltpu.PrefetchScalarGridSpec(
            num_scalar_prefetch=1, grid=(Aw // ta,),
            in_specs=[pl.BlockSpec(gblk, lambda i, where_ref: (2 * where_ref[0] + where_ref[1], i, 0)),
                      pl.BlockSpec(gblk, lambda i, where_ref: (where_ref[0], i, 0))] + rspecs + [row, row, row],
            out_specs=(row, row, row, row)),
        compiler_params=_params(("parallel",)), name=name)(where, mine, sib, r, r, r, w, m, v)


def _adamw(w, g, m, v, *, name):
    def body(w_ref, g_ref, m_ref, v_ref, d_out, m_out, v_out):
        delta, m_new, v_new = _adamw_math(w_ref[...], g_ref[...], m_ref[...], v_ref[...])
        d_out[...] = delta
        m_out[...] = m_new
        v_out[...] = v_new

    vm = pl.BlockSpec(memory_space=pltpu.VMEM)
    shp = jax.ShapeDtypeStruct(w.shape, F32)
    return pl.pallas_call(body, out_shape=(shp, shp, shp), in_specs=[vm] * 4, out_specs=(vm, vm, vm),
                          compiler_params=_params(), name=name)(w, g, m, v)


def _small_allreduce_adamw(s, w, m, v, *, name):
    R, W = s.shape

    def body(s_ref, w_ref, m_ref, v_ref, g_out, d_out, m_out, v_out, gath, send_sems, recv_sems):
        x, y, c = _coords()
        mine = 4 * x + 2 * y + c
        gath[mine] = s_ref[...]
        peers = [((1 - x) if k & 4 else x, (1 - y) if k & 2 else y, (1 - c) if k & 1 else c) for k in range(1, N_DEV)]
        sends = []
        for k in range(1, N_DEV):
            peer = peers[k - 1]
            sends.append(pltpu.make_async_remote_copy(
                src_ref=s_ref, dst_ref=gath.at[mine], send_sem=send_sems.at[k - 1], recv_sem=recv_sems.at[k - 1],
                device_id=peer, device_id_type=MESH))
        for cp in sends:
            cp.start()
        for k in range(1, N_DEV):
            peer = peers[k - 1]
            pltpu.make_async_remote_copy(
                src_ref=s_ref, dst_ref=gath.at[4 * peer[0] + 2 * peer[1] + peer[2]],
                send_sem=send_sems.at[k - 1], recv_sem=recv_sems.at[k - 1],
                device_id=peer, device_id_type=MESH).wait_recv()
        for cp in sends:
            cp.wait_send()
        g = gath[0]
        for d in range(1, N_DEV):
            g = g + gath[d]
        delta, m_new, v_new = _adamw_math(w_ref[...], g, m_ref[...], v_ref[...])
        g_out[...] = g
        d_out[...] = delta
        m_out[...] = m_new
        v_out[...] = v_new

    vm = pl.BlockSpec(memory_space=pltpu.VMEM)
    shp = jax.ShapeDtypeStruct((R, W), F32)
    return pl.pallas_call(
        body, out_shape=(shp, shp, shp, shp), in_specs=[vm] * 4, out_specs=(vm, vm, vm, vm),
        scratch_shapes=[pltpu.VMEM((N_DEV, R, W), F32), pltpu.SemaphoreType.DMA((N_DEV - 1,)),
                        pltpu.SemaphoreType.DMA((N_DEV - 1,))],
        compiler_params=_params(), name=name)(s, w, m, v)


def _pack_small(rel_bias, g1, g2, g3, g4, b_forget, sinks, extra=None, meta=None):
    misc = jnp.concatenate([rel_bias.reshape(-1), b_forget.reshape(-1), sinks.reshape(-1)])
    misc = jnp.concatenate([misc, jnp.zeros((D_MODEL - misc.shape[0],), F32)])[None]
    last = jnp.zeros((1, D_MODEL), F32) if extra is None else extra
    meta = jnp.zeros((N_META, D_MODEL), F32) if meta is None else meta
    return jnp.concatenate([g1, g2, g3, g4, misc, last, jnp.zeros((2, D_MODEL), F32), meta], axis=0)


def _unpack_small(p):
    nrb = N_BUCKETS * SWA_Q_HEADS
    misc = p[4]
    return dict(rel_bias=misc[:nrb].reshape(N_BUCKETS, SWA_Q_HEADS), ln_pre_mix=p[0:1], ln_post_mix=p[1:2],
                ln_pre_ffn=p[2:3], ln_post_ffn=p[3:4], b_forget=misc[nrb:nrb + 8].reshape(1, 8),
                sinks=misc[nrb + 8:nrb + 16].reshape(1, 8))


def _proj_runs():
    gw = FOX_GROUP * HEAD_DIM
    swa = SWA_Q_W + 2 * SWA_KV_HEADS * HEAD_DIM
    runs = [(0, swa)]
    for grp in range(FOX_HEADS // FOX_GROUP):
        runs += [(swa + part * FOX_W + grp * gw, swa + part * FOX_W + (grp + 1) * gw) for part in range(3)]
    return runs


def _columns_from_shards(gathered, runs, shard):
    pieces = []
    for start, stop in runs:
        for d in range(start // shard, (stop - 1) // shard + 1):
            lo = d * shard
            pieces.append(gathered[d][:, max(start, lo) - lo:min(stop, lo + shard) - lo])
    return jnp.concatenate(pieces, axis=1)


def _device_shards(qkv, gate, shard, padded):
    pos, segments = 0, []
    for start, stop in _proj_runs():
        segments.append((start, stop, qkv, pos))
        pos += stop - start
    segments.append((pos, pos + gate.shape[1], gate, 0))
    total = pos + gate.shape[1]
    assert total % shard == 0
    zeros = jnp.zeros((qkv.shape[0], padded - shard), qkv.dtype)
    out = []
    for d in range(total // shard):
        lo, hi = d * shard, (d + 1) * shard
        pieces = [arr[:, src + max(lo, s) - s:src + min(hi, e) - s]
                  for s, e, arr, src in sorted(segments, key=lambda seg: seg[0]) if max(lo, s) < min(hi, e)]
        out.append(jnp.concatenate(pieces + [zeros], axis=1))
    return jnp.stack(out)


def kernel(x, meta_tokens, rel_bias, ln_pre_mix, ln_post_mix, ln_pre_ffn, ln_post_ffn, w_in, b_forget, sinks, w_out, w_gate_up, w_down, loss_target, m_meta_tokens, m_rel_bias, m_ln_pre_mix, m_ln_post_mix, m_ln_pre_ffn, m_ln_post_ffn, m_w_in, m_b_forget, m_sinks, m_w_out, m_w_gate_up, m_w_down, v_meta_tokens, v_rel_bias, v_ln_pre_mix, v_ln_post_mix, v_ln_pre_ffn, v_ln_post_ffn, v_w_in, v_b_forget, v_sinks, v_w_out, v_w_gate_up, v_w_down):
    seq = x.shape[1]
    T = BLOCK + seq
    assert T % FOX_TILE == 0
    nq = T // FOX_TILE
    tm = _tile(T, 1056)
    cin = w_in.shape[2]
    hid = w_down.shape[1]
    F = N_DEV * hid
    assert w_gate_up.shape[2] == 2 * hid and cin <= W_IN_PAD and hid % 16 == 0

    x_i, y_i, c_i = _coords()
    core = jnp.reshape(c_i, (1,)).astype(jnp.int32)
    where = jnp.stack([2 * x_i + y_i, c_i]).astype(jnp.int32)
    w_in_s = jnp.pad(w_in[0].astype(BF16), ((0, 0), (0, W_IN_PAD - cin)))
    w_gu_t = w_gate_up[0].T
    h0, target, hn1, hn1_t, g_in, _ = _pad_rows_rms(x[0], loss_target[0], ln_pre_mix,
                                                    _gather_exchange([w_in_s, meta_tokens]), name="ag_w_in_rms_pre_mix")
    gather_rest = _gather_exchange([w_out[0].astype(BF16), w_gu_t.astype(BF16), w_down[0].astype(BF16)])
    w_qkv = _columns_from_shards(g_in, _proj_runs(), cin)
    w_f = jnp.pad(_columns_from_shards(g_in, [(D_QKV, D_PROJ)], cin), ((0, 0), (0, BLOCK - FOX_HEADS)))

    proj = _matmul(hn1, w_qkv, out_dtype=BF16, tm=tm, tn=D_QKV, name="mm_in_proj")
    proj_f = _matmul(hn1, w_f, out_dtype=F32, tm=tm, tn=BLOCK, name="mm_in_proj_f")

    f_t = proj_f[:, :FOX_HEADS].T
    bf_col = b_forget.reshape(FOX_HEADS, 1)

    oh_cur, oh_prev = _bucket_onehots()
    bias_c, bias_p = _bias_tiles(rel_bias.T, jnp.asarray(oh_cur.T), jnp.asarray(oh_prev.T), name="bias_tiles")
    far = rel_bias[N_BUCKETS - 1]
    sink_v = sinks[0]
    mix_a = _swa_fwd(proj, bias_c, bias_p, far, sink_v, name="swa_fwd")

    cum_col = _fox_gates_fwd(f_t, bf_col, name="fox_gates_fwd")
    q_b, k_b, v_b = _fox_prep(proj, cum_col, name="fox_prep")
    mix, lse_row, g_out, g_gu, g_down = _fox_fwd(q_b, k_b, v_b, mix_a, ex=gather_rest, name="fox_fwd")
    w_out_full = g_out.reshape(D_MODEL, D_MODEL)
    w_gu_full_t = g_gu.reshape(2 * F, D_MODEL)
    w_down_full = g_down.reshape(F, D_MODEL)

    a1 = _matmul(mix, w_out_full, out_dtype=F32, tm=tm, tn=D_MODEL, name="mm_out_proj")
    h1, hn2 = _post_res_norm(a1, ln_post_mix, h0, ln_pre_ffn, name="post_mix_pre_ffn")
    gate, up, act, act_t = _gate_up_swiglu(hn2, w_gu_full_t, name="mm_gate_up")
    ff = _matmul(act, w_down_full, out_dtype=F32, tm=tm, tn=512, name="mm_down")
    dh2, dff, dg_post_ffn, loss_acc = _loss_head(ff, ln_post_ffn, h1, target, name="loss_head")

    dgu = _d_act_swiglu(dff, w_down_full, gate, up, name="mm_d_act")
    d_w_down = _matmul(act_t, dff, out_dtype=F32, tm=_tile(F, 768), tn=512, name="mm_dw_down")
    dhn2 = _matmul(dgu, w_gu_full_t, out_dtype=F32, tm=tm, tn=512, name="mm_d_hn2")
    d_w_gu_t = _matmul(dgu, hn2, ta=True, out_dtype=F32, tm=512, tn=D_MODEL, name="mm_dw_gate_up")
    dh1, dg_pre_ffn, da1, dg_post_mix = _rms_bwd_twice(h1, ln_pre_ffn, dhn2, dh2, a1, ln_post_mix,
                                                       name="rms_bwd_pre_ffn_post_mix")
    dmix = _matmul(da1, w_out_full, nt=True, out_dtype=BF16, tm=tm, tn=D_MODEL, name="mm_d_mix")
    d_w_out = _matmul(mix, da1, ta=True, out_dtype=F32, tm=512, tn=D_MODEL, name="mm_dw_out")

    ffn_grads = [g.reshape(N_DEV, -1, D_MODEL) for g in (d_w_out, d_w_gu_t, d_w_down)]
    dproj_a, dbc, dbp, dbf, dsk, *ffn_sibling = _swa_bwd(
        proj, dmix, bias_c, bias_p, far, sink_v, ex=_cores_exchange(ffn_grads), name="swa_bwd")
    ffn_sums = [_add_cores(g, r, core, name="rs_add_" + t)
                for g, r, t in zip(ffn_grads, ffn_sibling, ["w_out", "w_gate_up", "w_down"])]

    do_b = _fox_prep_bwd(dmix, mix, name="fox_prep_bwd")
    dproj, dcq, dck, *ffn_chips = _fox_bwd(
        q_b, k_b, v_b, do_b, lse_row, dproj_a, ex=_chips_exchange(ffn_sums), name="fox_bwd")
    df_t, d_bf = _fox_gates_bwd(dcq.reshape(FOX_HEADS, T), dck.reshape(FOX_HEADS, T), f_t, bf_col,
                                name="fox_gates_bwd")
    df = jnp.pad(df_t.T.astype(BF16), ((0, 0), (0, BLOCK - FOX_HEADS)))

    d_w_qkv = _matmul(hn1_t, dproj, out_dtype=F32, tm=512, tn=768, name="mm_dw_in")
    d_w_f = _matmul(hn1_t, df, out_dtype=F32, tm=512, tn=BLOCK, name="mm_dw_in_f")
    d_w_in = _device_shards(d_w_qkv, d_w_f[:, :FOX_HEADS], cin, W_IN_PAD)
    d_tab, d_sink, in_sibling = _small_grads(dbc, dbp, dbf, dsk, jnp.asarray(oh_cur), jnp.asarray(oh_prev),
                                             ex=_cores_exchange([d_w_in]), name="small_grads")
    in_sum = _add_cores(d_w_in, in_sibling, core, name="rs_add_w_in")
    dhn1, in_chips = _matmul(dproj, w_qkv, nt=True, out_dtype=F32, tm=tm, tn=512,
                             ex=_chips_exchange([in_sum]), name="mm_d_hn1")
    dx_rows, dg_pre_mix, dh0_head = _rms_bwd_rows(h0, ln_pre_mix, dhn1, df, w_f, dh1, name="rms_bwd_pre_mix")
    grad_x = dx_rows[None]
    d_meta = dh0_head[PAD_ROWS:]

    rs_out, rs_gu, rs_down = zip(ffn_grads, ffn_sibling, ffn_chips)
    updates = [("w_in", (d_w_in, in_sibling, in_chips), (w_in[0], m_w_in[0], v_w_in[0]), 256),
               ("w_out", rs_out, (w_out[0], m_w_out[0], v_w_out[0]), BLOCK),
               ("w_gate_up", rs_gu, (w_gu_t, m_w_gate_up[0].T, v_w_gate_up[0].T), hid),
               ("w_down", rs_down, (w_down[0], m_w_down[0], v_w_down[0]), hid)]
    big = [{}, {}, {}, {}]
    for t, grads, shard, ta in updates:
        res = _sum_adamw(*grads, where, *shard, ta=ta, name="rs_adamw_" + t)
        for kind in range(4):
            big[kind][t] = (res[kind].T if t == "w_gate_up" else res[kind])[None]

    loss_row = jnp.pad(loss_acc[0:1, 0:1] * (0.5 / D_MODEL), ((0, 0), (0, D_MODEL - 1)))
    s_small = _pack_small(d_tab.T, dg_pre_mix, dg_post_mix, dg_pre_ffn, dg_post_ffn, d_bf, d_sink,
                          extra=loss_row, meta=d_meta)
    w_s = _pack_small(rel_bias, ln_pre_mix, ln_post_mix, ln_pre_ffn, ln_post_ffn, b_forget, sinks)
    m_s = _pack_small(m_rel_bias, m_ln_pre_mix, m_ln_post_mix, m_ln_pre_ffn, m_ln_post_ffn, m_b_forget, m_sinks)
    v_s = _pack_small(v_rel_bias, v_ln_pre_mix, v_ln_post_mix, v_ln_pre_ffn, v_ln_post_ffn, v_b_forget, v_sinks)
    small = _small_allreduce_adamw(s_small, w_s, m_s, v_s, name="small_allreduce_adamw")
    loss = small[0][5, 0]
    mcols = meta_tokens.shape[1]
    g_meta_mine = lax.dynamic_slice(small[0][8:8 + N_META], (0, (4 * x_i + 2 * y_i + c_i) * mcols), (N_META, mcols))
    big[0]["meta_tokens"] = g_meta_mine
    for kind, arr in enumerate(_adamw(meta_tokens, g_meta_mine, m_meta_tokens, v_meta_tokens, name="adamw_meta")):
        big[kind + 1]["meta_tokens"] = arr
    small = [_unpack_small(p) for p in small]

    names = ["meta_tokens", "rel_bias", "ln_pre_mix", "ln_post_mix", "ln_pre_ffn", "ln_post_ffn", "w_in",
             "b_forget", "sinks", "w_out", "w_gate_up", "w_down"]
    outs = [loss, grad_x]
    for kind in range(4):
        for nme in names:
            outs.append(big[kind][nme] if nme in big[kind] else small[kind][nme])
    return tuple(outs)
```

```python
import math

import numpy as np
import jax
import jax.numpy as jnp
from jax import lax
from jax.experimental import pallas as pl
from jax.experimental.pallas import tpu as pltpu

F32 = jnp.float32
BF16 = jnp.bfloat16
HIGHEST = lax.Precision.HIGHEST
MESH = pl.DeviceIdType.MESH

N_DEV = 8
D_MODEL = 1024
N_META = 16
HEAD_DIM = 64
SWA_Q_HEADS = 8
SWA_KV_HEADS = 2
SWA_GROUP = 4
FOX_HEADS = 8
FOX_W = FOX_HEADS * HEAD_DIM
SWA_Q_W = SWA_Q_HEADS * HEAD_DIM
BLOCK = 128
PAD_ROWS = BLOCK - N_META
N_BUCKETS = 32
MAX_DISTANCE = 128
D_FF = 2816
D_QKV = 2304
D_PROJ = D_QKV + FOX_HEADS
D_PROJ_PAD = 2560
EPS = 1e-6
NEG = -1e30
SCALE = HEAD_DIM ** -0.5
ADAM_LR, ADAM_B1, ADAM_B2, ADAM_EPS, ADAM_WD, ADAM_STEP = 0.001, 0.9, 0.999, 1e-08, 0.01, 10
VMEM_LIMIT = 56 * 1024 * 1024
FOX_TILE = 384
FOX_GROUP = 4
W_IN_PAD = 384

NT = (((1,), (1,)), ((), ()))
NN = (((1,), (0,)), ((), ()))
TN = (((0,), (0,)), ((), ()))


def _params(sem=None, **kw):
    if sem is not None:
        kw["dimension_semantics"] = sem
    return pltpu.CompilerParams(vmem_limit_bytes=VMEM_LIMIT, **kw)


def _tile(n, target, mult=16):
    best = None
    for t in range(mult, min(n, target) + 1, mult):
        if n % t == 0:
            best = t
    assert best is not None, (n, target)
    return best


def _matmul(a, b, *, nt=False, ta=False, out_dtype, tm, tn, tk=None, ex=None, name):
    M, K = a.shape[::-1] if ta else a.shape
    assert not (ta and nt)
    N = b.shape[0] if nt else b.shape[1]
    tk = K if tk is None else tk
    assert M % tm == 0 and N % tn == 0 and K % tk == 0, (name, a.shape, b.shape, tm, tn, tk)
    nk = K // tk
    dn = NT if nt else (TN if ta else NN)
    a_spec = pl.BlockSpec((tk, tm), lambda i, j, k: (k, i)) if ta else pl.BlockSpec((tm, tk), lambda i, j, k: (i, k))

    def body(a_ref, b_ref, o_ref, *scr):
        part = lax.dot_general(a_ref[...], b_ref[...], dn, preferred_element_type=F32)
        if nk == 1:
            o_ref[...] = part.astype(o_ref.dtype)
        else:
            acc = scr[0]
            k = pl.program_id(2)

            @pl.when(k == 0)
            def _():
                acc[...] = part

            @pl.when(k > 0)
            def _():
                acc[...] += part

            @pl.when(k == nk - 1)
            def _():
                o_ref[...] = acc[...].astype(o_ref.dtype)

    if nt:
        b_spec = pl.BlockSpec((tn, tk), lambda i, j, k: (j, k))
    else:
        b_spec = pl.BlockSpec((tk, tn), lambda i, j, k: (k, j))
    out_shape = jax.ShapeDtypeStruct((M, N), out_dtype)
    out_spec = pl.BlockSpec((tm, tn), lambda i, j, k: (i, j))
    grid = (M // tm, N // tn, nk)
    body, x_in, x_in_specs, x_out, x_out_specs, x_scr = _carry(ex, grid, 2, 1, body)
    res = pl.pallas_call(
        body,
        out_shape=(out_shape, *x_out),
        grid=grid,
        in_specs=[a_spec, b_spec] + x_in_specs,
        out_specs=(out_spec, *x_out_specs),
        scratch_shapes=([pltpu.VMEM((tm, tn), F32)] if nk > 1 else []) + x_scr,
        compiler_params=_params(("parallel", "parallel", "arbitrary") if ex is None else ("arbitrary",) * 3),
        name=name,
    )(a, b, *x_in)
    return res[0] if ex is None else res


def _rstd(x):
    return lax.rsqrt(jnp.mean(x * x, axis=-1, keepdims=True) + EPS)


def _pad_rows_rms(x, target, g, ex, *, name):
    S, D = x.shape
    nb = S // BLOCK + 1
    ni, no = len(ex.inputs), len(ex.out_shapes)
    mcols = D // N_DEV

    def body(x_ref, t_ref, g_ref, *rest):
        side_in, (h_ref, to_ref, y_ref, yt_ref) = rest[:ni], rest[ni:ni + 4]
        side_out = rest[ni + 4:ni + 4 + no]
        meta_buf, meta_sems, *sems = rest[ni + 4 + no:]
        i = pl.program_id(0)

        @pl.when(i == 0)
        def _():
            ex.start(side_in, side_out, sems)

        def norm():
            h = h_ref[...]
            y = h * _rstd(h) * g_ref[...]
            y_ref[...] = y.astype(y_ref.dtype)
            yt_ref[...] = y.T.astype(yt_ref.dtype)

        @pl.when(i < nb - 1)
        def _():
            h_ref[...] = x_ref[...]
            to_ref[...] = t_ref[...]
            norm()

        @pl.when(i == nb - 1)
        def _():
            ex.finish(side_in, side_out, sems)
            copies = [pltpu.make_async_copy(side_out[-1].at[d], meta_buf.at[:, d * mcols:(d + 1) * mcols],
                                            meta_sems.at[d]) for d in range(N_DEV)]
            for cp in copies:
                cp.start()
            for cp in copies:
                cp.wait()
            h_ref[:PAD_ROWS, :] = jnp.zeros((PAD_ROWS, D), F32)
            h_ref[PAD_ROWS:, :] = meta_buf[...]
            to_ref[...] = jnp.zeros_like(to_ref)
            norm()

    src = pl.BlockSpec((BLOCK, D), lambda i: (jnp.minimum(i, nb - 2), 0))
    dst = pl.BlockSpec((BLOCK, D), lambda i: ((i + 1) % nb, 0))
    hbm = pl.BlockSpec(memory_space=pl.ANY)
    rows = jax.ShapeDtypeStruct((BLOCK + S, D), F32)
    return pl.pallas_call(
        body,
        out_shape=(rows, rows, jax.ShapeDtypeStruct((BLOCK + S, D), BF16), jax.ShapeDtypeStruct((D, BLOCK + S), BF16),
                   *ex.out_shapes),
        grid=(nb,),
        in_specs=[src, src, pl.BlockSpec((1, D), lambda i: (0, 0))] + [hbm] * ni,
        out_specs=(dst, dst, dst, pl.BlockSpec((D, BLOCK), lambda i: (0, (i + 1) % nb)), *([hbm] * no)),
        scratch_shapes=[pltpu.VMEM((N_META, D), F32), pltpu.SemaphoreType.DMA((N_DEV,))] + list(ex.scratch),
        compiler_params=_params(("arbitrary",)), name=name)(x, target, g, *ex.inputs)


def _post_res_norm(a, g_post, h, g_pre, *, name):
    T, D = a.shape
    tm = _tile(T, 384, BLOCK)

    def body(a_ref, gp_ref, h_ref, gn_ref, h1_ref, o_ref):
        a = a_ref[...]
        h1 = h_ref[...] + a * _rstd(a) * gp_ref[...]
        h1_ref[...] = h1
        o_ref[...] = (h1 * _rstd(h1) * gn_ref[...]).astype(o_ref.dtype)

    row = pl.BlockSpec((tm, D), lambda i: (i, 0))
    vec = pl.BlockSpec((1, D), lambda i: (0, 0))
    return pl.pallas_call(
        body, out_shape=(jax.ShapeDtypeStruct((T, D), F32), jax.ShapeDtypeStruct((T, D), BF16)), grid=(T // tm,),
        in_specs=[row, vec, row, vec], out_specs=(row, row),
        compiler_params=_params(("parallel",)), name=name)(a, g_post, h, g_pre)


def _loss_head(a, g, h, target, *, name):
    T, D = a.shape
    tm = _tile(T, 512)

    def body(a_ref, g_ref, h_ref, t_ref, dy_ref, da_ref, dg_ref, loss_ref):
        i = pl.program_id(0)
        a = a_ref[...]
        r = _rstd(a)
        ah = a * r
        y = h_ref[...] + ah * g_ref[...]
        rows = i * tm + lax.broadcasted_iota(jnp.int32, (tm, 1), 0)
        err = jnp.where(rows >= BLOCK, y - t_ref[...], 0.0)
        dy = err / D
        dy_ref[...] = dy
        dah = dy * g_ref[...]
        da_ref[...] = (r * (dah - ah * jnp.mean(dah * ah, axis=-1, keepdims=True))).astype(da_ref.dtype)
        part = jnp.sum(jnp.sum(err * err, axis=1, keepdims=True), axis=0, keepdims=True)

        @pl.when(i == 0)
        def _():
            loss_ref[...] = jnp.zeros_like(loss_ref)
            dg_ref[...] = jnp.zeros_like(dg_ref)

        loss_ref[...] += jnp.broadcast_to(part, loss_ref.shape)
        dg_ref[...] += jnp.sum(dy * ah, axis=0, keepdims=True)

    row = pl.BlockSpec((tm, D), lambda i: (i, 0))
    vec = pl.BlockSpec((1, D), lambda i: (0, 0))
    return pl.pallas_call(
        body, out_shape=(jax.ShapeDtypeStruct((T, D), F32), jax.ShapeDtypeStruct((T, D), BF16),
                         jax.ShapeDtypeStruct((1, D), F32), jax.ShapeDtypeStruct((8, 128), F32)),
        grid=(T // tm,),
        in_specs=[row, vec, row, row],
        out_specs=(row, row, vec, pl.BlockSpec((8, 128), lambda i: (0, 0))),
        compiler_params=_params(("arbitrary",)), name=name)(a, g, h, target)


def _rms_pull_back(x, g, dy):
    r = _rstd(x)
    xh = x * r
    dxh = dy * g
    return r * (dxh - xh * jnp.mean(dxh * xh, axis=-1, keepdims=True)), jnp.sum(dy * xh, axis=0, keepdims=True)


def _rms_bwd_twice(x, g, dy, res, x2, g2, *, name):
    T, D = x.shape
    tm = _tile(T, 512)

    def body(x_ref, g_ref, dy_ref, res_ref, x2_ref, g2_ref, dx_ref, dg_ref, dx2_ref, dg2_ref):
        @pl.when(pl.program_id(0) == 0)
        def _():
            dg_ref[...] = jnp.zeros_like(dg_ref)
            dg2_ref[...] = jnp.zeros_like(dg2_ref)

        dx, dg = _rms_pull_back(x_ref[...], g_ref[...], dy_ref[...].astype(F32))
        dx = dx + res_ref[...]
        dx_ref[...] = dx
        dg_ref[...] += dg
        dx2, dg2 = _rms_pull_back(x2_ref[...], g2_ref[...], dx)
        dx2_ref[...] = dx2.astype(dx2_ref.dtype)
        dg2_ref[...] += dg2

    row = pl.BlockSpec((tm, D), lambda i: (i, 0))
    vec = pl.BlockSpec((1, D), lambda i: (0, 0))
    gain = jax.ShapeDtypeStruct((1, D), F32)
    return pl.pallas_call(
        body, out_shape=(jax.ShapeDtypeStruct((T, D), F32), gain, jax.ShapeDtypeStruct((T, D), BF16), gain),
        grid=(T // tm,), in_specs=[row, vec, row, row, row, vec], out_specs=(row, vec, row, vec),
        compiler_params=_params(("arbitrary",)), name=name)(x, g, dy, res, x2, g2)


def _rms_bwd_rows(x, g, dy, a, b, res, *, name):
    T, D = x.shape
    n = a.shape[1]
    n_tail = T // BLOCK - 1
    per_step = max(p for p in (4, 3, 2, 1) if n_tail % p == 0)
    steps = n_tail // per_step
    assert T == BLOCK * (1 + n_tail)
    n_rows = 4 * (per_step + 1)

    def body(*refs):
        rows, (g_ref, b_ref), (tail_ref, dg_ref, head_ref) = refs[:n_rows], refs[n_rows:n_rows + 2], refs[n_rows + 2:]

        def block(s):
            x_ref, dy_ref, a_ref, res_ref = rows[4 * s:4 * s + 4]
            dy_all = dy_ref[...] + lax.dot_general(a_ref[...], b_ref[...], NT, preferred_element_type=F32)
            dx, dg = _rms_pull_back(x_ref[...], g_ref[...], dy_all)
            return dx + res_ref[...], dg

        @pl.when(pl.program_id(0) == 0)
        def _():
            dx, dg = block(per_step)
            head_ref[...] = dx
            dg_ref[...] = dg

        for s in range(per_step):
            dx, dg = block(s)
            tail_ref[s * BLOCK:(s + 1) * BLOCK, :] = dx
            dg_ref[...] += dg

    def blocks(width):
        tail = [pl.BlockSpec((BLOCK, width), lambda i, s=s: (per_step * i + s + 1, 0)) for s in range(per_step)]
        return tail + [pl.BlockSpec((BLOCK, width), lambda i: (0, 0))]

    specs, args = [], []
    for bx, bdy, ba, bres in zip(blocks(D), blocks(D), blocks(n), blocks(D)):
        specs += [bx, bdy, ba, bres]
        args += [x, dy, a, res]
    vec = pl.BlockSpec((1, D), lambda i: (0, 0))
    return pl.pallas_call(
        body,
        out_shape=(jax.ShapeDtypeStruct((T - BLOCK, D), F32), jax.ShapeDtypeStruct((1, D), F32),
                   jax.ShapeDtypeStruct((BLOCK, D), F32)),
        grid=(steps,), in_specs=specs + [vec, pl.BlockSpec(b.shape, lambda i: (0, 0))],
        out_specs=(pl.BlockSpec((per_step * BLOCK, D), lambda i: (i, 0)), vec, pl.BlockSpec((BLOCK, D), lambda i: (0, 0))),
        compiler_params=_params(("arbitrary",)), name=name)(*args, g, b)


def _gate_up_swiglu(a, w_t, *, name):
    T, D = a.shape
    F = w_t.shape[0] // 2
    tm = _tile(T, 1408, BLOCK)
    n = _tile(F, 256, BLOCK)
    rows = 3 * BLOCK

    def body(a_ref, wg_ref, wu_ref, g_ref, u_ref, o_ref, ot_ref):
        wg, wu = wg_ref[...], wu_ref[...]
        for r in range(0, tm, rows):
            e = min(r + rows, tm)
            x = a_ref[r:e, :]
            g = lax.dot_general(x, wg, NT, preferred_element_type=F32)
            u = lax.dot_general(x, wu, NT, preferred_element_type=F32)
            g16, u16 = g.astype(BF16), u.astype(BF16)
            g_ref[r:e, :] = g16
            u_ref[r:e, :] = u16
            gr = g16.astype(F32)
            act = gr / (1.0 + jnp.exp(-gr)) * u16.astype(F32)
            o_ref[r:e, :] = act.astype(o_ref.dtype)
            ot_ref[:, r:e] = act.T.astype(ot_ref.dtype)

    tile = pl.BlockSpec((tm, n), lambda i, j: (i, j))
    shp = jax.ShapeDtypeStruct((T, F), BF16)
    return pl.pallas_call(
        body, out_shape=(shp, shp, shp, jax.ShapeDtypeStruct((F, T), BF16)), grid=(T // tm, F // n),
        in_specs=[pl.BlockSpec((tm, D), lambda i, j: (i, 0)),
                  pl.BlockSpec((n, D), lambda i, j: (j, 0)),
                  pl.BlockSpec((n, D), lambda i, j: (j + F // n, 0))],
        out_specs=(tile, tile, tile, pl.BlockSpec((n, tm), lambda i, j: (j, i))),
        compiler_params=_params(("parallel", "parallel")), name=name)(a, w_t, w_t)


def _d_act_swiglu(dff, w_down, gate, up, *, name):
    T, D = dff.shape
    F = w_down.shape[0]
    tm = _tile(T, 384)
    chunk = 768
    assert F % BLOCK == 0

    def body(d_ref, w_ref, g_ref, u_ref, o_ref):
        dy = d_ref[...]
        for c in range(0, F, chunk):
            e = min(c + chunk, F)
            d = lax.dot_general(dy, w_ref[c:e, :], NT, preferred_element_type=F32)
            g = g_ref[:, c:e].astype(F32)
            u = u_ref[:, c:e].astype(F32)
            sg = 1.0 / (1.0 + jnp.exp(-g))
            o_ref[:, c:e] = (d * u * (sg * (1.0 + g * (1.0 - sg)))).astype(o_ref.dtype)
            o_ref[:, F + c:F + e] = (d * (g * sg)).astype(o_ref.dtype)

    row = pl.BlockSpec((tm, F), lambda i: (i, 0))
    return pl.pallas_call(
        body, out_shape=jax.ShapeDtypeStruct((T, 2 * F), BF16), grid=(T // tm,),
        in_specs=[pl.BlockSpec((tm, D), lambda i: (i, 0)), pl.BlockSpec((F, D), lambda i: (0, 0)), row, row],
        out_specs=pl.BlockSpec((tm, 2 * F), lambda i: (i, 0)),
        compiler_params=_params(("parallel",)), name=name)(dff, w_down, gate, up)


def _fox_gates_fwd(f_t, b, *, name):
    H, T = f_t.shape
    nb = T // BLOCK

    def body(f_ref, b_ref, col_ref):
        f = f_ref[...] + b_ref[...]
        ls = jnp.minimum(f, 0.0) - jnp.log(1.0 + jnp.exp(-jnp.abs(f)))
        t = lax.broadcasted_iota(jnp.int32, (H, T), 1)
        ls = jnp.where(t >= PAD_ROWS, ls, 0.0)
        upper = (lax.broadcasted_iota(jnp.int32, (BLOCK, BLOCK), 0)
                 <= lax.broadcasted_iota(jnp.int32, (BLOCK, BLOCK), 1)).astype(F32)
        carry = jnp.zeros((H, 1), F32)
        for blk in range(nb):
            seg = ls[:, blk * BLOCK:(blk + 1) * BLOCK]
            pre = jnp.dot(seg, upper, precision=HIGHEST, preferred_element_type=F32) + carry
            key_gate = jnp.where(t[:, blk * BLOCK:(blk + 1) * BLOCK] >= PAD_ROWS, pre, -NEG)
            terms = list(_split3(pre)) + list(_split3(key_gate))
            col_ref[blk * BLOCK:(blk + 1) * BLOCK, :] = jnp.concatenate(
                terms + [jnp.zeros((BLOCK - len(terms) * H, BLOCK), F32)], axis=0).T.astype(col_ref.dtype)
            carry = pre[:, BLOCK - 1:BLOCK]

    vm = pl.BlockSpec(memory_space=pltpu.VMEM)
    return pl.pallas_call(
        body, out_shape=jax.ShapeDtypeStruct((T, BLOCK), BF16),
        in_specs=[vm, vm], out_specs=vm,
        compiler_params=_params(), name=name)(f_t, b)


def _fox_gates_bwd(dcq, dck, f_t, b, *, name):
    H, T = f_t.shape
    nb = T // BLOCK

    def body(dq_ref, d_ref, f_ref, b_ref, df_ref, db_ref):
        lower = (lax.broadcasted_iota(jnp.int32, (BLOCK, BLOCK), 0)
                 >= lax.broadcasted_iota(jnp.int32, (BLOCK, BLOCK), 1)).astype(F32)
        carry = jnp.zeros((H, 1), F32)
        for blk in range(nb - 1, -1, -1):
            seg = dq_ref[:, blk * BLOCK:(blk + 1) * BLOCK] - d_ref[:, blk * BLOCK:(blk + 1) * BLOCK]
            suf = jnp.dot(seg, lower, precision=HIGHEST, preferred_element_type=F32) + carry
            df_ref[:, blk * BLOCK:(blk + 1) * BLOCK] = suf
            carry = suf[:, 0:1]
        f = f_ref[...] + b_ref[...]
        t = lax.broadcasted_iota(jnp.int32, (H, T), 1)
        df = jnp.where(t >= PAD_ROWS, df_ref[...] / (1.0 + jnp.exp(f)), 0.0)
        df_ref[...] = df
        db_ref[...] = jnp.sum(df, axis=1, keepdims=True)

    vm = pl.BlockSpec(memory_space=pltpu.VMEM)
    return pl.pallas_call(
        body, out_shape=(jax.ShapeDtypeStruct((H, T), F32), jax.ShapeDtypeStruct((H, 1), F32)),
        in_specs=[vm, vm, vm, vm], out_specs=(vm, vm),
        compiler_params=_params(), name=name)(dcq, dck, f_t, b)


def _fox_lanes(parity):
    base = HEAD_DIM * (1 - parity)
    return base, base + 3


def _split3(c):
    hi = c.astype(BF16).astype(F32)
    r = c - hi
    mid = r.astype(BF16).astype(F32)
    lo = (r - mid).astype(BF16).astype(F32)
    return hi, mid, lo


def _lanes(lane, parity, data, start, terms, ones_at=None, fill=1.0):
    out = jnp.zeros((), F32) if ones_at is None else jnp.where((lane >= ones_at) & (lane < ones_at + 3), fill, 0.0)
    for i, t in enumerate(terms):
        out = jnp.where(lane == start + i, t, out)
    return jnp.where(lane // HEAD_DIM == parity, data, out)


def _fox_prep(proj, cum_col, *, name):
    T = proj.shape[0]
    tm = _tile(T, 1408, BLOCK)
    nt = T // tm
    H = FOX_HEADS
    lanes = 2 * HEAD_DIM
    first = (proj.shape[1] - 3 * H * HEAD_DIM) // lanes

    def body(q_ref, k_ref, v_ref, c_ref, qa_ref, ka_ref, va_ref):
        p = pl.program_id(0)
        i = pl.program_id(1)
        lane = lax.broadcasted_iota(jnp.int32, (1, lanes), 1)
        src = lax.broadcasted_iota(jnp.int32, (lanes, lanes), 0)
        dst = lax.broadcasted_iota(jnp.int32, (lanes, lanes), 1)
        q2 = q_ref[...].astype(F32) * SCALE
        k2 = k_ref[...].astype(F32)
        v2 = v_ref[...].astype(F32)
        gates = c_ref[...]
        def placed(h, first_term, start):
            pick = ((src % FOX_HEADS == h) & (src // FOX_HEADS - first_term == dst - start)
                    & (dst >= start) & (dst < start + 3))
            return jnp.dot(gates, pick.astype(BF16), preferred_element_type=F32)

        moved = [(placed(2 * p + e, 0, _fox_lanes(e)[1]), placed(2 * p + e, 3, _fox_lanes(e)[0])) for e in range(2)]
        for e in range(2):
            kc, qc = _fox_lanes(e)
            own = lane // HEAD_DIM == e
            minus = jnp.where((lane >= kc) & (lane < kc + 3), -1.0, 0.0)
            ones_q = jnp.where((lane >= qc) & (lane < qc + 3), 1.0, 0.0)
            ones_k = jnp.where((lane >= kc) & (lane < kc + 3), 1.0, 0.0)
            qa_ref[e] = jnp.where(own, q2, moved[e][0] + minus).astype(BF16)
            ka_ref[e] = jnp.where(own, k2, moved[e][1] + ones_q).astype(BF16)
            va_ref[e] = jnp.where(own, v2, ones_k).astype(BF16)

    pairs = FOX_GROUP // 2

    def col(part):
        return pl.BlockSpec((tm, lanes),
                            lambda p, i: (i, first + 3 * pairs * (p // pairs) + part * pairs + p % pairs))

    out = pl.BlockSpec((2, tm, lanes), lambda p, i: (p, i, 0))
    shp = jax.ShapeDtypeStruct((H, T, lanes), BF16)
    return pl.pallas_call(
        body, out_shape=(shp, shp, shp), grid=(H // 2, nt),
        in_specs=[col(0), col(1), col(2), pl.BlockSpec((tm, lanes), lambda p, i: (i, 0))],
        out_specs=(out, out, out),
        compiler_params=_params(("parallel", "parallel")), name=name)(proj, proj, proj, cum_col)


def _fox_fwd(q_aug, k_aug, v_aug, mix, *, ex=None, name):
    H, T, lanes = q_aug.shape
    tq = FOX_TILE
    nq = T // tq
    G = FOX_HEADS

    def body(q_ref, k_ref, v_ref, mix_ref, o_ref, lse_ref, m_scr, acc_scr):
        i = pl.program_id(1)
        m_scr[...] = jnp.full(m_scr.shape, NEG, F32)
        acc_scr[...] = jnp.zeros(acc_scr.shape, F32)

        def step(kb, diag):
            off = pl.multiple_of(kb * tq, tq)
            s_t = [lax.dot_general(k_ref[g, pl.ds(off, tq), :], q_ref[g], NT, preferred_element_type=F32)
                   for g in range(G)]
            if diag:
                r = lax.broadcasted_iota(jnp.int32, (tq, tq), 0)
                c = lax.broadcasted_iota(jnp.int32, (tq, tq), 1)
                s_t = [jnp.where(c >= r, s, NEG) for s in s_t]
            m_prev = [m_scr[g] for g in range(G)]
            m_new = [jnp.maximum(m_prev[g], jnp.max(s_t[g], axis=0, keepdims=True)) for g in range(G)]
            p_t = [jnp.exp(s_t[g] - m_new[g]).astype(BF16) for g in range(G)]
            pv = [lax.dot_general(v_ref[g, pl.ds(off, tq), :], p_t[g], TN, preferred_element_type=F32)
                  for g in range(G)]
            for g in range(G):
                acc_scr[g] = jnp.exp(m_prev[g] - m_new[g]) * acc_scr[g] + pv[g]
                m_scr[g] = m_new[g]

        def loop_body(kb, carry):
            step(kb, False)
            return carry

        lax.fori_loop(0, i, loop_body, 0)
        step(i, True)
        lane = lax.broadcasted_iota(jnp.int32, (tq, lanes), 1)
        outs = []
        for g in range(G):
            ones = _fox_lanes(g % 2)[0]
            acc = acc_scr[g]
            lse_ref[g] = m_scr[g] + jnp.log(acc[ones:ones + 1, :])
            acc_t = acc.T
            outs.append(acc_t / acc_t[:, ones:ones + 1])
        for pair in range(G // 2):
            o_ref[:, pair * lanes:(pair + 1) * lanes] = jnp.where(
                lane < HEAD_DIM, outs[2 * pair], outs[2 * pair + 1]).astype(o_ref.dtype)

    blk = pl.BlockSpec((G, tq, lanes), lambda h, i: (h, i, 0))
    full = pl.BlockSpec((G, T, lanes), lambda h, i: (h, 0, 0))
    grid = (H // G, nq)
    first = mix.shape[1] // (G * HEAD_DIM) - H // G
    body, x_in, x_in_specs, x_out, x_out_specs, x_scr = _carry(ex, grid, 4, 2, body)
    return pl.pallas_call(
        body,
        out_shape=(jax.ShapeDtypeStruct(mix.shape, mix.dtype), jax.ShapeDtypeStruct((H, nq, 1, tq), F32), *x_out),
        grid=grid,
        in_specs=[blk, full, full, pl.BlockSpec(memory_space=pl.ANY)] + x_in_specs,
        out_specs=(pl.BlockSpec((tq, G * HEAD_DIM), lambda h, i: (i, first + h)),
                   pl.BlockSpec((G, None, 1, tq), lambda h, i: (h, i, 0, 0)), *x_out_specs),
        input_output_aliases={3: 0},
        scratch_shapes=[pltpu.VMEM((G, 1, tq), F32), pltpu.VMEM((G, lanes, tq), F32)] + x_scr,
        compiler_params=_params(("arbitrary", "arbitrary")), name=name)(q_aug, k_aug, v_aug, mix, *x_in)


def _fox_prep_bwd(dmix, mix, *, name):
    T = dmix.shape[0]
    H = FOX_HEADS
    tm = _tile(T, 1408, BLOCK)
    lanes = 2 * HEAD_DIM
    first = mix.shape[1] // lanes - H // 2

    def body(d_ref, o_ref, da_ref):
        lane = lax.broadcasted_iota(jnp.int32, (1, lanes), 1)
        d2 = d_ref[...].astype(F32)
        prod = d2 * o_ref[...].astype(F32)
        for e in range(2):
            delta = jnp.sum(jnp.where(lane // HEAD_DIM == e, prod, 0.0), axis=1, keepdims=True)
            da_ref[e] = _lanes(lane, e, d2, _fox_lanes(e)[0], _split3(-delta)).astype(BF16)

    pair = pl.BlockSpec((tm, lanes), lambda p, i: (i, first + p))
    return pl.pallas_call(
        body, out_shape=jax.ShapeDtypeStruct((H, T, lanes), BF16), grid=(H // 2, T // tm),
        in_specs=[pair, pair],
        out_specs=pl.BlockSpec((2, tm, lanes), lambda p, i: (p, i, 0)),
        compiler_params=_params(("parallel", "parallel")), name=name)(dmix, mix)


def _fox_bwd(q_aug, k_aug, v_aug, do_aug, lse_row, dproj, *, ex=None, name):
    H, T, lanes = q_aug.shape
    tq = FOX_TILE
    nq = T // tq
    G = FOX_GROUP

    def side_by_side(tiles, scale=None):
        lane = lax.broadcasted_iota(jnp.int32, tiles[0].shape, 1)
        out = [jnp.where(lane < HEAD_DIM, tiles[2 * p], tiles[2 * p + 1]) for p in range(G // 2)]
        out = jnp.concatenate(out, axis=1)
        return out if scale is None else out * scale

    def body(q_ref, k_ref, v_ref, do_ref, lse_ref, dproj_in, out_ref, dcq_ref, dck_ref, dk_acc, dv_acc, dq_ref):
        j = pl.program_id(1)

        @pl.when(j == 0)
        def _():
            dq_ref[...] = jnp.zeros(dq_ref.shape, F32)
            dcq_ref[...] = jnp.zeros(dcq_ref.shape, F32)

        dk_acc[...] = jnp.zeros(dk_acc.shape, F32)
        dv_acc[...] = jnp.zeros(dv_acc.shape, F32)

        def step(qb, diag):
            off = pl.multiple_of(qb * tq, tq)
            heads = range(G)
            qa = [q_ref[g, pl.ds(off, tq), :] for g in heads]
            da = [do_ref[g, pl.ds(off, tq), :] for g in heads]
            s_t = [lax.dot_general(k_ref[g], qa[g], NT, preferred_element_type=F32) for g in heads]
            dp_t = [lax.dot_general(v_ref[g], da[g], NT, preferred_element_type=F32) for g in heads]
            p_t = [jnp.exp(s_t[g] - lse_ref[g, qb]) for g in heads]
            if diag:
                r = lax.broadcasted_iota(jnp.int32, (tq, tq), 0)
                c = lax.broadcasted_iota(jnp.int32, (tq, tq), 1)
                p_t = [jnp.where(c >= r, p, 0.0) for p in p_t]
            dsb = [(p_t[g] * dp_t[g]).astype(BF16) for g in heads]
            dv = [jnp.dot(p_t[g].astype(BF16), da[g], preferred_element_type=F32) for g in heads]
            dk = [jnp.dot(dsb[g], qa[g], preferred_element_type=F32) for g in heads]
            dq = [lax.dot_general(k_ref[g], dsb[g], TN, preferred_element_type=F32) for g in heads]
            for g in heads:
                dv_acc[g] += dv[g]
                dk_acc[g] += dk[g]
                dq_ref[g, qb] += dq[g]
                dcq_ref[g, qb] += jnp.sum(dsb[g].astype(F32), axis=0, keepdims=True)

        step(j, True)

        def loop_body(qb, carry):
            step(qb, False)
            return carry

        lax.fori_loop(j + 1, nq, loop_body, 0)
        dk = [dk_acc[g] for g in range(G)]
        out_ref[:, 0:wide] = side_by_side([dq_ref[g, j].T for g in range(G)], SCALE).astype(out_ref.dtype)
        out_ref[:, wide:2 * wide] = side_by_side(dk).astype(out_ref.dtype)
        out_ref[:, 2 * wide:3 * wide] = side_by_side([dv_acc[g] for g in range(G)]).astype(out_ref.dtype)
        for g in range(G):
            kc = _fox_lanes(g % 2)[0]
            dck_ref[g] = -dk[g].T[kc:kc + 1, :]

    blk = pl.BlockSpec((G, tq, lanes), lambda h, j: (h, j, 0))
    full = pl.BlockSpec((G, T, lanes), lambda h, j: (h, 0, 0))
    wide = G * HEAD_DIM
    first = dproj.shape[1] // (3 * wide) - H // G
    grid = (H // G, nq)
    body, x_in, x_in_specs, x_out, x_out_specs, x_scr = _carry(ex, grid, 6, 3, body)
    rows = jax.ShapeDtypeStruct((H, nq, 1, tq), F32)
    all_rows = pl.BlockSpec((G, nq, 1, tq), lambda h, j: (h, 0, 0, 0))
    return pl.pallas_call(
        body,
        out_shape=(jax.ShapeDtypeStruct(dproj.shape, dproj.dtype), rows, rows, *x_out),
        grid=grid,
        in_specs=[full, blk, blk, full, all_rows, pl.BlockSpec(memory_space=pl.ANY)] + x_in_specs,
        out_specs=(pl.BlockSpec((tq, 3 * wide), lambda h, j: (j, first + h)), all_rows,
                   pl.BlockSpec((G, None, 1, tq), lambda h, j: (h, j, 0, 0)), *x_out_specs),
        input_output_aliases={5: 0},
        scratch_shapes=[pltpu.VMEM((G, tq, lanes), F32), pltpu.VMEM((G, tq, lanes), F32),
                        pltpu.VMEM((G, nq, lanes, tq), F32)] + x_scr,
        compiler_params=_params(("arbitrary", "arbitrary")), name=name,
    )(q_aug, k_aug, v_aug, do_aug, lse_row, dproj, *x_in)


def _t5_bucket_np(d):
    n = np.maximum(d, 0).astype(np.int32)
    max_exact = N_BUCKETS // 2
    nf = np.maximum(n, 1).astype(np.float32)
    large = max_exact + (np.log(nf / max_exact) / math.log(MAX_DISTANCE / max_exact)
                         * (N_BUCKETS - max_exact)).astype(np.int32)
    large = np.minimum(large, N_BUCKETS - 1)
    return np.where(n < max_exact, n, large)


def _bucket_onehots():
    k = np.arange(BLOCK)[:, None]
    q = np.arange(BLOCK)[None, :]
    eye = np.eye(N_BUCKETS, dtype=np.float32)
    cur = eye[_t5_bucket_np(q - k).reshape(-1)]
    prev = eye[_t5_bucket_np(BLOCK + q - k).reshape(-1)]
    return cur, prev


SWA_K_COL = SWA_Q_HEADS * HEAD_DIM // (2 * HEAD_DIM)
SWA_V_COL = SWA_K_COL + 1


def _swa_terms(raw, bc, bp, far, sink, n):
    k = lax.broadcasted_iota(jnp.int32, (BLOCK, BLOCK), 0)
    q = lax.broadcasted_iota(jnp.int32, (BLOCK, BLOCK), 1)
    never = 2 * BLOCK
    s_c = raw[0] + bc
    s_p = raw[1] + bp
    s_m = raw[2] + jnp.where(n == 1, bp, far)
    s_c = jnp.where((k <= q) & (k >= jnp.where(n >= 1, 0, PAD_ROWS)), s_c, NEG)
    s_p = jnp.where(k > q + jnp.where(n >= 2, 0, never), s_p, NEG)
    s_m = jnp.where(k >= jnp.where(n >= 1, PAD_ROWS, never), s_m, NEG)
    m = jnp.maximum(jnp.maximum(jnp.max(s_c, axis=0, keepdims=True), jnp.max(s_p, axis=0, keepdims=True)),
                    jnp.maximum(jnp.max(s_m, axis=0, keepdims=True), sink))
    e = [jnp.exp(s_c - m), jnp.exp(s_p - m), jnp.exp(s_m - m)]
    e_s = jnp.exp(sink - m)
    l = (jnp.sum(e[0], axis=0, keepdims=True) + jnp.sum(e[1], axis=0, keepdims=True)
         + jnp.sum(e[2], axis=0, keepdims=True) + e_s)
    return e, e_s, l


SWA_STEP = 3


def _swa_specs():
    R = SWA_STEP

    def window(col):
        return ([pl.BlockSpec((BLOCK, BLOCK), lambda s, w=w: (jnp.maximum(R * s - 1 + w, 0), col)) for w in range(R + 1)]
                + [pl.BlockSpec((BLOCK, BLOCK), lambda s: (0, col))])

    qblk = pl.BlockSpec((R * BLOCK, SWA_Q_HEADS * HEAD_DIM), lambda s: (s, 0))
    bias = pl.BlockSpec((SWA_Q_HEADS, BLOCK, BLOCK), lambda s: (0, 0, 0))
    smem = pl.BlockSpec(memory_space=pltpu.SMEM)
    return qblk, window(SWA_K_COL), window(SWA_V_COL), bias, smem


def _swa_own_kv(tile_ref, kv):
    lane = lax.broadcasted_iota(jnp.int32, (BLOCK, 2 * HEAD_DIM), 1)
    t = tile_ref[...].astype(F32)
    return jnp.where(lane // HEAD_DIM == kv, t, pltpu.roll(t, HEAD_DIM, 1)).astype(BF16)


def _swa_fwd(proj, bc, bp, far, sinks, *, name):
    T = proj.shape[0]
    nb = T // BLOCK
    G = SWA_GROUP
    Hq = SWA_Q_HEADS
    lanes = 2 * HEAD_DIM

    R = SWA_STEP
    assert nb % R == 0

    def body(*refs):
        q_ref, k_refs, v_refs = refs[0], refs[1:R + 3], refs[R + 3:2 * R + 5]
        bc_ref, bp_ref, far_ref, sink_ref, o_ref = refs[2 * R + 5:]
        s = pl.program_id(0)
        lane = lax.broadcasted_iota(jnp.int32, (BLOCK, lanes), 1)
        kvs = range(SWA_KV_HEADS)
        kk = [[_swa_own_kv(ref, kv) for ref in k_refs] for kv in kvs]
        vv = [[_swa_own_kv(ref, kv) for ref in v_refs] for kv in kvs]
        chains = [(r, h) for r in range(R) for h in range(Hq)]
        tiles = lambda r: (r + 1, r, R + 1)
        q2 = {(r, pair): q_ref[r * BLOCK:(r + 1) * BLOCK, pair * lanes:(pair + 1) * lanes].astype(F32) * SCALE
              for r in range(R) for pair in range(Hq // 2)}
        qm = {c: jnp.where(lane // HEAD_DIM == c[1] % 2, q2[c[0], c[1] // 2], 0.0).astype(BF16) for c in chains}
        raw = {c: [lax.dot_general(kk[c[1] // G][w], qm[c], NT, preferred_element_type=F32) for w in tiles(c[0])]
               for c in chains}
        terms = {c: _swa_terms(raw[c], bc_ref[c[1]], bp_ref[c[1]], far_ref[c[1]], sink_ref[c[1]], R * s + c[0])
                 for c in chains}
        o_t = {c: sum(lax.dot_general(vv[c[1] // G][w], terms[c][0][b].astype(BF16), TN, preferred_element_type=F32)
                      for b, w in enumerate(tiles(c[0]))) for c in chains}
        outs = {c: (o_t[c] / terms[c][2]).T for c in chains}
        for r in range(R):
            for pair in range(Hq // 2):
                o_ref[r * BLOCK:(r + 1) * BLOCK, pair * lanes:(pair + 1) * lanes] = jnp.where(
                    lane < HEAD_DIM, outs[r, 2 * pair], outs[r, 2 * pair + 1]).astype(o_ref.dtype)

    qblk, keys, vals, bias, smem = _swa_specs()
    return pl.pallas_call(
        body, out_shape=jax.ShapeDtypeStruct((T, D_MODEL), BF16), grid=(nb // R,),
        in_specs=[qblk] + keys + vals + [bias, bias, smem, smem],
        out_specs=qblk,
        compiler_params=_params(("parallel",)), name=name,
    )(proj, *([proj] * (2 * R + 4)), bc, bp, far, sinks)


def _swa_bwd(proj, dmix, bc, bp, far, sinks, *, ex=None, name):
    T, width = proj.shape
    nb = T // BLOCK
    G = SWA_GROUP
    Hq = SWA_Q_HEADS
    lanes = 2 * HEAD_DIM
    qw = Hq * HEAD_DIM
    own_w = qw + 2 * lanes

    R = SWA_STEP
    assert nb % R == 0
    n_in = 2 * R + 10

    def body(*refs):
        q_ref, k_refs, v_refs = refs[0], refs[1:R + 3], refs[R + 3:2 * R + 5]
        do_ref, bc_ref, bp_ref, far_ref, sink_ref = refs[2 * R + 5:n_in]
        dp_ref, dbc_ref, dbp_ref, dbf_ref, dsk_ref, dk_acc, dv_acc = refs[n_in:]
        s = pl.program_id(0)

        @pl.when(s == 0)
        def _():
            for ref in (dk_acc, dv_acc, dbc_ref, dbp_ref, dbf_ref, dsk_ref):
                ref[...] = jnp.zeros(ref.shape, F32)

        lane = lax.broadcasted_iota(jnp.int32, (BLOCK, lanes), 1)
        kvs = range(SWA_KV_HEADS)
        kk = [[_swa_own_kv(ref, kv) for ref in k_refs] for kv in kvs]
        vv = [[_swa_own_kv(ref, kv) for ref in v_refs] for kv in kvs]
        chains = [(r, h) for r in range(R) for h in range(Hq)]
        blocks = range(3)
        tiles = lambda r: (r + 1, r, R + 1)
        sub = lambda ref, r, pair: ref[r * BLOCK:(r + 1) * BLOCK, pair * lanes:(pair + 1) * lanes]
        q2 = {(r, pair): sub(q_ref, r, pair).astype(F32) * SCALE for r in range(R) for pair in range(Hq // 2)}
        d2 = {(r, pair): sub(do_ref, r, pair) for r in range(R) for pair in range(Hq // 2)}
        own = [lane // HEAD_DIM == half for half in range(2)]
        qm = {c: jnp.where(own[c[1] % 2], q2[c[0], c[1] // 2], 0.0).astype(BF16) for c in chains}
        dom = {c: jnp.where(own[c[1] % 2], d2[c[0], c[1] // 2], jnp.zeros_like(d2[0, 0])) for c in chains}
        raw = {c: [lax.dot_general(kk[c[1] // G][w], qm[c], NT, preferred_element_type=F32) for w in tiles(c[0])]
               for c in chains}
        dp = {c: [lax.dot_general(vv[c[1] // G][w], dom[c], NT, preferred_element_type=F32) for w in tiles(c[0])]
              for c in chains}
        p, ds16 = {}, {}
        for c in chains:
            r, h = c
            n = R * s + r
            e, e_s, l = _swa_terms(raw[c], bc_ref[h], bp_ref[h], far_ref[h], sink_ref[h], n)
            inv = 1.0 / l
            ph = [e[b] * inv for b in blocks]
            delta = sum(jnp.sum(ph[b] * dp[c][b], axis=0, keepdims=True) for b in blocks)
            ds = [ph[b] * (dp[c][b] - delta) for b in blocks]
            dsk_ref[h] += -(e_s * inv) * delta
            dbc_ref[h] += ds[0]
            dbp_ref[h] += ds[1] + jnp.where(n == 1, ds[2], 0.0)
            dbf_ref[h] += jnp.where(n >= 2, ds[2], 0.0)
            p[c] = [x.astype(BF16) for x in ph]
            ds16[c] = [x.astype(BF16) for x in ds]
        dq_t = {c: sum(lax.dot_general(kk[c[1] // G][w], ds16[c][b], TN, preferred_element_type=F32)
                       for b, w in enumerate(tiles(c[0]))) for c in chains}
        group = [range(kv * G, (kv + 1) * G) for kv in kvs]
        dk = {(r, kv): [sum(jnp.dot(ds16[r, h][b], qm[r, h], preferred_element_type=F32) for h in group[kv])
                        for b in blocks] for r in range(R) for kv in kvs}
        dv = {(r, kv): [sum(jnp.dot(p[r, h][b], dom[r, h], preferred_element_type=F32) for h in group[kv])
                        for b in blocks] for r in range(R) for kv in kvs}
        for r in range(R):
            n = R * s + r
            rows = pl.ds(pl.multiple_of(n * BLOCK, BLOCK), BLOCK)
            prev_rows = pl.ds(pl.multiple_of(jnp.maximum(n - 1, 0) * BLOCK, BLOCK), BLOCK)
            for pair in range(Hq // 2):
                dp_ref[rows, pair * lanes:(pair + 1) * lanes] = (jnp.where(
                    lane < HEAD_DIM, dq_t[r, 2 * pair].T, dq_t[r, 2 * pair + 1].T) * SCALE).astype(dp_ref.dtype)
            for acc, ref in ((dk, dk_acc), (dv, dv_acc)):
                tot = [[a + pltpu.roll(a, HEAD_DIM, 1) for a in acc[r, kv]] for kv in kvs]
                both = [jnp.where(lane < HEAD_DIM, tot[0][b], tot[1][b]) for b in blocks]
                ref[rows, :] += both[0]
                ref[prev_rows, :] += both[1]
                ref[0:BLOCK, :] += both[2]

        @pl.when(s == nb // R - 1)
        def _():
            dp_ref[:, qw:qw + lanes] = dk_acc[...].astype(dp_ref.dtype)
            dp_ref[:, qw + lanes:own_w] = dv_acc[...].astype(dp_ref.dtype)

    qblk, keys, vals, bias, smem = _swa_specs()
    dsk = pl.BlockSpec((Hq, 1, BLOCK), lambda s: (0, 0, 0))
    grid = (nb // R,)
    body, x_in, x_in_specs, x_out, x_out_specs, x_scr = _carry(ex, grid, n_in, 5, body)
    tile = jax.ShapeDtypeStruct((Hq, BLOCK, BLOCK), F32)
    return pl.pallas_call(
        body,
        out_shape=(jax.ShapeDtypeStruct((T, width), BF16), tile, tile, tile,
                   jax.ShapeDtypeStruct((Hq, 1, BLOCK), F32), *x_out),
        grid=grid,
        in_specs=[qblk] + keys + vals + [qblk, bias, bias, smem, smem] + x_in_specs,
        out_specs=(pl.BlockSpec((T, own_w), lambda s: (0, 0)), bias, bias, bias, dsk, *x_out_specs),
        scratch_shapes=[pltpu.VMEM((T, lanes), F32), pltpu.VMEM((T, lanes), F32)] + x_scr,
        compiler_params=_params(("arbitrary",)), name=name,
    )(proj, *([proj] * (2 * R + 4)), dmix, bc, bp, far, sinks, *x_in)


def _bias_tiles(tab_t, oh_cur_t, oh_prev_t, *, name):
    Hq = tab_t.shape[0]

    def body(t_ref, oc_ref, op_ref, bc_ref, bp_ref):
        bc_ref[...] = jnp.dot(t_ref[...], oc_ref[...], precision=HIGHEST, preferred_element_type=F32)
        bp_ref[...] = jnp.dot(t_ref[...], op_ref[...], precision=HIGHEST, preferred_element_type=F32)

    vm = pl.BlockSpec(memory_space=pltpu.VMEM)
    shp = jax.ShapeDtypeStruct((Hq, BLOCK * BLOCK), F32)
    bc, bp = pl.pallas_call(body, out_shape=(shp, shp), in_specs=[vm] * 3, out_specs=(vm, vm),
                            compiler_params=_params(), name=name)(tab_t, oh_cur_t, oh_prev_t)
    return bc.reshape(Hq, BLOCK, BLOCK), bp.reshape(Hq, BLOCK, BLOCK)


def _small_grads(dbc, dbp, dbf, dsk, oh_cur, oh_prev, *, ex=None, name):
    Hq = dbc.shape[0]

    def body(dbc_ref, dbp_ref, dbf_ref, dsk_ref, oc_ref, op_ref, tab_ref, sink_ref):
        tab = (jnp.dot(dbc_ref[...], oc_ref[...], precision=HIGHEST, preferred_element_type=F32)
               + jnp.dot(dbp_ref[...], op_ref[...], precision=HIGHEST, preferred_element_type=F32))
        far = jnp.sum(dbf_ref[...], axis=1, keepdims=True)
        last = lax.broadcasted_iota(jnp.int32, (Hq, N_BUCKETS), 1) == N_BUCKETS - 1
        tab_ref[...] = tab + jnp.where(last, far, 0.0)
        sink_ref[...] = jnp.sum(dsk_ref[...], axis=1, keepdims=True)

    vm = pl.BlockSpec(memory_space=pltpu.VMEM)
    body, x_in, x_in_specs, x_out, x_out_specs, x_scr = _carry(ex, (), 6, 2, body)
    return pl.pallas_call(
        body, out_shape=(jax.ShapeDtypeStruct((Hq, N_BUCKETS), F32), jax.ShapeDtypeStruct((Hq, 1), F32), *x_out),
        in_specs=[vm] * 6 + x_in_specs, out_specs=(vm, vm, *x_out_specs), scratch_shapes=x_scr,
        compiler_params=_params(), name=name,
    )(dbc.reshape(Hq, -1), dbp.reshape(Hq, -1), dbf.reshape(Hq, -1), dsk.reshape(Hq, -1), oh_cur, oh_prev, *x_in)


def _coords():
    return lax.axis_index("x"), lax.axis_index("y"), lax.axis_index("c")


class _Exchange:
    def __init__(self, inputs, out_shapes, scratch, start, finish):
        self.inputs, self.out_shapes, self.scratch, self.start, self.finish = inputs, out_shapes, scratch, start, finish


def _carry(ex, grid, n_in, n_out, body):
    if ex is None:
        return body, [], [], [], [], []
    ni, no = len(ex.inputs), len(ex.out_shapes)

    def at_step(which):
        cond = jnp.bool_(True)
        for axis, n in enumerate(grid):
            cond = cond & (pl.program_id(axis) == (0 if which == "first" else n - 1))
        return cond

    def wrapped(*refs):
        refs = list(refs)
        n_own_scr = len(refs) - (n_in + ni + n_out + no) - len(ex.scratch)
        own_in, side_in = refs[:n_in], refs[n_in:n_in + ni]
        own_out = refs[n_in + ni:n_in + ni + n_out]
        side_out = refs[n_in + ni + n_out:n_in + ni + n_out + no]
        rest = refs[n_in + ni + n_out + no:]
        own_scr, sems = rest[:n_own_scr], rest[n_own_scr:]

        @pl.when(at_step("first"))
        def _():
            ex.start(side_in, side_out, sems)

        body(*own_in, *own_out, *own_scr)

        @pl.when(at_step("last"))
        def _():
            ex.finish(side_in, side_out, sems)

    hbm = pl.BlockSpec(memory_space=pl.ANY)
    return wrapped, list(ex.inputs), [hbm] * ni, list(ex.out_shapes), [hbm] * no, list(ex.scratch)


def _gather_exchange(shards):
    nt = len(shards)

    def copies(ins, outs, sems):
        send_sems, recv_sems, local_sems = sems
        x, y, c = _coords()
        me, sibling = (x, y, c), (x, y, 1 - c)
        chips = [(1 - x, y), (x, 1 - y), (1 - x, 1 - y)]

        def slot(t, dev):
            return outs[t].at[4 * dev[0] + 2 * dev[1] + dev[2]]

        def copy(t, k, block, to, src=None):
            dst = slot(t, block)
            return pltpu.make_async_remote_copy(
                src_ref=dst if src is None else src, dst_ref=dst,
                send_sem=send_sems.at[t, k], recv_sem=recv_sems.at[t, k], device_id=to, device_id_type=MESH)

        mine = [pltpu.make_async_copy(ins[t], slot(t, me), local_sems.at[t]) for t in range(nt)]
        first = []
        for t in range(nt):
            first.append(copy(t, 0, me, sibling, src=ins[t]))
            first += [copy(t, 1 + j, me, (*chip, c), src=ins[t]) for j, chip in enumerate(chips)]
        return copy, mine, first, me, sibling, chips, c

    def start(ins, outs, sems):
        _, mine, first, *_ = copies(ins, outs, sems)
        for cp in mine + first:
            cp.start()

    def finish(ins, outs, sems):
        copy, mine, first, me, sibling, chips, c = copies(ins, outs, sems)
        passed = []
        for j, chip in enumerate(chips):
            for t in range(nt):
                copy(t, 1 + j, (*chip, c), me).wait_recv()
                cp = copy(t, 4 + j, (*chip, c), sibling)
                cp.start()
                passed.append(cp)
        for t in range(nt):
            copy(t, 0, sibling, me).wait_recv()
            for j, chip in enumerate(chips):
                copy(t, 4 + j, (*chip, 1 - c), me).wait_recv()
        for cp in first + passed:
            cp.wait_send()
        for cp in mine:
            cp.wait()

    return _Exchange(
        list(shards), [jax.ShapeDtypeStruct((N_DEV,) + s.shape, s.dtype) for s in shards],
        [pltpu.SemaphoreType.DMA((nt, 7)), pltpu.SemaphoreType.DMA((nt, 7)), pltpu.SemaphoreType.DMA((nt,))],
        start, finish)


def _swap_exchange(arrays, n_slices, copies):
    nt = len(arrays)

    def start(ins, outs, sems):
        for cp in copies(ins, outs, sems):
            cp.start()

    def finish(ins, outs, sems):
        sends = copies(ins, outs, sems)
        for cp in sends:
            cp.wait_recv()
        for cp in sends:
            cp.wait_send()

    return _Exchange(
        list(arrays), [jax.ShapeDtypeStruct((n_slices,) + a.shape[1:], a.dtype) for a in arrays],
        [pltpu.SemaphoreType.DMA((nt, n_slices)), pltpu.SemaphoreType.DMA((nt, n_slices))], start, finish)


def _cores_exchange(gs):
    def copies(ins, outs, sems):
        send_sems, recv_sems = sems
        x, y, c = _coords()
        return [pltpu.make_async_remote_copy(
            src_ref=ins[t].at[2 * j + (1 - c)], dst_ref=outs[t].at[j],
            send_sem=send_sems.at[t, j], recv_sem=recv_sems.at[t, j], device_id=(x, y, 1 - c), device_id_type=MESH)
            for t in range(len(gs)) for j in range(4)]

    return _swap_exchange(gs, 4, copies)


def _chips_exchange(ps):
    def copies(ins, outs, sems):
        send_sems, recv_sems = sems
        x, y, c = _coords()
        peers = [(1 - x, y), (x, 1 - y), (1 - x, 1 - y)]
        return [pltpu.make_async_remote_copy(
            src_ref=ins[t].at[2 * px + py], dst_ref=outs[t].at[k],
            send_sem=send_sems.at[t, k], recv_sem=recv_sems.at[t, k], device_id=(px, py, c), device_id_type=MESH)
            for t in range(len(ps)) for k, (px, py) in enumerate(peers)]

    return _swap_exchange(ps, 3, copies)


def _add_cores(g, r, core, *, name):
    _, A, B = g.shape
    ta = _tile(A, 512, 16)

    def body(core_ref, a_ref, b_ref, o16_ref):
        o16_ref[...] = (a_ref[...] + b_ref[...]).astype(BF16)

    blk = (None, ta, B)
    return pl.pallas_call(
        body, out_shape=jax.ShapeDtypeStruct((4, A, B), BF16),
        grid_spec=pltpu.PrefetchScalarGridSpec(
            num_scalar_prefetch=1, grid=(4, A // ta),
            in_specs=[pl.BlockSpec(blk, lambda j, i, core_ref: (2 * j + core_ref[0], i, 0)),
                      pl.BlockSpec(blk, lambda j, i, core_ref: (j, i, 0))],
            out_specs=pl.BlockSpec(blk, lambda j, i, core_ref: (j, i, 0))),
        compiler_params=_params(("parallel", "parallel")), name=name)(core, g, r)


def _adamw_math(w, g, m, v):
    m = ADAM_B1 * m + (1.0 - ADAM_B1) * g
    v = ADAM_B2 * v + (1.0 - ADAM_B2) * (g * g)
    m_hat = m / (1.0 - ADAM_B1 ** ADAM_STEP)
    v_hat = v / (1.0 - ADAM_B2 ** ADAM_STEP)
    delta = -ADAM_LR * (m_hat / (jnp.sqrt(v_hat) + ADAM_EPS) + ADAM_WD * w)
    return delta, m, v


def _sum_adamw(mine, sib, r, where, w, m, v, *, ta, name):
    Aw, Bw = w.shape
    Bg = mine.shape[2]
    assert Aw % ta == 0 and Bw <= Bg and mine.shape[1] == Aw

    def body(where_ref, p_ref, s_ref, r0, r1, r2, w_ref, m_ref, v_ref, g_out, d_out, m_out, v_out):
        g = (((p_ref[:, :Bw] + s_ref[:, :Bw]) + r0[:, :Bw].astype(F32))
             + r1[:, :Bw].astype(F32)) + r2[:, :Bw].astype(F32)
        delta, m_new, v_new = _adamw_math(w_ref[...], g, m_ref[...], v_ref[...])
        g_out[...] = g
        d_out[...] = delta
        m_out[...] = m_new
        v_out[...] = v_new

    gblk = (None, ta, Bg)
    row = pl.BlockSpec((ta, Bw), lambda i, where_ref: (i, 0))
    rspecs = [pl.BlockSpec(gblk, (lambda i, where_ref, k=k: (k, i, 0))) for k in range(3)]
    shp = jax.ShapeDtypeStruct((Aw, Bw), F32)
    return pl.pallas_call(
        body, out_shape=(shp, shp, shp, shp),
        grid_spec=pltpu.PrefetchScalarGridSpec(
            num_scalar_prefetch=1, grid=(Aw // ta,),
            in_specs=[pl.BlockSpec(gblk, lambda i, where_ref: (2 * where_ref[0] + where_ref[1], i, 0)),
                      pl.BlockSpec(gblk, lambda i, where_ref: (where_ref[0], i, 0))] + rspecs + [row, row, row],
            out_specs=(row, row, row, row)),
        compiler_params=_params(("parallel",)), name=name)(where, mine, sib, r, r, r, w, m, v)


def _adamw(w, g, m, v, *, name):
    def body(w_ref, g_ref, m_ref, v_ref, d_out, m_out, v_out):
        delta, m_new, v_new = _adamw_math(w_ref[...], g_ref[...], m_ref[...], v_ref[...])
        d_out[...] = delta
        m_out[...] = m_new
        v_out[...] = v_new

    vm = pl.BlockSpec(memory_space=pltpu.VMEM)
    shp = jax.ShapeDtypeStruct(w.shape, F32)
    return pl.pallas_call(body, out_shape=(shp, shp, shp), in_specs=[vm] * 4, out_specs=(vm, vm, vm),
                          compiler_params=_params(), name=name)(w, g, m, v)


def _small_allreduce_adamw(s, w, m, v, *, name):
    R, W = s.shape

    def body(s_ref, w_ref, m_ref, v_ref, g_out, d_out, m_out, v_out, gath, send_sems, recv_sems):
        x, y, c = _coords()
        mine = 4 * x + 2 * y + c
        gath[mine] = s_ref[...]
        peers = [((1 - x) if k & 4 else x, (1 - y) if k & 2 else y, (1 - c) if k & 1 else c) for k in range(1, N_DEV)]
        sends = []
        for k in range(1, N_DEV):
            peer = peers[k - 1]
            sends.append(pltpu.make_async_remote_copy(
                src_ref=s_ref, dst_ref=gath.at[mine], send_sem=send_sems.at[k - 1], recv_sem=recv_sems.at[k - 1],
                device_id=peer, device_id_type=MESH))
        for cp in sends:
            cp.start()
        for k in range(1, N_DEV):
            peer = peers[k - 1]
            pltpu.make_async_remote_copy(
                src_ref=s_ref, dst_ref=gath.at[4 * peer[0] + 2 * peer[1] + peer[2]],
                send_sem=send_sems.at[k - 1], recv_sem=recv_sems.at[k - 1],
                device_id=peer, device_id_type=MESH).wait_recv()
        for cp in sends:
            cp.wait_send()
        g = gath[0]
        for d in range(1, N_DEV):
            g = g + gath[d]
        delta, m_new, v_new = _adamw_math(w_ref[...], g, m_ref[...], v_ref[...])
        g_out[...] = g
        d_out[...] = delta
        m_out[...] = m_new
        v_out[...] = v_new

    vm = pl.BlockSpec(memory_space=pltpu.VMEM)
    shp = jax.ShapeDtypeStruct((R, W), F32)
    return pl.pallas_call(
        body, out_shape=(shp, shp, shp, shp), in_specs=[vm] * 4, out_specs=(vm, vm, vm, vm),
        scratch_shapes=[pltpu.VMEM((N_DEV, R, W), F32), pltpu.SemaphoreType.DMA((N_DEV - 1,)),
                        pltpu.SemaphoreType.DMA((N_DEV - 1,))],
        compiler_params=_params(), name=name)(s, w, m, v)


def _pack_small(rel_bias, g1, g2, g3, g4, b_forget, sinks, extra=None, meta=None):
    misc = jnp.concatenate([rel_bias.reshape(-1), b_forget.reshape(-1), sinks.reshape(-1)])
    misc = jnp.concatenate([misc, jnp.zeros((D_MODEL - misc.shape[0],), F32)])[None]
    last = jnp.zeros((1, D_MODEL), F32) if extra is None else extra
    meta = jnp.zeros((N_META, D_MODEL), F32) if meta is None else meta
    return jnp.concatenate([g1, g2, g3, g4, misc, last, jnp.zeros((2, D_MODEL), F32), meta], axis=0)


def _unpack_small(p):
    nrb = N_BUCKETS * SWA_Q_HEADS
    misc = p[4]
    return dict(rel_bias=misc[:nrb].reshape(N_BUCKETS, SWA_Q_HEADS), ln_pre_mix=p[0:1], ln_post_mix=p[1:2],
                ln_pre_ffn=p[2:3], ln_post_ffn=p[3:4], b_forget=misc[nrb:nrb + 8].reshape(1, 8),
                sinks=misc[nrb + 8:nrb + 16].reshape(1, 8))


def _proj_runs():
    gw = FOX_GROUP * HEAD_DIM
    swa = SWA_Q_W + 2 * SWA_KV_HEADS * HEAD_DIM
    runs = [(0, swa)]
    for grp in range(FOX_HEADS // FOX_GROUP):
        runs += [(swa + part * FOX_W + grp * gw, swa + part * FOX_W + (grp + 1) * gw) for part in range(3)]
    return runs


def _columns_from_shards(gathered, runs, shard):
    pieces = []
    for start, stop in runs:
        for d in range(start // shard, (stop - 1) // shard + 1):
            lo = d * shard
            pieces.append(gathered[d][:, max(start, lo) - lo:min(stop, lo + shard) - lo])
    return jnp.concatenate(pieces, axis=1)


def _device_shards(qkv, gate, shard, padded):
    pos, segments = 0, []
    for start, stop in _proj_runs():
        segments.append((start, stop, qkv, pos))
        pos += stop - start
    segments.append((pos, pos + gate.shape[1], gate, 0))
    total = pos + gate.shape[1]
    assert total % shard == 0
    zeros = jnp.zeros((qkv.shape[0], padded - shard), qkv.dtype)
    out = []
    for d in range(total // shard):
        lo, hi = d * shard, (d + 1) * shard
        pieces = [arr[:, src + max(lo, s) - s:src + min(hi, e) - s]
                  for s, e, arr, src in sorted(segments, key=lambda seg: seg[0]) if max(lo, s) < min(hi, e)]
        out.append(jnp.concatenate(pieces + [zeros], axis=1))
    return jnp.stack(out)


def kernel(x, meta_tokens, rel_bias, ln_pre_mix, ln_post_mix, ln_pre_ffn, ln_post_ffn, w_in, b_forget, sinks, w_out, w_gate_up, w_down, loss_target, m_meta_tokens, m_rel_bias, m_ln_pre_mix, m_ln_post_mix, m_ln_pre_ffn, m_ln_post_ffn, m_w_in, m_b_forget, m_sinks, m_w_out, m_w_gate_up, m_w_down, v_meta_tokens, v_rel_bias, v_ln_pre_mix, v_ln_post_mix, v_ln_pre_ffn, v_ln_post_ffn, v_w_in, v_b_forget, v_sinks, v_w_out, v_w_gate_up, v_w_down):
    seq = x.shape[1]
    T = BLOCK + seq
    assert T % FOX_TILE == 0
    nq = T // FOX_TILE
    tm = _tile(T, 1056)
    cin = w_in.shape[2]
    hid = w_down.shape[1]
    F = N_DEV * hid
    assert w_gate_up.shape[2] == 2 * hid and cin <= W_IN_PAD and hid % 16 == 0

    x_i, y_i, c_i = _coords()
    core = jnp.reshape(c_i, (1,)).astype(jnp.int32)
    where = jnp.stack([2 * x_i + y_i, c_i]).astype(jnp.int32)
    w_in_s = jnp.pad(w_in[0].astype(BF16), ((0, 0), (0, W_IN_PAD - cin)))
    w_gu_t = w_gate_up[0].T
    h0, target, hn1, hn1_t, g_in, _ = _pad_rows_rms(x[0], loss_target[0], ln_pre_mix,
                                                    _gather_exchange([w_in_s, meta_tokens]), name="ag_w_in_rms_pre_mix")
    gather_rest = _gather_exchange([w_out[0].astype(BF16), w_gu_t.astype(BF16), w_down[0].astype(BF16)])
    w_qkv = _columns_from_shards(g_in, _proj_runs(), cin)
    w_f = jnp.pad(_columns_from_shards(g_in, [(D_QKV, D_PROJ)], cin), ((0, 0), (0, BLOCK - FOX_HEADS)))

    proj = _matmul(hn1, w_qkv, out_dtype=BF16, tm=tm, tn=D_QKV, name="mm_in_proj")
    proj_f = _matmul(hn1, w_f, out_dtype=F32, tm=tm, tn=BLOCK, name="mm_in_proj_f")

    f_t = proj_f[:, :FOX_HEADS].T
    bf_col = b_forget.reshape(FOX_HEADS, 1)

    oh_cur, oh_prev = _bucket_onehots()
    bias_c, bias_p = _bias_tiles(rel_bias.T, jnp.asarray(oh_cur.T), jnp.asarray(oh_prev.T), name="bias_tiles")
    far = rel_bias[N_BUCKETS - 1]
    sink_v = sinks[0]
    mix_a = _swa_fwd(proj, bias_c, bias_p, far, sink_v, name="swa_fwd")

    cum_col = _fox_gates_fwd(f_t, bf_col, name="fox_gates_fwd")
    q_b, k_b, v_b = _fox_prep(proj, cum_col, name="fox_prep")
    mix, lse_row, g_out, g_gu, g_down = _fox_fwd(q_b, k_b, v_b, mix_a, ex=gather_rest, name="fox_fwd")
    w_out_full = g_out.reshape(D_MODEL, D_MODEL)
    w_gu_full_t = g_gu.reshape(2 * F, D_MODEL)
    w_down_full = g_down.reshape(F, D_MODEL)

    a1 = _matmul(mix, w_out_full, out_dtype=F32, tm=tm, tn=D_MODEL, name="mm_out_proj")
    h1, hn2 = _post_res_norm(a1, ln_post_mix, h0, ln_pre_ffn, name="post_mix_pre_ffn")
    gate, up, act, act_t = _gate_up_swiglu(hn2, w_gu_full_t, name="mm_gate_up")
    ff = _matmul(act, w_down_full, out_dtype=F32, tm=tm, tn=D_MODEL, name="mm_down")
    dh2, dff, dg_post_ffn, loss_acc = _loss_head(ff, ln_post_ffn, h1, target, name="loss_head")

    dgu = _d_act_swiglu(dff, w_down_full, gate, up, name="mm_d_act")
    d_w_down = _matmul(act_t, dff, out_dtype=F32, tm=_tile(F, 768), tn=512, name="mm_dw_down")
    dhn2 = _matmul(dgu, w_gu_full_t, out_dtype=F32, tm=tm, tn=512, name="mm_d_hn2")
    d_w_gu_t = _matmul(dgu, hn2, ta=True, out_dtype=F32, tm=512, tn=D_MODEL, name="mm_dw_gate_up")
    dh1, dg_pre_ffn, da1, dg_post_mix = _rms_bwd_twice(h1, ln_pre_ffn, dhn2, dh2, a1, ln_post_mix,
                                                       name="rms_bwd_pre_ffn_post_mix")
    dmix = _matmul(da1, w_out_full, nt=True, out_dtype=BF16, tm=tm, tn=D_MODEL, name="mm_d_mix")
    d_w_out = _matmul(mix, da1, ta=True, out_dtype=F32, tm=512, tn=D_MODEL, name="mm_dw_out")

    ffn_grads = [g.reshape(N_DEV, -1, D_MODEL) for g in (d_w_out, d_w_gu_t, d_w_down)]
    dproj_a, dbc, dbp, dbf, dsk, *ffn_sibling = _swa_bwd(
        proj, dmix, bias_c, bias_p, far, sink_v, ex=_cores_exchange(ffn_grads), name="swa_bwd")
    ffn_sums = [_add_cores(g, r, core, name="rs_add_" + t)
                for g, r, t in zip(ffn_grads, ffn_sibling, ["w_out", "w_gate_up", "w_down"])]

    do_b = _fox_prep_bwd(dmix, mix, name="fox_prep_bwd")
    dproj, dcq, dck, *ffn_chips = _fox_bwd(
        q_b, k_b, v_b, do_b, lse_row, dproj_a, ex=_chips_exchange(ffn_sums), name="fox_bwd")
    df_t, d_bf = _fox_gates_bwd(dcq.reshape(FOX_HEADS, T), dck.reshape(FOX_HEADS, T), f_t, bf_col,
                                name="fox_gates_bwd")
    df = jnp.pad(df_t.T.astype(BF16), ((0, 0), (0, BLOCK - FOX_HEADS)))

    d_w_qkv = _matmul(hn1_t, dproj, out_dtype=F32, tm=512, tn=768, name="mm_dw_in")
    d_w_f = _matmul(hn1_t, df, out_dtype=F32, tm=512, tn=BLOCK, name="mm_dw_in_f")
    d_w_in = _device_shards(d_w_qkv, d_w_f[:, :FOX_HEADS], cin, W_IN_PAD)
    d_tab, d_sink, in_sibling = _small_grads(dbc, dbp, dbf, dsk, jnp.asarray(oh_cur), jnp.asarray(oh_prev),
                                             ex=_cores_exchange([d_w_in]), name="small_grads")
    in_sum = _add_cores(d_w_in, in_sibling, core, name="rs_add_w_in")
    dhn1, in_chips = _matmul(dproj, w_qkv, nt=True, out_dtype=F32, tm=tm, tn=512,
                             ex=_chips_exchange([in_sum]), name="mm_d_hn1")
    dx_rows, dg_pre_mix, dh0_head = _rms_bwd_rows(h0, ln_pre_mix, dhn1, df, w_f, dh1, name="rms_bwd_pre_mix")
    grad_x = dx_rows[None]
    d_meta = dh0_head[PAD_ROWS:]

    rs_out, rs_gu, rs_down = zip(ffn_grads, ffn_sibling, ffn_chips)
    updates = [("w_in", (d_w_in, in_sibling, in_chips), (w_in[0], m_w_in[0], v_w_in[0]), 256),
               ("w_out", rs_out, (w_out[0], m_w_out[0], v_w_out[0]), BLOCK),
               ("w_gate_up", rs_gu, (w_gu_t, m_w_gate_up[0].T, v_w_gate_up[0].T), hid),
               ("w_down", rs_down, (w_down[0], m_w_down[0], v_w_down[0]), hid)]
    big = [{}, {}, {}, {}]
    for t, grads, shard, ta in updates:
        res = _sum_adamw(*grads, where, *shard, ta=ta, name="rs_adamw_" + t)
        for kind in range(4):
            big[kind][t] = (res[kind].T if t == "w_gate_up" else res[kind])[None]

    loss_row = jnp.pad(loss_acc[0:1, 0:1] * (0.5 / D_MODEL), ((0, 0), (0, D_MODEL - 1)))
    s_small = _pack_small(d_tab.T, dg_pre_mix, dg_post_mix, dg_pre_ffn, dg_post_ffn, d_bf, d_sink,
                          extra=loss_row, meta=d_meta)
    w_s = _pack_small(rel_bias, ln_pre_mix, ln_post_mix, ln_pre_ffn, ln_post_ffn, b_forget, sinks)
    m_s = _pack_small(m_rel_bias, m_ln_pre_mix, m_ln_post_mix, m_ln_pre_ffn, m_ln_post_ffn, m_b_forget, m_sinks)
    v_s = _pack_small(v_rel_bias, v_ln_pre_mix, v_ln_post_mix, v_ln_pre_ffn, v_ln_post_ffn, v_b_forget, v_sinks)
    small = _small_allreduce_adamw(s_small, w_s, m_s, v_s, name="small_allreduce_adamw")
    loss = small[0][5, 0]
    mcols = meta_tokens.shape[1]
    g_meta_mine = lax.dynamic_slice(small[0][8:8 + N_META], (0, (4 * x_i + 2 * y_i + c_i) * mcols), (N_META, mcols))
    big[0]["meta_tokens"] = g_meta_mine
    for kind, arr in enumerate(_adamw(meta_tokens, g_meta_mine, m_meta_tokens, v_meta_tokens, name="adamw_meta")):
        big[kind + 1]["meta_tokens"] = arr
    small = [_unpack_small(p) for p in small]

    names = ["meta_tokens", "rel_bias", "ln_pre_mix", "ln_post_mix", "ln_pre_ffn", "ln_post_ffn", "w_in",
             "b_forget", "sinks", "w_out", "w_gate_up", "w_down"]
    outs = [loss, grad_x]
    for kind in range(4):
        for nme in names:
            outs.append(big[kind][nme] if nme in big[kind] else small[kind][nme])
    return tuple(outs)
```

```python
import math

import numpy as np
import jax
import jax.numpy as jnp
from jax import lax
from jax.experimental import pallas as pl
from jax.experimental.pallas import tpu as pltpu

F32 = jnp.float32
BF16 = jnp.bfloat16
HIGHEST = lax.Precision.HIGHEST
MESH = pl.DeviceIdType.MESH

N_DEV = 8
D_MODEL = 1024
N_META = 16
HEAD_DIM = 64
SWA_Q_HEADS = 8
SWA_KV_HEADS = 2
SWA_GROUP = 4
FOX_HEADS = 8
FOX_W = FOX_HEADS * HEAD_DIM
SWA_Q_W = SWA_Q_HEADS * HEAD_DIM
BLOCK = 128
PAD_ROWS = BLOCK - N_META
N_BUCKETS = 32
MAX_DISTANCE = 128
D_FF = 2816
D_QKV = 2304
D_PROJ = D_QKV + FOX_HEADS
D_PROJ_PAD = 2560
EPS = 1e-6
NEG = -1e30
SCALE = HEAD_DIM ** -0.5
ADAM_LR, ADAM_B1, ADAM_B2, ADAM_EPS, ADAM_WD, ADAM_STEP = 0.001, 0.9, 0.999, 1e-08, 0.01, 10
VMEM_LIMIT = 56 * 1024 * 1024
FOX_TILE = 384
FOX_GROUP = 4
W_IN_PAD = 384

NT = (((1,), (1,)), ((), ()))
NN = (((1,), (0,)), ((), ()))
TN = (((0,), (0,)), ((), ()))


def _params(sem=None, **kw):
    if sem is not None:
        kw["dimension_semantics"] = sem
    return pltpu.CompilerParams(vmem_limit_bytes=VMEM_LIMIT, **kw)


def _tile(n, target, mult=16):
    best = None
    for t in range(mult, min(n, target) + 1, mult):
        if n % t == 0:
            best = t
    assert best is not None, (n, target)
    return best


def _matmul(a, b, *, nt=False, ta=False, out_dtype, tm, tn, tk=None, ex=None, name):
    M, K = a.shape[::-1] if ta else a.shape
    assert not (ta and nt)
    N = b.shape[0] if nt else b.shape[1]
    tk = K if tk is None else tk
    assert M % tm == 0 and N % tn == 0 and K % tk == 0, (name, a.shape, b.shape, tm, tn, tk)
    nk = K // tk
    dn = NT if nt else (TN if ta else NN)
    a_spec = pl.BlockSpec((tk, tm), lambda i, j, k: (k, i)) if ta else pl.BlockSpec((tm, tk), lambda i, j, k: (i, k))

    def body(a_ref, b_ref, o_ref, *scr):
        part = lax.dot_general(a_ref[...], b_ref[...], dn, preferred_element_type=F32)
        if nk == 1:
            o_ref[...] = part.astype(o_ref.dtype)
        else:
            acc = scr[0]
            k = pl.program_id(2)

            @pl.when(k == 0)
            def _():
                acc[...] = part

            @pl.when(k > 0)
            def _():
                acc[...] += part

            @pl.when(k == nk - 1)
            def _():
                o_ref[...] = acc[...].astype(o_ref.dtype)

    if nt:
        b_spec = pl.BlockSpec((tn, tk), lambda i, j, k: (j, k))
    else:
        b_spec = pl.BlockSpec((tk, tn), lambda i, j, k: (k, j))
    out_shape = jax.ShapeDtypeStruct((M, N), out_dtype)
    out_spec = pl.BlockSpec((tm, tn), lambda i, j, k: (i, j))
    grid = (M // tm, N // tn, nk)
    body, x_in, x_in_specs, x_out, x_out_specs, x_scr = _carry(ex, grid, 2, 1, body)
    res = pl.pallas_call(
        body,
        out_shape=(out_shape, *x_out),
        grid=grid,
        in_specs=[a_spec, b_spec] + x_in_specs,
        out_specs=(out_spec, *x_out_specs),
        scratch_shapes=([pltpu.VMEM((tm, tn), F32)] if nk > 1 else []) + x_scr,
        compiler_params=_params(("parallel", "parallel", "arbitrary") if ex is None else ("arbitrary",) * 3),
        name=name,
    )(a, b, *x_in)
    return res[0] if ex is None else res


def _rstd(x):
    return lax.rsqrt(jnp.mean(x * x, axis=-1, keepdims=True) + EPS)


def _pad_rows_rms(x, target, g, ex, *, name):
    S, D = x.shape
    nb = S // BLOCK + 1
    ni, no = len(ex.inputs), len(ex.out_shapes)
    mcols = D // N_DEV

    def body(x_ref, t_ref, g_ref, *rest):
        side_in, (h_ref, to_ref, y_ref, yt_ref) = rest[:ni], rest[ni:ni + 4]
        side_out = rest[ni + 4:ni + 4 + no]
        meta_buf, meta_sems, *sems = rest[ni + 4 + no:]
        i = pl.program_id(0)

        @pl.when(i == 0)
        def _():
            ex.start(side_in, side_out, sems)

        def norm():
            h = h_ref[...]
            y = h * _rstd(h) * g_ref[...]
            y_ref[...] = y.astype(y_ref.dtype)
            yt_ref[...] = y.T.astype(yt_ref.dtype)

        @pl.when(i < nb - 1)
        def _():
            h_ref[...] = x_ref[...]
            to_ref[...] = t_ref[...]
            norm()

        @pl.when(i == nb - 1)
        def _():
            ex.finish(side_in, side_out, sems)
            copies = [pltpu.make_async_copy(side_out[-1].at[d], meta_buf.at[:, d * mcols:(d + 1) * mcols],
                                            meta_sems.at[d]) for d in range(N_DEV)]
            for cp in copies:
                cp.start()
            for cp in copies:
                cp.wait()
            h_ref[:PAD_ROWS, :] = jnp.zeros((PAD_ROWS, D), F32)
            h_ref[PAD_ROWS:, :] = meta_buf[...]
            to_ref[...] = jnp.zeros_like(to_ref)
            norm()

    src = pl.BlockSpec((BLOCK, D), lambda i: (jnp.minimum(i, nb - 2), 0))
    dst = pl.BlockSpec((BLOCK, D), lambda i: ((i + 1) % nb, 0))
    hbm = pl.BlockSpec(memory_space=pl.ANY)
    rows = jax.ShapeDtypeStruct((BLOCK + S, D), F32)
    return pl.pallas_call(
        body,
        out_shape=(rows, rows, jax.ShapeDtypeStruct((BLOCK + S, D), BF16), jax.ShapeDtypeStruct((D, BLOCK + S), BF16),
                   *ex.out_shapes),
        grid=(nb,),
        in_specs=[src, src, pl.BlockSpec((1, D), lambda i: (0, 0))] + [hbm] * ni,
        out_specs=(dst, dst, dst, pl.BlockSpec((D, BLOCK), lambda i: (0, (i + 1) % nb)), *([hbm] * no)),
        scratch_shapes=[pltpu.VMEM((N_META, D), F32), pltpu.SemaphoreType.DMA((N_DEV,))] + list(ex.scratch),
        compiler_params=_params(("arbitrary",)), name=name)(x, target, g, *ex.inputs)


def _post_res_norm(a, g_post, h, g_pre, *, name):
    T, D = a.shape
    tm = _tile(T, 384, BLOCK)

    def body(a_ref, gp_ref, h_ref, gn_ref, h1_ref, o_ref):
        a = a_ref[...]
        h1 = h_ref[...] + a * _rstd(a) * gp_ref[...]
        h1_ref[...] = h1
        o_ref[...] = (h1 * _rstd(h1) * gn_ref[...]).astype(o_ref.dtype)

    row = pl.BlockSpec((tm, D), lambda i: (i, 0))
    vec = pl.BlockSpec((1, D), lambda i: (0, 0))
    return pl.pallas_call(
        body, out_shape=(jax.ShapeDtypeStruct((T, D), F32), jax.ShapeDtypeStruct((T, D), BF16)), grid=(T // tm,),
        in_specs=[row, vec, row, vec], out_specs=(row, row),
        compiler_params=_params(("parallel",)), name=name)(a, g_post, h, g_pre)


def _loss_head(a, g, h, target, *, name):
    T, D = a.shape
    tm = _tile(T, 512)

    def body(a_ref, g_ref, h_ref, t_ref, dy_ref, da_ref, dg_ref, loss_ref):
        i = pl.program_id(0)
        a = a_ref[...]
        r = _rstd(a)
        ah = a * r
        y = h_ref[...] + ah * g_ref[...]
        rows = i * tm + lax.broadcasted_iota(jnp.int32, (tm, 1), 0)
        err = jnp.where(rows >= BLOCK, y - t_ref[...], 0.0)
        dy = err / D
        dy_ref[...] = dy
        dah = dy * g_ref[...]
        da_ref[...] = (r * (dah - ah * jnp.mean(dah * ah, axis=-1, keepdims=True))).astype(da_ref.dtype)
        part = jnp.sum(jnp.sum(err * err, axis=1, keepdims=True), axis=0, keepdims=True)

        @pl.when(i == 0)
        def _():
            loss_ref[...] = jnp.zeros_like(loss_ref)
            dg_ref[...] = jnp.zeros_like(dg_ref)

        loss_ref[...] += jnp.broadcast_to(part, loss_ref.shape)
        dg_ref[...] += jnp.sum(dy * ah, axis=0, keepdims=True)

    row = pl.BlockSpec((tm, D), lambda i: (i, 0))
    vec = pl.BlockSpec((1, D), lambda i: (0, 0))
    return pl.pallas_call(
        body, out_shape=(jax.ShapeDtypeStruct((T, D), F32), jax.ShapeDtypeStruct((T, D), BF16),
                         jax.ShapeDtypeStruct((1, D), F32), jax.ShapeDtypeStruct((8, 128), F32)),
        grid=(T // tm,),
        in_specs=[row, vec, row, row],
        out_specs=(row, row, vec, pl.BlockSpec((8, 128), lambda i: (0, 0))),
        compiler_params=_params(("arbitrary",)), name=name)(a, g, h, target)


def _rms_pull_back(x, g, dy):
    r = _rstd(x)
    xh = x * r
    dxh = dy * g
    return r * (dxh - xh * jnp.mean(dxh * xh, axis=-1, keepdims=True)), jnp.sum(dy * xh, axis=0, keepdims=True)


def _rms_bwd_twice(x, g, dy, res, x2, g2, *, name):
    T, D = x.shape
    tm = _tile(T, 512)

    def body(x_ref, g_ref, dy_ref, res_ref, x2_ref, g2_ref, dx_ref, dg_ref, dx2_ref, dg2_ref):
        @pl.when(pl.program_id(0) == 0)
        def _():
            dg_ref[...] = jnp.zeros_like(dg_ref)
            dg2_ref[...] = jnp.zeros_like(dg2_ref)

        dx, dg = _rms_pull_back(x_ref[...], g_ref[...], dy_ref[...].astype(F32))
        dx = dx + res_ref[...]
        dx_ref[...] = dx
        dg_ref[...] += dg
        dx2, dg2 = _rms_pull_back(x2_ref[...], g2_ref[...], dx)
        dx2_ref[...] = dx2.astype(dx2_ref.dtype)
        dg2_ref[...] += dg2

    row = pl.BlockSpec((tm, D), lambda i: (i, 0))
    vec = pl.BlockSpec((1, D), lambda i: (0, 0))
    gain = jax.ShapeDtypeStruct((1, D), F32)
    return pl.pallas_call(
        body, out_shape=(jax.ShapeDtypeStruct((T, D), F32), gain, jax.ShapeDtypeStruct((T, D), BF16), gain),
        grid=(T // tm,), in_specs=[row, vec, row, row, row, vec], out_specs=(row, vec, row, vec),
        compiler_params=_params(("arbitrary",)), name=name)(x, g, dy, res, x2, g2)


def _rms_bwd_rows(x, g, dy, a, b, res, *, name):
    T, D = x.shape
    n = a.shape[1]
    n_tail = T // BLOCK - 1
    per_step = max(p for p in (4, 3, 2, 1) if n_tail % p == 0)
    steps = n_tail // per_step
    assert T == BLOCK * (1 + n_tail)
    n_rows = 4 * (per_step + 1)

    def body(*refs):
        rows, (g_ref, b_ref), (tail_ref, dg_ref, head_ref) = refs[:n_rows], refs[n_rows:n_rows + 2], refs[n_rows + 2:]

        def block(s):
            x_ref, dy_ref, a_ref, res_ref = rows[4 * s:4 * s + 4]
            dy_all = dy_ref[...] + lax.dot_general(a_ref[...], b_ref[...], NT, preferred_element_type=F32)
            dx, dg = _rms_pull_back(x_ref[...], g_ref[...], dy_all)
            return dx + res_ref[...], dg

        @pl.when(pl.program_id(0) == 0)
        def _():
            dx, dg = block(per_step)
            head_ref[...] = dx
            dg_ref[...] = dg

        for s in range(per_step):
            dx, dg = block(s)
            tail_ref[s * BLOCK:(s + 1) * BLOCK, :] = dx
            dg_ref[...] += dg

    def blocks(width):
        tail = [pl.BlockSpec((BLOCK, width), lambda i, s=s: (per_step * i + s + 1, 0)) for s in range(per_step)]
        return tail + [pl.BlockSpec((BLOCK, width), lambda i: (0, 0))]

    specs, args = [], []
    for bx, bdy, ba, bres in zip(blocks(D), blocks(D), blocks(n), blocks(D)):
        specs += [bx, bdy, ba, bres]
        args += [x, dy, a, res]
    vec = pl.BlockSpec((1, D), lambda i: (0, 0))
    return pl.pallas_call(
        body,
        out_shape=(jax.ShapeDtypeStruct((T - BLOCK, D), F32), jax.ShapeDtypeStruct((1, D), F32),
                   jax.ShapeDtypeStruct((BLOCK, D), F32)),
        grid=(steps,), in_specs=specs + [vec, pl.BlockSpec(b.shape, lambda i: (0, 0))],
        out_specs=(pl.BlockSpec((per_step * BLOCK, D), lambda i: (i, 0)), vec, pl.BlockSpec((BLOCK, D), lambda i: (0, 0))),
        compiler_params=_params(("arbitrary",)), name=name)(*args, g, b)


def _gate_up_swiglu(a, w_t, *, name):
    T, D = a.shape
    F = w_t.shape[0] // 2
    tm = _tile(T, 1408, BLOCK)
    n = _tile(F, 256, BLOCK)
    rows = 3 * BLOCK

    def body(a_ref, wg_ref, wu_ref, g_ref, u_ref, o_ref, ot_ref):
        wg, wu = wg_ref[...], wu_ref[...]
        for r in range(0, tm, rows):
            e = min(r + rows, tm)
            x = a_ref[r:e, :]
            g = lax.dot_general(x, wg, NT, preferred_element_type=F32)
            u = lax.dot_general(x, wu, NT, preferred_element_type=F32)
            g16, u16 = g.astype(BF16), u.astype(BF16)
            g_ref[r:e, :] = g16
            u_ref[r:e, :] = u16
            gr = g16.astype(F32)
            act = gr / (1.0 + jnp.exp(-gr)) * u16.astype(F32)
            o_ref[r:e, :] = act.astype(o_ref.dtype)
            ot_ref[:, r:e] = act.T.astype(ot_ref.dtype)

    tile = pl.BlockSpec((tm, n), lambda i, j: (i, j))
    shp = jax.ShapeDtypeStruct((T, F), BF16)
    return pl.pallas_call(
        body, out_shape=(shp, shp, shp, jax.ShapeDtypeStruct((F, T), BF16)), grid=(T // tm, F // n),
        in_specs=[pl.BlockSpec((tm, D), lambda i, j: (i, 0)),
                  pl.BlockSpec((n, D), lambda i, j: (j, 0)),
                  pl.BlockSpec((n, D), lambda i, j: (j + F // n, 0))],
        out_specs=(tile, tile, tile, pl.BlockSpec((n, tm), lambda i, j: (j, i))),
        compiler_params=_params(("parallel", "parallel")), name=name)(a, w_t, w_t)


def _d_act_swiglu(dff, w_down, gate, up, *, name):
    T, D = dff.shape
    F = w_down.shape[0]
    tm = _tile(T, 384)
    chunk = 768
    assert F % BLOCK == 0

    def body(d_ref, w_ref, g_ref, u_ref, o_ref):
        dy = d_ref[...]
        for c in range(0, F, chunk):
            e = min(c + chunk, F)
            d = lax.dot_general(dy, w_ref[c:e, :], NT, preferred_element_type=F32)
            g = g_ref[:, c:e].astype(F32)
            u = u_ref[:, c:e].astype(F32)
            sg = 1.0 / (1.0 + jnp.exp(-g))
            o_ref[:, c:e] = (d * u * (sg * (1.0 + g * (1.0 - sg)))).astype(o_ref.dtype)
            o_ref[:, F + c:F + e] = (d * (g * sg)).astype(o_ref.dtype)

    row = pl.BlockSpec((tm, F), lambda i: (i, 0))
    return pl.pallas_call(
        body, out_shape=jax.ShapeDtypeStruct((T, 2 * F), BF16), grid=(T // tm,),
        in_specs=[pl.BlockSpec((tm, D), lambda i: (i, 0)), pl.BlockSpec((F, D), lambda i: (0, 0)), row, row],
        out_specs=pl.BlockSpec((tm, 2 * F), lambda i: (i, 0)),
        compiler_params=_params(("parallel",)), name=name)(dff, w_down, gate, up)


def _fox_gates_fwd(f_t, b, *, name):
    H, T = f_t.shape
    nb = T // BLOCK

    def body(f_ref, b_ref, col_ref):
        f = f_ref[...] + b_ref[...]
        ls = jnp.minimum(f, 0.0) - jnp.log(1.0 + jnp.exp(-jnp.abs(f)))
        t = lax.broadcasted_iota(jnp.int32, (H, T), 1)
        ls = jnp.where(t >= PAD_ROWS, ls, 0.0)
        upper = (lax.broadcasted_iota(jnp.int32, (BLOCK, BLOCK), 0)
                 <= lax.broadcasted_iota(jnp.int32, (BLOCK, BLOCK), 1)).astype(F32)
        carry = jnp.zeros((H, 1), F32)
        for blk in range(nb):
            seg = ls[:, blk * BLOCK:(blk + 1) * BLOCK]
            pre = jnp.dot(seg, upper, precision=HIGHEST, preferred_element_type=F32) + carry
            key_gate = jnp.where(t[:, blk * BLOCK:(blk + 1) * BLOCK] >= PAD_ROWS, pre, -NEG)
            terms = list(_split3(pre)) + list(_split3(key_gate))
            col_ref[blk * BLOCK:(blk + 1) * BLOCK, :] = jnp.concatenate(
                terms + [jnp.zeros((BLOCK - len(terms) * H, BLOCK), F32)], axis=0).T.astype(col_ref.dtype)
            carry = pre[:, BLOCK - 1:BLOCK]

    vm = pl.BlockSpec(memory_space=pltpu.VMEM)
    return pl.pallas_call(
        body, out_shape=jax.ShapeDtypeStruct((T, BLOCK), BF16),
        in_specs=[vm, vm], out_specs=vm,
        compiler_params=_params(), name=name)(f_t, b)


def _fox_gates_bwd(dcq, dck, f_t, b, *, name):
    H, T = f_t.shape
    nb = T // BLOCK

    def body(dq_ref, d_ref, f_ref, b_ref, df_ref, db_ref):
        lower = (lax.broadcasted_iota(jnp.int32, (BLOCK, BLOCK), 0)
                 >= lax.broadcasted_iota(jnp.int32, (BLOCK, BLOCK), 1)).astype(F32)
        carry = jnp.zeros((H, 1), F32)
        for blk in range(nb - 1, -1, -1):
            seg = dq_ref[:, blk * BLOCK:(blk + 1) * BLOCK] - d_ref[:, blk * BLOCK:(blk + 1) * BLOCK]
            suf = jnp.dot(seg, lower, precision=HIGHEST, preferred_element_type=F32) + carry
            df_ref[:, blk * BLOCK:(blk + 1) * BLOCK] = suf
            carry = suf[:, 0:1]
        f = f_ref[...] + b_ref[...]
        t = lax.broadcasted_iota(jnp.int32, (H, T), 1)
        df = jnp.where(t >= PAD_ROWS, df_ref[...] / (1.0 + jnp.exp(f)), 0.0)
        df_ref[...] = df
        db_ref[...] = jnp.sum(df, axis=1, keepdims=True)

    vm = pl.BlockSpec(memory_space=pltpu.VMEM)
    return pl.pallas_call(
        body, out_shape=(jax.ShapeDtypeStruct((H, T), F32), jax.ShapeDtypeStruct((H, 1), F32)),
        in_specs=[vm, vm, vm, vm], out_specs=(vm, vm),
        compiler_params=_params(), name=name)(dcq, dck, f_t, b)


def _fox_lanes(parity):
    base = HEAD_DIM * (1 - parity)
    return base, base + 3


def _split3(c):
    hi = c.astype(BF16).astype(F32)
    r = c - hi
    mid = r.astype(BF16).astype(F32)
    lo = (r - mid).astype(BF16).astype(F32)
    return hi, mid, lo


def _lanes(lane, parity, data, start, terms, ones_at=None, fill=1.0):
    out = jnp.zeros((), F32) if ones_at is None else jnp.where((lane >= ones_at) & (lane < ones_at + 3), fill, 0.0)
    for i, t in enumerate(terms):
        out = jnp.where(lane == start + i, t, out)
    return jnp.where(lane // HEAD_DIM == parity, data, out)


def _fox_prep(proj, cum_col, *, name):
    T = proj.shape[0]
    tm = _tile(T, 1408, BLOCK)
    nt = T // tm
    H = FOX_HEADS
    lanes = 2 * HEAD_DIM
    first = (proj.shape[1] - 3 * H * HEAD_DIM) // lanes

    def body(q_ref, k_ref, v_ref, c_ref, qa_ref, ka_ref, va_ref):
        p = pl.program_id(0)
        i = pl.program_id(1)
        lane = lax.broadcasted_iota(jnp.int32, (1, lanes), 1)
        src = lax.broadcasted_iota(jnp.int32, (lanes, lanes), 0)
        dst = lax.broadcasted_iota(jnp.int32, (lanes, lanes), 1)
        q2 = q_ref[...].astype(F32) * SCALE
        k2 = k_ref[...].astype(F32)
        v2 = v_ref[...].astype(F32)
        gates = c_ref[...]
        def placed(h, first_term, start):
            pick = ((src % FOX_HEADS == h) & (src // FOX_HEADS - first_term == dst - start)
                    & (dst >= start) & (dst < start + 3))
            return jnp.dot(gates, pick.astype(BF16), preferred_element_type=F32)

        moved = [(placed(2 * p + e, 0, _fox_lanes(e)[1]), placed(2 * p + e, 3, _fox_lanes(e)[0])) for e in range(2)]
        for e in range(2):
            kc, qc = _fox_lanes(e)
            own = lane // HEAD_DIM == e
            minus = jnp.where((lane >= kc) & (lane < kc + 3), -1.0, 0.0)
            ones_q = jnp.where((lane >= qc) & (lane < qc + 3), 1.0, 0.0)
            ones_k = jnp.where((lane >= kc) & (lane < kc + 3), 1.0, 0.0)
            qa_ref[e] = jnp.where(own, q2, moved[e][0] + minus).astype(BF16)
            ka_ref[e] = jnp.where(own, k2, moved[e][1] + ones_q).astype(BF16)
            va_ref[e] = jnp.where(own, v2, ones_k).astype(BF16)

    pairs = FOX_GROUP // 2

    def col(part):
        return pl.BlockSpec((tm, lanes),
                            lambda p, i: (i, first + 3 * pairs * (p // pairs) + part * pairs + p % pairs))

    out = pl.BlockSpec((2, tm, lanes), lambda p, i: (p, i, 0))
    shp = jax.ShapeDtypeStruct((H, T, lanes), BF16)
    return pl.pallas_call(
        body, out_shape=(shp, shp, shp), grid=(H // 2, nt),
        in_specs=[col(0), col(1), col(2), pl.BlockSpec((tm, lanes), lambda p, i: (i, 0))],
        out_specs=(out, out, out),
        compiler_params=_params(("parallel", "parallel")), name=name)(proj, proj, proj, cum_col)


def _fox_fwd(q_aug, k_aug, v_aug, mix, *, ex=None, name):
    H, T, lanes = q_aug.shape
    tq = FOX_TILE
    nq = T // tq
    G = FOX_HEADS

    def body(q_ref, k_ref, v_ref, mix_ref, o_ref, lse_ref, m_scr, acc_scr):
        i = pl.program_id(1)
        m_scr[...] = jnp.full(m_scr.shape, NEG, F32)
        acc_scr[...] = jnp.zeros(acc_scr.shape, F32)

        def step(kb, diag):
            off = pl.multiple_of(kb * tq, tq)
            s_t = [lax.dot_general(k_ref[g, pl.ds(off, tq), :], q_ref[g], NT, preferred_element_type=F32)
                   for g in range(G)]
            if diag:
                r = lax.broadcasted_iota(jnp.int32, (tq, tq), 0)
                c = lax.broadcasted_iota(jnp.int32, (tq, tq), 1)
                s_t = [jnp.where(c >= r, s, NEG) for s in s_t]
            m_prev = [m_scr[g] for g in range(G)]
            m_new = [jnp.maximum(m_prev[g], jnp.max(s_t[g], axis=0, keepdims=True)) for g in range(G)]
            p_t = [jnp.exp(s_t[g] - m_new[g]).astype(BF16) for g in range(G)]
            pv = [lax.dot_general(v_ref[g, pl.ds(off, tq), :], p_t[g], TN, preferred_element_type=F32)
                  for g in range(G)]
            for g in range(G):
                acc_scr[g] = jnp.exp(m_prev[g] - m_new[g]) * acc_scr[g] + pv[g]
                m_scr[g] = m_new[g]

        def loop_body(kb, carry):
            step(kb, False)
            return carry

        lax.fori_loop(0, i, loop_body, 0)
        step(i, True)
        lane = lax.broadcasted_iota(jnp.int32, (tq, lanes), 1)
        outs = []
        for g in range(G):
            ones = _fox_lanes(g % 2)[0]
            acc = acc_scr[g]
            lse_ref[g] = m_scr[g] + jnp.log(acc[ones:ones + 1, :])
            acc_t = acc.T
            outs.append(acc_t / acc_t[:, ones:ones + 1])
        for pair in range(G // 2):
            o_ref[:, pair * lanes:(pair + 1) * lanes] = jnp.where(
                lane < HEAD_DIM, outs[2 * pair], outs[2 * pair + 1]).astype(o_ref.dtype)

    blk = pl.BlockSpec((G, tq, lanes), lambda h, i: (h, i, 0))
    full = pl.BlockSpec((G, T, lanes), lambda h, i: (h, 0, 0))
    grid = (H // G, nq)
    first = mix.shape[1] // (G * HEAD_DIM) - H // G
    body, x_in, x_in_specs, x_out, x_out_specs, x_scr = _carry(ex, grid, 4, 2, body)
    return pl.pallas_call(
        body,
        out_shape=(jax.ShapeDtypeStruct(mix.shape, mix.dtype), jax.ShapeDtypeStruct((H, nq, 1, tq), F32), *x_out),
        grid=grid,
        in_specs=[blk, full, full, pl.BlockSpec(memory_space=pl.ANY)] + x_in_specs,
        out_specs=(pl.BlockSpec((tq, G * HEAD_DIM), lambda h, i: (i, first + h)),
                   pl.BlockSpec((G, None, 1, tq), lambda h, i: (h, i, 0, 0)), *x_out_specs),
        input_output_aliases={3: 0},
        scratch_shapes=[pltpu.VMEM((G, 1, tq), F32), pltpu.VMEM((G, lanes, tq), F32)] + x_scr,
        compiler_params=_params(("arbitrary", "arbitrary")), name=name)(q_aug, k_aug, v_aug, mix, *x_in)


def _fox_prep_bwd(dmix, mix, *, name):
    T = dmix.shape[0]
    H = FOX_HEADS
    tm = _tile(T, 1408, BLOCK)
    lanes = 2 * HEAD_DIM
    first = mix.shape[1] // lanes - H // 2

    def body(d_ref, o_ref, da_ref):
        lane = lax.broadcasted_iota(jnp.int32, (1, lanes), 1)
        d2 = d_ref[...].astype(F32)
        prod = d2 * o_ref[...].astype(F32)
        for e in range(2):
            delta = jnp.sum(jnp.where(lane // HEAD_DIM == e, prod, 0.0), axis=1, keepdims=True)
            da_ref[e] = _lanes(lane, e, d2, _fox_lanes(e)[0], _split3(-delta)).astype(BF16)

    pair = pl.BlockSpec((tm, lanes), lambda p, i: (i, first + p))
    return pl.pallas_call(
        body, out_shape=jax.ShapeDtypeStruct((H, T, lanes), BF16), grid=(H // 2, T // tm),
        in_specs=[pair, pair],
        out_specs=pl.BlockSpec((2, tm, lanes), lambda p, i: (p, i, 0)),
        compiler_params=_params(("parallel", "parallel")), name=name)(dmix, mix)


def _fox_bwd(q_aug, k_aug, v_aug, do_aug, lse_row, dproj, *, ex=None, name):
    H, T, lanes = q_aug.shape
    tq = FOX_TILE
    nq = T // tq
    G = FOX_GROUP

    def side_by_side(tiles, scale=None):
        lane = lax.broadcasted_iota(jnp.int32, tiles[0].shape, 1)
        out = [jnp.where(lane < HEAD_DIM, tiles[2 * p], tiles[2 * p + 1]) for p in range(G // 2)]
        out = jnp.concatenate(out, axis=1)
        return out if scale is None else out * scale

    def body(q_ref, k_ref, v_ref, do_ref, lse_ref, dproj_in, out_ref, dcq_ref, dck_ref, dk_acc, dv_acc, dq_ref):
        j = pl.program_id(1)

        @pl.when(j == 0)
        def _():
            dq_ref[...] = jnp.zeros(dq_ref.shape, F32)
            dcq_ref[...] = jnp.zeros(dcq_ref.shape, F32)

        dk_acc[...] = jnp.zeros(dk_acc.shape, F32)
        dv_acc[...] = jnp.zeros(dv_acc.shape, F32)

        def step(qb, diag):
            off = pl.multiple_of(qb * tq, tq)
            heads = range(G)
            qa = [q_ref[g, pl.ds(off, tq), :] for g in heads]
            da = [do_ref[g, pl.ds(off, tq), :] for g in heads]
            s_t = [lax.dot_general(k_ref[g], qa[g], NT, preferred_element_type=F32) for g in heads]
            dp_t = [lax.dot_general(v_ref[g], da[g], NT, preferred_element_type=F32) for g in heads]
            p_t = [jnp.exp(s_t[g] - lse_ref[g, qb]) for g in heads]
            if diag:
                r = lax.broadcasted_iota(jnp.int32, (tq, tq), 0)
                c = lax.broadcasted_iota(jnp.int32, (tq, tq), 1)
                p_t = [jnp.where(c >= r, p, 0.0) for p in p_t]
            dsb = [(p_t[g] * dp_t[g]).astype(BF16) for g in heads]
            dv = [jnp.dot(p_t[g].astype(BF16), da[g], preferred_element_type=F32) for g in heads]
            dk = [jnp.dot(dsb[g], qa[g], preferred_element_type=F32) for g in heads]
            dq = [lax.dot_general(k_ref[g], dsb[g], TN, preferred_element_type=F32) for g in heads]
            for g in heads:
                dv_acc[g] += dv[g]
                dk_acc[g] += dk[g]
                dq_ref[g, qb] += dq[g]
                dcq_ref[g, qb] += jnp.sum(dsb[g].astype(F32), axis=0, keepdims=True)

        step(j, True)

        def loop_body(qb, carry):
            step(qb, False)
            return carry

        lax.fori_loop(j + 1, nq, loop_body, 0)
        dk = [dk_acc[g] for g in range(G)]
        out_ref[:, 0:wide] = side_by_side([dq_ref[g, j].T for g in range(G)], SCALE).astype(out_ref.dtype)
        out_ref[:, wide:2 * wide] = side_by_side(dk).astype(out_ref.dtype)
        out_ref[:, 2 * wide:3 * wide] = side_by_side([dv_acc[g] for g in range(G)]).astype(out_ref.dtype)
        for g in range(G):
            kc = _fox_lanes(g % 2)[0]
            dck_ref[g] = -dk[g].T[kc:kc + 1, :]

    blk = pl.BlockSpec((G, tq, lanes), lambda h, j: (h, j, 0))
    full = pl.BlockSpec((G, T, lanes), lambda h, j: (h, 0, 0))
    wide = G * HEAD_DIM
    first = dproj.shape[1] // (3 * wide) - H // G
    grid = (H // G, nq)
    body, x_in, x_in_specs, x_out, x_out_specs, x_scr = _carry(ex, grid, 6, 3, body)
    rows = jax.ShapeDtypeStruct((H, nq, 1, tq), F32)
    all_rows = pl.BlockSpec((G, nq, 1, tq), lambda h, j: (h, 0, 0, 0))
    return pl.pallas_call(
        body,
        out_shape=(jax.ShapeDtypeStruct(dproj.shape, dproj.dtype), rows, rows, *x_out),
        grid=grid,
        in_specs=[full, blk, blk, full, all_rows, pl.BlockSpec(memory_space=pl.ANY)] + x_in_specs,
        out_specs=(pl.BlockSpec((tq, 3 * wide), lambda h, j: (j, first + h)), all_rows,
                   pl.BlockSpec((G, None, 1, tq), lambda h, j: (h, j, 0, 0)), *x_out_specs),
        input_output_aliases={5: 0},
        scratch_shapes=[pltpu.VMEM((G, tq, lanes), F32), pltpu.VMEM((G, tq, lanes), F32),
                        pltpu.VMEM((G, nq, lanes, tq), F32)] + x_scr,
        compiler_params=_params(("arbitrary", "arbitrary")), name=name,
    )(q_aug, k_aug, v_aug, do_aug, lse_row, dproj, *x_in)


def _t5_bucket_np(d):
    n = np.maximum(d, 0).astype(np.int32)
    max_exact = N_BUCKETS // 2
    nf = np.maximum(n, 1).astype(np.float32)
    large = max_exact + (np.log(nf / max_exact) / math.log(MAX_DISTANCE / max_exact)
                         * (N_BUCKETS - max_exact)).astype(np.int32)
    large = np.minimum(large, N_BUCKETS - 1)
    return np.where(n < max_exact, n, large)


def _bucket_onehots():
    k = np.arange(BLOCK)[:, None]
    q = np.arange(BLOCK)[None, :]
    eye = np.eye(N_BUCKETS, dtype=np.float32)
    cur = eye[_t5_bucket_np(q - k).reshape(-1)]
    prev = eye[_t5_bucket_np(BLOCK + q - k).reshape(-1)]
    return cur, prev


SWA_K_COL = SWA_Q_HEADS * HEAD_DIM // (2 * HEAD_DIM)
SWA_V_COL = SWA_K_COL + 1


def _swa_terms(raw, bc, bp, far, sink, n):
    k = lax.broadcasted_iota(jnp.int32, (BLOCK, BLOCK), 0)
    q = lax.broadcasted_iota(jnp.int32, (BLOCK, BLOCK), 1)
    never = 2 * BLOCK
    s_c = raw[0] + bc
    s_p = raw[1] + bp
    s_m = raw[2] + jnp.where(n == 1, bp, far)
    s_c = jnp.where((k <= q) & (k >= jnp.where(n >= 1, 0, PAD_ROWS)), s_c, NEG)
    s_p = jnp.where(k > q + jnp.where(n >= 2, 0, never), s_p, NEG)
    s_m = jnp.where(k >= jnp.where(n >= 1, PAD_ROWS, never), s_m, NEG)
    m = jnp.maximum(jnp.maximum(jnp.max(s_c, axis=0, keepdims=True), jnp.max(s_p, axis=0, keepdims=True)),
                    jnp.maximum(jnp.max(s_m, axis=0, keepdims=True), sink))
    e = [jnp.exp(s_c - m), jnp.exp(s_p - m), jnp.exp(s_m - m)]
    e_s = jnp.exp(sink - m)
    l = (jnp.sum(e[0], axis=0, keepdims=True) + jnp.sum(e[1], axis=0, keepdims=True)
         + jnp.sum(e[2], axis=0, keepdims=True) + e_s)
    return e, e_s, l


SWA_STEP = 3


def _swa_specs():
    R = SWA_STEP

    def window(col):
        return ([pl.BlockSpec((BLOCK, BLOCK), lambda s, w=w: (jnp.maximum(R * s - 1 + w, 0), col)) for w in range(R + 1)]
                + [pl.BlockSpec((BLOCK, BLOCK), lambda s: (0, col))])

    qblk = pl.BlockSpec((R * BLOCK, SWA_Q_HEADS * HEAD_DIM), lambda s: (s, 0))
    bias = pl.BlockSpec((SWA_Q_HEADS, BLOCK, BLOCK), lambda s: (0, 0, 0))
    smem = pl.BlockSpec(memory_space=pltpu.SMEM)
    return qblk, window(SWA_K_COL), window(SWA_V_COL), bias, smem


def _swa_own_kv(tile_ref, kv):
    lane = lax.broadcasted_iota(jnp.int32, (BLOCK, 2 * HEAD_DIM), 1)
    t = tile_ref[...].astype(F32)
    return jnp.where(lane // HEAD_DIM == kv, t, pltpu.roll(t, HEAD_DIM, 1)).astype(BF16)


def _swa_fwd(proj, bc, bp, far, sinks, *, name):
    T = proj.shape[0]
    nb = T // BLOCK
    G = SWA_GROUP
    Hq = SWA_Q_HEADS
    lanes = 2 * HEAD_DIM

    R = SWA_STEP
    assert nb % R == 0

    def body(*refs):
        q_ref, k_refs, v_refs = refs[0], refs[1:R + 3], refs[R + 3:2 * R + 5]
        bc_ref, bp_ref, far_ref, sink_ref, o_ref = refs[2 * R + 5:]
        s = pl.program_id(0)
        lane = lax.broadcasted_iota(jnp.int32, (BLOCK, lanes), 1)
        kvs = range(SWA_KV_HEADS)
        kk = [[_swa_own_kv(ref, kv) for ref in k_refs] for kv in kvs]
        vv = [[_swa_own_kv(ref, kv) for ref in v_refs] for kv in kvs]
        chains = [(r, h) for r in range(R) for h in range(Hq)]
        tiles = lambda r: (r + 1, r, R + 1)
        q2 = {(r, pair): q_ref[r * BLOCK:(r + 1) * BLOCK, pair * lanes:(pair + 1) * lanes].astype(F32) * SCALE
              for r in range(R) for pair in range(Hq // 2)}
        qm = {c: jnp.where(lane // HEAD_DIM == c[1] % 2, q2[c[0], c[1] // 2], 0.0).astype(BF16) for c in chains}
        raw = {c: [lax.dot_general(kk[c[1] // G][w], qm[c], NT, preferred_element_type=F32) for w in tiles(c[0])]
               for c in chains}
        terms = {c: _swa_terms(raw[c], bc_ref[c[1]], bp_ref[c[1]], far_ref[c[1]], sink_ref[c[1]], R * s + c[0])
                 for c in chains}
        o_t = {c: sum(lax.dot_general(vv[c[1] // G][w], terms[c][0][b].astype(BF16), TN, preferred_element_type=F32)
                      for b, w in enumerate(tiles(c[0]))) for c in chains}
        outs = {c: (o_t[c] / terms[c][2]).T for c in chains}
        for r in range(R):
            for pair in range(Hq // 2):
                o_ref[r * BLOCK:(r + 1) * BLOCK, pair * lanes:(pair + 1) * lanes] = jnp.where(
                    lane < HEAD_DIM, outs[r, 2 * pair], outs[r, 2 * pair + 1]).astype(o_ref.dtype)

    qblk, keys, vals, bias, smem = _swa_specs()
    return pl.pallas_call(
        body, out_shape=jax.ShapeDtypeStruct((T, D_MODEL), BF16), grid=(nb // R,),
        in_specs=[qblk] + keys + vals + [bias, bias, smem, smem],
        out_specs=qblk,
        compiler_params=_params(("parallel",)), name=name,
    )(proj, *([proj] * (2 * R + 4)), bc, bp, far, sinks)


def _swa_bwd(proj, dmix, bc, bp, far, sinks, *, ex=None, name):
    T, width = proj.shape
    nb = T // BLOCK
    G = SWA_GROUP
    Hq = SWA_Q_HEADS
    lanes = 2 * HEAD_DIM
    qw = Hq * HEAD_DIM
    own_w = qw + 2 * lanes

    R = SWA_STEP
    assert nb % R == 0
    n_in = 2 * R + 10

    def body(*refs):
        q_ref, k_refs, v_refs = refs[0], refs[1:R + 3], refs[R + 3:2 * R + 5]
        do_ref, bc_ref, bp_ref, far_ref, sink_ref = refs[2 * R + 5:n_in]
        dp_ref, dbc_ref, dbp_ref, dbf_ref, dsk_ref, dk_acc, dv_acc = refs[n_in:]
        s = pl.program_id(0)

        @pl.when(s == 0)
        def _():
            for ref in (dk_acc, dv_acc, dbc_ref, dbp_ref, dbf_ref, dsk_ref):
                ref[...] = jnp.zeros(ref.shape, F32)

        lane = lax.broadcasted_iota(jnp.int32, (BLOCK, lanes), 1)
        kvs = range(SWA_KV_HEADS)
        kk = [[_swa_own_kv(ref, kv) for ref in k_refs] for kv in kvs]
        vv = [[_swa_own_kv(ref, kv) for ref in v_refs] for kv in kvs]
        chains = [(r, h) for r in range(R) for h in range(Hq)]
        blocks = range(3)
        tiles = lambda r: (r + 1, r, R + 1)
        sub = lambda ref, r, pair: ref[r * BLOCK:(r + 1) * BLOCK, pair * lanes:(pair + 1) * lanes]
        q2 = {(r, pair): sub(q_ref, r, pair).astype(F32) * SCALE for r in range(R) for pair in range(Hq // 2)}
        d2 = {(r, pair): sub(do_ref, r, pair) for r in range(R) for pair in range(Hq // 2)}
        own = [lane // HEAD_DIM == half for half in range(2)]
        qm = {c: jnp.where(own[c[1] % 2], q2[c[0], c[1] // 2], 0.0).astype(BF16) for c in chains}
        dom = {c: jnp.where(own[c[1] % 2], d2[c[0], c[1] // 2], jnp.zeros_like(d2[0, 0])) for c in chains}
        raw = {c: [lax.dot_general(kk[c[1] // G][w], qm[c], NT, preferred_element_type=F32) for w in tiles(c[0])]
               for c in chains}
        dp = {c: [lax.dot_general(vv[c[1] // G][w], dom[c], NT, preferred_element_type=F32) for w in tiles(c[0])]
              for c in chains}
        p, ds16 = {}, {}
        for c in chains:
            r, h = c
            n = R * s + r
            e, e_s, l = _swa_terms(raw[c], bc_ref[h], bp_ref[h], far_ref[h], sink_ref[h], n)
            inv = 1.0 / l
            ph = [e[b] * inv for b in blocks]
            delta = sum(jnp.sum(ph[b] * dp[c][b], axis=0, keepdims=True) for b in blocks)
            ds = [ph[b] * (dp[c][b] - delta) for b in blocks]
            dsk_ref[h] += -(e_s * inv) * delta
            dbc_ref[h] += ds[0]
            dbp_ref[h] += ds[1] + jnp.where(n == 1, ds[2], 0.0)
            dbf_ref[h] += jnp.where(n >= 2, ds[2], 0.0)
            p[c] = [x.astype(BF16) for x in ph]
            ds16[c] = [x.astype(BF16) for x in ds]
        dq_t = {c: sum(lax.dot_general(kk[c[1] // G][w], ds16[c][b], TN, preferred_element_type=F32)
                       for b, w in enumerate(tiles(c[0]))) for c in chains}
        group = [range(kv * G, (kv + 1) * G) for kv in kvs]
        dk = {(r, kv): [sum(jnp.dot(ds16[r, h][b], qm[r, h], preferred_element_type=F32) for h in group[kv])
                        for b in blocks] for r in range(R) for kv in kvs}
        dv = {(r, kv): [sum(jnp.dot(p[r, h][b], dom[r, h], preferred_element_type=F32) for h in group[kv])
                        for b in blocks] for r in range(R) for kv in kvs}
        for r in range(R):
            n = R * s + r
            rows = pl.ds(pl.multiple_of(n * BLOCK, BLOCK), BLOCK)
            prev_rows = pl.ds(pl.multiple_of(jnp.maximum(n - 1, 0) * BLOCK, BLOCK), BLOCK)
            for pair in range(Hq // 2):
                dp_ref[rows, pair * lanes:(pair + 1) * lanes] = (jnp.where(
                    lane < HEAD_DIM, dq_t[r, 2 * pair].T, dq_t[r, 2 * pair + 1].T) * SCALE).astype(dp_ref.dtype)
            for acc, ref in ((dk, dk_acc), (dv, dv_acc)):
                tot = [[a + pltpu.roll(a, HEAD_DIM, 1) for a in acc[r, kv]] for kv in kvs]
                both = [jnp.where(lane < HEAD_DIM, tot[0][b], tot[1][b]) for b in blocks]
                ref[rows, :] += both[0]
                ref[prev_rows, :] += both[1]
                ref[0:BLOCK, :] += both[2]

        @pl.when(s == nb // R - 1)
        def _():
            dp_ref[:, qw:qw + lanes] = dk_acc[...].astype(dp_ref.dtype)
            dp_ref[:, qw + lanes:own_w] = dv_acc[...].astype(dp_ref.dtype)

    qblk, keys, vals, bias, smem = _swa_specs()
    dsk = pl.BlockSpec((Hq, 1, BLOCK), lambda s: (0, 0, 0))
    grid = (nb // R,)
    body, x_in, x_in_specs, x_out, x_out_specs, x_scr = _carry(ex, grid, n_in, 5, body)
    tile = jax.ShapeDtypeStruct((Hq, BLOCK, BLOCK), F32)
    return pl.pallas_call(
        body,
        out_shape=(jax.ShapeDtypeStruct((T, width), BF16), tile, tile, tile,
                   jax.ShapeDtypeStruct((Hq, 1, BLOCK), F32), *x_out),
        grid=grid,
        in_specs=[qblk] + keys + vals + [qblk, bias, bias, smem, smem] + x_in_specs,
        out_specs=(pl.BlockSpec((T, own_w), lambda s: (0, 0)), bias, bias, bias, dsk, *x_out_specs),
        scratch_shapes=[pltpu.VMEM((T, lanes), F32), pltpu.VMEM((T, lanes), F32)] + x_scr,
        compiler_params=_params(("arbitrary",)), name=name,
    )(proj, *([proj] * (2 * R + 4)), dmix, bc, bp, far, sinks, *x_in)


def _bias_tiles(tab_t, oh_cur_t, oh_prev_t, *, name):
    Hq = tab_t.shape[0]

    def body(t_ref, oc_ref, op_ref, bc_ref, bp_ref):
        bc_ref[...] = jnp.dot(t_ref[...], oc_ref[...], precision=HIGHEST, preferred_element_type=F32)
        bp_ref[...] = jnp.dot(t_ref[...], op_ref[...], precision=HIGHEST, preferred_element_type=F32)

    vm = pl.BlockSpec(memory_space=pltpu.VMEM)
    shp = jax.ShapeDtypeStruct((Hq, BLOCK * BLOCK), F32)
    bc, bp = pl.pallas_call(body, out_shape=(shp, shp), in_specs=[vm] * 3, out_specs=(vm, vm),
                            compiler_params=_params(), name=name)(tab_t, oh_cur_t, oh_prev_t)
    return bc.reshape(Hq, BLOCK, BLOCK), bp.reshape(Hq, BLOCK, BLOCK)


def _small_grads(dbc, dbp, dbf, dsk, oh_cur, oh_prev, *, ex=None, name):
    Hq = dbc.shape[0]

    def body(dbc_ref, dbp_ref, dbf_ref, dsk_ref, oc_ref, op_ref, tab_ref, sink_ref):
        tab = (jnp.dot(dbc_ref[...], oc_ref[...], precision=HIGHEST, preferred_element_type=F32)
               + jnp.dot(dbp_ref[...], op_ref[...], precision=HIGHEST, preferred_element_type=F32))
        far = jnp.sum(dbf_ref[...], axis=1, keepdims=True)
        last = lax.broadcasted_iota(jnp.int32, (Hq, N_BUCKETS), 1) == N_BUCKETS - 1
        tab_ref[...] = tab + jnp.where(last, far, 0.0)
        sink_ref[...] = jnp.sum(dsk_ref[...], axis=1, keepdims=True)

    vm = pl.BlockSpec(memory_space=pltpu.VMEM)
    body, x_in, x_in_specs, x_out, x_out_specs, x_scr = _carry(ex, (), 6, 2, body)
    return pl.pallas_call(
        body, out_shape=(jax.ShapeDtypeStruct((Hq, N_BUCKETS), F32), jax.ShapeDtypeStruct((Hq, 1), F32), *x_out),
        in_specs=[vm] * 6 + x_in_specs, out_specs=(vm, vm, *x_out_specs), scratch_shapes=x_scr,
        compiler_params=_params(), name=name,
    )(dbc.reshape(Hq, -1), dbp.reshape(Hq, -1), dbf.reshape(Hq, -1), dsk.reshape(Hq, -1), oh_cur, oh_prev, *x_in)


def _coords():
    return lax.axis_index("x"), lax.axis_index("y"), lax.axis_index("c")


class _Exchange:
    def __init__(self, inputs, out_shapes, scratch, start, finish):
        self.inputs, self.out_shapes, self.scratch, self.start, self.finish = inputs, out_shapes, scratch, start, finish


def _carry(ex, grid, n_in, n_out, body):
    if ex is None:
        return body, [], [], [], [], []
    ni, no = len(ex.inputs), len(ex.out_shapes)

    def at_step(which):
        cond = jnp.bool_(True)
        for axis, n in enumerate(grid):
            cond = cond & (pl.program_id(axis) == (0 if which == "first" else n - 1))
        return cond

    def wrapped(*refs):
        refs = list(refs)
        n_own_scr = len(refs) - (n_in + ni + n_out + no) - len(ex.scratch)
        own_in, side_in = refs[:n_in], refs[n_in:n_in + ni]
        own_out = refs[n_in + ni:n_in + ni + n_out]
        side_out = refs[n_in + ni + n_out:n_in + ni + n_out + no]
        rest = refs[n_in + ni + n_out + no:]
        own_scr, sems = rest[:n_own_scr], rest[n_own_scr:]

        @pl.when(at_step("first"))
        def _():
            ex.start(side_in, side_out, sems)

        body(*own_in, *own_out, *own_scr)

        @pl.when(at_step("last"))
        def _():
            ex.finish(side_in, side_out, sems)

    hbm = pl.BlockSpec(memory_space=pl.ANY)
    return wrapped, list(ex.inputs), [hbm] * ni, list(ex.out_shapes), [hbm] * no, list(ex.scratch)


def _gather_exchange(shards):
    nt = len(shards)

    def copies(ins, outs, sems):
        send_sems, recv_sems, local_sems = sems
        x, y, c = _coords()
        me, sibling = (x, y, c), (x, y, 1 - c)
        chips = [(1 - x, y), (x, 1 - y), (1 - x, 1 - y)]

        def slot(t, dev):
            return outs[t].at[4 * dev[0] + 2 * dev[1] + dev[2]]

        def copy(t, k, block, to, src=None):
            dst = slot(t, block)
            return pltpu.make_async_remote_copy(
                src_ref=dst if src is None else src, dst_ref=dst,
                send_sem=send_sems.at[t, k], recv_sem=recv_sems.at[t, k], device_id=to, device_id_type=MESH)

        mine = [pltpu.make_async_copy(ins[t], slot(t, me), local_sems.at[t]) for t in range(nt)]
        first = []
        for t in range(nt):
            first.append(copy(t, 0, me, sibling, src=ins[t]))
            first += [copy(t, 1 + j, me, (*chip, c), src=ins[t]) for j, chip in enumerate(chips)]
        return copy, mine, first, me, sibling, chips, c

    def start(ins, outs, sems):
        _, mine, first, *_ = copies(ins, outs, sems)
        for cp in mine + first:
            cp.start()

    def finish(ins, outs, sems):
        copy, mine, first, me, sibling, chips, c = copies(ins, outs, sems)
        passed = []
        for j, chip in enumerate(chips):
            for t in range(nt):
                copy(t, 1 + j, (*chip, c), me).wait_recv()
                cp = copy(t, 4 + j, (*chip, c), sibling)
                cp.start()
                passed.append(cp)
        for t in range(nt):
            copy(t, 0, sibling, me).wait_recv()
            for j, chip in enumerate(chips):
                copy(t, 4 + j, (*chip, 1 - c), me).wait_recv()
        for cp in first + passed:
            cp.wait_send()
        for cp in mine:
            cp.wait()

    return _Exchange(
        list(shards), [jax.ShapeDtypeStruct((N_DEV,) + s.shape, s.dtype) for s in shards],
        [pltpu.SemaphoreType.DMA((nt, 7)), pltpu.SemaphoreType.DMA((nt, 7)), pltpu.SemaphoreType.DMA((nt,))],
        start, finish)


def _swap_exchange(arrays, n_slices, copies):
    nt = len(arrays)

    def start(ins, outs, sems):
        for cp in copies(ins, outs, sems):
            cp.start()

    def finish(ins, outs, sems):
        sends = copies(ins, outs, sems)
        for cp in sends:
            cp.wait_recv()
        for cp in sends:
            cp.wait_send()

    return _Exchange(
        list(arrays), [jax.ShapeDtypeStruct((n_slices,) + a.shape[1:], a.dtype) for a in arrays],
        [pltpu.SemaphoreType.DMA((nt, n_slices)), pltpu.SemaphoreType.DMA((nt, n_slices))], start, finish)


def _cores_exchange(gs):
    def copies(ins, outs, sems):
        send_sems, recv_sems = sems
        x, y, c = _coords()
        return [pltpu.make_async_remote_copy(
            src_ref=ins[t].at[2 * j + (1 - c)], dst_ref=outs[t].at[j],
            send_sem=send_sems.at[t, j], recv_sem=recv_sems.at[t, j], device_id=(x, y, 1 - c), device_id_type=MESH)
            for t in range(len(gs)) for j in range(4)]

    return _swap_exchange(gs, 4, copies)


def _chips_exchange(ps):
    def copies(ins, outs, sems):
        send_sems, recv_sems = sems
        x, y, c = _coords()
        peers = [(1 - x, y), (x, 1 - y), (1 - x, 1 - y)]
        return [pltpu.make_async_remote_copy(
            src_ref=ins[t].at[2 * px + py], dst_ref=outs[t].at[k],
            send_sem=send_sems.at[t, k], recv_sem=recv_sems.at[t, k], device_id=(px, py, c), device_id_type=MESH)
            for t in range(len(ps)) for k, (px, py) in enumerate(peers)]

    return _swap_exchange(ps, 3, copies)


def _add_cores(g, r, core, *, name):
    _, A, B = g.shape
    ta = _tile(A, 512, 16)

    def body(core_ref, a_ref, b_ref, o16_ref):
        o16_ref[...] = (a_ref[...] + b_ref[...]).astype(BF16)

    blk = (None, ta, B)
    return pl.pallas_call(
        body, out_shape=jax.ShapeDtypeStruct((4, A, B), BF16),
        grid_spec=pltpu.PrefetchScalarGridSpec(
            num_scalar_prefetch=1, grid=(4, A // ta),
            in_specs=[pl.BlockSpec(blk, lambda j, i, core_ref: (2 * j + core_ref[0], i, 0)),
                      pl.BlockSpec(blk, lambda j, i, core_ref: (j, i, 0))],
            out_specs=pl.BlockSpec(blk, lambda j, i, core_ref: (j, i, 0))),
        compiler_params=_params(("parallel", "parallel")), name=name)(core, g, r)


def _adamw_math(w, g, m, v):
    m = ADAM_B1 * m + (1.0 - ADAM_B1) * g
    v = ADAM_B2 * v + (1.0 - ADAM_B2) * (g * g)
    m_hat = m / (1.0 - ADAM_B1 ** ADAM_STEP)
    v_hat = v / (1.0 - ADAM_B2 ** ADAM_STEP)
    delta = -ADAM_LR * (m_hat / (jnp.sqrt(v_hat) + ADAM_EPS) + ADAM_WD * w)
    return delta, m, v


def _sum_adamw(mine, sib, r, where, w, m, v, *, ta, name):
    Aw, Bw = w.shape
    Bg = mine.shape[2]
    assert Aw % ta == 0 and Bw <= Bg and mine.shape[1] == Aw

    def body(where_ref, p_ref, s_ref, r0, r1, r2, w_ref, m_ref, v_ref, g_out, d_out, m_out, v_out):
        g = (((p_ref[:, :Bw] + s_ref[:, :Bw]) + r0[:, :Bw].astype(F32))
             + r1[:, :Bw].astype(F32)) + r2[:, :Bw].astype(F32)
        delta, m_new, v_new = _adamw_math(w_ref[...], g, m_ref[...], v_ref[...])
        g_out[...] = g
        d_out[...] = delta
        m_out[...] = m_new
        v_out[...] = v_new

    gblk = (None, ta, Bg)
    row = pl.BlockSpec((ta, Bw), lambda i, where_ref: (i, 0))
    rspecs = [pl.BlockSpec(gblk, (lambda i, where_ref, k=k: (k, i, 0))) for k in range(3)]
    shp = jax.ShapeDtypeStruct((Aw, Bw), F32)
    return pl.pallas_call(
        body, out_shape=(shp, shp, shp, shp),
        grid_spec=pltpu.PrefetchScalarGridSpec(
            num_scalar_prefetch=1, grid=(Aw // ta,),
            in_specs=[pl.BlockSpec(gblk, lambda i, where_ref: (2 * where_ref[0] + where_ref[1], i, 0)),
                      pl.BlockSpec(gblk, lambda i, where_ref: (where_ref[0], i, 0))] + rspecs + [row, row, row],
            out_specs=(row, row, row, row)),
        compiler_params=_params(("parallel",)), name=name)(where, mine, sib, r, r, r, w, m, v)


def _adamw(w, g, m, v, *, name):
    def body(w_ref, g_ref, m_ref, v_ref, d_out, m_out, v_out):
        delta, m_new, v_new = _adamw_math(w_ref[...], g_ref[...], m_ref[...], v_ref[...])
        d_out[...] = delta
        m_out[...] = m_new
        v_out[...] = v_new

    vm = pl.BlockSpec(memory_space=pltpu.VMEM)
    shp = jax.ShapeDtypeStruct(w.shape, F32)
    return pl.pallas_call(body, out_shape=(shp, shp, shp), in_specs=[vm] * 4, out_specs=(vm, vm, vm),
                          compiler_params=_params(), name=name)(w, g, m, v)


def _small_allreduce_adamw(s, w, m, v, *, name):
    R, W = s.shape

    def body(s_ref, w_ref, m_ref, v_ref, g_out, d_out, m_out, v_out, gath, send_sems, recv_sems):
        x, y, c = _coords()
        mine = 4 * x + 2 * y + c
        gath[mine] = s_ref[...]
        peers = [((1 - x) if k & 4 else x, (1 - y) if k & 2 else y, (1 - c) if k & 1 else c) for k in range(1, N_DEV)]
        sends = []
        for k in range(1, N_DEV):
            peer = peers[k - 1]
            sends.append(pltpu.make_async_remote_copy(
                src_ref=s_ref, dst_ref=gath.at[mine], send_sem=send_sems.at[k - 1], recv_sem=recv_sems.at[k - 1],
                device_id=peer, device_id_type=MESH))
        for cp in sends:
            cp.start()
        for k in range(1, N_DEV):
            peer = peers[k - 1]
            pltpu.make_async_remote_copy(
                src_ref=s_ref, dst_ref=gath.at[4 * peer[0] + 2 * peer[1] + peer[2]],
                send_sem=send_sems.at[k - 1], recv_sem=recv_sems.at[k - 1],
                device_id=peer, device_id_type=MESH).wait_recv()
        for cp in sends:
            cp.wait_send()
        g = gath[0]
        for d in range(1, N_DEV):
            g = g + gath[d]
        delta, m_new, v_new = _adamw_math(w_ref[...], g, m_ref[...], v_ref[...])
        g_out[...] = g
        d_out[...] = delta
        m_out[...] = m_new
        v_out[...] = v_new

    vm = pl.BlockSpec(memory_space=pltpu.VMEM)
    shp = jax.ShapeDtypeStruct((R, W), F32)
    return pl.pallas_call(
        body, out_shape=(shp, shp, shp, shp), in_specs=[vm] * 4, out_specs=(vm, vm, vm, vm),
        scratch_shapes=[pltpu.VMEM((N_DEV, R, W), F32), pltpu.SemaphoreType.DMA((N_DEV - 1,)),
                        pltpu.SemaphoreType.DMA((N_DEV - 1,))],
        compiler_params=_params(), name=name)(s, w, m, v)


def _pack_small(rel_bias, g1, g2, g3, g4, b_forget, sinks, extra=None, meta=None):
    misc = jnp.concatenate([rel_bias.reshape(-1), b_forget.reshape(-1), sinks.reshape(-1)])
    misc = jnp.concatenate([misc, jnp.zeros((D_MODEL - misc.shape[0],), F32)])[None]
    last = jnp.zeros((1, D_MODEL), F32) if extra is None else extra
    meta = jnp.zeros((N_META, D_MODEL), F32) if meta is None else meta
    return jnp.concatenate([g1, g2, g3, g4, misc, last, jnp.zeros((2, D_MODEL), F32), meta], axis=0)


def _unpack_small(p):
    nrb = N_BUCKETS * SWA_Q_HEADS
    misc = p[4]
    return dict(rel_bias=misc[:nrb].reshape(N_BUCKETS, SWA_Q_HEADS), ln_pre_mix=p[0:1], ln_post_mix=p[1:2],
                ln_pre_ffn=p[2:3], ln_post_ffn=p[3:4], b_forget=misc[nrb:nrb + 8].reshape(1, 8),
                sinks=misc[nrb + 8:nrb + 16].reshape(1, 8))


def _proj_runs():
    gw = FOX_GROUP * HEAD_DIM
    swa = SWA_Q_W + 2 * SWA_KV_HEADS * HEAD_DIM
    runs = [(0, swa)]
    for grp in range(FOX_HEADS // FOX_GROUP):
        runs += [(swa + part * FOX_W + grp * gw, swa + part * FOX_W + (grp + 1) * gw) for part in range(3)]
    return runs


def _columns_from_shards(gathered, runs, shard):
    pieces = []
    for start, stop in runs:
        for d in range(start // shard, (stop - 1) // shard + 1):
            lo = d * shard
            pieces.append(gathered[d][:, max(start, lo) - lo:min(stop, lo + shard) - lo])
    return jnp.concatenate(pieces, axis=1)


def _device_shards(qkv, gate, shard, padded):
    pos, segments = 0, []
    for start, stop in _proj_runs():
        segments.append((start, stop, qkv, pos))
        pos += stop - start
    segments.append((pos, pos + gate.shape[1], gate, 0))
    total = pos + gate.shape[1]
    assert total % shard == 0
    zeros = jnp.zeros((qkv.shape[0], padded - shard), qkv.dtype)
    out = []
    for d in range(total // shard):
        lo, hi = d * shard, (d + 1) * shard
        pieces = [arr[:, src + max(lo, s) - s:src + min(hi, e) - s]
                  for s, e, arr, src in sorted(segments, key=lambda seg: seg[0]) if max(lo, s) < min(hi, e)]
        out.append(jnp.concatenate(pieces + [zeros], axis=1))
    return jnp.stack(out)


def kernel(x, meta_tokens, rel_bias, ln_pre_mix, ln_post_mix, ln_pre_ffn, ln_post_ffn, w_in, b_forget, sinks, w_out, w_gate_up, w_down, loss_target, m_meta_tokens, m_rel_bias, m_ln_pre_mix, m_ln_post_mix, m_ln_pre_ffn, m_ln_post_ffn, m_w_in, m_b_forget, m_sinks, m_w_out, m_w_gate_up, m_w_down, v_meta_tokens, v_rel_bias, v_ln_pre_mix, v_ln_post_mix, v_ln_pre_ffn, v_ln_post_ffn, v_w_in, v_b_forget, v_sinks, v_w_out, v_w_gate_up, v_w_down):
    seq = x.shape[1]
    T = BLOCK + seq
    assert T % FOX_TILE == 0
    nq = T // FOX_TILE
    tm = _tile(T, 1056)
    cin = w_in.shape[2]
    hid = w_down.shape[1]
    F = N_DEV * hid
    assert w_gate_up.shape[2] == 2 * hid and cin <= W_IN_PAD and hid % 16 == 0

    x_i, y_i, c_i = _coords()
    core = jnp.reshape(c_i, (1,)).astype(jnp.int32)
    where = jnp.stack([2 * x_i + y_i, c_i]).astype(jnp.int32)
    w_in_s = jnp.pad(w_in[0].astype(BF16), ((0, 0), (0, W_IN_PAD - cin)))
    w_gu_t = w_gate_up[0].T
    h0, target, hn1, hn1_t, g_in, _ = _pad_rows_rms(x[0], loss_target[0], ln_pre_mix,
                                                    _gather_exchange([w_in_s, meta_tokens]), name="ag_w_in_rms_pre_mix")
    gather_rest = _gather_exchange([w_out[0].astype(BF16), w_gu_t.astype(BF16), w_down[0].astype(BF16)])
    w_qkv = _columns_from_shards(g_in, _proj_runs(), cin)
    w_f = jnp.pad(_columns_from_shards(g_in, [(D_QKV, D_PROJ)], cin), ((0, 0), (0, BLOCK - FOX_HEADS)))

    proj = _matmul(hn1, w_qkv, out_dtype=BF16, tm=tm, tn=D_QKV, name="mm_in_proj")
    proj_f = _matmul(hn1, w_f, out_dtype=F32, tm=tm, tn=BLOCK, name="mm_in_proj_f")

    f_t = proj_f[:, :FOX_HEADS].T
    bf_col = b_forget.reshape(FOX_HEADS, 1)

    oh_cur, oh_prev = _bucket_onehots()
    bias_c, bias_p = _bias_tiles(rel_bias.T, jnp.asarray(oh_cur.T), jnp.asarray(oh_prev.T), name="bias_tiles")
    far = rel_bias[N_BUCKETS - 1]
    sink_v = sinks[0]
    mix_a = _swa_fwd(proj, bias_c, bias_p, far, sink_v, name="swa_fwd")

    cum_col = _fox_gates_fwd(f_t, bf_col, name="fox_gates_fwd")
    q_b, k_b, v_b = _fox_prep(proj, cum_col, name="fox_prep")
    mix, lse_row, g_out, g_gu, g_down = _fox_fwd(q_b, k_b, v_b, mix_a, ex=gather_rest, name="fox_fwd")
    w_out_full = g_out.reshape(D_MODEL, D_MODEL)
    w_gu_full_t = g_gu.reshape(2 * F, D_MODEL)
    w_down_full = g_down.reshape(F, D_MODEL)

    a1 = _matmul(mix, w_out_full, out_dtype=F32, tm=tm, tn=D_MODEL, name="mm_out_proj")
    h1, hn2 = _post_res_norm(a1, ln_post_mix, h0, ln_pre_ffn, name="post_mix_pre_ffn")
    gate, up, act, act_t = _gate_up_swiglu(hn2, w_gu_full_t, name="mm_gate_up")
    ff = _matmul(act, w_down_full, out_dtype=F32, tm=tm, tn=D_MODEL, name="mm_down")
    dh2, dff, dg_post_ffn, loss_acc = _loss_head(ff, ln_post_ffn, h1, target, name="loss_head")

    dgu = _d_act_swiglu(dff, w_down_full, gate, up, name="mm_d_act")
    d_w_down = _matmul(act_t, dff, out_dtype=F32, tm=_tile(F, 768), tn=D_MODEL, name="mm_dw_down")
    dhn2 = _matmul(dgu, w_gu_full_t, out_dtype=F32, tm=tm, tn=512, name="mm_d_hn2")
    d_w_gu_t = _matmul(dgu, hn2, ta=True, out_dtype=F32, tm=512, tn=D_MODEL, name="mm_dw_gate_up")
    dh1, dg_pre_ffn, da1, dg_post_mix = _rms_bwd_twice(h1, ln_pre_ffn, dhn2, dh2, a1, ln_post_mix,
                                                       name="rms_bwd_pre_ffn_post_mix")
    dmix = _matmul(da1, w_out_full, nt=True, out_dtype=BF16, tm=tm, tn=D_MODEL, name="mm_d_mix")
    d_w_out = _matmul(mix, da1, ta=True, out_dtype=F32, tm=512, tn=D_MODEL, name="mm_dw_out")

    ffn_grads = [g.reshape(N_DEV, -1, D_MODEL) for g in (d_w_out, d_w_gu_t, d_w_down)]
    dproj_a, dbc, dbp, dbf, dsk, *ffn_sibling = _swa_bwd(
        proj, dmix, bias_c, bias_p, far, sink_v, ex=_cores_exchange(ffn_grads), name="swa_bwd")
    ffn_sums = [_add_cores(g, r, core, name="rs_add_" + t)
                for g, r, t in zip(ffn_grads, ffn_sibling, ["w_out", "w_gate_up", "w_down"])]

    do_b = _fox_prep_bwd(dmix, mix, name="fox_prep_bwd")
    dproj, dcq, dck, *ffn_chips = _fox_bwd(
        q_b, k_b, v_b, do_b, lse_row, dproj_a, ex=_chips_exchange(ffn_sums), name="fox_bwd")
    df_t, d_bf = _fox_gates_bwd(dcq.reshape(FOX_HEADS, T), dck.reshape(FOX_HEADS, T), f_t, bf_col,
                                name="fox_gates_bwd")
    df = jnp.pad(df_t.T.astype(BF16), ((0, 0), (0, BLOCK - FOX_HEADS)))

    d_w_qkv = _matmul(hn1_t, dproj, out_dtype=F32, tm=512, tn=768, name="mm_dw_in")
    d_w_f = _matmul(hn1_t, df, out_dtype=F32, tm=512, tn=BLOCK, name="mm_dw_in_f")
    d_w_in = _device_shards(d_w_qkv, d_w_f[:, :FOX_HEADS], cin, W_IN_PAD)
    d_tab, d_sink, in_sibling = _small_grads(dbc, dbp, dbf, dsk, jnp.asarray(oh_cur), jnp.asarray(oh_prev),
                                             ex=_cores_exchange([d_w_in]), name="small_grads")
    in_sum = _add_cores(d_w_in, in_sibling, core, name="rs_add_w_in")
    dhn1, in_chips = _matmul(dproj, w_qkv, nt=True, out_dtype=F32, tm=tm, tn=512,
                             ex=_chips_exchange([in_sum]), name="mm_d_hn1")
    dx_rows, dg_pre_mix, dh0_head = _rms_bwd_rows(h0, ln_pre_mix, dhn1, df, w_f, dh1, name="rms_bwd_pre_mix")
    grad_x = dx_rows[None]
    d_meta = dh0_head[PAD_ROWS:]

    rs_out, rs_gu, rs_down = zip(ffn_grads, ffn_sibling, ffn_chips)
    updates = [("w_in", (d_w_in, in_sibling, in_chips), (w_in[0], m_w_in[0], v_w_in[0]), 256),
               ("w_out", rs_out, (w_out[0], m_w_out[0], v_w_out[0]), BLOCK),
               ("w_gate_up", rs_gu, (w_gu_t, m_w_gate_up[0].T, v_w_gate_up[0].T), hid),
               ("w_down", rs_down, (w_down[0], m_w_down[0], v_w_down[0]), hid)]
    big = [{}, {}, {}, {}]
    for t, grads, shard, ta in updates:
        res = _sum_adamw(*grads, where, *shard, ta=ta, name="rs_adamw_" + t)
        for kind in range(4):
            big[kind][t] = (res[kind].T if t == "w_gate_up" else res[kind])[None]

    loss_row = jnp.pad(loss_acc[0:1, 0:1] * (0.5 / D_MODEL), ((0, 0), (0, D_MODEL - 1)))
    s_small = _pack_small(d_tab.T, dg_pre_mix, dg_post_mix, dg_pre_ffn, dg_post_ffn, d_bf, d_sink,
                          extra=loss_row, meta=d_meta)
    w_s = _pack_small(rel_bias, ln_pre_mix, ln_post_mix, ln_pre_ffn, ln_post_ffn, b_forget, sinks)
    m_s = _pack_small(m_rel_bias, m_ln_pre_mix, m_ln_post_mix, m_ln_pre_ffn, m_ln_post_ffn, m_b_forget, m_sinks)
    v_s = _pack_small(v_rel_bias, v_ln_pre_mix, v_ln_post_mix, v_ln_pre_ffn, v_ln_post_ffn, v_b_forget, v_sinks)
    small = _small_allreduce_adamw(s_small, w_s, m_s, v_s, name="small_allreduce_adamw")
    loss = small[0][5, 0]
    mcols = meta_tokens.shape[1]
    g_meta_mine = lax.dynamic_slice(small[0][8:8 + N_META], (0, (4 * x_i + 2 * y_i + c_i) * mcols), (N_META, mcols))
    big[0]["meta_tokens"] = g_meta_mine
    for kind, arr in enumerate(_adamw(meta_tokens, g_meta_mine, m_meta_tokens, v_meta_tokens, name="adamw_meta")):
        big[kind + 1]["meta_tokens"] = arr
    small = [_unpack_small(p) for p in small]

    names = ["meta_tokens", "rel_bias", "ln_pre_mix", "ln_post_mix", "ln_pre_ffn", "ln_post_ffn", "w_in",
             "b_forget", "sinks", "w_out", "w_gate_up", "w_down"]
    outs = [loss, grad_x]
    for kind in range(4):
        for nme in names:
            outs.append(big[kind][nme] if nme in big[kind] else small[kind][nme])
    return tuple(outs)
```

```python
import math

import numpy as np
import jax
import jax.numpy as jnp
from jax import lax
from jax.experimental import pallas as pl
from jax.experimental.pallas import tpu as pltpu

F32 = jnp.float32
BF16 = jnp.bfloat16
HIGHEST = lax.Precision.HIGHEST
MESH = pl.DeviceIdType.MESH

N_DEV = 8
D_MODEL = 1024
N_META = 16
HEAD_DIM = 64
SWA_Q_HEADS = 8
SWA_KV_HEADS = 2
SWA_GROUP = 4
FOX_HEADS = 8
FOX_W = FOX_HEADS * HEAD_DIM
SWA_Q_W = SWA_Q_HEADS * HEAD_DIM
BLOCK = 128
PAD_ROWS = BLOCK - N_META
N_BUCKETS = 32
MAX_DISTANCE = 128
D_FF = 2816
D_QKV = 2304
D_PROJ = D_QKV + FOX_HEADS
D_PROJ_PAD = 2560
EPS = 1e-6
NEG = -1e30
SCALE = HEAD_DIM ** -0.5
ADAM_LR, ADAM_B1, ADAM_B2, ADAM_EPS, ADAM_WD, ADAM_STEP = 0.001, 0.9, 0.999, 1e-08, 0.01, 10
VMEM_LIMIT = 56 * 1024 * 1024
FOX_TILE = 384
FOX_GROUP = 4
W_IN_PAD = 384

NT = (((1,), (1,)), ((), ()))
NN = (((1,), (0,)), ((), ()))
TN = (((0,), (0,)), ((), ()))


def _params(sem=None, **kw):
    if sem is not None:
        kw["dimension_semantics"] = sem
    return pltpu.CompilerParams(vmem_limit_bytes=VMEM_LIMIT, **kw)


def _tile(n, target, mult=16):
    best = None
    for t in range(mult, min(n, target) + 1, mult):
        if n % t == 0:
            best = t
    assert best is not None, (n, target)
    return best


def _matmul(a, b, *, nt=False, ta=False, out_dtype, tm, tn, tk=None, ex=None, name):
    M, K = a.shape[::-1] if ta else a.shape
    assert not (ta and nt)
    N = b.shape[0] if nt else b.shape[1]
    tk = K if tk is None else tk
    assert M % tm == 0 and N % tn == 0 and K % tk == 0, (name, a.shape, b.shape, tm, tn, tk)
    nk = K // tk
    dn = NT if nt else (TN if ta else NN)
    a_spec = pl.BlockSpec((tk, tm), lambda i, j, k: (k, i)) if ta else pl.BlockSpec((tm, tk), lambda i, j, k: (i, k))

    def body(a_ref, b_ref, o_ref, *scr):
        part = lax.dot_general(a_ref[...], b_ref[...], dn, preferred_element_type=F32)
        if nk == 1:
            o_ref[...] = part.astype(o_ref.dtype)
        else:
            acc = scr[0]
            k = pl.program_id(2)

            @pl.when(k == 0)
            def _():
                acc[...] = part

            @pl.when(k > 0)
            def _():
                acc[...] += part

            @pl.when(k == nk - 1)
            def _():
                o_ref[...] = acc[...].astype(o_ref.dtype)

    if nt:
        b_spec = pl.BlockSpec((tn, tk), lambda i, j, k: (j, k))
    else:
        b_spec = pl.BlockSpec((tk, tn), lambda i, j, k: (k, j))
    out_shape = jax.ShapeDtypeStruct((M, N), out_dtype)
    out_spec = pl.BlockSpec((tm, tn), lambda i, j, k: (i, j))
    grid = (M // tm, N // tn, nk)
    body, x_in, x_in_specs, x_out, x_out_specs, x_scr = _carry(ex, grid, 2, 1, body)
    res = pl.pallas_call(
        body,
        out_shape=(out_shape, *x_out),
        grid=grid,
        in_specs=[a_spec, b_spec] + x_in_specs,
        out_specs=(out_spec, *x_out_specs),
        scratch_shapes=([pltpu.VMEM((tm, tn), F32)] if nk > 1 else []) + x_scr,
        compiler_params=_params(("parallel", "parallel", "arbitrary") if ex is None else ("arbitrary",) * 3),
        name=name,
    )(a, b, *x_in)
    return res[0] if ex is None else res


def _rstd(x):
    return lax.rsqrt(jnp.mean(x * x, axis=-1, keepdims=True) + EPS)


def _pad_rows_rms(x, target, g, ex, *, name):
    S, D = x.shape
    nb = S // BLOCK + 1
    ni, no = len(ex.inputs), len(ex.out_shapes)
    mcols = D // N_DEV

    def body(x_ref, t_ref, g_ref, *rest):
        side_in, (h_ref, to_ref, y_ref, yt_ref) = rest[:ni], rest[ni:ni + 4]
        side_out = rest[ni + 4:ni + 4 + no]
        meta_buf, meta_sems, *sems = rest[ni + 4 + no:]
        i = pl.program_id(0)

        @pl.when(i == 0)
        def _():
            ex.start(side_in, side_out, sems)

        def norm():
            h = h_ref[...]
            y = h * _rstd(h) * g_ref[...]
            y_ref[...] = y.astype(y_ref.dtype)
            yt_ref[...] = y.T.astype(yt_ref.dtype)

        @pl.when(i < nb - 1)
        def _():
            h_ref[...] = x_ref[...]
            to_ref[...] = t_ref[...]
            norm()

        @pl.when(i == nb - 1)
        def _():
            ex.finish(side_in, side_out, sems)
            copies = [pltpu.make_async_copy(side_out[-1].at[d], meta_buf.at[:, d * mcols:(d + 1) * mcols],
                                            meta_sems.at[d]) for d in range(N_DEV)]
            for cp in copies:
                cp.start()
            for cp in copies:
                cp.wait()
            h_ref[:PAD_ROWS, :] = jnp.zeros((PAD_ROWS, D), F32)
            h_ref[PAD_ROWS:, :] = meta_buf[...]
            to_ref[...] = jnp.zeros_like(to_ref)
            norm()

    src = pl.BlockSpec((BLOCK, D), lambda i: (jnp.minimum(i, nb - 2), 0))
    dst = pl.BlockSpec((BLOCK, D), lambda i: ((i + 1) % nb, 0))
    hbm = pl.BlockSpec(memory_space=pl.ANY)
    rows = jax.ShapeDtypeStruct((BLOCK + S, D), F32)
    return pl.pallas_call(
        body,
        out_shape=(rows, rows, jax.ShapeDtypeStruct((BLOCK + S, D), BF16), jax.ShapeDtypeStruct((D, BLOCK + S), BF16),
                   *ex.out_shapes),
        grid=(nb,),
        in_specs=[src, src, pl.BlockSpec((1, D), lambda i: (0, 0))] + [hbm] * ni,
        out_specs=(dst, dst, dst, pl.BlockSpec((D, BLOCK), lambda i: (0, (i + 1) % nb)), *([hbm] * no)),
        scratch_shapes=[pltpu.VMEM((N_META, D), F32), pltpu.SemaphoreType.DMA((N_DEV,))] + list(ex.scratch),
        compiler_params=_params(("arbitrary",)), name=name)(x, target, g, *ex.inputs)


def _post_res_norm(a, g_post, h, g_pre, *, name):
    T, D = a.shape
    tm = _tile(T, 384, BLOCK)

    def body(a_ref, gp_ref, h_ref, gn_ref, h1_ref, o_ref):
        a = a_ref[...]
        h1 = h_ref[...] + a * _rstd(a) * gp_ref[...]
        h1_ref[...] = h1
        o_ref[...] = (h1 * _rstd(h1) * gn_ref[...]).astype(o_ref.dtype)

    row = pl.BlockSpec((tm, D), lambda i: (i, 0))
    vec = pl.BlockSpec((1, D), lambda i: (0, 0))
    return pl.pallas_call(
        body, out_shape=(jax.ShapeDtypeStruct((T, D), F32), jax.ShapeDtypeStruct((T, D), BF16)), grid=(T // tm,),
        in_specs=[row, vec, row, vec], out_specs=(row, row),
        compiler_params=_params(("parallel",)), name=name)(a, g_post, h, g_pre)


def _loss_head(a, g, h, target, *, name):
    T, D = a.shape
    tm = _tile(T, 512)

    def body(a_ref, g_ref, h_ref, t_ref, dy_ref, da_ref, dg_ref, loss_ref):
        i = pl.program_id(0)
        a = a_ref[...]
        r = _rstd(a)
        ah = a * r
        y = h_ref[...] + ah * g_ref[...]
        rows = i * tm + lax.broadcasted_iota(jnp.int32, (tm, 1), 0)
        err = jnp.where(rows >= BLOCK, y - t_ref[...], 0.0)
        dy = err / D
        dy_ref[...] = dy
        dah = dy * g_ref[...]
        da_ref[...] = (r * (dah - ah * jnp.mean(dah * ah, axis=-1, keepdims=True))).astype(da_ref.dtype)
        part = jnp.sum(jnp.sum(err * err, axis=1, keepdims=True), axis=0, keepdims=True)

        @pl.when(i == 0)
        def _():
            loss_ref[...] = jnp.zeros_like(loss_ref)
            dg_ref[...] = jnp.zeros_like(dg_ref)

        loss_ref[...] += jnp.broadcast_to(part, loss_ref.shape)
        dg_ref[...] += jnp.sum(dy * ah, axis=0, keepdims=True)

    row = pl.BlockSpec((tm, D), lambda i: (i, 0))
    vec = pl.BlockSpec((1, D), lambda i: (0, 0))
    return pl.pallas_call(
        body, out_shape=(jax.ShapeDtypeStruct((T, D), F32), jax.ShapeDtypeStruct((T, D), BF16),
                         jax.ShapeDtypeStruct((1, D), F32), jax.ShapeDtypeStruct((8, 128), F32)),
        grid=(T // tm,),
        in_specs=[row, vec, row, row],
        out_specs=(row, row, vec, pl.BlockSpec((8, 128), lambda i: (0, 0))),
        compiler_params=_params(("arbitrary",)), name=name)(a, g, h, target)


def _rms_pull_back(x, g, dy):
    r = _rstd(x)
    xh = x * r
    dxh = dy * g
    return r * (dxh - xh * jnp.mean(dxh * xh, axis=-1, keepdims=True)), jnp.sum(dy * xh, axis=0, keepdims=True)


def _rms_bwd_twice(x, g, dy, res, x2, g2, *, name):
    T, D = x.shape
    tm = _tile(T, 512)

    def body(x_ref, g_ref, dy_ref, res_ref, x2_ref, g2_ref, dx_ref, dg_ref, dx2_ref, dg2_ref):
        @pl.when(pl.program_id(0) == 0)
        def _():
            dg_ref[...] = jnp.zeros_like(dg_ref)
            dg2_ref[...] = jnp.zeros_like(dg2_ref)

        dx, dg = _rms_pull_back(x_ref[...], g_ref[...], dy_ref[...].astype(F32))
        dx = dx + res_ref[...]
        dx_ref[...] = dx
        dg_ref[...] += dg
        dx2, dg2 = _rms_pull_back(x2_ref[...], g2_ref[...], dx)
        dx2_ref[...] = dx2.astype(dx2_ref.dtype)
        dg2_ref[...] += dg2

    row = pl.BlockSpec((tm, D), lambda i: (i, 0))
    vec = pl.BlockSpec((1, D), lambda i: (0, 0))
    gain = jax.ShapeDtypeStruct((1, D), F32)
    return pl.pallas_call(
        body, out_shape=(jax.ShapeDtypeStruct((T, D), F32), gain, jax.ShapeDtypeStruct((T, D), BF16), gain),
        grid=(T // tm,), in_specs=[row, vec, row, row, row, vec], out_specs=(row, vec, row, vec),
        compiler_params=_params(("arbitrary",)), name=name)(x, g, dy, res, x2, g2)


def _rms_bwd_rows(x, g, dy, a, b, res, *, name):
    T, D = x.shape
    n = a.shape[1]
    n_tail = T // BLOCK - 1
    per_step = max(p for p in (4, 3, 2, 1) if n_tail % p == 0)
    steps = n_tail // per_step
    assert T == BLOCK * (1 + n_tail)
    n_rows = 4 * (per_step + 1)

    def body(*refs):
        rows, (g_ref, b_ref), (tail_ref, dg_ref, head_ref) = refs[:n_rows], refs[n_rows:n_rows + 2], refs[n_rows + 2:]

        def block(s):
            x_ref, dy_ref, a_ref, res_ref = rows[4 * s:4 * s + 4]
            dy_all = dy_ref[...] + lax.dot_general(a_ref[...], b_ref[...], NT, preferred_element_type=F32)
            dx, dg = _rms_pull_back(x_ref[...], g_ref[...], dy_all)
            return dx + res_ref[...], dg

        @pl.when(pl.program_id(0) == 0)
        def _():
            dx, dg = block(per_step)
            head_ref[...] = dx
            dg_ref[...] = dg

        for s in range(per_step):
            dx, dg = block(s)
            tail_ref[s * BLOCK:(s + 1) * BLOCK, :] = dx
            dg_ref[...] += dg

    def blocks(width):
        tail = [pl.BlockSpec((BLOCK, width), lambda i, s=s: (per_step * i + s + 1, 0)) for s in range(per_step)]
        return tail + [pl.BlockSpec((BLOCK, width), lambda i: (0, 0))]

    specs, args = [], []
    for bx, bdy, ba, bres in zip(blocks(D), blocks(D), blocks(n), blocks(D)):
        specs += [bx, bdy, ba, bres]
        args += [x, dy, a, res]
    vec = pl.BlockSpec((1, D), lambda i: (0, 0))
    return pl.pallas_call(
        body,
        out_shape=(jax.ShapeDtypeStruct((T - BLOCK, D), F32), jax.ShapeDtypeStruct((1, D), F32),
                   jax.ShapeDtypeStruct((BLOCK, D), F32)),
        grid=(steps,), in_specs=specs + [vec, pl.BlockSpec(b.shape, lambda i: (0, 0))],
        out_specs=(pl.BlockSpec((per_step * BLOCK, D), lambda i: (i, 0)), vec, pl.BlockSpec((BLOCK, D), lambda i: (0, 0))),
        compiler_params=_params(("arbitrary",)), name=name)(*args, g, b)


def _gate_up_swiglu(a, w_t, *, name):
    T, D = a.shape
    F = w_t.shape[0] // 2
    tm = _tile(T, 1408, BLOCK)
    n = _tile(F, 256, BLOCK)
    rows = 3 * BLOCK

    def body(a_ref, wg_ref, wu_ref, g_ref, u_ref, o_ref, ot_ref):
        wg, wu = wg_ref[...], wu_ref[...]
        for r in range(0, tm, rows):
            e = min(r + rows, tm)
            x = a_ref[r:e, :]
            g = lax.dot_general(x, wg, NT, preferred_element_type=F32)
            u = lax.dot_general(x, wu, NT, preferred_element_type=F32)
            g16, u16 = g.astype(BF16), u.astype(BF16)
            g_ref[r:e, :] = g16
            u_ref[r:e, :] = u16
            gr = g16.astype(F32)
            act = gr / (1.0 + jnp.exp(-gr)) * u16.astype(F32)
            o_ref[r:e, :] = act.astype(o_ref.dtype)
            ot_ref[:, r:e] = act.T.astype(ot_ref.dtype)

    tile = pl.BlockSpec((tm, n), lambda i, j: (i, j))
    shp = jax.ShapeDtypeStruct((T, F), BF16)
    return pl.pallas_call(
        body, out_shape=(shp, shp, shp, jax.ShapeDtypeStruct((F, T), BF16)), grid=(T // tm, F // n),
        in_specs=[pl.BlockSpec((tm, D), lambda i, j: (i, 0)),
                  pl.BlockSpec((n, D), lambda i, j: (j, 0)),
                  pl.BlockSpec((n, D), lambda i, j: (j + F // n, 0))],
        out_specs=(tile, tile, tile, pl.BlockSpec((n, tm), lambda i, j: (j, i))),
        compiler_params=_params(("parallel", "parallel")), name=name)(a, w_t, w_t)


def _d_act_swiglu(dff, w_down, gate, up, *, name):
    T, D = dff.shape
    F = w_down.shape[0]
    tm = _tile(T, 384)
    chunk = 768
    assert F % BLOCK == 0

    def body(d_ref, w_ref, g_ref, u_ref, o_ref):
        dy = d_ref[...]
        for c in range(0, F, chunk):
            e = min(c + chunk, F)
            d = lax.dot_general(dy, w_ref[c:e, :], NT, preferred_element_type=F32)
            g = g_ref[:, c:e].astype(F32)
            u = u_ref[:, c:e].astype(F32)
            sg = 1.0 / (1.0 + jnp.exp(-g))
            o_ref[:, c:e] = (d * u * (sg * (1.0 + g * (1.0 - sg)))).astype(o_ref.dtype)
            o_ref[:, F + c:F + e] = (d * (g * sg)).astype(o_ref.dtype)

    row = pl.BlockSpec((tm, F), lambda i: (i, 0))
    return pl.pallas_call(
        body, out_shape=jax.ShapeDtypeStruct((T, 2 * F), BF16), grid=(T // tm,),
        in_specs=[pl.BlockSpec((tm, D), lambda i: (i, 0)), pl.BlockSpec((F, D), lambda i: (0, 0)), row, row],
        out_specs=pl.BlockSpec((tm, 2 * F), lambda i: (i, 0)),
        compiler_params=_params(("parallel",)), name=name)(dff, w_down, gate, up)


def _fox_gates_fwd(f_t, b, *, name):
    H, T = f_t.shape
    nb = T // BLOCK

    def body(f_ref, b_ref, col_ref):
        f = f_ref[...] + b_ref[...]
        ls = jnp.minimum(f, 0.0) - jnp.log(1.0 + jnp.exp(-jnp.abs(f)))
        t = lax.broadcasted_iota(jnp.int32, (H, T), 1)
        ls = jnp.where(t >= PAD_ROWS, ls, 0.0)
        upper = (lax.broadcasted_iota(jnp.int32, (BLOCK, BLOCK), 0)
                 <= lax.broadcasted_iota(jnp.int32, (BLOCK, BLOCK), 1)).astype(F32)
        carry = jnp.zeros((H, 1), F32)
        for blk in range(nb):
            seg = ls[:, blk * BLOCK:(blk + 1) * BLOCK]
            pre = jnp.dot(seg, upper, precision=HIGHEST, preferred_element_type=F32) + carry
            key_gate = jnp.where(t[:, blk * BLOCK:(blk + 1) * BLOCK] >= PAD_ROWS, pre, -NEG)
            terms = list(_split3(pre)) + list(_split3(key_gate))
            col_ref[blk * BLOCK:(blk + 1) * BLOCK, :] = jnp.concatenate(
                terms + [jnp.zeros((BLOCK - len(terms) * H, BLOCK), F32)], axis=0).T.astype(col_ref.dtype)
            carry = pre[:, BLOCK - 1:BLOCK]

    vm = pl.BlockSpec(memory_space=pltpu.VMEM)
    return pl.pallas_call(
        body, out_shape=jax.ShapeDtypeStruct((T, BLOCK), BF16),
        in_specs=[vm, vm], out_specs=vm,
        compiler_params=_params(), name=name)(f_t, b)


def _fox_gates_bwd(dcq, dck, f_t, b, *, name):
    H, T = f_t.shape
    nb = T // BLOCK

    def body(dq_ref, d_ref, f_ref, b_ref, df_ref, db_ref):
        lower = (lax.broadcasted_iota(jnp.int32, (BLOCK, BLOCK), 0)
                 >= lax.broadcasted_iota(jnp.int32, (BLOCK, BLOCK), 1)).astype(F32)
        carry = jnp.zeros((H, 1), F32)
        for blk in range(nb - 1, -1, -1):
            seg = dq_ref[:, blk * BLOCK:(blk + 1) * BLOCK] - d_ref[:, blk * BLOCK:(blk + 1) * BLOCK]
            suf = jnp.dot(seg, lower, precision=HIGHEST, preferred_element_type=F32) + carry
            df_ref[:, blk * BLOCK:(blk + 1) * BLOCK] = suf
            carry = suf[:, 0:1]
        f = f_ref[...] + b_ref[...]
        t = lax.broadcasted_iota(jnp.int32, (H, T), 1)
        df = jnp.where(t >= PAD_ROWS, df_ref[...] / (1.0 + jnp.exp(f)), 0.0)
        df_ref[...] = df
        db_ref[...] = jnp.sum(df, axis=1, keepdims=True)

    vm = pl.BlockSpec(memory_space=pltpu.VMEM)
    return pl.pallas_call(
        body, out_shape=(jax.ShapeDtypeStruct((H, T), F32), jax.ShapeDtypeStruct((H, 1), F32)),
        in_specs=[vm, vm, vm, vm], out_specs=(vm, vm),
        compiler_params=_params(), name=name)(dcq, dck, f_t, b)


def _fox_lanes(parity):
    base = HEAD_DIM * (1 - parity)
    return base, base + 3


def _split3(c):
    hi = c.astype(BF16).astype(F32)
    r = c - hi
    mid = r.astype(BF16).astype(F32)
    lo = (r - mid).astype(BF16).astype(F32)
    return hi, mid, lo


def _lanes(lane, parity, data, start, terms, ones_at=None, fill=1.0):
    out = jnp.zeros((), F32) if ones_at is None else jnp.where((lane >= ones_at) & (lane < ones_at + 3), fill, 0.0)
    for i, t in enumerate(terms):
        out = jnp.where(lane == start + i, t, out)
    return jnp.where(lane // HEAD_DIM == parity, data, out)


def _fox_prep(proj, cum_col, *, name):
    T = proj.shape[0]
    tm = _tile(T, 1408, BLOCK)
    nt = T // tm
    H = FOX_HEADS
    lanes = 2 * HEAD_DIM
    first = (proj.shape[1] - 3 * H * HEAD_DIM) // lanes

    def body(q_ref, k_ref, v_ref, c_ref, qa_ref, ka_ref, va_ref):
        p = pl.program_id(0)
        i = pl.program_id(1)
        lane = lax.broadcasted_iota(jnp.int32, (1, lanes), 1)
        src = lax.broadcasted_iota(jnp.int32, (lanes, lanes), 0)
        dst = lax.broadcasted_iota(jnp.int32, (lanes, lanes), 1)
        q2 = q_ref[...].astype(F32) * SCALE
        k2 = k_ref[...].astype(F32)
        v2 = v_ref[...].astype(F32)
        gates = c_ref[...]
        def placed(h, first_term, start):
            pick = ((src % FOX_HEADS == h) & (src // FOX_HEADS - first_term == dst - start)
                    & (dst >= start) & (dst < start + 3))
            return jnp.dot(gates, pick.astype(BF16), preferred_element_type=F32)

        moved = [(placed(2 * p + e, 0, _fox_lanes(e)[1]), placed(2 * p + e, 3, _fox_lanes(e)[0])) for e in range(2)]
        for e in range(2):
            kc, qc = _fox_lanes(e)
            own = lane // HEAD_DIM == e
            minus = jnp.where((lane >= kc) & (lane < kc + 3), -1.0, 0.0)
            ones_q = jnp.where((lane >= qc) & (lane < qc + 3), 1.0, 0.0)
            ones_k = jnp.where((lane >= kc) & (lane < kc + 3), 1.0, 0.0)
            qa_ref[e] = jnp.where(own, q2, moved[e][0] + minus).astype(BF16)
            ka_ref[e] = jnp.where(own, k2, moved[e][1] + ones_q).astype(BF16)
            va_ref[e] = jnp.where(own, v2, ones_k).astype(BF16)

    pairs = FOX_GROUP // 2

    def col(part):
        return pl.BlockSpec((tm, lanes),
                            lambda p, i: (i, first + 3 * pairs * (p // pairs) + part * pairs + p % pairs))

    out = pl.BlockSpec((2, tm, lanes), lambda p, i: (p, i, 0))
    shp = jax.ShapeDtypeStruct((H, T, lanes), BF16)
    return pl.pallas_call(
        body, out_shape=(shp, shp, shp), grid=(H // 2, nt),
        in_specs=[col(0), col(1), col(2), pl.BlockSpec((tm, lanes), lambda p, i: (i, 0))],
        out_specs=(out, out, out),
        compiler_params=_params(("parallel", "parallel")), name=name)(proj, proj, proj, cum_col)


def _fox_fwd(q_aug, k_aug, v_aug, mix, *, ex=None, name):
    H, T, lanes = q_aug.shape
    tq = FOX_TILE
    nq = T // tq
    G = FOX_HEADS

    def body(q_ref, k_ref, v_ref, mix_ref, o_ref, lse_ref, m_scr, acc_scr):
        i = pl.program_id(1)
        m_scr[...] = jnp.full(m_scr.shape, NEG, F32)
        acc_scr[...] = jnp.zeros(acc_scr.shape, F32)

        def step(kb, diag):
            off = pl.multiple_of(kb * tq, tq)
            s_t = [lax.dot_general(k_ref[g, pl.ds(off, tq), :], q_ref[g], NT, preferred_element_type=F32)
                   for g in range(G)]
            if diag:
                r = lax.broadcasted_iota(jnp.int32, (tq, tq), 0)
                c = lax.broadcasted_iota(jnp.int32, (tq, tq), 1)
                s_t = [jnp.where(c >= r, s, NEG) for s in s_t]
            m_prev = [m_scr[g] for g in range(G)]
            m_new = [jnp.maximum(m_prev[g], jnp.max(s_t[g], axis=0, keepdims=True)) for g in range(G)]
            p_t = [jnp.exp(s_t[g] - m_new[g]).astype(BF16) for g in range(G)]
            pv = [lax.dot_general(v_ref[g, pl.ds(off, tq), :], p_t[g], TN, preferred_element_type=F32)
                  for g in range(G)]
            for g in range(G):
                acc_scr[g] = jnp.exp(m_prev[g] - m_new[g]) * acc_scr[g] + pv[g]
                m_scr[g] = m_new[g]

        def loop_body(kb, carry):
            step(kb, False)
            return carry

        lax.fori_loop(0, i, loop_body, 0)
        step(i, True)
        lane = lax.broadcasted_iota(jnp.int32, (tq, lanes), 1)
        outs = []
        for g in range(G):
            ones = _fox_lanes(g % 2)[0]
            acc = acc_scr[g]
            lse_ref[g] = m_scr[g] + jnp.log(acc[ones:ones + 1, :])
            acc_t = acc.T
            outs.append(acc_t / acc_t[:, ones:ones + 1])
        for pair in range(G // 2):
            o_ref[:, pair * lanes:(pair + 1) * lanes] = jnp.where(
                lane < HEAD_DIM, outs[2 * pair], outs[2 * pair + 1]).astype(o_ref.dtype)

    blk = pl.BlockSpec((G, tq, lanes), lambda h, i: (h, i, 0))
    full = pl.BlockSpec((G, T, lanes), lambda h, i: (h, 0, 0))
    grid = (H // G, nq)
    first = mix.shape[1] // (G * HEAD_DIM) - H // G
    body, x_in, x_in_specs, x_out, x_out_specs, x_scr = _carry(ex, grid, 4, 2, body)
    return pl.pallas_call(
        body,
        out_shape=(jax.ShapeDtypeStruct(mix.shape, mix.dtype), jax.ShapeDtypeStruct((H, nq, 1, tq), F32), *x_out),
        grid=grid,
        in_specs=[blk, full, full, pl.BlockSpec(memory_space=pl.ANY)] + x_in_specs,
        out_specs=(pl.BlockSpec((tq, G * HEAD_DIM), lambda h, i: (i, first + h)),
                   pl.BlockSpec((G, None, 1, tq), lambda h, i: (h, i, 0, 0)), *x_out_specs),
        input_output_aliases={3: 0},
        scratch_shapes=[pltpu.VMEM((G, 1, tq), F32), pltpu.VMEM((G, lanes, tq), F32)] + x_scr,
        compiler_params=_params(("arbitrary", "arbitrary")), name=name)(q_aug, k_aug, v_aug, mix, *x_in)


def _fox_prep_bwd(dmix, mix, *, name):
    T = dmix.shape[0]
    H = FOX_HEADS
    tm = _tile(T, 1408, BLOCK)
    lanes = 2 * HEAD_DIM
    first = mix.shape[1] // lanes - H // 2

    def body(d_ref, o_ref, da_ref):
        lane = lax.broadcasted_iota(jnp.int32, (1, lanes), 1)
        d2 = d_ref[...].astype(F32)
        prod = d2 * o_ref[...].astype(F32)
        for e in range(2):
            delta = jnp.sum(jnp.where(lane // HEAD_DIM == e, prod, 0.0), axis=1, keepdims=True)
            da_ref[e] = _lanes(lane, e, d2, _fox_lanes(e)[0], _split3(-delta)).astype(BF16)

    pair = pl.BlockSpec((tm, lanes), lambda p, i: (i, first + p))
    return pl.pallas_call(
        body, out_shape=jax.ShapeDtypeStruct((H, T, lanes), BF16), grid=(H // 2, T // tm),
        in_specs=[pair, pair],
        out_specs=pl.BlockSpec((2, tm, lanes), lambda p, i: (p, i, 0)),
        compiler_params=_params(("parallel", "parallel")), name=name)(dmix, mix)


def _fox_bwd(q_aug, k_aug, v_aug, do_aug, lse_row, dproj, *, ex=None, name):
    H, T, lanes = q_aug.shape
    tq = FOX_TILE
    nq = T // tq
    G = FOX_GROUP

    def side_by_side(tiles, scale=None):
        lane = lax.broadcasted_iota(jnp.int32, tiles[0].shape, 1)
        out = [jnp.where(lane < HEAD_DIM, tiles[2 * p], tiles[2 * p + 1]) for p in range(G // 2)]
        out = jnp.concatenate(out, axis=1)
        return out if scale is None else out * scale

    def body(q_ref, k_ref, v_ref, do_ref, lse_ref, dproj_in, out_ref, dcq_ref, dck_ref, dk_acc, dv_acc, dq_ref):
        j = pl.program_id(1)

        @pl.when(j == 0)
        def _():
            dq_ref[...] = jnp.zeros(dq_ref.shape, F32)
            dcq_ref[...] = jnp.zeros(dcq_ref.shape, F32)

        dk_acc[...] = jnp.zeros(dk_acc.shape, F32)
        dv_acc[...] = jnp.zeros(dv_acc.shape, F32)

        def step(qb, diag):
            off = pl.multiple_of(qb * tq, tq)
            heads = range(G)
            qa = [q_ref[g, pl.ds(off, tq), :] for g in heads]
            da = [do_ref[g, pl.ds(off, tq), :] for g in heads]
            s_t = [lax.dot_general(k_ref[g], qa[g], NT, preferred_element_type=F32) for g in heads]
            dp_t = [lax.dot_general(v_ref[g], da[g], NT, preferred_element_type=F32) for g in heads]
            p_t = [jnp.exp(s_t[g] - lse_ref[g, qb]) for g in heads]
            if diag:
                r = lax.broadcasted_iota(jnp.int32, (tq, tq), 0)
                c = lax.broadcasted_iota(jnp.int32, (tq, tq), 1)
                p_t = [jnp.where(c >= r, p, 0.0) for p in p_t]
            dsb = [(p_t[g] * dp_t[g]).astype(BF16) for g in heads]
            dv = [jnp.dot(p_t[g].astype(BF16), da[g], preferred_element_type=F32) for g in heads]
            dk = [jnp.dot(dsb[g], qa[g], preferred_element_type=F32) for g in heads]
            dq = [lax.dot_general(k_ref[g], dsb[g], TN, preferred_element_type=F32) for g in heads]
            for g in heads:
                dv_acc[g] += dv[g]
                dk_acc[g] += dk[g]
                dq_ref[g, qb] += dq[g]
                dcq_ref[g, qb] += jnp.sum(dsb[g].astype(F32), axis=0, keepdims=True)

        step(j, True)

        def loop_body(qb, carry):
            step(qb, False)
            return carry

        lax.fori_loop(j + 1, nq, loop_body, 0)
        dk = [dk_acc[g] for g in range(G)]
        out_ref[:, 0:wide] = side_by_side([dq_ref[g, j].T for g in range(G)], SCALE).astype(out_ref.dtype)
        out_ref[:, wide:2 * wide] = side_by_side(dk).astype(out_ref.dtype)
        out_ref[:, 2 * wide:3 * wide] = side_by_side([dv_acc[g] for g in range(G)]).astype(out_ref.dtype)
        for g in range(G):
            kc = _fox_lanes(g % 2)[0]
            dck_ref[g] = -dk[g].T[kc:kc + 1, :]

    blk = pl.BlockSpec((G, tq, lanes), lambda h, j: (h, j, 0))
    full = pl.BlockSpec((G, T, lanes), lambda h, j: (h, 0, 0))
    wide = G * HEAD_DIM
    first = dproj.shape[1] // (3 * wide) - H // G
    grid = (H // G, nq)
    body, x_in, x_in_specs, x_out, x_out_specs, x_scr = _carry(ex, grid, 6, 3, body)
    rows = jax.ShapeDtypeStruct((H, nq, 1, tq), F32)
    all_rows = pl.BlockSpec((G, nq, 1, tq), lambda h, j: (h, 0, 0, 0))
    return pl.pallas_call(
        body,
        out_shape=(jax.ShapeDtypeStruct(dproj.shape, dproj.dtype), rows, rows, *x_out),
        grid=grid,
        in_specs=[full, blk, blk, full, all_rows, pl.BlockSpec(memory_space=pl.ANY)] + x_in_specs,
        out_specs=(pl.BlockSpec((tq, 3 * wide), lambda h, j: (j, first + h)), all_rows,
                   pl.BlockSpec((G, None, 1, tq), lambda h, j: (h, j, 0, 0)), *x_out_specs),
        input_output_aliases={5: 0},
        scratch_shapes=[pltpu.VMEM((G, tq, lanes), F32), pltpu.VMEM((G, tq, lanes), F32),
                        pltpu.VMEM((G, nq, lanes, tq), F32)] + x_scr,
        compiler_params=_params(("arbitrary", "arbitrary")), name=name,
    )(q_aug, k_aug, v_aug, do_aug, lse_row, dproj, *x_in)


def _t5_bucket_np(d):
    n = np.maximum(d, 0).astype(np.int32)
    max_exact = N_BUCKETS // 2
    nf = np.maximum(n, 1).astype(np.float32)
    large = max_exact + (np.log(nf / max_exact) / math.log(MAX_DISTANCE / max_exact)
                         * (N_BUCKETS - max_exact)).astype(np.int32)
    large = np.minimum(large, N_BUCKETS - 1)
    return np.where(n < max_exact, n, large)


def _bucket_onehots():
    k = np.arange(BLOCK)[:, None]
    q = np.arange(BLOCK)[None, :]
    eye = np.eye(N_BUCKETS, dtype=np.float32)
    cur = eye[_t5_bucket_np(q - k).reshape(-1)]
    prev = eye[_t5_bucket_np(BLOCK + q - k).reshape(-1)]
    return cur, prev


SWA_K_COL = SWA_Q_HEADS * HEAD_DIM // (2 * HEAD_DIM)
SWA_V_COL = SWA_K_COL + 1


def _swa_terms(raw, bc, bp, far, sink, n):
    k = lax.broadcasted_iota(jnp.int32, (BLOCK, BLOCK), 0)
    q = lax.broadcasted_iota(jnp.int32, (BLOCK, BLOCK), 1)
    never = 2 * BLOCK
    s_c = raw[0] + bc
    s_p = raw[1] + bp
    s_m = raw[2] + jnp.where(n == 1, bp, far)
    s_c = jnp.where((k <= q) & (k >= jnp.where(n >= 1, 0, PAD_ROWS)), s_c, NEG)
    s_p = jnp.where(k > q + jnp.where(n >= 2, 0, never), s_p, NEG)
    s_m = jnp.where(k >= jnp.where(n >= 1, PAD_ROWS, never), s_m, NEG)
    m = jnp.maximum(jnp.maximum(jnp.max(s_c, axis=0, keepdims=True), jnp.max(s_p, axis=0, keepdims=True)),
                    jnp.maximum(jnp.max(s_m, axis=0, keepdims=True), sink))
    e = [jnp.exp(s_c - m), jnp.exp(s_p - m), jnp.exp(s_m - m)]
    e_s = jnp.exp(sink - m)
    l = (jnp.sum(e[0], axis=0, keepdims=True) + jnp.sum(e[1], axis=0, keepdims=True)
         + jnp.sum(e[2], axis=0, keepdims=True) + e_s)
    return e, e_s, l


SWA_STEP = 3


def _swa_specs():
    R = SWA_STEP

    def window(col):
        return ([pl.BlockSpec((BLOCK, BLOCK), lambda s, w=w: (jnp.maximum(R * s - 1 + w, 0), col)) for w in range(R + 1)]
                + [pl.BlockSpec((BLOCK, BLOCK), lambda s: (0, col))])

    qblk = pl.BlockSpec((R * BLOCK, SWA_Q_HEADS * HEAD_DIM), lambda s: (s, 0))
    bias = pl.BlockSpec((SWA_Q_HEADS, BLOCK, BLOCK), lambda s: (0, 0, 0))
    smem = pl.BlockSpec(memory_space=pltpu.SMEM)
    return qblk, window(SWA_K_COL), window(SWA_V_COL), bias, smem


def _swa_own_kv(tile_ref, kv):
    lane = lax.broadcasted_iota(jnp.int32, (BLOCK, 2 * HEAD_DIM), 1)
    t = tile_ref[...].astype(F32)
    return jnp.where(lane // HEAD_DIM == kv, t, pltpu.roll(t, HEAD_DIM, 1)).astype(BF16)


def _swa_fwd(proj, bc, bp, far, sinks, *, name):
    T = proj.shape[0]
    nb = T // BLOCK
    G = SWA_GROUP
    Hq = SWA_Q_HEADS
    lanes = 2 * HEAD_DIM

    R = SWA_STEP
    assert nb % R == 0

    def body(*refs):
        q_ref, k_refs, v_refs = refs[0], refs[1:R + 3], refs[R + 3:2 * R + 5]
        bc_ref, bp_ref, far_ref, sink_ref, o_ref = refs[2 * R + 5:]
        s = pl.program_id(0)
        lane = lax.broadcasted_iota(jnp.int32, (BLOCK, lanes), 1)
        kvs = range(SWA_KV_HEADS)
        kk = [[_swa_own_kv(ref, kv) for ref in k_refs] for kv in kvs]
        vv = [[_swa_own_kv(ref, kv) for ref in v_refs] for kv in kvs]
        chains = [(r, h) for r in range(R) for h in range(Hq)]
        tiles = lambda r: (r + 1, r, R + 1)
        q2 = {(r, pair): q_ref[r * BLOCK:(r + 1) * BLOCK, pair * lanes:(pair + 1) * lanes].astype(F32) * SCALE
              for r in range(R) for pair in range(Hq // 2)}
        qm = {c: jnp.where(lane // HEAD_DIM == c[1] % 2, q2[c[0], c[1] // 2], 0.0).astype(BF16) for c in chains}
        raw = {c: [lax.dot_general(kk[c[1] // G][w], qm[c], NT, preferred_element_type=F32) for w in tiles(c[0])]
               for c in chains}
        terms = {c: _swa_terms(raw[c], bc_ref[c[1]], bp_ref[c[1]], far_ref[c[1]], sink_ref[c[1]], R * s + c[0])
                 for c in chains}
        o_t = {c: sum(lax.dot_general(vv[c[1] // G][w], terms[c][0][b].astype(BF16), TN, preferred_element_type=F32)
                      for b, w in enumerate(tiles(c[0]))) for c in chains}
        outs = {c: (o_t[c] / terms[c][2]).T for c in chains}
        for r in range(R):
            for pair in range(Hq // 2):
                o_ref[r * BLOCK:(r + 1) * BLOCK, pair * lanes:(pair + 1) * lanes] = jnp.where(
                    lane < HEAD_DIM, outs[r, 2 * pair], outs[r, 2 * pair + 1]).astype(o_ref.dtype)

    qblk, keys, vals, bias, smem = _swa_specs()
    return pl.pallas_call(
        body, out_shape=jax.ShapeDtypeStruct((T, D_MODEL), BF16), grid=(nb // R,),
        in_specs=[qblk] + keys + vals + [bias, bias, smem, smem],
        out_specs=qblk,
        compiler_params=_params(("parallel",)), name=name,
    )(proj, *([proj] * (2 * R + 4)), bc, bp, far, sinks)


def _swa_bwd(proj, dmix, bc, bp, far, sinks, *, ex=None, name):
    T, width = proj.shape
    nb = T // BLOCK
    G = SWA_GROUP
    Hq = SWA_Q_HEADS
    lanes = 2 * HEAD_DIM
    qw = Hq * HEAD_DIM
    own_w = qw + 2 * lanes

    R = SWA_STEP
    assert nb % R == 0
    n_in = 2 * R + 10

    def body(*refs):
        q_ref, k_refs, v_refs = refs[0], refs[1:R + 3], refs[R + 3:2 * R + 5]
        do_ref, bc_ref, bp_ref, far_ref, sink_ref = refs[2 * R + 5:n_in]
        dp_ref, dbc_ref, dbp_ref, dbf_ref, dsk_ref, dk_acc, dv_acc = refs[n_in:]
        s = pl.program_id(0)

        @pl.when(s == 0)
        def _():
            for ref in (dk_acc, dv_acc, dbc_ref, dbp_ref, dbf_ref, dsk_ref):
                ref[...] = jnp.zeros(ref.shape, F32)

        lane = lax.broadcasted_iota(jnp.int32, (BLOCK, lanes), 1)
        kvs = range(SWA_KV_HEADS)
        kk = [[_swa_own_kv(ref, kv) for ref in k_refs] for kv in kvs]
        vv = [[_swa_own_kv(ref, kv) for ref in v_refs] for kv in kvs]
        chains = [(r, h) for r in range(R) for h in range(Hq)]
        blocks = range(3)
        tiles = lambda r: (r + 1, r, R + 1)
        sub = lambda ref, r, pair: ref[r * BLOCK:(r + 1) * BLOCK, pair * lanes:(pair + 1) * lanes]
        q2 = {(r, pair): sub(q_ref, r, pair).astype(F32) * SCALE for r in range(R) for pair in range(Hq // 2)}
        d2 = {(r, pair): sub(do_ref, r, pair) for r in range(R) for pair in range(Hq // 2)}
        own = [lane // HEAD_DIM == half for half in range(2)]
        qm = {c: jnp.where(own[c[1] % 2], q2[c[0], c[1] // 2], 0.0).astype(BF16) for c in chains}
        dom = {c: jnp.where(own[c[1] % 2], d2[c[0], c[1] // 2], jnp.zeros_like(d2[0, 0])) for c in chains}
        raw = {c: [lax.dot_general(kk[c[1] // G][w], qm[c], NT, preferred_element_type=F32) for w in tiles(c[0])]
               for c in chains}
        dp = {c: [lax.dot_general(vv[c[1] // G][w], dom[c], NT, preferred_element_type=F32) for w in tiles(c[0])]
              for c in chains}
        p, ds16 = {}, {}
        for c in chains:
            r, h = c
            n = R * s + r
            e, e_s, l = _swa_terms(raw[c], bc_ref[h], bp_ref[h], far_ref[h], sink_ref[h], n)
            inv = 1.0 / l
            ph = [e[b] * inv for b in blocks]
            delta = sum(jnp.sum(ph[b] * dp[c][b], axis=0, keepdims=True) for b in blocks)
            ds = [ph[b] * (dp[c][b] - delta) for b in blocks]
            dsk_ref[h] += -(e_s * inv) * delta
            dbc_ref[h] += ds[0]
            dbp_ref[h] += ds[1] + jnp.where(n == 1, ds[2], 0.0)
            dbf_ref[h] += jnp.where(n >= 2, ds[2], 0.0)
            p[c] = [x.astype(BF16) for x in ph]
            ds16[c] = [x.astype(BF16) for x in ds]
        dq_t = {c: sum(lax.dot_general(kk[c[1] // G][w], ds16[c][b], TN, preferred_element_type=F32)
                       for b, w in enumerate(tiles(c[0]))) for c in chains}
        group = [range(kv * G, (kv + 1) * G) for kv in kvs]
        dk = {(r, kv): [sum(jnp.dot(ds16[r, h][b], qm[r, h], preferred_element_type=F32) for h in group[kv])
                        for b in blocks] for r in range(R) for kv in kvs}
        dv = {(r, kv): [sum(jnp.dot(p[r, h][b], dom[r, h], preferred_element_type=F32) for h in group[kv])
                        for b in blocks] for r in range(R) for kv in kvs}
        for r in range(R):
            n = R * s + r
            rows = pl.ds(pl.multiple_of(n * BLOCK, BLOCK), BLOCK)
            prev_rows = pl.ds(pl.multiple_of(jnp.maximum(n - 1, 0) * BLOCK, BLOCK), BLOCK)
            for pair in range(Hq // 2):
                dp_ref[rows, pair * lanes:(pair + 1) * lanes] = (jnp.where(
                    lane < HEAD_DIM, dq_t[r, 2 * pair].T, dq_t[r, 2 * pair + 1].T) * SCALE).astype(dp_ref.dtype)
            for acc, ref in ((dk, dk_acc), (dv, dv_acc)):
                tot = [[a + pltpu.roll(a, HEAD_DIM, 1) for a in acc[r, kv]] for kv in kvs]
                both = [jnp.where(lane < HEAD_DIM, tot[0][b], tot[1][b]) for b in blocks]
                ref[rows, :] += both[0]
                ref[prev_rows, :] += both[1]
                ref[0:BLOCK, :] += both[2]

        @pl.when(s == nb // R - 1)
        def _():
            dp_ref[:, qw:qw + lanes] = dk_acc[...].astype(dp_ref.dtype)
            dp_ref[:, qw + lanes:own_w] = dv_acc[...].astype(dp_ref.dtype)

    qblk, keys, vals, bias, smem = _swa_specs()
    dsk = pl.BlockSpec((Hq, 1, BLOCK), lambda s: (0, 0, 0))
    grid = (nb // R,)
    body, x_in, x_in_specs, x_out, x_out_specs, x_scr = _carry(ex, grid, n_in, 5, body)
    tile = jax.ShapeDtypeStruct((Hq, BLOCK, BLOCK), F32)
    return pl.pallas_call(
        body,
        out_shape=(jax.ShapeDtypeStruct((T, width), BF16), tile, tile, tile,
                   jax.ShapeDtypeStruct((Hq, 1, BLOCK), F32), *x_out),
        grid=grid,
        in_specs=[qblk] + keys + vals + [qblk, bias, bias, smem, smem] + x_in_specs,
        out_specs=(pl.BlockSpec((T, own_w), lambda s: (0, 0)), bias, bias, bias, dsk, *x_out_specs),
        scratch_shapes=[pltpu.VMEM((T, lanes), F32), pltpu.VMEM((T, lanes), F32)] + x_scr,
        compiler_params=_params(("arbitrary",)), name=name,
    )(proj, *([proj] * (2 * R + 4)), dmix, bc, bp, far, sinks, *x_in)


def _bias_tiles(tab_t, oh_cur_t, oh_prev_t, *, name):
    Hq = tab_t.shape[0]

    def body(t_ref, oc_ref, op_ref, bc_ref, bp_ref):
        bc_ref[...] = jnp.dot(t_ref[...], oc_ref[...], precision=HIGHEST, preferred_element_type=F32)
        bp_ref[...] = jnp.dot(t_ref[...], op_ref[...], precision=HIGHEST, preferred_element_type=F32)

    vm = pl.BlockSpec(memory_space=pltpu.VMEM)
    shp = jax.ShapeDtypeStruct((Hq, BLOCK * BLOCK), F32)
    bc, bp = pl.pallas_call(body, out_shape=(shp, shp), in_specs=[vm] * 3, out_specs=(vm, vm),
                            compiler_params=_params(), name=name)(tab_t, oh_cur_t, oh_prev_t)
    return bc.reshape(Hq, BLOCK, BLOCK), bp.reshape(Hq, BLOCK, BLOCK)


def _small_grads(dbc, dbp, dbf, dsk, oh_cur, oh_prev, *, ex=None, name):
    Hq = dbc.shape[0]

    def body(dbc_ref, dbp_ref, dbf_ref, dsk_ref, oc_ref, op_ref, tab_ref, sink_ref):
        tab = (jnp.dot(dbc_ref[...], oc_ref[...], precision=HIGHEST, preferred_element_type=F32)
               + jnp.dot(dbp_ref[...], op_ref[...], precision=HIGHEST, preferred_element_type=F32))
        far = jnp.sum(dbf_ref[...], axis=1, keepdims=True)
        last = lax.broadcasted_iota(jnp.int32, (Hq, N_BUCKETS), 1) == N_BUCKETS - 1
        tab_ref[...] = tab + jnp.where(last, far, 0.0)
        sink_ref[...] = jnp.sum(dsk_ref[...], axis=1, keepdims=True)

    vm = pl.BlockSpec(memory_space=pltpu.VMEM)
    body, x_in, x_in_specs, x_out, x_out_specs, x_scr = _carry(ex, (), 6, 2, body)
    return pl.pallas_call(
        body, out_shape=(jax.ShapeDtypeStruct((Hq, N_BUCKETS), F32), jax.ShapeDtypeStruct((Hq, 1), F32), *x_out),
        in_specs=[vm] * 6 + x_in_specs, out_specs=(vm, vm, *x_out_specs), scratch_shapes=x_scr,
        compiler_params=_params(), name=name,
    )(dbc.reshape(Hq, -1), dbp.reshape(Hq, -1), dbf.reshape(Hq, -1), dsk.reshape(Hq, -1), oh_cur, oh_prev, *x_in)


def _coords():
    return lax.axis_index("x"), lax.axis_index("y"), lax.axis_index("c")


class _Exchange:
    def __init__(self, inputs, out_shapes, scratch, start, finish):
        self.inputs, self.out_shapes, self.scratch, self.start, self.finish = inputs, out_shapes, scratch, start, finish


def _carry(ex, grid, n_in, n_out, body):
    if ex is None:
        return body, [], [], [], [], []
    ni, no = len(ex.inputs), len(ex.out_shapes)

    def at_step(which):
        cond = jnp.bool_(True)
        for axis, n in enumerate(grid):
            cond = cond & (pl.program_id(axis) == (0 if which == "first" else n - 1))
        return cond

    def wrapped(*refs):
        refs = list(refs)
        n_own_scr = len(refs) - (n_in + ni + n_out + no) - len(ex.scratch)
        own_in, side_in = refs[:n_in], refs[n_in:n_in + ni]
        own_out = refs[n_in + ni:n_in + ni + n_out]
        side_out = refs[n_in + ni + n_out:n_in + ni + n_out + no]
        rest = refs[n_in + ni + n_out + no:]
        own_scr, sems = rest[:n_own_scr], rest[n_own_scr:]

        @pl.when(at_step("first"))
        def _():
            ex.start(side_in, side_out, sems)

        body(*own_in, *own_out, *own_scr)

        @pl.when(at_step("last"))
        def _():
            ex.finish(side_in, side_out, sems)

    hbm = pl.BlockSpec(memory_space=pl.ANY)
    return wrapped, list(ex.inputs), [hbm] * ni, list(ex.out_shapes), [hbm] * no, list(ex.scratch)


def _gather_exchange(shards):
    nt = len(shards)

    def copies(ins, outs, sems):
        send_sems, recv_sems, local_sems = sems
        x, y, c = _coords()
        me, sibling = (x, y, c), (x, y, 1 - c)
        chips = [(1 - x, y), (x, 1 - y), (1 - x, 1 - y)]

        def slot(t, dev):
            return outs[t].at[4 * dev[0] + 2 * dev[1] + dev[2]]

        def copy(t, k, block, to, src=None):
            dst = slot(t, block)
            return pltpu.make_async_remote_copy(
                src_ref=dst if src is None else src, dst_ref=dst,
                send_sem=send_sems.at[t, k], recv_sem=recv_sems.at[t, k], device_id=to, device_id_type=MESH)

        mine = [pltpu.make_async_copy(ins[t], slot(t, me), local_sems.at[t]) for t in range(nt)]
        first = []
        for t in range(nt):
            first.append(copy(t, 0, me, sibling, src=ins[t]))
            first += [copy(t, 1 + j, me, (*chip, c), src=ins[t]) for j, chip in enumerate(chips)]
        return copy, mine, first, me, sibling, chips, c

    def start(ins, outs, sems):
        _, mine, first, *_ = copies(ins, outs, sems)
        for cp in mine + first:
            cp.start()

    def finish(ins, outs, sems):
        copy, mine, first, me, sibling, chips, c = copies(ins, outs, sems)
        passed = []
        for j, chip in enumerate(chips):
            for t in range(nt):
                copy(t, 1 + j, (*chip, c), me).wait_recv()
                cp = copy(t, 4 + j, (*chip, c), sibling)
                cp.start()
                passed.append(cp)
        for t in range(nt):
            copy(t, 0, sibling, me).wait_recv()
            for j, chip in enumerate(chips):
                copy(t, 4 + j, (*chip, 1 - c), me).wait_recv()
        for cp in first + passed:
            cp.wait_send()
        for cp in mine:
            cp.wait()

    return _Exchange(
        list(shards), [jax.ShapeDtypeStruct((N_DEV,) + s.shape, s.dtype) for s in shards],
        [pltpu.SemaphoreType.DMA((nt, 7)), pltpu.SemaphoreType.DMA((nt, 7)), pltpu.SemaphoreType.DMA((nt,))],
        start, finish)


def _swap_exchange(arrays, n_slices, copies):
    nt = len(arrays)

    def start(ins, outs, sems):
        for cp in copies(ins, outs, sems):
            cp.start()

    def finish(ins, outs, sems):
        sends = copies(ins, outs, sems)
        for cp in sends:
            cp.wait_recv()
        for cp in sends:
            cp.wait_send()

    return _Exchange(
        list(arrays), [jax.ShapeDtypeStruct((n_slices,) + a.shape[1:], a.dtype) for a in arrays],
        [pltpu.SemaphoreType.DMA((nt, n_slices)), pltpu.SemaphoreType.DMA((nt, n_slices))], start, finish)


def _cores_exchange(gs):
    def copies(ins, outs, sems):
        send_sems, recv_sems = sems
        x, y, c = _coords()
        return [pltpu.make_async_remote_copy(
            src_ref=ins[t].at[2 * j + (1 - c)], dst_ref=outs[t].at[j],
            send_sem=send_sems.at[t, j], recv_sem=recv_sems.at[t, j], device_id=(x, y, 1 - c), device_id_type=MESH)
            for t in range(len(gs)) for j in range(4)]

    return _swap_exchange(gs, 4, copies)


def _chips_exchange(ps):
    def copies(ins, outs, sems):
        send_sems, recv_sems = sems
        x, y, c = _coords()
        peers = [(1 - x, y), (x, 1 - y), (1 - x, 1 - y)]
        return [pltpu.make_async_remote_copy(
            src_ref=ins[t].at[2 * px + py], dst_ref=outs[t].at[k],
            send_sem=send_sems.at[t, k], recv_sem=recv_sems.at[t, k], device_id=(px, py, c), device_id_type=MESH)
            for t in range(len(ps)) for k, (px, py) in enumerate(peers)]

    return _swap_exchange(ps, 3, copies)


def _add_cores(g, r, core, *, name):
    _, A, B = g.shape
    ta = _tile(A, 512, 16)

    def body(core_ref, a_ref, b_ref, o16_ref):
        o16_ref[...] = (a_ref[...] + b_ref[...]).astype(BF16)

    blk = (None, ta, B)
    return pl.pallas_call(
        body, out_shape=jax.ShapeDtypeStruct((4, A, B), BF16),
        grid_spec=pltpu.PrefetchScalarGridSpec(
            num_scalar_prefetch=1, grid=(4, A // ta),
            in_specs=[pl.BlockSpec(blk, lambda j, i, core_ref: (2 * j + core_ref[0], i, 0)),
                      pl.BlockSpec(blk, lambda j, i, core_ref: (j, i, 0))],
            out_specs=pl.BlockSpec(blk, lambda j, i, core_ref: (j, i, 0))),
        compiler_params=_params(("parallel", "parallel")), name=name)(core, g, r)


def _adamw_math(w, g, m, v):
    m = ADAM_B1 * m + (1.0 - ADAM_B1) * g
    v = ADAM_B2 * v + (1.0 - ADAM_B2) * (g * g)
    m_hat = m / (1.0 - ADAM_B1 ** ADAM_STEP)
    v_hat = v / (1.0 - ADAM_B2 ** ADAM_STEP)
    delta = -ADAM_LR * (m_hat / (jnp.sqrt(v_hat) + ADAM_EPS) + ADAM_WD * w)
    return delta, m, v


def _sum_adamw(mine, sib, r, where, w, m, v, *, ta, name):
    Aw, Bw = w.shape
    Bg = mine.shape[2]
    assert Aw % ta == 0 and Bw <= Bg and mine.shape[1] == Aw

    def body(where_ref, p_ref, s_ref, r0, r1, r2, w_ref, m_ref, v_ref, g_out, d_out, m_out, v_out):
        g = (((p_ref[:, :Bw] + s_ref[:, :Bw]) + r0[:, :Bw].astype(F32))
             + r1[:, :Bw].astype(F32)) + r2[:, :Bw].astype(F32)
        delta, m_new, v_new = _adamw_math(w_ref[...], g, m_ref[...], v_ref[...])
        g_out[...] = g
        d_out[...] = delta
        m_out[...] = m_new
        v_out[...] = v_new

    gblk = (None, ta, Bg)
    row = pl.BlockSpec((ta, Bw), lambda i, where_ref: (i, 0))
    rspecs = [pl.BlockSpec(gblk, (lambda i, where_ref, k=k: (k, i, 0))) for k in range(3)]
    shp = jax.ShapeDtypeStruct((Aw, Bw), F32)
    return pl.pallas_call(
        body, out_shape=(shp, shp, shp, shp),
        grid_spec=pltpu.PrefetchScalarGridSpec(
            num_scalar_prefetch=1, grid=(Aw // ta,),
            in_specs=[pl.BlockSpec(gblk, lambda i, where_ref: (2 * where_ref[0] + where_ref[1], i, 0)),
                      pl.BlockSpec(gblk, lambda i, where_ref: (where_ref[0], i, 0))] + rspecs + [row, row, row],
            out_specs=(row, row, row, row)),
        compiler_params=_params(("parallel",)), name=name)(where, mine, sib, r, r, r, w, m, v)


def _adamw(w, g, m, v, *, name):
    def body(w_ref, g_ref, m_ref, v_ref, d_out, m_out, v_out):
        delta, m_new, v_new = _adamw_math(w_ref[...], g_ref[...], m_ref[...], v_ref[...])
        d_out[...] = delta
        m_out[...] = m_new
        v_out[...] = v_new

    vm = pl.BlockSpec(memory_space=pltpu.VMEM)
    shp = jax.ShapeDtypeStruct(w.shape, F32)
    return pl.pallas_call(body, out_shape=(shp, shp, shp), in_specs=[vm] * 4, out_specs=(vm, vm, vm),
                          compiler_params=_params(), name=name)(w, g, m, v)


def _small_allreduce_adamw(s, w, m, v, *, name):
    R, W = s.shape

    def body(s_ref, w_ref, m_ref, v_ref, g_out, d_out, m_out, v_out, gath, send_sems, recv_sems):
        x, y, c = _coords()
        mine = 4 * x + 2 * y + c
        gath[mine] = s_ref[...]
        peers = [((1 - x) if k & 4 else x, (1 - y) if k & 2 else y, (1 - c) if k & 1 else c) for k in range(1, N_DEV)]
        sends = []
        for k in range(1, N_DEV):
            peer = peers[k - 1]
            sends.append(pltpu.make_async_remote_copy(
                src_ref=s_ref, dst_ref=gath.at[mine], send_sem=send_sems.at[k - 1], recv_sem=recv_sems.at[k - 1],
                device_id=peer, device_id_type=MESH))
        for cp in sends:
            cp.start()
        for k in range(1, N_DEV):
            peer = peers[k - 1]
            pltpu.make_async_remote_copy(
                src_ref=s_ref, dst_ref=gath.at[4 * peer[0] + 2 * peer[1] + peer[2]],
                send_sem=send_sems.at[k - 1], recv_sem=recv_sems.at[k - 1],
                device_id=peer, device_id_type=MESH).wait_recv()
        for cp in sends:
            cp.wait_send()
        g = gath[0]
        for d in range(1, N_DEV):
            g = g + gath[d]
        delta, m_new, v_new = _adamw_math(w_ref[...], g, m_ref[...], v_ref[...])
        g_out[...] = g
        d_out[...] = delta
        m_out[...] = m_new
        v_out[...] = v_new

    vm = pl.BlockSpec(memory_space=pltpu.VMEM)
    shp = jax.ShapeDtypeStruct((R, W), F32)
    return pl.pallas_call(
        body, out_shape=(shp, shp, shp, shp), in_specs=[vm] * 4, out_specs=(vm, vm, vm, vm),
        scratch_shapes=[pltpu.VMEM((N_DEV, R, W), F32), pltpu.SemaphoreType.DMA((N_DEV - 1,)),
                        pltpu.SemaphoreType.DMA((N_DEV - 1,))],
        compiler_params=_params(), name=name)(s, w, m, v)


def _pack_small(rel_bias, g1, g2, g3, g4, b_forget, sinks, extra=None, meta=None):
    misc = jnp.concatenate([rel_bias.reshape(-1), b_forget.reshape(-1), sinks.reshape(-1)])
    misc = jnp.concatenate([misc, jnp.zeros((D_MODEL - misc.shape[0],), F32)])[None]
    last = jnp.zeros((1, D_MODEL), F32) if extra is None else extra
    meta = jnp.zeros((N_META, D_MODEL), F32) if meta is None else meta
    return jnp.concatenate([g1, g2, g3, g4, misc, last, jnp.zeros((2, D_MODEL), F32), meta], axis=0)


def _unpack_small(p):
    nrb = N_BUCKETS * SWA_Q_HEADS
    misc = p[4]
    return dict(rel_bias=misc[:nrb].reshape(N_BUCKETS, SWA_Q_HEADS), ln_pre_mix=p[0:1], ln_post_mix=p[1:2],
                ln_pre_ffn=p[2:3], ln_post_ffn=p[3:4], b_forget=misc[nrb:nrb + 8].reshape(1, 8),
                sinks=misc[nrb + 8:nrb + 16].reshape(1, 8))


def _proj_runs():
    gw = FOX_GROUP * HEAD_DIM
    swa = SWA_Q_W + 2 * SWA_KV_HEADS * HEAD_DIM
    runs = [(0, swa)]
    for grp in range(FOX_HEADS // FOX_GROUP):
        runs += [(swa + part * FOX_W + grp * gw, swa + part * FOX_W + (grp + 1) * gw) for part in range(3)]
    return runs


def _columns_from_shards(gathered, runs, shard):
    pieces = []
    for start, stop in runs:
        for d in range(start // shard, (stop - 1) // shard + 1):
            lo = d * shard
            pieces.append(gathered[d][:, max(start, lo) - lo:min(stop, lo + shard) - lo])
    return jnp.concatenate(pieces, axis=1)


def _device_shards(qkv, gate, shard, padded):
    pos, segments = 0, []
    for start, stop in _proj_runs():
        segments.append((start, stop, qkv, pos))
        pos += stop - start
    segments.append((pos, pos + gate.shape[1], gate, 0))
    total = pos + gate.shape[1]
    assert total % shard == 0
    zeros = jnp.zeros((qkv.shape[0], padded - shard), qkv.dtype)
    out = []
    for d in range(total // shard):
        lo, hi = d * shard, (d + 1) * shard
        pieces = [arr[:, src + max(lo, s) - s:src + min(hi, e) - s]
                  for s, e, arr, src in sorted(segments, key=lambda seg: seg[0]) if max(lo, s) < min(hi, e)]
        out.append(jnp.concatenate(pieces + [zeros], axis=1))
    return jnp.stack(out)


def kernel(x, meta_tokens, rel_bias, ln_pre_mix, ln_post_mix, ln_pre_ffn, ln_post_ffn, w_in, b_forget, sinks, w_out, w_gate_up, w_down, loss_target, m_meta_tokens, m_rel_bias, m_ln_pre_mix, m_ln_post_mix, m_ln_pre_ffn, m_ln_post_ffn, m_w_in, m_b_forget, m_sinks, m_w_out, m_w_gate_up, m_w_down, v_meta_tokens, v_rel_bias, v_ln_pre_mix, v_ln_post_mix, v_ln_pre_ffn, v_ln_post_ffn, v_w_in, v_b_forget, v_sinks, v_w_out, v_w_gate_up, v_w_down):
    seq = x.shape[1]
    T = BLOCK + seq
    assert T % FOX_TILE == 0
    nq = T // FOX_TILE
    tm = _tile(T, 1056)
    cin = w_in.shape[2]
    hid = w_down.shape[1]
    F = N_DEV * hid
    assert w_gate_up.shape[2] == 2 * hid and cin <= W_IN_PAD and hid % 16 == 0

    x_i, y_i, c_i = _coords()
    core = jnp.reshape(c_i, (1,)).astype(jnp.int32)
    where = jnp.stack([2 * x_i + y_i, c_i]).astype(jnp.int32)
    w_in_s = jnp.pad(w_in[0].astype(BF16), ((0, 0), (0, W_IN_PAD - cin)))
    w_gu_t = w_gate_up[0].T
    h0, target, hn1, hn1_t, g_in, _ = _pad_rows_rms(x[0], loss_target[0], ln_pre_mix,
                                                    _gather_exchange([w_in_s, meta_tokens]), name="ag_w_in_rms_pre_mix")
    gather_rest = _gather_exchange([w_out[0].astype(BF16), w_gu_t.astype(BF16), w_down[0].astype(BF16)])
    w_qkv = _columns_from_shards(g_in, _proj_runs(), cin)
    w_f = jnp.pad(_columns_from_shards(g_in, [(D_QKV, D_PROJ)], cin), ((0, 0), (0, BLOCK - FOX_HEADS)))

    proj = _matmul(hn1, w_qkv, out_dtype=BF16, tm=tm, tn=D_QKV, name="mm_in_proj")
    proj_f = _matmul(hn1, w_f, out_dtype=F32, tm=tm, tn=BLOCK, name="mm_in_proj_f")

    f_t = proj_f[:, :FOX_HEADS].T
    bf_col = b_forget.reshape(FOX_HEADS, 1)

    oh_cur, oh_prev = _bucket_onehots()
    bias_c, bias_p = _bias_tiles(rel_bias.T, jnp.asarray(oh_cur.T), jnp.asarray(oh_prev.T), name="bias_tiles")
    far = rel_bias[N_BUCKETS - 1]
    sink_v = sinks[0]
    mix_a = _swa_fwd(proj, bias_c, bias_p, far, sink_v, name="swa_fwd")

    cum_col = _fox_gates_fwd(f_t, bf_col, name="fox_gates_fwd")
    q_b, k_b, v_b = _fox_prep(proj, cum_col, name="fox_prep")
    mix, lse_row, g_out, g_gu, g_down = _fox_fwd(q_b, k_b, v_b, mix_a, ex=gather_rest, name="fox_fwd")
    w_out_full = g_out.reshape(D_MODEL, D_MODEL)
    w_gu_full_t = g_gu.reshape(2 * F, D_MODEL)
    w_down_full = g_down.reshape(F, D_MODEL)

    a1 = _matmul(mix, w_out_full, out_dtype=F32, tm=tm, tn=D_MODEL, name="mm_out_proj")
    h1, hn2 = _post_res_norm(a1, ln_post_mix, h0, ln_pre_ffn, name="post_mix_pre_ffn")
    gate, up, act, act_t = _gate_up_swiglu(hn2, w_gu_full_t, name="mm_gate_up")
    ff = _matmul(act, w_down_full, out_dtype=F32, tm=tm, tn=D_MODEL, name="mm_down")
    dh2, dff, dg_post_ffn, loss_acc = _loss_head(ff, ln_post_ffn, h1, target, name="loss_head")

    dgu = _d_act_swiglu(dff, w_down_full, gate, up, name="mm_d_act")
    d_w_down = _matmul(act_t, dff, out_dtype=F32, tm=_tile(F, 768), tn=D_MODEL, name="mm_dw_down")
    dhn2 = _matmul(dgu, w_gu_full_t, out_dtype=F32, tm=_tile(T, 704), tn=D_MODEL, name="mm_d_hn2")
    d_w_gu_t = _matmul(dgu, hn2, ta=True, out_dtype=F32, tm=512, tn=D_MODEL, name="mm_dw_gate_up")
    dh1, dg_pre_ffn, da1, dg_post_mix = _rms_bwd_twice(h1, ln_pre_ffn, dhn2, dh2, a1, ln_post_mix,
                                                       name="rms_bwd_pre_ffn_post_mix")
    dmix = _matmul(da1, w_out_full, nt=True, out_dtype=BF16, tm=tm, tn=D_MODEL, name="mm_d_mix")
    d_w_out = _matmul(mix, da1, ta=True, out_dtype=F32, tm=512, tn=D_MODEL, name="mm_dw_out")

    ffn_grads = [g.reshape(N_DEV, -1, D_MODEL) for g in (d_w_out, d_w_gu_t, d_w_down)]
    dproj_a, dbc, dbp, dbf, dsk, *ffn_sibling = _swa_bwd(
        proj, dmix, bias_c, bias_p, far, sink_v, ex=_cores_exchange(ffn_grads), name="swa_bwd")
    ffn_sums = [_add_cores(g, r, core, name="rs_add_" + t)
                for g, r, t in zip(ffn_grads, ffn_sibling, ["w_out", "w_gate_up", "w_down"])]

    do_b = _fox_prep_bwd(dmix, mix, name="fox_prep_bwd")
    dproj, dcq, dck, *ffn_chips = _fox_bwd(
        q_b, k_b, v_b, do_b, lse_row, dproj_a, ex=_chips_exchange(ffn_sums), name="fox_bwd")
    df_t, d_bf = _fox_gates_bwd(dcq.reshape(FOX_HEADS, T), dck.reshape(FOX_HEADS, T), f_t, bf_col,
                                name="fox_gates_bwd")
    df = jnp.pad(df_t.T.astype(BF16), ((0, 0), (0, BLOCK - FOX_HEADS)))

    d_w_qkv = _matmul(hn1_t, dproj, out_dtype=F32, tm=512, tn=768, name="mm_dw_in")
    d_w_f = _matmul(hn1_t, df, out_dtype=F32, tm=512, tn=BLOCK, name="mm_dw_in_f")
    d_w_in = _device_shards(d_w_qkv, d_w_f[:, :FOX_HEADS], cin, W_IN_PAD)
    d_tab, d_sink, in_sibling = _small_grads(dbc, dbp, dbf, dsk, jnp.asarray(oh_cur), jnp.asarray(oh_prev),
                                             ex=_cores_exchange([d_w_in]), name="small_grads")
    in_sum = _add_cores(d_w_in, in_sibling, core, name="rs_add_w_in")
    dhn1, in_chips = _matmul(dproj, w_qkv, nt=True, out_dtype=F32, tm=tm, tn=512,
                             ex=_chips_exchange([in_sum]), name="mm_d_hn1")
    dx_rows, dg_pre_mix, dh0_head = _rms_bwd_rows(h0, ln_pre_mix, dhn1, df, w_f, dh1, name="rms_bwd_pre_mix")
    grad_x = dx_rows[None]
    d_meta = dh0_head[PAD_ROWS:]

    rs_out, rs_gu, rs_down = zip(ffn_grads, ffn_sibling, ffn_chips)
    updates = [("w_in", (d_w_in, in_sibling, in_chips), (w_in[0], m_w_in[0], v_w_in[0]), 256),
               ("w_out", rs_out, (w_out[0], m_w_out[0], v_w_out[0]), BLOCK),
               ("w_gate_up", rs_gu, (w_gu_t, m_w_gate_up[0].T, v_w_gate_up[0].T), hid),
               ("w_down", rs_down, (w_down[0], m_w_down[0], v_w_down[0]), hid)]
    big = [{}, {}, {}, {}]
    for t, grads, shard, ta in updates:
        res = _sum_adamw(*grads, where, *shard, ta=ta, name="rs_adamw_" + t)
        for kind in range(4):
            big[kind][t] = (res[kind].T if t == "w_gate_up" else res[kind])[None]

    loss_row = jnp.pad(loss_acc[0:1, 0:1] * (0.5 / D_MODEL), ((0, 0), (0, D_MODEL - 1)))
    s_small = _pack_small(d_tab.T, dg_pre_mix, dg_post_mix, dg_pre_ffn, dg_post_ffn, d_bf, d_sink,
                          extra=loss_row, meta=d_meta)
    w_s = _pack_small(rel_bias, ln_pre_mix, ln_post_mix, ln_pre_ffn, ln_post_ffn, b_forget, sinks)
    m_s = _pack_small(m_rel_bias, m_ln_pre_mix, m_ln_post_mix, m_ln_pre_ffn, m_ln_post_ffn, m_b_forget, m_sinks)
    v_s = _pack_small(v_rel_bias, v_ln_pre_mix, v_ln_post_mix, v_ln_pre_ffn, v_ln_post_ffn, v_b_forget, v_sinks)
    small = _small_allreduce_adamw(s_small, w_s, m_s, v_s, name="small_allreduce_adamw")
    loss = small[0][5, 0]
    mcols = meta_tokens.shape[1]
    g_meta_mine = lax.dynamic_slice(small[0][8:8 + N_META], (0, (4 * x_i + 2 * y_i + c_i) * mcols), (N_META, mcols))
    big[0]["meta_tokens"] = g_meta_mine
    for kind, arr in enumerate(_adamw(meta_tokens, g_meta_mine, m_meta_tokens, v_meta_tokens, name="adamw_meta")):
        big[kind + 1]["meta_tokens"] = arr
    small = [_unpack_small(p) for p in small]

    names = ["meta_tokens", "rel_bias", "ln_pre_mix", "ln_post_mix", "ln_pre_ffn", "ln_post_ffn", "w_in",
             "b_forget", "sinks", "w_out", "w_gate_up", "w_down"]
    outs = [loss, grad_x]
    for kind in range(4):
        for nme in names:
            outs.append(big[kind][nme] if nme in big[kind] else small[kind][nme])
    return tuple(outs)
```

```python
import math

import numpy as np
import jax
import jax.numpy as jnp
from jax import lax
from jax.experimental import pallas as pl
from jax.experimental.pallas import tpu as pltpu

F32 = jnp.float32
BF16 = jnp.bfloat16
HIGHEST = lax.Precision.HIGHEST
MESH = pl.DeviceIdType.MESH

N_DEV = 8
D_MODEL = 1024
N_META = 16
HEAD_DIM = 64
SWA_Q_HEADS = 8
SWA_KV_HEADS = 2
SWA_GROUP = 4
FOX_HEADS = 8
FOX_W = FOX_HEADS * HEAD_DIM
SWA_Q_W = SWA_Q_HEADS * HEAD_DIM
BLOCK = 128
PAD_ROWS = BLOCK - N_META
N_BUCKETS = 32
MAX_DISTANCE = 128
D_FF = 2816
D_QKV = 2304
D_PROJ = D_QKV + FOX_HEADS
D_PROJ_PAD = 2560
EPS = 1e-6
NEG = -1e30
SCALE = HEAD_DIM ** -0.5
ADAM_LR, ADAM_B1, ADAM_B2, ADAM_EPS, ADAM_WD, ADAM_STEP = 0.001, 0.9, 0.999, 1e-08, 0.01, 10
VMEM_LIMIT = 56 * 1024 * 1024
FOX_TILE = 384
FOX_GROUP = 4
W_IN_PAD = 384

NT = (((1,), (1,)), ((), ()))
NN = (((1,), (0,)), ((), ()))
TN = (((0,), (0,)), ((), ()))


def _params(sem=None, **kw):
    if sem is not None:
        kw["dimension_semantics"] = sem
    return pltpu.CompilerParams(vmem_limit_bytes=VMEM_LIMIT, **kw)


def _tile(n, target, mult=16):
    best = None
    for t in range(mult, min(n, target) + 1, mult):
        if n % t == 0:
            best = t
    assert best is not None, (n, target)
    return best


def _matmul(a, b, *, nt=False, ta=False, out_dtype, tm, tn, tk=None, ex=None, name):
    M, K = a.shape[::-1] if ta else a.shape
    assert not (ta and nt)
    N = b.shape[0] if nt else b.shape[1]
    tk = K if tk is None else tk
    assert M % tm == 0 and N % tn == 0 and K % tk == 0, (name, a.shape, b.shape, tm, tn, tk)
    nk = K // tk
    dn = NT if nt else (TN if ta else NN)
    a_spec = pl.BlockSpec((tk, tm), lambda i, j, k: (k, i)) if ta else pl.BlockSpec((tm, tk), lambda i, j, k: (i, k))

    def body(a_ref, b_ref, o_ref, *scr):
        part = lax.dot_general(a_ref[...], b_ref[...], dn, preferred_element_type=F32)
        if nk == 1:
            o_ref[...] = part.astype(o_ref.dtype)
        else:
            acc = scr[0]
            k = pl.program_id(2)

            @pl.when(k == 0)
            def _():
                acc[...] = part

            @pl.when(k > 0)
            def _():
                acc[...] += part

            @pl.when(k == nk - 1)
            def _():
                o_ref[...] = acc[...].astype(o_ref.dtype)

    if nt:
        b_spec = pl.BlockSpec((tn, tk), lambda i, j, k: (j, k))
    else:
        b_spec = pl.BlockSpec((tk, tn), lambda i, j, k: (k, j))
    out_shape = jax.ShapeDtypeStruct((M, N), out_dtype)
    out_spec = pl.BlockSpec((tm, tn), lambda i, j, k: (i, j))
    grid = (M // tm, N // tn, nk)
    body, x_in, x_in_specs, x_out, x_out_specs, x_scr = _carry(ex, grid, 2, 1, body)
    res = pl.pallas_call(
        body,
        out_shape=(out_shape, *x_out),
        grid=grid,
        in_specs=[a_spec, b_spec] + x_in_specs,
        out_specs=(out_spec, *x_out_specs),
        scratch_shapes=([pltpu.VMEM((tm, tn), F32)] if nk > 1 else []) + x_scr,
        compiler_params=_params(("parallel", "parallel", "arbitrary") if ex is None else ("arbitrary",) * 3),
        name=name,
    )(a, b, *x_in)
    return res[0] if ex is None else res


def _rstd(x):
    return lax.rsqrt(jnp.mean(x * x, axis=-1, keepdims=True) + EPS)


def _pad_rows_rms(x, target, g, ex, *, name):
    S, D = x.shape
    nb = S // BLOCK + 1
    ni, no = len(ex.inputs), len(ex.out_shapes)
    mcols = D // N_DEV

    def body(x_ref, t_ref, g_ref, *rest):
        side_in, (h_ref, to_ref, y_ref, yt_ref) = rest[:ni], rest[ni:ni + 4]
        side_out = rest[ni + 4:ni + 4 + no]
        meta_buf, meta_sems, *sems = rest[ni + 4 + no:]
        i = pl.program_id(0)

        @pl.when(i == 0)
        def _():
            ex.start(side_in, side_out, sems)

        def norm():
            h = h_ref[...]
            y = h * _rstd(h) * g_ref[...]
            y_ref[...] = y.astype(y_ref.dtype)
            yt_ref[...] = y.T.astype(yt_ref.dtype)

        @pl.when(i < nb - 1)
        def _():
            h_ref[...] = x_ref[...]
            to_ref[...] = t_ref[...]
            norm()

        @pl.when(i == nb - 1)
        def _():
            ex.finish(side_in, side_out, sems)
            copies = [pltpu.make_async_copy(side_out[-1].at[d], meta_buf.at[:, d * mcols:(d + 1) * mcols],
                                            meta_sems.at[d]) for d in range(N_DEV)]
            for cp in copies:
                cp.start()
            for cp in copies:
                cp.wait()
            h_ref[:PAD_ROWS, :] = jnp.zeros((PAD_ROWS, D), F32)
            h_ref[PAD_ROWS:, :] = meta_buf[...]
            to_ref[...] = jnp.zeros_like(to_ref)
            norm()

    src = pl.BlockSpec((BLOCK, D), lambda i: (jnp.minimum(i, nb - 2), 0))
    dst = pl.BlockSpec((BLOCK, D), lambda i: ((i + 1) % nb, 0))
    hbm = pl.BlockSpec(memory_space=pl.ANY)
    rows = jax.ShapeDtypeStruct((BLOCK + S, D), F32)
    return pl.pallas_call(
        body,
        out_shape=(rows, rows, jax.ShapeDtypeStruct((BLOCK + S, D), BF16), jax.ShapeDtypeStruct((D, BLOCK + S), BF16),
                   *ex.out_shapes),
        grid=(nb,),
        in_specs=[src, src, pl.BlockSpec((1, D), lambda i: (0, 0))] + [hbm] * ni,
        out_specs=(dst, dst, dst, pl.BlockSpec((D, BLOCK), lambda i: (0, (i + 1) % nb)), *([hbm] * no)),
        scratch_shapes=[pltpu.VMEM((N_META, D), F32), pltpu.SemaphoreType.DMA((N_DEV,))] + list(ex.scratch),
        compiler_params=_params(("arbitrary",)), name=name)(x, target, g, *ex.inputs)


def _post_res_norm(a, g_post, h, g_pre, *, name):
    T, D = a.shape
    tm = _tile(T, 384, BLOCK)

    def body(a_ref, gp_ref, h_ref, gn_ref, h1_ref, o_ref):
        a = a_ref[...]
        h1 = h_ref[...] + a * _rstd(a) * gp_ref[...]
        h1_ref[...] = h1
        o_ref[...] = (h1 * _rstd(h1) * gn_ref[...]).astype(o_ref.dtype)

    row = pl.BlockSpec((tm, D), lambda i: (i, 0))
    vec = pl.BlockSpec((1, D), lambda i: (0, 0))
    return pl.pallas_call(
        body, out_shape=(jax.ShapeDtypeStruct((T, D), F32), jax.ShapeDtypeStruct((T, D), BF16)), grid=(T // tm,),
        in_specs=[row, vec, row, vec], out_specs=(row, row),
        compiler_params=_params(("parallel",)), name=name)(a, g_post, h, g_pre)


def _loss_head(a, g, h, target, *, name):
    T, D = a.shape
    tm = _tile(T, 512)

    def body(a_ref, g_ref, h_ref, t_ref, dy_ref, da_ref, dg_ref, loss_ref):
        i = pl.program_id(0)
        a = a_ref[...]
        r = _rstd(a)
        ah = a * r
        y = h_ref[...] + ah * g_ref[...]
        rows = i * tm + lax.broadcasted_iota(jnp.int32, (tm, 1), 0)
        err = jnp.where(rows >= BLOCK, y - t_ref[...], 0.0)
        dy = err / D
        dy_ref[...] = dy
        dah = dy * g_ref[...]
        da_ref[...] = (r * (dah - ah * jnp.mean(dah * ah, axis=-1, keepdims=True))).astype(da_ref.dtype)
        part = jnp.sum(jnp.sum(err * err, axis=1, keepdims=True), axis=0, keepdims=True)

        @pl.when(i == 0)
        def _():
            loss_ref[...] = jnp.zeros_like(loss_ref)
            dg_ref[...] = jnp.zeros_like(dg_ref)

        loss_ref[...] += jnp.broadcast_to(part, loss_ref.shape)
        dg_ref[...] += jnp.sum(dy * ah, axis=0, keepdims=True)

    row = pl.BlockSpec((tm, D), lambda i: (i, 0))
    vec = pl.BlockSpec((1, D), lambda i: (0, 0))
    return pl.pallas_call(
        body, out_shape=(jax.ShapeDtypeStruct((T, D), F32), jax.ShapeDtypeStruct((T, D), BF16),
                         jax.ShapeDtypeStruct((1, D), F32), jax.ShapeDtypeStruct((8, 128), F32)),
        grid=(T // tm,),
        in_specs=[row, vec, row, row],
        out_specs=(row, row, vec, pl.BlockSpec((8, 128), lambda i: (0, 0))),
        compiler_params=_params(("arbitrary",)), name=name)(a, g, h, target)


def _rms_pull_back(x, g, dy):
    r = _rstd(x)
    xh = x * r
    dxh = dy * g
    return r * (dxh - xh * jnp.mean(dxh * xh, axis=-1, keepdims=True)), jnp.sum(dy * xh, axis=0, keepdims=True)


def _rms_bwd_twice(x, g, dy, res, x2, g2, *, name):
    T, D = x.shape
    tm = _tile(T, 512)

    def body(x_ref, g_ref, dy_ref, res_ref, x2_ref, g2_ref, dx_ref, dg_ref, dx2_ref, dg2_ref):
        @pl.when(pl.program_id(0) == 0)
        def _():
            dg_ref[...] = jnp.zeros_like(dg_ref)
            dg2_ref[...] = jnp.zeros_like(dg2_ref)

        dx, dg = _rms_pull_back(x_ref[...], g_ref[...], dy_ref[...].astype(F32))
        dx = dx + res_ref[...]
        dx_ref[...] = dx
        dg_ref[...] += dg
        dx2, dg2 = _rms_pull_back(x2_ref[...], g2_ref[...], dx)
        dx2_ref[...] = dx2.astype(dx2_ref.dtype)
        dg2_ref[...] += dg2

    row = pl.BlockSpec((tm, D), lambda i: (i, 0))
    vec = pl.BlockSpec((1, D), lambda i: (0, 0))
    gain = jax.ShapeDtypeStruct((1, D), F32)
    return pl.pallas_call(
        body, out_shape=(jax.ShapeDtypeStruct((T, D), F32), gain, jax.ShapeDtypeStruct((T, D), BF16), gain),
        grid=(T // tm,), in_specs=[row, vec, row, row, row, vec], out_specs=(row, vec, row, vec),
        compiler_params=_params(("arbitrary",)), name=name)(x, g, dy, res, x2, g2)


def _rms_bwd_rows(x, g, dy, a, b, res, *, name):
    T, D = x.shape
    n = a.shape[1]
    n_tail = T // BLOCK - 1
    per_step = max(p for p in (4, 3, 2, 1) if n_tail % p == 0)
    steps = n_tail // per_step
    assert T == BLOCK * (1 + n_tail)
    n_rows = 4 * (per_step + 1)

    def body(*refs):
        rows, (g_ref, b_ref), (tail_ref, dg_ref, head_ref) = refs[:n_rows], refs[n_rows:n_rows + 2], refs[n_rows + 2:]

        def block(s):
            x_ref, dy_ref, a_ref, res_ref = rows[4 * s:4 * s + 4]
            dy_all = dy_ref[...] + lax.dot_general(a_ref[...], b_ref[...], NT, preferred_element_type=F32)
            dx, dg = _rms_pull_back(x_ref[...], g_ref[...], dy_all)
            return dx + res_ref[...], dg

        @pl.when(pl.program_id(0) == 0)
        def _():
            dx, dg = block(per_step)
            head_ref[...] = dx
            dg_ref[...] = dg

        for s in range(per_step):
            dx, dg = block(s)
            tail_ref[s * BLOCK:(s + 1) * BLOCK, :] = dx
            dg_ref[...] += dg

    def blocks(width):
        tail = [pl.BlockSpec((BLOCK, width), lambda i, s=s: (per_step * i + s + 1, 0)) for s in range(per_step)]
        return tail + [pl.BlockSpec((BLOCK, width), lambda i: (0, 0))]

    specs, args = [], []
    for bx, bdy, ba, bres in zip(blocks(D), blocks(D), blocks(n), blocks(D)):
        specs += [bx, bdy, ba, bres]
        args += [x, dy, a, res]
    vec = pl.BlockSpec((1, D), lambda i: (0, 0))
    return pl.pallas_call(
        body,
        out_shape=(jax.ShapeDtypeStruct((T - BLOCK, D), F32), jax.ShapeDtypeStruct((1, D), F32),
                   jax.ShapeDtypeStruct((BLOCK, D), F32)),
        grid=(steps,), in_specs=specs + [vec, pl.BlockSpec(b.shape, lambda i: (0, 0))],
        out_specs=(pl.BlockSpec((per_step * BLOCK, D), lambda i: (i, 0)), vec, pl.BlockSpec((BLOCK, D), lambda i: (0, 0))),
        compiler_params=_params(("arbitrary",)), name=name)(*args, g, b)


def _gate_up_swiglu(a, w_t, *, name):
    T, D = a.shape
    F = w_t.shape[0] // 2
    tm = _tile(T, 1408, BLOCK)
    n = _tile(F, 256, BLOCK)
    rows = 3 * BLOCK

    def body(a_ref, wg_ref, wu_ref, g_ref, u_ref, o_ref, ot_ref):
        wg, wu = wg_ref[...], wu_ref[...]
        for r in range(0, tm, rows):
            e = min(r + rows, tm)
            x = a_ref[r:e, :]
            g = lax.dot_general(x, wg, NT, preferred_element_type=F32)
            u = lax.dot_general(x, wu, NT, preferred_element_type=F32)
            g16, u16 = g.astype(BF16), u.astype(BF16)
            g_ref[r:e, :] = g16
            u_ref[r:e, :] = u16
            gr = g16.astype(F32)
            act = gr / (1.0 + jnp.exp(-gr)) * u16.astype(F32)
            o_ref[r:e, :] = act.astype(o_ref.dtype)
            ot_ref[:, r:e] = act.T.astype(ot_ref.dtype)

    tile = pl.BlockSpec((tm, n), lambda i, j: (i, j))
    shp = jax.ShapeDtypeStruct((T, F), BF16)
    return pl.pallas_call(
        body, out_shape=(shp, shp, shp, jax.ShapeDtypeStruct((F, T), BF16)), grid=(T // tm, F // n),
        in_specs=[pl.BlockSpec((tm, D), lambda i, j: (i, 0)),
                  pl.BlockSpec((n, D), lambda i, j: (j, 0)),
                  pl.BlockSpec((n, D), lambda i, j: (j + F // n, 0))],
        out_specs=(tile, tile, tile, pl.BlockSpec((n, tm), lambda i, j: (j, i))),
        compiler_params=_params(("parallel", "parallel")), name=name)(a, w_t, w_t)


def _d_act_swiglu(dff, w_down, gate, up, *, name):
    T, D = dff.shape
    F = w_down.shape[0]
    tm = _tile(T, 384)
    chunk = 768
    assert F % BLOCK == 0

    def body(d_ref, w_ref, g_ref, u_ref, o_ref):
        dy = d_ref[...]
        for c in range(0, F, chunk):
            e = min(c + chunk, F)
            d = lax.dot_general(dy, w_ref[c:e, :], NT, preferred_element_type=F32)
            g = g_ref[:, c:e].astype(F32)
            u = u_ref[:, c:e].astype(F32)
            sg = 1.0 / (1.0 + jnp.exp(-g))
            o_ref[:, c:e] = (d * u * (sg * (1.0 + g * (1.0 - sg)))).astype(o_ref.dtype)
            o_ref[:, F + c:F + e] = (d * (g * sg)).astype(o_ref.dtype)

    row = pl.BlockSpec((tm, F), lambda i: (i, 0))
    return pl.pallas_call(
        body, out_shape=jax.ShapeDtypeStruct((T, 2 * F), BF16), grid=(T // tm,),
        in_specs=[pl.BlockSpec((tm, D), lambda i: (i, 0)), pl.BlockSpec((F, D), lambda i: (0, 0)), row, row],
        out_specs=pl.BlockSpec((tm, 2 * F), lambda i: (i, 0)),
        compiler_params=_params(("parallel",)), name=name)(dff, w_down, gate, up)


def _fox_gates_fwd(f_t, b, *, name):
    H, T = f_t.shape
    nb = T // BLOCK

    def body(f_ref, b_ref, col_ref):
        f = f_ref[...] + b_ref[...]
        ls = jnp.minimum(f, 0.0) - jnp.log(1.0 + jnp.exp(-jnp.abs(f)))
        t = lax.broadcasted_iota(jnp.int32, (H, T), 1)
        ls = jnp.where(t >= PAD_ROWS, ls, 0.0)
        upper = (lax.broadcasted_iota(jnp.int32, (BLOCK, BLOCK), 0)
                 <= lax.broadcasted_iota(jnp.int32, (BLOCK, BLOCK), 1)).astype(F32)
        carry = jnp.zeros((H, 1), F32)
        for blk in range(nb):
            seg = ls[:, blk * BLOCK:(blk + 1) * BLOCK]
            pre = jnp.dot(seg, upper, precision=HIGHEST, preferred_element_type=F32) + carry
            key_gate = jnp.where(t[:, blk * BLOCK:(blk + 1) * BLOCK] >= PAD_ROWS, pre, -NEG)
            terms = list(_split3(pre)) + list(_split3(key_gate))
            col_ref[blk * BLOCK:(blk + 1) * BLOCK, :] = jnp.concatenate(
                terms + [jnp.zeros((BLOCK - len(terms) * H, BLOCK), F32)], axis=0).T.astype(col_ref.dtype)
            carry = pre[:, BLOCK - 1:BLOCK]

    vm = pl.BlockSpec(memory_space=pltpu.VMEM)
    return pl.pallas_call(
        body, out_shape=jax.ShapeDtypeStruct((T, BLOCK), BF16),
        in_specs=[vm, vm], out_specs=vm,
        compiler_params=_params(), name=name)(f_t, b)


def _fox_gates_bwd(dcq, dck, f_t, b, *, name):
    H, T = f_t.shape
    nb = T // BLOCK

    def body(dq_ref, d_ref, f_ref, b_ref, df_ref, db_ref):
        lower = (lax.broadcasted_iota(jnp.int32, (BLOCK, BLOCK), 0)
                 >= lax.broadcasted_iota(jnp.int32, (BLOCK, BLOCK), 1)).astype(F32)
        carry = jnp.zeros((H, 1), F32)
        for blk in range(nb - 1, -1, -1):
            seg = dq_ref[:, blk * BLOCK:(blk + 1) * BLOCK] - d_ref[:, blk * BLOCK:(blk + 1) * BLOCK]
            suf = jnp.dot(seg, lower, precision=HIGHEST, preferred_element_type=F32) + carry
            df_ref[:, blk * BLOCK:(blk + 1) * BLOCK] = suf
            carry = suf[:, 0:1]
        f = f_ref[...] + b_ref[...]
        t = lax.broadcasted_iota(jnp.int32, (H, T), 1)
        df = jnp.where(t >= PAD_ROWS, df_ref[...] / (1.0 + jnp.exp(f)), 0.0)
        df_ref[...] = df
        db_ref[...] = jnp.sum(df, axis=1, keepdims=True)

    vm = pl.BlockSpec(memory_space=pltpu.VMEM)
    return pl.pallas_call(
        body, out_shape=(jax.ShapeDtypeStruct((H, T), F32), jax.ShapeDtypeStruct((H, 1), F32)),
        in_specs=[vm, vm, vm, vm], out_specs=(vm, vm),
        compiler_params=_params(), name=name)(dcq, dck, f_t, b)


def _fox_lanes(parity):
    base = HEAD_DIM * (1 - parity)
    return base, base + 3


def _split3(c):
    hi = c.astype(BF16).astype(F32)
    r = c - hi
    mid = r.astype(BF16).astype(F32)
    lo = (r - mid).astype(BF16).astype(F32)
    return hi, mid, lo


def _lanes(lane, parity, data, start, terms, ones_at=None, fill=1.0):
    out = jnp.zeros((), F32) if ones_at is None else jnp.where((lane >= ones_at) & (lane < ones_at + 3), fill, 0.0)
    for i, t in enumerate(terms):
        out = jnp.where(lane == start + i, t, out)
    return jnp.where(lane // HEAD_DIM == parity, data, out)


def _fox_prep(proj, cum_col, *, name):
    T = proj.shape[0]
    tm = _tile(T, 1408, BLOCK)
    nt = T // tm
    H = FOX_HEADS
    lanes = 2 * HEAD_DIM
    first = (proj.shape[1] - 3 * H * HEAD_DIM) // lanes

    def body(q_ref, k_ref, v_ref, c_ref, qa_ref, ka_ref, va_ref):
        p = pl.program_id(0)
        i = pl.program_id(1)
        lane = lax.broadcasted_iota(jnp.int32, (1, lanes), 1)
        src = lax.broadcasted_iota(jnp.int32, (lanes, lanes), 0)
        dst = lax.broadcasted_iota(jnp.int32, (lanes, lanes), 1)
        q2 = q_ref[...].astype(F32) * SCALE
        k2 = k_ref[...].astype(F32)
        v2 = v_ref[...].astype(F32)
        gates = c_ref[...]
        def placed(h, first_term, start):
            pick = ((src % FOX_HEADS == h) & (src // FOX_HEADS - first_term == dst - start)
                    & (dst >= start) & (dst < start + 3))
            return jnp.dot(gates, pick.astype(BF16), preferred_element_type=F32)

        moved = [(placed(2 * p + e, 0, _fox_lanes(e)[1]), placed(2 * p + e, 3, _fox_lanes(e)[0])) for e in range(2)]
        for e in range(2):
            kc, qc = _fox_lanes(e)
            own = lane // HEAD_DIM == e
            minus = jnp.where((lane >= kc) & (lane < kc + 3), -1.0, 0.0)
            ones_q = jnp.where((lane >= qc) & (lane < qc + 3), 1.0, 0.0)
            ones_k = jnp.where((lane >= kc) & (lane < kc + 3), 1.0, 0.0)
            qa_ref[e] = jnp.where(own, q2, moved[e][0] + minus).astype(BF16)
            ka_ref[e] = jnp.where(own, k2, moved[e][1] + ones_q).astype(BF16)
            va_ref[e] = jnp.where(own, v2, ones_k).astype(BF16)

    pairs = FOX_GROUP // 2

    def col(part):
        return pl.BlockSpec((tm, lanes),
                            lambda p, i: (i, first + 3 * pairs * (p // pairs) + part * pairs + p % pairs))

    out = pl.BlockSpec((2, tm, lanes), lambda p, i: (p, i, 0))
    shp = jax.ShapeDtypeStruct((H, T, lanes), BF16)
    return pl.pallas_call(
        body, out_shape=(shp, shp, shp), grid=(H // 2, nt),
        in_specs=[col(0), col(1), col(2), pl.BlockSpec((tm, lanes), lambda p, i: (i, 0))],
        out_specs=(out, out, out),
        compiler_params=_params(("parallel", "parallel")), name=name)(proj, proj, proj, cum_col)


def _fox_fwd(q_aug, k_aug, v_aug, mix, *, ex=None, name):
    H, T, lanes = q_aug.shape
    tq = FOX_TILE
    nq = T // tq
    G = FOX_HEADS

    def body(q_ref, k_ref, v_ref, mix_ref, o_ref, lse_ref, m_scr, acc_scr):
        i = pl.program_id(1)
        m_scr[...] = jnp.full(m_scr.shape, NEG, F32)
        acc_scr[...] = jnp.zeros(acc_scr.shape, F32)

        def step(kb, diag):
            off = pl.multiple_of(kb * tq, tq)
            s_t = [lax.dot_general(k_ref[g, pl.ds(off, tq), :], q_ref[g], NT, preferred_element_type=F32)
                   for g in range(G)]
            if diag:
                r = lax.broadcasted_iota(jnp.int32, (tq, tq), 0)
                c = lax.broadcasted_iota(jnp.int32, (tq, tq), 1)
                s_t = [jnp.where(c >= r, s, NEG) for s in s_t]
            m_prev = [m_scr[g] for g in range(G)]
            m_new = [jnp.maximum(m_prev[g], jnp.max(s_t[g], axis=0, keepdims=True)) for g in range(G)]
            p_t = [jnp.exp(s_t[g] - m_new[g]).astype(BF16) for g in range(G)]
            pv = [lax.dot_general(v_ref[g, pl.ds(off, tq), :], p_t[g], TN, preferred_element_type=F32)
                  for g in range(G)]
            for g in range(G):
                acc_scr[g] = jnp.exp(m_prev[g] - m_new[g]) * acc_scr[g] + pv[g]
                m_scr[g] = m_new[g]

        def loop_body(kb, carry):
            step(kb, False)
            return carry

        lax.fori_loop(0, i, loop_body, 0)
        step(i, True)
        lane = lax.broadcasted_iota(jnp.int32, (tq, lanes), 1)
        outs = []
        for g in range(G):
            ones = _fox_lanes(g % 2)[0]
            acc = acc_scr[g]
            lse_ref[g] = m_scr[g] + jnp.log(acc[ones:ones + 1, :])
            acc_t = acc.T
            outs.append(acc_t / acc_t[:, ones:ones + 1])
        for pair in range(G // 2):
            o_ref[:, pair * lanes:(pair + 1) * lanes] = jnp.where(
                lane < HEAD_DIM, outs[2 * pair], outs[2 * pair + 1]).astype(o_ref.dtype)

    blk = pl.BlockSpec((G, tq, lanes), lambda h, i: (h, i, 0))
    full = pl.BlockSpec((G, T, lanes), lambda h, i: (h, 0, 0))
    grid = (H // G, nq)
    first = mix.shape[1] // (G * HEAD_DIM) - H // G
    body, x_in, x_in_specs, x_out, x_out_specs, x_scr = _carry(ex, grid, 4, 2, body)
    return pl.pallas_call(
        body,
        out_shape=(jax.ShapeDtypeStruct(mix.shape, mix.dtype), jax.ShapeDtypeStruct((H, nq, 1, tq), F32), *x_out),
        grid=grid,
        in_specs=[blk, full, full, pl.BlockSpec(memory_space=pl.ANY)] + x_in_specs,
        out_specs=(pl.BlockSpec((tq, G * HEAD_DIM), lambda h, i: (i, first + h)),
                   pl.BlockSpec((G, None, 1, tq), lambda h, i: (h, i, 0, 0)), *x_out_specs),
        input_output_aliases={3: 0},
        scratch_shapes=[pltpu.VMEM((G, 1, tq), F32), pltpu.VMEM((G, lanes, tq), F32)] + x_scr,
        compiler_params=_params(("arbitrary", "arbitrary")), name=name)(q_aug, k_aug, v_aug, mix, *x_in)


def _fox_prep_bwd(dmix, mix, *, name):
    T = dmix.shape[0]
    H = FOX_HEADS
    tm = _tile(T, 1408, BLOCK)
    lanes = 2 * HEAD_DIM
    first = mix.shape[1] // lanes - H // 2

    def body(d_ref, o_ref, da_ref):
        lane = lax.broadcasted_iota(jnp.int32, (1, lanes), 1)
        d2 = d_ref[...].astype(F32)
        prod = d2 * o_ref[...].astype(F32)
        for e in range(2):
            delta = jnp.sum(jnp.where(lane // HEAD_DIM == e, prod, 0.0), axis=1, keepdims=True)
            da_ref[e] = _lanes(lane, e, d2, _fox_lanes(e)[0], _split3(-delta)).astype(BF16)

    pair = pl.BlockSpec((tm, lanes), lambda p, i: (i, first + p))
    return pl.pallas_call(
        body, out_shape=jax.ShapeDtypeStruct((H, T, lanes), BF16), grid=(H // 2, T // tm),
        in_specs=[pair, pair],
        out_specs=pl.BlockSpec((2, tm, lanes), lambda p, i: (p, i, 0)),
        compiler_params=_params(("parallel", "parallel")), name=name)(dmix, mix)


def _fox_bwd(q_aug, k_aug, v_aug, do_aug, lse_row, dproj, *, ex=None, name):
    H, T, lanes = q_aug.shape
    tq = FOX_TILE
    nq = T // tq
    G = FOX_GROUP

    def side_by_side(tiles, scale=None):
        lane = lax.broadcasted_iota(jnp.int32, tiles[0].shape, 1)
        out = [jnp.where(lane < HEAD_DIM, tiles[2 * p], tiles[2 * p + 1]) for p in range(G // 2)]
        out = jnp.concatenate(out, axis=1)
        return out if scale is None else out * scale

    def body(q_ref, k_ref, v_ref, do_ref, lse_ref, dproj_in, out_ref, dcq_ref, dck_ref, dk_acc, dv_acc, dq_ref):
        j = pl.program_id(1)

        @pl.when(j == 0)
        def _():
            dq_ref[...] = jnp.zeros(dq_ref.shape, F32)
            dcq_ref[...] = jnp.zeros(dcq_ref.shape, F32)

        dk_acc[...] = jnp.zeros(dk_acc.shape, F32)
        dv_acc[...] = jnp.zeros(dv_acc.shape, F32)

        def step(qb, diag):
            off = pl.multiple_of(qb * tq, tq)
            heads = range(G)
            qa = [q_ref[g, pl.ds(off, tq), :] for g in heads]
            da = [do_ref[g, pl.ds(off, tq), :] for g in heads]
            s_t = [lax.dot_general(k_ref[g], qa[g], NT, preferred_element_type=F32) for g in heads]
            dp_t = [lax.dot_general(v_ref[g], da[g], NT, preferred_element_type=F32) for g in heads]
            p_t = [jnp.exp(s_t[g] - lse_ref[g, qb]) for g in heads]
            if diag:
                r = lax.broadcasted_iota(jnp.int32, (tq, tq), 0)
                c = lax.broadcasted_iota(jnp.int32, (tq, tq), 1)
                p_t = [jnp.where(c >= r, p, 0.0) for p in p_t]
            dsb = [(p_t[g] * dp_t[g]).astype(BF16) for g in heads]
            dv = [jnp.dot(p_t[g].astype(BF16), da[g], preferred_element_type=F32) for g in heads]
            dk = [jnp.dot(dsb[g], qa[g], preferred_element_type=F32) for g in heads]
            dq = [lax.dot_general(k_ref[g], dsb[g], TN, preferred_element_type=F32) for g in heads]
            for g in heads:
                dv_acc[g] += dv[g]
                dk_acc[g] += dk[g]
                dq_ref[g, qb] += dq[g]
                dcq_ref[g, qb] += jnp.sum(dsb[g].astype(F32), axis=0, keepdims=True)

        step(j, True)

        def loop_body(qb, carry):
            step(qb, False)
            return carry

        lax.fori_loop(j + 1, nq, loop_body, 0)
        dk = [dk_acc[g] for g in range(G)]
        out_ref[:, 0:wide] = side_by_side([dq_ref[g, j].T for g in range(G)], SCALE).astype(out_ref.dtype)
        out_ref[:, wide:2 * wide] = side_by_side(dk).astype(out_ref.dtype)
        out_ref[:, 2 * wide:3 * wide] = side_by_side([dv_acc[g] for g in range(G)]).astype(out_ref.dtype)
        for g in range(G):
            kc = _fox_lanes(g % 2)[0]
            dck_ref[g] = -dk[g].T[kc:kc + 1, :]

    blk = pl.BlockSpec((G, tq, lanes), lambda h, j: (h, j, 0))
    full = pl.BlockSpec((G, T, lanes), lambda h, j: (h, 0, 0))
    wide = G * HEAD_DIM
    first = dproj.shape[1] // (3 * wide) - H // G
    grid = (H // G, nq)
    body, x_in, x_in_specs, x_out, x_out_specs, x_scr = _carry(ex, grid, 6, 3, body)
    rows = jax.ShapeDtypeStruct((H, nq, 1, tq), F32)
    all_rows = pl.BlockSpec((G, nq, 1, tq), lambda h, j: (h, 0, 0, 0))
    return pl.pallas_call(
        body,
        out_shape=(jax.ShapeDtypeStruct(dproj.shape, dproj.dtype), rows, rows, *x_out),
        grid=grid,
        in_specs=[full, blk, blk, full, all_rows, pl.BlockSpec(memory_space=pl.ANY)] + x_in_specs,
        out_specs=(pl.BlockSpec((tq, 3 * wide), lambda h, j: (j, first + h)), all_rows,
                   pl.BlockSpec((G, None, 1, tq), lambda h, j: (h, j, 0, 0)), *x_out_specs),
        input_output_aliases={5: 0},
        scratch_shapes=[pltpu.VMEM((G, tq, lanes), F32), pltpu.VMEM((G, tq, lanes), F32),
                        pltpu.VMEM((G, nq, lanes, tq), F32)] + x_scr,
        compiler_params=_params(("arbitrary", "arbitrary")), name=name,
    )(q_aug, k_aug, v_aug, do_aug, lse_row, dproj, *x_in)


def _t5_bucket_np(d):
    n = np.maximum(d, 0).astype(np.int32)
    max_exact = N_BUCKETS // 2
    nf = np.maximum(n, 1).astype(np.float32)
    large = max_exact + (np.log(nf / max_exact) / math.log(MAX_DISTANCE / max_exact)
                         * (N_BUCKETS - max_exact)).astype(np.int32)
    large = np.minimum(large, N_BUCKETS - 1)
    return np.where(n < max_exact, n, large)


def _bucket_onehots():
    k = np.arange(BLOCK)[:, None]
    q = np.arange(BLOCK)[None, :]
    eye = np.eye(N_BUCKETS, dtype=np.float32)
    cur = eye[_t5_bucket_np(q - k).reshape(-1)]
    prev = eye[_t5_bucket_np(BLOCK + q - k).reshape(-1)]
    return cur, prev


SWA_K_COL = SWA_Q_HEADS * HEAD_DIM // (2 * HEAD_DIM)
SWA_V_COL = SWA_K_COL + 1


def _swa_terms(raw, bc, bp, far, sink, n):
    k = lax.broadcasted_iota(jnp.int32, (BLOCK, BLOCK), 0)
    q = lax.broadcasted_iota(jnp.int32, (BLOCK, BLOCK), 1)
    never = 2 * BLOCK
    s_c = raw[0] + bc
    s_p = raw[1] + bp
    s_m = raw[2] + jnp.where(n == 1, bp, far)
    s_c = jnp.where((k <= q) & (k >= jnp.where(n >= 1, 0, PAD_ROWS)), s_c, NEG)
    s_p = jnp.where(k > q + jnp.where(n >= 2, 0, never), s_p, NEG)
    s_m = jnp.where(k >= jnp.where(n >= 1, PAD_ROWS, never), s_m, NEG)
    m = jnp.maximum(jnp.maximum(jnp.max(s_c, axis=0, keepdims=True), jnp.max(s_p, axis=0, keepdims=True)),
                    jnp.maximum(jnp.max(s_m, axis=0, keepdims=True), sink))
    e = [jnp.exp(s_c - m), jnp.exp(s_p - m), jnp.exp(s_m - m)]
    e_s = jnp.exp(sink - m)
    l = (jnp.sum(e[0], axis=0, keepdims=True) + jnp.sum(e[1], axis=0, keepdims=True)
         + jnp.sum(e[2], axis=0, keepdims=True) + e_s)
    return e, e_s, l


SWA_STEP = 3


def _swa_specs():
    R = SWA_STEP

    def window(col):
        return ([pl.BlockSpec((BLOCK, BLOCK), lambda s, w=w: (jnp.maximum(R * s - 1 + w, 0), col)) for w in range(R + 1)]
                + [pl.BlockSpec((BLOCK, BLOCK), lambda s: (0, col))])

    qblk = pl.BlockSpec((R * BLOCK, SWA_Q_HEADS * HEAD_DIM), lambda s: (s, 0))
    bias = pl.BlockSpec((SWA_Q_HEADS, BLOCK, BLOCK), lambda s: (0, 0, 0))
    smem = pl.BlockSpec(memory_space=pltpu.SMEM)
    return qblk, window(SWA_K_COL), window(SWA_V_COL), bias, smem


def _swa_own_kv(tile_ref, kv):
    lane = lax.broadcasted_iota(jnp.int32, (BLOCK, 2 * HEAD_DIM), 1)
    t = tile_ref[...].astype(F32)
    return jnp.where(lane // HEAD_DIM == kv, t, pltpu.roll(t, HEAD_DIM, 1)).astype(BF16)


def _swa_fwd(proj, bc, bp, far, sinks, *, name):
    T = proj.shape[0]
    nb = T // BLOCK
    G = SWA_GROUP
    Hq = SWA_Q_HEADS
    lanes = 2 * HEAD_DIM

    R = SWA_STEP
    assert nb % R == 0

    def body(*refs):
        q_ref, k_refs, v_refs = refs[0], refs[1:R + 3], refs[R + 3:2 * R + 5]
        bc_ref, bp_ref, far_ref, sink_ref, o_ref = refs[2 * R + 5:]
        s = pl.program_id(0)
        lane = lax.broadcasted_iota(jnp.int32, (BLOCK, lanes), 1)
        kvs = range(SWA_KV_HEADS)
        kk = [[_swa_own_kv(ref, kv) for ref in k_refs] for kv in kvs]
        vv = [[_swa_own_kv(ref, kv) for ref in v_refs] for kv in kvs]
        chains = [(r, h) for r in range(R) for h in range(Hq)]
        tiles = lambda r: (r + 1, r, R + 1)
        q2 = {(r, pair): q_ref[r * BLOCK:(r + 1) * BLOCK, pair * lanes:(pair + 1) * lanes].astype(F32) * SCALE
              for r in range(R) for pair in range(Hq // 2)}
        qm = {c: jnp.where(lane // HEAD_DIM == c[1] % 2, q2[c[0], c[1] // 2], 0.0).astype(BF16) for c in chains}
        raw = {c: [lax.dot_general(kk[c[1] // G][w], qm[c], NT, preferred_element_type=F32) for w in tiles(c[0])]
               for c in chains}
        terms = {c: _swa_terms(raw[c], bc_ref[c[1]], bp_ref[c[1]], far_ref[c[1]], sink_ref[c[1]], R * s + c[0])
                 for c in chains}
        o_t = {c: sum(lax.dot_general(vv[c[1] // G][w], terms[c][0][b].astype(BF16), TN, preferred_element_type=F32)
                      for b, w in enumerate(tiles(c[0]))) for c in chains}
        outs = {c: (o_t[c] / terms[c][2]).T for c in chains}
        for r in range(R):
            for pair in range(Hq // 2):
                o_ref[r * BLOCK:(r + 1) * BLOCK, pair * lanes:(pair + 1) * lanes] = jnp.where(
                    lane < HEAD_DIM, outs[r, 2 * pair], outs[r, 2 * pair + 1]).astype(o_ref.dtype)

    qblk, keys, vals, bias, smem = _swa_specs()
    return pl.pallas_call(
        body, out_shape=jax.ShapeDtypeStruct((T, D_MODEL), BF16), grid=(nb // R,),
        in_specs=[qblk] + keys + vals + [bias, bias, smem, smem],
        out_specs=qblk,
        compiler_params=_params(("parallel",)), name=name,
    )(proj, *([proj] * (2 * R + 4)), bc, bp, far, sinks)


def _swa_bwd(proj, dmix, bc, bp, far, sinks, *, ex=None, name):
    T, width = proj.shape
    nb = T // BLOCK
    G = SWA_GROUP
    Hq = SWA_Q_HEADS
    lanes = 2 * HEAD_DIM
    qw = Hq * HEAD_DIM
    own_w = qw + 2 * lanes

    R = SWA_STEP
    assert nb % R == 0
    n_in = 2 * R + 10

    def body(*refs):
        q_ref, k_refs, v_refs = refs[0], refs[1:R + 3], refs[R + 3:2 * R + 5]
        do_ref, bc_ref, bp_ref, far_ref, sink_ref = refs[2 * R + 5:n_in]
        dp_ref, dbc_ref, dbp_ref, dbf_ref, dsk_ref, dk_acc, dv_acc = refs[n_in:]
        s = pl.program_id(0)

        @pl.when(s == 0)
        def _():
            for ref in (dk_acc, dv_acc, dbc_ref, dbp_ref, dbf_ref, dsk_ref):
                ref[...] = jnp.zeros(ref.shape, F32)

        lane = lax.broadcasted_iota(jnp.int32, (BLOCK, lanes), 1)
        kvs = range(SWA_KV_HEADS)
        kk = [[_swa_own_kv(ref, kv) for ref in k_refs] for kv in kvs]
        vv = [[_swa_own_kv(ref, kv) for ref in v_refs] for kv in kvs]
        chains = [(r, h) for r in range(R) for h in range(Hq)]
        blocks = range(3)
        tiles = lambda r: (r + 1, r, R + 1)
        sub = lambda ref, r, pair: ref[r * BLOCK:(r + 1) * BLOCK, pair * lanes:(pair + 1) * lanes]
        q2 = {(r, pair): sub(q_ref, r, pair).astype(F32) * SCALE for r in range(R) for pair in range(Hq // 2)}
        d2 = {(r, pair): sub(do_ref, r, pair) for r in range(R) for pair in range(Hq // 2)}
        own = [lane // HEAD_DIM == half for half in range(2)]
        qm = {c: jnp.where(own[c[1] % 2], q2[c[0], c[1] // 2], 0.0).astype(BF16) for c in chains}
        dom = {c: jnp.where(own[c[1] % 2], d2[c[0], c[1] // 2], jnp.zeros_like(d2[0, 0])) for c in chains}
        raw = {c: [lax.dot_general(kk[c[1] // G][w], qm[c], NT, preferred_element_type=F32) for w in tiles(c[0])]
               for c in chains}
        dp = {c: [lax.dot_general(vv[c[1] // G][w], dom[c], NT, preferred_element_type=F32) for w in tiles(c[0])]
              for c in chains}
        p, ds16 = {}, {}
        for c in chains:
            r, h = c
            n = R * s + r
            e, e_s, l = _swa_terms(raw[c], bc_ref[h], bp_ref[h], far_ref[h], sink_ref[h], n)
            inv = 1.0 / l
            ph = [e[b] * inv for b in blocks]
            delta = sum(jnp.sum(ph[b] * dp[c][b], axis=0, keepdims=True) for b in blocks)
            ds = [ph[b] * (dp[c][b] - delta) for b in blocks]
            dsk_ref[h] += -(e_s * inv) * delta
            dbc_ref[h] += ds[0]
            dbp_ref[h] += ds[1] + jnp.where(n == 1, ds[2], 0.0)
            dbf_ref[h] += jnp.where(n >= 2, ds[2], 0.0)
            p[c] = [x.astype(BF16) for x in ph]
            ds16[c] = [x.astype(BF16) for x in ds]
        dq_t = {c: sum(lax.dot_general(kk[c[1] // G][w], ds16[c][b], TN, preferred_element_type=F32)
                       for b, w in enumerate(tiles(c[0]))) for c in chains}
        group = [range(kv * G, (kv + 1) * G) for kv in kvs]
        dk = {(r, kv): [sum(jnp.dot(ds16[r, h][b], qm[r, h], preferred_element_type=F32) for h in group[kv])
                        for b in blocks] for r in range(R) for kv in kvs}
        dv = {(r, kv): [sum(jnp.dot(p[r, h][b], dom[r, h], preferred_element_type=F32) for h in group[kv])
                        for b in blocks] for r in range(R) for kv in kvs}
        for r in range(R):
            n = R * s + r
            rows = pl.ds(pl.multiple_of(n * BLOCK, BLOCK), BLOCK)
            prev_rows = pl.ds(pl.multiple_of(jnp.maximum(n - 1, 0) * BLOCK, BLOCK), BLOCK)
            for pair in range(Hq // 2):
                dp_ref[rows, pair * lanes:(pair + 1) * lanes] = (jnp.where(
                    lane < HEAD_DIM, dq_t[r, 2 * pair].T, dq_t[r, 2 * pair + 1].T) * SCALE).astype(dp_ref.dtype)
            for acc, ref in ((dk, dk_acc), (dv, dv_acc)):
                tot = [[a + pltpu.roll(a, HEAD_DIM, 1) for a in acc[r, kv]] for kv in kvs]
                both = [jnp.where(lane < HEAD_DIM, tot[0][b], tot[1][b]) for b in blocks]
                ref[rows, :] += both[0]
                ref[prev_rows, :] += both[1]
                ref[0:BLOCK, :] += both[2]

        @pl.when(s == nb // R - 1)
        def _():
            dp_ref[:, qw:qw + lanes] = dk_acc[...].astype(dp_ref.dtype)
            dp_ref[:, qw + lanes:own_w] = dv_acc[...].astype(dp_ref.dtype)

    qblk, keys, vals, bias, smem = _swa_specs()
    dsk = pl.BlockSpec((Hq, 1, BLOCK), lambda s: (0, 0, 0))
    grid = (nb // R,)
    body, x_in, x_in_specs, x_out, x_out_specs, x_scr = _carry(ex, grid, n_in, 5, body)
    tile = jax.ShapeDtypeStruct((Hq, BLOCK, BLOCK), F32)
    return pl.pallas_call(
        body,
        out_shape=(jax.ShapeDtypeStruct((T, width), BF16), tile, tile, tile,
                   jax.ShapeDtypeStruct((Hq, 1, BLOCK), F32), *x_out),
        grid=grid,
        in_specs=[qblk] + keys + vals + [qblk, bias, bias, smem, smem] + x_in_specs,
        out_specs=(pl.BlockSpec((T, own_w), lambda s: (0, 0)), bias, bias, bias, dsk, *x_out_specs),
        scratch_shapes=[pltpu.VMEM((T, lanes), F32), pltpu.VMEM((T, lanes), F32)] + x_scr,
        compiler_params=_params(("arbitrary",)), name=name,
    )(proj, *([proj] * (2 * R + 4)), dmix, bc, bp, far, sinks, *x_in)


def _bias_tiles(tab_t, oh_cur_t, oh_prev_t, *, name):
    Hq = tab_t.shape[0]

    def body(t_ref, oc_ref, op_ref, bc_ref, bp_ref):
        bc_ref[...] = jnp.dot(t_ref[...], oc_ref[...], precision=HIGHEST, preferred_element_type=F32)
        bp_ref[...] = jnp.dot(t_ref[...], op_ref[...], precision=HIGHEST, preferred_element_type=F32)

    vm = pl.BlockSpec(memory_space=pltpu.VMEM)
    shp = jax.ShapeDtypeStruct((Hq, BLOCK * BLOCK), F32)
    bc, bp = pl.pallas_call(body, out_shape=(shp, shp), in_specs=[vm] * 3, out_specs=(vm, vm),
                            compiler_params=_params(), name=name)(tab_t, oh_cur_t, oh_prev_t)
    return bc.reshape(Hq, BLOCK, BLOCK), bp.reshape(Hq, BLOCK, BLOCK)


def _small_grads(dbc, dbp, dbf, dsk, oh_cur, oh_prev, *, ex=None, name):
    Hq = dbc.shape[0]

    def body(dbc_ref, dbp_ref, dbf_ref, dsk_ref, oc_ref, op_ref, tab_ref, sink_ref):
        tab = (jnp.dot(dbc_ref[...], oc_ref[...], precision=HIGHEST, preferred_element_type=F32)
               + jnp.dot(dbp_ref[...], op_ref[...], precision=HIGHEST, preferred_element_type=F32))
        far = jnp.sum(dbf_ref[...], axis=1, keepdims=True)
        last = lax.broadcasted_iota(jnp.int32, (Hq, N_BUCKETS), 1) == N_BUCKETS - 1
        tab_ref[...] = tab + jnp.where(last, far, 0.0)
        sink_ref[...] = jnp.sum(dsk_ref[...], axis=1, keepdims=True)

    vm = pl.BlockSpec(memory_space=pltpu.VMEM)
    body, x_in, x_in_specs, x_out, x_out_specs, x_scr = _carry(ex, (), 6, 2, body)
    return pl.pallas_call(
        body, out_shape=(jax.ShapeDtypeStruct((Hq, N_BUCKETS), F32), jax.ShapeDtypeStruct((Hq, 1), F32), *x_out),
        in_specs=[vm] * 6 + x_in_specs, out_specs=(vm, vm, *x_out_specs), scratch_shapes=x_scr,
        compiler_params=_params(), name=name,
    )(dbc.reshape(Hq, -1), dbp.reshape(Hq, -1), dbf.reshape(Hq, -1), dsk.reshape(Hq, -1), oh_cur, oh_prev, *x_in)


def _coords():
    return lax.axis_index("x"), lax.axis_index("y"), lax.axis_index("c")


class _Exchange:
    def __init__(self, inputs, out_shapes, scratch, start, finish):
        self.inputs, self.out_shapes, self.scratch, self.start, self.finish = inputs, out_shapes, scratch, start, finish


def _carry(ex, grid, n_in, n_out, body):
    if ex is None:
        return body, [], [], [], [], []
    ni, no = len(ex.inputs), len(ex.out_shapes)

    def at_step(which):
        cond = jnp.bool_(True)
        for axis, n in enumerate(grid):
            cond = cond & (pl.program_id(axis) == (0 if which == "first" else n - 1))
        return cond

    def wrapped(*refs):
        refs = list(refs)
        n_own_scr = len(refs) - (n_in + ni + n_out + no) - len(ex.scratch)
        own_in, side_in = refs[:n_in], refs[n_in:n_in + ni]
        own_out = refs[n_in + ni:n_in + ni + n_out]
        side_out = refs[n_in + ni + n_out:n_in + ni + n_out + no]
        rest = refs[n_in + ni + n_out + no:]
        own_scr, sems = rest[:n_own_scr], rest[n_own_scr:]

        @pl.when(at_step("first"))
        def _():
            ex.start(side_in, side_out, sems)

        body(*own_in, *own_out, *own_scr)

        @pl.when(at_step("last"))
        def _():
            ex.finish(side_in, side_out, sems)

    hbm = pl.BlockSpec(memory_space=pl.ANY)
    return wrapped, list(ex.inputs), [hbm] * ni, list(ex.out_shapes), [hbm] * no, list(ex.scratch)


def _gather_exchange(shards):
    nt = len(shards)

    def copies(ins, outs, sems):
        send_sems, recv_sems, local_sems = sems
        x, y, c = _coords()
        me, sibling = (x, y, c), (x, y, 1 - c)
        chips = [(1 - x, y), (x, 1 - y), (1 - x, 1 - y)]

        def slot(t, dev):
            return outs[t].at[4 * dev[0] + 2 * dev[1] + dev[2]]

        def copy(t, k, block, to, src=None):
            dst = slot(t, block)
            return pltpu.make_async_remote_copy(
                src_ref=dst if src is None else src, dst_ref=dst,
                send_sem=send_sems.at[t, k], recv_sem=recv_sems.at[t, k], device_id=to, device_id_type=MESH)

        mine = [pltpu.make_async_copy(ins[t], slot(t, me), local_sems.at[t]) for t in range(nt)]
        first = []
        for t in range(nt):
            first.append(copy(t, 0, me, sibling, src=ins[t]))
            first += [copy(t, 1 + j, me, (*chip, c), src=ins[t]) for j, chip in enumerate(chips)]
        return copy, mine, first, me, sibling, chips, c

    def start(ins, outs, sems):
        _, mine, first, *_ = copies(ins, outs, sems)
        for cp in mine + first:
            cp.start()

    def finish(ins, outs, sems):
        copy, mine, first, me, sibling, chips, c = copies(ins, outs, sems)
        passed = []
        for j, chip in enumerate(chips):
            for t in range(nt):
                copy(t, 1 + j, (*chip, c), me).wait_recv()
                cp = copy(t, 4 + j, (*chip, c), sibling)
                cp.start()
                passed.append(cp)
        for t in range(nt):
            copy(t, 0, sibling, me).wait_recv()
            for j, chip in enumerate(chips):
                copy(t, 4 + j, (*chip, 1 - c), me).wait_recv()
        for cp in first + passed:
            cp.wait_send()
        for cp in mine:
            cp.wait()

    return _Exchange(
        list(shards), [jax.ShapeDtypeStruct((N_DEV,) + s.shape, s.dtype) for s in shards],
        [pltpu.SemaphoreType.DMA((nt, 7)), pltpu.SemaphoreType.DMA((nt, 7)), pltpu.SemaphoreType.DMA((nt,))],
        start, finish)


def _swap_exchange(arrays, n_slices, copies):
    nt = len(arrays)

    def start(ins, outs, sems):
        for cp in copies(ins, outs, sems):
            cp.start()

    def finish(ins, outs, sems):
        sends = copies(ins, outs, sems)
        for cp in sends:
            cp.wait_recv()
        for cp in sends:
            cp.wait_send()

    return _Exchange(
        list(arrays), [jax.ShapeDtypeStruct((n_slices,) + a.shape[1:], a.dtype) for a in arrays],
        [pltpu.SemaphoreType.DMA((nt, n_slices)), pltpu.SemaphoreType.DMA((nt, n_slices))], start, finish)


def _cores_exchange(gs):
    def copies(ins, outs, sems):
        send_sems, recv_sems = sems
        x, y, c = _coords()
        return [pltpu.make_async_remote_copy(
            src_ref=ins[t].at[2 * j + (1 - c)], dst_ref=outs[t].at[j],
            send_sem=send_sems.at[t, j], recv_sem=recv_sems.at[t, j], device_id=(x, y, 1 - c), device_id_type=MESH)
            for t in range(len(gs)) for j in range(4)]

    return _swap_exchange(gs, 4, copies)


def _chips_exchange(ps):
    def copies(ins, outs, sems):
        send_sems, recv_sems = sems
        x, y, c = _coords()
        peers = [(1 - x, y), (x, 1 - y), (1 - x, 1 - y)]
        return [pltpu.make_async_remote_copy(
            src_ref=ins[t].at[2 * px + py], dst_ref=outs[t].at[k],
            send_sem=send_sems.at[t, k], recv_sem=recv_sems.at[t, k], device_id=(px, py, c), device_id_type=MESH)
            for t in range(len(ps)) for k, (px, py) in enumerate(peers)]

    return _swap_exchange(ps, 3, copies)


def _add_cores(g, r, core, *, name):
    _, A, B = g.shape
    ta = _tile(A, 512, 16)

    def body(core_ref, a_ref, b_ref, o16_ref):
        o16_ref[...] = (a_ref[...] + b_ref[...]).astype(BF16)

    blk = (None, ta, B)
    return pl.pallas_call(
        body, out_shape=jax.ShapeDtypeStruct((4, A, B), BF16),
        grid_spec=pltpu.PrefetchScalarGridSpec(
            num_scalar_prefetch=1, grid=(4, A // ta),
            in_specs=[pl.BlockSpec(blk, lambda j, i, core_ref: (2 * j + core_ref[0], i, 0)),
                      pl.BlockSpec(blk, lambda j, i, core_ref: (j, i, 0))],
            out_specs=pl.BlockSpec(blk, lambda j, i, core_ref: (j, i, 0))),
        compiler_params=_params(("parallel", "parallel")), name=name)(core, g, r)


def _adamw_math(w, g, m, v):
    m = ADAM_B1 * m + (1.0 - ADAM_B1) * g
    v = ADAM_B2 * v + (1.0 - ADAM_B2) * (g * g)
    m_hat = m / (1.0 - ADAM_B1 ** ADAM_STEP)
    v_hat = v / (1.0 - ADAM_B2 ** ADAM_STEP)
    delta = -ADAM_LR * (m_hat / (jnp.sqrt(v_hat) + ADAM_EPS) + ADAM_WD * w)
    return delta, m, v


def _sum_adamw(mine, sib, r, where, w, m, v, *, ta, name):
    Aw, Bw = w.shape
    Bg = mine.shape[2]
    assert Aw % ta == 0 and Bw <= Bg and mine.shape[1] == Aw

    def body(where_ref, p_ref, s_ref, r0, r1, r2, w_ref, m_ref, v_ref, g_out, d_out, m_out, v_out):
        g = (((p_ref[:, :Bw] + s_ref[:, :Bw]) + r0[:, :Bw].astype(F32))
             + r1[:, :Bw].astype(F32)) + r2[:, :Bw].astype(F32)
        delta, m_new, v_new = _adamw_math(w_ref[...], g, m_ref[...], v_ref[...])
        g_out[...] = g
        d_out[...] = delta
        m_out[...] = m_new
        v_out[...] = v_new

    gblk = (None, ta, Bg)
    row = pl.BlockSpec((ta, Bw), lambda i, where_ref: (i, 0))
    rspecs = [pl.BlockSpec(gblk, (lambda i, where_ref, k=k: (k, i, 0))) for k in range(3)]
    shp = jax.ShapeDtypeStruct((Aw, Bw), F32)
    return pl.pallas_call(
        body, out_shape=(shp, shp, shp, shp),
        grid_spec=pltpu.PrefetchScalarGridSpec(
            num_scalar_prefetch=1, grid=(Aw // ta,),
            in_specs=[pl.BlockSpec(gblk, lambda i, where_ref: (2 * where_ref[0] + where_ref[1], i, 0)),
                      pl.BlockSpec(gblk, lambda i, where_ref: (where_ref[0], i, 0))] + rspecs + [row, row, row],
            out_specs=(row, row, row, row)),
        compiler_params=_params(("parallel",)), name=name)(where, mine, sib, r, r, r, w, m, v)


def _adamw(w, g, m, v, *, name):
    def body(w_ref, g_ref, m_ref, v_ref, d_out, m_out, v_out):
        delta, m_new, v_new = _adamw_math(w_ref[...], g_ref[...], m_ref[...], v_ref[...])
        d_out[...] = delta
        m_out[...] = m_new
        v_out[...] = v_new

    vm = pl.BlockSpec(memory_space=pltpu.VMEM)
    shp = jax.ShapeDtypeStruct(w.shape, F32)
    return pl.pallas_call(body, out_shape=(shp, shp, shp), in_specs=[vm] * 4, out_specs=(vm, vm, vm),
                          compiler_params=_params(), name=name)(w, g, m, v)


def _small_allreduce_adamw(s, w, m, v, *, name):
    R, W = s.shape

    def body(s_ref, w_ref, m_ref, v_ref, g_out, d_out, m_out, v_out, gath, send_sems, recv_sems):
        x, y, c = _coords()
        mine = 4 * x + 2 * y + c
        gath[mine] = s_ref[...]
        peers = [((1 - x) if k & 4 else x, (1 - y) if k & 2 else y, (1 - c) if k & 1 else c) for k in range(1, N_DEV)]
        sends = []
        for k in range(1, N_DEV):
            peer = peers[k - 1]
            sends.append(pltpu.make_async_remote_copy(
                src_ref=s_ref, dst_ref=gath.at[mine], send_sem=send_sems.at[k - 1], recv_sem=recv_sems.at[k - 1],
                device_id=peer, device_id_type=MESH))
        for cp in sends:
            cp.start()
        for k in range(1, N_DEV):
            peer = peers[k - 1]
            pltpu.make_async_remote_copy(
                src_ref=s_ref, dst_ref=gath.at[4 * peer[0] + 2 * peer[1] + peer[2]],
                send_sem=send_sems.at[k - 1], recv_sem=recv_sems.at[k - 1],
                device_id=peer, device_id_type=MESH).wait_recv()
        for cp in sends:
            cp.wait_send()
        g = gath[0]
        for d in range(1, N_DEV):
            g = g + gath[d]
        delta, m_new, v_new = _adamw_math(w_ref[...], g, m_ref[...], v_ref[...])
        g_out[...] = g
        d_out[...] = delta
        m_out[...] = m_new
        v_out[...] = v_new

    vm = pl.BlockSpec(memory_space=pltpu.VMEM)
    shp = jax.ShapeDtypeStruct((R, W), F32)
    return pl.pallas_call(
        body, out_shape=(shp, shp, shp, shp), in_specs=[vm] * 4, out_specs=(vm, vm, vm, vm),
        scratch_shapes=[pltpu.VMEM((N_DEV, R, W), F32), pltpu.SemaphoreType.DMA((N_DEV - 1,)),
                        pltpu.SemaphoreType.DMA((N_DEV - 1,))],
        compiler_params=_params(), name=name)(s, w, m, v)


def _pack_small(rel_bias, g1, g2, g3, g4, b_forget, sinks, extra=None, meta=None):
    misc = jnp.concatenate([rel_bias.reshape(-1), b_forget.reshape(-1), sinks.reshape(-1)])
    misc = jnp.concatenate([misc, jnp.zeros((D_MODEL - misc.shape[0],), F32)])[None]
    last = jnp.zeros((1, D_MODEL), F32) if extra is None else extra
    meta = jnp.zeros((N_META, D_MODEL), F32) if meta is None else meta
    return jnp.concatenate([g1, g2, g3, g4, misc, last, jnp.zeros((2, D_MODEL), F32), meta], axis=0)


def _unpack_small(p):
    nrb = N_BUCKETS * SWA_Q_HEADS
    misc = p[4]
    return dict(rel_bias=misc[:nrb].reshape(N_BUCKETS, SWA_Q_HEADS), ln_pre_mix=p[0:1], ln_post_mix=p[1:2],
                ln_pre_ffn=p[2:3], ln_post_ffn=p[3:4], b_forget=misc[nrb:nrb + 8].reshape(1, 8),
                sinks=misc[nrb + 8:nrb + 16].reshape(1, 8))


def _proj_runs():
    gw = FOX_GROUP * HEAD_DIM
    swa = SWA_Q_W + 2 * SWA_KV_HEADS * HEAD_DIM
    runs = [(0, swa)]
    for grp in range(FOX_HEADS // FOX_GROUP):
        runs += [(swa + part * FOX_W + grp * gw, swa + part * FOX_W + (grp + 1) * gw) for part in range(3)]
    return runs


def _columns_from_shards(gathered, runs, shard):
    pieces = []
    for start, stop in runs:
        for d in range(start // shard, (stop - 1) // shard + 1):
            lo = d * shard
            pieces.append(gathered[d][:, max(start, lo) - lo:min(stop, lo + shard) - lo])
    return jnp.concatenate(pieces, axis=1)


def _device_shards(qkv, gate, shard, padded):
    pos, segments = 0, []
    for start, stop in _proj_runs():
        segments.append((start, stop, qkv, pos))
        pos += stop - start
    segments.append((pos, pos + gate.shape[1], gate, 0))
    total = pos + gate.shape[1]
    assert total % shard == 0
    zeros = jnp.zeros((qkv.shape[0], padded - shard), qkv.dtype)
    out = []
    for d in range(total // shard):
        lo, hi = d * shard, (d + 1) * shard
        pieces = [arr[:, src + max(lo, s) - s:src + min(hi, e) - s]
                  for s, e, arr, src in sorted(segments, key=lambda seg: seg[0]) if max(lo, s) < min(hi, e)]
        out.append(jnp.concatenate(pieces + [zeros], axis=1))
    return jnp.stack(out)


def kernel(x, meta_tokens, rel_bias, ln_pre_mix, ln_post_mix, ln_pre_ffn, ln_post_ffn, w_in, b_forget, sinks, w_out, w_gate_up, w_down, loss_target, m_meta_tokens, m_rel_bias, m_ln_pre_mix, m_ln_post_mix, m_ln_pre_ffn, m_ln_post_ffn, m_w_in, m_b_forget, m_sinks, m_w_out, m_w_gate_up, m_w_down, v_meta_tokens, v_rel_bias, v_ln_pre_mix, v_ln_post_mix, v_ln_pre_ffn, v_ln_post_ffn, v_w_in, v_b_forget, v_sinks, v_w_out, v_w_gate_up, v_w_down):
    seq = x.shape[1]
    T = BLOCK + seq
    assert T % FOX_TILE == 0
    nq = T // FOX_TILE
    tm = _tile(T, 1056)
    cin = w_in.shape[2]
    hid = w_down.shape[1]
    F = N_DEV * hid
    assert w_gate_up.shape[2] == 2 * hid and cin <= W_IN_PAD and hid % 16 == 0

    x_i, y_i, c_i = _coords()
    core = jnp.reshape(c_i, (1,)).astype(jnp.int32)
    where = jnp.stack([2 * x_i + y_i, c_i]).astype(jnp.int32)
    w_in_s = jnp.pad(w_in[0].astype(BF16), ((0, 0), (0, W_IN_PAD - cin)))
    w_gu_t = w_gate_up[0].T
    h0, target, hn1, hn1_t, g_in, _ = _pad_rows_rms(x[0], loss_target[0], ln_pre_mix,
                                                    _gather_exchange([w_in_s, meta_tokens]), name="ag_w_in_rms_pre_mix")
    gather_rest = _gather_exchange([w_out[0].astype(BF16), w_gu_t.astype(BF16), w_down[0].astype(BF16)])
    w_qkv = _columns_from_shards(g_in, _proj_runs(), cin)
    w_f = jnp.pad(_columns_from_shards(g_in, [(D_QKV, D_PROJ)], cin), ((0, 0), (0, BLOCK - FOX_HEADS)))

    proj = _matmul(hn1, w_qkv, out_dtype=BF16, tm=tm, tn=D_QKV, name="mm_in_proj")
    proj_f = _matmul(hn1, w_f, out_dtype=F32, tm=tm, tn=BLOCK, name="mm_in_proj_f")

    f_t = proj_f[:, :FOX_HEADS].T
    bf_col = b_forget.reshape(FOX_HEADS, 1)

    oh_cur, oh_prev = _bucket_onehots()
    bias_c, bias_p = _bias_tiles(rel_bias.T, jnp.asarray(oh_cur.T), jnp.asarray(oh_prev.T), name="bias_tiles")
    far = rel_bias[N_BUCKETS - 1]
    sink_v = sinks[0]
    mix_a = _swa_fwd(proj, bias_c, bias_p, far, sink_v, name="swa_fwd")

    cum_col = _fox_gates_fwd(f_t, bf_col, name="fox_gates_fwd")
    q_b, k_b, v_b = _fox_prep(proj, cum_col, name="fox_prep")
    mix, lse_row, g_out, g_gu, g_down = _fox_fwd(q_b, k_b, v_b, mix_a, ex=gather_rest, name="fox_fwd")
    w_out_full = g_out.reshape(D_MODEL, D_MODEL)
    w_gu_full_t = g_gu.reshape(2 * F, D_MODEL)
    w_down_full = g_down.reshape(F, D_MODEL)

    a1 = _matmul(mix, w_out_full, out_dtype=F32, tm=tm, tn=D_MODEL, name="mm_out_proj")
    h1, hn2 = _post_res_norm(a1, ln_post_mix, h0, ln_pre_ffn, name="post_mix_pre_ffn")
    gate, up, act, act_t = _gate_up_swiglu(hn2, w_gu_full_t, name="mm_gate_up")
    ff = _matmul(act, w_down_full, out_dtype=F32, tm=tm, tn=D_MODEL, name="mm_down")
    dh2, dff, dg_post_ffn, loss_acc = _loss_head(ff, ln_post_ffn, h1, target, name="loss_head")

    dgu = _d_act_swiglu(dff, w_down_full, gate, up, name="mm_d_act")
    d_w_down = _matmul(act_t, dff, out_dtype=F32, tm=_tile(F, 768), tn=D_MODEL, name="mm_dw_down")
    dhn2 = _matmul(dgu, w_gu_full_t, out_dtype=F32, tm=tm, tn=512, name="mm_d_hn2")
    d_w_gu_t = _matmul(dgu, hn2, ta=True, out_dtype=F32, tm=512, tn=D_MODEL, name="mm_dw_gate_up")
    dh1, dg_pre_ffn, da1, dg_post_mix = _rms_bwd_twice(h1, ln_pre_ffn, dhn2, dh2, a1, ln_post_mix,
                                                       name="rms_bwd_pre_ffn_post_mix")
    dmix = _matmul(da1, w_out_full, nt=True, out_dtype=BF16, tm=tm, tn=D_MODEL, name="mm_d_mix")
    d_w_out = _matmul(mix, da1, ta=True, out_dtype=F32, tm=512, tn=D_MODEL, name="mm_dw_out")

    ffn_grads = [g.reshape(N_DEV, -1, D_MODEL) for g in (d_w_out, d_w_gu_t, d_w_down)]
    dproj_a, dbc, dbp, dbf, dsk, *ffn_sibling = _swa_bwd(
        proj, dmix, bias_c, bias_p, far, sink_v, ex=_cores_exchange(ffn_grads), name="swa_bwd")
    ffn_sums = [_add_cores(g, r, core, name="rs_add_" + t)
                for g, r, t in zip(ffn_grads, ffn_sibling, ["w_out", "w_gate_up", "w_down"])]

    do_b = _fox_prep_bwd(dmix, mix, name="fox_prep_bwd")
    dproj, dcq, dck, *ffn_chips = _fox_bwd(
        q_b, k_b, v_b, do_b, lse_row, dproj_a, ex=_chips_exchange(ffn_sums), name="fox_bwd")
    df_t, d_bf = _fox_gates_bwd(dcq.reshape(FOX_HEADS, T), dck.reshape(FOX_HEADS, T), f_t, bf_col,
                                name="fox_gates_bwd")
    df = jnp.pad(df_t.T.astype(BF16), ((0, 0), (0, BLOCK - FOX_HEADS)))

    d_w_qkv = _matmul(hn1_t, dproj, out_dtype=F32, tm=D_MODEL, tn=768, name="mm_dw_in")
    d_w_f = _matmul(hn1_t, df, out_dtype=F32, tm=512, tn=BLOCK, name="mm_dw_in_f")
    d_w_in = _device_shards(d_w_qkv, d_w_f[:, :FOX_HEADS], cin, W_IN_PAD)
    d_tab, d_sink, in_sibling = _small_grads(dbc, dbp, dbf, dsk, jnp.asarray(oh_cur), jnp.asarray(oh_prev),
                                             ex=_cores_exchange([d_w_in]), name="small_grads")
    in_sum = _add_cores(d_w_in, in_sibling, core, name="rs_add_w_in")
    dhn1, in_chips = _matmul(dproj, w_qkv, nt=True, out_dtype=F32, tm=tm, tn=512,
                             ex=_chips_exchange([in_sum]), name="mm_d_hn1")
    dx_rows, dg_pre_mix, dh0_head = _rms_bwd_rows(h0, ln_pre_mix, dhn1, df, w_f, dh1, name="rms_bwd_pre_mix")
    grad_x = dx_rows[None]
    d_meta = dh0_head[PAD_ROWS:]

    rs_out, rs_gu, rs_down = zip(ffn_grads, ffn_sibling, ffn_chips)
    updates = [("w_in", (d_w_in, in_sibling, in_chips), (w_in[0], m_w_in[0], v_w_in[0]), 256),
               ("w_out", rs_out, (w_out[0], m_w_out[0], v_w_out[0]), BLOCK),
               ("w_gate_up", rs_gu, (w_gu_t, m_w_gate_up[0].T, v_w_gate_up[0].T), hid),
               ("w_down", rs_down, (w_down[0], m_w_down[0], v_w_down[0]), hid)]
    big = [{}, {}, {}, {}]
    for t, grads, shard, ta in updates:
        res = _sum_adamw(*grads, where, *shard, ta=ta, name="rs_adamw_" + t)
        for kind in range(4):
            big[kind][t] = (res[kind].T if t == "w_gate_up" else res[kind])[None]

    loss_row = jnp.pad(loss_acc[0:1, 0:1] * (0.5 / D_MODEL), ((0, 0), (0, D_MODEL - 1)))
    s_small = _pack_small(d_tab.T, dg_pre_mix, dg_post_mix, dg_pre_ffn, dg_post_ffn, d_bf, d_sink,
                          extra=loss_row, meta=d_meta)
    w_s = _pack_small(rel_bias, ln_pre_mix, ln_post_mix, ln_pre_ffn, ln_post_ffn, b_forget, sinks)
    m_s = _pack_small(m_rel_bias, m_ln_pre_mix, m_ln_post_mix, m_ln_pre_ffn, m_ln_post_ffn, m_b_forget, m_sinks)
    v_s = _pack_small(v_rel_bias, v_ln_pre_mix, v_ln_post_mix, v_ln_pre_ffn, v_ln_post_ffn, v_b_forget, v_sinks)
    small = _small_allreduce_adamw(s_small, w_s, m_s, v_s, name="small_allreduce_adamw")
    loss = small[0][5, 0]
    mcols = meta_tokens.shape[1]
    g_meta_mine = lax.dynamic_slice(small[0][8:8 + N_META], (0, (4 * x_i + 2 * y_i + c_i) * mcols), (N_META, mcols))
    big[0]["meta_tokens"] = g_meta_mine
    for kind, arr in enumerate(_adamw(meta_tokens, g_meta_mine, m_meta_tokens, v_meta_tokens, name="adamw_meta")):
        big[kind + 1]["meta_tokens"] = arr
    small = [_unpack_small(p) for p in small]

    names = ["meta_tokens", "rel_bias", "ln_pre_mix", "ln_post_mix", "ln_pre_ffn", "ln_post_ffn", "w_in",
             "b_forget", "sinks", "w_out", "w_gate_up", "w_down"]
    outs = [loss, grad_x]
    for kind in range(4):
        for nme in names:
            outs.append(big[kind][nme] if nme in big[kind] else small[kind][nme])
    return tuple(outs)
```

```python
import math

import numpy as np
import jax
import jax.numpy as jnp
from jax import lax
from jax.experimental import pallas as pl
from jax.experimental.pallas import tpu as pltpu

F32 = jnp.float32
BF16 = jnp.bfloat16
HIGHEST = lax.Precision.HIGHEST
MESH = pl.DeviceIdType.MESH

N_DEV = 8
D_MODEL = 1024
N_META = 16
HEAD_DIM = 64
SWA_Q_HEADS = 8
SWA_KV_HEADS = 2
SWA_GROUP = 4
FOX_HEADS = 8
FOX_W = FOX_HEADS * HEAD_DIM
SWA_Q_W = SWA_Q_HEADS * HEAD_DIM
BLOCK = 128
PAD_ROWS = BLOCK - N_META
N_BUCKETS = 32
MAX_DISTANCE = 128
D_FF = 2816
D_QKV = 2304
D_PROJ = D_QKV + FOX_HEADS
D_PROJ_PAD = 2560
EPS = 1e-6
NEG = -1e30
SCALE = HEAD_DIM ** -0.5
ADAM_LR, ADAM_B1, ADAM_B2, ADAM_EPS, ADAM_WD, ADAM_STEP = 0.001, 0.9, 0.999, 1e-08, 0.01, 10
VMEM_LIMIT = 56 * 1024 * 1024
FOX_TILE = 384
FOX_GROUP = 4
W_IN_PAD = 384

NT = (((1,), (1,)), ((), ()))
NN = (((1,), (0,)), ((), ()))
TN = (((0,), (0,)), ((), ()))


def _params(sem=None, **kw):
    if sem is not None:
        kw["dimension_semantics"] = sem
    return pltpu.CompilerParams(vmem_limit_bytes=VMEM_LIMIT, **kw)


def _tile(n, target, mult=16):
    best = None
    for t in range(mult, min(n, target) + 1, mult):
        if n % t == 0:
            best = t
    assert best is not None, (n, target)
    return best


def _matmul(a, b, *, nt=False, ta=False, out_dtype, tm, tn, tk=None, ex=None, name):
    M, K = a.shape[::-1] if ta else a.shape
    assert not (ta and nt)
    N = b.shape[0] if nt else b.shape[1]
    tk = K if tk is None else tk
    assert M % tm == 0 and N % tn == 0 and K % tk == 0, (name, a.shape, b.shape, tm, tn, tk)
    nk = K // tk
    dn = NT if nt else (TN if ta else NN)
    a_spec = pl.BlockSpec((tk, tm), lambda i, j, k: (k, i)) if ta else pl.BlockSpec((tm, tk), lambda i, j, k: (i, k))

    def body(a_ref, b_ref, o_ref, *scr):
        part = lax.dot_general(a_ref[...], b_ref[...], dn, preferred_element_type=F32)
        if nk == 1:
            o_ref[...] = part.astype(o_ref.dtype)
        else:
            acc = scr[0]
            k = pl.program_id(2)

            @pl.when(k == 0)
            def _():
                acc[...] = part

            @pl.when(k > 0)
            def _():
                acc[...] += part

            @pl.when(k == nk - 1)
            def _():
                o_ref[...] = acc[...].astype(o_ref.dtype)

    if nt:
        b_spec = pl.BlockSpec((tn, tk), lambda i, j, k: (j, k))
    else:
        b_spec = pl.BlockSpec((tk, tn), lambda i, j, k: (k, j))
    out_shape = jax.ShapeDtypeStruct((M, N), out_dtype)
    out_spec = pl.BlockSpec((tm, tn), lambda i, j, k: (i, j))
    grid = (M // tm, N // tn, nk)
    body, x_in, x_in_specs, x_out, x_out_specs, x_scr = _carry(ex, grid, 2, 1, body)
    res = pl.pallas_call(
        body,
        out_shape=(out_shape, *x_out),
        grid=grid,
        in_specs=[a_spec, b_spec] + x_in_specs,
        out_specs=(out_spec, *x_out_specs),
        scratch_shapes=([pltpu.VMEM((tm, tn), F32)] if nk > 1 else []) + x_scr,
        compiler_params=_params(("parallel", "parallel", "arbitrary") if ex is None else ("arbitrary",) * 3),
        name=name,
    )(a, b, *x_in)
    return res[0] if ex is None else res


def _rstd(x):
    return lax.rsqrt(jnp.mean(x * x, axis=-1, keepdims=True) + EPS)


def _pad_rows_rms(x, target, g, ex, *, name):
    S, D = x.shape
    nb = S // BLOCK + 1
    ni, no = len(ex.inputs), len(ex.out_shapes)
    mcols = D // N_DEV

    def body(x_ref, t_ref, g_ref, *rest):
        side_in, (h_ref, to_ref, y_ref, yt_ref) = rest[:ni], rest[ni:ni + 4]
        side_out = rest[ni + 4:ni + 4 + no]
        meta_buf, meta_sems, *sems = rest[ni + 4 + no:]
        i = pl.program_id(0)

        @pl.when(i == 0)
        def _():
            ex.start(side_in, side_out, sems)

        def norm():
            h = h_ref[...]
            y = h * _rstd(h) * g_ref[...]
            y_ref[...] = y.astype(y_ref.dtype)
            yt_ref[...] = y.T.astype(yt_ref.dtype)

        @pl.when(i < nb - 1)
        def _():
            h_ref[...] = x_ref[...]
            to_ref[...] = t_ref[...]
            norm()

        @pl.when(i == nb - 1)
        def _():
            ex.finish(side_in, side_out, sems)
            copies = [pltpu.make_async_copy(side_out[-1].at[d], meta_buf.at[:, d * mcols:(d + 1) * mcols],
                                            meta_sems.at[d]) for d in range(N_DEV)]
            for cp in copies:
                cp.start()
            for cp in copies:
                cp.wait()
            h_ref[:PAD_ROWS, :] = jnp.zeros((PAD_ROWS, D), F32)
            h_ref[PAD_ROWS:, :] = meta_buf[...]
            to_ref[...] = jnp.zeros_like(to_ref)
            norm()

    src = pl.BlockSpec((BLOCK, D), lambda i: (jnp.minimum(i, nb - 2), 0))
    dst = pl.BlockSpec((BLOCK, D), lambda i: ((i + 1) % nb, 0))
    hbm = pl.BlockSpec(memory_space=pl.ANY)
    rows = jax.ShapeDtypeStruct((BLOCK + S, D), F32)
    return pl.pallas_call(
        body,
        out_shape=(rows, rows, jax.ShapeDtypeStruct((BLOCK + S, D), BF16), jax.ShapeDtypeStruct((D, BLOCK + S), BF16),
                   *ex.out_shapes),
        grid=(nb,),
        in_specs=[src, src, pl.BlockSpec((1, D), lambda i: (0, 0))] + [hbm] * ni,
        out_specs=(dst, dst, dst, pl.BlockSpec((D, BLOCK), lambda i: (0, (i + 1) % nb)), *([hbm] * no)),
        scratch_shapes=[pltpu.VMEM((N_META, D), F32), pltpu.SemaphoreType.DMA((N_DEV,))] + list(ex.scratch),
        compiler_params=_params(("arbitrary",)), name=name)(x, target, g, *ex.inputs)


def _post_res_norm(a, g_post, h, g_pre, *, name):
    T, D = a.shape
    tm = _tile(T, 384, BLOCK)

    def body(a_ref, gp_ref, h_ref, gn_ref, h1_ref, o_ref):
        a = a_ref[...]
        h1 = h_ref[...] + a * _rstd(a) * gp_ref[...]
        h1_ref[...] = h1
        o_ref[...] = (h1 * _rstd(h1) * gn_ref[...]).astype(o_ref.dtype)

    row = pl.BlockSpec((tm, D), lambda i: (i, 0))
    vec = pl.BlockSpec((1, D), lambda i: (0, 0))
    return pl.pallas_call(
        body, out_shape=(jax.ShapeDtypeStruct((T, D), F32), jax.ShapeDtypeStruct((T, D), BF16)), grid=(T // tm,),
        in_specs=[row, vec, row, vec], out_specs=(row, row),
        compiler_params=_params(("parallel",)), name=name)(a, g_post, h, g_pre)


def _loss_head(a, g, h, target, *, name):
    T, D = a.shape
    tm = _tile(T, 512)

    def body(a_ref, g_ref, h_ref, t_ref, dy_ref, da_ref, dg_ref, loss_ref):
        i = pl.program_id(0)
        a = a_ref[...]
        r = _rstd(a)
        ah = a * r
        y = h_ref[...] + ah * g_ref[...]
        rows = i * tm + lax.broadcasted_iota(jnp.int32, (tm, 1), 0)
        err = jnp.where(rows >= BLOCK, y - t_ref[...], 0.0)
        dy = err / D
        dy_ref[...] = dy
        dah = dy * g_ref[...]
        da_ref[...] = (r * (dah - ah * jnp.mean(dah * ah, axis=-1, keepdims=True))).astype(da_ref.dtype)
        part = jnp.sum(jnp.sum(err * err, axis=1, keepdims=True), axis=0, keepdims=True)

        @pl.when(i == 0)
        def _():
            loss_ref[...] = jnp.zeros_like(loss_ref)
            dg_ref[...] = jnp.zeros_like(dg_ref)

        loss_ref[...] += jnp.broadcast_to(part, loss_ref.shape)
        dg_ref[...] += jnp.sum(dy * ah, axis=0, keepdims=True)

    row = pl.BlockSpec((tm, D), lambda i: (i, 0))
    vec = pl.BlockSpec((1, D), lambda i: (0, 0))
    return pl.pallas_call(
        body, out_shape=(jax.ShapeDtypeStruct((T, D), F32), jax.ShapeDtypeStruct((T, D), BF16),
                         jax.ShapeDtypeStruct((1, D), F32), jax.ShapeDtypeStruct((8, 128), F32)),
        grid=(T // tm,),
        in_specs=[row, vec, row, row],
        out_specs=(row, row, vec, pl.BlockSpec((8, 128), lambda i: (0, 0))),
        compiler_params=_params(("arbitrary",)), name=name)(a, g, h, target)


def _rms_pull_back(x, g, dy):
    r = _rstd(x)
    xh = x * r
    dxh = dy * g
    return r * (dxh - xh * jnp.mean(dxh * xh, axis=-1, keepdims=True)), jnp.sum(dy * xh, axis=0, keepdims=True)


def _rms_bwd_twice(x, g, dy, res, x2, g2, *, name):
    T, D = x.shape
    tm = _tile(T, 512)

    def body(x_ref, g_ref, dy_ref, res_ref, x2_ref, g2_ref, dx_ref, dg_ref, dx2_ref, dg2_ref):
        @pl.when(pl.program_id(0) == 0)
        def _():
            dg_ref[...] = jnp.zeros_like(dg_ref)
            dg2_ref[...] = jnp.zeros_like(dg2_ref)

        dx, dg = _rms_pull_back(x_ref[...], g_ref[...], dy_ref[...].astype(F32))
        dx = dx + res_ref[...]
        dx_ref[...] = dx
        dg_ref[...] += dg
        dx2, dg2 = _rms_pull_back(x2_ref[...], g2_ref[...], dx)
        dx2_ref[...] = dx2.astype(dx2_ref.dtype)
        dg2_ref[...] += dg2

    row = pl.BlockSpec((tm, D), lambda i: (i, 0))
    vec = pl.BlockSpec((1, D), lambda i: (0, 0))
    gain = jax.ShapeDtypeStruct((1, D), F32)
    return pl.pallas_call(
        body, out_shape=(jax.ShapeDtypeStruct((T, D), F32), gain, jax.ShapeDtypeStruct((T, D), BF16), gain),
        grid=(T // tm,), in_specs=[row, vec, row, row, row, vec], out_specs=(row, vec, row, vec),
        compiler_params=_params(("arbitrary",)), name=name)(x, g, dy, res, x2, g2)


def _rms_bwd_rows(x, g, dy, a, b, res, *, name):
    T, D = x.shape
    n = a.shape[1]
    n_tail = T // BLOCK - 1
    per_step = max(p for p in (4, 3, 2, 1) if n_tail % p == 0)
    steps = n_tail // per_step
    assert T == BLOCK * (1 + n_tail)
    n_rows = 4 * (per_step + 1)

    def body(*refs):
        rows, (g_ref, b_ref), (tail_ref, dg_ref, head_ref) = refs[:n_rows], refs[n_rows:n_rows + 2], refs[n_rows + 2:]

        def block(s):
            x_ref, dy_ref, a_ref, res_ref = rows[4 * s:4 * s + 4]
            dy_all = dy_ref[...] + lax.dot_general(a_ref[...], b_ref[...], NT, preferred_element_type=F32)
            dx, dg = _rms_pull_back(x_ref[...], g_ref[...], dy_all)
            return dx + res_ref[...], dg

        @pl.when(pl.program_id(0) == 0)
        def _():
            dx, dg = block(per_step)
            head_ref[...] = dx
            dg_ref[...] = dg

        for s in range(per_step):
            dx, dg = block(s)
            tail_ref[s * BLOCK:(s + 1) * BLOCK, :] = dx
            dg_ref[...] += dg

    def blocks(width):
        tail = [pl.BlockSpec((BLOCK, width), lambda i, s=s: (per_step * i + s + 1, 0)) for s in range(per_step)]
        return tail + [pl.BlockSpec((BLOCK, width), lambda i: (0, 0))]

    specs, args = [], []
    for bx, bdy, ba, bres in zip(blocks(D), blocks(D), blocks(n), blocks(D)):
        specs += [bx, bdy, ba, bres]
        args += [x, dy, a, res]
    vec = pl.BlockSpec((1, D), lambda i: (0, 0))
    return pl.pallas_call(
        body,
        out_shape=(jax.ShapeDtypeStruct((T - BLOCK, D), F32), jax.ShapeDtypeStruct((1, D), F32),
                   jax.ShapeDtypeStruct((BLOCK, D), F32)),
        grid=(steps,), in_specs=specs + [vec, pl.BlockSpec(b.shape, lambda i: (0, 0))],
        out_specs=(pl.BlockSpec((per_step * BLOCK, D), lambda i: (i, 0)), vec, pl.BlockSpec((BLOCK, D), lambda i: (0, 0))),
        compiler_params=_params(("arbitrary",)), name=name)(*args, g, b)


def _gate_up_swiglu(a, w_t, *, name):
    T, D = a.shape
    F = w_t.shape[0] // 2
    tm = _tile(T, 1408, BLOCK)
    n = _tile(F, 256, BLOCK)
    rows = 3 * BLOCK

    def body(a_ref, wg_ref, wu_ref, g_ref, u_ref, o_ref, ot_ref):
        wg, wu = wg_ref[...], wu_ref[...]
        for r in range(0, tm, rows):
            e = min(r + rows, tm)
            x = a_ref[r:e, :]
            g = lax.dot_general(x, wg, NT, preferred_element_type=F32)
            u = lax.dot_general(x, wu, NT, preferred_element_type=F32)
            g16, u16 = g.astype(BF16), u.astype(BF16)
            g_ref[r:e, :] = g16
            u_ref[r:e, :] = u16
            gr = g16.astype(F32)
            act = gr / (1.0 + jnp.exp(-gr)) * u16.astype(F32)
            o_ref[r:e, :] = act.astype(o_ref.dtype)
            ot_ref[:, r:e] = act.T.astype(ot_ref.dtype)

    tile = pl.BlockSpec((tm, n), lambda i, j: (i, j))
    shp = jax.ShapeDtypeStruct((T, F), BF16)
    return pl.pallas_call(
        body, out_shape=(shp, shp, shp, jax.ShapeDtypeStruct((F, T), BF16)), grid=(T // tm, F // n),
        in_specs=[pl.BlockSpec((tm, D), lambda i, j: (i, 0)),
                  pl.BlockSpec((n, D), lambda i, j: (j, 0)),
                  pl.BlockSpec((n, D), lambda i, j: (j + F // n, 0))],
        out_specs=(tile, tile, tile, pl.BlockSpec((n, tm), lambda i, j: (j, i))),
        compiler_params=_params(("parallel", "parallel")), name=name)(a, w_t, w_t)


def _d_act_swiglu(dff, w_down, gate, up, *, name):
    T, D = dff.shape
    F = w_down.shape[0]
    tm = _tile(T, 384)
    chunk = 1408
    assert F % BLOCK == 0

    def body(d_ref, w_ref, g_ref, u_ref, o_ref):
        dy = d_ref[...]
        for c in range(0, F, chunk):
            e = min(c + chunk, F)
            d = lax.dot_general(dy, w_ref[c:e, :], NT, preferred_element_type=F32)
            g = g_ref[:, c:e].astype(F32)
            u = u_ref[:, c:e].astype(F32)
            sg = 1.0 / (1.0 + jnp.exp(-g))
            o_ref[:, c:e] = (d * u * (sg * (1.0 + g * (1.0 - sg)))).astype(o_ref.dtype)
            o_ref[:, F + c:F + e] = (d * (g * sg)).astype(o_ref.dtype)

    row = pl.BlockSpec((tm, F), lambda i: (i, 0))
    return pl.pallas_call(
        body, out_shape=jax.ShapeDtypeStruct((T, 2 * F), BF16), grid=(T // tm,),
        in_specs=[pl.BlockSpec((tm, D), lambda i: (i, 0)), pl.BlockSpec((F, D), lambda i: (0, 0)), row, row],
        out_specs=pl.BlockSpec((tm, 2 * F), lambda i: (i, 0)),
        compiler_params=_params(("parallel",)), name=name)(dff, w_down, gate, up)


def _fox_gates_fwd(f_t, b, *, name):
    H, T = f_t.shape
    nb = T // BLOCK

    def body(f_ref, b_ref, col_ref):
        f = f_ref[...] + b_ref[...]
        ls = jnp.minimum(f, 0.0) - jnp.log(1.0 + jnp.exp(-jnp.abs(f)))
        t = lax.broadcasted_iota(jnp.int32, (H, T), 1)
        ls = jnp.where(t >= PAD_ROWS, ls, 0.0)
        upper = (lax.broadcasted_iota(jnp.int32, (BLOCK, BLOCK), 0)
                 <= lax.broadcasted_iota(jnp.int32, (BLOCK, BLOCK), 1)).astype(F32)
        carry = jnp.zeros((H, 1), F32)
        for blk in range(nb):
            seg = ls[:, blk * BLOCK:(blk + 1) * BLOCK]
            pre = jnp.dot(seg, upper, precision=HIGHEST, preferred_element_type=F32) + carry
            key_gate = jnp.where(t[:, blk * BLOCK:(blk + 1) * BLOCK] >= PAD_ROWS, pre, -NEG)
            terms = list(_split3(pre)) + list(_split3(key_gate))
            col_ref[blk * BLOCK:(blk + 1) * BLOCK, :] = jnp.concatenate(
                terms + [jnp.zeros((BLOCK - len(terms) * H, BLOCK), F32)], axis=0).T.astype(col_ref.dtype)
            carry = pre[:, BLOCK - 1:BLOCK]

    vm = pl.BlockSpec(memory_space=pltpu.VMEM)
    return pl.pallas_call(
        body, out_shape=jax.ShapeDtypeStruct((T, BLOCK), BF16),
        in_specs=[vm, vm], out_specs=vm,
        compiler_params=_params(), name=name)(f_t, b)


def _fox_gates_bwd(dcq, dck, f_t, b, *, name):
    H, T = f_t.shape
    nb = T // BLOCK

    def body(dq_ref, d_ref, f_ref, b_ref, df_ref, db_ref):
        lower = (lax.broadcasted_iota(jnp.int32, (BLOCK, BLOCK), 0)
                 >= lax.broadcasted_iota(jnp.int32, (BLOCK, BLOCK), 1)).astype(F32)
        carry = jnp.zeros((H, 1), F32)
        for blk in range(nb - 1, -1, -1):
            seg = dq_ref[:, blk * BLOCK:(blk + 1) * BLOCK] - d_ref[:, blk * BLOCK:(blk + 1) * BLOCK]
            suf = jnp.dot(seg, lower, precision=HIGHEST, preferred_element_type=F32) + carry
            df_ref[:, blk * BLOCK:(blk + 1) * BLOCK] = suf
            carry = suf[:, 0:1]
        f = f_ref[...] + b_ref[...]
        t = lax.broadcasted_iota(jnp.int32, (H, T), 1)
        df = jnp.where(t >= PAD_ROWS, df_ref[...] / (1.0 + jnp.exp(f)), 0.0)
        df_ref[...] = df
        db_ref[...] = jnp.sum(df, axis=1, keepdims=True)

    vm = pl.BlockSpec(memory_space=pltpu.VMEM)
    return pl.pallas_call(
        body, out_shape=(jax.ShapeDtypeStruct((H, T), F32), jax.ShapeDtypeStruct((H, 1), F32)),
        in_specs=[vm, vm, vm, vm], out_specs=(vm, vm),
        compiler_params=_params(), name=name)(dcq, dck, f_t, b)


def _fox_lanes(parity):
    base = HEAD_DIM * (1 - parity)
    return base, base + 3


def _split3(c):
    hi = c.astype(BF16).astype(F32)
    r = c - hi
    mid = r.astype(BF16).astype(F32)
    lo = (r - mid).astype(BF16).astype(F32)
    return hi, mid, lo


def _lanes(lane, parity, data, start, terms, ones_at=None, fill=1.0):
    out = jnp.zeros((), F32) if ones_at is None else jnp.where((lane >= ones_at) & (lane < ones_at + 3), fill, 0.0)
    for i, t in enumerate(terms):
        out = jnp.where(lane == start + i, t, out)
    return jnp.where(lane // HEAD_DIM == parity, data, out)


def _fox_prep(proj, cum_col, *, name):
    T = proj.shape[0]
    tm = _tile(T, 1408, BLOCK)
    nt = T // tm
    H = FOX_HEADS
    lanes = 2 * HEAD_DIM
    first = (proj.shape[1] - 3 * H * HEAD_DIM) // lanes

    def body(q_ref, k_ref, v_ref, c_ref, qa_ref, ka_ref, va_ref):
        p = pl.program_id(0)
        i = pl.program_id(1)
        lane = lax.broadcasted_iota(jnp.int32, (1, lanes), 1)
        src = lax.broadcasted_iota(jnp.int32, (lanes, lanes), 0)
        dst = lax.broadcasted_iota(jnp.int32, (lanes, lanes), 1)
        q2 = q_ref[...].astype(F32) * SCALE
        k2 = k_ref[...].astype(F32)
        v2 = v_ref[...].astype(F32)
        gates = c_ref[...]
        def placed(h, first_term, start):
            pick = ((src % FOX_HEADS == h) & (src // FOX_HEADS - first_term == dst - start)
                    & (dst >= start) & (dst < start + 3))
            return jnp.dot(gates, pick.astype(BF16), preferred_element_type=F32)

        moved = [(placed(2 * p + e, 0, _fox_lanes(e)[1]), placed(2 * p + e, 3, _fox_lanes(e)[0])) for e in range(2)]
        for e in range(2):
            kc, qc = _fox_lanes(e)
            own = lane // HEAD_DIM == e
            minus = jnp.where((lane >= kc) & (lane < kc + 3), -1.0, 0.0)
            ones_q = jnp.where((lane >= qc) & (lane < qc + 3), 1.0, 0.0)
            ones_k = jnp.where((lane >= kc) & (lane < kc + 3), 1.0, 0.0)
            qa_ref[e] = jnp.where(own, q2, moved[e][0] + minus).astype(BF16)
            ka_ref[e] = jnp.where(own, k2, moved[e][1] + ones_q).astype(BF16)
            va_ref[e] = jnp.where(own, v2, ones_k).astype(BF16)

    pairs = FOX_GROUP // 2

    def col(part):
        return pl.BlockSpec((tm, lanes),
                            lambda p, i: (i, first + 3 * pairs * (p // pairs) + part * pairs + p % pairs))

    out = pl.BlockSpec((2, tm, lanes), lambda p, i: (p, i, 0))
    shp = jax.ShapeDtypeStruct((H, T, lanes), BF16)
    return pl.pallas_call(
        body, out_shape=(shp, shp, shp), grid=(H // 2, nt),
        in_specs=[col(0), col(1), col(2), pl.BlockSpec((tm, lanes), lambda p, i: (i, 0))],
        out_specs=(out, out, out),
        compiler_params=_params(("parallel", "parallel")), name=name)(proj, proj, proj, cum_col)


def _fox_fwd(q_aug, k_aug, v_aug, mix, *, ex=None, name):
    H, T, lanes = q_aug.shape
    tq = FOX_TILE
    nq = T // tq
    G = FOX_HEADS

    def body(q_ref, k_ref, v_ref, mix_ref, o_ref, lse_ref, m_scr, acc_scr):
        i = pl.program_id(1)
        m_scr[...] = jnp.full(m_scr.shape, NEG, F32)
        acc_scr[...] = jnp.zeros(acc_scr.shape, F32)

        def step(kb, diag):
            off = pl.multiple_of(kb * tq, tq)
            s_t = [lax.dot_general(k_ref[g, pl.ds(off, tq), :], q_ref[g], NT, preferred_element_type=F32)
                   for g in range(G)]
            if diag:
                r = lax.broadcasted_iota(jnp.int32, (tq, tq), 0)
                c = lax.broadcasted_iota(jnp.int32, (tq, tq), 1)
                s_t = [jnp.where(c >= r, s, NEG) for s in s_t]
            m_prev = [m_scr[g] for g in range(G)]
            m_new = [jnp.maximum(m_prev[g], jnp.max(s_t[g], axis=0, keepdims=True)) for g in range(G)]
            p_t = [jnp.exp(s_t[g] - m_new[g]).astype(BF16) for g in range(G)]
            pv = [lax.dot_general(v_ref[g, pl.ds(off, tq), :], p_t[g], TN, preferred_element_type=F32)
                  for g in range(G)]
            for g in range(G):
                acc_scr[g] = jnp.exp(m_prev[g] - m_new[g]) * acc_scr[g] + pv[g]
                m_scr[g] = m_new[g]

        def loop_body(kb, carry):
            step(kb, False)
            return carry

        lax.fori_loop(0, i, loop_body, 0)
        step(i, True)
        lane = lax.broadcasted_iota(jnp.int32, (tq, lanes), 1)
        outs = []
        for g in range(G):
            ones = _fox_lanes(g % 2)[0]
            acc = acc_scr[g]
            lse_ref[g] = m_scr[g] + jnp.log(acc[ones:ones + 1, :])
            acc_t = acc.T
            outs.append(acc_t / acc_t[:, ones:ones + 1])
        for pair in range(G // 2):
            o_ref[:, pair * lanes:(pair + 1) * lanes] = jnp.where(
                lane < HEAD_DIM, outs[2 * pair], outs[2 * pair + 1]).astype(o_ref.dtype)

    blk = pl.BlockSpec((G, tq, lanes), lambda h, i: (h, i, 0))
    full = pl.BlockSpec((G, T, lanes), lambda h, i: (h, 0, 0))
    grid = (H // G, nq)
    first = mix.shape[1] // (G * HEAD_DIM) - H // G
    body, x_in, x_in_specs, x_out, x_out_specs, x_scr = _carry(ex, grid, 4, 2, body)
    return pl.pallas_call(
        body,
        out_shape=(jax.ShapeDtypeStruct(mix.shape, mix.dtype), jax.ShapeDtypeStruct((H, nq, 1, tq), F32), *x_out),
        grid=grid,
        in_specs=[blk, full, full, pl.BlockSpec(memory_space=pl.ANY)] + x_in_specs,
        out_specs=(pl.BlockSpec((tq, G * HEAD_DIM), lambda h, i: (i, first + h)),
                   pl.BlockSpec((G, None, 1, tq), lambda h, i: (h, i, 0, 0)), *x_out_specs),
        input_output_aliases={3: 0},
        scratch_shapes=[pltpu.VMEM((G, 1, tq), F32), pltpu.VMEM((G, lanes, tq), F32)] + x_scr,
        compiler_params=_params(("arbitrary", "arbitrary")), name=name)(q_aug, k_aug, v_aug, mix, *x_in)


def _fox_prep_bwd(dmix, mix, *, name):
    T = dmix.shape[0]
    H = FOX_HEADS
    tm = _tile(T, 1408, BLOCK)
    lanes = 2 * HEAD_DIM
    first = mix.shape[1] // lanes - H // 2

    def body(d_ref, o_ref, da_ref):
        lane = lax.broadcasted_iota(jnp.int32, (1, lanes), 1)
        d2 = d_ref[...].astype(F32)
        prod = d2 * o_ref[...].astype(F32)
        for e in range(2):
            delta = jnp.sum(jnp.where(lane // HEAD_DIM == e, prod, 0.0), axis=1, keepdims=True)
            da_ref[e] = _lanes(lane, e, d2, _fox_lanes(e)[0], _split3(-delta)).astype(BF16)

    pair = pl.BlockSpec((tm, lanes), lambda p, i: (i, first + p))
    return pl.pallas_call(
        body, out_shape=jax.ShapeDtypeStruct((H, T, lanes), BF16), grid=(H // 2, T // tm),
        in_specs=[pair, pair],
        out_specs=pl.BlockSpec((2, tm, lanes), lambda p, i: (p, i, 0)),
        compiler_params=_params(("parallel", "parallel")), name=name)(dmix, mix)


def _fox_bwd(q_aug, k_aug, v_aug, do_aug, lse_row, dproj, *, ex=None, name):
    H, T, lanes = q_aug.shape
    tq = FOX_TILE
    nq = T // tq
    G = FOX_GROUP

    def side_by_side(tiles, scale=None):
        lane = lax.broadcasted_iota(jnp.int32, tiles[0].shape, 1)
        out = [jnp.where(lane < HEAD_DIM, tiles[2 * p], tiles[2 * p + 1]) for p in range(G // 2)]
        out = jnp.concatenate(out, axis=1)
        return out if scale is None else out * scale

    def body(q_ref, k_ref, v_ref, do_ref, lse_ref, dproj_in, out_ref, dcq_ref, dck_ref, dk_acc, dv_acc, dq_ref):
        j = pl.program_id(1)

        @pl.when(j == 0)
        def _():
            dq_ref[...] = jnp.zeros(dq_ref.shape, F32)
            dcq_ref[...] = jnp.zeros(dcq_ref.shape, F32)

        dk_acc[...] = jnp.zeros(dk_acc.shape, F32)
        dv_acc[...] = jnp.zeros(dv_acc.shape, F32)

        def step(qb, diag):
            off = pl.multiple_of(qb * tq, tq)
            heads = range(G)
            qa = [q_ref[g, pl.ds(off, tq), :] for g in heads]
            da = [do_ref[g, pl.ds(off, tq), :] for g in heads]
            s_t = [lax.dot_general(k_ref[g], qa[g], NT, preferred_element_type=F32) for g in heads]
            dp_t = [lax.dot_general(v_ref[g], da[g], NT, preferred_element_type=F32) for g in heads]
            p_t = [jnp.exp(s_t[g] - lse_ref[g, qb]) for g in heads]
            if diag:
                r = lax.broadcasted_iota(jnp.int32, (tq, tq), 0)
                c = lax.broadcasted_iota(jnp.int32, (tq, tq), 1)
                p_t = [jnp.where(c >= r, p, 0.0) for p in p_t]
            dsb = [(p_t[g] * dp_t[g]).astype(BF16) for g in heads]
            dv = [jnp.dot(p_t[g].astype(BF16), da[g], preferred_element_type=F32) for g in heads]
            dk = [jnp.dot(dsb[g], qa[g], preferred_element_type=F32) for g in heads]
            dq = [lax.dot_general(k_ref[g], dsb[g], TN, preferred_element_type=F32) for g in heads]
            for g in heads:
                dv_acc[g] += dv[g]
                dk_acc[g] += dk[g]
                dq_ref[g, qb] += dq[g]
                dcq_ref[g, qb] += jnp.sum(dsb[g].astype(F32), axis=0, keepdims=True)

        step(j, True)

        def loop_body(qb, carry):
            step(qb, False)
            return carry

        lax.fori_loop(j + 1, nq, loop_body, 0)
        dk = [dk_acc[g] for g in range(G)]
        out_ref[:, 0:wide] = side_by_side([dq_ref[g, j].T for g in range(G)], SCALE).astype(out_ref.dtype)
        out_ref[:, wide:2 * wide] = side_by_side(dk).astype(out_ref.dtype)
        out_ref[:, 2 * wide:3 * wide] = side_by_side([dv_acc[g] for g in range(G)]).astype(out_ref.dtype)
        for g in range(G):
            kc = _fox_lanes(g % 2)[0]
            dck_ref[g] = -dk[g].T[kc:kc + 1, :]

    blk = pl.BlockSpec((G, tq, lanes), lambda h, j: (h, j, 0))
    full = pl.BlockSpec((G, T, lanes), lambda h, j: (h, 0, 0))
    wide = G * HEAD_DIM
    first = dproj.shape[1] // (3 * wide) - H // G
    grid = (H // G, nq)
    body, x_in, x_in_specs, x_out, x_out_specs, x_scr = _carry(ex, grid, 6, 3, body)
    rows = jax.ShapeDtypeStruct((H, nq, 1, tq), F32)
    all_rows = pl.BlockSpec((G, nq, 1, tq), lambda h, j: (h, 0, 0, 0))
    return pl.pallas_call(
        body,
        out_shape=(jax.ShapeDtypeStruct(dproj.shape, dproj.dtype), rows, rows, *x_out),
        grid=grid,
        in_specs=[full, blk, blk, full, all_rows, pl.BlockSpec(memory_space=pl.ANY)] + x_in_specs,
        out_specs=(pl.BlockSpec((tq, 3 * wide), lambda h, j: (j, first + h)), all_rows,
                   pl.BlockSpec((G, None, 1, tq), lambda h, j: (h, j, 0, 0)), *x_out_specs),
        input_output_aliases={5: 0},
        scratch_shapes=[pltpu.VMEM((G, tq, lanes), F32), pltpu.VMEM((G, tq, lanes), F32),
                        pltpu.VMEM((G, nq, lanes, tq), F32)] + x_scr,
        compiler_params=_params(("arbitrary", "arbitrary")), name=name,
    )(q_aug, k_aug, v_aug, do_aug, lse_row, dproj, *x_in)


def _t5_bucket_np(d):
    n = np.maximum(d, 0).astype(np.int32)
    max_exact = N_BUCKETS // 2
    nf = np.maximum(n, 1).astype(np.float32)
    large = max_exact + (np.log(nf / max_exact) / math.log(MAX_DISTANCE / max_exact)
                         * (N_BUCKETS - max_exact)).astype(np.int32)
    large = np.minimum(large, N_BUCKETS - 1)
    return np.where(n < max_exact, n, large)


def _bucket_onehots():
    k = np.arange(BLOCK)[:, None]
    q = np.arange(BLOCK)[None, :]
    eye = np.eye(N_BUCKETS, dtype=np.float32)
    cur = eye[_t5_bucket_np(q - k).reshape(-1)]
    prev = eye[_t5_bucket_np(BLOCK + q - k).reshape(-1)]
    return cur, prev


SWA_K_COL = SWA_Q_HEADS * HEAD_DIM // (2 * HEAD_DIM)
SWA_V_COL = SWA_K_COL + 1


def _swa_terms(raw, bc, bp, far, sink, n):
    k = lax.broadcasted_iota(jnp.int32, (BLOCK, BLOCK), 0)
    q = lax.broadcasted_iota(jnp.int32, (BLOCK, BLOCK), 1)
    never = 2 * BLOCK
    s_c = raw[0] + bc
    s_p = raw[1] + bp
    s_m = raw[2] + jnp.where(n == 1, bp, far)
    s_c = jnp.where((k <= q) & (k >= jnp.where(n >= 1, 0, PAD_ROWS)), s_c, NEG)
    s_p = jnp.where(k > q + jnp.where(n >= 2, 0, never), s_p, NEG)
    s_m = jnp.where(k >= jnp.where(n >= 1, PAD_ROWS, never), s_m, NEG)
    m = jnp.maximum(jnp.maximum(jnp.max(s_c, axis=0, keepdims=True), jnp.max(s_p, axis=0, keepdims=True)),
                    jnp.maximum(jnp.max(s_m, axis=0, keepdims=True), sink))
    e = [jnp.exp(s_c - m), jnp.exp(s_p - m), jnp.exp(s_m - m)]
    e_s = jnp.exp(sink - m)
    l = (jnp.sum(e[0], axis=0, keepdims=True) + jnp.sum(e[1], axis=0, keepdims=True)
         + jnp.sum(e[2], axis=0, keepdims=True) + e_s)
    return e, e_s, l


SWA_STEP = 3


def _swa_specs():
    R = SWA_STEP

    def window(col):
        return ([pl.BlockSpec((BLOCK, BLOCK), lambda s, w=w: (jnp.maximum(R * s - 1 + w, 0), col)) for w in range(R + 1)]
                + [pl.BlockSpec((BLOCK, BLOCK), lambda s: (0, col))])

    qblk = pl.BlockSpec((R * BLOCK, SWA_Q_HEADS * HEAD_DIM), lambda s: (s, 0))
    bias = pl.BlockSpec((SWA_Q_HEADS, BLOCK, BLOCK), lambda s: (0, 0, 0))
    smem = pl.BlockSpec(memory_space=pltpu.SMEM)
    return qblk, window(SWA_K_COL), window(SWA_V_COL), bias, smem


def _swa_own_kv(tile_ref, kv):
    lane = lax.broadcasted_iota(jnp.int32, (BLOCK, 2 * HEAD_DIM), 1)
    t = tile_ref[...].astype(F32)
    return jnp.where(lane // HEAD_DIM == kv, t, pltpu.roll(t, HEAD_DIM, 1)).astype(BF16)


def _swa_fwd(proj, bc, bp, far, sinks, *, name):
    T = proj.shape[0]
    nb = T // BLOCK
    G = SWA_GROUP
    Hq = SWA_Q_HEADS
    lanes = 2 * HEAD_DIM

    R = SWA_STEP
    assert nb % R == 0

    def body(*refs):
        q_ref, k_refs, v_refs = refs[0], refs[1:R + 3], refs[R + 3:2 * R + 5]
        bc_ref, bp_ref, far_ref, sink_ref, o_ref = refs[2 * R + 5:]
        s = pl.program_id(0)
        lane = lax.broadcasted_iota(jnp.int32, (BLOCK, lanes), 1)
        kvs = range(SWA_KV_HEADS)
        kk = [[_swa_own_kv(ref, kv) for ref in k_refs] for kv in kvs]
        vv = [[_swa_own_kv(ref, kv) for ref in v_refs] for kv in kvs]
        chains = [(r, h) for r in range(R) for h in range(Hq)]
        tiles = lambda r: (r + 1, r, R + 1)
        q2 = {(r, pair): q_ref[r * BLOCK:(r + 1) * BLOCK, pair * lanes:(pair + 1) * lanes].astype(F32) * SCALE
              for r in range(R) for pair in range(Hq // 2)}
        qm = {c: jnp.where(lane // HEAD_DIM == c[1] % 2, q2[c[0], c[1] // 2], 0.0).astype(BF16) for c in chains}
        raw = {c: [lax.dot_general(kk[c[1] // G][w], qm[c], NT, preferred_element_type=F32) for w in tiles(c[0])]
               for c in chains}
        terms = {c: _swa_terms(raw[c], bc_ref[c[1]], bp_ref[c[1]], far_ref[c[1]], sink_ref[c[1]], R * s + c[0])
                 for c in chains}
        o_t = {c: sum(lax.dot_general(vv[c[1] // G][w], terms[c][0][b].astype(BF16), TN, preferred_element_type=F32)
                      for b, w in enumerate(tiles(c[0]))) for c in chains}
        outs = {c: (o_t[c] / terms[c][2]).T for c in chains}
        for r in range(R):
            for pair in range(Hq // 2):
                o_ref[r * BLOCK:(r + 1) * BLOCK, pair * lanes:(pair + 1) * lanes] = jnp.where(
                    lane < HEAD_DIM, outs[r, 2 * pair], outs[r, 2 * pair + 1]).astype(o_ref.dtype)

    qblk, keys, vals, bias, smem = _swa_specs()
    return pl.pallas_call(
        body, out_shape=jax.ShapeDtypeStruct((T, D_MODEL), BF16), grid=(nb // R,),
        in_specs=[qblk] + keys + vals + [bias, bias, smem, smem],
        out_specs=qblk,
        compiler_params=_params(("parallel",)), name=name,
    )(proj, *([proj] * (2 * R + 4)), bc, bp, far, sinks)


def _swa_bwd(proj, dmix, bc, bp, far, sinks, *, ex=None, name):
    T, width = proj.shape
    nb = T // BLOCK
    G = SWA_GROUP
    Hq = SWA_Q_HEADS
    lanes = 2 * HEAD_DIM
    qw = Hq * HEAD_DIM
    own_w = qw + 2 * lanes

    R = SWA_STEP
    assert nb % R == 0
    n_in = 2 * R + 10

    def body(*refs):
        q_ref, k_refs, v_refs = refs[0], refs[1:R + 3], refs[R + 3:2 * R + 5]
        do_ref, bc_ref, bp_ref, far_ref, sink_ref = refs[2 * R + 5:n_in]
        dp_ref, dbc_ref, dbp_ref, dbf_ref, dsk_ref, dk_acc, dv_acc = refs[n_in:]
        s = pl.program_id(0)

        @pl.when(s == 0)
        def _():
            for ref in (dk_acc, dv_acc, dbc_ref, dbp_ref, dbf_ref, dsk_ref):
                ref[...] = jnp.zeros(ref.shape, F32)

        lane = lax.broadcasted_iota(jnp.int32, (BLOCK, lanes), 1)
        kvs = range(SWA_KV_HEADS)
        kk = [[_swa_own_kv(ref, kv) for ref in k_refs] for kv in kvs]
        vv = [[_swa_own_kv(ref, kv) for ref in v_refs] for kv in kvs]
        chains = [(r, h) for r in range(R) for h in range(Hq)]
        blocks = range(3)
        tiles = lambda r: (r + 1, r, R + 1)
        sub = lambda ref, r, pair: ref[r * BLOCK:(r + 1) * BLOCK, pair * lanes:(pair + 1) * lanes]
        q2 = {(r, pair): sub(q_ref, r, pair).astype(F32) * SCALE for r in range(R) for pair in range(Hq // 2)}
        d2 = {(r, pair): sub(do_ref, r, pair) for r in range(R) for pair in range(Hq // 2)}
        own = [lane // HEAD_DIM == half for half in range(2)]
        qm = {c: jnp.where(own[c[1] % 2], q2[c[0], c[1] // 2], 0.0).astype(BF16) for c in chains}
        dom = {c: jnp.where(own[c[1] % 2], d2[c[0], c[1] // 2], jnp.zeros_like(d2[0, 0])) for c in chains}
        raw = {c: [lax.dot_general(kk[c[1] // G][w], qm[c], NT, preferred_element_type=F32) for w in tiles(c[0])]
               for c in chains}
        dp = {c: [lax.dot_general(vv[c[1] // G][w], dom[c], NT, preferred_element_type=F32) for w in tiles(c[0])]
              for c in chains}
        p, ds16 = {}, {}
        for c in chains:
            r, h = c
            n = R * s + r
            e, e_s, l = _swa_terms(raw[c], bc_ref[h], bp_ref[h], far_ref[h], sink_ref[h], n)
            inv = 1.0 / l
            ph = [e[b] * inv for b in blocks]
            delta = sum(jnp.sum(ph[b] * dp[c][b], axis=0, keepdims=True) for b in blocks)
            ds = [ph[b] * (dp[c][b] - delta) for b in blocks]
            dsk_ref[h] += -(e_s * inv) * delta
            dbc_ref[h] += ds[0]
            dbp_ref[h] += ds[1] + jnp.where(n == 1, ds[2], 0.0)
            dbf_ref[h] += jnp.where(n >= 2, ds[2], 0.0)
            p[c] = [x.astype(BF16) for x in ph]
            ds16[c] = [x.astype(BF16) for x in ds]
        dq_t = {c: sum(lax.dot_general(kk[c[1] // G][w], ds16[c][b], TN, preferred_element_type=F32)
                       for b, w in enumerate(tiles(c[0]))) for c in chains}
        group = [range(kv * G, (kv + 1) * G) for kv in kvs]
        dk = {(r, kv): [sum(jnp.dot(ds16[r, h][b], qm[r, h], preferred_element_type=F32) for h in group[kv])
                        for b in blocks] for r in range(R) for kv in kvs}
        dv = {(r, kv): [sum(jnp.dot(p[r, h][b], dom[r, h], preferred_element_type=F32) for h in group[kv])
                        for b in blocks] for r in range(R) for kv in kvs}
        for r in range(R):
            n = R * s + r
            rows = pl.ds(pl.multiple_of(n * BLOCK, BLOCK), BLOCK)
            prev_rows = pl.ds(pl.multiple_of(jnp.maximum(n - 1, 0) * BLOCK, BLOCK), BLOCK)
            for pair in range(Hq // 2):
                dp_ref[rows, pair * lanes:(pair + 1) * lanes] = (jnp.where(
                    lane < HEAD_DIM, dq_t[r, 2 * pair].T, dq_t[r, 2 * pair + 1].T) * SCALE).astype(dp_ref.dtype)
            for acc, ref in ((dk, dk_acc), (dv, dv_acc)):
                tot = [[a + pltpu.roll(a, HEAD_DIM, 1) for a in acc[r, kv]] for kv in kvs]
                both = [jnp.where(lane < HEAD_DIM, tot[0][b], tot[1][b]) for b in blocks]
                ref[rows, :] += both[0]
                ref[prev_rows, :] += both[1]
                ref[0:BLOCK, :] += both[2]

        @pl.when(s == nb // R - 1)
        def _():
            dp_ref[:, qw:qw + lanes] = dk_acc[...].astype(dp_ref.dtype)
            dp_ref[:, qw + lanes:own_w] = dv_acc[...].astype(dp_ref.dtype)

    qblk, keys, vals, bias, smem = _swa_specs()
    dsk = pl.BlockSpec((Hq, 1, BLOCK), lambda s: (0, 0, 0))
    grid = (nb // R,)
    body, x_in, x_in_specs, x_out, x_out_specs, x_scr = _carry(ex, grid, n_in, 5, body)
    tile = jax.ShapeDtypeStruct((Hq, BLOCK, BLOCK), F32)
    return pl.pallas_call(
        body,
        out_shape=(jax.ShapeDtypeStruct((T, width), BF16), tile, tile, tile,
                   jax.ShapeDtypeStruct((Hq, 1, BLOCK), F32), *x_out),
        grid=grid,
        in_specs=[qblk] + keys + vals + [qblk, bias, bias, smem, smem] + x_in_specs,
        out_specs=(pl.BlockSpec((T, own_w), lambda s: (0, 0)), bias, bias, bias, dsk, *x_out_specs),
        scratch_shapes=[pltpu.VMEM((T, lanes), F32), pltpu.VMEM((T, lanes), F32)] + x_scr,
        compiler_params=_params(("arbitrary",)), name=name,
    )(proj, *([proj] * (2 * R + 4)), dmix, bc, bp, far, sinks, *x_in)


def _bias_tiles(tab_t, oh_cur_t, oh_prev_t, *, name):
    Hq = tab_t.shape[0]

    def body(t_ref, oc_ref, op_ref, bc_ref, bp_ref):
        bc_ref[...] = jnp.dot(t_ref[...], oc_ref[...], precision=HIGHEST, preferred_element_type=F32)
        bp_ref[...] = jnp.dot(t_ref[...], op_ref[...], precision=HIGHEST, preferred_element_type=F32)

    vm = pl.BlockSpec(memory_space=pltpu.VMEM)
    shp = jax.ShapeDtypeStruct((Hq, BLOCK * BLOCK), F32)
    bc, bp = pl.pallas_call(body, out_shape=(shp, shp), in_specs=[vm] * 3, out_specs=(vm, vm),
                            compiler_params=_params(), name=name)(tab_t, oh_cur_t, oh_prev_t)
    return bc.reshape(Hq, BLOCK, BLOCK), bp.reshape(Hq, BLOCK, BLOCK)


def _small_grads(dbc, dbp, dbf, dsk, oh_cur, oh_prev, *, ex=None, name):
    Hq = dbc.shape[0]

    def body(dbc_ref, dbp_ref, dbf_ref, dsk_ref, oc_ref, op_ref, tab_ref, sink_ref):
        tab = (jnp.dot(dbc_ref[...], oc_ref[...], precision=HIGHEST, preferred_element_type=F32)
               + jnp.dot(dbp_ref[...], op_ref[...], precision=HIGHEST, preferred_element_type=F32))
        far = jnp.sum(dbf_ref[...], axis=1, keepdims=True)
        last = lax.broadcasted_iota(jnp.int32, (Hq, N_BUCKETS), 1) == N_BUCKETS - 1
        tab_ref[...] = tab + jnp.where(last, far, 0.0)
        sink_ref[...] = jnp.sum(dsk_ref[...], axis=1, keepdims=True)

    vm = pl.BlockSpec(memory_space=pltpu.VMEM)
    body, x_in, x_in_specs, x_out, x_out_specs, x_scr = _carry(ex, (), 6, 2, body)
    return pl.pallas_call(
        body, out_shape=(jax.ShapeDtypeStruct((Hq, N_BUCKETS), F32), jax.ShapeDtypeStruct((Hq, 1), F32), *x_out),
        in_specs=[vm] * 6 + x_in_specs, out_specs=(vm, vm, *x_out_specs), scratch_shapes=x_scr,
        compiler_params=_params(), name=name,
    )(dbc.reshape(Hq, -1), dbp.reshape(Hq, -1), dbf.reshape(Hq, -1), dsk.reshape(Hq, -1), oh_cur, oh_prev, *x_in)


def _coords():
    return lax.axis_index("x"), lax.axis_index("y"), lax.axis_index("c")


class _Exchange:
    def __init__(self, inputs, out_shapes, scratch, start, finish):
        self.inputs, self.out_shapes, self.scratch, self.start, self.finish = inputs, out_shapes, scratch, start, finish


def _carry(ex, grid, n_in, n_out, body):
    if ex is None:
        return body, [], [], [], [], []
    ni, no = len(ex.inputs), len(ex.out_shapes)

    def at_step(which):
        cond = jnp.bool_(True)
        for axis, n in enumerate(grid):
            cond = cond & (pl.program_id(axis) == (0 if which == "first" else n - 1))
        return cond

    def wrapped(*refs):
        refs = list(refs)
        n_own_scr = len(refs) - (n_in + ni + n_out + no) - len(ex.scratch)
        own_in, side_in = refs[:n_in], refs[n_in:n_in + ni]
        own_out = refs[n_in + ni:n_in + ni + n_out]
        side_out = refs[n_in + ni + n_out:n_in + ni + n_out + no]
        rest = refs[n_in + ni + n_out + no:]
        own_scr, sems = rest[:n_own_scr], rest[n_own_scr:]

        @pl.when(at_step("first"))
        def _():
            ex.start(side_in, side_out, sems)

        body(*own_in, *own_out, *own_scr)

        @pl.when(at_step("last"))
        def _():
            ex.finish(side_in, side_out, sems)

    hbm = pl.BlockSpec(memory_space=pl.ANY)
    return wrapped, list(ex.inputs), [hbm] * ni, list(ex.out_shapes), [hbm] * no, list(ex.scratch)


def _gather_exchange(shards):
    nt = len(shards)

    def copies(ins, outs, sems):
        send_sems, recv_sems, local_sems = sems
        x, y, c = _coords()
        me, sibling = (x, y, c), (x, y, 1 - c)
        chips = [(1 - x, y), (x, 1 - y), (1 - x, 1 - y)]

        def slot(t, dev):
            return outs[t].at[4 * dev[0] + 2 * dev[1] + dev[2]]

        def copy(t, k, block, to, src=None):
            dst = slot(t, block)
            return pltpu.make_async_remote_copy(
                src_ref=dst if src is None else src, dst_ref=dst,
                send_sem=send_sems.at[t, k], recv_sem=recv_sems.at[t, k], device_id=to, device_id_type=MESH)

        mine = [pltpu.make_async_copy(ins[t], slot(t, me), local_sems.at[t]) for t in range(nt)]
        first = []
        for t in range(nt):
            first.append(copy(t, 0, me, sibling, src=ins[t]))
            first += [copy(t, 1 + j, me, (*chip, c), src=ins[t]) for j, chip in enumerate(chips)]
        return copy, mine, first, me, sibling, chips, c

    def start(ins, outs, sems):
        _, mine, first, *_ = copies(ins, outs, sems)
        for cp in mine + first:
            cp.start()

    def finish(ins, outs, sems):
        copy, mine, first, me, sibling, chips, c = copies(ins, outs, sems)
        passed = []
        for j, chip in enumerate(chips):
            for t in range(nt):
                copy(t, 1 + j, (*chip, c), me).wait_recv()
                cp = copy(t, 4 + j, (*chip, c), sibling)
                cp.start()
                passed.append(cp)
        for t in range(nt):
            copy(t, 0, sibling, me).wait_recv()
            for j, chip in enumerate(chips):
                copy(t, 4 + j, (*chip, 1 - c), me).wait_recv()
        for cp in first + passed:
            cp.wait_send()
        for cp in mine:
            cp.wait()

    return _Exchange(
        list(shards), [jax.ShapeDtypeStruct((N_DEV,) + s.shape, s.dtype) for s in shards],
        [pltpu.SemaphoreType.DMA((nt, 7)), pltpu.SemaphoreType.DMA((nt, 7)), pltpu.SemaphoreType.DMA((nt,))],
        start, finish)


def _swap_exchange(arrays, n_slices, copies):
    nt = len(arrays)

    def start(ins, outs, sems):
        for cp in copies(ins, outs, sems):
            cp.start()

    def finish(ins, outs, sems):
        sends = copies(ins, outs, sems)
        for cp in sends:
            cp.wait_recv()
        for cp in sends:
            cp.wait_send()

    return _Exchange(
        list(arrays), [jax.ShapeDtypeStruct((n_slices,) + a.shape[1:], a.dtype) for a in arrays],
        [pltpu.SemaphoreType.DMA((nt, n_slices)), pltpu.SemaphoreType.DMA((nt, n_slices))], start, finish)


def _cores_exchange(gs):
    def copies(ins, outs, sems):
        send_sems, recv_sems = sems
        x, y, c = _coords()
        return [pltpu.make_async_remote_copy(
            src_ref=ins[t].at[2 * j + (1 - c)], dst_ref=outs[t].at[j],
            send_sem=send_sems.at[t, j], recv_sem=recv_sems.at[t, j], device_id=(x, y, 1 - c), device_id_type=MESH)
            for t in range(len(gs)) for j in range(4)]

    return _swap_exchange(gs, 4, copies)


def _chips_exchange(ps):
    def copies(ins, outs, sems):
        send_sems, recv_sems = sems
        x, y, c = _coords()
        peers = [(1 - x, y), (x, 1 - y), (1 - x, 1 - y)]
        return [pltpu.make_async_remote_copy(
            src_ref=ins[t].at[2 * px + py], dst_ref=outs[t].at[k],
            send_sem=send_sems.at[t, k], recv_sem=recv_sems.at[t, k], device_id=(px, py, c), device_id_type=MESH)
            for t in range(len(ps)) for k, (px, py) in enumerate(peers)]

    return _swap_exchange(ps, 3, copies)


def _add_cores(g, r, core, *, name):
    _, A, B = g.shape
    ta = _tile(A, 512, 16)

    def body(core_ref, a_ref, b_ref, o16_ref):
        o16_ref[...] = (a_ref[...] + b_ref[...]).astype(BF16)

    blk = (None, ta, B)
    return pl.pallas_call(
        body, out_shape=jax.ShapeDtypeStruct((4, A, B), BF16),
        grid_spec=pltpu.PrefetchScalarGridSpec(
            num_scalar_prefetch=1, grid=(4, A // ta),
            in_specs=[pl.BlockSpec(blk, lambda j, i, core_ref: (2 * j + core_ref[0], i, 0)),
                      pl.BlockSpec(blk, lambda j, i, core_ref: (j, i, 0))],
            out_specs=pl.BlockSpec(blk, lambda j, i, core_ref: (j, i, 0))),
        compiler_params=_params(("parallel", "parallel")), name=name)(core, g, r)


def _adamw_math(w, g, m, v):
    m = ADAM_B1 * m + (1.0 - ADAM_B1) * g
    v = ADAM_B2 * v + (1.0 - ADAM_B2) * (g * g)
    m_hat = m / (1.0 - ADAM_B1 ** ADAM_STEP)
    v_hat = v / (1.0 - ADAM_B2 ** ADAM_STEP)
    delta = -ADAM_LR * (m_hat / (jnp.sqrt(v_hat) + ADAM_EPS) + ADAM_WD * w)
    return delta, m, v


def _sum_adamw(mine, sib, r, where, w, m, v, *, ta, name):
    Aw, Bw = w.shape
    Bg = mine.shape[2]
    assert Aw % ta == 0 and Bw <= Bg and mine.shape[1] == Aw

    def body(where_ref, p_ref, s_ref, r0, r1, r2, w_ref, m_ref, v_ref, g_out, d_out, m_out, v_out):
        g = (((p_ref[:, :Bw] + s_ref[:, :Bw]) + r0[:, :Bw].astype(F32))
             + r1[:, :Bw].astype(F32)) + r2[:, :Bw].astype(F32)
        delta, m_new, v_new = _adamw_math(w_ref[...], g, m_ref[...], v_ref[...])
        g_out[...] = g
        d_out[...] = delta
        m_out[...] = m_new
        v_out[...] = v_new

    gblk = (None, ta, Bg)
    row = pl.BlockSpec((ta, Bw), lambda i, where_ref: (i, 0))
    rspecs = [pl.BlockSpec(gblk, (lambda i, where_ref, k=k: (k, i, 0))) for k in range(3)]
    shp = jax.ShapeDtypeStruct((Aw, Bw), F32)
    return pl.pallas_call(
        body, out_shape=(shp, shp, shp, shp),
        grid_spec=pltpu.PrefetchScalarGridSpec(
            num_scalar_prefetch=1, grid=(Aw // ta,),
            in_specs=[pl.BlockSpec(gblk, lambda i, where_ref: (2 * where_ref[0] + where_ref[1], i, 0)),
                      pl.BlockSpec(gblk, lambda i, where_ref: (where_ref[0], i, 0))] + rspecs + [row, row, row],
            out_specs=(row, row, row, row)),
        compiler_params=_params(("parallel",)), name=name)(where, mine, sib, r, r, r, w, m, v)


def _adamw(w, g, m, v, *, name):
    def body(w_ref, g_ref, m_ref, v_ref, d_out, m_out, v_out):
        delta, m_new, v_new = _adamw_math(w_ref[...], g_ref[...], m_ref[...], v_ref[...])
        d_out[...] = delta
        m_out[...] = m_new
        v_out[...] = v_new

    vm = pl.BlockSpec(memory_space=pltpu.VMEM)
    shp = jax.ShapeDtypeStruct(w.shape, F32)
    return pl.pallas_call(body, out_shape=(shp, shp, shp), in_specs=[vm] * 4, out_specs=(vm, vm, vm),
                          compiler_params=_params(), name=name)(w, g, m, v)


def _small_allreduce_adamw(s, w, m, v, *, name):
    R, W = s.shape

    def body(s_ref, w_ref, m_ref, v_ref, g_out, d_out, m_out, v_out, gath, send_sems, recv_sems):
        x, y, c = _coords()
        mine = 4 * x + 2 * y + c
        gath[mine] = s_ref[...]
        peers = [((1 - x) if k & 4 else x, (1 - y) if k & 2 else y, (1 - c) if k & 1 else c) for k in range(1, N_DEV)]
        sends = []
        for k in range(1, N_DEV):
            peer = peers[k - 1]
            sends.append(pltpu.make_async_remote_copy(
                src_ref=s_ref, dst_ref=gath.at[mine], send_sem=send_sems.at[k - 1], recv_sem=recv_sems.at[k - 1],
                device_id=peer, device_id_type=MESH))
        for cp in sends:
            cp.start()
        for k in range(1, N_DEV):
            peer = peers[k - 1]
            pltpu.make_async_remote_copy(
                src_ref=s_ref, dst_ref=gath.at[4 * peer[0] + 2 * peer[1] + peer[2]],
                send_sem=send_sems.at[k - 1], recv_sem=recv_sems.at[k - 1],
                device_id=peer, device_id_type=MESH).wait_recv()
        for cp in sends:
            cp.wait_send()
        g = gath[0]
        for d in range(1, N_DEV):
            g = g + gath[d]
        delta, m_new, v_new = _adamw_math(w_ref[...], g, m_ref[...], v_ref[...])
        g_out[...] = g
        d_out[...] = delta
        m_out[...] = m_new
        v_out[...] = v_new

    vm = pl.BlockSpec(memory_space=pltpu.VMEM)
    shp = jax.ShapeDtypeStruct((R, W), F32)
    return pl.pallas_call(
        body, out_shape=(shp, shp, shp, shp), in_specs=[vm] * 4, out_specs=(vm, vm, vm, vm),
        scratch_shapes=[pltpu.VMEM((N_DEV, R, W), F32), pltpu.SemaphoreType.DMA((N_DEV - 1,)),
                        pltpu.SemaphoreType.DMA((N_DEV - 1,))],
        compiler_params=_params(), name=name)(s, w, m, v)


def _pack_small(rel_bias, g1, g2, g3, g4, b_forget, sinks, extra=None, meta=None):
    misc = jnp.concatenate([rel_bias.reshape(-1), b_forget.reshape(-1), sinks.reshape(-1)])
    misc = jnp.concatenate([misc, jnp.zeros((D_MODEL - misc.shape[0],), F32)])[None]
    last = jnp.zeros((1, D_MODEL), F32) if extra is None else extra
    meta = jnp.zeros((N_META, D_MODEL), F32) if meta is None else meta
    return jnp.concatenate([g1, g2, g3, g4, misc, last, jnp.zeros((2, D_MODEL), F32), meta], axis=0)


def _unpack_small(p):
    nrb = N_BUCKETS * SWA_Q_HEADS
    misc = p[4]
    return dict(rel_bias=misc[:nrb].reshape(N_BUCKETS, SWA_Q_HEADS), ln_pre_mix=p[0:1], ln_post_mix=p[1:2],
                ln_pre_ffn=p[2:3], ln_post_ffn=p[3:4], b_forget=misc[nrb:nrb + 8].reshape(1, 8),
                sinks=misc[nrb + 8:nrb + 16].reshape(1, 8))


def _proj_runs():
    gw = FOX_GROUP * HEAD_DIM
    swa = SWA_Q_W + 2 * SWA_KV_HEADS * HEAD_DIM
    runs = [(0, swa)]
    for grp in range(FOX_HEADS // FOX_GROUP):
        runs += [(swa + part * FOX_W + grp * gw, swa + part * FOX_W + (grp + 1) * gw) for part in range(3)]
    return runs


def _columns_from_shards(gathered, runs, shard):
    pieces = []
    for start, stop in runs:
        for d in range(start // shard, (stop - 1) // shard + 1):
            lo = d * shard
            pieces.append(gathered[d][:, max(start, lo) - lo:min(stop, lo + shard) - lo])
    return jnp.concatenate(pieces, axis=1)


def _device_shards(qkv, gate, shard, padded):
    pos, segments = 0, []
    for start, stop in _proj_runs():
        segments.append((start, stop, qkv, pos))
        pos += stop - start
    segments.append((pos, pos + gate.shape[1], gate, 0))
    total = pos + gate.shape[1]
    assert total % shard == 0
    zeros = jnp.zeros((qkv.shape[0], padded - shard), qkv.dtype)
    out = []
    for d in range(total // shard):
        lo, hi = d * shard, (d + 1) * shard
        pieces = [arr[:, src + max(lo, s) - s:src + min(hi, e) - s]
                  for s, e, arr, src in sorted(segments, key=lambda seg: seg[0]) if max(lo, s) < min(hi, e)]
        out.append(jnp.concatenate(pieces + [zeros], axis=1))
    return jnp.stack(out)


def kernel(x, meta_tokens, rel_bias, ln_pre_mix, ln_post_mix, ln_pre_ffn, ln_post_ffn, w_in, b_forget, sinks, w_out, w_gate_up, w_down, loss_target, m_meta_tokens, m_rel_bias, m_ln_pre_mix, m_ln_post_mix, m_ln_pre_ffn, m_ln_post_ffn, m_w_in, m_b_forget, m_sinks, m_w_out, m_w_gate_up, m_w_down, v_meta_tokens, v_rel_bias, v_ln_pre_mix, v_ln_post_mix, v_ln_pre_ffn, v_ln_post_ffn, v_w_in, v_b_forget, v_sinks, v_w_out, v_w_gate_up, v_w_down):
    seq = x.shape[1]
    T = BLOCK + seq
    assert T % FOX_TILE == 0
    nq = T // FOX_TILE
    tm = _tile(T, 1056)
    cin = w_in.shape[2]
    hid = w_down.shape[1]
    F = N_DEV * hid
    assert w_gate_up.shape[2] == 2 * hid and cin <= W_IN_PAD and hid % 16 == 0

    x_i, y_i, c_i = _coords()
    core = jnp.reshape(c_i, (1,)).astype(jnp.int32)
    where = jnp.stack([2 * x_i + y_i, c_i]).astype(jnp.int32)
    w_in_s = jnp.pad(w_in[0].astype(BF16), ((0, 0), (0, W_IN_PAD - cin)))
    w_gu_t = w_gate_up[0].T
    h0, target, hn1, hn1_t, g_in, _ = _pad_rows_rms(x[0], loss_target[0], ln_pre_mix,
                                                    _gather_exchange([w_in_s, meta_tokens]), name="ag_w_in_rms_pre_mix")
    gather_rest = _gather_exchange([w_out[0].astype(BF16), w_gu_t.astype(BF16), w_down[0].astype(BF16)])
    w_qkv = _columns_from_shards(g_in, _proj_runs(), cin)
    w_f = jnp.pad(_columns_from_shards(g_in, [(D_QKV, D_PROJ)], cin), ((0, 0), (0, BLOCK - FOX_HEADS)))

    proj = _matmul(hn1, w_qkv, out_dtype=BF16, tm=tm, tn=D_QKV, name="mm_in_proj")
    proj_f = _matmul(hn1, w_f, out_dtype=F32, tm=tm, tn=BLOCK, name="mm_in_proj_f")

    f_t = proj_f[:, :FOX_HEADS].T
    bf_col = b_forget.reshape(FOX_HEADS, 1)

    oh_cur, oh_prev = _bucket_onehots()
    bias_c, bias_p = _bias_tiles(rel_bias.T, jnp.asarray(oh_cur.T), jnp.asarray(oh_prev.T), name="bias_tiles")
    far = rel_bias[N_BUCKETS - 1]
    sink_v = sinks[0]
    mix_a = _swa_fwd(proj, bias_c, bias_p, far, sink_v, name="swa_fwd")

    cum_col = _fox_gates_fwd(f_t, bf_col, name="fox_gates_fwd")
    q_b, k_b, v_b = _fox_prep(proj, cum_col, name="fox_prep")
    mix, lse_row, g_out, g_gu, g_down = _fox_fwd(q_b, k_b, v_b, mix_a, ex=gather_rest, name="fox_fwd")
    w_out_full = g_out.reshape(D_MODEL, D_MODEL)
    w_gu_full_t = g_gu.reshape(2 * F, D_MODEL)
    w_down_full = g_down.reshape(F, D_MODEL)

    a1 = _matmul(mix, w_out_full, out_dtype=F32, tm=tm, tn=D_MODEL, name="mm_out_proj")
    h1, hn2 = _post_res_norm(a1, ln_post_mix, h0, ln_pre_ffn, name="post_mix_pre_ffn")
    gate, up, act, act_t = _gate_up_swiglu(hn2, w_gu_full_t, name="mm_gate_up")
    ff = _matmul(act, w_down_full, out_dtype=F32, tm=tm, tn=D_MODEL, name="mm_down")
    dh2, dff, dg_post_ffn, loss_acc = _loss_head(ff, ln_post_ffn, h1, target, name="loss_head")

    dgu = _d_act_swiglu(dff, w_down_full, gate, up, name="mm_d_act")
    d_w_down = _matmul(act_t, dff, out_dtype=F32, tm=_tile(F, 768), tn=D_MODEL, name="mm_dw_down")
    dhn2 = _matmul(dgu, w_gu_full_t, out_dtype=F32, tm=tm, tn=512, name="mm_d_hn2")
    d_w_gu_t = _matmul(dgu, hn2, ta=True, out_dtype=F32, tm=512, tn=D_MODEL, name="mm_dw_gate_up")
    dh1, dg_pre_ffn, da1, dg_post_mix = _rms_bwd_twice(h1, ln_pre_ffn, dhn2, dh2, a1, ln_post_mix,
                                                       name="rms_bwd_pre_ffn_post_mix")
    dmix = _matmul(da1, w_out_full, nt=True, out_dtype=BF16, tm=tm, tn=D_MODEL, name="mm_d_mix")
    d_w_out = _matmul(mix, da1, ta=True, out_dtype=F32, tm=512, tn=D_MODEL, name="mm_dw_out")

    ffn_grads = [g.reshape(N_DEV, -1, D_MODEL) for g in (d_w_out, d_w_gu_t, d_w_down)]
    dproj_a, dbc, dbp, dbf, dsk, *ffn_sibling = _swa_bwd(
        proj, dmix, bias_c, bias_p, far, sink_v, ex=_cores_exchange(ffn_grads), name="swa_bwd")
    ffn_sums = [_add_cores(g, r, core, name="rs_add_" + t)
                for g, r, t in zip(ffn_grads, ffn_sibling, ["w_out", "w_gate_up", "w_down"])]

    do_b = _fox_prep_bwd(dmix, mix, name="fox_prep_bwd")
    dproj, dcq, dck, *ffn_chips = _fox_bwd(
        q_b, k_b, v_b, do_b, lse_row, dproj_a, ex=_chips_exchange(ffn_sums), name="fox_bwd")
    df_t, d_bf = _fox_gates_bwd(dcq.reshape(FOX_HEADS, T), dck.reshape(FOX_HEADS, T), f_t, bf_col,
                                name="fox_gates_bwd")
    df = jnp.pad(df_t.T.astype(BF16), ((0, 0), (0, BLOCK - FOX_HEADS)))

    d_w_qkv = _matmul(hn1_t, dproj, out_dtype=F32, tm=512, tn=768, name="mm_dw_in")
    d_w_f = _matmul(hn1_t, df, out_dtype=F32, tm=512, tn=BLOCK, name="mm_dw_in_f")
    d_w_in = _device_shards(d_w_qkv, d_w_f[:, :FOX_HEADS], cin, W_IN_PAD)
    d_tab, d_sink, in_sibling = _small_grads(dbc, dbp, dbf, dsk, jnp.asarray(oh_cur), jnp.asarray(oh_prev),
                                             ex=_cores_exchange([d_w_in]), name="small_grads")
    in_sum = _add_cores(d_w_in, in_sibling, core, name="rs_add_w_in")
    dhn1, in_chips = _matmul(dproj, w_qkv, nt=True, out_dtype=F32, tm=tm, tn=512,
                             ex=_chips_exchange([in_sum]), name="mm_d_hn1")
    dx_rows, dg_pre_mix, dh0_head = _rms_bwd_rows(h0, ln_pre_mix, dhn1, df, w_f, dh1, name="rms_bwd_pre_mix")
    grad_x = dx_rows[None]
    d_meta = dh0_head[PAD_ROWS:]

    rs_out, rs_gu, rs_down = zip(ffn_grads, ffn_sibling, ffn_chips)
    updates = [("w_in", (d_w_in, in_sibling, in_chips), (w_in[0], m_w_in[0], v_w_in[0]), 256),
               ("w_out", rs_out, (w_out[0], m_w_out[0], v_w_out[0]), BLOCK),
               ("w_gate_up", rs_gu, (w_gu_t, m_w_gate_up[0].T, v_w_gate_up[0].T), hid),
               ("w_down", rs_down, (w_down[0], m_w_down[0], v_w_down[0]), hid)]
    big = [{}, {}, {}, {}]
    for t, grads, shard, ta in updates:
        res = _sum_adamw(*grads, where, *shard, ta=ta, name="rs_adamw_" + t)
        for kind in range(4):
            big[kind][t] = (res[kind].T if t == "w_gate_up" else res[kind])[None]

    loss_row = jnp.pad(loss_acc[0:1, 0:1] * (0.5 / D_MODEL), ((0, 0), (0, D_MODEL - 1)))
    s_small = _pack_small(d_tab.T, dg_pre_mix, dg_post_mix, dg_pre_ffn, dg_post_ffn, d_bf, d_sink,
                          extra=loss_row, meta=d_meta)
    w_s = _pack_small(rel_bias, ln_pre_mix, ln_post_mix, ln_pre_ffn, ln_post_ffn, b_forget, sinks)
    m_s = _pack_small(m_rel_bias, m_ln_pre_mix, m_ln_post_mix, m_ln_pre_ffn, m_ln_post_ffn, m_b_forget, m_sinks)
    v_s = _pack_small(v_rel_bias, v_ln_pre_mix, v_ln_post_mix, v_ln_pre_ffn, v_ln_post_ffn, v_b_forget, v_sinks)
    small = _small_allreduce_adamw(s_small, w_s, m_s, v_s, name="small_allreduce_adamw")
    loss = small[0][5, 0]
    mcols = meta_tokens.shape[1]
    g_meta_mine = lax.dynamic_slice(small[0][8:8 + N_META], (0, (4 * x_i + 2 * y_i + c_i) * mcols), (N_META, mcols))
    big[0]["meta_tokens"] = g_meta_mine
    for kind, arr in enumerate(_adamw(meta_tokens, g_meta_mine, m_meta_tokens, v_meta_tokens, name="adamw_meta")):
        big[kind + 1]["meta_tokens"] = arr
    small = [_unpack_small(p) for p in small]

    names = ["meta_tokens", "rel_bias", "ln_pre_mix", "ln_post_mix", "ln_pre_ffn", "ln_post_ffn", "w_in",
             "b_forget", "sinks", "w_out", "w_gate_up", "w_down"]
    outs = [loss, grad_x]
    for kind in range(4):
        for nme in names:
            outs.append(big[kind][nme] if nme in big[kind] else small[kind][nme])
    return tuple(outs)
```
